```python
import math
import jax, jax.numpy as jnp
from jax import lax
import numpy as np

D_MODEL = 1024
BATCH = 16
SEQ = 2048
DEPTH = 2

CHUNK = 64
HEAD_DIM = 64
N_MAIN_HEADS = 12
N_MEM_HEADS = 4
MAIN_WIDTH = N_MAIN_HEADS * HEAD_DIM
MEM_WIDTH = N_MEM_HEADS * HEAD_DIM
MIX_WIDTH = MAIN_WIDTH + MEM_WIDTH
N_MEM = 256
D_FF = 2816
CONV_WIDTH = 3
Q_BLOCK = 128
N_A_LAYERS = DEPTH // 2
N_B_LAYERS = DEPTH - N_A_LAYERS
A_IN_WIDTH = 3 * MAIN_WIDTH + N_MAIN_HEADS + MEM_WIDTH
B_IN_WIDTH = MAIN_WIDTH + MEM_WIDTH
FORGET_BIAS_INIT = 3.0
FORGET_W_SCALE = 0.1
EPS = 1e-6

kernel_name = "yoco_fox_stickbreak_memory_convffn"


def rmsnorm(x, g):
    xf = x.astype(jnp.float32)
    y = xf * lax.rsqrt(jnp.mean(xf * xf, axis=-1, keepdims=True) + EPS)
    return (y * g.astype(jnp.float32)).astype(x.dtype)


def split_heads(t, n_heads):
    b, s, _ = t.shape
    return t.reshape(b, s, n_heads, HEAD_DIM).transpose(0, 2, 1, 3)


def merge_heads(t):
    b, n, s, d = t.shape
    return t.transpose(0, 2, 1, 3).reshape(b, s, n * d)


def forgetting_attention(q, k, v, log_f):
    seq = q.shape[2]
    c = jnp.cumsum(log_f, axis=-1)
    scale = HEAD_DIM ** -0.5
    outs = []
    for i in range(seq // Q_BLOCK):
        q0, q1 = i * Q_BLOCK, (i + 1) * Q_BLOCK
        kb, vb = k[:, :, :q1], v[:, :, :q1]
        logits = jnp.einsum('bhqd,bhkd->bhqk', q[:, :, q0:q1], kb).astype(jnp.float32) * scale
        logits = logits + c[:, :, q0:q1, None] - c[:, :, None, :q1]
        t_idx = jnp.arange(q0, q1)[:, None]
        s_idx = jnp.arange(q1)[None, :]
        logits = jnp.where(s_idx <= t_idx, logits, -jnp.inf)
        p = jax.nn.softmax(logits, axis=-1)
        outs.append(jnp.einsum('bhqk,bhkd->bhqd', p.astype(vb.dtype), vb))
    return jnp.concatenate(outs, axis=2)


def stick_breaking_attention(q, k, v):
    seq = q.shape[2]
    scale = HEAD_DIM ** -0.5
    outs = []
    for i in range(seq // Q_BLOCK):
        q0, q1 = i * Q_BLOCK, (i + 1) * Q_BLOCK
        kb, vb = k[:, :, :q1], v[:, :, :q1]
        z = jnp.einsum('bhqd,bhkd->bhqk', q[:, :, q0:q1], kb).astype(jnp.float32) * scale
        t_idx = jnp.arange(q0, q1)[:, None]
        s_idx = jnp.arange(q1)[None, :]
        causal = s_idx < t_idx
        log_1m_beta = jnp.where(causal, jax.nn.log_sigmoid(-z), 0.0)
        cum = jnp.cumsum(log_1m_beta, axis=-1)
        log_a = jax.nn.log_sigmoid(z) + cum[..., -1:] - cum
        a = jnp.where(causal, jnp.exp(log_a), 0.0)
        outs.append(jnp.einsum('bhqk,bhkd->bhqd', a.astype(vb.dtype), vb))
    return jnp.concatenate(outs, axis=2)


def memory_attention(q, mem_k, mem_v):
    logits = jnp.einsum('bhqd,bhmd->bhqm', q, mem_k).astype(jnp.float32) * (HEAD_DIM ** -0.5)
    p = jax.nn.softmax(logits, axis=-1)
    return jnp.einsum('bhqm,bhmd->bhqd', p.astype(mem_v.dtype), mem_v)


def conv_ffn(h, w_up, conv_w, conv_b, w_down):
    u = h @ w_up
    s = u.shape[1]
    up = jnp.pad(u, ((0, 0), (CONV_WIDTH - 1, 0), (0, 0)))
    uc = conv_b
    for j in range(CONV_WIDTH):
        uc = uc + conv_w[j] * up[:, j:j + s]
    gate, val = jnp.split(uc, 2, axis=-1)
    return (jax.nn.silu(gate) * val) @ w_down


def _fwd_setup_inputs(seed: int = 0) -> dict:
    key = jax.random.key(seed)
    ks = jax.random.split(key, 17)
    f32 = jnp.float32

    def nrm(k, shape, scale):
        return jax.random.normal(k, shape, f32) * scale

    def gain(k, shape):
        return 1.0 + 0.05 * jax.random.normal(k, shape, f32)

    x = nrm(ks[0], (BATCH, SEQ, D_MODEL), 1.0)
    mem = nrm(ks[1], (BATCH, N_MEM, D_MODEL), 1.0)
    ln_mix_g = gain(ks[2], (DEPTH, D_MODEL))
    w_in_a = nrm(ks[3], (N_A_LAYERS, D_MODEL, A_IN_WIDTH), D_MODEL ** -0.5)
    w_in_a = w_in_a.at[:, :, 3 * MAIN_WIDTH:3 * MAIN_WIDTH + N_MAIN_HEADS].multiply(FORGET_W_SCALE)
    b_f_a = FORGET_BIAS_INIT + 0.5 * jax.random.normal(ks[4], (N_A_LAYERS, N_MAIN_HEADS), f32)
    w_in_b = nrm(ks[5], (N_B_LAYERS, D_MODEL, B_IN_WIDTH), D_MODEL ** -0.5)
    ln_kv_g = gain(ks[6], (D_MODEL,))
    w_kv = nrm(ks[7], (D_MODEL, 2 * MAIN_WIDTH), D_MODEL ** -0.5)
    ln_mem_g = gain(ks[8], (DEPTH, D_MODEL))
    w_memkv = nrm(ks[9], (DEPTH, D_MODEL, 2 * MEM_WIDTH), D_MODEL ** -0.5)
    w_out = nrm(ks[10], (DEPTH, MIX_WIDTH, D_MODEL), MIX_WIDTH ** -0.5)
    ln_ffn_g = gain(ks[11], (DEPTH, D_MODEL))
    w_up = nrm(ks[12], (DEPTH, D_MODEL, 2 * D_FF), D_MODEL ** -0.5)
    conv_w = nrm(ks[13], (DEPTH, CONV_WIDTH, 2 * D_FF), CONV_WIDTH ** -0.5)
    conv_b = nrm(ks[14], (DEPTH, 2 * D_FF), 0.02)
    w_down = nrm(ks[15], (DEPTH, D_FF, D_MODEL), D_FF ** -0.5)
    final_g = gain(ks[16], (D_MODEL,))
    return {'x': x, 'mem': mem, 'ln_mix_g': ln_mix_g, 'w_in_a': w_in_a, 'b_f_a': b_f_a,
            'w_in_b': w_in_b, 'ln_kv_g': ln_kv_g, 'w_kv': w_kv, 'ln_mem_g': ln_mem_g,
            'w_memkv': w_memkv, 'w_out': w_out, 'ln_ffn_g': ln_ffn_g, 'w_up': w_up,
            'conv_w': conv_w, 'conv_b': conv_b, 'w_down': w_down, 'final_g': final_g}


def _fwd_reference(x, mem, ln_mix_g, w_in_a, b_f_a, w_in_b, ln_kv_g, w_kv, ln_mem_g,
              w_memkv, w_out, ln_ffn_g, w_up, conv_w, conv_b, w_down, final_g):
    k_sh = None
    v_sh = None
    for layer in range(DEPTH):
        h = rmsnorm(x, ln_mix_g[layer])
        if layer < N_A_LAYERS:
            proj = h @ w_in_a[layer]
            q, k, v, f_logit, q_mem = jnp.split(
                proj, [MAIN_WIDTH, 2 * MAIN_WIDTH, 3 * MAIN_WIDTH, 3 * MAIN_WIDTH + N_MAIN_HEADS], axis=-1)
            log_f = jax.nn.log_sigmoid((f_logit + b_f_a[layer]).astype(jnp.float32))
            o_main = forgetting_attention(split_heads(q, N_MAIN_HEADS), split_heads(k, N_MAIN_HEADS),
                                          split_heads(v, N_MAIN_HEADS), log_f.transpose(0, 2, 1))
        else:
            if layer == N_A_LAYERS:
                kv = rmsnorm(x, ln_kv_g) @ w_kv
                k_s, v_s = jnp.split(kv, 2, axis=-1)
                k_sh = split_heads(k_s, N_MAIN_HEADS)
                v_sh = split_heads(v_s, N_MAIN_HEADS)
            proj = h @ w_in_b[layer - N_A_LAYERS]
            q, q_mem = jnp.split(proj, [MAIN_WIDTH], axis=-1)
            o_main = stick_breaking_attention(split_heads(q, N_MAIN_HEADS), k_sh, v_sh)
        mem_kv = rmsnorm(mem, ln_mem_g[layer]) @ w_memkv[layer]
        mk, mv = jnp.split(mem_kv, 2, axis=-1)
        o_mem = memory_attention(split_heads(q_mem, N_MEM_HEADS), split_heads(mk, N_MEM_HEADS),
                                 split_heads(mv, N_MEM_HEADS))
        o = jnp.concatenate([merge_heads(o_main), merge_heads(o_mem)], axis=-1) @ w_out[layer]
        x = x + o
        x = x + conv_ffn(rmsnorm(x, ln_ffn_g[layer]), w_up[layer], conv_w[layer], conv_b[layer], w_down[layer])
    return rmsnorm(x, final_g)


import jax as _jax
import jax.numpy as _jnp

TWIN_FORMAT = 'train_step'
FWD_PARAMS = ['x', 'mem', 'ln_mix_g', 'w_in_a', 'b_f_a', 'w_in_b', 'ln_kv_g', 'w_kv', 'ln_mem_g', 'w_memkv', 'w_out', 'ln_ffn_g', 'w_up', 'conv_w', 'conv_b', 'w_down', 'final_g']
TWIN_WEIGHTS = ['ln_mix_g', 'w_in_a', 'b_f_a', 'w_in_b', 'ln_kv_g', 'w_kv', 'ln_mem_g', 'w_memkv', 'w_out', 'ln_ffn_g', 'w_up', 'conv_w', 'conv_b', 'w_down', 'final_g']
TWIN_DIFF_INPUT = 'x'
TWIN_INPUTS = ['x', 'mem', 'ln_mix_g', 'w_in_a', 'b_f_a', 'w_in_b', 'ln_kv_g', 'w_kv', 'ln_mem_g', 'w_memkv', 'w_out', 'ln_ffn_g', 'w_up', 'conv_w', 'conv_b', 'w_down', 'final_g', 'loss_target', 'm_ln_mix_g', 'm_w_in_a', 'm_b_f_a', 'm_w_in_b', 'm_ln_kv_g', 'm_w_kv', 'm_ln_mem_g', 'm_w_memkv', 'm_w_out', 'm_ln_ffn_g', 'm_w_up', 'm_conv_w', 'm_conv_b', 'm_w_down', 'm_final_g', 'v_ln_mix_g', 'v_w_in_a', 'v_b_f_a', 'v_w_in_b', 'v_ln_kv_g', 'v_w_kv', 'v_ln_mem_g', 'v_w_memkv', 'v_w_out', 'v_ln_ffn_g', 'v_w_up', 'v_conv_w', 'v_conv_b', 'v_w_down', 'v_final_g']
TWIN_OUTPUTS = ['loss', 'grad_x', 'grad_ln_mix_g', 'grad_w_in_a', 'grad_b_f_a', 'grad_w_in_b', 'grad_ln_kv_g', 'grad_w_kv', 'grad_ln_mem_g', 'grad_w_memkv', 'grad_w_out', 'grad_ln_ffn_g', 'grad_w_up', 'grad_conv_w', 'grad_conv_b', 'grad_w_down', 'grad_final_g', 'delta_ln_mix_g', 'delta_w_in_a', 'delta_b_f_a', 'delta_w_in_b', 'delta_ln_kv_g', 'delta_w_kv', 'delta_ln_mem_g', 'delta_w_memkv', 'delta_w_out', 'delta_ln_ffn_g', 'delta_w_up', 'delta_conv_w', 'delta_conv_b', 'delta_w_down', 'delta_final_g', 'new_m_ln_mix_g', 'new_m_w_in_a', 'new_m_b_f_a', 'new_m_w_in_b', 'new_m_ln_kv_g', 'new_m_w_kv', 'new_m_ln_mem_g', 'new_m_w_memkv', 'new_m_w_out', 'new_m_ln_ffn_g', 'new_m_w_up', 'new_m_conv_w', 'new_m_conv_b', 'new_m_w_down', 'new_m_final_g', 'new_v_ln_mix_g', 'new_v_w_in_a', 'new_v_b_f_a', 'new_v_w_in_b', 'new_v_ln_kv_g', 'new_v_w_kv', 'new_v_ln_mem_g', 'new_v_w_memkv', 'new_v_w_out', 'new_v_ln_ffn_g', 'new_v_w_up', 'new_v_conv_w', 'new_v_conv_b', 'new_v_w_down', 'new_v_final_g']
TWIN_LEAF_KINDS = {'loss': 'loss', 'grad_x': 'grad_x', 'grad_ln_mix_g': 'grad_w', 'grad_w_in_a': 'grad_w', 'grad_b_f_a': 'grad_w', 'grad_w_in_b': 'grad_w', 'grad_ln_kv_g': 'grad_w', 'grad_w_kv': 'grad_w', 'grad_ln_mem_g': 'grad_w', 'grad_w_memkv': 'grad_w', 'grad_w_out': 'grad_w', 'grad_ln_ffn_g': 'grad_w', 'grad_w_up': 'grad_w', 'grad_conv_w': 'grad_w', 'grad_conv_b': 'grad_w', 'grad_w_down': 'grad_w', 'grad_final_g': 'grad_w', 'delta_ln_mix_g': 'delta_w', 'delta_w_in_a': 'delta_w', 'delta_b_f_a': 'delta_w', 'delta_w_in_b': 'delta_w', 'delta_ln_kv_g': 'delta_w', 'delta_w_kv': 'delta_w', 'delta_ln_mem_g': 'delta_w', 'delta_w_memkv': 'delta_w', 'delta_w_out': 'delta_w', 'delta_ln_ffn_g': 'delta_w', 'delta_w_up': 'delta_w', 'delta_conv_w': 'delta_w', 'delta_conv_b': 'delta_w', 'delta_w_down': 'delta_w', 'delta_final_g': 'delta_w', 'new_m_ln_mix_g': 'new_m', 'new_m_w_in_a': 'new_m', 'new_m_b_f_a': 'new_m', 'new_m_w_in_b': 'new_m', 'new_m_ln_kv_g': 'new_m', 'new_m_w_kv': 'new_m', 'new_m_ln_mem_g': 'new_m', 'new_m_w_memkv': 'new_m', 'new_m_w_out': 'new_m', 'new_m_ln_ffn_g': 'new_m', 'new_m_w_up': 'new_m', 'new_m_conv_w': 'new_m', 'new_m_conv_b': 'new_m', 'new_m_w_down': 'new_m', 'new_m_final_g': 'new_m', 'new_v_ln_mix_g': 'new_v', 'new_v_w_in_a': 'new_v', 'new_v_b_f_a': 'new_v', 'new_v_w_in_b': 'new_v', 'new_v_ln_kv_g': 'new_v', 'new_v_w_kv': 'new_v', 'new_v_ln_mem_g': 'new_v', 'new_v_w_memkv': 'new_v', 'new_v_w_out': 'new_v', 'new_v_ln_ffn_g': 'new_v', 'new_v_w_up': 'new_v', 'new_v_conv_w': 'new_v', 'new_v_conv_b': 'new_v', 'new_v_w_down': 'new_v', 'new_v_final_g': 'new_v'}


def _forward(args):
    return _fwd_reference(*[args[k] for k in FWD_PARAMS])


def _output_shape():
    out = _jax.eval_shape(lambda: _forward(_fwd_setup_inputs(0)))
    return out.shape, out.dtype

N_MICROBATCH = 1
ADAM_LR = 0.001
ADAM_B1 = 0.9
ADAM_B2 = 0.999
ADAM_EPS = 1e-08
ADAM_WD = 0.01
ADAM_STEP = 10
PER_EXAMPLE_BATCH_AXIS = {'x': 0, 'mem': 0, 'loss_target': 0}
SHARED_INPUTS = []
_WEIGHT_DTYPES = {'ln_mix_g': _jnp.float32, 'w_in_a': _jnp.float32, 'b_f_a': _jnp.float32, 'w_in_b': _jnp.float32, 'ln_kv_g': _jnp.float32, 'w_kv': _jnp.float32, 'ln_mem_g': _jnp.float32, 'w_memkv': _jnp.float32, 'w_out': _jnp.float32, 'ln_ffn_g': _jnp.float32, 'w_up': _jnp.float32, 'conv_w': _jnp.float32, 'conv_b': _jnp.float32, 'w_down': _jnp.float32, 'final_g': _jnp.float32}
MOMENT_SCALE = {'ln_mix_g': 6.625205e-02, 'w_in_a': 5.557575e-02, 'b_f_a': 6.193457e-01, 'w_in_b': 3.873221e-02, 'ln_kv_g': 9.178984e-02, 'w_kv': 7.419791e-02, 'ln_mem_g': 1.523534e-02, 'w_memkv': 2.074669e-02, 'w_out': 7.188558e-02, 'ln_ffn_g': 1.318348e-01, 'w_up': 5.649061e-02, 'conv_w': 5.576313e-02, 'conv_b': 5.625151e-02, 'w_down': 9.277709e-02, 'final_g': 3.208607e+01}


def _to_microbatches(a, axis):
    t = _jnp.moveaxis(a, axis, 0)
    t = t.reshape((N_MICROBATCH, t.shape[0] // N_MICROBATCH) + t.shape[1:])
    return _jnp.moveaxis(t, 1, axis + 1)


def setup_inputs(seed: int = 0) -> dict:
    inp = _fwd_setup_inputs(seed)
    key = _jax.random.fold_in(_jax.random.key(seed), 7919)
    shape, _ = _output_shape()
    out = dict(inp)
    out["loss_target"] = _jax.random.normal(_jax.random.fold_in(key, 0), shape, _jnp.float32)
    for i, name in enumerate(TWIN_WEIGHTS):
        w = inp[name].astype(_jnp.float32)
        if MOMENT_SCALE is None:
            s = _jnp.sqrt(_jnp.mean(_jnp.square(w)) + 1e-30)
        else:
            s = MOMENT_SCALE[name]
        km, kv = _jax.random.split(_jax.random.fold_in(key, i + 1))
        out[name] = w
        out["m_" + name] = s * _jax.random.normal(km, w.shape, _jnp.float32)
        out["v_" + name] = (s * s) * _jax.random.uniform(kv, w.shape, _jnp.float32, 0.5, 1.5)
    if N_MICROBATCH > 1:
        for name, axis in PER_EXAMPLE_BATCH_AXIS.items():
            out[name] = _to_microbatches(out[name], axis)
    return {'x': out['x'], 'mem': out['mem'], 'ln_mix_g': out['ln_mix_g'], 'w_in_a': out['w_in_a'], 'b_f_a': out['b_f_a'], 'w_in_b': out['w_in_b'], 'ln_kv_g': out['ln_kv_g'], 'w_kv': out['w_kv'], 'ln_mem_g': out['ln_mem_g'], 'w_memkv': out['w_memkv'], 'w_out': out['w_out'], 'ln_ffn_g': out['ln_ffn_g'], 'w_up': out['w_up'], 'conv_w': out['conv_w'], 'conv_b': out['conv_b'], 'w_down': out['w_down'], 'final_g': out['final_g'], 'loss_target': out['loss_target'], 'm_ln_mix_g': out['m_ln_mix_g'], 'm_w_in_a': out['m_w_in_a'], 'm_b_f_a': out['m_b_f_a'], 'm_w_in_b': out['m_w_in_b'], 'm_ln_kv_g': out['m_ln_kv_g'], 'm_w_kv': out['m_w_kv'], 'm_ln_mem_g': out['m_ln_mem_g'], 'm_w_memkv': out['m_w_memkv'], 'm_w_out': out['m_w_out'], 'm_ln_ffn_g': out['m_ln_ffn_g'], 'm_w_up': out['m_w_up'], 'm_conv_w': out['m_conv_w'], 'm_conv_b': out['m_conv_b'], 'm_w_down': out['m_w_down'], 'm_final_g': out['m_final_g'], 'v_ln_mix_g': out['v_ln_mix_g'], 'v_w_in_a': out['v_w_in_a'], 'v_b_f_a': out['v_b_f_a'], 'v_w_in_b': out['v_w_in_b'], 'v_ln_kv_g': out['v_ln_kv_g'], 'v_w_kv': out['v_w_kv'], 'v_ln_mem_g': out['v_ln_mem_g'], 'v_w_memkv': out['v_w_memkv'], 'v_w_out': out['v_w_out'], 'v_ln_ffn_g': out['v_ln_ffn_g'], 'v_w_up': out['v_w_up'], 'v_conv_w': out['v_conv_w'], 'v_conv_b': out['v_conv_b'], 'v_w_down': out['v_w_down'], 'v_final_g': out['v_final_g']}


def _loss(weights, diff, rest, loss_target):
    with _jax.named_scope("forward"):
        args = {**rest, TWIN_DIFF_INPUT: diff, **{k: w.astype(_WEIGHT_DTYPES[k]) for k, w in weights.items()}}
        y = _forward(args)
    with _jax.named_scope("loss_head"):
        err = _jnp.square(y.astype(_jnp.float32) - loss_target)
        return 0.5 * _jnp.sum(_jnp.mean(err, axis=-1)) if err.ndim else 0.5 * err


def _adamw(w, g, m, v):
    m = ADAM_B1 * m + (1.0 - ADAM_B1) * g
    v = ADAM_B2 * v + (1.0 - ADAM_B2) * _jnp.square(g)
    m_hat = m / (1.0 - ADAM_B1 ** ADAM_STEP)
    v_hat = v / (1.0 - ADAM_B2 ** ADAM_STEP)
    delta = -ADAM_LR * (m_hat / (_jnp.sqrt(v_hat) + ADAM_EPS) + ADAM_WD * w)
    return delta, m, v


def reference(x, mem, ln_mix_g, w_in_a, b_f_a, w_in_b, ln_kv_g, w_kv, ln_mem_g, w_memkv, w_out, ln_ffn_g, w_up, conv_w, conv_b, w_down, final_g, loss_target, m_ln_mix_g, m_w_in_a, m_b_f_a, m_w_in_b, m_ln_kv_g, m_w_kv, m_ln_mem_g, m_w_memkv, m_w_out, m_ln_ffn_g, m_w_up, m_conv_w, m_conv_b, m_w_down, m_final_g, v_ln_mix_g, v_w_in_a, v_b_f_a, v_w_in_b, v_ln_kv_g, v_w_kv, v_ln_mem_g, v_w_memkv, v_w_out, v_ln_ffn_g, v_w_up, v_conv_w, v_conv_b, v_w_down, v_final_g):
    given = dict(x=x, mem=mem, ln_mix_g=ln_mix_g, w_in_a=w_in_a, b_f_a=b_f_a, w_in_b=w_in_b, ln_kv_g=ln_kv_g, w_kv=w_kv, ln_mem_g=ln_mem_g, w_memkv=w_memkv, w_out=w_out, ln_ffn_g=ln_ffn_g, w_up=w_up, conv_w=conv_w, conv_b=conv_b, w_down=w_down, final_g=final_g, loss_target=loss_target, m_ln_mix_g=m_ln_mix_g, m_w_in_a=m_w_in_a, m_b_f_a=m_b_f_a, m_w_in_b=m_w_in_b, m_ln_kv_g=m_ln_kv_g, m_w_kv=m_w_kv, m_ln_mem_g=m_ln_mem_g, m_w_memkv=m_w_memkv, m_w_out=m_w_out, m_ln_ffn_g=m_ln_ffn_g, m_w_up=m_w_up, m_conv_w=m_conv_w, m_conv_b=m_conv_b, m_w_down=m_w_down, m_final_g=m_final_g, v_ln_mix_g=v_ln_mix_g, v_w_in_a=v_w_in_a, v_b_f_a=v_b_f_a, v_w_in_b=v_w_in_b, v_ln_kv_g=v_ln_kv_g, v_w_kv=v_w_kv, v_ln_mem_g=v_ln_mem_g, v_w_memkv=v_w_memkv, v_w_out=v_w_out, v_ln_ffn_g=v_ln_ffn_g, v_w_up=v_w_up, v_conv_w=v_conv_w, v_conv_b=v_conv_b, v_w_down=v_w_down, v_final_g=v_final_g)
    weights = {n: given[n] for n in TWIN_WEIGHTS}
    shared = {n: given[n] for n in SHARED_INPUTS}
    per_example = {n: given[n] for n in ['x', 'mem']}
    grad_fn = _jax.value_and_grad(_loss, argnums=(0, 1))

    def one_microbatch(ex, loss_target):
        ex = dict(ex)
        diff = ex.pop(TWIN_DIFF_INPUT)
        return grad_fn(weights, diff, {**shared, **ex}, loss_target)

    if N_MICROBATCH == 1:
        loss, (grad_w, grad_x) = one_microbatch(per_example, given["loss_target"])
    else:
        def body(carry, xs):
            loss_sum, grad_sum = carry
            l_k, (gw_k, gx_k) = one_microbatch(xs[0], xs[1])
            with _jax.named_scope("update"):
                return (loss_sum + l_k, _jax.tree.map(_jnp.add, grad_sum, gw_k)), gx_k

        init = (_jnp.zeros((), _jnp.float32), _jax.tree.map(_jnp.zeros_like, weights))
        (loss, grad_w), grad_x = _jax.lax.scan(body, init, (per_example, given["loss_target"]))
    with _jax.named_scope("update"):
        delta_w, new_m, new_v = {}, {}, {}
        for n in TWIN_WEIGHTS:
            delta_w[n], new_m[n], new_v[n] = _adamw(weights[n], grad_w[n], given["m_" + n], given["v_" + n])
    return (loss, grad_x, *[grad_w[n] for n in TWIN_WEIGHTS], *[delta_w[n] for n in TWIN_WEIGHTS],
            *[new_m[n] for n in TWIN_WEIGHTS], *[new_v[n] for n in TWIN_WEIGHTS])
```

```python
import functools

import jax
import jax.numpy as jnp
from jax import lax
from jax.experimental import pallas as pl
from jax.experimental.pallas import tpu as pltpu

F32 = jnp.float32
BF16 = jnp.bfloat16
LANES = 128
HEAD_DIM = 64
N_MAIN_HEADS = 12
N_MEM_HEADS = 4
MAIN_W = N_MAIN_HEADS * HEAD_DIM
MEM_W = N_MEM_HEADS * HEAD_DIM
SCALE = HEAD_DIM ** -0.5
EPS = 1e-6
NEG = -1e30
N_DEV = 8
ATT_TILE = 256
VMEM_BIG = 56 * 2 ** 20
MESH = pl.DeviceIdType.MESH

ADAM_LR = 0.001
ADAM_B1 = 0.9
ADAM_B2 = 0.999
ADAM_EPS = 1e-08
ADAM_WD = 0.01
ADAM_STEP = 10

NT = (((1,), (1,)), ((), ()))
TN = (((0,), (0,)), ((), ()))


def _pc(body, *, name, out_shape, grid=None, in_specs=None, out_specs=None, scratch_shapes=(),
        semantics=None, vmem=None):
    kw = {}
    if grid is not None:
        kw["grid"] = grid
    params = pltpu.CompilerParams(dimension_semantics=semantics, vmem_limit_bytes=vmem)
    return pl.pallas_call(body, name=name, out_shape=out_shape, in_specs=in_specs, out_specs=out_specs,
                          scratch_shapes=list(scratch_shapes), compiler_params=params, **kw)


def _sds(shape, dtype):
    return jax.ShapeDtypeStruct(shape, dtype)


def _mm_fwd(a, w, *, name, tm, tn, out_dtype, g=None, res=None, col0=0, ncols=None, save_h=False):
    m_rows, k = a.shape
    n = w.shape[1] if ncols is None else ncols
    grid = (m_rows // tm, n // tn)
    norm = g is not None

    def body(*refs):
        refs = list(refs)
        a_ref = refs.pop(0)
        g_ref = refs.pop(0) if norm else None
        w_ref = refs.pop(0)
        res_ref = refs.pop(0) if res is not None else None
        o_ref = refs.pop(0)
        hout_ref = refs.pop(0) if save_h else None
        h_ref = refs.pop(0) if norm else None
        if norm:
            @pl.when(pl.program_id(1) == 0)
            def _():
                xv = a_ref[...]
                r = lax.rsqrt(jnp.mean(xv * xv, axis=-1, keepdims=True) + EPS)
                h = ((xv * r) * g_ref[...]).astype(BF16)
                h_ref[...] = h
                if save_h:
                    hout_ref[...] = h
            lhs = h_ref[...]
        else:
            lhs = a_ref[...].astype(BF16)
        acc = jnp.dot(lhs, w_ref[...], preferred_element_type=F32)
        if res is not None:
            acc = acc + res_ref[...]
        o_ref[...] = acc.astype(out_dtype)

    in_specs = [pl.BlockSpec((tm, k), lambda i, j: (i, 0))]
    args = [a]
    if norm:
        in_specs.append(pl.BlockSpec((1, k), lambda i, j: (0, 0)))
        args.append(g.reshape(1, k))
    in_specs.append(pl.BlockSpec((k, tn), lambda i, j: (0, j + col0)))
    args.append(w)
    if res is not None:
        in_specs.append(pl.BlockSpec((tm, tn), lambda i, j: (i, j)))
        args.append(res)
    out_shape = [_sds((m_rows, n), out_dtype)]
    out_specs = [pl.BlockSpec((tm, tn), lambda i, j: (i, j))]
    if save_h:
        out_shape.append(_sds((m_rows, k), BF16))
        out_specs.append(pl.BlockSpec((tm, k), lambda i, j: (i, 0)))
    scratch = [pltpu.VMEM((tm, k), BF16)] if norm else []
    outs = _pc(body, name=name, out_shape=out_shape, grid=grid, in_specs=in_specs, out_specs=out_specs,
               scratch_shapes=scratch, semantics=("arbitrary", "arbitrary"), vmem=VMEM_BIG)(*args)
    return outs if save_h else outs[0]


def _mm_nt(a, w, *, name, tm, tn, out_dtype):
    m_rows, k = a.shape
    n = w.shape[0]

    def body(a_ref, w_ref, o_ref):
        acc = lax.dot_general(a_ref[...].astype(BF16), w_ref[...], NT, preferred_element_type=F32)
        o_ref[...] = acc.astype(out_dtype)

    return _pc(body, name=name, out_shape=_sds((m_rows, n), out_dtype), grid=(m_rows // tm, n // tn),
               in_specs=[pl.BlockSpec((tm, k), lambda i, j: (i, 0)), pl.BlockSpec((tn, k), lambda i, j: (j, 0))],
               out_specs=pl.BlockSpec((tm, tn), lambda i, j: (i, j)),
               semantics=("arbitrary", "arbitrary"), vmem=VMEM_BIG)(a, w)


def _mm_tn(a, b, *, name, ta, tn, tt):
    t_rows, ka = a.shape
    n = b.shape[1]

    def body(a_ref, b_ref, o_ref):
        @pl.when(pl.program_id(2) == 0)
        def _():
            o_ref[...] = jnp.zeros_like(o_ref)
        o_ref[...] += lax.dot_general(a_ref[...].astype(BF16), b_ref[...].astype(BF16), TN,
                                      preferred_element_type=F32)

    return _pc(body, name=name, out_shape=_sds((ka, n), F32), grid=(ka // ta, n // tn, t_rows // tt),
               in_specs=[pl.BlockSpec((tt, ta), lambda i, j, t: (t, i)),
                         pl.BlockSpec((tt, tn), lambda i, j, t: (t, j))],
               out_specs=pl.BlockSpec((ta, tn), lambda i, j, t: (i, j)),
               semantics=("arbitrary", "arbitrary", "arbitrary"), vmem=VMEM_BIG)(a, b)


def _wgrad(a, b, name):
    t_rows, ka = a.shape
    n = b.shape[1]
    ta = ka if ka <= 1024 else ka // 2
    tn = n
    while ta * tn * 4 > 6 * 2 ** 20 and tn % 256 == 0:
        tn //= 2
    tt = min(512, t_rows)
    return _mm_tn(a, b, name=name, ta=ta, tn=tn, tt=tt)


def _mm_nt_rmsbwd(parts, w, x, g, *, name, dres=None, want_dx=True):
    m_rows, d = x.shape
    tm = min(256, m_rows)
    n_parts = len(parts)

    def body(*refs):
        refs = list(refs)
        dy_refs = [refs.pop(0) for _ in range(n_parts)]
        w_refs = [refs.pop(0) for _ in range(n_parts)]
        x_ref = refs.pop(0)
        g_ref = refs.pop(0)
        dres_ref = refs.pop(0) if dres is not None else None
        dx_ref = refs.pop(0) if want_dx else None
        dg_ref = refs.pop(0)

        @pl.when(pl.program_id(0) == 0)
        def _():
            dg_ref[...] = jnp.zeros_like(dg_ref)

        dh = None
        for dy_ref, w_ref in zip(dy_refs, w_refs):
            t = lax.dot_general(dy_ref[...].astype(BF16), w_ref[...], NT, preferred_element_type=F32)
            dh = t if dh is None else dh + t
        xv = x_ref[...]
        r = lax.rsqrt(jnp.mean(xv * xv, axis=-1, keepdims=True) + EPS)
        xh = xv * r
        dg_ref[...] += jnp.sum(dh * xh, axis=0, keepdims=True)
        if want_dx:
            dhg = dh * g_ref[...]
            dx = r * (dhg - xh * jnp.mean(dhg * xh, axis=-1, keepdims=True))
            if dres is not None:
                dx = dx + dres_ref[...]
            dx_ref[...] = dx

    in_specs, args = [], []
    for dy, _ in parts:
        in_specs.append(pl.BlockSpec((tm, dy.shape[1]), lambda i: (i, 0)))
        args.append(dy)
    for dy, cb in parts:
        in_specs.append(pl.BlockSpec((d, dy.shape[1]), functools.partial(lambda i, cb: (0, cb), cb=cb)))
        args.append(w)
    in_specs += [pl.BlockSpec((tm, d), lambda i: (i, 0)), pl.BlockSpec((1, d), lambda i: (0, 0))]
    args += [x, g.reshape(1, d)]
    if dres is not None:
        in_specs.append(pl.BlockSpec((tm, d), lambda i: (i, 0)))
        args.append(dres)
    out_shape, out_specs = [], []
    if want_dx:
        out_shape.append(_sds((m_rows, d), F32))
        out_specs.append(pl.BlockSpec((tm, d), lambda i: (i, 0)))
    out_shape.append(_sds((1, d), F32))
    out_specs.append(pl.BlockSpec((1, d), lambda i: (0, 0)))
    outs = _pc(body, name=name, out_shape=out_shape, grid=(m_rows // tm,), in_specs=in_specs,
               out_specs=out_specs, semantics=("arbitrary",), vmem=VMEM_BIG)(*args)
    return (outs[0], outs[1]) if want_dx else (None, outs[0])


def _loss_head(x, g, tgt, *, name):
    m_rows, d = x.shape
    tm = min(256, m_rows)

    def body(x_ref, g_ref, t_ref, dx_ref, dg_ref, loss_ref):
        @pl.when(pl.program_id(0) == 0)
        def _():
            dg_ref[...] = jnp.zeros_like(dg_ref)
            loss_ref[...] = jnp.zeros_like(loss_ref)

        xv = x_ref[...]
        r = lax.rsqrt(jnp.mean(xv * xv, axis=-1, keepdims=True) + EPS)
        xh = xv * r
        gv = g_ref[...]
        err = xh * gv - t_ref[...]
        per_tok = jnp.mean(err * err, axis=-1, keepdims=True)
        loss_ref[...] += 0.5 * jnp.sum(per_tok, axis=0, keepdims=True)
        dout = err * (1.0 / d)
        dg_ref[...] += jnp.sum(dout * xh, axis=0, keepdims=True)
        dhg = dout * gv
        dx_ref[...] = r * (dhg - xh * jnp.mean(dhg * xh, axis=-1, keepdims=True))

    row = pl.BlockSpec((tm, d), lambda i: (i, 0))
    return _pc(body, name=name, out_shape=[_sds((m_rows, d), F32), _sds((1, d), F32), _sds((1, LANES), F32)],
               grid=(m_rows // tm,), in_specs=[row, pl.BlockSpec((1, d), lambda i: (0, 0)), row],
               out_specs=[row, pl.BlockSpec((1, d), lambda i: (0, 0)), pl.BlockSpec((1, LANES), lambda i: (0, 0))],
               semantics=("arbitrary",))(x, g.reshape(1, d), tgt)


def _split3(v):
    hi = v.astype(BF16)
    r1 = v - hi.astype(F32)
    mid = r1.astype(BF16)
    lo = (r1 - mid.astype(F32)).astype(BF16)
    return hi, mid, lo


def _split2(v):
    hi = v.astype(BF16)
    lo = (v - hi.astype(F32)).astype(BF16)
    return hi, lo


def _tri_dot3(tri, v):
    hi, mid, lo = _split3(v)
    return (jnp.dot(tri, hi, preferred_element_type=F32) + jnp.dot(tri, mid, preferred_element_type=F32)
            + jnp.dot(tri, lo, preferred_element_type=F32))


def _dot_tri2(v, tri):
    hi, lo = _split2(v)
    return jnp.dot(hi, tri, preferred_element_type=F32) + jnp.dot(lo, tri, preferred_element_type=F32)


def _log_sigmoid(v):
    return jnp.minimum(v, 0.0) - jnp.log(1.0 + jnp.exp(-jnp.abs(v)))


def _forget_cumsum(f_logit, b_f, *, B, S, name):
    ch = min(256, S)
    nch = S // ch

    def body(f_ref, b_ref, c_ref):
        r_i = lax.broadcasted_iota(jnp.int32, (ch, ch), 0)
        c_i = lax.broadcasted_iota(jnp.int32, (ch, ch), 1)
        tri = (c_i <= r_i).astype(BF16)
        bv = b_ref[...]

        def step(k, carry):
            rows = pl.ds(pl.multiple_of(k * ch, ch), ch)
            lf = _log_sigmoid(f_ref[rows, :] + bv)
            c_ref[rows, :] = _tri_dot3(tri, lf) + carry
            return carry + jnp.sum(lf, axis=0, keepdims=True)

        lax.fori_loop(0, nch, step, jnp.zeros((1, LANES), F32))

    blk = pl.BlockSpec((S, LANES), lambda b: (b, 0))
    return _pc(body, name=name, out_shape=_sds((B * S, LANES), F32), grid=(B,),
               in_specs=[blk, pl.BlockSpec((1, LANES), lambda b: (0, 0))], out_specs=blk,
               semantics=("arbitrary",))(f_logit, b_f)


def _forget_cumsum_bwd(dc, f_logit, b_f, *, B, S, name):
    ch = min(256, S)
    nch = S // ch

    def body(dc_ref, f_ref, b_ref, df_ref, db_ref):
        @pl.when(pl.program_id(0) == 0)
        def _():
            db_ref[...] = jnp.zeros_like(db_ref)

        r_i = lax.broadcasted_iota(jnp.int32, (ch, ch), 0)
        c_i = lax.broadcasted_iota(jnp.int32, (ch, ch), 1)
        tri = (c_i >= r_i).astype(BF16)
        bv = b_ref[...]

        def step(kk, carry):
            tail, dbs = carry
            k = nch - 1 - kk
            rows = pl.ds(pl.multiple_of(k * ch, ch), ch)
            dcv = dc_ref[rows, :]
            dlf = _tri_dot3(tri, dcv) + tail
            z = f_ref[rows, :] + bv
            df = dlf * (1.0 / (1.0 + jnp.exp(z)))
            df_ref[rows, :] = df.astype(BF16)
            return tail + jnp.sum(dcv, axis=0, keepdims=True), dbs + jnp.sum(df, axis=0, keepdims=True)

        zero = jnp.zeros((1, LANES), F32)
        _, dbs = lax.fori_loop(0, nch, step, (zero, zero))
        db_ref[...] += dbs

    blk = pl.BlockSpec((S, LANES), lambda b: (b, 0))
    one = pl.BlockSpec((1, LANES), lambda b: (0, 0))
    return _pc(body, name=name, out_shape=[_sds((B * S, LANES), BF16), _sds((1, LANES), F32)], grid=(B,),
               in_specs=[blk, blk, one], out_specs=[blk, one], semantics=("arbitrary",))(dc, f_logit, b_f)


def _head_mask(lane, hh):
    return (lane < HEAD_DIM) if hh == 0 else (lane >= HEAD_DIM)


def _col_spec(rows, nblk_rows, cb):
    return pl.BlockSpec((rows, LANES), lambda b, p, i: (b * nblk_rows + i, cb + p))


def _kv_spec(rows, cb):
    return pl.BlockSpec((rows, LANES), lambda b, p, i: (b, cb + p))


def _stat_col_spec(tq):
    return pl.BlockSpec((1, 2, tq, 1), lambda b, p, i: (b, p, i, 0))


def _stat_row_spec(S):
    return pl.BlockSpec((1, 2, 1, S), lambda b, p, i: (b, p, 0, 0))


def _softmax_fwd(qa, ka, va, *, name, B, S, Sk, P, q_cb, k_cb, v_cb, causal, cc=None, cr=None):
    tq = min(ATT_TILE, S)
    tk = min(ATT_TILE, Sk)
    nq, nk = S // tq, Sk // tk
    decay = cc is not None
    assert not causal or (tq == tk and S == Sk)

    def body(*refs):
        if decay:
            q_ref, k_ref, v_ref, cc_ref, cr_ref, o_ref, lse_ref = refs
        else:
            q_ref, k_ref, v_ref, o_ref, lse_ref = refs
        i = pl.program_id(2)
        q = q_ref[...]
        lane = lax.broadcasted_iota(jnp.int32, (tq, LANES), 1)
        row = lax.broadcasted_iota(jnp.int32, (tq, tk), 0) + i * tq
        col0 = lax.broadcasted_iota(jnp.int32, (tq, tk), 1)
        outs = []
        for hh in range(2):
            qh = jnp.where(_head_mask(lane, hh), q, jnp.zeros_like(q))

            def step(kb, carry, hh=hh, qh=qh):
                m, l, acc = carry
                ks = pl.multiple_of(kb * tk, tk)
                kblk = k_ref[pl.ds(ks, tk), :]
                vblk = v_ref[pl.ds(ks, tk), :]
                s = lax.dot_general(qh, kblk, NT, preferred_element_type=F32) * SCALE
                if decay:
                    s = s + (cc_ref[0, hh] - cr_ref[0, hh, :, pl.ds(ks, tk)])
                if causal:
                    s = jnp.where(col0 + kb * tk <= row, s, NEG)
                m_new = jnp.maximum(m, jnp.max(s, axis=-1, keepdims=True))
                alpha = jnp.exp(m - m_new)
                p = jnp.exp(s - m_new)
                l = alpha * l + jnp.sum(p, axis=-1, keepdims=True)
                acc = alpha * acc + jnp.dot(p.astype(BF16), vblk, preferred_element_type=F32)
                return m_new, l, acc

            init = (jnp.full((tq, 1), NEG, F32), jnp.zeros((tq, 1), F32), jnp.zeros((tq, LANES), F32))
            m, l, acc = lax.fori_loop(0, (i + 1) if causal else nk, step, init)
            outs.append(acc / l)
            lse_ref[0, hh] = m + jnp.log(l)
        o_ref[...] = jnp.where(lane < HEAD_DIM, outs[0], outs[1]).astype(BF16)

    in_specs = [_col_spec(tq, nq, q_cb), _kv_spec(Sk, k_cb), _kv_spec(Sk, v_cb)]
    args = [qa, ka, va]
    if decay:
        in_specs += [_stat_col_spec(tq), _stat_row_spec(S)]
        args += [cc, cr]
    return _pc(body, name=name,
               out_shape=[_sds((B * S, P * LANES), BF16), _sds((B, 2 * P, S, 1), F32)],
               grid=(B, P, nq), in_specs=in_specs, out_specs=[_col_spec(tq, nq, 0), _stat_col_spec(tq)],
               semantics=("arbitrary", "arbitrary", "arbitrary"), vmem=VMEM_BIG)(*args)


def _softmax_bwd(qa, ka, va, doa, oa, lse, *, name, B, S, Sk, P, q_cb, k_cb, v_cb, do_cb, causal,
                 cc=None, cr=None):
    tq = min(ATT_TILE, S)
    tk = min(ATT_TILE, Sk)
    nq, nk = S // tq, Sk // tk
    decay = cc is not None

    def body(*refs):
        if decay:
            q_ref, k_ref, v_ref, do_ref, o_ref, lse_ref, cc_ref, cr_ref, dq_ref, dk_ref, dv_ref, dcs_ref = refs
        else:
            q_ref, k_ref, v_ref, do_ref, o_ref, lse_ref, dq_ref, dk_ref, dv_ref = refs
        i = pl.program_id(2)

        @pl.when(i == 0)
        def _():
            dk_ref[...] = jnp.zeros_like(dk_ref)
            dv_ref[...] = jnp.zeros_like(dv_ref)
            if decay:
                dcs_ref[...] = jnp.zeros_like(dcs_ref)

        q = q_ref[...]
        do = do_ref[...]
        prod = do.astype(F32) * o_ref[...].astype(F32)
        lane = lax.broadcasted_iota(jnp.int32, (tq, LANES), 1)
        row = lax.broadcasted_iota(jnp.int32, (tq, tk), 0) + i * tq
        col0 = lax.broadcasted_iota(jnp.int32, (tq, tk), 1)
        dqs = []
        for hh in range(2):
            hmask = _head_mask(lane, hh)
            qh = jnp.where(hmask, q, jnp.zeros_like(q))
            doh = jnp.where(hmask, do, jnp.zeros_like(do))
            lse_h = lse_ref[0, hh]
            n_blocks = (i + 1) if causal else nk

            def probs(kb, hh=hh, qh=qh, doh=doh, lse_h=lse_h):
                ks = pl.multiple_of(kb * tk, tk)
                kblk = k_ref[pl.ds(ks, tk), :]
                vblk = v_ref[pl.ds(ks, tk), :]
                s = lax.dot_general(qh, kblk, NT, preferred_element_type=F32) * SCALE
                if decay:
                    s = s + (cc_ref[0, hh] - cr_ref[0, hh, :, pl.ds(ks, tk)])
                if causal:
                    s = jnp.where(col0 + kb * tk <= row, s, NEG)
                p = jnp.exp(s - lse_h)
                dp = lax.dot_general(doh, vblk, NT, preferred_element_type=F32)
                return ks, kblk, p, dp

            if decay:
                def delta_step(kb, acc):
                    _, _, p, dp = probs(kb)
                    return acc + jnp.sum(p * dp, axis=-1, keepdims=True)

                delta = lax.fori_loop(0, n_blocks, delta_step, jnp.zeros((tq, 1), F32))
            else:
                delta = jnp.sum(jnp.where(hmask, prod, 0.0), axis=-1, keepdims=True)

            def step(kb, dq_acc, hh=hh, qh=qh, doh=doh, delta=delta):
                ks, kblk, p, dp = probs(kb)
                ds = p * (dp - delta)
                dsb = ds.astype(BF16)
                dk_ref[pl.ds(ks, tk), :] += lax.dot_general(dsb, qh, TN, preferred_element_type=F32) * SCALE
                dv_ref[pl.ds(ks, tk), :] += lax.dot_general(p.astype(BF16), doh, TN, preferred_element_type=F32)
                if decay:
                    dcs_ref[0, hh, :, pl.ds(ks, tk)] -= jnp.sum(ds, axis=0, keepdims=True)
                return dq_acc + jnp.dot(dsb, kblk, preferred_element_type=F32)

            dqs.append(lax.fori_loop(0, n_blocks, step, jnp.zeros((tq, LANES), F32)) * SCALE)
        dq_ref[...] = jnp.where(lane < HEAD_DIM, dqs[0], dqs[1]).astype(BF16)

    in_specs = [_col_spec(tq, nq, q_cb), _kv_spec(Sk, k_cb), _kv_spec(Sk, v_cb), _col_spec(tq, nq, do_cb),
                _col_spec(tq, nq, 0), _stat_col_spec(tq)]
    args = [qa, ka, va, doa, oa, lse]
    out_shape = [_sds((B * S, P * LANES), BF16), _sds((B * Sk, P * LANES), F32), _sds((B * Sk, P * LANES), F32)]
    out_specs = [_col_spec(tq, nq, 0), _kv_spec(Sk, 0), _kv_spec(Sk, 0)]
    if decay:
        in_specs += [_stat_col_spec(tq), _stat_row_spec(S)]
        args += [cc, cr]
        out_shape.append(_sds((B, 2 * P, 1, S), F32))
        out_specs.append(_stat_row_spec(S))
    return _pc(body, name=name, out_shape=out_shape, grid=(B, P, nq), in_specs=in_specs, out_specs=out_specs,
               semantics=("arbitrary", "arbitrary", "arbitrary"), vmem=VMEM_BIG)(*args)


def _sb_terms(qh, kblk, row, col0, kb, tk):
    z = lax.dot_general(qh, kblk, NT, preferred_element_type=F32) * SCALE
    causal = (col0 + kb * tk) < row
    sp = jnp.maximum(z, 0.0) + jnp.log(1.0 + jnp.exp(-jnp.abs(z)))
    ls = z - sp
    lm = jnp.where(causal, -sp, 0.0)
    return causal, ls, lm


def _stickbreak_fwd(qa, ka, va, *, name, B, S, P, q_cb, k_cb, v_cb):
    tq = tk = min(ATT_TILE, S)
    nq = S // tq

    def body(q_ref, k_ref, v_ref, o_ref, rt_ref):
        i = pl.program_id(2)
        q = q_ref[...]
        lane = lax.broadcasted_iota(jnp.int32, (tq, LANES), 1)
        row = lax.broadcasted_iota(jnp.int32, (tq, tk), 0) + i * tq
        col0 = lax.broadcasted_iota(jnp.int32, (tq, tk), 1)
        t_r = lax.broadcasted_iota(jnp.int32, (tk, tk), 0)
        t_c = lax.broadcasted_iota(jnp.int32, (tk, tk), 1)
        after = (t_r > t_c).astype(BF16)
        outs = []
        for hh in range(2):
            qh = jnp.where(_head_mask(lane, hh), q, jnp.zeros_like(q))

            def step(jj, carry, qh=qh):
                run, acc = carry
                kb = i - jj
                ks = pl.multiple_of(kb * tk, tk)
                kblk = k_ref[pl.ds(ks, tk), :]
                vblk = v_ref[pl.ds(ks, tk), :]
                causal, ls, lm = _sb_terms(qh, kblk, row, col0, kb, tk)
                suf = _dot_tri2(lm, after)
                a = jnp.where(causal, jnp.exp(ls + run + suf), 0.0)
                acc = acc + jnp.dot(a.astype(BF16), vblk, preferred_element_type=F32)
                return run + jnp.sum(lm, axis=-1, keepdims=True), acc

            run, acc = lax.fori_loop(0, i + 1, step, (jnp.zeros((tq, 1), F32), jnp.zeros((tq, LANES), F32)))
            outs.append(acc)
            rt_ref[0, hh] = run
        o_ref[...] = jnp.where(lane < HEAD_DIM, outs[0], outs[1]).astype(BF16)

    return _pc(body, name=name, out_shape=[_sds((B * S, P * LANES), BF16), _sds((B, 2 * P, S, 1), F32)],
               grid=(B, P, nq), in_specs=[_col_spec(tq, nq, q_cb), _kv_spec(S, k_cb), _kv_spec(S, v_cb)],
               out_specs=[_col_spec(tq, nq, 0), _stat_col_spec(tq)],
               semantics=("arbitrary", "arbitrary", "arbitrary"), vmem=VMEM_BIG)(qa, ka, va)


def _stickbreak_bwd(qa, ka, va, doa, rt, *, name, B, S, P, q_cb, k_cb, v_cb, do_cb):
    tq = tk = min(ATT_TILE, S)
    nq = S // tq

    def body(q_ref, k_ref, v_ref, do_ref, rt_ref, dq_ref, dk_ref, dv_ref):
        i = pl.program_id(2)

        @pl.when(i == 0)
        def _():
            dk_ref[...] = jnp.zeros_like(dk_ref)
            dv_ref[...] = jnp.zeros_like(dv_ref)

        q = q_ref[...]
        do = do_ref[...]
        lane = lax.broadcasted_iota(jnp.int32, (tq, LANES), 1)
        row = lax.broadcasted_iota(jnp.int32, (tq, tk), 0) + i * tq
        col0 = lax.broadcasted_iota(jnp.int32, (tq, tk), 1)
        t_r = lax.broadcasted_iota(jnp.int32, (tk, tk), 0)
        t_c = lax.broadcasted_iota(jnp.int32, (tk, tk), 1)
        upto = (t_r <= t_c).astype(BF16)
        before = (t_r < t_c).astype(BF16)
        dqs = []
        for hh in range(2):
            hmask = _head_mask(lane, hh)
            qh = jnp.where(hmask, q, jnp.zeros_like(q))
            doh = jnp.where(hmask, do, jnp.zeros_like(do))
            rt_h = rt_ref[0, hh]

            def step(kb, carry, qh=qh, doh=doh, rt_h=rt_h):
                pl_sum, pg_sum, dq_acc = carry
                ks = pl.multiple_of(kb * tk, tk)
                kblk = k_ref[pl.ds(ks, tk), :]
                vblk = v_ref[pl.ds(ks, tk), :]
                causal, ls, lm = _sb_terms(qh, kblk, row, col0, kb, tk)
                pin = _dot_tri2(lm, upto)
                a = jnp.where(causal, jnp.exp(ls + (rt_h - pl_sum) - pin), 0.0)
                da = lax.dot_general(doh, vblk, NT, preferred_element_type=F32)
                gm = a * da
                pg = _dot_tri2(gm, before) + pg_sum
                beta = jnp.exp(ls)
                dz = jnp.where(causal, gm * (1.0 - beta) - pg * beta, 0.0)
                dzb = dz.astype(BF16)
                dk_ref[pl.ds(ks, tk), :] += lax.dot_general(dzb, qh, TN, preferred_element_type=F32) * SCALE
                dv_ref[pl.ds(ks, tk), :] += lax.dot_general(a.astype(BF16), doh, TN, preferred_element_type=F32)
                return (pl_sum + jnp.sum(lm, axis=-1, keepdims=True),
                        pg_sum + jnp.sum(gm, axis=-1, keepdims=True),
                        dq_acc + jnp.dot(dzb, kblk, preferred_element_type=F32))

            zc = jnp.zeros((tq, 1), F32)
            _, _, dq_h = lax.fori_loop(0, i + 1, step, (zc, zc, jnp.zeros((tq, LANES), F32)))
            dqs.append(dq_h * SCALE)
        dq_ref[...] = jnp.where(lane < HEAD_DIM, dqs[0], dqs[1]).astype(BF16)

    return _pc(body, name=name,
               out_shape=[_sds((B * S, P * LANES), BF16), _sds((B * S, P * LANES), F32), _sds((B * S, P * LANES), F32)],
               grid=(B, P, nq),
               in_specs=[_col_spec(tq, nq, q_cb), _kv_spec(S, k_cb), _kv_spec(S, v_cb), _col_spec(tq, nq, do_cb),
                         _stat_col_spec(tq)],
               out_specs=[_col_spec(tq, nq, 0), _kv_spec(S, 0), _kv_spec(S, 0)],
               semantics=("arbitrary", "arbitrary", "arbitrary"), vmem=VMEM_BIG)(qa, ka, va, doa, rt)


def _shift_rows(cur, halo_ref, first, rows_idx, k):
    out = pltpu.roll(cur, k, 0)
    for r in range(k):
        edge = jnp.where(first, 0.0, halo_ref[8 - k + r:8 - k + r + 1, :])
        out = jnp.where(rows_idx == r, edge, out)
    return out


def _shift_rows_up(cur, halo_ref, last, rows_idx, k, ts):
    out = pltpu.roll(cur, ts - k, 0)
    for r in range(k):
        edge = jnp.where(last, 0.0, halo_ref[r:r + 1, :])
        out = jnp.where(rows_idx == ts - k + r, edge, out)
    return out


def _conv_taps(main_ref, halo_ref, w_ref, b_ref, first, rows_idx):
    cur = main_ref[...]
    m1 = _shift_rows(cur, halo_ref, first, rows_idx, 1)
    m2 = _shift_rows(cur, halo_ref, first, rows_idx, 2)
    uc = b_ref[...] + w_ref[0:1, :] * m2 + w_ref[1:2, :] * m1 + w_ref[2:3, :] * cur
    return uc, cur, m1, m2


def _conv_specs(ts, tf, ns, nf, S, order):
    def wrap(fn):
        return lambda *g: fn(*order(*g))
    specs = []
    for off in (0, nf):
        specs.append(pl.BlockSpec((ts, tf), wrap(lambda b, i, j, off=off: (b * ns + i, j + off))))
        specs.append(pl.BlockSpec((8, tf), wrap(
            lambda b, i, j, off=off: (jnp.maximum((b * S + i * ts) // 8 - 1, 0), j + off))))
    for off in (0, nf):
        specs.append(pl.BlockSpec((3, tf), wrap(lambda b, i, j, off=off: (0, j + off))))
    for off in (0, nf):
        specs.append(pl.BlockSpec((1, tf), wrap(lambda b, i, j, off=off: (0, j + off))))
    return specs


def _conv_gate_fwd(u, cw, cb, *, name, B, S):
    F = u.shape[1] // 2
    tf = F // 2
    ts = min(256, S)
    ns, nf = S // ts, F // tf

    def body(ug_ref, ugh_ref, uv_ref, uvh_ref, wg_ref, wv_ref, bg_ref, bv_ref, a_ref):
        first = pl.program_id(1) == 0
        rows_idx = lax.broadcasted_iota(jnp.int32, (ts, tf), 0)
        ucg = _conv_taps(ug_ref, ugh_ref, wg_ref, bg_ref, first, rows_idx)[0]
        ucv = _conv_taps(uv_ref, uvh_ref, wv_ref, bv_ref, first, rows_idx)[0]
        a_ref[...] = (ucg * (1.0 / (1.0 + jnp.exp(-ucg))) * ucv).astype(BF16)

    specs = _conv_specs(ts, tf, ns, nf, S, lambda b, i, j: (b, i, j))
    return _pc(body, name=name, out_shape=_sds((B * S, F), BF16), grid=(B, ns, nf), in_specs=specs,
               out_specs=pl.BlockSpec((ts, tf), lambda b, i, j: (b * ns + i, j)),
               semantics=("arbitrary", "arbitrary", "arbitrary"), vmem=VMEM_BIG)(u, u, u, u, cw, cw, cb, cb)


def _conv_gate_bwd(da, u, cw, cb, *, name, B, S):
    F = u.shape[1] // 2
    tf = F // 2
    ts = min(256, S)
    ns, nf = S // ts, F // tf

    def body(da_ref, ug_ref, ugh_ref, uv_ref, uvh_ref, wg_ref, wv_ref, bg_ref, bv_ref,
             dg_ref, dv_ref, pg_ref, pv_ref):
        first = pl.program_id(2) == 0

        @pl.when(jnp.logical_and(pl.program_id(1) == 0, first))
        def _():
            pg_ref[...] = jnp.zeros_like(pg_ref)
            pv_ref[...] = jnp.zeros_like(pv_ref)

        rows_idx = lax.broadcasted_iota(jnp.int32, (ts, tf), 0)
        ucg, g0, g1, g2 = _conv_taps(ug_ref, ugh_ref, wg_ref, bg_ref, first, rows_idx)
        ucv, v0, v1, v2 = _conv_taps(uv_ref, uvh_ref, wv_ref, bv_ref, first, rows_idx)
        sg = 1.0 / (1.0 + jnp.exp(-ucg))
        dav = da_ref[...]
        d_v = dav * (ucg * sg)
        d_g = dav * ucv * (sg * (1.0 + ucg * (1.0 - sg)))
        dg_ref[...] = d_g
        dv_ref[...] = d_v
        for p_ref, d, taps in ((pg_ref, d_g, (g2, g1, g0)), (pv_ref, d_v, (v2, v1, v0))):
            for k in range(3):
                p_ref[k:k + 1, :] += jnp.sum(d * taps[k], axis=0, keepdims=True)
            p_ref[3:4, :] += jnp.sum(d, axis=0, keepdims=True)

    specs = [pl.BlockSpec((ts, tf), lambda j, b, i: (b * ns + i, j))]
    specs += _conv_specs(ts, tf, ns, nf, S, lambda j, b, i: (b, i, j))
    row = pl.BlockSpec((ts, tf), lambda j, b, i: (b * ns + i, j))
    par = pl.BlockSpec((8, tf), lambda j, b, i: (0, j))
    return _pc(body, name=name,
               out_shape=[_sds((B * S, F), F32), _sds((B * S, F), F32), _sds((8, F), F32), _sds((8, F), F32)],
               grid=(nf, B, ns), in_specs=specs, out_specs=[row, row, par, par],
               semantics=("arbitrary", "arbitrary", "arbitrary"), vmem=VMEM_BIG)(da, u, u, u, u, cw, cw, cb, cb)


def _conv_transpose(d, cw, *, name, B, S, col_off):
    F = d.shape[1]
    tf = F // 2
    ts = min(256, S)
    ns, nf = S // ts, F // tf
    nblk8 = B * S // 8

    def body(d_ref, dh_ref, w_ref, o_ref):
        last = pl.program_id(1) == ns - 1
        rows_idx = lax.broadcasted_iota(jnp.int32, (ts, tf), 0)
        cur = d_ref[...]
        p1 = _shift_rows_up(cur, dh_ref, last, rows_idx, 1, ts)
        p2 = _shift_rows_up(cur, dh_ref, last, rows_idx, 2, ts)
        o_ref[...] = (w_ref[2:3, :] * cur + w_ref[1:2, :] * p1 + w_ref[0:1, :] * p2).astype(BF16)

    return _pc(body, name=name, out_shape=_sds((B * S, F), BF16), grid=(B, ns, nf),
               in_specs=[pl.BlockSpec((ts, tf), lambda b, i, j: (b * ns + i, j)),
                         pl.BlockSpec((8, tf), lambda b, i, j: (jnp.minimum((b * S + (i + 1) * ts) // 8, nblk8 - 1), j)),
                         pl.BlockSpec((3, tf), lambda b, i, j: (0, j + col_off * nf))],
               out_specs=pl.BlockSpec((ts, tf), lambda b, i, j: (b * ns + i, j)),
               semantics=("arbitrary", "arbitrary", "arbitrary"))(d, d, cw)


def _adamw(w, g, m, v, *, name):
    rows, cols = w.shape
    tr = rows
    while tr * cols * 4 > 2 ** 20 and tr % 16 == 0:
        tr //= 2

    def body(w_ref, g_ref, m_ref, v_ref, d_ref, nm_ref, nv_ref):
        gv = g_ref[...]
        m_new = ADAM_B1 * m_ref[...] + (1.0 - ADAM_B1) * gv
        v_new = ADAM_B2 * v_ref[...] + (1.0 - ADAM_B2) * (gv * gv)
        m_hat = m_new / (1.0 - ADAM_B1 ** ADAM_STEP)
        v_hat = v_new / (1.0 - ADAM_B2 ** ADAM_STEP)
        d_ref[...] = -ADAM_LR * (m_hat / (jnp.sqrt(v_hat) + ADAM_EPS) + ADAM_WD * w_ref[...])
        nm_ref[...] = m_new
        nv_ref[...] = v_new

    blk = pl.BlockSpec((tr, cols), lambda i: (i, 0))
    return _pc(body, name=name, out_shape=[_sds((rows, cols), F32)] * 3, grid=(rows // tr,),
               in_specs=[blk] * 4, out_specs=[blk] * 3, semantics=("arbitrary",))(w, g, m, v)


def _my_pos():
    return lax.axis_index("x"), lax.axis_index("y"), lax.axis_index("c")


def _all_gather(shard, *, name):
    rows = shard.shape[0]

    def body(x_ref, out_ref, send_sems, recv_sems, local_sem):
        x, y, c = _my_pos()
        me, sibling = (x, y, c), (x, y, 1 - c)
        chips = [(1 - x, y), (x, 1 - y), (1 - x, 1 - y)]

        def slot(px, py, pc):
            return out_ref.at[4 * px + 2 * py + pc]

        def copy(k, block, to, src=None):
            return pltpu.make_async_remote_copy(
                src_ref=slot(*block) if src is None else src, dst_ref=slot(*block),
                send_sem=send_sems.at[k], recv_sem=recv_sems.at[k], device_id=to, device_id_type=MESH)

        mine = pltpu.make_async_copy(x_ref, slot(*me), local_sem)
        mine.start()
        first = [copy(0, me, sibling, src=x_ref)]
        first += [copy(1 + j, me, (*chip, c), src=x_ref) for j, chip in enumerate(chips)]
        for cp in first:
            cp.start()
        passed = [copy(4 + j, (*chip, c), sibling) for j, chip in enumerate(chips)]
        for j, chip in enumerate(chips):
            copy(1 + j, (*chip, c), me).wait_recv()
            passed[j].start()
        copy(0, sibling, me).wait_recv()
        for j, chip in enumerate(chips):
            copy(4 + j, (*chip, 1 - c), me).wait_recv()
        for cp in first + passed:
            cp.wait_send()
        mine.wait()

    return _pc(body, name=name, out_shape=_sds((N_DEV, rows, LANES), shard.dtype),
               in_specs=[pl.BlockSpec(memory_space=pl.ANY)], out_specs=pl.BlockSpec(memory_space=pl.ANY),
               scratch_shapes=[pltpu.SemaphoreType.DMA((7,)), pltpu.SemaphoreType.DMA((7,)),
                               pltpu.SemaphoreType.DMA])(shard)


def _exchange(big, small, *, name):
    rs = small.shape[0]

    def body(big_ref, small_ref, bout_ref, sout_ref, send_sems, recv_sems, local_sems):
        x, y, c = _my_pos()
        me = 4 * x + 2 * y + c
        lb = pltpu.make_async_copy(big_ref.at[me], bout_ref.at[me], local_sems.at[0])
        ls = pltpu.make_async_copy(small_ref, sout_ref.at[me], local_sems.at[1])
        lb.start()
        ls.start()
        copies = []
        for k in range(1, N_DEV):
            px = x ^ ((k >> 2) & 1)
            py = y ^ ((k >> 1) & 1)
            pc = c ^ (k & 1)
            peer = 4 * px + 2 * py + pc
            copies.append(pltpu.make_async_remote_copy(
                src_ref=big_ref.at[peer], dst_ref=bout_ref.at[me], send_sem=send_sems.at[k - 1],
                recv_sem=recv_sems.at[k - 1], device_id=(px, py, pc), device_id_type=MESH))
            copies.append(pltpu.make_async_remote_copy(
                src_ref=small_ref, dst_ref=sout_ref.at[me], send_sem=send_sems.at[7 + k - 1],
                recv_sem=recv_sems.at[7 + k - 1], device_id=(px, py, pc), device_id_type=MESH))
        for cp in copies:
            cp.start()
        for cp in copies:
            cp.wait()
        lb.wait()
        ls.wait()

    return _pc(body, name=name,
               out_shape=[_sds(big.shape, big.dtype), _sds((N_DEV, rs, LANES), F32)],
               in_specs=[pl.BlockSpec(memory_space=pl.ANY), pl.BlockSpec(memory_space=pl.ANY)],
               out_specs=[pl.BlockSpec(memory_space=pl.ANY), pl.BlockSpec(memory_space=pl.ANY)],
               scratch_shapes=[pltpu.SemaphoreType.DMA((14,)), pltpu.SemaphoreType.DMA((14,)),
                               pltpu.SemaphoreType.DMA((2,))])(big, small)


def _sum_slots(a, *, name, tr):
    rows = a.shape[1]

    def body(a_ref, o_ref):
        acc = a_ref[0].astype(F32)
        for j in range(1, N_DEV):
            acc = acc + a_ref[j].astype(F32)
        o_ref[...] = acc

    return _pc(body, name=name, out_shape=_sds((rows, LANES), F32), grid=(rows // tr,),
               in_specs=[pl.BlockSpec((N_DEV, tr, LANES), lambda i: (0, i, 0))],
               out_specs=pl.BlockSpec((tr, LANES), lambda i: (i, 0)), semantics=("arbitrary",))(a)


PACK_ROWS = 25600
SUM_TILE = 512


def _rows128(a):
    return a.reshape(-1, LANES)


def _to_slots(full, kind):
    if kind == "rows2":
        r, c = full.shape
        return full.reshape(N_DEV, r // N_DEV, c)
    if kind == "cols2":
        r, c = full.shape
        return full.reshape(r, N_DEV, c // N_DEV).transpose(1, 0, 2)
    if kind == "rows3":
        l, r, c = full.shape
        return full.reshape(l, N_DEV, r // N_DEV, c).transpose(1, 0, 2, 3)
    if kind == "cols3":
        l, r, c = full.shape
        return full.reshape(l, r, N_DEV, c // N_DEV).transpose(2, 0, 1, 3)
    raise ValueError(kind)


def _from_slots(slots, kind):
    if kind == "rows2":
        _, r, c = slots.shape
        return slots.reshape(N_DEV * r, c)
    if kind == "cols2":
        _, r, c = slots.shape
        return slots.transpose(1, 0, 2).reshape(r, N_DEV * c)
    if kind == "rows3":
        _, l, r, c = slots.shape
        return slots.transpose(1, 0, 2, 3).reshape(l, N_DEV * r, c)
    if kind == "cols3":
        _, l, r, c = slots.shape
        return slots.transpose(1, 2, 0, 3).reshape(l, r, N_DEV * c)
    raise ValueError(kind)


BIG = (("w_in_a", "rows2"), ("w_in_b", "rows2"), ("w_kv", "cols2"), ("w_memkv", "rows3"),
       ("w_out", "rows3"), ("w_up", "cols3"), ("w_down", "rows3"))


def _pad_rows(a, rows, axis):
    pad = [(0, 0)] * a.ndim
    pad[axis] = (0, rows - a.shape[axis])
    return jnp.pad(a, pad)


def kernel(x, mem, ln_mix_g, w_in_a, b_f_a, w_in_b, ln_kv_g, w_kv, ln_mem_g, w_memkv, w_out, ln_ffn_g, w_up, conv_w, conv_b, w_down, final_g, loss_target, m_ln_mix_g, m_w_in_a, m_b_f_a, m_w_in_b, m_ln_kv_g, m_w_kv, m_ln_mem_g, m_w_memkv, m_w_out, m_ln_ffn_g, m_w_up, m_conv_w, m_conv_b, m_w_down, m_final_g, v_ln_mix_g, v_w_in_a, v_b_f_a, v_w_in_b, v_ln_kv_g, v_w_kv, v_ln_mem_g, v_w_memkv, v_w_out, v_ln_ffn_g, v_w_up, v_conv_w, v_conv_b, v_w_down, v_final_g):
    B, S, D = x.shape
    NM = mem.shape[1]
    T = B * S
    F = w_down.shape[1] * N_DEV
    my_idx = 4 * lax.axis_index("x") + 2 * lax.axis_index("y") + lax.axis_index("c")

    shards = {"w_in_a": w_in_a[0], "w_in_b": w_in_b[0], "w_kv": w_kv, "w_memkv": w_memkv, "w_out": w_out,
              "w_up": w_up, "w_down": w_down}
    moms = {"w_in_a": (m_w_in_a[0], v_w_in_a[0]), "w_in_b": (m_w_in_b[0], v_w_in_b[0]), "w_kv": (m_w_kv, v_w_kv),
            "w_memkv": (m_w_memkv, v_w_memkv), "w_out": (m_w_out, v_w_out), "w_up": (m_w_up, v_w_up),
            "w_down": (m_w_down, v_w_down)}

    parts = [_rows128(shards[n].astype(BF16)) for n, _ in BIG]
    parts.append(_rows128(lax.bitcast_convert_type(conv_w, BF16)))
    packed = jnp.concatenate(parts, axis=0)
    packed = _pad_rows(packed, PACK_ROWS, 0)
    gathered = _all_gather(packed, name="gather_weights")
    full = {}
    off = 0
    for n, kind in BIG:
        shp = shards[n].shape
        nrows = shards[n].size // LANES
        full[n] = _from_slots(gathered[:, off:off + nrows].reshape((N_DEV,) + shp), kind)
        off += nrows
    nrows = conv_w.size * 2 // LANES
    cw_bits = gathered[:, off:off + nrows].reshape((N_DEV,) + conv_w.shape + (2,))
    conv_w_full = _from_slots(lax.bitcast_convert_type(cw_bits, F32), "cols3")

    wa = full["w_in_a"]
    n_qkv = 3 * MAIN_W
    wa = jnp.concatenate([wa[:, :n_qkv], wa[:, n_qkv + N_MAIN_HEADS:], wa[:, n_qkv:n_qkv + N_MAIN_HEADS],
                          jnp.zeros((D, LANES - N_MAIN_HEADS), BF16)], axis=1)
    n_main = n_qkv + MEM_W
    wb = full["w_in_b"]
    wkv = full["w_kv"]
    b_f = _pad_rows(b_f_a.reshape(1, N_MAIN_HEADS), LANES, 1)

    x2d = x.reshape(T, D)
    mem2d = mem.reshape(B * NM, D)
    tgt2d = loss_target.reshape(T, D)
    PM, PX = N_MAIN_HEADS // 2, N_MEM_HEADS // 2

    def stats_to_heads(c2d):
        c = c2d.reshape(B, S, LANES)[:, :, :N_MAIN_HEADS].transpose(0, 2, 1)
        return c[..., None], c[:, :, None, :]

    def mem_kv(layer):
        return _mm_fwd(mem2d, full["w_memkv"][layer], name=f"memkv{layer}", tm=B * NM, tn=2 * MEM_W,
                       out_dtype=BF16, g=ln_mem_g[layer], save_h=True)

    def conv_ffn_fwd(xin, layer):
        u, h = _mm_fwd(xin, full["w_up"][layer], name=f"ffn_up{layer}", tm=min(1024, T), tn=512, out_dtype=F32,
                       g=ln_ffn_g[layer], save_h=True)
        a = _conv_gate_fwd(u, conv_w_full[layer], conv_b[layer].reshape(1, 2 * F), name=f"conv_gate{layer}", B=B, S=S)
        xo = _mm_fwd(a, full["w_down"][layer], name=f"ffn_down{layer}", tm=min(512, T), tn=512, out_dtype=F32, res=xin)
        return xo, (u, h, a)

    proj_a, h_mix0 = _mm_fwd(x2d, wa, name="in_proj_a", tm=min(1024, T), tn=512, out_dtype=BF16, g=ln_mix_g[0],
                             ncols=n_main, save_h=True)
    f_logit = _mm_fwd(x2d, wa, name="in_proj_f", tm=min(1024, T), tn=LANES, out_dtype=F32, g=ln_mix_g[0],
                      col0=n_main // LANES, ncols=LANES)
    c2d = _forget_cumsum(f_logit, b_f, B=B, S=S, name="forget_cumsum")
    cc, cr = stats_to_heads(c2d)
    o_main0, lse0 = _softmax_fwd(proj_a, proj_a, proj_a, name="fox_fwd", B=B, S=S, Sk=S, P=PM, q_cb=0, k_cb=PM,
                                 v_cb=2 * PM, causal=True, cc=cc, cr=cr)
    memkv0, h_mem0 = mem_kv(0)
    o_mem0, lse_m0 = _softmax_fwd(proj_a, memkv0, memkv0, name="mem_fwd0", B=B, S=S, Sk=NM, P=PX, q_cb=3 * PM,
                                  k_cb=0, v_cb=PX, causal=False)
    o_cat0 = jnp.concatenate([o_main0, o_mem0], axis=1)
    x1 = _mm_fwd(o_cat0, full["w_out"][0], name="out_proj0", tm=min(512, T), tn=512, out_dtype=F32, res=x2d)
    x2, (u0, h_ffn0, a0) = conv_ffn_fwd(x1, 0)
    kv, h_kv = _mm_fwd(x2, wkv, name="kv_proj", tm=min(1024, T), tn=512, out_dtype=BF16, g=ln_kv_g, save_h=True)
    proj_b, h_mix1 = _mm_fwd(x2, wb, name="in_proj_b", tm=min(1024, T), tn=512, out_dtype=BF16, g=ln_mix_g[1],
                             save_h=True)
    o_main1, rt1 = _stickbreak_fwd(proj_b, kv, kv, name="sb_fwd", B=B, S=S, P=PM, q_cb=0, k_cb=0, v_cb=PM)
    memkv1, h_mem1 = mem_kv(1)
    o_mem1, lse_m1 = _softmax_fwd(proj_b, memkv1, memkv1, name="mem_fwd1", B=B, S=S, Sk=NM, P=PX, q_cb=PM,
                                  k_cb=0, v_cb=PX, causal=False)
    o_cat1 = jnp.concatenate([o_main1, o_mem1], axis=1)
    x3 = _mm_fwd(o_cat1, full["w_out"][1], name="out_proj1", tm=min(512, T), tn=512, out_dtype=F32, res=x2)
    x4, (u1, h_ffn1, a1) = conv_ffn_fwd(x3, 1)
    dx4, dg_final, loss_part = _loss_head(x4, final_g, tgt2d, name="loss_head")

    grads = {}
    small = {}

    def conv_ffn_bwd(dxo, xin, u, h, a, layer):
        w_dn = full["w_down"][layer]
        da = _mm_nt(dxo, w_dn, name=f"d_act{layer}", tm=min(512, T), tn=F // 2, out_dtype=F32)
        grads[("w_down", layer)] = _wgrad(a, dxo, f"g_w_down{layer}")
        cwl = conv_w_full[layer]
        d_g, d_v, p_g, p_v = _conv_gate_bwd(da, u, cwl, conv_b[layer].reshape(1, 2 * F), name=f"conv_bwd{layer}", B=B, S=S)
        small[("conv_w", layer)] = jnp.concatenate([p_g[0:3], p_v[0:3]], axis=1)
        small[("conv_b", layer)] = jnp.concatenate([p_g[3], p_v[3]], axis=0)
        du_g = _conv_transpose(d_g, cwl, name=f"conv_t_gate{layer}", B=B, S=S, col_off=0)
        du_v = _conv_transpose(d_v, cwl, name=f"conv_t_val{layer}", B=B, S=S, col_off=1)
        grads[("w_up", layer)] = jnp.concatenate(
            [_wgrad(h, du_g, f"g_w_up_gate{layer}"), _wgrad(h, du_v, f"g_w_up_val{layer}")], axis=1)
        dxi, dg = _mm_nt_rmsbwd([(du_g, 0), (du_v, 1)], full["w_up"][layer], xin, ln_ffn_g[layer],
                                name=f"d_ffn_in{layer}", dres=dxo)
        small[("ln_ffn_g", layer)] = dg[0]
        return dxi

    def mem_bwd(proj, q_cb, memkv, h_mem, do_cat, o_mem, lse_m, layer):
        dqm, dmk, dmv = _softmax_bwd(proj, memkv, memkv, do_cat, o_mem, lse_m, name=f"mem_bwd{layer}", B=B, S=S,
                                     Sk=NM, P=PX, q_cb=q_cb, k_cb=0, v_cb=PX, do_cb=PM, causal=False)
        grads[("w_memkv", layer)] = jnp.concatenate(
            [_wgrad(h_mem, dmk, f"g_w_memk{layer}"), _wgrad(h_mem, dmv, f"g_w_memv{layer}")], axis=1)
        _, dg = _mm_nt_rmsbwd([(dmk, 0), (dmv, 1)], full["w_memkv"][layer], mem2d, ln_mem_g[layer],
                              name=f"d_mem_in{layer}", want_dx=False)
        small[("ln_mem_g", layer)] = dg[0]
        return dqm

    dx3 = conv_ffn_bwd(dx4, x3, u1, h_ffn1, a1, 1)
    do_cat1 = _mm_nt(dx3, full["w_out"][1], name="d_o_cat1", tm=min(512, T), tn=512, out_dtype=BF16)
    grads[("w_out", 1)] = _wgrad(o_cat1, dx3, "g_w_out1")
    dq1, dk1, dv1 = _stickbreak_bwd(proj_b, kv, kv, do_cat1, rt1, name="sb_bwd", B=B, S=S, P=PM, q_cb=0, k_cb=0,
                                    v_cb=PM, do_cb=0)
    dqm1 = mem_bwd(proj_b, PM, memkv1, h_mem1, do_cat1, o_mem1, lse_m1, 1)
    grads["w_in_b"] = jnp.concatenate([_wgrad(h_mix1, dq1, "g_w_in_b_q"), _wgrad(h_mix1, dqm1, "g_w_in_b_m")], axis=1)
    dx2, dg = _mm_nt_rmsbwd([(dq1, 0), (dqm1, MAIN_W // MEM_W)], wb, x2, ln_mix_g[1], name="d_mix_in1", dres=dx3)
    small[("ln_mix_g", 1)] = dg[0]
    grads["w_kv"] = jnp.concatenate([_wgrad(h_kv, dk1, "g_w_kv_k"), _wgrad(h_kv, dv1, "g_w_kv_v")], axis=1)
    dx2, dg = _mm_nt_rmsbwd([(dk1, 0), (dv1, 1)], wkv, x2, ln_kv_g, name="d_kv_in", dres=dx2)
    small["ln_kv_g"] = dg[0]
    dx1 = conv_ffn_bwd(dx2, x1, u0, h_ffn0, a0, 0)
    do_cat0 = _mm_nt(dx1, full["w_out"][0], name="d_o_cat0", tm=min(512, T), tn=512, out_dtype=BF16)
    grads[("w_out", 0)] = _wgrad(o_cat0, dx1, "g_w_out0")
    dq0, dk0, dv0, dcs = _softmax_bwd(proj_a, proj_a, proj_a, do_cat0, o_main0, lse0, name="fox_bwd", B=B, S=S,
                                      Sk=S, P=PM, q_cb=0, k_cb=PM, v_cb=2 * PM, do_cb=0, causal=True, cc=cc, cr=cr)
    dqm0 = mem_bwd(proj_a, 3 * PM, memkv0, h_mem0, do_cat0, o_mem0, lse_m0, 0)
    dc2d = _pad_rows(dcs[:, :, 0, :].transpose(0, 2, 1).reshape(T, N_MAIN_HEADS), LANES, 1)
    df, db_f = _forget_cumsum_bwd(dc2d, f_logit, b_f, B=B, S=S, name="forget_cumsum_bwd")
    a_parts = [(dq0, 0), (dk0, 1), (dv0, 2), (dqm0, n_qkv // MEM_W), (df, n_main // LANES)]
    g_wa = jnp.concatenate([_wgrad(h_mix0, p, f"g_w_in_a{k}") for k, (p, _) in enumerate(a_parts)], axis=1)
    grads["w_in_a"] = jnp.concatenate([g_wa[:, :n_qkv], g_wa[:, n_main:n_main + N_MAIN_HEADS], g_wa[:, n_qkv:n_main]],
                                      axis=1)
    dx0, dg = _mm_nt_rmsbwd(a_parts, wa, x2d, ln_mix_g[0], name="d_mix_in0", dres=dx1)
    small[("ln_mix_g", 0)] = dg[0]
    grad_x = dx0.reshape(B, S, D)

    def both(name):
        return jnp.stack([grads[(name, 0)], grads[(name, 1)]])

    gfull = {"w_in_a": grads["w_in_a"], "w_in_b": grads["w_in_b"], "w_kv": grads["w_kv"],
             "w_memkv": both("w_memkv"), "w_out": both("w_out"), "w_up": both("w_up"), "w_down": both("w_down")}
    gparts = [_to_slots(gfull[n], kind).astype(BF16).reshape(N_DEV, -1, LANES) for n, kind in BIG]
    gpack = _pad_rows(jnp.concatenate(gparts, axis=1), PACK_ROWS, 1)

    def both_small(name):
        return jnp.stack([small[(name, 0)], small[(name, 1)]])

    small_list = [("ln_mix_g", both_small("ln_mix_g")), ("b_f_a", db_f[:, :N_MAIN_HEADS]), ("ln_kv_g", small["ln_kv_g"]),
                  ("ln_mem_g", both_small("ln_mem_g")), ("ln_ffn_g", both_small("ln_ffn_g")),
                  ("conv_w", both_small("conv_w")), ("conv_b", both_small("conv_b")), ("final_g", dg_final[0]),
                  ("loss", loss_part[0, :1])]
    sm_rows = []
    for _, a in small_list:
        flat = a.reshape(-1)
        sm_rows.append(_pad_rows(flat, -(-flat.size // LANES) * LANES, 0).reshape(-1, LANES))
    spack = jnp.concatenate(sm_rows, axis=0)
    n_small = spack.shape[0]
    small_rows = -(-n_small // 8) * 8
    spack = _pad_rows(spack, small_rows, 0)

    bout, sout = _exchange(gpack, spack, name="exchange_grads")
    gsum = _sum_slots(bout, name="sum_grads", tr=SUM_TILE)
    ssum = _sum_slots(sout, name="sum_small", tr=small_rows)

    red = {}
    off = 0
    for n, _ in BIG:
        nrows = shards[n].size // LANES
        red[n] = gsum[off:off + nrows].reshape(shards[n].shape)
        off += nrows
    off = 0
    for (n, a), rows in zip(small_list, sm_rows):
        red[n] = ssum[off:off + rows.shape[0]].reshape(-1)[:a.size].reshape(a.shape)
        off += rows.shape[0]
    loss = red["loss"][0]
    shard_cols = conv_w.shape[2]
    red["conv_w"] = lax.dynamic_slice_in_dim(red["conv_w"], my_idx * shard_cols, shard_cols, axis=2)
    red["b_f_a"] = red["b_f_a"].reshape(b_f_a.shape)

    weights = {"ln_mix_g": ln_mix_g, "w_in_a": w_in_a, "b_f_a": b_f_a, "w_in_b": w_in_b, "ln_kv_g": ln_kv_g,
               "w_kv": w_kv, "ln_mem_g": ln_mem_g, "w_memkv": w_memkv, "w_out": w_out, "ln_ffn_g": ln_ffn_g,
               "w_up": w_up, "conv_w": conv_w, "conv_b": conv_b, "w_down": w_down, "final_g": final_g}
    m_in = {"ln_mix_g": m_ln_mix_g, "w_in_a": m_w_in_a, "b_f_a": m_b_f_a, "w_in_b": m_w_in_b, "ln_kv_g": m_ln_kv_g,
            "w_kv": m_w_kv, "ln_mem_g": m_ln_mem_g, "w_memkv": m_w_memkv, "w_out": m_w_out, "ln_ffn_g": m_ln_ffn_g,
            "w_up": m_w_up, "conv_w": m_conv_w, "conv_b": m_conv_b, "w_down": m_w_down, "final_g": m_final_g}
    v_in = {"ln_mix_g": v_ln_mix_g, "w_in_a": v_w_in_a, "b_f_a": v_b_f_a, "w_in_b": v_w_in_b, "ln_kv_g": v_ln_kv_g,
            "w_kv": v_w_kv, "ln_mem_g": v_ln_mem_g, "w_memkv": v_w_memkv, "w_out": v_w_out, "ln_ffn_g": v_ln_ffn_g,
            "w_up": v_w_up, "conv_w": v_conv_w, "conv_b": v_conv_b, "w_down": v_w_down, "final_g": v_final_g}
    order = list(weights)
    big_names = [n for n, _ in BIG]
    g_out, d_out, nm_out, nv_out = {}, {}, {}, {}
    for n in big_names + ["conv_w"]:
        w = weights[n]
        cols = w.shape[-1]
        g = red[n].reshape(w.shape)
        d, nm, nv = _adamw(w.reshape(-1, cols), g.reshape(-1, cols), m_in[n].reshape(-1, cols),
                           v_in[n].reshape(-1, cols), name=f"adamw_{n}")
        g_out[n], d_out[n], nm_out[n], nv_out[n] = g, d.reshape(w.shape), nm.reshape(w.shape), nv.reshape(w.shape)
    small_names = [n for n in order if n not in g_out]

    def pack_small(src):
        rows = []
        for n in small_names:
            flat = src[n].reshape(-1)
            rows.append(_pad_rows(flat, -(-flat.size // LANES) * LANES, 0).reshape(-1, LANES))
        p = jnp.concatenate(rows, axis=0)
        return _pad_rows(p, -(-p.shape[0] // 8) * 8, 0), [r.shape[0] for r in rows]

    red_small = {n: red[n].reshape(weights[n].shape) for n in small_names}
    wp, counts = pack_small(weights)
    gp, _ = pack_small(red_small)
    mp, _ = pack_small(m_in)
    vp, _ = pack_small(v_in)
    dp, nmp, nvp = _adamw(wp, gp, mp, vp, name="adamw_small")
    off = 0
    for n, cnt in zip(small_names, counts):
        shp = weights[n].shape
        size = weights[n].size
        g_out[n] = red_small[n]
        d_out[n] = dp[off:off + cnt].reshape(-1)[:size].reshape(shp)
        nm_out[n] = nmp[off:off + cnt].reshape(-1)[:size].reshape(shp)
        nv_out[n] = nvp[off:off + cnt].reshape(-1)[:size].reshape(shp)
        off += cnt

    return (loss, grad_x, *[g_out[n] for n in order], *[d_out[n] for n in order],
            *[nm_out[n] for n in order], *[nv_out[n] for n in order])
```

```python
import functools

import jax
import jax.numpy as jnp
from jax import lax
from jax.experimental import pallas as pl
from jax.experimental.pallas import tpu as pltpu

F32 = jnp.float32
BF16 = jnp.bfloat16
LANES = 128
HEAD_DIM = 64
N_MAIN_HEADS = 12
N_MEM_HEADS = 4
MAIN_W = N_MAIN_HEADS * HEAD_DIM
MEM_W = N_MEM_HEADS * HEAD_DIM
SCALE = HEAD_DIM ** -0.5
EPS = 1e-6
NEG = -1e30
N_DEV = 8
ATT_TILE = 256
VMEM_BIG = 56 * 2 ** 20
MESH = pl.DeviceIdType.MESH

ADAM_LR = 0.001
ADAM_B1 = 0.9
ADAM_B2 = 0.999
ADAM_EPS = 1e-08
ADAM_WD = 0.01
ADAM_STEP = 10

NT = (((1,), (1,)), ((), ()))
TN = (((0,), (0,)), ((), ()))


def _pc(body, *, name, out_shape, grid=None, in_specs=None, out_specs=None, scratch_shapes=(),
        semantics=None, vmem=None):
    kw = {}
    if grid is not None:
        kw["grid"] = grid
    params = pltpu.CompilerParams(dimension_semantics=semantics, vmem_limit_bytes=vmem)
    return pl.pallas_call(body, name=name, out_shape=out_shape, in_specs=in_specs, out_specs=out_specs,
                          scratch_shapes=list(scratch_shapes), compiler_params=params, **kw)


def _sds(shape, dtype):
    return jax.ShapeDtypeStruct(shape, dtype)


def _mm_fwd(a, w, *, name, tm, tn, out_dtype, g=None, res=None, col0=0, ncols=None, save_h=False):
    m_rows, k = a.shape
    n = w.shape[1] if ncols is None else ncols
    grid = (m_rows // tm, n // tn)
    norm = g is not None

    def body(*refs):
        refs = list(refs)
        a_ref = refs.pop(0)
        g_ref = refs.pop(0) if norm else None
        w_ref = refs.pop(0)
        res_ref = refs.pop(0) if res is not None else None
        o_ref = refs.pop(0)
        hout_ref = refs.pop(0) if save_h else None
        h_ref = refs.pop(0) if norm else None
        if norm:
            @pl.when(pl.program_id(1) == 0)
            def _():
                xv = a_ref[...]
                r = lax.rsqrt(jnp.mean(xv * xv, axis=-1, keepdims=True) + EPS)
                h = ((xv * r) * g_ref[...]).astype(BF16)
                h_ref[...] = h
                if save_h:
                    hout_ref[...] = h
            lhs = h_ref[...]
        else:
            lhs = a_ref[...].astype(BF16)
        acc = jnp.dot(lhs, w_ref[...], preferred_element_type=F32)
        if res is not None:
            acc = acc + res_ref[...]
        o_ref[...] = acc.astype(out_dtype)

    in_specs = [pl.BlockSpec((tm, k), lambda i, j: (i, 0))]
    args = [a]
    if norm:
        in_specs.append(pl.BlockSpec((1, k), lambda i, j: (0, 0)))
        args.append(g.reshape(1, k))
    in_specs.append(pl.BlockSpec((k, tn), lambda i, j: (0, j + col0)))
    args.append(w)
    if res is not None:
        in_specs.append(pl.BlockSpec((tm, tn), lambda i, j: (i, j)))
        args.append(res)
    out_shape = [_sds((m_rows, n), out_dtype)]
    out_specs = [pl.BlockSpec((tm, tn), lambda i, j: (i, j))]
    if save_h:
        out_shape.append(_sds((m_rows, k), BF16))
        out_specs.append(pl.BlockSpec((tm, k), lambda i, j: (i, 0)))
    scratch = [pltpu.VMEM((tm, k), BF16)] if norm else []
    outs = _pc(body, name=name, out_shape=out_shape, grid=grid, in_specs=in_specs, out_specs=out_specs,
               scratch_shapes=scratch, semantics=("arbitrary", "arbitrary"), vmem=VMEM_BIG)(*args)
    return outs if save_h else outs[0]


def _mm_nt(a, w, *, name, tm, tn, out_dtype):
    m_rows, k = a.shape
    n = w.shape[0]

    def body(a_ref, w_ref, o_ref):
        acc = lax.dot_general(a_ref[...].astype(BF16), w_ref[...], NT, preferred_element_type=F32)
        o_ref[...] = acc.astype(out_dtype)

    return _pc(body, name=name, out_shape=_sds((m_rows, n), out_dtype), grid=(m_rows // tm, n // tn),
               in_specs=[pl.BlockSpec((tm, k), lambda i, j: (i, 0)), pl.BlockSpec((tn, k), lambda i, j: (j, 0))],
               out_specs=pl.BlockSpec((tm, tn), lambda i, j: (i, j)),
               semantics=("arbitrary", "arbitrary"), vmem=VMEM_BIG)(a, w)


def _mm_tn(a, b, *, name, ta, tn, tt):
    t_rows, ka = a.shape
    n = b.shape[1]

    def body(a_ref, b_ref, o_ref):
        @pl.when(pl.program_id(2) == 0)
        def _():
            o_ref[...] = jnp.zeros_like(o_ref)
        o_ref[...] += lax.dot_general(a_ref[...].astype(BF16), b_ref[...].astype(BF16), TN,
                                      preferred_element_type=F32)

    return _pc(body, name=name, out_shape=_sds((ka, n), F32), grid=(ka // ta, n // tn, t_rows // tt),
               in_specs=[pl.BlockSpec((tt, ta), lambda i, j, t: (t, i)),
                         pl.BlockSpec((tt, tn), lambda i, j, t: (t, j))],
               out_specs=pl.BlockSpec((ta, tn), lambda i, j, t: (i, j)),
               semantics=("arbitrary", "arbitrary", "arbitrary"), vmem=VMEM_BIG)(a, b)


def _wgrad(a, b, name):
    t_rows, ka = a.shape
    n = b.shape[1]
    ta = ka if ka <= 1024 else ka // 2
    tn = n
    while ta * tn * 4 > 6 * 2 ** 20 and tn % 256 == 0:
        tn //= 2
    tt = min(512, t_rows)
    return _mm_tn(a, b, name=name, ta=ta, tn=tn, tt=tt)


def _mm_nt_rmsbwd(parts, w, x, g, *, name, dres=None, want_dx=True):
    m_rows, d = x.shape
    tm = min(256, m_rows)
    n_parts = len(parts)

    def body(*refs):
        refs = list(refs)
        dy_refs = [refs.pop(0) for _ in range(n_parts)]
        w_refs = [refs.pop(0) for _ in range(n_parts)]
        x_ref = refs.pop(0)
        g_ref = refs.pop(0)
        dres_ref = refs.pop(0) if dres is not None else None
        dx_ref = refs.pop(0) if want_dx else None
        dg_ref = refs.pop(0)

        @pl.when(pl.program_id(0) == 0)
        def _():
            dg_ref[...] = jnp.zeros_like(dg_ref)

        dh = None
        for dy_ref, w_ref in zip(dy_refs, w_refs):
            t = lax.dot_general(dy_ref[...].astype(BF16), w_ref[...], NT, preferred_element_type=F32)
            dh = t if dh is None else dh + t
        xv = x_ref[...]
        r = lax.rsqrt(jnp.mean(xv * xv, axis=-1, keepdims=True) + EPS)
        xh = xv * r
        dg_ref[...] += jnp.sum(dh * xh, axis=0, keepdims=True)
        if want_dx:
            dhg = dh * g_ref[...]
            dx = r * (dhg - xh * jnp.mean(dhg * xh, axis=-1, keepdims=True))
            if dres is not None:
                dx = dx + dres_ref[...]
            dx_ref[...] = dx

    in_specs, args = [], []
    for dy, _ in parts:
        in_specs.append(pl.BlockSpec((tm, dy.shape[1]), lambda i: (i, 0)))
        args.append(dy)
    for dy, cb in parts:
        in_specs.append(pl.BlockSpec((d, dy.shape[1]), functools.partial(lambda i, cb: (0, cb), cb=cb)))
        args.append(w)
    in_specs += [pl.BlockSpec((tm, d), lambda i: (i, 0)), pl.BlockSpec((1, d), lambda i: (0, 0))]
    args += [x, g.reshape(1, d)]
    if dres is not None:
        in_specs.append(pl.BlockSpec((tm, d), lambda i: (i, 0)))
        args.append(dres)
    out_shape, out_specs = [], []
    if want_dx:
        out_shape.append(_sds((m_rows, d), F32))
        out_specs.append(pl.BlockSpec((tm, d), lambda i: (i, 0)))
    out_shape.append(_sds((1, d), F32))
    out_specs.append(pl.BlockSpec((1, d), lambda i: (0, 0)))
    outs = _pc(body, name=name, out_shape=out_shape, grid=(m_rows // tm,), in_specs=in_specs,
               out_specs=out_specs, semantics=("arbitrary",), vmem=VMEM_BIG)(*args)
    return (outs[0], outs[1]) if want_dx else (None, outs[0])


def _loss_head(x, g, tgt, *, name):
    m_rows, d = x.shape
    tm = min(256, m_rows)

    def body(x_ref, g_ref, t_ref, dx_ref, dg_ref, loss_ref):
        @pl.when(pl.program_id(0) == 0)
        def _():
            dg_ref[...] = jnp.zeros_like(dg_ref)
            loss_ref[...] = jnp.zeros_like(loss_ref)

        xv = x_ref[...]
        r = lax.rsqrt(jnp.mean(xv * xv, axis=-1, keepdims=True) + EPS)
        xh = xv * r
        gv = g_ref[...]
        err = xh * gv - t_ref[...]
        per_tok = jnp.mean(err * err, axis=-1, keepdims=True)
        loss_ref[...] += 0.5 * jnp.sum(per_tok, axis=0, keepdims=True)
        dout = err * (1.0 / d)
        dg_ref[...] += jnp.sum(dout * xh, axis=0, keepdims=True)
        dhg = dout * gv
        dx_ref[...] = r * (dhg - xh * jnp.mean(dhg * xh, axis=-1, keepdims=True))

    row = pl.BlockSpec((tm, d), lambda i: (i, 0))
    return _pc(body, name=name, out_shape=[_sds((m_rows, d), F32), _sds((1, d), F32), _sds((1, LANES), F32)],
               grid=(m_rows // tm,), in_specs=[row, pl.BlockSpec((1, d), lambda i: (0, 0)), row],
               out_specs=[row, pl.BlockSpec((1, d), lambda i: (0, 0)), pl.BlockSpec((1, LANES), lambda i: (0, 0))],
               semantics=("arbitrary",))(x, g.reshape(1, d), tgt)


def _split3(v):
    hi = v.astype(BF16)
    r1 = v - hi.astype(F32)
    mid = r1.astype(BF16)
    lo = (r1 - mid.astype(F32)).astype(BF16)
    return hi, mid, lo


def _split2(v):
    hi = v.astype(BF16)
    lo = (v - hi.astype(F32)).astype(BF16)
    return hi, lo


def _tri_dot3(tri, v):
    hi, mid, lo = _split3(v)
    return (jnp.dot(tri, hi, preferred_element_type=F32) + jnp.dot(tri, mid, preferred_element_type=F32)
            + jnp.dot(tri, lo, preferred_element_type=F32))


def _dot_tri2(v, tri):
    hi, lo = _split2(v)
    return jnp.dot(hi, tri, preferred_element_type=F32) + jnp.dot(lo, tri, preferred_element_type=F32)


def _log_sigmoid(v):
    return jnp.minimum(v, 0.0) - jnp.log(1.0 + jnp.exp(-jnp.abs(v)))


def _forget_cumsum(f_logit, b_f, *, B, S, name):
    ch = min(256, S)
    nch = S // ch

    def body(f_ref, b_ref, c_ref):
        r_i = lax.broadcasted_iota(jnp.int32, (ch, ch), 0)
        c_i = lax.broadcasted_iota(jnp.int32, (ch, ch), 1)
        tri = (c_i <= r_i).astype(BF16)
        bv = b_ref[...]

        def step(k, carry):
            rows = pl.ds(pl.multiple_of(k * ch, ch), ch)
            lf = _log_sigmoid(f_ref[rows, :] + bv)
            c_ref[rows, :] = _tri_dot3(tri, lf) + carry
            return carry + jnp.sum(lf, axis=0, keepdims=True)

        lax.fori_loop(0, nch, step, jnp.zeros((1, LANES), F32))

    blk = pl.BlockSpec((S, LANES), lambda b: (b, 0))
    return _pc(body, name=name, out_shape=_sds((B * S, LANES), F32), grid=(B,),
               in_specs=[blk, pl.BlockSpec((1, LANES), lambda b: (0, 0))], out_specs=blk,
               semantics=("arbitrary",))(f_logit, b_f)


def _forget_cumsum_bwd(dc, f_logit, b_f, *, B, S, name):
    ch = min(256, S)
    nch = S // ch

    def body(dc_ref, f_ref, b_ref, df_ref, db_ref):
        @pl.when(pl.program_id(0) == 0)
        def _():
            db_ref[...] = jnp.zeros_like(db_ref)

        r_i = lax.broadcasted_iota(jnp.int32, (ch, ch), 0)
        c_i = lax.broadcasted_iota(jnp.int32, (ch, ch), 1)
        tri = (c_i >= r_i).astype(BF16)
        bv = b_ref[...]

        def step(kk, carry):
            tail, dbs = carry
            k = nch - 1 - kk
            rows = pl.ds(pl.multiple_of(k * ch, ch), ch)
            dcv = dc_ref[rows, :]
            dlf = _tri_dot3(tri, dcv) + tail
            z = f_ref[rows, :] + bv
            df = dlf * (1.0 / (1.0 + jnp.exp(z)))
            df_ref[rows, :] = df.astype(BF16)
            return tail + jnp.sum(dcv, axis=0, keepdims=True), dbs + jnp.sum(df, axis=0, keepdims=True)

        zero = jnp.zeros((1, LANES), F32)
        _, dbs = lax.fori_loop(0, nch, step, (zero, zero))
        db_ref[...] += dbs

    blk = pl.BlockSpec((S, LANES), lambda b: (b, 0))
    one = pl.BlockSpec((1, LANES), lambda b: (0, 0))
    return _pc(body, name=name, out_shape=[_sds((B * S, LANES), BF16), _sds((1, LANES), F32)], grid=(B,),
               in_specs=[blk, blk, one], out_specs=[blk, one], semantics=("arbitrary",))(dc, f_logit, b_f)


def _head_mask(lane, hh):
    return (lane < HEAD_DIM) if hh == 0 else (lane >= HEAD_DIM)


def _col_spec(rows, nblk_rows, cb):
    return pl.BlockSpec((rows, LANES), lambda b, p, i: (b * nblk_rows + i, cb + p))


def _kv_spec(rows, cb):
    return pl.BlockSpec((rows, LANES), lambda b, p, i: (b, cb + p))


def _stat_col_spec(tq):
    return pl.BlockSpec((1, 2, tq, 1), lambda b, p, i: (b, p, i, 0))


def _stat_row_spec(S):
    return pl.BlockSpec((1, 2, 1, S), lambda b, p, i: (b, p, 0, 0))


def _softmax_fwd(qa, ka, va, *, name, B, S, Sk, P, q_cb, k_cb, v_cb, causal, cc=None, cr=None):
    tq = min(ATT_TILE, S)
    tk = min(ATT_TILE, Sk)
    nq, nk = S // tq, Sk // tk
    decay = cc is not None
    assert not causal or (tq == tk and S == Sk)

    def body(*refs):
        if decay:
            q_ref, k_ref, v_ref, cc_ref, cr_ref, o_ref, lse_ref = refs
        else:
            q_ref, k_ref, v_ref, o_ref, lse_ref = refs
        i = pl.program_id(2)
        q = q_ref[...]
        lane = lax.broadcasted_iota(jnp.int32, (tq, LANES), 1)
        row = lax.broadcasted_iota(jnp.int32, (tq, tk), 0) + i * tq
        col0 = lax.broadcasted_iota(jnp.int32, (tq, tk), 1)
        outs = []
        for hh in range(2):
            qh = jnp.where(_head_mask(lane, hh), q, jnp.zeros_like(q))

            def step(kb, carry, hh=hh, qh=qh):
                m, l, acc = carry
                ks = pl.multiple_of(kb * tk, tk)
                kblk = k_ref[pl.ds(ks, tk), :]
                vblk = v_ref[pl.ds(ks, tk), :]
                s = lax.dot_general(qh, kblk, NT, preferred_element_type=F32) * SCALE
                if decay:
                    s = s + (cc_ref[0, hh] - cr_ref[0, hh, :, pl.ds(ks, tk)])
                if causal:
                    s = jnp.where(col0 + kb * tk <= row, s, NEG)
                m_new = jnp.maximum(m, jnp.max(s, axis=-1, keepdims=True))
                alpha = jnp.exp(m - m_new)
                p = jnp.exp(s - m_new)
                l = alpha * l + jnp.sum(p, axis=-1, keepdims=True)
                acc = alpha * acc + jnp.dot(p.astype(BF16), vblk, preferred_element_type=F32)
                return m_new, l, acc

            init = (jnp.full((tq, 1), NEG, F32), jnp.zeros((tq, 1), F32), jnp.zeros((tq, LANES), F32))
            m, l, acc = lax.fori_loop(0, (i + 1) if causal else nk, step, init)
            outs.append(acc / l)
            lse_ref[0, hh] = m + jnp.log(l)
        o_ref[...] = jnp.where(lane < HEAD_DIM, outs[0], outs[1]).astype(BF16)

    in_specs = [_col_spec(tq, nq, q_cb), _kv_spec(Sk, k_cb), _kv_spec(Sk, v_cb)]
    args = [qa, ka, va]
    if decay:
        in_specs += [_stat_col_spec(tq), _stat_row_spec(S)]
        args += [cc, cr]
    return _pc(body, name=name,
               out_shape=[_sds((B * S, P * LANES), BF16), _sds((B, 2 * P, S, 1), F32)],
               grid=(B, P, nq), in_specs=in_specs, out_specs=[_col_spec(tq, nq, 0), _stat_col_spec(tq)],
               semantics=("arbitrary", "arbitrary", "arbitrary"), vmem=VMEM_BIG)(*args)


def _softmax_bwd(qa, ka, va, doa, oa, lse, *, name, B, S, Sk, P, q_cb, k_cb, v_cb, do_cb, causal,
                 cc=None, cr=None):
    tq = min(ATT_TILE, S)
    tk = min(ATT_TILE, Sk)
    nq, nk = S // tq, Sk // tk
    decay = cc is not None

    def body(*refs):
        if decay:
            q_ref, k_ref, v_ref, do_ref, o_ref, lse_ref, cc_ref, cr_ref, dq_ref, dk_ref, dv_ref, dcs_ref = refs
        else:
            q_ref, k_ref, v_ref, do_ref, o_ref, lse_ref, dq_ref, dk_ref, dv_ref = refs
        i = pl.program_id(2)

        @pl.when(i == 0)
        def _():
            dk_ref[...] = jnp.zeros_like(dk_ref)
            dv_ref[...] = jnp.zeros_like(dv_ref)
            if decay:
                dcs_ref[...] = jnp.zeros_like(dcs_ref)

        q = q_ref[...]
        do = do_ref[...]
        prod = do.astype(F32) * o_ref[...].astype(F32)
        lane = lax.broadcasted_iota(jnp.int32, (tq, LANES), 1)
        row = lax.broadcasted_iota(jnp.int32, (tq, tk), 0) + i * tq
        col0 = lax.broadcasted_iota(jnp.int32, (tq, tk), 1)
        dqs = []
        for hh in range(2):
            hmask = _head_mask(lane, hh)
            qh = jnp.where(hmask, q, jnp.zeros_like(q))
            doh = jnp.where(hmask, do, jnp.zeros_like(do))
            lse_h = lse_ref[0, hh]
            n_blocks = (i + 1) if causal else nk

            def probs(kb, hh=hh, qh=qh, doh=doh, lse_h=lse_h):
                ks = pl.multiple_of(kb * tk, tk)
                kblk = k_ref[pl.ds(ks, tk), :]
                vblk = v_ref[pl.ds(ks, tk), :]
                s = lax.dot_general(qh, kblk, NT, preferred_element_type=F32) * SCALE
                if decay:
                    s = s + (cc_ref[0, hh] - cr_ref[0, hh, :, pl.ds(ks, tk)])
                if causal:
                    s = jnp.where(col0 + kb * tk <= row, s, NEG)
                p = jnp.exp(s - lse_h)
                dp = lax.dot_general(doh, vblk, NT, preferred_element_type=F32)
                return ks, kblk, p, dp

            if decay:
                def delta_step(kb, acc):
                    _, _, p, dp = probs(kb)
                    return acc + jnp.sum(p * dp, axis=-1, keepdims=True)

                delta = lax.fori_loop(0, n_blocks, delta_step, jnp.zeros((tq, 1), F32))
            else:
                delta = jnp.sum(jnp.where(hmask, prod, 0.0), axis=-1, keepdims=True)

            def step(kb, dq_acc, hh=hh, qh=qh, doh=doh, delta=delta):
                ks, kblk, p, dp = probs(kb)
                ds = p * (dp - delta)
                dsb = ds.astype(BF16)
                dk_ref[pl.ds(ks, tk), :] += lax.dot_general(dsb, qh, TN, preferred_element_type=F32) * SCALE
                dv_ref[pl.ds(ks, tk), :] += lax.dot_general(p.astype(BF16), doh, TN, preferred_element_type=F32)
                if decay:
                    dcs_ref[0, hh, :, pl.ds(ks, tk)] -= jnp.sum(ds, axis=0, keepdims=True)
                return dq_acc + jnp.dot(dsb, kblk, preferred_element_type=F32)

            dqs.append(lax.fori_loop(0, n_blocks, step, jnp.zeros((tq, LANES), F32)) * SCALE)
        dq_ref[...] = jnp.where(lane < HEAD_DIM, dqs[0], dqs[1]).astype(BF16)

    in_specs = [_col_spec(tq, nq, q_cb), _kv_spec(Sk, k_cb), _kv_spec(Sk, v_cb), _col_spec(tq, nq, do_cb),
                _col_spec(tq, nq, 0), _stat_col_spec(tq)]
    args = [qa, ka, va, doa, oa, lse]
    out_shape = [_sds((B * S, P * LANES), BF16), _sds((B * Sk, P * LANES), F32), _sds((B * Sk, P * LANES), F32)]
    out_specs = [_col_spec(tq, nq, 0), _kv_spec(Sk, 0), _kv_spec(Sk, 0)]
    if decay:
        in_specs += [_stat_col_spec(tq), _stat_row_spec(S)]
        args += [cc, cr]
        out_shape.append(_sds((B, 2 * P, 1, S), F32))
        out_specs.append(_stat_row_spec(S))
    return _pc(body, name=name, out_shape=out_shape, grid=(B, P, nq), in_specs=in_specs, out_specs=out_specs,
               semantics=("arbitrary", "arbitrary", "arbitrary"), vmem=VMEM_BIG)(*args)


def _sb_terms(qh, kblk, row, col0, kb, tk):
    z = lax.dot_general(qh, kblk, NT, preferred_element_type=F32) * SCALE
    causal = (col0 + kb * tk) < row
    sp = jnp.maximum(z, 0.0) + jnp.log(1.0 + jnp.exp(-jnp.abs(z)))
    ls = z - sp
    lm = jnp.where(causal, -sp, 0.0)
    return causal, ls, lm


def _stickbreak_fwd(qa, ka, va, *, name, B, S, P, q_cb, k_cb, v_cb):
    tq = tk = min(ATT_TILE, S)
    nq = S // tq

    def body(q_ref, k_ref, v_ref, o_ref, rt_ref):
        i = pl.program_id(2)
        q = q_ref[...]
        lane = lax.broadcasted_iota(jnp.int32, (tq, LANES), 1)
        row = lax.broadcasted_iota(jnp.int32, (tq, tk), 0) + i * tq
        col0 = lax.broadcasted_iota(jnp.int32, (tq, tk), 1)
        t_r = lax.broadcasted_iota(jnp.int32, (tk, tk), 0)
        t_c = lax.broadcasted_iota(jnp.int32, (tk, tk), 1)
        after = (t_r > t_c).astype(BF16)
        outs = []
        for hh in range(2):
            qh = jnp.where(_head_mask(lane, hh), q, jnp.zeros_like(q))

            def step(jj, carry, qh=qh):
                run, acc = carry
                kb = i - jj
                ks = pl.multiple_of(kb * tk, tk)
                kblk = k_ref[pl.ds(ks, tk), :]
                vblk = v_ref[pl.ds(ks, tk), :]
                causal, ls, lm = _sb_terms(qh, kblk, row, col0, kb, tk)
                suf = _dot_tri2(lm, after)
                a = jnp.where(causal, jnp.exp(ls + run + suf), 0.0)
                acc = acc + jnp.dot(a.astype(BF16), vblk, preferred_element_type=F32)
                return run + jnp.sum(lm, axis=-1, keepdims=True), acc

            run, acc = lax.fori_loop(0, i + 1, step, (jnp.zeros((tq, 1), F32), jnp.zeros((tq, LANES), F32)))
            outs.append(acc)
            rt_ref[0, hh] = run
        o_ref[...] = jnp.where(lane < HEAD_DIM, outs[0], outs[1]).astype(BF16)

    return _pc(body, name=name, out_shape=[_sds((B * S, P * LANES), BF16), _sds((B, 2 * P, S, 1), F32)],
               grid=(B, P, nq), in_specs=[_col_spec(tq, nq, q_cb), _kv_spec(S, k_cb), _kv_spec(S, v_cb)],
               out_specs=[_col_spec(tq, nq, 0), _stat_col_spec(tq)],
               semantics=("arbitrary", "arbitrary", "arbitrary"), vmem=VMEM_BIG)(qa, ka, va)


def _stickbreak_bwd(qa, ka, va, doa, rt, *, name, B, S, P, q_cb, k_cb, v_cb, do_cb):
    tq = tk = min(ATT_TILE, S)
    nq = S // tq

    def body(q_ref, k_ref, v_ref, do_ref, rt_ref, dq_ref, dk_ref, dv_ref):
        i = pl.program_id(2)

        @pl.when(i == 0)
        def _():
            dk_ref[...] = jnp.zeros_like(dk_ref)
            dv_ref[...] = jnp.zeros_like(dv_ref)

        q = q_ref[...]
        do = do_ref[...]
        lane = lax.broadcasted_iota(jnp.int32, (tq, LANES), 1)
        row = lax.broadcasted_iota(jnp.int32, (tq, tk), 0) + i * tq
        col0 = lax.broadcasted_iota(jnp.int32, (tq, tk), 1)
        t_r = lax.broadcasted_iota(jnp.int32, (tk, tk), 0)
        t_c = lax.broadcasted_iota(jnp.int32, (tk, tk), 1)
        upto = (t_r <= t_c).astype(BF16)
        before = (t_r < t_c).astype(BF16)
        dqs = []
        for hh in range(2):
            hmask = _head_mask(lane, hh)
            qh = jnp.where(hmask, q, jnp.zeros_like(q))
            doh = jnp.where(hmask, do, jnp.zeros_like(do))
            rt_h = rt_ref[0, hh]

            def step(kb, carry, qh=qh, doh=doh, rt_h=rt_h):
                pl_sum, pg_sum, dq_acc = carry
                ks = pl.multiple_of(kb * tk, tk)
                kblk = k_ref[pl.ds(ks, tk), :]
                vblk = v_ref[pl.ds(ks, tk), :]
                causal, ls, lm = _sb_terms(qh, kblk, row, col0, kb, tk)
                pin = _dot_tri2(lm, upto)
                a = jnp.where(causal, jnp.exp(ls + (rt_h - pl_sum) - pin), 0.0)
                da = lax.dot_general(doh, vblk, NT, preferred_element_type=F32)
                gm = a * da
                pg = _dot_tri2(gm, before) + pg_sum
                beta = jnp.exp(ls)
                dz = jnp.where(causal, gm * (1.0 - beta) - pg * beta, 0.0)
                dzb = dz.astype(BF16)
                dk_ref[pl.ds(ks, tk), :] += lax.dot_general(dzb, qh, TN, preferred_element_type=F32) * SCALE
                dv_ref[pl.ds(ks, tk), :] += lax.dot_general(a.astype(BF16), doh, TN, preferred_element_type=F32)
                return (pl_sum + jnp.sum(lm, axis=-1, keepdims=True),
                        pg_sum + jnp.sum(gm, axis=-1, keepdims=True),
                        dq_acc + jnp.dot(dzb, kblk, preferred_element_type=F32))

            zc = jnp.zeros((tq, 1), F32)
            _, _, dq_h = lax.fori_loop(0, i + 1, step, (zc, zc, jnp.zeros((tq, LANES), F32)))
            dqs.append(dq_h * SCALE)
        dq_ref[...] = jnp.where(lane < HEAD_DIM, dqs[0], dqs[1]).astype(BF16)

    return _pc(body, name=name,
               out_shape=[_sds((B * S, P * LANES), BF16), _sds((B * S, P * LANES), F32), _sds((B * S, P * LANES), F32)],
               grid=(B, P, nq),
               in_specs=[_col_spec(tq, nq, q_cb), _kv_spec(S, k_cb), _kv_spec(S, v_cb), _col_spec(tq, nq, do_cb),
                         _stat_col_spec(tq)],
               out_specs=[_col_spec(tq, nq, 0), _kv_spec(S, 0), _kv_spec(S, 0)],
               semantics=("arbitrary", "arbitrary", "arbitrary"), vmem=VMEM_BIG)(qa, ka, va, doa, rt)


def _pair_setup(q_ref, tq, tk):
    q = q_ref[...] * jnp.asarray(SCALE, BF16)
    lane = lax.broadcasted_iota(jnp.int32, (tq, LANES), 1)
    masks = [_head_mask(lane, hh) for hh in range(2)]
    qh = [jnp.where(mk, q, jnp.zeros_like(q)) for mk in masks]
    on_or_below = (lax.broadcasted_iota(jnp.int32, (tq, tk), 1) <= lax.broadcasted_iota(jnp.int32, (tq, tk), 0))
    return lane, masks, qh, on_or_below


def _fox_fwd(qa, ka, va, cr, *, name, B, S, P, q_cb, k_cb, v_cb):
    tq = tk = min(ATT_TILE, S)
    nq = S // tq

    def body(q_ref, k_ref, v_ref, cr_ref, o_ref, lse_ref, s_buf, acc_ref, m_ref, l_ref):
        i = pl.program_id(2)
        lane, _, qh, on_or_below = _pair_setup(q_ref, tq, tk)
        m_ref[...] = jnp.full(m_ref.shape, NEG, F32)
        l_ref[...] = jnp.zeros(l_ref.shape, F32)
        acc_ref[...] = jnp.zeros(acc_ref.shape, F32)

        def scores(kb, slot):
            kblk = k_ref[pl.ds(pl.multiple_of(kb * tk, tk), tk), :]
            for hh in range(2):
                s_buf[slot, hh] = lax.dot_general(qh[hh], kblk, NT, preferred_element_type=F32)

        def block(kb, slot, diag):
            ks = pl.multiple_of(kb * tk, tk)
            vblk = v_ref[pl.ds(ks, tk), :]
            ps = []
            for hh in range(2):
                s = s_buf[slot, hh] - cr_ref[0, hh, :, pl.ds(ks, tk)]
                if diag:
                    s = jnp.where(on_or_below, s, NEG)
                m = m_ref[hh]
                m_new = jnp.maximum(m, jnp.max(s, axis=-1, keepdims=True))
                alpha = jnp.exp(m - m_new)
                p = jnp.exp(s - m_new)
                m_ref[hh] = m_new
                l_ref[hh] = alpha * l_ref[hh] + jnp.sum(p, axis=-1, keepdims=True)
                ps.append((alpha, p.astype(BF16)))
            for hh in range(2):
                acc_ref[hh] = ps[hh][0] * acc_ref[hh] + jnp.dot(ps[hh][1], vblk, preferred_element_type=F32)

        def step(kb, _):
            slot = lax.rem(kb, 2)
            scores(kb + 1, 1 - slot)
            block(kb, slot, False)
            return 0

        scores(0, 0)
        lax.fori_loop(0, i, step, 0)
        block(i, lax.rem(i, 2), True)
        l0, l1 = l_ref[0], l_ref[1]
        lse_ref[0, 0] = m_ref[0] + jnp.log(l0)
        lse_ref[0, 1] = m_ref[1] + jnp.log(l1)
        o_ref[...] = jnp.where(lane < HEAD_DIM, acc_ref[0] / l0, acc_ref[1] / l1).astype(BF16)

    return _pc(body, name=name, out_shape=[_sds((B * S, P * LANES), BF16), _sds((B, 2 * P, S, 1), F32)],
               grid=(B, P, nq),
               in_specs=[_col_spec(tq, nq, q_cb), _kv_spec(S, k_cb), _kv_spec(S, v_cb), _stat_row_spec(S)],
               out_specs=[_col_spec(tq, nq, 0), _stat_col_spec(tq)],
               scratch_shapes=[pltpu.VMEM((2, 2, tq, tk), F32), pltpu.VMEM((2, tq, LANES), F32),
                               pltpu.VMEM((2, tq, 1), F32), pltpu.VMEM((2, tq, 1), F32)],
               semantics=("arbitrary", "arbitrary", "arbitrary"), vmem=VMEM_BIG)(qa, ka, va, cr)


def _fox_bwd(qa, ka, va, doa, lse, cr, *, name, B, S, P, q_cb, k_cb, v_cb, do_cb):
    tq = tk = min(ATT_TILE, S)
    nq = S // tq

    def body(q_ref, k_ref, v_ref, do_ref, lse_ref, cr_ref, dq_ref, dk_ref, dv_ref, dcs_ref):
        i = pl.program_id(2)

        @pl.when(i == 0)
        def _():
            dk_ref[...] = jnp.zeros_like(dk_ref)
            dv_ref[...] = jnp.zeros_like(dv_ref)
            dcs_ref[...] = jnp.zeros_like(dcs_ref)

        lane, masks, qh, on_or_below = _pair_setup(q_ref, tq, tk)
        do = do_ref[...]
        doh = [jnp.where(mk, do, jnp.zeros_like(do)) for mk in masks]
        lse_h = [lse_ref[0, hh] for hh in range(2)]

        def probs(kb, hh, diag):
            ks = pl.multiple_of(kb * tk, tk)
            kblk = k_ref[pl.ds(ks, tk), :]
            vblk = v_ref[pl.ds(ks, tk), :]
            s = lax.dot_general(qh[hh], kblk, NT, preferred_element_type=F32) - cr_ref[0, hh, :, pl.ds(ks, tk)]
            if diag:
                s = jnp.where(on_or_below, s, NEG)
            p = jnp.exp(s - lse_h[hh])
            dp = lax.dot_general(doh[hh], vblk, NT, preferred_element_type=F32)
            return ks, kblk, p, dp

        def delta_block(kb, carry, diag):
            out = []
            for hh in range(2):
                _, _, p, dp = probs(kb, hh, diag)
                out.append(carry[hh] + jnp.sum(p * dp, axis=-1, keepdims=True))
            return tuple(out)

        zc = jnp.zeros((tq, 1), F32)
        delta = lax.fori_loop(0, i, lambda kb, c: delta_block(kb, c, False), (zc, zc))
        delta = delta_block(i, delta, True)

        def grad_block(kb, carry, diag):
            out = []
            for hh in range(2):
                ks, kblk, p, dp = probs(kb, hh, diag)
                ds = p * (dp - delta[hh])
                dsb = ds.astype(BF16)
                rows = pl.ds(ks, tk)
                dk_ref[rows, :] += lax.dot_general(dsb, qh[hh], TN, preferred_element_type=F32)
                dv_ref[rows, :] += lax.dot_general(p.astype(BF16), doh[hh], TN, preferred_element_type=F32)
                dcs_ref[0, hh, :, rows] -= jnp.sum(ds, axis=0, keepdims=True)
                out.append(carry[hh] + jnp.dot(dsb, kblk, preferred_element_type=F32))
            return tuple(out)

        za = jnp.zeros((tq, LANES), F32)
        dq = lax.fori_loop(0, i, lambda kb, c: grad_block(kb, c, False), (za, za))
        dq = grad_block(i, dq, True)
        dq_ref[...] = (jnp.where(lane < HEAD_DIM, dq[0], dq[1]) * SCALE).astype(BF16)

    return _pc(body, name=name,
               out_shape=[_sds((B * S, P * LANES), BF16), _sds((B * S, P * LANES), F32), _sds((B * S, P * LANES), F32),
                          _sds((B, 2 * P, 1, S), F32)],
               grid=(B, P, nq),
               in_specs=[_col_spec(tq, nq, q_cb), _kv_spec(S, k_cb), _kv_spec(S, v_cb), _col_spec(tq, nq, do_cb),
                         _stat_col_spec(tq), _stat_row_spec(S)],
               out_specs=[_col_spec(tq, nq, 0), _kv_spec(S, 0), _kv_spec(S, 0), _stat_row_spec(S)],
               semantics=("arbitrary", "arbitrary", "arbitrary"), vmem=VMEM_BIG)(qa, ka, va, doa, lse, cr)


def _sb_logs(qh, kblk):
    z = lax.dot_general(qh, kblk, NT, preferred_element_type=F32)
    nz = -z
    lg = jnp.log(1.0 + jnp.exp(jnp.minimum(z, nz)))
    lm = jnp.minimum(nz, 0.0) - lg
    return lm + z, lm


def _sb_fwd(qa, ka, va, *, name, B, S, P, q_cb, k_cb, v_cb):
    tq = tk = min(ATT_TILE, S)
    nq = S // tq

    def body(q_ref, k_ref, v_ref, o_ref, rt_ref):
        i = pl.program_id(2)
        lane, _, qh, on_or_below = _pair_setup(q_ref, tq, tk)
        t_r = lax.broadcasted_iota(jnp.int32, (tk, tk), 0)
        t_c = lax.broadcasted_iota(jnp.int32, (tk, tk), 1)
        after = (t_r > t_c).astype(BF16)
        below = t_c < t_r

        def block(kb, carry, diag):
            ks = pl.multiple_of(kb * tk, tk)
            kblk = k_ref[pl.ds(ks, tk), :]
            vblk = v_ref[pl.ds(ks, tk), :]
            out = []
            for hh in range(2):
                run, acc = carry[hh]
                ls, lm = _sb_logs(qh[hh], kblk)
                if diag:
                    lm = jnp.where(below, lm, 0.0)
                a = jnp.exp(ls + run + _dot_tri2(lm, after))
                if diag:
                    a = jnp.where(below, a, 0.0)
                acc = acc + jnp.dot(a.astype(BF16), vblk, preferred_element_type=F32)
                out.append((run + jnp.sum(lm, axis=-1, keepdims=True), acc))
            return tuple(out)

        one = (jnp.zeros((tq, 1), F32), jnp.zeros((tq, LANES), F32))
        carry = block(i, (one, one), True)
        (r0, a0), (r1, a1) = lax.fori_loop(0, i, lambda jj, c: block(i - 1 - jj, c, False), carry)
        rt_ref[0, 0] = r0
        rt_ref[0, 1] = r1
        o_ref[...] = jnp.where(lane < HEAD_DIM, a0, a1).astype(BF16)

    return _pc(body, name=name, out_shape=[_sds((B * S, P * LANES), BF16), _sds((B, 2 * P, S, 1), F32)],
               grid=(B, P, nq), in_specs=[_col_spec(tq, nq, q_cb), _kv_spec(S, k_cb), _kv_spec(S, v_cb)],
               out_specs=[_col_spec(tq, nq, 0), _stat_col_spec(tq)],
               semantics=("arbitrary", "arbitrary", "arbitrary"), vmem=VMEM_BIG)(qa, ka, va)


def _sb_bwd(qa, ka, va, doa, rt, *, name, B, S, P, q_cb, k_cb, v_cb, do_cb):
    tq = tk = min(ATT_TILE, S)
    nq = S // tq

    def body(q_ref, k_ref, v_ref, do_ref, rt_ref, dq_ref, dk_ref, dv_ref):
        i = pl.program_id(2)

        @pl.when(i == 0)
        def _():
            dk_ref[...] = jnp.zeros_like(dk_ref)
            dv_ref[...] = jnp.zeros_like(dv_ref)

        lane, masks, qh, _ = _pair_setup(q_ref, tq, tk)
        do = do_ref[...]
        doh = [jnp.where(mk, do, jnp.zeros_like(do)) for mk in masks]
        rt_h = [rt_ref[0, hh] for hh in range(2)]
        t_r = lax.broadcasted_iota(jnp.int32, (tk, tk), 0)
        t_c = lax.broadcasted_iota(jnp.int32, (tk, tk), 1)
        upto = (t_r <= t_c).astype(BF16)
        before = (t_r < t_c).astype(BF16)
        below = t_c < t_r

        def block(kb, carry, diag):
            ks = pl.multiple_of(kb * tk, tk)
            rows = pl.ds(ks, tk)
            kblk = k_ref[rows, :]
            vblk = v_ref[rows, :]
            out = []
            for hh in range(2):
                pl_sum, pg_sum, dq_acc = carry[hh]
                ls, lm = _sb_logs(qh[hh], kblk)
                if diag:
                    lm = jnp.where(below, lm, 0.0)
                a = jnp.exp(ls + (rt_h[hh] - pl_sum) - _dot_tri2(lm, upto))
                if diag:
                    a = jnp.where(below, a, 0.0)
                gm = a * lax.dot_general(doh[hh], vblk, NT, preferred_element_type=F32)
                pg = _dot_tri2(gm, before) + pg_sum
                dz = gm - jnp.exp(ls) * (gm + pg)
                if diag:
                    dz = jnp.where(below, dz, 0.0)
                dzb = dz.astype(BF16)
                dk_ref[rows, :] += lax.dot_general(dzb, qh[hh], TN, preferred_element_type=F32)
                dv_ref[rows, :] += lax.dot_general(a.astype(BF16), doh[hh], TN, preferred_element_type=F32)
                out.append((pl_sum + jnp.sum(lm, axis=-1, keepdims=True),
                            pg_sum + jnp.sum(gm, axis=-1, keepdims=True),
                            dq_acc + jnp.dot(dzb, kblk, preferred_element_type=F32)))
            return tuple(out)

        zc = jnp.zeros((tq, 1), F32)
        one = (zc, zc, jnp.zeros((tq, LANES), F32))
        carry = lax.fori_loop(0, i, lambda kb, c: block(kb, c, False), (one, one))
        (_, _, dq0), (_, _, dq1) = block(i, carry, True)
        dq_ref[...] = (jnp.where(lane < HEAD_DIM, dq0, dq1) * SCALE).astype(BF16)

    return _pc(body, name=name,
               out_shape=[_sds((B * S, P * LANES), BF16), _sds((B * S, P * LANES), F32), _sds((B * S, P * LANES), F32)],
               grid=(B, P, nq),
               in_specs=[_col_spec(tq, nq, q_cb), _kv_spec(S, k_cb), _kv_spec(S, v_cb), _col_spec(tq, nq, do_cb),
                         _stat_col_spec(tq)],
               out_specs=[_col_spec(tq, nq, 0), _kv_spec(S, 0), _kv_spec(S, 0)],
               semantics=("arbitrary", "arbitrary", "arbitrary"), vmem=VMEM_BIG)(qa, ka, va, doa, rt)


HEAD_GROUP = 2


def _g_col_spec(rows, nblk_rows, cb, G):
    return pl.BlockSpec((rows, G * LANES), lambda b, p, i: (b * nblk_rows + i, cb // G + p))


def _g_kv_spec(rows, cb, G):
    return pl.BlockSpec((rows, G * LANES), lambda b, p, i: (b, cb // G + p))


def _g_stat_col_spec(tq, G):
    return pl.BlockSpec((1, 2 * G, tq, 1), lambda b, p, i: (b, p, i, 0))


def _g_stat_row_spec(S, G):
    return pl.BlockSpec((1, 2 * G, 1, S), lambda b, p, i: (b, p, 0, 0))


def _lanes(g):
    return slice(g * LANES, (g + 1) * LANES)


def _streams(x_ref, G, scale=None):
    rows = x_ref.shape[0]
    lane = lax.broadcasted_iota(jnp.int32, (rows, LANES), 1)
    out = []
    for g in range(G):
        x = x_ref[:, _lanes(g)]
        if scale is not None:
            x = x * jnp.asarray(scale, x.dtype)
        for hh in range(2):
            out.append(jnp.where(_head_mask(lane, hh), x, jnp.zeros_like(x)))
    return lane, out


def _kv_blocks(ref, ks, tk, G):
    return [ref[pl.ds(ks, tk), _lanes(g)] for g in range(G)]


def _sweep(i, block):
    def step(kb, c):
        block(kb, False)
        return c
    lax.fori_loop(0, i, step, 0)
    block(i, True)


def _fox_fwd_g(qa, ka, va, cr, *, name, B, S, P, q_cb, k_cb, v_cb, G=HEAD_GROUP):
    tq = tk = min(ATT_TILE, S)
    nq = S // tq
    NS = 2 * G

    def body(q_ref, k_ref, v_ref, cr_ref, o_ref, lse_ref, acc_ref, m_ref, l_ref):
        i = pl.program_id(2)
        lane, qh = _streams(q_ref, G, SCALE)
        on_or_below = (lax.broadcasted_iota(jnp.int32, (tq, tk), 1) <= lax.broadcasted_iota(jnp.int32, (tq, tk), 0))
        m_ref[...] = jnp.full(m_ref.shape, NEG, F32)
        l_ref[...] = jnp.zeros(l_ref.shape, F32)
        acc_ref[...] = jnp.zeros(acc_ref.shape, F32)

        def block(kb, diag):
            ks = pl.multiple_of(kb * tk, tk)
            kblk = _kv_blocks(k_ref, ks, tk, G)
            vblk = _kv_blocks(v_ref, ks, tk, G)
            ss = [lax.dot_general(qh[st], kblk[st // 2], NT, preferred_element_type=F32) for st in range(NS)]
            ps = []
            for st in range(NS):
                s = ss[st] - cr_ref[0, st, :, pl.ds(ks, tk)]
                if diag:
                    s = jnp.where(on_or_below, s, NEG)
                m = m_ref[st]
                m_new = jnp.maximum(m, jnp.max(s, axis=-1, keepdims=True))
                alpha = jnp.exp(m - m_new)
                p = jnp.exp(s - m_new)
                m_ref[st] = m_new
                l_ref[st] = alpha * l_ref[st] + jnp.sum(p, axis=-1, keepdims=True)
                ps.append((alpha, p.astype(BF16)))
            pvs = [jnp.dot(ps[st][1], vblk[st // 2], preferred_element_type=F32) for st in range(NS)]
            for st in range(NS):
                acc_ref[st] = ps[st][0] * acc_ref[st] + pvs[st]

        _sweep(i, block)
        for st in range(NS):
            lse_ref[0, st] = m_ref[st] + jnp.log(l_ref[st])
        for g in range(G):
            o_ref[:, _lanes(g)] = jnp.where(lane < HEAD_DIM, acc_ref[2 * g] / l_ref[2 * g],
                                            acc_ref[2 * g + 1] / l_ref[2 * g + 1]).astype(BF16)

    return _pc(body, name=name, out_shape=[_sds((B * S, P * LANES), BF16), _sds((B, 2 * P, S, 1), F32)],
               grid=(B, P // G, nq),
               in_specs=[_g_col_spec(tq, nq, q_cb, G), _g_kv_spec(S, k_cb, G), _g_kv_spec(S, v_cb, G),
                         _g_stat_row_spec(S, G)],
               out_specs=[_g_col_spec(tq, nq, 0, G), _g_stat_col_spec(tq, G)],
               scratch_shapes=[pltpu.VMEM((NS, tq, LANES), F32), pltpu.VMEM((NS, tq, 1), F32),
                               pltpu.VMEM((NS, tq, 1), F32)],
               semantics=("arbitrary", "arbitrary", "arbitrary"), vmem=VMEM_BIG)(qa, ka, va, cr)


def _fox_bwd_g(qa, ka, va, doa, lse, cr, *, name, B, S, P, q_cb, k_cb, v_cb, do_cb, G=HEAD_GROUP):
    tq = tk = min(ATT_TILE, S)
    nq = S // tq
    NS = 2 * G

    def body(q_ref, k_ref, v_ref, do_ref, lse_ref, cr_ref, dq_ref, dk_ref, dv_ref, dcs_ref, dqa_ref, delta_ref):
        i = pl.program_id(2)

        @pl.when(i == 0)
        def _():
            dk_ref[...] = jnp.zeros_like(dk_ref)
            dv_ref[...] = jnp.zeros_like(dv_ref)
            dcs_ref[...] = jnp.zeros_like(dcs_ref)

        lane, qh = _streams(q_ref, G, SCALE)
        _, doh = _streams(do_ref, G)
        on_or_below = (lax.broadcasted_iota(jnp.int32, (tq, tk), 1) <= lax.broadcasted_iota(jnp.int32, (tq, tk), 0))
        delta_ref[...] = jnp.zeros(delta_ref.shape, F32)
        dqa_ref[...] = jnp.zeros(dqa_ref.shape, F32)

        def probs(kb, diag):
            ks = pl.multiple_of(kb * tk, tk)
            kblk = _kv_blocks(k_ref, ks, tk, G)
            vblk = _kv_blocks(v_ref, ks, tk, G)
            ss = [lax.dot_general(qh[st], kblk[st // 2], NT, preferred_element_type=F32) for st in range(NS)]
            dps = [lax.dot_general(doh[st], vblk[st // 2], NT, preferred_element_type=F32) for st in range(NS)]
            ps = []
            for st in range(NS):
                s = ss[st] - cr_ref[0, st, :, pl.ds(ks, tk)]
                if diag:
                    s = jnp.where(on_or_below, s, NEG)
                ps.append(jnp.exp(s - lse_ref[0, st]))
            return ks, kblk, ps, dps

        def delta_block(kb, diag):
            _, _, ps, dps = probs(kb, diag)
            for st in range(NS):
                delta_ref[st] += jnp.sum(ps[st] * dps[st], axis=-1, keepdims=True)

        _sweep(i, delta_block)

        def grad_block(kb, diag):
            ks, kblk, ps, dps = probs(kb, diag)
            rows = pl.ds(ks, tk)
            dsb, pb = [], []
            for st in range(NS):
                ds = ps[st] * (dps[st] - delta_ref[st])
                dcs_ref[0, st, :, rows] -= jnp.sum(ds, axis=0, keepdims=True)
                dsb.append(ds.astype(BF16))
                pb.append(ps[st].astype(BF16))
            dks = [lax.dot_general(dsb[st], qh[st], TN, preferred_element_type=F32) for st in range(NS)]
            dvs = [lax.dot_general(pb[st], doh[st], TN, preferred_element_type=F32) for st in range(NS)]
            dqs = [jnp.dot(dsb[st], kblk[st // 2], preferred_element_type=F32) for st in range(NS)]
            for g in range(G):
                dk_ref[rows, _lanes(g)] += dks[2 * g] + dks[2 * g + 1]
                dv_ref[rows, _lanes(g)] += dvs[2 * g] + dvs[2 * g + 1]
            for st in range(NS):
                dqa_ref[st] += dqs[st]

        _sweep(i, grad_block)
        for g in range(G):
            dq_ref[:, _lanes(g)] = (jnp.where(lane < HEAD_DIM, dqa_ref[2 * g], dqa_ref[2 * g + 1]) * SCALE).astype(BF16)

    return _pc(body, name=name,
               out_shape=[_sds((B * S, P * LANES), BF16), _sds((B * S, P * LANES), F32), _sds((B * S, P * LANES), F32),
                          _sds((B, 2 * P, 1, S), F32)],
               grid=(B, P // G, nq),
               in_specs=[_g_col_spec(tq, nq, q_cb, G), _g_kv_spec(S, k_cb, G), _g_kv_spec(S, v_cb, G),
                         _g_col_spec(tq, nq, do_cb, G), _g_stat_col_spec(tq, G), _g_stat_row_spec(S, G)],
               out_specs=[_g_col_spec(tq, nq, 0, G), _g_kv_spec(S, 0, G), _g_kv_spec(S, 0, G), _g_stat_row_spec(S, G)],
               scratch_shapes=[pltpu.VMEM((NS, tq, LANES), F32), pltpu.VMEM((NS, tq, 1), F32)],
               semantics=("arbitrary", "arbitrary", "arbitrary"), vmem=VMEM_BIG)(qa, ka, va, doa, lse, cr)


def _sb_logs_z(z):
    nz = -z
    lm = jnp.minimum(nz, 0.0) - jnp.log(1.0 + jnp.exp(jnp.minimum(z, nz)))
    return lm + z, lm


def _sb_fwd_g(qa, ka, va, *, name, B, S, P, q_cb, k_cb, v_cb, G=HEAD_GROUP):
    tq = tk = min(ATT_TILE, S)
    nq = S // tq
    NS = 2 * G

    def body(q_ref, k_ref, v_ref, o_ref, rt_ref, acc_ref, run_ref):
        i = pl.program_id(2)
        lane, qh = _streams(q_ref, G, SCALE)
        t_r = lax.broadcasted_iota(jnp.int32, (tk, tk), 0)
        t_c = lax.broadcasted_iota(jnp.int32, (tk, tk), 1)
        after = (t_r > t_c).astype(BF16)
        below = t_c < t_r
        acc_ref[...] = jnp.zeros(acc_ref.shape, F32)
        run_ref[...] = jnp.zeros(run_ref.shape, F32)

        def block(kb, diag):
            ks = pl.multiple_of(kb * tk, tk)
            kblk = _kv_blocks(k_ref, ks, tk, G)
            vblk = _kv_blocks(v_ref, ks, tk, G)
            zs = [lax.dot_general(qh[st], kblk[st // 2], NT, preferred_element_type=F32) for st in range(NS)]
            lss, parts = [], []
            for st in range(NS):
                ls, lm = _sb_logs_z(zs[st])
                if diag:
                    lm = jnp.where(below, lm, 0.0)
                lss.append(ls + run_ref[st])
                run_ref[st] += jnp.sum(lm, axis=-1, keepdims=True)
                parts.append(_split2(lm))
            sufs = [jnp.dot(parts[st][0], after, preferred_element_type=F32)
                    + jnp.dot(parts[st][1], after, preferred_element_type=F32) for st in range(NS)]
            ab = []
            for st in range(NS):
                a = jnp.exp(lss[st] + sufs[st])
                if diag:
                    a = jnp.where(below, a, 0.0)
                ab.append(a.astype(BF16))
            pvs = [jnp.dot(ab[st], vblk[st // 2], preferred_element_type=F32) for st in range(NS)]
            for st in range(NS):
                acc_ref[st] += pvs[st]

        block(i, True)

        def step(jj, c):
            block(i - 1 - jj, False)
            return c

        lax.fori_loop(0, i, step, 0)
        for st in range(NS):
            rt_ref[0, st] = run_ref[st]
        for g in range(G):
            o_ref[:, _lanes(g)] = jnp.where(lane < HEAD_DIM, acc_ref[2 * g], acc_ref[2 * g + 1]).astype(BF16)

    return _pc(body, name=name, out_shape=[_sds((B * S, P * LANES), BF16), _sds((B, 2 * P, S, 1), F32)],
               grid=(B, P // G, nq),
               in_specs=[_g_col_spec(tq, nq, q_cb, G), _g_kv_spec(S, k_cb, G), _g_kv_spec(S, v_cb, G)],
               out_specs=[_g_col_spec(tq, nq, 0, G), _g_stat_col_spec(tq, G)],
               scratch_shapes=[pltpu.VMEM((NS, tq, LANES), F32), pltpu.VMEM((NS, tq, 1), F32)],
               semantics=("arbitrary", "arbitrary", "arbitrary"), vmem=VMEM_BIG)(qa, ka, va)


def _sb_bwd_g(qa, ka, va, doa, rt, *, name, B, S, P, q_cb, k_cb, v_cb, do_cb, G=HEAD_GROUP):
    tq = tk = min(ATT_TILE, S)
    nq = S // tq
    NS = 2 * G

    def body(q_ref, k_ref, v_ref, do_ref, rt_ref, dq_ref, dk_ref, dv_ref, dqa_ref, pl_ref, pg_ref):
        i = pl.program_id(2)

        @pl.when(i == 0)
        def _():
            dk_ref[...] = jnp.zeros_like(dk_ref)
            dv_ref[...] = jnp.zeros_like(dv_ref)

        lane, qh = _streams(q_ref, G, SCALE)
        _, doh = _streams(do_ref, G)
        t_r = lax.broadcasted_iota(jnp.int32, (tk, tk), 0)
        t_c = lax.broadcasted_iota(jnp.int32, (tk, tk), 1)
        upto = (t_r <= t_c).astype(BF16)
        before = (t_r < t_c).astype(BF16)
        below = t_c < t_r
        dqa_ref[...] = jnp.zeros(dqa_ref.shape, F32)
        pg_ref[...] = jnp.zeros(pg_ref.shape, F32)
        for st in range(NS):
            pl_ref[st] = rt_ref[0, st]

        def block(kb, diag):
            ks = pl.multiple_of(kb * tk, tk)
            rows = pl.ds(ks, tk)
            kblk = _kv_blocks(k_ref, ks, tk, G)
            vblk = _kv_blocks(v_ref, ks, tk, G)
            zs = [lax.dot_general(qh[st], kblk[st // 2], NT, preferred_element_type=F32) for st in range(NS)]
            das = [lax.dot_general(doh[st], vblk[st // 2], NT, preferred_element_type=F32) for st in range(NS)]
            lss, parts = [], []
            for st in range(NS):
                ls, lm = _sb_logs_z(zs[st])
                if diag:
                    lm = jnp.where(below, lm, 0.0)
                lss.append((ls, ls + pl_ref[st]))
                pl_ref[st] -= jnp.sum(lm, axis=-1, keepdims=True)
                parts.append(_split2(lm))
            pins = [jnp.dot(parts[st][0], upto, preferred_element_type=F32)
                    + jnp.dot(parts[st][1], upto, preferred_element_type=F32) for st in range(NS)]
            gms, ab, gparts = [], [], []
            for st in range(NS):
                a = jnp.exp(lss[st][1] - pins[st])
                if diag:
                    a = jnp.where(below, a, 0.0)
                gm = a * das[st]
                gms.append(gm)
                ab.append(a.astype(BF16))
                gparts.append(_split2(gm))
            pgs = [jnp.dot(gparts[st][0], before, preferred_element_type=F32)
                   + jnp.dot(gparts[st][1], before, preferred_element_type=F32) for st in range(NS)]
            dzb = []
            for st in range(NS):
                gm = gms[st]
                dz = gm - jnp.exp(lss[st][0]) * (gm + (pgs[st] + pg_ref[st]))
                if diag:
                    dz = jnp.where(below, dz, 0.0)
                pg_ref[st] += jnp.sum(gm, axis=-1, keepdims=True)
                dzb.append(dz.astype(BF16))
            dks = [lax.dot_general(dzb[st], qh[st], TN, preferred_element_type=F32) for st in range(NS)]
            dvs = [lax.dot_general(ab[st], doh[st], TN, preferred_element_type=F32) for st in range(NS)]
            dqs = [jnp.dot(dzb[st], kblk[st // 2], preferred_element_type=F32) for st in range(NS)]
            for g in range(G):
                dk_ref[rows, _lanes(g)] += dks[2 * g] + dks[2 * g + 1]
                dv_ref[rows, _lanes(g)] += dvs[2 * g] + dvs[2 * g + 1]
            for st in range(NS):
                dqa_ref[st] += dqs[st]

        _sweep(i, block)
        for g in range(G):
            dq_ref[:, _lanes(g)] = (jnp.where(lane < HEAD_DIM, dqa_ref[2 * g], dqa_ref[2 * g + 1]) * SCALE).astype(BF16)

    return _pc(body, name=name,
               out_shape=[_sds((B * S, P * LANES), BF16), _sds((B * S, P * LANES), F32), _sds((B * S, P * LANES), F32)],
               grid=(B, P // G, nq),
               in_specs=[_g_col_spec(tq, nq, q_cb, G), _g_kv_spec(S, k_cb, G), _g_kv_spec(S, v_cb, G),
                         _g_col_spec(tq, nq, do_cb, G), _g_stat_col_spec(tq, G)],
               out_specs=[_g_col_spec(tq, nq, 0, G), _g_kv_spec(S, 0, G), _g_kv_spec(S, 0, G)],
               scratch_shapes=[pltpu.VMEM((NS, tq, LANES), F32), pltpu.VMEM((NS, tq, 1), F32),
                               pltpu.VMEM((NS, tq, 1), F32)],
               semantics=("arbitrary", "arbitrary", "arbitrary"), vmem=VMEM_BIG)(qa, ka, va, doa, rt)


def _shift_rows(cur, halo_ref, first, rows_idx, k):
    out = pltpu.roll(cur, k, 0)
    for r in range(k):
        edge = jnp.where(first, 0.0, halo_ref[8 - k + r:8 - k + r + 1, :])
        out = jnp.where(rows_idx == r, edge, out)
    return out


def _shift_rows_up(cur, halo_ref, last, rows_idx, k, ts):
    out = pltpu.roll(cur, ts - k, 0)
    for r in range(k):
        edge = jnp.where(last, 0.0, halo_ref[r:r + 1, :])
        out = jnp.where(rows_idx == ts - k + r, edge, out)
    return out


def _conv_taps(main_ref, halo_ref, w_ref, b_ref, first, rows_idx):
    cur = main_ref[...]
    m1 = _shift_rows(cur, halo_ref, first, rows_idx, 1)
    m2 = _shift_rows(cur, halo_ref, first, rows_idx, 2)
    uc = b_ref[...] + w_ref[0:1, :] * m2 + w_ref[1:2, :] * m1 + w_ref[2:3, :] * cur
    return uc, cur, m1, m2


def _conv_specs(ts, tf, ns, nf, S, order):
    def wrap(fn):
        return lambda *g: fn(*order(*g))
    specs = []
    for off in (0, nf):
        specs.append(pl.BlockSpec((ts, tf), wrap(lambda b, i, j, off=off: (b * ns + i, j + off))))
        specs.append(pl.BlockSpec((8, tf), wrap(
            lambda b, i, j, off=off: (jnp.maximum((b * S + i * ts) // 8 - 1, 0), j + off))))
    for off in (0, nf):
        specs.append(pl.BlockSpec((3, tf), wrap(lambda b, i, j, off=off: (0, j + off))))
    for off in (0, nf):
        specs.append(pl.BlockSpec((1, tf), wrap(lambda b, i, j, off=off: (0, j + off))))
    return specs


def _conv_gate_fwd(u, cw, cb, *, name, B, S):
    F = u.shape[1] // 2
    tf = F // 2
    ts = min(256, S)
    ns, nf = S // ts, F // tf

    def body(ug_ref, ugh_ref, uv_ref, uvh_ref, wg_ref, wv_ref, bg_ref, bv_ref, a_ref):
        first = pl.program_id(1) == 0
        rows_idx = lax.broadcasted_iota(jnp.int32, (ts, tf), 0)
        ucg = _conv_taps(ug_ref, ugh_ref, wg_ref, bg_ref, first, rows_idx)[0]
        ucv = _conv_taps(uv_ref, uvh_ref, wv_ref, bv_ref, first, rows_idx)[0]
        a_ref[...] = (ucg * (1.0 / (1.0 + jnp.exp(-ucg))) * ucv).astype(BF16)

    specs = _conv_specs(ts, tf, ns, nf, S, lambda b, i, j: (b, i, j))
    return _pc(body, name=name, out_shape=_sds((B * S, F), BF16), grid=(B, ns, nf), in_specs=specs,
               out_specs=pl.BlockSpec((ts, tf), lambda b, i, j: (b * ns + i, j)),
               semantics=("arbitrary", "arbitrary", "arbitrary"), vmem=VMEM_BIG)(u, u, u, u, cw, cw, cb, cb)


def _conv_gate_bwd(da, u, cw, cb, *, name, B, S):
    F = u.shape[1] // 2
    tf = F // 2
    ts = min(256, S)
    ns, nf = S // ts, F // tf

    def body(da_ref, ug_ref, ugh_ref, uv_ref, uvh_ref, wg_ref, wv_ref, bg_ref, bv_ref,
             dg_ref, dv_ref, pg_ref, pv_ref):
        first = pl.program_id(2) == 0

        @pl.when(jnp.logical_and(pl.program_id(1) == 0, first))
        def _():
            pg_ref[...] = jnp.zeros_like(pg_ref)
            pv_ref[...] = jnp.zeros_like(pv_ref)

        rows_idx = lax.broadcasted_iota(jnp.int32, (ts, tf), 0)
        ucg, g0, g1, g2 = _conv_taps(ug_ref, ugh_ref, wg_ref, bg_ref, first, rows_idx)
        ucv, v0, v1, v2 = _conv_taps(uv_ref, uvh_ref, wv_ref, bv_ref, first, rows_idx)
        sg = 1.0 / (1.0 + jnp.exp(-ucg))
        dav = da_ref[...]
        d_v = dav * (ucg * sg)
        d_g = dav * ucv * (sg * (1.0 + ucg * (1.0 - sg)))
        dg_ref[...] = d_g
        dv_ref[...] = d_v
        for p_ref, d, taps in ((pg_ref, d_g, (g2, g1, g0)), (pv_ref, d_v, (v2, v1, v0))):
            for k in range(3):
                p_ref[k:k + 1, :] += jnp.sum(d * taps[k], axis=0, keepdims=True)
            p_ref[3:4, :] += jnp.sum(d, axis=0, keepdims=True)

    specs = [pl.BlockSpec((ts, tf), lambda j, b, i: (b * ns + i, j))]
    specs += _conv_specs(ts, tf, ns, nf, S, lambda j, b, i: (b, i, j))
    row = pl.BlockSpec((ts, tf), lambda j, b, i: (b * ns + i, j))
    par = pl.BlockSpec((8, tf), lambda j, b, i: (0, j))
    return _pc(body, name=name,
               out_shape=[_sds((B * S, F), F32), _sds((B * S, F), F32), _sds((8, F), F32), _sds((8, F), F32)],
               grid=(nf, B, ns), in_specs=specs, out_specs=[row, row, par, par],
               semantics=("arbitrary", "arbitrary", "arbitrary"), vmem=VMEM_BIG)(da, u, u, u, u, cw, cw, cb, cb)


def _conv_transpose(d, cw, *, name, B, S, col_off):
    F = d.shape[1]
    tf = F // 2
    ts = min(256, S)
    ns, nf = S // ts, F // tf
    nblk8 = B * S // 8

    def body(d_ref, dh_ref, w_ref, o_ref):
        last = pl.program_id(1) == ns - 1
        rows_idx = lax.broadcasted_iota(jnp.int32, (ts, tf), 0)
        cur = d_ref[...]
        p1 = _shift_rows_up(cur, dh_ref, last, rows_idx, 1, ts)
        p2 = _shift_rows_up(cur, dh_ref, last, rows_idx, 2, ts)
        o_ref[...] = (w_ref[2:3, :] * cur + w_ref[1:2, :] * p1 + w_ref[0:1, :] * p2).astype(BF16)

    return _pc(body, name=name, out_shape=_sds((B * S, F), BF16), grid=(B, ns, nf),
               in_specs=[pl.BlockSpec((ts, tf), lambda b, i, j: (b * ns + i, j)),
                         pl.BlockSpec((8, tf), lambda b, i, j: (jnp.minimum((b * S + (i + 1) * ts) // 8, nblk8 - 1), j)),
                         pl.BlockSpec((3, tf), lambda b, i, j: (0, j + col_off * nf))],
               out_specs=pl.BlockSpec((ts, tf), lambda b, i, j: (b * ns + i, j)),
               semantics=("arbitrary", "arbitrary", "arbitrary"))(d, d, cw)


def _adamw(w, g, m, v, *, name):
    rows, cols = w.shape
    tr = rows
    while tr * cols * 4 > 2 ** 20 and tr % 16 == 0:
        tr //= 2

    def body(w_ref, g_ref, m_ref, v_ref, d_ref, nm_ref, nv_ref):
        gv = g_ref[...]
        m_new = ADAM_B1 * m_ref[...] + (1.0 - ADAM_B1) * gv
        v_new = ADAM_B2 * v_ref[...] + (1.0 - ADAM_B2) * (gv * gv)
        m_hat = m_new / (1.0 - ADAM_B1 ** ADAM_STEP)
        v_hat = v_new / (1.0 - ADAM_B2 ** ADAM_STEP)
        d_ref[...] = -ADAM_LR * (m_hat / (jnp.sqrt(v_hat) + ADAM_EPS) + ADAM_WD * w_ref[...])
        nm_ref[...] = m_new
        nv_ref[...] = v_new

    blk = pl.BlockSpec((tr, cols), lambda i: (i, 0))
    return _pc(body, name=name, out_shape=[_sds((rows, cols), F32)] * 3, grid=(rows // tr,),
               in_specs=[blk] * 4, out_specs=[blk] * 3, semantics=("arbitrary",))(w, g, m, v)


def _my_pos():
    return lax.axis_index("x"), lax.axis_index("y"), lax.axis_index("c")


def _all_gather(shard, *, name):
    rows = shard.shape[0]

    def body(x_ref, out_ref, send_sems, recv_sems, local_sem):
        x, y, c = _my_pos()
        me, sibling = (x, y, c), (x, y, 1 - c)
        chips = [(1 - x, y), (x, 1 - y), (1 - x, 1 - y)]

        def slot(px, py, pc):
            return out_ref.at[4 * px + 2 * py + pc]

        def copy(k, block, to, src=None):
            return pltpu.make_async_remote_copy(
                src_ref=slot(*block) if src is None else src, dst_ref=slot(*block),
                send_sem=send_sems.at[k], recv_sem=recv_sems.at[k], device_id=to, device_id_type=MESH)

        mine = pltpu.make_async_copy(x_ref, slot(*me), local_sem)
        mine.start()
        first = [copy(0, me, sibling, src=x_ref)]
        first += [copy(1 + j, me, (*chip, c), src=x_ref) for j, chip in enumerate(chips)]
        for cp in first:
            cp.start()
        passed = [copy(4 + j, (*chip, c), sibling) for j, chip in enumerate(chips)]
        for j, chip in enumerate(chips):
            copy(1 + j, (*chip, c), me).wait_recv()
            passed[j].start()
        copy(0, sibling, me).wait_recv()
        for j, chip in enumerate(chips):
            copy(4 + j, (*chip, 1 - c), me).wait_recv()
        for cp in first + passed:
            cp.wait_send()
        mine.wait()

    return _pc(body, name=name, out_shape=_sds((N_DEV, rows, LANES), shard.dtype),
               in_specs=[pl.BlockSpec(memory_space=pl.ANY)], out_specs=pl.BlockSpec(memory_space=pl.ANY),
               scratch_shapes=[pltpu.SemaphoreType.DMA((7,)), pltpu.SemaphoreType.DMA((7,)),
                               pltpu.SemaphoreType.DMA])(shard)


def _exchange(big, small, *, name):
    rs = small.shape[0]

    def body(big_ref, small_ref, bout_ref, sout_ref, send_sems, recv_sems, local_sems):
        x, y, c = _my_pos()
        me = 4 * x + 2 * y + c
        lb = pltpu.make_async_copy(big_ref.at[me], bout_ref.at[me], local_sems.at[0])
        ls = pltpu.make_async_copy(small_ref, sout_ref.at[me], local_sems.at[1])
        lb.start()
        ls.start()
        copies = []
        for k in range(1, N_DEV):
            px = x ^ ((k >> 2) & 1)
            py = y ^ ((k >> 1) & 1)
            pc = c ^ (k & 1)
            peer = 4 * px + 2 * py + pc
            copies.append(pltpu.make_async_remote_copy(
                src_ref=big_ref.at[peer], dst_ref=bout_ref.at[me], send_sem=send_sems.at[k - 1],
                recv_sem=recv_sems.at[k - 1], device_id=(px, py, pc), device_id_type=MESH))
            copies.append(pltpu.make_async_remote_copy(
                src_ref=small_ref, dst_ref=sout_ref.at[me], send_sem=send_sems.at[7 + k - 1],
                recv_sem=recv_sems.at[7 + k - 1], device_id=(px, py, pc), device_id_type=MESH))
        for cp in copies:
            cp.start()
        for cp in copies:
            cp.wait()
        lb.wait()
        ls.wait()

    return _pc(body, name=name,
               out_shape=[_sds(big.shape, big.dtype), _sds((N_DEV, rs, LANES), F32)],
               in_specs=[pl.BlockSpec(memory_space=pl.ANY), pl.BlockSpec(memory_space=pl.ANY)],
               out_specs=[pl.BlockSpec(memory_space=pl.ANY), pl.BlockSpec(memory_space=pl.ANY)],
               scratch_shapes=[pltpu.SemaphoreType.DMA((14,)), pltpu.SemaphoreType.DMA((14,)),
                               pltpu.SemaphoreType.DMA((2,))])(big, small)


def _sum_slots(a, *, name, tr):
    rows = a.shape[1]

    def body(a_ref, o_ref):
        acc = a_ref[0].astype(F32)
        for j in range(1, N_DEV):
            acc = acc + a_ref[j].astype(F32)
        o_ref[...] = acc

    return _pc(body, name=name, out_shape=_sds((rows, LANES), F32), grid=(rows // tr,),
               in_specs=[pl.BlockSpec((N_DEV, tr, LANES), lambda i: (0, i, 0))],
               out_specs=pl.BlockSpec((tr, LANES), lambda i: (i, 0)), semantics=("arbitrary",))(a)


PACK_ROWS = 25600
SUM_TILE = 512


def _rows128(a):
    return a.reshape(-1, LANES)


def _to_slots(full, kind):
    if kind == "rows2":
        r, c = full.shape
        return full.reshape(N_DEV, r // N_DEV, c)
    if kind == "cols2":
        r, c = full.shape
        return full.reshape(r, N_DEV, c // N_DEV).transpose(1, 0, 2)
    if kind == "rows3":
        l, r, c = full.shape
        return full.reshape(l, N_DEV, r // N_DEV, c).transpose(1, 0, 2, 3)
    if kind == "cols3":
        l, r, c = full.shape
        return full.reshape(l, r, N_DEV, c // N_DEV).transpose(2, 0, 1, 3)
    raise ValueError(kind)


def _from_slots(slots, kind):
    if kind == "rows2":
        _, r, c = slots.shape
        return slots.reshape(N_DEV * r, c)
    if kind == "cols2":
        _, r, c = slots.shape
        return slots.transpose(1, 0, 2).reshape(r, N_DEV * c)
    if kind == "rows3":
        _, l, r, c = slots.shape
        return slots.transpose(1, 0, 2, 3).reshape(l, N_DEV * r, c)
    if kind == "cols3":
        _, l, r, c = slots.shape
        return slots.transpose(1, 2, 0, 3).reshape(l, r, N_DEV * c)
    raise ValueError(kind)


BIG = (("w_in_a", "rows2"), ("w_in_b", "rows2"), ("w_kv", "cols2"), ("w_memkv", "rows3"),
       ("w_out", "rows3"), ("w_up", "cols3"), ("w_down", "rows3"))


def _pad_rows(a, rows, axis):
    pad = [(0, 0)] * a.ndim
    pad[axis] = (0, rows - a.shape[axis])
    return jnp.pad(a, pad)


def kernel(x, mem, ln_mix_g, w_in_a, b_f_a, w_in_b, ln_kv_g, w_kv, ln_mem_g, w_memkv, w_out, ln_ffn_g, w_up, conv_w, conv_b, w_down, final_g, loss_target, m_ln_mix_g, m_w_in_a, m_b_f_a, m_w_in_b, m_ln_kv_g, m_w_kv, m_ln_mem_g, m_w_memkv, m_w_out, m_ln_ffn_g, m_w_up, m_conv_w, m_conv_b, m_w_down, m_final_g, v_ln_mix_g, v_w_in_a, v_b_f_a, v_w_in_b, v_ln_kv_g, v_w_kv, v_ln_mem_g, v_w_memkv, v_w_out, v_ln_ffn_g, v_w_up, v_conv_w, v_conv_b, v_w_down, v_final_g):
    B, S, D = x.shape
    NM = mem.shape[1]
    T = B * S
    F = w_down.shape[1] * N_DEV
    my_idx = 4 * lax.axis_index("x") + 2 * lax.axis_index("y") + lax.axis_index("c")

    shards = {"w_in_a": w_in_a[0], "w_in_b": w_in_b[0], "w_kv": w_kv, "w_memkv": w_memkv, "w_out": w_out,
              "w_up": w_up, "w_down": w_down}
    moms = {"w_in_a": (m_w_in_a[0], v_w_in_a[0]), "w_in_b": (m_w_in_b[0], v_w_in_b[0]), "w_kv": (m_w_kv, v_w_kv),
            "w_memkv": (m_w_memkv, v_w_memkv), "w_out": (m_w_out, v_w_out), "w_up": (m_w_up, v_w_up),
            "w_down": (m_w_down, v_w_down)}

    parts = [_rows128(shards[n].astype(BF16)) for n, _ in BIG]
    parts.append(_rows128(lax.bitcast_convert_type(conv_w, BF16)))
    packed = jnp.concatenate(parts, axis=0)
    packed = _pad_rows(packed, PACK_ROWS, 0)
    gathered = _all_gather(packed, name="gather_weights")
    full = {}
    off = 0
    for n, kind in BIG:
        shp = shards[n].shape
        nrows = shards[n].size // LANES
        full[n] = _from_slots(gathered[:, off:off + nrows].reshape((N_DEV,) + shp), kind)
        off += nrows
    nrows = conv_w.size * 2 // LANES
    cw_bits = gathered[:, off:off + nrows].reshape((N_DEV,) + conv_w.shape + (2,))
    conv_w_full = _from_slots(lax.bitcast_convert_type(cw_bits, F32), "cols3")

    wa = full["w_in_a"]
    n_qkv = 3 * MAIN_W
    wa = jnp.concatenate([wa[:, :n_qkv], wa[:, n_qkv + N_MAIN_HEADS:], wa[:, n_qkv:n_qkv + N_MAIN_HEADS],
                          jnp.zeros((D, LANES - N_MAIN_HEADS), BF16)], axis=1)
    n_main = n_qkv + MEM_W
    wb = full["w_in_b"]
    wkv = full["w_kv"]
    b_f = _pad_rows(b_f_a.reshape(1, N_MAIN_HEADS), LANES, 1)

    x2d = x.reshape(T, D)
    mem2d = mem.reshape(B * NM, D)
    tgt2d = loss_target.reshape(T, D)
    PM, PX = N_MAIN_HEADS // 2, N_MEM_HEADS // 2

    def stats_to_heads(c2d):
        c = c2d.reshape(B, S, LANES)[:, :, :N_MAIN_HEADS].transpose(0, 2, 1)
        return c[:, :, None, :]

    def mem_kv(layer):
        return _mm_fwd(mem2d, full["w_memkv"][layer], name=f"memkv{layer}", tm=B * NM, tn=2 * MEM_W,
                       out_dtype=BF16, g=ln_mem_g[layer], save_h=True)

    def conv_ffn_fwd(xin, layer):
        u, h = _mm_fwd(xin, full["w_up"][layer], name=f"ffn_up{layer}", tm=min(1024, T), tn=512, out_dtype=F32,
                       g=ln_ffn_g[layer], save_h=True)
        a = _conv_gate_fwd(u, conv_w_full[layer], conv_b[layer].reshape(1, 2 * F), name=f"conv_gate{layer}", B=B, S=S)
        xo = _mm_fwd(a, full["w_down"][layer], name=f"ffn_down{layer}", tm=min(512, T), tn=512, out_dtype=F32, res=xin)
        return xo, (u, h, a)

    proj_a, h_mix0 = _mm_fwd(x2d, wa, name="in_proj_a", tm=min(1024, T), tn=512, out_dtype=BF16, g=ln_mix_g[0],
                             ncols=n_main, save_h=True)
    f_logit = _mm_fwd(x2d, wa, name="in_proj_f", tm=min(1024, T), tn=LANES, out_dtype=F32, g=ln_mix_g[0],
                      col0=n_main // LANES, ncols=LANES)
    c2d = _forget_cumsum(f_logit, b_f, B=B, S=S, name="forget_cumsum")
    cr = stats_to_heads(c2d)
    o_main0, lse0 = _fox_fwd_g(proj_a, proj_a, proj_a, cr, name="fox_fwd", B=B, S=S, P=PM, q_cb=0, k_cb=PM, v_cb=2 * PM,
                               G=1)
    memkv0, h_mem0 = mem_kv(0)
    o_mem0, lse_m0 = _softmax_fwd(proj_a, memkv0, memkv0, name="mem_fwd0", B=B, S=S, Sk=NM, P=PX, q_cb=3 * PM,
                                  k_cb=0, v_cb=PX, causal=False)
    o_cat0 = jnp.concatenate([o_main0, o_mem0], axis=1)
    x1 = _mm_fwd(o_cat0, full["w_out"][0], name="out_proj0", tm=min(512, T), tn=512, out_dtype=F32, res=x2d)
    x2, (u0, h_ffn0, a0) = conv_ffn_fwd(x1, 0)
    kv, h_kv = _mm_fwd(x2, wkv, name="kv_proj", tm=min(1024, T), tn=512, out_dtype=BF16, g=ln_kv_g, save_h=True)
    proj_b, h_mix1 = _mm_fwd(x2, wb, name="in_proj_b", tm=min(1024, T), tn=512, out_dtype=BF16, g=ln_mix_g[1],
                             save_h=True)
    o_main1, rt1 = _sb_fwd_g(proj_b, kv, kv, name="sb_fwd", B=B, S=S, P=PM, q_cb=0, k_cb=0, v_cb=PM)
    memkv1, h_mem1 = mem_kv(1)
    o_mem1, lse_m1 = _softmax_fwd(proj_b, memkv1, memkv1, name="mem_fwd1", B=B, S=S, Sk=NM, P=PX, q_cb=PM,
                                  k_cb=0, v_cb=PX, causal=False)
    o_cat1 = jnp.concatenate([o_main1, o_mem1], axis=1)
    x3 = _mm_fwd(o_cat1, full["w_out"][1], name="out_proj1", tm=min(512, T), tn=512, out_dtype=F32, res=x2)
    x4, (u1, h_ffn1, a1) = conv_ffn_fwd(x3, 1)
    dx4, dg_final, loss_part = _loss_head(x4, final_g, tgt2d, name="loss_head")

    grads = {}
    small = {}

    def conv_ffn_bwd(dxo, xin, u, h, a, layer):
        w_dn = full["w_down"][layer]
        da = _mm_nt(dxo, w_dn, name=f"d_act{layer}", tm=min(512, T), tn=F // 2, out_dtype=F32)
        grads[("w_down", layer)] = _wgrad(a, dxo, f"g_w_down{layer}")
        cwl = conv_w_full[layer]
        d_g, d_v, p_g, p_v = _conv_gate_bwd(da, u, cwl, conv_b[layer].reshape(1, 2 * F), name=f"conv_bwd{layer}", B=B, S=S)
        small[("conv_w", layer)] = jnp.concatenate([p_g[0:3], p_v[0:3]], axis=1)
        small[("conv_b", layer)] = jnp.concatenate([p_g[3], p_v[3]], axis=0)
        du_g = _conv_transpose(d_g, cwl, name=f"conv_t_gate{layer}", B=B, S=S, col_off=0)
        du_v = _conv_transpose(d_v, cwl, name=f"conv_t_val{layer}", B=B, S=S, col_off=1)
        grads[("w_up", layer)] = jnp.concatenate(
            [_wgrad(h, du_g, f"g_w_up_gate{layer}"), _wgrad(h, du_v, f"g_w_up_val{layer}")], axis=1)
        dxi, dg = _mm_nt_rmsbwd([(du_g, 0), (du_v, 1)], full["w_up"][layer], xin, ln_ffn_g[layer],
                                name=f"d_ffn_in{layer}", dres=dxo)
        small[("ln_ffn_g", layer)] = dg[0]
        return dxi

    def mem_bwd(proj, q_cb, memkv, h_mem, do_cat, o_mem, lse_m, layer):
        dqm, dmk, dmv = _softmax_bwd(proj, memkv, memkv, do_cat, o_mem, lse_m, name=f"mem_bwd{layer}", B=B, S=S,
                                     Sk=NM, P=PX, q_cb=q_cb, k_cb=0, v_cb=PX, do_cb=PM, causal=False)
        grads[("w_memkv", layer)] = jnp.concatenate(
            [_wgrad(h_mem, dmk, f"g_w_memk{layer}"), _wgrad(h_mem, dmv, f"g_w_memv{layer}")], axis=1)
        _, dg = _mm_nt_rmsbwd([(dmk, 0), (dmv, 1)], full["w_memkv"][layer], mem2d, ln_mem_g[layer],
                              name=f"d_mem_in{layer}", want_dx=False)
        small[("ln_mem_g", layer)] = dg[0]
        return dqm

    dx3 = conv_ffn_bwd(dx4, x3, u1, h_ffn1, a1, 1)
    do_cat1 = _mm_nt(dx3, full["w_out"][1], name="d_o_cat1", tm=min(512, T), tn=512, out_dtype=BF16)
    grads[("w_out", 1)] = _wgrad(o_cat1, dx3, "g_w_out1")
    dq1, dk1, dv1 = _sb_bwd_g(proj_b, kv, kv, do_cat1, rt1, name="sb_bwd", B=B, S=S, P=PM, q_cb=0, k_cb=0, v_cb=PM,
                            do_cb=0)
    dqm1 = mem_bwd(proj_b, PM, memkv1, h_mem1, do_cat1, o_mem1, lse_m1, 1)
    grads["w_in_b"] = jnp.concatenate([_wgrad(h_mix1, dq1, "g_w_in_b_q"), _wgrad(h_mix1, dqm1, "g_w_in_b_m")], axis=1)
    dx2, dg = _mm_nt_rmsbwd([(dq1, 0), (dqm1, MAIN_W // MEM_W)], wb, x2, ln_mix_g[1], name="d_mix_in1", dres=dx3)
    small[("ln_mix_g", 1)] = dg[0]
    grads["w_kv"] = jnp.concatenate([_wgrad(h_kv, dk1, "g_w_kv_k"), _wgrad(h_kv, dv1, "g_w_kv_v")], axis=1)
    dx2, dg = _mm_nt_rmsbwd([(dk1, 0), (dv1, 1)], wkv, x2, ln_kv_g, name="d_kv_in", dres=dx2)
    small["ln_kv_g"] = dg[0]
    dx1 = conv_ffn_bwd(dx2, x1, u0, h_ffn0, a0, 0)
    do_cat0 = _mm_nt(dx1, full["w_out"][0], name="d_o_cat0", tm=min(512, T), tn=512, out_dtype=BF16)
    grads[("w_out", 0)] = _wgrad(o_cat0, dx1, "g_w_out0")
    dq0, dk0, dv0, dcs = _fox_bwd_g(proj_a, proj_a, proj_a, do_cat0, lse0, cr, name="fox_bwd", B=B, S=S, P=PM, q_cb=0,
                                  k_cb=PM, v_cb=2 * PM, do_cb=0)
    dqm0 = mem_bwd(proj_a, 3 * PM, memkv0, h_mem0, do_cat0, o_mem0, lse_m0, 0)
    dc2d = _pad_rows(dcs[:, :, 0, :].transpose(0, 2, 1).reshape(T, N_MAIN_HEADS), LANES, 1)
    df, db_f = _forget_cumsum_bwd(dc2d, f_logit, b_f, B=B, S=S, name="forget_cumsum_bwd")
    a_parts = [(dq0, 0), (dk0, 1), (dv0, 2), (dqm0, n_qkv // MEM_W), (df, n_main // LANES)]
    g_wa = jnp.concatenate([_wgrad(h_mix0, p, f"g_w_in_a{k}") for k, (p, _) in enumerate(a_parts)], axis=1)
    grads["w_in_a"] = jnp.concatenate([g_wa[:, :n_qkv], g_wa[:, n_main:n_main + N_MAIN_HEADS], g_wa[:, n_qkv:n_main]],
                                      axis=1)
    dx0, dg = _mm_nt_rmsbwd(a_parts, wa, x2d, ln_mix_g[0], name="d_mix_in0", dres=dx1)
    small[("ln_mix_g", 0)] = dg[0]
    grad_x = dx0.reshape(B, S, D)

    def both(name):
        return jnp.stack([grads[(name, 0)], grads[(name, 1)]])

    gfull = {"w_in_a": grads["w_in_a"], "w_in_b": grads["w_in_b"], "w_kv": grads["w_kv"],
             "w_memkv": both("w_memkv"), "w_out": both("w_out"), "w_up": both("w_up"), "w_down": both("w_down")}
    gparts = [_to_slots(gfull[n], kind).astype(BF16).reshape(N_DEV, -1, LANES) for n, kind in BIG]
    gpack = _pad_rows(jnp.concatenate(gparts, axis=1), PACK_ROWS, 1)

    def both_small(name):
        return jnp.stack([small[(name, 0)], small[(name, 1)]])

    small_list = [("ln_mix_g", both_small("ln_mix_g")), ("b_f_a", db_f[:, :N_MAIN_HEADS]), ("ln_kv_g", small["ln_kv_g"]),
                  ("ln_mem_g", both_small("ln_mem_g")), ("ln_ffn_g", both_small("ln_ffn_g")),
                  ("conv_w", both_small("conv_w")), ("conv_b", both_small("conv_b")), ("final_g", dg_final[0]),
                  ("loss", loss_part[0, :1])]
    sm_rows = []
    for _, a in small_list:
        flat = a.reshape(-1)
        sm_rows.append(_pad_rows(flat, -(-flat.size // LANES) * LANES, 0).reshape(-1, LANES))
    spack = jnp.concatenate(sm_rows, axis=0)
    n_small = spack.shape[0]
    small_rows = -(-n_small // 8) * 8
    spack = _pad_rows(spack, small_rows, 0)

    bout, sout = _exchange(gpack, spack, name="exchange_grads")
    gsum = _sum_slots(bout, name="sum_grads", tr=SUM_TILE)
    ssum = _sum_slots(sout, name="sum_small", tr=small_rows)

    red = {}
    off = 0
    for n, _ in BIG:
        nrows = shards[n].size // LANES
        red[n] = gsum[off:off + nrows].reshape(shards[n].shape)
        off += nrows
    off = 0
    for (n, a), rows in zip(small_list, sm_rows):
        red[n] = ssum[off:off + rows.shape[0]].reshape(-1)[:a.size].reshape(a.shape)
        off += rows.shape[0]
    loss = red["loss"][0]
    shard_cols = conv_w.shape[2]
    red["conv_w"] = lax.dynamic_slice_in_dim(red["conv_w"], my_idx * shard_cols, shard_cols, axis=2)
    red["b_f_a"] = red["b_f_a"].reshape(b_f_a.shape)

    weights = {"ln_mix_g": ln_mix_g, "w_in_a": w_in_a, "b_f_a": b_f_a, "w_in_b": w_in_b, "ln_kv_g": ln_kv_g,
               "w_kv": w_kv, "ln_mem_g": ln_mem_g, "w_memkv": w_memkv, "w_out": w_out, "ln_ffn_g": ln_ffn_g,
               "w_up": w_up, "conv_w": conv_w, "conv_b": conv_b, "w_down": w_down, "final_g": final_g}
    m_in = {"ln_mix_g": m_ln_mix_g, "w_in_a": m_w_in_a, "b_f_a": m_b_f_a, "w_in_b": m_w_in_b, "ln_kv_g": m_ln_kv_g,
            "w_kv": m_w_kv, "ln_mem_g": m_ln_mem_g, "w_memkv": m_w_memkv, "w_out": m_w_out, "ln_ffn_g": m_ln_ffn_g,
            "w_up": m_w_up, "conv_w": m_conv_w, "conv_b": m_conv_b, "w_down": m_w_down, "final_g": m_final_g}
    v_in = {"ln_mix_g": v_ln_mix_g, "w_in_a": v_w_in_a, "b_f_a": v_b_f_a, "w_in_b": v_w_in_b, "ln_kv_g": v_ln_kv_g,
            "w_kv": v_w_kv, "ln_mem_g": v_ln_mem_g, "w_memkv": v_w_memkv, "w_out": v_w_out, "ln_ffn_g": v_ln_ffn_g,
            "w_up": v_w_up, "conv_w": v_conv_w, "conv_b": v_conv_b, "w_down": v_w_down, "final_g": v_final_g}
    order = list(weights)
    big_names = [n for n, _ in BIG]
    g_out, d_out, nm_out, nv_out = {}, {}, {}, {}
    for n in big_names + ["conv_w"]:
        w = weights[n]
        cols = w.shape[-1]
        g = red[n].reshape(w.shape)
        d, nm, nv = _adamw(w.reshape(-1, cols), g.reshape(-1, cols), m_in[n].reshape(-1, cols),
                           v_in[n].reshape(-1, cols), name=f"adamw_{n}")
        g_out[n], d_out[n], nm_out[n], nv_out[n] = g, d.reshape(w.shape), nm.reshape(w.shape), nv.reshape(w.shape)
    small_names = [n for n in order if n not in g_out]

    def pack_small(src):
        rows = []
        for n in small_names:
            flat = src[n].reshape(-1)
            rows.append(_pad_rows(flat, -(-flat.size // LANES) * LANES, 0).reshape(-1, LANES))
        p = jnp.concatenate(rows, axis=0)
        return _pad_rows(p, -(-p.shape[0] // 8) * 8, 0), [r.shape[0] for r in rows]

    red_small = {n: red[n].reshape(weights[n].shape) for n in small_names}
    wp, counts = pack_small(weights)
    gp, _ = pack_small(red_small)
    mp, _ = pack_small(m_in)
    vp, _ = pack_small(v_in)
    dp, nmp, nvp = _adamw(wp, gp, mp, vp, name="adamw_small")
    off = 0
    for n, cnt in zip(small_names, counts):
        shp = weights[n].shape
        size = weights[n].size
        g_out[n] = red_small[n]
        d_out[n] = dp[off:off + cnt].reshape(-1)[:size].reshape(shp)
        nm_out[n] = nmp[off:off + cnt].reshape(-1)[:size].reshape(shp)
        nv_out[n] = nvp[off:off + cnt].reshape(-1)[:size].reshape(shp)
        off += cnt

    return (loss, grad_x, *[g_out[n] for n in order], *[d_out[n] for n in order],
            *[nm_out[n] for n in order], *[nv_out[n] for n in order])
```

```python
import functools

import jax
import jax.numpy as jnp
from jax import lax
from jax.experimental import pallas as pl
from jax.experimental.pallas import tpu as pltpu

F32 = jnp.float32
BF16 = jnp.bfloat16
LANES = 128
HEAD_DIM = 64
N_MAIN_HEADS = 12
N_MEM_HEADS = 4
MAIN_W = N_MAIN_HEADS * HEAD_DIM
MEM_W = N_MEM_HEADS * HEAD_DIM
SCALE = HEAD_DIM ** -0.5
EPS = 1e-6
NEG = -1e30
N_DEV = 8
ATT_TILE = 256
VMEM_BIG = 56 * 2 ** 20
MESH = pl.DeviceIdType.MESH

ADAM_LR = 0.001
ADAM_B1 = 0.9
ADAM_B2 = 0.999
ADAM_EPS = 1e-08
ADAM_WD = 0.01
ADAM_STEP = 10

NT = (((1,), (1,)), ((), ()))
TN = (((0,), (0,)), ((), ()))


def _pc(body, *, name, out_shape, grid=None, in_specs=None, out_specs=None, scratch_shapes=(),
        semantics=None, vmem=None):
    kw = {}
    if grid is not None:
        kw["grid"] = grid
    params = pltpu.CompilerParams(dimension_semantics=semantics, vmem_limit_bytes=vmem)
    return pl.pallas_call(body, name=name, out_shape=out_shape, in_specs=in_specs, out_specs=out_specs,
                          scratch_shapes=list(scratch_shapes), compiler_params=params, **kw)


def _sds(shape, dtype):
    return jax.ShapeDtypeStruct(shape, dtype)


def _mm_fwd(a, w, *, name, tm, tn, out_dtype, g=None, res=None, col0=0, ncols=None, save_h=False):
    m_rows, k = a.shape
    n = w.shape[1] if ncols is None else ncols
    grid = (m_rows // tm, n // tn)
    norm = g is not None

    def body(*refs):
        refs = list(refs)
        a_ref = refs.pop(0)
        g_ref = refs.pop(0) if norm else None
        w_ref = refs.pop(0)
        res_ref = refs.pop(0) if res is not None else None
        o_ref = refs.pop(0)
        hout_ref = refs.pop(0) if save_h else None
        h_ref = refs.pop(0) if norm else None
        if norm:
            @pl.when(pl.program_id(1) == 0)
            def _():
                xv = a_ref[...]
                r = lax.rsqrt(jnp.mean(xv * xv, axis=-1, keepdims=True) + EPS)
                h = ((xv * r) * g_ref[...]).astype(BF16)
                h_ref[...] = h
                if save_h:
                    hout_ref[...] = h
            lhs = h_ref[...]
        else:
            lhs = a_ref[...].astype(BF16)
        acc = jnp.dot(lhs, w_ref[...], preferred_element_type=F32)
        if res is not None:
            acc = acc + res_ref[...]
        o_ref[...] = acc.astype(out_dtype)

    in_specs = [pl.BlockSpec((tm, k), lambda i, j: (i, 0))]
    args = [a]
    if norm:
        in_specs.append(pl.BlockSpec((1, k), lambda i, j: (0, 0)))
        args.append(g.reshape(1, k))
    in_specs.append(pl.BlockSpec((k, tn), lambda i, j: (0, j + col0)))
    args.append(w)
    if res is not None:
        in_specs.append(pl.BlockSpec((tm, tn), lambda i, j: (i, j)))
        args.append(res)
    out_shape = [_sds((m_rows, n), out_dtype)]
    out_specs = [pl.BlockSpec((tm, tn), lambda i, j: (i, j))]
    if save_h:
        out_shape.append(_sds((m_rows, k), BF16))
        out_specs.append(pl.BlockSpec((tm, k), lambda i, j: (i, 0)))
    scratch = [pltpu.VMEM((tm, k), BF16)] if norm else []
    outs = _pc(body, name=name, out_shape=out_shape, grid=grid, in_specs=in_specs, out_specs=out_specs,
               scratch_shapes=scratch, semantics=("arbitrary", "arbitrary"), vmem=VMEM_BIG)(*args)
    return outs if save_h else outs[0]


def _mm_nt(a, w, *, name, tm, tn, out_dtype):
    m_rows, k = a.shape
    n = w.shape[0]

    def body(a_ref, w_ref, o_ref):
        acc = lax.dot_general(a_ref[...].astype(BF16), w_ref[...], NT, preferred_element_type=F32)
        o_ref[...] = acc.astype(out_dtype)

    return _pc(body, name=name, out_shape=_sds((m_rows, n), out_dtype), grid=(m_rows // tm, n // tn),
               in_specs=[pl.BlockSpec((tm, k), lambda i, j: (i, 0)), pl.BlockSpec((tn, k), lambda i, j: (j, 0))],
               out_specs=pl.BlockSpec((tm, tn), lambda i, j: (i, j)),
               semantics=("arbitrary", "arbitrary"), vmem=VMEM_BIG)(a, w)


def _mm_tn(a, b, *, name, ta, tn, tt):
    t_rows, ka = a.shape
    n = b.shape[1]
    nt = t_rows // tt

    def body(a_ref, b_ref, o_ref, acc_ref):
        t = pl.program_id(2)

        @pl.when(t == 0)
        def _():
            acc_ref[...] = jnp.zeros_like(acc_ref)

        acc_ref[...] += lax.dot_general(a_ref[...].astype(BF16), b_ref[...].astype(BF16), TN,
                                        preferred_element_type=F32)

        @pl.when(t == nt - 1)
        def _():
            o_ref[...] = acc_ref[...].astype(BF16)

    return _pc(body, name=name, out_shape=_sds((ka, n), BF16), grid=(ka // ta, n // tn, nt),
               in_specs=[pl.BlockSpec((tt, ta), lambda i, j, t: (t, i)),
                         pl.BlockSpec((tt, tn), lambda i, j, t: (t, j))],
               out_specs=pl.BlockSpec((ta, tn), lambda i, j, t: (i, j)),
               scratch_shapes=[pltpu.VMEM((ta, tn), F32)],
               semantics=("arbitrary", "arbitrary", "arbitrary"), vmem=VMEM_BIG)(a, b)


def _wgrad(a, b, name):
    t_rows, ka = a.shape
    n = b.shape[1]
    ta = ka if ka <= 1024 else ka // 2
    tn = n
    while ta * tn * 4 > 6 * 2 ** 20 and tn % 256 == 0:
        tn //= 2
    tt = min(512, t_rows)
    return _mm_tn(a, b, name=name, ta=ta, tn=tn, tt=tt)


def _mm_nt_rmsbwd(parts, w, x, g, *, name, dres=None, want_dx=True):
    m_rows, d = x.shape
    tm = min(256, m_rows)
    n_parts = len(parts)

    def body(*refs):
        refs = list(refs)
        dy_refs = [refs.pop(0) for _ in range(n_parts)]
        w_refs = [refs.pop(0) for _ in range(n_parts)]
        x_ref = refs.pop(0)
        g_ref = refs.pop(0)
        dres_ref = refs.pop(0) if dres is not None else None
        dx_ref = refs.pop(0) if want_dx else None
        dg_ref = refs.pop(0)

        @pl.when(pl.program_id(0) == 0)
        def _():
            dg_ref[...] = jnp.zeros_like(dg_ref)

        dh = None
        for dy_ref, w_ref in zip(dy_refs, w_refs):
            t = lax.dot_general(dy_ref[...].astype(BF16), w_ref[...], NT, preferred_element_type=F32)
            dh = t if dh is None else dh + t
        xv = x_ref[...]
        r = lax.rsqrt(jnp.mean(xv * xv, axis=-1, keepdims=True) + EPS)
        xh = xv * r
        dg_ref[...] += jnp.sum(dh * xh, axis=0, keepdims=True)
        if want_dx:
            dhg = dh * g_ref[...]
            dx = r * (dhg - xh * jnp.mean(dhg * xh, axis=-1, keepdims=True))
            if dres is not None:
                dx = dx + dres_ref[...]
            dx_ref[...] = dx

    in_specs, args = [], []
    for dy, _ in parts:
        in_specs.append(pl.BlockSpec((tm, dy.shape[1]), lambda i: (i, 0)))
        args.append(dy)
    for dy, cb in parts:
        in_specs.append(pl.BlockSpec((d, dy.shape[1]), functools.partial(lambda i, cb: (0, cb), cb=cb)))
        args.append(w)
    in_specs += [pl.BlockSpec((tm, d), lambda i: (i, 0)), pl.BlockSpec((1, d), lambda i: (0, 0))]
    args += [x, g.reshape(1, d)]
    if dres is not None:
        in_specs.append(pl.BlockSpec((tm, d), lambda i: (i, 0)))
        args.append(dres)
    out_shape, out_specs = [], []
    if want_dx:
        out_shape.append(_sds((m_rows, d), F32))
        out_specs.append(pl.BlockSpec((tm, d), lambda i: (i, 0)))
    out_shape.append(_sds((1, d), F32))
    out_specs.append(pl.BlockSpec((1, d), lambda i: (0, 0)))
    outs = _pc(body, name=name, out_shape=out_shape, grid=(m_rows // tm,), in_specs=in_specs,
               out_specs=out_specs, semantics=("arbitrary",), vmem=VMEM_BIG)(*args)
    return (outs[0], outs[1]) if want_dx else (None, outs[0])


def _loss_head(x, g, tgt, *, name):
    m_rows, d = x.shape
    tm = min(256, m_rows)

    def body(x_ref, g_ref, t_ref, dx_ref, dg_ref, loss_ref):
        @pl.when(pl.program_id(0) == 0)
        def _():
            dg_ref[...] = jnp.zeros_like(dg_ref)
            loss_ref[...] = jnp.zeros_like(loss_ref)

        xv = x_ref[...]
        r = lax.rsqrt(jnp.mean(xv * xv, axis=-1, keepdims=True) + EPS)
        xh = xv * r
        gv = g_ref[...]
        err = xh * gv - t_ref[...]
        per_tok = jnp.mean(err * err, axis=-1, keepdims=True)
        loss_ref[...] += 0.5 * jnp.sum(per_tok, axis=0, keepdims=True)
        dout = err * (1.0 / d)
        dg_ref[...] += jnp.sum(dout * xh, axis=0, keepdims=True)
        dhg = dout * gv
        dx_ref[...] = r * (dhg - xh * jnp.mean(dhg * xh, axis=-1, keepdims=True))

    row = pl.BlockSpec((tm, d), lambda i: (i, 0))
    return _pc(body, name=name, out_shape=[_sds((m_rows, d), F32), _sds((1, d), F32), _sds((1, LANES), F32)],
               grid=(m_rows // tm,), in_specs=[row, pl.BlockSpec((1, d), lambda i: (0, 0)), row],
               out_specs=[row, pl.BlockSpec((1, d), lambda i: (0, 0)), pl.BlockSpec((1, LANES), lambda i: (0, 0))],
               semantics=("arbitrary",))(x, g.reshape(1, d), tgt)


def _split3(v):
    hi = v.astype(BF16)
    r1 = v - hi.astype(F32)
    mid = r1.astype(BF16)
    lo = (r1 - mid.astype(F32)).astype(BF16)
    return hi, mid, lo


def _split2(v):
    hi = v.astype(BF16)
    lo = (v - hi.astype(F32)).astype(BF16)
    return hi, lo


def _tri_dot3(tri, v):
    hi, mid, lo = _split3(v)
    return (jnp.dot(tri, hi, preferred_element_type=F32) + jnp.dot(tri, mid, preferred_element_type=F32)
            + jnp.dot(tri, lo, preferred_element_type=F32))


def _dot_tri2(v, tri):
    hi, lo = _split2(v)
    return jnp.dot(hi, tri, preferred_element_type=F32) + jnp.dot(lo, tri, preferred_element_type=F32)


def _log_sigmoid(v):
    return jnp.minimum(v, 0.0) - jnp.log(1.0 + jnp.exp(-jnp.abs(v)))


def _forget_cumsum(f_logit, b_f, *, B, S, name):
    ch = min(256, S)
    nch = S // ch

    def body(f_ref, b_ref, c_ref):
        r_i = lax.broadcasted_iota(jnp.int32, (ch, ch), 0)
        c_i = lax.broadcasted_iota(jnp.int32, (ch, ch), 1)
        tri = (c_i <= r_i).astype(BF16)
        bv = b_ref[...]

        def step(k, carry):
            rows = pl.ds(pl.multiple_of(k * ch, ch), ch)
            lf = _log_sigmoid(f_ref[rows, :] + bv)
            c_ref[rows, :] = _tri_dot3(tri, lf) + carry
            return carry + jnp.sum(lf, axis=0, keepdims=True)

        lax.fori_loop(0, nch, step, jnp.zeros((1, LANES), F32))

    blk = pl.BlockSpec((S, LANES), lambda b: (b, 0))
    return _pc(body, name=name, out_shape=_sds((B * S, LANES), F32), grid=(B,),
               in_specs=[blk, pl.BlockSpec((1, LANES), lambda b: (0, 0))], out_specs=blk,
               semantics=("arbitrary",))(f_logit, b_f)


def _forget_cumsum_bwd(dc, f_logit, b_f, *, B, S, name):
    ch = min(256, S)
    nch = S // ch

    def body(dc_ref, f_ref, b_ref, df_ref, db_ref):
        @pl.when(pl.program_id(0) == 0)
        def _():
            db_ref[...] = jnp.zeros_like(db_ref)

        r_i = lax.broadcasted_iota(jnp.int32, (ch, ch), 0)
        c_i = lax.broadcasted_iota(jnp.int32, (ch, ch), 1)
        tri = (c_i >= r_i).astype(BF16)
        bv = b_ref[...]

        def step(kk, carry):
            tail, dbs = carry
            k = nch - 1 - kk
            rows = pl.ds(pl.multiple_of(k * ch, ch), ch)
            dcv = dc_ref[rows, :]
            dlf = _tri_dot3(tri, dcv) + tail
            z = f_ref[rows, :] + bv
            df = dlf * (1.0 / (1.0 + jnp.exp(z)))
            df_ref[rows, :] = df.astype(BF16)
            return tail + jnp.sum(dcv, axis=0, keepdims=True), dbs + jnp.sum(df, axis=0, keepdims=True)

        zero = jnp.zeros((1, LANES), F32)
        _, dbs = lax.fori_loop(0, nch, step, (zero, zero))
        db_ref[...] += dbs

    blk = pl.BlockSpec((S, LANES), lambda b: (b, 0))
    one = pl.BlockSpec((1, LANES), lambda b: (0, 0))
    return _pc(body, name=name, out_shape=[_sds((B * S, LANES), BF16), _sds((1, LANES), F32)], grid=(B,),
               in_specs=[blk, blk, one], out_specs=[blk, one], semantics=("arbitrary",))(dc, f_logit, b_f)


def _head_mask(lane, hh):
    return (lane < HEAD_DIM) if hh == 0 else (lane >= HEAD_DIM)


def _col_spec(rows, nblk_rows, cb):
    return pl.BlockSpec((rows, LANES), lambda b, p, i: (b * nblk_rows + i, cb + p))


def _kv_spec(rows, cb):
    return pl.BlockSpec((rows, LANES), lambda b, p, i: (b, cb + p))


def _stat_col_spec(tq):
    return pl.BlockSpec((1, 2, tq, 1), lambda b, p, i: (b, p, i, 0))


def _stat_row_spec(S):
    return pl.BlockSpec((1, 2, 1, S), lambda b, p, i: (b, p, 0, 0))


def _softmax_fwd(qa, ka, va, *, name, B, S, Sk, P, q_cb, k_cb, v_cb, causal, cc=None, cr=None):
    tq = min(ATT_TILE, S)
    tk = min(ATT_TILE, Sk)
    nq, nk = S // tq, Sk // tk
    decay = cc is not None
    assert not causal or (tq == tk and S == Sk)

    def body(*refs):
        if decay:
            q_ref, k_ref, v_ref, cc_ref, cr_ref, o_ref, lse_ref = refs
        else:
            q_ref, k_ref, v_ref, o_ref, lse_ref = refs
        i = pl.program_id(2)
        q = q_ref[...]
        lane = lax.broadcasted_iota(jnp.int32, (tq, LANES), 1)
        row = lax.broadcasted_iota(jnp.int32, (tq, tk), 0) + i * tq
        col0 = lax.broadcasted_iota(jnp.int32, (tq, tk), 1)
        outs = []
        for hh in range(2):
            qh = jnp.where(_head_mask(lane, hh), q, jnp.zeros_like(q))

            def step(kb, carry, hh=hh, qh=qh):
                m, l, acc = carry
                ks = pl.multiple_of(kb * tk, tk)
                kblk = k_ref[pl.ds(ks, tk), :]
                vblk = v_ref[pl.ds(ks, tk), :]
                s = lax.dot_general(qh, kblk, NT, preferred_element_type=F32) * SCALE
                if decay:
                    s = s + (cc_ref[0, hh] - cr_ref[0, hh, :, pl.ds(ks, tk)])
                if causal:
                    s = jnp.where(col0 + kb * tk <= row, s, NEG)
                m_new = jnp.maximum(m, jnp.max(s, axis=-1, keepdims=True))
                alpha = jnp.exp(m - m_new)
                p = jnp.exp(s - m_new)
                l = alpha * l + jnp.sum(p, axis=-1, keepdims=True)
                acc = alpha * acc + jnp.dot(p.astype(BF16), vblk, preferred_element_type=F32)
                return m_new, l, acc

            init = (jnp.full((tq, 1), NEG, F32), jnp.zeros((tq, 1), F32), jnp.zeros((tq, LANES), F32))
            m, l, acc = lax.fori_loop(0, (i + 1) if causal else nk, step, init)
            outs.append(acc / l)
            lse_ref[0, hh] = m + jnp.log(l)
        o_ref[...] = jnp.where(lane < HEAD_DIM, outs[0], outs[1]).astype(BF16)

    in_specs = [_col_spec(tq, nq, q_cb), _kv_spec(Sk, k_cb), _kv_spec(Sk, v_cb)]
    args = [qa, ka, va]
    if decay:
        in_specs += [_stat_col_spec(tq), _stat_row_spec(S)]
        args += [cc, cr]
    return _pc(body, name=name,
               out_shape=[_sds((B * S, P * LANES), BF16), _sds((B, 2 * P, S, 1), F32)],
               grid=(B, P, nq), in_specs=in_specs, out_specs=[_col_spec(tq, nq, 0), _stat_col_spec(tq)],
               semantics=("arbitrary", "arbitrary", "arbitrary"), vmem=VMEM_BIG)(*args)


def _softmax_bwd(qa, ka, va, doa, oa, lse, *, name, B, S, Sk, P, q_cb, k_cb, v_cb, do_cb, causal,
                 cc=None, cr=None):
    tq = min(ATT_TILE, S)
    tk = min(ATT_TILE, Sk)
    nq, nk = S // tq, Sk // tk
    decay = cc is not None

    def body(*refs):
        if decay:
            q_ref, k_ref, v_ref, do_ref, o_ref, lse_ref, cc_ref, cr_ref, dq_ref, dk_ref, dv_ref, dcs_ref = refs
        else:
            q_ref, k_ref, v_ref, do_ref, o_ref, lse_ref, dq_ref, dk_ref, dv_ref = refs
        i = pl.program_id(2)

        @pl.when(i == 0)
        def _():
            dk_ref[...] = jnp.zeros_like(dk_ref)
            dv_ref[...] = jnp.zeros_like(dv_ref)
            if decay:
                dcs_ref[...] = jnp.zeros_like(dcs_ref)

        q = q_ref[...]
        do = do_ref[...]
        prod = do.astype(F32) * o_ref[...].astype(F32)
        lane = lax.broadcasted_iota(jnp.int32, (tq, LANES), 1)
        row = lax.broadcasted_iota(jnp.int32, (tq, tk), 0) + i * tq
        col0 = lax.broadcasted_iota(jnp.int32, (tq, tk), 1)
        dqs = []
        for hh in range(2):
            hmask = _head_mask(lane, hh)
            qh = jnp.where(hmask, q, jnp.zeros_like(q))
            doh = jnp.where(hmask, do, jnp.zeros_like(do))
            lse_h = lse_ref[0, hh]
            n_blocks = (i + 1) if causal else nk

            def probs(kb, hh=hh, qh=qh, doh=doh, lse_h=lse_h):
                ks = pl.multiple_of(kb * tk, tk)
                kblk = k_ref[pl.ds(ks, tk), :]
                vblk = v_ref[pl.ds(ks, tk), :]
                s = lax.dot_general(qh, kblk, NT, preferred_element_type=F32) * SCALE
                if decay:
                    s = s + (cc_ref[0, hh] - cr_ref[0, hh, :, pl.ds(ks, tk)])
                if causal:
                    s = jnp.where(col0 + kb * tk <= row, s, NEG)
                p = jnp.exp(s - lse_h)
                dp = lax.dot_general(doh, vblk, NT, preferred_element_type=F32)
                return ks, kblk, p, dp

            if decay:
                def delta_step(kb, acc):
                    _, _, p, dp = probs(kb)
                    return acc + jnp.sum(p * dp, axis=-1, keepdims=True)

                delta = lax.fori_loop(0, n_blocks, delta_step, jnp.zeros((tq, 1), F32))
            else:
                delta = jnp.sum(jnp.where(hmask, prod, 0.0), axis=-1, keepdims=True)

            def step(kb, dq_acc, hh=hh, qh=qh, doh=doh, delta=delta):
                ks, kblk, p, dp = probs(kb)
                ds = p * (dp - delta)
                dsb = ds.astype(BF16)
                dk_ref[pl.ds(ks, tk), :] += lax.dot_general(dsb, qh, TN, preferred_element_type=F32) * SCALE
                dv_ref[pl.ds(ks, tk), :] += lax.dot_general(p.astype(BF16), doh, TN, preferred_element_type=F32)
                if decay:
                    dcs_ref[0, hh, :, pl.ds(ks, tk)] -= jnp.sum(ds, axis=0, keepdims=True)
                return dq_acc + jnp.dot(dsb, kblk, preferred_element_type=F32)

            dqs.append(lax.fori_loop(0, n_blocks, step, jnp.zeros((tq, LANES), F32)) * SCALE)
        dq_ref[...] = jnp.where(lane < HEAD_DIM, dqs[0], dqs[1]).astype(BF16)

    in_specs = [_col_spec(tq, nq, q_cb), _kv_spec(Sk, k_cb), _kv_spec(Sk, v_cb), _col_spec(tq, nq, do_cb),
                _col_spec(tq, nq, 0), _stat_col_spec(tq)]
    args = [qa, ka, va, doa, oa, lse]
    out_shape = [_sds((B * S, P * LANES), BF16), _sds((B * Sk, P * LANES), F32), _sds((B * Sk, P * LANES), F32)]
    out_specs = [_col_spec(tq, nq, 0), _kv_spec(Sk, 0), _kv_spec(Sk, 0)]
    if decay:
        in_specs += [_stat_col_spec(tq), _stat_row_spec(S)]
        args += [cc, cr]
        out_shape.append(_sds((B, 2 * P, 1, S), F32))
        out_specs.append(_stat_row_spec(S))
    return _pc(body, name=name, out_shape=out_shape, grid=(B, P, nq), in_specs=in_specs, out_specs=out_specs,
               semantics=("arbitrary", "arbitrary", "arbitrary"), vmem=VMEM_BIG)(*args)


def _sb_terms(qh, kblk, row, col0, kb, tk):
    z = lax.dot_general(qh, kblk, NT, preferred_element_type=F32) * SCALE
    causal = (col0 + kb * tk) < row
    sp = jnp.maximum(z, 0.0) + jnp.log(1.0 + jnp.exp(-jnp.abs(z)))
    ls = z - sp
    lm = jnp.where(causal, -sp, 0.0)
    return causal, ls, lm


def _stickbreak_fwd(qa, ka, va, *, name, B, S, P, q_cb, k_cb, v_cb):
    tq = tk = min(ATT_TILE, S)
    nq = S // tq

    def body(q_ref, k_ref, v_ref, o_ref, rt_ref):
        i = pl.program_id(2)
        q = q_ref[...]
        lane = lax.broadcasted_iota(jnp.int32, (tq, LANES), 1)
        row = lax.broadcasted_iota(jnp.int32, (tq, tk), 0) + i * tq
        col0 = lax.broadcasted_iota(jnp.int32, (tq, tk), 1)
        t_r = lax.broadcasted_iota(jnp.int32, (tk, tk), 0)
        t_c = lax.broadcasted_iota(jnp.int32, (tk, tk), 1)
        after = (t_r > t_c).astype(BF16)
        outs = []
        for hh in range(2):
            qh = jnp.where(_head_mask(lane, hh), q, jnp.zeros_like(q))

            def step(jj, carry, qh=qh):
                run, acc = carry
                kb = i - jj
                ks = pl.multiple_of(kb * tk, tk)
                kblk = k_ref[pl.ds(ks, tk), :]
                vblk = v_ref[pl.ds(ks, tk), :]
                causal, ls, lm = _sb_terms(qh, kblk, row, col0, kb, tk)
                suf = _dot_tri2(lm, after)
                a = jnp.where(causal, jnp.exp(ls + run + suf), 0.0)
                acc = acc + jnp.dot(a.astype(BF16), vblk, preferred_element_type=F32)
                return run + jnp.sum(lm, axis=-1, keepdims=True), acc

            run, acc = lax.fori_loop(0, i + 1, step, (jnp.zeros((tq, 1), F32), jnp.zeros((tq, LANES), F32)))
            outs.append(acc)
            rt_ref[0, hh] = run
        o_ref[...] = jnp.where(lane < HEAD_DIM, outs[0], outs[1]).astype(BF16)

    return _pc(body, name=name, out_shape=[_sds((B * S, P * LANES), BF16), _sds((B, 2 * P, S, 1), F32)],
               grid=(B, P, nq), in_specs=[_col_spec(tq, nq, q_cb), _kv_spec(S, k_cb), _kv_spec(S, v_cb)],
               out_specs=[_col_spec(tq, nq, 0), _stat_col_spec(tq)],
               semantics=("arbitrary", "arbitrary", "arbitrary"), vmem=VMEM_BIG)(qa, ka, va)


def _stickbreak_bwd(qa, ka, va, doa, rt, *, name, B, S, P, q_cb, k_cb, v_cb, do_cb):
    tq = tk = min(ATT_TILE, S)
    nq = S // tq

    def body(q_ref, k_ref, v_ref, do_ref, rt_ref, dq_ref, dk_ref, dv_ref):
        i = pl.program_id(2)

        @pl.when(i == 0)
        def _():
            dk_ref[...] = jnp.zeros_like(dk_ref)
            dv_ref[...] = jnp.zeros_like(dv_ref)

        q = q_ref[...]
        do = do_ref[...]
        lane = lax.broadcasted_iota(jnp.int32, (tq, LANES), 1)
        row = lax.broadcasted_iota(jnp.int32, (tq, tk), 0) + i * tq
        col0 = lax.broadcasted_iota(jnp.int32, (tq, tk), 1)
        t_r = lax.broadcasted_iota(jnp.int32, (tk, tk), 0)
        t_c = lax.broadcasted_iota(jnp.int32, (tk, tk), 1)
        upto = (t_r <= t_c).astype(BF16)
        before = (t_r < t_c).astype(BF16)
        dqs = []
        for hh in range(2):
            hmask = _head_mask(lane, hh)
            qh = jnp.where(hmask, q, jnp.zeros_like(q))
            doh = jnp.where(hmask, do, jnp.zeros_like(do))
            rt_h = rt_ref[0, hh]

            def step(kb, carry, qh=qh, doh=doh, rt_h=rt_h):
                pl_sum, pg_sum, dq_acc = carry
                ks = pl.multiple_of(kb * tk, tk)
                kblk = k_ref[pl.ds(ks, tk), :]
                vblk = v_ref[pl.ds(ks, tk), :]
                causal, ls, lm = _sb_terms(qh, kblk, row, col0, kb, tk)
                pin = _dot_tri2(lm, upto)
                a = jnp.where(causal, jnp.exp(ls + (rt_h - pl_sum) - pin), 0.0)
                da = lax.dot_general(doh, vblk, NT, preferred_element_type=F32)
                gm = a * da
                pg = _dot_tri2(gm, before) + pg_sum
                beta = jnp.exp(ls)
                dz = jnp.where(causal, gm * (1.0 - beta) - pg * beta, 0.0)
                dzb = dz.astype(BF16)
                dk_ref[pl.ds(ks, tk), :] += lax.dot_general(dzb, qh, TN, preferred_element_type=F32) * SCALE
                dv_ref[pl.ds(ks, tk), :] += lax.dot_general(a.astype(BF16), doh, TN, preferred_element_type=F32)
                return (pl_sum + jnp.sum(lm, axis=-1, keepdims=True),
                        pg_sum + jnp.sum(gm, axis=-1, keepdims=True),
                        dq_acc + jnp.dot(dzb, kblk, preferred_element_type=F32))

            zc = jnp.zeros((tq, 1), F32)
            _, _, dq_h = lax.fori_loop(0, i + 1, step, (zc, zc, jnp.zeros((tq, LANES), F32)))
            dqs.append(dq_h * SCALE)
        dq_ref[...] = jnp.where(lane < HEAD_DIM, dqs[0], dqs[1]).astype(BF16)

    return _pc(body, name=name,
               out_shape=[_sds((B * S, P * LANES), BF16), _sds((B * S, P * LANES), F32), _sds((B * S, P * LANES), F32)],
               grid=(B, P, nq),
               in_specs=[_col_spec(tq, nq, q_cb), _kv_spec(S, k_cb), _kv_spec(S, v_cb), _col_spec(tq, nq, do_cb),
                         _stat_col_spec(tq)],
               out_specs=[_col_spec(tq, nq, 0), _kv_spec(S, 0), _kv_spec(S, 0)],
               semantics=("arbitrary", "arbitrary", "arbitrary"), vmem=VMEM_BIG)(qa, ka, va, doa, rt)


def _pair_setup(q_ref, tq, tk):
    q = q_ref[...] * jnp.asarray(SCALE, BF16)
    lane = lax.broadcasted_iota(jnp.int32, (tq, LANES), 1)
    masks = [_head_mask(lane, hh) for hh in range(2)]
    qh = [jnp.where(mk, q, jnp.zeros_like(q)) for mk in masks]
    on_or_below = (lax.broadcasted_iota(jnp.int32, (tq, tk), 1) <= lax.broadcasted_iota(jnp.int32, (tq, tk), 0))
    return lane, masks, qh, on_or_below


def _fox_fwd(qa, ka, va, cr, *, name, B, S, P, q_cb, k_cb, v_cb):
    tq = tk = min(ATT_TILE, S)
    nq = S // tq

    def body(q_ref, k_ref, v_ref, cr_ref, o_ref, lse_ref, s_buf, acc_ref, m_ref, l_ref):
        i = pl.program_id(2)
        lane, _, qh, on_or_below = _pair_setup(q_ref, tq, tk)
        m_ref[...] = jnp.full(m_ref.shape, NEG, F32)
        l_ref[...] = jnp.zeros(l_ref.shape, F32)
        acc_ref[...] = jnp.zeros(acc_ref.shape, F32)

        def scores(kb, slot):
            kblk = k_ref[pl.ds(pl.multiple_of(kb * tk, tk), tk), :]
            for hh in range(2):
                s_buf[slot, hh] = lax.dot_general(qh[hh], kblk, NT, preferred_element_type=F32)

        def block(kb, slot, diag):
            ks = pl.multiple_of(kb * tk, tk)
            vblk = v_ref[pl.ds(ks, tk), :]
            ps = []
            for hh in range(2):
                s = s_buf[slot, hh] - cr_ref[0, hh, :, pl.ds(ks, tk)]
                if diag:
                    s = jnp.where(on_or_below, s, NEG)
                m = m_ref[hh]
                m_new = jnp.maximum(m, jnp.max(s, axis=-1, keepdims=True))
                alpha = jnp.exp(m - m_new)
                p = jnp.exp(s - m_new)
                m_ref[hh] = m_new
                l_ref[hh] = alpha * l_ref[hh] + jnp.sum(p, axis=-1, keepdims=True)
                ps.append((alpha, p.astype(BF16)))
            for hh in range(2):
                acc_ref[hh] = ps[hh][0] * acc_ref[hh] + jnp.dot(ps[hh][1], vblk, preferred_element_type=F32)

        def step(kb, _):
            slot = lax.rem(kb, 2)
            scores(kb + 1, 1 - slot)
            block(kb, slot, False)
            return 0

        scores(0, 0)
        lax.fori_loop(0, i, step, 0)
        block(i, lax.rem(i, 2), True)
        l0, l1 = l_ref[0], l_ref[1]
        lse_ref[0, 0] = m_ref[0] + jnp.log(l0)
        lse_ref[0, 1] = m_ref[1] + jnp.log(l1)
        o_ref[...] = jnp.where(lane < HEAD_DIM, acc_ref[0] / l0, acc_ref[1] / l1).astype(BF16)

    return _pc(body, name=name, out_shape=[_sds((B * S, P * LANES), BF16), _sds((B, 2 * P, S, 1), F32)],
               grid=(B, P, nq),
               in_specs=[_col_spec(tq, nq, q_cb), _kv_spec(S, k_cb), _kv_spec(S, v_cb), _stat_row_spec(S)],
               out_specs=[_col_spec(tq, nq, 0), _stat_col_spec(tq)],
               scratch_shapes=[pltpu.VMEM((2, 2, tq, tk), F32), pltpu.VMEM((2, tq, LANES), F32),
                               pltpu.VMEM((2, tq, 1), F32), pltpu.VMEM((2, tq, 1), F32)],
               semantics=("arbitrary", "arbitrary", "arbitrary"), vmem=VMEM_BIG)(qa, ka, va, cr)


def _fox_bwd(qa, ka, va, doa, lse, cr, *, name, B, S, P, q_cb, k_cb, v_cb, do_cb):
    tq = tk = min(ATT_TILE, S)
    nq = S // tq

    def body(q_ref, k_ref, v_ref, do_ref, lse_ref, cr_ref, dq_ref, dk_ref, dv_ref, dcs_ref):
        i = pl.program_id(2)

        @pl.when(i == 0)
        def _():
            dk_ref[...] = jnp.zeros_like(dk_ref)
            dv_ref[...] = jnp.zeros_like(dv_ref)
            dcs_ref[...] = jnp.zeros_like(dcs_ref)

        lane, masks, qh, on_or_below = _pair_setup(q_ref, tq, tk)
        do = do_ref[...]
        doh = [jnp.where(mk, do, jnp.zeros_like(do)) for mk in masks]
        lse_h = [lse_ref[0, hh] for hh in range(2)]

        def probs(kb, hh, diag):
            ks = pl.multiple_of(kb * tk, tk)
            kblk = k_ref[pl.ds(ks, tk), :]
            vblk = v_ref[pl.ds(ks, tk), :]
            s = lax.dot_general(qh[hh], kblk, NT, preferred_element_type=F32) - cr_ref[0, hh, :, pl.ds(ks, tk)]
            if diag:
                s = jnp.where(on_or_below, s, NEG)
            p = jnp.exp(s - lse_h[hh])
            dp = lax.dot_general(doh[hh], vblk, NT, preferred_element_type=F32)
            return ks, kblk, p, dp

        def delta_block(kb, carry, diag):
            out = []
            for hh in range(2):
                _, _, p, dp = probs(kb, hh, diag)
                out.append(carry[hh] + jnp.sum(p * dp, axis=-1, keepdims=True))
            return tuple(out)

        zc = jnp.zeros((tq, 1), F32)
        delta = lax.fori_loop(0, i, lambda kb, c: delta_block(kb, c, False), (zc, zc))
        delta = delta_block(i, delta, True)

        def grad_block(kb, carry, diag):
            out = []
            for hh in range(2):
                ks, kblk, p, dp = probs(kb, hh, diag)
                ds = p * (dp - delta[hh])
                dsb = ds.astype(BF16)
                rows = pl.ds(ks, tk)
                dk_ref[rows, :] += lax.dot_general(dsb, qh[hh], TN, preferred_element_type=F32)
                dv_ref[rows, :] += lax.dot_general(p.astype(BF16), doh[hh], TN, preferred_element_type=F32)
                dcs_ref[0, hh, :, rows] -= jnp.sum(ds, axis=0, keepdims=True)
                out.append(carry[hh] + jnp.dot(dsb, kblk, preferred_element_type=F32))
            return tuple(out)

        za = jnp.zeros((tq, LANES), F32)
        dq = lax.fori_loop(0, i, lambda kb, c: grad_block(kb, c, False), (za, za))
        dq = grad_block(i, dq, True)
        dq_ref[...] = (jnp.where(lane < HEAD_DIM, dq[0], dq[1]) * SCALE).astype(BF16)

    return _pc(body, name=name,
               out_shape=[_sds((B * S, P * LANES), BF16), _sds((B * S, P * LANES), F32), _sds((B * S, P * LANES), F32),
                          _sds((B, 2 * P, 1, S), F32)],
               grid=(B, P, nq),
               in_specs=[_col_spec(tq, nq, q_cb), _kv_spec(S, k_cb), _kv_spec(S, v_cb), _col_spec(tq, nq, do_cb),
                         _stat_col_spec(tq), _stat_row_spec(S)],
               out_specs=[_col_spec(tq, nq, 0), _kv_spec(S, 0), _kv_spec(S, 0), _stat_row_spec(S)],
               semantics=("arbitrary", "arbitrary", "arbitrary"), vmem=VMEM_BIG)(qa, ka, va, doa, lse, cr)


def _sb_logs(qh, kblk):
    z = lax.dot_general(qh, kblk, NT, preferred_element_type=F32)
    nz = -z
    lg = jnp.log(1.0 + jnp.exp(jnp.minimum(z, nz)))
    lm = jnp.minimum(nz, 0.0) - lg
    return lm + z, lm


def _sb_fwd(qa, ka, va, *, name, B, S, P, q_cb, k_cb, v_cb):
    tq = tk = min(ATT_TILE, S)
    nq = S // tq

    def body(q_ref, k_ref, v_ref, o_ref, rt_ref):
        i = pl.program_id(2)
        lane, _, qh, on_or_below = _pair_setup(q_ref, tq, tk)
        t_r = lax.broadcasted_iota(jnp.int32, (tk, tk), 0)
        t_c = lax.broadcasted_iota(jnp.int32, (tk, tk), 1)
        after = (t_r > t_c).astype(BF16)
        below = t_c < t_r

        def block(kb, carry, diag):
            ks = pl.multiple_of(kb * tk, tk)
            kblk = k_ref[pl.ds(ks, tk), :]
            vblk = v_ref[pl.ds(ks, tk), :]
            out = []
            for hh in range(2):
                run, acc = carry[hh]
                ls, lm = _sb_logs(qh[hh], kblk)
                if diag:
                    lm = jnp.where(below, lm, 0.0)
                a = jnp.exp(ls + run + _dot_tri2(lm, after))
                if diag:
                    a = jnp.where(below, a, 0.0)
                acc = acc + jnp.dot(a.astype(BF16), vblk, preferred_element_type=F32)
                out.append((run + jnp.sum(lm, axis=-1, keepdims=True), acc))
            return tuple(out)

        one = (jnp.zeros((tq, 1), F32), jnp.zeros((tq, LANES), F32))
        carry = block(i, (one, one), True)
        (r0, a0), (r1, a1) = lax.fori_loop(0, i, lambda jj, c: block(i - 1 - jj, c, False), carry)
        rt_ref[0, 0] = r0
        rt_ref[0, 1] = r1
        o_ref[...] = jnp.where(lane < HEAD_DIM, a0, a1).astype(BF16)

    return _pc(body, name=name, out_shape=[_sds((B * S, P * LANES), BF16), _sds((B, 2 * P, S, 1), F32)],
               grid=(B, P, nq), in_specs=[_col_spec(tq, nq, q_cb), _kv_spec(S, k_cb), _kv_spec(S, v_cb)],
               out_specs=[_col_spec(tq, nq, 0), _stat_col_spec(tq)],
               semantics=("arbitrary", "arbitrary", "arbitrary"), vmem=VMEM_BIG)(qa, ka, va)


def _sb_bwd(qa, ka, va, doa, rt, *, name, B, S, P, q_cb, k_cb, v_cb, do_cb):
    tq = tk = min(ATT_TILE, S)
    nq = S // tq

    def body(q_ref, k_ref, v_ref, do_ref, rt_ref, dq_ref, dk_ref, dv_ref):
        i = pl.program_id(2)

        @pl.when(i == 0)
        def _():
            dk_ref[...] = jnp.zeros_like(dk_ref)
            dv_ref[...] = jnp.zeros_like(dv_ref)

        lane, masks, qh, _ = _pair_setup(q_ref, tq, tk)
        do = do_ref[...]
        doh = [jnp.where(mk, do, jnp.zeros_like(do)) for mk in masks]
        rt_h = [rt_ref[0, hh] for hh in range(2)]
        t_r = lax.broadcasted_iota(jnp.int32, (tk, tk), 0)
        t_c = lax.broadcasted_iota(jnp.int32, (tk, tk), 1)
        upto = (t_r <= t_c).astype(BF16)
        before = (t_r < t_c).astype(BF16)
        below = t_c < t_r

        def block(kb, carry, diag):
            ks = pl.multiple_of(kb * tk, tk)
            rows = pl.ds(ks, tk)
            kblk = k_ref[rows, :]
            vblk = v_ref[rows, :]
            out = []
            for hh in range(2):
                pl_sum, pg_sum, dq_acc = carry[hh]
                ls, lm = _sb_logs(qh[hh], kblk)
                if diag:
                    lm = jnp.where(below, lm, 0.0)
                a = jnp.exp(ls + (rt_h[hh] - pl_sum) - _dot_tri2(lm, upto))
                if diag:
                    a = jnp.where(below, a, 0.0)
                gm = a * lax.dot_general(doh[hh], vblk, NT, preferred_element_type=F32)
                pg = _dot_tri2(gm, before) + pg_sum
                dz = gm - jnp.exp(ls) * (gm + pg)
                if diag:
                    dz = jnp.where(below, dz, 0.0)
                dzb = dz.astype(BF16)
                dk_ref[rows, :] += lax.dot_general(dzb, qh[hh], TN, preferred_element_type=F32)
                dv_ref[rows, :] += lax.dot_general(a.astype(BF16), doh[hh], TN, preferred_element_type=F32)
                out.append((pl_sum + jnp.sum(lm, axis=-1, keepdims=True),
                            pg_sum + jnp.sum(gm, axis=-1, keepdims=True),
                            dq_acc + jnp.dot(dzb, kblk, preferred_element_type=F32)))
            return tuple(out)

        zc = jnp.zeros((tq, 1), F32)
        one = (zc, zc, jnp.zeros((tq, LANES), F32))
        carry = lax.fori_loop(0, i, lambda kb, c: block(kb, c, False), (one, one))
        (_, _, dq0), (_, _, dq1) = block(i, carry, True)
        dq_ref[...] = (jnp.where(lane < HEAD_DIM, dq0, dq1) * SCALE).astype(BF16)

    return _pc(body, name=name,
               out_shape=[_sds((B * S, P * LANES), BF16), _sds((B * S, P * LANES), F32), _sds((B * S, P * LANES), F32)],
               grid=(B, P, nq),
               in_specs=[_col_spec(tq, nq, q_cb), _kv_spec(S, k_cb), _kv_spec(S, v_cb), _col_spec(tq, nq, do_cb),
                         _stat_col_spec(tq)],
               out_specs=[_col_spec(tq, nq, 0), _kv_spec(S, 0), _kv_spec(S, 0)],
               semantics=("arbitrary", "arbitrary", "arbitrary"), vmem=VMEM_BIG)(qa, ka, va, doa, rt)


HEAD_GROUP = 2


def _g_col_spec(rows, nblk_rows, cb, G):
    return pl.BlockSpec((rows, G * LANES), lambda b, p, i: (b * nblk_rows + i, cb // G + p))


def _g_kv_spec(rows, cb, G):
    return pl.BlockSpec((rows, G * LANES), lambda b, p, i: (b, cb // G + p))


def _g_stat_col_spec(tq, G):
    return pl.BlockSpec((1, 2 * G, tq, 1), lambda b, p, i: (b, p, i, 0))


def _g_stat_row_spec(S, G):
    return pl.BlockSpec((1, 2 * G, 1, S), lambda b, p, i: (b, p, 0, 0))


def _lanes(g):
    return slice(g * LANES, (g + 1) * LANES)


def _streams(x_ref, G, scale=None):
    rows = x_ref.shape[0]
    lane = lax.broadcasted_iota(jnp.int32, (rows, LANES), 1)
    out = []
    for g in range(G):
        x = x_ref[:, _lanes(g)]
        if scale is not None:
            x = x * jnp.asarray(scale, x.dtype)
        for hh in range(2):
            out.append(jnp.where(_head_mask(lane, hh), x, jnp.zeros_like(x)))
    return lane, out


def _kv_blocks(ref, ks, tk, G):
    return [ref[pl.ds(ks, tk), _lanes(g)] for g in range(G)]


def _sweep(i, block):
    def step(kb, c):
        block(kb, False)
        return c
    lax.fori_loop(0, i, step, 0)
    block(i, True)


def _fox_fwd_g(qa, ka, va, cr, *, name, B, S, P, q_cb, k_cb, v_cb, G=HEAD_GROUP):
    tq = tk = min(ATT_TILE, S)
    nq = S // tq
    NS = 2 * G

    def body(q_ref, k_ref, v_ref, cr_ref, o_ref, lse_ref, acc_ref, m_ref, l_ref):
        i = pl.program_id(2)
        lane, qh = _streams(q_ref, G, SCALE)
        on_or_below = (lax.broadcasted_iota(jnp.int32, (tq, tk), 1) <= lax.broadcasted_iota(jnp.int32, (tq, tk), 0))
        m_ref[...] = jnp.full(m_ref.shape, NEG, F32)
        l_ref[...] = jnp.zeros(l_ref.shape, F32)
        acc_ref[...] = jnp.zeros(acc_ref.shape, F32)

        def block(kb, diag):
            ks = pl.multiple_of(kb * tk, tk)
            kblk = _kv_blocks(k_ref, ks, tk, G)
            vblk = _kv_blocks(v_ref, ks, tk, G)
            ss = [lax.dot_general(qh[st], kblk[st // 2], NT, preferred_element_type=F32) for st in range(NS)]
            ps = []
            for st in range(NS):
                s = ss[st] - cr_ref[0, st, :, pl.ds(ks, tk)]
                if diag:
                    s = jnp.where(on_or_below, s, NEG)
                m = m_ref[st]
                m_new = jnp.maximum(m, jnp.max(s, axis=-1, keepdims=True))
                alpha = jnp.exp(m - m_new)
                p = jnp.exp(s - m_new)
                m_ref[st] = m_new
                l_ref[st] = alpha * l_ref[st] + jnp.sum(p, axis=-1, keepdims=True)
                ps.append((alpha, p.astype(BF16)))
            pvs = [jnp.dot(ps[st][1], vblk[st // 2], preferred_element_type=F32) for st in range(NS)]
            for st in range(NS):
                acc_ref[st] = ps[st][0] * acc_ref[st] + pvs[st]

        _sweep(i, block)
        for st in range(NS):
            lse_ref[0, st] = m_ref[st] + jnp.log(l_ref[st])
        for g in range(G):
            o_ref[:, _lanes(g)] = jnp.where(lane < HEAD_DIM, acc_ref[2 * g] / l_ref[2 * g],
                                            acc_ref[2 * g + 1] / l_ref[2 * g + 1]).astype(BF16)

    return _pc(body, name=name, out_shape=[_sds((B * S, P * LANES), BF16), _sds((B, 2 * P, S, 1), F32)],
               grid=(B, P // G, nq),
               in_specs=[_g_col_spec(tq, nq, q_cb, G), _g_kv_spec(S, k_cb, G), _g_kv_spec(S, v_cb, G),
                         _g_stat_row_spec(S, G)],
               out_specs=[_g_col_spec(tq, nq, 0, G), _g_stat_col_spec(tq, G)],
               scratch_shapes=[pltpu.VMEM((NS, tq, LANES), F32), pltpu.VMEM((NS, tq, 1), F32),
                               pltpu.VMEM((NS, tq, 1), F32)],
               semantics=("arbitrary", "arbitrary", "arbitrary"), vmem=VMEM_BIG)(qa, ka, va, cr)


def _fox_bwd_g(qa, ka, va, doa, lse, cr, *, name, B, S, P, q_cb, k_cb, v_cb, do_cb, G=HEAD_GROUP):
    tq = tk = min(ATT_TILE, S)
    nq = S // tq
    NS = 2 * G

    def body(q_ref, k_ref, v_ref, do_ref, lse_ref, cr_ref, dq_ref, dk_ref, dv_ref, dcs_ref, dqa_ref, delta_ref):
        i = pl.program_id(2)

        @pl.when(i == 0)
        def _():
            dk_ref[...] = jnp.zeros_like(dk_ref)
            dv_ref[...] = jnp.zeros_like(dv_ref)
            dcs_ref[...] = jnp.zeros_like(dcs_ref)

        lane, qh = _streams(q_ref, G, SCALE)
        _, doh = _streams(do_ref, G)
        on_or_below = (lax.broadcasted_iota(jnp.int32, (tq, tk), 1) <= lax.broadcasted_iota(jnp.int32, (tq, tk), 0))
        delta_ref[...] = jnp.zeros(delta_ref.shape, F32)
        dqa_ref[...] = jnp.zeros(dqa_ref.shape, F32)

        def probs(kb, diag):
            ks = pl.multiple_of(kb * tk, tk)
            kblk = _kv_blocks(k_ref, ks, tk, G)
            vblk = _kv_blocks(v_ref, ks, tk, G)
            ss = [lax.dot_general(qh[st], kblk[st // 2], NT, preferred_element_type=F32) for st in range(NS)]
            dps = [lax.dot_general(doh[st], vblk[st // 2], NT, preferred_element_type=F32) for st in range(NS)]
            ps = []
            for st in range(NS):
                s = ss[st] - cr_ref[0, st, :, pl.ds(ks, tk)]
                if diag:
                    s = jnp.where(on_or_below, s, NEG)
                ps.append(jnp.exp(s - lse_ref[0, st]))
            return ks, kblk, ps, dps

        def delta_block(kb, diag):
            _, _, ps, dps = probs(kb, diag)
            for st in range(NS):
                delta_ref[st] += jnp.sum(ps[st] * dps[st], axis=-1, keepdims=True)

        _sweep(i, delta_block)

        def grad_block(kb, diag):
            ks, kblk, ps, dps = probs(kb, diag)
            rows = pl.ds(ks, tk)
            dsb, pb = [], []
            for st in range(NS):
                ds = ps[st] * (dps[st] - delta_ref[st])
                dcs_ref[0, st, :, rows] -= jnp.sum(ds, axis=0, keepdims=True)
                dsb.append(ds.astype(BF16))
                pb.append(ps[st].astype(BF16))
            dks = [lax.dot_general(dsb[st], qh[st], TN, preferred_element_type=F32) for st in range(NS)]
            dvs = [lax.dot_general(pb[st], doh[st], TN, preferred_element_type=F32) for st in range(NS)]
            dqs = [jnp.dot(dsb[st], kblk[st // 2], preferred_element_type=F32) for st in range(NS)]
            for g in range(G):
                dk_ref[rows, _lanes(g)] += dks[2 * g] + dks[2 * g + 1]
                dv_ref[rows, _lanes(g)] += dvs[2 * g] + dvs[2 * g + 1]
            for st in range(NS):
                dqa_ref[st] += dqs[st]

        _sweep(i, grad_block)
        for g in range(G):
            dq_ref[:, _lanes(g)] = (jnp.where(lane < HEAD_DIM, dqa_ref[2 * g], dqa_ref[2 * g + 1]) * SCALE).astype(BF16)

    return _pc(body, name=name,
               out_shape=[_sds((B * S, P * LANES), BF16), _sds((B * S, P * LANES), F32), _sds((B * S, P * LANES), F32),
                          _sds((B, 2 * P, 1, S), F32)],
               grid=(B, P // G, nq),
               in_specs=[_g_col_spec(tq, nq, q_cb, G), _g_kv_spec(S, k_cb, G), _g_kv_spec(S, v_cb, G),
                         _g_col_spec(tq, nq, do_cb, G), _g_stat_col_spec(tq, G), _g_stat_row_spec(S, G)],
               out_specs=[_g_col_spec(tq, nq, 0, G), _g_kv_spec(S, 0, G), _g_kv_spec(S, 0, G), _g_stat_row_spec(S, G)],
               scratch_shapes=[pltpu.VMEM((NS, tq, LANES), F32), pltpu.VMEM((NS, tq, 1), F32)],
               semantics=("arbitrary", "arbitrary", "arbitrary"), vmem=VMEM_BIG)(qa, ka, va, doa, lse, cr)


def _sb_logs_z(z):
    nz = -z
    lm = jnp.minimum(nz, 0.0) - jnp.log(1.0 + jnp.exp(jnp.minimum(z, nz)))
    return lm + z, lm


def _sb_fwd_g(qa, ka, va, *, name, B, S, P, q_cb, k_cb, v_cb, G=HEAD_GROUP):
    tq = tk = min(ATT_TILE, S)
    nq = S // tq
    NS = 2 * G

    def body(q_ref, k_ref, v_ref, o_ref, rt_ref, acc_ref, run_ref):
        i = pl.program_id(2)
        lane, qh = _streams(q_ref, G, SCALE)
        t_r = lax.broadcasted_iota(jnp.int32, (tk, tk), 0)
        t_c = lax.broadcasted_iota(jnp.int32, (tk, tk), 1)
        after = (t_r > t_c).astype(BF16)
        below = t_c < t_r
        acc_ref[...] = jnp.zeros(acc_ref.shape, F32)
        run_ref[...] = jnp.zeros(run_ref.shape, F32)

        def block(kb, diag):
            ks = pl.multiple_of(kb * tk, tk)
            kblk = _kv_blocks(k_ref, ks, tk, G)
            vblk = _kv_blocks(v_ref, ks, tk, G)
            zs = [lax.dot_general(qh[st], kblk[st // 2], NT, preferred_element_type=F32) for st in range(NS)]
            lss, parts = [], []
            for st in range(NS):
                ls, lm = _sb_logs_z(zs[st])
                if diag:
                    lm = jnp.where(below, lm, 0.0)
                lss.append(ls + run_ref[st])
                run_ref[st] += jnp.sum(lm, axis=-1, keepdims=True)
                parts.append(_split2(lm))
            sufs = [jnp.dot(parts[st][0], after, preferred_element_type=F32)
                    + jnp.dot(parts[st][1], after, preferred_element_type=F32) for st in range(NS)]
            ab = []
            for st in range(NS):
                a = jnp.exp(lss[st] + sufs[st])
                if diag:
                    a = jnp.where(below, a, 0.0)
                ab.append(a.astype(BF16))
            pvs = [jnp.dot(ab[st], vblk[st // 2], preferred_element_type=F32) for st in range(NS)]
            for st in range(NS):
                acc_ref[st] += pvs[st]

        block(i, True)

        def step(jj, c):
            block(i - 1 - jj, False)
            return c

        lax.fori_loop(0, i, step, 0)
        for st in range(NS):
            rt_ref[0, st] = run_ref[st]
        for g in range(G):
            o_ref[:, _lanes(g)] = jnp.where(lane < HEAD_DIM, acc_ref[2 * g], acc_ref[2 * g + 1]).astype(BF16)

    return _pc(body, name=name, out_shape=[_sds((B * S, P * LANES), BF16), _sds((B, 2 * P, S, 1), F32)],
               grid=(B, P // G, nq),
               in_specs=[_g_col_spec(tq, nq, q_cb, G), _g_kv_spec(S, k_cb, G), _g_kv_spec(S, v_cb, G)],
               out_specs=[_g_col_spec(tq, nq, 0, G), _g_stat_col_spec(tq, G)],
               scratch_shapes=[pltpu.VMEM((NS, tq, LANES), F32), pltpu.VMEM((NS, tq, 1), F32)],
               semantics=("arbitrary", "arbitrary", "arbitrary"), vmem=VMEM_BIG)(qa, ka, va)


def _sb_bwd_g(qa, ka, va, doa, rt, *, name, B, S, P, q_cb, k_cb, v_cb, do_cb, G=HEAD_GROUP):
    tq = tk = min(ATT_TILE, S)
    nq = S // tq
    NS = 2 * G

    def body(q_ref, k_ref, v_ref, do_ref, rt_ref, dq_ref, dk_ref, dv_ref, dqa_ref, pl_ref, pg_ref):
        i = pl.program_id(2)

        @pl.when(i == 0)
        def _():
            dk_ref[...] = jnp.zeros_like(dk_ref)
            dv_ref[...] = jnp.zeros_like(dv_ref)

        lane, qh = _streams(q_ref, G, SCALE)
        _, doh = _streams(do_ref, G)
        t_r = lax.broadcasted_iota(jnp.int32, (tk, tk), 0)
        t_c = lax.broadcasted_iota(jnp.int32, (tk, tk), 1)
        upto = (t_r <= t_c).astype(BF16)
        before = (t_r < t_c).astype(BF16)
        below = t_c < t_r
        dqa_ref[...] = jnp.zeros(dqa_ref.shape, F32)
        pg_ref[...] = jnp.zeros(pg_ref.shape, F32)
        for st in range(NS):
            pl_ref[st] = rt_ref[0, st]

        def block(kb, diag):
            ks = pl.multiple_of(kb * tk, tk)
            rows = pl.ds(ks, tk)
            kblk = _kv_blocks(k_ref, ks, tk, G)
            vblk = _kv_blocks(v_ref, ks, tk, G)
            zs = [lax.dot_general(qh[st], kblk[st // 2], NT, preferred_element_type=F32) for st in range(NS)]
            das = [lax.dot_general(doh[st], vblk[st // 2], NT, preferred_element_type=F32) for st in range(NS)]
            lss, parts = [], []
            for st in range(NS):
                ls, lm = _sb_logs_z(zs[st])
                if diag:
                    lm = jnp.where(below, lm, 0.0)
                lss.append((ls, ls + pl_ref[st]))
                pl_ref[st] -= jnp.sum(lm, axis=-1, keepdims=True)
                parts.append(_split2(lm))
            pins = [jnp.dot(parts[st][0], upto, preferred_element_type=F32)
                    + jnp.dot(parts[st][1], upto, preferred_element_type=F32) for st in range(NS)]
            gms, ab, gparts = [], [], []
            for st in range(NS):
                a = jnp.exp(lss[st][1] - pins[st])
                if diag:
                    a = jnp.where(below, a, 0.0)
                gm = a * das[st]
                gms.append(gm)
                ab.append(a.astype(BF16))
                gparts.append(_split2(gm))
            pgs = [jnp.dot(gparts[st][0], before, preferred_element_type=F32)
                   + jnp.dot(gparts[st][1], before, preferred_element_type=F32) for st in range(NS)]
            dzb = []
            for st in range(NS):
                gm = gms[st]
                dz = gm - jnp.exp(lss[st][0]) * (gm + (pgs[st] + pg_ref[st]))
                if diag:
                    dz = jnp.where(below, dz, 0.0)
                pg_ref[st] += jnp.sum(gm, axis=-1, keepdims=True)
                dzb.append(dz.astype(BF16))
            dks = [lax.dot_general(dzb[st], qh[st], TN, preferred_element_type=F32) for st in range(NS)]
            dvs = [lax.dot_general(ab[st], doh[st], TN, preferred_element_type=F32) for st in range(NS)]
            dqs = [jnp.dot(dzb[st], kblk[st // 2], preferred_element_type=F32) for st in range(NS)]
            for g in range(G):
                dk_ref[rows, _lanes(g)] += dks[2 * g] + dks[2 * g + 1]
                dv_ref[rows, _lanes(g)] += dvs[2 * g] + dvs[2 * g + 1]
            for st in range(NS):
                dqa_ref[st] += dqs[st]

        _sweep(i, block)
        for g in range(G):
            dq_ref[:, _lanes(g)] = (jnp.where(lane < HEAD_DIM, dqa_ref[2 * g], dqa_ref[2 * g + 1]) * SCALE).astype(BF16)

    return _pc(body, name=name,
               out_shape=[_sds((B * S, P * LANES), BF16), _sds((B * S, P * LANES), F32), _sds((B * S, P * LANES), F32)],
               grid=(B, P // G, nq),
               in_specs=[_g_col_spec(tq, nq, q_cb, G), _g_kv_spec(S, k_cb, G), _g_kv_spec(S, v_cb, G),
                         _g_col_spec(tq, nq, do_cb, G), _g_stat_col_spec(tq, G)],
               out_specs=[_g_col_spec(tq, nq, 0, G), _g_kv_spec(S, 0, G), _g_kv_spec(S, 0, G)],
               scratch_shapes=[pltpu.VMEM((NS, tq, LANES), F32), pltpu.VMEM((NS, tq, 1), F32),
                               pltpu.VMEM((NS, tq, 1), F32)],
               semantics=("arbitrary", "arbitrary", "arbitrary"), vmem=VMEM_BIG)(qa, ka, va, doa, rt)


def _shift_rows(cur, halo_ref, first, rows_idx, k):
    out = pltpu.roll(cur, k, 0)
    for r in range(k):
        edge = jnp.where(first, 0.0, halo_ref[8 - k + r:8 - k + r + 1, :])
        out = jnp.where(rows_idx == r, edge, out)
    return out


def _shift_rows_up(cur, halo_ref, last, rows_idx, k, ts):
    out = pltpu.roll(cur, ts - k, 0)
    for r in range(k):
        edge = jnp.where(last, 0.0, halo_ref[r:r + 1, :])
        out = jnp.where(rows_idx == ts - k + r, edge, out)
    return out


def _conv_taps(main_ref, halo_ref, w_ref, b_ref, first, rows_idx):
    cur = main_ref[...]
    m1 = _shift_rows(cur, halo_ref, first, rows_idx, 1)
    m2 = _shift_rows(cur, halo_ref, first, rows_idx, 2)
    uc = b_ref[...] + w_ref[0:1, :] * m2 + w_ref[1:2, :] * m1 + w_ref[2:3, :] * cur
    return uc, cur, m1, m2


def _conv_specs(ts, tf, ns, nf, S, order):
    def wrap(fn):
        return lambda *g: fn(*order(*g))
    specs = []
    for off in (0, nf):
        specs.append(pl.BlockSpec((ts, tf), wrap(lambda b, i, j, off=off: (b * ns + i, j + off))))
        specs.append(pl.BlockSpec((8, tf), wrap(
            lambda b, i, j, off=off: (jnp.maximum((b * S + i * ts) // 8 - 1, 0), j + off))))
    for off in (0, nf):
        specs.append(pl.BlockSpec((3, tf), wrap(lambda b, i, j, off=off: (0, j + off))))
    for off in (0, nf):
        specs.append(pl.BlockSpec((1, tf), wrap(lambda b, i, j, off=off: (0, j + off))))
    return specs


def _conv_gate_fwd(u, cw, cb, *, name, B, S):
    F = u.shape[1] // 2
    tf = F // 2
    ts = min(256, S)
    ns, nf = S // ts, F // tf

    def body(ug_ref, ugh_ref, uv_ref, uvh_ref, wg_ref, wv_ref, bg_ref, bv_ref, a_ref):
        first = pl.program_id(1) == 0
        rows_idx = lax.broadcasted_iota(jnp.int32, (ts, tf), 0)
        ucg = _conv_taps(ug_ref, ugh_ref, wg_ref, bg_ref, first, rows_idx)[0]
        ucv = _conv_taps(uv_ref, uvh_ref, wv_ref, bv_ref, first, rows_idx)[0]
        a_ref[...] = (ucg * (1.0 / (1.0 + jnp.exp(-ucg))) * ucv).astype(BF16)

    specs = _conv_specs(ts, tf, ns, nf, S, lambda b, i, j: (b, i, j))
    return _pc(body, name=name, out_shape=_sds((B * S, F), BF16), grid=(B, ns, nf), in_specs=specs,
               out_specs=pl.BlockSpec((ts, tf), lambda b, i, j: (b * ns + i, j)),
               semantics=("arbitrary", "arbitrary", "arbitrary"), vmem=VMEM_BIG)(u, u, u, u, cw, cw, cb, cb)


def _conv_gate_bwd(da, u, cw, cb, *, name, B, S):
    F = u.shape[1] // 2
    tf = F // 2
    ts = min(256, S)
    ns, nf = S // ts, F // tf

    def body(da_ref, ug_ref, ugh_ref, uv_ref, uvh_ref, wg_ref, wv_ref, bg_ref, bv_ref,
             dg_ref, dv_ref, pg_ref, pv_ref):
        first = pl.program_id(2) == 0

        @pl.when(jnp.logical_and(pl.program_id(1) == 0, first))
        def _():
            pg_ref[...] = jnp.zeros_like(pg_ref)
            pv_ref[...] = jnp.zeros_like(pv_ref)

        rows_idx = lax.broadcasted_iota(jnp.int32, (ts, tf), 0)
        ucg, g0, g1, g2 = _conv_taps(ug_ref, ugh_ref, wg_ref, bg_ref, first, rows_idx)
        ucv, v0, v1, v2 = _conv_taps(uv_ref, uvh_ref, wv_ref, bv_ref, first, rows_idx)
        sg = 1.0 / (1.0 + jnp.exp(-ucg))
        dav = da_ref[...]
        d_v = dav * (ucg * sg)
        d_g = dav * ucv * (sg * (1.0 + ucg * (1.0 - sg)))
        dg_ref[...] = d_g
        dv_ref[...] = d_v
        for p_ref, d, taps in ((pg_ref, d_g, (g2, g1, g0)), (pv_ref, d_v, (v2, v1, v0))):
            for k in range(3):
                p_ref[k:k + 1, :] += jnp.sum(d * taps[k], axis=0, keepdims=True)
            p_ref[3:4, :] += jnp.sum(d, axis=0, keepdims=True)

    specs = [pl.BlockSpec((ts, tf), lambda j, b, i: (b * ns + i, j))]
    specs += _conv_specs(ts, tf, ns, nf, S, lambda j, b, i: (b, i, j))
    row = pl.BlockSpec((ts, tf), lambda j, b, i: (b * ns + i, j))
    par = pl.BlockSpec((8, tf), lambda j, b, i: (0, j))
    return _pc(body, name=name,
               out_shape=[_sds((B * S, F), F32), _sds((B * S, F), F32), _sds((8, F), F32), _sds((8, F), F32)],
               grid=(nf, B, ns), in_specs=specs, out_specs=[row, row, par, par],
               semantics=("arbitrary", "arbitrary", "arbitrary"), vmem=VMEM_BIG)(da, u, u, u, u, cw, cw, cb, cb)


def _conv_transpose(d, cw, *, name, B, S, col_off):
    F = d.shape[1]
    tf = F // 2
    ts = min(256, S)
    ns, nf = S // ts, F // tf
    nblk8 = B * S // 8

    def body(d_ref, dh_ref, w_ref, o_ref):
        last = pl.program_id(1) == ns - 1
        rows_idx = lax.broadcasted_iota(jnp.int32, (ts, tf), 0)
        cur = d_ref[...]
        p1 = _shift_rows_up(cur, dh_ref, last, rows_idx, 1, ts)
        p2 = _shift_rows_up(cur, dh_ref, last, rows_idx, 2, ts)
        o_ref[...] = (w_ref[2:3, :] * cur + w_ref[1:2, :] * p1 + w_ref[0:1, :] * p2).astype(BF16)

    return _pc(body, name=name, out_shape=_sds((B * S, F), BF16), grid=(B, ns, nf),
               in_specs=[pl.BlockSpec((ts, tf), lambda b, i, j: (b * ns + i, j)),
                         pl.BlockSpec((8, tf), lambda b, i, j: (jnp.minimum((b * S + (i + 1) * ts) // 8, nblk8 - 1), j)),
                         pl.BlockSpec((3, tf), lambda b, i, j: (0, j + col_off * nf))],
               out_specs=pl.BlockSpec((ts, tf), lambda b, i, j: (b * ns + i, j)),
               semantics=("arbitrary", "arbitrary", "arbitrary"))(d, d, cw)


def _adamw(w, g, m, v, *, name):
    rows, cols = w.shape
    tr = rows
    while tr * cols * 4 > 2 ** 20 and tr % 16 == 0:
        tr //= 2

    def body(w_ref, g_ref, m_ref, v_ref, d_ref, nm_ref, nv_ref):
        gv = g_ref[...]
        m_new = ADAM_B1 * m_ref[...] + (1.0 - ADAM_B1) * gv
        v_new = ADAM_B2 * v_ref[...] + (1.0 - ADAM_B2) * (gv * gv)
        m_hat = m_new / (1.0 - ADAM_B1 ** ADAM_STEP)
        v_hat = v_new / (1.0 - ADAM_B2 ** ADAM_STEP)
        d_ref[...] = -ADAM_LR * (m_hat / (jnp.sqrt(v_hat) + ADAM_EPS) + ADAM_WD * w_ref[...])
        nm_ref[...] = m_new
        nv_ref[...] = v_new

    blk = pl.BlockSpec((tr, cols), lambda i: (i, 0))
    return _pc(body, name=name, out_shape=[_sds((rows, cols), F32)] * 3, grid=(rows // tr,),
               in_specs=[blk] * 4, out_specs=[blk] * 3, semantics=("arbitrary",))(w, g, m, v)


def _my_pos():
    return lax.axis_index("x"), lax.axis_index("y"), lax.axis_index("c")


_HBM = pl.BlockSpec(memory_space=pltpu.HBM)
_SEM = pl.BlockSpec(memory_space=pltpu.SEMAPHORE)
_EFFECT = pltpu.SideEffectType.DATAFLOW_SIDE_EFFECTING


def _peers():
    x, y, c = _my_pos()
    out = []
    for k in range(1, N_DEV):
        px, py, pc = x ^ ((k >> 2) & 1), y ^ ((k >> 1) & 1), c ^ (k & 1)
        out.append(((px, py, pc), 4 * px + 2 * py + pc))
    return out


def _scatter_start(srcs, slot_of, *, name, order_after=None):
    n = len(srcs)
    lands = [lax.empty((N_DEV,) + slot_of(s, 0, shape_only=True), s.dtype) for s in srcs]
    extra = [] if order_after is None else [order_after]

    def body(*refs):
        src_refs, land_refs = refs[:n], refs[n:2 * n]
        send_sems, recv_sems = refs[2 * n + len(extra)], refs[2 * n + len(extra) + 1]
        token = refs[-1]
        x, y, c = _my_pos()
        me = 4 * x + 2 * y + c
        for a in range(n):
            for k, (peer, peer_idx) in enumerate(_peers()):
                pltpu.make_async_remote_copy(
                    src_ref=slot_of(src_refs[a], peer_idx), dst_ref=land_refs[a].at[me],
                    send_sem=send_sems.at[a * 7 + k], recv_sem=recv_sems.at[a * 7 + k],
                    device_id=peer, device_id_type=MESH).start()
        token[...] = jnp.zeros_like(token)

    hbm = lambda a: pltpu.HBM(a.shape, a.dtype)
    args = [pltpu.with_memory_space_constraint(a, pltpu.HBM) for a in list(srcs) + lands] + extra
    outs = pl.pallas_call(
        body, name=name,
        out_shape=(pltpu.SemaphoreType.DMA((7 * n,)), pltpu.SemaphoreType.DMA((7 * n,)),
                   *[hbm(a) for a in srcs], *[hbm(a) for a in lands], _sds((8, LANES), F32)),
        in_specs=[_HBM] * (2 * n) + [pl.BlockSpec(memory_space=pl.ANY)] * len(extra),
        out_specs=(_SEM, _SEM, *([_HBM] * (2 * n)), pl.BlockSpec(memory_space=pltpu.VMEM)),
        input_output_aliases={a: 2 + a for a in range(2 * n)},
        compiler_params=pltpu.CompilerParams(has_side_effects=_EFFECT))(*args)
    return outs[0], outs[1], list(outs[2:2 + n]), list(outs[2 + n:2 + 2 * n]), outs[-1]


def _scatter_wait(send_sems, recv_sems, srcs, lands, slot_of, after, *, name):
    n = len(srcs)

    def body(*refs):
        src_refs, land_refs = refs[:n], refs[n:2 * n]
        ssem, rsem = refs[2 * n], refs[2 * n + 1]
        x, y, c = _my_pos()
        me = 4 * x + 2 * y + c
        for a in range(n):
            for k, (peer, peer_idx) in enumerate(_peers()):
                cp = pltpu.make_async_remote_copy(
                    src_ref=slot_of(src_refs[a], peer_idx), dst_ref=land_refs[a].at[me],
                    send_sem=ssem.at[a * 7 + k], recv_sem=rsem.at[a * 7 + k],
                    device_id=peer, device_id_type=MESH)
                cp.wait_send()
                cp.wait_recv()

    hbm = lambda a: pltpu.HBM(a.shape, a.dtype)
    outs = pl.pallas_call(
        body, name=name, out_shape=tuple(hbm(a) for a in list(srcs) + list(lands)),
        in_specs=[_HBM] * (2 * n) + [_SEM, _SEM, pl.BlockSpec(memory_space=pl.ANY)],
        out_specs=tuple([_HBM] * (2 * n)), input_output_aliases={a: a for a in range(2 * n)},
        compiler_params=pltpu.CompilerParams(has_side_effects=_EFFECT))(*srcs, *lands, send_sems, recv_sems, after)
    return list(outs[:n]), list(outs[n:])


def _whole(a, peer_idx, shape_only=False):
    return a.shape if shape_only else a


def _slot(a, peer_idx, shape_only=False):
    return a.shape[1:] if shape_only else a.at[peer_idx]


def _all_gather(shard, *, name):
    rows = shard.shape[0]

    def body(x_ref, out_ref, send_sems, recv_sems, local_sem):
        x, y, c = _my_pos()
        me, sibling = (x, y, c), (x, y, 1 - c)
        chips = [(1 - x, y), (x, 1 - y), (1 - x, 1 - y)]

        def slot(px, py, pc):
            return out_ref.at[4 * px + 2 * py + pc]

        def copy(k, block, to, src=None):
            return pltpu.make_async_remote_copy(
                src_ref=slot(*block) if src is None else src, dst_ref=slot(*block),
                send_sem=send_sems.at[k], recv_sem=recv_sems.at[k], device_id=to, device_id_type=MESH)

        mine = pltpu.make_async_copy(x_ref, slot(*me), local_sem)
        mine.start()
        first = [copy(0, me, sibling, src=x_ref)]
        first += [copy(1 + j, me, (*chip, c), src=x_ref) for j, chip in enumerate(chips)]
        for cp in first:
            cp.start()
        passed = [copy(4 + j, (*chip, c), sibling) for j, chip in enumerate(chips)]
        for j, chip in enumerate(chips):
            copy(1 + j, (*chip, c), me).wait_recv()
            passed[j].start()
        copy(0, sibling, me).wait_recv()
        for j, chip in enumerate(chips):
            copy(4 + j, (*chip, 1 - c), me).wait_recv()
        for cp in first + passed:
            cp.wait_send()
        mine.wait()

    return _pc(body, name=name, out_shape=_sds((N_DEV, rows, LANES), shard.dtype),
               in_specs=[pl.BlockSpec(memory_space=pl.ANY)], out_specs=pl.BlockSpec(memory_space=pl.ANY),
               scratch_shapes=[pltpu.SemaphoreType.DMA((7,)), pltpu.SemaphoreType.DMA((7,)),
                               pltpu.SemaphoreType.DMA])(shard)


def _exchange(big, small, *, name):
    rs = small.shape[0]

    def body(big_ref, small_ref, bout_ref, sout_ref, send_sems, recv_sems, local_sems):
        x, y, c = _my_pos()
        me = 4 * x + 2 * y + c
        lb = pltpu.make_async_copy(big_ref.at[me], bout_ref.at[me], local_sems.at[0])
        ls = pltpu.make_async_copy(small_ref, sout_ref.at[me], local_sems.at[1])
        lb.start()
        ls.start()
        copies = []
        for k in range(1, N_DEV):
            px = x ^ ((k >> 2) & 1)
            py = y ^ ((k >> 1) & 1)
            pc = c ^ (k & 1)
            peer = 4 * px + 2 * py + pc
            copies.append(pltpu.make_async_remote_copy(
                src_ref=big_ref.at[peer], dst_ref=bout_ref.at[me], send_sem=send_sems.at[k - 1],
                recv_sem=recv_sems.at[k - 1], device_id=(px, py, pc), device_id_type=MESH))
            copies.append(pltpu.make_async_remote_copy(
                src_ref=small_ref, dst_ref=sout_ref.at[me], send_sem=send_sems.at[7 + k - 1],
                recv_sem=recv_sems.at[7 + k - 1], device_id=(px, py, pc), device_id_type=MESH))
        for cp in copies:
            cp.start()
        for cp in copies:
            cp.wait()
        lb.wait()
        ls.wait()

    return _pc(body, name=name,
               out_shape=[_sds(big.shape, big.dtype), _sds((N_DEV, rs, LANES), F32)],
               in_specs=[pl.BlockSpec(memory_space=pl.ANY), pl.BlockSpec(memory_space=pl.ANY)],
               out_specs=[pl.BlockSpec(memory_space=pl.ANY), pl.BlockSpec(memory_space=pl.ANY)],
               scratch_shapes=[pltpu.SemaphoreType.DMA((14,)), pltpu.SemaphoreType.DMA((14,)),
                               pltpu.SemaphoreType.DMA((2,))])(big, small)


def _all_gather_small(small, *, name):
    rs = small.shape[0]

    def body(small_ref, out_ref, send_sems, recv_sems, local_sem):
        x, y, c = _my_pos()
        me = 4 * x + 2 * y + c
        mine = pltpu.make_async_copy(small_ref, out_ref.at[me], local_sem)
        mine.start()
        copies = [pltpu.make_async_remote_copy(
            src_ref=small_ref, dst_ref=out_ref.at[me], send_sem=send_sems.at[k], recv_sem=recv_sems.at[k],
            device_id=peer, device_id_type=MESH) for k, (peer, _) in enumerate(_peers())]
        for cp in copies:
            cp.start()
        for cp in copies:
            cp.wait()
        mine.wait()

    return _pc(body, name=name, out_shape=_sds((N_DEV, rs, LANES), F32),
               in_specs=[pl.BlockSpec(memory_space=pl.ANY)], out_specs=pl.BlockSpec(memory_space=pl.ANY),
               scratch_shapes=[pltpu.SemaphoreType.DMA((7,)), pltpu.SemaphoreType.DMA((7,)),
                               pltpu.SemaphoreType.DMA])(small)


def _sum_slots(a, *, name, tr=None):
    rows, cols = a.shape[1], a.shape[2]
    if tr is None:
        tr = rows
        while N_DEV * tr * cols * a.dtype.itemsize > 3 * 2 ** 20 and tr % 32 == 0:
            tr //= 2

    def body(a_ref, o_ref):
        acc = a_ref[0].astype(F32)
        for j in range(1, N_DEV):
            acc = acc + a_ref[j].astype(F32)
        o_ref[...] = acc

    return _pc(body, name=name, out_shape=_sds((rows, cols), F32), grid=(rows // tr,),
               in_specs=[pl.BlockSpec((N_DEV, tr, cols), lambda i: (0, i, 0))],
               out_specs=pl.BlockSpec((tr, cols), lambda i: (i, 0)), semantics=("arbitrary",), vmem=VMEM_BIG)(a)


PACK_ROWS = 25600
SUM_TILE = 512


def _rows128(a):
    return a.reshape(-1, LANES)


def _to_slots(full, kind):
    if kind == "rows2":
        r, c = full.shape
        return full.reshape(N_DEV, r // N_DEV, c)
    if kind == "cols2":
        r, c = full.shape
        return full.reshape(r, N_DEV, c // N_DEV).transpose(1, 0, 2)
    if kind == "rows3":
        l, r, c = full.shape
        return full.reshape(l, N_DEV, r // N_DEV, c).transpose(1, 0, 2, 3)
    if kind == "cols3":
        l, r, c = full.shape
        return full.reshape(l, r, N_DEV, c // N_DEV).transpose(2, 0, 1, 3)
    raise ValueError(kind)


def _from_slots(slots, kind):
    if kind == "rows2":
        _, r, c = slots.shape
        return slots.reshape(N_DEV * r, c)
    if kind == "cols2":
        _, r, c = slots.shape
        return slots.transpose(1, 0, 2).reshape(r, N_DEV * c)
    if kind == "rows3":
        _, l, r, c = slots.shape
        return slots.transpose(1, 0, 2, 3).reshape(l, N_DEV * r, c)
    if kind == "cols3":
        _, l, r, c = slots.shape
        return slots.transpose(1, 2, 0, 3).reshape(l, r, N_DEV * c)
    raise ValueError(kind)


BIG = (("w_in_a", "rows2"), ("w_in_b", "rows2"), ("w_kv", "cols2"), ("w_memkv", "rows3"),
       ("w_out", "rows3"), ("w_up", "cols3"), ("w_down", "rows3"))


def _round_up(n, m):
    return -(-n // m) * m


def _pad_rows(a, rows, axis):
    pad = [(0, 0)] * a.ndim
    pad[axis] = (0, rows - a.shape[axis])
    return jnp.pad(a, pad)


def kernel(x, mem, ln_mix_g, w_in_a, b_f_a, w_in_b, ln_kv_g, w_kv, ln_mem_g, w_memkv, w_out, ln_ffn_g, w_up, conv_w, conv_b, w_down, final_g, loss_target, m_ln_mix_g, m_w_in_a, m_b_f_a, m_w_in_b, m_ln_kv_g, m_w_kv, m_ln_mem_g, m_w_memkv, m_w_out, m_ln_ffn_g, m_w_up, m_conv_w, m_conv_b, m_w_down, m_final_g, v_ln_mix_g, v_w_in_a, v_b_f_a, v_w_in_b, v_ln_kv_g, v_w_kv, v_ln_mem_g, v_w_memkv, v_w_out, v_ln_ffn_g, v_w_up, v_conv_w, v_conv_b, v_w_down, v_final_g):
    B, S, D = x.shape
    NM = mem.shape[1]
    T = B * S
    F = w_down.shape[1] * N_DEV
    my_idx = 4 * lax.axis_index("x") + 2 * lax.axis_index("y") + lax.axis_index("c")

    shards = {"w_in_a": w_in_a[0], "w_in_b": w_in_b[0], "w_kv": w_kv, "w_memkv": w_memkv, "w_out": w_out,
              "w_up": w_up, "w_down": w_down}
    moms = {"w_in_a": (m_w_in_a[0], v_w_in_a[0]), "w_in_b": (m_w_in_b[0], v_w_in_b[0]), "w_kv": (m_w_kv, v_w_kv),
            "w_memkv": (m_w_memkv, v_w_memkv), "w_out": (m_w_out, v_w_out), "w_up": (m_w_up, v_w_up),
            "w_down": (m_w_down, v_w_down)}

    groups = [("a1", [("w_in_a", None)]),
              ("a2", [("w_in_b", None), ("w_kv", None), ("w_memkv", None), ("w_out", None), ("conv_w", None)]),
              ("b0", [("w_up", 0), ("w_down", 0)]), ("b1", [("w_up", 1), ("w_down", 1)])]
    sources = dict(shards, conv_w=conv_w)
    started, token = {}, None
    for gname, members in groups:
        srcs = []
        for n, layer in members:
            a = sources[n] if layer is None else sources[n][layer]
            srcs.append(a if n == "conv_w" else a.astype(BF16))
        ssem, rsem, thru, lands, token = _scatter_start(srcs, _whole, name=f"gather_start_{gname}", order_after=token)
        started[gname] = (ssem, rsem, thru, lands)

    def gathered(gname, after):
        ssem, rsem, thru, lands = started[gname]
        thru, lands = _scatter_wait(ssem, rsem, thru, lands, _whole, after, name=f"gather_wait_{gname}")
        return [lax.dynamic_update_index_in_dim(land, s, my_idx, 0) for land, s in zip(lands, thru)]

    full = {}
    (g_wa,) = gathered("a1", token)
    full["w_in_a"] = _from_slots(g_wa, "rows2")

    wa = full["w_in_a"]
    n_qkv = 3 * MAIN_W
    wa = jnp.concatenate([wa[:, :n_qkv], wa[:, n_qkv + N_MAIN_HEADS:], wa[:, n_qkv:n_qkv + N_MAIN_HEADS],
                          jnp.zeros((D, LANES - N_MAIN_HEADS), BF16)], axis=1)
    n_main = n_qkv + MEM_W
    full["w_up"], full["w_down"] = {}, {}
    b_f =_pad_rows(b_f_a.reshape(1, N_MAIN_HEADS), LANES, 1)

    x2d = x.reshape(T, D)
    mem2d = mem.reshape(B * NM, D)
    tgt2d = loss_target.reshape(T, D)
    PM, PX = N_MAIN_HEADS // 2, N_MEM_HEADS // 2

    def stats_to_heads(c2d):
        c = c2d.reshape(B, S, LANES)[:, :, :N_MAIN_HEADS].transpose(0, 2, 1)
        return c[:, :, None, :]

    def mem_kv(layer):
        return _mm_fwd(mem2d, full["w_memkv"][layer], name=f"memkv{layer}", tm=B * NM, tn=2 * MEM_W,
                       out_dtype=BF16, g=ln_mem_g[layer], save_h=True)

    def conv_ffn_fwd(xin, layer):
        u, h = _mm_fwd(xin, full["w_up"][layer], name=f"ffn_up{layer}", tm=min(1024, T), tn=512, out_dtype=F32,
                       g=ln_ffn_g[layer], save_h=True)
        a = _conv_gate_fwd(u, conv_w_full[layer], conv_b[layer].reshape(1, 2 * F), name=f"conv_gate{layer}", B=B, S=S)
        xo = _mm_fwd(a, full["w_down"][layer], name=f"ffn_down{layer}", tm=min(512, T), tn=512, out_dtype=F32, res=xin)
        return xo, (u, h, a)

    proj_a, h_mix0 = _mm_fwd(x2d, wa, name="in_proj_a", tm=min(1024, T), tn=512, out_dtype=BF16, g=ln_mix_g[0],
                             ncols=n_main, save_h=True)
    f_logit = _mm_fwd(x2d, wa, name="in_proj_f", tm=min(1024, T), tn=LANES, out_dtype=F32, g=ln_mix_g[0],
                      col0=n_main // LANES, ncols=LANES)
    c2d = _forget_cumsum(f_logit, b_f, B=B, S=S, name="forget_cumsum")
    cr = stats_to_heads(c2d)
    o_main0, lse0 = _fox_fwd_g(proj_a, proj_a, proj_a, cr, name="fox_fwd", B=B, S=S, P=PM, q_cb=0, k_cb=PM, v_cb=2 * PM,
                               G=1)
    g_wb, g_wkv, g_wmem, g_wout, g_cw = gathered("a2", lse0)
    wb = _from_slots(g_wb, "rows2")
    wkv = _from_slots(g_wkv, "cols2")
    full["w_memkv"] = _from_slots(g_wmem, "rows3")
    full["w_out"] = _from_slots(g_wout, "rows3")
    conv_w_full = _from_slots(g_cw, "cols3")
    memkv0, h_mem0 = mem_kv(0)
    o_mem0, lse_m0 = _softmax_fwd(proj_a, memkv0, memkv0, name="mem_fwd0", B=B, S=S, Sk=NM, P=PX, q_cb=3 * PM,
                                  k_cb=0, v_cb=PX, causal=False)
    o_cat0 = jnp.concatenate([o_main0, o_mem0], axis=1)
    x1 = _mm_fwd(o_cat0, full["w_out"][0], name="out_proj0", tm=min(512, T), tn=512, out_dtype=F32, res=x2d)
    g_up, g_dn = gathered("b0", x1)
    full["w_up"][0], full["w_down"][0] = _from_slots(g_up, "cols2"), _from_slots(g_dn, "rows2")
    x2, (u0, h_ffn0, a0) = conv_ffn_fwd(x1, 0)
    kv, h_kv = _mm_fwd(x2, wkv, name="kv_proj", tm=min(1024, T), tn=512, out_dtype=BF16, g=ln_kv_g, save_h=True)
    proj_b, h_mix1 = _mm_fwd(x2, wb, name="in_proj_b", tm=min(1024, T), tn=512, out_dtype=BF16, g=ln_mix_g[1],
                             save_h=True)
    o_main1, rt1 = _sb_fwd_g(proj_b, kv, kv, name="sb_fwd", B=B, S=S, P=PM, q_cb=0, k_cb=0, v_cb=PM)
    memkv1, h_mem1 = mem_kv(1)
    o_mem1, lse_m1 = _softmax_fwd(proj_b, memkv1, memkv1, name="mem_fwd1", B=B, S=S, Sk=NM, P=PX, q_cb=PM,
                                  k_cb=0, v_cb=PX, causal=False)
    o_cat1 = jnp.concatenate([o_main1, o_mem1], axis=1)
    x3 = _mm_fwd(o_cat1, full["w_out"][1], name="out_proj1", tm=min(512, T), tn=512, out_dtype=F32, res=x2)
    g_up, g_dn = gathered("b1", x3)
    full["w_up"][1], full["w_down"][1] = _from_slots(g_up, "cols2"), _from_slots(g_dn, "rows2")
    x4, (u1, h_ffn1, a1) = conv_ffn_fwd(x3, 1)
    dx4, dg_final, loss_part = _loss_head(x4, final_g, tgt2d, name="loss_head")

    grads = {}
    small = {}
    reduce_groups = []

    def start_reduce(gname, keys, kinds):
        slots = [_to_slots(grads[k], kind) for k, kind in zip(keys, kinds)]
        ssem, rsem, thru, lands, tok = _scatter_start(slots, _slot, name=f"reduce_start_{gname}")
        reduce_groups.append((gname, keys, ssem, rsem, thru, lands))
        return tok[0, 0]

    def conv_ffn_bwd(dxo, xin, u, h, a, layer):
        w_dn = full["w_down"][layer]
        da = _mm_nt(dxo, w_dn, name=f"d_act{layer}", tm=min(512, T), tn=F // 2, out_dtype=F32)
        grads[("w_down", layer)] = _wgrad(a, dxo, f"g_w_down{layer}")
        cwl = conv_w_full[layer]
        d_g, d_v, p_g, p_v = _conv_gate_bwd(da, u, cwl, conv_b[layer].reshape(1, 2 * F), name=f"conv_bwd{layer}", B=B, S=S)
        small[("conv_w", layer)] = jnp.concatenate([p_g[0:3], p_v[0:3]], axis=1)
        small[("conv_b", layer)] = jnp.concatenate([p_g[3], p_v[3]], axis=0)
        du_g = _conv_transpose(d_g, cwl, name=f"conv_t_gate{layer}", B=B, S=S, col_off=0)
        du_v = _conv_transpose(d_v, cwl, name=f"conv_t_val{layer}", B=B, S=S, col_off=1)
        grads[("w_up", layer)] = jnp.concatenate(
            [_wgrad(h, du_g, f"g_w_up_gate{layer}"), _wgrad(h, du_v, f"g_w_up_val{layer}")], axis=1)
        tok = start_reduce(f"ffn{layer}", [("w_down", layer), ("w_up", layer)], ["rows2", "cols2"])
        dxi, dg = _mm_nt_rmsbwd([(du_g, 0), (du_v, 1)], full["w_up"][layer], xin, ln_ffn_g[layer] + tok,
                                name=f"d_ffn_in{layer}", dres=dxo)
        small[("ln_ffn_g", layer)] = dg[0]
        return dxi

    def mem_bwd(proj, q_cb, memkv, h_mem, do_cat, o_mem, lse_m, layer):
        dqm, dmk, dmv = _softmax_bwd(proj, memkv, memkv, do_cat, o_mem, lse_m, name=f"mem_bwd{layer}", B=B, S=S,
                                     Sk=NM, P=PX, q_cb=q_cb, k_cb=0, v_cb=PX, do_cb=PM, causal=False)
        grads[("w_memkv", layer)] = jnp.concatenate(
            [_wgrad(h_mem, dmk, f"g_w_memk{layer}"), _wgrad(h_mem, dmv, f"g_w_memv{layer}")], axis=1)
        _, dg = _mm_nt_rmsbwd([(dmk, 0), (dmv, 1)], full["w_memkv"][layer], mem2d, ln_mem_g[layer],
                              name=f"d_mem_in{layer}", want_dx=False)
        small[("ln_mem_g", layer)] = dg[0]
        return dqm

    dx3 = conv_ffn_bwd(dx4, x3, u1, h_ffn1, a1, 1)
    do_cat1 = _mm_nt(dx3, full["w_out"][1], name="d_o_cat1", tm=min(512, T), tn=512, out_dtype=BF16)
    grads[("w_out", 1)] = _wgrad(o_cat1, dx3, "g_w_out1")
    dq1, dk1, dv1 = _sb_bwd_g(proj_b, kv, kv, do_cat1, rt1, name="sb_bwd", B=B, S=S, P=PM, q_cb=0, k_cb=0, v_cb=PM,
                            do_cb=0)
    dqm1 = mem_bwd(proj_b, PM, memkv1, h_mem1, do_cat1, o_mem1, lse_m1, 1)
    grads["w_in_b"] = jnp.concatenate([_wgrad(h_mix1, dq1, "g_w_in_b_q"), _wgrad(h_mix1, dqm1, "g_w_in_b_m")], axis=1)
    grads["w_kv"] = jnp.concatenate([_wgrad(h_kv, dk1, "g_w_kv_k"), _wgrad(h_kv, dv1, "g_w_kv_v")], axis=1)
    tok = start_reduce("mix1", [("w_out", 1), "w_in_b", "w_kv", ("w_memkv", 1)], ["rows2", "rows2", "cols2", "rows2"])
    dx2, dg = _mm_nt_rmsbwd([(dq1, 0), (dqm1, MAIN_W // MEM_W)], wb, x2, ln_mix_g[1] + tok, name="d_mix_in1", dres=dx3)
    small[("ln_mix_g", 1)] = dg[0]
    dx2, dg = _mm_nt_rmsbwd([(dk1, 0), (dv1, 1)], wkv, x2, ln_kv_g, name="d_kv_in", dres=dx2)
    small["ln_kv_g"] = dg[0]
    dx1 = conv_ffn_bwd(dx2, x1, u0, h_ffn0, a0, 0)
    do_cat0 = _mm_nt(dx1, full["w_out"][0], name="d_o_cat0", tm=min(512, T), tn=512, out_dtype=BF16)
    grads[("w_out", 0)] = _wgrad(o_cat0, dx1, "g_w_out0")
    dq0, dk0, dv0, dcs = _fox_bwd_g(proj_a, proj_a, proj_a, do_cat0, lse0, cr, name="fox_bwd", B=B, S=S, P=PM, q_cb=0,
                                  k_cb=PM, v_cb=2 * PM, do_cb=0)
    dqm0 = mem_bwd(proj_a, 3 * PM, memkv0, h_mem0, do_cat0, o_mem0, lse_m0, 0)
    dc2d = _pad_rows(dcs[:, :, 0, :].transpose(0, 2, 1).reshape(T, N_MAIN_HEADS), LANES, 1)
    df, db_f = _forget_cumsum_bwd(dc2d, f_logit, b_f, B=B, S=S, name="forget_cumsum_bwd")
    a_parts = [(dq0, 0), (dk0, 1), (dv0, 2), (dqm0, n_qkv // MEM_W), (df, n_main // LANES)]
    g_wa = jnp.concatenate([_wgrad(h_mix0, p, f"g_w_in_a{k}") for k, (p, _) in enumerate(a_parts)], axis=1)
    grads["w_in_a"] = jnp.concatenate([g_wa[:, :n_qkv], g_wa[:, n_main:n_main + N_MAIN_HEADS], g_wa[:, n_qkv:n_main]],
                                      axis=1)
    tok = start_reduce("mix0", [("w_out", 0), ("w_memkv", 0), "w_in_a"], ["rows2", "rows2", "rows2"])
    dx0, dg = _mm_nt_rmsbwd(a_parts, wa, x2d, ln_mix_g[0] + tok, name="d_mix_in0", dres=dx1)
    small[("ln_mix_g", 0)] = dg[0]
    grad_x = dx0.reshape(B, S, D)

    pieces = {}
    for gname, keys, ssem, rsem, thru, lands in reduce_groups:
        thru, lands = _scatter_wait(ssem, rsem, thru, lands, _slot, dx0, name=f"reduce_wait_{gname}")
        for key, mine, land in zip(keys, thru, lands):
            own = lax.dynamic_index_in_dim(mine, my_idx, 0, keepdims=False)
            land = lax.dynamic_update_index_in_dim(land, own, my_idx, 0)
            tag = key if isinstance(key, str) else f"{key[0]}{key[1]}"
            pieces[key] = _sum_slots(land, name=f"sum_{tag}")

    def both_small(name):
        return jnp.stack([small[(name, 0)], small[(name, 1)]])

    small_list = [("ln_mix_g", both_small("ln_mix_g")), ("b_f_a", db_f[:, :N_MAIN_HEADS]), ("ln_kv_g", small["ln_kv_g"]),
                  ("ln_mem_g", both_small("ln_mem_g")), ("ln_ffn_g", both_small("ln_ffn_g")),
                  ("conv_w", both_small("conv_w")), ("conv_b", both_small("conv_b")), ("final_g", dg_final[0]),
                  ("loss", loss_part[0, :1])]
    sm_rows = []
    for _, a in small_list:
        flat = a.reshape(-1)
        sm_rows.append(_pad_rows(flat, _round_up(flat.size, 8 * LANES), 0).reshape(-1, LANES))
    spack = jnp.concatenate(sm_rows, axis=0)
    ssum = _sum_slots(_all_gather_small(spack, name="gather_small_grads"), name="sum_small")

    red = {}
    for n in ("w_in_a", "w_in_b", "w_kv"):
        red[n] = pieces[n].reshape(shards[n].shape)
    for n in ("w_memkv", "w_out", "w_up", "w_down"):
        red[n] = jnp.stack([pieces[(n, 0)], pieces[(n, 1)]])
    off = 0
    for (n, a), rows in zip(small_list, sm_rows):
        red[n] = ssum[off:off + rows.shape[0]].reshape(-1)[:a.size].reshape(a.shape)
        off += rows.shape[0]
    loss = red["loss"][0]
    shard_cols = conv_w.shape[2]
    red["conv_w"] = lax.dynamic_slice_in_dim(red["conv_w"], my_idx * shard_cols, shard_cols, axis=2)
    red["b_f_a"] = red["b_f_a"].reshape(b_f_a.shape)

    weights = {"ln_mix_g": ln_mix_g, "w_in_a": w_in_a, "b_f_a": b_f_a, "w_in_b": w_in_b, "ln_kv_g": ln_kv_g,
               "w_kv": w_kv, "ln_mem_g": ln_mem_g, "w_memkv": w_memkv, "w_out": w_out, "ln_ffn_g": ln_ffn_g,
               "w_up": w_up, "conv_w": conv_w, "conv_b": conv_b, "w_down": w_down, "final_g": final_g}
    m_in = {"ln_mix_g": m_ln_mix_g, "w_in_a": m_w_in_a, "b_f_a": m_b_f_a, "w_in_b": m_w_in_b, "ln_kv_g": m_ln_kv_g,
            "w_kv": m_w_kv, "ln_mem_g": m_ln_mem_g, "w_memkv": m_w_memkv, "w_out": m_w_out, "ln_ffn_g": m_ln_ffn_g,
            "w_up": m_w_up, "conv_w": m_conv_w, "conv_b": m_conv_b, "w_down": m_w_down, "final_g": m_final_g}
    v_in = {"ln_mix_g": v_ln_mix_g, "w_in_a": v_w_in_a, "b_f_a": v_b_f_a, "w_in_b": v_w_in_b, "ln_kv_g": v_ln_kv_g,
            "w_kv": v_w_kv, "ln_mem_g": v_ln_mem_g, "w_memkv": v_w_memkv, "w_out": v_w_out, "ln_ffn_g": v_ln_ffn_g,
            "w_up": v_w_up, "conv_w": v_conv_w, "conv_b": v_conv_b, "w_down": v_w_down, "final_g": v_final_g}
    order = list(weights)
    big_names = [n for n, _ in BIG]
    g_out, d_out, nm_out, nv_out = {}, {}, {}, {}
    for n in big_names + ["conv_w"]:
        w = weights[n]
        cols = w.shape[-1]
        g = red[n].reshape(w.shape)
        d, nm, nv = _adamw(w.reshape(-1, cols), g.reshape(-1, cols), m_in[n].reshape(-1, cols),
                           v_in[n].reshape(-1, cols), name=f"adamw_{n}")
        g_out[n], d_out[n], nm_out[n], nv_out[n] = g, d.reshape(w.shape), nm.reshape(w.shape), nv.reshape(w.shape)
    small_names = [n for n in order if n not in g_out]

    def pack_small(src):
        rows = []
        for n in small_names:
            flat = src[n].reshape(-1)
            rows.append(_pad_rows(flat, _round_up(flat.size, 8 * LANES), 0).reshape(-1, LANES))
        return jnp.concatenate(rows, axis=0), [r.shape[0] for r in rows]

    red_small = {n: red[n].reshape(weights[n].shape) for n in small_names}
    wp, counts = pack_small(weights)
    gp, _ = pack_small(red_small)
    mp, _ = pack_small(m_in)
    vp, _ = pack_small(v_in)
    dp, nmp, nvp = _adamw(wp, gp, mp, vp, name="adamw_small")
    off = 0
    for n, cnt in zip(small_names, counts):
        shp = weights[n].shape
        size = weights[n].size
        g_out[n] = red_small[n]
        d_out[n] = dp[off:off + cnt].reshape(-1)[:size].reshape(shp)
        nm_out[n] = nmp[off:off + cnt].reshape(-1)[:size].reshape(shp)
        nv_out[n] = nvp[off:off + cnt].reshape(-1)[:size].reshape(shp)
        off += cnt

    return (loss, grad_x, *[g_out[n] for n in order], *[d_out[n] for n in order],
            *[nm_out[n] for n in order], *[nv_out[n] for n in order])
```

```python
import functools

import jax
import jax.numpy as jnp
from jax import lax
from jax.experimental import pallas as pl
from jax.experimental.pallas import tpu as pltpu

F32 = jnp.float32
BF16 = jnp.bfloat16
LANES = 128
HEAD_DIM = 64
N_MAIN_HEADS = 12
N_MEM_HEADS = 4
MAIN_W = N_MAIN_HEADS * HEAD_DIM
MEM_W = N_MEM_HEADS * HEAD_DIM
SCALE = HEAD_DIM ** -0.5
EPS = 1e-6
NEG = -1e30
N_DEV = 8
ATT_TILE = 256
VMEM_BIG = 56 * 2 ** 20
MESH = pl.DeviceIdType.MESH

ADAM_LR = 0.001
ADAM_B1 = 0.9
ADAM_B2 = 0.999
ADAM_EPS = 1e-08
ADAM_WD = 0.01
ADAM_STEP = 10

NT = (((1,), (1,)), ((), ()))
TN = (((0,), (0,)), ((), ()))


def _pc(body, *, name, out_shape, grid=None, in_specs=None, out_specs=None, scratch_shapes=(),
        semantics=None, vmem=None):
    kw = {}
    if grid is not None:
        kw["grid"] = grid
    params = pltpu.CompilerParams(dimension_semantics=semantics, vmem_limit_bytes=vmem)
    return pl.pallas_call(body, name=name, out_shape=out_shape, in_specs=in_specs, out_specs=out_specs,
                          scratch_shapes=list(scratch_shapes), compiler_params=params, **kw)


def _sds(shape, dtype):
    return jax.ShapeDtypeStruct(shape, dtype)


def _mm_fwd(a, w, *, name, tm, tn, out_dtype, g=None, res=None, col0=0, ncols=None, save_h=False):
    m_rows, k = a.shape
    n = w.shape[1] if ncols is None else ncols
    grid = (m_rows // tm, n // tn)
    norm = g is not None

    def body(*refs):
        refs = list(refs)
        a_ref = refs.pop(0)
        g_ref = refs.pop(0) if norm else None
        w_ref = refs.pop(0)
        res_ref = refs.pop(0) if res is not None else None
        o_ref = refs.pop(0)
        hout_ref = refs.pop(0) if save_h else None
        h_ref = refs.pop(0) if norm else None
        if norm:
            @pl.when(pl.program_id(1) == 0)
            def _():
                xv = a_ref[...]
                r = lax.rsqrt(jnp.mean(xv * xv, axis=-1, keepdims=True) + EPS)
                h = ((xv * r) * g_ref[...]).astype(BF16)
                h_ref[...] = h
                if save_h:
                    hout_ref[...] = h
            lhs = h_ref[...]
        else:
            lhs = a_ref[...].astype(BF16)
        acc = jnp.dot(lhs, w_ref[...], preferred_element_type=F32)
        if res is not None:
            acc = acc + res_ref[...]
        o_ref[...] = acc.astype(out_dtype)

    in_specs = [pl.BlockSpec((tm, k), lambda i, j: (i, 0))]
    args = [a]
    if norm:
        in_specs.append(pl.BlockSpec((1, k), lambda i, j: (0, 0)))
        args.append(g.reshape(1, k))
    in_specs.append(pl.BlockSpec((k, tn), lambda i, j: (0, j + col0)))
    args.append(w)
    if res is not None:
        in_specs.append(pl.BlockSpec((tm, tn), lambda i, j: (i, j)))
        args.append(res)
    out_shape = [_sds((m_rows, n), out_dtype)]
    out_specs = [pl.BlockSpec((tm, tn), lambda i, j: (i, j))]
    if save_h:
        out_shape.append(_sds((m_rows, k), BF16))
        out_specs.append(pl.BlockSpec((tm, k), lambda i, j: (i, 0)))
    scratch = [pltpu.VMEM((tm, k), BF16)] if norm else []
    outs = _pc(body, name=name, out_shape=out_shape, grid=grid, in_specs=in_specs, out_specs=out_specs,
               scratch_shapes=scratch, semantics=("arbitrary", "arbitrary"), vmem=VMEM_BIG)(*args)
    return outs if save_h else outs[0]


def _mm_nt(a, w, *, name, tm, tn, out_dtype):
    m_rows, k = a.shape
    n = w.shape[0]

    def body(a_ref, w_ref, o_ref):
        acc = lax.dot_general(a_ref[...].astype(BF16), w_ref[...], NT, preferred_element_type=F32)
        o_ref[...] = acc.astype(out_dtype)

    return _pc(body, name=name, out_shape=_sds((m_rows, n), out_dtype), grid=(m_rows // tm, n // tn),
               in_specs=[pl.BlockSpec((tm, k), lambda i, j: (i, 0)), pl.BlockSpec((tn, k), lambda i, j: (j, 0))],
               out_specs=pl.BlockSpec((tm, tn), lambda i, j: (i, j)),
               semantics=("arbitrary", "arbitrary"), vmem=VMEM_BIG)(a, w)


def _mm_tn(a, b, *, name, ta, tn, tt):
    t_rows, ka = a.shape
    n = b.shape[1]
    nt = t_rows // tt

    def body(a_ref, b_ref, o_ref, acc_ref):
        t = pl.program_id(2)

        @pl.when(t == 0)
        def _():
            acc_ref[...] = jnp.zeros_like(acc_ref)

        acc_ref[...] += lax.dot_general(a_ref[...].astype(BF16), b_ref[...].astype(BF16), TN,
                                        preferred_element_type=F32)

        @pl.when(t == nt - 1)
        def _():
            o_ref[...] = acc_ref[...].astype(BF16)

    return _pc(body, name=name, out_shape=_sds((ka, n), BF16), grid=(ka // ta, n // tn, nt),
               in_specs=[pl.BlockSpec((tt, ta), lambda i, j, t: (t, i)),
                         pl.BlockSpec((tt, tn), lambda i, j, t: (t, j))],
               out_specs=pl.BlockSpec((ta, tn), lambda i, j, t: (i, j)),
               scratch_shapes=[pltpu.VMEM((ta, tn), F32)],
               semantics=("arbitrary", "arbitrary", "arbitrary"), vmem=VMEM_BIG)(a, b)


def _wgrad(a, b, name):
    t_rows, ka = a.shape
    n = b.shape[1]
    ta = ka if ka <= 1024 else ka // 2
    tn = n
    while ta * tn * 4 > 6 * 2 ** 20 and tn % 256 == 0:
        tn //= 2
    tt = min(512, t_rows)
    return _mm_tn(a, b, name=name, ta=ta, tn=tn, tt=tt)


def _mm_nt_rmsbwd(parts, w, x, g, *, name, dres=None, want_dx=True):
    m_rows, d = x.shape
    tm = min(256, m_rows)
    n_parts = len(parts)

    def body(*refs):
        refs = list(refs)
        dy_refs = [refs.pop(0) for _ in range(n_parts)]
        w_refs = [refs.pop(0) for _ in range(n_parts)]
        x_ref = refs.pop(0)
        g_ref = refs.pop(0)
        dres_ref = refs.pop(0) if dres is not None else None
        dx_ref = refs.pop(0) if want_dx else None
        dg_ref = refs.pop(0)

        @pl.when(pl.program_id(0) == 0)
        def _():
            dg_ref[...] = jnp.zeros_like(dg_ref)

        dh = None
        for dy_ref, w_ref in zip(dy_refs, w_refs):
            t = lax.dot_general(dy_ref[...].astype(BF16), w_ref[...], NT, preferred_element_type=F32)
            dh = t if dh is None else dh + t
        xv = x_ref[...]
        r = lax.rsqrt(jnp.mean(xv * xv, axis=-1, keepdims=True) + EPS)
        xh = xv * r
        dg_ref[...] += jnp.sum(dh * xh, axis=0, keepdims=True)
        if want_dx:
            dhg = dh * g_ref[...]
            dx = r * (dhg - xh * jnp.mean(dhg * xh, axis=-1, keepdims=True))
            if dres is not None:
                dx = dx + dres_ref[...]
            dx_ref[...] = dx

    in_specs, args = [], []
    for dy, _ in parts:
        in_specs.append(pl.BlockSpec((tm, dy.shape[1]), lambda i: (i, 0)))
        args.append(dy)
    for dy, cb in parts:
        in_specs.append(pl.BlockSpec((d, dy.shape[1]), functools.partial(lambda i, cb: (0, cb), cb=cb)))
        args.append(w)
    in_specs += [pl.BlockSpec((tm, d), lambda i: (i, 0)), pl.BlockSpec((1, d), lambda i: (0, 0))]
    args += [x, g.reshape(1, d)]
    if dres is not None:
        in_specs.append(pl.BlockSpec((tm, d), lambda i: (i, 0)))
        args.append(dres)
    out_shape, out_specs = [], []
    if want_dx:
        out_shape.append(_sds((m_rows, d), F32))
        out_specs.append(pl.BlockSpec((tm, d), lambda i: (i, 0)))
    out_shape.append(_sds((1, d), F32))
    out_specs.append(pl.BlockSpec((1, d), lambda i: (0, 0)))
    outs = _pc(body, name=name, out_shape=out_shape, grid=(m_rows // tm,), in_specs=in_specs,
               out_specs=out_specs, semantics=("arbitrary",), vmem=VMEM_BIG)(*args)
    return (outs[0], outs[1]) if want_dx else (None, outs[0])


def _loss_head(x, g, tgt, *, name):
    m_rows, d = x.shape
    tm = min(256, m_rows)

    def body(x_ref, g_ref, t_ref, dx_ref, dg_ref, loss_ref):
        @pl.when(pl.program_id(0) == 0)
        def _():
            dg_ref[...] = jnp.zeros_like(dg_ref)
            loss_ref[...] = jnp.zeros_like(loss_ref)

        xv = x_ref[...]
        r = lax.rsqrt(jnp.mean(xv * xv, axis=-1, keepdims=True) + EPS)
        xh = xv * r
        gv = g_ref[...]
        err = xh * gv - t_ref[...]
        per_tok = jnp.mean(err * err, axis=-1, keepdims=True)
        loss_ref[...] += 0.5 * jnp.sum(per_tok, axis=0, keepdims=True)
        dout = err * (1.0 / d)
        dg_ref[...] += jnp.sum(dout * xh, axis=0, keepdims=True)
        dhg = dout * gv
        dx_ref[...] = r * (dhg - xh * jnp.mean(dhg * xh, axis=-1, keepdims=True))

    row = pl.BlockSpec((tm, d), lambda i: (i, 0))
    return _pc(body, name=name, out_shape=[_sds((m_rows, d), F32), _sds((1, d), F32), _sds((1, LANES), F32)],
               grid=(m_rows // tm,), in_specs=[row, pl.BlockSpec((1, d), lambda i: (0, 0)), row],
               out_specs=[row, pl.BlockSpec((1, d), lambda i: (0, 0)), pl.BlockSpec((1, LANES), lambda i: (0, 0))],
               semantics=("arbitrary",))(x, g.reshape(1, d), tgt)


def _split3(v):
    hi = v.astype(BF16)
    r1 = v - hi.astype(F32)
    mid = r1.astype(BF16)
    lo = (r1 - mid.astype(F32)).astype(BF16)
    return hi, mid, lo


def _split2(v):
    hi = v.astype(BF16)
    lo = (v - hi.astype(F32)).astype(BF16)
    return hi, lo


def _tri_dot3(tri, v):
    hi, mid, lo = _split3(v)
    return (jnp.dot(tri, hi, preferred_element_type=F32) + jnp.dot(tri, mid, preferred_element_type=F32)
            + jnp.dot(tri, lo, preferred_element_type=F32))


def _dot_tri2(v, tri):
    hi, lo = _split2(v)
    return jnp.dot(hi, tri, preferred_element_type=F32) + jnp.dot(lo, tri, preferred_element_type=F32)


def _log_sigmoid(v):
    return jnp.minimum(v, 0.0) - jnp.log(1.0 + jnp.exp(-jnp.abs(v)))


def _forget_cumsum(f_logit, b_f, *, B, S, name):
    ch = min(256, S)
    nch = S // ch

    def body(f_ref, b_ref, c_ref):
        r_i = lax.broadcasted_iota(jnp.int32, (ch, ch), 0)
        c_i = lax.broadcasted_iota(jnp.int32, (ch, ch), 1)
        tri = (c_i <= r_i).astype(BF16)
        bv = b_ref[...]

        def step(k, carry):
            rows = pl.ds(pl.multiple_of(k * ch, ch), ch)
            lf = _log_sigmoid(f_ref[rows, :] + bv)
            c_ref[rows, :] = _tri_dot3(tri, lf) + carry
            return carry + jnp.sum(lf, axis=0, keepdims=True)

        lax.fori_loop(0, nch, step, jnp.zeros((1, LANES), F32))

    blk = pl.BlockSpec((S, LANES), lambda b: (b, 0))
    return _pc(body, name=name, out_shape=_sds((B * S, LANES), F32), grid=(B,),
               in_specs=[blk, pl.BlockSpec((1, LANES), lambda b: (0, 0))], out_specs=blk,
               semantics=("arbitrary",))(f_logit, b_f)


def _forget_cumsum_bwd(dc, f_logit, b_f, *, B, S, name):
    ch = min(256, S)
    nch = S // ch

    def body(dc_ref, f_ref, b_ref, df_ref, db_ref):
        @pl.when(pl.program_id(0) == 0)
        def _():
            db_ref[...] = jnp.zeros_like(db_ref)

        r_i = lax.broadcasted_iota(jnp.int32, (ch, ch), 0)
        c_i = lax.broadcasted_iota(jnp.int32, (ch, ch), 1)
        tri = (c_i >= r_i).astype(BF16)
        bv = b_ref[...]

        def step(kk, carry):
            tail, dbs = carry
            k = nch - 1 - kk
            rows = pl.ds(pl.multiple_of(k * ch, ch), ch)
            dcv = dc_ref[rows, :]
            dlf = _tri_dot3(tri, dcv) + tail
            z = f_ref[rows, :] + bv
            df = dlf * (1.0 / (1.0 + jnp.exp(z)))
            df_ref[rows, :] = df.astype(BF16)
            return tail + jnp.sum(dcv, axis=0, keepdims=True), dbs + jnp.sum(df, axis=0, keepdims=True)

        zero = jnp.zeros((1, LANES), F32)
        _, dbs = lax.fori_loop(0, nch, step, (zero, zero))
        db_ref[...] += dbs

    blk = pl.BlockSpec((S, LANES), lambda b: (b, 0))
    one = pl.BlockSpec((1, LANES), lambda b: (0, 0))
    return _pc(body, name=name, out_shape=[_sds((B * S, LANES), BF16), _sds((1, LANES), F32)], grid=(B,),
               in_specs=[blk, blk, one], out_specs=[blk, one], semantics=("arbitrary",))(dc, f_logit, b_f)


def _head_mask(lane, hh):
    return (lane < HEAD_DIM) if hh == 0 else (lane >= HEAD_DIM)


def _col_spec(rows, nblk_rows, cb):
    return pl.BlockSpec((rows, LANES), lambda b, p, i: (b * nblk_rows + i, cb + p))


def _kv_spec(rows, cb):
    return pl.BlockSpec((rows, LANES), lambda b, p, i: (b, cb + p))


def _stat_col_spec(tq):
    return pl.BlockSpec((1, 2, tq, 1), lambda b, p, i: (b, p, i, 0))


def _stat_row_spec(S):
    return pl.BlockSpec((1, 2, 1, S), lambda b, p, i: (b, p, 0, 0))


def _softmax_fwd(qa, ka, va, *, name, B, S, Sk, P, q_cb, k_cb, v_cb, causal, cc=None, cr=None):
    tq = min(ATT_TILE, S)
    tk = min(ATT_TILE, Sk)
    nq, nk = S // tq, Sk // tk
    decay = cc is not None
    assert not causal or (tq == tk and S == Sk)

    def body(*refs):
        if decay:
            q_ref, k_ref, v_ref, cc_ref, cr_ref, o_ref, lse_ref = refs
        else:
            q_ref, k_ref, v_ref, o_ref, lse_ref = refs
        i = pl.program_id(2)
        q = q_ref[...]
        lane = lax.broadcasted_iota(jnp.int32, (tq, LANES), 1)
        row = lax.broadcasted_iota(jnp.int32, (tq, tk), 0) + i * tq
        col0 = lax.broadcasted_iota(jnp.int32, (tq, tk), 1)
        outs = []
        for hh in range(2):
            qh = jnp.where(_head_mask(lane, hh), q, jnp.zeros_like(q))

            def step(kb, carry, hh=hh, qh=qh):
                m, l, acc = carry
                ks = pl.multiple_of(kb * tk, tk)
                kblk = k_ref[pl.ds(ks, tk), :]
                vblk = v_ref[pl.ds(ks, tk), :]
                s = lax.dot_general(qh, kblk, NT, preferred_element_type=F32) * SCALE
                if decay:
                    s = s + (cc_ref[0, hh] - cr_ref[0, hh, :, pl.ds(ks, tk)])
                if causal:
                    s = jnp.where(col0 + kb * tk <= row, s, NEG)
                m_new = jnp.maximum(m, jnp.max(s, axis=-1, keepdims=True))
                alpha = jnp.exp(m - m_new)
                p = jnp.exp(s - m_new)
                l = alpha * l + jnp.sum(p, axis=-1, keepdims=True)
                acc = alpha * acc + jnp.dot(p.astype(BF16), vblk, preferred_element_type=F32)
                return m_new, l, acc

            init = (jnp.full((tq, 1), NEG, F32), jnp.zeros((tq, 1), F32), jnp.zeros((tq, LANES), F32))
            m, l, acc = lax.fori_loop(0, (i + 1) if causal else nk, step, init)
            outs.append(acc / l)
            lse_ref[0, hh] = m + jnp.log(l)
        o_ref[...] = jnp.where(lane < HEAD_DIM, outs[0], outs[1]).astype(BF16)

    in_specs = [_col_spec(tq, nq, q_cb), _kv_spec(Sk, k_cb), _kv_spec(Sk, v_cb)]
    args = [qa, ka, va]
    if decay:
        in_specs += [_stat_col_spec(tq), _stat_row_spec(S)]
        args += [cc, cr]
    return _pc(body, name=name,
               out_shape=[_sds((B * S, P * LANES), BF16), _sds((B, 2 * P, S, 1), F32)],
               grid=(B, P, nq), in_specs=in_specs, out_specs=[_col_spec(tq, nq, 0), _stat_col_spec(tq)],
               semantics=("arbitrary", "arbitrary", "arbitrary"), vmem=VMEM_BIG)(*args)


def _softmax_bwd(qa, ka, va, doa, oa, lse, *, name, B, S, Sk, P, q_cb, k_cb, v_cb, do_cb, causal,
                 cc=None, cr=None):
    tq = min(ATT_TILE, S)
    tk = min(ATT_TILE, Sk)
    nq, nk = S // tq, Sk // tk
    decay = cc is not None

    def body(*refs):
        if decay:
            q_ref, k_ref, v_ref, do_ref, o_ref, lse_ref, cc_ref, cr_ref, dq_ref, dk_ref, dv_ref, dcs_ref = refs
        else:
            q_ref, k_ref, v_ref, do_ref, o_ref, lse_ref, dq_ref, dk_ref, dv_ref = refs
        i = pl.program_id(2)

        @pl.when(i == 0)
        def _():
            dk_ref[...] = jnp.zeros_like(dk_ref)
            dv_ref[...] = jnp.zeros_like(dv_ref)
            if decay:
                dcs_ref[...] = jnp.zeros_like(dcs_ref)

        q = q_ref[...]
        do = do_ref[...]
        prod = do.astype(F32) * o_ref[...].astype(F32)
        lane = lax.broadcasted_iota(jnp.int32, (tq, LANES), 1)
        row = lax.broadcasted_iota(jnp.int32, (tq, tk), 0) + i * tq
        col0 = lax.broadcasted_iota(jnp.int32, (tq, tk), 1)
        dqs = []
        for hh in range(2):
            hmask = _head_mask(lane, hh)
            qh = jnp.where(hmask, q, jnp.zeros_like(q))
            doh = jnp.where(hmask, do, jnp.zeros_like(do))
            lse_h = lse_ref[0, hh]
            n_blocks = (i + 1) if causal else nk

            def probs(kb, hh=hh, qh=qh, doh=doh, lse_h=lse_h):
                ks = pl.multiple_of(kb * tk, tk)
                kblk = k_ref[pl.ds(ks, tk), :]
                vblk = v_ref[pl.ds(ks, tk), :]
                s = lax.dot_general(qh, kblk, NT, preferred_element_type=F32) * SCALE
                if decay:
                    s = s + (cc_ref[0, hh] - cr_ref[0, hh, :, pl.ds(ks, tk)])
                if causal:
                    s = jnp.where(col0 + kb * tk <= row, s, NEG)
                p = jnp.exp(s - lse_h)
                dp = lax.dot_general(doh, vblk, NT, preferred_element_type=F32)
                return ks, kblk, p, dp

            if decay:
                def delta_step(kb, acc):
                    _, _, p, dp = probs(kb)
                    return acc + jnp.sum(p * dp, axis=-1, keepdims=True)

                delta = lax.fori_loop(0, n_blocks, delta_step, jnp.zeros((tq, 1), F32))
            else:
                delta = jnp.sum(jnp.where(hmask, prod, 0.0), axis=-1, keepdims=True)

            def step(kb, dq_acc, hh=hh, qh=qh, doh=doh, delta=delta):
                ks, kblk, p, dp = probs(kb)
                ds = p * (dp - delta)
                dsb = ds.astype(BF16)
                dk_ref[pl.ds(ks, tk), :] += lax.dot_general(dsb, qh, TN, preferred_element_type=F32) * SCALE
                dv_ref[pl.ds(ks, tk), :] += lax.dot_general(p.astype(BF16), doh, TN, preferred_element_type=F32)
                if decay:
                    dcs_ref[0, hh, :, pl.ds(ks, tk)] -= jnp.sum(ds, axis=0, keepdims=True)
                return dq_acc + jnp.dot(dsb, kblk, preferred_element_type=F32)

            dqs.append(lax.fori_loop(0, n_blocks, step, jnp.zeros((tq, LANES), F32)) * SCALE)
        dq_ref[...] = jnp.where(lane < HEAD_DIM, dqs[0], dqs[1]).astype(BF16)

    in_specs = [_col_spec(tq, nq, q_cb), _kv_spec(Sk, k_cb), _kv_spec(Sk, v_cb), _col_spec(tq, nq, do_cb),
                _col_spec(tq, nq, 0), _stat_col_spec(tq)]
    args = [qa, ka, va, doa, oa, lse]
    out_shape = [_sds((B * S, P * LANES), BF16), _sds((B * Sk, P * LANES), F32), _sds((B * Sk, P * LANES), F32)]
    out_specs = [_col_spec(tq, nq, 0), _kv_spec(Sk, 0), _kv_spec(Sk, 0)]
    if decay:
        in_specs += [_stat_col_spec(tq), _stat_row_spec(S)]
        args += [cc, cr]
        out_shape.append(_sds((B, 2 * P, 1, S), F32))
        out_specs.append(_stat_row_spec(S))
    return _pc(body, name=name, out_shape=out_shape, grid=(B, P, nq), in_specs=in_specs, out_specs=out_specs,
               semantics=("arbitrary", "arbitrary", "arbitrary"), vmem=VMEM_BIG)(*args)


def _sb_terms(qh, kblk, row, col0, kb, tk):
    z = lax.dot_general(qh, kblk, NT, preferred_element_type=F32) * SCALE
    causal = (col0 + kb * tk) < row
    sp = jnp.maximum(z, 0.0) + jnp.log(1.0 + jnp.exp(-jnp.abs(z)))
    ls = z - sp
    lm = jnp.where(causal, -sp, 0.0)
    return causal, ls, lm


def _stickbreak_fwd(qa, ka, va, *, name, B, S, P, q_cb, k_cb, v_cb):
    tq = tk = min(ATT_TILE, S)
    nq = S // tq

    def body(q_ref, k_ref, v_ref, o_ref, rt_ref):
        i = pl.program_id(2)
        q = q_ref[...]
        lane = lax.broadcasted_iota(jnp.int32, (tq, LANES), 1)
        row = lax.broadcasted_iota(jnp.int32, (tq, tk), 0) + i * tq
        col0 = lax.broadcasted_iota(jnp.int32, (tq, tk), 1)
        t_r = lax.broadcasted_iota(jnp.int32, (tk, tk), 0)
        t_c = lax.broadcasted_iota(jnp.int32, (tk, tk), 1)
        after = (t_r > t_c).astype(BF16)
        outs = []
        for hh in range(2):
            qh = jnp.where(_head_mask(lane, hh), q, jnp.zeros_like(q))

            def step(jj, carry, qh=qh):
                run, acc = carry
                kb = i - jj
                ks = pl.multiple_of(kb * tk, tk)
                kblk = k_ref[pl.ds(ks, tk), :]
                vblk = v_ref[pl.ds(ks, tk), :]
                causal, ls, lm = _sb_terms(qh, kblk, row, col0, kb, tk)
                suf = _dot_tri2(lm, after)
                a = jnp.where(causal, jnp.exp(ls + run + suf), 0.0)
                acc = acc + jnp.dot(a.astype(BF16), vblk, preferred_element_type=F32)
                return run + jnp.sum(lm, axis=-1, keepdims=True), acc

            run, acc = lax.fori_loop(0, i + 1, step, (jnp.zeros((tq, 1), F32), jnp.zeros((tq, LANES), F32)))
            outs.append(acc)
            rt_ref[0, hh] = run
        o_ref[...] = jnp.where(lane < HEAD_DIM, outs[0], outs[1]).astype(BF16)

    return _pc(body, name=name, out_shape=[_sds((B * S, P * LANES), BF16), _sds((B, 2 * P, S, 1), F32)],
               grid=(B, P, nq), in_specs=[_col_spec(tq, nq, q_cb), _kv_spec(S, k_cb), _kv_spec(S, v_cb)],
               out_specs=[_col_spec(tq, nq, 0), _stat_col_spec(tq)],
               semantics=("arbitrary", "arbitrary", "arbitrary"), vmem=VMEM_BIG)(qa, ka, va)


def _stickbreak_bwd(qa, ka, va, doa, rt, *, name, B, S, P, q_cb, k_cb, v_cb, do_cb):
    tq = tk = min(ATT_TILE, S)
    nq = S // tq

    def body(q_ref, k_ref, v_ref, do_ref, rt_ref, dq_ref, dk_ref, dv_ref):
        i = pl.program_id(2)

        @pl.when(i == 0)
        def _():
            dk_ref[...] = jnp.zeros_like(dk_ref)
            dv_ref[...] = jnp.zeros_like(dv_ref)

        q = q_ref[...]
        do = do_ref[...]
        lane = lax.broadcasted_iota(jnp.int32, (tq, LANES), 1)
        row = lax.broadcasted_iota(jnp.int32, (tq, tk), 0) + i * tq
        col0 = lax.broadcasted_iota(jnp.int32, (tq, tk), 1)
        t_r = lax.broadcasted_iota(jnp.int32, (tk, tk), 0)
        t_c = lax.broadcasted_iota(jnp.int32, (tk, tk), 1)
        upto = (t_r <= t_c).astype(BF16)
        before = (t_r < t_c).astype(BF16)
        dqs = []
        for hh in range(2):
            hmask = _head_mask(lane, hh)
            qh = jnp.where(hmask, q, jnp.zeros_like(q))
            doh = jnp.where(hmask, do, jnp.zeros_like(do))
            rt_h = rt_ref[0, hh]

            def step(kb, carry, qh=qh, doh=doh, rt_h=rt_h):
                pl_sum, pg_sum, dq_acc = carry
                ks = pl.multiple_of(kb * tk, tk)
                kblk = k_ref[pl.ds(ks, tk), :]
                vblk = v_ref[pl.ds(ks, tk), :]
                causal, ls, lm = _sb_terms(qh, kblk, row, col0, kb, tk)
                pin = _dot_tri2(lm, upto)
                a = jnp.where(causal, jnp.exp(ls + (rt_h - pl_sum) - pin), 0.0)
                da = lax.dot_general(doh, vblk, NT, preferred_element_type=F32)
                gm = a * da
                pg = _dot_tri2(gm, before) + pg_sum
                beta = jnp.exp(ls)
                dz = jnp.where(causal, gm * (1.0 - beta) - pg * beta, 0.0)
                dzb = dz.astype(BF16)
                dk_ref[pl.ds(ks, tk), :] += lax.dot_general(dzb, qh, TN, preferred_element_type=F32) * SCALE
                dv_ref[pl.ds(ks, tk), :] += lax.dot_general(a.astype(BF16), doh, TN, preferred_element_type=F32)
                return (pl_sum + jnp.sum(lm, axis=-1, keepdims=True),
                        pg_sum + jnp.sum(gm, axis=-1, keepdims=True),
                        dq_acc + jnp.dot(dzb, kblk, preferred_element_type=F32))

            zc = jnp.zeros((tq, 1), F32)
            _, _, dq_h = lax.fori_loop(0, i + 1, step, (zc, zc, jnp.zeros((tq, LANES), F32)))
            dqs.append(dq_h * SCALE)
        dq_ref[...] = jnp.where(lane < HEAD_DIM, dqs[0], dqs[1]).astype(BF16)

    return _pc(body, name=name,
               out_shape=[_sds((B * S, P * LANES), BF16), _sds((B * S, P * LANES), F32), _sds((B * S, P * LANES), F32)],
               grid=(B, P, nq),
               in_specs=[_col_spec(tq, nq, q_cb), _kv_spec(S, k_cb), _kv_spec(S, v_cb), _col_spec(tq, nq, do_cb),
                         _stat_col_spec(tq)],
               out_specs=[_col_spec(tq, nq, 0), _kv_spec(S, 0), _kv_spec(S, 0)],
               semantics=("arbitrary", "arbitrary", "arbitrary"), vmem=VMEM_BIG)(qa, ka, va, doa, rt)


def _pair_setup(q_ref, tq, tk):
    q = q_ref[...] * jnp.asarray(SCALE, BF16)
    lane = lax.broadcasted_iota(jnp.int32, (tq, LANES), 1)
    masks = [_head_mask(lane, hh) for hh in range(2)]
    qh = [jnp.where(mk, q, jnp.zeros_like(q)) for mk in masks]
    on_or_below = (lax.broadcasted_iota(jnp.int32, (tq, tk), 1) <= lax.broadcasted_iota(jnp.int32, (tq, tk), 0))
    return lane, masks, qh, on_or_below


def _fox_fwd(qa, ka, va, cr, *, name, B, S, P, q_cb, k_cb, v_cb):
    tq = tk = min(ATT_TILE, S)
    nq = S // tq

    def body(q_ref, k_ref, v_ref, cr_ref, o_ref, lse_ref, s_buf, acc_ref, m_ref, l_ref):
        i = pl.program_id(2)
        lane, _, qh, on_or_below = _pair_setup(q_ref, tq, tk)
        m_ref[...] = jnp.full(m_ref.shape, NEG, F32)
        l_ref[...] = jnp.zeros(l_ref.shape, F32)
        acc_ref[...] = jnp.zeros(acc_ref.shape, F32)

        def scores(kb, slot):
            kblk = k_ref[pl.ds(pl.multiple_of(kb * tk, tk), tk), :]
            for hh in range(2):
                s_buf[slot, hh] = lax.dot_general(qh[hh], kblk, NT, preferred_element_type=F32)

        def block(kb, slot, diag):
            ks = pl.multiple_of(kb * tk, tk)
            vblk = v_ref[pl.ds(ks, tk), :]
            ps = []
            for hh in range(2):
                s = s_buf[slot, hh] - cr_ref[0, hh, :, pl.ds(ks, tk)]
                if diag:
                    s = jnp.where(on_or_below, s, NEG)
                m = m_ref[hh]
                m_new = jnp.maximum(m, jnp.max(s, axis=-1, keepdims=True))
                alpha = jnp.exp(m - m_new)
                p = jnp.exp(s - m_new)
                m_ref[hh] = m_new
                l_ref[hh] = alpha * l_ref[hh] + jnp.sum(p, axis=-1, keepdims=True)
                ps.append((alpha, p.astype(BF16)))
            for hh in range(2):
                acc_ref[hh] = ps[hh][0] * acc_ref[hh] + jnp.dot(ps[hh][1], vblk, preferred_element_type=F32)

        def step(kb, _):
            slot = lax.rem(kb, 2)
            scores(kb + 1, 1 - slot)
            block(kb, slot, False)
            return 0

        scores(0, 0)
        lax.fori_loop(0, i, step, 0)
        block(i, lax.rem(i, 2), True)
        l0, l1 = l_ref[0], l_ref[1]
        lse_ref[0, 0] = m_ref[0] + jnp.log(l0)
        lse_ref[0, 1] = m_ref[1] + jnp.log(l1)
        o_ref[...] = jnp.where(lane < HEAD_DIM, acc_ref[0] / l0, acc_ref[1] / l1).astype(BF16)

    return _pc(body, name=name, out_shape=[_sds((B * S, P * LANES), BF16), _sds((B, 2 * P, S, 1), F32)],
               grid=(B, P, nq),
               in_specs=[_col_spec(tq, nq, q_cb), _kv_spec(S, k_cb), _kv_spec(S, v_cb), _stat_row_spec(S)],
               out_specs=[_col_spec(tq, nq, 0), _stat_col_spec(tq)],
               scratch_shapes=[pltpu.VMEM((2, 2, tq, tk), F32), pltpu.VMEM((2, tq, LANES), F32),
                               pltpu.VMEM((2, tq, 1), F32), pltpu.VMEM((2, tq, 1), F32)],
               semantics=("arbitrary", "arbitrary", "arbitrary"), vmem=VMEM_BIG)(qa, ka, va, cr)


def _fox_bwd(qa, ka, va, doa, lse, cr, *, name, B, S, P, q_cb, k_cb, v_cb, do_cb):
    tq = tk = min(ATT_TILE, S)
    nq = S // tq

    def body(q_ref, k_ref, v_ref, do_ref, lse_ref, cr_ref, dq_ref, dk_ref, dv_ref, dcs_ref):
        i = pl.program_id(2)

        @pl.when(i == 0)
        def _():
            dk_ref[...] = jnp.zeros_like(dk_ref)
            dv_ref[...] = jnp.zeros_like(dv_ref)
            dcs_ref[...] = jnp.zeros_like(dcs_ref)

        lane, masks, qh, on_or_below = _pair_setup(q_ref, tq, tk)
        do = do_ref[...]
        doh = [jnp.where(mk, do, jnp.zeros_like(do)) for mk in masks]
        lse_h = [lse_ref[0, hh] for hh in range(2)]

        def probs(kb, hh, diag):
            ks = pl.multiple_of(kb * tk, tk)
            kblk = k_ref[pl.ds(ks, tk), :]
            vblk = v_ref[pl.ds(ks, tk), :]
            s = lax.dot_general(qh[hh], kblk, NT, preferred_element_type=F32) - cr_ref[0, hh, :, pl.ds(ks, tk)]
            if diag:
                s = jnp.where(on_or_below, s, NEG)
            p = jnp.exp(s - lse_h[hh])
            dp = lax.dot_general(doh[hh], vblk, NT, preferred_element_type=F32)
            return ks, kblk, p, dp

        def delta_block(kb, carry, diag):
            out = []
            for hh in range(2):
                _, _, p, dp = probs(kb, hh, diag)
                out.append(carry[hh] + jnp.sum(p * dp, axis=-1, keepdims=True))
            return tuple(out)

        zc = jnp.zeros((tq, 1), F32)
        delta = lax.fori_loop(0, i, lambda kb, c: delta_block(kb, c, False), (zc, zc))
        delta = delta_block(i, delta, True)

        def grad_block(kb, carry, diag):
            out = []
            for hh in range(2):
                ks, kblk, p, dp = probs(kb, hh, diag)
                ds = p * (dp - delta[hh])
                dsb = ds.astype(BF16)
                rows = pl.ds(ks, tk)
                dk_ref[rows, :] += lax.dot_general(dsb, qh[hh], TN, preferred_element_type=F32)
                dv_ref[rows, :] += lax.dot_general(p.astype(BF16), doh[hh], TN, preferred_element_type=F32)
                dcs_ref[0, hh, :, rows] -= jnp.sum(ds, axis=0, keepdims=True)
                out.append(carry[hh] + jnp.dot(dsb, kblk, preferred_element_type=F32))
            return tuple(out)

        za = jnp.zeros((tq, LANES), F32)
        dq = lax.fori_loop(0, i, lambda kb, c: grad_block(kb, c, False), (za, za))
        dq = grad_block(i, dq, True)
        dq_ref[...] = (jnp.where(lane < HEAD_DIM, dq[0], dq[1]) * SCALE).astype(BF16)

    return _pc(body, name=name,
               out_shape=[_sds((B * S, P * LANES), BF16), _sds((B * S, P * LANES), F32), _sds((B * S, P * LANES), F32),
                          _sds((B, 2 * P, 1, S), F32)],
               grid=(B, P, nq),
               in_specs=[_col_spec(tq, nq, q_cb), _kv_spec(S, k_cb), _kv_spec(S, v_cb), _col_spec(tq, nq, do_cb),
                         _stat_col_spec(tq), _stat_row_spec(S)],
               out_specs=[_col_spec(tq, nq, 0), _kv_spec(S, 0), _kv_spec(S, 0), _stat_row_spec(S)],
               semantics=("arbitrary", "arbitrary", "arbitrary"), vmem=VMEM_BIG)(qa, ka, va, doa, lse, cr)


def _sb_logs(qh, kblk):
    z = lax.dot_general(qh, kblk, NT, preferred_element_type=F32)
    nz = -z
    lg = jnp.log(1.0 + jnp.exp(jnp.minimum(z, nz)))
    lm = jnp.minimum(nz, 0.0) - lg
    return lm + z, lm


def _sb_fwd(qa, ka, va, *, name, B, S, P, q_cb, k_cb, v_cb):
    tq = tk = min(ATT_TILE, S)
    nq = S // tq

    def body(q_ref, k_ref, v_ref, o_ref, rt_ref):
        i = pl.program_id(2)
        lane, _, qh, on_or_below = _pair_setup(q_ref, tq, tk)
        t_r = lax.broadcasted_iota(jnp.int32, (tk, tk), 0)
        t_c = lax.broadcasted_iota(jnp.int32, (tk, tk), 1)
        after = (t_r > t_c).astype(BF16)
        below = t_c < t_r

        def block(kb, carry, diag):
            ks = pl.multiple_of(kb * tk, tk)
            kblk = k_ref[pl.ds(ks, tk), :]
            vblk = v_ref[pl.ds(ks, tk), :]
            out = []
            for hh in range(2):
                run, acc = carry[hh]
                ls, lm = _sb_logs(qh[hh], kblk)
                if diag:
                    lm = jnp.where(below, lm, 0.0)
                a = jnp.exp(ls + run + _dot_tri2(lm, after))
                if diag:
                    a = jnp.where(below, a, 0.0)
                acc = acc + jnp.dot(a.astype(BF16), vblk, preferred_element_type=F32)
                out.append((run + jnp.sum(lm, axis=-1, keepdims=True), acc))
            return tuple(out)

        one = (jnp.zeros((tq, 1), F32), jnp.zeros((tq, LANES), F32))
        carry = block(i, (one, one), True)
        (r0, a0), (r1, a1) = lax.fori_loop(0, i, lambda jj, c: block(i - 1 - jj, c, False), carry)
        rt_ref[0, 0] = r0
        rt_ref[0, 1] = r1
        o_ref[...] = jnp.where(lane < HEAD_DIM, a0, a1).astype(BF16)

    return _pc(body, name=name, out_shape=[_sds((B * S, P * LANES), BF16), _sds((B, 2 * P, S, 1), F32)],
               grid=(B, P, nq), in_specs=[_col_spec(tq, nq, q_cb), _kv_spec(S, k_cb), _kv_spec(S, v_cb)],
               out_specs=[_col_spec(tq, nq, 0), _stat_col_spec(tq)],
               semantics=("arbitrary", "arbitrary", "arbitrary"), vmem=VMEM_BIG)(qa, ka, va)


def _sb_bwd(qa, ka, va, doa, rt, *, name, B, S, P, q_cb, k_cb, v_cb, do_cb):
    tq = tk = min(ATT_TILE, S)
    nq = S // tq

    def body(q_ref, k_ref, v_ref, do_ref, rt_ref, dq_ref, dk_ref, dv_ref):
        i = pl.program_id(2)

        @pl.when(i == 0)
        def _():
            dk_ref[...] = jnp.zeros_like(dk_ref)
            dv_ref[...] = jnp.zeros_like(dv_ref)

        lane, masks, qh, _ = _pair_setup(q_ref, tq, tk)
        do = do_ref[...]
        doh = [jnp.where(mk, do, jnp.zeros_like(do)) for mk in masks]
        rt_h = [rt_ref[0, hh] for hh in range(2)]
        t_r = lax.broadcasted_iota(jnp.int32, (tk, tk), 0)
        t_c = lax.broadcasted_iota(jnp.int32, (tk, tk), 1)
        upto = (t_r <= t_c).astype(BF16)
        before = (t_r < t_c).astype(BF16)
        below = t_c < t_r

        def block(kb, carry, diag):
            ks = pl.multiple_of(kb * tk, tk)
            rows = pl.ds(ks, tk)
            kblk = k_ref[rows, :]
            vblk = v_ref[rows, :]
            out = []
            for hh in range(2):
                pl_sum, pg_sum, dq_acc = carry[hh]
                ls, lm = _sb_logs(qh[hh], kblk)
                if diag:
                    lm = jnp.where(below, lm, 0.0)
                a = jnp.exp(ls + (rt_h[hh] - pl_sum) - _dot_tri2(lm, upto))
                if diag:
                    a = jnp.where(below, a, 0.0)
                gm = a * lax.dot_general(doh[hh], vblk, NT, preferred_element_type=F32)
                pg = _dot_tri2(gm, before) + pg_sum
                dz = gm - jnp.exp(ls) * (gm + pg)
                if diag:
                    dz = jnp.where(below, dz, 0.0)
                dzb = dz.astype(BF16)
                dk_ref[rows, :] += lax.dot_general(dzb, qh[hh], TN, preferred_element_type=F32)
                dv_ref[rows, :] += lax.dot_general(a.astype(BF16), doh[hh], TN, preferred_element_type=F32)
                out.append((pl_sum + jnp.sum(lm, axis=-1, keepdims=True),
                            pg_sum + jnp.sum(gm, axis=-1, keepdims=True),
                            dq_acc + jnp.dot(dzb, kblk, preferred_element_type=F32)))
            return tuple(out)

        zc = jnp.zeros((tq, 1), F32)
        one = (zc, zc, jnp.zeros((tq, LANES), F32))
        carry = lax.fori_loop(0, i, lambda kb, c: block(kb, c, False), (one, one))
        (_, _, dq0), (_, _, dq1) = block(i, carry, True)
        dq_ref[...] = (jnp.where(lane < HEAD_DIM, dq0, dq1) * SCALE).astype(BF16)

    return _pc(body, name=name,
               out_shape=[_sds((B * S, P * LANES), BF16), _sds((B * S, P * LANES), F32), _sds((B * S, P * LANES), F32)],
               grid=(B, P, nq),
               in_specs=[_col_spec(tq, nq, q_cb), _kv_spec(S, k_cb), _kv_spec(S, v_cb), _col_spec(tq, nq, do_cb),
                         _stat_col_spec(tq)],
               out_specs=[_col_spec(tq, nq, 0), _kv_spec(S, 0), _kv_spec(S, 0)],
               semantics=("arbitrary", "arbitrary", "arbitrary"), vmem=VMEM_BIG)(qa, ka, va, doa, rt)


HEAD_GROUP = 3


def _g_col_spec(rows, nblk_rows, cb, G):
    return pl.BlockSpec((rows, G * LANES), lambda b, p, i: (b * nblk_rows + i, cb // G + p))


def _g_kv_spec(rows, cb, G):
    return pl.BlockSpec((rows, G * LANES), lambda b, p, i: (b, cb // G + p))


def _g_stat_col_spec(tq, G):
    return pl.BlockSpec((1, 2 * G, tq, 1), lambda b, p, i: (b, p, i, 0))


def _g_stat_row_spec(S, G):
    return pl.BlockSpec((1, 2 * G, 1, S), lambda b, p, i: (b, p, 0, 0))


def _lanes(g):
    return slice(g * LANES, (g + 1) * LANES)


def _streams(x_ref, G, scale=None):
    rows = x_ref.shape[0]
    lane = lax.broadcasted_iota(jnp.int32, (rows, LANES), 1)
    out = []
    for g in range(G):
        x = x_ref[:, _lanes(g)]
        if scale is not None:
            x = x * jnp.asarray(scale, x.dtype)
        for hh in range(2):
            out.append(jnp.where(_head_mask(lane, hh), x, jnp.zeros_like(x)))
    return lane, out


def _wide(stat, width):
    return jnp.tile(stat, (1, width // LANES))


def _fold_lanes(v):
    out = v[:, :LANES]
    for j in range(1, v.shape[1] // LANES):
        out = out + v[:, j * LANES:(j + 1) * LANES]
    return out


def _kv_blocks(ref, ks, tk, G):
    return [ref[pl.ds(ks, tk), _lanes(g)] for g in range(G)]


def _sweep(i, block):
    def step(kb, c):
        block(kb, False)
        return c
    lax.fori_loop(0, i, step, 0)
    block(i, True)


def _fox_fwd_g(qa, ka, va, cr, *, name, B, S, P, q_cb, k_cb, v_cb, G=HEAD_GROUP):
    tq = tk = min(ATT_TILE, S)
    nq = S // tq
    NS = 2 * G

    def body(q_ref, k_ref, v_ref, cr_ref, o_ref, lse_ref, acc_ref, m_ref, l_ref):
        i = pl.program_id(2)
        lane, qh = _streams(q_ref, G, SCALE)
        on_or_below = (lax.broadcasted_iota(jnp.int32, (tq, tk), 1) <= lax.broadcasted_iota(jnp.int32, (tq, tk), 0))
        m_ref[...] = jnp.full(m_ref.shape, NEG, F32)
        l_ref[...] = jnp.zeros(l_ref.shape, F32)
        acc_ref[...] = jnp.zeros(acc_ref.shape, F32)

        def block(kb, diag):
            ks = pl.multiple_of(kb * tk, tk)
            kblk = _kv_blocks(k_ref, ks, tk, G)
            vblk = _kv_blocks(v_ref, ks, tk, G)
            ss = [lax.dot_general(qh[st], kblk[st // 2], NT, preferred_element_type=F32) for st in range(NS)]
            ps = []
            for st in range(NS):
                s = ss[st] - cr_ref[0, st, :, pl.ds(ks, tk)]
                if diag:
                    s = jnp.where(on_or_below, s, NEG)
                m = m_ref[st]
                m_new = jnp.maximum(m, jnp.max(s, axis=-1, keepdims=True))
                alpha = jnp.exp(m - m_new)
                p = jnp.exp(s - _wide(m_new, tk))
                m_ref[st] = m_new
                l_ref[st] = alpha * l_ref[st] + _fold_lanes(p)
                ps.append((alpha, p.astype(BF16)))
            pvs = [jnp.dot(ps[st][1], vblk[st // 2], preferred_element_type=F32) for st in range(NS)]
            for st in range(NS):
                acc_ref[st] = ps[st][0] * acc_ref[st] + pvs[st]

        _sweep(i, block)
        ls = [jnp.sum(l_ref[st], axis=-1, keepdims=True) for st in range(NS)]
        for st in range(NS):
            lse_ref[0, st] = jnp.max(m_ref[st], axis=-1, keepdims=True) + jnp.log(ls[st])
        for g in range(G):
            o_ref[:, _lanes(g)] = jnp.where(lane < HEAD_DIM, acc_ref[2 * g] / ls[2 * g],
                                            acc_ref[2 * g + 1] / ls[2 * g + 1]).astype(BF16)

    return _pc(body, name=name, out_shape=[_sds((B * S, P * LANES), BF16), _sds((B, 2 * P, S, 1), F32)],
               grid=(B, P // G, nq),
               in_specs=[_g_col_spec(tq, nq, q_cb, G), _g_kv_spec(S, k_cb, G), _g_kv_spec(S, v_cb, G),
                         _g_stat_row_spec(S, G)],
               out_specs=[_g_col_spec(tq, nq, 0, G), _g_stat_col_spec(tq, G)],
               scratch_shapes=[pltpu.VMEM((NS, tq, LANES), F32)] * 3,
               semantics=("arbitrary", "arbitrary", "arbitrary"), vmem=VMEM_BIG)(qa, ka, va, cr)


def _fox_bwd_g(qa, ka, va, doa, lse, cr, *, name, B, S, P, q_cb, k_cb, v_cb, do_cb, G=HEAD_GROUP):
    tq = tk = min(ATT_TILE, S)
    nq = S // tq
    NS = 2 * G

    def body(q_ref, k_ref, v_ref, do_ref, lse_ref, cr_ref, dq_ref, dk_ref, dv_ref, dcs_ref, dqa_ref, delta_ref, lse_s,
             p_buf, dp_buf):
        i = pl.program_id(2)

        @pl.when(i == 0)
        def _():
            dk_ref[...] = jnp.zeros_like(dk_ref)
            dv_ref[...] = jnp.zeros_like(dv_ref)
            dcs_ref[...] = jnp.zeros_like(dcs_ref)

        lane, qh = _streams(q_ref, G, SCALE)
        _, doh = _streams(do_ref, G)
        on_or_below = (lax.broadcasted_iota(jnp.int32, (tq, tk), 1) <= lax.broadcasted_iota(jnp.int32, (tq, tk), 0))
        delta_ref[...] = jnp.zeros(delta_ref.shape, F32)
        dqa_ref[...] = jnp.zeros(dqa_ref.shape, F32)
        for st in range(NS):
            lse_s[st] = jnp.broadcast_to(lse_ref[0, st], (tq, LANES))

        def probs(kb, diag):
            ks = pl.multiple_of(kb * tk, tk)
            kblk = _kv_blocks(k_ref, ks, tk, G)
            vblk = _kv_blocks(v_ref, ks, tk, G)
            ss = [lax.dot_general(qh[st], kblk[st // 2], NT, preferred_element_type=F32) for st in range(NS)]
            dps = [lax.dot_general(doh[st], vblk[st // 2], NT, preferred_element_type=F32) for st in range(NS)]
            ps = []
            for st in range(NS):
                s = ss[st] - cr_ref[0, st, :, pl.ds(ks, tk)]
                if diag:
                    s = jnp.where(on_or_below, s, NEG)
                ps.append(jnp.exp(s - _wide(lse_s[st], tk)))
            return ks, kblk, ps, dps

        def delta_block(kb, diag):
            _, _, ps, dps = probs(kb, diag)
            for st in range(NS):
                delta_ref[st] += _fold_lanes(ps[st] * dps[st])
                p_buf[st, kb] = ps[st]
                dp_buf[st, kb] = dps[st]

        _sweep(i, delta_block)
        for st in range(NS):
            delta_ref[st] = jnp.broadcast_to(jnp.sum(delta_ref[st], axis=-1, keepdims=True), (tq, LANES))

        def grad_block(kb, diag):
            ks = pl.multiple_of(kb * tk, tk)
            kblk = _kv_blocks(k_ref, ks, tk, G)
            rows = pl.ds(ks, tk)
            dsb, pb = [], []
            for st in range(NS):
                p = p_buf[st, kb]
                ds = p * (dp_buf[st, kb] - _wide(delta_ref[st], tk))
                dcs_ref[0, st, :, rows] -= jnp.sum(ds, axis=0, keepdims=True)
                dsb.append(ds.astype(BF16))
                pb.append(p.astype(BF16))
            dks = [lax.dot_general(dsb[st], qh[st], TN, preferred_element_type=F32) for st in range(NS)]
            dvs = [lax.dot_general(pb[st], doh[st], TN, preferred_element_type=F32) for st in range(NS)]
            dqs = [jnp.dot(dsb[st], kblk[st // 2], preferred_element_type=F32) for st in range(NS)]
            for g in range(G):
                dk_ref[rows, _lanes(g)] += dks[2 * g] + dks[2 * g + 1]
                dv_ref[rows, _lanes(g)] += dvs[2 * g] + dvs[2 * g + 1]
            for st in range(NS):
                dqa_ref[st] += dqs[st]

        _sweep(i, grad_block)
        for g in range(G):
            dq_ref[:, _lanes(g)] = (jnp.where(lane < HEAD_DIM, dqa_ref[2 * g], dqa_ref[2 * g + 1]) * SCALE).astype(BF16)

    return _pc(body, name=name,
               out_shape=[_sds((B * S, P * LANES), BF16), _sds((B * S, P * LANES), F32), _sds((B * S, P * LANES), F32),
                          _sds((B, 2 * P, 1, S), F32)],
               grid=(B, P // G, nq),
               in_specs=[_g_col_spec(tq, nq, q_cb, G), _g_kv_spec(S, k_cb, G), _g_kv_spec(S, v_cb, G),
                         _g_col_spec(tq, nq, do_cb, G), _g_stat_col_spec(tq, G), _g_stat_row_spec(S, G)],
               out_specs=[_g_col_spec(tq, nq, 0, G), _g_kv_spec(S, 0, G), _g_kv_spec(S, 0, G), _g_stat_row_spec(S, G)],
               scratch_shapes=[pltpu.VMEM((NS, tq, LANES), F32)] * 3 + [pltpu.VMEM((NS, nq, tq, tk), F32)] * 2,
               semantics=("arbitrary", "arbitrary", "arbitrary"), vmem=VMEM_BIG)(qa, ka, va, doa, lse, cr)


def _sb_logs_z(z):
    nz = -z
    lm = jnp.minimum(nz, 0.0) - jnp.log(1.0 + jnp.exp(jnp.minimum(z, nz)))
    return lm + z, lm


def _sb_fwd_g(qa, ka, va, *, name, B, S, P, q_cb, k_cb, v_cb, G=HEAD_GROUP):
    tq = tk = min(ATT_TILE, S)
    nq = S // tq
    NS = 2 * G

    def body(q_ref, k_ref, v_ref, o_ref, rt_ref, acc_ref, run_ref):
        i = pl.program_id(2)
        lane, qh = _streams(q_ref, G, SCALE)
        t_r = lax.broadcasted_iota(jnp.int32, (tk, tk), 0)
        t_c = lax.broadcasted_iota(jnp.int32, (tk, tk), 1)
        after = (t_r > t_c).astype(BF16)
        below = t_c < t_r
        acc_ref[...] = jnp.zeros(acc_ref.shape, F32)
        run_ref[...] = jnp.zeros(run_ref.shape, F32)

        def block(kb, diag):
            ks = pl.multiple_of(kb * tk, tk)
            kblk = _kv_blocks(k_ref, ks, tk, G)
            vblk = _kv_blocks(v_ref, ks, tk, G)
            zs = [lax.dot_general(qh[st], kblk[st // 2], NT, preferred_element_type=F32) for st in range(NS)]
            lss, parts = [], []
            for st in range(NS):
                ls, lm = _sb_logs_z(zs[st])
                if diag:
                    lm = jnp.where(below, lm, 0.0)
                lss.append(ls + _wide(run_ref[st], tk))
                run_ref[st] += jnp.sum(lm, axis=-1, keepdims=True)
                parts.append(_split2(lm))
            sufs = [jnp.dot(parts[st][0], after, preferred_element_type=F32)
                    + jnp.dot(parts[st][1], after, preferred_element_type=F32) for st in range(NS)]
            ab = []
            for st in range(NS):
                a = jnp.exp(lss[st] + sufs[st])
                if diag:
                    a = jnp.where(below, a, 0.0)
                ab.append(a.astype(BF16))
            pvs = [jnp.dot(ab[st], vblk[st // 2], preferred_element_type=F32) for st in range(NS)]
            for st in range(NS):
                acc_ref[st] += pvs[st]

        block(i, True)

        def step(jj, c):
            block(i - 1 - jj, False)
            return c

        lax.fori_loop(0, i, step, 0)
        for st in range(NS):
            rt_ref[0, st] = jnp.max(run_ref[st], axis=-1, keepdims=True)
        for g in range(G):
            o_ref[:, _lanes(g)] = jnp.where(lane < HEAD_DIM, acc_ref[2 * g], acc_ref[2 * g + 1]).astype(BF16)

    return _pc(body, name=name, out_shape=[_sds((B * S, P * LANES), BF16), _sds((B, 2 * P, S, 1), F32)],
               grid=(B, P // G, nq),
               in_specs=[_g_col_spec(tq, nq, q_cb, G), _g_kv_spec(S, k_cb, G), _g_kv_spec(S, v_cb, G)],
               out_specs=[_g_col_spec(tq, nq, 0, G), _g_stat_col_spec(tq, G)],
               scratch_shapes=[pltpu.VMEM((NS, tq, LANES), F32)] * 2,
               semantics=("arbitrary", "arbitrary", "arbitrary"), vmem=VMEM_BIG)(qa, ka, va)


def _sb_bwd_g(qa, ka, va, doa, rt, *, name, B, S, P, q_cb, k_cb, v_cb, do_cb, G=HEAD_GROUP):
    tq = tk = min(ATT_TILE, S)
    nq = S // tq
    NS = 2 * G

    def body(q_ref, k_ref, v_ref, do_ref, rt_ref, dq_ref, dk_ref, dv_ref, dqa_ref, pl_ref, pg_ref):
        i = pl.program_id(2)

        @pl.when(i == 0)
        def _():
            dk_ref[...] = jnp.zeros_like(dk_ref)
            dv_ref[...] = jnp.zeros_like(dv_ref)

        lane, qh = _streams(q_ref, G, SCALE)
        _, doh = _streams(do_ref, G)
        t_r = lax.broadcasted_iota(jnp.int32, (tk, tk), 0)
        t_c = lax.broadcasted_iota(jnp.int32, (tk, tk), 1)
        upto = (t_r <= t_c).astype(BF16)
        before = (t_r < t_c).astype(BF16)
        below = t_c < t_r
        dqa_ref[...] = jnp.zeros(dqa_ref.shape, F32)
        pg_ref[...] = jnp.zeros(pg_ref.shape, F32)
        for st in range(NS):
            pl_ref[st] = jnp.broadcast_to(rt_ref[0, st], (tq, LANES))

        def block(kb, diag):
            ks = pl.multiple_of(kb * tk, tk)
            rows = pl.ds(ks, tk)
            kblk = _kv_blocks(k_ref, ks, tk, G)
            vblk = _kv_blocks(v_ref, ks, tk, G)
            zs = [lax.dot_general(qh[st], kblk[st // 2], NT, preferred_element_type=F32) for st in range(NS)]
            das = [lax.dot_general(doh[st], vblk[st // 2], NT, preferred_element_type=F32) for st in range(NS)]
            lss, parts = [], []
            for st in range(NS):
                ls, lm = _sb_logs_z(zs[st])
                if diag:
                    lm = jnp.where(below, lm, 0.0)
                lss.append((ls, ls + _wide(pl_ref[st], tk)))
                pl_ref[st] -= jnp.sum(lm, axis=-1, keepdims=True)
                parts.append(_split2(lm))
            pins = [jnp.dot(parts[st][0], upto, preferred_element_type=F32)
                    + jnp.dot(parts[st][1], upto, preferred_element_type=F32) for st in range(NS)]
            gms, ab, gparts = [], [], []
            for st in range(NS):
                a = jnp.exp(lss[st][1] - pins[st])
                if diag:
                    a = jnp.where(below, a, 0.0)
                gm = a * das[st]
                gms.append(gm)
                ab.append(a.astype(BF16))
                gparts.append(_split2(gm))
            pgs = [jnp.dot(gparts[st][0], before, preferred_element_type=F32)
                   + jnp.dot(gparts[st][1], before, preferred_element_type=F32) for st in range(NS)]
            dzb = []
            for st in range(NS):
                gm = gms[st]
                dz = gm - jnp.exp(lss[st][0]) * (gm + (pgs[st] + _wide(pg_ref[st], tk)))
                if diag:
                    dz = jnp.where(below, dz, 0.0)
                pg_ref[st] += jnp.sum(gm, axis=-1, keepdims=True)
                dzb.append(dz.astype(BF16))
            dks = [lax.dot_general(dzb[st], qh[st], TN, preferred_element_type=F32) for st in range(NS)]
            dvs = [lax.dot_general(ab[st], doh[st], TN, preferred_element_type=F32) for st in range(NS)]
            dqs = [jnp.dot(dzb[st], kblk[st // 2], preferred_element_type=F32) for st in range(NS)]
            for g in range(G):
                dk_ref[rows, _lanes(g)] += dks[2 * g] + dks[2 * g + 1]
                dv_ref[rows, _lanes(g)] += dvs[2 * g] + dvs[2 * g + 1]
            for st in range(NS):
                dqa_ref[st] += dqs[st]

        _sweep(i, block)
        for g in range(G):
            dq_ref[:, _lanes(g)] = (jnp.where(lane < HEAD_DIM, dqa_ref[2 * g], dqa_ref[2 * g + 1]) * SCALE).astype(BF16)

    return _pc(body, name=name,
               out_shape=[_sds((B * S, P * LANES), BF16), _sds((B * S, P * LANES), F32), _sds((B * S, P * LANES), F32)],
               grid=(B, P // G, nq),
               in_specs=[_g_col_spec(tq, nq, q_cb, G), _g_kv_spec(S, k_cb, G), _g_kv_spec(S, v_cb, G),
                         _g_col_spec(tq, nq, do_cb, G), _g_stat_col_spec(tq, G)],
               out_specs=[_g_col_spec(tq, nq, 0, G), _g_kv_spec(S, 0, G), _g_kv_spec(S, 0, G)],
               scratch_shapes=[pltpu.VMEM((NS, tq, LANES), F32)] * 3,
               semantics=("arbitrary", "arbitrary", "arbitrary"), vmem=VMEM_BIG)(qa, ka, va, doa, rt)


def _shift_rows(cur, halo_ref, first, rows_idx, k):
    out = pltpu.roll(cur, k, 0)
    for r in range(k):
        edge = jnp.where(first, 0.0, halo_ref[8 - k + r:8 - k + r + 1, :])
        out = jnp.where(rows_idx == r, edge, out)
    return out


def _shift_rows_up(cur, halo_ref, last, rows_idx, k, ts):
    out = pltpu.roll(cur, ts - k, 0)
    for r in range(k):
        edge = jnp.where(last, 0.0, halo_ref[r:r + 1, :])
        out = jnp.where(rows_idx == ts - k + r, edge, out)
    return out


def _conv_taps(main_ref, halo_ref, w_ref, b_ref, first, rows_idx):
    cur = main_ref[...]
    m1 = _shift_rows(cur, halo_ref, first, rows_idx, 1)
    m2 = _shift_rows(cur, halo_ref, first, rows_idx, 2)
    uc = b_ref[...] + w_ref[0:1, :] * m2 + w_ref[1:2, :] * m1 + w_ref[2:3, :] * cur
    return uc, cur, m1, m2


def _conv_specs(ts, tf, ns, nf, S, order):
    def wrap(fn):
        return lambda *g: fn(*order(*g))
    specs = []
    for off in (0, nf):
        specs.append(pl.BlockSpec((ts, tf), wrap(lambda b, i, j, off=off: (b * ns + i, j + off))))
        specs.append(pl.BlockSpec((8, tf), wrap(
            lambda b, i, j, off=off: (jnp.maximum((b * S + i * ts) // 8 - 1, 0), j + off))))
    for off in (0, nf):
        specs.append(pl.BlockSpec((3, tf), wrap(lambda b, i, j, off=off: (0, j + off))))
    for off in (0, nf):
        specs.append(pl.BlockSpec((1, tf), wrap(lambda b, i, j, off=off: (0, j + off))))
    return specs


def _conv_gate_fwd(u, cw, cb, *, name, B, S):
    F = u.shape[1] // 2
    tf = F // 2
    ts = min(256, S)
    ns, nf = S // ts, F // tf

    def body(ug_ref, ugh_ref, uv_ref, uvh_ref, wg_ref, wv_ref, bg_ref, bv_ref, a_ref):
        first = pl.program_id(1) == 0
        rows_idx = lax.broadcasted_iota(jnp.int32, (ts, tf), 0)
        ucg = _conv_taps(ug_ref, ugh_ref, wg_ref, bg_ref, first, rows_idx)[0]
        ucv = _conv_taps(uv_ref, uvh_ref, wv_ref, bv_ref, first, rows_idx)[0]
        a_ref[...] = (ucg * (1.0 / (1.0 + jnp.exp(-ucg))) * ucv).astype(BF16)

    specs = _conv_specs(ts, tf, ns, nf, S, lambda b, i, j: (b, i, j))
    return _pc(body, name=name, out_shape=_sds((B * S, F), BF16), grid=(B, ns, nf), in_specs=specs,
               out_specs=pl.BlockSpec((ts, tf), lambda b, i, j: (b * ns + i, j)),
               semantics=("arbitrary", "arbitrary", "arbitrary"), vmem=VMEM_BIG)(u, u, u, u, cw, cw, cb, cb)


def _conv_gate_bwd(da, u, cw, cb, *, name, B, S):
    F = u.shape[1] // 2
    tf = F // 2
    ts = min(256, S)
    ns, nf = S // ts, F // tf

    def body(da_ref, ug_ref, ugh_ref, uv_ref, uvh_ref, wg_ref, wv_ref, bg_ref, bv_ref,
             dug_ref, duv_ref, pg_ref, pv_ref, nxt_g, nxt_v):
        step = pl.program_id(2)
        first = step == ns - 1
        last = step == 0

        @pl.when(jnp.logical_and(pl.program_id(1) == 0, last))
        def _():
            pg_ref[...] = jnp.zeros_like(pg_ref)
            pv_ref[...] = jnp.zeros_like(pv_ref)

        rows_idx = lax.broadcasted_iota(jnp.int32, (ts, tf), 0)
        ucg, g0, g1, g2 = _conv_taps(ug_ref, ugh_ref, wg_ref, bg_ref, first, rows_idx)
        ucv, v0, v1, v2 = _conv_taps(uv_ref, uvh_ref, wv_ref, bv_ref, first, rows_idx)
        sg = 1.0 / (1.0 + jnp.exp(-ucg))
        dav = da_ref[...]
        d_v = dav * (ucg * sg)
        d_g = dav * ucv * (sg * (1.0 + ucg * (1.0 - sg)))
        for p_ref, d, taps in ((pg_ref, d_g, (g2, g1, g0)), (pv_ref, d_v, (v2, v1, v0))):
            for k in range(3):
                p_ref[k:k + 1, :] += jnp.sum(d * taps[k], axis=0, keepdims=True)
            p_ref[3:4, :] += jnp.sum(d, axis=0, keepdims=True)
        for o_ref, d, w_ref, nxt in ((dug_ref, d_g, wg_ref, nxt_g), (duv_ref, d_v, wv_ref, nxt_v)):
            p1 = _shift_rows_up(d, nxt, last, rows_idx, 1, ts)
            p2 = _shift_rows_up(d, nxt, last, rows_idx, 2, ts)
            o_ref[...] = (w_ref[2:3, :] * d + w_ref[1:2, :] * p1 + w_ref[0:1, :] * p2).astype(BF16)
            nxt[...] = d[0:8, :]

    def order(j, b, r):
        return b, ns - 1 - r, j

    row = pl.BlockSpec((ts, tf), lambda j, b, r: (b * ns + ns - 1 - r, j))
    specs = [row] + _conv_specs(ts, tf, ns, nf, S, order)
    par = pl.BlockSpec((8, tf), lambda j, b, r: (0, j))
    return _pc(body, name=name,
               out_shape=[_sds((B * S, F), BF16), _sds((B * S, F), BF16), _sds((8, F), F32), _sds((8, F), F32)],
               grid=(nf, B, ns), in_specs=specs, out_specs=[row, row, par, par],
               scratch_shapes=[pltpu.VMEM((8, tf), F32), pltpu.VMEM((8, tf), F32)],
               semantics=("arbitrary", "arbitrary", "arbitrary"), vmem=VMEM_BIG)(da, u, u, u, u, cw, cw, cb, cb)


def _conv_transpose(d, cw, *, name, B, S, col_off):
    F = d.shape[1]
    tf = F // 2
    ts = min(256, S)
    ns, nf = S // ts, F // tf
    nblk8 = B * S // 8

    def body(d_ref, dh_ref, w_ref, o_ref):
        last = pl.program_id(1) == ns - 1
        rows_idx = lax.broadcasted_iota(jnp.int32, (ts, tf), 0)
        cur = d_ref[...]
        p1 = _shift_rows_up(cur, dh_ref, last, rows_idx, 1, ts)
        p2 = _shift_rows_up(cur, dh_ref, last, rows_idx, 2, ts)
        o_ref[...] = (w_ref[2:3, :] * cur + w_ref[1:2, :] * p1 + w_ref[0:1, :] * p2).astype(BF16)

    return _pc(body, name=name, out_shape=_sds((B * S, F), BF16), grid=(B, ns, nf),
               in_specs=[pl.BlockSpec((ts, tf), lambda b, i, j: (b * ns + i, j)),
                         pl.BlockSpec((8, tf), lambda b, i, j: (jnp.minimum((b * S + (i + 1) * ts) // 8, nblk8 - 1), j)),
                         pl.BlockSpec((3, tf), lambda b, i, j: (0, j + col_off * nf))],
               out_specs=pl.BlockSpec((ts, tf), lambda b, i, j: (b * ns + i, j)),
               semantics=("arbitrary", "arbitrary", "arbitrary"))(d, d, cw)


def _adamw(w, g, m, v, *, name):
    rows, cols = w.shape
    tr = rows
    while tr * cols * 4 > 2 ** 20 and tr % 16 == 0:
        tr //= 2

    def body(w_ref, g_ref, m_ref, v_ref, d_ref, nm_ref, nv_ref):
        gv = g_ref[...]
        m_new = ADAM_B1 * m_ref[...] + (1.0 - ADAM_B1) * gv
        v_new = ADAM_B2 * v_ref[...] + (1.0 - ADAM_B2) * (gv * gv)
        m_hat = m_new / (1.0 - ADAM_B1 ** ADAM_STEP)
        v_hat = v_new / (1.0 - ADAM_B2 ** ADAM_STEP)
        d_ref[...] = -ADAM_LR * (m_hat / (jnp.sqrt(v_hat) + ADAM_EPS) + ADAM_WD * w_ref[...])
        nm_ref[...] = m_new
        nv_ref[...] = v_new

    blk = pl.BlockSpec((tr, cols), lambda i: (i, 0))
    return _pc(body, name=name, out_shape=[_sds((rows, cols), F32)] * 3, grid=(rows // tr,),
               in_specs=[blk] * 4, out_specs=[blk] * 3, semantics=("arbitrary",))(w, g, m, v)


def _my_pos():
    return lax.axis_index("x"), lax.axis_index("y"), lax.axis_index("c")


_HBM = pl.BlockSpec(memory_space=pltpu.HBM)
_SEM = pl.BlockSpec(memory_space=pltpu.SEMAPHORE)
_EFFECT = pltpu.SideEffectType.DATAFLOW_SIDE_EFFECTING


def _peers():
    x, y, c = _my_pos()
    out = []
    for k in range(1, N_DEV):
        px, py, pc = x ^ ((k >> 2) & 1), y ^ ((k >> 1) & 1), c ^ (k & 1)
        out.append(((px, py, pc), 4 * px + 2 * py + pc))
    return out


def _scatter_start(srcs, slot_of, *, name, order_after=None):
    n = len(srcs)
    lands = [lax.empty((N_DEV,) + slot_of(s, 0, shape_only=True), s.dtype) for s in srcs]
    extra = [] if order_after is None else [order_after]

    def body(*refs):
        src_refs, land_refs = refs[:n], refs[n:2 * n]
        send_sems, recv_sems = refs[2 * n + len(extra)], refs[2 * n + len(extra) + 1]
        token = refs[-1]
        x, y, c = _my_pos()
        me = 4 * x + 2 * y + c
        for a in range(n):
            for k, (peer, peer_idx) in enumerate(_peers()):
                pltpu.make_async_remote_copy(
                    src_ref=slot_of(src_refs[a], peer_idx), dst_ref=land_refs[a].at[me],
                    send_sem=send_sems.at[a * 7 + k], recv_sem=recv_sems.at[a * 7 + k],
                    device_id=peer, device_id_type=MESH).start()
        token[...] = jnp.zeros_like(token)

    hbm = lambda a: pltpu.HBM(a.shape, a.dtype)
    args = [pltpu.with_memory_space_constraint(a, pltpu.HBM) for a in list(srcs) + lands] + extra
    outs = pl.pallas_call(
        body, name=name,
        out_shape=(pltpu.SemaphoreType.DMA((7 * n,)), pltpu.SemaphoreType.DMA((7 * n,)),
                   *[hbm(a) for a in srcs], *[hbm(a) for a in lands], _sds((8, LANES), F32)),
        in_specs=[_HBM] * (2 * n) + [pl.BlockSpec(memory_space=pl.ANY)] * len(extra),
        out_specs=(_SEM, _SEM, *([_HBM] * (2 * n)), pl.BlockSpec(memory_space=pltpu.VMEM)),
        input_output_aliases={a: 2 + a for a in range(2 * n)},
        compiler_params=pltpu.CompilerParams(has_side_effects=_EFFECT))(*args)
    return outs[0], outs[1], list(outs[2:2 + n]), list(outs[2 + n:2 + 2 * n]), outs[-1]


def _scatter_wait(send_sems, recv_sems, srcs, lands, slot_of, after, *, name):
    n = len(srcs)

    def body(*refs):
        src_refs, land_refs = refs[:n], refs[n:2 * n]
        ssem, rsem = refs[2 * n], refs[2 * n + 1]
        x, y, c = _my_pos()
        me = 4 * x + 2 * y + c
        for a in range(n):
            for k, (peer, peer_idx) in enumerate(_peers()):
                cp = pltpu.make_async_remote_copy(
                    src_ref=slot_of(src_refs[a], peer_idx), dst_ref=land_refs[a].at[me],
                    send_sem=ssem.at[a * 7 + k], recv_sem=rsem.at[a * 7 + k],
                    device_id=peer, device_id_type=MESH)
                cp.wait_send()
                cp.wait_recv()

    hbm = lambda a: pltpu.HBM(a.shape, a.dtype)
    outs = pl.pallas_call(
        body, name=name, out_shape=tuple(hbm(a) for a in list(srcs) + list(lands)),
        in_specs=[_HBM] * (2 * n) + [_SEM, _SEM, pl.BlockSpec(memory_space=pl.ANY)],
        out_specs=tuple([_HBM] * (2 * n)), input_output_aliases={a: a for a in range(2 * n)},
        compiler_params=pltpu.CompilerParams(has_side_effects=_EFFECT))(*srcs, *lands, send_sems, recv_sems, after)
    return list(outs[:n]), list(outs[n:])


def _whole(a, peer_idx, shape_only=False):
    return a.shape if shape_only else a


def _slot(a, peer_idx, shape_only=False):
    return a.shape[1:] if shape_only else a.at[peer_idx]


def _all_gather(shard, *, name):
    rows = shard.shape[0]

    def body(x_ref, out_ref, send_sems, recv_sems, local_sem):
        x, y, c = _my_pos()
        me, sibling = (x, y, c), (x, y, 1 - c)
        chips = [(1 - x, y), (x, 1 - y), (1 - x, 1 - y)]

        def slot(px, py, pc):
            return out_ref.at[4 * px + 2 * py + pc]

        def copy(k, block, to, src=None):
            return pltpu.make_async_remote_copy(
                src_ref=slot(*block) if src is None else src, dst_ref=slot(*block),
                send_sem=send_sems.at[k], recv_sem=recv_sems.at[k], device_id=to, device_id_type=MESH)

        mine = pltpu.make_async_copy(x_ref, slot(*me), local_sem)
        mine.start()
        first = [copy(0, me, sibling, src=x_ref)]
        first += [copy(1 + j, me, (*chip, c), src=x_ref) for j, chip in enumerate(chips)]
        for cp in first:
            cp.start()
        passed = [copy(4 + j, (*chip, c), sibling) for j, chip in enumerate(chips)]
        for j, chip in enumerate(chips):
            copy(1 + j, (*chip, c), me).wait_recv()
            passed[j].start()
        copy(0, sibling, me).wait_recv()
        for j, chip in enumerate(chips):
            copy(4 + j, (*chip, 1 - c), me).wait_recv()
        for cp in first + passed:
            cp.wait_send()
        mine.wait()

    return _pc(body, name=name, out_shape=_sds((N_DEV, rows, LANES), shard.dtype),
               in_specs=[pl.BlockSpec(memory_space=pl.ANY)], out_specs=pl.BlockSpec(memory_space=pl.ANY),
               scratch_shapes=[pltpu.SemaphoreType.DMA((7,)), pltpu.SemaphoreType.DMA((7,)),
                               pltpu.SemaphoreType.DMA])(shard)


def _exchange(big, small, *, name):
    rs = small.shape[0]

    def body(big_ref, small_ref, bout_ref, sout_ref, send_sems, recv_sems, local_sems):
        x, y, c = _my_pos()
        me = 4 * x + 2 * y + c
        lb = pltpu.make_async_copy(big_ref.at[me], bout_ref.at[me], local_sems.at[0])
        ls = pltpu.make_async_copy(small_ref, sout_ref.at[me], local_sems.at[1])
        lb.start()
        ls.start()
        copies = []
        for k in range(1, N_DEV):
            px = x ^ ((k >> 2) & 1)
            py = y ^ ((k >> 1) & 1)
            pc = c ^ (k & 1)
            peer = 4 * px + 2 * py + pc
            copies.append(pltpu.make_async_remote_copy(
                src_ref=big_ref.at[peer], dst_ref=bout_ref.at[me], send_sem=send_sems.at[k - 1],
                recv_sem=recv_sems.at[k - 1], device_id=(px, py, pc), device_id_type=MESH))
            copies.append(pltpu.make_async_remote_copy(
                src_ref=small_ref, dst_ref=sout_ref.at[me], send_sem=send_sems.at[7 + k - 1],
                recv_sem=recv_sems.at[7 + k - 1], device_id=(px, py, pc), device_id_type=MESH))
        for cp in copies:
            cp.start()
        for cp in copies:
            cp.wait()
        lb.wait()
        ls.wait()

    return _pc(body, name=name,
               out_shape=[_sds(big.shape, big.dtype), _sds((N_DEV, rs, LANES), F32)],
               in_specs=[pl.BlockSpec(memory_space=pl.ANY), pl.BlockSpec(memory_space=pl.ANY)],
               out_specs=[pl.BlockSpec(memory_space=pl.ANY), pl.BlockSpec(memory_space=pl.ANY)],
               scratch_shapes=[pltpu.SemaphoreType.DMA((14,)), pltpu.SemaphoreType.DMA((14,)),
                               pltpu.SemaphoreType.DMA((2,))])(big, small)


def _all_gather_small(small, *, name):
    rs = small.shape[0]

    def body(small_ref, out_ref, send_sems, recv_sems, local_sem):
        x, y, c = _my_pos()
        me = 4 * x + 2 * y + c
        mine = pltpu.make_async_copy(small_ref, out_ref.at[me], local_sem)
        mine.start()
        copies = [pltpu.make_async_remote_copy(
            src_ref=small_ref, dst_ref=out_ref.at[me], send_sem=send_sems.at[k], recv_sem=recv_sems.at[k],
            device_id=peer, device_id_type=MESH) for k, (peer, _) in enumerate(_peers())]
        for cp in copies:
            cp.start()
        for cp in copies:
            cp.wait()
        mine.wait()

    return _pc(body, name=name, out_shape=_sds((N_DEV, rs, LANES), F32),
               in_specs=[pl.BlockSpec(memory_space=pl.ANY)], out_specs=pl.BlockSpec(memory_space=pl.ANY),
               scratch_shapes=[pltpu.SemaphoreType.DMA((7,)), pltpu.SemaphoreType.DMA((7,)),
                               pltpu.SemaphoreType.DMA])(small)


def _sum_slots(a, *, name, tr=None):
    rows, cols = a.shape[1], a.shape[2]
    if tr is None:
        tr = rows
        while N_DEV * tr * cols * a.dtype.itemsize > 3 * 2 ** 20 and tr % 32 == 0:
            tr //= 2

    def body(a_ref, o_ref):
        acc = a_ref[0].astype(F32)
        for j in range(1, N_DEV):
            acc = acc + a_ref[j].astype(F32)
        o_ref[...] = acc

    return _pc(body, name=name, out_shape=_sds((rows, cols), F32), grid=(rows // tr,),
               in_specs=[pl.BlockSpec((N_DEV, tr, cols), lambda i: (0, i, 0))],
               out_specs=pl.BlockSpec((tr, cols), lambda i: (i, 0)), semantics=("arbitrary",), vmem=VMEM_BIG)(a)


PACK_ROWS = 25600
SUM_TILE = 512


def _rows128(a):
    return a.reshape(-1, LANES)


def _to_slots(full, kind):
    if kind == "rows2":
        r, c = full.shape
        return full.reshape(N_DEV, r // N_DEV, c)
    if kind == "cols2":
        r, c = full.shape
        return full.reshape(r, N_DEV, c // N_DEV).transpose(1, 0, 2)
    if kind == "rows3":
        l, r, c = full.shape
        return full.reshape(l, N_DEV, r // N_DEV, c).transpose(1, 0, 2, 3)
    if kind == "cols3":
        l, r, c = full.shape
        return full.reshape(l, r, N_DEV, c // N_DEV).transpose(2, 0, 1, 3)
    raise ValueError(kind)


def _from_slots(slots, kind):
    if kind == "rows2":
        _, r, c = slots.shape
        return slots.reshape(N_DEV * r, c)
    if kind == "cols2":
        _, r, c = slots.shape
        return slots.transpose(1, 0, 2).reshape(r, N_DEV * c)
    if kind == "rows3":
        _, l, r, c = slots.shape
        return slots.transpose(1, 0, 2, 3).reshape(l, N_DEV * r, c)
    if kind == "cols3":
        _, l, r, c = slots.shape
        return slots.transpose(1, 2, 0, 3).reshape(l, r, N_DEV * c)
    raise ValueError(kind)


BIG = (("w_in_a", "rows2"), ("w_in_b", "rows2"), ("w_kv", "cols2"), ("w_memkv", "rows3"),
       ("w_out", "rows3"), ("w_up", "cols3"), ("w_down", "rows3"))


def _round_up(n, m):
    return -(-n // m) * m


def _pad_rows(a, rows, axis):
    pad = [(0, 0)] * a.ndim
    pad[axis] = (0, rows - a.shape[axis])
    return jnp.pad(a, pad)


def kernel(x, mem, ln_mix_g, w_in_a, b_f_a, w_in_b, ln_kv_g, w_kv, ln_mem_g, w_memkv, w_out, ln_ffn_g, w_up, conv_w, conv_b, w_down, final_g, loss_target, m_ln_mix_g, m_w_in_a, m_b_f_a, m_w_in_b, m_ln_kv_g, m_w_kv, m_ln_mem_g, m_w_memkv, m_w_out, m_ln_ffn_g, m_w_up, m_conv_w, m_conv_b, m_w_down, m_final_g, v_ln_mix_g, v_w_in_a, v_b_f_a, v_w_in_b, v_ln_kv_g, v_w_kv, v_ln_mem_g, v_w_memkv, v_w_out, v_ln_ffn_g, v_w_up, v_conv_w, v_conv_b, v_w_down, v_final_g):
    B, S, D = x.shape
    NM = mem.shape[1]
    T = B * S
    F = w_down.shape[1] * N_DEV
    my_idx = 4 * lax.axis_index("x") + 2 * lax.axis_index("y") + lax.axis_index("c")

    shards = {"w_in_a": w_in_a[0], "w_in_b": w_in_b[0], "w_kv": w_kv, "w_memkv": w_memkv, "w_out": w_out,
              "w_up": w_up, "w_down": w_down}
    moms = {"w_in_a": (m_w_in_a[0], v_w_in_a[0]), "w_in_b": (m_w_in_b[0], v_w_in_b[0]), "w_kv": (m_w_kv, v_w_kv),
            "w_memkv": (m_w_memkv, v_w_memkv), "w_out": (m_w_out, v_w_out), "w_up": (m_w_up, v_w_up),
            "w_down": (m_w_down, v_w_down)}

    groups = [("a1", [("w_in_a", None)]),
              ("a2", [("w_in_b", None), ("w_kv", None), ("w_memkv", None), ("w_out", None), ("conv_w", None)]),
              ("b0", [("w_up", 0), ("w_down", 0)]), ("b1", [("w_up", 1), ("w_down", 1)])]
    sources = dict(shards, conv_w=conv_w)
    started, token = {}, None
    for gname, members in groups:
        srcs = []
        for n, layer in members:
            a = sources[n] if layer is None else sources[n][layer]
            srcs.append(a if n == "conv_w" else a.astype(BF16))
        ssem, rsem, thru, lands, token = _scatter_start(srcs, _whole, name=f"gather_start_{gname}", order_after=token)
        started[gname] = (ssem, rsem, thru, lands)

    def gathered(gname, after):
        ssem, rsem, thru, lands = started[gname]
        thru, lands = _scatter_wait(ssem, rsem, thru, lands, _whole, after, name=f"gather_wait_{gname}")
        return [lax.dynamic_update_index_in_dim(land, s, my_idx, 0) for land, s in zip(lands, thru)]

    full = {}
    (g_wa,) = gathered("a1", token)
    full["w_in_a"] = _from_slots(g_wa, "rows2")

    wa = full["w_in_a"]
    n_qkv = 3 * MAIN_W
    wa = jnp.concatenate([wa[:, :n_qkv], wa[:, n_qkv + N_MAIN_HEADS:], wa[:, n_qkv:n_qkv + N_MAIN_HEADS],
                          jnp.zeros((D, LANES - N_MAIN_HEADS), BF16)], axis=1)
    n_main = n_qkv + MEM_W
    full["w_up"], full["w_down"] = {}, {}
    b_f =_pad_rows(b_f_a.reshape(1, N_MAIN_HEADS), LANES, 1)

    x2d = x.reshape(T, D)
    mem2d = mem.reshape(B * NM, D)
    tgt2d = loss_target.reshape(T, D)
    PM, PX = N_MAIN_HEADS // 2, N_MEM_HEADS // 2

    def stats_to_heads(c2d):
        c = c2d.reshape(B, S, LANES)[:, :, :N_MAIN_HEADS].transpose(0, 2, 1)
        return c[:, :, None, :]

    def mem_kv(layer):
        return _mm_fwd(mem2d, full["w_memkv"][layer], name=f"memkv{layer}", tm=B * NM, tn=2 * MEM_W,
                       out_dtype=BF16, g=ln_mem_g[layer], save_h=True)

    def conv_ffn_fwd(xin, layer):
        u, h = _mm_fwd(xin, full["w_up"][layer], name=f"ffn_up{layer}", tm=min(1024, T), tn=512, out_dtype=F32,
                       g=ln_ffn_g[layer], save_h=True)
        a = _conv_gate_fwd(u, conv_w_full[layer], conv_b[layer].reshape(1, 2 * F), name=f"conv_gate{layer}", B=B, S=S)
        xo = _mm_fwd(a, full["w_down"][layer], name=f"ffn_down{layer}", tm=min(512, T), tn=512, out_dtype=F32, res=xin)
        return xo, (u, h, a)

    proj_a, h_mix0 = _mm_fwd(x2d, wa, name="in_proj_a", tm=min(1024, T), tn=512, out_dtype=BF16, g=ln_mix_g[0],
                             ncols=n_main, save_h=True)
    f_logit = _mm_fwd(x2d, wa, name="in_proj_f", tm=min(1024, T), tn=LANES, out_dtype=F32, g=ln_mix_g[0],
                      col0=n_main // LANES, ncols=LANES)
    c2d = _forget_cumsum(f_logit, b_f, B=B, S=S, name="forget_cumsum")
    cr = stats_to_heads(c2d)
    o_main0, lse0 = _fox_fwd_g(proj_a, proj_a, proj_a, cr, name="fox_fwd", B=B, S=S, P=PM, q_cb=0, k_cb=PM, v_cb=2 * PM)
    g_wb, g_wkv, g_wmem, g_wout, g_cw = gathered("a2", lse0)
    wb = _from_slots(g_wb, "rows2")
    wkv = _from_slots(g_wkv, "cols2")
    full["w_memkv"] = _from_slots(g_wmem, "rows3")
    full["w_out"] = _from_slots(g_wout, "rows3")
    conv_w_full = _from_slots(g_cw, "cols3")
    memkv0, h_mem0 = mem_kv(0)
    o_mem0, lse_m0 = _softmax_fwd(proj_a, memkv0, memkv0, name="mem_fwd0", B=B, S=S, Sk=NM, P=PX, q_cb=3 * PM,
                                  k_cb=0, v_cb=PX, causal=False)
    o_cat0 = jnp.concatenate([o_main0, o_mem0], axis=1)
    x1 = _mm_fwd(o_cat0, full["w_out"][0], name="out_proj0", tm=min(512, T), tn=512, out_dtype=F32, res=x2d)
    g_up, g_dn = gathered("b0", x1)
    full["w_up"][0], full["w_down"][0] = _from_slots(g_up, "cols2"), _from_slots(g_dn, "rows2")
    x2, (u0, h_ffn0, a0) = conv_ffn_fwd(x1, 0)
    kv, h_kv = _mm_fwd(x2, wkv, name="kv_proj", tm=min(1024, T), tn=512, out_dtype=BF16, g=ln_kv_g, save_h=True)
    proj_b, h_mix1 = _mm_fwd(x2, wb, name="in_proj_b", tm=min(1024, T), tn=512, out_dtype=BF16, g=ln_mix_g[1],
                             save_h=True)
    o_main1, rt1 = _sb_fwd_g(proj_b, kv, kv, name="sb_fwd", B=B, S=S, P=PM, q_cb=0, k_cb=0, v_cb=PM)
    memkv1, h_mem1 = mem_kv(1)
    o_mem1, lse_m1 = _softmax_fwd(proj_b, memkv1, memkv1, name="mem_fwd1", B=B, S=S, Sk=NM, P=PX, q_cb=PM,
                                  k_cb=0, v_cb=PX, causal=False)
    o_cat1 = jnp.concatenate([o_main1, o_mem1], axis=1)
    x3 = _mm_fwd(o_cat1, full["w_out"][1], name="out_proj1", tm=min(512, T), tn=512, out_dtype=F32, res=x2)
    g_up, g_dn = gathered("b1", x3)
    full["w_up"][1], full["w_down"][1] = _from_slots(g_up, "cols2"), _from_slots(g_dn, "rows2")
    x4, (u1, h_ffn1, a1) = conv_ffn_fwd(x3, 1)
    dx4, dg_final, loss_part = _loss_head(x4, final_g, tgt2d, name="loss_head")

    grads = {}
    small = {}
    reduce_groups = []

    def start_reduce(gname, keys, kinds):
        slots = [_to_slots(grads[k], kind) for k, kind in zip(keys, kinds)]
        ssem, rsem, thru, lands, tok = _scatter_start(slots, _slot, name=f"reduce_start_{gname}")
        reduce_groups.append((gname, keys, ssem, rsem, thru, lands))
        return tok[0, 0]

    def conv_ffn_bwd(dxo, xin, u, h, a, layer):
        w_dn = full["w_down"][layer]
        da = _mm_nt(dxo, w_dn, name=f"d_act{layer}", tm=min(512, T), tn=F // 2, out_dtype=F32)
        grads[("w_down", layer)] = _wgrad(a, dxo, f"g_w_down{layer}")
        cwl = conv_w_full[layer]
        du_g, du_v, p_g, p_v = _conv_gate_bwd(da, u, cwl, conv_b[layer].reshape(1, 2 * F), name=f"conv_bwd{layer}",
                                              B=B, S=S)
        small[("conv_w", layer)] = jnp.concatenate([p_g[0:3], p_v[0:3]], axis=1)
        small[("conv_b", layer)] = jnp.concatenate([p_g[3], p_v[3]], axis=0)
        grads[("w_up", layer)] = jnp.concatenate(
            [_wgrad(h, du_g, f"g_w_up_gate{layer}"), _wgrad(h, du_v, f"g_w_up_val{layer}")], axis=1)
        tok = start_reduce(f"ffn{layer}", [("w_down", layer), ("w_up", layer)], ["rows2", "cols2"])
        dxi, dg = _mm_nt_rmsbwd([(du_g, 0), (du_v, 1)], full["w_up"][layer], xin, ln_ffn_g[layer] + tok,
                                name=f"d_ffn_in{layer}", dres=dxo)
        small[("ln_ffn_g", layer)] = dg[0]
        return dxi

    def mem_bwd(proj, q_cb, memkv, h_mem, do_cat, o_mem, lse_m, layer):
        dqm, dmk, dmv = _softmax_bwd(proj, memkv, memkv, do_cat, o_mem, lse_m, name=f"mem_bwd{layer}", B=B, S=S,
                                     Sk=NM, P=PX, q_cb=q_cb, k_cb=0, v_cb=PX, do_cb=PM, causal=False)
        grads[("w_memkv", layer)] = jnp.concatenate(
            [_wgrad(h_mem, dmk, f"g_w_memk{layer}"), _wgrad(h_mem, dmv, f"g_w_memv{layer}")], axis=1)
        _, dg = _mm_nt_rmsbwd([(dmk, 0), (dmv, 1)], full["w_memkv"][layer], mem2d, ln_mem_g[layer],
                              name=f"d_mem_in{layer}", want_dx=False)
        small[("ln_mem_g", layer)] = dg[0]
        return dqm

    dx3 = conv_ffn_bwd(dx4, x3, u1, h_ffn1, a1, 1)
    do_cat1 = _mm_nt(dx3, full["w_out"][1], name="d_o_cat1", tm=min(512, T), tn=512, out_dtype=BF16)
    grads[("w_out", 1)] = _wgrad(o_cat1, dx3, "g_w_out1")
    dq1, dk1, dv1 = _sb_bwd_g(proj_b, kv, kv, do_cat1, rt1, name="sb_bwd", B=B, S=S, P=PM, q_cb=0, k_cb=0, v_cb=PM,
                            do_cb=0)
    dqm1 = mem_bwd(proj_b, PM, memkv1, h_mem1, do_cat1, o_mem1, lse_m1, 1)
    grads["w_in_b"] = jnp.concatenate([_wgrad(h_mix1, dq1, "g_w_in_b_q"), _wgrad(h_mix1, dqm1, "g_w_in_b_m")], axis=1)
    grads["w_kv"] = jnp.concatenate([_wgrad(h_kv, dk1, "g_w_kv_k"), _wgrad(h_kv, dv1, "g_w_kv_v")], axis=1)
    tok = start_reduce("mix1", [("w_out", 1), "w_in_b", "w_kv", ("w_memkv", 1)], ["rows2", "rows2", "cols2", "rows2"])
    dx2, dg = _mm_nt_rmsbwd([(dq1, 0), (dqm1, MAIN_W // MEM_W)], wb, x2, ln_mix_g[1] + tok, name="d_mix_in1", dres=dx3)
    small[("ln_mix_g", 1)] = dg[0]
    dx2, dg = _mm_nt_rmsbwd([(dk1, 0), (dv1, 1)], wkv, x2, ln_kv_g, name="d_kv_in", dres=dx2)
    small["ln_kv_g"] = dg[0]
    dx1 = conv_ffn_bwd(dx2, x1, u0, h_ffn0, a0, 0)
    do_cat0 = _mm_nt(dx1, full["w_out"][0], name="d_o_cat0", tm=min(512, T), tn=512, out_dtype=BF16)
    grads[("w_out", 0)] = _wgrad(o_cat0, dx1, "g_w_out0")
    dq0, dk0, dv0, dcs = _fox_bwd_g(proj_a, proj_a, proj_a, do_cat0, lse0, cr, name="fox_bwd", B=B, S=S, P=PM, q_cb=0,
                                  k_cb=PM, v_cb=2 * PM, do_cb=0)
    dqm0 = mem_bwd(proj_a, 3 * PM, memkv0, h_mem0, do_cat0, o_mem0, lse_m0, 0)
    dc2d = _pad_rows(dcs[:, :, 0, :].transpose(0, 2, 1).reshape(T, N_MAIN_HEADS), LANES, 1)
    df, db_f = _forget_cumsum_bwd(dc2d, f_logit, b_f, B=B, S=S, name="forget_cumsum_bwd")
    a_parts = [(dq0, 0), (dk0, 1), (dv0, 2), (dqm0, n_qkv // MEM_W), (df, n_main // LANES)]
    g_wa = jnp.concatenate([_wgrad(h_mix0, p, f"g_w_in_a{k}") for k, (p, _) in enumerate(a_parts)], axis=1)
    grads["w_in_a"] = jnp.concatenate([g_wa[:, :n_qkv], g_wa[:, n_main:n_main + N_MAIN_HEADS], g_wa[:, n_qkv:n_main]],
                                      axis=1)
    tok = start_reduce("mix0", [("w_out", 0), ("w_memkv", 0), "w_in_a"], ["rows2", "rows2", "rows2"])
    dx0, dg = _mm_nt_rmsbwd(a_parts, wa, x2d, ln_mix_g[0] + tok, name="d_mix_in0", dres=dx1)
    small[("ln_mix_g", 0)] = dg[0]
    grad_x = dx0.reshape(B, S, D)

    pieces = {}
    for gname, keys, ssem, rsem, thru, lands in reduce_groups:
        thru, lands = _scatter_wait(ssem, rsem, thru, lands, _slot, dx0, name=f"reduce_wait_{gname}")
        for key, mine, land in zip(keys, thru, lands):
            own = lax.dynamic_index_in_dim(mine, my_idx, 0, keepdims=False)
            land = lax.dynamic_update_index_in_dim(land, own, my_idx, 0)
            tag = key if isinstance(key, str) else f"{key[0]}{key[1]}"
            pieces[key] = _sum_slots(land, name=f"sum_{tag}")

    def both_small(name):
        return jnp.stack([small[(name, 0)], small[(name, 1)]])

    small_list = [("ln_mix_g", both_small("ln_mix_g")), ("b_f_a", db_f[:, :N_MAIN_HEADS]), ("ln_kv_g", small["ln_kv_g"]),
                  ("ln_mem_g", both_small("ln_mem_g")), ("ln_ffn_g", both_small("ln_ffn_g")),
                  ("conv_w", both_small("conv_w")), ("conv_b", both_small("conv_b")), ("final_g", dg_final[0]),
                  ("loss", loss_part[0, :1])]
    sm_rows = []
    for _, a in small_list:
        flat = a.reshape(-1)
        sm_rows.append(_pad_rows(flat, _round_up(flat.size, 8 * LANES), 0).reshape(-1, LANES))
    spack = jnp.concatenate(sm_rows, axis=0)
    ssum = _sum_slots(_all_gather_small(spack, name="gather_small_grads"), name="sum_small")

    red = {}
    for n in ("w_in_a", "w_in_b", "w_kv"):
        red[n] = pieces[n].reshape(shards[n].shape)
    for n in ("w_memkv", "w_out", "w_up", "w_down"):
        red[n] = jnp.stack([pieces[(n, 0)], pieces[(n, 1)]])
    off = 0
    for (n, a), rows in zip(small_list, sm_rows):
        red[n] = ssum[off:off + rows.shape[0]].reshape(-1)[:a.size].reshape(a.shape)
        off += rows.shape[0]
    loss = red["loss"][0]
    shard_cols = conv_w.shape[2]
    red["conv_w"] = lax.dynamic_slice_in_dim(red["conv_w"], my_idx * shard_cols, shard_cols, axis=2)
    red["b_f_a"] = red["b_f_a"].reshape(b_f_a.shape)

    weights = {"ln_mix_g": ln_mix_g, "w_in_a": w_in_a, "b_f_a": b_f_a, "w_in_b": w_in_b, "ln_kv_g": ln_kv_g,
               "w_kv": w_kv, "ln_mem_g": ln_mem_g, "w_memkv": w_memkv, "w_out": w_out, "ln_ffn_g": ln_ffn_g,
               "w_up": w_up, "conv_w": conv_w, "conv_b": conv_b, "w_down": w_down, "final_g": final_g}
    m_in = {"ln_mix_g": m_ln_mix_g, "w_in_a": m_w_in_a, "b_f_a": m_b_f_a, "w_in_b": m_w_in_b, "ln_kv_g": m_ln_kv_g,
            "w_kv": m_w_kv, "ln_mem_g": m_ln_mem_g, "w_memkv": m_w_memkv, "w_out": m_w_out, "ln_ffn_g": m_ln_ffn_g,
            "w_up": m_w_up, "conv_w": m_conv_w, "conv_b": m_conv_b, "w_down": m_w_down, "final_g": m_final_g}
    v_in = {"ln_mix_g": v_ln_mix_g, "w_in_a": v_w_in_a, "b_f_a": v_b_f_a, "w_in_b": v_w_in_b, "ln_kv_g": v_ln_kv_g,
            "w_kv": v_w_kv, "ln_mem_g": v_ln_mem_g, "w_memkv": v_w_memkv, "w_out": v_w_out, "ln_ffn_g": v_ln_ffn_g,
            "w_up": v_w_up, "conv_w": v_conv_w, "conv_b": v_conv_b, "w_down": v_w_down, "final_g": v_final_g}
    order = list(weights)
    big_names = [n for n, _ in BIG]
    g_out, d_out, nm_out, nv_out = {}, {}, {}, {}
    for n in big_names + ["conv_w"]:
        w = weights[n]
        cols = w.shape[-1]
        g = red[n].reshape(w.shape)
        d, nm, nv = _adamw(w.reshape(-1, cols), g.reshape(-1, cols), m_in[n].reshape(-1, cols),
                           v_in[n].reshape(-1, cols), name=f"adamw_{n}")
        g_out[n], d_out[n], nm_out[n], nv_out[n] = g, d.reshape(w.shape), nm.reshape(w.shape), nv.reshape(w.shape)
    small_names = [n for n in order if n not in g_out]

    def pack_small(src):
        rows = []
        for n in small_names:
            flat = src[n].reshape(-1)
            rows.append(_pad_rows(flat, _round_up(flat.size, 8 * LANES), 0).reshape(-1, LANES))
        return jnp.concatenate(rows, axis=0), [r.shape[0] for r in rows]

    red_small = {n: red[n].reshape(weights[n].shape) for n in small_names}
    wp, counts = pack_small(weights)
    gp, _ = pack_small(red_small)
    mp, _ = pack_small(m_in)
    vp, _ = pack_small(v_in)
    dp, nmp, nvp = _adamw(wp, gp, mp, vp, name="adamw_small")
    off = 0
    for n, cnt in zip(small_names, counts):
        shp = weights[n].shape
        size = weights[n].size
        g_out[n] = red_small[n]
        d_out[n] = dp[off:off + cnt].reshape(-1)[:size].reshape(shp)
        nm_out[n] = nmp[off:off + cnt].reshape(-1)[:size].reshape(shp)
        nv_out[n] = nvp[off:off + cnt].reshape(-1)[:size].reshape(shp)
        off += cnt

    return (loss, grad_x, *[g_out[n] for n in order], *[d_out[n] for n in order],
            *[nm_out[n] for n in order], *[nv_out[n] for n in order])
```

```python
import functools

import jax
import jax.numpy as jnp
from jax import lax
from jax.experimental import pallas as pl
from jax.experimental.pallas import tpu as pltpu

F32 = jnp.float32
BF16 = jnp.bfloat16
LANES = 128
HEAD_DIM = 64
N_MAIN_HEADS = 12
N_MEM_HEADS = 4
MAIN_W = N_MAIN_HEADS * HEAD_DIM
MEM_W = N_MEM_HEADS * HEAD_DIM
SCALE = HEAD_DIM ** -0.5
EPS = 1e-6
NEG = -1e30
N_DEV = 8
ATT_TILE = 256
MEM_Q_TILE = 1024
VMEM_BIG = 56 * 2 ** 20
MESH = pl.DeviceIdType.MESH

ADAM_LR = 0.001
ADAM_B1 = 0.9
ADAM_B2 = 0.999
ADAM_EPS = 1e-08
ADAM_WD = 0.01
ADAM_STEP = 10

NT = (((1,), (1,)), ((), ()))
TN = (((0,), (0,)), ((), ()))


def _pc(body, *, name, out_shape, grid=None, in_specs=None, out_specs=None, scratch_shapes=(),
        semantics=None, vmem=None):
    kw = {}
    if grid is not None:
        kw["grid"] = grid
    params = pltpu.CompilerParams(dimension_semantics=semantics, vmem_limit_bytes=vmem)
    return pl.pallas_call(body, name=name, out_shape=out_shape, in_specs=in_specs, out_specs=out_specs,
                          scratch_shapes=list(scratch_shapes), compiler_params=params, **kw)


def _sds(shape, dtype):
    return jax.ShapeDtypeStruct(shape, dtype)


def _mm_fwd(a, w, *, name, tm, tn, out_dtype, g=None, res=None, col0=0, ncols=None, save_h=False):
    m_rows, k = a.shape
    n = w.shape[1] if ncols is None else ncols
    grid = (m_rows // tm, n // tn)
    norm = g is not None

    def body(*refs):
        refs = list(refs)
        a_ref = refs.pop(0)
        g_ref = refs.pop(0) if norm else None
        w_ref = refs.pop(0)
        res_ref = refs.pop(0) if res is not None else None
        o_ref = refs.pop(0)
        hout_ref = refs.pop(0) if save_h else None
        h_ref = refs.pop(0) if norm else None
        if norm:
            @pl.when(pl.program_id(1) == 0)
            def _():
                xv = a_ref[...]
                r = lax.rsqrt(jnp.mean(xv * xv, axis=-1, keepdims=True) + EPS)
                h = ((xv * r) * g_ref[...]).astype(BF16)
                h_ref[...] = h
                if save_h:
                    hout_ref[...] = h
            lhs = h_ref[...]
        else:
            lhs = a_ref[...].astype(BF16)
        acc = jnp.dot(lhs, w_ref[...], preferred_element_type=F32)
        if res is not None:
            acc = acc + res_ref[...]
        o_ref[...] = acc.astype(out_dtype)

    in_specs = [pl.BlockSpec((tm, k), lambda i, j: (i, 0))]
    args = [a]
    if norm:
        in_specs.append(pl.BlockSpec((1, k), lambda i, j: (0, 0)))
        args.append(g.reshape(1, k))
    in_specs.append(pl.BlockSpec((k, tn), lambda i, j: (0, j + col0)))
    args.append(w)
    if res is not None:
        in_specs.append(pl.BlockSpec((tm, tn), lambda i, j: (i, j)))
        args.append(res)
    out_shape = [_sds((m_rows, n), out_dtype)]
    out_specs = [pl.BlockSpec((tm, tn), lambda i, j: (i, j))]
    if save_h:
        out_shape.append(_sds((m_rows, k), BF16))
        out_specs.append(pl.BlockSpec((tm, k), lambda i, j: (i, 0)))
    scratch = [pltpu.VMEM((tm, k), BF16)] if norm else []
    outs = _pc(body, name=name, out_shape=out_shape, grid=grid, in_specs=in_specs, out_specs=out_specs,
               scratch_shapes=scratch, semantics=("arbitrary", "arbitrary"), vmem=VMEM_BIG)(*args)
    return outs if save_h else outs[0]


def _mm_nt(a, w, *, name, tm, tn, out_dtype):
    m_rows, k = a.shape
    n = w.shape[0]

    def body(a_ref, w_ref, o_ref):
        acc = lax.dot_general(a_ref[...].astype(BF16), w_ref[...], NT, preferred_element_type=F32)
        o_ref[...] = acc.astype(out_dtype)

    return _pc(body, name=name, out_shape=_sds((m_rows, n), out_dtype), grid=(m_rows // tm, n // tn),
               in_specs=[pl.BlockSpec((tm, k), lambda i, j: (i, 0)), pl.BlockSpec((tn, k), lambda i, j: (j, 0))],
               out_specs=pl.BlockSpec((tm, tn), lambda i, j: (i, j)),
               semantics=("arbitrary", "arbitrary"), vmem=VMEM_BIG)(a, w)


def _mm_tn(a, b, *, name, ta, tn, tt):
    t_rows, ka = a.shape
    n = b.shape[1]
    nt = t_rows // tt

    def body(a_ref, b_ref, o_ref, acc_ref):
        t = pl.program_id(2)

        @pl.when(t == 0)
        def _():
            acc_ref[...] = jnp.zeros_like(acc_ref)

        acc_ref[...] += lax.dot_general(a_ref[...].astype(BF16), b_ref[...].astype(BF16), TN,
                                        preferred_element_type=F32)

        @pl.when(t == nt - 1)
        def _():
            o_ref[...] = acc_ref[...].astype(BF16)

    return _pc(body, name=name, out_shape=_sds((ka, n), BF16), grid=(ka // ta, n // tn, nt),
               in_specs=[pl.BlockSpec((tt, ta), lambda i, j, t: (t, i)),
                         pl.BlockSpec((tt, tn), lambda i, j, t: (t, j))],
               out_specs=pl.BlockSpec((ta, tn), lambda i, j, t: (i, j)),
               scratch_shapes=[pltpu.VMEM((ta, tn), F32)],
               semantics=("arbitrary", "arbitrary", "arbitrary"), vmem=VMEM_BIG)(a, b)


def _wgrad(a, b, name):
    t_rows, ka = a.shape
    n = b.shape[1]
    ta = ka if ka <= 1024 else ka // 2
    tn = n
    while ta * tn * 4 > 6 * 2 ** 20 and tn % 256 == 0:
        tn //= 2
    tt = min(512, t_rows)
    return _mm_tn(a, b, name=name, ta=ta, tn=tn, tt=tt)


def _mm_nt_rmsbwd(parts, w, x, g, *, name, dres=None, want_dx=True):
    m_rows, d = x.shape
    tm = min(256, m_rows)
    n_parts = len(parts)

    def body(*refs):
        refs = list(refs)
        dy_refs = [refs.pop(0) for _ in range(n_parts)]
        w_refs = [refs.pop(0) for _ in range(n_parts)]
        x_ref = refs.pop(0)
        g_ref = refs.pop(0)
        dres_ref = refs.pop(0) if dres is not None else None
        dx_ref = refs.pop(0) if want_dx else None
        dg_ref = refs.pop(0)

        @pl.when(pl.program_id(0) == 0)
        def _():
            dg_ref[...] = jnp.zeros_like(dg_ref)

        dh = None
        for dy_ref, w_ref in zip(dy_refs, w_refs):
            t = lax.dot_general(dy_ref[...].astype(BF16), w_ref[...], NT, preferred_element_type=F32)
            dh = t if dh is None else dh + t
        xv = x_ref[...]
        r = lax.rsqrt(jnp.mean(xv * xv, axis=-1, keepdims=True) + EPS)
        xh = xv * r
        dg_ref[...] += jnp.sum(dh * xh, axis=0, keepdims=True)
        if want_dx:
            dhg = dh * g_ref[...]
            dx = r * (dhg - xh * jnp.mean(dhg * xh, axis=-1, keepdims=True))
            if dres is not None:
                dx = dx + dres_ref[...]
            dx_ref[...] = dx

    in_specs, args = [], []
    for dy, _ in parts:
        in_specs.append(pl.BlockSpec((tm, dy.shape[1]), lambda i: (i, 0)))
        args.append(dy)
    for dy, cb in parts:
        in_specs.append(pl.BlockSpec((d, dy.shape[1]), functools.partial(lambda i, cb: (0, cb), cb=cb)))
        args.append(w)
    in_specs += [pl.BlockSpec((tm, d), lambda i: (i, 0)), pl.BlockSpec((1, d), lambda i: (0, 0))]
    args += [x, g.reshape(1, d)]
    if dres is not None:
        in_specs.append(pl.BlockSpec((tm, d), lambda i: (i, 0)))
        args.append(dres)
    out_shape, out_specs = [], []
    if want_dx:
        out_shape.append(_sds((m_rows, d), F32))
        out_specs.append(pl.BlockSpec((tm, d), lambda i: (i, 0)))
    out_shape.append(_sds((1, d), F32))
    out_specs.append(pl.BlockSpec((1, d), lambda i: (0, 0)))
    outs = _pc(body, name=name, out_shape=out_shape, grid=(m_rows // tm,), in_specs=in_specs,
               out_specs=out_specs, semantics=("arbitrary",), vmem=VMEM_BIG)(*args)
    return (outs[0], outs[1]) if want_dx else (None, outs[0])


def _loss_head(x, g, tgt, *, name):
    m_rows, d = x.shape
    tm = min(256, m_rows)

    def body(x_ref, g_ref, t_ref, dx_ref, dg_ref, loss_ref):
        @pl.when(pl.program_id(0) == 0)
        def _():
            dg_ref[...] = jnp.zeros_like(dg_ref)
            loss_ref[...] = jnp.zeros_like(loss_ref)

        xv = x_ref[...]
        r = lax.rsqrt(jnp.mean(xv * xv, axis=-1, keepdims=True) + EPS)
        xh = xv * r
        gv = g_ref[...]
        err = xh * gv - t_ref[...]
        per_tok = jnp.mean(err * err, axis=-1, keepdims=True)
        loss_ref[...] += 0.5 * jnp.sum(per_tok, axis=0, keepdims=True)
        dout = err * (1.0 / d)
        dg_ref[...] += jnp.sum(dout * xh, axis=0, keepdims=True)
        dhg = dout * gv
        dx_ref[...] = r * (dhg - xh * jnp.mean(dhg * xh, axis=-1, keepdims=True))

    row = pl.BlockSpec((tm, d), lambda i: (i, 0))
    return _pc(body, name=name, out_shape=[_sds((m_rows, d), F32), _sds((1, d), F32), _sds((1, LANES), F32)],
               grid=(m_rows // tm,), in_specs=[row, pl.BlockSpec((1, d), lambda i: (0, 0)), row],
               out_specs=[row, pl.BlockSpec((1, d), lambda i: (0, 0)), pl.BlockSpec((1, LANES), lambda i: (0, 0))],
               semantics=("arbitrary",))(x, g.reshape(1, d), tgt)


def _split3(v):
    hi = v.astype(BF16)
    r1 = v - hi.astype(F32)
    mid = r1.astype(BF16)
    lo = (r1 - mid.astype(F32)).astype(BF16)
    return hi, mid, lo


def _split2(v):
    hi = v.astype(BF16)
    lo = (v - hi.astype(F32)).astype(BF16)
    return hi, lo


def _tri_dot3(tri, v):
    hi, mid, lo = _split3(v)
    return (jnp.dot(tri, hi, preferred_element_type=F32) + jnp.dot(tri, mid, preferred_element_type=F32)
            + jnp.dot(tri, lo, preferred_element_type=F32))


def _dot_tri2(v, tri):
    hi, lo = _split2(v)
    return jnp.dot(hi, tri, preferred_element_type=F32) + jnp.dot(lo, tri, preferred_element_type=F32)


def _log_sigmoid(v):
    return jnp.minimum(v, 0.0) - jnp.log(1.0 + jnp.exp(-jnp.abs(v)))


def _forget_cumsum(f_logit, b_f, *, B, S, name):
    ch = min(256, S)
    nch = S // ch

    def body(f_ref, b_ref, c_ref):
        r_i = lax.broadcasted_iota(jnp.int32, (ch, ch), 0)
        c_i = lax.broadcasted_iota(jnp.int32, (ch, ch), 1)
        tri = (c_i <= r_i).astype(BF16)
        bv = b_ref[...]

        def step(k, carry):
            rows = pl.ds(pl.multiple_of(k * ch, ch), ch)
            lf = _log_sigmoid(f_ref[rows, :] + bv)
            c_ref[rows, :] = _tri_dot3(tri, lf) + carry
            return carry + jnp.sum(lf, axis=0, keepdims=True)

        lax.fori_loop(0, nch, step, jnp.zeros((1, LANES), F32))

    blk = pl.BlockSpec((S, LANES), lambda b: (b, 0))
    return _pc(body, name=name, out_shape=_sds((B * S, LANES), F32), grid=(B,),
               in_specs=[blk, pl.BlockSpec((1, LANES), lambda b: (0, 0))], out_specs=blk,
               semantics=("arbitrary",))(f_logit, b_f)


def _forget_cumsum_bwd(dc, f_logit, b_f, *, B, S, name):
    ch = min(256, S)
    nch = S // ch

    def body(dc_ref, f_ref, b_ref, df_ref, db_ref):
        @pl.when(pl.program_id(0) == 0)
        def _():
            db_ref[...] = jnp.zeros_like(db_ref)

        r_i = lax.broadcasted_iota(jnp.int32, (ch, ch), 0)
        c_i = lax.broadcasted_iota(jnp.int32, (ch, ch), 1)
        tri = (c_i >= r_i).astype(BF16)
        bv = b_ref[...]

        def step(kk, carry):
            tail, dbs = carry
            k = nch - 1 - kk
            rows = pl.ds(pl.multiple_of(k * ch, ch), ch)
            dcv = dc_ref[rows, :]
            dlf = _tri_dot3(tri, dcv) + tail
            z = f_ref[rows, :] + bv
            df = dlf * (1.0 / (1.0 + jnp.exp(z)))
            df_ref[rows, :] = df.astype(BF16)
            return tail + jnp.sum(dcv, axis=0, keepdims=True), dbs + jnp.sum(df, axis=0, keepdims=True)

        zero = jnp.zeros((1, LANES), F32)
        _, dbs = lax.fori_loop(0, nch, step, (zero, zero))
        db_ref[...] += dbs

    blk = pl.BlockSpec((S, LANES), lambda b: (b, 0))
    one = pl.BlockSpec((1, LANES), lambda b: (0, 0))
    return _pc(body, name=name, out_shape=[_sds((B * S, LANES), BF16), _sds((1, LANES), F32)], grid=(B,),
               in_specs=[blk, blk, one], out_specs=[blk, one], semantics=("arbitrary",))(dc, f_logit, b_f)


def _head_mask(lane, hh):
    return (lane < HEAD_DIM) if hh == 0 else (lane >= HEAD_DIM)


def _col_spec(rows, nblk_rows, cb):
    return pl.BlockSpec((rows, LANES), lambda b, p, i: (b * nblk_rows + i, cb + p))


def _kv_spec(rows, cb):
    return pl.BlockSpec((rows, LANES), lambda b, p, i: (b, cb + p))


def _stat_col_spec(tq):
    return pl.BlockSpec((1, 2, tq, 1), lambda b, p, i: (b, p, i, 0))


def _stat_row_spec(S):
    return pl.BlockSpec((1, 2, 1, S), lambda b, p, i: (b, p, 0, 0))


def _softmax_fwd(qa, ka, va, *, name, B, S, Sk, P, q_cb, k_cb, v_cb, causal, cc=None, cr=None):
    tq = min(ATT_TILE if causal else MEM_Q_TILE, S)
    tk = min(ATT_TILE, Sk)
    nq, nk = S // tq, Sk // tk
    decay = cc is not None
    assert not causal or (tq == tk and S == Sk)

    def body(*refs):
        if decay:
            q_ref, k_ref, v_ref, cc_ref, cr_ref, o_ref, lse_ref = refs
        else:
            q_ref, k_ref, v_ref, o_ref, lse_ref = refs
        i = pl.program_id(2)
        q = q_ref[...]
        lane = lax.broadcasted_iota(jnp.int32, (tq, LANES), 1)
        row = lax.broadcasted_iota(jnp.int32, (tq, tk), 0) + i * tq
        col0 = lax.broadcasted_iota(jnp.int32, (tq, tk), 1)
        outs = []
        for hh in range(2):
            qh = jnp.where(_head_mask(lane, hh), q, jnp.zeros_like(q))

            def step(kb, carry, hh=hh, qh=qh):
                m, l, acc = carry
                ks = pl.multiple_of(kb * tk, tk)
                kblk = k_ref[pl.ds(ks, tk), :]
                vblk = v_ref[pl.ds(ks, tk), :]
                s = lax.dot_general(qh, kblk, NT, preferred_element_type=F32) * SCALE
                if decay:
                    s = s + (cc_ref[0, hh] - cr_ref[0, hh, :, pl.ds(ks, tk)])
                if causal:
                    s = jnp.where(col0 + kb * tk <= row, s, NEG)
                m_new = jnp.maximum(m, jnp.max(s, axis=-1, keepdims=True))
                alpha = jnp.exp(m - m_new)
                p = jnp.exp(s - m_new)
                l = alpha * l + jnp.sum(p, axis=-1, keepdims=True)
                acc = alpha * acc + jnp.dot(p.astype(BF16), vblk, preferred_element_type=F32)
                return m_new, l, acc

            init = (jnp.full((tq, 1), NEG, F32), jnp.zeros((tq, 1), F32), jnp.zeros((tq, LANES), F32))
            m, l, acc = lax.fori_loop(0, (i + 1) if causal else nk, step, init)
            outs.append(acc / l)
            lse_ref[0, hh] = m + jnp.log(l)
        o_ref[...] = jnp.where(lane < HEAD_DIM, outs[0], outs[1]).astype(BF16)

    in_specs = [_col_spec(tq, nq, q_cb), _kv_spec(Sk, k_cb), _kv_spec(Sk, v_cb)]
    args = [qa, ka, va]
    if decay:
        in_specs += [_stat_col_spec(tq), _stat_row_spec(S)]
        args += [cc, cr]
    return _pc(body, name=name,
               out_shape=[_sds((B * S, P * LANES), BF16), _sds((B, 2 * P, S, 1), F32)],
               grid=(B, P, nq), in_specs=in_specs, out_specs=[_col_spec(tq, nq, 0), _stat_col_spec(tq)],
               semantics=("arbitrary", "arbitrary", "arbitrary"), vmem=VMEM_BIG)(*args)


def _softmax_bwd(qa, ka, va, doa, oa, lse, *, name, B, S, Sk, P, q_cb, k_cb, v_cb, do_cb, causal,
                 cc=None, cr=None):
    tq = min(ATT_TILE if causal else MEM_Q_TILE, S)
    tk = min(ATT_TILE, Sk)
    nq, nk = S // tq, Sk // tk
    decay = cc is not None

    def body(*refs):
        if decay:
            q_ref, k_ref, v_ref, do_ref, o_ref, lse_ref, cc_ref, cr_ref, dq_ref, dk_ref, dv_ref, dcs_ref = refs
        else:
            q_ref, k_ref, v_ref, do_ref, o_ref, lse_ref, dq_ref, dk_ref, dv_ref = refs
        i = pl.program_id(2)

        @pl.when(i == 0)
        def _():
            dk_ref[...] = jnp.zeros_like(dk_ref)
            dv_ref[...] = jnp.zeros_like(dv_ref)
            if decay:
                dcs_ref[...] = jnp.zeros_like(dcs_ref)

        q = q_ref[...]
        do = do_ref[...]
        prod = do.astype(F32) * o_ref[...].astype(F32)
        lane = lax.broadcasted_iota(jnp.int32, (tq, LANES), 1)
        row = lax.broadcasted_iota(jnp.int32, (tq, tk), 0) + i * tq
        col0 = lax.broadcasted_iota(jnp.int32, (tq, tk), 1)
        dqs = []
        for hh in range(2):
            hmask = _head_mask(lane, hh)
            qh = jnp.where(hmask, q, jnp.zeros_like(q))
            doh = jnp.where(hmask, do, jnp.zeros_like(do))
            lse_h = lse_ref[0, hh]
            n_blocks = (i + 1) if causal else nk

            def probs(kb, hh=hh, qh=qh, doh=doh, lse_h=lse_h):
                ks = pl.multiple_of(kb * tk, tk)
                kblk = k_ref[pl.ds(ks, tk), :]
                vblk = v_ref[pl.ds(ks, tk), :]
                s = lax.dot_general(qh, kblk, NT, preferred_element_type=F32) * SCALE
                if decay:
                    s = s + (cc_ref[0, hh] - cr_ref[0, hh, :, pl.ds(ks, tk)])
                if causal:
                    s = jnp.where(col0 + kb * tk <= row, s, NEG)
                p = jnp.exp(s - lse_h)
                dp = lax.dot_general(doh, vblk, NT, preferred_element_type=F32)
                return ks, kblk, p, dp

            if decay:
                def delta_step(kb, acc):
                    _, _, p, dp = probs(kb)
                    return acc + jnp.sum(p * dp, axis=-1, keepdims=True)

                delta = lax.fori_loop(0, n_blocks, delta_step, jnp.zeros((tq, 1), F32))
            else:
                delta = jnp.sum(jnp.where(hmask, prod, 0.0), axis=-1, keepdims=True)

            def step(kb, dq_acc, hh=hh, qh=qh, doh=doh, delta=delta):
                ks, kblk, p, dp = probs(kb)
                ds = p * (dp - delta)
                dsb = ds.astype(BF16)
                dk_ref[pl.ds(ks, tk), :] += lax.dot_general(dsb, qh, TN, preferred_element_type=F32) * SCALE
                dv_ref[pl.ds(ks, tk), :] += lax.dot_general(p.astype(BF16), doh, TN, preferred_element_type=F32)
                if decay:
                    dcs_ref[0, hh, :, pl.ds(ks, tk)] -= jnp.sum(ds, axis=0, keepdims=True)
                return dq_acc + jnp.dot(dsb, kblk, preferred_element_type=F32)

            dqs.append(lax.fori_loop(0, n_blocks, step, jnp.zeros((tq, LANES), F32)) * SCALE)
        dq_ref[...] = jnp.where(lane < HEAD_DIM, dqs[0], dqs[1]).astype(BF16)

    in_specs = [_col_spec(tq, nq, q_cb), _kv_spec(Sk, k_cb), _kv_spec(Sk, v_cb), _col_spec(tq, nq, do_cb),
                _col_spec(tq, nq, 0), _stat_col_spec(tq)]
    args = [qa, ka, va, doa, oa, lse]
    out_shape = [_sds((B * S, P * LANES), BF16), _sds((B * Sk, P * LANES), F32), _sds((B * Sk, P * LANES), F32)]
    out_specs = [_col_spec(tq, nq, 0), _kv_spec(Sk, 0), _kv_spec(Sk, 0)]
    if decay:
        in_specs += [_stat_col_spec(tq), _stat_row_spec(S)]
        args += [cc, cr]
        out_shape.append(_sds((B, 2 * P, 1, S), F32))
        out_specs.append(_stat_row_spec(S))
    return _pc(body, name=name, out_shape=out_shape, grid=(B, P, nq), in_specs=in_specs, out_specs=out_specs,
               semantics=("arbitrary", "arbitrary", "arbitrary"), vmem=VMEM_BIG)(*args)


def _sb_terms(qh, kblk, row, col0, kb, tk):
    z = lax.dot_general(qh, kblk, NT, preferred_element_type=F32) * SCALE
    causal = (col0 + kb * tk) < row
    sp = jnp.maximum(z, 0.0) + jnp.log(1.0 + jnp.exp(-jnp.abs(z)))
    ls = z - sp
    lm = jnp.where(causal, -sp, 0.0)
    return causal, ls, lm


def _stickbreak_fwd(qa, ka, va, *, name, B, S, P, q_cb, k_cb, v_cb):
    tq = tk = min(ATT_TILE, S)
    nq = S // tq

    def body(q_ref, k_ref, v_ref, o_ref, rt_ref):
        i = pl.program_id(2)
        q = q_ref[...]
        lane = lax.broadcasted_iota(jnp.int32, (tq, LANES), 1)
        row = lax.broadcasted_iota(jnp.int32, (tq, tk), 0) + i * tq
        col0 = lax.broadcasted_iota(jnp.int32, (tq, tk), 1)
        t_r = lax.broadcasted_iota(jnp.int32, (tk, tk), 0)
        t_c = lax.broadcasted_iota(jnp.int32, (tk, tk), 1)
        after = (t_r > t_c).astype(BF16)
        outs = []
        for hh in range(2):
            qh = jnp.where(_head_mask(lane, hh), q, jnp.zeros_like(q))

            def step(jj, carry, qh=qh):
                run, acc = carry
                kb = i - jj
                ks = pl.multiple_of(kb * tk, tk)
                kblk = k_ref[pl.ds(ks, tk), :]
                vblk = v_ref[pl.ds(ks, tk), :]
                causal, ls, lm = _sb_terms(qh, kblk, row, col0, kb, tk)
                suf = _dot_tri2(lm, after)
                a = jnp.where(causal, jnp.exp(ls + run + suf), 0.0)
                acc = acc + jnp.dot(a.astype(BF16), vblk, preferred_element_type=F32)
                return run + jnp.sum(lm, axis=-1, keepdims=True), acc

            run, acc = lax.fori_loop(0, i + 1, step, (jnp.zeros((tq, 1), F32), jnp.zeros((tq, LANES), F32)))
            outs.append(acc)
            rt_ref[0, hh] = run
        o_ref[...] = jnp.where(lane < HEAD_DIM, outs[0], outs[1]).astype(BF16)

    return _pc(body, name=name, out_shape=[_sds((B * S, P * LANES), BF16), _sds((B, 2 * P, S, 1), F32)],
               grid=(B, P, nq), in_specs=[_col_spec(tq, nq, q_cb), _kv_spec(S, k_cb), _kv_spec(S, v_cb)],
               out_specs=[_col_spec(tq, nq, 0), _stat_col_spec(tq)],
               semantics=("arbitrary", "arbitrary", "arbitrary"), vmem=VMEM_BIG)(qa, ka, va)


def _stickbreak_bwd(qa, ka, va, doa, rt, *, name, B, S, P, q_cb, k_cb, v_cb, do_cb):
    tq = tk = min(ATT_TILE, S)
    nq = S // tq

    def body(q_ref, k_ref, v_ref, do_ref, rt_ref, dq_ref, dk_ref, dv_ref):
        i = pl.program_id(2)

        @pl.when(i == 0)
        def _():
            dk_ref[...] = jnp.zeros_like(dk_ref)
            dv_ref[...] = jnp.zeros_like(dv_ref)

        q = q_ref[...]
        do = do_ref[...]
        lane = lax.broadcasted_iota(jnp.int32, (tq, LANES), 1)
        row = lax.broadcasted_iota(jnp.int32, (tq, tk), 0) + i * tq
        col0 = lax.broadcasted_iota(jnp.int32, (tq, tk), 1)
        t_r = lax.broadcasted_iota(jnp.int32, (tk, tk), 0)
        t_c = lax.broadcasted_iota(jnp.int32, (tk, tk), 1)
        upto = (t_r <= t_c).astype(BF16)
        before = (t_r < t_c).astype(BF16)
        dqs = []
        for hh in range(2):
            hmask = _head_mask(lane, hh)
            qh = jnp.where(hmask, q, jnp.zeros_like(q))
            doh = jnp.where(hmask, do, jnp.zeros_like(do))
            rt_h = rt_ref[0, hh]

            def step(kb, carry, qh=qh, doh=doh, rt_h=rt_h):
                pl_sum, pg_sum, dq_acc = carry
                ks = pl.multiple_of(kb * tk, tk)
                kblk = k_ref[pl.ds(ks, tk), :]
                vblk = v_ref[pl.ds(ks, tk), :]
                causal, ls, lm = _sb_terms(qh, kblk, row, col0, kb, tk)
                pin = _dot_tri2(lm, upto)
                a = jnp.where(causal, jnp.exp(ls + (rt_h - pl_sum) - pin), 0.0)
                da = lax.dot_general(doh, vblk, NT, preferred_element_type=F32)
                gm = a * da
                pg = _dot_tri2(gm, before) + pg_sum
                beta = jnp.exp(ls)
                dz = jnp.where(causal, gm * (1.0 - beta) - pg * beta, 0.0)
                dzb = dz.astype(BF16)
                dk_ref[pl.ds(ks, tk), :] += lax.dot_general(dzb, qh, TN, preferred_element_type=F32) * SCALE
                dv_ref[pl.ds(ks, tk), :] += lax.dot_general(a.astype(BF16), doh, TN, preferred_element_type=F32)
                return (pl_sum + jnp.sum(lm, axis=-1, keepdims=True),
                        pg_sum + jnp.sum(gm, axis=-1, keepdims=True),
                        dq_acc + jnp.dot(dzb, kblk, preferred_element_type=F32))

            zc = jnp.zeros((tq, 1), F32)
            _, _, dq_h = lax.fori_loop(0, i + 1, step, (zc, zc, jnp.zeros((tq, LANES), F32)))
            dqs.append(dq_h * SCALE)
        dq_ref[...] = jnp.where(lane < HEAD_DIM, dqs[0], dqs[1]).astype(BF16)

    return _pc(body, name=name,
               out_shape=[_sds((B * S, P * LANES), BF16), _sds((B * S, P * LANES), F32), _sds((B * S, P * LANES), F32)],
               grid=(B, P, nq),
               in_specs=[_col_spec(tq, nq, q_cb), _kv_spec(S, k_cb), _kv_spec(S, v_cb), _col_spec(tq, nq, do_cb),
                         _stat_col_spec(tq)],
               out_specs=[_col_spec(tq, nq, 0), _kv_spec(S, 0), _kv_spec(S, 0)],
               semantics=("arbitrary", "arbitrary", "arbitrary"), vmem=VMEM_BIG)(qa, ka, va, doa, rt)


def _pair_setup(q_ref, tq, tk):
    q = q_ref[...] * jnp.asarray(SCALE, BF16)
    lane = lax.broadcasted_iota(jnp.int32, (tq, LANES), 1)
    masks = [_head_mask(lane, hh) for hh in range(2)]
    qh = [jnp.where(mk, q, jnp.zeros_like(q)) for mk in masks]
    on_or_below = (lax.broadcasted_iota(jnp.int32, (tq, tk), 1) <= lax.broadcasted_iota(jnp.int32, (tq, tk), 0))
    return lane, masks, qh, on_or_below


def _fox_fwd(qa, ka, va, cr, *, name, B, S, P, q_cb, k_cb, v_cb):
    tq = tk = min(ATT_TILE, S)
    nq = S // tq

    def body(q_ref, k_ref, v_ref, cr_ref, o_ref, lse_ref, s_buf, acc_ref, m_ref, l_ref):
        i = pl.program_id(2)
        lane, _, qh, on_or_below = _pair_setup(q_ref, tq, tk)
        m_ref[...] = jnp.full(m_ref.shape, NEG, F32)
        l_ref[...] = jnp.zeros(l_ref.shape, F32)
        acc_ref[...] = jnp.zeros(acc_ref.shape, F32)

        def scores(kb, slot):
            kblk = k_ref[pl.ds(pl.multiple_of(kb * tk, tk), tk), :]
            for hh in range(2):
                s_buf[slot, hh] = lax.dot_general(qh[hh], kblk, NT, preferred_element_type=F32)

        def block(kb, slot, diag):
            ks = pl.multiple_of(kb * tk, tk)
            vblk = v_ref[pl.ds(ks, tk), :]
            ps = []
            for hh in range(2):
                s = s_buf[slot, hh] - cr_ref[0, hh, :, pl.ds(ks, tk)]
                if diag:
                    s = jnp.where(on_or_below, s, NEG)
                m = m_ref[hh]
                m_new = jnp.maximum(m, jnp.max(s, axis=-1, keepdims=True))
                alpha = jnp.exp(m - m_new)
                p = jnp.exp(s - m_new)
                m_ref[hh] = m_new
                l_ref[hh] = alpha * l_ref[hh] + jnp.sum(p, axis=-1, keepdims=True)
                ps.append((alpha, p.astype(BF16)))
            for hh in range(2):
                acc_ref[hh] = ps[hh][0] * acc_ref[hh] + jnp.dot(ps[hh][1], vblk, preferred_element_type=F32)

        def step(kb, _):
            slot = lax.rem(kb, 2)
            scores(kb + 1, 1 - slot)
            block(kb, slot, False)
            return 0

        scores(0, 0)
        lax.fori_loop(0, i, step, 0)
        block(i, lax.rem(i, 2), True)
        l0, l1 = l_ref[0], l_ref[1]
        lse_ref[0, 0] = m_ref[0] + jnp.log(l0)
        lse_ref[0, 1] = m_ref[1] + jnp.log(l1)
        o_ref[...] = jnp.where(lane < HEAD_DIM, acc_ref[0] / l0, acc_ref[1] / l1).astype(BF16)

    return _pc(body, name=name, out_shape=[_sds((B * S, P * LANES), BF16), _sds((B, 2 * P, S, 1), F32)],
               grid=(B, P, nq),
               in_specs=[_col_spec(tq, nq, q_cb), _kv_spec(S, k_cb), _kv_spec(S, v_cb), _stat_row_spec(S)],
               out_specs=[_col_spec(tq, nq, 0), _stat_col_spec(tq)],
               scratch_shapes=[pltpu.VMEM((2, 2, tq, tk), F32), pltpu.VMEM((2, tq, LANES), F32),
                               pltpu.VMEM((2, tq, 1), F32), pltpu.VMEM((2, tq, 1), F32)],
               semantics=("arbitrary", "arbitrary", "arbitrary"), vmem=VMEM_BIG)(qa, ka, va, cr)


def _fox_bwd(qa, ka, va, doa, lse, cr, *, name, B, S, P, q_cb, k_cb, v_cb, do_cb):
    tq = tk = min(ATT_TILE, S)
    nq = S // tq

    def body(q_ref, k_ref, v_ref, do_ref, lse_ref, cr_ref, dq_ref, dk_ref, dv_ref, dcs_ref):
        i = pl.program_id(2)

        @pl.when(i == 0)
        def _():
            dk_ref[...] = jnp.zeros_like(dk_ref)
            dv_ref[...] = jnp.zeros_like(dv_ref)
            dcs_ref[...] = jnp.zeros_like(dcs_ref)

        lane, masks, qh, on_or_below = _pair_setup(q_ref, tq, tk)
        do = do_ref[...]
        doh = [jnp.where(mk, do, jnp.zeros_like(do)) for mk in masks]
        lse_h = [lse_ref[0, hh] for hh in range(2)]

        def probs(kb, hh, diag):
            ks = pl.multiple_of(kb * tk, tk)
            kblk = k_ref[pl.ds(ks, tk), :]
            vblk = v_ref[pl.ds(ks, tk), :]
            s = lax.dot_general(qh[hh], kblk, NT, preferred_element_type=F32) - cr_ref[0, hh, :, pl.ds(ks, tk)]
            if diag:
                s = jnp.where(on_or_below, s, NEG)
            p = jnp.exp(s - lse_h[hh])
            dp = lax.dot_general(doh[hh], vblk, NT, preferred_element_type=F32)
            return ks, kblk, p, dp

        def delta_block(kb, carry, diag):
            out = []
            for hh in range(2):
                _, _, p, dp = probs(kb, hh, diag)
                out.append(carry[hh] + jnp.sum(p * dp, axis=-1, keepdims=True))
            return tuple(out)

        zc = jnp.zeros((tq, 1), F32)
        delta = lax.fori_loop(0, i, lambda kb, c: delta_block(kb, c, False), (zc, zc))
        delta = delta_block(i, delta, True)

        def grad_block(kb, carry, diag):
            out = []
            for hh in range(2):
                ks, kblk, p, dp = probs(kb, hh, diag)
                ds = p * (dp - delta[hh])
                dsb = ds.astype(BF16)
                rows = pl.ds(ks, tk)
                dk_ref[rows, :] += lax.dot_general(dsb, qh[hh], TN, preferred_element_type=F32)
                dv_ref[rows, :] += lax.dot_general(p.astype(BF16), doh[hh], TN, preferred_element_type=F32)
                dcs_ref[0, hh, :, rows] -= jnp.sum(ds, axis=0, keepdims=True)
                out.append(carry[hh] + jnp.dot(dsb, kblk, preferred_element_type=F32))
            return tuple(out)

        za = jnp.zeros((tq, LANES), F32)
        dq = lax.fori_loop(0, i, lambda kb, c: grad_block(kb, c, False), (za, za))
        dq = grad_block(i, dq, True)
        dq_ref[...] = (jnp.where(lane < HEAD_DIM, dq[0], dq[1]) * SCALE).astype(BF16)

    return _pc(body, name=name,
               out_shape=[_sds((B * S, P * LANES), BF16), _sds((B * S, P * LANES), F32), _sds((B * S, P * LANES), F32),
                          _sds((B, 2 * P, 1, S), F32)],
               grid=(B, P, nq),
               in_specs=[_col_spec(tq, nq, q_cb), _kv_spec(S, k_cb), _kv_spec(S, v_cb), _col_spec(tq, nq, do_cb),
                         _stat_col_spec(tq), _stat_row_spec(S)],
               out_specs=[_col_spec(tq, nq, 0), _kv_spec(S, 0), _kv_spec(S, 0), _stat_row_spec(S)],
               semantics=("arbitrary", "arbitrary", "arbitrary"), vmem=VMEM_BIG)(qa, ka, va, doa, lse, cr)


def _sb_logs(qh, kblk):
    z = lax.dot_general(qh, kblk, NT, preferred_element_type=F32)
    nz = -z
    lg = jnp.log(1.0 + jnp.exp(jnp.minimum(z, nz)))
    lm = jnp.minimum(nz, 0.0) - lg
    return lm + z, lm


def _sb_fwd(qa, ka, va, *, name, B, S, P, q_cb, k_cb, v_cb):
    tq = tk = min(ATT_TILE, S)
    nq = S // tq

    def body(q_ref, k_ref, v_ref, o_ref, rt_ref):
        i = pl.program_id(2)
        lane, _, qh, on_or_below = _pair_setup(q_ref, tq, tk)
        t_r = lax.broadcasted_iota(jnp.int32, (tk, tk), 0)
        t_c = lax.broadcasted_iota(jnp.int32, (tk, tk), 1)
        after = (t_r > t_c).astype(BF16)
        below = t_c < t_r

        def block(kb, carry, diag):
            ks = pl.multiple_of(kb * tk, tk)
            kblk = k_ref[pl.ds(ks, tk), :]
            vblk = v_ref[pl.ds(ks, tk), :]
            out = []
            for hh in range(2):
                run, acc = carry[hh]
                ls, lm = _sb_logs(qh[hh], kblk)
                if diag:
                    lm = jnp.where(below, lm, 0.0)
                a = jnp.exp(ls + run + _dot_tri2(lm, after))
                if diag:
                    a = jnp.where(below, a, 0.0)
                acc = acc + jnp.dot(a.astype(BF16), vblk, preferred_element_type=F32)
                out.append((run + jnp.sum(lm, axis=-1, keepdims=True), acc))
            return tuple(out)

        one = (jnp.zeros((tq, 1), F32), jnp.zeros((tq, LANES), F32))
        carry = block(i, (one, one), True)
        (r0, a0), (r1, a1) = lax.fori_loop(0, i, lambda jj, c: block(i - 1 - jj, c, False), carry)
        rt_ref[0, 0] = r0
        rt_ref[0, 1] = r1
        o_ref[...] = jnp.where(lane < HEAD_DIM, a0, a1).astype(BF16)

    return _pc(body, name=name, out_shape=[_sds((B * S, P * LANES), BF16), _sds((B, 2 * P, S, 1), F32)],
               grid=(B, P, nq), in_specs=[_col_spec(tq, nq, q_cb), _kv_spec(S, k_cb), _kv_spec(S, v_cb)],
               out_specs=[_col_spec(tq, nq, 0), _stat_col_spec(tq)],
               semantics=("arbitrary", "arbitrary", "arbitrary"), vmem=VMEM_BIG)(qa, ka, va)


def _sb_bwd(qa, ka, va, doa, rt, *, name, B, S, P, q_cb, k_cb, v_cb, do_cb):
    tq = tk = min(ATT_TILE, S)
    nq = S // tq

    def body(q_ref, k_ref, v_ref, do_ref, rt_ref, dq_ref, dk_ref, dv_ref):
        i = pl.program_id(2)

        @pl.when(i == 0)
        def _():
            dk_ref[...] = jnp.zeros_like(dk_ref)
            dv_ref[...] = jnp.zeros_like(dv_ref)

        lane, masks, qh, _ = _pair_setup(q_ref, tq, tk)
        do = do_ref[...]
        doh = [jnp.where(mk, do, jnp.zeros_like(do)) for mk in masks]
        rt_h = [rt_ref[0, hh] for hh in range(2)]
        t_r = lax.broadcasted_iota(jnp.int32, (tk, tk), 0)
        t_c = lax.broadcasted_iota(jnp.int32, (tk, tk), 1)
        upto = (t_r <= t_c).astype(BF16)
        before = (t_r < t_c).astype(BF16)
        below = t_c < t_r

        def block(kb, carry, diag):
            ks = pl.multiple_of(kb * tk, tk)
            rows = pl.ds(ks, tk)
            kblk = k_ref[rows, :]
            vblk = v_ref[rows, :]
            out = []
            for hh in range(2):
                pl_sum, pg_sum, dq_acc = carry[hh]
                ls, lm = _sb_logs(qh[hh], kblk)
                if diag:
                    lm = jnp.where(below, lm, 0.0)
                a = jnp.exp(ls + (rt_h[hh] - pl_sum) - _dot_tri2(lm, upto))
                if diag:
                    a = jnp.where(below, a, 0.0)
                gm = a * lax.dot_general(doh[hh], vblk, NT, preferred_element_type=F32)
                pg = _dot_tri2(gm, before) + pg_sum
                dz = gm - jnp.exp(ls) * (gm + pg)
                if diag:
                    dz = jnp.where(below, dz, 0.0)
                dzb = dz.astype(BF16)
                dk_ref[rows, :] += lax.dot_general(dzb, qh[hh], TN, preferred_element_type=F32)
                dv_ref[rows, :] += lax.dot_general(a.astype(BF16), doh[hh], TN, preferred_element_type=F32)
                out.append((pl_sum + jnp.sum(lm, axis=-1, keepdims=True),
                            pg_sum + jnp.sum(gm, axis=-1, keepdims=True),
                            dq_acc + jnp.dot(dzb, kblk, preferred_element_type=F32)))
            return tuple(out)

        zc = jnp.zeros((tq, 1), F32)
        one = (zc, zc, jnp.zeros((tq, LANES), F32))
        carry = lax.fori_loop(0, i, lambda kb, c: block(kb, c, False), (one, one))
        (_, _, dq0), (_, _, dq1) = block(i, carry, True)
        dq_ref[...] = (jnp.where(lane < HEAD_DIM, dq0, dq1) * SCALE).astype(BF16)

    return _pc(body, name=name,
               out_shape=[_sds((B * S, P * LANES), BF16), _sds((B * S, P * LANES), F32), _sds((B * S, P * LANES), F32)],
               grid=(B, P, nq),
               in_specs=[_col_spec(tq, nq, q_cb), _kv_spec(S, k_cb), _kv_spec(S, v_cb), _col_spec(tq, nq, do_cb),
                         _stat_col_spec(tq)],
               out_specs=[_col_spec(tq, nq, 0), _kv_spec(S, 0), _kv_spec(S, 0)],
               semantics=("arbitrary", "arbitrary", "arbitrary"), vmem=VMEM_BIG)(qa, ka, va, doa, rt)


HEAD_GROUP = 3


def _g_col_spec(rows, nblk_rows, cb, G):
    return pl.BlockSpec((rows, G * LANES), lambda b, p, i: (b * nblk_rows + i, cb // G + p))


def _g_kv_spec(rows, cb, G):
    return pl.BlockSpec((rows, G * LANES), lambda b, p, i: (b, cb // G + p))


def _g_stat_col_spec(tq, G):
    return pl.BlockSpec((1, 2 * G, tq, 1), lambda b, p, i: (b, p, i, 0))


def _g_stat_row_spec(S, G):
    return pl.BlockSpec((1, 2 * G, 1, S), lambda b, p, i: (b, p, 0, 0))


def _lanes(g):
    return slice(g * LANES, (g + 1) * LANES)


def _streams(x_ref, G, scale=None):
    rows = x_ref.shape[0]
    lane = lax.broadcasted_iota(jnp.int32, (rows, LANES), 1)
    out = []
    for g in range(G):
        x = x_ref[:, _lanes(g)]
        if scale is not None:
            x = x * jnp.asarray(scale, x.dtype)
        for hh in range(2):
            out.append(jnp.where(_head_mask(lane, hh), x, jnp.zeros_like(x)))
    return lane, out


def _wide(stat, width):
    return jnp.tile(stat, (1, width // LANES))


def _fold_lanes(v):
    out = v[:, :LANES]
    for j in range(1, v.shape[1] // LANES):
        out = out + v[:, j * LANES:(j + 1) * LANES]
    return out


def _kv_blocks(ref, ks, tk, G):
    return [ref[pl.ds(ks, tk), _lanes(g)] for g in range(G)]


def _sweep(i, block):
    def step(kb, c):
        block(kb, False)
        return c
    lax.fori_loop(0, i, step, 0)
    block(i, True)


def _fox_fwd_g(qa, ka, va, cr, *, name, B, S, P, q_cb, k_cb, v_cb, G=HEAD_GROUP):
    tq = tk = min(ATT_TILE, S)
    nq = S // tq
    NS = 2 * G

    def body(q_ref, k_ref, v_ref, cr_ref, o_ref, lse_ref, acc_ref, m_ref, l_ref):
        i = pl.program_id(2)
        lane, qh = _streams(q_ref, G, SCALE)
        on_or_below = (lax.broadcasted_iota(jnp.int32, (tq, tk), 1) <= lax.broadcasted_iota(jnp.int32, (tq, tk), 0))
        m_ref[...] = jnp.full(m_ref.shape, NEG, F32)
        l_ref[...] = jnp.zeros(l_ref.shape, F32)
        acc_ref[...] = jnp.zeros(acc_ref.shape, F32)

        def block(kb, diag):
            ks = pl.multiple_of(kb * tk, tk)
            kblk = _kv_blocks(k_ref, ks, tk, G)
            vblk = _kv_blocks(v_ref, ks, tk, G)
            ss = [lax.dot_general(qh[st], kblk[st // 2], NT, preferred_element_type=F32) for st in range(NS)]
            ps = []
            for st in range(NS):
                s = ss[st] - cr_ref[0, st, :, pl.ds(ks, tk)]
                if diag:
                    s = jnp.where(on_or_below, s, NEG)
                m = m_ref[st]
                m_new = jnp.maximum(m, jnp.max(s, axis=-1, keepdims=True))
                alpha = jnp.exp(m - m_new)
                p = jnp.exp(s - _wide(m_new, tk))
                m_ref[st] = m_new
                l_ref[st] = alpha * l_ref[st] + _fold_lanes(p)
                ps.append((alpha, p.astype(BF16)))
            pvs = [jnp.dot(ps[st][1], vblk[st // 2], preferred_element_type=F32) for st in range(NS)]
            for st in range(NS):
                acc_ref[st] = ps[st][0] * acc_ref[st] + pvs[st]

        _sweep(i, block)
        ls = [jnp.sum(l_ref[st], axis=-1, keepdims=True) for st in range(NS)]
        for st in range(NS):
            lse_ref[0, st] = jnp.max(m_ref[st], axis=-1, keepdims=True) + jnp.log(ls[st])
        for g in range(G):
            o_ref[:, _lanes(g)] = jnp.where(lane < HEAD_DIM, acc_ref[2 * g] / ls[2 * g],
                                            acc_ref[2 * g + 1] / ls[2 * g + 1]).astype(BF16)

    return _pc(body, name=name, out_shape=[_sds((B * S, P * LANES), BF16), _sds((B, 2 * P, S, 1), F32)],
               grid=(B, P // G, nq),
               in_specs=[_g_col_spec(tq, nq, q_cb, G), _g_kv_spec(S, k_cb, G), _g_kv_spec(S, v_cb, G),
                         _g_stat_row_spec(S, G)],
               out_specs=[_g_col_spec(tq, nq, 0, G), _g_stat_col_spec(tq, G)],
               scratch_shapes=[pltpu.VMEM((NS, tq, LANES), F32)] * 3,
               semantics=("arbitrary", "arbitrary", "arbitrary"), vmem=VMEM_BIG)(qa, ka, va, cr)


def _fox_bwd_g(qa, ka, va, doa, lse, cr, *, name, B, S, P, q_cb, k_cb, v_cb, do_cb, G=HEAD_GROUP):
    tq = tk = min(ATT_TILE, S)
    nq = S // tq
    NS = 2 * G

    def body(q_ref, k_ref, v_ref, do_ref, lse_ref, cr_ref, dq_ref, dk_ref, dv_ref, dcs_ref, dqa_ref, delta_ref, lse_s,
             p_buf, dp_buf):
        i = pl.program_id(2)

        @pl.when(i == 0)
        def _():
            dk_ref[...] = jnp.zeros_like(dk_ref)
            dv_ref[...] = jnp.zeros_like(dv_ref)
            dcs_ref[...] = jnp.zeros_like(dcs_ref)

        lane, qh = _streams(q_ref, G, SCALE)
        _, doh = _streams(do_ref, G)
        on_or_below = (lax.broadcasted_iota(jnp.int32, (tq, tk), 1) <= lax.broadcasted_iota(jnp.int32, (tq, tk), 0))
        delta_ref[...] = jnp.zeros(delta_ref.shape, F32)
        dqa_ref[...] = jnp.zeros(dqa_ref.shape, F32)
        for st in range(NS):
            lse_s[st] = jnp.broadcast_to(lse_ref[0, st], (tq, LANES))

        def probs(kb, diag):
            ks = pl.multiple_of(kb * tk, tk)
            kblk = _kv_blocks(k_ref, ks, tk, G)
            vblk = _kv_blocks(v_ref, ks, tk, G)
            ss = [lax.dot_general(qh[st], kblk[st // 2], NT, preferred_element_type=F32) for st in range(NS)]
            dps = [lax.dot_general(doh[st], vblk[st // 2], NT, preferred_element_type=F32) for st in range(NS)]
            ps = []
            for st in range(NS):
                s = ss[st] - cr_ref[0, st, :, pl.ds(ks, tk)]
                if diag:
                    s = jnp.where(on_or_below, s, NEG)
                ps.append(jnp.exp(s - _wide(lse_s[st], tk)))
            return ks, kblk, ps, dps

        def delta_block(kb, diag):
            _, _, ps, dps = probs(kb, diag)
            for st in range(NS):
                delta_ref[st] += _fold_lanes(ps[st] * dps[st])
                p_buf[st, kb] = ps[st]
                dp_buf[st, kb] = dps[st]

        _sweep(i, delta_block)
        for st in range(NS):
            delta_ref[st] = jnp.broadcast_to(jnp.sum(delta_ref[st], axis=-1, keepdims=True), (tq, LANES))

        def grad_block(kb, diag):
            ks = pl.multiple_of(kb * tk, tk)
            kblk = _kv_blocks(k_ref, ks, tk, G)
            rows = pl.ds(ks, tk)
            dsb, pb = [], []
            for st in range(NS):
                p = p_buf[st, kb]
                ds = p * (dp_buf[st, kb] - _wide(delta_ref[st], tk))
                dcs_ref[0, st, :, rows] -= jnp.sum(ds, axis=0, keepdims=True)
                dsb.append(ds.astype(BF16))
                pb.append(p.astype(BF16))
            dks = [lax.dot_general(dsb[st], qh[st], TN, preferred_element_type=F32) for st in range(NS)]
            dvs = [lax.dot_general(pb[st], doh[st], TN, preferred_element_type=F32) for st in range(NS)]
            dqs = [jnp.dot(dsb[st], kblk[st // 2], preferred_element_type=F32) for st in range(NS)]
            for g in range(G):
                dk_ref[rows, _lanes(g)] += dks[2 * g] + dks[2 * g + 1]
                dv_ref[rows, _lanes(g)] += dvs[2 * g] + dvs[2 * g + 1]
            for st in range(NS):
                dqa_ref[st] += dqs[st]

        _sweep(i, grad_block)
        for g in range(G):
            dq_ref[:, _lanes(g)] = (jnp.where(lane < HEAD_DIM, dqa_ref[2 * g], dqa_ref[2 * g + 1]) * SCALE).astype(BF16)

    return _pc(body, name=name,
               out_shape=[_sds((B * S, P * LANES), BF16), _sds((B * S, P * LANES), F32), _sds((B * S, P * LANES), F32),
                          _sds((B, 2 * P, 1, S), F32)],
               grid=(B, P // G, nq),
               in_specs=[_g_col_spec(tq, nq, q_cb, G), _g_kv_spec(S, k_cb, G), _g_kv_spec(S, v_cb, G),
                         _g_col_spec(tq, nq, do_cb, G), _g_stat_col_spec(tq, G), _g_stat_row_spec(S, G)],
               out_specs=[_g_col_spec(tq, nq, 0, G), _g_kv_spec(S, 0, G), _g_kv_spec(S, 0, G), _g_stat_row_spec(S, G)],
               scratch_shapes=[pltpu.VMEM((NS, tq, LANES), F32)] * 3 + [pltpu.VMEM((NS, nq, tq, tk), F32)] * 2,
               semantics=("arbitrary", "arbitrary", "arbitrary"), vmem=VMEM_BIG)(qa, ka, va, doa, lse, cr)


def _sb_logs_z(z):
    nz = -z
    lm = jnp.minimum(nz, 0.0) - jnp.log(1.0 + jnp.exp(jnp.minimum(z, nz)))
    return lm + z, lm


def _sb_fwd_g(qa, ka, va, *, name, B, S, P, q_cb, k_cb, v_cb, G=HEAD_GROUP):
    tq = tk = min(ATT_TILE, S)
    nq = S // tq
    NS = 2 * G

    def body(q_ref, k_ref, v_ref, o_ref, rt_ref, acc_ref, run_ref):
        i = pl.program_id(2)
        lane, qh = _streams(q_ref, G, SCALE)
        t_r = lax.broadcasted_iota(jnp.int32, (tk, tk), 0)
        t_c = lax.broadcasted_iota(jnp.int32, (tk, tk), 1)
        after = (t_r > t_c).astype(BF16)
        below = t_c < t_r
        acc_ref[...] = jnp.zeros(acc_ref.shape, F32)
        run_ref[...] = jnp.zeros(run_ref.shape, F32)

        def block(kb, diag):
            ks = pl.multiple_of(kb * tk, tk)
            kblk = _kv_blocks(k_ref, ks, tk, G)
            vblk = _kv_blocks(v_ref, ks, tk, G)
            zs = [lax.dot_general(qh[st], kblk[st // 2], NT, preferred_element_type=F32) for st in range(NS)]
            lss, parts = [], []
            for st in range(NS):
                ls, lm = _sb_logs_z(zs[st])
                if diag:
                    lm = jnp.where(below, lm, 0.0)
                lss.append(ls + _wide(run_ref[st], tk))
                run_ref[st] += jnp.sum(lm, axis=-1, keepdims=True)
                parts.append(_split2(lm))
            sufs = [jnp.dot(parts[st][0], after, preferred_element_type=F32)
                    + jnp.dot(parts[st][1], after, preferred_element_type=F32) for st in range(NS)]
            ab = []
            for st in range(NS):
                a = jnp.exp(lss[st] + sufs[st])
                if diag:
                    a = jnp.where(below, a, 0.0)
                ab.append(a.astype(BF16))
            pvs = [jnp.dot(ab[st], vblk[st // 2], preferred_element_type=F32) for st in range(NS)]
            for st in range(NS):
                acc_ref[st] += pvs[st]

        block(i, True)

        def step(jj, c):
            block(i - 1 - jj, False)
            return c

        lax.fori_loop(0, i, step, 0)
        for st in range(NS):
            rt_ref[0, st] = jnp.max(run_ref[st], axis=-1, keepdims=True)
        for g in range(G):
            o_ref[:, _lanes(g)] = jnp.where(lane < HEAD_DIM, acc_ref[2 * g], acc_ref[2 * g + 1]).astype(BF16)

    return _pc(body, name=name, out_shape=[_sds((B * S, P * LANES), BF16), _sds((B, 2 * P, S, 1), F32)],
               grid=(B, P // G, nq),
               in_specs=[_g_col_spec(tq, nq, q_cb, G), _g_kv_spec(S, k_cb, G), _g_kv_spec(S, v_cb, G)],
               out_specs=[_g_col_spec(tq, nq, 0, G), _g_stat_col_spec(tq, G)],
               scratch_shapes=[pltpu.VMEM((NS, tq, LANES), F32)] * 2,
               semantics=("arbitrary", "arbitrary", "arbitrary"), vmem=VMEM_BIG)(qa, ka, va)


def _sb_bwd_g(qa, ka, va, doa, rt, *, name, B, S, P, q_cb, k_cb, v_cb, do_cb, G=HEAD_GROUP):
    tq = tk = min(ATT_TILE, S)
    nq = S // tq
    NS = 2 * G

    def body(q_ref, k_ref, v_ref, do_ref, rt_ref, dq_ref, dk_ref, dv_ref, dqa_ref, pl_ref, pg_ref):
        i = pl.program_id(2)

        @pl.when(i == 0)
        def _():
            dk_ref[...] = jnp.zeros_like(dk_ref)
            dv_ref[...] = jnp.zeros_like(dv_ref)

        lane, qh = _streams(q_ref, G, SCALE)
        _, doh = _streams(do_ref, G)
        t_r = lax.broadcasted_iota(jnp.int32, (tk, tk), 0)
        t_c = lax.broadcasted_iota(jnp.int32, (tk, tk), 1)
        upto = (t_r <= t_c).astype(BF16)
        before = (t_r < t_c).astype(BF16)
        below = t_c < t_r
        dqa_ref[...] = jnp.zeros(dqa_ref.shape, F32)
        pg_ref[...] = jnp.zeros(pg_ref.shape, F32)
        for st in range(NS):
            pl_ref[st] = jnp.broadcast_to(rt_ref[0, st], (tq, LANES))

        def block(kb, diag):
            ks = pl.multiple_of(kb * tk, tk)
            rows = pl.ds(ks, tk)
            kblk = _kv_blocks(k_ref, ks, tk, G)
            vblk = _kv_blocks(v_ref, ks, tk, G)
            zs = [lax.dot_general(qh[st], kblk[st // 2], NT, preferred_element_type=F32) for st in range(NS)]
            das = [lax.dot_general(doh[st], vblk[st // 2], NT, preferred_element_type=F32) for st in range(NS)]
            lss, parts = [], []
            for st in range(NS):
                ls, lm = _sb_logs_z(zs[st])
                if diag:
                    lm = jnp.where(below, lm, 0.0)
                lss.append((ls, ls + _wide(pl_ref[st], tk)))
                pl_ref[st] -= jnp.sum(lm, axis=-1, keepdims=True)
                parts.append(_split2(lm))
            pins = [jnp.dot(parts[st][0], upto, preferred_element_type=F32)
                    + jnp.dot(parts[st][1], upto, preferred_element_type=F32) for st in range(NS)]
            gms, ab, gparts = [], [], []
            for st in range(NS):
                a = jnp.exp(lss[st][1] - pins[st])
                if diag:
                    a = jnp.where(below, a, 0.0)
                gm = a * das[st]
                gms.append(gm)
                ab.append(a.astype(BF16))
                gparts.append(gm.astype(BF16))
            pgs = [jnp.dot(gparts[st], before, preferred_element_type=F32) for st in range(NS)]
            dzb = []
            for st in range(NS):
                gm = gms[st]
                dz = gm - jnp.exp(lss[st][0]) * (gm + (pgs[st] + _wide(pg_ref[st], tk)))
                if diag:
                    dz = jnp.where(below, dz, 0.0)
                pg_ref[st] += jnp.sum(gm, axis=-1, keepdims=True)
                dzb.append(dz.astype(BF16))
            dks = [lax.dot_general(dzb[st], qh[st], TN, preferred_element_type=F32) for st in range(NS)]
            dvs = [lax.dot_general(ab[st], doh[st], TN, preferred_element_type=F32) for st in range(NS)]
            dqs = [jnp.dot(dzb[st], kblk[st // 2], preferred_element_type=F32) for st in range(NS)]
            for g in range(G):
                dk_ref[rows, _lanes(g)] += dks[2 * g] + dks[2 * g + 1]
                dv_ref[rows, _lanes(g)] += dvs[2 * g] + dvs[2 * g + 1]
            for st in range(NS):
                dqa_ref[st] += dqs[st]

        _sweep(i, block)
        for g in range(G):
            dq_ref[:, _lanes(g)] = (jnp.where(lane < HEAD_DIM, dqa_ref[2 * g], dqa_ref[2 * g + 1]) * SCALE).astype(BF16)

    return _pc(body, name=name,
               out_shape=[_sds((B * S, P * LANES), BF16), _sds((B * S, P * LANES), F32), _sds((B * S, P * LANES), F32)],
               grid=(B, P // G, nq),
               in_specs=[_g_col_spec(tq, nq, q_cb, G), _g_kv_spec(S, k_cb, G), _g_kv_spec(S, v_cb, G),
                         _g_col_spec(tq, nq, do_cb, G), _g_stat_col_spec(tq, G)],
               out_specs=[_g_col_spec(tq, nq, 0, G), _g_kv_spec(S, 0, G), _g_kv_spec(S, 0, G)],
               scratch_shapes=[pltpu.VMEM((NS, tq, LANES), F32)] * 3,
               semantics=("arbitrary", "arbitrary", "arbitrary"), vmem=VMEM_BIG)(qa, ka, va, doa, rt)


def _shift_rows(cur, halo_ref, first, rows_idx, k):
    out = pltpu.roll(cur, k, 0)
    for r in range(k):
        hr = halo_ref.shape[0] - k + r
        edge = jnp.where(first, 0.0, halo_ref[hr:hr + 1, :])
        out = jnp.where(rows_idx == r, edge, out)
    return out


def _shift_rows_up(cur, halo_ref, last, rows_idx, k, ts):
    out = pltpu.roll(cur, ts - k, 0)
    for r in range(k):
        edge = jnp.where(last, 0.0, halo_ref[r:r + 1, :])
        out = jnp.where(rows_idx == ts - k + r, edge, out)
    return out


def _conv_taps(main_ref, halo_ref, w_ref, b_ref, first, rows_idx):
    cur = main_ref[...]
    m1 = _shift_rows(cur, halo_ref, first, rows_idx, 1)
    m2 = _shift_rows(cur, halo_ref, first, rows_idx, 2)
    uc = b_ref[...] + w_ref[0:1, :] * m2 + w_ref[1:2, :] * m1 + w_ref[2:3, :] * cur
    return uc, cur, m1, m2


def _conv_specs(ts, tf, ns, nf, S, order):
    def wrap(fn):
        return lambda *g: fn(*order(*g))
    specs = []
    for half in (0, 1):
        specs.append(pl.BlockSpec((None, ts, tf), wrap(lambda b, i, j, half=half: (half, b * ns + i, j))))
        specs.append(pl.BlockSpec((None, 8, tf), wrap(
            lambda b, i, j, half=half: (half, jnp.maximum((b * S + i * ts) // 8 - 1, 0), j))))
    for off in (0, nf):
        specs.append(pl.BlockSpec((3, tf), wrap(lambda b, i, j, off=off: (0, j + off))))
    for off in (0, nf):
        specs.append(pl.BlockSpec((1, tf), wrap(lambda b, i, j, off=off: (0, j + off))))
    return specs


def _ffn_up_gate(x, g, w, cw, cb, *, name, S):
    T, D = x.shape
    F = w.shape[1] // 2
    tm = min(1024, S)
    tn = 256
    nj = F // tn
    tiles_per_seq = S // tm
    halo = 16

    def body(x_ref, xh_ref, g_ref, wg_ref, wv_ref, cwg_ref, cwv_ref, cbg_ref, cbv_ref,
             u_ref, a_ref, hout_ref, h_ref, hh_ref, eg_ref, ev_ref):
        first = lax.rem(pl.program_id(0), tiles_per_seq) == 0

        @pl.when(pl.program_id(1) == 0)
        def _():
            def norm(v):
                r = lax.rsqrt(jnp.mean(v * v, axis=-1, keepdims=True) + EPS)
                return ((v * r) * g_ref[...]).astype(BF16)
            h = norm(x_ref[...])
            h_ref[...] = h
            hout_ref[...] = h
            hh_ref[...] = norm(xh_ref[...])

        h = h_ref[...]
        rows_idx = lax.broadcasted_iota(jnp.int32, (tm, tn), 0)
        uc = []
        for half, (w_ref, cw_ref, cb_ref, e_ref) in enumerate(((wg_ref, cwg_ref, cbg_ref, eg_ref),
                                                               (wv_ref, cwv_ref, cbv_ref, ev_ref))):
            acc = jnp.dot(h, w_ref[...], preferred_element_type=F32)
            e_ref[...] = jnp.dot(hh_ref[...], w_ref[...], preferred_element_type=F32)
            u_ref[half] = acc
            m1 = _shift_rows(acc, e_ref, first, rows_idx, 1)
            m2 = _shift_rows(acc, e_ref, first, rows_idx, 2)
            uc.append(cb_ref[...] + cw_ref[0:1, :] * m2 + cw_ref[1:2, :] * m1 + cw_ref[2:3, :] * acc)
        a_ref[...] = (uc[0] * (1.0 / (1.0 + jnp.exp(-uc[0]))) * uc[1]).astype(BF16)

    in_specs = [pl.BlockSpec((tm, D), lambda i, j: (i, 0)),
                pl.BlockSpec((halo, D), lambda i, j: (jnp.maximum(i * (tm // halo) - 1, 0), 0)),
                pl.BlockSpec((1, D), lambda i, j: (0, 0)),
                pl.BlockSpec((D, tn), lambda i, j: (0, j)), pl.BlockSpec((D, tn), lambda i, j: (0, j + nj)),
                pl.BlockSpec((3, tn), lambda i, j: (0, j)), pl.BlockSpec((3, tn), lambda i, j: (0, j + nj)),
                pl.BlockSpec((1, tn), lambda i, j: (0, j)), pl.BlockSpec((1, tn), lambda i, j: (0, j + nj))]
    return _pc(body, name=name,
               out_shape=[_sds((2, T, F), F32), _sds((T, F), BF16), _sds((T, D), BF16)],
               grid=(T // tm, nj), in_specs=in_specs,
               out_specs=[pl.BlockSpec((2, tm, tn), lambda i, j: (0, i, j)), pl.BlockSpec((tm, tn), lambda i, j: (i, j)),
                          pl.BlockSpec((tm, D), lambda i, j: (i, 0))],
               scratch_shapes=[pltpu.VMEM((tm, D), BF16), pltpu.VMEM((halo, D), BF16),
                               pltpu.VMEM((halo, tn), F32), pltpu.VMEM((halo, tn), F32)],
               semantics=("arbitrary", "arbitrary"), vmem=VMEM_BIG)(x, x, g.reshape(1, D), w, w, cw, cw, cb, cb)


def _conv_gate_bwd(da, u, cw, cb, *, name, B, S):
    F = u.shape[2]
    tf = F // 2
    ts = min(256, S)
    ns, nf = S // ts, F // tf

    def body(da_ref, ug_ref, ugh_ref, uv_ref, uvh_ref, wg_ref, wv_ref, bg_ref, bv_ref,
             dug_ref, duv_ref, pg_ref, pv_ref, nxt_g, nxt_v):
        step = pl.program_id(2)
        first = step == ns - 1
        last = step == 0

        @pl.when(jnp.logical_and(pl.program_id(1) == 0, last))
        def _():
            pg_ref[...] = jnp.zeros_like(pg_ref)
            pv_ref[...] = jnp.zeros_like(pv_ref)

        rows_idx = lax.broadcasted_iota(jnp.int32, (ts, tf), 0)
        ucg, g0, g1, g2 = _conv_taps(ug_ref, ugh_ref, wg_ref, bg_ref, first, rows_idx)
        ucv, v0, v1, v2 = _conv_taps(uv_ref, uvh_ref, wv_ref, bv_ref, first, rows_idx)
        sg = 1.0 / (1.0 + jnp.exp(-ucg))
        dav = da_ref[...]
        d_v = dav * (ucg * sg)
        d_g = dav * ucv * (sg * (1.0 + ucg * (1.0 - sg)))
        for p_ref, d, taps in ((pg_ref, d_g, (g2, g1, g0)), (pv_ref, d_v, (v2, v1, v0))):
            for k in range(3):
                p_ref[k:k + 1, :] += jnp.sum(d * taps[k], axis=0, keepdims=True)
            p_ref[3:4, :] += jnp.sum(d, axis=0, keepdims=True)
        for o_ref, d, w_ref, nxt in ((dug_ref, d_g, wg_ref, nxt_g), (duv_ref, d_v, wv_ref, nxt_v)):
            p1 = _shift_rows_up(d, nxt, last, rows_idx, 1, ts)
            p2 = _shift_rows_up(d, nxt, last, rows_idx, 2, ts)
            o_ref[...] = (w_ref[2:3, :] * d + w_ref[1:2, :] * p1 + w_ref[0:1, :] * p2).astype(BF16)
            nxt[...] = d[0:8, :]

    def order(j, b, r):
        return b, ns - 1 - r, j

    row = pl.BlockSpec((ts, tf), lambda j, b, r: (b * ns + ns - 1 - r, j))
    specs = [row] + _conv_specs(ts, tf, ns, nf, S, order)
    par = pl.BlockSpec((8, tf), lambda j, b, r: (0, j))
    return _pc(body, name=name,
               out_shape=[_sds((B * S, F), BF16), _sds((B * S, F), BF16), _sds((8, F), F32), _sds((8, F), F32)],
               grid=(nf, B, ns), in_specs=specs, out_specs=[row, row, par, par],
               scratch_shapes=[pltpu.VMEM((8, tf), F32), pltpu.VMEM((8, tf), F32)],
               semantics=("arbitrary", "arbitrary", "arbitrary"), vmem=VMEM_BIG)(da, u, u, u, u, cw, cw, cb, cb)


def _conv_transpose(d, cw, *, name, B, S, col_off):
    F = d.shape[1]
    tf = F // 2
    ts = min(256, S)
    ns, nf = S // ts, F // tf
    nblk8 = B * S // 8

    def body(d_ref, dh_ref, w_ref, o_ref):
        last = pl.program_id(1) == ns - 1
        rows_idx = lax.broadcasted_iota(jnp.int32, (ts, tf), 0)
        cur = d_ref[...]
        p1 = _shift_rows_up(cur, dh_ref, last, rows_idx, 1, ts)
        p2 = _shift_rows_up(cur, dh_ref, last, rows_idx, 2, ts)
        o_ref[...] = (w_ref[2:3, :] * cur + w_ref[1:2, :] * p1 + w_ref[0:1, :] * p2).astype(BF16)

    return _pc(body, name=name, out_shape=_sds((B * S, F), BF16), grid=(B, ns, nf),
               in_specs=[pl.BlockSpec((ts, tf), lambda b, i, j: (b * ns + i, j)),
                         pl.BlockSpec((8, tf), lambda b, i, j: (jnp.minimum((b * S + (i + 1) * ts) // 8, nblk8 - 1), j)),
                         pl.BlockSpec((3, tf), lambda b, i, j: (0, j + col_off * nf))],
               out_specs=pl.BlockSpec((ts, tf), lambda b, i, j: (b * ns + i, j)),
               semantics=("arbitrary", "arbitrary", "arbitrary"))(d, d, cw)


def _adamw(w, g, m, v, *, name):
    rows, cols = w.shape
    tr = rows
    while tr * cols * 4 > 2 ** 20 and tr % 16 == 0:
        tr //= 2

    def body(w_ref, g_ref, m_ref, v_ref, d_ref, nm_ref, nv_ref):
        gv = g_ref[...]
        m_new = ADAM_B1 * m_ref[...] + (1.0 - ADAM_B1) * gv
        v_new = ADAM_B2 * v_ref[...] + (1.0 - ADAM_B2) * (gv * gv)
        m_hat = m_new / (1.0 - ADAM_B1 ** ADAM_STEP)
        v_hat = v_new / (1.0 - ADAM_B2 ** ADAM_STEP)
        d_ref[...] = -ADAM_LR * (m_hat / (jnp.sqrt(v_hat) + ADAM_EPS) + ADAM_WD * w_ref[...])
        nm_ref[...] = m_new
        nv_ref[...] = v_new

    blk = pl.BlockSpec((tr, cols), lambda i: (i, 0))
    return _pc(body, name=name, out_shape=[_sds((rows, cols), F32)] * 3, grid=(rows // tr,),
               in_specs=[blk] * 4, out_specs=[blk] * 3, semantics=("arbitrary",))(w, g, m, v)


def _my_pos():
    return lax.axis_index("x"), lax.axis_index("y"), lax.axis_index("c")


_HBM = pl.BlockSpec(memory_space=pltpu.HBM)
_SEM = pl.BlockSpec(memory_space=pltpu.SEMAPHORE)
_EFFECT = pltpu.SideEffectType.DATAFLOW_SIDE_EFFECTING


def _peers():
    x, y, c = _my_pos()
    out = []
    for k in range(1, N_DEV):
        px, py, pc = x ^ ((k >> 2) & 1), y ^ ((k >> 1) & 1), c ^ (k & 1)
        out.append(((px, py, pc), 4 * px + 2 * py + pc))
    return out


def _scatter_start(srcs, slot_of, *, name, order_after=None):
    n = len(srcs)
    lands = [lax.empty((N_DEV,) + slot_of(s, 0, shape_only=True), s.dtype) for s in srcs]
    extra = [] if order_after is None else [order_after]

    def body(*refs):
        src_refs, land_refs = refs[:n], refs[n:2 * n]
        send_sems, recv_sems = refs[2 * n + len(extra)], refs[2 * n + len(extra) + 1]
        token = refs[-1]
        x, y, c = _my_pos()
        me = 4 * x + 2 * y + c
        for a in range(n):
            for k, (peer, peer_idx) in enumerate(_peers()):
                pltpu.make_async_remote_copy(
                    src_ref=slot_of(src_refs[a], peer_idx), dst_ref=land_refs[a].at[me],
                    send_sem=send_sems.at[a * 7 + k], recv_sem=recv_sems.at[a * 7 + k],
                    device_id=peer, device_id_type=MESH).start()
        token[...] = jnp.zeros_like(token)

    hbm = lambda a: pltpu.HBM(a.shape, a.dtype)
    args = [pltpu.with_memory_space_constraint(a, pltpu.HBM) for a in list(srcs) + lands] + extra
    outs = pl.pallas_call(
        body, name=name,
        out_shape=(pltpu.SemaphoreType.DMA((7 * n,)), pltpu.SemaphoreType.DMA((7 * n,)),
                   *[hbm(a) for a in srcs], *[hbm(a) for a in lands], _sds((8, LANES), F32)),
        in_specs=[_HBM] * (2 * n) + [pl.BlockSpec(memory_space=pl.ANY)] * len(extra),
        out_specs=(_SEM, _SEM, *([_HBM] * (2 * n)), pl.BlockSpec(memory_space=pltpu.VMEM)),
        input_output_aliases={a: 2 + a for a in range(2 * n)},
        compiler_params=pltpu.CompilerParams(has_side_effects=_EFFECT))(*args)
    return outs[0], outs[1], list(outs[2:2 + n]), list(outs[2 + n:2 + 2 * n]), outs[-1]


def _scatter_wait(send_sems, recv_sems, srcs, lands, slot_of, after, *, name):
    n = len(srcs)

    def body(*refs):
        src_refs, land_refs = refs[:n], refs[n:2 * n]
        ssem, rsem = refs[2 * n], refs[2 * n + 1]
        x, y, c = _my_pos()
        me = 4 * x + 2 * y + c
        for a in range(n):
            for k, (peer, peer_idx) in enumerate(_peers()):
                cp = pltpu.make_async_remote_copy(
                    src_ref=slot_of(src_refs[a], peer_idx), dst_ref=land_refs[a].at[me],
                    send_sem=ssem.at[a * 7 + k], recv_sem=rsem.at[a * 7 + k],
                    device_id=peer, device_id_type=MESH)
                cp.wait_send()
                cp.wait_recv()

    hbm = lambda a: pltpu.HBM(a.shape, a.dtype)
    outs = pl.pallas_call(
        body, name=name, out_shape=tuple(hbm(a) for a in list(srcs) + list(lands)),
        in_specs=[_HBM] * (2 * n) + [_SEM, _SEM, pl.BlockSpec(memory_space=pl.ANY)],
        out_specs=tuple([_HBM] * (2 * n)), input_output_aliases={a: a for a in range(2 * n)},
        compiler_params=pltpu.CompilerParams(has_side_effects=_EFFECT))(*srcs, *lands, send_sems, recv_sems, after)
    return list(outs[:n]), list(outs[n:])


def _whole(a, peer_idx, shape_only=False):
    return a.shape if shape_only else a


def _slot(a, peer_idx, shape_only=False):
    return a.shape[1:] if shape_only else a.at[peer_idx]


def _all_gather(shard, *, name):
    rows = shard.shape[0]

    def body(x_ref, out_ref, send_sems, recv_sems, local_sem):
        x, y, c = _my_pos()
        me, sibling = (x, y, c), (x, y, 1 - c)
        chips = [(1 - x, y), (x, 1 - y), (1 - x, 1 - y)]

        def slot(px, py, pc):
            return out_ref.at[4 * px + 2 * py + pc]

        def copy(k, block, to, src=None):
            return pltpu.make_async_remote_copy(
                src_ref=slot(*block) if src is None else src, dst_ref=slot(*block),
                send_sem=send_sems.at[k], recv_sem=recv_sems.at[k], device_id=to, device_id_type=MESH)

        mine = pltpu.make_async_copy(x_ref, slot(*me), local_sem)
        mine.start()
        first = [copy(0, me, sibling, src=x_ref)]
        first += [copy(1 + j, me, (*chip, c), src=x_ref) for j, chip in enumerate(chips)]
        for cp in first:
            cp.start()
        passed = [copy(4 + j, (*chip, c), sibling) for j, chip in enumerate(chips)]
        for j, chip in enumerate(chips):
            copy(1 + j, (*chip, c), me).wait_recv()
            passed[j].start()
        copy(0, sibling, me).wait_recv()
        for j, chip in enumerate(chips):
            copy(4 + j, (*chip, 1 - c), me).wait_recv()
        for cp in first + passed:
            cp.wait_send()
        mine.wait()

    return _pc(body, name=name, out_shape=_sds((N_DEV, rows, LANES), shard.dtype),
               in_specs=[pl.BlockSpec(memory_space=pl.ANY)], out_specs=pl.BlockSpec(memory_space=pl.ANY),
               scratch_shapes=[pltpu.SemaphoreType.DMA((7,)), pltpu.SemaphoreType.DMA((7,)),
                               pltpu.SemaphoreType.DMA])(shard)


def _exchange(big, small, *, name):
    rs = small.shape[0]

    def body(big_ref, small_ref, bout_ref, sout_ref, send_sems, recv_sems, local_sems):
        x, y, c = _my_pos()
        me = 4 * x + 2 * y + c
        lb = pltpu.make_async_copy(big_ref.at[me], bout_ref.at[me], local_sems.at[0])
        ls = pltpu.make_async_copy(small_ref, sout_ref.at[me], local_sems.at[1])
        lb.start()
        ls.start()
        copies = []
        for k in range(1, N_DEV):
            px = x ^ ((k >> 2) & 1)
            py = y ^ ((k >> 1) & 1)
            pc = c ^ (k & 1)
            peer = 4 * px + 2 * py + pc
            copies.append(pltpu.make_async_remote_copy(
                src_ref=big_ref.at[peer], dst_ref=bout_ref.at[me], send_sem=send_sems.at[k - 1],
                recv_sem=recv_sems.at[k - 1], device_id=(px, py, pc), device_id_type=MESH))
            copies.append(pltpu.make_async_remote_copy(
                src_ref=small_ref, dst_ref=sout_ref.at[me], send_sem=send_sems.at[7 + k - 1],
                recv_sem=recv_sems.at[7 + k - 1], device_id=(px, py, pc), device_id_type=MESH))
        for cp in copies:
            cp.start()
        for cp in copies:
            cp.wait()
        lb.wait()
        ls.wait()

    return _pc(body, name=name,
               out_shape=[_sds(big.shape, big.dtype), _sds((N_DEV, rs, LANES), F32)],
               in_specs=[pl.BlockSpec(memory_space=pl.ANY), pl.BlockSpec(memory_space=pl.ANY)],
               out_specs=[pl.BlockSpec(memory_space=pl.ANY), pl.BlockSpec(memory_space=pl.ANY)],
               scratch_shapes=[pltpu.SemaphoreType.DMA((14,)), pltpu.SemaphoreType.DMA((14,)),
                               pltpu.SemaphoreType.DMA((2,))])(big, small)


def _all_gather_small(small, *, name):
    rs = small.shape[0]

    def body(small_ref, out_ref, send_sems, recv_sems, local_sem):
        x, y, c = _my_pos()
        me = 4 * x + 2 * y + c
        mine = pltpu.make_async_copy(small_ref, out_ref.at[me], local_sem)
        mine.start()
        copies = [pltpu.make_async_remote_copy(
            src_ref=small_ref, dst_ref=out_ref.at[me], send_sem=send_sems.at[k], recv_sem=recv_sems.at[k],
            device_id=peer, device_id_type=MESH) for k, (peer, _) in enumerate(_peers())]
        for cp in copies:
            cp.start()
        for cp in copies:
            cp.wait()
        mine.wait()

    return _pc(body, name=name, out_shape=_sds((N_DEV, rs, LANES), F32),
               in_specs=[pl.BlockSpec(memory_space=pl.ANY)], out_specs=pl.BlockSpec(memory_space=pl.ANY),
               scratch_shapes=[pltpu.SemaphoreType.DMA((7,)), pltpu.SemaphoreType.DMA((7,)),
                               pltpu.SemaphoreType.DMA])(small)


def _sum_slots(a, *, name, tr=None):
    rows, cols = a.shape[1], a.shape[2]
    if tr is None:
        tr = rows
        while N_DEV * tr * cols * a.dtype.itemsize > 3 * 2 ** 20 and tr % 32 == 0:
            tr //= 2

    def body(a_ref, o_ref):
        acc = a_ref[0].astype(F32)
        for j in range(1, N_DEV):
            acc = acc + a_ref[j].astype(F32)
        o_ref[...] = acc

    return _pc(body, name=name, out_shape=_sds((rows, cols), F32), grid=(rows // tr,),
               in_specs=[pl.BlockSpec((N_DEV, tr, cols), lambda i: (0, i, 0))],
               out_specs=pl.BlockSpec((tr, cols), lambda i: (i, 0)), semantics=("arbitrary",), vmem=VMEM_BIG)(a)


PACK_ROWS = 25600
SUM_TILE = 512


def _rows128(a):
    return a.reshape(-1, LANES)


def _to_slots(full, kind):
    if kind == "rows2":
        r, c = full.shape
        return full.reshape(N_DEV, r // N_DEV, c)
    if kind == "cols2":
        r, c = full.shape
        return full.reshape(r, N_DEV, c // N_DEV).transpose(1, 0, 2)
    if kind == "rows3":
        l, r, c = full.shape
        return full.reshape(l, N_DEV, r // N_DEV, c).transpose(1, 0, 2, 3)
    if kind == "cols3":
        l, r, c = full.shape
        return full.reshape(l, r, N_DEV, c // N_DEV).transpose(2, 0, 1, 3)
    raise ValueError(kind)


def _from_slots(slots, kind):
    if kind == "rows2":
        _, r, c = slots.shape
        return slots.reshape(N_DEV * r, c)
    if kind == "cols2":
        _, r, c = slots.shape
        return slots.transpose(1, 0, 2).reshape(r, N_DEV * c)
    if kind == "rows3":
        _, l, r, c = slots.shape
        return slots.transpose(1, 0, 2, 3).reshape(l, N_DEV * r, c)
    if kind == "cols3":
        _, l, r, c = slots.shape
        return slots.transpose(1, 2, 0, 3).reshape(l, r, N_DEV * c)
    raise ValueError(kind)


BIG = (("w_in_a", "rows2"), ("w_in_b", "rows2"), ("w_kv", "cols2"), ("w_memkv", "rows3"),
       ("w_out", "rows3"), ("w_up", "cols3"), ("w_down", "rows3"))


def _round_up(n, m):
    return -(-n // m) * m


def _pad_rows(a, rows, axis):
    pad = [(0, 0)] * a.ndim
    pad[axis] = (0, rows - a.shape[axis])
    return jnp.pad(a, pad)


def kernel(x, mem, ln_mix_g, w_in_a, b_f_a, w_in_b, ln_kv_g, w_kv, ln_mem_g, w_memkv, w_out, ln_ffn_g, w_up, conv_w, conv_b, w_down, final_g, loss_target, m_ln_mix_g, m_w_in_a, m_b_f_a, m_w_in_b, m_ln_kv_g, m_w_kv, m_ln_mem_g, m_w_memkv, m_w_out, m_ln_ffn_g, m_w_up, m_conv_w, m_conv_b, m_w_down, m_final_g, v_ln_mix_g, v_w_in_a, v_b_f_a, v_w_in_b, v_ln_kv_g, v_w_kv, v_ln_mem_g, v_w_memkv, v_w_out, v_ln_ffn_g, v_w_up, v_conv_w, v_conv_b, v_w_down, v_final_g):
    B, S, D = x.shape
    NM = mem.shape[1]
    T = B * S
    F = w_down.shape[1] * N_DEV
    my_idx = 4 * lax.axis_index("x") + 2 * lax.axis_index("y") + lax.axis_index("c")

    shards = {"w_in_a": w_in_a[0], "w_in_b": w_in_b[0], "w_kv": w_kv, "w_memkv": w_memkv, "w_out": w_out,
              "w_up": w_up, "w_down": w_down}
    moms = {"w_in_a": (m_w_in_a[0], v_w_in_a[0]), "w_in_b": (m_w_in_b[0], v_w_in_b[0]), "w_kv": (m_w_kv, v_w_kv),
            "w_memkv": (m_w_memkv, v_w_memkv), "w_out": (m_w_out, v_w_out), "w_up": (m_w_up, v_w_up),
            "w_down": (m_w_down, v_w_down)}

    groups = [("a1", [("w_in_a", None)]),
              ("a2", [("w_in_b", None), ("w_kv", None), ("w_memkv", None), ("w_out", None), ("conv_w", None)]),
              ("b0", [("w_up", 0), ("w_down", 0)]), ("b1", [("w_up", 1), ("w_down", 1)])]
    sources = dict(shards, conv_w=conv_w)
    started, token = {}, None
    for gname, members in groups:
        srcs = []
        for n, layer in members:
            a = sources[n] if layer is None else sources[n][layer]
            srcs.append(a if n == "conv_w" else a.astype(BF16))
        ssem, rsem, thru, lands, token = _scatter_start(srcs, _whole, name=f"gather_start_{gname}", order_after=token)
        started[gname] = (ssem, rsem, thru, lands)

    def gathered(gname, after):
        ssem, rsem, thru, lands = started[gname]
        thru, lands = _scatter_wait(ssem, rsem, thru, lands, _whole, after, name=f"gather_wait_{gname}")
        return [lax.dynamic_update_index_in_dim(land, s, my_idx, 0) for land, s in zip(lands, thru)]

    full = {}
    (g_wa,) = gathered("a1", token)
    full["w_in_a"] = _from_slots(g_wa, "rows2")

    wa = full["w_in_a"]
    n_qkv = 3 * MAIN_W
    wa = jnp.concatenate([wa[:, :n_qkv], wa[:, n_qkv + N_MAIN_HEADS:], wa[:, n_qkv:n_qkv + N_MAIN_HEADS],
                          jnp.zeros((D, LANES - N_MAIN_HEADS), BF16)], axis=1)
    n_main = n_qkv + MEM_W
    full["w_up"], full["w_down"] = {}, {}
    b_f =_pad_rows(b_f_a.reshape(1, N_MAIN_HEADS), LANES, 1)

    x2d = x.reshape(T, D)
    mem2d = mem.reshape(B * NM, D)
    tgt2d = loss_target.reshape(T, D)
    PM, PX = N_MAIN_HEADS // 2, N_MEM_HEADS // 2

    def stats_to_heads(c2d):
        c = c2d.reshape(B, S, LANES)[:, :, :N_MAIN_HEADS].transpose(0, 2, 1)
        return c[:, :, None, :]

    def mem_kv(layer):
        return _mm_fwd(mem2d, full["w_memkv"][layer], name=f"memkv{layer}", tm=B * NM, tn=2 * MEM_W,
                       out_dtype=BF16, g=ln_mem_g[layer], save_h=True)

    def conv_ffn_fwd(xin, layer):
        u, a, h = _ffn_up_gate(xin, ln_ffn_g[layer], full["w_up"][layer], conv_w_full[layer],
                               conv_b[layer].reshape(1, 2 * F), name=f"ffn_up{layer}", S=S)
        xo = _mm_fwd(a, full["w_down"][layer], name=f"ffn_down{layer}", tm=min(512, T), tn=512, out_dtype=F32, res=xin)
        return xo, (u, h, a)

    proj_a, h_mix0 = _mm_fwd(x2d, wa, name="in_proj_a", tm=min(1024, T), tn=512, out_dtype=BF16, g=ln_mix_g[0],
                             ncols=n_main, save_h=True)
    f_logit = _mm_fwd(x2d, wa, name="in_proj_f", tm=min(1024, T), tn=LANES, out_dtype=F32, g=ln_mix_g[0],
                      col0=n_main // LANES, ncols=LANES)
    c2d = _forget_cumsum(f_logit, b_f, B=B, S=S, name="forget_cumsum")
    cr = stats_to_heads(c2d)
    o_main0, lse0 = _fox_fwd_g(proj_a, proj_a, proj_a, cr, name="fox_fwd", B=B, S=S, P=PM, q_cb=0, k_cb=PM, v_cb=2 * PM)
    g_wb, g_wkv, g_wmem, g_wout, g_cw = gathered("a2", lse0)
    wb = _from_slots(g_wb, "rows2")
    wkv = _from_slots(g_wkv, "cols2")
    full["w_memkv"] = _from_slots(g_wmem, "rows3")
    full["w_out"] = _from_slots(g_wout, "rows3")
    conv_w_full = _from_slots(g_cw, "cols3")
    memkv0, h_mem0 = mem_kv(0)
    o_mem0, lse_m0 = _softmax_fwd(proj_a, memkv0, memkv0, name="mem_fwd0", B=B, S=S, Sk=NM, P=PX, q_cb=3 * PM,
                                  k_cb=0, v_cb=PX, causal=False)
    o_cat0 = jnp.concatenate([o_main0, o_mem0], axis=1)
    x1 = _mm_fwd(o_cat0, full["w_out"][0], name="out_proj0", tm=min(512, T), tn=512, out_dtype=F32, res=x2d)
    g_up, g_dn = gathered("b0", x1)
    full["w_up"][0], full["w_down"][0] = _from_slots(g_up, "cols2"), _from_slots(g_dn, "rows2")
    x2, (u0, h_ffn0, a0) = conv_ffn_fwd(x1, 0)
    kv, h_kv = _mm_fwd(x2, wkv, name="kv_proj", tm=min(1024, T), tn=512, out_dtype=BF16, g=ln_kv_g, save_h=True)
    proj_b, h_mix1 = _mm_fwd(x2, wb, name="in_proj_b", tm=min(1024, T), tn=512, out_dtype=BF16, g=ln_mix_g[1],
                             save_h=True)
    o_main1, rt1 = _sb_fwd_g(proj_b, kv, kv, name="sb_fwd", B=B, S=S, P=PM, q_cb=0, k_cb=0, v_cb=PM)
    memkv1, h_mem1 = mem_kv(1)
    o_mem1, lse_m1 = _softmax_fwd(proj_b, memkv1, memkv1, name="mem_fwd1", B=B, S=S, Sk=NM, P=PX, q_cb=PM,
                                  k_cb=0, v_cb=PX, causal=False)
    o_cat1 = jnp.concatenate([o_main1, o_mem1], axis=1)
    x3 = _mm_fwd(o_cat1, full["w_out"][1], name="out_proj1", tm=min(512, T), tn=512, out_dtype=F32, res=x2)
    g_up, g_dn = gathered("b1", x3)
    full["w_up"][1], full["w_down"][1] = _from_slots(g_up, "cols2"), _from_slots(g_dn, "rows2")
    x4, (u1, h_ffn1, a1) = conv_ffn_fwd(x3, 1)
    dx4, dg_final, loss_part = _loss_head(x4, final_g, tgt2d, name="loss_head")

    grads = {}
    small = {}
    reduce_groups = []

    def start_reduce(gname, keys, kinds):
        slots = [_to_slots(grads[k], kind) for k, kind in zip(keys, kinds)]
        ssem, rsem, thru, lands, tok = _scatter_start(slots, _slot, name=f"reduce_start_{gname}")
        reduce_groups.append((gname, keys, ssem, rsem, thru, lands))
        return tok[0, 0]

    def conv_ffn_bwd(dxo, xin, u, h, a, layer):
        w_dn = full["w_down"][layer]
        da = _mm_nt(dxo, w_dn, name=f"d_act{layer}", tm=min(512, T), tn=F // 2, out_dtype=F32)
        grads[("w_down", layer)] = _wgrad(a, dxo, f"g_w_down{layer}")
        cwl = conv_w_full[layer]
        du_g, du_v, p_g, p_v = _conv_gate_bwd(da, u, cwl, conv_b[layer].reshape(1, 2 * F), name=f"conv_bwd{layer}",
                                              B=B, S=S)
        small[("conv_w", layer)] = jnp.concatenate([p_g[0:3], p_v[0:3]], axis=1)
        small[("conv_b", layer)] = jnp.concatenate([p_g[3], p_v[3]], axis=0)
        grads[("w_up", layer)] = jnp.concatenate(
            [_wgrad(h, du_g, f"g_w_up_gate{layer}"), _wgrad(h, du_v, f"g_w_up_val{layer}")], axis=1)
        tok = start_reduce(f"ffn{layer}", [("w_down", layer), ("w_up", layer)], ["rows2", "cols2"])
        dxi, dg = _mm_nt_rmsbwd([(du_g, 0), (du_v, 1)], full["w_up"][layer], xin, ln_ffn_g[layer] + tok,
                                name=f"d_ffn_in{layer}", dres=dxo)
        small[("ln_ffn_g", layer)] = dg[0]
        return dxi

    def mem_bwd(proj, q_cb, memkv, h_mem, do_cat, o_mem, lse_m, layer):
        dqm, dmk, dmv = _softmax_bwd(proj, memkv, memkv, do_cat, o_mem, lse_m, name=f"mem_bwd{layer}", B=B, S=S,
                                     Sk=NM, P=PX, q_cb=q_cb, k_cb=0, v_cb=PX, do_cb=PM, causal=False)
        grads[("w_memkv", layer)] = jnp.concatenate(
            [_wgrad(h_mem, dmk, f"g_w_memk{layer}"), _wgrad(h_mem, dmv, f"g_w_memv{layer}")], axis=1)
        _, dg = _mm_nt_rmsbwd([(dmk, 0), (dmv, 1)], full["w_memkv"][layer], mem2d, ln_mem_g[layer],
                              name=f"d_mem_in{layer}", want_dx=False)
        small[("ln_mem_g", layer)] = dg[0]
        return dqm

    dx3 = conv_ffn_bwd(dx4, x3, u1, h_ffn1, a1, 1)
    do_cat1 = _mm_nt(dx3, full["w_out"][1], name="d_o_cat1", tm=min(512, T), tn=512, out_dtype=BF16)
    grads[("w_out", 1)] = _wgrad(o_cat1, dx3, "g_w_out1")
    dq1, dk1, dv1 = _sb_bwd_g(proj_b, kv, kv, do_cat1, rt1, name="sb_bwd", B=B, S=S, P=PM, q_cb=0, k_cb=0, v_cb=PM,
                            do_cb=0)
    dqm1 = mem_bwd(proj_b, PM, memkv1, h_mem1, do_cat1, o_mem1, lse_m1, 1)
    grads["w_in_b"] = jnp.concatenate([_wgrad(h_mix1, dq1, "g_w_in_b_q"), _wgrad(h_mix1, dqm1, "g_w_in_b_m")], axis=1)
    grads["w_kv"] = jnp.concatenate([_wgrad(h_kv, dk1, "g_w_kv_k"), _wgrad(h_kv, dv1, "g_w_kv_v")], axis=1)
    tok = start_reduce("mix1", [("w_out", 1), "w_in_b", "w_kv", ("w_memkv", 1)], ["rows2", "rows2", "cols2", "rows2"])
    dx2, dg = _mm_nt_rmsbwd([(dq1, 0), (dqm1, MAIN_W // MEM_W)], wb, x2, ln_mix_g[1] + tok, name="d_mix_in1", dres=dx3)
    small[("ln_mix_g", 1)] = dg[0]
    dx2, dg = _mm_nt_rmsbwd([(dk1, 0), (dv1, 1)], wkv, x2, ln_kv_g, name="d_kv_in", dres=dx2)
    small["ln_kv_g"] = dg[0]
    dx1 = conv_ffn_bwd(dx2, x1, u0, h_ffn0, a0, 0)
    do_cat0 = _mm_nt(dx1, full["w_out"][0], name="d_o_cat0", tm=min(512, T), tn=512, out_dtype=BF16)
    grads[("w_out", 0)] = _wgrad(o_cat0, dx1, "g_w_out0")
    dq0, dk0, dv0, dcs = _fox_bwd_g(proj_a, proj_a, proj_a, do_cat0, lse0, cr, name="fox_bwd", B=B, S=S, P=PM, q_cb=0,
                                  k_cb=PM, v_cb=2 * PM, do_cb=0)
    dqm0 = mem_bwd(proj_a, 3 * PM, memkv0, h_mem0, do_cat0, o_mem0, lse_m0, 0)
    dc2d = _pad_rows(dcs[:, :, 0, :].transpose(0, 2, 1).reshape(T, N_MAIN_HEADS), LANES, 1)
    df, db_f = _forget_cumsum_bwd(dc2d, f_logit, b_f, B=B, S=S, name="forget_cumsum_bwd")
    a_parts = [(dq0, 0), (dk0, 1), (dv0, 2), (dqm0, n_qkv // MEM_W), (df, n_main // LANES)]
    g_wa = jnp.concatenate([_wgrad(h_mix0, p, f"g_w_in_a{k}") for k, (p, _) in enumerate(a_parts)], axis=1)
    grads["w_in_a"] = jnp.concatenate([g_wa[:, :n_qkv], g_wa[:, n_main:n_main + N_MAIN_HEADS], g_wa[:, n_qkv:n_main]],
                                      axis=1)
    tok = start_reduce("mix0", [("w_out", 0), ("w_memkv", 0), "w_in_a"], ["rows2", "rows2", "rows2"])
    dx0, dg = _mm_nt_rmsbwd(a_parts, wa, x2d, ln_mix_g[0] + tok, name="d_mix_in0", dres=dx1)
    small[("ln_mix_g", 0)] = dg[0]
    grad_x = dx0.reshape(B, S, D)

    pieces = {}
    for gname, keys, ssem, rsem, thru, lands in reduce_groups:
        thru, lands = _scatter_wait(ssem, rsem, thru, lands, _slot, dx0, name=f"reduce_wait_{gname}")
        for key, mine, land in zip(keys, thru, lands):
            own = lax.dynamic_index_in_dim(mine, my_idx, 0, keepdims=False)
            land = lax.dynamic_update_index_in_dim(land, own, my_idx, 0)
            tag = key if isinstance(key, str) else f"{key[0]}{key[1]}"
            pieces[key] = _sum_slots(land, name=f"sum_{tag}")

    def both_small(name):
        return jnp.stack([small[(name, 0)], small[(name, 1)]])

    small_list = [("ln_mix_g", both_small("ln_mix_g")), ("b_f_a", db_f[:, :N_MAIN_HEADS]), ("ln_kv_g", small["ln_kv_g"]),
                  ("ln_mem_g", both_small("ln_mem_g")), ("ln_ffn_g", both_small("ln_ffn_g")),
                  ("conv_w", both_small("conv_w")), ("conv_b", both_small("conv_b")), ("final_g", dg_final[0]),
                  ("loss", loss_part[0, :1])]
    sm_rows = []
    for _, a in small_list:
        flat = a.reshape(-1)
        sm_rows.append(_pad_rows(flat, _round_up(flat.size, 8 * LANES), 0).reshape(-1, LANES))
    spack = jnp.concatenate(sm_rows, axis=0)
    ssum = _sum_slots(_all_gather_small(spack, name="gather_small_grads"), name="sum_small")

    red = {}
    for n in ("w_in_a", "w_in_b", "w_kv"):
        red[n] = pieces[n].reshape(shards[n].shape)
    for n in ("w_memkv", "w_out", "w_up", "w_down"):
        red[n] = jnp.stack([pieces[(n, 0)], pieces[(n, 1)]])
    off = 0
    for (n, a), rows in zip(small_list, sm_rows):
        red[n] = ssum[off:off + rows.shape[0]].reshape(-1)[:a.size].reshape(a.shape)
        off += rows.shape[0]
    loss = red["loss"][0]
    shard_cols = conv_w.shape[2]
    red["conv_w"] = lax.dynamic_slice_in_dim(red["conv_w"], my_idx * shard_cols, shard_cols, axis=2)
    red["b_f_a"] = red["b_f_a"].reshape(b_f_a.shape)

    weights = {"ln_mix_g": ln_mix_g, "w_in_a": w_in_a, "b_f_a": b_f_a, "w_in_b": w_in_b, "ln_kv_g": ln_kv_g,
               "w_kv": w_kv, "ln_mem_g": ln_mem_g, "w_memkv": w_memkv, "w_out": w_out, "ln_ffn_g": ln_ffn_g,
               "w_up": w_up, "conv_w": conv_w, "conv_b": conv_b, "w_down": w_down, "final_g": final_g}
    m_in = {"ln_mix_g": m_ln_mix_g, "w_in_a": m_w_in_a, "b_f_a": m_b_f_a, "w_in_b": m_w_in_b, "ln_kv_g": m_ln_kv_g,
            "w_kv": m_w_kv, "ln_mem_g": m_ln_mem_g, "w_memkv": m_w_memkv, "w_out": m_w_out, "ln_ffn_g": m_ln_ffn_g,
            "w_up": m_w_up, "conv_w": m_conv_w, "conv_b": m_conv_b, "w_down": m_w_down, "final_g": m_final_g}
    v_in = {"ln_mix_g": v_ln_mix_g, "w_in_a": v_w_in_a, "b_f_a": v_b_f_a, "w_in_b": v_w_in_b, "ln_kv_g": v_ln_kv_g,
            "w_kv": v_w_kv, "ln_mem_g": v_ln_mem_g, "w_memkv": v_w_memkv, "w_out": v_w_out, "ln_ffn_g": v_ln_ffn_g,
            "w_up": v_w_up, "conv_w": v_conv_w, "conv_b": v_conv_b, "w_down": v_w_down, "final_g": v_final_g}
    order = list(weights)
    big_names = [n for n, _ in BIG]
    g_out, d_out, nm_out, nv_out = {}, {}, {}, {}
    for n in big_names + ["conv_w"]:
        w = weights[n]
        cols = w.shape[-1]
        g = red[n].reshape(w.shape)
        d, nm, nv = _adamw(w.reshape(-1, cols), g.reshape(-1, cols), m_in[n].reshape(-1, cols),
                           v_in[n].reshape(-1, cols), name=f"adamw_{n}")
        g_out[n], d_out[n], nm_out[n], nv_out[n] = g, d.reshape(w.shape), nm.reshape(w.shape), nv.reshape(w.shape)
    small_names = [n for n in order if n not in g_out]

    def pack_small(src):
        rows = []
        for n in small_names:
            flat = src[n].reshape(-1)
            rows.append(_pad_rows(flat, _round_up(flat.size, 8 * LANES), 0).reshape(-1, LANES))
        return jnp.concatenate(rows, axis=0), [r.shape[0] for r in rows]

    red_small = {n: red[n].reshape(weights[n].shape) for n in small_names}
    wp, counts = pack_small(weights)
    gp, _ = pack_small(red_small)
    mp, _ = pack_small(m_in)
    vp, _ = pack_small(v_in)
    dp, nmp, nvp = _adamw(wp, gp, mp, vp, name="adamw_small")
    off = 0
    for n, cnt in zip(small_names, counts):
        shp = weights[n].shape
        size = weights[n].size
        g_out[n] = red_small[n]
        d_out[n] = dp[off:off + cnt].reshape(-1)[:size].reshape(shp)
        nm_out[n] = nmp[off:off + cnt].reshape(-1)[:size].reshape(shp)
        nv_out[n] = nvp[off:off + cnt].reshape(-1)[:size].reshape(shp)
        off += cnt

    return (loss, grad_x, *[g_out[n] for n in order], *[d_out[n] for n in order],
            *[nm_out[n] for n in order], *[nv_out[n] for n in order])
```

```python
import functools

import jax
import jax.numpy as jnp
from jax import lax
from jax.experimental import pallas as pl
from jax.experimental.pallas import tpu as pltpu

F32 = jnp.float32
BF16 = jnp.bfloat16
LANES = 128
HEAD_DIM = 64
N_MAIN_HEADS = 12
N_MEM_HEADS = 4
MAIN_W = N_MAIN_HEADS * HEAD_DIM
MEM_W = N_MEM_HEADS * HEAD_DIM
SCALE = HEAD_DIM ** -0.5
EPS = 1e-6
NEG = -1e30
N_DEV = 8
ATT_TILE = 256
MEM_Q_TILE = 1024
VMEM_BIG = 56 * 2 ** 20
MESH = pl.DeviceIdType.MESH

ADAM_LR = 0.001
ADAM_B1 = 0.9
ADAM_B2 = 0.999
ADAM_EPS = 1e-08
ADAM_WD = 0.01
ADAM_STEP = 10

NT = (((1,), (1,)), ((), ()))
TN = (((0,), (0,)), ((), ()))


def _pc(body, *, name, out_shape, grid=None, in_specs=None, out_specs=None, scratch_shapes=(),
        semantics=None, vmem=None):
    kw = {}
    if grid is not None:
        kw["grid"] = grid
    params = pltpu.CompilerParams(dimension_semantics=semantics, vmem_limit_bytes=vmem)
    return pl.pallas_call(body, name=name, out_shape=out_shape, in_specs=in_specs, out_specs=out_specs,
                          scratch_shapes=list(scratch_shapes), compiler_params=params, **kw)


def _sds(shape, dtype):
    return jax.ShapeDtypeStruct(shape, dtype)


def _mm_fwd(a, w, *, name, tm, tn, out_dtype, g=None, res=None, col0=0, ncols=None, save_h=False):
    m_rows, k = a.shape
    n = w.shape[1] if ncols is None else ncols
    grid = (m_rows // tm, n // tn)
    norm = g is not None

    def body(*refs):
        refs = list(refs)
        a_ref = refs.pop(0)
        g_ref = refs.pop(0) if norm else None
        w_ref = refs.pop(0)
        res_ref = refs.pop(0) if res is not None else None
        o_ref = refs.pop(0)
        hout_ref = refs.pop(0) if save_h else None
        h_ref = refs.pop(0) if norm else None
        if norm:
            @pl.when(pl.program_id(1) == 0)
            def _():
                xv = a_ref[...]
                r = lax.rsqrt(jnp.mean(xv * xv, axis=-1, keepdims=True) + EPS)
                h = ((xv * r) * g_ref[...]).astype(BF16)
                h_ref[...] = h
                if save_h:
                    hout_ref[...] = h
            lhs = h_ref[...]
        else:
            lhs = a_ref[...].astype(BF16)
        acc = jnp.dot(lhs, w_ref[...], preferred_element_type=F32)
        if res is not None:
            acc = acc + res_ref[...]
        o_ref[...] = acc.astype(out_dtype)

    in_specs = [pl.BlockSpec((tm, k), lambda i, j: (i, 0))]
    args = [a]
    if norm:
        in_specs.append(pl.BlockSpec((1, k), lambda i, j: (0, 0)))
        args.append(g.reshape(1, k))
    in_specs.append(pl.BlockSpec((k, tn), lambda i, j: (0, j + col0)))
    args.append(w)
    if res is not None:
        in_specs.append(pl.BlockSpec((tm, tn), lambda i, j: (i, j)))
        args.append(res)
    out_shape = [_sds((m_rows, n), out_dtype)]
    out_specs = [pl.BlockSpec((tm, tn), lambda i, j: (i, j))]
    if save_h:
        out_shape.append(_sds((m_rows, k), BF16))
        out_specs.append(pl.BlockSpec((tm, k), lambda i, j: (i, 0)))
    scratch = [pltpu.VMEM((tm, k), BF16)] if norm else []
    outs = _pc(body, name=name, out_shape=out_shape, grid=grid, in_specs=in_specs, out_specs=out_specs,
               scratch_shapes=scratch, semantics=("arbitrary", "arbitrary"), vmem=VMEM_BIG)(*args)
    return outs if save_h else outs[0]


def _mm_nt(a, w, *, name, tm, tn, out_dtype):
    m_rows, k = a.shape
    n = w.shape[0]

    def body(a_ref, w_ref, o_ref):
        acc = lax.dot_general(a_ref[...].astype(BF16), w_ref[...], NT, preferred_element_type=F32)
        o_ref[...] = acc.astype(out_dtype)

    return _pc(body, name=name, out_shape=_sds((m_rows, n), out_dtype), grid=(m_rows // tm, n // tn),
               in_specs=[pl.BlockSpec((tm, k), lambda i, j: (i, 0)), pl.BlockSpec((tn, k), lambda i, j: (j, 0))],
               out_specs=pl.BlockSpec((tm, tn), lambda i, j: (i, j)),
               semantics=("arbitrary", "arbitrary"), vmem=VMEM_BIG)(a, w)


def _mm_tn(a, b, *, name, ta, tn, tt):
    t_rows, ka = a.shape
    n = b.shape[1]
    nt = t_rows // tt

    def body(a_ref, b_ref, o_ref, acc_ref):
        t = pl.program_id(2)

        @pl.when(t == 0)
        def _():
            acc_ref[...] = jnp.zeros_like(acc_ref)

        acc_ref[...] += lax.dot_general(a_ref[...].astype(BF16), b_ref[...].astype(BF16), TN,
                                        preferred_element_type=F32)

        @pl.when(t == nt - 1)
        def _():
            o_ref[...] = acc_ref[...].astype(BF16)

    return _pc(body, name=name, out_shape=_sds((ka, n), BF16), grid=(ka // ta, n // tn, nt),
               in_specs=[pl.BlockSpec((tt, ta), lambda i, j, t: (t, i)),
                         pl.BlockSpec((tt, tn), lambda i, j, t: (t, j))],
               out_specs=pl.BlockSpec((ta, tn), lambda i, j, t: (i, j)),
               scratch_shapes=[pltpu.VMEM((ta, tn), F32)],
               semantics=("arbitrary", "arbitrary", "arbitrary"), vmem=VMEM_BIG)(a, b)


def _wgrad(a, b, name):
    t_rows, ka = a.shape
    n = b.shape[1]
    ta = ka if ka <= 1024 else ka // 2
    tn = n
    while ta * tn * 4 > 6 * 2 ** 20 and tn % 256 == 0:
        tn //= 2
    tt = min(512, t_rows)
    return _mm_tn(a, b, name=name, ta=ta, tn=tn, tt=tt)


def _mm_nt_rmsbwd(parts, w, x, g, *, name, dres=None, want_dx=True):
    m_rows, d = x.shape
    tm = min(256, m_rows)
    n_parts = len(parts)

    def body(*refs):
        refs = list(refs)
        dy_refs = [refs.pop(0) for _ in range(n_parts)]
        w_refs = [refs.pop(0) for _ in range(n_parts)]
        x_ref = refs.pop(0)
        g_ref = refs.pop(0)
        dres_ref = refs.pop(0) if dres is not None else None
        dx_ref = refs.pop(0) if want_dx else None
        dg_ref = refs.pop(0)

        @pl.when(pl.program_id(0) == 0)
        def _():
            dg_ref[...] = jnp.zeros_like(dg_ref)

        dh = None
        for dy_ref, w_ref in zip(dy_refs, w_refs):
            t = lax.dot_general(dy_ref[...].astype(BF16), w_ref[...], NT, preferred_element_type=F32)
            dh = t if dh is None else dh + t
        xv = x_ref[...]
        r = lax.rsqrt(jnp.mean(xv * xv, axis=-1, keepdims=True) + EPS)
        xh = xv * r
        dg_ref[...] += jnp.sum(dh * xh, axis=0, keepdims=True)
        if want_dx:
            dhg = dh * g_ref[...]
            dx = r * (dhg - xh * jnp.mean(dhg * xh, axis=-1, keepdims=True))
            if dres is not None:
                dx = dx + dres_ref[...]
            dx_ref[...] = dx

    in_specs, args = [], []
    for dy, _ in parts:
        in_specs.append(pl.BlockSpec((tm, dy.shape[1]), lambda i: (i, 0)))
        args.append(dy)
    for dy, cb in parts:
        in_specs.append(pl.BlockSpec((d, dy.shape[1]), functools.partial(lambda i, cb: (0, cb), cb=cb)))
        args.append(w)
    in_specs += [pl.BlockSpec((tm, d), lambda i: (i, 0)), pl.BlockSpec((1, d), lambda i: (0, 0))]
    args += [x, g.reshape(1, d)]
    if dres is not None:
        in_specs.append(pl.BlockSpec((tm, d), lambda i: (i, 0)))
        args.append(dres)
    out_shape, out_specs = [], []
    if want_dx:
        out_shape.append(_sds((m_rows, d), F32))
        out_specs.append(pl.BlockSpec((tm, d), lambda i: (i, 0)))
    out_shape.append(_sds((1, d), F32))
    out_specs.append(pl.BlockSpec((1, d), lambda i: (0, 0)))
    outs = _pc(body, name=name, out_shape=out_shape, grid=(m_rows // tm,), in_specs=in_specs,
               out_specs=out_specs, semantics=("arbitrary",), vmem=VMEM_BIG)(*args)
    return (outs[0], outs[1]) if want_dx else (None, outs[0])


def _loss_head(x, g, tgt, *, name):
    m_rows, d = x.shape
    tm = min(256, m_rows)

    def body(x_ref, g_ref, t_ref, dx_ref, dg_ref, loss_ref):
        @pl.when(pl.program_id(0) == 0)
        def _():
            dg_ref[...] = jnp.zeros_like(dg_ref)
            loss_ref[...] = jnp.zeros_like(loss_ref)

        xv = x_ref[...]
        r = lax.rsqrt(jnp.mean(xv * xv, axis=-1, keepdims=True) + EPS)
        xh = xv * r
        gv = g_ref[...]
        err = xh * gv - t_ref[...]
        per_tok = jnp.mean(err * err, axis=-1, keepdims=True)
        loss_ref[...] += 0.5 * jnp.sum(per_tok, axis=0, keepdims=True)
        dout = err * (1.0 / d)
        dg_ref[...] += jnp.sum(dout * xh, axis=0, keepdims=True)
        dhg = dout * gv
        dx_ref[...] = r * (dhg - xh * jnp.mean(dhg * xh, axis=-1, keepdims=True))

    row = pl.BlockSpec((tm, d), lambda i: (i, 0))
    return _pc(body, name=name, out_shape=[_sds((m_rows, d), F32), _sds((1, d), F32), _sds((1, LANES), F32)],
               grid=(m_rows // tm,), in_specs=[row, pl.BlockSpec((1, d), lambda i: (0, 0)), row],
               out_specs=[row, pl.BlockSpec((1, d), lambda i: (0, 0)), pl.BlockSpec((1, LANES), lambda i: (0, 0))],
               semantics=("arbitrary",))(x, g.reshape(1, d), tgt)


def _split3(v):
    hi = v.astype(BF16)
    r1 = v - hi.astype(F32)
    mid = r1.astype(BF16)
    lo = (r1 - mid.astype(F32)).astype(BF16)
    return hi, mid, lo


def _split2(v):
    hi = v.astype(BF16)
    lo = (v - hi.astype(F32)).astype(BF16)
    return hi, lo


def _tri_dot3(tri, v):
    hi, mid, lo = _split3(v)
    return (jnp.dot(tri, hi, preferred_element_type=F32) + jnp.dot(tri, mid, preferred_element_type=F32)
            + jnp.dot(tri, lo, preferred_element_type=F32))


def _dot_tri2(v, tri):
    hi, lo = _split2(v)
    return jnp.dot(hi, tri, preferred_element_type=F32) + jnp.dot(lo, tri, preferred_element_type=F32)


def _log_sigmoid(v):
    return jnp.minimum(v, 0.0) - jnp.log(1.0 + jnp.exp(-jnp.abs(v)))


def _forget_cumsum(f_logit, b_f, *, B, S, name):
    ch = min(256, S)
    nch = S // ch

    def body(f_ref, b_ref, c_ref):
        r_i = lax.broadcasted_iota(jnp.int32, (ch, ch), 0)
        c_i = lax.broadcasted_iota(jnp.int32, (ch, ch), 1)
        tri = (c_i <= r_i).astype(BF16)
        bv = b_ref[...]

        def step(k, carry):
            rows = pl.ds(pl.multiple_of(k * ch, ch), ch)
            lf = _log_sigmoid(f_ref[rows, :] + bv)
            c_ref[rows, :] = _tri_dot3(tri, lf) + carry
            return carry + jnp.sum(lf, axis=0, keepdims=True)

        lax.fori_loop(0, nch, step, jnp.zeros((1, LANES), F32))

    blk = pl.BlockSpec((S, LANES), lambda b: (b, 0))
    return _pc(body, name=name, out_shape=_sds((B * S, LANES), F32), grid=(B,),
               in_specs=[blk, pl.BlockSpec((1, LANES), lambda b: (0, 0))], out_specs=blk,
               semantics=("arbitrary",))(f_logit, b_f)


def _forget_cumsum_bwd(dc, f_logit, b_f, *, B, S, name):
    ch = min(256, S)
    nch = S // ch

    def body(dc_ref, f_ref, b_ref, df_ref, db_ref):
        @pl.when(pl.program_id(0) == 0)
        def _():
            db_ref[...] = jnp.zeros_like(db_ref)

        r_i = lax.broadcasted_iota(jnp.int32, (ch, ch), 0)
        c_i = lax.broadcasted_iota(jnp.int32, (ch, ch), 1)
        tri = (c_i >= r_i).astype(BF16)
        bv = b_ref[...]

        def step(kk, carry):
            tail, dbs = carry
            k = nch - 1 - kk
            rows = pl.ds(pl.multiple_of(k * ch, ch), ch)
            dcv = dc_ref[rows, :]
            dlf = _tri_dot3(tri, dcv) + tail
            z = f_ref[rows, :] + bv
            df = dlf * (1.0 / (1.0 + jnp.exp(z)))
            df_ref[rows, :] = df.astype(BF16)
            return tail + jnp.sum(dcv, axis=0, keepdims=True), dbs + jnp.sum(df, axis=0, keepdims=True)

        zero = jnp.zeros((1, LANES), F32)
        _, dbs = lax.fori_loop(0, nch, step, (zero, zero))
        db_ref[...] += dbs

    blk = pl.BlockSpec((S, LANES), lambda b: (b, 0))
    one = pl.BlockSpec((1, LANES), lambda b: (0, 0))
    return _pc(body, name=name, out_shape=[_sds((B * S, LANES), BF16), _sds((1, LANES), F32)], grid=(B,),
               in_specs=[blk, blk, one], out_specs=[blk, one], semantics=("arbitrary",))(dc, f_logit, b_f)


def _head_mask(lane, hh):
    return (lane < HEAD_DIM) if hh == 0 else (lane >= HEAD_DIM)


def _col_spec(rows, nblk_rows, cb):
    return pl.BlockSpec((rows, LANES), lambda b, p, i: (b * nblk_rows + i, cb + p))


def _kv_spec(rows, cb):
    return pl.BlockSpec((rows, LANES), lambda b, p, i: (b, cb + p))


def _stat_col_spec(tq):
    return pl.BlockSpec((1, 2, tq, 1), lambda b, p, i: (b, p, i, 0))


def _stat_row_spec(S):
    return pl.BlockSpec((1, 2, 1, S), lambda b, p, i: (b, p, 0, 0))


def _softmax_fwd(qa, ka, va, *, name, B, S, Sk, P, q_cb, k_cb, v_cb, causal, cc=None, cr=None):
    tq = min(ATT_TILE if causal else MEM_Q_TILE, S)
    tk = min(ATT_TILE, Sk)
    nq, nk = S // tq, Sk // tk
    decay = cc is not None
    assert not causal or (tq == tk and S == Sk)

    def body(*refs):
        if decay:
            q_ref, k_ref, v_ref, cc_ref, cr_ref, o_ref, lse_ref = refs
        else:
            q_ref, k_ref, v_ref, o_ref, lse_ref = refs
        i = pl.program_id(2)
        q = q_ref[...]
        lane = lax.broadcasted_iota(jnp.int32, (tq, LANES), 1)
        row = lax.broadcasted_iota(jnp.int32, (tq, tk), 0) + i * tq
        col0 = lax.broadcasted_iota(jnp.int32, (tq, tk), 1)
        outs = []
        for hh in range(2):
            qh = jnp.where(_head_mask(lane, hh), q, jnp.zeros_like(q))

            def step(kb, carry, hh=hh, qh=qh):
                m, l, acc = carry
                ks = pl.multiple_of(kb * tk, tk)
                kblk = k_ref[pl.ds(ks, tk), :]
                vblk = v_ref[pl.ds(ks, tk), :]
                s = lax.dot_general(qh, kblk, NT, preferred_element_type=F32) * SCALE
                if decay:
                    s = s + (cc_ref[0, hh] - cr_ref[0, hh, :, pl.ds(ks, tk)])
                if causal:
                    s = jnp.where(col0 + kb * tk <= row, s, NEG)
                m_new = jnp.maximum(m, jnp.max(s, axis=-1, keepdims=True))
                alpha = jnp.exp(m - m_new)
                p = jnp.exp(s - m_new)
                l = alpha * l + jnp.sum(p, axis=-1, keepdims=True)
                acc = alpha * acc + jnp.dot(p.astype(BF16), vblk, preferred_element_type=F32)
                return m_new, l, acc

            init = (jnp.full((tq, 1), NEG, F32), jnp.zeros((tq, 1), F32), jnp.zeros((tq, LANES), F32))
            m, l, acc = lax.fori_loop(0, (i + 1) if causal else nk, step, init)
            outs.append(acc / l)
            lse_ref[0, hh] = m + jnp.log(l)
        o_ref[...] = jnp.where(lane < HEAD_DIM, outs[0], outs[1]).astype(BF16)

    in_specs = [_col_spec(tq, nq, q_cb), _kv_spec(Sk, k_cb), _kv_spec(Sk, v_cb)]
    args = [qa, ka, va]
    if decay:
        in_specs += [_stat_col_spec(tq), _stat_row_spec(S)]
        args += [cc, cr]
    return _pc(body, name=name,
               out_shape=[_sds((B * S, P * LANES), BF16), _sds((B, 2 * P, S, 1), F32)],
               grid=(B, P, nq), in_specs=in_specs, out_specs=[_col_spec(tq, nq, 0), _stat_col_spec(tq)],
               semantics=("arbitrary", "arbitrary", "arbitrary"), vmem=VMEM_BIG)(*args)


def _softmax_bwd(qa, ka, va, doa, oa, lse, *, name, B, S, Sk, P, q_cb, k_cb, v_cb, do_cb, causal,
                 cc=None, cr=None):
    tq = min(ATT_TILE if causal else MEM_Q_TILE, S)
    tk = min(ATT_TILE, Sk)
    nq, nk = S // tq, Sk // tk
    decay = cc is not None

    def body(*refs):
        if decay:
            q_ref, k_ref, v_ref, do_ref, o_ref, lse_ref, cc_ref, cr_ref, dq_ref, dk_ref, dv_ref, dcs_ref = refs
        else:
            q_ref, k_ref, v_ref, do_ref, o_ref, lse_ref, dq_ref, dk_ref, dv_ref = refs
        i = pl.program_id(2)

        @pl.when(i == 0)
        def _():
            dk_ref[...] = jnp.zeros_like(dk_ref)
            dv_ref[...] = jnp.zeros_like(dv_ref)
            if decay:
                dcs_ref[...] = jnp.zeros_like(dcs_ref)

        q = q_ref[...]
        do = do_ref[...]
        prod = do.astype(F32) * o_ref[...].astype(F32)
        lane = lax.broadcasted_iota(jnp.int32, (tq, LANES), 1)
        row = lax.broadcasted_iota(jnp.int32, (tq, tk), 0) + i * tq
        col0 = lax.broadcasted_iota(jnp.int32, (tq, tk), 1)
        dqs = []
        for hh in range(2):
            hmask = _head_mask(lane, hh)
            qh = jnp.where(hmask, q, jnp.zeros_like(q))
            doh = jnp.where(hmask, do, jnp.zeros_like(do))
            lse_h = lse_ref[0, hh]
            n_blocks = (i + 1) if causal else nk

            def probs(kb, hh=hh, qh=qh, doh=doh, lse_h=lse_h):
                ks = pl.multiple_of(kb * tk, tk)
                kblk = k_ref[pl.ds(ks, tk), :]
                vblk = v_ref[pl.ds(ks, tk), :]
                s = lax.dot_general(qh, kblk, NT, preferred_element_type=F32) * SCALE
                if decay:
                    s = s + (cc_ref[0, hh] - cr_ref[0, hh, :, pl.ds(ks, tk)])
                if causal:
                    s = jnp.where(col0 + kb * tk <= row, s, NEG)
                p = jnp.exp(s - lse_h)
                dp = lax.dot_general(doh, vblk, NT, preferred_element_type=F32)
                return ks, kblk, p, dp

            if decay:
                def delta_step(kb, acc):
                    _, _, p, dp = probs(kb)
                    return acc + jnp.sum(p * dp, axis=-1, keepdims=True)

                delta = lax.fori_loop(0, n_blocks, delta_step, jnp.zeros((tq, 1), F32))
            else:
                delta = jnp.sum(jnp.where(hmask, prod, 0.0), axis=-1, keepdims=True)

            def step(kb, dq_acc, hh=hh, qh=qh, doh=doh, delta=delta):
                ks, kblk, p, dp = probs(kb)
                ds = p * (dp - delta)
                dsb = ds.astype(BF16)
                dk_ref[pl.ds(ks, tk), :] += lax.dot_general(dsb, qh, TN, preferred_element_type=F32) * SCALE
                dv_ref[pl.ds(ks, tk), :] += lax.dot_general(p.astype(BF16), doh, TN, preferred_element_type=F32)
                if decay:
                    dcs_ref[0, hh, :, pl.ds(ks, tk)] -= jnp.sum(ds, axis=0, keepdims=True)
                return dq_acc + jnp.dot(dsb, kblk, preferred_element_type=F32)

            dqs.append(lax.fori_loop(0, n_blocks, step, jnp.zeros((tq, LANES), F32)) * SCALE)
        dq_ref[...] = jnp.where(lane < HEAD_DIM, dqs[0], dqs[1]).astype(BF16)

    in_specs = [_col_spec(tq, nq, q_cb), _kv_spec(Sk, k_cb), _kv_spec(Sk, v_cb), _col_spec(tq, nq, do_cb),
                _col_spec(tq, nq, 0), _stat_col_spec(tq)]
    args = [qa, ka, va, doa, oa, lse]
    out_shape = [_sds((B * S, P * LANES), BF16), _sds((B * Sk, P * LANES), F32), _sds((B * Sk, P * LANES), F32)]
    out_specs = [_col_spec(tq, nq, 0), _kv_spec(Sk, 0), _kv_spec(Sk, 0)]
    if decay:
        in_specs += [_stat_col_spec(tq), _stat_row_spec(S)]
        args += [cc, cr]
        out_shape.append(_sds((B, 2 * P, 1, S), F32))
        out_specs.append(_stat_row_spec(S))
    return _pc(body, name=name, out_shape=out_shape, grid=(B, P, nq), in_specs=in_specs, out_specs=out_specs,
               semantics=("arbitrary", "arbitrary", "arbitrary"), vmem=VMEM_BIG)(*args)


def _sb_terms(qh, kblk, row, col0, kb, tk):
    z = lax.dot_general(qh, kblk, NT, preferred_element_type=F32) * SCALE
    causal = (col0 + kb * tk) < row
    sp = jnp.maximum(z, 0.0) + jnp.log(1.0 + jnp.exp(-jnp.abs(z)))
    ls = z - sp
    lm = jnp.where(causal, -sp, 0.0)
    return causal, ls, lm


def _stickbreak_fwd(qa, ka, va, *, name, B, S, P, q_cb, k_cb, v_cb):
    tq = tk = min(ATT_TILE, S)
    nq = S // tq

    def body(q_ref, k_ref, v_ref, o_ref, rt_ref):
        i = pl.program_id(2)
        q = q_ref[...]
        lane = lax.broadcasted_iota(jnp.int32, (tq, LANES), 1)
        row = lax.broadcasted_iota(jnp.int32, (tq, tk), 0) + i * tq
        col0 = lax.broadcasted_iota(jnp.int32, (tq, tk), 1)
        t_r = lax.broadcasted_iota(jnp.int32, (tk, tk), 0)
        t_c = lax.broadcasted_iota(jnp.int32, (tk, tk), 1)
        after = (t_r > t_c).astype(BF16)
        outs = []
        for hh in range(2):
            qh = jnp.where(_head_mask(lane, hh), q, jnp.zeros_like(q))

            def step(jj, carry, qh=qh):
                run, acc = carry
                kb = i - jj
                ks = pl.multiple_of(kb * tk, tk)
                kblk = k_ref[pl.ds(ks, tk), :]
                vblk = v_ref[pl.ds(ks, tk), :]
                causal, ls, lm = _sb_terms(qh, kblk, row, col0, kb, tk)
                suf = _dot_tri2(lm, after)
                a = jnp.where(causal, jnp.exp(ls + run + suf), 0.0)
                acc = acc + jnp.dot(a.astype(BF16), vblk, preferred_element_type=F32)
                return run + jnp.sum(lm, axis=-1, keepdims=True), acc

            run, acc = lax.fori_loop(0, i + 1, step, (jnp.zeros((tq, 1), F32), jnp.zeros((tq, LANES), F32)))
            outs.append(acc)
            rt_ref[0, hh] = run
        o_ref[...] = jnp.where(lane < HEAD_DIM, outs[0], outs[1]).astype(BF16)

    return _pc(body, name=name, out_shape=[_sds((B * S, P * LANES), BF16), _sds((B, 2 * P, S, 1), F32)],
               grid=(B, P, nq), in_specs=[_col_spec(tq, nq, q_cb), _kv_spec(S, k_cb), _kv_spec(S, v_cb)],
               out_specs=[_col_spec(tq, nq, 0), _stat_col_spec(tq)],
               semantics=("arbitrary", "arbitrary", "arbitrary"), vmem=VMEM_BIG)(qa, ka, va)


def _stickbreak_bwd(qa, ka, va, doa, rt, *, name, B, S, P, q_cb, k_cb, v_cb, do_cb):
    tq = tk = min(ATT_TILE, S)
    nq = S // tq

    def body(q_ref, k_ref, v_ref, do_ref, rt_ref, dq_ref, dk_ref, dv_ref):
        i = pl.program_id(2)

        @pl.when(i == 0)
        def _():
            dk_ref[...] = jnp.zeros_like(dk_ref)
            dv_ref[...] = jnp.zeros_like(dv_ref)

        q = q_ref[...]
        do = do_ref[...]
        lane = lax.broadcasted_iota(jnp.int32, (tq, LANES), 1)
        row = lax.broadcasted_iota(jnp.int32, (tq, tk), 0) + i * tq
        col0 = lax.broadcasted_iota(jnp.int32, (tq, tk), 1)
        t_r = lax.broadcasted_iota(jnp.int32, (tk, tk), 0)
        t_c = lax.broadcasted_iota(jnp.int32, (tk, tk), 1)
        upto = (t_r <= t_c).astype(BF16)
        before = (t_r < t_c).astype(BF16)
        dqs = []
        for hh in range(2):
            hmask = _head_mask(lane, hh)
            qh = jnp.where(hmask, q, jnp.zeros_like(q))
            doh = jnp.where(hmask, do, jnp.zeros_like(do))
            rt_h = rt_ref[0, hh]

            def step(kb, carry, qh=qh, doh=doh, rt_h=rt_h):
                pl_sum, pg_sum, dq_acc = carry
                ks = pl.multiple_of(kb * tk, tk)
                kblk = k_ref[pl.ds(ks, tk), :]
                vblk = v_ref[pl.ds(ks, tk), :]
                causal, ls, lm = _sb_terms(qh, kblk, row, col0, kb, tk)
                pin = _dot_tri2(lm, upto)
                a = jnp.where(causal, jnp.exp(ls + (rt_h - pl_sum) - pin), 0.0)
                da = lax.dot_general(doh, vblk, NT, preferred_element_type=F32)
                gm = a * da
                pg = _dot_tri2(gm, before) + pg_sum
                beta = jnp.exp(ls)
                dz = jnp.where(causal, gm * (1.0 - beta) - pg * beta, 0.0)
                dzb = dz.astype(BF16)
                dk_ref[pl.ds(ks, tk), :] += lax.dot_general(dzb, qh, TN, preferred_element_type=F32) * SCALE
                dv_ref[pl.ds(ks, tk), :] += lax.dot_general(a.astype(BF16), doh, TN, preferred_element_type=F32)
                return (pl_sum + jnp.sum(lm, axis=-1, keepdims=True),
                        pg_sum + jnp.sum(gm, axis=-1, keepdims=True),
                        dq_acc + jnp.dot(dzb, kblk, preferred_element_type=F32))

            zc = jnp.zeros((tq, 1), F32)
            _, _, dq_h = lax.fori_loop(0, i + 1, step, (zc, zc, jnp.zeros((tq, LANES), F32)))
            dqs.append(dq_h * SCALE)
        dq_ref[...] = jnp.where(lane < HEAD_DIM, dqs[0], dqs[1]).astype(BF16)

    return _pc(body, name=name,
               out_shape=[_sds((B * S, P * LANES), BF16), _sds((B * S, P * LANES), F32), _sds((B * S, P * LANES), F32)],
               grid=(B, P, nq),
               in_specs=[_col_spec(tq, nq, q_cb), _kv_spec(S, k_cb), _kv_spec(S, v_cb), _col_spec(tq, nq, do_cb),
                         _stat_col_spec(tq)],
               out_specs=[_col_spec(tq, nq, 0), _kv_spec(S, 0), _kv_spec(S, 0)],
               semantics=("arbitrary", "arbitrary", "arbitrary"), vmem=VMEM_BIG)(qa, ka, va, doa, rt)


def _pair_setup(q_ref, tq, tk):
    q = q_ref[...] * jnp.asarray(SCALE, BF16)
    lane = lax.broadcasted_iota(jnp.int32, (tq, LANES), 1)
    masks = [_head_mask(lane, hh) for hh in range(2)]
    qh = [jnp.where(mk, q, jnp.zeros_like(q)) for mk in masks]
    on_or_below = (lax.broadcasted_iota(jnp.int32, (tq, tk), 1) <= lax.broadcasted_iota(jnp.int32, (tq, tk), 0))
    return lane, masks, qh, on_or_below


def _fox_fwd(qa, ka, va, cr, *, name, B, S, P, q_cb, k_cb, v_cb):
    tq = tk = min(ATT_TILE, S)
    nq = S // tq

    def body(q_ref, k_ref, v_ref, cr_ref, o_ref, lse_ref, s_buf, acc_ref, m_ref, l_ref):
        i = pl.program_id(2)
        lane, _, qh, on_or_below = _pair_setup(q_ref, tq, tk)
        m_ref[...] = jnp.full(m_ref.shape, NEG, F32)
        l_ref[...] = jnp.zeros(l_ref.shape, F32)
        acc_ref[...] = jnp.zeros(acc_ref.shape, F32)

        def scores(kb, slot):
            kblk = k_ref[pl.ds(pl.multiple_of(kb * tk, tk), tk), :]
            for hh in range(2):
                s_buf[slot, hh] = lax.dot_general(qh[hh], kblk, NT, preferred_element_type=F32)

        def block(kb, slot, diag):
            ks = pl.multiple_of(kb * tk, tk)
            vblk = v_ref[pl.ds(ks, tk), :]
            ps = []
            for hh in range(2):
                s = s_buf[slot, hh] - cr_ref[0, hh, :, pl.ds(ks, tk)]
                if diag:
                    s = jnp.where(on_or_below, s, NEG)
                m = m_ref[hh]
                m_new = jnp.maximum(m, jnp.max(s, axis=-1, keepdims=True))
                alpha = jnp.exp(m - m_new)
                p = jnp.exp(s - m_new)
                m_ref[hh] = m_new
                l_ref[hh] = alpha * l_ref[hh] + jnp.sum(p, axis=-1, keepdims=True)
                ps.append((alpha, p.astype(BF16)))
            for hh in range(2):
                acc_ref[hh] = ps[hh][0] * acc_ref[hh] + jnp.dot(ps[hh][1], vblk, preferred_element_type=F32)

        def step(kb, _):
            slot = lax.rem(kb, 2)
            scores(kb + 1, 1 - slot)
            block(kb, slot, False)
            return 0

        scores(0, 0)
        lax.fori_loop(0, i, step, 0)
        block(i, lax.rem(i, 2), True)
        l0, l1 = l_ref[0], l_ref[1]
        lse_ref[0, 0] = m_ref[0] + jnp.log(l0)
        lse_ref[0, 1] = m_ref[1] + jnp.log(l1)
        o_ref[...] = jnp.where(lane < HEAD_DIM, acc_ref[0] / l0, acc_ref[1] / l1).astype(BF16)

    return _pc(body, name=name, out_shape=[_sds((B * S, P * LANES), BF16), _sds((B, 2 * P, S, 1), F32)],
               grid=(B, P, nq),
               in_specs=[_col_spec(tq, nq, q_cb), _kv_spec(S, k_cb), _kv_spec(S, v_cb), _stat_row_spec(S)],
               out_specs=[_col_spec(tq, nq, 0), _stat_col_spec(tq)],
               scratch_shapes=[pltpu.VMEM((2, 2, tq, tk), F32), pltpu.VMEM((2, tq, LANES), F32),
                               pltpu.VMEM((2, tq, 1), F32), pltpu.VMEM((2, tq, 1), F32)],
               semantics=("arbitrary", "arbitrary", "arbitrary"), vmem=VMEM_BIG)(qa, ka, va, cr)


def _fox_bwd(qa, ka, va, doa, lse, cr, *, name, B, S, P, q_cb, k_cb, v_cb, do_cb):
    tq = tk = min(ATT_TILE, S)
    nq = S // tq

    def body(q_ref, k_ref, v_ref, do_ref, lse_ref, cr_ref, dq_ref, dk_ref, dv_ref, dcs_ref):
        i = pl.program_id(2)

        @pl.when(i == 0)
        def _():
            dk_ref[...] = jnp.zeros_like(dk_ref)
            dv_ref[...] = jnp.zeros_like(dv_ref)
            dcs_ref[...] = jnp.zeros_like(dcs_ref)

        lane, masks, qh, on_or_below = _pair_setup(q_ref, tq, tk)
        do = do_ref[...]
        doh = [jnp.where(mk, do, jnp.zeros_like(do)) for mk in masks]
        lse_h = [lse_ref[0, hh] for hh in range(2)]

        def probs(kb, hh, diag):
            ks = pl.multiple_of(kb * tk, tk)
            kblk = k_ref[pl.ds(ks, tk), :]
            vblk = v_ref[pl.ds(ks, tk), :]
            s = lax.dot_general(qh[hh], kblk, NT, preferred_element_type=F32) - cr_ref[0, hh, :, pl.ds(ks, tk)]
            if diag:
                s = jnp.where(on_or_below, s, NEG)
            p = jnp.exp(s - lse_h[hh])
            dp = lax.dot_general(doh[hh], vblk, NT, preferred_element_type=F32)
            return ks, kblk, p, dp

        def delta_block(kb, carry, diag):
            out = []
            for hh in range(2):
                _, _, p, dp = probs(kb, hh, diag)
                out.append(carry[hh] + jnp.sum(p * dp, axis=-1, keepdims=True))
            return tuple(out)

        zc = jnp.zeros((tq, 1), F32)
        delta = lax.fori_loop(0, i, lambda kb, c: delta_block(kb, c, False), (zc, zc))
        delta = delta_block(i, delta, True)

        def grad_block(kb, carry, diag):
            out = []
            for hh in range(2):
                ks, kblk, p, dp = probs(kb, hh, diag)
                ds = p * (dp - delta[hh])
                dsb = ds.astype(BF16)
                rows = pl.ds(ks, tk)
                dk_ref[rows, :] += lax.dot_general(dsb, qh[hh], TN, preferred_element_type=F32)
                dv_ref[rows, :] += lax.dot_general(p.astype(BF16), doh[hh], TN, preferred_element_type=F32)
                dcs_ref[0, hh, :, rows] -= jnp.sum(ds, axis=0, keepdims=True)
                out.append(carry[hh] + jnp.dot(dsb, kblk, preferred_element_type=F32))
            return tuple(out)

        za = jnp.zeros((tq, LANES), F32)
        dq = lax.fori_loop(0, i, lambda kb, c: grad_block(kb, c, False), (za, za))
        dq = grad_block(i, dq, True)
        dq_ref[...] = (jnp.where(lane < HEAD_DIM, dq[0], dq[1]) * SCALE).astype(BF16)

    return _pc(body, name=name,
               out_shape=[_sds((B * S, P * LANES), BF16), _sds((B * S, P * LANES), F32), _sds((B * S, P * LANES), F32),
                          _sds((B, 2 * P, 1, S), F32)],
               grid=(B, P, nq),
               in_specs=[_col_spec(tq, nq, q_cb), _kv_spec(S, k_cb), _kv_spec(S, v_cb), _col_spec(tq, nq, do_cb),
                         _stat_col_spec(tq), _stat_row_spec(S)],
               out_specs=[_col_spec(tq, nq, 0), _kv_spec(S, 0), _kv_spec(S, 0), _stat_row_spec(S)],
               semantics=("arbitrary", "arbitrary", "arbitrary"), vmem=VMEM_BIG)(qa, ka, va, doa, lse, cr)


def _sb_logs(qh, kblk):
    z = lax.dot_general(qh, kblk, NT, preferred_element_type=F32)
    nz = -z
    lg = jnp.log(1.0 + jnp.exp(jnp.minimum(z, nz)))
    lm = jnp.minimum(nz, 0.0) - lg
    return lm + z, lm


def _sb_fwd(qa, ka, va, *, name, B, S, P, q_cb, k_cb, v_cb):
    tq = tk = min(ATT_TILE, S)
    nq = S // tq

    def body(q_ref, k_ref, v_ref, o_ref, rt_ref):
        i = pl.program_id(2)
        lane, _, qh, on_or_below = _pair_setup(q_ref, tq, tk)
        t_r = lax.broadcasted_iota(jnp.int32, (tk, tk), 0)
        t_c = lax.broadcasted_iota(jnp.int32, (tk, tk), 1)
        after = (t_r > t_c).astype(BF16)
        below = t_c < t_r

        def block(kb, carry, diag):
            ks = pl.multiple_of(kb * tk, tk)
            kblk = k_ref[pl.ds(ks, tk), :]
            vblk = v_ref[pl.ds(ks, tk), :]
            out = []
            for hh in range(2):
                run, acc = carry[hh]
                ls, lm = _sb_logs(qh[hh], kblk)
                if diag:
                    lm = jnp.where(below, lm, 0.0)
                a = jnp.exp(ls + run + _dot_tri2(lm, after))
                if diag:
                    a = jnp.where(below, a, 0.0)
                acc = acc + jnp.dot(a.astype(BF16), vblk, preferred_element_type=F32)
                out.append((run + jnp.sum(lm, axis=-1, keepdims=True), acc))
            return tuple(out)

        one = (jnp.zeros((tq, 1), F32), jnp.zeros((tq, LANES), F32))
        carry = block(i, (one, one), True)
        (r0, a0), (r1, a1) = lax.fori_loop(0, i, lambda jj, c: block(i - 1 - jj, c, False), carry)
        rt_ref[0, 0] = r0
        rt_ref[0, 1] = r1
        o_ref[...] = jnp.where(lane < HEAD_DIM, a0, a1).astype(BF16)

    return _pc(body, name=name, out_shape=[_sds((B * S, P * LANES), BF16), _sds((B, 2 * P, S, 1), F32)],
               grid=(B, P, nq), in_specs=[_col_spec(tq, nq, q_cb), _kv_spec(S, k_cb), _kv_spec(S, v_cb)],
               out_specs=[_col_spec(tq, nq, 0), _stat_col_spec(tq)],
               semantics=("arbitrary", "arbitrary", "arbitrary"), vmem=VMEM_BIG)(qa, ka, va)


def _sb_bwd(qa, ka, va, doa, rt, *, name, B, S, P, q_cb, k_cb, v_cb, do_cb):
    tq = tk = min(ATT_TILE, S)
    nq = S // tq

    def body(q_ref, k_ref, v_ref, do_ref, rt_ref, dq_ref, dk_ref, dv_ref):
        i = pl.program_id(2)

        @pl.when(i == 0)
        def _():
            dk_ref[...] = jnp.zeros_like(dk_ref)
            dv_ref[...] = jnp.zeros_like(dv_ref)

        lane, masks, qh, _ = _pair_setup(q_ref, tq, tk)
        do = do_ref[...]
        doh = [jnp.where(mk, do, jnp.zeros_like(do)) for mk in masks]
        rt_h = [rt_ref[0, hh] for hh in range(2)]
        t_r = lax.broadcasted_iota(jnp.int32, (tk, tk), 0)
        t_c = lax.broadcasted_iota(jnp.int32, (tk, tk), 1)
        upto = (t_r <= t_c).astype(BF16)
        before = (t_r < t_c).astype(BF16)
        below = t_c < t_r

        def block(kb, carry, diag):
            ks = pl.multiple_of(kb * tk, tk)
            rows = pl.ds(ks, tk)
            kblk = k_ref[rows, :]
            vblk = v_ref[rows, :]
            out = []
            for hh in range(2):
                pl_sum, pg_sum, dq_acc = carry[hh]
                ls, lm = _sb_logs(qh[hh], kblk)
                if diag:
                    lm = jnp.where(below, lm, 0.0)
                a = jnp.exp(ls + (rt_h[hh] - pl_sum) - _dot_tri2(lm, upto))
                if diag:
                    a = jnp.where(below, a, 0.0)
                gm = a * lax.dot_general(doh[hh], vblk, NT, preferred_element_type=F32)
                pg = _dot_tri2(gm, before) + pg_sum
                dz = gm - jnp.exp(ls) * (gm + pg)
                if diag:
                    dz = jnp.where(below, dz, 0.0)
                dzb = dz.astype(BF16)
                dk_ref[rows, :] += lax.dot_general(dzb, qh[hh], TN, preferred_element_type=F32)
                dv_ref[rows, :] += lax.dot_general(a.astype(BF16), doh[hh], TN, preferred_element_type=F32)
                out.append((pl_sum + jnp.sum(lm, axis=-1, keepdims=True),
                            pg_sum + jnp.sum(gm, axis=-1, keepdims=True),
                            dq_acc + jnp.dot(dzb, kblk, preferred_element_type=F32)))
            return tuple(out)

        zc = jnp.zeros((tq, 1), F32)
        one = (zc, zc, jnp.zeros((tq, LANES), F32))
        carry = lax.fori_loop(0, i, lambda kb, c: block(kb, c, False), (one, one))
        (_, _, dq0), (_, _, dq1) = block(i, carry, True)
        dq_ref[...] = (jnp.where(lane < HEAD_DIM, dq0, dq1) * SCALE).astype(BF16)

    return _pc(body, name=name,
               out_shape=[_sds((B * S, P * LANES), BF16), _sds((B * S, P * LANES), F32), _sds((B * S, P * LANES), F32)],
               grid=(B, P, nq),
               in_specs=[_col_spec(tq, nq, q_cb), _kv_spec(S, k_cb), _kv_spec(S, v_cb), _col_spec(tq, nq, do_cb),
                         _stat_col_spec(tq)],
               out_specs=[_col_spec(tq, nq, 0), _kv_spec(S, 0), _kv_spec(S, 0)],
               semantics=("arbitrary", "arbitrary", "arbitrary"), vmem=VMEM_BIG)(qa, ka, va, doa, rt)


HEAD_GROUP = 3


def _g_col_spec(rows, nblk_rows, cb, G):
    return pl.BlockSpec((rows, G * LANES), lambda b, p, i: (b * nblk_rows + i, cb // G + p))


def _g_kv_spec(rows, cb, G):
    return pl.BlockSpec((rows, G * LANES), lambda b, p, i: (b, cb // G + p))


def _g_stat_col_spec(tq, G):
    return pl.BlockSpec((1, 2 * G, tq, 1), lambda b, p, i: (b, p, i, 0))


def _g_stat_row_spec(S, G):
    return pl.BlockSpec((1, 2 * G, 1, S), lambda b, p, i: (b, p, 0, 0))


def _lanes(g):
    return slice(g * LANES, (g + 1) * LANES)


def _streams(x_ref, G, scale=None):
    rows = x_ref.shape[0]
    lane = lax.broadcasted_iota(jnp.int32, (rows, LANES), 1)
    out = []
    for g in range(G):
        x = x_ref[:, _lanes(g)]
        if scale is not None:
            x = x * jnp.asarray(scale, x.dtype)
        for hh in range(2):
            out.append(jnp.where(_head_mask(lane, hh), x, jnp.zeros_like(x)))
    return lane, out


def _wide(stat, width):
    return jnp.tile(stat, (1, width // LANES))


def _fold_lanes(v):
    out = v[:, :LANES]
    for j in range(1, v.shape[1] // LANES):
        out = out + v[:, j * LANES:(j + 1) * LANES]
    return out


def _kv_blocks(ref, ks, tk, G):
    return [ref[pl.ds(ks, tk), _lanes(g)] for g in range(G)]


def _sweep(i, block):
    def step(kb, c):
        block(kb, False)
        return c
    lax.fori_loop(0, i, step, 0)
    block(i, True)


def _fox_fwd_g(qa, ka, va, cr, *, name, B, S, P, q_cb, k_cb, v_cb, G=HEAD_GROUP):
    tq = tk = min(ATT_TILE, S)
    nq = S // tq
    NS = 2 * G

    def body(q_ref, k_ref, v_ref, cr_ref, o_ref, lse_ref, acc_ref, m_ref, l_ref):
        i = pl.program_id(2)
        lane, qh = _streams(q_ref, G, SCALE)
        on_or_below = (lax.broadcasted_iota(jnp.int32, (tq, tk), 1) <= lax.broadcasted_iota(jnp.int32, (tq, tk), 0))
        m_ref[...] = jnp.full(m_ref.shape, NEG, F32)
        l_ref[...] = jnp.zeros(l_ref.shape, F32)
        acc_ref[...] = jnp.zeros(acc_ref.shape, F32)

        def block(kb, diag):
            ks = pl.multiple_of(kb * tk, tk)
            kblk = _kv_blocks(k_ref, ks, tk, G)
            vblk = _kv_blocks(v_ref, ks, tk, G)
            ss = [lax.dot_general(qh[st], kblk[st // 2], NT, preferred_element_type=F32) for st in range(NS)]
            ps = []
            for st in range(NS):
                s = ss[st] - cr_ref[0, st, :, pl.ds(ks, tk)]
                if diag:
                    s = jnp.where(on_or_below, s, NEG)
                m = m_ref[st]
                m_new = jnp.maximum(m, jnp.max(s, axis=-1, keepdims=True))
                alpha = jnp.exp(m - m_new)
                p = jnp.exp(s - _wide(m_new, tk))
                m_ref[st] = m_new
                l_ref[st] = alpha * l_ref[st] + _fold_lanes(p)
                ps.append((alpha, p.astype(BF16)))
            pvs = [jnp.dot(ps[st][1], vblk[st // 2], preferred_element_type=F32) for st in range(NS)]
            for st in range(NS):
                acc_ref[st] = ps[st][0] * acc_ref[st] + pvs[st]

        _sweep(i, block)
        ls = [jnp.sum(l_ref[st], axis=-1, keepdims=True) for st in range(NS)]
        for st in range(NS):
            lse_ref[0, st] = jnp.max(m_ref[st], axis=-1, keepdims=True) + jnp.log(ls[st])
        for g in range(G):
            o_ref[:, _lanes(g)] = jnp.where(lane < HEAD_DIM, acc_ref[2 * g] / ls[2 * g],
                                            acc_ref[2 * g + 1] / ls[2 * g + 1]).astype(BF16)

    return _pc(body, name=name, out_shape=[_sds((B * S, P * LANES), BF16), _sds((B, 2 * P, S, 1), F32)],
               grid=(B, P // G, nq),
               in_specs=[_g_col_spec(tq, nq, q_cb, G), _g_kv_spec(S, k_cb, G), _g_kv_spec(S, v_cb, G),
                         _g_stat_row_spec(S, G)],
               out_specs=[_g_col_spec(tq, nq, 0, G), _g_stat_col_spec(tq, G)],
               scratch_shapes=[pltpu.VMEM((NS, tq, LANES), F32)] * 3,
               semantics=("arbitrary", "arbitrary", "arbitrary"), vmem=VMEM_BIG)(qa, ka, va, cr)


def _fox_bwd_g(qa, ka, va, doa, lse, cr, *, name, B, S, P, q_cb, k_cb, v_cb, do_cb, G=HEAD_GROUP):
    tq = tk = min(ATT_TILE, S)
    nq = S // tq
    NS = 2 * G

    def body(q_ref, k_ref, v_ref, do_ref, lse_ref, cr_ref, dq_ref, dk_ref, dv_ref, dcs_ref, dqa_ref, delta_ref, lse_s,
             p_buf, dp_buf):
        i = pl.program_id(2)

        @pl.when(i == 0)
        def _():
            dk_ref[...] = jnp.zeros_like(dk_ref)
            dv_ref[...] = jnp.zeros_like(dv_ref)
            dcs_ref[...] = jnp.zeros_like(dcs_ref)

        lane, qh = _streams(q_ref, G, SCALE)
        _, doh = _streams(do_ref, G)
        on_or_below = (lax.broadcasted_iota(jnp.int32, (tq, tk), 1) <= lax.broadcasted_iota(jnp.int32, (tq, tk), 0))
        delta_ref[...] = jnp.zeros(delta_ref.shape, F32)
        dqa_ref[...] = jnp.zeros(dqa_ref.shape, F32)
        for st in range(NS):
            lse_s[st] = jnp.broadcast_to(lse_ref[0, st], (tq, LANES))

        def probs(kb, diag):
            ks = pl.multiple_of(kb * tk, tk)
            kblk = _kv_blocks(k_ref, ks, tk, G)
            vblk = _kv_blocks(v_ref, ks, tk, G)
            ss = [lax.dot_general(qh[st], kblk[st // 2], NT, preferred_element_type=F32) for st in range(NS)]
            dps = [lax.dot_general(doh[st], vblk[st // 2], NT, preferred_element_type=F32) for st in range(NS)]
            ps = []
            for st in range(NS):
                s = ss[st] - cr_ref[0, st, :, pl.ds(ks, tk)]
                if diag:
                    s = jnp.where(on_or_below, s, NEG)
                ps.append(jnp.exp(s - _wide(lse_s[st], tk)))
            return ks, kblk, ps, dps

        def delta_block(kb, diag):
            _, _, ps, dps = probs(kb, diag)
            for st in range(NS):
                delta_ref[st] += _fold_lanes(ps[st] * dps[st])
                p_buf[st, kb] = ps[st]
                dp_buf[st, kb] = dps[st]

        _sweep(i, delta_block)
        for st in range(NS):
            delta_ref[st] = jnp.broadcast_to(jnp.sum(delta_ref[st], axis=-1, keepdims=True), (tq, LANES))

        def grad_block(kb, diag):
            ks = pl.multiple_of(kb * tk, tk)
            kblk = _kv_blocks(k_ref, ks, tk, G)
            rows = pl.ds(ks, tk)
            dsb, pb = [], []
            for st in range(NS):
                p = p_buf[st, kb]
                ds = p * (dp_buf[st, kb] - _wide(delta_ref[st], tk))
                dcs_ref[0, st, :, rows] -= jnp.sum(ds, axis=0, keepdims=True)
                dsb.append(ds.astype(BF16))
                pb.append(p.astype(BF16))
            dks = [lax.dot_general(dsb[st], qh[st], TN, preferred_element_type=F32) for st in range(NS)]
            dvs = [lax.dot_general(pb[st], doh[st], TN, preferred_element_type=F32) for st in range(NS)]
            dqs = [jnp.dot(dsb[st], kblk[st // 2], preferred_element_type=F32) for st in range(NS)]
            for g in range(G):
                dk_ref[rows, _lanes(g)] += dks[2 * g] + dks[2 * g + 1]
                dv_ref[rows, _lanes(g)] += dvs[2 * g] + dvs[2 * g + 1]
            for st in range(NS):
                dqa_ref[st] += dqs[st]

        _sweep(i, grad_block)
        for g in range(G):
            dq_ref[:, _lanes(g)] = (jnp.where(lane < HEAD_DIM, dqa_ref[2 * g], dqa_ref[2 * g + 1]) * SCALE).astype(BF16)

    return _pc(body, name=name,
               out_shape=[_sds((B * S, P * LANES), BF16), _sds((B * S, P * LANES), F32), _sds((B * S, P * LANES), F32),
                          _sds((B, 2 * P, 1, S), F32)],
               grid=(B, P // G, nq),
               in_specs=[_g_col_spec(tq, nq, q_cb, G), _g_kv_spec(S, k_cb, G), _g_kv_spec(S, v_cb, G),
                         _g_col_spec(tq, nq, do_cb, G), _g_stat_col_spec(tq, G), _g_stat_row_spec(S, G)],
               out_specs=[_g_col_spec(tq, nq, 0, G), _g_kv_spec(S, 0, G), _g_kv_spec(S, 0, G), _g_stat_row_spec(S, G)],
               scratch_shapes=[pltpu.VMEM((NS, tq, LANES), F32)] * 3 + [pltpu.VMEM((NS, nq, tq, tk), F32)] * 2,
               semantics=("arbitrary", "arbitrary", "arbitrary"), vmem=VMEM_BIG)(qa, ka, va, doa, lse, cr)


def _sb_logs_z(z):
    nz = -z
    lm = jnp.minimum(nz, 0.0) - jnp.log(1.0 + jnp.exp(jnp.minimum(z, nz)))
    return lm + z, lm


def _sb_fwd_g(qa, ka, va, *, name, B, S, P, q_cb, k_cb, v_cb, G=HEAD_GROUP):
    tq = tk = min(ATT_TILE, S)
    nq = S // tq
    NS = 2 * G

    def body(q_ref, k_ref, v_ref, o_ref, rt_ref, acc_ref, run_ref):
        i = pl.program_id(2)
        lane, qh = _streams(q_ref, G, SCALE)
        t_r = lax.broadcasted_iota(jnp.int32, (tk, tk), 0)
        t_c = lax.broadcasted_iota(jnp.int32, (tk, tk), 1)
        after = (t_r > t_c).astype(BF16)
        below = t_c < t_r
        acc_ref[...] = jnp.zeros(acc_ref.shape, F32)
        run_ref[...] = jnp.zeros(run_ref.shape, F32)

        def block(kb, diag):
            ks = pl.multiple_of(kb * tk, tk)
            kblk = _kv_blocks(k_ref, ks, tk, G)
            vblk = _kv_blocks(v_ref, ks, tk, G)
            zs = [lax.dot_general(qh[st], kblk[st // 2], NT, preferred_element_type=F32) for st in range(NS)]
            lss, parts = [], []
            for st in range(NS):
                ls, lm = _sb_logs_z(zs[st])
                if diag:
                    lm = jnp.where(below, lm, 0.0)
                lss.append(ls + _wide(run_ref[st], tk))
                run_ref[st] += jnp.sum(lm, axis=-1, keepdims=True)
                parts.append(_split2(lm))
            sufs = [jnp.dot(parts[st][0], after, preferred_element_type=F32)
                    + jnp.dot(parts[st][1], after, preferred_element_type=F32) for st in range(NS)]
            ab = []
            for st in range(NS):
                a = jnp.exp(lss[st] + sufs[st])
                if diag:
                    a = jnp.where(below, a, 0.0)
                ab.append(a.astype(BF16))
            pvs = [jnp.dot(ab[st], vblk[st // 2], preferred_element_type=F32) for st in range(NS)]
            for st in range(NS):
                acc_ref[st] += pvs[st]

        block(i, True)

        def step(jj, c):
            block(i - 1 - jj, False)
            return c

        lax.fori_loop(0, i, step, 0)
        for st in range(NS):
            rt_ref[0, st] = jnp.max(run_ref[st], axis=-1, keepdims=True)
        for g in range(G):
            o_ref[:, _lanes(g)] = jnp.where(lane < HEAD_DIM, acc_ref[2 * g], acc_ref[2 * g + 1]).astype(BF16)

    return _pc(body, name=name, out_shape=[_sds((B * S, P * LANES), BF16), _sds((B, 2 * P, S, 1), F32)],
               grid=(B, P // G, nq),
               in_specs=[_g_col_spec(tq, nq, q_cb, G), _g_kv_spec(S, k_cb, G), _g_kv_spec(S, v_cb, G)],
               out_specs=[_g_col_spec(tq, nq, 0, G), _g_stat_col_spec(tq, G)],
               scratch_shapes=[pltpu.VMEM((NS, tq, LANES), F32)] * 2,
               semantics=("arbitrary", "arbitrary", "arbitrary"), vmem=VMEM_BIG)(qa, ka, va)


def _sb_bwd_g(qa, ka, va, doa, rt, *, name, B, S, P, q_cb, k_cb, v_cb, do_cb, G=HEAD_GROUP):
    tq = tk = min(ATT_TILE, S)
    nq = S // tq
    NS = 2 * G

    def body(q_ref, k_ref, v_ref, do_ref, rt_ref, dq_ref, dk_ref, dv_ref, dqa_ref, pl_ref, pg_ref):
        i = pl.program_id(2)

        @pl.when(i == 0)
        def _():
            dk_ref[...] = jnp.zeros_like(dk_ref)
            dv_ref[...] = jnp.zeros_like(dv_ref)

        lane, qh = _streams(q_ref, G, SCALE)
        _, doh = _streams(do_ref, G)
        t_r = lax.broadcasted_iota(jnp.int32, (tk, tk), 0)
        t_c = lax.broadcasted_iota(jnp.int32, (tk, tk), 1)
        upto = (t_r <= t_c).astype(BF16)
        before = (t_r < t_c).astype(BF16)
        below = t_c < t_r
        dqa_ref[...] = jnp.zeros(dqa_ref.shape, F32)
        pg_ref[...] = jnp.zeros(pg_ref.shape, F32)
        for st in range(NS):
            pl_ref[st] = jnp.broadcast_to(rt_ref[0, st], (tq, LANES))

        def block(kb, diag):
            ks = pl.multiple_of(kb * tk, tk)
            rows = pl.ds(ks, tk)
            kblk = _kv_blocks(k_ref, ks, tk, G)
            vblk = _kv_blocks(v_ref, ks, tk, G)
            zs = [lax.dot_general(qh[st], kblk[st // 2], NT, preferred_element_type=F32) for st in range(NS)]
            das = [lax.dot_general(doh[st], vblk[st // 2], NT, preferred_element_type=F32) for st in range(NS)]
            lss, parts = [], []
            for st in range(NS):
                ls, lm = _sb_logs_z(zs[st])
                if diag:
                    lm = jnp.where(below, lm, 0.0)
                lss.append((ls, ls + _wide(pl_ref[st], tk)))
                pl_ref[st] -= jnp.sum(lm, axis=-1, keepdims=True)
                parts.append(_split2(lm))
            pins = [jnp.dot(parts[st][0], upto, preferred_element_type=F32)
                    + jnp.dot(parts[st][1], upto, preferred_element_type=F32) for st in range(NS)]
            gms, ab, gparts = [], [], []
            for st in range(NS):
                a = jnp.exp(lss[st][1] - pins[st])
                if diag:
                    a = jnp.where(below, a, 0.0)
                gm = a * das[st]
                gms.append(gm)
                ab.append(a.astype(BF16))
                gparts.append(gm.astype(BF16))
            pgs = [jnp.dot(gparts[st], before, preferred_element_type=F32) for st in range(NS)]
            dzb = []
            for st in range(NS):
                gm = gms[st]
                dz = gm - jnp.exp(lss[st][0]) * (gm + (pgs[st] + _wide(pg_ref[st], tk)))
                if diag:
                    dz = jnp.where(below, dz, 0.0)
                pg_ref[st] += jnp.sum(gm, axis=-1, keepdims=True)
                dzb.append(dz.astype(BF16))
            dks = [lax.dot_general(dzb[st], qh[st], TN, preferred_element_type=F32) for st in range(NS)]
            dvs = [lax.dot_general(ab[st], doh[st], TN, preferred_element_type=F32) for st in range(NS)]
            dqs = [jnp.dot(dzb[st], kblk[st // 2], preferred_element_type=F32) for st in range(NS)]
            for g in range(G):
                dk_ref[rows, _lanes(g)] += dks[2 * g] + dks[2 * g + 1]
                dv_ref[rows, _lanes(g)] += dvs[2 * g] + dvs[2 * g + 1]
            for st in range(NS):
                dqa_ref[st] += dqs[st]

        _sweep(i, block)
        for g in range(G):
            dq_ref[:, _lanes(g)] = (jnp.where(lane < HEAD_DIM, dqa_ref[2 * g], dqa_ref[2 * g + 1]) * SCALE).astype(BF16)

    return _pc(body, name=name,
               out_shape=[_sds((B * S, P * LANES), BF16), _sds((B * S, P * LANES), F32), _sds((B * S, P * LANES), F32)],
               grid=(B, P // G, nq),
               in_specs=[_g_col_spec(tq, nq, q_cb, G), _g_kv_spec(S, k_cb, G), _g_kv_spec(S, v_cb, G),
                         _g_col_spec(tq, nq, do_cb, G), _g_stat_col_spec(tq, G)],
               out_specs=[_g_col_spec(tq, nq, 0, G), _g_kv_spec(S, 0, G), _g_kv_spec(S, 0, G)],
               scratch_shapes=[pltpu.VMEM((NS, tq, LANES), F32)] * 3,
               semantics=("arbitrary", "arbitrary", "arbitrary"), vmem=VMEM_BIG)(qa, ka, va, doa, rt)


def _shift_rows(cur, halo_ref, first, rows_idx, k):
    out = pltpu.roll(cur, k, 0)
    top = out[0:8, :]
    for r in range(k):
        hr = halo_ref.shape[0] - k + r
        edge = jnp.where(first, 0.0, halo_ref[hr:hr + 1, :])
        top = jnp.where(rows_idx[0:8, :] == r, edge, top)
    return jnp.concatenate([top, out[8:, :]], axis=0)


def _shift_rows_up(cur, halo_ref, last, rows_idx, k, ts):
    out = pltpu.roll(cur, ts - k, 0)
    bottom = out[ts - 8:, :]
    for r in range(k):
        edge = jnp.where(last, 0.0, halo_ref[r:r + 1, :])
        bottom = jnp.where(rows_idx[0:8, :] == 8 - k + r, edge, bottom)
    return jnp.concatenate([out[:ts - 8, :], bottom], axis=0)


def _conv_taps(main_ref, halo_ref, w_ref, b_ref, first, rows_idx):
    cur = main_ref[...]
    m1 = _shift_rows(cur, halo_ref, first, rows_idx, 1)
    m2 = _shift_rows(cur, halo_ref, first, rows_idx, 2)
    uc = b_ref[...] + w_ref[0:1, :] * m2 + w_ref[1:2, :] * m1 + w_ref[2:3, :] * cur
    return uc, cur, m1, m2


def _conv_specs(ts, tf, ns, nf, S, order):
    def wrap(fn):
        return lambda *g: fn(*order(*g))
    specs = []
    for half in (0, 1):
        specs.append(pl.BlockSpec((None, ts, tf), wrap(lambda b, i, j, half=half: (half, b * ns + i, j))))
        specs.append(pl.BlockSpec((None, 8, tf), wrap(
            lambda b, i, j, half=half: (half, jnp.maximum((b * S + i * ts) // 8 - 1, 0), j))))
    for off in (0, nf):
        specs.append(pl.BlockSpec((3, tf), wrap(lambda b, i, j, off=off: (0, j + off))))
    for off in (0, nf):
        specs.append(pl.BlockSpec((1, tf), wrap(lambda b, i, j, off=off: (0, j + off))))
    return specs


def _ffn_up_gate(x, g, w, cw, cb, *, name, S):
    T, D = x.shape
    F = w.shape[1] // 2
    tm = min(1024, S)
    tn = 256
    nj = F // tn
    tiles_per_seq = S // tm
    halo = 16

    def body(x_ref, xh_ref, g_ref, wg_ref, wv_ref, cwg_ref, cwv_ref, cbg_ref, cbv_ref,
             u_ref, a_ref, hout_ref, h_ref, hh_ref, eg_ref, ev_ref):
        first = lax.rem(pl.program_id(0), tiles_per_seq) == 0

        @pl.when(pl.program_id(1) == 0)
        def _():
            def norm(v):
                r = lax.rsqrt(jnp.mean(v * v, axis=-1, keepdims=True) + EPS)
                return ((v * r) * g_ref[...]).astype(BF16)
            h = norm(x_ref[...])
            h_ref[...] = h
            hout_ref[...] = h
            hh_ref[...] = norm(xh_ref[...])

        h = h_ref[...]
        rows_idx = lax.broadcasted_iota(jnp.int32, (tm, tn), 0)
        uc = []
        for half, (w_ref, cw_ref, cb_ref, e_ref) in enumerate(((wg_ref, cwg_ref, cbg_ref, eg_ref),
                                                               (wv_ref, cwv_ref, cbv_ref, ev_ref))):
            acc = jnp.dot(h, w_ref[...], preferred_element_type=F32)
            e_ref[...] = jnp.dot(hh_ref[...], w_ref[...], preferred_element_type=F32)
            u_ref[half] = acc
            m1 = _shift_rows(acc, e_ref, first, rows_idx, 1)
            m2 = _shift_rows(acc, e_ref, first, rows_idx, 2)
            uc.append(cb_ref[...] + cw_ref[0:1, :] * m2 + cw_ref[1:2, :] * m1 + cw_ref[2:3, :] * acc)
        a_ref[...] = (uc[0] * (1.0 / (1.0 + jnp.exp(-uc[0]))) * uc[1]).astype(BF16)

    in_specs = [pl.BlockSpec((tm, D), lambda i, j: (i, 0)),
                pl.BlockSpec((halo, D), lambda i, j: (jnp.maximum(i * (tm // halo) - 1, 0), 0)),
                pl.BlockSpec((1, D), lambda i, j: (0, 0)),
                pl.BlockSpec((D, tn), lambda i, j: (0, j)), pl.BlockSpec((D, tn), lambda i, j: (0, j + nj)),
                pl.BlockSpec((3, tn), lambda i, j: (0, j)), pl.BlockSpec((3, tn), lambda i, j: (0, j + nj)),
                pl.BlockSpec((1, tn), lambda i, j: (0, j)), pl.BlockSpec((1, tn), lambda i, j: (0, j + nj))]
    return _pc(body, name=name,
               out_shape=[_sds((2, T, F), F32), _sds((T, F), BF16), _sds((T, D), BF16)],
               grid=(T // tm, nj), in_specs=in_specs,
               out_specs=[pl.BlockSpec((2, tm, tn), lambda i, j: (0, i, j)), pl.BlockSpec((tm, tn), lambda i, j: (i, j)),
                          pl.BlockSpec((tm, D), lambda i, j: (i, 0))],
               scratch_shapes=[pltpu.VMEM((tm, D), BF16), pltpu.VMEM((halo, D), BF16),
                               pltpu.VMEM((halo, tn), F32), pltpu.VMEM((halo, tn), F32)],
               semantics=("arbitrary", "arbitrary"), vmem=VMEM_BIG)(x, x, g.reshape(1, D), w, w, cw, cw, cb, cb)


def _conv_gate_bwd(da, u, cw, cb, *, name, B, S):
    F = u.shape[2]
    tf = F // 2
    ts = min(256, S)
    ns, nf = S // ts, F // tf

    def body(da_ref, ug_ref, ugh_ref, uv_ref, uvh_ref, wg_ref, wv_ref, bg_ref, bv_ref,
             dug_ref, duv_ref, pg_ref, pv_ref, nxt_g, nxt_v):
        step = pl.program_id(2)
        first = step == ns - 1
        last = step == 0

        @pl.when(jnp.logical_and(pl.program_id(1) == 0, last))
        def _():
            pg_ref[...] = jnp.zeros_like(pg_ref)
            pv_ref[...] = jnp.zeros_like(pv_ref)

        rows_idx = lax.broadcasted_iota(jnp.int32, (ts, tf), 0)
        ucg, g0, g1, g2 = _conv_taps(ug_ref, ugh_ref, wg_ref, bg_ref, first, rows_idx)
        ucv, v0, v1, v2 = _conv_taps(uv_ref, uvh_ref, wv_ref, bv_ref, first, rows_idx)
        sg = 1.0 / (1.0 + jnp.exp(-ucg))
        dav = da_ref[...]
        d_v = dav * (ucg * sg)
        d_g = dav * ucv * (sg * (1.0 + ucg * (1.0 - sg)))
        for p_ref, d, taps in ((pg_ref, d_g, (g2, g1, g0)), (pv_ref, d_v, (v2, v1, v0))):
            for k in range(3):
                p_ref[k:k + 1, :] += jnp.sum(d * taps[k], axis=0, keepdims=True)
            p_ref[3:4, :] += jnp.sum(d, axis=0, keepdims=True)
        for o_ref, d, w_ref, nxt in ((dug_ref, d_g, wg_ref, nxt_g), (duv_ref, d_v, wv_ref, nxt_v)):
            p1 = _shift_rows_up(d, nxt, last, rows_idx, 1, ts)
            p2 = _shift_rows_up(d, nxt, last, rows_idx, 2, ts)
            o_ref[...] = (w_ref[2:3, :] * d + w_ref[1:2, :] * p1 + w_ref[0:1, :] * p2).astype(BF16)
            nxt[...] = d[0:8, :]

    def order(j, b, r):
        return b, ns - 1 - r, j

    row = pl.BlockSpec((ts, tf), lambda j, b, r: (b * ns + ns - 1 - r, j))
    specs = [row] + _conv_specs(ts, tf, ns, nf, S, order)
    par = pl.BlockSpec((8, tf), lambda j, b, r: (0, j))
    return _pc(body, name=name,
               out_shape=[_sds((B * S, F), BF16), _sds((B * S, F), BF16), _sds((8, F), F32), _sds((8, F), F32)],
               grid=(nf, B, ns), in_specs=specs, out_specs=[row, row, par, par],
               scratch_shapes=[pltpu.VMEM((8, tf), F32), pltpu.VMEM((8, tf), F32)],
               semantics=("arbitrary", "arbitrary", "arbitrary"), vmem=VMEM_BIG)(da, u, u, u, u, cw, cw, cb, cb)


def _conv_transpose(d, cw, *, name, B, S, col_off):
    F = d.shape[1]
    tf = F // 2
    ts = min(256, S)
    ns, nf = S // ts, F // tf
    nblk8 = B * S // 8

    def body(d_ref, dh_ref, w_ref, o_ref):
        last = pl.program_id(1) == ns - 1
        rows_idx = lax.broadcasted_iota(jnp.int32, (ts, tf), 0)
        cur = d_ref[...]
        p1 = _shift_rows_up(cur, dh_ref, last, rows_idx, 1, ts)
        p2 = _shift_rows_up(cur, dh_ref, last, rows_idx, 2, ts)
        o_ref[...] = (w_ref[2:3, :] * cur + w_ref[1:2, :] * p1 + w_ref[0:1, :] * p2).astype(BF16)

    return _pc(body, name=name, out_shape=_sds((B * S, F), BF16), grid=(B, ns, nf),
               in_specs=[pl.BlockSpec((ts, tf), lambda b, i, j: (b * ns + i, j)),
                         pl.BlockSpec((8, tf), lambda b, i, j: (jnp.minimum((b * S + (i + 1) * ts) // 8, nblk8 - 1), j)),
                         pl.BlockSpec((3, tf), lambda b, i, j: (0, j + col_off * nf))],
               out_specs=pl.BlockSpec((ts, tf), lambda b, i, j: (b * ns + i, j)),
               semantics=("arbitrary", "arbitrary", "arbitrary"))(d, d, cw)


def _adamw(w, g, m, v, *, name):
    rows, cols = w.shape
    tr = rows
    while tr * cols * 4 > 2 ** 20 and tr % 16 == 0:
        tr //= 2

    def body(w_ref, g_ref, m_ref, v_ref, d_ref, nm_ref, nv_ref):
        gv = g_ref[...]
        m_new = ADAM_B1 * m_ref[...] + (1.0 - ADAM_B1) * gv
        v_new = ADAM_B2 * v_ref[...] + (1.0 - ADAM_B2) * (gv * gv)
        m_hat = m_new / (1.0 - ADAM_B1 ** ADAM_STEP)
        v_hat = v_new / (1.0 - ADAM_B2 ** ADAM_STEP)
        d_ref[...] = -ADAM_LR * (m_hat / (jnp.sqrt(v_hat) + ADAM_EPS) + ADAM_WD * w_ref[...])
        nm_ref[...] = m_new
        nv_ref[...] = v_new

    blk = pl.BlockSpec((tr, cols), lambda i: (i, 0))
    return _pc(body, name=name, out_shape=[_sds((rows, cols), F32)] * 3, grid=(rows // tr,),
               in_specs=[blk] * 4, out_specs=[blk] * 3, semantics=("arbitrary",))(w, g, m, v)


def _my_pos():
    return lax.axis_index("x"), lax.axis_index("y"), lax.axis_index("c")


_HBM = pl.BlockSpec(memory_space=pltpu.HBM)
_SEM = pl.BlockSpec(memory_space=pltpu.SEMAPHORE)
_EFFECT = pltpu.SideEffectType.DATAFLOW_SIDE_EFFECTING


def _peers():
    x, y, c = _my_pos()
    out = []
    for k in range(1, N_DEV):
        px, py, pc = x ^ ((k >> 2) & 1), y ^ ((k >> 1) & 1), c ^ (k & 1)
        out.append(((px, py, pc), 4 * px + 2 * py + pc))
    return out


def _scatter_start(srcs, slot_of, *, name, order_after=None):
    n = len(srcs)
    lands = [lax.empty((N_DEV,) + slot_of(s, 0, shape_only=True), s.dtype) for s in srcs]
    extra = [] if order_after is None else [order_after]

    def body(*refs):
        src_refs, land_refs = refs[:n], refs[n:2 * n]
        send_sems, recv_sems = refs[2 * n + len(extra)], refs[2 * n + len(extra) + 1]
        token = refs[-1]
        x, y, c = _my_pos()
        me = 4 * x + 2 * y + c
        for a in range(n):
            for k, (peer, peer_idx) in enumerate(_peers()):
                pltpu.make_async_remote_copy(
                    src_ref=slot_of(src_refs[a], peer_idx), dst_ref=land_refs[a].at[me],
                    send_sem=send_sems.at[a * 7 + k], recv_sem=recv_sems.at[a * 7 + k],
                    device_id=peer, device_id_type=MESH).start()
        token[...] = jnp.zeros_like(token)

    hbm = lambda a: pltpu.HBM(a.shape, a.dtype)
    args = [pltpu.with_memory_space_constraint(a, pltpu.HBM) for a in list(srcs) + lands] + extra
    outs = pl.pallas_call(
        body, name=name,
        out_shape=(pltpu.SemaphoreType.DMA((7 * n,)), pltpu.SemaphoreType.DMA((7 * n,)),
                   *[hbm(a) for a in srcs], *[hbm(a) for a in lands], _sds((8, LANES), F32)),
        in_specs=[_HBM] * (2 * n) + [pl.BlockSpec(memory_space=pl.ANY)] * len(extra),
        out_specs=(_SEM, _SEM, *([_HBM] * (2 * n)), pl.BlockSpec(memory_space=pltpu.VMEM)),
        input_output_aliases={a: 2 + a for a in range(2 * n)},
        compiler_params=pltpu.CompilerParams(has_side_effects=_EFFECT))(*args)
    return outs[0], outs[1], list(outs[2:2 + n]), list(outs[2 + n:2 + 2 * n]), outs[-1]


def _scatter_wait(send_sems, recv_sems, srcs, lands, slot_of, after, *, name):
    n = len(srcs)

    def body(*refs):
        src_refs, land_refs = refs[:n], refs[n:2 * n]
        ssem, rsem = refs[2 * n], refs[2 * n + 1]
        x, y, c = _my_pos()
        me = 4 * x + 2 * y + c
        for a in range(n):
            for k, (peer, peer_idx) in enumerate(_peers()):
                cp = pltpu.make_async_remote_copy(
                    src_ref=slot_of(src_refs[a], peer_idx), dst_ref=land_refs[a].at[me],
                    send_sem=ssem.at[a * 7 + k], recv_sem=rsem.at[a * 7 + k],
                    device_id=peer, device_id_type=MESH)
                cp.wait_send()
                cp.wait_recv()

    hbm = lambda a: pltpu.HBM(a.shape, a.dtype)
    outs = pl.pallas_call(
        body, name=name, out_shape=tuple(hbm(a) for a in list(srcs) + list(lands)),
        in_specs=[_HBM] * (2 * n) + [_SEM, _SEM, pl.BlockSpec(memory_space=pl.ANY)],
        out_specs=tuple([_HBM] * (2 * n)), input_output_aliases={a: a for a in range(2 * n)},
        compiler_params=pltpu.CompilerParams(has_side_effects=_EFFECT))(*srcs, *lands, send_sems, recv_sems, after)
    return list(outs[:n]), list(outs[n:])


def _whole(a, peer_idx, shape_only=False):
    return a.shape if shape_only else a


def _slot(a, peer_idx, shape_only=False):
    return a.shape[1:] if shape_only else a.at[peer_idx]


def _all_gather(shard, *, name):
    rows = shard.shape[0]

    def body(x_ref, out_ref, send_sems, recv_sems, local_sem):
        x, y, c = _my_pos()
        me, sibling = (x, y, c), (x, y, 1 - c)
        chips = [(1 - x, y), (x, 1 - y), (1 - x, 1 - y)]

        def slot(px, py, pc):
            return out_ref.at[4 * px + 2 * py + pc]

        def copy(k, block, to, src=None):
            return pltpu.make_async_remote_copy(
                src_ref=slot(*block) if src is None else src, dst_ref=slot(*block),
                send_sem=send_sems.at[k], recv_sem=recv_sems.at[k], device_id=to, device_id_type=MESH)

        mine = pltpu.make_async_copy(x_ref, slot(*me), local_sem)
        mine.start()
        first = [copy(0, me, sibling, src=x_ref)]
        first += [copy(1 + j, me, (*chip, c), src=x_ref) for j, chip in enumerate(chips)]
        for cp in first:
            cp.start()
        passed = [copy(4 + j, (*chip, c), sibling) for j, chip in enumerate(chips)]
        for j, chip in enumerate(chips):
            copy(1 + j, (*chip, c), me).wait_recv()
            passed[j].start()
        copy(0, sibling, me).wait_recv()
        for j, chip in enumerate(chips):
            copy(4 + j, (*chip, 1 - c), me).wait_recv()
        for cp in first + passed:
            cp.wait_send()
        mine.wait()

    return _pc(body, name=name, out_shape=_sds((N_DEV, rows, LANES), shard.dtype),
               in_specs=[pl.BlockSpec(memory_space=pl.ANY)], out_specs=pl.BlockSpec(memory_space=pl.ANY),
               scratch_shapes=[pltpu.SemaphoreType.DMA((7,)), pltpu.SemaphoreType.DMA((7,)),
                               pltpu.SemaphoreType.DMA])(shard)


def _exchange(big, small, *, name):
    rs = small.shape[0]

    def body(big_ref, small_ref, bout_ref, sout_ref, send_sems, recv_sems, local_sems):
        x, y, c = _my_pos()
        me = 4 * x + 2 * y + c
        lb = pltpu.make_async_copy(big_ref.at[me], bout_ref.at[me], local_sems.at[0])
        ls = pltpu.make_async_copy(small_ref, sout_ref.at[me], local_sems.at[1])
        lb.start()
        ls.start()
        copies = []
        for k in range(1, N_DEV):
            px = x ^ ((k >> 2) & 1)
            py = y ^ ((k >> 1) & 1)
            pc = c ^ (k & 1)
            peer = 4 * px + 2 * py + pc
            copies.append(pltpu.make_async_remote_copy(
                src_ref=big_ref.at[peer], dst_ref=bout_ref.at[me], send_sem=send_sems.at[k - 1],
                recv_sem=recv_sems.at[k - 1], device_id=(px, py, pc), device_id_type=MESH))
            copies.append(pltpu.make_async_remote_copy(
                src_ref=small_ref, dst_ref=sout_ref.at[me], send_sem=send_sems.at[7 + k - 1],
                recv_sem=recv_sems.at[7 + k - 1], device_id=(px, py, pc), device_id_type=MESH))
        for cp in copies:
            cp.start()
        for cp in copies:
            cp.wait()
        lb.wait()
        ls.wait()

    return _pc(body, name=name,
               out_shape=[_sds(big.shape, big.dtype), _sds((N_DEV, rs, LANES), F32)],
               in_specs=[pl.BlockSpec(memory_space=pl.ANY), pl.BlockSpec(memory_space=pl.ANY)],
               out_specs=[pl.BlockSpec(memory_space=pl.ANY), pl.BlockSpec(memory_space=pl.ANY)],
               scratch_shapes=[pltpu.SemaphoreType.DMA((14,)), pltpu.SemaphoreType.DMA((14,)),
                               pltpu.SemaphoreType.DMA((2,))])(big, small)


def _all_gather_small(small, *, name):
    rs = small.shape[0]

    def body(small_ref, out_ref, send_sems, recv_sems, local_sem):
        x, y, c = _my_pos()
        me = 4 * x + 2 * y + c
        mine = pltpu.make_async_copy(small_ref, out_ref.at[me], local_sem)
        mine.start()
        copies = [pltpu.make_async_remote_copy(
            src_ref=small_ref, dst_ref=out_ref.at[me], send_sem=send_sems.at[k], recv_sem=recv_sems.at[k],
            device_id=peer, device_id_type=MESH) for k, (peer, _) in enumerate(_peers())]
        for cp in copies:
            cp.start()
        for cp in copies:
            cp.wait()
        mine.wait()

    return _pc(body, name=name, out_shape=_sds((N_DEV, rs, LANES), F32),
               in_specs=[pl.BlockSpec(memory_space=pl.ANY)], out_specs=pl.BlockSpec(memory_space=pl.ANY),
               scratch_shapes=[pltpu.SemaphoreType.DMA((7,)), pltpu.SemaphoreType.DMA((7,)),
                               pltpu.SemaphoreType.DMA])(small)


def _sum_slots(a, *, name, tr=None):
    rows, cols = a.shape[1], a.shape[2]
    if tr is None:
        tr = rows
        while N_DEV * tr * cols * a.dtype.itemsize > 3 * 2 ** 20 and tr % 32 == 0:
            tr //= 2

    def body(a_ref, o_ref):
        acc = a_ref[0].astype(F32)
        for j in range(1, N_DEV):
            acc = acc + a_ref[j].astype(F32)
        o_ref[...] = acc

    return _pc(body, name=name, out_shape=_sds((rows, cols), F32), grid=(rows // tr,),
               in_specs=[pl.BlockSpec((N_DEV, tr, cols), lambda i: (0, i, 0))],
               out_specs=pl.BlockSpec((tr, cols), lambda i: (i, 0)), semantics=("arbitrary",), vmem=VMEM_BIG)(a)


PACK_ROWS = 25600
SUM_TILE = 512


def _rows128(a):
    return a.reshape(-1, LANES)


def _to_slots(full, kind):
    if kind == "rows2":
        r, c = full.shape
        return full.reshape(N_DEV, r // N_DEV, c)
    if kind == "cols2":
        r, c = full.shape
        return full.reshape(r, N_DEV, c // N_DEV).transpose(1, 0, 2)
    if kind == "rows3":
        l, r, c = full.shape
        return full.reshape(l, N_DEV, r // N_DEV, c).transpose(1, 0, 2, 3)
    if kind == "cols3":
        l, r, c = full.shape
        return full.reshape(l, r, N_DEV, c // N_DEV).transpose(2, 0, 1, 3)
    raise ValueError(kind)


def _from_slots(slots, kind):
    if kind == "rows2":
        _, r, c = slots.shape
        return slots.reshape(N_DEV * r, c)
    if kind == "cols2":
        _, r, c = slots.shape
        return slots.transpose(1, 0, 2).reshape(r, N_DEV * c)
    if kind == "rows3":
        _, l, r, c = slots.shape
        return slots.transpose(1, 0, 2, 3).reshape(l, N_DEV * r, c)
    if kind == "cols3":
        _, l, r, c = slots.shape
        return slots.transpose(1, 2, 0, 3).reshape(l, r, N_DEV * c)
    raise ValueError(kind)


BIG = (("w_in_a", "rows2"), ("w_in_b", "rows2"), ("w_kv", "cols2"), ("w_memkv", "rows3"),
       ("w_out", "rows3"), ("w_up", "cols3"), ("w_down", "rows3"))


def _round_up(n, m):
    return -(-n // m) * m


def _pad_rows(a, rows, axis):
    pad = [(0, 0)] * a.ndim
    pad[axis] = (0, rows - a.shape[axis])
    return jnp.pad(a, pad)


def kernel(x, mem, ln_mix_g, w_in_a, b_f_a, w_in_b, ln_kv_g, w_kv, ln_mem_g, w_memkv, w_out, ln_ffn_g, w_up, conv_w, conv_b, w_down, final_g, loss_target, m_ln_mix_g, m_w_in_a, m_b_f_a, m_w_in_b, m_ln_kv_g, m_w_kv, m_ln_mem_g, m_w_memkv, m_w_out, m_ln_ffn_g, m_w_up, m_conv_w, m_conv_b, m_w_down, m_final_g, v_ln_mix_g, v_w_in_a, v_b_f_a, v_w_in_b, v_ln_kv_g, v_w_kv, v_ln_mem_g, v_w_memkv, v_w_out, v_ln_ffn_g, v_w_up, v_conv_w, v_conv_b, v_w_down, v_final_g):
    B, S, D = x.shape
    NM = mem.shape[1]
    T = B * S
    F = w_down.shape[1] * N_DEV
    my_idx = 4 * lax.axis_index("x") + 2 * lax.axis_index("y") + lax.axis_index("c")

    shards = {"w_in_a": w_in_a[0], "w_in_b": w_in_b[0], "w_kv": w_kv, "w_memkv": w_memkv, "w_out": w_out,
              "w_up": w_up, "w_down": w_down}
    moms = {"w_in_a": (m_w_in_a[0], v_w_in_a[0]), "w_in_b": (m_w_in_b[0], v_w_in_b[0]), "w_kv": (m_w_kv, v_w_kv),
            "w_memkv": (m_w_memkv, v_w_memkv), "w_out": (m_w_out, v_w_out), "w_up": (m_w_up, v_w_up),
            "w_down": (m_w_down, v_w_down)}

    groups = [("a1", [("w_in_a", None)]),
              ("a2", [("w_in_b", None), ("w_kv", None), ("w_memkv", None), ("w_out", None), ("conv_w", None)]),
              ("b0", [("w_up", 0), ("w_down", 0)]), ("b1", [("w_up", 1), ("w_down", 1)])]
    sources = dict(shards, conv_w=conv_w)
    started, token = {}, None
    for gname, members in groups:
        srcs = []
        for n, layer in members:
            a = sources[n] if layer is None else sources[n][layer]
            srcs.append(a if n == "conv_w" else a.astype(BF16))
        ssem, rsem, thru, lands, token = _scatter_start(srcs, _whole, name=f"gather_start_{gname}", order_after=token)
        started[gname] = (ssem, rsem, thru, lands)

    def gathered(gname, after):
        ssem, rsem, thru, lands = started[gname]
        thru, lands = _scatter_wait(ssem, rsem, thru, lands, _whole, after, name=f"gather_wait_{gname}")
        return [lax.dynamic_update_index_in_dim(land, s, my_idx, 0) for land, s in zip(lands, thru)]

    full = {}
    (g_wa,) = gathered("a1", token)
    full["w_in_a"] = _from_slots(g_wa, "rows2")

    wa = full["w_in_a"]
    n_qkv = 3 * MAIN_W
    wa = jnp.concatenate([wa[:, :n_qkv], wa[:, n_qkv + N_MAIN_HEADS:], wa[:, n_qkv:n_qkv + N_MAIN_HEADS],
                          jnp.zeros((D, LANES - N_MAIN_HEADS), BF16)], axis=1)
    n_main = n_qkv + MEM_W
    full["w_up"], full["w_down"] = {}, {}
    b_f =_pad_rows(b_f_a.reshape(1, N_MAIN_HEADS), LANES, 1)

    x2d = x.reshape(T, D)
    mem2d = mem.reshape(B * NM, D)
    tgt2d = loss_target.reshape(T, D)
    PM, PX = N_MAIN_HEADS // 2, N_MEM_HEADS // 2

    def stats_to_heads(c2d):
        c = c2d.reshape(B, S, LANES)[:, :, :N_MAIN_HEADS].transpose(0, 2, 1)
        return c[:, :, None, :]

    def mem_kv(layer):
        return _mm_fwd(mem2d, full["w_memkv"][layer], name=f"memkv{layer}", tm=B * NM, tn=2 * MEM_W,
                       out_dtype=BF16, g=ln_mem_g[layer], save_h=True)

    def conv_ffn_fwd(xin, layer):
        u, a, h = _ffn_up_gate(xin, ln_ffn_g[layer], full["w_up"][layer], conv_w_full[layer],
                               conv_b[layer].reshape(1, 2 * F), name=f"ffn_up{layer}", S=S)
        xo = _mm_fwd(a, full["w_down"][layer], name=f"ffn_down{layer}", tm=min(512, T), tn=512, out_dtype=F32, res=xin)
        return xo, (u, h, a)

    proj_a, h_mix0 = _mm_fwd(x2d, wa, name="in_proj_a", tm=min(1024, T), tn=512, out_dtype=BF16, g=ln_mix_g[0],
                             ncols=n_main, save_h=True)
    f_logit = _mm_fwd(x2d, wa, name="in_proj_f", tm=min(1024, T), tn=LANES, out_dtype=F32, g=ln_mix_g[0],
                      col0=n_main // LANES, ncols=LANES)
    c2d = _forget_cumsum(f_logit, b_f, B=B, S=S, name="forget_cumsum")
    cr = stats_to_heads(c2d)
    o_main0, lse0 = _fox_fwd_g(proj_a, proj_a, proj_a, cr, name="fox_fwd", B=B, S=S, P=PM, q_cb=0, k_cb=PM, v_cb=2 * PM)
    g_wb, g_wkv, g_wmem, g_wout, g_cw = gathered("a2", lse0)
    wb = _from_slots(g_wb, "rows2")
    wkv = _from_slots(g_wkv, "cols2")
    full["w_memkv"] = _from_slots(g_wmem, "rows3")
    full["w_out"] = _from_slots(g_wout, "rows3")
    conv_w_full = _from_slots(g_cw, "cols3")
    memkv0, h_mem0 = mem_kv(0)
    o_mem0, lse_m0 = _softmax_fwd(proj_a, memkv0, memkv0, name="mem_fwd0", B=B, S=S, Sk=NM, P=PX, q_cb=3 * PM,
                                  k_cb=0, v_cb=PX, causal=False)
    o_cat0 = jnp.concatenate([o_main0, o_mem0], axis=1)
    x1 = _mm_fwd(o_cat0, full["w_out"][0], name="out_proj0", tm=min(512, T), tn=512, out_dtype=F32, res=x2d)
    g_up, g_dn = gathered("b0", x1)
    full["w_up"][0], full["w_down"][0] = _from_slots(g_up, "cols2"), _from_slots(g_dn, "rows2")
    x2, (u0, h_ffn0, a0) = conv_ffn_fwd(x1, 0)
    kv, h_kv = _mm_fwd(x2, wkv, name="kv_proj", tm=min(1024, T), tn=512, out_dtype=BF16, g=ln_kv_g, save_h=True)
    proj_b, h_mix1 = _mm_fwd(x2, wb, name="in_proj_b", tm=min(1024, T), tn=512, out_dtype=BF16, g=ln_mix_g[1],
                             save_h=True)
    o_main1, rt1 = _sb_fwd_g(proj_b, kv, kv, name="sb_fwd", B=B, S=S, P=PM, q_cb=0, k_cb=0, v_cb=PM)
    memkv1, h_mem1 = mem_kv(1)
    o_mem1, lse_m1 = _softmax_fwd(proj_b, memkv1, memkv1, name="mem_fwd1", B=B, S=S, Sk=NM, P=PX, q_cb=PM,
                                  k_cb=0, v_cb=PX, causal=False)
    o_cat1 = jnp.concatenate([o_main1, o_mem1], axis=1)
    x3 = _mm_fwd(o_cat1, full["w_out"][1], name="out_proj1", tm=min(512, T), tn=512, out_dtype=F32, res=x2)
    g_up, g_dn = gathered("b1", x3)
    full["w_up"][1], full["w_down"][1] = _from_slots(g_up, "cols2"), _from_slots(g_dn, "rows2")
    x4, (u1, h_ffn1, a1) = conv_ffn_fwd(x3, 1)
    dx4, dg_final, loss_part = _loss_head(x4, final_g, tgt2d, name="loss_head")

    grads = {}
    small = {}
    reduce_groups = []

    def start_reduce(gname, keys, kinds):
        slots = [_to_slots(grads[k], kind) for k, kind in zip(keys, kinds)]
        ssem, rsem, thru, lands, tok = _scatter_start(slots, _slot, name=f"reduce_start_{gname}")
        reduce_groups.append((gname, keys, ssem, rsem, thru, lands))
        return tok[0, 0]

    def conv_ffn_bwd(dxo, xin, u, h, a, layer):
        w_dn = full["w_down"][layer]
        da = _mm_nt(dxo, w_dn, name=f"d_act{layer}", tm=min(512, T), tn=F // 2, out_dtype=F32)
        grads[("w_down", layer)] = _wgrad(a, dxo, f"g_w_down{layer}")
        cwl = conv_w_full[layer]
        du_g, du_v, p_g, p_v = _conv_gate_bwd(da, u, cwl, conv_b[layer].reshape(1, 2 * F), name=f"conv_bwd{layer}",
                                              B=B, S=S)
        small[("conv_w", layer)] = jnp.concatenate([p_g[0:3], p_v[0:3]], axis=1)
        small[("conv_b", layer)] = jnp.concatenate([p_g[3], p_v[3]], axis=0)
        grads[("w_up", layer)] = jnp.concatenate(
            [_wgrad(h, du_g, f"g_w_up_gate{layer}"), _wgrad(h, du_v, f"g_w_up_val{layer}")], axis=1)
        tok = start_reduce(f"ffn{layer}", [("w_down", layer), ("w_up", layer)], ["rows2", "cols2"])
        dxi, dg = _mm_nt_rmsbwd([(du_g, 0), (du_v, 1)], full["w_up"][layer], xin, ln_ffn_g[layer] + tok,
                                name=f"d_ffn_in{layer}", dres=dxo)
        small[("ln_ffn_g", layer)] = dg[0]
        return dxi

    def mem_bwd(proj, q_cb, memkv, h_mem, do_cat, o_mem, lse_m, layer):
        dqm, dmk, dmv = _softmax_bwd(proj, memkv, memkv, do_cat, o_mem, lse_m, name=f"mem_bwd{layer}", B=B, S=S,
                                     Sk=NM, P=PX, q_cb=q_cb, k_cb=0, v_cb=PX, do_cb=PM, causal=False)
        grads[("w_memkv", layer)] = jnp.concatenate(
            [_wgrad(h_mem, dmk, f"g_w_memk{layer}"), _wgrad(h_mem, dmv, f"g_w_memv{layer}")], axis=1)
        _, dg = _mm_nt_rmsbwd([(dmk, 0), (dmv, 1)], full["w_memkv"][layer], mem2d, ln_mem_g[layer],
                              name=f"d_mem_in{layer}", want_dx=False)
        small[("ln_mem_g", layer)] = dg[0]
        return dqm

    dx3 = conv_ffn_bwd(dx4, x3, u1, h_ffn1, a1, 1)
    do_cat1 = _mm_nt(dx3, full["w_out"][1], name="d_o_cat1", tm=min(512, T), tn=512, out_dtype=BF16)
    grads[("w_out", 1)] = _wgrad(o_cat1, dx3, "g_w_out1")
    dq1, dk1, dv1 = _sb_bwd_g(proj_b, kv, kv, do_cat1, rt1, name="sb_bwd", B=B, S=S, P=PM, q_cb=0, k_cb=0, v_cb=PM,
                            do_cb=0)
    dqm1 = mem_bwd(proj_b, PM, memkv1, h_mem1, do_cat1, o_mem1, lse_m1, 1)
    grads["w_in_b"] = jnp.concatenate([_wgrad(h_mix1, dq1, "g_w_in_b_q"), _wgrad(h_mix1, dqm1, "g_w_in_b_m")], axis=1)
    grads["w_kv"] = jnp.concatenate([_wgrad(h_kv, dk1, "g_w_kv_k"), _wgrad(h_kv, dv1, "g_w_kv_v")], axis=1)
    tok = start_reduce("mix1", [("w_out", 1), "w_in_b", "w_kv", ("w_memkv", 1)], ["rows2", "rows2", "cols2", "rows2"])
    dx2, dg = _mm_nt_rmsbwd([(dq1, 0), (dqm1, MAIN_W // MEM_W)], wb, x2, ln_mix_g[1] + tok, name="d_mix_in1", dres=dx3)
    small[("ln_mix_g", 1)] = dg[0]
    dx2, dg = _mm_nt_rmsbwd([(dk1, 0), (dv1, 1)], wkv, x2, ln_kv_g, name="d_kv_in", dres=dx2)
    small["ln_kv_g"] = dg[0]
    dx1 = conv_ffn_bwd(dx2, x1, u0, h_ffn0, a0, 0)
    do_cat0 = _mm_nt(dx1, full["w_out"][0], name="d_o_cat0", tm=min(512, T), tn=512, out_dtype=BF16)
    grads[("w_out", 0)] = _wgrad(o_cat0, dx1, "g_w_out0")
    dq0, dk0, dv0, dcs = _fox_bwd_g(proj_a, proj_a, proj_a, do_cat0, lse0, cr, name="fox_bwd", B=B, S=S, P=PM, q_cb=0,
                                  k_cb=PM, v_cb=2 * PM, do_cb=0)
    dqm0 = mem_bwd(proj_a, 3 * PM, memkv0, h_mem0, do_cat0, o_mem0, lse_m0, 0)
    dc2d = _pad_rows(dcs[:, :, 0, :].transpose(0, 2, 1).reshape(T, N_MAIN_HEADS), LANES, 1)
    df, db_f = _forget_cumsum_bwd(dc2d, f_logit, b_f, B=B, S=S, name="forget_cumsum_bwd")
    a_parts = [(dq0, 0), (dk0, 1), (dv0, 2), (dqm0, n_qkv // MEM_W), (df, n_main // LANES)]
    g_wa = jnp.concatenate([_wgrad(h_mix0, p, f"g_w_in_a{k}") for k, (p, _) in enumerate(a_parts)], axis=1)
    grads["w_in_a"] = jnp.concatenate([g_wa[:, :n_qkv], g_wa[:, n_main:n_main + N_MAIN_HEADS], g_wa[:, n_qkv:n_main]],
                                      axis=1)
    tok = start_reduce("mix0", [("w_out", 0), ("w_memkv", 0), "w_in_a"], ["rows2", "rows2", "rows2"])
    dx0, dg = _mm_nt_rmsbwd(a_parts, wa, x2d, ln_mix_g[0] + tok, name="d_mix_in0", dres=dx1)
    small[("ln_mix_g", 0)] = dg[0]
    grad_x = dx0.reshape(B, S, D)

    def both_small(name):
        return jnp.stack([small[(name, 0)], small[(name, 1)]])

    small_list = [("ln_mix_g", both_small("ln_mix_g")), ("b_f_a", db_f[:, :N_MAIN_HEADS]), ("ln_kv_g", small["ln_kv_g"]),
                  ("ln_mem_g", both_small("ln_mem_g")), ("ln_ffn_g", both_small("ln_ffn_g")),
                  ("conv_w", both_small("conv_w")), ("conv_b", both_small("conv_b")), ("final_g", dg_final[0]),
                  ("loss", loss_part[0, :1])]
    sm_rows = []
    for _, a in small_list:
        flat = a.reshape(-1)
        sm_rows.append(_pad_rows(flat, _round_up(flat.size, 8 * LANES), 0).reshape(-1, LANES))
    spack = jnp.concatenate(sm_rows, axis=0)
    s_ssem, s_rsem, s_thru, s_lands, s_tok = _scatter_start([spack], _whole, name="small_start")

    pieces = {}
    for gname, keys, ssem, rsem, thru, lands in reduce_groups:
        thru, lands = _scatter_wait(ssem, rsem, thru, lands, _slot, s_tok, name=f"reduce_wait_{gname}")
        for key, mine, land in zip(keys, thru, lands):
            own = lax.dynamic_index_in_dim(mine, my_idx, 0, keepdims=False)
            land = lax.dynamic_update_index_in_dim(land, own, my_idx, 0)
            tag = key if isinstance(key, str) else f"{key[0]}{key[1]}"
            pieces[key] = _sum_slots(land, name=f"sum_{tag}")

    red = {}
    for n in ("w_in_a", "w_in_b", "w_kv"):
        red[n] = pieces[n].reshape(shards[n].shape)
    for n in ("w_memkv", "w_out", "w_up", "w_down"):
        red[n] = jnp.stack([pieces[(n, 0)], pieces[(n, 1)]])

    weights = {"ln_mix_g": ln_mix_g, "w_in_a": w_in_a, "b_f_a": b_f_a, "w_in_b": w_in_b, "ln_kv_g": ln_kv_g,
               "w_kv": w_kv, "ln_mem_g": ln_mem_g, "w_memkv": w_memkv, "w_out": w_out, "ln_ffn_g": ln_ffn_g,
               "w_up": w_up, "conv_w": conv_w, "conv_b": conv_b, "w_down": w_down, "final_g": final_g}
    m_in = {"ln_mix_g": m_ln_mix_g, "w_in_a": m_w_in_a, "b_f_a": m_b_f_a, "w_in_b": m_w_in_b, "ln_kv_g": m_ln_kv_g,
            "w_kv": m_w_kv, "ln_mem_g": m_ln_mem_g, "w_memkv": m_w_memkv, "w_out": m_w_out, "ln_ffn_g": m_ln_ffn_g,
            "w_up": m_w_up, "conv_w": m_conv_w, "conv_b": m_conv_b, "w_down": m_w_down, "final_g": m_final_g}
    v_in = {"ln_mix_g": v_ln_mix_g, "w_in_a": v_w_in_a, "b_f_a": v_b_f_a, "w_in_b": v_w_in_b, "ln_kv_g": v_ln_kv_g,
            "w_kv": v_w_kv, "ln_mem_g": v_ln_mem_g, "w_memkv": v_w_memkv, "w_out": v_w_out, "ln_ffn_g": v_ln_ffn_g,
            "w_up": v_w_up, "conv_w": v_conv_w, "conv_b": v_conv_b, "w_down": v_w_down, "final_g": v_final_g}
    order = list(weights)
    big_names = [n for n, _ in BIG]
    g_out, d_out, nm_out, nv_out = {}, {}, {}, {}

    def update(n):
        w = weights[n]
        cols = w.shape[-1]
        g = red[n].reshape(w.shape)
        d, nm, nv = _adamw(w.reshape(-1, cols), g.reshape(-1, cols), m_in[n].reshape(-1, cols),
                           v_in[n].reshape(-1, cols), name=f"adamw_{n}")
        g_out[n], d_out[n], nm_out[n], nv_out[n] = g, d.reshape(w.shape), nm.reshape(w.shape), nv.reshape(w.shape)

    for n in big_names:
        update(n)
    s_thru, s_lands = _scatter_wait(s_ssem, s_rsem, s_thru, s_lands, _whole, d_out[big_names[-1]], name="small_wait")
    ssum = _sum_slots(lax.dynamic_update_index_in_dim(s_lands[0], s_thru[0], my_idx, 0), name="sum_small")
    off = 0
    for (n, a), rows in zip(small_list, sm_rows):
        red[n] = ssum[off:off + rows.shape[0]].reshape(-1)[:a.size].reshape(a.shape)
        off += rows.shape[0]
    loss = red["loss"][0]
    shard_cols = conv_w.shape[2]
    red["conv_w"] = lax.dynamic_slice_in_dim(red["conv_w"], my_idx * shard_cols, shard_cols, axis=2)
    red["b_f_a"] = red["b_f_a"].reshape(b_f_a.shape)
    update("conv_w")
    small_names = [n for n in order if n not in g_out]

    def pack_small(src):
        rows = []
        for n in small_names:
            flat = src[n].reshape(-1)
            rows.append(_pad_rows(flat, _round_up(flat.size, 8 * LANES), 0).reshape(-1, LANES))
        return jnp.concatenate(rows, axis=0), [r.shape[0] for r in rows]

    red_small = {n: red[n].reshape(weights[n].shape) for n in small_names}
    wp, counts = pack_small(weights)
    gp, _ = pack_small(red_small)
    mp, _ = pack_small(m_in)
    vp, _ = pack_small(v_in)
    dp, nmp, nvp = _adamw(wp, gp, mp, vp, name="adamw_small")
    off = 0
    for n, cnt in zip(small_names, counts):
        shp = weights[n].shape
        size = weights[n].size
        g_out[n] = red_small[n]
        d_out[n] = dp[off:off + cnt].reshape(-1)[:size].reshape(shp)
        nm_out[n] = nmp[off:off + cnt].reshape(-1)[:size].reshape(shp)
        nv_out[n] = nvp[off:off + cnt].reshape(-1)[:size].reshape(shp)
        off += cnt

    return (loss, grad_x, *[g_out[n] for n in order], *[d_out[n] for n in order],
            *[nm_out[n] for n in order], *[nv_out[n] for n in order])
```

```python
import functools

import jax
import jax.numpy as jnp
from jax import lax
from jax.experimental import pallas as pl
from jax.experimental.pallas import tpu as pltpu

F32 = jnp.float32
BF16 = jnp.bfloat16
LANES = 128
HEAD_DIM = 64
N_MAIN_HEADS = 12
N_MEM_HEADS = 4
MAIN_W = N_MAIN_HEADS * HEAD_DIM
MEM_W = N_MEM_HEADS * HEAD_DIM
SCALE = HEAD_DIM ** -0.5
EPS = 1e-6
NEG = -1e30
N_DEV = 8
ATT_TILE = 256
MEM_Q_TILE = 1024
VMEM_BIG = 56 * 2 ** 20
MESH = pl.DeviceIdType.MESH

ADAM_LR = 0.001
ADAM_B1 = 0.9
ADAM_B2 = 0.999
ADAM_EPS = 1e-08
ADAM_WD = 0.01
ADAM_STEP = 10

NT = (((1,), (1,)), ((), ()))
TN = (((0,), (0,)), ((), ()))


def _pc(body, *, name, out_shape, grid=None, in_specs=None, out_specs=None, scratch_shapes=(),
        semantics=None, vmem=None):
    kw = {}
    if grid is not None:
        kw["grid"] = grid
    params = pltpu.CompilerParams(dimension_semantics=semantics, vmem_limit_bytes=vmem)
    return pl.pallas_call(body, name=name, out_shape=out_shape, in_specs=in_specs, out_specs=out_specs,
                          scratch_shapes=list(scratch_shapes), compiler_params=params, **kw)


def _sds(shape, dtype):
    return jax.ShapeDtypeStruct(shape, dtype)


def _mm_fwd(a, w, *, name, tm, tn, out_dtype, g=None, res=None, col0=0, ncols=None, save_h=False):
    m_rows, k = a.shape
    n = w.shape[1] if ncols is None else ncols
    grid = (m_rows // tm, n // tn)
    norm = g is not None

    def body(*refs):
        refs = list(refs)
        a_ref = refs.pop(0)
        g_ref = refs.pop(0) if norm else None
        w_ref = refs.pop(0)
        res_ref = refs.pop(0) if res is not None else None
        o_ref = refs.pop(0)
        hout_ref = refs.pop(0) if save_h else None
        h_ref = refs.pop(0) if norm else None
        if norm:
            @pl.when(pl.program_id(1) == 0)
            def _():
                xv = a_ref[...]
                r = lax.rsqrt(jnp.mean(xv * xv, axis=-1, keepdims=True) + EPS)
                h = ((xv * r) * g_ref[...]).astype(BF16)
                h_ref[...] = h
                if save_h:
                    hout_ref[...] = h
            lhs = h_ref[...]
        else:
            lhs = a_ref[...].astype(BF16)
        acc = jnp.dot(lhs, w_ref[...], preferred_element_type=F32)
        if res is not None:
            acc = acc + res_ref[...]
        o_ref[...] = acc.astype(out_dtype)

    in_specs = [pl.BlockSpec((tm, k), lambda i, j: (i, 0))]
    args = [a]
    if norm:
        in_specs.append(pl.BlockSpec((1, k), lambda i, j: (0, 0)))
        args.append(g.reshape(1, k))
    in_specs.append(pl.BlockSpec((k, tn), lambda i, j: (0, j + col0)))
    args.append(w)
    if res is not None:
        in_specs.append(pl.BlockSpec((tm, tn), lambda i, j: (i, j)))
        args.append(res)
    out_shape = [_sds((m_rows, n), out_dtype)]
    out_specs = [pl.BlockSpec((tm, tn), lambda i, j: (i, j))]
    if save_h:
        out_shape.append(_sds((m_rows, k), BF16))
        out_specs.append(pl.BlockSpec((tm, k), lambda i, j: (i, 0)))
    scratch = [pltpu.VMEM((tm, k), BF16)] if norm else []
    outs = _pc(body, name=name, out_shape=out_shape, grid=grid, in_specs=in_specs, out_specs=out_specs,
               scratch_shapes=scratch, semantics=("arbitrary", "arbitrary"), vmem=VMEM_BIG)(*args)
    return outs if save_h else outs[0]


def _mm_nt(a, w, *, name, tm, tn, out_dtype):
    m_rows, k = a.shape
    n = w.shape[0]

    def body(a_ref, w_ref, o_ref):
        acc = lax.dot_general(a_ref[...].astype(BF16), w_ref[...], NT, preferred_element_type=F32)
        o_ref[...] = acc.astype(out_dtype)

    return _pc(body, name=name, out_shape=_sds((m_rows, n), out_dtype), grid=(m_rows // tm, n // tn),
               in_specs=[pl.BlockSpec((tm, k), lambda i, j: (i, 0)), pl.BlockSpec((tn, k), lambda i, j: (j, 0))],
               out_specs=pl.BlockSpec((tm, tn), lambda i, j: (i, j)),
               semantics=("arbitrary", "arbitrary"), vmem=VMEM_BIG)(a, w)


def _mm_tn(a, b, *, name, ta, tn, tt):
    t_rows, ka = a.shape
    n = b.shape[1]
    nt = t_rows // tt

    def body(a_ref, b_ref, o_ref, acc_ref):
        t = pl.program_id(2)

        @pl.when(t == 0)
        def _():
            acc_ref[...] = jnp.zeros_like(acc_ref)

        acc_ref[...] += lax.dot_general(a_ref[...].astype(BF16), b_ref[...].astype(BF16), TN,
                                        preferred_element_type=F32)

        @pl.when(t == nt - 1)
        def _():
            o_ref[...] = acc_ref[...].astype(BF16)

    return _pc(body, name=name, out_shape=_sds((ka, n), BF16), grid=(ka // ta, n // tn, nt),
               in_specs=[pl.BlockSpec((tt, ta), lambda i, j, t: (t, i)),
                         pl.BlockSpec((tt, tn), lambda i, j, t: (t, j))],
               out_specs=pl.BlockSpec((ta, tn), lambda i, j, t: (i, j)),
               scratch_shapes=[pltpu.VMEM((ta, tn), F32)],
               semantics=("arbitrary", "arbitrary", "arbitrary"), vmem=VMEM_BIG)(a, b)


def _wgrad(a, b, name):
    t_rows, ka = a.shape
    n = b.shape[1]
    ta = ka if ka <= 1024 else ka // 2
    tn = n
    while ta * tn * 4 > 6 * 2 ** 20 and tn % 256 == 0:
        tn //= 2
    tt = min(1024, t_rows)
    return _mm_tn(a, b, name=name, ta=ta, tn=tn, tt=tt)


def _mm_nt_rmsbwd(parts, w, x, g, *, name, dres=None, want_dx=True):
    m_rows, d = x.shape
    tm = min(256, m_rows)
    n_parts = len(parts)

    def body(*refs):
        refs = list(refs)
        dy_refs = [refs.pop(0) for _ in range(n_parts)]
        w_refs = [refs.pop(0) for _ in range(n_parts)]
        x_ref = refs.pop(0)
        g_ref = refs.pop(0)
        dres_ref = refs.pop(0) if dres is not None else None
        dx_ref = refs.pop(0) if want_dx else None
        dg_ref = refs.pop(0)

        @pl.when(pl.program_id(0) == 0)
        def _():
            dg_ref[...] = jnp.zeros_like(dg_ref)

        dh = None
        for dy_ref, w_ref in zip(dy_refs, w_refs):
            t = lax.dot_general(dy_ref[...].astype(BF16), w_ref[...], NT, preferred_element_type=F32)
            dh = t if dh is None else dh + t
        xv = x_ref[...]
        r = lax.rsqrt(jnp.mean(xv * xv, axis=-1, keepdims=True) + EPS)
        xh = xv * r
        dg_ref[...] += jnp.sum(dh * xh, axis=0, keepdims=True)
        if want_dx:
            dhg = dh * g_ref[...]
            dx = r * (dhg - xh * jnp.mean(dhg * xh, axis=-1, keepdims=True))
            if dres is not None:
                dx = dx + dres_ref[...]
            dx_ref[...] = dx

    in_specs, args = [], []
    for dy, _ in parts:
        in_specs.append(pl.BlockSpec((tm, dy.shape[1]), lambda i: (i, 0)))
        args.append(dy)
    for dy, cb in parts:
        in_specs.append(pl.BlockSpec((d, dy.shape[1]), functools.partial(lambda i, cb: (0, cb), cb=cb)))
        args.append(w)
    in_specs += [pl.BlockSpec((tm, d), lambda i: (i, 0)), pl.BlockSpec((1, d), lambda i: (0, 0))]
    args += [x, g.reshape(1, d)]
    if dres is not None:
        in_specs.append(pl.BlockSpec((tm, d), lambda i: (i, 0)))
        args.append(dres)
    out_shape, out_specs = [], []
    if want_dx:
        out_shape.append(_sds((m_rows, d), F32))
        out_specs.append(pl.BlockSpec((tm, d), lambda i: (i, 0)))
    out_shape.append(_sds((1, d), F32))
    out_specs.append(pl.BlockSpec((1, d), lambda i: (0, 0)))
    outs = _pc(body, name=name, out_shape=out_shape, grid=(m_rows // tm,), in_specs=in_specs,
               out_specs=out_specs, semantics=("arbitrary",), vmem=VMEM_BIG)(*args)
    return (outs[0], outs[1]) if want_dx else (None, outs[0])


def _loss_head(x, g, tgt, *, name):
    m_rows, d = x.shape
    tm = min(256, m_rows)

    def body(x_ref, g_ref, t_ref, dx_ref, dg_ref, loss_ref):
        @pl.when(pl.program_id(0) == 0)
        def _():
            dg_ref[...] = jnp.zeros_like(dg_ref)
            loss_ref[...] = jnp.zeros_like(loss_ref)

        xv = x_ref[...]
        r = lax.rsqrt(jnp.mean(xv * xv, axis=-1, keepdims=True) + EPS)
        xh = xv * r
        gv = g_ref[...]
        err = xh * gv - t_ref[...]
        per_tok = jnp.mean(err * err, axis=-1, keepdims=True)
        loss_ref[...] += 0.5 * jnp.sum(per_tok, axis=0, keepdims=True)
        dout = err * (1.0 / d)
        dg_ref[...] += jnp.sum(dout * xh, axis=0, keepdims=True)
        dhg = dout * gv
        dx_ref[...] = r * (dhg - xh * jnp.mean(dhg * xh, axis=-1, keepdims=True))

    row = pl.BlockSpec((tm, d), lambda i: (i, 0))
    return _pc(body, name=name, out_shape=[_sds((m_rows, d), F32), _sds((1, d), F32), _sds((1, LANES), F32)],
               grid=(m_rows // tm,), in_specs=[row, pl.BlockSpec((1, d), lambda i: (0, 0)), row],
               out_specs=[row, pl.BlockSpec((1, d), lambda i: (0, 0)), pl.BlockSpec((1, LANES), lambda i: (0, 0))],
               semantics=("arbitrary",))(x, g.reshape(1, d), tgt)


def _split3(v):
    hi = v.astype(BF16)
    r1 = v - hi.astype(F32)
    mid = r1.astype(BF16)
    lo = (r1 - mid.astype(F32)).astype(BF16)
    return hi, mid, lo


def _split2(v):
    hi = v.astype(BF16)
    lo = (v - hi.astype(F32)).astype(BF16)
    return hi, lo


def _tri_dot3(tri, v):
    hi, mid, lo = _split3(v)
    return (jnp.dot(tri, hi, preferred_element_type=F32) + jnp.dot(tri, mid, preferred_element_type=F32)
            + jnp.dot(tri, lo, preferred_element_type=F32))


def _dot_tri2(v, tri):
    hi, lo = _split2(v)
    return jnp.dot(hi, tri, preferred_element_type=F32) + jnp.dot(lo, tri, preferred_element_type=F32)


def _log_sigmoid(v):
    return jnp.minimum(v, 0.0) - jnp.log(1.0 + jnp.exp(-jnp.abs(v)))


def _forget_cumsum(f_logit, b_f, *, B, S, name):
    ch = min(256, S)
    nch = S // ch

    def body(f_ref, b_ref, c_ref):
        r_i = lax.broadcasted_iota(jnp.int32, (ch, ch), 0)
        c_i = lax.broadcasted_iota(jnp.int32, (ch, ch), 1)
        tri = (c_i <= r_i).astype(BF16)
        bv = b_ref[...]

        def step(k, carry):
            rows = pl.ds(pl.multiple_of(k * ch, ch), ch)
            lf = _log_sigmoid(f_ref[rows, :] + bv)
            c_ref[rows, :] = _tri_dot3(tri, lf) + carry
            return carry + jnp.sum(lf, axis=0, keepdims=True)

        lax.fori_loop(0, nch, step, jnp.zeros((1, LANES), F32))

    blk = pl.BlockSpec((S, LANES), lambda b: (b, 0))
    return _pc(body, name=name, out_shape=_sds((B * S, LANES), F32), grid=(B,),
               in_specs=[blk, pl.BlockSpec((1, LANES), lambda b: (0, 0))], out_specs=blk,
               semantics=("arbitrary",))(f_logit, b_f)


def _forget_cumsum_bwd(dc, f_logit, b_f, *, B, S, name):
    ch = min(256, S)
    nch = S // ch

    def body(dc_ref, f_ref, b_ref, df_ref, db_ref):
        @pl.when(pl.program_id(0) == 0)
        def _():
            db_ref[...] = jnp.zeros_like(db_ref)

        r_i = lax.broadcasted_iota(jnp.int32, (ch, ch), 0)
        c_i = lax.broadcasted_iota(jnp.int32, (ch, ch), 1)
        tri = (c_i >= r_i).astype(BF16)
        bv = b_ref[...]

        def step(kk, carry):
            tail, dbs = carry
            k = nch - 1 - kk
            rows = pl.ds(pl.multiple_of(k * ch, ch), ch)
            dcv = dc_ref[rows, :]
            dlf = _tri_dot3(tri, dcv) + tail
            z = f_ref[rows, :] + bv
            df = dlf * (1.0 / (1.0 + jnp.exp(z)))
            df_ref[rows, :] = df.astype(BF16)
            return tail + jnp.sum(dcv, axis=0, keepdims=True), dbs + jnp.sum(df, axis=0, keepdims=True)

        zero = jnp.zeros((1, LANES), F32)
        _, dbs = lax.fori_loop(0, nch, step, (zero, zero))
        db_ref[...] += dbs

    blk = pl.BlockSpec((S, LANES), lambda b: (b, 0))
    one = pl.BlockSpec((1, LANES), lambda b: (0, 0))
    return _pc(body, name=name, out_shape=[_sds((B * S, LANES), BF16), _sds((1, LANES), F32)], grid=(B,),
               in_specs=[blk, blk, one], out_specs=[blk, one], semantics=("arbitrary",))(dc, f_logit, b_f)


def _head_mask(lane, hh):
    return (lane < HEAD_DIM) if hh == 0 else (lane >= HEAD_DIM)


def _col_spec(rows, nblk_rows, cb):
    return pl.BlockSpec((rows, LANES), lambda b, p, i: (b * nblk_rows + i, cb + p))


def _kv_spec(rows, cb):
    return pl.BlockSpec((rows, LANES), lambda b, p, i: (b, cb + p))


def _stat_col_spec(tq):
    return pl.BlockSpec((1, 2, tq, 1), lambda b, p, i: (b, p, i, 0))


def _stat_row_spec(S):
    return pl.BlockSpec((1, 2, 1, S), lambda b, p, i: (b, p, 0, 0))


def _softmax_fwd(qa, ka, va, *, name, B, S, Sk, P, q_cb, k_cb, v_cb, causal, cc=None, cr=None):
    tq = min(ATT_TILE if causal else MEM_Q_TILE, S)
    tk = min(ATT_TILE, Sk)
    nq, nk = S // tq, Sk // tk
    decay = cc is not None
    assert not causal or (tq == tk and S == Sk)

    def body(*refs):
        if decay:
            q_ref, k_ref, v_ref, cc_ref, cr_ref, o_ref, lse_ref = refs
        else:
            q_ref, k_ref, v_ref, o_ref, lse_ref = refs
        i = pl.program_id(2)
        q = q_ref[...]
        lane = lax.broadcasted_iota(jnp.int32, (tq, LANES), 1)
        row = lax.broadcasted_iota(jnp.int32, (tq, tk), 0) + i * tq
        col0 = lax.broadcasted_iota(jnp.int32, (tq, tk), 1)
        outs = []
        for hh in range(2):
            qh = jnp.where(_head_mask(lane, hh), q, jnp.zeros_like(q))

            def step(kb, carry, hh=hh, qh=qh):
                m, l, acc = carry
                ks = pl.multiple_of(kb * tk, tk)
                kblk = k_ref[pl.ds(ks, tk), :]
                vblk = v_ref[pl.ds(ks, tk), :]
                s = lax.dot_general(qh, kblk, NT, preferred_element_type=F32) * SCALE
                if decay:
                    s = s + (cc_ref[0, hh] - cr_ref[0, hh, :, pl.ds(ks, tk)])
                if causal:
                    s = jnp.where(col0 + kb * tk <= row, s, NEG)
                m_new = jnp.maximum(m, jnp.max(s, axis=-1, keepdims=True))
                alpha = jnp.exp(m - m_new)
                p = jnp.exp(s - m_new)
                l = alpha * l + jnp.sum(p, axis=-1, keepdims=True)
                acc = alpha * acc + jnp.dot(p.astype(BF16), vblk, preferred_element_type=F32)
                return m_new, l, acc

            init = (jnp.full((tq, 1), NEG, F32), jnp.zeros((tq, 1), F32), jnp.zeros((tq, LANES), F32))
            m, l, acc = lax.fori_loop(0, (i + 1) if causal else nk, step, init)
            outs.append(acc / l)
            lse_ref[0, hh] = m + jnp.log(l)
        o_ref[...] = jnp.where(lane < HEAD_DIM, outs[0], outs[1]).astype(BF16)

    in_specs = [_col_spec(tq, nq, q_cb), _kv_spec(Sk, k_cb), _kv_spec(Sk, v_cb)]
    args = [qa, ka, va]
    if decay:
        in_specs += [_stat_col_spec(tq), _stat_row_spec(S)]
        args += [cc, cr]
    return _pc(body, name=name,
               out_shape=[_sds((B * S, P * LANES), BF16), _sds((B, 2 * P, S, 1), F32)],
               grid=(B, P, nq), in_specs=in_specs, out_specs=[_col_spec(tq, nq, 0), _stat_col_spec(tq)],
               semantics=("arbitrary", "arbitrary", "arbitrary"), vmem=VMEM_BIG)(*args)


def _softmax_bwd(qa, ka, va, doa, oa, lse, *, name, B, S, Sk, P, q_cb, k_cb, v_cb, do_cb, causal,
                 cc=None, cr=None):
    tq = min(ATT_TILE if causal else MEM_Q_TILE, S)
    tk = min(ATT_TILE, Sk)
    nq, nk = S // tq, Sk // tk
    decay = cc is not None

    def body(*refs):
        if decay:
            q_ref, k_ref, v_ref, do_ref, o_ref, lse_ref, cc_ref, cr_ref, dq_ref, dk_ref, dv_ref, dcs_ref = refs
        else:
            q_ref, k_ref, v_ref, do_ref, o_ref, lse_ref, dq_ref, dk_ref, dv_ref = refs
        i = pl.program_id(2)

        @pl.when(i == 0)
        def _():
            dk_ref[...] = jnp.zeros_like(dk_ref)
            dv_ref[...] = jnp.zeros_like(dv_ref)
            if decay:
                dcs_ref[...] = jnp.zeros_like(dcs_ref)

        q = q_ref[...]
        do = do_ref[...]
        prod = do.astype(F32) * o_ref[...].astype(F32)
        lane = lax.broadcasted_iota(jnp.int32, (tq, LANES), 1)
        row = lax.broadcasted_iota(jnp.int32, (tq, tk), 0) + i * tq
        col0 = lax.broadcasted_iota(jnp.int32, (tq, tk), 1)
        dqs = []
        for hh in range(2):
            hmask = _head_mask(lane, hh)
            qh = jnp.where(hmask, q, jnp.zeros_like(q))
            doh = jnp.where(hmask, do, jnp.zeros_like(do))
            lse_h = lse_ref[0, hh]
            n_blocks = (i + 1) if causal else nk

            def probs(kb, hh=hh, qh=qh, doh=doh, lse_h=lse_h):
                ks = pl.multiple_of(kb * tk, tk)
                kblk = k_ref[pl.ds(ks, tk), :]
                vblk = v_ref[pl.ds(ks, tk), :]
                s = lax.dot_general(qh, kblk, NT, preferred_element_type=F32) * SCALE
                if decay:
                    s = s + (cc_ref[0, hh] - cr_ref[0, hh, :, pl.ds(ks, tk)])
                if causal:
                    s = jnp.where(col0 + kb * tk <= row, s, NEG)
                p = jnp.exp(s - lse_h)
                dp = lax.dot_general(doh, vblk, NT, preferred_element_type=F32)
                return ks, kblk, p, dp

            if decay:
                def delta_step(kb, acc):
                    _, _, p, dp = probs(kb)
                    return acc + jnp.sum(p * dp, axis=-1, keepdims=True)

                delta = lax.fori_loop(0, n_blocks, delta_step, jnp.zeros((tq, 1), F32))
            else:
                delta = jnp.sum(jnp.where(hmask, prod, 0.0), axis=-1, keepdims=True)

            def step(kb, dq_acc, hh=hh, qh=qh, doh=doh, delta=delta):
                ks, kblk, p, dp = probs(kb)
                ds = p * (dp - delta)
                dsb = ds.astype(BF16)
                dk_ref[pl.ds(ks, tk), :] += lax.dot_general(dsb, qh, TN, preferred_element_type=F32) * SCALE
                dv_ref[pl.ds(ks, tk), :] += lax.dot_general(p.astype(BF16), doh, TN, preferred_element_type=F32)
                if decay:
                    dcs_ref[0, hh, :, pl.ds(ks, tk)] -= jnp.sum(ds, axis=0, keepdims=True)
                return dq_acc + jnp.dot(dsb, kblk, preferred_element_type=F32)

            dqs.append(lax.fori_loop(0, n_blocks, step, jnp.zeros((tq, LANES), F32)) * SCALE)
        dq_ref[...] = jnp.where(lane < HEAD_DIM, dqs[0], dqs[1]).astype(BF16)

    in_specs = [_col_spec(tq, nq, q_cb), _kv_spec(Sk, k_cb), _kv_spec(Sk, v_cb), _col_spec(tq, nq, do_cb),
                _col_spec(tq, nq, 0), _stat_col_spec(tq)]
    args = [qa, ka, va, doa, oa, lse]
    out_shape = [_sds((B * S, P * LANES), BF16), _sds((B * Sk, P * LANES), F32), _sds((B * Sk, P * LANES), F32)]
    out_specs = [_col_spec(tq, nq, 0), _kv_spec(Sk, 0), _kv_spec(Sk, 0)]
    if decay:
        in_specs += [_stat_col_spec(tq), _stat_row_spec(S)]
        args += [cc, cr]
        out_shape.append(_sds((B, 2 * P, 1, S), F32))
        out_specs.append(_stat_row_spec(S))
    return _pc(body, name=name, out_shape=out_shape, grid=(B, P, nq), in_specs=in_specs, out_specs=out_specs,
               semantics=("arbitrary", "arbitrary", "arbitrary"), vmem=VMEM_BIG)(*args)


def _sb_terms(qh, kblk, row, col0, kb, tk):
    z = lax.dot_general(qh, kblk, NT, preferred_element_type=F32) * SCALE
    causal = (col0 + kb * tk) < row
    sp = jnp.maximum(z, 0.0) + jnp.log(1.0 + jnp.exp(-jnp.abs(z)))
    ls = z - sp
    lm = jnp.where(causal, -sp, 0.0)
    return causal, ls, lm


def _stickbreak_fwd(qa, ka, va, *, name, B, S, P, q_cb, k_cb, v_cb):
    tq = tk = min(ATT_TILE, S)
    nq = S // tq

    def body(q_ref, k_ref, v_ref, o_ref, rt_ref):
        i = pl.program_id(2)
        q = q_ref[...]
        lane = lax.broadcasted_iota(jnp.int32, (tq, LANES), 1)
        row = lax.broadcasted_iota(jnp.int32, (tq, tk), 0) + i * tq
        col0 = lax.broadcasted_iota(jnp.int32, (tq, tk), 1)
        t_r = lax.broadcasted_iota(jnp.int32, (tk, tk), 0)
        t_c = lax.broadcasted_iota(jnp.int32, (tk, tk), 1)
        after = (t_r > t_c).astype(BF16)
        outs = []
        for hh in range(2):
            qh = jnp.where(_head_mask(lane, hh), q, jnp.zeros_like(q))

            def step(jj, carry, qh=qh):
                run, acc = carry
                kb = i - jj
                ks = pl.multiple_of(kb * tk, tk)
                kblk = k_ref[pl.ds(ks, tk), :]
                vblk = v_ref[pl.ds(ks, tk), :]
                causal, ls, lm = _sb_terms(qh, kblk, row, col0, kb, tk)
                suf = _dot_tri2(lm, after)
                a = jnp.where(causal, jnp.exp(ls + run + suf), 0.0)
                acc = acc + jnp.dot(a.astype(BF16), vblk, preferred_element_type=F32)
                return run + jnp.sum(lm, axis=-1, keepdims=True), acc

            run, acc = lax.fori_loop(0, i + 1, step, (jnp.zeros((tq, 1), F32), jnp.zeros((tq, LANES), F32)))
            outs.append(acc)
            rt_ref[0, hh] = run
        o_ref[...] = jnp.where(lane < HEAD_DIM, outs[0], outs[1]).astype(BF16)

    return _pc(body, name=name, out_shape=[_sds((B * S, P * LANES), BF16), _sds((B, 2 * P, S, 1), F32)],
               grid=(B, P, nq), in_specs=[_col_spec(tq, nq, q_cb), _kv_spec(S, k_cb), _kv_spec(S, v_cb)],
               out_specs=[_col_spec(tq, nq, 0), _stat_col_spec(tq)],
               semantics=("arbitrary", "arbitrary", "arbitrary"), vmem=VMEM_BIG)(qa, ka, va)


def _stickbreak_bwd(qa, ka, va, doa, rt, *, name, B, S, P, q_cb, k_cb, v_cb, do_cb):
    tq = tk = min(ATT_TILE, S)
    nq = S // tq

    def body(q_ref, k_ref, v_ref, do_ref, rt_ref, dq_ref, dk_ref, dv_ref):
        i = pl.program_id(2)

        @pl.when(i == 0)
        def _():
            dk_ref[...] = jnp.zeros_like(dk_ref)
            dv_ref[...] = jnp.zeros_like(dv_ref)

        q = q_ref[...]
        do = do_ref[...]
        lane = lax.broadcasted_iota(jnp.int32, (tq, LANES), 1)
        row = lax.broadcasted_iota(jnp.int32, (tq, tk), 0) + i * tq
        col0 = lax.broadcasted_iota(jnp.int32, (tq, tk), 1)
        t_r = lax.broadcasted_iota(jnp.int32, (tk, tk), 0)
        t_c = lax.broadcasted_iota(jnp.int32, (tk, tk), 1)
        upto = (t_r <= t_c).astype(BF16)
        before = (t_r < t_c).astype(BF16)
        dqs = []
        for hh in range(2):
            hmask = _head_mask(lane, hh)
            qh = jnp.where(hmask, q, jnp.zeros_like(q))
            doh = jnp.where(hmask, do, jnp.zeros_like(do))
            rt_h = rt_ref[0, hh]

            def step(kb, carry, qh=qh, doh=doh, rt_h=rt_h):
                pl_sum, pg_sum, dq_acc = carry
                ks = pl.multiple_of(kb * tk, tk)
                kblk = k_ref[pl.ds(ks, tk), :]
                vblk = v_ref[pl.ds(ks, tk), :]
                causal, ls, lm = _sb_terms(qh, kblk, row, col0, kb, tk)
                pin = _dot_tri2(lm, upto)
                a = jnp.where(causal, jnp.exp(ls + (rt_h - pl_sum) - pin), 0.0)
                da = lax.dot_general(doh, vblk, NT, preferred_element_type=F32)
                gm = a * da
                pg = _dot_tri2(gm, before) + pg_sum
                beta = jnp.exp(ls)
                dz = jnp.where(causal, gm * (1.0 - beta) - pg * beta, 0.0)
                dzb = dz.astype(BF16)
                dk_ref[pl.ds(ks, tk), :] += lax.dot_general(dzb, qh, TN, preferred_element_type=F32) * SCALE
                dv_ref[pl.ds(ks, tk), :] += lax.dot_general(a.astype(BF16), doh, TN, preferred_element_type=F32)
                return (pl_sum + jnp.sum(lm, axis=-1, keepdims=True),
                        pg_sum + jnp.sum(gm, axis=-1, keepdims=True),
                        dq_acc + jnp.dot(dzb, kblk, preferred_element_type=F32))

            zc = jnp.zeros((tq, 1), F32)
            _, _, dq_h = lax.fori_loop(0, i + 1, step, (zc, zc, jnp.zeros((tq, LANES), F32)))
            dqs.append(dq_h * SCALE)
        dq_ref[...] = jnp.where(lane < HEAD_DIM, dqs[0], dqs[1]).astype(BF16)

    return _pc(body, name=name,
               out_shape=[_sds((B * S, P * LANES), BF16), _sds((B * S, P * LANES), F32), _sds((B * S, P * LANES), F32)],
               grid=(B, P, nq),
               in_specs=[_col_spec(tq, nq, q_cb), _kv_spec(S, k_cb), _kv_spec(S, v_cb), _col_spec(tq, nq, do_cb),
                         _stat_col_spec(tq)],
               out_specs=[_col_spec(tq, nq, 0), _kv_spec(S, 0), _kv_spec(S, 0)],
               semantics=("arbitrary", "arbitrary", "arbitrary"), vmem=VMEM_BIG)(qa, ka, va, doa, rt)


def _pair_setup(q_ref, tq, tk):
    q = q_ref[...] * jnp.asarray(SCALE, BF16)
    lane = lax.broadcasted_iota(jnp.int32, (tq, LANES), 1)
    masks = [_head_mask(lane, hh) for hh in range(2)]
    qh = [jnp.where(mk, q, jnp.zeros_like(q)) for mk in masks]
    on_or_below = (lax.broadcasted_iota(jnp.int32, (tq, tk), 1) <= lax.broadcasted_iota(jnp.int32, (tq, tk), 0))
    return lane, masks, qh, on_or_below


def _fox_fwd(qa, ka, va, cr, *, name, B, S, P, q_cb, k_cb, v_cb):
    tq = tk = min(ATT_TILE, S)
    nq = S // tq

    def body(q_ref, k_ref, v_ref, cr_ref, o_ref, lse_ref, s_buf, acc_ref, m_ref, l_ref):
        i = pl.program_id(2)
        lane, _, qh, on_or_below = _pair_setup(q_ref, tq, tk)
        m_ref[...] = jnp.full(m_ref.shape, NEG, F32)
        l_ref[...] = jnp.zeros(l_ref.shape, F32)
        acc_ref[...] = jnp.zeros(acc_ref.shape, F32)

        def scores(kb, slot):
            kblk = k_ref[pl.ds(pl.multiple_of(kb * tk, tk), tk), :]
            for hh in range(2):
                s_buf[slot, hh] = lax.dot_general(qh[hh], kblk, NT, preferred_element_type=F32)

        def block(kb, slot, diag):
            ks = pl.multiple_of(kb * tk, tk)
            vblk = v_ref[pl.ds(ks, tk), :]
            ps = []
            for hh in range(2):
                s = s_buf[slot, hh] - cr_ref[0, hh, :, pl.ds(ks, tk)]
                if diag:
                    s = jnp.where(on_or_below, s, NEG)
                m = m_ref[hh]
                m_new = jnp.maximum(m, jnp.max(s, axis=-1, keepdims=True))
                alpha = jnp.exp(m - m_new)
                p = jnp.exp(s - m_new)
                m_ref[hh] = m_new
                l_ref[hh] = alpha * l_ref[hh] + jnp.sum(p, axis=-1, keepdims=True)
                ps.append((alpha, p.astype(BF16)))
            for hh in range(2):
                acc_ref[hh] = ps[hh][0] * acc_ref[hh] + jnp.dot(ps[hh][1], vblk, preferred_element_type=F32)

        def step(kb, _):
            slot = lax.rem(kb, 2)
            scores(kb + 1, 1 - slot)
            block(kb, slot, False)
            return 0

        scores(0, 0)
        lax.fori_loop(0, i, step, 0)
        block(i, lax.rem(i, 2), True)
        l0, l1 = l_ref[0], l_ref[1]
        lse_ref[0, 0] = m_ref[0] + jnp.log(l0)
        lse_ref[0, 1] = m_ref[1] + jnp.log(l1)
        o_ref[...] = jnp.where(lane < HEAD_DIM, acc_ref[0] / l0, acc_ref[1] / l1).astype(BF16)

    return _pc(body, name=name, out_shape=[_sds((B * S, P * LANES), BF16), _sds((B, 2 * P, S, 1), F32)],
               grid=(B, P, nq),
               in_specs=[_col_spec(tq, nq, q_cb), _kv_spec(S, k_cb), _kv_spec(S, v_cb), _stat_row_spec(S)],
               out_specs=[_col_spec(tq, nq, 0), _stat_col_spec(tq)],
               scratch_shapes=[pltpu.VMEM((2, 2, tq, tk), F32), pltpu.VMEM((2, tq, LANES), F32),
                               pltpu.VMEM((2, tq, 1), F32), pltpu.VMEM((2, tq, 1), F32)],
               semantics=("arbitrary", "arbitrary", "arbitrary"), vmem=VMEM_BIG)(qa, ka, va, cr)


def _fox_bwd(qa, ka, va, doa, lse, cr, *, name, B, S, P, q_cb, k_cb, v_cb, do_cb):
    tq = tk = min(ATT_TILE, S)
    nq = S // tq

    def body(q_ref, k_ref, v_ref, do_ref, lse_ref, cr_ref, dq_ref, dk_ref, dv_ref, dcs_ref):
        i = pl.program_id(2)

        @pl.when(i == 0)
        def _():
            dk_ref[...] = jnp.zeros_like(dk_ref)
            dv_ref[...] = jnp.zeros_like(dv_ref)
            dcs_ref[...] = jnp.zeros_like(dcs_ref)

        lane, masks, qh, on_or_below = _pair_setup(q_ref, tq, tk)
        do = do_ref[...]
        doh = [jnp.where(mk, do, jnp.zeros_like(do)) for mk in masks]
        lse_h = [lse_ref[0, hh] for hh in range(2)]

        def probs(kb, hh, diag):
            ks = pl.multiple_of(kb * tk, tk)
            kblk = k_ref[pl.ds(ks, tk), :]
            vblk = v_ref[pl.ds(ks, tk), :]
            s = lax.dot_general(qh[hh], kblk, NT, preferred_element_type=F32) - cr_ref[0, hh, :, pl.ds(ks, tk)]
            if diag:
                s = jnp.where(on_or_below, s, NEG)
            p = jnp.exp(s - lse_h[hh])
            dp = lax.dot_general(doh[hh], vblk, NT, preferred_element_type=F32)
            return ks, kblk, p, dp

        def delta_block(kb, carry, diag):
            out = []
            for hh in range(2):
                _, _, p, dp = probs(kb, hh, diag)
                out.append(carry[hh] + jnp.sum(p * dp, axis=-1, keepdims=True))
            return tuple(out)

        zc = jnp.zeros((tq, 1), F32)
        delta = lax.fori_loop(0, i, lambda kb, c: delta_block(kb, c, False), (zc, zc))
        delta = delta_block(i, delta, True)

        def grad_block(kb, carry, diag):
            out = []
            for hh in range(2):
                ks, kblk, p, dp = probs(kb, hh, diag)
                ds = p * (dp - delta[hh])
                dsb = ds.astype(BF16)
                rows = pl.ds(ks, tk)
                dk_ref[rows, :] += lax.dot_general(dsb, qh[hh], TN, preferred_element_type=F32)
                dv_ref[rows, :] += lax.dot_general(p.astype(BF16), doh[hh], TN, preferred_element_type=F32)
                dcs_ref[0, hh, :, rows] -= jnp.sum(ds, axis=0, keepdims=True)
                out.append(carry[hh] + jnp.dot(dsb, kblk, preferred_element_type=F32))
            return tuple(out)

        za = jnp.zeros((tq, LANES), F32)
        dq = lax.fori_loop(0, i, lambda kb, c: grad_block(kb, c, False), (za, za))
        dq = grad_block(i, dq, True)
        dq_ref[...] = (jnp.where(lane < HEAD_DIM, dq[0], dq[1]) * SCALE).astype(BF16)

    return _pc(body, name=name,
               out_shape=[_sds((B * S, P * LANES), BF16), _sds((B * S, P * LANES), F32), _sds((B * S, P * LANES), F32),
                          _sds((B, 2 * P, 1, S), F32)],
               grid=(B, P, nq),
               in_specs=[_col_spec(tq, nq, q_cb), _kv_spec(S, k_cb), _kv_spec(S, v_cb), _col_spec(tq, nq, do_cb),
                         _stat_col_spec(tq), _stat_row_spec(S)],
               out_specs=[_col_spec(tq, nq, 0), _kv_spec(S, 0), _kv_spec(S, 0), _stat_row_spec(S)],
               semantics=("arbitrary", "arbitrary", "arbitrary"), vmem=VMEM_BIG)(qa, ka, va, doa, lse, cr)


def _sb_logs(qh, kblk):
    z = lax.dot_general(qh, kblk, NT, preferred_element_type=F32)
    nz = -z
    lg = jnp.log(1.0 + jnp.exp(jnp.minimum(z, nz)))
    lm = jnp.minimum(nz, 0.0) - lg
    return lm + z, lm


def _sb_fwd(qa, ka, va, *, name, B, S, P, q_cb, k_cb, v_cb):
    tq = tk = min(ATT_TILE, S)
    nq = S // tq

    def body(q_ref, k_ref, v_ref, o_ref, rt_ref):
        i = pl.program_id(2)
        lane, _, qh, on_or_below = _pair_setup(q_ref, tq, tk)
        t_r = lax.broadcasted_iota(jnp.int32, (tk, tk), 0)
        t_c = lax.broadcasted_iota(jnp.int32, (tk, tk), 1)
        after = (t_r > t_c).astype(BF16)
        below = t_c < t_r

        def block(kb, carry, diag):
            ks = pl.multiple_of(kb * tk, tk)
            kblk = k_ref[pl.ds(ks, tk), :]
            vblk = v_ref[pl.ds(ks, tk), :]
            out = []
            for hh in range(2):
                run, acc = carry[hh]
                ls, lm = _sb_logs(qh[hh], kblk)
                if diag:
                    lm = jnp.where(below, lm, 0.0)
                a = jnp.exp(ls + run + _dot_tri2(lm, after))
                if diag:
                    a = jnp.where(below, a, 0.0)
                acc = acc + jnp.dot(a.astype(BF16), vblk, preferred_element_type=F32)
                out.append((run + jnp.sum(lm, axis=-1, keepdims=True), acc))
            return tuple(out)

        one = (jnp.zeros((tq, 1), F32), jnp.zeros((tq, LANES), F32))
        carry = block(i, (one, one), True)
        (r0, a0), (r1, a1) = lax.fori_loop(0, i, lambda jj, c: block(i - 1 - jj, c, False), carry)
        rt_ref[0, 0] = r0
        rt_ref[0, 1] = r1
        o_ref[...] = jnp.where(lane < HEAD_DIM, a0, a1).astype(BF16)

    return _pc(body, name=name, out_shape=[_sds((B * S, P * LANES), BF16), _sds((B, 2 * P, S, 1), F32)],
               grid=(B, P, nq), in_specs=[_col_spec(tq, nq, q_cb), _kv_spec(S, k_cb), _kv_spec(S, v_cb)],
               out_specs=[_col_spec(tq, nq, 0), _stat_col_spec(tq)],
               semantics=("arbitrary", "arbitrary", "arbitrary"), vmem=VMEM_BIG)(qa, ka, va)


def _sb_bwd(qa, ka, va, doa, rt, *, name, B, S, P, q_cb, k_cb, v_cb, do_cb):
    tq = tk = min(ATT_TILE, S)
    nq = S // tq

    def body(q_ref, k_ref, v_ref, do_ref, rt_ref, dq_ref, dk_ref, dv_ref):
        i = pl.program_id(2)

        @pl.when(i == 0)
        def _():
            dk_ref[...] = jnp.zeros_like(dk_ref)
            dv_ref[...] = jnp.zeros_like(dv_ref)

        lane, masks, qh, _ = _pair_setup(q_ref, tq, tk)
        do = do_ref[...]
        doh = [jnp.where(mk, do, jnp.zeros_like(do)) for mk in masks]
        rt_h = [rt_ref[0, hh] for hh in range(2)]
        t_r = lax.broadcasted_iota(jnp.int32, (tk, tk), 0)
        t_c = lax.broadcasted_iota(jnp.int32, (tk, tk), 1)
        upto = (t_r <= t_c).astype(BF16)
        before = (t_r < t_c).astype(BF16)
        below = t_c < t_r

        def block(kb, carry, diag):
            ks = pl.multiple_of(kb * tk, tk)
            rows = pl.ds(ks, tk)
            kblk = k_ref[rows, :]
            vblk = v_ref[rows, :]
            out = []
            for hh in range(2):
                pl_sum, pg_sum, dq_acc = carry[hh]
                ls, lm = _sb_logs(qh[hh], kblk)
                if diag:
                    lm = jnp.where(below, lm, 0.0)
                a = jnp.exp(ls + (rt_h[hh] - pl_sum) - _dot_tri2(lm, upto))
                if diag:
                    a = jnp.where(below, a, 0.0)
                gm = a * lax.dot_general(doh[hh], vblk, NT, preferred_element_type=F32)
                pg = _dot_tri2(gm, before) + pg_sum
                dz = gm - jnp.exp(ls) * (gm + pg)
                if diag:
                    dz = jnp.where(below, dz, 0.0)
                dzb = dz.astype(BF16)
                dk_ref[rows, :] += lax.dot_general(dzb, qh[hh], TN, preferred_element_type=F32)
                dv_ref[rows, :] += lax.dot_general(a.astype(BF16), doh[hh], TN, preferred_element_type=F32)
                out.append((pl_sum + jnp.sum(lm, axis=-1, keepdims=True),
                            pg_sum + jnp.sum(gm, axis=-1, keepdims=True),
                            dq_acc + jnp.dot(dzb, kblk, preferred_element_type=F32)))
            return tuple(out)

        zc = jnp.zeros((tq, 1), F32)
        one = (zc, zc, jnp.zeros((tq, LANES), F32))
        carry = lax.fori_loop(0, i, lambda kb, c: block(kb, c, False), (one, one))
        (_, _, dq0), (_, _, dq1) = block(i, carry, True)
        dq_ref[...] = (jnp.where(lane < HEAD_DIM, dq0, dq1) * SCALE).astype(BF16)

    return _pc(body, name=name,
               out_shape=[_sds((B * S, P * LANES), BF16), _sds((B * S, P * LANES), F32), _sds((B * S, P * LANES), F32)],
               grid=(B, P, nq),
               in_specs=[_col_spec(tq, nq, q_cb), _kv_spec(S, k_cb), _kv_spec(S, v_cb), _col_spec(tq, nq, do_cb),
                         _stat_col_spec(tq)],
               out_specs=[_col_spec(tq, nq, 0), _kv_spec(S, 0), _kv_spec(S, 0)],
               semantics=("arbitrary", "arbitrary", "arbitrary"), vmem=VMEM_BIG)(qa, ka, va, doa, rt)


HEAD_GROUP = 3


def _g_col_spec(rows, nblk_rows, cb, G):
    return pl.BlockSpec((rows, G * LANES), lambda b, p, i: (b * nblk_rows + i, cb // G + p))


def _g_kv_spec(rows, cb, G):
    return pl.BlockSpec((rows, G * LANES), lambda b, p, i: (b, cb // G + p))


def _g_stat_col_spec(tq, G):
    return pl.BlockSpec((1, 2 * G, tq, 1), lambda b, p, i: (b, p, i, 0))


def _g_stat_row_spec(S, G):
    return pl.BlockSpec((1, 2 * G, 1, S), lambda b, p, i: (b, p, 0, 0))


def _lanes(g):
    return slice(g * LANES, (g + 1) * LANES)


def _streams(x_ref, G, scale=None):
    rows = x_ref.shape[0]
    lane = lax.broadcasted_iota(jnp.int32, (rows, LANES), 1)
    out = []
    for g in range(G):
        x = x_ref[:, _lanes(g)]
        if scale is not None:
            x = x * jnp.asarray(scale, x.dtype)
        for hh in range(2):
            out.append(jnp.where(_head_mask(lane, hh), x, jnp.zeros_like(x)))
    return lane, out


def _wide(stat, width):
    return jnp.tile(stat, (1, width // LANES))


def _fold_lanes(v):
    out = v[:, :LANES]
    for j in range(1, v.shape[1] // LANES):
        out = out + v[:, j * LANES:(j + 1) * LANES]
    return out


def _kv_blocks(ref, ks, tk, G):
    return [ref[pl.ds(ks, tk), _lanes(g)] for g in range(G)]


def _sweep(i, block):
    def step(kb, c):
        block(kb, False)
        return c
    lax.fori_loop(0, i, step, 0)
    block(i, True)


def _fox_fwd_g(qa, ka, va, cr, *, name, B, S, P, q_cb, k_cb, v_cb, G=HEAD_GROUP):
    tq = tk = min(ATT_TILE, S)
    nq = S // tq
    NS = 2 * G

    def body(q_ref, k_ref, v_ref, cr_ref, o_ref, lse_ref, acc_ref, m_ref, l_ref):
        i = pl.program_id(2)
        lane, qh = _streams(q_ref, G, SCALE)
        on_or_below = (lax.broadcasted_iota(jnp.int32, (tq, tk), 1) <= lax.broadcasted_iota(jnp.int32, (tq, tk), 0))
        m_ref[...] = jnp.full(m_ref.shape, NEG, F32)
        l_ref[...] = jnp.zeros(l_ref.shape, F32)
        acc_ref[...] = jnp.zeros(acc_ref.shape, F32)

        def block(kb, diag):
            ks = pl.multiple_of(kb * tk, tk)
            kblk = _kv_blocks(k_ref, ks, tk, G)
            vblk = _kv_blocks(v_ref, ks, tk, G)
            ss = [lax.dot_general(qh[st], kblk[st // 2], NT, preferred_element_type=F32) for st in range(NS)]
            ps = []
            for st in range(NS):
                s = ss[st] - cr_ref[0, st, :, pl.ds(ks, tk)]
                if diag:
                    s = jnp.where(on_or_below, s, NEG)
                m = m_ref[st]
                m_new = jnp.maximum(m, jnp.max(s, axis=-1, keepdims=True))
                alpha = jnp.exp(m - m_new)
                p = jnp.exp(s - _wide(m_new, tk))
                m_ref[st] = m_new
                l_ref[st] = alpha * l_ref[st] + _fold_lanes(p)
                ps.append((alpha, p.astype(BF16)))
            pvs = [jnp.dot(ps[st][1], vblk[st // 2], preferred_element_type=F32) for st in range(NS)]
            for st in range(NS):
                acc_ref[st] = ps[st][0] * acc_ref[st] + pvs[st]

        _sweep(i, block)
        ls = [jnp.sum(l_ref[st], axis=-1, keepdims=True) for st in range(NS)]
        for st in range(NS):
            lse_ref[0, st] = jnp.max(m_ref[st], axis=-1, keepdims=True) + jnp.log(ls[st])
        for g in range(G):
            o_ref[:, _lanes(g)] = jnp.where(lane < HEAD_DIM, acc_ref[2 * g] / ls[2 * g],
                                            acc_ref[2 * g + 1] / ls[2 * g + 1]).astype(BF16)

    return _pc(body, name=name, out_shape=[_sds((B * S, P * LANES), BF16), _sds((B, 2 * P, S, 1), F32)],
               grid=(B, P // G, nq),
               in_specs=[_g_col_spec(tq, nq, q_cb, G), _g_kv_spec(S, k_cb, G), _g_kv_spec(S, v_cb, G),
                         _g_stat_row_spec(S, G)],
               out_specs=[_g_col_spec(tq, nq, 0, G), _g_stat_col_spec(tq, G)],
               scratch_shapes=[pltpu.VMEM((NS, tq, LANES), F32)] * 3,
               semantics=("arbitrary", "arbitrary", "arbitrary"), vmem=VMEM_BIG)(qa, ka, va, cr)


def _fox_bwd_g(qa, ka, va, doa, lse, cr, *, name, B, S, P, q_cb, k_cb, v_cb, do_cb, G=HEAD_GROUP):
    tq = tk = min(ATT_TILE, S)
    nq = S // tq
    NS = 2 * G

    def body(q_ref, k_ref, v_ref, do_ref, lse_ref, cr_ref, dq_ref, dk_ref, dv_ref, dcs_ref, dqa_ref, delta_ref, lse_s,
             p_buf, dp_buf):
        i = pl.program_id(2)

        @pl.when(i == 0)
        def _():
            dk_ref[...] = jnp.zeros_like(dk_ref)
            dv_ref[...] = jnp.zeros_like(dv_ref)
            dcs_ref[...] = jnp.zeros_like(dcs_ref)

        lane, qh = _streams(q_ref, G, SCALE)
        _, doh = _streams(do_ref, G)
        on_or_below = (lax.broadcasted_iota(jnp.int32, (tq, tk), 1) <= lax.broadcasted_iota(jnp.int32, (tq, tk), 0))
        delta_ref[...] = jnp.zeros(delta_ref.shape, F32)
        dqa_ref[...] = jnp.zeros(dqa_ref.shape, F32)
        for st in range(NS):
            lse_s[st] = jnp.broadcast_to(lse_ref[0, st], (tq, LANES))

        def probs(kb, diag):
            ks = pl.multiple_of(kb * tk, tk)
            kblk = _kv_blocks(k_ref, ks, tk, G)
            vblk = _kv_blocks(v_ref, ks, tk, G)
            ss = [lax.dot_general(qh[st], kblk[st // 2], NT, preferred_element_type=F32) for st in range(NS)]
            dps = [lax.dot_general(doh[st], vblk[st // 2], NT, preferred_element_type=F32) for st in range(NS)]
            ps = []
            for st in range(NS):
                s = ss[st] - cr_ref[0, st, :, pl.ds(ks, tk)]
                if diag:
                    s = jnp.where(on_or_below, s, NEG)
                ps.append(jnp.exp(s - _wide(lse_s[st], tk)))
            return ks, kblk, ps, dps

        def delta_block(kb, diag):
            _, _, ps, dps = probs(kb, diag)
            for st in range(NS):
                delta_ref[st] += _fold_lanes(ps[st] * dps[st])
                p_buf[st, kb] = ps[st]
                dp_buf[st, kb] = dps[st]

        _sweep(i, delta_block)
        for st in range(NS):
            delta_ref[st] = jnp.broadcast_to(jnp.sum(delta_ref[st], axis=-1, keepdims=True), (tq, LANES))

        def grad_block(kb, diag):
            ks = pl.multiple_of(kb * tk, tk)
            kblk = _kv_blocks(k_ref, ks, tk, G)
            rows = pl.ds(ks, tk)
            dsb, pb = [], []
            for st in range(NS):
                p = p_buf[st, kb]
                ds = p * (dp_buf[st, kb] - _wide(delta_ref[st], tk))
                dcs_ref[0, st, :, rows] -= jnp.sum(ds, axis=0, keepdims=True)
                dsb.append(ds.astype(BF16))
                pb.append(p.astype(BF16))
            dks = [lax.dot_general(dsb[st], qh[st], TN, preferred_element_type=F32) for st in range(NS)]
            dvs = [lax.dot_general(pb[st], doh[st], TN, preferred_element_type=F32) for st in range(NS)]
            dqs = [jnp.dot(dsb[st], kblk[st // 2], preferred_element_type=F32) for st in range(NS)]
            for g in range(G):
                dk_ref[rows, _lanes(g)] += dks[2 * g] + dks[2 * g + 1]
                dv_ref[rows, _lanes(g)] += dvs[2 * g] + dvs[2 * g + 1]
            for st in range(NS):
                dqa_ref[st] += dqs[st]

        _sweep(i, grad_block)
        for g in range(G):
            dq_ref[:, _lanes(g)] = (jnp.where(lane < HEAD_DIM, dqa_ref[2 * g], dqa_ref[2 * g + 1]) * SCALE).astype(BF16)

    return _pc(body, name=name,
               out_shape=[_sds((B * S, P * LANES), BF16), _sds((B * S, P * LANES), F32), _sds((B * S, P * LANES), F32),
                          _sds((B, 2 * P, 1, S), F32)],
               grid=(B, P // G, nq),
               in_specs=[_g_col_spec(tq, nq, q_cb, G), _g_kv_spec(S, k_cb, G), _g_kv_spec(S, v_cb, G),
                         _g_col_spec(tq, nq, do_cb, G), _g_stat_col_spec(tq, G), _g_stat_row_spec(S, G)],
               out_specs=[_g_col_spec(tq, nq, 0, G), _g_kv_spec(S, 0, G), _g_kv_spec(S, 0, G), _g_stat_row_spec(S, G)],
               scratch_shapes=[pltpu.VMEM((NS, tq, LANES), F32)] * 3 + [pltpu.VMEM((NS, nq, tq, tk), F32)] * 2,
               semantics=("arbitrary", "arbitrary", "arbitrary"), vmem=VMEM_BIG)(qa, ka, va, doa, lse, cr)


def _sb_logs_z(z):
    nz = -z
    lm = jnp.minimum(nz, 0.0) - jnp.log(1.0 + jnp.exp(jnp.minimum(z, nz)))
    return lm + z, lm


def _sb_fwd_g(qa, ka, va, *, name, B, S, P, q_cb, k_cb, v_cb, G=HEAD_GROUP):
    tq = tk = min(ATT_TILE, S)
    nq = S // tq
    NS = 2 * G

    def body(q_ref, k_ref, v_ref, o_ref, rt_ref, acc_ref, run_ref):
        i = pl.program_id(2)
        lane, qh = _streams(q_ref, G, SCALE)
        t_r = lax.broadcasted_iota(jnp.int32, (tk, tk), 0)
        t_c = lax.broadcasted_iota(jnp.int32, (tk, tk), 1)
        after = (t_r > t_c).astype(BF16)
        below = t_c < t_r
        acc_ref[...] = jnp.zeros(acc_ref.shape, F32)
        run_ref[...] = jnp.zeros(run_ref.shape, F32)

        def block(kb, diag):
            ks = pl.multiple_of(kb * tk, tk)
            kblk = _kv_blocks(k_ref, ks, tk, G)
            vblk = _kv_blocks(v_ref, ks, tk, G)
            zs = [lax.dot_general(qh[st], kblk[st // 2], NT, preferred_element_type=F32) for st in range(NS)]
            lss, parts = [], []
            for st in range(NS):
                ls, lm = _sb_logs_z(zs[st])
                if diag:
                    lm = jnp.where(below, lm, 0.0)
                lss.append(ls + _wide(run_ref[st], tk))
                run_ref[st] += jnp.sum(lm, axis=-1, keepdims=True)
                parts.append(_split2(lm))
            sufs = [jnp.dot(parts[st][0], after, preferred_element_type=F32)
                    + jnp.dot(parts[st][1], after, preferred_element_type=F32) for st in range(NS)]
            ab = []
            for st in range(NS):
                a = jnp.exp(lss[st] + sufs[st])
                if diag:
                    a = jnp.where(below, a, 0.0)
                ab.append(a.astype(BF16))
            pvs = [jnp.dot(ab[st], vblk[st // 2], preferred_element_type=F32) for st in range(NS)]
            for st in range(NS):
                acc_ref[st] += pvs[st]

        block(i, True)

        def step(jj, c):
            block(i - 1 - jj, False)
            return c

        lax.fori_loop(0, i, step, 0)
        for st in range(NS):
            rt_ref[0, st] = jnp.max(run_ref[st], axis=-1, keepdims=True)
        for g in range(G):
            o_ref[:, _lanes(g)] = jnp.where(lane < HEAD_DIM, acc_ref[2 * g], acc_ref[2 * g + 1]).astype(BF16)

    return _pc(body, name=name, out_shape=[_sds((B * S, P * LANES), BF16), _sds((B, 2 * P, S, 1), F32)],
               grid=(B, P // G, nq),
               in_specs=[_g_col_spec(tq, nq, q_cb, G), _g_kv_spec(S, k_cb, G), _g_kv_spec(S, v_cb, G)],
               out_specs=[_g_col_spec(tq, nq, 0, G), _g_stat_col_spec(tq, G)],
               scratch_shapes=[pltpu.VMEM((NS, tq, LANES), F32)] * 2,
               semantics=("arbitrary", "arbitrary", "arbitrary"), vmem=VMEM_BIG)(qa, ka, va)


def _sb_bwd_g(qa, ka, va, doa, rt, *, name, B, S, P, q_cb, k_cb, v_cb, do_cb, G=HEAD_GROUP):
    tq = tk = min(ATT_TILE, S)
    nq = S // tq
    NS = 2 * G

    def body(q_ref, k_ref, v_ref, do_ref, rt_ref, dq_ref, dk_ref, dv_ref, dqa_ref, pl_ref, pg_ref):
        i = pl.program_id(2)

        @pl.when(i == 0)
        def _():
            dk_ref[...] = jnp.zeros_like(dk_ref)
            dv_ref[...] = jnp.zeros_like(dv_ref)

        lane, qh = _streams(q_ref, G, SCALE)
        _, doh = _streams(do_ref, G)
        t_r = lax.broadcasted_iota(jnp.int32, (tk, tk), 0)
        t_c = lax.broadcasted_iota(jnp.int32, (tk, tk), 1)
        upto = (t_r <= t_c).astype(BF16)
        before = (t_r < t_c).astype(BF16)
        below = t_c < t_r
        dqa_ref[...] = jnp.zeros(dqa_ref.shape, F32)
        pg_ref[...] = jnp.zeros(pg_ref.shape, F32)
        for st in range(NS):
            pl_ref[st] = jnp.broadcast_to(rt_ref[0, st], (tq, LANES))

        def block(kb, diag):
            ks = pl.multiple_of(kb * tk, tk)
            rows = pl.ds(ks, tk)
            kblk = _kv_blocks(k_ref, ks, tk, G)
            vblk = _kv_blocks(v_ref, ks, tk, G)
            zs = [lax.dot_general(qh[st], kblk[st // 2], NT, preferred_element_type=F32) for st in range(NS)]
            das = [lax.dot_general(doh[st], vblk[st // 2], NT, preferred_element_type=F32) for st in range(NS)]
            lss, parts = [], []
            for st in range(NS):
                ls, lm = _sb_logs_z(zs[st])
                if diag:
                    lm = jnp.where(below, lm, 0.0)
                lss.append((ls, ls + _wide(pl_ref[st], tk)))
                pl_ref[st] -= jnp.sum(lm, axis=-1, keepdims=True)
                parts.append(_split2(lm))
            pins = [jnp.dot(parts[st][0], upto, preferred_element_type=F32)
                    + jnp.dot(parts[st][1], upto, preferred_element_type=F32) for st in range(NS)]
            gms, ab, gparts = [], [], []
            for st in range(NS):
                a = jnp.exp(lss[st][1] - pins[st])
                if diag:
                    a = jnp.where(below, a, 0.0)
                gm = a * das[st]
                gms.append(gm)
                ab.append(a.astype(BF16))
                gparts.append(gm.astype(BF16))
            pgs = [jnp.dot(gparts[st], before, preferred_element_type=F32) for st in range(NS)]
            dzb = []
            for st in range(NS):
                gm = gms[st]
                dz = gm - jnp.exp(lss[st][0]) * (gm + (pgs[st] + _wide(pg_ref[st], tk)))
                if diag:
                    dz = jnp.where(below, dz, 0.0)
                pg_ref[st] += jnp.sum(gm, axis=-1, keepdims=True)
                dzb.append(dz.astype(BF16))
            dks = [lax.dot_general(dzb[st], qh[st], TN, preferred_element_type=F32) for st in range(NS)]
            dvs = [lax.dot_general(ab[st], doh[st], TN, preferred_element_type=F32) for st in range(NS)]
            dqs = [jnp.dot(dzb[st], kblk[st // 2], preferred_element_type=F32) for st in range(NS)]
            for g in range(G):
                dk_ref[rows, _lanes(g)] += dks[2 * g] + dks[2 * g + 1]
                dv_ref[rows, _lanes(g)] += dvs[2 * g] + dvs[2 * g + 1]
            for st in range(NS):
                dqa_ref[st] += dqs[st]

        _sweep(i, block)
        for g in range(G):
            dq_ref[:, _lanes(g)] = (jnp.where(lane < HEAD_DIM, dqa_ref[2 * g], dqa_ref[2 * g + 1]) * SCALE).astype(BF16)

    return _pc(body, name=name,
               out_shape=[_sds((B * S, P * LANES), BF16), _sds((B * S, P * LANES), F32), _sds((B * S, P * LANES), F32)],
               grid=(B, P // G, nq),
               in_specs=[_g_col_spec(tq, nq, q_cb, G), _g_kv_spec(S, k_cb, G), _g_kv_spec(S, v_cb, G),
                         _g_col_spec(tq, nq, do_cb, G), _g_stat_col_spec(tq, G)],
               out_specs=[_g_col_spec(tq, nq, 0, G), _g_kv_spec(S, 0, G), _g_kv_spec(S, 0, G)],
               scratch_shapes=[pltpu.VMEM((NS, tq, LANES), F32)] * 3,
               semantics=("arbitrary", "arbitrary", "arbitrary"), vmem=VMEM_BIG)(qa, ka, va, doa, rt)


def _shift_rows(cur, halo_ref, first, rows_idx, k):
    out = pltpu.roll(cur, k, 0)
    top = out[0:8, :]
    for r in range(k):
        hr = halo_ref.shape[0] - k + r
        edge = jnp.where(first, 0.0, halo_ref[hr:hr + 1, :])
        top = jnp.where(rows_idx[0:8, :] == r, edge, top)
    return jnp.concatenate([top, out[8:, :]], axis=0)


def _shift_rows_up(cur, halo_ref, last, rows_idx, k, ts):
    out = pltpu.roll(cur, ts - k, 0)
    bottom = out[ts - 8:, :]
    for r in range(k):
        edge = jnp.where(last, 0.0, halo_ref[r:r + 1, :])
        bottom = jnp.where(rows_idx[0:8, :] == 8 - k + r, edge, bottom)
    return jnp.concatenate([out[:ts - 8, :], bottom], axis=0)


def _conv_taps(main_ref, halo_ref, w_ref, b_ref, first, rows_idx):
    cur = main_ref[...]
    m1 = _shift_rows(cur, halo_ref, first, rows_idx, 1)
    m2 = _shift_rows(cur, halo_ref, first, rows_idx, 2)
    uc = b_ref[...] + w_ref[0:1, :] * m2 + w_ref[1:2, :] * m1 + w_ref[2:3, :] * cur
    return uc, cur, m1, m2


def _conv_specs(ts, tf, ns, nf, S, order):
    def wrap(fn):
        return lambda *g: fn(*order(*g))
    specs = []
    for half in (0, 1):
        specs.append(pl.BlockSpec((None, ts, tf), wrap(lambda b, i, j, half=half: (half, b * ns + i, j))))
        specs.append(pl.BlockSpec((None, 8, tf), wrap(
            lambda b, i, j, half=half: (half, jnp.maximum((b * S + i * ts) // 8 - 1, 0), j))))
    for off in (0, nf):
        specs.append(pl.BlockSpec((3, tf), wrap(lambda b, i, j, off=off: (0, j + off))))
    for off in (0, nf):
        specs.append(pl.BlockSpec((1, tf), wrap(lambda b, i, j, off=off: (0, j + off))))
    return specs


def _ffn_up_gate(x, g, w, cw, cb, *, name, S):
    T, D = x.shape
    F = w.shape[1] // 2
    tm = min(1024, S)
    tn = 256
    nj = F // tn
    tiles_per_seq = S // tm
    halo = 16

    def body(x_ref, xh_ref, g_ref, wg_ref, wv_ref, cwg_ref, cwv_ref, cbg_ref, cbv_ref,
             u_ref, a_ref, hout_ref, h_ref, hh_ref, eg_ref, ev_ref):
        first = lax.rem(pl.program_id(0), tiles_per_seq) == 0

        @pl.when(pl.program_id(1) == 0)
        def _():
            def norm(v):
                r = lax.rsqrt(jnp.mean(v * v, axis=-1, keepdims=True) + EPS)
                return ((v * r) * g_ref[...]).astype(BF16)
            h = norm(x_ref[...])
            h_ref[...] = h
            hout_ref[...] = h
            hh_ref[...] = norm(xh_ref[...])

        h = h_ref[...]
        rows_idx = lax.broadcasted_iota(jnp.int32, (tm, tn), 0)
        uc = []
        for half, (w_ref, cw_ref, cb_ref, e_ref) in enumerate(((wg_ref, cwg_ref, cbg_ref, eg_ref),
                                                               (wv_ref, cwv_ref, cbv_ref, ev_ref))):
            acc = jnp.dot(h, w_ref[...], preferred_element_type=F32)
            e_ref[...] = jnp.dot(hh_ref[...], w_ref[...], preferred_element_type=F32)
            u_ref[half] = acc
            m1 = _shift_rows(acc, e_ref, first, rows_idx, 1)
            m2 = _shift_rows(acc, e_ref, first, rows_idx, 2)
            uc.append(cb_ref[...] + cw_ref[0:1, :] * m2 + cw_ref[1:2, :] * m1 + cw_ref[2:3, :] * acc)
        a_ref[...] = (uc[0] * (1.0 / (1.0 + jnp.exp(-uc[0]))) * uc[1]).astype(BF16)

    in_specs = [pl.BlockSpec((tm, D), lambda i, j: (i, 0)),
                pl.BlockSpec((halo, D), lambda i, j: (jnp.maximum(i * (tm // halo) - 1, 0), 0)),
                pl.BlockSpec((1, D), lambda i, j: (0, 0)),
                pl.BlockSpec((D, tn), lambda i, j: (0, j)), pl.BlockSpec((D, tn), lambda i, j: (0, j + nj)),
                pl.BlockSpec((3, tn), lambda i, j: (0, j)), pl.BlockSpec((3, tn), lambda i, j: (0, j + nj)),
                pl.BlockSpec((1, tn), lambda i, j: (0, j)), pl.BlockSpec((1, tn), lambda i, j: (0, j + nj))]
    return _pc(body, name=name,
               out_shape=[_sds((2, T, F), F32), _sds((T, F), BF16), _sds((T, D), BF16)],
               grid=(T // tm, nj), in_specs=in_specs,
               out_specs=[pl.BlockSpec((2, tm, tn), lambda i, j: (0, i, j)), pl.BlockSpec((tm, tn), lambda i, j: (i, j)),
                          pl.BlockSpec((tm, D), lambda i, j: (i, 0))],
               scratch_shapes=[pltpu.VMEM((tm, D), BF16), pltpu.VMEM((halo, D), BF16),
                               pltpu.VMEM((halo, tn), F32), pltpu.VMEM((halo, tn), F32)],
               semantics=("arbitrary", "arbitrary"), vmem=VMEM_BIG)(x, x, g.reshape(1, D), w, w, cw, cw, cb, cb)


def _conv_gate_bwd(da, u, cw, cb, *, name, B, S):
    F = u.shape[2]
    tf = F // 2
    ts = min(256, S)
    ns, nf = S // ts, F // tf

    def body(da_ref, ug_ref, ugh_ref, uv_ref, uvh_ref, wg_ref, wv_ref, bg_ref, bv_ref,
             dug_ref, duv_ref, pg_ref, pv_ref, nxt_g, nxt_v):
        step = pl.program_id(2)
        first = step == ns - 1
        last = step == 0

        @pl.when(jnp.logical_and(pl.program_id(1) == 0, last))
        def _():
            pg_ref[...] = jnp.zeros_like(pg_ref)
            pv_ref[...] = jnp.zeros_like(pv_ref)

        rows_idx = lax.broadcasted_iota(jnp.int32, (ts, tf), 0)
        ucg, g0, g1, g2 = _conv_taps(ug_ref, ugh_ref, wg_ref, bg_ref, first, rows_idx)
        ucv, v0, v1, v2 = _conv_taps(uv_ref, uvh_ref, wv_ref, bv_ref, first, rows_idx)
        sg = 1.0 / (1.0 + jnp.exp(-ucg))
        dav = da_ref[...]
        d_v = dav * (ucg * sg)
        d_g = dav * ucv * (sg * (1.0 + ucg * (1.0 - sg)))
        for p_ref, d, taps in ((pg_ref, d_g, (g2, g1, g0)), (pv_ref, d_v, (v2, v1, v0))):
            for k in range(3):
                p_ref[k:k + 1, :] += jnp.sum(d * taps[k], axis=0, keepdims=True)
            p_ref[3:4, :] += jnp.sum(d, axis=0, keepdims=True)
        for o_ref, d, w_ref, nxt in ((dug_ref, d_g, wg_ref, nxt_g), (duv_ref, d_v, wv_ref, nxt_v)):
            p1 = _shift_rows_up(d, nxt, last, rows_idx, 1, ts)
            p2 = _shift_rows_up(d, nxt, last, rows_idx, 2, ts)
            o_ref[...] = (w_ref[2:3, :] * d + w_ref[1:2, :] * p1 + w_ref[0:1, :] * p2).astype(BF16)
            nxt[...] = d[0:8, :]

    def order(j, b, r):
        return b, ns - 1 - r, j

    row = pl.BlockSpec((ts, tf), lambda j, b, r: (b * ns + ns - 1 - r, j))
    specs = [row] + _conv_specs(ts, tf, ns, nf, S, order)
    par = pl.BlockSpec((8, tf), lambda j, b, r: (0, j))
    return _pc(body, name=name,
               out_shape=[_sds((B * S, F), BF16), _sds((B * S, F), BF16), _sds((8, F), F32), _sds((8, F), F32)],
               grid=(nf, B, ns), in_specs=specs, out_specs=[row, row, par, par],
               scratch_shapes=[pltpu.VMEM((8, tf), F32), pltpu.VMEM((8, tf), F32)],
               semantics=("arbitrary", "arbitrary", "arbitrary"), vmem=VMEM_BIG)(da, u, u, u, u, cw, cw, cb, cb)


def _conv_transpose(d, cw, *, name, B, S, col_off):
    F = d.shape[1]
    tf = F // 2
    ts = min(256, S)
    ns, nf = S // ts, F // tf
    nblk8 = B * S // 8

    def body(d_ref, dh_ref, w_ref, o_ref):
        last = pl.program_id(1) == ns - 1
        rows_idx = lax.broadcasted_iota(jnp.int32, (ts, tf), 0)
        cur = d_ref[...]
        p1 = _shift_rows_up(cur, dh_ref, last, rows_idx, 1, ts)
        p2 = _shift_rows_up(cur, dh_ref, last, rows_idx, 2, ts)
        o_ref[...] = (w_ref[2:3, :] * cur + w_ref[1:2, :] * p1 + w_ref[0:1, :] * p2).astype(BF16)

    return _pc(body, name=name, out_shape=_sds((B * S, F), BF16), grid=(B, ns, nf),
               in_specs=[pl.BlockSpec((ts, tf), lambda b, i, j: (b * ns + i, j)),
                         pl.BlockSpec((8, tf), lambda b, i, j: (jnp.minimum((b * S + (i + 1) * ts) // 8, nblk8 - 1), j)),
                         pl.BlockSpec((3, tf), lambda b, i, j: (0, j + col_off * nf))],
               out_specs=pl.BlockSpec((ts, tf), lambda b, i, j: (b * ns + i, j)),
               semantics=("arbitrary", "arbitrary", "arbitrary"))(d, d, cw)


def _adamw(w, g, m, v, *, name):
    rows, cols = w.shape
    tr = rows
    while tr * cols * 4 > 2 ** 20 and tr % 16 == 0:
        tr //= 2

    def body(w_ref, g_ref, m_ref, v_ref, d_ref, nm_ref, nv_ref):
        gv = g_ref[...]
        m_new = ADAM_B1 * m_ref[...] + (1.0 - ADAM_B1) * gv
        v_new = ADAM_B2 * v_ref[...] + (1.0 - ADAM_B2) * (gv * gv)
        m_hat = m_new / (1.0 - ADAM_B1 ** ADAM_STEP)
        v_hat = v_new / (1.0 - ADAM_B2 ** ADAM_STEP)
        d_ref[...] = -ADAM_LR * (m_hat / (jnp.sqrt(v_hat) + ADAM_EPS) + ADAM_WD * w_ref[...])
        nm_ref[...] = m_new
        nv_ref[...] = v_new

    blk = pl.BlockSpec((tr, cols), lambda i: (i, 0))
    return _pc(body, name=name, out_shape=[_sds((rows, cols), F32)] * 3, grid=(rows // tr,),
               in_specs=[blk] * 4, out_specs=[blk] * 3, semantics=("arbitrary",))(w, g, m, v)


def _my_pos():
    return lax.axis_index("x"), lax.axis_index("y"), lax.axis_index("c")


_HBM = pl.BlockSpec(memory_space=pltpu.HBM)
_SEM = pl.BlockSpec(memory_space=pltpu.SEMAPHORE)
_EFFECT = pltpu.SideEffectType.DATAFLOW_SIDE_EFFECTING


def _peers():
    x, y, c = _my_pos()
    out = []
    for k in range(1, N_DEV):
        px, py, pc = x ^ ((k >> 2) & 1), y ^ ((k >> 1) & 1), c ^ (k & 1)
        out.append(((px, py, pc), 4 * px + 2 * py + pc))
    return out


def _scatter_start(srcs, slot_of, *, name, order_after=None):
    n = len(srcs)
    lands = [lax.empty((N_DEV,) + slot_of(s, 0, shape_only=True), s.dtype) for s in srcs]
    extra = [] if order_after is None else [order_after]

    def body(*refs):
        src_refs, land_refs = refs[:n], refs[n:2 * n]
        send_sems, recv_sems = refs[2 * n + len(extra)], refs[2 * n + len(extra) + 1]
        token = refs[-1]
        x, y, c = _my_pos()
        me = 4 * x + 2 * y + c
        for a in range(n):
            for k, (peer, peer_idx) in enumerate(_peers()):
                pltpu.make_async_remote_copy(
                    src_ref=slot_of(src_refs[a], peer_idx), dst_ref=land_refs[a].at[me],
                    send_sem=send_sems.at[a * 7 + k], recv_sem=recv_sems.at[a * 7 + k],
                    device_id=peer, device_id_type=MESH).start()
        token[...] = jnp.zeros_like(token)

    hbm = lambda a: pltpu.HBM(a.shape, a.dtype)
    args = [pltpu.with_memory_space_constraint(a, pltpu.HBM) for a in list(srcs) + lands] + extra
    outs = pl.pallas_call(
        body, name=name,
        out_shape=(pltpu.SemaphoreType.DMA((7 * n,)), pltpu.SemaphoreType.DMA((7 * n,)),
                   *[hbm(a) for a in srcs], *[hbm(a) for a in lands], _sds((8, LANES), F32)),
        in_specs=[_HBM] * (2 * n) + [pl.BlockSpec(memory_space=pl.ANY)] * len(extra),
        out_specs=(_SEM, _SEM, *([_HBM] * (2 * n)), pl.BlockSpec(memory_space=pltpu.VMEM)),
        input_output_aliases={a: 2 + a for a in range(2 * n)},
        compiler_params=pltpu.CompilerParams(has_side_effects=_EFFECT))(*args)
    return outs[0], outs[1], list(outs[2:2 + n]), list(outs[2 + n:2 + 2 * n]), outs[-1]


def _scatter_wait(send_sems, recv_sems, srcs, lands, slot_of, after, *, name):
    n = len(srcs)

    def body(*refs):
        src_refs, land_refs = refs[:n], refs[n:2 * n]
        ssem, rsem = refs[2 * n], refs[2 * n + 1]
        x, y, c = _my_pos()
        me = 4 * x + 2 * y + c
        for a in range(n):
            for k, (peer, peer_idx) in enumerate(_peers()):
                cp = pltpu.make_async_remote_copy(
                    src_ref=slot_of(src_refs[a], peer_idx), dst_ref=land_refs[a].at[me],
                    send_sem=ssem.at[a * 7 + k], recv_sem=rsem.at[a * 7 + k],
                    device_id=peer, device_id_type=MESH)
                cp.wait_send()
                cp.wait_recv()

    hbm = lambda a: pltpu.HBM(a.shape, a.dtype)
    outs = pl.pallas_call(
        body, name=name, out_shape=tuple(hbm(a) for a in list(srcs) + list(lands)),
        in_specs=[_HBM] * (2 * n) + [_SEM, _SEM, pl.BlockSpec(memory_space=pl.ANY)],
        out_specs=tuple([_HBM] * (2 * n)), input_output_aliases={a: a for a in range(2 * n)},
        compiler_params=pltpu.CompilerParams(has_side_effects=_EFFECT))(*srcs, *lands, send_sems, recv_sems, after)
    return list(outs[:n]), list(outs[n:])


def _whole(a, peer_idx, shape_only=False):
    return a.shape if shape_only else a


def _slot(a, peer_idx, shape_only=False):
    return a.shape[1:] if shape_only else a.at[peer_idx]


def _all_gather(shard, *, name):
    rows = shard.shape[0]

    def body(x_ref, out_ref, send_sems, recv_sems, local_sem):
        x, y, c = _my_pos()
        me, sibling = (x, y, c), (x, y, 1 - c)
        chips = [(1 - x, y), (x, 1 - y), (1 - x, 1 - y)]

        def slot(px, py, pc):
            return out_ref.at[4 * px + 2 * py + pc]

        def copy(k, block, to, src=None):
            return pltpu.make_async_remote_copy(
                src_ref=slot(*block) if src is None else src, dst_ref=slot(*block),
                send_sem=send_sems.at[k], recv_sem=recv_sems.at[k], device_id=to, device_id_type=MESH)

        mine = pltpu.make_async_copy(x_ref, slot(*me), local_sem)
        mine.start()
        first = [copy(0, me, sibling, src=x_ref)]
        first += [copy(1 + j, me, (*chip, c), src=x_ref) for j, chip in enumerate(chips)]
        for cp in first:
            cp.start()
        passed = [copy(4 + j, (*chip, c), sibling) for j, chip in enumerate(chips)]
        for j, chip in enumerate(chips):
            copy(1 + j, (*chip, c), me).wait_recv()
            passed[j].start()
        copy(0, sibling, me).wait_recv()
        for j, chip in enumerate(chips):
            copy(4 + j, (*chip, 1 - c), me).wait_recv()
        for cp in first + passed:
            cp.wait_send()
        mine.wait()

    return _pc(body, name=name, out_shape=_sds((N_DEV, rows, LANES), shard.dtype),
               in_specs=[pl.BlockSpec(memory_space=pl.ANY)], out_specs=pl.BlockSpec(memory_space=pl.ANY),
               scratch_shapes=[pltpu.SemaphoreType.DMA((7,)), pltpu.SemaphoreType.DMA((7,)),
                               pltpu.SemaphoreType.DMA])(shard)


def _exchange(big, small, *, name):
    rs = small.shape[0]

    def body(big_ref, small_ref, bout_ref, sout_ref, send_sems, recv_sems, local_sems):
        x, y, c = _my_pos()
        me = 4 * x + 2 * y + c
        lb = pltpu.make_async_copy(big_ref.at[me], bout_ref.at[me], local_sems.at[0])
        ls = pltpu.make_async_copy(small_ref, sout_ref.at[me], local_sems.at[1])
        lb.start()
        ls.start()
        copies = []
        for k in range(1, N_DEV):
            px = x ^ ((k >> 2) & 1)
            py = y ^ ((k >> 1) & 1)
            pc = c ^ (k & 1)
            peer = 4 * px + 2 * py + pc
            copies.append(pltpu.make_async_remote_copy(
                src_ref=big_ref.at[peer], dst_ref=bout_ref.at[me], send_sem=send_sems.at[k - 1],
                recv_sem=recv_sems.at[k - 1], device_id=(px, py, pc), device_id_type=MESH))
            copies.append(pltpu.make_async_remote_copy(
                src_ref=small_ref, dst_ref=sout_ref.at[me], send_sem=send_sems.at[7 + k - 1],
                recv_sem=recv_sems.at[7 + k - 1], device_id=(px, py, pc), device_id_type=MESH))
        for cp in copies:
            cp.start()
        for cp in copies:
            cp.wait()
        lb.wait()
        ls.wait()

    return _pc(body, name=name,
               out_shape=[_sds(big.shape, big.dtype), _sds((N_DEV, rs, LANES), F32)],
               in_specs=[pl.BlockSpec(memory_space=pl.ANY), pl.BlockSpec(memory_space=pl.ANY)],
               out_specs=[pl.BlockSpec(memory_space=pl.ANY), pl.BlockSpec(memory_space=pl.ANY)],
               scratch_shapes=[pltpu.SemaphoreType.DMA((14,)), pltpu.SemaphoreType.DMA((14,)),
                               pltpu.SemaphoreType.DMA((2,))])(big, small)


def _all_gather_small(small, *, name):
    rs = small.shape[0]

    def body(small_ref, out_ref, send_sems, recv_sems, local_sem):
        x, y, c = _my_pos()
        me = 4 * x + 2 * y + c
        mine = pltpu.make_async_copy(small_ref, out_ref.at[me], local_sem)
        mine.start()
        copies = [pltpu.make_async_remote_copy(
            src_ref=small_ref, dst_ref=out_ref.at[me], send_sem=send_sems.at[k], recv_sem=recv_sems.at[k],
            device_id=peer, device_id_type=MESH) for k, (peer, _) in enumerate(_peers())]
        for cp in copies:
            cp.start()
        for cp in copies:
            cp.wait()
        mine.wait()

    return _pc(body, name=name, out_shape=_sds((N_DEV, rs, LANES), F32),
               in_specs=[pl.BlockSpec(memory_space=pl.ANY)], out_specs=pl.BlockSpec(memory_space=pl.ANY),
               scratch_shapes=[pltpu.SemaphoreType.DMA((7,)), pltpu.SemaphoreType.DMA((7,)),
                               pltpu.SemaphoreType.DMA])(small)


def _sum_slots(a, *, name, tr=None):
    rows, cols = a.shape[1], a.shape[2]
    if tr is None:
        tr = rows
        while N_DEV * tr * cols * a.dtype.itemsize > 3 * 2 ** 20 and tr % 32 == 0:
            tr //= 2

    def body(a_ref, o_ref):
        acc = a_ref[0].astype(F32)
        for j in range(1, N_DEV):
            acc = acc + a_ref[j].astype(F32)
        o_ref[...] = acc

    return _pc(body, name=name, out_shape=_sds((rows, cols), F32), grid=(rows // tr,),
               in_specs=[pl.BlockSpec((N_DEV, tr, cols), lambda i: (0, i, 0))],
               out_specs=pl.BlockSpec((tr, cols), lambda i: (i, 0)), semantics=("arbitrary",), vmem=VMEM_BIG)(a)


PACK_ROWS = 25600
SUM_TILE = 512


def _rows128(a):
    return a.reshape(-1, LANES)


def _to_slots(full, kind):
    if kind == "rows2":
        r, c = full.shape
        return full.reshape(N_DEV, r // N_DEV, c)
    if kind == "cols2":
        r, c = full.shape
        return full.reshape(r, N_DEV, c // N_DEV).transpose(1, 0, 2)
    if kind == "rows3":
        l, r, c = full.shape
        return full.reshape(l, N_DEV, r // N_DEV, c).transpose(1, 0, 2, 3)
    if kind == "cols3":
        l, r, c = full.shape
        return full.reshape(l, r, N_DEV, c // N_DEV).transpose(2, 0, 1, 3)
    raise ValueError(kind)


def _from_slots(slots, kind):
    if kind == "rows2":
        _, r, c = slots.shape
        return slots.reshape(N_DEV * r, c)
    if kind == "cols2":
        _, r, c = slots.shape
        return slots.transpose(1, 0, 2).reshape(r, N_DEV * c)
    if kind == "rows3":
        _, l, r, c = slots.shape
        return slots.transpose(1, 0, 2, 3).reshape(l, N_DEV * r, c)
    if kind == "cols3":
        _, l, r, c = slots.shape
        return slots.transpose(1, 2, 0, 3).reshape(l, r, N_DEV * c)
    raise ValueError(kind)


BIG = (("w_in_a", "rows2"), ("w_in_b", "rows2"), ("w_kv", "cols2"), ("w_memkv", "rows3"),
       ("w_out", "rows3"), ("w_up", "cols3"), ("w_down", "rows3"))


def _round_up(n, m):
    return -(-n // m) * m


def _pad_rows(a, rows, axis):
    pad = [(0, 0)] * a.ndim
    pad[axis] = (0, rows - a.shape[axis])
    return jnp.pad(a, pad)


def kernel(x, mem, ln_mix_g, w_in_a, b_f_a, w_in_b, ln_kv_g, w_kv, ln_mem_g, w_memkv, w_out, ln_ffn_g, w_up, conv_w, conv_b, w_down, final_g, loss_target, m_ln_mix_g, m_w_in_a, m_b_f_a, m_w_in_b, m_ln_kv_g, m_w_kv, m_ln_mem_g, m_w_memkv, m_w_out, m_ln_ffn_g, m_w_up, m_conv_w, m_conv_b, m_w_down, m_final_g, v_ln_mix_g, v_w_in_a, v_b_f_a, v_w_in_b, v_ln_kv_g, v_w_kv, v_ln_mem_g, v_w_memkv, v_w_out, v_ln_ffn_g, v_w_up, v_conv_w, v_conv_b, v_w_down, v_final_g):
    B, S, D = x.shape
    NM = mem.shape[1]
    T = B * S
    F = w_down.shape[1] * N_DEV
    my_idx = 4 * lax.axis_index("x") + 2 * lax.axis_index("y") + lax.axis_index("c")

    shards = {"w_in_a": w_in_a[0], "w_in_b": w_in_b[0], "w_kv": w_kv, "w_memkv": w_memkv, "w_out": w_out,
              "w_up": w_up, "w_down": w_down}
    moms = {"w_in_a": (m_w_in_a[0], v_w_in_a[0]), "w_in_b": (m_w_in_b[0], v_w_in_b[0]), "w_kv": (m_w_kv, v_w_kv),
            "w_memkv": (m_w_memkv, v_w_memkv), "w_out": (m_w_out, v_w_out), "w_up": (m_w_up, v_w_up),
            "w_down": (m_w_down, v_w_down)}

    groups = [("a1", [("w_in_a", None)]),
              ("a2", [("w_memkv", None), ("w_out", None), ("conv_w", None)]),
              ("b0", [("w_up", 0), ("w_down", 0)]), ("a3", [("w_in_b", None), ("w_kv", None)]),
              ("b1", [("w_up", 1), ("w_down", 1)])]
    sources = dict(shards, conv_w=conv_w)
    started, token = {}, None
    for gname, members in groups:
        srcs = []
        for n, layer in members:
            a = sources[n] if layer is None else sources[n][layer]
            srcs.append(a if n == "conv_w" else a.astype(BF16))
        ssem, rsem, thru, lands, token = _scatter_start(srcs, _whole, name=f"gather_start_{gname}", order_after=token)
        started[gname] = (ssem, rsem, thru, lands)

    def gathered(gname, after):
        ssem, rsem, thru, lands = started[gname]
        thru, lands = _scatter_wait(ssem, rsem, thru, lands, _whole, after, name=f"gather_wait_{gname}")
        return [lax.dynamic_update_index_in_dim(land, s, my_idx, 0) for land, s in zip(lands, thru)]

    full = {}
    (g_wa,) = gathered("a1", token)
    full["w_in_a"] = _from_slots(g_wa, "rows2")

    wa = full["w_in_a"]
    n_qkv = 3 * MAIN_W
    wa = jnp.concatenate([wa[:, :n_qkv], wa[:, n_qkv + N_MAIN_HEADS:], wa[:, n_qkv:n_qkv + N_MAIN_HEADS],
                          jnp.zeros((D, LANES - N_MAIN_HEADS), BF16)], axis=1)
    n_main = n_qkv + MEM_W
    full["w_up"], full["w_down"] = {}, {}
    b_f =_pad_rows(b_f_a.reshape(1, N_MAIN_HEADS), LANES, 1)

    x2d = x.reshape(T, D)
    mem2d = mem.reshape(B * NM, D)
    tgt2d = loss_target.reshape(T, D)
    PM, PX = N_MAIN_HEADS // 2, N_MEM_HEADS // 2

    def stats_to_heads(c2d):
        c = c2d.reshape(B, S, LANES)[:, :, :N_MAIN_HEADS].transpose(0, 2, 1)
        return c[:, :, None, :]

    def mem_kv(layer):
        return _mm_fwd(mem2d, full["w_memkv"][layer], name=f"memkv{layer}", tm=B * NM, tn=2 * MEM_W,
                       out_dtype=BF16, g=ln_mem_g[layer], save_h=True)

    def conv_ffn_fwd(xin, layer):
        u, a, h = _ffn_up_gate(xin, ln_ffn_g[layer], full["w_up"][layer], conv_w_full[layer],
                               conv_b[layer].reshape(1, 2 * F), name=f"ffn_up{layer}", S=S)
        xo = _mm_fwd(a, full["w_down"][layer], name=f"ffn_down{layer}", tm=min(512, T), tn=512, out_dtype=F32, res=xin)
        return xo, (u, h, a)

    proj_a, h_mix0 = _mm_fwd(x2d, wa, name="in_proj_a", tm=min(1024, T), tn=512, out_dtype=BF16, g=ln_mix_g[0],
                             ncols=n_main, save_h=True)
    f_logit = _mm_fwd(x2d, wa, name="in_proj_f", tm=min(1024, T), tn=LANES, out_dtype=F32, g=ln_mix_g[0],
                      col0=n_main // LANES, ncols=LANES)
    c2d = _forget_cumsum(f_logit, b_f, B=B, S=S, name="forget_cumsum")
    cr = stats_to_heads(c2d)
    o_main0, lse0 = _fox_fwd_g(proj_a, proj_a, proj_a, cr, name="fox_fwd", B=B, S=S, P=PM, q_cb=0, k_cb=PM, v_cb=2 * PM)
    g_wmem, g_wout, g_cw = gathered("a2", lse0)
    full["w_memkv"] = _from_slots(g_wmem, "rows3")
    full["w_out"] = _from_slots(g_wout, "rows3")
    conv_w_full = _from_slots(g_cw, "cols3")
    memkv0, h_mem0 = mem_kv(0)
    o_mem0, lse_m0 = _softmax_fwd(proj_a, memkv0, memkv0, name="mem_fwd0", B=B, S=S, Sk=NM, P=PX, q_cb=3 * PM,
                                  k_cb=0, v_cb=PX, causal=False)
    o_cat0 = jnp.concatenate([o_main0, o_mem0], axis=1)
    x1 = _mm_fwd(o_cat0, full["w_out"][0], name="out_proj0", tm=min(512, T), tn=512, out_dtype=F32, res=x2d)
    g_up, g_dn = gathered("b0", x1)
    full["w_up"][0], full["w_down"][0] = _from_slots(g_up, "cols2"), _from_slots(g_dn, "rows2")
    x2, (u0, h_ffn0, a0) = conv_ffn_fwd(x1, 0)
    g_wb, g_wkv = gathered("a3", x2)
    wb, wkv = _from_slots(g_wb, "rows2"), _from_slots(g_wkv, "cols2")
    kv, h_kv =_mm_fwd(x2, wkv, name="kv_proj", tm=min(1024, T), tn=512, out_dtype=BF16, g=ln_kv_g, save_h=True)
    proj_b, h_mix1 = _mm_fwd(x2, wb, name="in_proj_b", tm=min(1024, T), tn=512, out_dtype=BF16, g=ln_mix_g[1],
                             save_h=True)
    o_main1, rt1 = _sb_fwd_g(proj_b, kv, kv, name="sb_fwd", B=B, S=S, P=PM, q_cb=0, k_cb=0, v_cb=PM)
    memkv1, h_mem1 = mem_kv(1)
    o_mem1, lse_m1 = _softmax_fwd(proj_b, memkv1, memkv1, name="mem_fwd1", B=B, S=S, Sk=NM, P=PX, q_cb=PM,
                                  k_cb=0, v_cb=PX, causal=False)
    o_cat1 = jnp.concatenate([o_main1, o_mem1], axis=1)
    x3 = _mm_fwd(o_cat1, full["w_out"][1], name="out_proj1", tm=min(512, T), tn=512, out_dtype=F32, res=x2)
    g_up, g_dn = gathered("b1", x3)
    full["w_up"][1], full["w_down"][1] = _from_slots(g_up, "cols2"), _from_slots(g_dn, "rows2")
    x4, (u1, h_ffn1, a1) = conv_ffn_fwd(x3, 1)
    dx4, dg_final, loss_part = _loss_head(x4, final_g, tgt2d, name="loss_head")

    grads = {}
    small = {}
    reduce_groups = []

    def start_reduce(gname, keys, kinds):
        slots = [_to_slots(grads[k], kind) for k, kind in zip(keys, kinds)]
        ssem, rsem, thru, lands, tok = _scatter_start(slots, _slot, name=f"reduce_start_{gname}")
        reduce_groups.append((gname, keys, ssem, rsem, thru, lands))
        return tok[0, 0]

    def conv_ffn_bwd(dxo, xin, u, h, a, layer):
        w_dn = full["w_down"][layer]
        da = _mm_nt(dxo, w_dn, name=f"d_act{layer}", tm=min(512, T), tn=F // 2, out_dtype=F32)
        grads[("w_down", layer)] = _wgrad(a, dxo, f"g_w_down{layer}")
        cwl = conv_w_full[layer]
        du_g, du_v, p_g, p_v = _conv_gate_bwd(da, u, cwl, conv_b[layer].reshape(1, 2 * F), name=f"conv_bwd{layer}",
                                              B=B, S=S)
        small[("conv_w", layer)] = jnp.concatenate([p_g[0:3], p_v[0:3]], axis=1)
        small[("conv_b", layer)] = jnp.concatenate([p_g[3], p_v[3]], axis=0)
        grads[("w_up", layer)] = jnp.concatenate(
            [_wgrad(h, du_g, f"g_w_up_gate{layer}"), _wgrad(h, du_v, f"g_w_up_val{layer}")], axis=1)
        tok = start_reduce(f"ffn{layer}", [("w_down", layer), ("w_up", layer)], ["rows2", "cols2"])
        dxi, dg = _mm_nt_rmsbwd([(du_g, 0), (du_v, 1)], full["w_up"][layer], xin, ln_ffn_g[layer] + tok,
                                name=f"d_ffn_in{layer}", dres=dxo)
        small[("ln_ffn_g", layer)] = dg[0]
        return dxi

    def mem_bwd(proj, q_cb, memkv, h_mem, do_cat, o_mem, lse_m, layer):
        dqm, dmk, dmv = _softmax_bwd(proj, memkv, memkv, do_cat, o_mem, lse_m, name=f"mem_bwd{layer}", B=B, S=S,
                                     Sk=NM, P=PX, q_cb=q_cb, k_cb=0, v_cb=PX, do_cb=PM, causal=False)
        grads[("w_memkv", layer)] = jnp.concatenate(
            [_wgrad(h_mem, dmk, f"g_w_memk{layer}"), _wgrad(h_mem, dmv, f"g_w_memv{layer}")], axis=1)
        _, dg = _mm_nt_rmsbwd([(dmk, 0), (dmv, 1)], full["w_memkv"][layer], mem2d, ln_mem_g[layer],
                              name=f"d_mem_in{layer}", want_dx=False)
        small[("ln_mem_g", layer)] = dg[0]
        return dqm

    dx3 = conv_ffn_bwd(dx4, x3, u1, h_ffn1, a1, 1)
    do_cat1 = _mm_nt(dx3, full["w_out"][1], name="d_o_cat1", tm=min(512, T), tn=512, out_dtype=BF16)
    grads[("w_out", 1)] = _wgrad(o_cat1, dx3, "g_w_out1")
    dq1, dk1, dv1 = _sb_bwd_g(proj_b, kv, kv, do_cat1, rt1, name="sb_bwd", B=B, S=S, P=PM, q_cb=0, k_cb=0, v_cb=PM,
                            do_cb=0)
    dqm1 = mem_bwd(proj_b, PM, memkv1, h_mem1, do_cat1, o_mem1, lse_m1, 1)
    grads["w_in_b"] = jnp.concatenate([_wgrad(h_mix1, dq1, "g_w_in_b_q"), _wgrad(h_mix1, dqm1, "g_w_in_b_m")], axis=1)
    grads["w_kv"] = jnp.concatenate([_wgrad(h_kv, dk1, "g_w_kv_k"), _wgrad(h_kv, dv1, "g_w_kv_v")], axis=1)
    tok = start_reduce("mix1", [("w_out", 1), "w_in_b", "w_kv", ("w_memkv", 1)], ["rows2", "rows2", "cols2", "rows2"])
    dx2, dg = _mm_nt_rmsbwd([(dq1, 0), (dqm1, MAIN_W // MEM_W)], wb, x2, ln_mix_g[1] + tok, name="d_mix_in1", dres=dx3)
    small[("ln_mix_g", 1)] = dg[0]
    dx2, dg = _mm_nt_rmsbwd([(dk1, 0), (dv1, 1)], wkv, x2, ln_kv_g, name="d_kv_in", dres=dx2)
    small["ln_kv_g"] = dg[0]
    dx1 = conv_ffn_bwd(dx2, x1, u0, h_ffn0, a0, 0)
    do_cat0 = _mm_nt(dx1, full["w_out"][0], name="d_o_cat0", tm=min(512, T), tn=512, out_dtype=BF16)
    grads[("w_out", 0)] = _wgrad(o_cat0, dx1, "g_w_out0")
    dq0, dk0, dv0, dcs = _fox_bwd_g(proj_a, proj_a, proj_a, do_cat0, lse0, cr, name="fox_bwd", B=B, S=S, P=PM, q_cb=0,
                                  k_cb=PM, v_cb=2 * PM, do_cb=0)
    dqm0 = mem_bwd(proj_a, 3 * PM, memkv0, h_mem0, do_cat0, o_mem0, lse_m0, 0)
    dc2d = _pad_rows(dcs[:, :, 0, :].transpose(0, 2, 1).reshape(T, N_MAIN_HEADS), LANES, 1)
    df, db_f = _forget_cumsum_bwd(dc2d, f_logit, b_f, B=B, S=S, name="forget_cumsum_bwd")
    a_parts = [(dq0, 0), (dk0, 1), (dv0, 2), (dqm0, n_qkv // MEM_W), (df, n_main // LANES)]
    g_wa = jnp.concatenate([_wgrad(h_mix0, p, f"g_w_in_a{k}") for k, (p, _) in enumerate(a_parts)], axis=1)
    grads["w_in_a"] = jnp.concatenate([g_wa[:, :n_qkv], g_wa[:, n_main:n_main + N_MAIN_HEADS], g_wa[:, n_qkv:n_main]],
                                      axis=1)
    tok = start_reduce("mix0", [("w_out", 0), ("w_memkv", 0), "w_in_a"], ["rows2", "rows2", "rows2"])
    dx0, dg = _mm_nt_rmsbwd(a_parts, wa, x2d, ln_mix_g[0] + tok, name="d_mix_in0", dres=dx1)
    small[("ln_mix_g", 0)] = dg[0]
    grad_x = dx0.reshape(B, S, D)

    def both_small(name):
        return jnp.stack([small[(name, 0)], small[(name, 1)]])

    small_list = [("ln_mix_g", both_small("ln_mix_g")), ("b_f_a", db_f[:, :N_MAIN_HEADS]), ("ln_kv_g", small["ln_kv_g"]),
                  ("ln_mem_g", both_small("ln_mem_g")), ("ln_ffn_g", both_small("ln_ffn_g")),
                  ("conv_w", both_small("conv_w")), ("conv_b", both_small("conv_b")), ("final_g", dg_final[0]),
                  ("loss", loss_part[0, :1])]
    sm_rows = []
    for _, a in small_list:
        flat = a.reshape(-1)
        sm_rows.append(_pad_rows(flat, _round_up(flat.size, 8 * LANES), 0).reshape(-1, LANES))
    spack = jnp.concatenate(sm_rows, axis=0)
    s_ssem, s_rsem, s_thru, s_lands, s_tok = _scatter_start([spack], _whole, name="small_start")

    pieces = {}
    for gname, keys, ssem, rsem, thru, lands in reduce_groups:
        thru, lands = _scatter_wait(ssem, rsem, thru, lands, _slot, s_tok, name=f"reduce_wait_{gname}")
        for key, mine, land in zip(keys, thru, lands):
            own = lax.dynamic_index_in_dim(mine, my_idx, 0, keepdims=False)
            land = lax.dynamic_update_index_in_dim(land, own, my_idx, 0)
            tag = key if isinstance(key, str) else f"{key[0]}{key[1]}"
            pieces[key] = _sum_slots(land, name=f"sum_{tag}")

    red = {}
    for n in ("w_in_a", "w_in_b", "w_kv"):
        red[n] = pieces[n].reshape(shards[n].shape)
    for n in ("w_memkv", "w_out", "w_up", "w_down"):
        red[n] = jnp.stack([pieces[(n, 0)], pieces[(n, 1)]])

    weights = {"ln_mix_g": ln_mix_g, "w_in_a": w_in_a, "b_f_a": b_f_a, "w_in_b": w_in_b, "ln_kv_g": ln_kv_g,
               "w_kv": w_kv, "ln_mem_g": ln_mem_g, "w_memkv": w_memkv, "w_out": w_out, "ln_ffn_g": ln_ffn_g,
               "w_up": w_up, "conv_w": conv_w, "conv_b": conv_b, "w_down": w_down, "final_g": final_g}
    m_in = {"ln_mix_g": m_ln_mix_g, "w_in_a": m_w_in_a, "b_f_a": m_b_f_a, "w_in_b": m_w_in_b, "ln_kv_g": m_ln_kv_g,
            "w_kv": m_w_kv, "ln_mem_g": m_ln_mem_g, "w_memkv": m_w_memkv, "w_out": m_w_out, "ln_ffn_g": m_ln_ffn_g,
            "w_up": m_w_up, "conv_w": m_conv_w, "conv_b": m_conv_b, "w_down": m_w_down, "final_g": m_final_g}
    v_in = {"ln_mix_g": v_ln_mix_g, "w_in_a": v_w_in_a, "b_f_a": v_b_f_a, "w_in_b": v_w_in_b, "ln_kv_g": v_ln_kv_g,
            "w_kv": v_w_kv, "ln_mem_g": v_ln_mem_g, "w_memkv": v_w_memkv, "w_out": v_w_out, "ln_ffn_g": v_ln_ffn_g,
            "w_up": v_w_up, "conv_w": v_conv_w, "conv_b": v_conv_b, "w_down": v_w_down, "final_g": v_final_g}
    order = list(weights)
    big_names = [n for n, _ in BIG]
    g_out, d_out, nm_out, nv_out = {}, {}, {}, {}

    def update(n):
        w = weights[n]
        cols = w.shape[-1]
        g = red[n].reshape(w.shape)
        d, nm, nv = _adamw(w.reshape(-1, cols), g.reshape(-1, cols), m_in[n].reshape(-1, cols),
                           v_in[n].reshape(-1, cols), name=f"adamw_{n}")
        g_out[n], d_out[n], nm_out[n], nv_out[n] = g, d.reshape(w.shape), nm.reshape(w.shape), nv.reshape(w.shape)

    for n in big_names:
        update(n)
    s_thru, s_lands = _scatter_wait(s_ssem, s_rsem, s_thru, s_lands, _whole, d_out[big_names[-1]], name="small_wait")
    ssum = _sum_slots(lax.dynamic_update_index_in_dim(s_lands[0], s_thru[0], my_idx, 0), name="sum_small")
    off = 0
    for (n, a), rows in zip(small_list, sm_rows):
        red[n] = ssum[off:off + rows.shape[0]].reshape(-1)[:a.size].reshape(a.shape)
        off += rows.shape[0]
    loss = red["loss"][0]
    shard_cols = conv_w.shape[2]
    red["conv_w"] = lax.dynamic_slice_in_dim(red["conv_w"], my_idx * shard_cols, shard_cols, axis=2)
    red["b_f_a"] = red["b_f_a"].reshape(b_f_a.shape)
    update("conv_w")
    small_names = [n for n in order if n not in g_out]

    def pack_small(src):
        rows = []
        for n in small_names:
            flat = src[n].reshape(-1)
            rows.append(_pad_rows(flat, _round_up(flat.size, 8 * LANES), 0).reshape(-1, LANES))
        return jnp.concatenate(rows, axis=0), [r.shape[0] for r in rows]

    red_small = {n: red[n].reshape(weights[n].shape) for n in small_names}
    wp, counts = pack_small(weights)
    gp, _ = pack_small(red_small)
    mp, _ = pack_small(m_in)
    vp, _ = pack_small(v_in)
    dp, nmp, nvp = _adamw(wp, gp, mp, vp, name="adamw_small")
    off = 0
    for n, cnt in zip(small_names, counts):
        shp = weights[n].shape
        size = weights[n].size
        g_out[n] = red_small[n]
        d_out[n] = dp[off:off + cnt].reshape(-1)[:size].reshape(shp)
        nm_out[n] = nmp[off:off + cnt].reshape(-1)[:size].reshape(shp)
        nv_out[n] = nvp[off:off + cnt].reshape(-1)[:size].reshape(shp)
        off += cnt

    return (loss, grad_x, *[g_out[n] for n in order], *[d_out[n] for n in order],
            *[nm_out[n] for n in order], *[nv_out[n] for n in order])
```

```python
import functools

import jax
import jax.numpy as jnp
from jax import lax
from jax.experimental import pallas as pl
from jax.experimental.pallas import tpu as pltpu

F32 = jnp.float32
BF16 = jnp.bfloat16
LANES = 128
HEAD_DIM = 64
N_MAIN_HEADS = 12
N_MEM_HEADS = 4
MAIN_W = N_MAIN_HEADS * HEAD_DIM
MEM_W = N_MEM_HEADS * HEAD_DIM
SCALE = HEAD_DIM ** -0.5
EPS = 1e-6
NEG = -1e30
N_DEV = 8
ATT_TILE = 256
MEM_Q_TILE = 1024
VMEM_BIG = 56 * 2 ** 20
MESH = pl.DeviceIdType.MESH

ADAM_LR = 0.001
ADAM_B1 = 0.9
ADAM_B2 = 0.999
ADAM_EPS = 1e-08
ADAM_WD = 0.01
ADAM_STEP = 10

NT = (((1,), (1,)), ((), ()))
TN = (((0,), (0,)), ((), ()))


def _pc(body, *, name, out_shape, grid=None, in_specs=None, out_specs=None, scratch_shapes=(),
        semantics=None, vmem=None):
    kw = {}
    if grid is not None:
        kw["grid"] = grid
    params = pltpu.CompilerParams(dimension_semantics=semantics, vmem_limit_bytes=vmem)
    return pl.pallas_call(body, name=name, out_shape=out_shape, in_specs=in_specs, out_specs=out_specs,
                          scratch_shapes=list(scratch_shapes), compiler_params=params, **kw)


def _sds(shape, dtype):
    return jax.ShapeDtypeStruct(shape, dtype)


def _mm_fwd(a, w, *, name, tm, tn, out_dtype, g=None, res=None, col0=0, ncols=None, save_h=False):
    m_rows, k = a.shape
    n = w.shape[1] if ncols is None else ncols
    grid = (m_rows // tm, n // tn)
    norm = g is not None

    def body(*refs):
        refs = list(refs)
        a_ref = refs.pop(0)
        g_ref = refs.pop(0) if norm else None
        w_ref = refs.pop(0)
        res_ref = refs.pop(0) if res is not None else None
        o_ref = refs.pop(0)
        hout_ref = refs.pop(0) if save_h else None
        h_ref = refs.pop(0) if norm else None
        if norm:
            @pl.when(pl.program_id(1) == 0)
            def _():
                xv = a_ref[...]
                r = lax.rsqrt(jnp.mean(xv * xv, axis=-1, keepdims=True) + EPS)
                h = ((xv * r) * g_ref[...]).astype(BF16)
                h_ref[...] = h
                if save_h:
                    hout_ref[...] = h
            lhs = h_ref[...]
        else:
            lhs = a_ref[...].astype(BF16)
        acc = jnp.dot(lhs, w_ref[...], preferred_element_type=F32)
        if res is not None:
            acc = acc + res_ref[...]
        o_ref[...] = acc.astype(out_dtype)

    in_specs = [pl.BlockSpec((tm, k), lambda i, j: (i, 0))]
    args = [a]
    if norm:
        in_specs.append(pl.BlockSpec((1, k), lambda i, j: (0, 0)))
        args.append(g.reshape(1, k))
    in_specs.append(pl.BlockSpec((k, tn), lambda i, j: (0, j + col0)))
    args.append(w)
    if res is not None:
        in_specs.append(pl.BlockSpec((tm, tn), lambda i, j: (i, j)))
        args.append(res)
    out_shape = [_sds((m_rows, n), out_dtype)]
    out_specs = [pl.BlockSpec((tm, tn), lambda i, j: (i, j))]
    if save_h:
        out_shape.append(_sds((m_rows, k), BF16))
        out_specs.append(pl.BlockSpec((tm, k), lambda i, j: (i, 0)))
    scratch = [pltpu.VMEM((tm, k), BF16)] if norm else []
    outs = _pc(body, name=name, out_shape=out_shape, grid=grid, in_specs=in_specs, out_specs=out_specs,
               scratch_shapes=scratch, semantics=("arbitrary", "arbitrary"), vmem=VMEM_BIG)(*args)
    return outs if save_h else outs[0]


def _mm_nt(a, w, *, name, tm, tn, out_dtype):
    m_rows, k = a.shape
    n = w.shape[0]

    def body(a_ref, w_ref, o_ref):
        acc = lax.dot_general(a_ref[...].astype(BF16), w_ref[...], NT, preferred_element_type=F32)
        o_ref[...] = acc.astype(out_dtype)

    return _pc(body, name=name, out_shape=_sds((m_rows, n), out_dtype), grid=(m_rows // tm, n // tn),
               in_specs=[pl.BlockSpec((tm, k), lambda i, j: (i, 0)), pl.BlockSpec((tn, k), lambda i, j: (j, 0))],
               out_specs=pl.BlockSpec((tm, tn), lambda i, j: (i, j)),
               semantics=("arbitrary", "arbitrary"), vmem=VMEM_BIG)(a, w)


def _mm_tn(a, b, *, name, ta, tn, tt):
    t_rows, ka = a.shape
    n = b.shape[1]
    nt = t_rows // tt

    def body(a_ref, b_ref, o_ref, acc_ref):
        t = pl.program_id(2)

        @pl.when(t == 0)
        def _():
            acc_ref[...] = jnp.zeros_like(acc_ref)

        acc_ref[...] += lax.dot_general(a_ref[...].astype(BF16), b_ref[...].astype(BF16), TN,
                                        preferred_element_type=F32)

        @pl.when(t == nt - 1)
        def _():
            o_ref[...] = acc_ref[...].astype(BF16)

    return _pc(body, name=name, out_shape=_sds((ka, n), BF16), grid=(ka // ta, n // tn, nt),
               in_specs=[pl.BlockSpec((tt, ta), lambda i, j, t: (t, i)),
                         pl.BlockSpec((tt, tn), lambda i, j, t: (t, j))],
               out_specs=pl.BlockSpec((ta, tn), lambda i, j, t: (i, j)),
               scratch_shapes=[pltpu.VMEM((ta, tn), F32)],
               semantics=("arbitrary", "arbitrary", "arbitrary"), vmem=VMEM_BIG)(a, b)


def _wgrad(a, b, name):
    t_rows, ka = a.shape
    n = b.shape[1]
    ta = ka if ka <= 1024 else ka // 2
    tn = n
    while ta * tn * 4 > 6 * 2 ** 20 and tn % 256 == 0:
        tn //= 2
    tt = min(1024, t_rows)
    return _mm_tn(a, b, name=name, ta=ta, tn=tn, tt=tt)


def _mm_nt_rmsbwd(parts, w, x, g, *, name, dres=None, want_dx=True):
    m_rows, d = x.shape
    tm = min(256, m_rows)
    n_parts = len(parts)

    def body(*refs):
        refs = list(refs)
        dy_refs = [refs.pop(0) for _ in range(n_parts)]
        w_refs = [refs.pop(0) for _ in range(n_parts)]
        x_ref = refs.pop(0)
        g_ref = refs.pop(0)
        dres_ref = refs.pop(0) if dres is not None else None
        dx_ref = refs.pop(0) if want_dx else None
        dg_ref = refs.pop(0)

        @pl.when(pl.program_id(0) == 0)
        def _():
            dg_ref[...] = jnp.zeros_like(dg_ref)

        dh = None
        for dy_ref, w_ref in zip(dy_refs, w_refs):
            t = lax.dot_general(dy_ref[...].astype(BF16), w_ref[...], NT, preferred_element_type=F32)
            dh = t if dh is None else dh + t
        xv = x_ref[...]
        r = lax.rsqrt(jnp.mean(xv * xv, axis=-1, keepdims=True) + EPS)
        xh = xv * r
        dg_ref[...] += jnp.sum(dh * xh, axis=0, keepdims=True)
        if want_dx:
            dhg = dh * g_ref[...]
            dx = r * (dhg - xh * jnp.mean(dhg * xh, axis=-1, keepdims=True))
            if dres is not None:
                dx = dx + dres_ref[...]
            dx_ref[...] = dx

    in_specs, args = [], []
    for dy, _ in parts:
        in_specs.append(pl.BlockSpec((tm, dy.shape[1]), lambda i: (i, 0)))
        args.append(dy)
    for dy, cb in parts:
        in_specs.append(pl.BlockSpec((d, dy.shape[1]), functools.partial(lambda i, cb: (0, cb), cb=cb)))
        args.append(w)
    in_specs += [pl.BlockSpec((tm, d), lambda i: (i, 0)), pl.BlockSpec((1, d), lambda i: (0, 0))]
    args += [x, g.reshape(1, d)]
    if dres is not None:
        in_specs.append(pl.BlockSpec((tm, d), lambda i: (i, 0)))
        args.append(dres)
    out_shape, out_specs = [], []
    if want_dx:
        out_shape.append(_sds((m_rows, d), F32))
        out_specs.append(pl.BlockSpec((tm, d), lambda i: (i, 0)))
    out_shape.append(_sds((1, d), F32))
    out_specs.append(pl.BlockSpec((1, d), lambda i: (0, 0)))
    outs = _pc(body, name=name, out_shape=out_shape, grid=(m_rows // tm,), in_specs=in_specs,
               out_specs=out_specs, semantics=("arbitrary",), vmem=VMEM_BIG)(*args)
    return (outs[0], outs[1]) if want_dx else (None, outs[0])


def _loss_head(x, g, tgt, *, name):
    m_rows, d = x.shape
    tm = min(256, m_rows)

    def body(x_ref, g_ref, t_ref, dx_ref, dg_ref, loss_ref):
        @pl.when(pl.program_id(0) == 0)
        def _():
            dg_ref[...] = jnp.zeros_like(dg_ref)
            loss_ref[...] = jnp.zeros_like(loss_ref)

        xv = x_ref[...]
        r = lax.rsqrt(jnp.mean(xv * xv, axis=-1, keepdims=True) + EPS)
        xh = xv * r
        gv = g_ref[...]
        err = xh * gv - t_ref[...]
        per_tok = jnp.mean(err * err, axis=-1, keepdims=True)
        loss_ref[...] += 0.5 * jnp.sum(per_tok, axis=0, keepdims=True)
        dout = err * (1.0 / d)
        dg_ref[...] += jnp.sum(dout * xh, axis=0, keepdims=True)
        dhg = dout * gv
        dx_ref[...] = r * (dhg - xh * jnp.mean(dhg * xh, axis=-1, keepdims=True))

    row = pl.BlockSpec((tm, d), lambda i: (i, 0))
    return _pc(body, name=name, out_shape=[_sds((m_rows, d), F32), _sds((1, d), F32), _sds((1, LANES), F32)],
               grid=(m_rows // tm,), in_specs=[row, pl.BlockSpec((1, d), lambda i: (0, 0)), row],
               out_specs=[row, pl.BlockSpec((1, d), lambda i: (0, 0)), pl.BlockSpec((1, LANES), lambda i: (0, 0))],
               semantics=("arbitrary",))(x, g.reshape(1, d), tgt)


def _split3(v):
    hi = v.astype(BF16)
    r1 = v - hi.astype(F32)
    mid = r1.astype(BF16)
    lo = (r1 - mid.astype(F32)).astype(BF16)
    return hi, mid, lo


def _split2(v):
    hi = v.astype(BF16)
    lo = (v - hi.astype(F32)).astype(BF16)
    return hi, lo


def _tri_dot3(tri, v):
    hi, mid, lo = _split3(v)
    return (jnp.dot(tri, hi, preferred_element_type=F32) + jnp.dot(tri, mid, preferred_element_type=F32)
            + jnp.dot(tri, lo, preferred_element_type=F32))


def _dot_tri2(v, tri):
    hi, lo = _split2(v)
    return jnp.dot(hi, tri, preferred_element_type=F32) + jnp.dot(lo, tri, preferred_element_type=F32)


def _log_sigmoid(v):
    return jnp.minimum(v, 0.0) - jnp.log(1.0 + jnp.exp(-jnp.abs(v)))


def _forget_cumsum(f_logit, b_f, *, B, S, name):
    ch = min(256, S)
    nch = S // ch

    def body(f_ref, b_ref, c_ref):
        r_i = lax.broadcasted_iota(jnp.int32, (ch, ch), 0)
        c_i = lax.broadcasted_iota(jnp.int32, (ch, ch), 1)
        tri = (c_i <= r_i).astype(BF16)
        bv = b_ref[...]

        def step(k, carry):
            rows = pl.ds(pl.multiple_of(k * ch, ch), ch)
            lf = _log_sigmoid(f_ref[rows, :] + bv)
            c_ref[rows, :] = _tri_dot3(tri, lf) + carry
            return carry + jnp.sum(lf, axis=0, keepdims=True)

        lax.fori_loop(0, nch, step, jnp.zeros((1, LANES), F32))

    blk = pl.BlockSpec((S, LANES), lambda b: (b, 0))
    return _pc(body, name=name, out_shape=_sds((B * S, LANES), F32), grid=(B,),
               in_specs=[blk, pl.BlockSpec((1, LANES), lambda b: (0, 0))], out_specs=blk,
               semantics=("arbitrary",))(f_logit, b_f)


def _forget_cumsum_bwd(dc, f_logit, b_f, *, B, S, name):
    ch = min(256, S)
    nch = S // ch

    def body(dc_ref, f_ref, b_ref, df_ref, db_ref):
        @pl.when(pl.program_id(0) == 0)
        def _():
            db_ref[...] = jnp.zeros_like(db_ref)

        r_i = lax.broadcasted_iota(jnp.int32, (ch, ch), 0)
        c_i = lax.broadcasted_iota(jnp.int32, (ch, ch), 1)
        tri = (c_i >= r_i).astype(BF16)
        bv = b_ref[...]

        def step(kk, carry):
            tail, dbs = carry
            k = nch - 1 - kk
            rows = pl.ds(pl.multiple_of(k * ch, ch), ch)
            dcv = dc_ref[rows, :]
            dlf = _tri_dot3(tri, dcv) + tail
            z = f_ref[rows, :] + bv
            df = dlf * (1.0 / (1.0 + jnp.exp(z)))
            df_ref[rows, :] = df.astype(BF16)
            return tail + jnp.sum(dcv, axis=0, keepdims=True), dbs + jnp.sum(df, axis=0, keepdims=True)

        zero = jnp.zeros((1, LANES), F32)
        _, dbs = lax.fori_loop(0, nch, step, (zero, zero))
        db_ref[...] += dbs

    blk = pl.BlockSpec((S, LANES), lambda b: (b, 0))
    one = pl.BlockSpec((1, LANES), lambda b: (0, 0))
    return _pc(body, name=name, out_shape=[_sds((B * S, LANES), BF16), _sds((1, LANES), F32)], grid=(B,),
               in_specs=[blk, blk, one], out_specs=[blk, one], semantics=("arbitrary",))(dc, f_logit, b_f)


def _head_mask(lane, hh):
    return (lane < HEAD_DIM) if hh == 0 else (lane >= HEAD_DIM)


def _col_spec(rows, nblk_rows, cb):
    return pl.BlockSpec((rows, LANES), lambda b, p, i: (b * nblk_rows + i, cb + p))


def _kv_spec(rows, cb):
    return pl.BlockSpec((rows, LANES), lambda b, p, i: (b, cb + p))


def _stat_col_spec(tq):
    return pl.BlockSpec((1, 2, tq, 1), lambda b, p, i: (b, p, i, 0))


def _stat_row_spec(S):
    return pl.BlockSpec((1, 2, 1, S), lambda b, p, i: (b, p, 0, 0))


def _softmax_fwd(qa, ka, va, *, name, B, S, Sk, P, q_cb, k_cb, v_cb, causal, cc=None, cr=None):
    tq = min(ATT_TILE if causal else MEM_Q_TILE, S)
    tk = min(ATT_TILE, Sk)
    nq, nk = S // tq, Sk // tk
    decay = cc is not None
    assert not causal or (tq == tk and S == Sk)

    def body(*refs):
        if decay:
            q_ref, k_ref, v_ref, cc_ref, cr_ref, o_ref, lse_ref = refs
        else:
            q_ref, k_ref, v_ref, o_ref, lse_ref = refs
        i = pl.program_id(2)
        q = q_ref[...]
        lane = lax.broadcasted_iota(jnp.int32, (tq, LANES), 1)
        row = lax.broadcasted_iota(jnp.int32, (tq, tk), 0) + i * tq
        col0 = lax.broadcasted_iota(jnp.int32, (tq, tk), 1)
        outs = []
        for hh in range(2):
            qh = jnp.where(_head_mask(lane, hh), q, jnp.zeros_like(q))

            def step(kb, carry, hh=hh, qh=qh):
                m, l, acc = carry
                ks = pl.multiple_of(kb * tk, tk)
                kblk = k_ref[pl.ds(ks, tk), :]
                vblk = v_ref[pl.ds(ks, tk), :]
                s = lax.dot_general(qh, kblk, NT, preferred_element_type=F32) * SCALE
                if decay:
                    s = s + (cc_ref[0, hh] - cr_ref[0, hh, :, pl.ds(ks, tk)])
                if causal:
                    s = jnp.where(col0 + kb * tk <= row, s, NEG)
                m_new = jnp.maximum(m, jnp.max(s, axis=-1, keepdims=True))
                alpha = jnp.exp(m - m_new)
                p = jnp.exp(s - m_new)
                l = alpha * l + jnp.sum(p, axis=-1, keepdims=True)
                acc = alpha * acc + jnp.dot(p.astype(BF16), vblk, preferred_element_type=F32)
                return m_new, l, acc

            init = (jnp.full((tq, 1), NEG, F32), jnp.zeros((tq, 1), F32), jnp.zeros((tq, LANES), F32))
            m, l, acc = lax.fori_loop(0, (i + 1) if causal else nk, step, init)
            outs.append(acc / l)
            lse_ref[0, hh] = m + jnp.log(l)
        o_ref[...] = jnp.where(lane < HEAD_DIM, outs[0], outs[1]).astype(BF16)

    in_specs = [_col_spec(tq, nq, q_cb), _kv_spec(Sk, k_cb), _kv_spec(Sk, v_cb)]
    args = [qa, ka, va]
    if decay:
        in_specs += [_stat_col_spec(tq), _stat_row_spec(S)]
        args += [cc, cr]
    return _pc(body, name=name,
               out_shape=[_sds((B * S, P * LANES), BF16), _sds((B, 2 * P, S, 1), F32)],
               grid=(B, P, nq), in_specs=in_specs, out_specs=[_col_spec(tq, nq, 0), _stat_col_spec(tq)],
               semantics=("arbitrary", "arbitrary", "arbitrary"), vmem=VMEM_BIG)(*args)


def _softmax_bwd(qa, ka, va, doa, oa, lse, *, name, B, S, Sk, P, q_cb, k_cb, v_cb, do_cb, causal,
                 cc=None, cr=None):
    tq = min(ATT_TILE if causal else MEM_Q_TILE, S)
    tk = min(ATT_TILE, Sk)
    nq, nk = S // tq, Sk // tk
    decay = cc is not None

    def body(*refs):
        if decay:
            q_ref, k_ref, v_ref, do_ref, o_ref, lse_ref, cc_ref, cr_ref, dq_ref, dk_ref, dv_ref, dcs_ref = refs
        else:
            q_ref, k_ref, v_ref, do_ref, o_ref, lse_ref, dq_ref, dk_ref, dv_ref = refs
        i = pl.program_id(2)

        @pl.when(i == 0)
        def _():
            dk_ref[...] = jnp.zeros_like(dk_ref)
            dv_ref[...] = jnp.zeros_like(dv_ref)
            if decay:
                dcs_ref[...] = jnp.zeros_like(dcs_ref)

        q = q_ref[...]
        do = do_ref[...]
        prod = do.astype(F32) * o_ref[...].astype(F32)
        lane = lax.broadcasted_iota(jnp.int32, (tq, LANES), 1)
        row = lax.broadcasted_iota(jnp.int32, (tq, tk), 0) + i * tq
        col0 = lax.broadcasted_iota(jnp.int32, (tq, tk), 1)
        dqs = []
        for hh in range(2):
            hmask = _head_mask(lane, hh)
            qh = jnp.where(hmask, q, jnp.zeros_like(q))
            doh = jnp.where(hmask, do, jnp.zeros_like(do))
            lse_h = lse_ref[0, hh]
            n_blocks = (i + 1) if causal else nk

            def probs(kb, hh=hh, qh=qh, doh=doh, lse_h=lse_h):
                ks = pl.multiple_of(kb * tk, tk)
                kblk = k_ref[pl.ds(ks, tk), :]
                vblk = v_ref[pl.ds(ks, tk), :]
                s = lax.dot_general(qh, kblk, NT, preferred_element_type=F32) * SCALE
                if decay:
                    s = s + (cc_ref[0, hh] - cr_ref[0, hh, :, pl.ds(ks, tk)])
                if causal:
                    s = jnp.where(col0 + kb * tk <= row, s, NEG)
                p = jnp.exp(s - lse_h)
                dp = lax.dot_general(doh, vblk, NT, preferred_element_type=F32)
                return ks, kblk, p, dp

            if decay:
                def delta_step(kb, acc):
                    _, _, p, dp = probs(kb)
                    return acc + jnp.sum(p * dp, axis=-1, keepdims=True)

                delta = lax.fori_loop(0, n_blocks, delta_step, jnp.zeros((tq, 1), F32))
            else:
                delta = jnp.sum(jnp.where(hmask, prod, 0.0), axis=-1, keepdims=True)

            def step(kb, dq_acc, hh=hh, qh=qh, doh=doh, delta=delta):
                ks, kblk, p, dp = probs(kb)
                ds = p * (dp - delta)
                dsb = ds.astype(BF16)
                dk_ref[pl.ds(ks, tk), :] += lax.dot_general(dsb, qh, TN, preferred_element_type=F32) * SCALE
                dv_ref[pl.ds(ks, tk), :] += lax.dot_general(p.astype(BF16), doh, TN, preferred_element_type=F32)
                if decay:
                    dcs_ref[0, hh, :, pl.ds(ks, tk)] -= jnp.sum(ds, axis=0, keepdims=True)
                return dq_acc + jnp.dot(dsb, kblk, preferred_element_type=F32)

            dqs.append(lax.fori_loop(0, n_blocks, step, jnp.zeros((tq, LANES), F32)) * SCALE)
        dq_ref[...] = jnp.where(lane < HEAD_DIM, dqs[0], dqs[1]).astype(BF16)

    in_specs = [_col_spec(tq, nq, q_cb), _kv_spec(Sk, k_cb), _kv_spec(Sk, v_cb), _col_spec(tq, nq, do_cb),
                _col_spec(tq, nq, 0), _stat_col_spec(tq)]
    args = [qa, ka, va, doa, oa, lse]
    out_shape = [_sds((B * S, P * LANES), BF16), _sds((B * Sk, P * LANES), F32), _sds((B * Sk, P * LANES), F32)]
    out_specs = [_col_spec(tq, nq, 0), _kv_spec(Sk, 0), _kv_spec(Sk, 0)]
    if decay:
        in_specs += [_stat_col_spec(tq), _stat_row_spec(S)]
        args += [cc, cr]
        out_shape.append(_sds((B, 2 * P, 1, S), F32))
        out_specs.append(_stat_row_spec(S))
    return _pc(body, name=name, out_shape=out_shape, grid=(B, P, nq), in_specs=in_specs, out_specs=out_specs,
               semantics=("arbitrary", "arbitrary", "arbitrary"), vmem=VMEM_BIG)(*args)


def _sb_terms(qh, kblk, row, col0, kb, tk):
    z = lax.dot_general(qh, kblk, NT, preferred_element_type=F32) * SCALE
    causal = (col0 + kb * tk) < row
    sp = jnp.maximum(z, 0.0) + jnp.log(1.0 + jnp.exp(-jnp.abs(z)))
    ls = z - sp
    lm = jnp.where(causal, -sp, 0.0)
    return causal, ls, lm


def _stickbreak_fwd(qa, ka, va, *, name, B, S, P, q_cb, k_cb, v_cb):
    tq = tk = min(ATT_TILE, S)
    nq = S // tq

    def body(q_ref, k_ref, v_ref, o_ref, rt_ref):
        i = pl.program_id(2)
        q = q_ref[...]
        lane = lax.broadcasted_iota(jnp.int32, (tq, LANES), 1)
        row = lax.broadcasted_iota(jnp.int32, (tq, tk), 0) + i * tq
        col0 = lax.broadcasted_iota(jnp.int32, (tq, tk), 1)
        t_r = lax.broadcasted_iota(jnp.int32, (tk, tk), 0)
        t_c = lax.broadcasted_iota(jnp.int32, (tk, tk), 1)
        after = (t_r > t_c).astype(BF16)
        outs = []
        for hh in range(2):
            qh = jnp.where(_head_mask(lane, hh), q, jnp.zeros_like(q))

            def step(jj, carry, qh=qh):
                run, acc = carry
                kb = i - jj
                ks = pl.multiple_of(kb * tk, tk)
                kblk = k_ref[pl.ds(ks, tk), :]
                vblk = v_ref[pl.ds(ks, tk), :]
                causal, ls, lm = _sb_terms(qh, kblk, row, col0, kb, tk)
                suf = _dot_tri2(lm, after)
                a = jnp.where(causal, jnp.exp(ls + run + suf), 0.0)
                acc = acc + jnp.dot(a.astype(BF16), vblk, preferred_element_type=F32)
                return run + jnp.sum(lm, axis=-1, keepdims=True), acc

            run, acc = lax.fori_loop(0, i + 1, step, (jnp.zeros((tq, 1), F32), jnp.zeros((tq, LANES), F32)))
            outs.append(acc)
            rt_ref[0, hh] = run
        o_ref[...] = jnp.where(lane < HEAD_DIM, outs[0], outs[1]).astype(BF16)

    return _pc(body, name=name, out_shape=[_sds((B * S, P * LANES), BF16), _sds((B, 2 * P, S, 1), F32)],
               grid=(B, P, nq), in_specs=[_col_spec(tq, nq, q_cb), _kv_spec(S, k_cb), _kv_spec(S, v_cb)],
               out_specs=[_col_spec(tq, nq, 0), _stat_col_spec(tq)],
               semantics=("arbitrary", "arbitrary", "arbitrary"), vmem=VMEM_BIG)(qa, ka, va)


def _stickbreak_bwd(qa, ka, va, doa, rt, *, name, B, S, P, q_cb, k_cb, v_cb, do_cb):
    tq = tk = min(ATT_TILE, S)
    nq = S // tq

    def body(q_ref, k_ref, v_ref, do_ref, rt_ref, dq_ref, dk_ref, dv_ref):
        i = pl.program_id(2)

        @pl.when(i == 0)
        def _():
            dk_ref[...] = jnp.zeros_like(dk_ref)
            dv_ref[...] = jnp.zeros_like(dv_ref)

        q = q_ref[...]
        do = do_ref[...]
        lane = lax.broadcasted_iota(jnp.int32, (tq, LANES), 1)
        row = lax.broadcasted_iota(jnp.int32, (tq, tk), 0) + i * tq
        col0 = lax.broadcasted_iota(jnp.int32, (tq, tk), 1)
        t_r = lax.broadcasted_iota(jnp.int32, (tk, tk), 0)
        t_c = lax.broadcasted_iota(jnp.int32, (tk, tk), 1)
        upto = (t_r <= t_c).astype(BF16)
        before = (t_r < t_c).astype(BF16)
        dqs = []
        for hh in range(2):
            hmask = _head_mask(lane, hh)
            qh = jnp.where(hmask, q, jnp.zeros_like(q))
            doh = jnp.where(hmask, do, jnp.zeros_like(do))
            rt_h = rt_ref[0, hh]

            def step(kb, carry, qh=qh, doh=doh, rt_h=rt_h):
                pl_sum, pg_sum, dq_acc = carry
                ks = pl.multiple_of(kb * tk, tk)
                kblk = k_ref[pl.ds(ks, tk), :]
                vblk = v_ref[pl.ds(ks, tk), :]
                causal, ls, lm = _sb_terms(qh, kblk, row, col0, kb, tk)
                pin = _dot_tri2(lm, upto)
                a = jnp.where(causal, jnp.exp(ls + (rt_h - pl_sum) - pin), 0.0)
                da = lax.dot_general(doh, vblk, NT, preferred_element_type=F32)
                gm = a * da
                pg = _dot_tri2(gm, before) + pg_sum
                beta = jnp.exp(ls)
                dz = jnp.where(causal, gm * (1.0 - beta) - pg * beta, 0.0)
                dzb = dz.astype(BF16)
                dk_ref[pl.ds(ks, tk), :] += lax.dot_general(dzb, qh, TN, preferred_element_type=F32) * SCALE
                dv_ref[pl.ds(ks, tk), :] += lax.dot_general(a.astype(BF16), doh, TN, preferred_element_type=F32)
                return (pl_sum + jnp.sum(lm, axis=-1, keepdims=True),
                        pg_sum + jnp.sum(gm, axis=-1, keepdims=True),
                        dq_acc + jnp.dot(dzb, kblk, preferred_element_type=F32))

            zc = jnp.zeros((tq, 1), F32)
            _, _, dq_h = lax.fori_loop(0, i + 1, step, (zc, zc, jnp.zeros((tq, LANES), F32)))
            dqs.append(dq_h * SCALE)
        dq_ref[...] = jnp.where(lane < HEAD_DIM, dqs[0], dqs[1]).astype(BF16)

    return _pc(body, name=name,
               out_shape=[_sds((B * S, P * LANES), BF16), _sds((B * S, P * LANES), F32), _sds((B * S, P * LANES), F32)],
               grid=(B, P, nq),
               in_specs=[_col_spec(tq, nq, q_cb), _kv_spec(S, k_cb), _kv_spec(S, v_cb), _col_spec(tq, nq, do_cb),
                         _stat_col_spec(tq)],
               out_specs=[_col_spec(tq, nq, 0), _kv_spec(S, 0), _kv_spec(S, 0)],
               semantics=("arbitrary", "arbitrary", "arbitrary"), vmem=VMEM_BIG)(qa, ka, va, doa, rt)


def _pair_setup(q_ref, tq, tk):
    q = q_ref[...] * jnp.asarray(SCALE, BF16)
    lane = lax.broadcasted_iota(jnp.int32, (tq, LANES), 1)
    masks = [_head_mask(lane, hh) for hh in range(2)]
    qh = [jnp.where(mk, q, jnp.zeros_like(q)) for mk in masks]
    on_or_below = (lax.broadcasted_iota(jnp.int32, (tq, tk), 1) <= lax.broadcasted_iota(jnp.int32, (tq, tk), 0))
    return lane, masks, qh, on_or_below


def _fox_fwd(qa, ka, va, cr, *, name, B, S, P, q_cb, k_cb, v_cb):
    tq = tk = min(ATT_TILE, S)
    nq = S // tq

    def body(q_ref, k_ref, v_ref, cr_ref, o_ref, lse_ref, s_buf, acc_ref, m_ref, l_ref):
        i = pl.program_id(2)
        lane, _, qh, on_or_below = _pair_setup(q_ref, tq, tk)
        m_ref[...] = jnp.full(m_ref.shape, NEG, F32)
        l_ref[...] = jnp.zeros(l_ref.shape, F32)
        acc_ref[...] = jnp.zeros(acc_ref.shape, F32)

        def scores(kb, slot):
            kblk = k_ref[pl.ds(pl.multiple_of(kb * tk, tk), tk), :]
            for hh in range(2):
                s_buf[slot, hh] = lax.dot_general(qh[hh], kblk, NT, preferred_element_type=F32)

        def block(kb, slot, diag):
            ks = pl.multiple_of(kb * tk, tk)
            vblk = v_ref[pl.ds(ks, tk), :]
            ps = []
            for hh in range(2):
                s = s_buf[slot, hh] - cr_ref[0, hh, :, pl.ds(ks, tk)]
                if diag:
                    s = jnp.where(on_or_below, s, NEG)
                m = m_ref[hh]
                m_new = jnp.maximum(m, jnp.max(s, axis=-1, keepdims=True))
                alpha = jnp.exp(m - m_new)
                p = jnp.exp(s - m_new)
                m_ref[hh] = m_new
                l_ref[hh] = alpha * l_ref[hh] + jnp.sum(p, axis=-1, keepdims=True)
                ps.append((alpha, p.astype(BF16)))
            for hh in range(2):
                acc_ref[hh] = ps[hh][0] * acc_ref[hh] + jnp.dot(ps[hh][1], vblk, preferred_element_type=F32)

        def step(kb, _):
            slot = lax.rem(kb, 2)
            scores(kb + 1, 1 - slot)
            block(kb, slot, False)
            return 0

        scores(0, 0)
        lax.fori_loop(0, i, step, 0)
        block(i, lax.rem(i, 2), True)
        l0, l1 = l_ref[0], l_ref[1]
        lse_ref[0, 0] = m_ref[0] + jnp.log(l0)
        lse_ref[0, 1] = m_ref[1] + jnp.log(l1)
        o_ref[...] = jnp.where(lane < HEAD_DIM, acc_ref[0] / l0, acc_ref[1] / l1).astype(BF16)

    return _pc(body, name=name, out_shape=[_sds((B * S, P * LANES), BF16), _sds((B, 2 * P, S, 1), F32)],
               grid=(B, P, nq),
               in_specs=[_col_spec(tq, nq, q_cb), _kv_spec(S, k_cb), _kv_spec(S, v_cb), _stat_row_spec(S)],
               out_specs=[_col_spec(tq, nq, 0), _stat_col_spec(tq)],
               scratch_shapes=[pltpu.VMEM((2, 2, tq, tk), F32), pltpu.VMEM((2, tq, LANES), F32),
                               pltpu.VMEM((2, tq, 1), F32), pltpu.VMEM((2, tq, 1), F32)],
               semantics=("arbitrary", "arbitrary", "arbitrary"), vmem=VMEM_BIG)(qa, ka, va, cr)


def _fox_bwd(qa, ka, va, doa, lse, cr, *, name, B, S, P, q_cb, k_cb, v_cb, do_cb):
    tq = tk = min(ATT_TILE, S)
    nq = S // tq

    def body(q_ref, k_ref, v_ref, do_ref, lse_ref, cr_ref, dq_ref, dk_ref, dv_ref, dcs_ref):
        i = pl.program_id(2)

        @pl.when(i == 0)
        def _():
            dk_ref[...] = jnp.zeros_like(dk_ref)
            dv_ref[...] = jnp.zeros_like(dv_ref)
            dcs_ref[...] = jnp.zeros_like(dcs_ref)

        lane, masks, qh, on_or_below = _pair_setup(q_ref, tq, tk)
        do = do_ref[...]
        doh = [jnp.where(mk, do, jnp.zeros_like(do)) for mk in masks]
        lse_h = [lse_ref[0, hh] for hh in range(2)]

        def probs(kb, hh, diag):
            ks = pl.multiple_of(kb * tk, tk)
            kblk = k_ref[pl.ds(ks, tk), :]
            vblk = v_ref[pl.ds(ks, tk), :]
            s = lax.dot_general(qh[hh], kblk, NT, preferred_element_type=F32) - cr_ref[0, hh, :, pl.ds(ks, tk)]
            if diag:
                s = jnp.where(on_or_below, s, NEG)
            p = jnp.exp(s - lse_h[hh])
            dp = lax.dot_general(doh[hh], vblk, NT, preferred_element_type=F32)
            return ks, kblk, p, dp

        def delta_block(kb, carry, diag):
            out = []
            for hh in range(2):
                _, _, p, dp = probs(kb, hh, diag)
                out.append(carry[hh] + jnp.sum(p * dp, axis=-1, keepdims=True))
            return tuple(out)

        zc = jnp.zeros((tq, 1), F32)
        delta = lax.fori_loop(0, i, lambda kb, c: delta_block(kb, c, False), (zc, zc))
        delta = delta_block(i, delta, True)

        def grad_block(kb, carry, diag):
            out = []
            for hh in range(2):
                ks, kblk, p, dp = probs(kb, hh, diag)
                ds = p * (dp - delta[hh])
                dsb = ds.astype(BF16)
                rows = pl.ds(ks, tk)
                dk_ref[rows, :] += lax.dot_general(dsb, qh[hh], TN, preferred_element_type=F32)
                dv_ref[rows, :] += lax.dot_general(p.astype(BF16), doh[hh], TN, preferred_element_type=F32)
                dcs_ref[0, hh, :, rows] -= jnp.sum(ds, axis=0, keepdims=True)
                out.append(carry[hh] + jnp.dot(dsb, kblk, preferred_element_type=F32))
            return tuple(out)

        za = jnp.zeros((tq, LANES), F32)
        dq = lax.fori_loop(0, i, lambda kb, c: grad_block(kb, c, False), (za, za))
        dq = grad_block(i, dq, True)
        dq_ref[...] = (jnp.where(lane < HEAD_DIM, dq[0], dq[1]) * SCALE).astype(BF16)

    return _pc(body, name=name,
               out_shape=[_sds((B * S, P * LANES), BF16), _sds((B * S, P * LANES), F32), _sds((B * S, P * LANES), F32),
                          _sds((B, 2 * P, 1, S), F32)],
               grid=(B, P, nq),
               in_specs=[_col_spec(tq, nq, q_cb), _kv_spec(S, k_cb), _kv_spec(S, v_cb), _col_spec(tq, nq, do_cb),
                         _stat_col_spec(tq), _stat_row_spec(S)],
               out_specs=[_col_spec(tq, nq, 0), _kv_spec(S, 0), _kv_spec(S, 0), _stat_row_spec(S)],
               semantics=("arbitrary", "arbitrary", "arbitrary"), vmem=VMEM_BIG)(qa, ka, va, doa, lse, cr)


def _sb_logs(qh, kblk):
    z = lax.dot_general(qh, kblk, NT, preferred_element_type=F32)
    nz = -z
    lg = jnp.log(1.0 + jnp.exp(jnp.minimum(z, nz)))
    lm = jnp.minimum(nz, 0.0) - lg
    return lm + z, lm


def _sb_fwd(qa, ka, va, *, name, B, S, P, q_cb, k_cb, v_cb):
    tq = tk = min(ATT_TILE, S)
    nq = S // tq

    def body(q_ref, k_ref, v_ref, o_ref, rt_ref):
        i = pl.program_id(2)
        lane, _, qh, on_or_below = _pair_setup(q_ref, tq, tk)
        t_r = lax.broadcasted_iota(jnp.int32, (tk, tk), 0)
        t_c = lax.broadcasted_iota(jnp.int32, (tk, tk), 1)
        after = (t_r > t_c).astype(BF16)
        below = t_c < t_r

        def block(kb, carry, diag):
            ks = pl.multiple_of(kb * tk, tk)
            kblk = k_ref[pl.ds(ks, tk), :]
            vblk = v_ref[pl.ds(ks, tk), :]
            out = []
            for hh in range(2):
                run, acc = carry[hh]
                ls, lm = _sb_logs(qh[hh], kblk)
                if diag:
                    lm = jnp.where(below, lm, 0.0)
                a = jnp.exp(ls + run + _dot_tri2(lm, after))
                if diag:
                    a = jnp.where(below, a, 0.0)
                acc = acc + jnp.dot(a.astype(BF16), vblk, preferred_element_type=F32)
                out.append((run + jnp.sum(lm, axis=-1, keepdims=True), acc))
            return tuple(out)

        one = (jnp.zeros((tq, 1), F32), jnp.zeros((tq, LANES), F32))
        carry = block(i, (one, one), True)
        (r0, a0), (r1, a1) = lax.fori_loop(0, i, lambda jj, c: block(i - 1 - jj, c, False), carry)
        rt_ref[0, 0] = r0
        rt_ref[0, 1] = r1
        o_ref[...] = jnp.where(lane < HEAD_DIM, a0, a1).astype(BF16)

    return _pc(body, name=name, out_shape=[_sds((B * S, P * LANES), BF16), _sds((B, 2 * P, S, 1), F32)],
               grid=(B, P, nq), in_specs=[_col_spec(tq, nq, q_cb), _kv_spec(S, k_cb), _kv_spec(S, v_cb)],
               out_specs=[_col_spec(tq, nq, 0), _stat_col_spec(tq)],
               semantics=("arbitrary", "arbitrary", "arbitrary"), vmem=VMEM_BIG)(qa, ka, va)


def _sb_bwd(qa, ka, va, doa, rt, *, name, B, S, P, q_cb, k_cb, v_cb, do_cb):
    tq = tk = min(ATT_TILE, S)
    nq = S // tq

    def body(q_ref, k_ref, v_ref, do_ref, rt_ref, dq_ref, dk_ref, dv_ref):
        i = pl.program_id(2)

        @pl.when(i == 0)
        def _():
            dk_ref[...] = jnp.zeros_like(dk_ref)
            dv_ref[...] = jnp.zeros_like(dv_ref)

        lane, masks, qh, _ = _pair_setup(q_ref, tq, tk)
        do = do_ref[...]
        doh = [jnp.where(mk, do, jnp.zeros_like(do)) for mk in masks]
        rt_h = [rt_ref[0, hh] for hh in range(2)]
        t_r = lax.broadcasted_iota(jnp.int32, (tk, tk), 0)
        t_c = lax.broadcasted_iota(jnp.int32, (tk, tk), 1)
        upto = (t_r <= t_c).astype(BF16)
        before = (t_r < t_c).astype(BF16)
        below = t_c < t_r

        def block(kb, carry, diag):
            ks = pl.multiple_of(kb * tk, tk)
            rows = pl.ds(ks, tk)
            kblk = k_ref[rows, :]
            vblk = v_ref[rows, :]
            out = []
            for hh in range(2):
                pl_sum, pg_sum, dq_acc = carry[hh]
                ls, lm = _sb_logs(qh[hh], kblk)
                if diag:
                    lm = jnp.where(below, lm, 0.0)
                a = jnp.exp(ls + (rt_h[hh] - pl_sum) - _dot_tri2(lm, upto))
                if diag:
                    a = jnp.where(below, a, 0.0)
                gm = a * lax.dot_general(doh[hh], vblk, NT, preferred_element_type=F32)
                pg = _dot_tri2(gm, before) + pg_sum
                dz = gm - jnp.exp(ls) * (gm + pg)
                if diag:
                    dz = jnp.where(below, dz, 0.0)
                dzb = dz.astype(BF16)
                dk_ref[rows, :] += lax.dot_general(dzb, qh[hh], TN, preferred_element_type=F32)
                dv_ref[rows, :] += lax.dot_general(a.astype(BF16), doh[hh], TN, preferred_element_type=F32)
                out.append((pl_sum + jnp.sum(lm, axis=-1, keepdims=True),
                            pg_sum + jnp.sum(gm, axis=-1, keepdims=True),
                            dq_acc + jnp.dot(dzb, kblk, preferred_element_type=F32)))
            return tuple(out)

        zc = jnp.zeros((tq, 1), F32)
        one = (zc, zc, jnp.zeros((tq, LANES), F32))
        carry = lax.fori_loop(0, i, lambda kb, c: block(kb, c, False), (one, one))
        (_, _, dq0), (_, _, dq1) = block(i, carry, True)
        dq_ref[...] = (jnp.where(lane < HEAD_DIM, dq0, dq1) * SCALE).astype(BF16)

    return _pc(body, name=name,
               out_shape=[_sds((B * S, P * LANES), BF16), _sds((B * S, P * LANES), F32), _sds((B * S, P * LANES), F32)],
               grid=(B, P, nq),
               in_specs=[_col_spec(tq, nq, q_cb), _kv_spec(S, k_cb), _kv_spec(S, v_cb), _col_spec(tq, nq, do_cb),
                         _stat_col_spec(tq)],
               out_specs=[_col_spec(tq, nq, 0), _kv_spec(S, 0), _kv_spec(S, 0)],
               semantics=("arbitrary", "arbitrary", "arbitrary"), vmem=VMEM_BIG)(qa, ka, va, doa, rt)


HEAD_GROUP = 3


def _g_col_spec(rows, nblk_rows, cb, G):
    return pl.BlockSpec((rows, G * LANES), lambda b, p, i: (b * nblk_rows + i, cb // G + p))


def _g_kv_spec(rows, cb, G):
    return pl.BlockSpec((rows, G * LANES), lambda b, p, i: (b, cb // G + p))


def _g_stat_col_spec(tq, G):
    return pl.BlockSpec((1, 2 * G, tq, 1), lambda b, p, i: (b, p, i, 0))


def _g_stat_row_spec(S, G):
    return pl.BlockSpec((1, 2 * G, 1, S), lambda b, p, i: (b, p, 0, 0))


def _lanes(g):
    return slice(g * LANES, (g + 1) * LANES)


def _streams(x_ref, G, scale=None):
    rows = x_ref.shape[0]
    lane = lax.broadcasted_iota(jnp.int32, (rows, LANES), 1)
    out = []
    for g in range(G):
        x = x_ref[:, _lanes(g)]
        if scale is not None:
            x = x * jnp.asarray(scale, x.dtype)
        for hh in range(2):
            out.append(jnp.where(_head_mask(lane, hh), x, jnp.zeros_like(x)))
    return lane, out


def _wide(stat, width):
    return jnp.tile(stat, (1, width // LANES))


def _fold_lanes(v):
    out = v[:, :LANES]
    for j in range(1, v.shape[1] // LANES):
        out = out + v[:, j * LANES:(j + 1) * LANES]
    return out


def _kv_blocks(ref, ks, tk, G):
    return [ref[pl.ds(ks, tk), _lanes(g)] for g in range(G)]


def _sweep(i, block):
    def step(kb, c):
        block(kb, False)
        return c
    lax.fori_loop(0, i, step, 0)
    block(i, True)


def _fox_fwd_g(qa, ka, va, cr, *, name, B, S, P, q_cb, k_cb, v_cb, G=HEAD_GROUP):
    tq = tk = min(ATT_TILE, S)
    nq = S // tq
    NS = 2 * G

    def body(q_ref, k_ref, v_ref, cr_ref, o_ref, lse_ref, acc_ref, m_ref, l_ref):
        i = pl.program_id(2)
        lane, qh = _streams(q_ref, G, SCALE)
        on_or_below = (lax.broadcasted_iota(jnp.int32, (tq, tk), 1) <= lax.broadcasted_iota(jnp.int32, (tq, tk), 0))
        m_ref[...] = jnp.full(m_ref.shape, NEG, F32)
        l_ref[...] = jnp.zeros(l_ref.shape, F32)
        acc_ref[...] = jnp.zeros(acc_ref.shape, F32)

        def block(kb, diag):
            ks = pl.multiple_of(kb * tk, tk)
            kblk = _kv_blocks(k_ref, ks, tk, G)
            vblk = _kv_blocks(v_ref, ks, tk, G)
            ss = [lax.dot_general(qh[st], kblk[st // 2], NT, preferred_element_type=F32) for st in range(NS)]
            ps = []
            for st in range(NS):
                s = ss[st] - cr_ref[0, st, :, pl.ds(ks, tk)]
                if diag:
                    s = jnp.where(on_or_below, s, NEG)
                m = m_ref[st]
                m_new = jnp.maximum(m, jnp.max(s, axis=-1, keepdims=True))
                alpha = jnp.exp(m - m_new)
                p = jnp.exp(s - _wide(m_new, tk))
                m_ref[st] = m_new
                l_ref[st] = alpha * l_ref[st] + _fold_lanes(p)
                ps.append((alpha, p.astype(BF16)))
            pvs = [jnp.dot(ps[st][1], vblk[st // 2], preferred_element_type=F32) for st in range(NS)]
            for st in range(NS):
                acc_ref[st] = ps[st][0] * acc_ref[st] + pvs[st]

        _sweep(i, block)
        ls = [jnp.sum(l_ref[st], axis=-1, keepdims=True) for st in range(NS)]
        for st in range(NS):
            lse_ref[0, st] = jnp.max(m_ref[st], axis=-1, keepdims=True) + jnp.log(ls[st])
        for g in range(G):
            o_ref[:, _lanes(g)] = jnp.where(lane < HEAD_DIM, acc_ref[2 * g] / ls[2 * g],
                                            acc_ref[2 * g + 1] / ls[2 * g + 1]).astype(BF16)

    return _pc(body, name=name, out_shape=[_sds((B * S, P * LANES), BF16), _sds((B, 2 * P, S, 1), F32)],
               grid=(B, P // G, nq),
               in_specs=[_g_col_spec(tq, nq, q_cb, G), _g_kv_spec(S, k_cb, G), _g_kv_spec(S, v_cb, G),
                         _g_stat_row_spec(S, G)],
               out_specs=[_g_col_spec(tq, nq, 0, G), _g_stat_col_spec(tq, G)],
               scratch_shapes=[pltpu.VMEM((NS, tq, LANES), F32)] * 3,
               semantics=("arbitrary", "arbitrary", "arbitrary"), vmem=VMEM_BIG)(qa, ka, va, cr)


def _fox_bwd_g(qa, ka, va, doa, lse, cr, *, name, B, S, P, q_cb, k_cb, v_cb, do_cb, G=HEAD_GROUP):
    tq = tk = min(ATT_TILE, S)
    nq = S // tq
    NS = 2 * G

    def body(q_ref, k_ref, v_ref, do_ref, lse_ref, cr_ref, dq_ref, dk_ref, dv_ref, dcs_ref, dqa_ref, delta_ref, lse_s,
             p_buf, dp_buf):
        i = pl.program_id(2)

        @pl.when(i == 0)
        def _():
            dk_ref[...] = jnp.zeros_like(dk_ref)
            dv_ref[...] = jnp.zeros_like(dv_ref)
            dcs_ref[...] = jnp.zeros_like(dcs_ref)

        lane, qh = _streams(q_ref, G, SCALE)
        _, doh = _streams(do_ref, G)
        on_or_below = (lax.broadcasted_iota(jnp.int32, (tq, tk), 1) <= lax.broadcasted_iota(jnp.int32, (tq, tk), 0))
        delta_ref[...] = jnp.zeros(delta_ref.shape, F32)
        dqa_ref[...] = jnp.zeros(dqa_ref.shape, F32)
        for st in range(NS):
            lse_s[st] = jnp.broadcast_to(lse_ref[0, st], (tq, LANES))

        def probs(kb, diag):
            ks = pl.multiple_of(kb * tk, tk)
            kblk = _kv_blocks(k_ref, ks, tk, G)
            vblk = _kv_blocks(v_ref, ks, tk, G)
            ss = [lax.dot_general(qh[st], kblk[st // 2], NT, preferred_element_type=F32) for st in range(NS)]
            dps = [lax.dot_general(doh[st], vblk[st // 2], NT, preferred_element_type=F32) for st in range(NS)]
            ps = []
            for st in range(NS):
                s = ss[st] - cr_ref[0, st, :, pl.ds(ks, tk)]
                if diag:
                    s = jnp.where(on_or_below, s, NEG)
                ps.append(jnp.exp(s - _wide(lse_s[st], tk)))
            return ks, kblk, ps, dps

        def delta_block(kb, diag):
            _, _, ps, dps = probs(kb, diag)
            for st in range(NS):
                delta_ref[st] += _fold_lanes(ps[st] * dps[st])
                p_buf[st, kb] = ps[st]
                dp_buf[st, kb] = dps[st]

        _sweep(i, delta_block)
        for st in range(NS):
            delta_ref[st] = jnp.broadcast_to(jnp.sum(delta_ref[st], axis=-1, keepdims=True), (tq, LANES))

        def grad_block(kb, diag):
            ks = pl.multiple_of(kb * tk, tk)
            kblk = _kv_blocks(k_ref, ks, tk, G)
            rows = pl.ds(ks, tk)
            dsb, pb = [], []
            for st in range(NS):
                p = p_buf[st, kb]
                ds = p * (dp_buf[st, kb] - _wide(delta_ref[st], tk))
                dcs_ref[0, st, :, rows] -= jnp.sum(ds, axis=0, keepdims=True)
                dsb.append(ds.astype(BF16))
                pb.append(p.astype(BF16))
            dks = [lax.dot_general(dsb[st], qh[st], TN, preferred_element_type=F32) for st in range(NS)]
            dvs = [lax.dot_general(pb[st], doh[st], TN, preferred_element_type=F32) for st in range(NS)]
            dqs = [jnp.dot(dsb[st], kblk[st // 2], preferred_element_type=F32) for st in range(NS)]
            for g in range(G):
                dk_ref[rows, _lanes(g)] += dks[2 * g] + dks[2 * g + 1]
                dv_ref[rows, _lanes(g)] += dvs[2 * g] + dvs[2 * g + 1]
            for st in range(NS):
                dqa_ref[st] += dqs[st]

        _sweep(i, grad_block)
        for g in range(G):
            dq_ref[:, _lanes(g)] = (jnp.where(lane < HEAD_DIM, dqa_ref[2 * g], dqa_ref[2 * g + 1]) * SCALE).astype(BF16)

    return _pc(body, name=name,
               out_shape=[_sds((B * S, P * LANES), BF16), _sds((B * S, P * LANES), F32), _sds((B * S, P * LANES), F32),
                          _sds((B, 2 * P, 1, S), F32)],
               grid=(B, P // G, nq),
               in_specs=[_g_col_spec(tq, nq, q_cb, G), _g_kv_spec(S, k_cb, G), _g_kv_spec(S, v_cb, G),
                         _g_col_spec(tq, nq, do_cb, G), _g_stat_col_spec(tq, G), _g_stat_row_spec(S, G)],
               out_specs=[_g_col_spec(tq, nq, 0, G), _g_kv_spec(S, 0, G), _g_kv_spec(S, 0, G), _g_stat_row_spec(S, G)],
               scratch_shapes=[pltpu.VMEM((NS, tq, LANES), F32)] * 3 + [pltpu.VMEM((NS, nq, tq, tk), F32)] * 2,
               semantics=("arbitrary", "arbitrary", "arbitrary"), vmem=VMEM_BIG)(qa, ka, va, doa, lse, cr)


def _sb_logs_z(z):
    nz = -z
    lm = jnp.minimum(nz, 0.0) - jnp.log(1.0 + jnp.exp(jnp.minimum(z, nz)))
    return lm + z, lm


def _sb_fwd_g(qa, ka, va, *, name, B, S, P, q_cb, k_cb, v_cb, G=HEAD_GROUP):
    tq = tk = min(ATT_TILE, S)
    nq = S // tq
    NS = 2 * G

    def body(q_ref, k_ref, v_ref, o_ref, rt_ref, acc_ref, run_ref):
        i = pl.program_id(2)
        lane, qh = _streams(q_ref, G, SCALE)
        t_r = lax.broadcasted_iota(jnp.int32, (tk, tk), 0)
        t_c = lax.broadcasted_iota(jnp.int32, (tk, tk), 1)
        after = (t_r > t_c).astype(BF16)
        below = t_c < t_r
        acc_ref[...] = jnp.zeros(acc_ref.shape, F32)
        run_ref[...] = jnp.zeros(run_ref.shape, F32)

        def block(kb, diag):
            ks = pl.multiple_of(kb * tk, tk)
            kblk = _kv_blocks(k_ref, ks, tk, G)
            vblk = _kv_blocks(v_ref, ks, tk, G)
            zs = [lax.dot_general(qh[st], kblk[st // 2], NT, preferred_element_type=F32) for st in range(NS)]
            lss, parts = [], []
            for st in range(NS):
                ls, lm = _sb_logs_z(zs[st])
                if diag:
                    lm = jnp.where(below, lm, 0.0)
                lss.append(ls + _wide(run_ref[st], tk))
                run_ref[st] += jnp.sum(lm, axis=-1, keepdims=True)
                parts.append(_split2(lm))
            sufs = [jnp.dot(parts[st][0], after, preferred_element_type=F32)
                    + jnp.dot(parts[st][1], after, preferred_element_type=F32) for st in range(NS)]
            ab = []
            for st in range(NS):
                a = jnp.exp(lss[st] + sufs[st])
                if diag:
                    a = jnp.where(below, a, 0.0)
                ab.append(a.astype(BF16))
            pvs = [jnp.dot(ab[st], vblk[st // 2], preferred_element_type=F32) for st in range(NS)]
            for st in range(NS):
                acc_ref[st] += pvs[st]

        block(i, True)

        def step(jj, c):
            block(i - 1 - jj, False)
            return c

        lax.fori_loop(0, i, step, 0)
        for st in range(NS):
            rt_ref[0, st] = jnp.max(run_ref[st], axis=-1, keepdims=True)
        for g in range(G):
            o_ref[:, _lanes(g)] = jnp.where(lane < HEAD_DIM, acc_ref[2 * g], acc_ref[2 * g + 1]).astype(BF16)

    return _pc(body, name=name, out_shape=[_sds((B * S, P * LANES), BF16), _sds((B, 2 * P, S, 1), F32)],
               grid=(B, P // G, nq),
               in_specs=[_g_col_spec(tq, nq, q_cb, G), _g_kv_spec(S, k_cb, G), _g_kv_spec(S, v_cb, G)],
               out_specs=[_g_col_spec(tq, nq, 0, G), _g_stat_col_spec(tq, G)],
               scratch_shapes=[pltpu.VMEM((NS, tq, LANES), F32)] * 2,
               semantics=("arbitrary", "arbitrary", "arbitrary"), vmem=VMEM_BIG)(qa, ka, va)


def _sb_bwd_g(qa, ka, va, doa, rt, *, name, B, S, P, q_cb, k_cb, v_cb, do_cb, G=HEAD_GROUP):
    tq = tk = min(ATT_TILE, S)
    nq = S // tq
    NS = 2 * G

    def body(q_ref, k_ref, v_ref, do_ref, rt_ref, dq_ref, dk_ref, dv_ref, dqa_ref, pl_ref, pg_ref):
        i = pl.program_id(2)

        @pl.when(i == 0)
        def _():
            dk_ref[...] = jnp.zeros_like(dk_ref)
            dv_ref[...] = jnp.zeros_like(dv_ref)

        lane, qh = _streams(q_ref, G, SCALE)
        _, doh = _streams(do_ref, G)
        t_r = lax.broadcasted_iota(jnp.int32, (tk, tk), 0)
        t_c = lax.broadcasted_iota(jnp.int32, (tk, tk), 1)
        upto = (t_r <= t_c).astype(BF16)
        before = (t_r < t_c).astype(BF16)
        below = t_c < t_r
        dqa_ref[...] = jnp.zeros(dqa_ref.shape, F32)
        pg_ref[...] = jnp.zeros(pg_ref.shape, F32)
        for st in range(NS):
            pl_ref[st] = jnp.broadcast_to(rt_ref[0, st], (tq, LANES))

        def block(kb, diag):
            ks = pl.multiple_of(kb * tk, tk)
            rows = pl.ds(ks, tk)
            kblk = _kv_blocks(k_ref, ks, tk, G)
            vblk = _kv_blocks(v_ref, ks, tk, G)
            zs = [lax.dot_general(qh[st], kblk[st // 2], NT, preferred_element_type=F32) for st in range(NS)]
            das = [lax.dot_general(doh[st], vblk[st // 2], NT, preferred_element_type=F32) for st in range(NS)]
            lss, parts = [], []
            for st in range(NS):
                ls, lm = _sb_logs_z(zs[st])
                if diag:
                    lm = jnp.where(below, lm, 0.0)
                lss.append((ls, ls + _wide(pl_ref[st], tk)))
                pl_ref[st] -= jnp.sum(lm, axis=-1, keepdims=True)
                parts.append(_split2(lm))
            pins = [jnp.dot(parts[st][0], upto, preferred_element_type=F32)
                    + jnp.dot(parts[st][1], upto, preferred_element_type=F32) for st in range(NS)]
            gms, ab, gparts = [], [], []
            for st in range(NS):
                a = jnp.exp(lss[st][1] - pins[st])
                if diag:
                    a = jnp.where(below, a, 0.0)
                gm = a * das[st]
                gms.append(gm)
                ab.append(a.astype(BF16))
                gparts.append(gm.astype(BF16))
            pgs = [jnp.dot(gparts[st], before, preferred_element_type=F32) for st in range(NS)]
            dzb = []
            for st in range(NS):
                gm = gms[st]
                dz = gm - jnp.exp(lss[st][0]) * (gm + (pgs[st] + _wide(pg_ref[st], tk)))
                if diag:
                    dz = jnp.where(below, dz, 0.0)
                pg_ref[st] += jnp.sum(gm, axis=-1, keepdims=True)
                dzb.append(dz.astype(BF16))
            dks = [lax.dot_general(dzb[st], qh[st], TN, preferred_element_type=F32) for st in range(NS)]
            dvs = [lax.dot_general(ab[st], doh[st], TN, preferred_element_type=F32) for st in range(NS)]
            dqs = [jnp.dot(dzb[st], kblk[st // 2], preferred_element_type=F32) for st in range(NS)]
            for g in range(G):
                dk_ref[rows, _lanes(g)] += dks[2 * g] + dks[2 * g + 1]
                dv_ref[rows, _lanes(g)] += dvs[2 * g] + dvs[2 * g + 1]
            for st in range(NS):
                dqa_ref[st] += dqs[st]

        _sweep(i, block)
        for g in range(G):
            dq_ref[:, _lanes(g)] = (jnp.where(lane < HEAD_DIM, dqa_ref[2 * g], dqa_ref[2 * g + 1]) * SCALE).astype(BF16)

    return _pc(body, name=name,
               out_shape=[_sds((B * S, P * LANES), BF16), _sds((B * S, P * LANES), F32), _sds((B * S, P * LANES), F32)],
               grid=(B, P // G, nq),
               in_specs=[_g_col_spec(tq, nq, q_cb, G), _g_kv_spec(S, k_cb, G), _g_kv_spec(S, v_cb, G),
                         _g_col_spec(tq, nq, do_cb, G), _g_stat_col_spec(tq, G)],
               out_specs=[_g_col_spec(tq, nq, 0, G), _g_kv_spec(S, 0, G), _g_kv_spec(S, 0, G)],
               scratch_shapes=[pltpu.VMEM((NS, tq, LANES), F32)] * 3,
               semantics=("arbitrary", "arbitrary", "arbitrary"), vmem=VMEM_BIG)(qa, ka, va, doa, rt)


def _shift_rows(cur, halo_ref, first, rows_idx, k):
    out = pltpu.roll(cur, k, 0)
    top = out[0:8, :]
    for r in range(k):
        hr = halo_ref.shape[0] - k + r
        edge = jnp.where(first, 0.0, halo_ref[hr:hr + 1, :])
        top = jnp.where(rows_idx[0:8, :] == r, edge, top)
    return jnp.concatenate([top, out[8:, :]], axis=0)


def _shift_rows_up(cur, halo_ref, last, rows_idx, k, ts):
    out = pltpu.roll(cur, ts - k, 0)
    bottom = out[ts - 8:, :]
    for r in range(k):
        edge = jnp.where(last, 0.0, halo_ref[r:r + 1, :])
        bottom = jnp.where(rows_idx[0:8, :] == 8 - k + r, edge, bottom)
    return jnp.concatenate([out[:ts - 8, :], bottom], axis=0)


def _conv_taps(main_ref, halo_ref, w_ref, b_ref, first, rows_idx):
    cur = main_ref[...]
    m1 = _shift_rows(cur, halo_ref, first, rows_idx, 1)
    m2 = _shift_rows(cur, halo_ref, first, rows_idx, 2)
    uc = b_ref[...] + w_ref[0:1, :] * m2 + w_ref[1:2, :] * m1 + w_ref[2:3, :] * cur
    return uc, cur, m1, m2


def _conv_specs(ts, tf, ns, nf, S, order):
    def wrap(fn):
        return lambda *g: fn(*order(*g))
    specs = []
    for half in (0, 1):
        specs.append(pl.BlockSpec((None, ts, tf), wrap(lambda b, i, j, half=half: (half, b * ns + i, j))))
        specs.append(pl.BlockSpec((None, 8, tf), wrap(
            lambda b, i, j, half=half: (half, jnp.maximum((b * S + i * ts) // 8 - 1, 0), j))))
    for off in (0, nf):
        specs.append(pl.BlockSpec((3, tf), wrap(lambda b, i, j, off=off: (0, j + off))))
    for off in (0, nf):
        specs.append(pl.BlockSpec((1, tf), wrap(lambda b, i, j, off=off: (0, j + off))))
    return specs


def _ffn_up_gate(x, g, w, cw, cb, *, name, S):
    T, D = x.shape
    F = w.shape[1] // 2
    tm = min(1024, S)
    tn = 256
    nj = F // tn
    tiles_per_seq = S // tm
    halo = 16

    def body(x_ref, xh_ref, g_ref, wg_ref, wv_ref, cwg_ref, cwv_ref, cbg_ref, cbv_ref,
             uc_ref, ub_ref, a_ref, hout_ref, h_ref, hh_ref, eg_ref, ev_ref):
        first = lax.rem(pl.program_id(0), tiles_per_seq) == 0

        @pl.when(pl.program_id(1) == 0)
        def _():
            def norm(v):
                r = lax.rsqrt(jnp.mean(v * v, axis=-1, keepdims=True) + EPS)
                return ((v * r) * g_ref[...]).astype(BF16)
            h = norm(x_ref[...])
            h_ref[...] = h
            hout_ref[...] = h
            hh_ref[...] = norm(xh_ref[...])

        h = h_ref[...]
        rows_idx = lax.broadcasted_iota(jnp.int32, (tm, tn), 0)
        uc = []
        for half, (w_ref, cw_ref, cb_ref, e_ref) in enumerate(((wg_ref, cwg_ref, cbg_ref, eg_ref),
                                                               (wv_ref, cwv_ref, cbv_ref, ev_ref))):
            acc = jnp.dot(h, w_ref[...], preferred_element_type=F32)
            e_ref[...] = jnp.dot(hh_ref[...], w_ref[...], preferred_element_type=F32)
            ub_ref[half] = acc.astype(BF16)
            m1 = _shift_rows(acc, e_ref, first, rows_idx, 1)
            m2 = _shift_rows(acc, e_ref, first, rows_idx, 2)
            uc.append(cb_ref[...] + cw_ref[0:1, :] * m2 + cw_ref[1:2, :] * m1 + cw_ref[2:3, :] * acc)
            uc_ref[half] = uc[half]
        a_ref[...] = (uc[0] * (1.0 / (1.0 + jnp.exp(-uc[0]))) * uc[1]).astype(BF16)

    in_specs = [pl.BlockSpec((tm, D), lambda i, j: (i, 0)),
                pl.BlockSpec((halo, D), lambda i, j: (jnp.maximum(i * (tm // halo) - 1, 0), 0)),
                pl.BlockSpec((1, D), lambda i, j: (0, 0)),
                pl.BlockSpec((D, tn), lambda i, j: (0, j)), pl.BlockSpec((D, tn), lambda i, j: (0, j + nj)),
                pl.BlockSpec((3, tn), lambda i, j: (0, j)), pl.BlockSpec((3, tn), lambda i, j: (0, j + nj)),
                pl.BlockSpec((1, tn), lambda i, j: (0, j)), pl.BlockSpec((1, tn), lambda i, j: (0, j + nj))]
    return _pc(body, name=name,
               out_shape=[_sds((2, T, F), F32), _sds((2, T, F), BF16), _sds((T, F), BF16), _sds((T, D), BF16)],
               grid=(T // tm, nj), in_specs=in_specs,
               out_specs=[pl.BlockSpec((2, tm, tn), lambda i, j: (0, i, j)), pl.BlockSpec((2, tm, tn), lambda i, j: (0, i, j)),
                          pl.BlockSpec((tm, tn), lambda i, j: (i, j)), pl.BlockSpec((tm, D), lambda i, j: (i, 0))],
               scratch_shapes=[pltpu.VMEM((tm, D), BF16), pltpu.VMEM((halo, D), BF16),
                               pltpu.VMEM((halo, tn), F32), pltpu.VMEM((halo, tn), F32)],
               semantics=("arbitrary", "arbitrary"), vmem=VMEM_BIG)(x, x, g.reshape(1, D), w, w, cw, cw, cb, cb)


def _conv_gate_bwd(da, uc, ub, cw, *, name, B, S):
    F = uc.shape[2]
    tf = F // 2
    ts = min(256, S)
    ns, nf = S // ts, F // tf

    def body(da_ref, uc_ref, ub_ref, wg_ref, wv_ref, dug_ref, duv_ref, pg_ref, pv_ref, nxt_g, nxt_v):
        last = pl.program_id(2) == 0

        @pl.when(jnp.logical_and(pl.program_id(1) == 0, last))
        def _():
            pg_ref[...] = jnp.zeros_like(pg_ref)
            pv_ref[...] = jnp.zeros_like(pv_ref)

        rows_idx = lax.broadcasted_iota(jnp.int32, (ts, tf), 0)
        ucg, ucv = uc_ref[0], uc_ref[1]
        sg = 1.0 / (1.0 + jnp.exp(-ucg))
        dav = da_ref[...]
        d_v = dav * (ucg * sg)
        d_g = dav * ucv * (sg * (1.0 + ucg * (1.0 - sg)))
        for half, (o_ref, p_ref, d, w_ref, nxt) in enumerate(((dug_ref, pg_ref, d_g, wg_ref, nxt_g),
                                                               (duv_ref, pv_ref, d_v, wv_ref, nxt_v))):
            p1 = _shift_rows_up(d, nxt, last, rows_idx, 1, ts)
            p2 = _shift_rows_up(d, nxt, last, rows_idx, 2, ts)
            o_ref[...] = (w_ref[2:3, :] * d + w_ref[1:2, :] * p1 + w_ref[0:1, :] * p2).astype(BF16)
            nxt[...] = d[0:8, :]
            uh = ub_ref[half].astype(F32)
            for k, dk in enumerate((p2, p1, d)):
                p_ref[k:k + 1, :] += jnp.sum(dk * uh, axis=0, keepdims=True)
            p_ref[3:4, :] += jnp.sum(d, axis=0, keepdims=True)

    row = pl.BlockSpec((ts, tf), lambda j, b, r: (b * ns + ns - 1 - r, j))
    both = pl.BlockSpec((2, ts, tf), lambda j, b, r: (0, b * ns + ns - 1 - r, j))
    par = pl.BlockSpec((8, tf), lambda j, b, r: (0, j))
    return _pc(body, name=name,
               out_shape=[_sds((B * S, F), BF16), _sds((B * S, F), BF16), _sds((8, F), F32), _sds((8, F), F32)],
               grid=(nf, B, ns),
               in_specs=[row, both, both, pl.BlockSpec((3, tf), lambda j, b, r: (0, j)),
                         pl.BlockSpec((3, tf), lambda j, b, r: (0, j + nf))],
               out_specs=[row, row, par, par],
               scratch_shapes=[pltpu.VMEM((8, tf), F32), pltpu.VMEM((8, tf), F32)],
               semantics=("arbitrary", "arbitrary", "arbitrary"), vmem=VMEM_BIG)(da, uc, ub, cw, cw)


def _conv_transpose(d, cw, *, name, B, S, col_off):
    F = d.shape[1]
    tf = F // 2
    ts = min(256, S)
    ns, nf = S // ts, F // tf
    nblk8 = B * S // 8

    def body(d_ref, dh_ref, w_ref, o_ref):
        last = pl.program_id(1) == ns - 1
        rows_idx = lax.broadcasted_iota(jnp.int32, (ts, tf), 0)
        cur = d_ref[...]
        p1 = _shift_rows_up(cur, dh_ref, last, rows_idx, 1, ts)
        p2 = _shift_rows_up(cur, dh_ref, last, rows_idx, 2, ts)
        o_ref[...] = (w_ref[2:3, :] * cur + w_ref[1:2, :] * p1 + w_ref[0:1, :] * p2).astype(BF16)

    return _pc(body, name=name, out_shape=_sds((B * S, F), BF16), grid=(B, ns, nf),
               in_specs=[pl.BlockSpec((ts, tf), lambda b, i, j: (b * ns + i, j)),
                         pl.BlockSpec((8, tf), lambda b, i, j: (jnp.minimum((b * S + (i + 1) * ts) // 8, nblk8 - 1), j)),
                         pl.BlockSpec((3, tf), lambda b, i, j: (0, j + col_off * nf))],
               out_specs=pl.BlockSpec((ts, tf), lambda b, i, j: (b * ns + i, j)),
               semantics=("arbitrary", "arbitrary", "arbitrary"))(d, d, cw)


def _adamw(w, g, m, v, *, name):
    rows, cols = w.shape
    tr = rows
    while tr * cols * 4 > 2 ** 20 and tr % 16 == 0:
        tr //= 2

    def body(w_ref, g_ref, m_ref, v_ref, d_ref, nm_ref, nv_ref):
        gv = g_ref[...]
        m_new = ADAM_B1 * m_ref[...] + (1.0 - ADAM_B1) * gv
        v_new = ADAM_B2 * v_ref[...] + (1.0 - ADAM_B2) * (gv * gv)
        m_hat = m_new / (1.0 - ADAM_B1 ** ADAM_STEP)
        v_hat = v_new / (1.0 - ADAM_B2 ** ADAM_STEP)
        d_ref[...] = -ADAM_LR * (m_hat / (jnp.sqrt(v_hat) + ADAM_EPS) + ADAM_WD * w_ref[...])
        nm_ref[...] = m_new
        nv_ref[...] = v_new

    blk = pl.BlockSpec((tr, cols), lambda i: (i, 0))
    return _pc(body, name=name, out_shape=[_sds((rows, cols), F32)] * 3, grid=(rows // tr,),
               in_specs=[blk] * 4, out_specs=[blk] * 3, semantics=("arbitrary",))(w, g, m, v)


def _my_pos():
    return lax.axis_index("x"), lax.axis_index("y"), lax.axis_index("c")


_HBM = pl.BlockSpec(memory_space=pltpu.HBM)
_SEM = pl.BlockSpec(memory_space=pltpu.SEMAPHORE)
_EFFECT = pltpu.SideEffectType.DATAFLOW_SIDE_EFFECTING


def _peers():
    x, y, c = _my_pos()
    out = []
    for k in range(1, N_DEV):
        px, py, pc = x ^ ((k >> 2) & 1), y ^ ((k >> 1) & 1), c ^ (k & 1)
        out.append(((px, py, pc), 4 * px + 2 * py + pc))
    return out


def _scatter_start(srcs, slot_of, *, name, order_after=None):
    n = len(srcs)
    lands = [lax.empty((N_DEV,) + slot_of(s, 0, shape_only=True), s.dtype) for s in srcs]
    extra = [] if order_after is None else [order_after]

    def body(*refs):
        src_refs, land_refs = refs[:n], refs[n:2 * n]
        send_sems, recv_sems = refs[2 * n + len(extra)], refs[2 * n + len(extra) + 1]
        token = refs[-1]
        x, y, c = _my_pos()
        me = 4 * x + 2 * y + c
        for a in range(n):
            for k, (peer, peer_idx) in enumerate(_peers()):
                pltpu.make_async_remote_copy(
                    src_ref=slot_of(src_refs[a], peer_idx), dst_ref=land_refs[a].at[me],
                    send_sem=send_sems.at[a * 7 + k], recv_sem=recv_sems.at[a * 7 + k],
                    device_id=peer, device_id_type=MESH).start()
        token[...] = jnp.zeros_like(token)

    hbm = lambda a: pltpu.HBM(a.shape, a.dtype)
    args = [pltpu.with_memory_space_constraint(a, pltpu.HBM) for a in list(srcs) + lands] + extra
    outs = pl.pallas_call(
        body, name=name,
        out_shape=(pltpu.SemaphoreType.DMA((7 * n,)), pltpu.SemaphoreType.DMA((7 * n,)),
                   *[hbm(a) for a in srcs], *[hbm(a) for a in lands], _sds((8, LANES), F32)),
        in_specs=[_HBM] * (2 * n) + [pl.BlockSpec(memory_space=pl.ANY)] * len(extra),
        out_specs=(_SEM, _SEM, *([_HBM] * (2 * n)), pl.BlockSpec(memory_space=pltpu.VMEM)),
        input_output_aliases={a: 2 + a for a in range(2 * n)},
        compiler_params=pltpu.CompilerParams(has_side_effects=_EFFECT))(*args)
    return outs[0], outs[1], list(outs[2:2 + n]), list(outs[2 + n:2 + 2 * n]), outs[-1]


def _scatter_wait(send_sems, recv_sems, srcs, lands, slot_of, after, *, name):
    n = len(srcs)

    def body(*refs):
        src_refs, land_refs = refs[:n], refs[n:2 * n]
        ssem, rsem = refs[2 * n], refs[2 * n + 1]
        x, y, c = _my_pos()
        me = 4 * x + 2 * y + c
        for a in range(n):
            for k, (peer, peer_idx) in enumerate(_peers()):
                cp = pltpu.make_async_remote_copy(
                    src_ref=slot_of(src_refs[a], peer_idx), dst_ref=land_refs[a].at[me],
                    send_sem=ssem.at[a * 7 + k], recv_sem=rsem.at[a * 7 + k],
                    device_id=peer, device_id_type=MESH)
                cp.wait_send()
                cp.wait_recv()

    hbm = lambda a: pltpu.HBM(a.shape, a.dtype)
    outs = pl.pallas_call(
        body, name=name, out_shape=tuple(hbm(a) for a in list(srcs) + list(lands)),
        in_specs=[_HBM] * (2 * n) + [_SEM, _SEM, pl.BlockSpec(memory_space=pl.ANY)],
        out_specs=tuple([_HBM] * (2 * n)), input_output_aliases={a: a for a in range(2 * n)},
        compiler_params=pltpu.CompilerParams(has_side_effects=_EFFECT))(*srcs, *lands, send_sems, recv_sems, after)
    return list(outs[:n]), list(outs[n:])


def _whole(a, peer_idx, shape_only=False):
    return a.shape if shape_only else a


def _slot(a, peer_idx, shape_only=False):
    return a.shape[1:] if shape_only else a.at[peer_idx]


def _all_gather(shard, *, name):
    rows = shard.shape[0]

    def body(x_ref, out_ref, send_sems, recv_sems, local_sem):
        x, y, c = _my_pos()
        me, sibling = (x, y, c), (x, y, 1 - c)
        chips = [(1 - x, y), (x, 1 - y), (1 - x, 1 - y)]

        def slot(px, py, pc):
            return out_ref.at[4 * px + 2 * py + pc]

        def copy(k, block, to, src=None):
            return pltpu.make_async_remote_copy(
                src_ref=slot(*block) if src is None else src, dst_ref=slot(*block),
                send_sem=send_sems.at[k], recv_sem=recv_sems.at[k], device_id=to, device_id_type=MESH)

        mine = pltpu.make_async_copy(x_ref, slot(*me), local_sem)
        mine.start()
        first = [copy(0, me, sibling, src=x_ref)]
        first += [copy(1 + j, me, (*chip, c), src=x_ref) for j, chip in enumerate(chips)]
        for cp in first:
            cp.start()
        passed = [copy(4 + j, (*chip, c), sibling) for j, chip in enumerate(chips)]
        for j, chip in enumerate(chips):
            copy(1 + j, (*chip, c), me).wait_recv()
            passed[j].start()
        copy(0, sibling, me).wait_recv()
        for j, chip in enumerate(chips):
            copy(4 + j, (*chip, 1 - c), me).wait_recv()
        for cp in first + passed:
            cp.wait_send()
        mine.wait()

    return _pc(body, name=name, out_shape=_sds((N_DEV, rows, LANES), shard.dtype),
               in_specs=[pl.BlockSpec(memory_space=pl.ANY)], out_specs=pl.BlockSpec(memory_space=pl.ANY),
               scratch_shapes=[pltpu.SemaphoreType.DMA((7,)), pltpu.SemaphoreType.DMA((7,)),
                               pltpu.SemaphoreType.DMA])(shard)


def _exchange(big, small, *, name):
    rs = small.shape[0]

    def body(big_ref, small_ref, bout_ref, sout_ref, send_sems, recv_sems, local_sems):
        x, y, c = _my_pos()
        me = 4 * x + 2 * y + c
        lb = pltpu.make_async_copy(big_ref.at[me], bout_ref.at[me], local_sems.at[0])
        ls = pltpu.make_async_copy(small_ref, sout_ref.at[me], local_sems.at[1])
        lb.start()
        ls.start()
        copies = []
        for k in range(1, N_DEV):
            px = x ^ ((k >> 2) & 1)
            py = y ^ ((k >> 1) & 1)
            pc = c ^ (k & 1)
            peer = 4 * px + 2 * py + pc
            copies.append(pltpu.make_async_remote_copy(
                src_ref=big_ref.at[peer], dst_ref=bout_ref.at[me], send_sem=send_sems.at[k - 1],
                recv_sem=recv_sems.at[k - 1], device_id=(px, py, pc), device_id_type=MESH))
            copies.append(pltpu.make_async_remote_copy(
                src_ref=small_ref, dst_ref=sout_ref.at[me], send_sem=send_sems.at[7 + k - 1],
                recv_sem=recv_sems.at[7 + k - 1], device_id=(px, py, pc), device_id_type=MESH))
        for cp in copies:
            cp.start()
        for cp in copies:
            cp.wait()
        lb.wait()
        ls.wait()

    return _pc(body, name=name,
               out_shape=[_sds(big.shape, big.dtype), _sds((N_DEV, rs, LANES), F32)],
               in_specs=[pl.BlockSpec(memory_space=pl.ANY), pl.BlockSpec(memory_space=pl.ANY)],
               out_specs=[pl.BlockSpec(memory_space=pl.ANY), pl.BlockSpec(memory_space=pl.ANY)],
               scratch_shapes=[pltpu.SemaphoreType.DMA((14,)), pltpu.SemaphoreType.DMA((14,)),
                               pltpu.SemaphoreType.DMA((2,))])(big, small)


def _all_gather_small(small, *, name):
    rs = small.shape[0]

    def body(small_ref, out_ref, send_sems, recv_sems, local_sem):
        x, y, c = _my_pos()
        me = 4 * x + 2 * y + c
        mine = pltpu.make_async_copy(small_ref, out_ref.at[me], local_sem)
        mine.start()
        copies = [pltpu.make_async_remote_copy(
            src_ref=small_ref, dst_ref=out_ref.at[me], send_sem=send_sems.at[k], recv_sem=recv_sems.at[k],
            device_id=peer, device_id_type=MESH) for k, (peer, _) in enumerate(_peers())]
        for cp in copies:
            cp.start()
        for cp in copies:
            cp.wait()
        mine.wait()

    return _pc(body, name=name, out_shape=_sds((N_DEV, rs, LANES), F32),
               in_specs=[pl.BlockSpec(memory_space=pl.ANY)], out_specs=pl.BlockSpec(memory_space=pl.ANY),
               scratch_shapes=[pltpu.SemaphoreType.DMA((7,)), pltpu.SemaphoreType.DMA((7,)),
                               pltpu.SemaphoreType.DMA])(small)


def _sum_slots(a, *, name, tr=None):
    rows, cols = a.shape[1], a.shape[2]
    if tr is None:
        tr = rows
        while N_DEV * tr * cols * a.dtype.itemsize > 3 * 2 ** 20 and tr % 32 == 0:
            tr //= 2

    def body(a_ref, o_ref):
        acc = a_ref[0].astype(F32)
        for j in range(1, N_DEV):
            acc = acc + a_ref[j].astype(F32)
        o_ref[...] = acc

    return _pc(body, name=name, out_shape=_sds((rows, cols), F32), grid=(rows // tr,),
               in_specs=[pl.BlockSpec((N_DEV, tr, cols), lambda i: (0, i, 0))],
               out_specs=pl.BlockSpec((tr, cols), lambda i: (i, 0)), semantics=("arbitrary",), vmem=VMEM_BIG)(a)


PACK_ROWS = 25600
SUM_TILE = 512


def _rows128(a):
    return a.reshape(-1, LANES)


def _to_slots(full, kind):
    if kind == "rows2":
        r, c = full.shape
        return full.reshape(N_DEV, r // N_DEV, c)
    if kind == "cols2":
        r, c = full.shape
        return full.reshape(r, N_DEV, c // N_DEV).transpose(1, 0, 2)
    if kind == "rows3":
        l, r, c = full.shape
        return full.reshape(l, N_DEV, r // N_DEV, c).transpose(1, 0, 2, 3)
    if kind == "cols3":
        l, r, c = full.shape
        return full.reshape(l, r, N_DEV, c // N_DEV).transpose(2, 0, 1, 3)
    raise ValueError(kind)


def _from_slots(slots, kind):
    if kind == "rows2":
        _, r, c = slots.shape
        return slots.reshape(N_DEV * r, c)
    if kind == "cols2":
        _, r, c = slots.shape
        return slots.transpose(1, 0, 2).reshape(r, N_DEV * c)
    if kind == "rows3":
        _, l, r, c = slots.shape
        return slots.transpose(1, 0, 2, 3).reshape(l, N_DEV * r, c)
    if kind == "cols3":
        _, l, r, c = slots.shape
        return slots.transpose(1, 2, 0, 3).reshape(l, r, N_DEV * c)
    raise ValueError(kind)


BIG = (("w_in_a", "rows2"), ("w_in_b", "rows2"), ("w_kv", "cols2"), ("w_memkv", "rows3"),
       ("w_out", "rows3"), ("w_up", "cols3"), ("w_down", "rows3"))


def _round_up(n, m):
    return -(-n // m) * m


def _pad_rows(a, rows, axis):
    pad = [(0, 0)] * a.ndim
    pad[axis] = (0, rows - a.shape[axis])
    return jnp.pad(a, pad)


def kernel(x, mem, ln_mix_g, w_in_a, b_f_a, w_in_b, ln_kv_g, w_kv, ln_mem_g, w_memkv, w_out, ln_ffn_g, w_up, conv_w, conv_b, w_down, final_g, loss_target, m_ln_mix_g, m_w_in_a, m_b_f_a, m_w_in_b, m_ln_kv_g, m_w_kv, m_ln_mem_g, m_w_memkv, m_w_out, m_ln_ffn_g, m_w_up, m_conv_w, m_conv_b, m_w_down, m_final_g, v_ln_mix_g, v_w_in_a, v_b_f_a, v_w_in_b, v_ln_kv_g, v_w_kv, v_ln_mem_g, v_w_memkv, v_w_out, v_ln_ffn_g, v_w_up, v_conv_w, v_conv_b, v_w_down, v_final_g):
    B, S, D = x.shape
    NM = mem.shape[1]
    T = B * S
    F = w_down.shape[1] * N_DEV
    my_idx = 4 * lax.axis_index("x") + 2 * lax.axis_index("y") + lax.axis_index("c")

    shards = {"w_in_a": w_in_a[0], "w_in_b": w_in_b[0], "w_kv": w_kv, "w_memkv": w_memkv, "w_out": w_out,
              "w_up": w_up, "w_down": w_down}
    moms = {"w_in_a": (m_w_in_a[0], v_w_in_a[0]), "w_in_b": (m_w_in_b[0], v_w_in_b[0]), "w_kv": (m_w_kv, v_w_kv),
            "w_memkv": (m_w_memkv, v_w_memkv), "w_out": (m_w_out, v_w_out), "w_up": (m_w_up, v_w_up),
            "w_down": (m_w_down, v_w_down)}

    groups = [("a1", [("w_in_a", None)]),
              ("a2", [("w_memkv", None), ("w_out", None), ("conv_w", None)]),
              ("b0", [("w_up", 0), ("w_down", 0)]), ("a3", [("w_in_b", None), ("w_kv", None)]),
              ("b1", [("w_up", 1), ("w_down", 1)])]
    sources = dict(shards, conv_w=conv_w)
    started, token = {}, None
    for gname, members in groups:
        srcs = []
        for n, layer in members:
            a = sources[n] if layer is None else sources[n][layer]
            srcs.append(a if n == "conv_w" else a.astype(BF16))
        ssem, rsem, thru, lands, token = _scatter_start(srcs, _whole, name=f"gather_start_{gname}", order_after=token)
        started[gname] = (ssem, rsem, thru, lands)

    def gathered(gname, after):
        ssem, rsem, thru, lands = started[gname]
        thru, lands = _scatter_wait(ssem, rsem, thru, lands, _whole, after, name=f"gather_wait_{gname}")
        return [lax.dynamic_update_index_in_dim(land, s, my_idx, 0) for land, s in zip(lands, thru)]

    full = {}
    (g_wa,) = gathered("a1", token)
    full["w_in_a"] = _from_slots(g_wa, "rows2")

    wa = full["w_in_a"]
    n_qkv = 3 * MAIN_W
    wa = jnp.concatenate([wa[:, :n_qkv], wa[:, n_qkv + N_MAIN_HEADS:], wa[:, n_qkv:n_qkv + N_MAIN_HEADS],
                          jnp.zeros((D, LANES - N_MAIN_HEADS), BF16)], axis=1)
    n_main = n_qkv + MEM_W
    full["w_up"], full["w_down"] = {}, {}
    b_f =_pad_rows(b_f_a.reshape(1, N_MAIN_HEADS), LANES, 1)

    x2d = x.reshape(T, D)
    mem2d = mem.reshape(B * NM, D)
    tgt2d = loss_target.reshape(T, D)
    PM, PX = N_MAIN_HEADS // 2, N_MEM_HEADS // 2

    def stats_to_heads(c2d):
        c = c2d.reshape(B, S, LANES)[:, :, :N_MAIN_HEADS].transpose(0, 2, 1)
        return c[:, :, None, :]

    def mem_kv(layer):
        return _mm_fwd(mem2d, full["w_memkv"][layer], name=f"memkv{layer}", tm=B * NM, tn=2 * MEM_W,
                       out_dtype=BF16, g=ln_mem_g[layer], save_h=True)

    def conv_ffn_fwd(xin, layer):
        uc, ub, a, h = _ffn_up_gate(xin, ln_ffn_g[layer], full["w_up"][layer], conv_w_full[layer],
                                    conv_b[layer].reshape(1, 2 * F), name=f"ffn_up{layer}", S=S)
        xo = _mm_fwd(a, full["w_down"][layer], name=f"ffn_down{layer}", tm=min(512, T), tn=512, out_dtype=F32, res=xin)
        return xo, (uc, ub, h, a)

    proj_a, h_mix0 = _mm_fwd(x2d, wa, name="in_proj_a", tm=min(1024, T), tn=512, out_dtype=BF16, g=ln_mix_g[0],
                             ncols=n_main, save_h=True)
    f_logit = _mm_fwd(x2d, wa, name="in_proj_f", tm=min(1024, T), tn=LANES, out_dtype=F32, g=ln_mix_g[0],
                      col0=n_main // LANES, ncols=LANES)
    c2d = _forget_cumsum(f_logit, b_f, B=B, S=S, name="forget_cumsum")
    cr = stats_to_heads(c2d)
    o_main0, lse0 = _fox_fwd_g(proj_a, proj_a, proj_a, cr, name="fox_fwd", B=B, S=S, P=PM, q_cb=0, k_cb=PM, v_cb=2 * PM)
    g_wmem, g_wout, g_cw = gathered("a2", lse0)
    full["w_memkv"] = _from_slots(g_wmem, "rows3")
    full["w_out"] = _from_slots(g_wout, "rows3")
    conv_w_full = _from_slots(g_cw, "cols3")
    memkv0, h_mem0 = mem_kv(0)
    o_mem0, lse_m0 = _softmax_fwd(proj_a, memkv0, memkv0, name="mem_fwd0", B=B, S=S, Sk=NM, P=PX, q_cb=3 * PM,
                                  k_cb=0, v_cb=PX, causal=False)
    o_cat0 = jnp.concatenate([o_main0, o_mem0], axis=1)
    x1 = _mm_fwd(o_cat0, full["w_out"][0], name="out_proj0", tm=min(512, T), tn=512, out_dtype=F32, res=x2d)
    g_up, g_dn = gathered("b0", x1)
    full["w_up"][0], full["w_down"][0] = _from_slots(g_up, "cols2"), _from_slots(g_dn, "rows2")
    x2, ffn_saved0 = conv_ffn_fwd(x1, 0)
    g_wb, g_wkv = gathered("a3", x2)
    wb, wkv = _from_slots(g_wb, "rows2"), _from_slots(g_wkv, "cols2")
    kv, h_kv =_mm_fwd(x2, wkv, name="kv_proj", tm=min(1024, T), tn=512, out_dtype=BF16, g=ln_kv_g, save_h=True)
    proj_b, h_mix1 = _mm_fwd(x2, wb, name="in_proj_b", tm=min(1024, T), tn=512, out_dtype=BF16, g=ln_mix_g[1],
                             save_h=True)
    o_main1, rt1 = _sb_fwd_g(proj_b, kv, kv, name="sb_fwd", B=B, S=S, P=PM, q_cb=0, k_cb=0, v_cb=PM)
    memkv1, h_mem1 = mem_kv(1)
    o_mem1, lse_m1 = _softmax_fwd(proj_b, memkv1, memkv1, name="mem_fwd1", B=B, S=S, Sk=NM, P=PX, q_cb=PM,
                                  k_cb=0, v_cb=PX, causal=False)
    o_cat1 = jnp.concatenate([o_main1, o_mem1], axis=1)
    x3 = _mm_fwd(o_cat1, full["w_out"][1], name="out_proj1", tm=min(512, T), tn=512, out_dtype=F32, res=x2)
    g_up, g_dn = gathered("b1", x3)
    full["w_up"][1], full["w_down"][1] = _from_slots(g_up, "cols2"), _from_slots(g_dn, "rows2")
    x4, ffn_saved1 = conv_ffn_fwd(x3, 1)
    dx4, dg_final, loss_part = _loss_head(x4, final_g, tgt2d, name="loss_head")

    grads = {}
    small = {}
    reduce_groups = []

    def start_reduce(gname, keys, kinds):
        slots = [_to_slots(grads[k], kind) for k, kind in zip(keys, kinds)]
        ssem, rsem, thru, lands, tok = _scatter_start(slots, _slot, name=f"reduce_start_{gname}")
        reduce_groups.append((gname, keys, ssem, rsem, thru, lands))
        return tok[0, 0]

    def conv_ffn_bwd(dxo, xin, saved, layer):
        uc, ub, h, a = saved
        w_dn = full["w_down"][layer]
        da = _mm_nt(dxo, w_dn, name=f"d_act{layer}", tm=min(512, T), tn=F // 2, out_dtype=F32)
        grads[("w_down", layer)] = _wgrad(a, dxo, f"g_w_down{layer}")
        cwl = conv_w_full[layer]
        du_g, du_v, p_g, p_v = _conv_gate_bwd(da, uc, ub, cwl, name=f"conv_bwd{layer}", B=B, S=S)
        small[("conv_w", layer)] = jnp.concatenate([p_g[0:3], p_v[0:3]], axis=1)
        small[("conv_b", layer)] = jnp.concatenate([p_g[3], p_v[3]], axis=0)
        grads[("w_up", layer)] = jnp.concatenate(
            [_wgrad(h, du_g, f"g_w_up_gate{layer}"), _wgrad(h, du_v, f"g_w_up_val{layer}")], axis=1)
        tok = start_reduce(f"ffn{layer}", [("w_down", layer), ("w_up", layer)], ["rows2", "cols2"])
        dxi, dg = _mm_nt_rmsbwd([(du_g, 0), (du_v, 1)], full["w_up"][layer], xin, ln_ffn_g[layer] + tok,
                                name=f"d_ffn_in{layer}", dres=dxo)
        small[("ln_ffn_g", layer)] = dg[0]
        return dxi

    def mem_bwd(proj, q_cb, memkv, h_mem, do_cat, o_mem, lse_m, layer):
        dqm, dmk, dmv = _softmax_bwd(proj, memkv, memkv, do_cat, o_mem, lse_m, name=f"mem_bwd{layer}", B=B, S=S,
                                     Sk=NM, P=PX, q_cb=q_cb, k_cb=0, v_cb=PX, do_cb=PM, causal=False)
        grads[("w_memkv", layer)] = jnp.concatenate(
            [_wgrad(h_mem, dmk, f"g_w_memk{layer}"), _wgrad(h_mem, dmv, f"g_w_memv{layer}")], axis=1)
        _, dg = _mm_nt_rmsbwd([(dmk, 0), (dmv, 1)], full["w_memkv"][layer], mem2d, ln_mem_g[layer],
                              name=f"d_mem_in{layer}", want_dx=False)
        small[("ln_mem_g", layer)] = dg[0]
        return dqm

    dx3 = conv_ffn_bwd(dx4, x3, ffn_saved1, 1)
    do_cat1 = _mm_nt(dx3, full["w_out"][1], name="d_o_cat1", tm=min(512, T), tn=512, out_dtype=BF16)
    grads[("w_out", 1)] = _wgrad(o_cat1, dx3, "g_w_out1")
    dq1, dk1, dv1 = _sb_bwd_g(proj_b, kv, kv, do_cat1, rt1, name="sb_bwd", B=B, S=S, P=PM, q_cb=0, k_cb=0, v_cb=PM,
                            do_cb=0)
    dqm1 = mem_bwd(proj_b, PM, memkv1, h_mem1, do_cat1, o_mem1, lse_m1, 1)
    grads["w_in_b"] = jnp.concatenate([_wgrad(h_mix1, dq1, "g_w_in_b_q"), _wgrad(h_mix1, dqm1, "g_w_in_b_m")], axis=1)
    grads["w_kv"] = jnp.concatenate([_wgrad(h_kv, dk1, "g_w_kv_k"), _wgrad(h_kv, dv1, "g_w_kv_v")], axis=1)
    tok = start_reduce("mix1", [("w_out", 1), "w_in_b", "w_kv", ("w_memkv", 1)], ["rows2", "rows2", "cols2", "rows2"])
    dx2, dg = _mm_nt_rmsbwd([(dq1, 0), (dqm1, MAIN_W // MEM_W)], wb, x2, ln_mix_g[1] + tok, name="d_mix_in1", dres=dx3)
    small[("ln_mix_g", 1)] = dg[0]
    dx2, dg = _mm_nt_rmsbwd([(dk1, 0), (dv1, 1)], wkv, x2, ln_kv_g, name="d_kv_in", dres=dx2)
    small["ln_kv_g"] = dg[0]
    dx1 = conv_ffn_bwd(dx2, x1, ffn_saved0, 0)
    do_cat0 = _mm_nt(dx1, full["w_out"][0], name="d_o_cat0", tm=min(512, T), tn=512, out_dtype=BF16)
    grads[("w_out", 0)] = _wgrad(o_cat0, dx1, "g_w_out0")
    dq0, dk0, dv0, dcs = _fox_bwd_g(proj_a, proj_a, proj_a, do_cat0, lse0, cr, name="fox_bwd", B=B, S=S, P=PM, q_cb=0,
                                  k_cb=PM, v_cb=2 * PM, do_cb=0)
    dqm0 = mem_bwd(proj_a, 3 * PM, memkv0, h_mem0, do_cat0, o_mem0, lse_m0, 0)
    dc2d = _pad_rows(dcs[:, :, 0, :].transpose(0, 2, 1).reshape(T, N_MAIN_HEADS), LANES, 1)
    df, db_f = _forget_cumsum_bwd(dc2d, f_logit, b_f, B=B, S=S, name="forget_cumsum_bwd")
    a_parts = [(dq0, 0), (dk0, 1), (dv0, 2), (dqm0, n_qkv // MEM_W), (df, n_main // LANES)]
    g_wa = jnp.concatenate([_wgrad(h_mix0, p, f"g_w_in_a{k}") for k, (p, _) in enumerate(a_parts)], axis=1)
    grads["w_in_a"] = jnp.concatenate([g_wa[:, :n_qkv], g_wa[:, n_main:n_main + N_MAIN_HEADS], g_wa[:, n_qkv:n_main]],
                                      axis=1)
    tok = start_reduce("mix0", [("w_out", 0), ("w_memkv", 0), "w_in_a"], ["rows2", "rows2", "rows2"])
    dx0, dg = _mm_nt_rmsbwd(a_parts, wa, x2d, ln_mix_g[0] + tok, name="d_mix_in0", dres=dx1)
    small[("ln_mix_g", 0)] = dg[0]
    grad_x = dx0.reshape(B, S, D)

    def both_small(name):
        return jnp.stack([small[(name, 0)], small[(name, 1)]])

    small_list = [("ln_mix_g", both_small("ln_mix_g")), ("b_f_a", db_f[:, :N_MAIN_HEADS]), ("ln_kv_g", small["ln_kv_g"]),
                  ("ln_mem_g", both_small("ln_mem_g")), ("ln_ffn_g", both_small("ln_ffn_g")),
                  ("conv_w", both_small("conv_w")), ("conv_b", both_small("conv_b")), ("final_g", dg_final[0]),
                  ("loss", loss_part[0, :1])]
    sm_rows = []
    for _, a in small_list:
        flat = a.reshape(-1)
        sm_rows.append(_pad_rows(flat, _round_up(flat.size, 8 * LANES), 0).reshape(-1, LANES))
    spack = jnp.concatenate(sm_rows, axis=0)
    s_ssem, s_rsem, s_thru, s_lands, s_tok = _scatter_start([spack], _whole, name="small_start")

    pieces = {}
    for gname, keys, ssem, rsem, thru, lands in reduce_groups:
        thru, lands = _scatter_wait(ssem, rsem, thru, lands, _slot, s_tok, name=f"reduce_wait_{gname}")
        for key, mine, land in zip(keys, thru, lands):
            own = lax.dynamic_index_in_dim(mine, my_idx, 0, keepdims=False)
            land = lax.dynamic_update_index_in_dim(land, own, my_idx, 0)
            tag = key if isinstance(key, str) else f"{key[0]}{key[1]}"
            pieces[key] = _sum_slots(land, name=f"sum_{tag}")

    red = {}
    for n in ("w_in_a", "w_in_b", "w_kv"):
        red[n] = pieces[n].reshape(shards[n].shape)
    for n in ("w_memkv", "w_out", "w_up", "w_down"):
        red[n] = jnp.stack([pieces[(n, 0)], pieces[(n, 1)]])

    weights = {"ln_mix_g": ln_mix_g, "w_in_a": w_in_a, "b_f_a": b_f_a, "w_in_b": w_in_b, "ln_kv_g": ln_kv_g,
               "w_kv": w_kv, "ln_mem_g": ln_mem_g, "w_memkv": w_memkv, "w_out": w_out, "ln_ffn_g": ln_ffn_g,
               "w_up": w_up, "conv_w": conv_w, "conv_b": conv_b, "w_down": w_down, "final_g": final_g}
    m_in = {"ln_mix_g": m_ln_mix_g, "w_in_a": m_w_in_a, "b_f_a": m_b_f_a, "w_in_b": m_w_in_b, "ln_kv_g": m_ln_kv_g,
            "w_kv": m_w_kv, "ln_mem_g": m_ln_mem_g, "w_memkv": m_w_memkv, "w_out": m_w_out, "ln_ffn_g": m_ln_ffn_g,
            "w_up": m_w_up, "conv_w": m_conv_w, "conv_b": m_conv_b, "w_down": m_w_down, "final_g": m_final_g}
    v_in = {"ln_mix_g": v_ln_mix_g, "w_in_a": v_w_in_a, "b_f_a": v_b_f_a, "w_in_b": v_w_in_b, "ln_kv_g": v_ln_kv_g,
            "w_kv": v_w_kv, "ln_mem_g": v_ln_mem_g, "w_memkv": v_w_memkv, "w_out": v_w_out, "ln_ffn_g": v_ln_ffn_g,
            "w_up": v_w_up, "conv_w": v_conv_w, "conv_b": v_conv_b, "w_down": v_w_down, "final_g": v_final_g}
    order = list(weights)
    big_names = [n for n, _ in BIG]
    g_out, d_out, nm_out, nv_out = {}, {}, {}, {}

    def update(n):
        w = weights[n]
        cols = w.shape[-1]
        g = red[n].reshape(w.shape)
        d, nm, nv = _adamw(w.reshape(-1, cols), g.reshape(-1, cols), m_in[n].reshape(-1, cols),
                           v_in[n].reshape(-1, cols), name=f"adamw_{n}")
        g_out[n], d_out[n], nm_out[n], nv_out[n] = g, d.reshape(w.shape), nm.reshape(w.shape), nv.reshape(w.shape)

    for n in big_names:
        update(n)
    all_updated = jnp.stack([d_out[n].reshape(-1)[0] for n in big_names])
    s_thru, s_lands = _scatter_wait(s_ssem, s_rsem, s_thru, s_lands, _whole, all_updated, name="small_wait")
    ssum = _sum_slots(lax.dynamic_update_index_in_dim(s_lands[0], s_thru[0], my_idx, 0), name="sum_small")
    off = 0
    for (n, a), rows in zip(small_list, sm_rows):
        red[n] = ssum[off:off + rows.shape[0]].reshape(-1)[:a.size].reshape(a.shape)
        off += rows.shape[0]
    loss = red["loss"][0]
    shard_cols = conv_w.shape[2]
    red["conv_w"] = lax.dynamic_slice_in_dim(red["conv_w"], my_idx * shard_cols, shard_cols, axis=2)
    red["b_f_a"] = red["b_f_a"].reshape(b_f_a.shape)
    update("conv_w")
    small_names = [n for n in order if n not in g_out]

    def pack_small(src):
        rows = []
        for n in small_names:
            flat = src[n].reshape(-1)
            rows.append(_pad_rows(flat, _round_up(flat.size, 8 * LANES), 0).reshape(-1, LANES))
        return jnp.concatenate(rows, axis=0), [r.shape[0] for r in rows]

    red_small = {n: red[n].reshape(weights[n].shape) for n in small_names}
    wp, counts = pack_small(weights)
    gp, _ = pack_small(red_small)
    mp, _ = pack_small(m_in)
    vp, _ = pack_small(v_in)
    dp, nmp, nvp = _adamw(wp, gp, mp, vp, name="adamw_small")
    off = 0
    for n, cnt in zip(small_names, counts):
        shp = weights[n].shape
        size = weights[n].size
        g_out[n] = red_small[n]
        d_out[n] = dp[off:off + cnt].reshape(-1)[:size].reshape(shp)
        nm_out[n] = nmp[off:off + cnt].reshape(-1)[:size].reshape(shp)
        nv_out[n] = nvp[off:off + cnt].reshape(-1)[:size].reshape(shp)
        off += cnt

    return (loss, grad_x, *[g_out[n] for n in order], *[d_out[n] for n in order],
            *[nm_out[n] for n in order], *[nv_out[n] for n in order])
```

```python
import functools

import jax
import jax.numpy as jnp
from jax import lax
from jax.experimental import pallas as pl
from jax.experimental.pallas import tpu as pltpu

F32 = jnp.float32
BF16 = jnp.bfloat16
LANES = 128
HEAD_DIM = 64
N_MAIN_HEADS = 12
N_MEM_HEADS = 4
MAIN_W = N_MAIN_HEADS * HEAD_DIM
MEM_W = N_MEM_HEADS * HEAD_DIM
SCALE = HEAD_DIM ** -0.5
EPS = 1e-6
NEG = -1e30
N_DEV = 8
ATT_TILE = 256
MEM_Q_TILE = 1024
VMEM_BIG = 56 * 2 ** 20
MESH = pl.DeviceIdType.MESH

ADAM_LR = 0.001
ADAM_B1 = 0.9
ADAM_B2 = 0.999
ADAM_EPS = 1e-08
ADAM_WD = 0.01
ADAM_STEP = 10

NT = (((1,), (1,)), ((), ()))
TN = (((0,), (0,)), ((), ()))


def _pc(body, *, name, out_shape, grid=None, in_specs=None, out_specs=None, scratch_shapes=(),
        semantics=None, vmem=None):
    kw = {}
    if grid is not None:
        kw["grid"] = grid
    params = pltpu.CompilerParams(dimension_semantics=semantics, vmem_limit_bytes=vmem)
    return pl.pallas_call(body, name=name, out_shape=out_shape, in_specs=in_specs, out_specs=out_specs,
                          scratch_shapes=list(scratch_shapes), compiler_params=params, **kw)


def _sds(shape, dtype):
    return jax.ShapeDtypeStruct(shape, dtype)


def _mm_fwd(a, w, *, name, tm, tn, out_dtype, g=None, res=None, col0=0, ncols=None, save_h=False):
    m_rows, k = a.shape
    n = w.shape[1] if ncols is None else ncols
    grid = (m_rows // tm, n // tn)
    norm = g is not None

    def body(*refs):
        refs = list(refs)
        a_ref = refs.pop(0)
        g_ref = refs.pop(0) if norm else None
        w_ref = refs.pop(0)
        res_ref = refs.pop(0) if res is not None else None
        o_ref = refs.pop(0)
        hout_ref = refs.pop(0) if save_h else None
        h_ref = refs.pop(0) if norm else None
        if norm:
            @pl.when(pl.program_id(1) == 0)
            def _():
                xv = a_ref[...]
                r = lax.rsqrt(jnp.mean(xv * xv, axis=-1, keepdims=True) + EPS)
                h = ((xv * r) * g_ref[...]).astype(BF16)
                h_ref[...] = h
                if save_h:
                    hout_ref[...] = h
            lhs = h_ref[...]
        else:
            lhs = a_ref[...].astype(BF16)
        acc = jnp.dot(lhs, w_ref[...], preferred_element_type=F32)
        if res is not None:
            acc = acc + res_ref[...]
        o_ref[...] = acc.astype(out_dtype)

    in_specs = [pl.BlockSpec((tm, k), lambda i, j: (i, 0))]
    args = [a]
    if norm:
        in_specs.append(pl.BlockSpec((1, k), lambda i, j: (0, 0)))
        args.append(g.reshape(1, k))
    in_specs.append(pl.BlockSpec((k, tn), lambda i, j: (0, j + col0)))
    args.append(w)
    if res is not None:
        in_specs.append(pl.BlockSpec((tm, tn), lambda i, j: (i, j)))
        args.append(res)
    out_shape = [_sds((m_rows, n), out_dtype)]
    out_specs = [pl.BlockSpec((tm, tn), lambda i, j: (i, j))]
    if save_h:
        out_shape.append(_sds((m_rows, k), BF16))
        out_specs.append(pl.BlockSpec((tm, k), lambda i, j: (i, 0)))
    scratch = [pltpu.VMEM((tm, k), BF16)] if norm else []
    outs = _pc(body, name=name, out_shape=out_shape, grid=grid, in_specs=in_specs, out_specs=out_specs,
               scratch_shapes=scratch, semantics=("arbitrary", "arbitrary"), vmem=VMEM_BIG)(*args)
    return outs if save_h else outs[0]


def _mm_nt(a, w, *, name, tm, tn, out_dtype):
    m_rows, k = a.shape
    n = w.shape[0]

    def body(a_ref, w_ref, o_ref):
        acc = lax.dot_general(a_ref[...].astype(BF16), w_ref[...], NT, preferred_element_type=F32)
        o_ref[...] = acc.astype(out_dtype)

    return _pc(body, name=name, out_shape=_sds((m_rows, n), out_dtype), grid=(m_rows // tm, n // tn),
               in_specs=[pl.BlockSpec((tm, k), lambda i, j: (i, 0)), pl.BlockSpec((tn, k), lambda i, j: (j, 0))],
               out_specs=pl.BlockSpec((tm, tn), lambda i, j: (i, j)),
               semantics=("arbitrary", "arbitrary"), vmem=VMEM_BIG)(a, w)


def _mm_tn(a, b, *, name, ta, tn, tt):
    t_rows, ka = a.shape
    n = b.shape[1]
    nt = t_rows // tt

    def body(a_ref, b_ref, o_ref, acc_ref):
        t = pl.program_id(2)

        @pl.when(t == 0)
        def _():
            acc_ref[...] = jnp.zeros_like(acc_ref)

        acc_ref[...] += lax.dot_general(a_ref[...].astype(BF16), b_ref[...].astype(BF16), TN,
                                        preferred_element_type=F32)

        @pl.when(t == nt - 1)
        def _():
            o_ref[...] = acc_ref[...].astype(BF16)

    return _pc(body, name=name, out_shape=_sds((ka, n), BF16), grid=(ka // ta, n // tn, nt),
               in_specs=[pl.BlockSpec((tt, ta), lambda i, j, t: (t, i)),
                         pl.BlockSpec((tt, tn), lambda i, j, t: (t, j))],
               out_specs=pl.BlockSpec((ta, tn), lambda i, j, t: (i, j)),
               scratch_shapes=[pltpu.VMEM((ta, tn), F32)],
               semantics=("arbitrary", "arbitrary", "arbitrary"), vmem=VMEM_BIG)(a, b)


def _wgrad(a, b, name):
    t_rows, ka = a.shape
    n = b.shape[1]
    ta = ka if ka <= 1024 else ka // 2
    tn = n
    while ta * tn * 4 > 6 * 2 ** 20 and tn % 256 == 0:
        tn //= 2
    tt = min(1024, t_rows)
    return _mm_tn(a, b, name=name, ta=ta, tn=tn, tt=tt)


def _mm_nt_rmsbwd(parts, w, x, g, *, name, dres=None, want_dx=True):
    m_rows, d = x.shape
    tm = min(256, m_rows)
    n_parts = len(parts)

    def body(*refs):
        refs = list(refs)
        dy_refs = [refs.pop(0) for _ in range(n_parts)]
        w_refs = [refs.pop(0) for _ in range(n_parts)]
        x_ref = refs.pop(0)
        g_ref = refs.pop(0)
        dres_ref = refs.pop(0) if dres is not None else None
        dx_ref = refs.pop(0) if want_dx else None
        dg_ref = refs.pop(0)

        @pl.when(pl.program_id(0) == 0)
        def _():
            dg_ref[...] = jnp.zeros_like(dg_ref)

        dh = None
        for dy_ref, w_ref in zip(dy_refs, w_refs):
            t = lax.dot_general(dy_ref[...].astype(BF16), w_ref[...], NT, preferred_element_type=F32)
            dh = t if dh is None else dh + t
        xv = x_ref[...]
        r = lax.rsqrt(jnp.mean(xv * xv, axis=-1, keepdims=True) + EPS)
        xh = xv * r
        dg_ref[...] += jnp.sum(dh * xh, axis=0, keepdims=True)
        if want_dx:
            dhg = dh * g_ref[...]
            dx = r * (dhg - xh * jnp.mean(dhg * xh, axis=-1, keepdims=True))
            if dres is not None:
                dx = dx + dres_ref[...]
            dx_ref[...] = dx

    in_specs, args = [], []
    for dy, _ in parts:
        in_specs.append(pl.BlockSpec((tm, dy.shape[1]), lambda i: (i, 0)))
        args.append(dy)
    for dy, cb in parts:
        in_specs.append(pl.BlockSpec((d, dy.shape[1]), functools.partial(lambda i, cb: (0, cb), cb=cb)))
        args.append(w)
    in_specs += [pl.BlockSpec((tm, d), lambda i: (i, 0)), pl.BlockSpec((1, d), lambda i: (0, 0))]
    args += [x, g.reshape(1, d)]
    if dres is not None:
        in_specs.append(pl.BlockSpec((tm, d), lambda i: (i, 0)))
        args.append(dres)
    out_shape, out_specs = [], []
    if want_dx:
        out_shape.append(_sds((m_rows, d), F32))
        out_specs.append(pl.BlockSpec((tm, d), lambda i: (i, 0)))
    out_shape.append(_sds((1, d), F32))
    out_specs.append(pl.BlockSpec((1, d), lambda i: (0, 0)))
    outs = _pc(body, name=name, out_shape=out_shape, grid=(m_rows // tm,), in_specs=in_specs,
               out_specs=out_specs, semantics=("arbitrary",), vmem=VMEM_BIG)(*args)
    return (outs[0], outs[1]) if want_dx else (None, outs[0])


def _loss_head(x, g, tgt, *, name):
    m_rows, d = x.shape
    tm = min(256, m_rows)

    def body(x_ref, g_ref, t_ref, dx_ref, dg_ref, loss_ref):
        @pl.when(pl.program_id(0) == 0)
        def _():
            dg_ref[...] = jnp.zeros_like(dg_ref)
            loss_ref[...] = jnp.zeros_like(loss_ref)

        xv = x_ref[...]
        r = lax.rsqrt(jnp.mean(xv * xv, axis=-1, keepdims=True) + EPS)
        xh = xv * r
        gv = g_ref[...]
        err = xh * gv - t_ref[...]
        per_tok = jnp.mean(err * err, axis=-1, keepdims=True)
        loss_ref[...] += 0.5 * jnp.sum(per_tok, axis=0, keepdims=True)
        dout = err * (1.0 / d)
        dg_ref[...] += jnp.sum(dout * xh, axis=0, keepdims=True)
        dhg = dout * gv
        dx_ref[...] = r * (dhg - xh * jnp.mean(dhg * xh, axis=-1, keepdims=True))

    row = pl.BlockSpec((tm, d), lambda i: (i, 0))
    return _pc(body, name=name, out_shape=[_sds((m_rows, d), F32), _sds((1, d), F32), _sds((1, LANES), F32)],
               grid=(m_rows // tm,), in_specs=[row, pl.BlockSpec((1, d), lambda i: (0, 0)), row],
               out_specs=[row, pl.BlockSpec((1, d), lambda i: (0, 0)), pl.BlockSpec((1, LANES), lambda i: (0, 0))],
               semantics=("arbitrary",))(x, g.reshape(1, d), tgt)


def _split3(v):
    hi = v.astype(BF16)
    r1 = v - hi.astype(F32)
    mid = r1.astype(BF16)
    lo = (r1 - mid.astype(F32)).astype(BF16)
    return hi, mid, lo


def _split2(v):
    hi = v.astype(BF16)
    lo = (v - hi.astype(F32)).astype(BF16)
    return hi, lo


def _tri_dot3(tri, v):
    hi, mid, lo = _split3(v)
    return (jnp.dot(tri, hi, preferred_element_type=F32) + jnp.dot(tri, mid, preferred_element_type=F32)
            + jnp.dot(tri, lo, preferred_element_type=F32))


def _log_sigmoid(v):
    return jnp.minimum(v, 0.0) - jnp.log(1.0 + jnp.exp(-jnp.abs(v)))


def _forget_cumsum(f_logit, b_f, *, B, S, name):
    ch = min(256, S)
    nch = S // ch

    def body(f_ref, b_ref, c_ref):
        r_i = lax.broadcasted_iota(jnp.int32, (ch, ch), 0)
        c_i = lax.broadcasted_iota(jnp.int32, (ch, ch), 1)
        tri = (c_i <= r_i).astype(BF16)
        bv = b_ref[...]

        def step(k, carry):
            rows = pl.ds(pl.multiple_of(k * ch, ch), ch)
            lf = _log_sigmoid(f_ref[rows, :] + bv)
            c_ref[rows, :] = _tri_dot3(tri, lf) + carry
            return carry + jnp.sum(lf, axis=0, keepdims=True)

        lax.fori_loop(0, nch, step, jnp.zeros((1, LANES), F32))

    blk = pl.BlockSpec((S, LANES), lambda b: (b, 0))
    return _pc(body, name=name, out_shape=_sds((B * S, LANES), F32), grid=(B,),
               in_specs=[blk, pl.BlockSpec((1, LANES), lambda b: (0, 0))], out_specs=blk,
               semantics=("arbitrary",))(f_logit, b_f)


def _forget_cumsum_bwd(dc, f_logit, b_f, *, B, S, name):
    ch = min(256, S)
    nch = S // ch

    def body(dc_ref, f_ref, b_ref, df_ref, db_ref):
        @pl.when(pl.program_id(0) == 0)
        def _():
            db_ref[...] = jnp.zeros_like(db_ref)

        r_i = lax.broadcasted_iota(jnp.int32, (ch, ch), 0)
        c_i = lax.broadcasted_iota(jnp.int32, (ch, ch), 1)
        tri = (c_i >= r_i).astype(BF16)
        bv = b_ref[...]

        def step(kk, carry):
            tail, dbs = carry
            k = nch - 1 - kk
            rows = pl.ds(pl.multiple_of(k * ch, ch), ch)
            dcv = dc_ref[rows, :]
            dlf = _tri_dot3(tri, dcv) + tail
            z = f_ref[rows, :] + bv
            df = dlf * (1.0 / (1.0 + jnp.exp(z)))
            df_ref[rows, :] = df.astype(BF16)
            return tail + jnp.sum(dcv, axis=0, keepdims=True), dbs + jnp.sum(df, axis=0, keepdims=True)

        zero = jnp.zeros((1, LANES), F32)
        _, dbs = lax.fori_loop(0, nch, step, (zero, zero))
        db_ref[...] += dbs

    blk = pl.BlockSpec((S, LANES), lambda b: (b, 0))
    one = pl.BlockSpec((1, LANES), lambda b: (0, 0))
    return _pc(body, name=name, out_shape=[_sds((B * S, LANES), BF16), _sds((1, LANES), F32)], grid=(B,),
               in_specs=[blk, blk, one], out_specs=[blk, one], semantics=("arbitrary",))(dc, f_logit, b_f)


def _head_mask(lane, hh):
    return (lane < HEAD_DIM) if hh == 0 else (lane >= HEAD_DIM)


def _col_spec(rows, nblk_rows, cb):
    return pl.BlockSpec((rows, LANES), lambda b, p, i: (b * nblk_rows + i, cb + p))


def _kv_spec(rows, cb):
    return pl.BlockSpec((rows, LANES), lambda b, p, i: (b, cb + p))


def _stat_col_spec(tq):
    return pl.BlockSpec((1, 2, tq, 1), lambda b, p, i: (b, p, i, 0))


def _stat_row_spec(S):
    return pl.BlockSpec((1, 2, 1, S), lambda b, p, i: (b, p, 0, 0))


def _softmax_fwd(qa, ka, va, *, name, B, S, Sk, P, q_cb, k_cb, v_cb, causal, cc=None, cr=None):
    tq = min(ATT_TILE if causal else MEM_Q_TILE, S)
    tk = min(ATT_TILE, Sk)
    nq, nk = S // tq, Sk // tk
    decay = cc is not None
    assert not causal or (tq == tk and S == Sk)

    def body(*refs):
        if decay:
            q_ref, k_ref, v_ref, cc_ref, cr_ref, o_ref, lse_ref = refs
        else:
            q_ref, k_ref, v_ref, o_ref, lse_ref = refs
        i = pl.program_id(2)
        q = q_ref[...]
        lane = lax.broadcasted_iota(jnp.int32, (tq, LANES), 1)
        row = lax.broadcasted_iota(jnp.int32, (tq, tk), 0) + i * tq
        col0 = lax.broadcasted_iota(jnp.int32, (tq, tk), 1)
        outs = []
        for hh in range(2):
            qh = jnp.where(_head_mask(lane, hh), q, jnp.zeros_like(q))

            def step(kb, carry, hh=hh, qh=qh):
                m, l, acc = carry
                ks = pl.multiple_of(kb * tk, tk)
                kblk = k_ref[pl.ds(ks, tk), :]
                vblk = v_ref[pl.ds(ks, tk), :]
                s = lax.dot_general(qh, kblk, NT, preferred_element_type=F32) * SCALE
                if decay:
                    s = s + (cc_ref[0, hh] - cr_ref[0, hh, :, pl.ds(ks, tk)])
                if causal:
                    s = jnp.where(col0 + kb * tk <= row, s, NEG)
                m_new = jnp.maximum(m, jnp.max(s, axis=-1, keepdims=True))
                alpha = jnp.exp(m - m_new)
                p = jnp.exp(s - m_new)
                l = alpha * l + jnp.sum(p, axis=-1, keepdims=True)
                acc = alpha * acc + jnp.dot(p.astype(BF16), vblk, preferred_element_type=F32)
                return m_new, l, acc

            init = (jnp.full((tq, 1), NEG, F32), jnp.zeros((tq, 1), F32), jnp.zeros((tq, LANES), F32))
            m, l, acc = lax.fori_loop(0, (i + 1) if causal else nk, step, init)
            outs.append(acc / l)
            lse_ref[0, hh] = m + jnp.log(l)
        o_ref[...] = jnp.where(lane < HEAD_DIM, outs[0], outs[1]).astype(BF16)

    in_specs = [_col_spec(tq, nq, q_cb), _kv_spec(Sk, k_cb), _kv_spec(Sk, v_cb)]
    args = [qa, ka, va]
    if decay:
        in_specs += [_stat_col_spec(tq), _stat_row_spec(S)]
        args += [cc, cr]
    return _pc(body, name=name,
               out_shape=[_sds((B * S, P * LANES), BF16), _sds((B, 2 * P, S, 1), F32)],
               grid=(B, P, nq), in_specs=in_specs, out_specs=[_col_spec(tq, nq, 0), _stat_col_spec(tq)],
               semantics=("arbitrary", "arbitrary", "arbitrary"), vmem=VMEM_BIG)(*args)


def _softmax_bwd(qa, ka, va, doa, oa, lse, *, name, B, S, Sk, P, q_cb, k_cb, v_cb, do_cb, causal,
                 cc=None, cr=None):
    tq = min(ATT_TILE if causal else MEM_Q_TILE, S)
    tk = min(ATT_TILE, Sk)
    nq, nk = S // tq, Sk // tk
    decay = cc is not None

    def body(*refs):
        if decay:
            q_ref, k_ref, v_ref, do_ref, o_ref, lse_ref, cc_ref, cr_ref, dq_ref, dk_ref, dv_ref, dcs_ref = refs
        else:
            q_ref, k_ref, v_ref, do_ref, o_ref, lse_ref, dq_ref, dk_ref, dv_ref = refs
        i = pl.program_id(2)

        @pl.when(i == 0)
        def _():
            dk_ref[...] = jnp.zeros_like(dk_ref)
            dv_ref[...] = jnp.zeros_like(dv_ref)
            if decay:
                dcs_ref[...] = jnp.zeros_like(dcs_ref)

        q = q_ref[...]
        do = do_ref[...]
        prod = do.astype(F32) * o_ref[...].astype(F32)
        lane = lax.broadcasted_iota(jnp.int32, (tq, LANES), 1)
        row = lax.broadcasted_iota(jnp.int32, (tq, tk), 0) + i * tq
        col0 = lax.broadcasted_iota(jnp.int32, (tq, tk), 1)
        dqs = []
        for hh in range(2):
            hmask = _head_mask(lane, hh)
            qh = jnp.where(hmask, q, jnp.zeros_like(q))
            doh = jnp.where(hmask, do, jnp.zeros_like(do))
            lse_h = lse_ref[0, hh]
            n_blocks = (i + 1) if causal else nk

            def probs(kb, hh=hh, qh=qh, doh=doh, lse_h=lse_h):
                ks = pl.multiple_of(kb * tk, tk)
                kblk = k_ref[pl.ds(ks, tk), :]
                vblk = v_ref[pl.ds(ks, tk), :]
                s = lax.dot_general(qh, kblk, NT, preferred_element_type=F32) * SCALE
                if decay:
                    s = s + (cc_ref[0, hh] - cr_ref[0, hh, :, pl.ds(ks, tk)])
                if causal:
                    s = jnp.where(col0 + kb * tk <= row, s, NEG)
                p = jnp.exp(s - lse_h)
                dp = lax.dot_general(doh, vblk, NT, preferred_element_type=F32)
                return ks, kblk, p, dp

            if decay:
                def delta_step(kb, acc):
                    _, _, p, dp = probs(kb)
                    return acc + jnp.sum(p * dp, axis=-1, keepdims=True)

                delta = lax.fori_loop(0, n_blocks, delta_step, jnp.zeros((tq, 1), F32))
            else:
                delta = jnp.sum(jnp.where(hmask, prod, 0.0), axis=-1, keepdims=True)

            def step(kb, dq_acc, hh=hh, qh=qh, doh=doh, delta=delta):
                ks, kblk, p, dp = probs(kb)
                ds = p * (dp - delta)
                dsb = ds.astype(BF16)
                dk_ref[pl.ds(ks, tk), :] += lax.dot_general(dsb, qh, TN, preferred_element_type=F32) * SCALE
                dv_ref[pl.ds(ks, tk), :] += lax.dot_general(p.astype(BF16), doh, TN, preferred_element_type=F32)
                if decay:
                    dcs_ref[0, hh, :, pl.ds(ks, tk)] -= jnp.sum(ds, axis=0, keepdims=True)
                return dq_acc + jnp.dot(dsb, kblk, preferred_element_type=F32)

            dqs.append(lax.fori_loop(0, n_blocks, step, jnp.zeros((tq, LANES), F32)) * SCALE)
        dq_ref[...] = jnp.where(lane < HEAD_DIM, dqs[0], dqs[1]).astype(BF16)

    in_specs = [_col_spec(tq, nq, q_cb), _kv_spec(Sk, k_cb), _kv_spec(Sk, v_cb), _col_spec(tq, nq, do_cb),
                _col_spec(tq, nq, 0), _stat_col_spec(tq)]
    args = [qa, ka, va, doa, oa, lse]
    out_shape = [_sds((B * S, P * LANES), BF16), _sds((B * Sk, P * LANES), F32), _sds((B * Sk, P * LANES), F32)]
    out_specs = [_col_spec(tq, nq, 0), _kv_spec(Sk, 0), _kv_spec(Sk, 0)]
    if decay:
        in_specs += [_stat_col_spec(tq), _stat_row_spec(S)]
        args += [cc, cr]
        out_shape.append(_sds((B, 2 * P, 1, S), F32))
        out_specs.append(_stat_row_spec(S))
    return _pc(body, name=name, out_shape=out_shape, grid=(B, P, nq), in_specs=in_specs, out_specs=out_specs,
               semantics=("arbitrary", "arbitrary", "arbitrary"), vmem=VMEM_BIG)(*args)


HEAD_GROUP = 3


def _g_col_spec(rows, nblk_rows, cb, G):
    return pl.BlockSpec((rows, G * LANES), lambda b, p, i: (b * nblk_rows + i, cb // G + p))


def _g_kv_spec(rows, cb, G):
    return pl.BlockSpec((rows, G * LANES), lambda b, p, i: (b, cb // G + p))


def _g_stat_col_spec(tq, G):
    return pl.BlockSpec((1, 2 * G, tq, 1), lambda b, p, i: (b, p, i, 0))


def _g_stat_row_spec(S, G):
    return pl.BlockSpec((1, 2 * G, 1, S), lambda b, p, i: (b, p, 0, 0))


def _lanes(g):
    return slice(g * LANES, (g + 1) * LANES)


def _streams(x_ref, G, scale=None):
    rows = x_ref.shape[0]
    lane = lax.broadcasted_iota(jnp.int32, (rows, LANES), 1)
    out = []
    for g in range(G):
        x = x_ref[:, _lanes(g)]
        if scale is not None:
            x = x * jnp.asarray(scale, x.dtype)
        for hh in range(2):
            out.append(jnp.where(_head_mask(lane, hh), x, jnp.zeros_like(x)))
    return lane, out


def _wide(stat, width):
    return jnp.tile(stat, (1, width // LANES))


def _fold_lanes(v):
    out = v[:, :LANES]
    for j in range(1, v.shape[1] // LANES):
        out = out + v[:, j * LANES:(j + 1) * LANES]
    return out


def _kv_blocks(ref, ks, tk, G):
    return [ref[pl.ds(ks, tk), _lanes(g)] for g in range(G)]


def _sweep(i, block):
    def step(kb, c):
        block(kb, False)
        return c
    lax.fori_loop(0, i, step, 0)
    block(i, True)


def _fox_fwd_g(qa, ka, va, cr, *, name, B, S, P, q_cb, k_cb, v_cb, G=HEAD_GROUP):
    tq = tk = min(ATT_TILE, S)
    nq = S // tq
    NS = 2 * G

    def body(q_ref, k_ref, v_ref, cr_ref, o_ref, lse_ref, acc_ref, m_ref, l_ref):
        i = pl.program_id(2)
        lane, qh = _streams(q_ref, G, SCALE)
        on_or_below = (lax.broadcasted_iota(jnp.int32, (tq, tk), 1) <= lax.broadcasted_iota(jnp.int32, (tq, tk), 0))
        m_ref[...] = jnp.full(m_ref.shape, NEG, F32)
        l_ref[...] = jnp.zeros(l_ref.shape, F32)
        acc_ref[...] = jnp.zeros(acc_ref.shape, F32)

        def block(kb, diag):
            ks = pl.multiple_of(kb * tk, tk)
            kblk = _kv_blocks(k_ref, ks, tk, G)
            vblk = _kv_blocks(v_ref, ks, tk, G)
            ss = [lax.dot_general(qh[st], kblk[st // 2], NT, preferred_element_type=F32) for st in range(NS)]
            ps = []
            for st in range(NS):
                s = ss[st] - cr_ref[0, st, :, pl.ds(ks, tk)]
                if diag:
                    s = jnp.where(on_or_below, s, NEG)
                m = m_ref[st]
                m_new = jnp.maximum(m, jnp.max(s, axis=-1, keepdims=True))
                alpha = jnp.exp(m - m_new)
                p = jnp.exp(s - _wide(m_new, tk))
                m_ref[st] = m_new
                l_ref[st] = alpha * l_ref[st] + _fold_lanes(p)
                ps.append((alpha, p.astype(BF16)))
            pvs = [jnp.dot(ps[st][1], vblk[st // 2], preferred_element_type=F32) for st in range(NS)]
            for st in range(NS):
                acc_ref[st] = ps[st][0] * acc_ref[st] + pvs[st]

        _sweep(i, block)
        ls = [jnp.sum(l_ref[st], axis=-1, keepdims=True) for st in range(NS)]
        for st in range(NS):
            lse_ref[0, st] = jnp.max(m_ref[st], axis=-1, keepdims=True) + jnp.log(ls[st])
        for g in range(G):
            o_ref[:, _lanes(g)] = jnp.where(lane < HEAD_DIM, acc_ref[2 * g] / ls[2 * g],
                                            acc_ref[2 * g + 1] / ls[2 * g + 1]).astype(BF16)

    return _pc(body, name=name, out_shape=[_sds((B * S, P * LANES), BF16), _sds((B, 2 * P, S, 1), F32)],
               grid=(B, P // G, nq),
               in_specs=[_g_col_spec(tq, nq, q_cb, G), _g_kv_spec(S, k_cb, G), _g_kv_spec(S, v_cb, G),
                         _g_stat_row_spec(S, G)],
               out_specs=[_g_col_spec(tq, nq, 0, G), _g_stat_col_spec(tq, G)],
               scratch_shapes=[pltpu.VMEM((NS, tq, LANES), F32)] * 3,
               semantics=("arbitrary", "arbitrary", "arbitrary"), vmem=VMEM_BIG)(qa, ka, va, cr)


def _fox_bwd_g(qa, ka, va, doa, lse, cr, *, name, B, S, P, q_cb, k_cb, v_cb, do_cb, G=HEAD_GROUP):
    tq = tk = min(ATT_TILE, S)
    nq = S // tq
    NS = 2 * G

    def body(q_ref, k_ref, v_ref, do_ref, lse_ref, cr_ref, dq_ref, dk_ref, dv_ref, dcs_ref, dqa_ref, delta_ref, lse_s,
             p_buf, dp_buf):
        i = pl.program_id(2)

        @pl.when(i == 0)
        def _():
            dk_ref[...] = jnp.zeros_like(dk_ref)
            dv_ref[...] = jnp.zeros_like(dv_ref)
            dcs_ref[...] = jnp.zeros_like(dcs_ref)

        lane, qh = _streams(q_ref, G, SCALE)
        _, doh = _streams(do_ref, G)
        on_or_below = (lax.broadcasted_iota(jnp.int32, (tq, tk), 1) <= lax.broadcasted_iota(jnp.int32, (tq, tk), 0))
        delta_ref[...] = jnp.zeros(delta_ref.shape, F32)
        dqa_ref[...] = jnp.zeros(dqa_ref.shape, F32)
        for st in range(NS):
            lse_s[st] = jnp.broadcast_to(lse_ref[0, st], (tq, LANES))

        def probs(kb, diag):
            ks = pl.multiple_of(kb * tk, tk)
            kblk = _kv_blocks(k_ref, ks, tk, G)
            vblk = _kv_blocks(v_ref, ks, tk, G)
            ss = [lax.dot_general(qh[st], kblk[st // 2], NT, preferred_element_type=F32) for st in range(NS)]
            dps = [lax.dot_general(doh[st], vblk[st // 2], NT, preferred_element_type=F32) for st in range(NS)]
            ps = []
            for st in range(NS):
                s = ss[st] - cr_ref[0, st, :, pl.ds(ks, tk)]
                if diag:
                    s = jnp.where(on_or_below, s, NEG)
                ps.append(jnp.exp(s - _wide(lse_s[st], tk)))
            return ks, kblk, ps, dps

        def delta_block(kb, diag):
            _, _, ps, dps = probs(kb, diag)
            for st in range(NS):
                delta_ref[st] += _fold_lanes(ps[st] * dps[st])
                p_buf[st, kb] = ps[st]
                dp_buf[st, kb] = dps[st]

        _sweep(i, delta_block)
        for st in range(NS):
            delta_ref[st] = jnp.broadcast_to(jnp.sum(delta_ref[st], axis=-1, keepdims=True), (tq, LANES))

        def grad_block(kb, diag):
            ks = pl.multiple_of(kb * tk, tk)
            kblk = _kv_blocks(k_ref, ks, tk, G)
            rows = pl.ds(ks, tk)
            dsb, pb = [], []
            for st in range(NS):
                p = p_buf[st, kb]
                ds = p * (dp_buf[st, kb] - _wide(delta_ref[st], tk))
                dcs_ref[0, st, :, rows] -= jnp.sum(ds, axis=0, keepdims=True)
                dsb.append(ds.astype(BF16))
                pb.append(p.astype(BF16))
            dks = [lax.dot_general(dsb[st], qh[st], TN, preferred_element_type=F32) for st in range(NS)]
            dvs = [lax.dot_general(pb[st], doh[st], TN, preferred_element_type=F32) for st in range(NS)]
            dqs = [jnp.dot(dsb[st], kblk[st // 2], preferred_element_type=F32) for st in range(NS)]
            for g in range(G):
                dk_ref[rows, _lanes(g)] += dks[2 * g] + dks[2 * g + 1]
                dv_ref[rows, _lanes(g)] += dvs[2 * g] + dvs[2 * g + 1]
            for st in range(NS):
                dqa_ref[st] += dqs[st]

        _sweep(i, grad_block)
        for g in range(G):
            dq_ref[:, _lanes(g)] = (jnp.where(lane < HEAD_DIM, dqa_ref[2 * g], dqa_ref[2 * g + 1]) * SCALE).astype(BF16)

    return _pc(body, name=name,
               out_shape=[_sds((B * S, P * LANES), BF16), _sds((B * S, P * LANES), F32), _sds((B * S, P * LANES), F32),
                          _sds((B, 2 * P, 1, S), F32)],
               grid=(B, P // G, nq),
               in_specs=[_g_col_spec(tq, nq, q_cb, G), _g_kv_spec(S, k_cb, G), _g_kv_spec(S, v_cb, G),
                         _g_col_spec(tq, nq, do_cb, G), _g_stat_col_spec(tq, G), _g_stat_row_spec(S, G)],
               out_specs=[_g_col_spec(tq, nq, 0, G), _g_kv_spec(S, 0, G), _g_kv_spec(S, 0, G), _g_stat_row_spec(S, G)],
               scratch_shapes=[pltpu.VMEM((NS, tq, LANES), F32)] * 3 + [pltpu.VMEM((NS, nq, tq, tk), F32)] * 2,
               semantics=("arbitrary", "arbitrary", "arbitrary"), vmem=VMEM_BIG)(qa, ka, va, doa, lse, cr)


def _sb_logs_z(z):
    nz = -z
    lm = jnp.minimum(nz, 0.0) - jnp.log(1.0 + jnp.exp(jnp.minimum(z, nz)))
    return lm + z, lm


def _sb_fwd_g(qa, ka, va, *, name, B, S, P, q_cb, k_cb, v_cb, G=HEAD_GROUP):
    tq = tk = min(ATT_TILE, S)
    nq = S // tq
    NS = 2 * G

    def body(q_ref, k_ref, v_ref, o_ref, rt_ref, acc_ref, run_ref):
        i = pl.program_id(2)
        lane, qh = _streams(q_ref, G, SCALE)
        t_r = lax.broadcasted_iota(jnp.int32, (tk, tk), 0)
        t_c = lax.broadcasted_iota(jnp.int32, (tk, tk), 1)
        after = (t_r > t_c).astype(BF16)
        below = t_c < t_r
        acc_ref[...] = jnp.zeros(acc_ref.shape, F32)
        run_ref[...] = jnp.zeros(run_ref.shape, F32)

        def block(kb, diag):
            ks = pl.multiple_of(kb * tk, tk)
            kblk = _kv_blocks(k_ref, ks, tk, G)
            vblk = _kv_blocks(v_ref, ks, tk, G)
            zs = [lax.dot_general(qh[st], kblk[st // 2], NT, preferred_element_type=F32) for st in range(NS)]
            lss, parts = [], []
            for st in range(NS):
                ls, lm = _sb_logs_z(zs[st])
                if diag:
                    lm = jnp.where(below, lm, 0.0)
                lss.append(ls + _wide(run_ref[st], tk))
                run_ref[st] += jnp.sum(lm, axis=-1, keepdims=True)
                parts.append(_split2(lm))
            sufs = [jnp.dot(parts[st][0], after, preferred_element_type=F32)
                    + jnp.dot(parts[st][1], after, preferred_element_type=F32) for st in range(NS)]
            ab = []
            for st in range(NS):
                a = jnp.exp(lss[st] + sufs[st])
                if diag:
                    a = jnp.where(below, a, 0.0)
                ab.append(a.astype(BF16))
            pvs = [jnp.dot(ab[st], vblk[st // 2], preferred_element_type=F32) for st in range(NS)]
            for st in range(NS):
                acc_ref[st] += pvs[st]

        block(i, True)

        def step(jj, c):
            block(i - 1 - jj, False)
            return c

        lax.fori_loop(0, i, step, 0)
        for st in range(NS):
            rt_ref[0, st] = jnp.max(run_ref[st], axis=-1, keepdims=True)
        for g in range(G):
            o_ref[:, _lanes(g)] = jnp.where(lane < HEAD_DIM, acc_ref[2 * g], acc_ref[2 * g + 1]).astype(BF16)

    return _pc(body, name=name, out_shape=[_sds((B * S, P * LANES), BF16), _sds((B, 2 * P, S, 1), F32)],
               grid=(B, P // G, nq),
               in_specs=[_g_col_spec(tq, nq, q_cb, G), _g_kv_spec(S, k_cb, G), _g_kv_spec(S, v_cb, G)],
               out_specs=[_g_col_spec(tq, nq, 0, G), _g_stat_col_spec(tq, G)],
               scratch_shapes=[pltpu.VMEM((NS, tq, LANES), F32)] * 2,
               semantics=("arbitrary", "arbitrary", "arbitrary"), vmem=VMEM_BIG)(qa, ka, va)


def _sb_bwd_g(qa, ka, va, doa, rt, *, name, B, S, P, q_cb, k_cb, v_cb, do_cb, G=HEAD_GROUP):
    tq = tk = min(ATT_TILE, S)
    nq = S // tq
    NS = 2 * G

    def body(q_ref, k_ref, v_ref, do_ref, rt_ref, dq_ref, dk_ref, dv_ref, dqa_ref, pl_ref, pg_ref):
        i = pl.program_id(2)

        @pl.when(i == 0)
        def _():
            dk_ref[...] = jnp.zeros_like(dk_ref)
            dv_ref[...] = jnp.zeros_like(dv_ref)

        lane, qh = _streams(q_ref, G, SCALE)
        _, doh = _streams(do_ref, G)
        t_r = lax.broadcasted_iota(jnp.int32, (tk, tk), 0)
        t_c = lax.broadcasted_iota(jnp.int32, (tk, tk), 1)
        upto = (t_r <= t_c).astype(BF16)
        before = (t_r < t_c).astype(BF16)
        below = t_c < t_r
        dqa_ref[...] = jnp.zeros(dqa_ref.shape, F32)
        pg_ref[...] = jnp.zeros(pg_ref.shape, F32)
        for st in range(NS):
            pl_ref[st] = jnp.broadcast_to(rt_ref[0, st], (tq, LANES))

        def block(kb, diag):
            ks = pl.multiple_of(kb * tk, tk)
            rows = pl.ds(ks, tk)
            kblk = _kv_blocks(k_ref, ks, tk, G)
            vblk = _kv_blocks(v_ref, ks, tk, G)
            zs = [lax.dot_general(qh[st], kblk[st // 2], NT, preferred_element_type=F32) for st in range(NS)]
            das = [lax.dot_general(doh[st], vblk[st // 2], NT, preferred_element_type=F32) for st in range(NS)]
            lss, parts = [], []
            for st in range(NS):
                ls, lm = _sb_logs_z(zs[st])
                if diag:
                    lm = jnp.where(below, lm, 0.0)
                lss.append((ls, ls + _wide(pl_ref[st], tk)))
                pl_ref[st] -= jnp.sum(lm, axis=-1, keepdims=True)
                parts.append(_split2(lm))
            pins = [jnp.dot(parts[st][0], upto, preferred_element_type=F32)
                    + jnp.dot(parts[st][1], upto, preferred_element_type=F32) for st in range(NS)]
            gms, ab, gparts = [], [], []
            for st in range(NS):
                a = jnp.exp(lss[st][1] - pins[st])
                if diag:
                    a = jnp.where(below, a, 0.0)
                gm = a * das[st]
                gms.append(gm)
                ab.append(a.astype(BF16))
                gparts.append(gm.astype(BF16))
            pgs = [jnp.dot(gparts[st], before, preferred_element_type=F32) for st in range(NS)]
            dzb = []
            for st in range(NS):
                gm = gms[st]
                dz = gm - jnp.exp(lss[st][0]) * (gm + (pgs[st] + _wide(pg_ref[st], tk)))
                if diag:
                    dz = jnp.where(below, dz, 0.0)
                pg_ref[st] += jnp.sum(gm, axis=-1, keepdims=True)
                dzb.append(dz.astype(BF16))
            dks = [lax.dot_general(dzb[st], qh[st], TN, preferred_element_type=F32) for st in range(NS)]
            dvs = [lax.dot_general(ab[st], doh[st], TN, preferred_element_type=F32) for st in range(NS)]
            dqs = [jnp.dot(dzb[st], kblk[st // 2], preferred_element_type=F32) for st in range(NS)]
            for g in range(G):
                dk_ref[rows, _lanes(g)] += dks[2 * g] + dks[2 * g + 1]
                dv_ref[rows, _lanes(g)] += dvs[2 * g] + dvs[2 * g + 1]
            for st in range(NS):
                dqa_ref[st] += dqs[st]

        _sweep(i, block)
        for g in range(G):
            dq_ref[:, _lanes(g)] = (jnp.where(lane < HEAD_DIM, dqa_ref[2 * g], dqa_ref[2 * g + 1]) * SCALE).astype(BF16)

    return _pc(body, name=name,
               out_shape=[_sds((B * S, P * LANES), BF16), _sds((B * S, P * LANES), F32), _sds((B * S, P * LANES), F32)],
               grid=(B, P // G, nq),
               in_specs=[_g_col_spec(tq, nq, q_cb, G), _g_kv_spec(S, k_cb, G), _g_kv_spec(S, v_cb, G),
                         _g_col_spec(tq, nq, do_cb, G), _g_stat_col_spec(tq, G)],
               out_specs=[_g_col_spec(tq, nq, 0, G), _g_kv_spec(S, 0, G), _g_kv_spec(S, 0, G)],
               scratch_shapes=[pltpu.VMEM((NS, tq, LANES), F32)] * 3,
               semantics=("arbitrary", "arbitrary", "arbitrary"), vmem=VMEM_BIG)(qa, ka, va, doa, rt)


def _shift_rows(cur, halo_ref, first, rows_idx, k):
    out = pltpu.roll(cur, k, 0)
    top = out[0:8, :]
    for r in range(k):
        hr = halo_ref.shape[0] - k + r
        edge = jnp.where(first, 0.0, halo_ref[hr:hr + 1, :])
        top = jnp.where(rows_idx[0:8, :] == r, edge, top)
    return jnp.concatenate([top, out[8:, :]], axis=0)


def _shift_rows_up(cur, halo_ref, last, rows_idx, k, ts):
    out = pltpu.roll(cur, ts - k, 0)
    bottom = out[ts - 8:, :]
    for r in range(k):
        edge = jnp.where(last, 0.0, halo_ref[r:r + 1, :])
        bottom = jnp.where(rows_idx[0:8, :] == 8 - k + r, edge, bottom)
    return jnp.concatenate([out[:ts - 8, :], bottom], axis=0)


def _ffn_up_gate(x, g, w, cw, cb, *, name, S):
    T, D = x.shape
    F = w.shape[1] // 2
    tm = min(1024, S)
    tn = 256
    nj = F // tn
    tiles_per_seq = S // tm
    halo = 16

    def body(x_ref, xh_ref, g_ref, wg_ref, wv_ref, cwg_ref, cwv_ref, cbg_ref, cbv_ref,
             uc_ref, ub_ref, a_ref, hout_ref, h_ref, hh_ref, eg_ref, ev_ref):
        first = lax.rem(pl.program_id(0), tiles_per_seq) == 0

        @pl.when(pl.program_id(1) == 0)
        def _():
            def norm(v):
                r = lax.rsqrt(jnp.mean(v * v, axis=-1, keepdims=True) + EPS)
                return ((v * r) * g_ref[...]).astype(BF16)
            h = norm(x_ref[...])
            h_ref[...] = h
            hout_ref[...] = h
            hh_ref[...] = norm(xh_ref[...])

        h = h_ref[...]
        rows_idx = lax.broadcasted_iota(jnp.int32, (tm, tn), 0)
        uc = []
        for half, (w_ref, cw_ref, cb_ref, e_ref) in enumerate(((wg_ref, cwg_ref, cbg_ref, eg_ref),
                                                               (wv_ref, cwv_ref, cbv_ref, ev_ref))):
            acc = jnp.dot(h, w_ref[...], preferred_element_type=F32)
            e_ref[...] = jnp.dot(hh_ref[...], w_ref[...], preferred_element_type=F32)
            ub_ref[half] = acc.astype(BF16)
            m1 = _shift_rows(acc, e_ref, first, rows_idx, 1)
            m2 = _shift_rows(acc, e_ref, first, rows_idx, 2)
            uc.append(cb_ref[...] + cw_ref[0:1, :] * m2 + cw_ref[1:2, :] * m1 + cw_ref[2:3, :] * acc)
            uc_ref[half] = uc[half]
        a_ref[...] = (uc[0] * (1.0 / (1.0 + jnp.exp(-uc[0]))) * uc[1]).astype(BF16)

    in_specs = [pl.BlockSpec((tm, D), lambda i, j: (i, 0)),
                pl.BlockSpec((halo, D), lambda i, j: (jnp.maximum(i * (tm // halo) - 1, 0), 0)),
                pl.BlockSpec((1, D), lambda i, j: (0, 0)),
                pl.BlockSpec((D, tn), lambda i, j: (0, j)), pl.BlockSpec((D, tn), lambda i, j: (0, j + nj)),
                pl.BlockSpec((3, tn), lambda i, j: (0, j)), pl.BlockSpec((3, tn), lambda i, j: (0, j + nj)),
                pl.BlockSpec((1, tn), lambda i, j: (0, j)), pl.BlockSpec((1, tn), lambda i, j: (0, j + nj))]
    return _pc(body, name=name,
               out_shape=[_sds((2, T, F), F32), _sds((2, T, F), BF16), _sds((T, F), BF16), _sds((T, D), BF16)],
               grid=(T // tm, nj), in_specs=in_specs,
               out_specs=[pl.BlockSpec((2, tm, tn), lambda i, j: (0, i, j)), pl.BlockSpec((2, tm, tn), lambda i, j: (0, i, j)),
                          pl.BlockSpec((tm, tn), lambda i, j: (i, j)), pl.BlockSpec((tm, D), lambda i, j: (i, 0))],
               scratch_shapes=[pltpu.VMEM((tm, D), BF16), pltpu.VMEM((halo, D), BF16),
                               pltpu.VMEM((halo, tn), F32), pltpu.VMEM((halo, tn), F32)],
               semantics=("arbitrary", "arbitrary"), vmem=VMEM_BIG)(x, x, g.reshape(1, D), w, w, cw, cw, cb, cb)


def _conv_gate_bwd(da, uc, ub, cw, *, name, B, S):
    F = uc.shape[2]
    tf = F // 2
    ts = min(256, S)
    ns, nf = S // ts, F // tf

    def body(da_ref, uc_ref, ub_ref, wg_ref, wv_ref, dug_ref, duv_ref, pg_ref, pv_ref, nxt_g, nxt_v):
        last = pl.program_id(2) == 0

        @pl.when(jnp.logical_and(pl.program_id(1) == 0, last))
        def _():
            pg_ref[...] = jnp.zeros_like(pg_ref)
            pv_ref[...] = jnp.zeros_like(pv_ref)

        rows_idx = lax.broadcasted_iota(jnp.int32, (ts, tf), 0)
        ucg, ucv = uc_ref[0], uc_ref[1]
        sg = 1.0 / (1.0 + jnp.exp(-ucg))
        dav = da_ref[...]
        d_v = dav * (ucg * sg)
        d_g = dav * ucv * (sg * (1.0 + ucg * (1.0 - sg)))
        for half, (o_ref, p_ref, d, w_ref, nxt) in enumerate(((dug_ref, pg_ref, d_g, wg_ref, nxt_g),
                                                               (duv_ref, pv_ref, d_v, wv_ref, nxt_v))):
            p1 = _shift_rows_up(d, nxt, last, rows_idx, 1, ts)
            p2 = _shift_rows_up(d, nxt, last, rows_idx, 2, ts)
            o_ref[...] = (w_ref[2:3, :] * d + w_ref[1:2, :] * p1 + w_ref[0:1, :] * p2).astype(BF16)
            nxt[...] = d[0:8, :]
            uh = ub_ref[half].astype(F32)
            for k, dk in enumerate((p2, p1, d)):
                p_ref[k:k + 1, :] += jnp.sum(dk * uh, axis=0, keepdims=True)
            p_ref[3:4, :] += jnp.sum(d, axis=0, keepdims=True)

    row = pl.BlockSpec((ts, tf), lambda j, b, r: (b * ns + ns - 1 - r, j))
    both = pl.BlockSpec((2, ts, tf), lambda j, b, r: (0, b * ns + ns - 1 - r, j))
    par = pl.BlockSpec((8, tf), lambda j, b, r: (0, j))
    return _pc(body, name=name,
               out_shape=[_sds((B * S, F), BF16), _sds((B * S, F), BF16), _sds((8, F), F32), _sds((8, F), F32)],
               grid=(nf, B, ns),
               in_specs=[row, both, both, pl.BlockSpec((3, tf), lambda j, b, r: (0, j)),
                         pl.BlockSpec((3, tf), lambda j, b, r: (0, j + nf))],
               out_specs=[row, row, par, par],
               scratch_shapes=[pltpu.VMEM((8, tf), F32), pltpu.VMEM((8, tf), F32)],
               semantics=("arbitrary", "arbitrary", "arbitrary"), vmem=VMEM_BIG)(da, uc, ub, cw, cw)


def _adamw(w, g, m, v, *, name):
    rows, cols = w.shape
    tr = rows
    while tr * cols * 4 > 2 ** 20 and tr % 16 == 0:
        tr //= 2

    def body(w_ref, g_ref, m_ref, v_ref, d_ref, nm_ref, nv_ref):
        gv = g_ref[...]
        m_new = ADAM_B1 * m_ref[...] + (1.0 - ADAM_B1) * gv
        v_new = ADAM_B2 * v_ref[...] + (1.0 - ADAM_B2) * (gv * gv)
        m_hat = m_new / (1.0 - ADAM_B1 ** ADAM_STEP)
        v_hat = v_new / (1.0 - ADAM_B2 ** ADAM_STEP)
        d_ref[...] = -ADAM_LR * (m_hat / (jnp.sqrt(v_hat) + ADAM_EPS) + ADAM_WD * w_ref[...])
        nm_ref[...] = m_new
        nv_ref[...] = v_new

    blk = pl.BlockSpec((tr, cols), lambda i: (i, 0))
    return _pc(body, name=name, out_shape=[_sds((rows, cols), F32)] * 3, grid=(rows // tr,),
               in_specs=[blk] * 4, out_specs=[blk] * 3, semantics=("arbitrary",))(w, g, m, v)


def _my_pos():
    return lax.axis_index("x"), lax.axis_index("y"), lax.axis_index("c")


_HBM = pl.BlockSpec(memory_space=pltpu.HBM)
_SEM = pl.BlockSpec(memory_space=pltpu.SEMAPHORE)
_EFFECT = pltpu.SideEffectType.DATAFLOW_SIDE_EFFECTING


def _peers():
    x, y, c = _my_pos()
    out = []
    for k in range(1, N_DEV):
        px, py, pc = x ^ ((k >> 2) & 1), y ^ ((k >> 1) & 1), c ^ (k & 1)
        out.append(((px, py, pc), 4 * px + 2 * py + pc))
    return out


def _scatter_start(srcs, slot_of, *, name, order_after=None):
    n = len(srcs)
    lands = [lax.empty((N_DEV,) + slot_of(s, 0, shape_only=True), s.dtype) for s in srcs]
    extra = [] if order_after is None else [order_after]

    def body(*refs):
        src_refs, land_refs = refs[:n], refs[n:2 * n]
        send_sems, recv_sems = refs[2 * n + len(extra)], refs[2 * n + len(extra) + 1]
        token = refs[-1]
        x, y, c = _my_pos()
        me = 4 * x + 2 * y + c
        for a in range(n):
            for k, (peer, peer_idx) in enumerate(_peers()):
                pltpu.make_async_remote_copy(
                    src_ref=slot_of(src_refs[a], peer_idx), dst_ref=land_refs[a].at[me],
                    send_sem=send_sems.at[a * 7 + k], recv_sem=recv_sems.at[a * 7 + k],
                    device_id=peer, device_id_type=MESH).start()
        token[...] = jnp.zeros_like(token)

    hbm = lambda a: pltpu.HBM(a.shape, a.dtype)
    args = [pltpu.with_memory_space_constraint(a, pltpu.HBM) for a in list(srcs) + lands] + extra
    outs = pl.pallas_call(
        body, name=name,
        out_shape=(pltpu.SemaphoreType.DMA((7 * n,)), pltpu.SemaphoreType.DMA((7 * n,)),
                   *[hbm(a) for a in srcs], *[hbm(a) for a in lands], _sds((8, LANES), F32)),
        in_specs=[_HBM] * (2 * n) + [pl.BlockSpec(memory_space=pl.ANY)] * len(extra),
        out_specs=(_SEM, _SEM, *([_HBM] * (2 * n)), pl.BlockSpec(memory_space=pltpu.VMEM)),
        input_output_aliases={a: 2 + a for a in range(2 * n)},
        compiler_params=pltpu.CompilerParams(has_side_effects=_EFFECT))(*args)
    return outs[0], outs[1], list(outs[2:2 + n]), list(outs[2 + n:2 + 2 * n]), outs[-1]


def _scatter_wait(send_sems, recv_sems, srcs, lands, slot_of, after, *, name):
    n = len(srcs)

    def body(*refs):
        src_refs, land_refs = refs[:n], refs[n:2 * n]
        ssem, rsem = refs[2 * n], refs[2 * n + 1]
        x, y, c = _my_pos()
        me = 4 * x + 2 * y + c
        for a in range(n):
            for k, (peer, peer_idx) in enumerate(_peers()):
                cp = pltpu.make_async_remote_copy(
                    src_ref=slot_of(src_refs[a], peer_idx), dst_ref=land_refs[a].at[me],
                    send_sem=ssem.at[a * 7 + k], recv_sem=rsem.at[a * 7 + k],
                    device_id=peer, device_id_type=MESH)
                cp.wait_send()
                cp.wait_recv()

    hbm = lambda a: pltpu.HBM(a.shape, a.dtype)
    outs = pl.pallas_call(
        body, name=name, out_shape=tuple(hbm(a) for a in list(srcs) + list(lands)),
        in_specs=[_HBM] * (2 * n) + [_SEM, _SEM, pl.BlockSpec(memory_space=pl.ANY)],
        out_specs=tuple([_HBM] * (2 * n)), input_output_aliases={a: a for a in range(2 * n)},
        compiler_params=pltpu.CompilerParams(has_side_effects=_EFFECT))(*srcs, *lands, send_sems, recv_sems, after)
    return list(outs[:n]), list(outs[n:])


def _whole(a, peer_idx, shape_only=False):
    return a.shape if shape_only else a


def _slot(a, peer_idx, shape_only=False):
    return a.shape[1:] if shape_only else a.at[peer_idx]


def _sum_slots(a, *, name, tr=None):
    rows, cols = a.shape[1], a.shape[2]
    if tr is None:
        tr = rows
        while N_DEV * tr * cols * a.dtype.itemsize > 3 * 2 ** 20 and tr % 32 == 0:
            tr //= 2

    def body(a_ref, o_ref):
        acc = a_ref[0].astype(F32)
        for j in range(1, N_DEV):
            acc = acc + a_ref[j].astype(F32)
        o_ref[...] = acc

    return _pc(body, name=name, out_shape=_sds((rows, cols), F32), grid=(rows // tr,),
               in_specs=[pl.BlockSpec((N_DEV, tr, cols), lambda i: (0, i, 0))],
               out_specs=pl.BlockSpec((tr, cols), lambda i: (i, 0)), semantics=("arbitrary",), vmem=VMEM_BIG)(a)


def _to_slots(full, kind):
    if kind == "rows2":
        r, c = full.shape
        return full.reshape(N_DEV, r // N_DEV, c)
    if kind == "cols2":
        r, c = full.shape
        return full.reshape(r, N_DEV, c // N_DEV).transpose(1, 0, 2)
    if kind == "rows3":
        l, r, c = full.shape
        return full.reshape(l, N_DEV, r // N_DEV, c).transpose(1, 0, 2, 3)
    if kind == "cols3":
        l, r, c = full.shape
        return full.reshape(l, r, N_DEV, c // N_DEV).transpose(2, 0, 1, 3)
    raise ValueError(kind)


def _from_slots(slots, kind):
    if kind == "rows2":
        _, r, c = slots.shape
        return slots.reshape(N_DEV * r, c)
    if kind == "cols2":
        _, r, c = slots.shape
        return slots.transpose(1, 0, 2).reshape(r, N_DEV * c)
    if kind == "rows3":
        _, l, r, c = slots.shape
        return slots.transpose(1, 0, 2, 3).reshape(l, N_DEV * r, c)
    if kind == "cols3":
        _, l, r, c = slots.shape
        return slots.transpose(1, 2, 0, 3).reshape(l, r, N_DEV * c)
    raise ValueError(kind)


BIG = (("w_in_a", "rows2"), ("w_in_b", "rows2"), ("w_kv", "cols2"), ("w_memkv", "rows3"),
       ("w_out", "rows3"), ("w_up", "cols3"), ("w_down", "rows3"))


def _round_up(n, m):
    return -(-n // m) * m


def _pad_rows(a, rows, axis):
    pad = [(0, 0)] * a.ndim
    pad[axis] = (0, rows - a.shape[axis])
    return jnp.pad(a, pad)


def kernel(x, mem, ln_mix_g, w_in_a, b_f_a, w_in_b, ln_kv_g, w_kv, ln_mem_g, w_memkv, w_out, ln_ffn_g, w_up, conv_w, conv_b, w_down, final_g, loss_target, m_ln_mix_g, m_w_in_a, m_b_f_a, m_w_in_b, m_ln_kv_g, m_w_kv, m_ln_mem_g, m_w_memkv, m_w_out, m_ln_ffn_g, m_w_up, m_conv_w, m_conv_b, m_w_down, m_final_g, v_ln_mix_g, v_w_in_a, v_b_f_a, v_w_in_b, v_ln_kv_g, v_w_kv, v_ln_mem_g, v_w_memkv, v_w_out, v_ln_ffn_g, v_w_up, v_conv_w, v_conv_b, v_w_down, v_final_g):
    B, S, D = x.shape
    NM = mem.shape[1]
    T = B * S
    F = w_down.shape[1] * N_DEV
    my_idx = 4 * lax.axis_index("x") + 2 * lax.axis_index("y") + lax.axis_index("c")

    shards = {"w_in_a": w_in_a[0], "w_in_b": w_in_b[0], "w_kv": w_kv, "w_memkv": w_memkv, "w_out": w_out,
              "w_up": w_up, "w_down": w_down}
    moms = {"w_in_a": (m_w_in_a[0], v_w_in_a[0]), "w_in_b": (m_w_in_b[0], v_w_in_b[0]), "w_kv": (m_w_kv, v_w_kv),
            "w_memkv": (m_w_memkv, v_w_memkv), "w_out": (m_w_out, v_w_out), "w_up": (m_w_up, v_w_up),
            "w_down": (m_w_down, v_w_down)}

    groups = [("a1", [("w_in_a", None)]),
              ("a2", [("w_memkv", None), ("w_out", None), ("conv_w", None)]),
              ("b0", [("w_up", 0), ("w_down", 0)]), ("a3", [("w_in_b", None), ("w_kv", None)]),
              ("b1", [("w_up", 1), ("w_down", 1)])]
    sources = dict(shards, conv_w=conv_w)
    started, token = {}, None
    for gname, members in groups:
        srcs = []
        for n, layer in members:
            a = sources[n] if layer is None else sources[n][layer]
            srcs.append(a if n == "conv_w" else a.astype(BF16))
        ssem, rsem, thru, lands, token = _scatter_start(srcs, _whole, name=f"gather_start_{gname}", order_after=token)
        started[gname] = (ssem, rsem, thru, lands)

    def gathered(gname, after):
        ssem, rsem, thru, lands = started[gname]
        thru, lands = _scatter_wait(ssem, rsem, thru, lands, _whole, after, name=f"gather_wait_{gname}")
        return [lax.dynamic_update_index_in_dim(land, s, my_idx, 0) for land, s in zip(lands, thru)]

    full = {}
    (g_wa,) = gathered("a1", token)
    full["w_in_a"] = _from_slots(g_wa, "rows2")

    wa = full["w_in_a"]
    n_qkv = 3 * MAIN_W
    wa = jnp.concatenate([wa[:, :n_qkv], wa[:, n_qkv + N_MAIN_HEADS:], wa[:, n_qkv:n_qkv + N_MAIN_HEADS],
                          jnp.zeros((D, LANES - N_MAIN_HEADS), BF16)], axis=1)
    n_main = n_qkv + MEM_W
    full["w_up"], full["w_down"] = {}, {}
    b_f =_pad_rows(b_f_a.reshape(1, N_MAIN_HEADS), LANES, 1)

    x2d = x.reshape(T, D)
    mem2d = mem.reshape(B * NM, D)
    tgt2d = loss_target.reshape(T, D)
    PM, PX = N_MAIN_HEADS // 2, N_MEM_HEADS // 2

    def stats_to_heads(c2d):
        c = c2d.reshape(B, S, LANES)[:, :, :N_MAIN_HEADS].transpose(0, 2, 1)
        return c[:, :, None, :]

    def mem_kv(layer):
        return _mm_fwd(mem2d, full["w_memkv"][layer], name=f"memkv{layer}", tm=B * NM, tn=2 * MEM_W,
                       out_dtype=BF16, g=ln_mem_g[layer], save_h=True)

    def conv_ffn_fwd(xin, layer):
        uc, ub, a, h = _ffn_up_gate(xin, ln_ffn_g[layer], full["w_up"][layer], conv_w_full[layer],
                                    conv_b[layer].reshape(1, 2 * F), name=f"ffn_up{layer}", S=S)
        xo = _mm_fwd(a, full["w_down"][layer], name=f"ffn_down{layer}", tm=min(1024, T), tn=512, out_dtype=F32, res=xin)
        return xo, (uc, ub, h, a)

    proj_a, h_mix0 = _mm_fwd(x2d, wa, name="in_proj_a", tm=min(1024, T), tn=512, out_dtype=BF16, g=ln_mix_g[0],
                             ncols=n_main, save_h=True)
    f_logit = _mm_fwd(x2d, wa, name="in_proj_f", tm=min(1024, T), tn=LANES, out_dtype=F32, g=ln_mix_g[0],
                      col0=n_main // LANES, ncols=LANES)
    c2d = _forget_cumsum(f_logit, b_f, B=B, S=S, name="forget_cumsum")
    cr = stats_to_heads(c2d)
    o_main0, lse0 = _fox_fwd_g(proj_a, proj_a, proj_a, cr, name="fox_fwd", B=B, S=S, P=PM, q_cb=0, k_cb=PM, v_cb=2 * PM)
    g_wmem, g_wout, g_cw = gathered("a2", lse0)
    full["w_memkv"] = _from_slots(g_wmem, "rows3")
    full["w_out"] = _from_slots(g_wout, "rows3")
    conv_w_full = _from_slots(g_cw, "cols3")
    memkv0, h_mem0 = mem_kv(0)
    o_mem0, lse_m0 = _softmax_fwd(proj_a, memkv0, memkv0, name="mem_fwd0", B=B, S=S, Sk=NM, P=PX, q_cb=3 * PM,
                                  k_cb=0, v_cb=PX, causal=False)
    o_cat0 = jnp.concatenate([o_main0, o_mem0], axis=1)
    x1 = _mm_fwd(o_cat0, full["w_out"][0], name="out_proj0", tm=min(1024, T), tn=1024, out_dtype=F32, res=x2d)
    g_up, g_dn = gathered("b0", x1)
    full["w_up"][0], full["w_down"][0] = _from_slots(g_up, "cols2"), _from_slots(g_dn, "rows2")
    x2, ffn_saved0 = conv_ffn_fwd(x1, 0)
    g_wb, g_wkv = gathered("a3", x2)
    wb, wkv = _from_slots(g_wb, "rows2"), _from_slots(g_wkv, "cols2")
    kv, h_kv =_mm_fwd(x2, wkv, name="kv_proj", tm=min(1024, T), tn=768, out_dtype=BF16, g=ln_kv_g, save_h=True)
    proj_b, h_mix1 = _mm_fwd(x2, wb, name="in_proj_b", tm=min(1024, T), tn=1024, out_dtype=BF16, g=ln_mix_g[1],
                             save_h=True)
    o_main1, rt1 = _sb_fwd_g(proj_b, kv, kv, name="sb_fwd", B=B, S=S, P=PM, q_cb=0, k_cb=0, v_cb=PM)
    memkv1, h_mem1 = mem_kv(1)
    o_mem1, lse_m1 = _softmax_fwd(proj_b, memkv1, memkv1, name="mem_fwd1", B=B, S=S, Sk=NM, P=PX, q_cb=PM,
                                  k_cb=0, v_cb=PX, causal=False)
    o_cat1 = jnp.concatenate([o_main1, o_mem1], axis=1)
    x3 = _mm_fwd(o_cat1, full["w_out"][1], name="out_proj1", tm=min(1024, T), tn=1024, out_dtype=F32, res=x2)
    g_up, g_dn = gathered("b1", x3)
    full["w_up"][1], full["w_down"][1] = _from_slots(g_up, "cols2"), _from_slots(g_dn, "rows2")
    x4, ffn_saved1 = conv_ffn_fwd(x3, 1)
    dx4, dg_final, loss_part = _loss_head(x4, final_g, tgt2d, name="loss_head")

    grads = {}
    small = {}
    reduce_groups = []

    def start_reduce(gname, keys, kinds):
        slots = [_to_slots(grads[k], kind) for k, kind in zip(keys, kinds)]
        ssem, rsem, thru, lands, tok = _scatter_start(slots, _slot, name=f"reduce_start_{gname}")
        reduce_groups.append((gname, keys, ssem, rsem, thru, lands))
        return tok[0, 0]

    def conv_ffn_bwd(dxo, xin, saved, layer):
        uc, ub, h, a = saved
        w_dn = full["w_down"][layer]
        da = _mm_nt(dxo, w_dn, name=f"d_act{layer}", tm=min(1024, T), tn=F // 2, out_dtype=F32)
        grads[("w_down", layer)] = _wgrad(a, dxo, f"g_w_down{layer}")
        cwl = conv_w_full[layer]
        du_g, du_v, p_g, p_v = _conv_gate_bwd(da, uc, ub, cwl, name=f"conv_bwd{layer}", B=B, S=S)
        small[("conv_w", layer)] = jnp.concatenate([p_g[0:3], p_v[0:3]], axis=1)
        small[("conv_b", layer)] = jnp.concatenate([p_g[3], p_v[3]], axis=0)
        grads[("w_up", layer)] = jnp.concatenate(
            [_wgrad(h, du_g, f"g_w_up_gate{layer}"), _wgrad(h, du_v, f"g_w_up_val{layer}")], axis=1)
        tok = start_reduce(f"ffn{layer}", [("w_down", layer), ("w_up", layer)], ["rows2", "cols2"])
        dxi, dg = _mm_nt_rmsbwd([(du_g, 0), (du_v, 1)], full["w_up"][layer], xin, ln_ffn_g[layer] + tok,
                                name=f"d_ffn_in{layer}", dres=dxo)
        small[("ln_ffn_g", layer)] = dg[0]
        return dxi

    def mem_bwd(proj, q_cb, memkv, h_mem, do_cat, o_mem, lse_m, layer):
        dqm, dmk, dmv = _softmax_bwd(proj, memkv, memkv, do_cat, o_mem, lse_m, name=f"mem_bwd{layer}", B=B, S=S,
                                     Sk=NM, P=PX, q_cb=q_cb, k_cb=0, v_cb=PX, do_cb=PM, causal=False)
        grads[("w_memkv", layer)] = jnp.concatenate(
            [_wgrad(h_mem, dmk, f"g_w_memk{layer}"), _wgrad(h_mem, dmv, f"g_w_memv{layer}")], axis=1)
        _, dg = _mm_nt_rmsbwd([(dmk, 0), (dmv, 1)], full["w_memkv"][layer], mem2d, ln_mem_g[layer],
                              name=f"d_mem_in{layer}", want_dx=False)
        small[("ln_mem_g", layer)] = dg[0]
        return dqm

    dx3 = conv_ffn_bwd(dx4, x3, ffn_saved1, 1)
    do_cat1 = _mm_nt(dx3, full["w_out"][1], name="d_o_cat1", tm=min(1024, T), tn=1024, out_dtype=BF16)
    grads[("w_out", 1)] = _wgrad(o_cat1, dx3, "g_w_out1")
    dq1, dk1, dv1 = _sb_bwd_g(proj_b, kv, kv, do_cat1, rt1, name="sb_bwd", B=B, S=S, P=PM, q_cb=0, k_cb=0, v_cb=PM,
                            do_cb=0)
    dqm1 = mem_bwd(proj_b, PM, memkv1, h_mem1, do_cat1, o_mem1, lse_m1, 1)
    grads["w_in_b"] = jnp.concatenate([_wgrad(h_mix1, dq1, "g_w_in_b_q"), _wgrad(h_mix1, dqm1, "g_w_in_b_m")], axis=1)
    grads["w_kv"] = jnp.concatenate([_wgrad(h_kv, dk1, "g_w_kv_k"), _wgrad(h_kv, dv1, "g_w_kv_v")], axis=1)
    tok = start_reduce("mix1", [("w_out", 1), "w_in_b", "w_kv", ("w_memkv", 1)], ["rows2", "rows2", "cols2", "rows2"])
    dx2, dg = _mm_nt_rmsbwd([(dq1, 0), (dqm1, MAIN_W // MEM_W)], wb, x2, ln_mix_g[1] + tok, name="d_mix_in1", dres=dx3)
    small[("ln_mix_g", 1)] = dg[0]
    dx2, dg = _mm_nt_rmsbwd([(dk1, 0), (dv1, 1)], wkv, x2, ln_kv_g, name="d_kv_in", dres=dx2)
    small["ln_kv_g"] = dg[0]
    dx1 = conv_ffn_bwd(dx2, x1, ffn_saved0, 0)
    do_cat0 = _mm_nt(dx1, full["w_out"][0], name="d_o_cat0", tm=min(1024, T), tn=1024, out_dtype=BF16)
    grads[("w_out", 0)] = _wgrad(o_cat0, dx1, "g_w_out0")
    dq0, dk0, dv0, dcs = _fox_bwd_g(proj_a, proj_a, proj_a, do_cat0, lse0, cr, name="fox_bwd", B=B, S=S, P=PM, q_cb=0,
                                  k_cb=PM, v_cb=2 * PM, do_cb=0)
    dqm0 = mem_bwd(proj_a, 3 * PM, memkv0, h_mem0, do_cat0, o_mem0, lse_m0, 0)
    dc2d = _pad_rows(dcs[:, :, 0, :].transpose(0, 2, 1).reshape(T, N_MAIN_HEADS), LANES, 1)
    df, db_f = _forget_cumsum_bwd(dc2d, f_logit, b_f, B=B, S=S, name="forget_cumsum_bwd")
    a_parts = [(dq0, 0), (dk0, 1), (dv0, 2), (dqm0, n_qkv // MEM_W), (df, n_main // LANES)]
    g_wa = jnp.concatenate([_wgrad(h_mix0, p, f"g_w_in_a{k}") for k, (p, _) in enumerate(a_parts)], axis=1)
    grads["w_in_a"] = jnp.concatenate([g_wa[:, :n_qkv], g_wa[:, n_main:n_main + N_MAIN_HEADS], g_wa[:, n_qkv:n_main]],
                                      axis=1)
    tok = start_reduce("mix0", [("w_out", 0), ("w_memkv", 0), "w_in_a"], ["rows2", "rows2", "rows2"])
    dx0, dg = _mm_nt_rmsbwd(a_parts, wa, x2d, ln_mix_g[0] + tok, name="d_mix_in0", dres=dx1)
    small[("ln_mix_g", 0)] = dg[0]
    grad_x = dx0.reshape(B, S, D)

    def both_small(name):
        return jnp.stack([small[(name, 0)], small[(name, 1)]])

    small_list = [("ln_mix_g", both_small("ln_mix_g")), ("b_f_a", db_f[:, :N_MAIN_HEADS]), ("ln_kv_g", small["ln_kv_g"]),
                  ("ln_mem_g", both_small("ln_mem_g")), ("ln_ffn_g", both_small("ln_ffn_g")),
                  ("conv_w", both_small("conv_w")), ("conv_b", both_small("conv_b")), ("final_g", dg_final[0]),
                  ("loss", loss_part[0, :1])]
    sm_rows = []
    for _, a in small_list:
        flat = a.reshape(-1)
        sm_rows.append(_pad_rows(flat, _round_up(flat.size, 8 * LANES), 0).reshape(-1, LANES))
    spack = jnp.concatenate(sm_rows, axis=0)
    s_ssem, s_rsem, s_thru, s_lands, s_tok = _scatter_start([spack], _whole, name="small_start")

    pieces = {}
    for gname, keys, ssem, rsem, thru, lands in reduce_groups:
        thru, lands = _scatter_wait(ssem, rsem, thru, lands, _slot, s_tok, name=f"reduce_wait_{gname}")
        for key, mine, land in zip(keys, thru, lands):
            own = lax.dynamic_index_in_dim(mine, my_idx, 0, keepdims=False)
            land = lax.dynamic_update_index_in_dim(land, own, my_idx, 0)
            tag = key if isinstance(key, str) else f"{key[0]}{key[1]}"
            pieces[key] = _sum_slots(land, name=f"sum_{tag}")

    red = {}
    for n in ("w_in_a", "w_in_b", "w_kv"):
        red[n] = pieces[n].reshape(shards[n].shape)
    for n in ("w_memkv", "w_out", "w_up", "w_down"):
        red[n] = jnp.stack([pieces[(n, 0)], pieces[(n, 1)]])

    weights = {"ln_mix_g": ln_mix_g, "w_in_a": w_in_a, "b_f_a": b_f_a, "w_in_b": w_in_b, "ln_kv_g": ln_kv_g,
               "w_kv": w_kv, "ln_mem_g": ln_mem_g, "w_memkv": w_memkv, "w_out": w_out, "ln_ffn_g": ln_ffn_g,
               "w_up": w_up, "conv_w": conv_w, "conv_b": conv_b, "w_down": w_down, "final_g": final_g}
    m_in = {"ln_mix_g": m_ln_mix_g, "w_in_a": m_w_in_a, "b_f_a": m_b_f_a, "w_in_b": m_w_in_b, "ln_kv_g": m_ln_kv_g,
            "w_kv": m_w_kv, "ln_mem_g": m_ln_mem_g, "w_memkv": m_w_memkv, "w_out": m_w_out, "ln_ffn_g": m_ln_ffn_g,
            "w_up": m_w_up, "conv_w": m_conv_w, "conv_b": m_conv_b, "w_down": m_w_down, "final_g": m_final_g}
    v_in = {"ln_mix_g": v_ln_mix_g, "w_in_a": v_w_in_a, "b_f_a": v_b_f_a, "w_in_b": v_w_in_b, "ln_kv_g": v_ln_kv_g,
            "w_kv": v_w_kv, "ln_mem_g": v_ln_mem_g, "w_memkv": v_w_memkv, "w_out": v_w_out, "ln_ffn_g": v_ln_ffn_g,
            "w_up": v_w_up, "conv_w": v_conv_w, "conv_b": v_conv_b, "w_down": v_w_down, "final_g": v_final_g}
    order = list(weights)
    big_names = [n for n, _ in BIG]
    g_out, d_out, nm_out, nv_out = {}, {}, {}, {}

    def update(n):
        w = weights[n]
        cols = w.shape[-1]
        g = red[n].reshape(w.shape)
        d, nm, nv = _adamw(w.reshape(-1, cols), g.reshape(-1, cols), m_in[n].reshape(-1, cols),
                           v_in[n].reshape(-1, cols), name=f"adamw_{n}")
        g_out[n], d_out[n], nm_out[n], nv_out[n] = g, d.reshape(w.shape), nm.reshape(w.shape), nv.reshape(w.shape)

    for n in big_names:
        update(n)
    all_updated = jnp.stack([d_out[n].reshape(-1)[0] for n in big_names])
    s_thru, s_lands = _scatter_wait(s_ssem, s_rsem, s_thru, s_lands, _whole, all_updated, name="small_wait")
    ssum = _sum_slots(lax.dynamic_update_index_in_dim(s_lands[0], s_thru[0], my_idx, 0), name="sum_small")
    off = 0
    for (n, a), rows in zip(small_list, sm_rows):
        red[n] = ssum[off:off + rows.shape[0]].reshape(-1)[:a.size].reshape(a.shape)
        off += rows.shape[0]
    loss = red["loss"][0]
    shard_cols = conv_w.shape[2]
    red["conv_w"] = lax.dynamic_slice_in_dim(red["conv_w"], my_idx * shard_cols, shard_cols, axis=2)
    red["b_f_a"] = red["b_f_a"].reshape(b_f_a.shape)
    update("conv_w")
    small_names = [n for n in order if n not in g_out]

    def pack_small(src):
        rows = []
        for n in small_names:
            flat = src[n].reshape(-1)
            rows.append(_pad_rows(flat, _round_up(flat.size, 8 * LANES), 0).reshape(-1, LANES))
        return jnp.concatenate(rows, axis=0), [r.shape[0] for r in rows]

    red_small = {n: red[n].reshape(weights[n].shape) for n in small_names}
    wp, counts = pack_small(weights)
    gp, _ = pack_small(red_small)
    mp, _ = pack_small(m_in)
    vp, _ = pack_small(v_in)
    dp, nmp, nvp = _adamw(wp, gp, mp, vp, name="adamw_small")
    off = 0
    for n, cnt in zip(small_names, counts):
        shp = weights[n].shape
        size = weights[n].size
        g_out[n] = red_small[n]
        d_out[n] = dp[off:off + cnt].reshape(-1)[:size].reshape(shp)
        nm_out[n] = nmp[off:off + cnt].reshape(-1)[:size].reshape(shp)
        nv_out[n] = nvp[off:off + cnt].reshape(-1)[:size].reshape(shp)
        off += cnt

    return (loss, grad_x, *[g_out[n] for n in order], *[d_out[n] for n in order],
            *[nm_out[n] for n in order], *[nv_out[n] for n in order])
```

```python
import functools

import jax
import jax.numpy as jnp
from jax import lax
from jax.experimental import pallas as pl
from jax.experimental.pallas import tpu as pltpu

F32 = jnp.float32
BF16 = jnp.bfloat16
LANES = 128
HEAD_DIM = 64
N_MAIN_HEADS = 12
N_MEM_HEADS = 4
MAIN_W = N_MAIN_HEADS * HEAD_DIM
MEM_W = N_MEM_HEADS * HEAD_DIM
SCALE = HEAD_DIM ** -0.5
EPS = 1e-6
NEG = -1e30
N_DEV = 8
ATT_TILE = 256
MEM_Q_TILE = 1024
VMEM_BIG = 56 * 2 ** 20
MESH = pl.DeviceIdType.MESH

ADAM_LR = 0.001
ADAM_B1 = 0.9
ADAM_B2 = 0.999
ADAM_EPS = 1e-08
ADAM_WD = 0.01
ADAM_STEP = 10

NT = (((1,), (1,)), ((), ()))
TN = (((0,), (0,)), ((), ()))


def _pc(body, *, name, out_shape, grid=None, in_specs=None, out_specs=None, scratch_shapes=(),
        semantics=None, vmem=None):
    kw = {}
    if grid is not None:
        kw["grid"] = grid
    params = pltpu.CompilerParams(dimension_semantics=semantics, vmem_limit_bytes=vmem)
    return pl.pallas_call(body, name=name, out_shape=out_shape, in_specs=in_specs, out_specs=out_specs,
                          scratch_shapes=list(scratch_shapes), compiler_params=params, **kw)


def _sds(shape, dtype):
    return jax.ShapeDtypeStruct(shape, dtype)


def _mm_fwd(a, w, *, name, tm, tn, out_dtype, g=None, res=None, col0=0, ncols=None, save_h=False):
    m_rows, k = a.shape
    n = w.shape[1] if ncols is None else ncols
    grid = (m_rows // tm, n // tn)
    norm = g is not None

    def body(*refs):
        refs = list(refs)
        a_ref = refs.pop(0)
        g_ref = refs.pop(0) if norm else None
        w_ref = refs.pop(0)
        res_ref = refs.pop(0) if res is not None else None
        o_ref = refs.pop(0)
        hout_ref = refs.pop(0) if save_h else None
        h_ref = refs.pop(0) if norm else None
        if norm:
            @pl.when(pl.program_id(1) == 0)
            def _():
                xv = a_ref[...]
                r = lax.rsqrt(jnp.mean(xv * xv, axis=-1, keepdims=True) + EPS)
                h = ((xv * r) * g_ref[...]).astype(BF16)
                h_ref[...] = h
                if save_h:
                    hout_ref[...] = h
            lhs = h_ref[...]
        else:
            lhs = a_ref[...].astype(BF16)
        acc = jnp.dot(lhs, w_ref[...], preferred_element_type=F32)
        if res is not None:
            acc = acc + res_ref[...]
        o_ref[...] = acc.astype(out_dtype)

    in_specs = [pl.BlockSpec((tm, k), lambda i, j: (i, 0))]
    args = [a]
    if norm:
        in_specs.append(pl.BlockSpec((1, k), lambda i, j: (0, 0)))
        args.append(g.reshape(1, k))
    in_specs.append(pl.BlockSpec((k, tn), lambda i, j: (0, j + col0)))
    args.append(w)
    if res is not None:
        in_specs.append(pl.BlockSpec((tm, tn), lambda i, j: (i, j)))
        args.append(res)
    out_shape = [_sds((m_rows, n), out_dtype)]
    out_specs = [pl.BlockSpec((tm, tn), lambda i, j: (i, j))]
    if save_h:
        out_shape.append(_sds((m_rows, k), BF16))
        out_specs.append(pl.BlockSpec((tm, k), lambda i, j: (i, 0)))
    scratch = [pltpu.VMEM((tm, k), BF16)] if norm else []
    outs = _pc(body, name=name, out_shape=out_shape, grid=grid, in_specs=in_specs, out_specs=out_specs,
               scratch_shapes=scratch, semantics=("arbitrary", "arbitrary"), vmem=VMEM_BIG)(*args)
    return outs if save_h else outs[0]


def _mm_nt(a, w, *, name, tm, tn, out_dtype):
    m_rows, k = a.shape
    n = w.shape[0]

    def body(a_ref, w_ref, o_ref):
        acc = lax.dot_general(a_ref[...].astype(BF16), w_ref[...], NT, preferred_element_type=F32)
        o_ref[...] = acc.astype(out_dtype)

    return _pc(body, name=name, out_shape=_sds((m_rows, n), out_dtype), grid=(m_rows // tm, n // tn),
               in_specs=[pl.BlockSpec((tm, k), lambda i, j: (i, 0)), pl.BlockSpec((tn, k), lambda i, j: (j, 0))],
               out_specs=pl.BlockSpec((tm, tn), lambda i, j: (i, j)),
               semantics=("arbitrary", "arbitrary"), vmem=VMEM_BIG)(a, w)


def _mm_tn(a, b, *, name, ta, tn, tt):
    t_rows, ka = a.shape
    n = b.shape[1]
    nt = t_rows // tt

    def body(a_ref, b_ref, o_ref, acc_ref):
        t = pl.program_id(2)

        @pl.when(t == 0)
        def _():
            acc_ref[...] = jnp.zeros_like(acc_ref)

        acc_ref[...] += lax.dot_general(a_ref[...].astype(BF16), b_ref[...].astype(BF16), TN,
                                        preferred_element_type=F32)

        @pl.when(t == nt - 1)
        def _():
            o_ref[...] = acc_ref[...].astype(BF16)

    return _pc(body, name=name, out_shape=_sds((ka, n), BF16), grid=(ka // ta, n // tn, nt),
               in_specs=[pl.BlockSpec((tt, ta), lambda i, j, t: (t, i)),
                         pl.BlockSpec((tt, tn), lambda i, j, t: (t, j))],
               out_specs=pl.BlockSpec((ta, tn), lambda i, j, t: (i, j)),
               scratch_shapes=[pltpu.VMEM((ta, tn), F32)],
               semantics=("arbitrary", "arbitrary", "arbitrary"), vmem=VMEM_BIG)(a, b)


def _wgrad(a, b, name):
    t_rows, ka = a.shape
    n = b.shape[1]
    ta = ka if ka <= 1024 else ka // 2
    tn = n
    while ta * tn * 4 > 6 * 2 ** 20 and tn % 256 == 0:
        tn //= 2
    tt = min(1024, t_rows)
    return _mm_tn(a, b, name=name, ta=ta, tn=tn, tt=tt)


def _mm_nt_rmsbwd(parts, w, x, g, *, name, dres=None, want_dx=True):
    m_rows, d = x.shape
    k_total = sum(dy.shape[1] for dy, _ in parts)
    tm = min(512 if k_total <= 2816 else 256, m_rows)
    n_parts = len(parts)

    def body(*refs):
        refs = list(refs)
        dy_refs = [refs.pop(0) for _ in range(n_parts)]
        w_refs = [refs.pop(0) for _ in range(n_parts)]
        x_ref = refs.pop(0)
        g_ref = refs.pop(0)
        dres_ref = refs.pop(0) if dres is not None else None
        dx_ref = refs.pop(0) if want_dx else None
        dg_ref = refs.pop(0)

        @pl.when(pl.program_id(0) == 0)
        def _():
            dg_ref[...] = jnp.zeros_like(dg_ref)

        dh = None
        for dy_ref, w_ref in zip(dy_refs, w_refs):
            t = lax.dot_general(dy_ref[...].astype(BF16), w_ref[...], NT, preferred_element_type=F32)
            dh = t if dh is None else dh + t
        xv = x_ref[...]
        r = lax.rsqrt(jnp.mean(xv * xv, axis=-1, keepdims=True) + EPS)
        xh = xv * r
        dg_ref[...] += jnp.sum(dh * xh, axis=0, keepdims=True)
        if want_dx:
            dhg = dh * g_ref[...]
            dx = r * (dhg - xh * jnp.mean(dhg * xh, axis=-1, keepdims=True))
            if dres is not None:
                dx = dx + dres_ref[...]
            dx_ref[...] = dx

    in_specs, args = [], []
    for dy, _ in parts:
        in_specs.append(pl.BlockSpec((tm, dy.shape[1]), lambda i: (i, 0)))
        args.append(dy)
    for dy, cb in parts:
        in_specs.append(pl.BlockSpec((d, dy.shape[1]), functools.partial(lambda i, cb: (0, cb), cb=cb)))
        args.append(w)
    in_specs += [pl.BlockSpec((tm, d), lambda i: (i, 0)), pl.BlockSpec((1, d), lambda i: (0, 0))]
    args += [x, g.reshape(1, d)]
    if dres is not None:
        in_specs.append(pl.BlockSpec((tm, d), lambda i: (i, 0)))
        args.append(dres)
    out_shape, out_specs = [], []
    if want_dx:
        out_shape.append(_sds((m_rows, d), F32))
        out_specs.append(pl.BlockSpec((tm, d), lambda i: (i, 0)))
    out_shape.append(_sds((1, d), F32))
    out_specs.append(pl.BlockSpec((1, d), lambda i: (0, 0)))
    outs = _pc(body, name=name, out_shape=out_shape, grid=(m_rows // tm,), in_specs=in_specs,
               out_specs=out_specs, semantics=("arbitrary",), vmem=VMEM_BIG)(*args)
    return (outs[0], outs[1]) if want_dx else (None, outs[0])


def _loss_head(x, g, tgt, *, name):
    m_rows, d = x.shape
    tm = min(512, m_rows)

    def body(x_ref, g_ref, t_ref, dx_ref, dg_ref, loss_ref):
        @pl.when(pl.program_id(0) == 0)
        def _():
            dg_ref[...] = jnp.zeros_like(dg_ref)
            loss_ref[...] = jnp.zeros_like(loss_ref)

        xv = x_ref[...]
        r = lax.rsqrt(jnp.mean(xv * xv, axis=-1, keepdims=True) + EPS)
        xh = xv * r
        gv = g_ref[...]
        err = xh * gv - t_ref[...]
        per_tok = jnp.mean(err * err, axis=-1, keepdims=True)
        loss_ref[...] += 0.5 * jnp.sum(per_tok, axis=0, keepdims=True)
        dout = err * (1.0 / d)
        dg_ref[...] += jnp.sum(dout * xh, axis=0, keepdims=True)
        dhg = dout * gv
        dx_ref[...] = r * (dhg - xh * jnp.mean(dhg * xh, axis=-1, keepdims=True))

    row = pl.BlockSpec((tm, d), lambda i: (i, 0))
    return _pc(body, name=name, out_shape=[_sds((m_rows, d), F32), _sds((1, d), F32), _sds((1, LANES), F32)],
               grid=(m_rows // tm,), in_specs=[row, pl.BlockSpec((1, d), lambda i: (0, 0)), row],
               out_specs=[row, pl.BlockSpec((1, d), lambda i: (0, 0)), pl.BlockSpec((1, LANES), lambda i: (0, 0))],
               semantics=("arbitrary",))(x, g.reshape(1, d), tgt)


def _split3(v):
    hi = v.astype(BF16)
    r1 = v - hi.astype(F32)
    mid = r1.astype(BF16)
    lo = (r1 - mid.astype(F32)).astype(BF16)
    return hi, mid, lo


def _split2(v):
    hi = v.astype(BF16)
    lo = (v - hi.astype(F32)).astype(BF16)
    return hi, lo


def _tri_dot3(tri, v):
    hi, mid, lo = _split3(v)
    return (jnp.dot(tri, hi, preferred_element_type=F32) + jnp.dot(tri, mid, preferred_element_type=F32)
            + jnp.dot(tri, lo, preferred_element_type=F32))


def _log_sigmoid(v):
    return jnp.minimum(v, 0.0) - jnp.log(1.0 + jnp.exp(-jnp.abs(v)))


def _forget_cumsum(f_logit, b_f, *, B, S, name):
    ch = min(256, S)
    nch = S // ch

    def body(f_ref, b_ref, c_ref):
        r_i = lax.broadcasted_iota(jnp.int32, (ch, ch), 0)
        c_i = lax.broadcasted_iota(jnp.int32, (ch, ch), 1)
        tri = (c_i <= r_i).astype(BF16)
        bv = b_ref[...]

        def step(k, carry):
            rows = pl.ds(pl.multiple_of(k * ch, ch), ch)
            lf = _log_sigmoid(f_ref[rows, :] + bv)
            c_ref[rows, :] = _tri_dot3(tri, lf) + carry
            return carry + jnp.sum(lf, axis=0, keepdims=True)

        lax.fori_loop(0, nch, step, jnp.zeros((1, LANES), F32))

    blk = pl.BlockSpec((S, LANES), lambda b: (b, 0))
    return _pc(body, name=name, out_shape=_sds((B * S, LANES), F32), grid=(B,),
               in_specs=[blk, pl.BlockSpec((1, LANES), lambda b: (0, 0))], out_specs=blk,
               semantics=("arbitrary",))(f_logit, b_f)


def _forget_cumsum_bwd(dc, f_logit, b_f, *, B, S, name):
    ch = min(256, S)
    nch = S // ch

    def body(dc_ref, f_ref, b_ref, df_ref, db_ref):
        @pl.when(pl.program_id(0) == 0)
        def _():
            db_ref[...] = jnp.zeros_like(db_ref)

        r_i = lax.broadcasted_iota(jnp.int32, (ch, ch), 0)
        c_i = lax.broadcasted_iota(jnp.int32, (ch, ch), 1)
        tri = (c_i >= r_i).astype(BF16)
        bv = b_ref[...]

        def step(kk, carry):
            tail, dbs = carry
            k = nch - 1 - kk
            rows = pl.ds(pl.multiple_of(k * ch, ch), ch)
            dcv = dc_ref[rows, :]
            dlf = _tri_dot3(tri, dcv) + tail
            z = f_ref[rows, :] + bv
            df = dlf * (1.0 / (1.0 + jnp.exp(z)))
            df_ref[rows, :] = df.astype(BF16)
            return tail + jnp.sum(dcv, axis=0, keepdims=True), dbs + jnp.sum(df, axis=0, keepdims=True)

        zero = jnp.zeros((1, LANES), F32)
        _, dbs = lax.fori_loop(0, nch, step, (zero, zero))
        db_ref[...] += dbs

    blk = pl.BlockSpec((S, LANES), lambda b: (b, 0))
    one = pl.BlockSpec((1, LANES), lambda b: (0, 0))
    return _pc(body, name=name, out_shape=[_sds((B * S, LANES), BF16), _sds((1, LANES), F32)], grid=(B,),
               in_specs=[blk, blk, one], out_specs=[blk, one], semantics=("arbitrary",))(dc, f_logit, b_f)


def _head_mask(lane, hh):
    return (lane < HEAD_DIM) if hh == 0 else (lane >= HEAD_DIM)


def _col_spec(rows, nblk_rows, cb):
    return pl.BlockSpec((rows, LANES), lambda b, p, i: (b * nblk_rows + i, cb + p))


def _kv_spec(rows, cb):
    return pl.BlockSpec((rows, LANES), lambda b, p, i: (b, cb + p))


def _stat_col_spec(tq):
    return pl.BlockSpec((1, 2, tq, 1), lambda b, p, i: (b, p, i, 0))


def _stat_row_spec(S):
    return pl.BlockSpec((1, 2, 1, S), lambda b, p, i: (b, p, 0, 0))


def _softmax_fwd(qa, ka, va, *, name, B, S, Sk, P, q_cb, k_cb, v_cb, causal, cc=None, cr=None):
    tq = min(ATT_TILE if causal else MEM_Q_TILE, S)
    tk = min(ATT_TILE, Sk)
    nq, nk = S // tq, Sk // tk
    decay = cc is not None
    assert not causal or (tq == tk and S == Sk)

    def body(*refs):
        if decay:
            q_ref, k_ref, v_ref, cc_ref, cr_ref, o_ref, lse_ref = refs
        else:
            q_ref, k_ref, v_ref, o_ref, lse_ref = refs
        i = pl.program_id(2)
        q = q_ref[...]
        lane = lax.broadcasted_iota(jnp.int32, (tq, LANES), 1)
        row = lax.broadcasted_iota(jnp.int32, (tq, tk), 0) + i * tq
        col0 = lax.broadcasted_iota(jnp.int32, (tq, tk), 1)
        outs = []
        for hh in range(2):
            qh = jnp.where(_head_mask(lane, hh), q, jnp.zeros_like(q))

            def step(kb, carry, hh=hh, qh=qh):
                m, l, acc = carry
                ks = pl.multiple_of(kb * tk, tk)
                kblk = k_ref[pl.ds(ks, tk), :]
                vblk = v_ref[pl.ds(ks, tk), :]
                s = lax.dot_general(qh, kblk, NT, preferred_element_type=F32) * SCALE
                if decay:
                    s = s + (cc_ref[0, hh] - cr_ref[0, hh, :, pl.ds(ks, tk)])
                if causal:
                    s = jnp.where(col0 + kb * tk <= row, s, NEG)
                m_new = jnp.maximum(m, jnp.max(s, axis=-1, keepdims=True))
                alpha = jnp.exp(m - m_new)
                p = jnp.exp(s - m_new)
                l = alpha * l + jnp.sum(p, axis=-1, keepdims=True)
                acc = alpha * acc + jnp.dot(p.astype(BF16), vblk, preferred_element_type=F32)
                return m_new, l, acc

            init = (jnp.full((tq, 1), NEG, F32), jnp.zeros((tq, 1), F32), jnp.zeros((tq, LANES), F32))
            m, l, acc = lax.fori_loop(0, (i + 1) if causal else nk, step, init)
            outs.append(acc / l)
            lse_ref[0, hh] = m + jnp.log(l)
        o_ref[...] = jnp.where(lane < HEAD_DIM, outs[0], outs[1]).astype(BF16)

    in_specs = [_col_spec(tq, nq, q_cb), _kv_spec(Sk, k_cb), _kv_spec(Sk, v_cb)]
    args = [qa, ka, va]
    if decay:
        in_specs += [_stat_col_spec(tq), _stat_row_spec(S)]
        args += [cc, cr]
    return _pc(body, name=name,
               out_shape=[_sds((B * S, P * LANES), BF16), _sds((B, 2 * P, S, 1), F32)],
               grid=(B, P, nq), in_specs=in_specs, out_specs=[_col_spec(tq, nq, 0), _stat_col_spec(tq)],
               semantics=("arbitrary", "arbitrary", "arbitrary"), vmem=VMEM_BIG)(*args)


def _softmax_bwd(qa, ka, va, doa, oa, lse, *, name, B, S, Sk, P, q_cb, k_cb, v_cb, do_cb, causal,
                 cc=None, cr=None):
    tq = min(ATT_TILE if causal else MEM_Q_TILE, S)
    tk = min(ATT_TILE, Sk)
    nq, nk = S // tq, Sk // tk
    decay = cc is not None

    def body(*refs):
        if decay:
            q_ref, k_ref, v_ref, do_ref, o_ref, lse_ref, cc_ref, cr_ref, dq_ref, dk_ref, dv_ref, dcs_ref = refs
        else:
            q_ref, k_ref, v_ref, do_ref, o_ref, lse_ref, dq_ref, dk_ref, dv_ref = refs
        i = pl.program_id(2)

        @pl.when(i == 0)
        def _():
            dk_ref[...] = jnp.zeros_like(dk_ref)
            dv_ref[...] = jnp.zeros_like(dv_ref)
            if decay:
                dcs_ref[...] = jnp.zeros_like(dcs_ref)

        q = q_ref[...]
        do = do_ref[...]
        prod = do.astype(F32) * o_ref[...].astype(F32)
        lane = lax.broadcasted_iota(jnp.int32, (tq, LANES), 1)
        row = lax.broadcasted_iota(jnp.int32, (tq, tk), 0) + i * tq
        col0 = lax.broadcasted_iota(jnp.int32, (tq, tk), 1)
        dqs = []
        for hh in range(2):
            hmask = _head_mask(lane, hh)
            qh = jnp.where(hmask, q, jnp.zeros_like(q))
            doh = jnp.where(hmask, do, jnp.zeros_like(do))
            lse_h = lse_ref[0, hh]
            n_blocks = (i + 1) if causal else nk

            def probs(kb, hh=hh, qh=qh, doh=doh, lse_h=lse_h):
                ks = pl.multiple_of(kb * tk, tk)
                kblk = k_ref[pl.ds(ks, tk), :]
                vblk = v_ref[pl.ds(ks, tk), :]
                s = lax.dot_general(qh, kblk, NT, preferred_element_type=F32) * SCALE
                if decay:
                    s = s + (cc_ref[0, hh] - cr_ref[0, hh, :, pl.ds(ks, tk)])
                if causal:
                    s = jnp.where(col0 + kb * tk <= row, s, NEG)
                p = jnp.exp(s - lse_h)
                dp = lax.dot_general(doh, vblk, NT, preferred_element_type=F32)
                return ks, kblk, p, dp

            if decay:
                def delta_step(kb, acc):
                    _, _, p, dp = probs(kb)
                    return acc + jnp.sum(p * dp, axis=-1, keepdims=True)

                delta = lax.fori_loop(0, n_blocks, delta_step, jnp.zeros((tq, 1), F32))
            else:
                delta = jnp.sum(jnp.where(hmask, prod, 0.0), axis=-1, keepdims=True)

            def step(kb, dq_acc, hh=hh, qh=qh, doh=doh, delta=delta):
                ks, kblk, p, dp = probs(kb)
                ds = p * (dp - delta)
                dsb = ds.astype(BF16)
                dk_ref[pl.ds(ks, tk), :] += lax.dot_general(dsb, qh, TN, preferred_element_type=F32) * SCALE
                dv_ref[pl.ds(ks, tk), :] += lax.dot_general(p.astype(BF16), doh, TN, preferred_element_type=F32)
                if decay:
                    dcs_ref[0, hh, :, pl.ds(ks, tk)] -= jnp.sum(ds, axis=0, keepdims=True)
                return dq_acc + jnp.dot(dsb, kblk, preferred_element_type=F32)

            dqs.append(lax.fori_loop(0, n_blocks, step, jnp.zeros((tq, LANES), F32)) * SCALE)
        dq_ref[...] = jnp.where(lane < HEAD_DIM, dqs[0], dqs[1]).astype(BF16)

    in_specs = [_col_spec(tq, nq, q_cb), _kv_spec(Sk, k_cb), _kv_spec(Sk, v_cb), _col_spec(tq, nq, do_cb),
                _col_spec(tq, nq, 0), _stat_col_spec(tq)]
    args = [qa, ka, va, doa, oa, lse]
    out_shape = [_sds((B * S, P * LANES), BF16), _sds((B * Sk, P * LANES), F32), _sds((B * Sk, P * LANES), F32)]
    out_specs = [_col_spec(tq, nq, 0), _kv_spec(Sk, 0), _kv_spec(Sk, 0)]
    if decay:
        in_specs += [_stat_col_spec(tq), _stat_row_spec(S)]
        args += [cc, cr]
        out_shape.append(_sds((B, 2 * P, 1, S), F32))
        out_specs.append(_stat_row_spec(S))
    return _pc(body, name=name, out_shape=out_shape, grid=(B, P, nq), in_specs=in_specs, out_specs=out_specs,
               semantics=("arbitrary", "arbitrary", "arbitrary"), vmem=VMEM_BIG)(*args)


HEAD_GROUP = 3


def _g_col_spec(rows, nblk_rows, cb, G):
    return pl.BlockSpec((rows, G * LANES), lambda b, p, i: (b * nblk_rows + i, cb // G + p))


def _g_kv_spec(rows, cb, G):
    return pl.BlockSpec((rows, G * LANES), lambda b, p, i: (b, cb // G + p))


def _g_stat_col_spec(tq, G):
    return pl.BlockSpec((1, 2 * G, tq, 1), lambda b, p, i: (b, p, i, 0))


def _g_stat_row_spec(S, G):
    return pl.BlockSpec((1, 2 * G, 1, S), lambda b, p, i: (b, p, 0, 0))


def _lanes(g):
    return slice(g * LANES, (g + 1) * LANES)


def _streams(x_ref, G, scale=None):
    rows = x_ref.shape[0]
    lane = lax.broadcasted_iota(jnp.int32, (rows, LANES), 1)
    out = []
    for g in range(G):
        x = x_ref[:, _lanes(g)]
        if scale is not None:
            x = x * jnp.asarray(scale, x.dtype)
        for hh in range(2):
            out.append(jnp.where(_head_mask(lane, hh), x, jnp.zeros_like(x)))
    return lane, out


def _wide(stat, width):
    return jnp.tile(stat, (1, width // LANES))


def _fold_lanes(v):
    out = v[:, :LANES]
    for j in range(1, v.shape[1] // LANES):
        out = out + v[:, j * LANES:(j + 1) * LANES]
    return out


def _kv_blocks(ref, ks, tk, G):
    return [ref[pl.ds(ks, tk), _lanes(g)] for g in range(G)]


def _sweep(i, block):
    def step(kb, c):
        block(kb, False)
        return c
    lax.fori_loop(0, i, step, 0)
    block(i, True)


def _fox_fwd_g(qa, ka, va, cr, *, name, B, S, P, q_cb, k_cb, v_cb, G=HEAD_GROUP):
    tq = tk = min(ATT_TILE, S)
    nq = S // tq
    NS = 2 * G

    def body(q_ref, k_ref, v_ref, cr_ref, o_ref, lse_ref, acc_ref, m_ref, l_ref):
        i = pl.program_id(2)
        lane, qh = _streams(q_ref, G, SCALE)
        on_or_below = (lax.broadcasted_iota(jnp.int32, (tq, tk), 1) <= lax.broadcasted_iota(jnp.int32, (tq, tk), 0))
        m_ref[...] = jnp.full(m_ref.shape, NEG, F32)
        l_ref[...] = jnp.zeros(l_ref.shape, F32)
        acc_ref[...] = jnp.zeros(acc_ref.shape, F32)

        def block(kb, diag):
            ks = pl.multiple_of(kb * tk, tk)
            kblk = _kv_blocks(k_ref, ks, tk, G)
            vblk = _kv_blocks(v_ref, ks, tk, G)
            ss = [lax.dot_general(qh[st], kblk[st // 2], NT, preferred_element_type=F32) for st in range(NS)]
            ps = []
            for st in range(NS):
                s = ss[st] - cr_ref[0, st, :, pl.ds(ks, tk)]
                if diag:
                    s = jnp.where(on_or_below, s, NEG)
                m = m_ref[st]
                m_new = jnp.maximum(m, jnp.max(s, axis=-1, keepdims=True))
                alpha = jnp.exp(m - m_new)
                p = jnp.exp(s - _wide(m_new, tk))
                m_ref[st] = m_new
                l_ref[st] = alpha * l_ref[st] + _fold_lanes(p)
                ps.append((alpha, p.astype(BF16)))
            pvs = [jnp.dot(ps[st][1], vblk[st // 2], preferred_element_type=F32) for st in range(NS)]
            for st in range(NS):
                acc_ref[st] = ps[st][0] * acc_ref[st] + pvs[st]

        _sweep(i, block)
        ls = [jnp.sum(l_ref[st], axis=-1, keepdims=True) for st in range(NS)]
        for st in range(NS):
            lse_ref[0, st] = jnp.max(m_ref[st], axis=-1, keepdims=True) + jnp.log(ls[st])
        for g in range(G):
            o_ref[:, _lanes(g)] = jnp.where(lane < HEAD_DIM, acc_ref[2 * g] / ls[2 * g],
                                            acc_ref[2 * g + 1] / ls[2 * g + 1]).astype(BF16)

    return _pc(body, name=name, out_shape=[_sds((B * S, P * LANES), BF16), _sds((B, 2 * P, S, 1), F32)],
               grid=(B, P // G, nq),
               in_specs=[_g_col_spec(tq, nq, q_cb, G), _g_kv_spec(S, k_cb, G), _g_kv_spec(S, v_cb, G),
                         _g_stat_row_spec(S, G)],
               out_specs=[_g_col_spec(tq, nq, 0, G), _g_stat_col_spec(tq, G)],
               scratch_shapes=[pltpu.VMEM((NS, tq, LANES), F32)] * 3,
               semantics=("arbitrary", "arbitrary", "arbitrary"), vmem=VMEM_BIG)(qa, ka, va, cr)


def _fox_bwd_g(qa, ka, va, doa, lse, cr, *, name, B, S, P, q_cb, k_cb, v_cb, do_cb, G=HEAD_GROUP):
    tq = tk = min(ATT_TILE, S)
    nq = S // tq
    NS = 2 * G

    def body(q_ref, k_ref, v_ref, do_ref, lse_ref, cr_ref, dq_ref, dk_ref, dv_ref, dcs_ref, dqa_ref, delta_ref, lse_s,
             p_buf, dp_buf):
        i = pl.program_id(2)

        @pl.when(i == 0)
        def _():
            dk_ref[...] = jnp.zeros_like(dk_ref)
            dv_ref[...] = jnp.zeros_like(dv_ref)
            dcs_ref[...] = jnp.zeros_like(dcs_ref)

        lane, qh = _streams(q_ref, G, SCALE)
        _, doh = _streams(do_ref, G)
        on_or_below = (lax.broadcasted_iota(jnp.int32, (tq, tk), 1) <= lax.broadcasted_iota(jnp.int32, (tq, tk), 0))
        delta_ref[...] = jnp.zeros(delta_ref.shape, F32)
        dqa_ref[...] = jnp.zeros(dqa_ref.shape, F32)
        for st in range(NS):
            lse_s[st] = jnp.broadcast_to(lse_ref[0, st], (tq, LANES))

        def probs(kb, diag):
            ks = pl.multiple_of(kb * tk, tk)
            kblk = _kv_blocks(k_ref, ks, tk, G)
            vblk = _kv_blocks(v_ref, ks, tk, G)
            ss = [lax.dot_general(qh[st], kblk[st // 2], NT, preferred_element_type=F32) for st in range(NS)]
            dps = [lax.dot_general(doh[st], vblk[st // 2], NT, preferred_element_type=F32) for st in range(NS)]
            ps = []
            for st in range(NS):
                s = ss[st] - cr_ref[0, st, :, pl.ds(ks, tk)]
                if diag:
                    s = jnp.where(on_or_below, s, NEG)
                ps.append(jnp.exp(s - _wide(lse_s[st], tk)))
            return ks, kblk, ps, dps

        def delta_block(kb, diag):
            _, _, ps, dps = probs(kb, diag)
            for st in range(NS):
                delta_ref[st] += _fold_lanes(ps[st] * dps[st])
                p_buf[st, kb] = ps[st]
                dp_buf[st, kb] = dps[st]

        _sweep(i, delta_block)
        for st in range(NS):
            delta_ref[st] = jnp.broadcast_to(jnp.sum(delta_ref[st], axis=-1, keepdims=True), (tq, LANES))

        def grad_block(kb, diag):
            ks = pl.multiple_of(kb * tk, tk)
            kblk = _kv_blocks(k_ref, ks, tk, G)
            rows = pl.ds(ks, tk)
            dsb, pb = [], []
            for st in range(NS):
                p = p_buf[st, kb]
                ds = p * (dp_buf[st, kb] - _wide(delta_ref[st], tk))
                dcs_ref[0, st, :, rows] -= jnp.sum(ds, axis=0, keepdims=True)
                dsb.append(ds.astype(BF16))
                pb.append(p.astype(BF16))
            dks = [lax.dot_general(dsb[st], qh[st], TN, preferred_element_type=F32) for st in range(NS)]
            dvs = [lax.dot_general(pb[st], doh[st], TN, preferred_element_type=F32) for st in range(NS)]
            dqs = [jnp.dot(dsb[st], kblk[st // 2], preferred_element_type=F32) for st in range(NS)]
            for g in range(G):
                dk_ref[rows, _lanes(g)] += dks[2 * g] + dks[2 * g + 1]
                dv_ref[rows, _lanes(g)] += dvs[2 * g] + dvs[2 * g + 1]
            for st in range(NS):
                dqa_ref[st] += dqs[st]

        _sweep(i, grad_block)
        for g in range(G):
            dq_ref[:, _lanes(g)] = (jnp.where(lane < HEAD_DIM, dqa_ref[2 * g], dqa_ref[2 * g + 1]) * SCALE).astype(BF16)

    return _pc(body, name=name,
               out_shape=[_sds((B * S, P * LANES), BF16), _sds((B * S, P * LANES), F32), _sds((B * S, P * LANES), F32),
                          _sds((B, 2 * P, 1, S), F32)],
               grid=(B, P // G, nq),
               in_specs=[_g_col_spec(tq, nq, q_cb, G), _g_kv_spec(S, k_cb, G), _g_kv_spec(S, v_cb, G),
                         _g_col_spec(tq, nq, do_cb, G), _g_stat_col_spec(tq, G), _g_stat_row_spec(S, G)],
               out_specs=[_g_col_spec(tq, nq, 0, G), _g_kv_spec(S, 0, G), _g_kv_spec(S, 0, G), _g_stat_row_spec(S, G)],
               scratch_shapes=[pltpu.VMEM((NS, tq, LANES), F32)] * 3 + [pltpu.VMEM((NS, nq, tq, tk), F32)] * 2,
               semantics=("arbitrary", "arbitrary", "arbitrary"), vmem=VMEM_BIG)(qa, ka, va, doa, lse, cr)


def _sb_logs_z(z):
    nz = -z
    lm = jnp.minimum(nz, 0.0) - jnp.log(1.0 + jnp.exp(jnp.minimum(z, nz)))
    return lm + z, lm


def _sb_fwd_g(qa, ka, va, *, name, B, S, P, q_cb, k_cb, v_cb, G=HEAD_GROUP):
    tq = tk = min(ATT_TILE, S)
    nq = S // tq
    NS = 2 * G

    def body(q_ref, k_ref, v_ref, o_ref, rt_ref, acc_ref, run_ref):
        i = pl.program_id(2)
        lane, qh = _streams(q_ref, G, SCALE)
        t_r = lax.broadcasted_iota(jnp.int32, (tk, tk), 0)
        t_c = lax.broadcasted_iota(jnp.int32, (tk, tk), 1)
        after = (t_r > t_c).astype(BF16)
        below = t_c < t_r
        acc_ref[...] = jnp.zeros(acc_ref.shape, F32)
        run_ref[...] = jnp.zeros(run_ref.shape, F32)

        def block(kb, diag):
            ks = pl.multiple_of(kb * tk, tk)
            kblk = _kv_blocks(k_ref, ks, tk, G)
            vblk = _kv_blocks(v_ref, ks, tk, G)
            zs = [lax.dot_general(qh[st], kblk[st // 2], NT, preferred_element_type=F32) for st in range(NS)]
            lss, parts = [], []
            for st in range(NS):
                ls, lm = _sb_logs_z(zs[st])
                if diag:
                    lm = jnp.where(below, lm, 0.0)
                lss.append(ls + _wide(run_ref[st], tk))
                run_ref[st] += jnp.sum(lm, axis=-1, keepdims=True)
                parts.append(_split2(lm))
            sufs = [jnp.dot(parts[st][0], after, preferred_element_type=F32)
                    + jnp.dot(parts[st][1], after, preferred_element_type=F32) for st in range(NS)]
            ab = []
            for st in range(NS):
                a = jnp.exp(lss[st] + sufs[st])
                if diag:
                    a = jnp.where(below, a, 0.0)
                ab.append(a.astype(BF16))
            pvs = [jnp.dot(ab[st], vblk[st // 2], preferred_element_type=F32) for st in range(NS)]
            for st in range(NS):
                acc_ref[st] += pvs[st]

        block(i, True)

        def step(jj, c):
            block(i - 1 - jj, False)
            return c

        lax.fori_loop(0, i, step, 0)
        for st in range(NS):
            rt_ref[0, st] = jnp.max(run_ref[st], axis=-1, keepdims=True)
        for g in range(G):
            o_ref[:, _lanes(g)] = jnp.where(lane < HEAD_DIM, acc_ref[2 * g], acc_ref[2 * g + 1]).astype(BF16)

    return _pc(body, name=name, out_shape=[_sds((B * S, P * LANES), BF16), _sds((B, 2 * P, S, 1), F32)],
               grid=(B, P // G, nq),
               in_specs=[_g_col_spec(tq, nq, q_cb, G), _g_kv_spec(S, k_cb, G), _g_kv_spec(S, v_cb, G)],
               out_specs=[_g_col_spec(tq, nq, 0, G), _g_stat_col_spec(tq, G)],
               scratch_shapes=[pltpu.VMEM((NS, tq, LANES), F32)] * 2,
               semantics=("arbitrary", "arbitrary", "arbitrary"), vmem=VMEM_BIG)(qa, ka, va)


def _sb_bwd_g(qa, ka, va, doa, rt, *, name, B, S, P, q_cb, k_cb, v_cb, do_cb, G=HEAD_GROUP):
    tq = tk = min(ATT_TILE, S)
    nq = S // tq
    NS = 2 * G

    def body(q_ref, k_ref, v_ref, do_ref, rt_ref, dq_ref, dk_ref, dv_ref, dqa_ref, pl_ref, pg_ref):
        i = pl.program_id(2)

        @pl.when(i == 0)
        def _():
            dk_ref[...] = jnp.zeros_like(dk_ref)
            dv_ref[...] = jnp.zeros_like(dv_ref)

        lane, qh = _streams(q_ref, G, SCALE)
        _, doh = _streams(do_ref, G)
        t_r = lax.broadcasted_iota(jnp.int32, (tk, tk), 0)
        t_c = lax.broadcasted_iota(jnp.int32, (tk, tk), 1)
        upto = (t_r <= t_c).astype(BF16)
        before = (t_r < t_c).astype(BF16)
        below = t_c < t_r
        dqa_ref[...] = jnp.zeros(dqa_ref.shape, F32)
        pg_ref[...] = jnp.zeros(pg_ref.shape, F32)
        for st in range(NS):
            pl_ref[st] = jnp.broadcast_to(rt_ref[0, st], (tq, LANES))

        def block(kb, diag):
            ks = pl.multiple_of(kb * tk, tk)
            rows = pl.ds(ks, tk)
            kblk = _kv_blocks(k_ref, ks, tk, G)
            vblk = _kv_blocks(v_ref, ks, tk, G)
            zs = [lax.dot_general(qh[st], kblk[st // 2], NT, preferred_element_type=F32) for st in range(NS)]
            das = [lax.dot_general(doh[st], vblk[st // 2], NT, preferred_element_type=F32) for st in range(NS)]
            lss, parts = [], []
            for st in range(NS):
                ls, lm = _sb_logs_z(zs[st])
                if diag:
                    lm = jnp.where(below, lm, 0.0)
                lss.append((ls, ls + _wide(pl_ref[st], tk)))
                pl_ref[st] -= jnp.sum(lm, axis=-1, keepdims=True)
                parts.append(_split2(lm))
            pins = [jnp.dot(parts[st][0], upto, preferred_element_type=F32)
                    + jnp.dot(parts[st][1], upto, preferred_element_type=F32) for st in range(NS)]
            gms, ab, gparts = [], [], []
            for st in range(NS):
                a = jnp.exp(lss[st][1] - pins[st])
                if diag:
                    a = jnp.where(below, a, 0.0)
                gm = a * das[st]
                gms.append(gm)
                ab.append(a.astype(BF16))
                gparts.append(gm.astype(BF16))
            pgs = [jnp.dot(gparts[st], before, preferred_element_type=F32) for st in range(NS)]
            dzb = []
            for st in range(NS):
                gm = gms[st]
                dz = gm - jnp.exp(lss[st][0]) * (gm + (pgs[st] + _wide(pg_ref[st], tk)))
                if diag:
                    dz = jnp.where(below, dz, 0.0)
                pg_ref[st] += jnp.sum(gm, axis=-1, keepdims=True)
                dzb.append(dz.astype(BF16))
            dks = [lax.dot_general(dzb[st], qh[st], TN, preferred_element_type=F32) for st in range(NS)]
            dvs = [lax.dot_general(ab[st], doh[st], TN, preferred_element_type=F32) for st in range(NS)]
            dqs = [jnp.dot(dzb[st], kblk[st // 2], preferred_element_type=F32) for st in range(NS)]
            for g in range(G):
                dk_ref[rows, _lanes(g)] += dks[2 * g] + dks[2 * g + 1]
                dv_ref[rows, _lanes(g)] += dvs[2 * g] + dvs[2 * g + 1]
            for st in range(NS):
                dqa_ref[st] += dqs[st]

        _sweep(i, block)
        for g in range(G):
            dq_ref[:, _lanes(g)] = (jnp.where(lane < HEAD_DIM, dqa_ref[2 * g], dqa_ref[2 * g + 1]) * SCALE).astype(BF16)

    return _pc(body, name=name,
               out_shape=[_sds((B * S, P * LANES), BF16), _sds((B * S, P * LANES), F32), _sds((B * S, P * LANES), F32)],
               grid=(B, P // G, nq),
               in_specs=[_g_col_spec(tq, nq, q_cb, G), _g_kv_spec(S, k_cb, G), _g_kv_spec(S, v_cb, G),
                         _g_col_spec(tq, nq, do_cb, G), _g_stat_col_spec(tq, G)],
               out_specs=[_g_col_spec(tq, nq, 0, G), _g_kv_spec(S, 0, G), _g_kv_spec(S, 0, G)],
               scratch_shapes=[pltpu.VMEM((NS, tq, LANES), F32)] * 3,
               semantics=("arbitrary", "arbitrary", "arbitrary"), vmem=VMEM_BIG)(qa, ka, va, doa, rt)


def _shift_rows(cur, halo_ref, first, rows_idx, k):
    out = pltpu.roll(cur, k, 0)
    top = out[0:8, :]
    for r in range(k):
        hr = halo_ref.shape[0] - k + r
        edge = jnp.where(first, 0.0, halo_ref[hr:hr + 1, :])
        top = jnp.where(rows_idx[0:8, :] == r, edge, top)
    return jnp.concatenate([top, out[8:, :]], axis=0)


def _shift_rows_up(cur, halo_ref, last, rows_idx, k, ts):
    out = pltpu.roll(cur, ts - k, 0)
    bottom = out[ts - 8:, :]
    for r in range(k):
        edge = jnp.where(last, 0.0, halo_ref[r:r + 1, :])
        bottom = jnp.where(rows_idx[0:8, :] == 8 - k + r, edge, bottom)
    return jnp.concatenate([out[:ts - 8, :], bottom], axis=0)


def _ffn_up_gate(x, g, w, cw, cb, *, name, S):
    T, D = x.shape
    F = w.shape[1] // 2
    tm = min(1024, S)
    tn = 256
    nj = F // tn
    tiles_per_seq = S // tm
    halo = 16

    def body(x_ref, xh_ref, g_ref, wg_ref, wv_ref, cwg_ref, cwv_ref, cbg_ref, cbv_ref,
             uc_ref, ub_ref, a_ref, hout_ref, h_ref, hh_ref, eg_ref, ev_ref):
        first = lax.rem(pl.program_id(0), tiles_per_seq) == 0

        @pl.when(pl.program_id(1) == 0)
        def _():
            def norm(v):
                r = lax.rsqrt(jnp.mean(v * v, axis=-1, keepdims=True) + EPS)
                return ((v * r) * g_ref[...]).astype(BF16)
            h = norm(x_ref[...])
            h_ref[...] = h
            hout_ref[...] = h
            hh_ref[...] = norm(xh_ref[...])

        h = h_ref[...]
        rows_idx = lax.broadcasted_iota(jnp.int32, (tm, tn), 0)
        uc = []
        for half, (w_ref, cw_ref, cb_ref, e_ref) in enumerate(((wg_ref, cwg_ref, cbg_ref, eg_ref),
                                                               (wv_ref, cwv_ref, cbv_ref, ev_ref))):
            acc = jnp.dot(h, w_ref[...], preferred_element_type=F32)
            e_ref[...] = jnp.dot(hh_ref[...], w_ref[...], preferred_element_type=F32)
            ub_ref[half] = acc.astype(BF16)
            m1 = _shift_rows(acc, e_ref, first, rows_idx, 1)
            m2 = _shift_rows(acc, e_ref, first, rows_idx, 2)
            uc.append(cb_ref[...] + cw_ref[0:1, :] * m2 + cw_ref[1:2, :] * m1 + cw_ref[2:3, :] * acc)
            uc_ref[half] = uc[half]
        a_ref[...] = (uc[0] * (1.0 / (1.0 + jnp.exp(-uc[0]))) * uc[1]).astype(BF16)

    in_specs = [pl.BlockSpec((tm, D), lambda i, j: (i, 0)),
                pl.BlockSpec((halo, D), lambda i, j: (jnp.maximum(i * (tm // halo) - 1, 0), 0)),
                pl.BlockSpec((1, D), lambda i, j: (0, 0)),
                pl.BlockSpec((D, tn), lambda i, j: (0, j)), pl.BlockSpec((D, tn), lambda i, j: (0, j + nj)),
                pl.BlockSpec((3, tn), lambda i, j: (0, j)), pl.BlockSpec((3, tn), lambda i, j: (0, j + nj)),
                pl.BlockSpec((1, tn), lambda i, j: (0, j)), pl.BlockSpec((1, tn), lambda i, j: (0, j + nj))]
    return _pc(body, name=name,
               out_shape=[_sds((2, T, F), F32), _sds((2, T, F), BF16), _sds((T, F), BF16), _sds((T, D), BF16)],
               grid=(T // tm, nj), in_specs=in_specs,
               out_specs=[pl.BlockSpec((2, tm, tn), lambda i, j: (0, i, j)), pl.BlockSpec((2, tm, tn), lambda i, j: (0, i, j)),
                          pl.BlockSpec((tm, tn), lambda i, j: (i, j)), pl.BlockSpec((tm, D), lambda i, j: (i, 0))],
               scratch_shapes=[pltpu.VMEM((tm, D), BF16), pltpu.VMEM((halo, D), BF16),
                               pltpu.VMEM((halo, tn), F32), pltpu.VMEM((halo, tn), F32)],
               semantics=("arbitrary", "arbitrary"), vmem=VMEM_BIG)(x, x, g.reshape(1, D), w, w, cw, cw, cb, cb)


def _conv_gate_bwd(da, uc, ub, cw, *, name, B, S):
    F = uc.shape[2]
    tf = F // 2
    ts = min(256, S)
    ns, nf = S // ts, F // tf

    def body(da_ref, uc_ref, ub_ref, wg_ref, wv_ref, dug_ref, duv_ref, pg_ref, pv_ref, nxt_g, nxt_v):
        last = pl.program_id(2) == 0

        @pl.when(jnp.logical_and(pl.program_id(1) == 0, last))
        def _():
            pg_ref[...] = jnp.zeros_like(pg_ref)
            pv_ref[...] = jnp.zeros_like(pv_ref)

        rows_idx = lax.broadcasted_iota(jnp.int32, (ts, tf), 0)
        ucg, ucv = uc_ref[0], uc_ref[1]
        sg = 1.0 / (1.0 + jnp.exp(-ucg))
        dav = da_ref[...]
        d_v = dav * (ucg * sg)
        d_g = dav * ucv * (sg * (1.0 + ucg * (1.0 - sg)))
        for half, (o_ref, p_ref, d, w_ref, nxt) in enumerate(((dug_ref, pg_ref, d_g, wg_ref, nxt_g),
                                                               (duv_ref, pv_ref, d_v, wv_ref, nxt_v))):
            p1 = _shift_rows_up(d, nxt, last, rows_idx, 1, ts)
            p2 = _shift_rows_up(d, nxt, last, rows_idx, 2, ts)
            o_ref[...] = (w_ref[2:3, :] * d + w_ref[1:2, :] * p1 + w_ref[0:1, :] * p2).astype(BF16)
            nxt[...] = d[0:8, :]
            uh = ub_ref[half].astype(F32)
            for k, dk in enumerate((p2, p1, d)):
                p_ref[k:k + 1, :] += jnp.sum(dk * uh, axis=0, keepdims=True)
            p_ref[3:4, :] += jnp.sum(d, axis=0, keepdims=True)

    row = pl.BlockSpec((ts, tf), lambda j, b, r: (b * ns + ns - 1 - r, j))
    both = pl.BlockSpec((2, ts, tf), lambda j, b, r: (0, b * ns + ns - 1 - r, j))
    par = pl.BlockSpec((8, tf), lambda j, b, r: (0, j))
    return _pc(body, name=name,
               out_shape=[_sds((B * S, F), BF16), _sds((B * S, F), BF16), _sds((8, F), F32), _sds((8, F), F32)],
               grid=(nf, B, ns),
               in_specs=[row, both, both, pl.BlockSpec((3, tf), lambda j, b, r: (0, j)),
                         pl.BlockSpec((3, tf), lambda j, b, r: (0, j + nf))],
               out_specs=[row, row, par, par],
               scratch_shapes=[pltpu.VMEM((8, tf), F32), pltpu.VMEM((8, tf), F32)],
               semantics=("arbitrary", "arbitrary", "arbitrary"), vmem=VMEM_BIG)(da, uc, ub, cw, cw)


def _adamw(w, g, m, v, *, name):
    rows, cols = w.shape
    tr = rows
    while tr * cols * 4 > 2 ** 20 and tr % 16 == 0:
        tr //= 2

    def body(w_ref, g_ref, m_ref, v_ref, d_ref, nm_ref, nv_ref):
        gv = g_ref[...]
        m_new = ADAM_B1 * m_ref[...] + (1.0 - ADAM_B1) * gv
        v_new = ADAM_B2 * v_ref[...] + (1.0 - ADAM_B2) * (gv * gv)
        m_hat = m_new / (1.0 - ADAM_B1 ** ADAM_STEP)
        v_hat = v_new / (1.0 - ADAM_B2 ** ADAM_STEP)
        d_ref[...] = -ADAM_LR * (m_hat / (jnp.sqrt(v_hat) + ADAM_EPS) + ADAM_WD * w_ref[...])
        nm_ref[...] = m_new
        nv_ref[...] = v_new

    blk = pl.BlockSpec((tr, cols), lambda i: (i, 0))
    return _pc(body, name=name, out_shape=[_sds((rows, cols), F32)] * 3, grid=(rows // tr,),
               in_specs=[blk] * 4, out_specs=[blk] * 3, semantics=("arbitrary",))(w, g, m, v)


def _my_pos():
    return lax.axis_index("x"), lax.axis_index("y"), lax.axis_index("c")


_HBM = pl.BlockSpec(memory_space=pltpu.HBM)
_SEM = pl.BlockSpec(memory_space=pltpu.SEMAPHORE)
_EFFECT = pltpu.SideEffectType.DATAFLOW_SIDE_EFFECTING


def _peers():
    x, y, c = _my_pos()
    out = []
    for k in range(1, N_DEV):
        px, py, pc = x ^ ((k >> 2) & 1), y ^ ((k >> 1) & 1), c ^ (k & 1)
        out.append(((px, py, pc), 4 * px + 2 * py + pc))
    return out


def _scatter_start(srcs, slot_of, *, name, order_after=None):
    n = len(srcs)
    lands = [lax.empty((N_DEV,) + slot_of(s, 0, shape_only=True), s.dtype) for s in srcs]
    extra = [] if order_after is None else [order_after]

    def body(*refs):
        src_refs, land_refs = refs[:n], refs[n:2 * n]
        send_sems, recv_sems = refs[2 * n + len(extra)], refs[2 * n + len(extra) + 1]
        token = refs[-1]
        x, y, c = _my_pos()
        me = 4 * x + 2 * y + c
        for a in range(n):
            for k, (peer, peer_idx) in enumerate(_peers()):
                pltpu.make_async_remote_copy(
                    src_ref=slot_of(src_refs[a], peer_idx), dst_ref=land_refs[a].at[me],
                    send_sem=send_sems.at[a * 7 + k], recv_sem=recv_sems.at[a * 7 + k],
                    device_id=peer, device_id_type=MESH).start()
        token[...] = jnp.zeros_like(token)

    hbm = lambda a: pltpu.HBM(a.shape, a.dtype)
    args = [pltpu.with_memory_space_constraint(a, pltpu.HBM) for a in list(srcs) + lands] + extra
    outs = pl.pallas_call(
        body, name=name,
        out_shape=(pltpu.SemaphoreType.DMA((7 * n,)), pltpu.SemaphoreType.DMA((7 * n,)),
                   *[hbm(a) for a in srcs], *[hbm(a) for a in lands], _sds((8, LANES), F32)),
        in_specs=[_HBM] * (2 * n) + [pl.BlockSpec(memory_space=pl.ANY)] * len(extra),
        out_specs=(_SEM, _SEM, *([_HBM] * (2 * n)), pl.BlockSpec(memory_space=pltpu.VMEM)),
        input_output_aliases={a: 2 + a for a in range(2 * n)},
        compiler_params=pltpu.CompilerParams(has_side_effects=_EFFECT))(*args)
    return outs[0], outs[1], list(outs[2:2 + n]), list(outs[2 + n:2 + 2 * n]), outs[-1]


def _scatter_wait(send_sems, recv_sems, srcs, lands, slot_of, after, *, name):
    n = len(srcs)

    def body(*refs):
        src_refs, land_refs = refs[:n], refs[n:2 * n]
        ssem, rsem = refs[2 * n], refs[2 * n + 1]
        x, y, c = _my_pos()
        me = 4 * x + 2 * y + c
        for a in range(n):
            for k, (peer, peer_idx) in enumerate(_peers()):
                cp = pltpu.make_async_remote_copy(
                    src_ref=slot_of(src_refs[a], peer_idx), dst_ref=land_refs[a].at[me],
                    send_sem=ssem.at[a * 7 + k], recv_sem=rsem.at[a * 7 + k],
                    device_id=peer, device_id_type=MESH)
                cp.wait_send()
                cp.wait_recv()

    hbm = lambda a: pltpu.HBM(a.shape, a.dtype)
    outs = pl.pallas_call(
        body, name=name, out_shape=tuple(hbm(a) for a in list(srcs) + list(lands)),
        in_specs=[_HBM] * (2 * n) + [_SEM, _SEM, pl.BlockSpec(memory_space=pl.ANY)],
        out_specs=tuple([_HBM] * (2 * n)), input_output_aliases={a: a for a in range(2 * n)},
        compiler_params=pltpu.CompilerParams(has_side_effects=_EFFECT))(*srcs, *lands, send_sems, recv_sems, after)
    return list(outs[:n]), list(outs[n:])


def _whole(a, peer_idx, shape_only=False):
    return a.shape if shape_only else a


def _slot(a, peer_idx, shape_only=False):
    return a.shape[1:] if shape_only else a.at[peer_idx]


def _sum_slots(a, *, name, tr=None):
    rows, cols = a.shape[1], a.shape[2]
    if tr is None:
        tr = rows
        while N_DEV * tr * cols * a.dtype.itemsize > 3 * 2 ** 20 and tr % 32 == 0:
            tr //= 2

    def body(a_ref, o_ref):
        acc = a_ref[0].astype(F32)
        for j in range(1, N_DEV):
            acc = acc + a_ref[j].astype(F32)
        o_ref[...] = acc

    return _pc(body, name=name, out_shape=_sds((rows, cols), F32), grid=(rows // tr,),
               in_specs=[pl.BlockSpec((N_DEV, tr, cols), lambda i: (0, i, 0))],
               out_specs=pl.BlockSpec((tr, cols), lambda i: (i, 0)), semantics=("arbitrary",), vmem=VMEM_BIG)(a)


def _to_slots(full, kind):
    if kind == "rows2":
        r, c = full.shape
        return full.reshape(N_DEV, r // N_DEV, c)
    if kind == "cols2":
        r, c = full.shape
        return full.reshape(r, N_DEV, c // N_DEV).transpose(1, 0, 2)
    if kind == "rows3":
        l, r, c = full.shape
        return full.reshape(l, N_DEV, r // N_DEV, c).transpose(1, 0, 2, 3)
    if kind == "cols3":
        l, r, c = full.shape
        return full.reshape(l, r, N_DEV, c // N_DEV).transpose(2, 0, 1, 3)
    raise ValueError(kind)


def _from_slots(slots, kind):
    if kind == "rows2":
        _, r, c = slots.shape
        return slots.reshape(N_DEV * r, c)
    if kind == "cols2":
        _, r, c = slots.shape
        return slots.transpose(1, 0, 2).reshape(r, N_DEV * c)
    if kind == "rows3":
        _, l, r, c = slots.shape
        return slots.transpose(1, 0, 2, 3).reshape(l, N_DEV * r, c)
    if kind == "cols3":
        _, l, r, c = slots.shape
        return slots.transpose(1, 2, 0, 3).reshape(l, r, N_DEV * c)
    raise ValueError(kind)


BIG = (("w_in_a", "rows2"), ("w_in_b", "rows2"), ("w_kv", "cols2"), ("w_memkv", "rows3"),
       ("w_out", "rows3"), ("w_up", "cols3"), ("w_down", "rows3"))


def _round_up(n, m):
    return -(-n // m) * m


def _pad_rows(a, rows, axis):
    pad = [(0, 0)] * a.ndim
    pad[axis] = (0, rows - a.shape[axis])
    return jnp.pad(a, pad)


def kernel(x, mem, ln_mix_g, w_in_a, b_f_a, w_in_b, ln_kv_g, w_kv, ln_mem_g, w_memkv, w_out, ln_ffn_g, w_up, conv_w, conv_b, w_down, final_g, loss_target, m_ln_mix_g, m_w_in_a, m_b_f_a, m_w_in_b, m_ln_kv_g, m_w_kv, m_ln_mem_g, m_w_memkv, m_w_out, m_ln_ffn_g, m_w_up, m_conv_w, m_conv_b, m_w_down, m_final_g, v_ln_mix_g, v_w_in_a, v_b_f_a, v_w_in_b, v_ln_kv_g, v_w_kv, v_ln_mem_g, v_w_memkv, v_w_out, v_ln_ffn_g, v_w_up, v_conv_w, v_conv_b, v_w_down, v_final_g):
    B, S, D = x.shape
    NM = mem.shape[1]
    T = B * S
    F = w_down.shape[1] * N_DEV
    my_idx = 4 * lax.axis_index("x") + 2 * lax.axis_index("y") + lax.axis_index("c")

    shards = {"w_in_a": w_in_a[0], "w_in_b": w_in_b[0], "w_kv": w_kv, "w_memkv": w_memkv, "w_out": w_out,
              "w_up": w_up, "w_down": w_down}
    moms = {"w_in_a": (m_w_in_a[0], v_w_in_a[0]), "w_in_b": (m_w_in_b[0], v_w_in_b[0]), "w_kv": (m_w_kv, v_w_kv),
            "w_memkv": (m_w_memkv, v_w_memkv), "w_out": (m_w_out, v_w_out), "w_up": (m_w_up, v_w_up),
            "w_down": (m_w_down, v_w_down)}

    groups = [("a1", [("w_in_a", None)]),
              ("a2", [("w_memkv", None), ("w_out", None), ("conv_w", None)]),
              ("b0", [("w_up", 0), ("w_down", 0)]), ("a3", [("w_in_b", None), ("w_kv", None)]),
              ("b1", [("w_up", 1), ("w_down", 1)])]
    sources = dict(shards, conv_w=conv_w)
    started, token = {}, None
    for gname, members in groups:
        srcs = []
        for n, layer in members:
            a = sources[n] if layer is None else sources[n][layer]
            srcs.append(a if n == "conv_w" else a.astype(BF16))
        ssem, rsem, thru, lands, token = _scatter_start(srcs, _whole, name=f"gather_start_{gname}", order_after=token)
        started[gname] = (ssem, rsem, thru, lands)

    def gathered(gname, after):
        ssem, rsem, thru, lands = started[gname]
        thru, lands = _scatter_wait(ssem, rsem, thru, lands, _whole, after, name=f"gather_wait_{gname}")
        return [lax.dynamic_update_index_in_dim(land, s, my_idx, 0) for land, s in zip(lands, thru)]

    full = {}
    (g_wa,) = gathered("a1", token)
    full["w_in_a"] = _from_slots(g_wa, "rows2")

    wa = full["w_in_a"]
    n_qkv = 3 * MAIN_W
    wa = jnp.concatenate([wa[:, :n_qkv], wa[:, n_qkv + N_MAIN_HEADS:], wa[:, n_qkv:n_qkv + N_MAIN_HEADS],
                          jnp.zeros((D, LANES - N_MAIN_HEADS), BF16)], axis=1)
    n_main = n_qkv + MEM_W
    full["w_up"], full["w_down"] = {}, {}
    b_f =_pad_rows(b_f_a.reshape(1, N_MAIN_HEADS), LANES, 1)

    x2d = x.reshape(T, D)
    mem2d = mem.reshape(B * NM, D)
    tgt2d = loss_target.reshape(T, D)
    PM, PX = N_MAIN_HEADS // 2, N_MEM_HEADS // 2

    def stats_to_heads(c2d):
        c = c2d.reshape(B, S, LANES)[:, :, :N_MAIN_HEADS].transpose(0, 2, 1)
        return c[:, :, None, :]

    def mem_kv(layer):
        return _mm_fwd(mem2d, full["w_memkv"][layer], name=f"memkv{layer}", tm=B * NM, tn=2 * MEM_W,
                       out_dtype=BF16, g=ln_mem_g[layer], save_h=True)

    def conv_ffn_fwd(xin, layer):
        uc, ub, a, h = _ffn_up_gate(xin, ln_ffn_g[layer], full["w_up"][layer], conv_w_full[layer],
                                    conv_b[layer].reshape(1, 2 * F), name=f"ffn_up{layer}", S=S)
        xo = _mm_fwd(a, full["w_down"][layer], name=f"ffn_down{layer}", tm=min(1024, T), tn=1024, out_dtype=F32, res=xin)
        return xo, (uc, ub, h, a)

    proj_a, h_mix0 = _mm_fwd(x2d, wa, name="in_proj_a", tm=min(1024, T), tn=1280, out_dtype=BF16, g=ln_mix_g[0],
                             ncols=n_main, save_h=True)
    f_logit = _mm_fwd(x2d, wa, name="in_proj_f", tm=min(1024, T), tn=LANES, out_dtype=F32, g=ln_mix_g[0],
                      col0=n_main // LANES, ncols=LANES)
    c2d = _forget_cumsum(f_logit, b_f, B=B, S=S, name="forget_cumsum")
    cr = stats_to_heads(c2d)
    o_main0, lse0 = _fox_fwd_g(proj_a, proj_a, proj_a, cr, name="fox_fwd", B=B, S=S, P=PM, q_cb=0, k_cb=PM, v_cb=2 * PM)
    g_wmem, g_wout, g_cw = gathered("a2", lse0)
    full["w_memkv"] = _from_slots(g_wmem, "rows3")
    full["w_out"] = _from_slots(g_wout, "rows3")
    conv_w_full = _from_slots(g_cw, "cols3")
    memkv0, h_mem0 = mem_kv(0)
    o_mem0, lse_m0 = _softmax_fwd(proj_a, memkv0, memkv0, name="mem_fwd0", B=B, S=S, Sk=NM, P=PX, q_cb=3 * PM,
                                  k_cb=0, v_cb=PX, causal=False)
    o_cat0 = jnp.concatenate([o_main0, o_mem0], axis=1)
    x1 = _mm_fwd(o_cat0, full["w_out"][0], name="out_proj0", tm=min(1024, T), tn=1024, out_dtype=F32, res=x2d)
    g_up, g_dn = gathered("b0", x1)
    full["w_up"][0], full["w_down"][0] = _from_slots(g_up, "cols2"), _from_slots(g_dn, "rows2")
    x2, ffn_saved0 = conv_ffn_fwd(x1, 0)
    g_wb, g_wkv = gathered("a3", x2)
    wb, wkv = _from_slots(g_wb, "rows2"), _from_slots(g_wkv, "cols2")
    kv, h_kv =_mm_fwd(x2, wkv, name="kv_proj", tm=min(1024, T), tn=1536, out_dtype=BF16, g=ln_kv_g, save_h=True)
    proj_b, h_mix1 = _mm_fwd(x2, wb, name="in_proj_b", tm=min(1024, T), tn=1024, out_dtype=BF16, g=ln_mix_g[1],
                             save_h=True)
    o_main1, rt1 = _sb_fwd_g(proj_b, kv, kv, name="sb_fwd", B=B, S=S, P=PM, q_cb=0, k_cb=0, v_cb=PM)
    memkv1, h_mem1 = mem_kv(1)
    o_mem1, lse_m1 = _softmax_fwd(proj_b, memkv1, memkv1, name="mem_fwd1", B=B, S=S, Sk=NM, P=PX, q_cb=PM,
                                  k_cb=0, v_cb=PX, causal=False)
    o_cat1 = jnp.concatenate([o_main1, o_mem1], axis=1)
    x3 = _mm_fwd(o_cat1, full["w_out"][1], name="out_proj1", tm=min(1024, T), tn=1024, out_dtype=F32, res=x2)
    g_up, g_dn = gathered("b1", x3)
    full["w_up"][1], full["w_down"][1] = _from_slots(g_up, "cols2"), _from_slots(g_dn, "rows2")
    x4, ffn_saved1 = conv_ffn_fwd(x3, 1)
    dx4, dg_final, loss_part = _loss_head(x4, final_g, tgt2d, name="loss_head")

    grads = {}
    small = {}
    reduce_groups = []

    def start_reduce(gname, keys, kinds):
        slots = [_to_slots(grads[k], kind) for k, kind in zip(keys, kinds)]
        ssem, rsem, thru, lands, tok = _scatter_start(slots, _slot, name=f"reduce_start_{gname}")
        reduce_groups.append((gname, keys, ssem, rsem, thru, lands))
        return tok[0, 0]

    def conv_ffn_bwd(dxo, xin, saved, layer):
        uc, ub, h, a = saved
        w_dn = full["w_down"][layer]
        da = _mm_nt(dxo, w_dn, name=f"d_act{layer}", tm=min(1024, T), tn=F // 2, out_dtype=F32)
        grads[("w_down", layer)] = _wgrad(a, dxo, f"g_w_down{layer}")
        cwl = conv_w_full[layer]
        du_g, du_v, p_g, p_v = _conv_gate_bwd(da, uc, ub, cwl, name=f"conv_bwd{layer}", B=B, S=S)
        small[("conv_w", layer)] = jnp.concatenate([p_g[0:3], p_v[0:3]], axis=1)
        small[("conv_b", layer)] = jnp.concatenate([p_g[3], p_v[3]], axis=0)
        grads[("w_up", layer)] = jnp.concatenate(
            [_wgrad(h, du_g, f"g_w_up_gate{layer}"), _wgrad(h, du_v, f"g_w_up_val{layer}")], axis=1)
        tok = start_reduce(f"ffn{layer}", [("w_down", layer), ("w_up", layer)], ["rows2", "cols2"])
        dxi, dg = _mm_nt_rmsbwd([(du_g, 0), (du_v, 1)], full["w_up"][layer], xin, ln_ffn_g[layer] + tok,
                                name=f"d_ffn_in{layer}", dres=dxo)
        small[("ln_ffn_g", layer)] = dg[0]
        return dxi

    def mem_bwd(proj, q_cb, memkv, h_mem, do_cat, o_mem, lse_m, layer):
        dqm, dmk, dmv = _softmax_bwd(proj, memkv, memkv, do_cat, o_mem, lse_m, name=f"mem_bwd{layer}", B=B, S=S,
                                     Sk=NM, P=PX, q_cb=q_cb, k_cb=0, v_cb=PX, do_cb=PM, causal=False)
        grads[("w_memkv", layer)] = jnp.concatenate(
            [_wgrad(h_mem, dmk, f"g_w_memk{layer}"), _wgrad(h_mem, dmv, f"g_w_memv{layer}")], axis=1)
        _, dg = _mm_nt_rmsbwd([(dmk, 0), (dmv, 1)], full["w_memkv"][layer], mem2d, ln_mem_g[layer],
                              name=f"d_mem_in{layer}", want_dx=False)
        small[("ln_mem_g", layer)] = dg[0]
        return dqm

    dx3 = conv_ffn_bwd(dx4, x3, ffn_saved1, 1)
    do_cat1 = _mm_nt(dx3, full["w_out"][1], name="d_o_cat1", tm=min(1024, T), tn=1024, out_dtype=BF16)
    grads[("w_out", 1)] = _wgrad(o_cat1, dx3, "g_w_out1")
    dq1, dk1, dv1 = _sb_bwd_g(proj_b, kv, kv, do_cat1, rt1, name="sb_bwd", B=B, S=S, P=PM, q_cb=0, k_cb=0, v_cb=PM,
                            do_cb=0)
    dqm1 = mem_bwd(proj_b, PM, memkv1, h_mem1, do_cat1, o_mem1, lse_m1, 1)
    grads["w_in_b"] = jnp.concatenate([_wgrad(h_mix1, dq1, "g_w_in_b_q"), _wgrad(h_mix1, dqm1, "g_w_in_b_m")], axis=1)
    grads["w_kv"] = jnp.concatenate([_wgrad(h_kv, dk1, "g_w_kv_k"), _wgrad(h_kv, dv1, "g_w_kv_v")], axis=1)
    tok = start_reduce("mix1", [("w_out", 1), "w_in_b", "w_kv", ("w_memkv", 1)], ["rows2", "rows2", "cols2", "rows2"])
    dx2, dg = _mm_nt_rmsbwd([(dq1, 0), (dqm1, MAIN_W // MEM_W)], wb, x2, ln_mix_g[1] + tok, name="d_mix_in1", dres=dx3)
    small[("ln_mix_g", 1)] = dg[0]
    dx2, dg = _mm_nt_rmsbwd([(dk1, 0), (dv1, 1)], wkv, x2, ln_kv_g, name="d_kv_in", dres=dx2)
    small["ln_kv_g"] = dg[0]
    dx1 = conv_ffn_bwd(dx2, x1, ffn_saved0, 0)
    do_cat0 = _mm_nt(dx1, full["w_out"][0], name="d_o_cat0", tm=min(1024, T), tn=1024, out_dtype=BF16)
    grads[("w_out", 0)] = _wgrad(o_cat0, dx1, "g_w_out0")
    dq0, dk0, dv0, dcs = _fox_bwd_g(proj_a, proj_a, proj_a, do_cat0, lse0, cr, name="fox_bwd", B=B, S=S, P=PM, q_cb=0,
                                  k_cb=PM, v_cb=2 * PM, do_cb=0)
    dqm0 = mem_bwd(proj_a, 3 * PM, memkv0, h_mem0, do_cat0, o_mem0, lse_m0, 0)
    dc2d = _pad_rows(dcs[:, :, 0, :].transpose(0, 2, 1).reshape(T, N_MAIN_HEADS), LANES, 1)
    df, db_f = _forget_cumsum_bwd(dc2d, f_logit, b_f, B=B, S=S, name="forget_cumsum_bwd")
    a_parts = [(dq0, 0), (dk0, 1), (dv0, 2), (dqm0, n_qkv // MEM_W), (df, n_main // LANES)]
    g_wa = jnp.concatenate([_wgrad(h_mix0, p, f"g_w_in_a{k}") for k, (p, _) in enumerate(a_parts)], axis=1)
    grads["w_in_a"] = jnp.concatenate([g_wa[:, :n_qkv], g_wa[:, n_main:n_main + N_MAIN_HEADS], g_wa[:, n_qkv:n_main]],
                                      axis=1)
    tok = start_reduce("mix0", [("w_out", 0), ("w_memkv", 0), "w_in_a"], ["rows2", "rows2", "rows2"])
    dx0, dg = _mm_nt_rmsbwd(a_parts, wa, x2d, ln_mix_g[0] + tok, name="d_mix_in0", dres=dx1)
    small[("ln_mix_g", 0)] = dg[0]
    grad_x = dx0.reshape(B, S, D)

    def both_small(name):
        return jnp.stack([small[(name, 0)], small[(name, 1)]])

    small_list = [("ln_mix_g", both_small("ln_mix_g")), ("b_f_a", db_f[:, :N_MAIN_HEADS]), ("ln_kv_g", small["ln_kv_g"]),
                  ("ln_mem_g", both_small("ln_mem_g")), ("ln_ffn_g", both_small("ln_ffn_g")),
                  ("conv_w", both_small("conv_w")), ("conv_b", both_small("conv_b")), ("final_g", dg_final[0]),
                  ("loss", loss_part[0, :1])]
    sm_rows = []
    for _, a in small_list:
        flat = a.reshape(-1)
        sm_rows.append(_pad_rows(flat, _round_up(flat.size, 8 * LANES), 0).reshape(-1, LANES))
    spack = jnp.concatenate(sm_rows, axis=0)
    s_ssem, s_rsem, s_thru, s_lands, s_tok = _scatter_start([spack], _whole, name="small_start")

    pieces = {}
    for gname, keys, ssem, rsem, thru, lands in reduce_groups:
        thru, lands = _scatter_wait(ssem, rsem, thru, lands, _slot, s_tok, name=f"reduce_wait_{gname}")
        for key, mine, land in zip(keys, thru, lands):
            own = lax.dynamic_index_in_dim(mine, my_idx, 0, keepdims=False)
            land = lax.dynamic_update_index_in_dim(land, own, my_idx, 0)
            tag = key if isinstance(key, str) else f"{key[0]}{key[1]}"
            pieces[key] = _sum_slots(land, name=f"sum_{tag}")

    red = {}
    for n in ("w_in_a", "w_in_b", "w_kv"):
        red[n] = pieces[n].reshape(shards[n].shape)
    for n in ("w_memkv", "w_out", "w_up", "w_down"):
        red[n] = jnp.stack([pieces[(n, 0)], pieces[(n, 1)]])

    weights = {"ln_mix_g": ln_mix_g, "w_in_a": w_in_a, "b_f_a": b_f_a, "w_in_b": w_in_b, "ln_kv_g": ln_kv_g,
               "w_kv": w_kv, "ln_mem_g": ln_mem_g, "w_memkv": w_memkv, "w_out": w_out, "ln_ffn_g": ln_ffn_g,
               "w_up": w_up, "conv_w": conv_w, "conv_b": conv_b, "w_down": w_down, "final_g": final_g}
    m_in = {"ln_mix_g": m_ln_mix_g, "w_in_a": m_w_in_a, "b_f_a": m_b_f_a, "w_in_b": m_w_in_b, "ln_kv_g": m_ln_kv_g,
            "w_kv": m_w_kv, "ln_mem_g": m_ln_mem_g, "w_memkv": m_w_memkv, "w_out": m_w_out, "ln_ffn_g": m_ln_ffn_g,
            "w_up": m_w_up, "conv_w": m_conv_w, "conv_b": m_conv_b, "w_down": m_w_down, "final_g": m_final_g}
    v_in = {"ln_mix_g": v_ln_mix_g, "w_in_a": v_w_in_a, "b_f_a": v_b_f_a, "w_in_b": v_w_in_b, "ln_kv_g": v_ln_kv_g,
            "w_kv": v_w_kv, "ln_mem_g": v_ln_mem_g, "w_memkv": v_w_memkv, "w_out": v_w_out, "ln_ffn_g": v_ln_ffn_g,
            "w_up": v_w_up, "conv_w": v_conv_w, "conv_b": v_conv_b, "w_down": v_w_down, "final_g": v_final_g}
    order = list(weights)
    big_names = [n for n, _ in BIG]
    g_out, d_out, nm_out, nv_out = {}, {}, {}, {}

    def update(n):
        w = weights[n]
        cols = w.shape[-1]
        g = red[n].reshape(w.shape)
        d, nm, nv = _adamw(w.reshape(-1, cols), g.reshape(-1, cols), m_in[n].reshape(-1, cols),
                           v_in[n].reshape(-1, cols), name=f"adamw_{n}")
        g_out[n], d_out[n], nm_out[n], nv_out[n] = g, d.reshape(w.shape), nm.reshape(w.shape), nv.reshape(w.shape)

    for n in big_names:
        update(n)
    all_updated = jnp.stack([d_out[n].reshape(-1)[0] for n in big_names])
    s_thru, s_lands = _scatter_wait(s_ssem, s_rsem, s_thru, s_lands, _whole, all_updated, name="small_wait")
    ssum = _sum_slots(lax.dynamic_update_index_in_dim(s_lands[0], s_thru[0], my_idx, 0), name="sum_small")
    off = 0
    for (n, a), rows in zip(small_list, sm_rows):
        red[n] = ssum[off:off + rows.shape[0]].reshape(-1)[:a.size].reshape(a.shape)
        off += rows.shape[0]
    loss = red["loss"][0]
    shard_cols = conv_w.shape[2]
    red["conv_w"] = lax.dynamic_slice_in_dim(red["conv_w"], my_idx * shard_cols, shard_cols, axis=2)
    red["b_f_a"] = red["b_f_a"].reshape(b_f_a.shape)
    update("conv_w")
    small_names = [n for n in order if n not in g_out]

    def pack_small(src):
        rows = []
        for n in small_names:
            flat = src[n].reshape(-1)
            rows.append(_pad_rows(flat, _round_up(flat.size, 8 * LANES), 0).reshape(-1, LANES))
        return jnp.concatenate(rows, axis=0), [r.shape[0] for r in rows]

    red_small = {n: red[n].reshape(weights[n].shape) for n in small_names}
    wp, counts = pack_small(weights)
    gp, _ = pack_small(red_small)
    mp, _ = pack_small(m_in)
    vp, _ = pack_small(v_in)
    dp, nmp, nvp = _adamw(wp, gp, mp, vp, name="adamw_small")
    off = 0
    for n, cnt in zip(small_names, counts):
        shp = weights[n].shape
        size = weights[n].size
        g_out[n] = red_small[n]
        d_out[n] = dp[off:off + cnt].reshape(-1)[:size].reshape(shp)
        nm_out[n] = nmp[off:off + cnt].reshape(-1)[:size].reshape(shp)
        nv_out[n] = nvp[off:off + cnt].reshape(-1)[:size].reshape(shp)
        off += cnt

    return (loss, grad_x, *[g_out[n] for n in order], *[d_out[n] for n in order],
            *[nm_out[n] for n in order], *[nv_out[n] for n in order])
```

```python
import functools

import jax
import jax.numpy as jnp
from jax import lax
from jax.experimental import pallas as pl
from jax.experimental.pallas import tpu as pltpu

F32 = jnp.float32
BF16 = jnp.bfloat16
LANES = 128
HEAD_DIM = 64
N_MAIN_HEADS = 12
N_MEM_HEADS = 4
MAIN_W = N_MAIN_HEADS * HEAD_DIM
MEM_W = N_MEM_HEADS * HEAD_DIM
SCALE = HEAD_DIM ** -0.5
EPS = 1e-6
NEG = -1e30
N_DEV = 8
ATT_TILE = 256
MEM_Q_TILE = 1024
VMEM_BIG = 56 * 2 ** 20
MESH = pl.DeviceIdType.MESH

ADAM_LR = 0.001
ADAM_B1 = 0.9
ADAM_B2 = 0.999
ADAM_EPS = 1e-08
ADAM_WD = 0.01
ADAM_STEP = 10

NT = (((1,), (1,)), ((), ()))
TN = (((0,), (0,)), ((), ()))


def _pc(body, *, name, out_shape, grid=None, in_specs=None, out_specs=None, scratch_shapes=(),
        semantics=None, vmem=None):
    kw = {}
    if grid is not None:
        kw["grid"] = grid
    params = pltpu.CompilerParams(dimension_semantics=semantics, vmem_limit_bytes=vmem)
    return pl.pallas_call(body, name=name, out_shape=out_shape, in_specs=in_specs, out_specs=out_specs,
                          scratch_shapes=list(scratch_shapes), compiler_params=params, **kw)


def _sds(shape, dtype):
    return jax.ShapeDtypeStruct(shape, dtype)


def _mm_fwd(a, w, *, name, tm, tn, out_dtype, g=None, res=None, col0=0, ncols=None, save_h=False):
    m_rows, k = a.shape
    n = w.shape[1] if ncols is None else ncols
    grid = (m_rows // tm, n // tn)
    norm = g is not None

    def body(*refs):
        refs = list(refs)
        a_ref = refs.pop(0)
        g_ref = refs.pop(0) if norm else None
        w_ref = refs.pop(0)
        res_ref = refs.pop(0) if res is not None else None
        o_ref = refs.pop(0)
        hout_ref = refs.pop(0) if save_h else None
        h_ref = refs.pop(0) if norm else None
        if norm:
            @pl.when(pl.program_id(1) == 0)
            def _():
                xv = a_ref[...]
                r = lax.rsqrt(jnp.mean(xv * xv, axis=-1, keepdims=True) + EPS)
                h = ((xv * r) * g_ref[...]).astype(BF16)
                h_ref[...] = h
                if save_h:
                    hout_ref[...] = h
            lhs = h_ref[...]
        else:
            lhs = a_ref[...].astype(BF16)
        acc = jnp.dot(lhs, w_ref[...], preferred_element_type=F32)
        if res is not None:
            acc = acc + res_ref[...]
        o_ref[...] = acc.astype(out_dtype)

    in_specs = [pl.BlockSpec((tm, k), lambda i, j: (i, 0))]
    args = [a]
    if norm:
        in_specs.append(pl.BlockSpec((1, k), lambda i, j: (0, 0)))
        args.append(g.reshape(1, k))
    in_specs.append(pl.BlockSpec((k, tn), lambda i, j: (0, j + col0)))
    args.append(w)
    if res is not None:
        in_specs.append(pl.BlockSpec((tm, tn), lambda i, j: (i, j)))
        args.append(res)
    out_shape = [_sds((m_rows, n), out_dtype)]
    out_specs = [pl.BlockSpec((tm, tn), lambda i, j: (i, j))]
    if save_h:
        out_shape.append(_sds((m_rows, k), BF16))
        out_specs.append(pl.BlockSpec((tm, k), lambda i, j: (i, 0)))
    scratch = [pltpu.VMEM((tm, k), BF16)] if norm else []
    outs = _pc(body, name=name, out_shape=out_shape, grid=grid, in_specs=in_specs, out_specs=out_specs,
               scratch_shapes=scratch, semantics=("arbitrary", "arbitrary"), vmem=VMEM_BIG)(*args)
    return outs if save_h else outs[0]


def _mm_nt(a, w, *, name, tm, tn, out_dtype):
    m_rows, k = a.shape
    n = w.shape[0]

    def body(a_ref, w_ref, o_ref):
        acc = lax.dot_general(a_ref[...].astype(BF16), w_ref[...], NT, preferred_element_type=F32)
        o_ref[...] = acc.astype(out_dtype)

    return _pc(body, name=name, out_shape=_sds((m_rows, n), out_dtype), grid=(m_rows // tm, n // tn),
               in_specs=[pl.BlockSpec((tm, k), lambda i, j: (i, 0)), pl.BlockSpec((tn, k), lambda i, j: (j, 0))],
               out_specs=pl.BlockSpec((tm, tn), lambda i, j: (i, j)),
               semantics=("arbitrary", "arbitrary"), vmem=VMEM_BIG)(a, w)


def _mm_tn(a, b, *, name, ta, tn, tt):
    t_rows, ka = a.shape
    n = b.shape[1]
    nt = t_rows // tt

    def body(a_ref, b_ref, o_ref, acc_ref):
        t = pl.program_id(2)

        @pl.when(t == 0)
        def _():
            acc_ref[...] = jnp.zeros_like(acc_ref)

        acc_ref[...] += lax.dot_general(a_ref[...].astype(BF16), b_ref[...].astype(BF16), TN,
                                        preferred_element_type=F32)

        @pl.when(t == nt - 1)
        def _():
            o_ref[...] = acc_ref[...].astype(BF16)

    return _pc(body, name=name, out_shape=_sds((ka, n), BF16), grid=(ka // ta, n // tn, nt),
               in_specs=[pl.BlockSpec((tt, ta), lambda i, j, t: (t, i)),
                         pl.BlockSpec((tt, tn), lambda i, j, t: (t, j))],
               out_specs=pl.BlockSpec((ta, tn), lambda i, j, t: (i, j)),
               scratch_shapes=[pltpu.VMEM((ta, tn), F32)],
               semantics=("arbitrary", "arbitrary", "arbitrary"), vmem=VMEM_BIG)(a, b)


def _wgrad(a, b, name):
    t_rows, ka = a.shape
    n = b.shape[1]
    ta = ka if ka <= 1024 else ka // 2
    tn = n
    while ta * tn * 4 > 6 * 2 ** 20 and tn % 256 == 0:
        tn //= 2
    tt = min(1024, t_rows)
    return _mm_tn(a, b, name=name, ta=ta, tn=tn, tt=tt)


def _mm_nt_rmsbwd(parts, w, x, g, *, name, dres=None, want_dx=True):
    m_rows, d = x.shape
    k_total = sum(dy.shape[1] for dy, _ in parts)
    tm = min(512 if k_total <= 2816 else 256, m_rows)
    n_parts = len(parts)

    def body(*refs):
        refs = list(refs)
        dy_refs = [refs.pop(0) for _ in range(n_parts)]
        w_refs = [refs.pop(0) for _ in range(n_parts)]
        x_ref = refs.pop(0)
        g_ref = refs.pop(0)
        dres_ref = refs.pop(0) if dres is not None else None
        dx_ref = refs.pop(0) if want_dx else None
        dg_ref = refs.pop(0)

        @pl.when(pl.program_id(0) == 0)
        def _():
            dg_ref[...] = jnp.zeros_like(dg_ref)

        dh = None
        for dy_ref, w_ref in zip(dy_refs, w_refs):
            t = lax.dot_general(dy_ref[...].astype(BF16), w_ref[...], NT, preferred_element_type=F32)
            dh = t if dh is None else dh + t
        xv = x_ref[...]
        r = lax.rsqrt(jnp.mean(xv * xv, axis=-1, keepdims=True) + EPS)
        xh = xv * r
        dg_ref[...] += jnp.sum(dh * xh, axis=0, keepdims=True)
        if want_dx:
            dhg = dh * g_ref[...]
            dx = r * (dhg - xh * jnp.mean(dhg * xh, axis=-1, keepdims=True))
            if dres is not None:
                dx = dx + dres_ref[...]
            dx_ref[...] = dx

    in_specs, args = [], []
    for dy, _ in parts:
        in_specs.append(pl.BlockSpec((tm, dy.shape[1]), lambda i: (i, 0)))
        args.append(dy)
    for dy, cb in parts:
        in_specs.append(pl.BlockSpec((d, dy.shape[1]), functools.partial(lambda i, cb: (0, cb), cb=cb)))
        args.append(w)
    in_specs += [pl.BlockSpec((tm, d), lambda i: (i, 0)), pl.BlockSpec((1, d), lambda i: (0, 0))]
    args += [x, g.reshape(1, d)]
    if dres is not None:
        in_specs.append(pl.BlockSpec((tm, d), lambda i: (i, 0)))
        args.append(dres)
    out_shape, out_specs = [], []
    if want_dx:
        out_shape.append(_sds((m_rows, d), F32))
        out_specs.append(pl.BlockSpec((tm, d), lambda i: (i, 0)))
    out_shape.append(_sds((1, d), F32))
    out_specs.append(pl.BlockSpec((1, d), lambda i: (0, 0)))
    outs = _pc(body, name=name, out_shape=out_shape, grid=(m_rows // tm,), in_specs=in_specs,
               out_specs=out_specs, semantics=("arbitrary",), vmem=VMEM_BIG)(*args)
    return (outs[0], outs[1]) if want_dx else (None, outs[0])


def _loss_head(x, g, tgt, *, name):
    m_rows, d = x.shape
    tm = min(512, m_rows)

    def body(x_ref, g_ref, t_ref, dx_ref, dg_ref, loss_ref):
        @pl.when(pl.program_id(0) == 0)
        def _():
            dg_ref[...] = jnp.zeros_like(dg_ref)
            loss_ref[...] = jnp.zeros_like(loss_ref)

        xv = x_ref[...]
        r = lax.rsqrt(jnp.mean(xv * xv, axis=-1, keepdims=True) + EPS)
        xh = xv * r
        gv = g_ref[...]
        err = xh * gv - t_ref[...]
        per_tok = jnp.mean(err * err, axis=-1, keepdims=True)
        loss_ref[...] += 0.5 * jnp.sum(per_tok, axis=0, keepdims=True)
        dout = err * (1.0 / d)
        dg_ref[...] += jnp.sum(dout * xh, axis=0, keepdims=True)
        dhg = dout * gv
        dx_ref[...] = r * (dhg - xh * jnp.mean(dhg * xh, axis=-1, keepdims=True))

    row = pl.BlockSpec((tm, d), lambda i: (i, 0))
    return _pc(body, name=name, out_shape=[_sds((m_rows, d), F32), _sds((1, d), F32), _sds((1, LANES), F32)],
               grid=(m_rows // tm,), in_specs=[row, pl.BlockSpec((1, d), lambda i: (0, 0)), row],
               out_specs=[row, pl.BlockSpec((1, d), lambda i: (0, 0)), pl.BlockSpec((1, LANES), lambda i: (0, 0))],
               semantics=("arbitrary",))(x, g.reshape(1, d), tgt)


def _split3(v):
    hi = v.astype(BF16)
    r1 = v - hi.astype(F32)
    mid = r1.astype(BF16)
    lo = (r1 - mid.astype(F32)).astype(BF16)
    return hi, mid, lo


def _split2(v):
    hi = v.astype(BF16)
    lo = (v - hi.astype(F32)).astype(BF16)
    return hi, lo


def _tri_dot3(tri, v):
    hi, mid, lo = _split3(v)
    return (jnp.dot(tri, hi, preferred_element_type=F32) + jnp.dot(tri, mid, preferred_element_type=F32)
            + jnp.dot(tri, lo, preferred_element_type=F32))


def _log_sigmoid(v):
    return jnp.minimum(v, 0.0) - jnp.log(1.0 + jnp.exp(-jnp.abs(v)))


def _forget_cumsum(f_logit, b_f, *, B, S, name):
    ch = min(256, S)
    nch = S // ch

    def body(f_ref, b_ref, c_ref):
        r_i = lax.broadcasted_iota(jnp.int32, (ch, ch), 0)
        c_i = lax.broadcasted_iota(jnp.int32, (ch, ch), 1)
        tri = (c_i <= r_i).astype(BF16)
        bv = b_ref[...]

        def step(k, carry):
            rows = pl.ds(pl.multiple_of(k * ch, ch), ch)
            lf = _log_sigmoid(f_ref[rows, :] + bv)
            c_ref[rows, :] = _tri_dot3(tri, lf) + carry
            return carry + jnp.sum(lf, axis=0, keepdims=True)

        lax.fori_loop(0, nch, step, jnp.zeros((1, LANES), F32))

    blk = pl.BlockSpec((S, LANES), lambda b: (b, 0))
    return _pc(body, name=name, out_shape=_sds((B * S, LANES), F32), grid=(B,),
               in_specs=[blk, pl.BlockSpec((1, LANES), lambda b: (0, 0))], out_specs=blk,
               semantics=("arbitrary",))(f_logit, b_f)


def _forget_cumsum_bwd(dc, f_logit, b_f, *, B, S, name):
    ch = min(256, S)
    nch = S // ch

    def body(dc_ref, f_ref, b_ref, df_ref, db_ref):
        @pl.when(pl.program_id(0) == 0)
        def _():
            db_ref[...] = jnp.zeros_like(db_ref)

        r_i = lax.broadcasted_iota(jnp.int32, (ch, ch), 0)
        c_i = lax.broadcasted_iota(jnp.int32, (ch, ch), 1)
        tri = (c_i >= r_i).astype(BF16)
        bv = b_ref[...]

        def step(kk, carry):
            tail, dbs = carry
            k = nch - 1 - kk
            rows = pl.ds(pl.multiple_of(k * ch, ch), ch)
            dcv = dc_ref[rows, :]
            dlf = _tri_dot3(tri, dcv) + tail
            z = f_ref[rows, :] + bv
            df = dlf * (1.0 / (1.0 + jnp.exp(z)))
            df_ref[rows, :] = df.astype(BF16)
            return tail + jnp.sum(dcv, axis=0, keepdims=True), dbs + jnp.sum(df, axis=0, keepdims=True)

        zero = jnp.zeros((1, LANES), F32)
        _, dbs = lax.fori_loop(0, nch, step, (zero, zero))
        db_ref[...] += dbs

    blk = pl.BlockSpec((S, LANES), lambda b: (b, 0))
    one = pl.BlockSpec((1, LANES), lambda b: (0, 0))
    return _pc(body, name=name, out_shape=[_sds((B * S, LANES), BF16), _sds((1, LANES), F32)], grid=(B,),
               in_specs=[blk, blk, one], out_specs=[blk, one], semantics=("arbitrary",))(dc, f_logit, b_f)


def _head_mask(lane, hh):
    return (lane < HEAD_DIM) if hh == 0 else (lane >= HEAD_DIM)


def _col_spec(rows, nblk_rows, cb):
    return pl.BlockSpec((rows, LANES), lambda b, p, i: (b * nblk_rows + i, cb + p))


def _kv_spec(rows, cb):
    return pl.BlockSpec((rows, LANES), lambda b, p, i: (b, cb + p))


def _stat_col_spec(tq):
    return pl.BlockSpec((1, 2, tq, 1), lambda b, p, i: (b, p, i, 0))


def _stat_row_spec(S):
    return pl.BlockSpec((1, 2, 1, S), lambda b, p, i: (b, p, 0, 0))


def _softmax_fwd(qa, ka, va, *, name, B, S, Sk, P, q_cb, k_cb, v_cb, causal, cc=None, cr=None):
    tq = min(ATT_TILE if causal else MEM_Q_TILE, S)
    tk = min(ATT_TILE, Sk)
    nq, nk = S // tq, Sk // tk
    decay = cc is not None
    assert not causal or (tq == tk and S == Sk)

    def body(*refs):
        if decay:
            q_ref, k_ref, v_ref, cc_ref, cr_ref, o_ref, lse_ref = refs
        else:
            q_ref, k_ref, v_ref, o_ref, lse_ref = refs
        i = pl.program_id(2)
        q = q_ref[...]
        lane = lax.broadcasted_iota(jnp.int32, (tq, LANES), 1)
        row = lax.broadcasted_iota(jnp.int32, (tq, tk), 0) + i * tq
        col0 = lax.broadcasted_iota(jnp.int32, (tq, tk), 1)
        outs = []
        for hh in range(2):
            qh = jnp.where(_head_mask(lane, hh), q, jnp.zeros_like(q))

            def step(kb, carry, hh=hh, qh=qh):
                m, l, acc = carry
                ks = pl.multiple_of(kb * tk, tk)
                kblk = k_ref[pl.ds(ks, tk), :]
                vblk = v_ref[pl.ds(ks, tk), :]
                s = lax.dot_general(qh, kblk, NT, preferred_element_type=F32) * SCALE
                if decay:
                    s = s + (cc_ref[0, hh] - cr_ref[0, hh, :, pl.ds(ks, tk)])
                if causal:
                    s = jnp.where(col0 + kb * tk <= row, s, NEG)
                m_new = jnp.maximum(m, jnp.max(s, axis=-1, keepdims=True))
                alpha = jnp.exp(m - m_new)
                p = jnp.exp(s - m_new)
                l = alpha * l + jnp.sum(p, axis=-1, keepdims=True)
                acc = alpha * acc + jnp.dot(p.astype(BF16), vblk, preferred_element_type=F32)
                return m_new, l, acc

            init = (jnp.full((tq, 1), NEG, F32), jnp.zeros((tq, 1), F32), jnp.zeros((tq, LANES), F32))
            m, l, acc = lax.fori_loop(0, (i + 1) if causal else nk, step, init)
            outs.append(acc / l)
            lse_ref[0, hh] = m + jnp.log(l)
        o_ref[...] = jnp.where(lane < HEAD_DIM, outs[0], outs[1]).astype(BF16)

    in_specs = [_col_spec(tq, nq, q_cb), _kv_spec(Sk, k_cb), _kv_spec(Sk, v_cb)]
    args = [qa, ka, va]
    if decay:
        in_specs += [_stat_col_spec(tq), _stat_row_spec(S)]
        args += [cc, cr]
    return _pc(body, name=name,
               out_shape=[_sds((B * S, P * LANES), BF16), _sds((B, 2 * P, S, 1), F32)],
               grid=(B, P, nq), in_specs=in_specs, out_specs=[_col_spec(tq, nq, 0), _stat_col_spec(tq)],
               semantics=("arbitrary", "arbitrary", "arbitrary"), vmem=VMEM_BIG)(*args)


def _softmax_bwd(qa, ka, va, doa, oa, lse, *, name, B, S, Sk, P, q_cb, k_cb, v_cb, do_cb, causal,
                 cc=None, cr=None):
    tq = min(ATT_TILE if causal else MEM_Q_TILE, S)
    tk = min(ATT_TILE, Sk)
    nq, nk = S // tq, Sk // tk
    decay = cc is not None

    def body(*refs):
        if decay:
            q_ref, k_ref, v_ref, do_ref, o_ref, lse_ref, cc_ref, cr_ref, dq_ref, dk_ref, dv_ref, dcs_ref = refs
        else:
            q_ref, k_ref, v_ref, do_ref, o_ref, lse_ref, dq_ref, dk_ref, dv_ref = refs
        i = pl.program_id(2)

        @pl.when(i == 0)
        def _():
            dk_ref[...] = jnp.zeros_like(dk_ref)
            dv_ref[...] = jnp.zeros_like(dv_ref)
            if decay:
                dcs_ref[...] = jnp.zeros_like(dcs_ref)

        q = q_ref[...]
        do = do_ref[...]
        prod = do.astype(F32) * o_ref[...].astype(F32)
        lane = lax.broadcasted_iota(jnp.int32, (tq, LANES), 1)
        row = lax.broadcasted_iota(jnp.int32, (tq, tk), 0) + i * tq
        col0 = lax.broadcasted_iota(jnp.int32, (tq, tk), 1)
        dqs = []
        for hh in range(2):
            hmask = _head_mask(lane, hh)
            qh = jnp.where(hmask, q, jnp.zeros_like(q))
            doh = jnp.where(hmask, do, jnp.zeros_like(do))
            lse_h = lse_ref[0, hh]
            n_blocks = (i + 1) if causal else nk

            def probs(kb, hh=hh, qh=qh, doh=doh, lse_h=lse_h):
                ks = pl.multiple_of(kb * tk, tk)
                kblk = k_ref[pl.ds(ks, tk), :]
                vblk = v_ref[pl.ds(ks, tk), :]
                s = lax.dot_general(qh, kblk, NT, preferred_element_type=F32) * SCALE
                if decay:
                    s = s + (cc_ref[0, hh] - cr_ref[0, hh, :, pl.ds(ks, tk)])
                if causal:
                    s = jnp.where(col0 + kb * tk <= row, s, NEG)
                p = jnp.exp(s - lse_h)
                dp = lax.dot_general(doh, vblk, NT, preferred_element_type=F32)
                return ks, kblk, p, dp

            if decay:
                def delta_step(kb, acc):
                    _, _, p, dp = probs(kb)
                    return acc + jnp.sum(p * dp, axis=-1, keepdims=True)

                delta = lax.fori_loop(0, n_blocks, delta_step, jnp.zeros((tq, 1), F32))
            else:
                delta = jnp.sum(jnp.where(hmask, prod, 0.0), axis=-1, keepdims=True)

            def step(kb, dq_acc, hh=hh, qh=qh, doh=doh, delta=delta):
                ks, kblk, p, dp = probs(kb)
                ds = p * (dp - delta)
                dsb = ds.astype(BF16)
                dk_ref[pl.ds(ks, tk), :] += lax.dot_general(dsb, qh, TN, preferred_element_type=F32) * SCALE
                dv_ref[pl.ds(ks, tk), :] += lax.dot_general(p.astype(BF16), doh, TN, preferred_element_type=F32)
                if decay:
                    dcs_ref[0, hh, :, pl.ds(ks, tk)] -= jnp.sum(ds, axis=0, keepdims=True)
                return dq_acc + jnp.dot(dsb, kblk, preferred_element_type=F32)

            dqs.append(lax.fori_loop(0, n_blocks, step, jnp.zeros((tq, LANES), F32)) * SCALE)
        dq_ref[...] = jnp.where(lane < HEAD_DIM, dqs[0], dqs[1]).astype(BF16)

    in_specs = [_col_spec(tq, nq, q_cb), _kv_spec(Sk, k_cb), _kv_spec(Sk, v_cb), _col_spec(tq, nq, do_cb),
                _col_spec(tq, nq, 0), _stat_col_spec(tq)]
    args = [qa, ka, va, doa, oa, lse]
    out_shape = [_sds((B * S, P * LANES), BF16), _sds((B * Sk, P * LANES), F32), _sds((B * Sk, P * LANES), F32)]
    out_specs = [_col_spec(tq, nq, 0), _kv_spec(Sk, 0), _kv_spec(Sk, 0)]
    if decay:
        in_specs += [_stat_col_spec(tq), _stat_row_spec(S)]
        args += [cc, cr]
        out_shape.append(_sds((B, 2 * P, 1, S), F32))
        out_specs.append(_stat_row_spec(S))
    return _pc(body, name=name, out_shape=out_shape, grid=(B, P, nq), in_specs=in_specs, out_specs=out_specs,
               semantics=("arbitrary", "arbitrary", "arbitrary"), vmem=VMEM_BIG)(*args)


HEAD_GROUP = 3
FWD_HEAD_GROUP = 6


def _g_col_spec(rows, nblk_rows, cb, G):
    return pl.BlockSpec((rows, G * LANES), lambda b, p, i: (b * nblk_rows + i, cb // G + p))


def _g_kv_spec(rows, cb, G):
    return pl.BlockSpec((rows, G * LANES), lambda b, p, i: (b, cb // G + p))


def _g_stat_col_spec(tq, G):
    return pl.BlockSpec((1, 2 * G, tq, 1), lambda b, p, i: (b, p, i, 0))


def _g_stat_row_spec(S, G):
    return pl.BlockSpec((1, 2 * G, 1, S), lambda b, p, i: (b, p, 0, 0))


def _lanes(g):
    return slice(g * LANES, (g + 1) * LANES)


def _streams(x_ref, G, scale=None):
    rows = x_ref.shape[0]
    lane = lax.broadcasted_iota(jnp.int32, (rows, LANES), 1)
    out = []
    for g in range(G):
        x = x_ref[:, _lanes(g)]
        if scale is not None:
            x = x * jnp.asarray(scale, x.dtype)
        for hh in range(2):
            out.append(jnp.where(_head_mask(lane, hh), x, jnp.zeros_like(x)))
    return lane, out


def _wide(stat, width):
    return jnp.tile(stat, (1, width // LANES))


def _fold_lanes(v):
    out = v[:, :LANES]
    for j in range(1, v.shape[1] // LANES):
        out = out + v[:, j * LANES:(j + 1) * LANES]
    return out


def _kv_blocks(ref, ks, tk, G):
    return [ref[pl.ds(ks, tk), _lanes(g)] for g in range(G)]


def _sweep(i, block):
    def step(kb, c):
        block(kb, False)
        return c
    lax.fori_loop(0, i, step, 0)
    block(i, True)


def _fox_fwd_g(qa, ka, va, cr, *, name, B, S, P, q_cb, k_cb, v_cb, G=HEAD_GROUP):
    tq = tk = min(ATT_TILE, S)
    nq = S // tq
    NS = 2 * G

    def body(q_ref, k_ref, v_ref, cr_ref, o_ref, lse_ref, acc_ref, m_ref, l_ref):
        i = pl.program_id(2)
        lane, qh = _streams(q_ref, G, SCALE)
        on_or_below = (lax.broadcasted_iota(jnp.int32, (tq, tk), 1) <= lax.broadcasted_iota(jnp.int32, (tq, tk), 0))
        m_ref[...] = jnp.full(m_ref.shape, NEG, F32)
        l_ref[...] = jnp.zeros(l_ref.shape, F32)
        acc_ref[...] = jnp.zeros(acc_ref.shape, F32)

        def block(kb, diag):
            ks = pl.multiple_of(kb * tk, tk)
            kblk = _kv_blocks(k_ref, ks, tk, G)
            vblk = _kv_blocks(v_ref, ks, tk, G)
            ss = [lax.dot_general(qh[st], kblk[st // 2], NT, preferred_element_type=F32) for st in range(NS)]
            ps = []
            for st in range(NS):
                s = ss[st] - cr_ref[0, st, :, pl.ds(ks, tk)]
                if diag:
                    s = jnp.where(on_or_below, s, NEG)
                m = m_ref[st]
                m_new = jnp.maximum(m, jnp.max(s, axis=-1, keepdims=True))
                alpha = jnp.exp(m - m_new)
                p = jnp.exp(s - _wide(m_new, tk))
                m_ref[st] = m_new
                l_ref[st] = alpha * l_ref[st] + _fold_lanes(p)
                ps.append((alpha, p.astype(BF16)))
            pvs = [jnp.dot(ps[st][1], vblk[st // 2], preferred_element_type=F32) for st in range(NS)]
            for st in range(NS):
                acc_ref[st] = ps[st][0] * acc_ref[st] + pvs[st]

        _sweep(i, block)
        ls = [jnp.sum(l_ref[st], axis=-1, keepdims=True) for st in range(NS)]
        for st in range(NS):
            lse_ref[0, st] = jnp.max(m_ref[st], axis=-1, keepdims=True) + jnp.log(ls[st])
        for g in range(G):
            o_ref[:, _lanes(g)] = jnp.where(lane < HEAD_DIM, acc_ref[2 * g] / ls[2 * g],
                                            acc_ref[2 * g + 1] / ls[2 * g + 1]).astype(BF16)

    return _pc(body, name=name, out_shape=[_sds((B * S, P * LANES), BF16), _sds((B, 2 * P, S, 1), F32)],
               grid=(B, P // G, nq),
               in_specs=[_g_col_spec(tq, nq, q_cb, G), _g_kv_spec(S, k_cb, G), _g_kv_spec(S, v_cb, G),
                         _g_stat_row_spec(S, G)],
               out_specs=[_g_col_spec(tq, nq, 0, G), _g_stat_col_spec(tq, G)],
               scratch_shapes=[pltpu.VMEM((NS, tq, LANES), F32)] * 3,
               semantics=("arbitrary", "arbitrary", "arbitrary"), vmem=VMEM_BIG)(qa, ka, va, cr)


def _fox_bwd_g(qa, ka, va, doa, lse, cr, *, name, B, S, P, q_cb, k_cb, v_cb, do_cb, G=HEAD_GROUP):
    tq = tk = min(ATT_TILE, S)
    nq = S // tq
    NS = 2 * G

    def body(q_ref, k_ref, v_ref, do_ref, lse_ref, cr_ref, dq_ref, dk_ref, dv_ref, dcs_ref, dqa_ref, delta_ref, lse_s,
             p_buf, dp_buf):
        i = pl.program_id(2)

        @pl.when(i == 0)
        def _():
            dk_ref[...] = jnp.zeros_like(dk_ref)
            dv_ref[...] = jnp.zeros_like(dv_ref)
            dcs_ref[...] = jnp.zeros_like(dcs_ref)

        lane, qh = _streams(q_ref, G, SCALE)
        _, doh = _streams(do_ref, G)
        on_or_below = (lax.broadcasted_iota(jnp.int32, (tq, tk), 1) <= lax.broadcasted_iota(jnp.int32, (tq, tk), 0))
        delta_ref[...] = jnp.zeros(delta_ref.shape, F32)
        dqa_ref[...] = jnp.zeros(dqa_ref.shape, F32)
        for st in range(NS):
            lse_s[st] = jnp.broadcast_to(lse_ref[0, st], (tq, LANES))

        def probs(kb, diag):
            ks = pl.multiple_of(kb * tk, tk)
            kblk = _kv_blocks(k_ref, ks, tk, G)
            vblk = _kv_blocks(v_ref, ks, tk, G)
            ss = [lax.dot_general(qh[st], kblk[st // 2], NT, preferred_element_type=F32) for st in range(NS)]
            dps = [lax.dot_general(doh[st], vblk[st // 2], NT, preferred_element_type=F32) for st in range(NS)]
            ps = []
            for st in range(NS):
                s = ss[st] - cr_ref[0, st, :, pl.ds(ks, tk)]
                if diag:
                    s = jnp.where(on_or_below, s, NEG)
                ps.append(jnp.exp(s - _wide(lse_s[st], tk)))
            return ks, kblk, ps, dps

        def delta_block(kb, diag):
            _, _, ps, dps = probs(kb, diag)
            for st in range(NS):
                delta_ref[st] += _fold_lanes(ps[st] * dps[st])
                p_buf[st, kb] = ps[st]
                dp_buf[st, kb] = dps[st]

        _sweep(i, delta_block)
        for st in range(NS):
            delta_ref[st] = jnp.broadcast_to(jnp.sum(delta_ref[st], axis=-1, keepdims=True), (tq, LANES))

        def grad_block(kb, diag):
            ks = pl.multiple_of(kb * tk, tk)
            kblk = _kv_blocks(k_ref, ks, tk, G)
            rows = pl.ds(ks, tk)
            dsb, pb = [], []
            for st in range(NS):
                p = p_buf[st, kb]
                ds = p * (dp_buf[st, kb] - _wide(delta_ref[st], tk))
                dcs_ref[0, st, :, rows] -= jnp.sum(ds, axis=0, keepdims=True)
                dsb.append(ds.astype(BF16))
                pb.append(p.astype(BF16))
            dks = [lax.dot_general(dsb[st], qh[st], TN, preferred_element_type=F32) for st in range(NS)]
            dvs = [lax.dot_general(pb[st], doh[st], TN, preferred_element_type=F32) for st in range(NS)]
            dqs = [jnp.dot(dsb[st], kblk[st // 2], preferred_element_type=F32) for st in range(NS)]
            for g in range(G):
                dk_ref[rows, _lanes(g)] += dks[2 * g] + dks[2 * g + 1]
                dv_ref[rows, _lanes(g)] += dvs[2 * g] + dvs[2 * g + 1]
            for st in range(NS):
                dqa_ref[st] += dqs[st]

        _sweep(i, grad_block)
        for g in range(G):
            dq_ref[:, _lanes(g)] = (jnp.where(lane < HEAD_DIM, dqa_ref[2 * g], dqa_ref[2 * g + 1]) * SCALE).astype(BF16)

    return _pc(body, name=name,
               out_shape=[_sds((B * S, P * LANES), BF16), _sds((B * S, P * LANES), F32), _sds((B * S, P * LANES), F32),
                          _sds((B, 2 * P, 1, S), F32)],
               grid=(B, P // G, nq),
               in_specs=[_g_col_spec(tq, nq, q_cb, G), _g_kv_spec(S, k_cb, G), _g_kv_spec(S, v_cb, G),
                         _g_col_spec(tq, nq, do_cb, G), _g_stat_col_spec(tq, G), _g_stat_row_spec(S, G)],
               out_specs=[_g_col_spec(tq, nq, 0, G), _g_kv_spec(S, 0, G), _g_kv_spec(S, 0, G), _g_stat_row_spec(S, G)],
               scratch_shapes=[pltpu.VMEM((NS, tq, LANES), F32)] * 3 + [pltpu.VMEM((NS, nq, tq, tk), F32)] * 2,
               semantics=("arbitrary", "arbitrary", "arbitrary"), vmem=VMEM_BIG)(qa, ka, va, doa, lse, cr)


def _sb_logs_z(z):
    nz = -z
    lm = jnp.minimum(nz, 0.0) - jnp.log(1.0 + jnp.exp(jnp.minimum(z, nz)))
    return lm + z, lm


def _sb_fwd_g(qa, ka, va, *, name, B, S, P, q_cb, k_cb, v_cb, G=HEAD_GROUP):
    tq = tk = min(ATT_TILE, S)
    nq = S // tq
    NS = 2 * G

    def body(q_ref, k_ref, v_ref, o_ref, rt_ref, acc_ref, run_ref):
        i = pl.program_id(2)
        lane, qh = _streams(q_ref, G, SCALE)
        t_r = lax.broadcasted_iota(jnp.int32, (tk, tk), 0)
        t_c = lax.broadcasted_iota(jnp.int32, (tk, tk), 1)
        after = (t_r > t_c).astype(BF16)
        below = t_c < t_r
        acc_ref[...] = jnp.zeros(acc_ref.shape, F32)
        run_ref[...] = jnp.zeros(run_ref.shape, F32)

        def block(kb, diag):
            ks = pl.multiple_of(kb * tk, tk)
            kblk = _kv_blocks(k_ref, ks, tk, G)
            vblk = _kv_blocks(v_ref, ks, tk, G)
            zs = [lax.dot_general(qh[st], kblk[st // 2], NT, preferred_element_type=F32) for st in range(NS)]
            lss, parts = [], []
            for st in range(NS):
                ls, lm = _sb_logs_z(zs[st])
                if diag:
                    lm = jnp.where(below, lm, 0.0)
                lss.append(ls + _wide(run_ref[st], tk))
                run_ref[st] += jnp.sum(lm, axis=-1, keepdims=True)
                parts.append(_split2(lm))
            sufs = [jnp.dot(parts[st][0], after, preferred_element_type=F32)
                    + jnp.dot(parts[st][1], after, preferred_element_type=F32) for st in range(NS)]
            ab = []
            for st in range(NS):
                a = jnp.exp(lss[st] + sufs[st])
                if diag:
                    a = jnp.where(below, a, 0.0)
                ab.append(a.astype(BF16))
            pvs = [jnp.dot(ab[st], vblk[st // 2], preferred_element_type=F32) for st in range(NS)]
            for st in range(NS):
                acc_ref[st] += pvs[st]

        block(i, True)

        def step(jj, c):
            block(i - 1 - jj, False)
            return c

        lax.fori_loop(0, i, step, 0)
        for st in range(NS):
            rt_ref[0, st] = jnp.max(run_ref[st], axis=-1, keepdims=True)
        for g in range(G):
            o_ref[:, _lanes(g)] = jnp.where(lane < HEAD_DIM, acc_ref[2 * g], acc_ref[2 * g + 1]).astype(BF16)

    return _pc(body, name=name, out_shape=[_sds((B * S, P * LANES), BF16), _sds((B, 2 * P, S, 1), F32)],
               grid=(B, P // G, nq),
               in_specs=[_g_col_spec(tq, nq, q_cb, G), _g_kv_spec(S, k_cb, G), _g_kv_spec(S, v_cb, G)],
               out_specs=[_g_col_spec(tq, nq, 0, G), _g_stat_col_spec(tq, G)],
               scratch_shapes=[pltpu.VMEM((NS, tq, LANES), F32)] * 2,
               semantics=("arbitrary", "arbitrary", "arbitrary"), vmem=VMEM_BIG)(qa, ka, va)


def _sb_bwd_g(qa, ka, va, doa, rt, *, name, B, S, P, q_cb, k_cb, v_cb, do_cb, G=HEAD_GROUP):
    tq = tk = min(ATT_TILE, S)
    nq = S // tq
    NS = 2 * G

    def body(q_ref, k_ref, v_ref, do_ref, rt_ref, dq_ref, dk_ref, dv_ref, dqa_ref, pl_ref, pg_ref):
        i = pl.program_id(2)

        @pl.when(i == 0)
        def _():
            dk_ref[...] = jnp.zeros_like(dk_ref)
            dv_ref[...] = jnp.zeros_like(dv_ref)

        lane, qh = _streams(q_ref, G, SCALE)
        _, doh = _streams(do_ref, G)
        t_r = lax.broadcasted_iota(jnp.int32, (tk, tk), 0)
        t_c = lax.broadcasted_iota(jnp.int32, (tk, tk), 1)
        upto = (t_r <= t_c).astype(BF16)
        before = (t_r < t_c).astype(BF16)
        below = t_c < t_r
        dqa_ref[...] = jnp.zeros(dqa_ref.shape, F32)
        pg_ref[...] = jnp.zeros(pg_ref.shape, F32)
        for st in range(NS):
            pl_ref[st] = jnp.broadcast_to(rt_ref[0, st], (tq, LANES))

        def block(kb, diag):
            ks = pl.multiple_of(kb * tk, tk)
            rows = pl.ds(ks, tk)
            kblk = _kv_blocks(k_ref, ks, tk, G)
            vblk = _kv_blocks(v_ref, ks, tk, G)
            zs = [lax.dot_general(qh[st], kblk[st // 2], NT, preferred_element_type=F32) for st in range(NS)]
            das = [lax.dot_general(doh[st], vblk[st // 2], NT, preferred_element_type=F32) for st in range(NS)]
            lss, parts = [], []
            for st in range(NS):
                ls, lm = _sb_logs_z(zs[st])
                if diag:
                    lm = jnp.where(below, lm, 0.0)
                lss.append((ls, ls + _wide(pl_ref[st], tk)))
                pl_ref[st] -= jnp.sum(lm, axis=-1, keepdims=True)
                parts.append(_split2(lm))
            pins = [jnp.dot(parts[st][0], upto, preferred_element_type=F32)
                    + jnp.dot(parts[st][1], upto, preferred_element_type=F32) for st in range(NS)]
            gms, ab, gparts = [], [], []
            for st in range(NS):
                a = jnp.exp(lss[st][1] - pins[st])
                if diag:
                    a = jnp.where(below, a, 0.0)
                gm = a * das[st]
                gms.append(gm)
                ab.append(a.astype(BF16))
                gparts.append(gm.astype(BF16))
            pgs = [jnp.dot(gparts[st], before, preferred_element_type=F32) for st in range(NS)]
            dzb = []
            for st in range(NS):
                gm = gms[st]
                dz = gm - jnp.exp(lss[st][0]) * (gm + (pgs[st] + _wide(pg_ref[st], tk)))
                if diag:
                    dz = jnp.where(below, dz, 0.0)
                pg_ref[st] += jnp.sum(gm, axis=-1, keepdims=True)
                dzb.append(dz.astype(BF16))
            dks = [lax.dot_general(dzb[st], qh[st], TN, preferred_element_type=F32) for st in range(NS)]
            dvs = [lax.dot_general(ab[st], doh[st], TN, preferred_element_type=F32) for st in range(NS)]
            dqs = [jnp.dot(dzb[st], kblk[st // 2], preferred_element_type=F32) for st in range(NS)]
            for g in range(G):
                dk_ref[rows, _lanes(g)] += dks[2 * g] + dks[2 * g + 1]
                dv_ref[rows, _lanes(g)] += dvs[2 * g] + dvs[2 * g + 1]
            for st in range(NS):
                dqa_ref[st] += dqs[st]

        _sweep(i, block)
        for g in range(G):
            dq_ref[:, _lanes(g)] = (jnp.where(lane < HEAD_DIM, dqa_ref[2 * g], dqa_ref[2 * g + 1]) * SCALE).astype(BF16)

    return _pc(body, name=name,
               out_shape=[_sds((B * S, P * LANES), BF16), _sds((B * S, P * LANES), F32), _sds((B * S, P * LANES), F32)],
               grid=(B, P // G, nq),
               in_specs=[_g_col_spec(tq, nq, q_cb, G), _g_kv_spec(S, k_cb, G), _g_kv_spec(S, v_cb, G),
                         _g_col_spec(tq, nq, do_cb, G), _g_stat_col_spec(tq, G)],
               out_specs=[_g_col_spec(tq, nq, 0, G), _g_kv_spec(S, 0, G), _g_kv_spec(S, 0, G)],
               scratch_shapes=[pltpu.VMEM((NS, tq, LANES), F32)] * 3,
               semantics=("arbitrary", "arbitrary", "arbitrary"), vmem=VMEM_BIG)(qa, ka, va, doa, rt)


def _sigmoid(v):
    return 0.5 * jnp.tanh(0.5 * v) + 0.5


def _shift_rows(cur, halo_ref, first, rows_idx, k):
    out = pltpu.roll(cur, k, 0)
    top = out[0:8, :]
    for r in range(k):
        hr = halo_ref.shape[0] - k + r
        edge = jnp.where(first, 0.0, halo_ref[hr:hr + 1, :])
        top = jnp.where(rows_idx[0:8, :] == r, edge, top)
    return jnp.concatenate([top, out[8:, :]], axis=0)


def _shift_rows_up(cur, halo_ref, last, rows_idx, k, ts):
    out = pltpu.roll(cur, ts - k, 0)
    bottom = out[ts - 8:, :]
    for r in range(k):
        edge = jnp.where(last, 0.0, halo_ref[r:r + 1, :])
        bottom = jnp.where(rows_idx[0:8, :] == 8 - k + r, edge, bottom)
    return jnp.concatenate([out[:ts - 8, :], bottom], axis=0)


def _ffn_up_gate(x, g, w, cw, cb, *, name, S):
    T, D = x.shape
    F = w.shape[1] // 2
    tm = min(1024, S)
    tn = 256
    nj = F // tn
    tiles_per_seq = S // tm
    halo = 16

    def body(x_ref, xh_ref, g_ref, wg_ref, wv_ref, cwg_ref, cwv_ref, cbg_ref, cbv_ref,
             uc_ref, ub_ref, a_ref, hout_ref, h_ref, hh_ref, eg_ref, ev_ref):
        first = lax.rem(pl.program_id(0), tiles_per_seq) == 0

        @pl.when(pl.program_id(1) == 0)
        def _():
            def norm(v):
                r = lax.rsqrt(jnp.mean(v * v, axis=-1, keepdims=True) + EPS)
                return ((v * r) * g_ref[...]).astype(BF16)
            h = norm(x_ref[...])
            h_ref[...] = h
            hout_ref[...] = h
            hh_ref[...] = norm(xh_ref[...])

        h = h_ref[...]
        rows_idx = lax.broadcasted_iota(jnp.int32, (tm, tn), 0)
        uc = []
        for half, (w_ref, cw_ref, cb_ref, e_ref) in enumerate(((wg_ref, cwg_ref, cbg_ref, eg_ref),
                                                               (wv_ref, cwv_ref, cbv_ref, ev_ref))):
            acc = jnp.dot(h, w_ref[...], preferred_element_type=F32)
            e_ref[...] = jnp.dot(hh_ref[...], w_ref[...], preferred_element_type=F32)
            ub_ref[half] = acc.astype(BF16)
            m1 = _shift_rows(acc, e_ref, first, rows_idx, 1)
            m2 = _shift_rows(acc, e_ref, first, rows_idx, 2)
            uc.append(cb_ref[...] + cw_ref[0:1, :] * m2 + cw_ref[1:2, :] * m1 + cw_ref[2:3, :] * acc)
            uc_ref[half] = uc[half]
        a_ref[...] = (uc[0] * _sigmoid(uc[0]) * uc[1]).astype(BF16)

    in_specs = [pl.BlockSpec((tm, D), lambda i, j: (i, 0)),
                pl.BlockSpec((halo, D), lambda i, j: (jnp.maximum(i * (tm // halo) - 1, 0), 0)),
                pl.BlockSpec((1, D), lambda i, j: (0, 0)),
                pl.BlockSpec((D, tn), lambda i, j: (0, j)), pl.BlockSpec((D, tn), lambda i, j: (0, j + nj)),
                pl.BlockSpec((3, tn), lambda i, j: (0, j)), pl.BlockSpec((3, tn), lambda i, j: (0, j + nj)),
                pl.BlockSpec((1, tn), lambda i, j: (0, j)), pl.BlockSpec((1, tn), lambda i, j: (0, j + nj))]
    return _pc(body, name=name,
               out_shape=[_sds((2, T, F), F32), _sds((2, T, F), BF16), _sds((T, F), BF16), _sds((T, D), BF16)],
               grid=(T // tm, nj), in_specs=in_specs,
               out_specs=[pl.BlockSpec((2, tm, tn), lambda i, j: (0, i, j)), pl.BlockSpec((2, tm, tn), lambda i, j: (0, i, j)),
                          pl.BlockSpec((tm, tn), lambda i, j: (i, j)), pl.BlockSpec((tm, D), lambda i, j: (i, 0))],
               scratch_shapes=[pltpu.VMEM((tm, D), BF16), pltpu.VMEM((halo, D), BF16),
                               pltpu.VMEM((halo, tn), F32), pltpu.VMEM((halo, tn), F32)],
               semantics=("arbitrary", "arbitrary"), vmem=VMEM_BIG)(x, x, g.reshape(1, D), w, w, cw, cw, cb, cb)


def _conv_gate_bwd(da, uc, ub, cw, *, name, B, S):
    F = uc.shape[2]
    tf = F // 2
    ts = min(256, S)
    ns, nf = S // ts, F // tf

    def body(da_ref, uc_ref, ub_ref, wg_ref, wv_ref, dug_ref, duv_ref, pg_ref, pv_ref, nxt_g, nxt_v):
        last = pl.program_id(2) == 0

        @pl.when(jnp.logical_and(pl.program_id(1) == 0, last))
        def _():
            pg_ref[...] = jnp.zeros_like(pg_ref)
            pv_ref[...] = jnp.zeros_like(pv_ref)

        rows_idx = lax.broadcasted_iota(jnp.int32, (ts, tf), 0)
        ucg, ucv = uc_ref[0], uc_ref[1]
        sg = _sigmoid(ucg)
        dav = da_ref[...]
        d_v = dav * (ucg * sg)
        d_g = dav * ucv * (sg * (1.0 + ucg * (1.0 - sg)))
        for half, (o_ref, p_ref, d, w_ref, nxt) in enumerate(((dug_ref, pg_ref, d_g, wg_ref, nxt_g),
                                                               (duv_ref, pv_ref, d_v, wv_ref, nxt_v))):
            p1 = _shift_rows_up(d, nxt, last, rows_idx, 1, ts)
            p2 = _shift_rows_up(d, nxt, last, rows_idx, 2, ts)
            o_ref[...] = (w_ref[2:3, :] * d + w_ref[1:2, :] * p1 + w_ref[0:1, :] * p2).astype(BF16)
            nxt[...] = d[0:8, :]
            uh = ub_ref[half].astype(F32)
            for k, dk in enumerate((p2, p1, d)):
                p_ref[k:k + 1, :] += jnp.sum(dk * uh, axis=0, keepdims=True)
            p_ref[3:4, :] += jnp.sum(d, axis=0, keepdims=True)

    row = pl.BlockSpec((ts, tf), lambda j, b, r: (b * ns + ns - 1 - r, j))
    both = pl.BlockSpec((2, ts, tf), lambda j, b, r: (0, b * ns + ns - 1 - r, j))
    par = pl.BlockSpec((8, tf), lambda j, b, r: (0, j))
    return _pc(body, name=name,
               out_shape=[_sds((B * S, F), BF16), _sds((B * S, F), BF16), _sds((8, F), F32), _sds((8, F), F32)],
               grid=(nf, B, ns),
               in_specs=[row, both, both, pl.BlockSpec((3, tf), lambda j, b, r: (0, j)),
                         pl.BlockSpec((3, tf), lambda j, b, r: (0, j + nf))],
               out_specs=[row, row, par, par],
               scratch_shapes=[pltpu.VMEM((8, tf), F32), pltpu.VMEM((8, tf), F32)],
               semantics=("arbitrary", "arbitrary", "arbitrary"), vmem=VMEM_BIG)(da, uc, ub, cw, cw)


def _adamw(w, g, m, v, *, name):
    rows, cols = w.shape
    tr = rows
    while tr * cols * 4 > 2 ** 20 and tr % 16 == 0:
        tr //= 2

    def body(w_ref, g_ref, m_ref, v_ref, d_ref, nm_ref, nv_ref):
        gv = g_ref[...]
        m_new = ADAM_B1 * m_ref[...] + (1.0 - ADAM_B1) * gv
        v_new = ADAM_B2 * v_ref[...] + (1.0 - ADAM_B2) * (gv * gv)
        m_hat = m_new / (1.0 - ADAM_B1 ** ADAM_STEP)
        v_hat = v_new / (1.0 - ADAM_B2 ** ADAM_STEP)
        d_ref[...] = -ADAM_LR * (m_hat / (jnp.sqrt(v_hat) + ADAM_EPS) + ADAM_WD * w_ref[...])
        nm_ref[...] = m_new
        nv_ref[...] = v_new

    blk = pl.BlockSpec((tr, cols), lambda i: (i, 0))
    return _pc(body, name=name, out_shape=[_sds((rows, cols), F32)] * 3, grid=(rows // tr,),
               in_specs=[blk] * 4, out_specs=[blk] * 3, semantics=("arbitrary",))(w, g, m, v)


def _my_pos():
    return lax.axis_index("x"), lax.axis_index("y"), lax.axis_index("c")


_HBM = pl.BlockSpec(memory_space=pltpu.HBM)
_SEM = pl.BlockSpec(memory_space=pltpu.SEMAPHORE)
_EFFECT = pltpu.SideEffectType.DATAFLOW_SIDE_EFFECTING


def _peers():
    x, y, c = _my_pos()
    out = []
    for k in range(1, N_DEV):
        px, py, pc = x ^ ((k >> 2) & 1), y ^ ((k >> 1) & 1), c ^ (k & 1)
        out.append(((px, py, pc), 4 * px + 2 * py + pc))
    return out


def _scatter_start(srcs, slot_of, *, name, order_after=None):
    n = len(srcs)
    lands = [lax.empty((N_DEV,) + slot_of(s, 0, shape_only=True), s.dtype) for s in srcs]
    extra = [] if order_after is None else [order_after]

    def body(*refs):
        src_refs, land_refs = refs[:n], refs[n:2 * n]
        send_sems, recv_sems = refs[2 * n + len(extra)], refs[2 * n + len(extra) + 1]
        token = refs[-1]
        x, y, c = _my_pos()
        me = 4 * x + 2 * y + c
        for a in range(n):
            for k, (peer, peer_idx) in enumerate(_peers()):
                pltpu.make_async_remote_copy(
                    src_ref=slot_of(src_refs[a], peer_idx), dst_ref=land_refs[a].at[me],
                    send_sem=send_sems.at[a * 7 + k], recv_sem=recv_sems.at[a * 7 + k],
                    device_id=peer, device_id_type=MESH).start()
        token[...] = jnp.zeros_like(token)

    hbm = lambda a: pltpu.HBM(a.shape, a.dtype)
    args = [pltpu.with_memory_space_constraint(a, pltpu.HBM) for a in list(srcs) + lands] + extra
    outs = pl.pallas_call(
        body, name=name,
        out_shape=(pltpu.SemaphoreType.DMA((7 * n,)), pltpu.SemaphoreType.DMA((7 * n,)),
                   *[hbm(a) for a in srcs], *[hbm(a) for a in lands], _sds((8, LANES), F32)),
        in_specs=[_HBM] * (2 * n) + [pl.BlockSpec(memory_space=pl.ANY)] * len(extra),
        out_specs=(_SEM, _SEM, *([_HBM] * (2 * n)), pl.BlockSpec(memory_space=pltpu.VMEM)),
        input_output_aliases={a: 2 + a for a in range(2 * n)},
        compiler_params=pltpu.CompilerParams(has_side_effects=_EFFECT))(*args)
    return outs[0], outs[1], list(outs[2:2 + n]), list(outs[2 + n:2 + 2 * n]), outs[-1]


def _scatter_wait(send_sems, recv_sems, srcs, lands, slot_of, after, *, name):
    n = len(srcs)

    def body(*refs):
        src_refs, land_refs = refs[:n], refs[n:2 * n]
        ssem, rsem = refs[2 * n], refs[2 * n + 1]
        x, y, c = _my_pos()
        me = 4 * x + 2 * y + c
        for a in range(n):
            for k, (peer, peer_idx) in enumerate(_peers()):
                cp = pltpu.make_async_remote_copy(
                    src_ref=slot_of(src_refs[a], peer_idx), dst_ref=land_refs[a].at[me],
                    send_sem=ssem.at[a * 7 + k], recv_sem=rsem.at[a * 7 + k],
                    device_id=peer, device_id_type=MESH)
                cp.wait_send()
                cp.wait_recv()

    hbm = lambda a: pltpu.HBM(a.shape, a.dtype)
    outs = pl.pallas_call(
        body, name=name, out_shape=tuple(hbm(a) for a in list(srcs) + list(lands)),
        in_specs=[_HBM] * (2 * n) + [_SEM, _SEM, pl.BlockSpec(memory_space=pl.ANY)],
        out_specs=tuple([_HBM] * (2 * n)), input_output_aliases={a: a for a in range(2 * n)},
        compiler_params=pltpu.CompilerParams(has_side_effects=_EFFECT))(*srcs, *lands, send_sems, recv_sems, after)
    return list(outs[:n]), list(outs[n:])


def _whole(a, peer_idx, shape_only=False):
    return a.shape if shape_only else a


def _slot(a, peer_idx, shape_only=False):
    return a.shape[1:] if shape_only else a.at[peer_idx]


def _sum_slots(a, *, name, tr=None):
    rows, cols = a.shape[1], a.shape[2]
    if tr is None:
        tr = rows
        while N_DEV * tr * cols * a.dtype.itemsize > 3 * 2 ** 20 and tr % 32 == 0:
            tr //= 2

    def body(a_ref, o_ref):
        acc = a_ref[0].astype(F32)
        for j in range(1, N_DEV):
            acc = acc + a_ref[j].astype(F32)
        o_ref[...] = acc

    return _pc(body, name=name, out_shape=_sds((rows, cols), F32), grid=(rows // tr,),
               in_specs=[pl.BlockSpec((N_DEV, tr, cols), lambda i: (0, i, 0))],
               out_specs=pl.BlockSpec((tr, cols), lambda i: (i, 0)), semantics=("arbitrary",), vmem=VMEM_BIG)(a)


def _to_slots(full, kind):
    if kind == "rows2":
        r, c = full.shape
        return full.reshape(N_DEV, r // N_DEV, c)
    if kind == "cols2":
        r, c = full.shape
        return full.reshape(r, N_DEV, c // N_DEV).transpose(1, 0, 2)
    if kind == "rows3":
        l, r, c = full.shape
        return full.reshape(l, N_DEV, r // N_DEV, c).transpose(1, 0, 2, 3)
    if kind == "cols3":
        l, r, c = full.shape
        return full.reshape(l, r, N_DEV, c // N_DEV).transpose(2, 0, 1, 3)
    raise ValueError(kind)


def _from_slots(slots, kind):
    if kind == "rows2":
        _, r, c = slots.shape
        return slots.reshape(N_DEV * r, c)
    if kind == "cols2":
        _, r, c = slots.shape
        return slots.transpose(1, 0, 2).reshape(r, N_DEV * c)
    if kind == "rows3":
        _, l, r, c = slots.shape
        return slots.transpose(1, 0, 2, 3).reshape(l, N_DEV * r, c)
    if kind == "cols3":
        _, l, r, c = slots.shape
        return slots.transpose(1, 2, 0, 3).reshape(l, r, N_DEV * c)
    raise ValueError(kind)


BIG = (("w_in_a", "rows2"), ("w_in_b", "rows2"), ("w_kv", "cols2"), ("w_memkv", "rows3"),
       ("w_out", "rows3"), ("w_up", "cols3"), ("w_down", "rows3"))


def _round_up(n, m):
    return -(-n // m) * m


def _pad_rows(a, rows, axis):
    pad = [(0, 0)] * a.ndim
    pad[axis] = (0, rows - a.shape[axis])
    return jnp.pad(a, pad)


def kernel(x, mem, ln_mix_g, w_in_a, b_f_a, w_in_b, ln_kv_g, w_kv, ln_mem_g, w_memkv, w_out, ln_ffn_g, w_up, conv_w, conv_b, w_down, final_g, loss_target, m_ln_mix_g, m_w_in_a, m_b_f_a, m_w_in_b, m_ln_kv_g, m_w_kv, m_ln_mem_g, m_w_memkv, m_w_out, m_ln_ffn_g, m_w_up, m_conv_w, m_conv_b, m_w_down, m_final_g, v_ln_mix_g, v_w_in_a, v_b_f_a, v_w_in_b, v_ln_kv_g, v_w_kv, v_ln_mem_g, v_w_memkv, v_w_out, v_ln_ffn_g, v_w_up, v_conv_w, v_conv_b, v_w_down, v_final_g):
    B, S, D = x.shape
    NM = mem.shape[1]
    T = B * S
    F = w_down.shape[1] * N_DEV
    my_idx = 4 * lax.axis_index("x") + 2 * lax.axis_index("y") + lax.axis_index("c")

    shards = {"w_in_a": w_in_a[0], "w_in_b": w_in_b[0], "w_kv": w_kv, "w_memkv": w_memkv, "w_out": w_out,
              "w_up": w_up, "w_down": w_down}
    moms = {"w_in_a": (m_w_in_a[0], v_w_in_a[0]), "w_in_b": (m_w_in_b[0], v_w_in_b[0]), "w_kv": (m_w_kv, v_w_kv),
            "w_memkv": (m_w_memkv, v_w_memkv), "w_out": (m_w_out, v_w_out), "w_up": (m_w_up, v_w_up),
            "w_down": (m_w_down, v_w_down)}

    groups = [("a1", [("w_in_a", None)]),
              ("a2", [("w_memkv", None), ("w_out", None), ("conv_w", None)]),
              ("b0", [("w_up", 0), ("w_down", 0)]), ("a3", [("w_in_b", None), ("w_kv", None)]),
              ("b1", [("w_up", 1), ("w_down", 1)])]
    sources = dict(shards, conv_w=conv_w)
    started, token = {}, None
    for gname, members in groups:
        srcs = []
        for n, layer in members:
            a = sources[n] if layer is None else sources[n][layer]
            srcs.append(a if n == "conv_w" else a.astype(BF16))
        ssem, rsem, thru, lands, token = _scatter_start(srcs, _whole, name=f"gather_start_{gname}", order_after=token)
        started[gname] = (ssem, rsem, thru, lands)

    def gathered(gname, after):
        ssem, rsem, thru, lands = started[gname]
        thru, lands = _scatter_wait(ssem, rsem, thru, lands, _whole, after, name=f"gather_wait_{gname}")
        return [lax.dynamic_update_index_in_dim(land, s, my_idx, 0) for land, s in zip(lands, thru)]

    full = {}
    (g_wa,) = gathered("a1", token)
    full["w_in_a"] = _from_slots(g_wa, "rows2")

    wa = full["w_in_a"]
    n_qkv = 3 * MAIN_W
    wa = jnp.concatenate([wa[:, :n_qkv], wa[:, n_qkv + N_MAIN_HEADS:], wa[:, n_qkv:n_qkv + N_MAIN_HEADS],
                          jnp.zeros((D, LANES - N_MAIN_HEADS), BF16)], axis=1)
    n_main = n_qkv + MEM_W
    full["w_up"], full["w_down"] = {}, {}
    b_f =_pad_rows(b_f_a.reshape(1, N_MAIN_HEADS), LANES, 1)

    x2d = x.reshape(T, D)
    mem2d = mem.reshape(B * NM, D)
    tgt2d = loss_target.reshape(T, D)
    PM, PX = N_MAIN_HEADS // 2, N_MEM_HEADS // 2

    def stats_to_heads(c2d):
        c = c2d.reshape(B, S, LANES)[:, :, :N_MAIN_HEADS].transpose(0, 2, 1)
        return c[:, :, None, :]

    def mem_kv(layer):
        return _mm_fwd(mem2d, full["w_memkv"][layer], name=f"memkv{layer}", tm=B * NM, tn=2 * MEM_W,
                       out_dtype=BF16, g=ln_mem_g[layer], save_h=True)

    def conv_ffn_fwd(xin, layer):
        uc, ub, a, h = _ffn_up_gate(xin, ln_ffn_g[layer], full["w_up"][layer], conv_w_full[layer],
                                    conv_b[layer].reshape(1, 2 * F), name=f"ffn_up{layer}", S=S)
        xo = _mm_fwd(a, full["w_down"][layer], name=f"ffn_down{layer}", tm=min(1024, T), tn=1024, out_dtype=F32, res=xin)
        return xo, (uc, ub, h, a)

    proj_a, h_mix0 = _mm_fwd(x2d, wa, name="in_proj_a", tm=min(1024, T), tn=1280, out_dtype=BF16, g=ln_mix_g[0],
                             ncols=n_main, save_h=True)
    f_logit = _mm_fwd(x2d, wa, name="in_proj_f", tm=min(1024, T), tn=LANES, out_dtype=F32, g=ln_mix_g[0],
                      col0=n_main // LANES, ncols=LANES)
    c2d = _forget_cumsum(f_logit, b_f, B=B, S=S, name="forget_cumsum")
    cr = stats_to_heads(c2d)
    o_main0, lse0 = _fox_fwd_g(proj_a, proj_a, proj_a, cr, name="fox_fwd", B=B, S=S, P=PM, q_cb=0, k_cb=PM, v_cb=2 * PM,
                               G=FWD_HEAD_GROUP)
    g_wmem, g_wout, g_cw = gathered("a2", lse0)
    full["w_memkv"] = _from_slots(g_wmem, "rows3")
    full["w_out"] = _from_slots(g_wout, "rows3")
    conv_w_full = _from_slots(g_cw, "cols3")
    memkv0, h_mem0 = mem_kv(0)
    o_mem0, lse_m0 = _softmax_fwd(proj_a, memkv0, memkv0, name="mem_fwd0", B=B, S=S, Sk=NM, P=PX, q_cb=3 * PM,
                                  k_cb=0, v_cb=PX, causal=False)
    o_cat0 = jnp.concatenate([o_main0, o_mem0], axis=1)
    x1 = _mm_fwd(o_cat0, full["w_out"][0], name="out_proj0", tm=min(1024, T), tn=1024, out_dtype=F32, res=x2d)
    g_up, g_dn = gathered("b0", x1)
    full["w_up"][0], full["w_down"][0] = _from_slots(g_up, "cols2"), _from_slots(g_dn, "rows2")
    x2, ffn_saved0 = conv_ffn_fwd(x1, 0)
    g_wb, g_wkv = gathered("a3", x2)
    wb, wkv = _from_slots(g_wb, "rows2"), _from_slots(g_wkv, "cols2")
    kv, h_kv =_mm_fwd(x2, wkv, name="kv_proj", tm=min(1024, T), tn=1536, out_dtype=BF16, g=ln_kv_g, save_h=True)
    proj_b, h_mix1 = _mm_fwd(x2, wb, name="in_proj_b", tm=min(1024, T), tn=1024, out_dtype=BF16, g=ln_mix_g[1],
                             save_h=True)
    o_main1, rt1 = _sb_fwd_g(proj_b, kv, kv, name="sb_fwd", B=B, S=S, P=PM, q_cb=0, k_cb=0, v_cb=PM,
                             G=FWD_HEAD_GROUP)
    memkv1, h_mem1 = mem_kv(1)
    o_mem1, lse_m1 = _softmax_fwd(proj_b, memkv1, memkv1, name="mem_fwd1", B=B, S=S, Sk=NM, P=PX, q_cb=PM,
                                  k_cb=0, v_cb=PX, causal=False)
    o_cat1 = jnp.concatenate([o_main1, o_mem1], axis=1)
    x3 = _mm_fwd(o_cat1, full["w_out"][1], name="out_proj1", tm=min(1024, T), tn=1024, out_dtype=F32, res=x2)
    g_up, g_dn = gathered("b1", x3)
    full["w_up"][1], full["w_down"][1] = _from_slots(g_up, "cols2"), _from_slots(g_dn, "rows2")
    x4, ffn_saved1 = conv_ffn_fwd(x3, 1)
    dx4, dg_final, loss_part = _loss_head(x4, final_g, tgt2d, name="loss_head")

    grads = {}
    small = {}
    reduce_groups = []

    def start_reduce(gname, keys, kinds):
        slots = [_to_slots(grads[k], kind) for k, kind in zip(keys, kinds)]
        ssem, rsem, thru, lands, tok = _scatter_start(slots, _slot, name=f"reduce_start_{gname}")
        reduce_groups.append((gname, keys, ssem, rsem, thru, lands))
        return tok[0, 0]

    def conv_ffn_bwd(dxo, xin, saved, layer):
        uc, ub, h, a = saved
        w_dn = full["w_down"][layer]
        da = _mm_nt(dxo, w_dn, name=f"d_act{layer}", tm=min(1024, T), tn=F // 2, out_dtype=F32)
        grads[("w_down", layer)] = _wgrad(a, dxo, f"g_w_down{layer}")
        cwl = conv_w_full[layer]
        du_g, du_v, p_g, p_v = _conv_gate_bwd(da, uc, ub, cwl, name=f"conv_bwd{layer}", B=B, S=S)
        small[("conv_w", layer)] = jnp.concatenate([p_g[0:3], p_v[0:3]], axis=1)
        small[("conv_b", layer)] = jnp.concatenate([p_g[3], p_v[3]], axis=0)
        grads[("w_up", layer)] = jnp.concatenate(
            [_wgrad(h, du_g, f"g_w_up_gate{layer}"), _wgrad(h, du_v, f"g_w_up_val{layer}")], axis=1)
        tok = start_reduce(f"ffn{layer}", [("w_down", layer), ("w_up", layer)], ["rows2", "cols2"])
        dxi, dg = _mm_nt_rmsbwd([(du_g, 0), (du_v, 1)], full["w_up"][layer], xin, ln_ffn_g[layer] + tok,
                                name=f"d_ffn_in{layer}", dres=dxo)
        small[("ln_ffn_g", layer)] = dg[0]
        return dxi

    def mem_bwd(proj, q_cb, memkv, h_mem, do_cat, o_mem, lse_m, layer):
        dqm, dmk, dmv = _softmax_bwd(proj, memkv, memkv, do_cat, o_mem, lse_m, name=f"mem_bwd{layer}", B=B, S=S,
                                     Sk=NM, P=PX, q_cb=q_cb, k_cb=0, v_cb=PX, do_cb=PM, causal=False)
        grads[("w_memkv", layer)] = jnp.concatenate(
            [_wgrad(h_mem, dmk, f"g_w_memk{layer}"), _wgrad(h_mem, dmv, f"g_w_memv{layer}")], axis=1)
        _, dg = _mm_nt_rmsbwd([(dmk, 0), (dmv, 1)], full["w_memkv"][layer], mem2d, ln_mem_g[layer],
                              name=f"d_mem_in{layer}", want_dx=False)
        small[("ln_mem_g", layer)] = dg[0]
        return dqm

    dx3 = conv_ffn_bwd(dx4, x3, ffn_saved1, 1)
    do_cat1 = _mm_nt(dx3, full["w_out"][1], name="d_o_cat1", tm=min(1024, T), tn=1024, out_dtype=BF16)
    grads[("w_out", 1)] = _wgrad(o_cat1, dx3, "g_w_out1")
    dq1, dk1, dv1 = _sb_bwd_g(proj_b, kv, kv, do_cat1, rt1, name="sb_bwd", B=B, S=S, P=PM, q_cb=0, k_cb=0, v_cb=PM,
                            do_cb=0)
    dqm1 = mem_bwd(proj_b, PM, memkv1, h_mem1, do_cat1, o_mem1, lse_m1, 1)
    grads["w_in_b"] = jnp.concatenate([_wgrad(h_mix1, dq1, "g_w_in_b_q"), _wgrad(h_mix1, dqm1, "g_w_in_b_m")], axis=1)
    grads["w_kv"] = jnp.concatenate([_wgrad(h_kv, dk1, "g_w_kv_k"), _wgrad(h_kv, dv1, "g_w_kv_v")], axis=1)
    tok = start_reduce("mix1", [("w_out", 1), "w_in_b", "w_kv", ("w_memkv", 1)], ["rows2", "rows2", "cols2", "rows2"])
    dx2, dg = _mm_nt_rmsbwd([(dq1, 0), (dqm1, MAIN_W // MEM_W)], wb, x2, ln_mix_g[1] + tok, name="d_mix_in1", dres=dx3)
    small[("ln_mix_g", 1)] = dg[0]
    dx2, dg = _mm_nt_rmsbwd([(dk1, 0), (dv1, 1)], wkv, x2, ln_kv_g, name="d_kv_in", dres=dx2)
    small["ln_kv_g"] = dg[0]
    dx1 = conv_ffn_bwd(dx2, x1, ffn_saved0, 0)
    do_cat0 = _mm_nt(dx1, full["w_out"][0], name="d_o_cat0", tm=min(1024, T), tn=1024, out_dtype=BF16)
    grads[("w_out", 0)] = _wgrad(o_cat0, dx1, "g_w_out0")
    dq0, dk0, dv0, dcs = _fox_bwd_g(proj_a, proj_a, proj_a, do_cat0, lse0, cr, name="fox_bwd", B=B, S=S, P=PM, q_cb=0,
                                  k_cb=PM, v_cb=2 * PM, do_cb=0)
    dqm0 = mem_bwd(proj_a, 3 * PM, memkv0, h_mem0, do_cat0, o_mem0, lse_m0, 0)
    dc2d = _pad_rows(dcs[:, :, 0, :].transpose(0, 2, 1).reshape(T, N_MAIN_HEADS), LANES, 1)
    df, db_f = _forget_cumsum_bwd(dc2d, f_logit, b_f, B=B, S=S, name="forget_cumsum_bwd")
    a_parts = [(dq0, 0), (dk0, 1), (dv0, 2), (dqm0, n_qkv // MEM_W), (df, n_main // LANES)]
    g_wa = jnp.concatenate([_wgrad(h_mix0, p, f"g_w_in_a{k}") for k, (p, _) in enumerate(a_parts)], axis=1)
    grads["w_in_a"] = jnp.concatenate([g_wa[:, :n_qkv], g_wa[:, n_main:n_main + N_MAIN_HEADS], g_wa[:, n_qkv:n_main]],
                                      axis=1)
    tok = start_reduce("mix0", [("w_out", 0), ("w_memkv", 0), "w_in_a"], ["rows2", "rows2", "rows2"])
    dx0, dg = _mm_nt_rmsbwd(a_parts, wa, x2d, ln_mix_g[0] + tok, name="d_mix_in0", dres=dx1)
    small[("ln_mix_g", 0)] = dg[0]
    grad_x = dx0.reshape(B, S, D)

    def both_small(name):
        return jnp.stack([small[(name, 0)], small[(name, 1)]])

    small_list = [("ln_mix_g", both_small("ln_mix_g")), ("b_f_a", db_f[:, :N_MAIN_HEADS]), ("ln_kv_g", small["ln_kv_g"]),
                  ("ln_mem_g", both_small("ln_mem_g")), ("ln_ffn_g", both_small("ln_ffn_g")),
                  ("conv_w", both_small("conv_w")), ("conv_b", both_small("conv_b")), ("final_g", dg_final[0]),
                  ("loss", loss_part[0, :1])]
    sm_rows = []
    for _, a in small_list:
        flat = a.reshape(-1)
        sm_rows.append(_pad_rows(flat, _round_up(flat.size, 8 * LANES), 0).reshape(-1, LANES))
    spack = jnp.concatenate(sm_rows, axis=0)
    s_ssem, s_rsem, s_thru, s_lands, s_tok = _scatter_start([spack], _whole, name="small_start")

    pieces = {}
    for gname, keys, ssem, rsem, thru, lands in reduce_groups:
        thru, lands = _scatter_wait(ssem, rsem, thru, lands, _slot, s_tok, name=f"reduce_wait_{gname}")
        for key, mine, land in zip(keys, thru, lands):
            own = lax.dynamic_index_in_dim(mine, my_idx, 0, keepdims=False)
            land = lax.dynamic_update_index_in_dim(land, own, my_idx, 0)
            tag = key if isinstance(key, str) else f"{key[0]}{key[1]}"
            pieces[key] = _sum_slots(land, name=f"sum_{tag}")

    red = {}
    for n in ("w_in_a", "w_in_b", "w_kv"):
        red[n] = pieces[n].reshape(shards[n].shape)
    for n in ("w_memkv", "w_out", "w_up", "w_down"):
        red[n] = jnp.stack([pieces[(n, 0)], pieces[(n, 1)]])

    weights = {"ln_mix_g": ln_mix_g, "w_in_a": w_in_a, "b_f_a": b_f_a, "w_in_b": w_in_b, "ln_kv_g": ln_kv_g,
               "w_kv": w_kv, "ln_mem_g": ln_mem_g, "w_memkv": w_memkv, "w_out": w_out, "ln_ffn_g": ln_ffn_g,
               "w_up": w_up, "conv_w": conv_w, "conv_b": conv_b, "w_down": w_down, "final_g": final_g}
    m_in = {"ln_mix_g": m_ln_mix_g, "w_in_a": m_w_in_a, "b_f_a": m_b_f_a, "w_in_b": m_w_in_b, "ln_kv_g": m_ln_kv_g,
            "w_kv": m_w_kv, "ln_mem_g": m_ln_mem_g, "w_memkv": m_w_memkv, "w_out": m_w_out, "ln_ffn_g": m_ln_ffn_g,
            "w_up": m_w_up, "conv_w": m_conv_w, "conv_b": m_conv_b, "w_down": m_w_down, "final_g": m_final_g}
    v_in = {"ln_mix_g": v_ln_mix_g, "w_in_a": v_w_in_a, "b_f_a": v_b_f_a, "w_in_b": v_w_in_b, "ln_kv_g": v_ln_kv_g,
            "w_kv": v_w_kv, "ln_mem_g": v_ln_mem_g, "w_memkv": v_w_memkv, "w_out": v_w_out, "ln_ffn_g": v_ln_ffn_g,
            "w_up": v_w_up, "conv_w": v_conv_w, "conv_b": v_conv_b, "w_down": v_w_down, "final_g": v_final_g}
    order = list(weights)
    big_names = [n for n, _ in BIG]
    g_out, d_out, nm_out, nv_out = {}, {}, {}, {}

    def update(n):
        w = weights[n]
        cols = w.shape[-1]
        g = red[n].reshape(w.shape)
        d, nm, nv = _adamw(w.reshape(-1, cols), g.reshape(-1, cols), m_in[n].reshape(-1, cols),
                           v_in[n].reshape(-1, cols), name=f"adamw_{n}")
        g_out[n], d_out[n], nm_out[n], nv_out[n] = g, d.reshape(w.shape), nm.reshape(w.shape), nv.reshape(w.shape)

    for n in big_names:
        update(n)
    all_updated = jnp.stack([d_out[n].reshape(-1)[0] for n in big_names])
    s_thru, s_lands = _scatter_wait(s_ssem, s_rsem, s_thru, s_lands, _whole, all_updated, name="small_wait")
    ssum = _sum_slots(lax.dynamic_update_index_in_dim(s_lands[0], s_thru[0], my_idx, 0), name="sum_small")
    off = 0
    for (n, a), rows in zip(small_list, sm_rows):
        red[n] = ssum[off:off + rows.shape[0]].reshape(-1)[:a.size].reshape(a.shape)
        off += rows.shape[0]
    loss = red["loss"][0]
    shard_cols = conv_w.shape[2]
    red["conv_w"] = lax.dynamic_slice_in_dim(red["conv_w"], my_idx * shard_cols, shard_cols, axis=2)
    red["b_f_a"] = red["b_f_a"].reshape(b_f_a.shape)
    update("conv_w")
    small_names = [n for n in order if n not in g_out]

    def pack_small(src):
        rows = []
        for n in small_names:
            flat = src[n].reshape(-1)
            rows.append(_pad_rows(flat, _round_up(flat.size, 8 * LANES), 0).reshape(-1, LANES))
        return jnp.concatenate(rows, axis=0), [r.shape[0] for r in rows]

    red_small = {n: red[n].reshape(weights[n].shape) for n in small_names}
    wp, counts = pack_small(weights)
    gp, _ = pack_small(red_small)
    mp, _ = pack_small(m_in)
    vp, _ = pack_small(v_in)
    dp, nmp, nvp = _adamw(wp, gp, mp, vp, name="adamw_small")
    off = 0
    for n, cnt in zip(small_names, counts):
        shp = weights[n].shape
        size = weights[n].size
        g_out[n] = red_small[n]
        d_out[n] = dp[off:off + cnt].reshape(-1)[:size].reshape(shp)
        nm_out[n] = nmp[off:off + cnt].reshape(-1)[:size].reshape(shp)
        nv_out[n] = nvp[off:off + cnt].reshape(-1)[:size].reshape(shp)
        off += cnt

    return (loss, grad_x, *[g_out[n] for n in order], *[d_out[n] for n in order],
            *[nm_out[n] for n in order], *[nv_out[n] for n in order])
```

```python
import functools

import jax
import jax.numpy as jnp
from jax import lax
from jax.experimental import pallas as pl
from jax.experimental.pallas import tpu as pltpu

F32 = jnp.float32
BF16 = jnp.bfloat16
LANES = 128
HEAD_DIM = 64
N_MAIN_HEADS = 12
N_MEM_HEADS = 4
MAIN_W = N_MAIN_HEADS * HEAD_DIM
MEM_W = N_MEM_HEADS * HEAD_DIM
SCALE = HEAD_DIM ** -0.5
EPS = 1e-6
NEG = -1e30
N_DEV = 8
ATT_TILE = 256
MEM_Q_TILE = 1024
VMEM_BIG = 56 * 2 ** 20
MESH = pl.DeviceIdType.MESH

ADAM_LR = 0.001
ADAM_B1 = 0.9
ADAM_B2 = 0.999
ADAM_EPS = 1e-08
ADAM_WD = 0.01
ADAM_STEP = 10

NT = (((1,), (1,)), ((), ()))
TN = (((0,), (0,)), ((), ()))


def _pc(body, *, name, out_shape, grid=None, in_specs=None, out_specs=None, scratch_shapes=(),
        semantics=None, vmem=None):
    kw = {}
    if grid is not None:
        kw["grid"] = grid
    params = pltpu.CompilerParams(dimension_semantics=semantics, vmem_limit_bytes=vmem)
    return pl.pallas_call(body, name=name, out_shape=out_shape, in_specs=in_specs, out_specs=out_specs,
                          scratch_shapes=list(scratch_shapes), compiler_params=params, **kw)


def _sds(shape, dtype):
    return jax.ShapeDtypeStruct(shape, dtype)


def _mm_fwd(a, w, *, name, tm, tn, out_dtype, g=None, res=None, col0=0, ncols=None, save_h=False):
    m_rows, k = a.shape
    n = w.shape[1] if ncols is None else ncols
    grid = (m_rows // tm, n // tn)
    norm = g is not None

    def body(*refs):
        refs = list(refs)
        a_ref = refs.pop(0)
        g_ref = refs.pop(0) if norm else None
        w_ref = refs.pop(0)
        res_ref = refs.pop(0) if res is not None else None
        o_ref = refs.pop(0)
        hout_ref = refs.pop(0) if save_h else None
        h_ref = refs.pop(0) if norm else None
        if norm:
            @pl.when(pl.program_id(1) == 0)
            def _():
                xv = a_ref[...]
                r = lax.rsqrt(jnp.mean(xv * xv, axis=-1, keepdims=True) + EPS)
                h = ((xv * r) * g_ref[...]).astype(BF16)
                h_ref[...] = h
                if save_h:
                    hout_ref[...] = h
            lhs = h_ref[...]
        else:
            lhs = a_ref[...].astype(BF16)
        acc = jnp.dot(lhs, w_ref[...], preferred_element_type=F32)
        if res is not None:
            acc = acc + res_ref[...]
        o_ref[...] = acc.astype(out_dtype)

    in_specs = [pl.BlockSpec((tm, k), lambda i, j: (i, 0))]
    args = [a]
    if norm:
        in_specs.append(pl.BlockSpec((1, k), lambda i, j: (0, 0)))
        args.append(g.reshape(1, k))
    in_specs.append(pl.BlockSpec((k, tn), lambda i, j: (0, j + col0)))
    args.append(w)
    if res is not None:
        in_specs.append(pl.BlockSpec((tm, tn), lambda i, j: (i, j)))
        args.append(res)
    out_shape = [_sds((m_rows, n), out_dtype)]
    out_specs = [pl.BlockSpec((tm, tn), lambda i, j: (i, j))]
    if save_h:
        out_shape.append(_sds((m_rows, k), BF16))
        out_specs.append(pl.BlockSpec((tm, k), lambda i, j: (i, 0)))
    scratch = [pltpu.VMEM((tm, k), BF16)] if norm else []
    outs = _pc(body, name=name, out_shape=out_shape, grid=grid, in_specs=in_specs, out_specs=out_specs,
               scratch_shapes=scratch, semantics=("arbitrary", "arbitrary"), vmem=VMEM_BIG)(*args)
    return outs if save_h else outs[0]


def _mm_nt(a, w, *, name, tm, tn, out_dtype):
    m_rows, k = a.shape
    n = w.shape[0]

    def body(a_ref, w_ref, o_ref):
        acc = lax.dot_general(a_ref[...].astype(BF16), w_ref[...], NT, preferred_element_type=F32)
        o_ref[...] = acc.astype(out_dtype)

    return _pc(body, name=name, out_shape=_sds((m_rows, n), out_dtype), grid=(m_rows // tm, n // tn),
               in_specs=[pl.BlockSpec((tm, k), lambda i, j: (i, 0)), pl.BlockSpec((tn, k), lambda i, j: (j, 0))],
               out_specs=pl.BlockSpec((tm, tn), lambda i, j: (i, j)),
               semantics=("arbitrary", "arbitrary"), vmem=VMEM_BIG)(a, w)


def _mm_tn(a, b, *, name, ta, tn, tt):
    t_rows, ka = a.shape
    n = b.shape[1]
    nt = t_rows // tt

    def body(a_ref, b_ref, o_ref, acc_ref):
        t = pl.program_id(2)

        @pl.when(t == 0)
        def _():
            acc_ref[...] = jnp.zeros_like(acc_ref)

        acc_ref[...] += lax.dot_general(a_ref[...].astype(BF16), b_ref[...].astype(BF16), TN,
                                        preferred_element_type=F32)

        @pl.when(t == nt - 1)
        def _():
            o_ref[...] = acc_ref[...].astype(BF16)

    return _pc(body, name=name, out_shape=_sds((ka, n), BF16), grid=(ka // ta, n // tn, nt),
               in_specs=[pl.BlockSpec((tt, ta), lambda i, j, t: (t, i)),
                         pl.BlockSpec((tt, tn), lambda i, j, t: (t, j))],
               out_specs=pl.BlockSpec((ta, tn), lambda i, j, t: (i, j)),
               scratch_shapes=[pltpu.VMEM((ta, tn), F32)],
               semantics=("arbitrary", "arbitrary", "arbitrary"), vmem=VMEM_BIG)(a, b)


def _wgrad(a, b, name):
    t_rows, ka = a.shape
    n = b.shape[1]
    ta = ka if ka <= 1024 else ka // 2
    tn = n
    while ta * tn * 4 > 6 * 2 ** 20 and tn % 256 == 0:
        tn //= 2
    tt = min(1024, t_rows)
    return _mm_tn(a, b, name=name, ta=ta, tn=tn, tt=tt)


def _mm_nt_rmsbwd(parts, w, x, g, *, name, dres=None, want_dx=True):
    m_rows, d = x.shape
    k_total = sum(dy.shape[1] for dy, _ in parts)
    tm = min(512 if k_total <= 2816 else 256, m_rows)
    n_parts = len(parts)

    def body(*refs):
        refs = list(refs)
        dy_refs = [refs.pop(0) for _ in range(n_parts)]
        w_refs = [refs.pop(0) for _ in range(n_parts)]
        x_ref = refs.pop(0)
        g_ref = refs.pop(0)
        dres_ref = refs.pop(0) if dres is not None else None
        dx_ref = refs.pop(0) if want_dx else None
        dg_ref = refs.pop(0)

        @pl.when(pl.program_id(0) == 0)
        def _():
            dg_ref[...] = jnp.zeros_like(dg_ref)

        dh = None
        for dy_ref, w_ref in zip(dy_refs, w_refs):
            t = lax.dot_general(dy_ref[...].astype(BF16), w_ref[...], NT, preferred_element_type=F32)
            dh = t if dh is None else dh + t
        xv = x_ref[...]
        r = lax.rsqrt(jnp.mean(xv * xv, axis=-1, keepdims=True) + EPS)
        xh = xv * r
        dg_ref[...] += jnp.sum(dh * xh, axis=0, keepdims=True)
        if want_dx:
            dhg = dh * g_ref[...]
            dx = r * (dhg - xh * jnp.mean(dhg * xh, axis=-1, keepdims=True))
            if dres is not None:
                dx = dx + dres_ref[...]
            dx_ref[...] = dx

    in_specs, args = [], []
    for dy, _ in parts:
        in_specs.append(pl.BlockSpec((tm, dy.shape[1]), lambda i: (i, 0)))
        args.append(dy)
    for dy, cb in parts:
        in_specs.append(pl.BlockSpec((d, dy.shape[1]), functools.partial(lambda i, cb: (0, cb), cb=cb)))
        args.append(w)
    in_specs += [pl.BlockSpec((tm, d), lambda i: (i, 0)), pl.BlockSpec((1, d), lambda i: (0, 0))]
    args += [x, g.reshape(1, d)]
    if dres is not None:
        in_specs.append(pl.BlockSpec((tm, d), lambda i: (i, 0)))
        args.append(dres)
    out_shape, out_specs = [], []
    if want_dx:
        out_shape.append(_sds((m_rows, d), F32))
        out_specs.append(pl.BlockSpec((tm, d), lambda i: (i, 0)))
    out_shape.append(_sds((1, d), F32))
    out_specs.append(pl.BlockSpec((1, d), lambda i: (0, 0)))
    outs = _pc(body, name=name, out_shape=out_shape, grid=(m_rows // tm,), in_specs=in_specs,
               out_specs=out_specs, semantics=("arbitrary",), vmem=VMEM_BIG)(*args)
    return (outs[0], outs[1]) if want_dx else (None, outs[0])


def _loss_head(x, g, tgt, *, name):
    m_rows, d = x.shape
    tm = min(512, m_rows)

    def body(x_ref, g_ref, t_ref, dx_ref, dg_ref, loss_ref):
        @pl.when(pl.program_id(0) == 0)
        def _():
            dg_ref[...] = jnp.zeros_like(dg_ref)
            loss_ref[...] = jnp.zeros_like(loss_ref)

        xv = x_ref[...]
        r = lax.rsqrt(jnp.mean(xv * xv, axis=-1, keepdims=True) + EPS)
        xh = xv * r
        gv = g_ref[...]
        err = xh * gv - t_ref[...]
        per_tok = jnp.mean(err * err, axis=-1, keepdims=True)
        loss_ref[...] += 0.5 * jnp.sum(per_tok, axis=0, keepdims=True)
        dout = err * (1.0 / d)
        dg_ref[...] += jnp.sum(dout * xh, axis=0, keepdims=True)
        dhg = dout * gv
        dx_ref[...] = r * (dhg - xh * jnp.mean(dhg * xh, axis=-1, keepdims=True))

    row = pl.BlockSpec((tm, d), lambda i: (i, 0))
    return _pc(body, name=name, out_shape=[_sds((m_rows, d), F32), _sds((1, d), F32), _sds((1, LANES), F32)],
               grid=(m_rows // tm,), in_specs=[row, pl.BlockSpec((1, d), lambda i: (0, 0)), row],
               out_specs=[row, pl.BlockSpec((1, d), lambda i: (0, 0)), pl.BlockSpec((1, LANES), lambda i: (0, 0))],
               semantics=("arbitrary",))(x, g.reshape(1, d), tgt)


def _split3(v):
    hi = v.astype(BF16)
    r1 = v - hi.astype(F32)
    mid = r1.astype(BF16)
    lo = (r1 - mid.astype(F32)).astype(BF16)
    return hi, mid, lo


def _split2(v):
    hi = v.astype(BF16)
    lo = (v - hi.astype(F32)).astype(BF16)
    return hi, lo


def _tri_dot3(tri, v):
    hi, mid, lo = _split3(v)
    return (jnp.dot(tri, hi, preferred_element_type=F32) + jnp.dot(tri, mid, preferred_element_type=F32)
            + jnp.dot(tri, lo, preferred_element_type=F32))


def _log_sigmoid(v):
    return jnp.minimum(v, 0.0) - jnp.log(1.0 + jnp.exp(-jnp.abs(v)))


def _forget_cumsum(f_logit, b_f, *, B, S, name):
    ch = min(256, S)
    nch = S // ch

    def body(f_ref, b_ref, c_ref):
        r_i = lax.broadcasted_iota(jnp.int32, (ch, ch), 0)
        c_i = lax.broadcasted_iota(jnp.int32, (ch, ch), 1)
        tri = (c_i <= r_i).astype(BF16)
        bv = b_ref[...]

        def step(k, carry):
            rows = pl.ds(pl.multiple_of(k * ch, ch), ch)
            lf = _log_sigmoid(f_ref[rows, :] + bv)
            c_ref[rows, :] = _tri_dot3(tri, lf) + carry
            return carry + jnp.sum(lf, axis=0, keepdims=True)

        lax.fori_loop(0, nch, step, jnp.zeros((1, LANES), F32))

    blk = pl.BlockSpec((S, LANES), lambda b: (b, 0))
    return _pc(body, name=name, out_shape=_sds((B * S, LANES), F32), grid=(B,),
               in_specs=[blk, pl.BlockSpec((1, LANES), lambda b: (0, 0))], out_specs=blk,
               semantics=("arbitrary",))(f_logit, b_f)


def _forget_cumsum_bwd(dc, f_logit, b_f, *, B, S, name):
    ch = min(256, S)
    nch = S // ch

    def body(dc_ref, f_ref, b_ref, df_ref, db_ref):
        @pl.when(pl.program_id(0) == 0)
        def _():
            db_ref[...] = jnp.zeros_like(db_ref)

        r_i = lax.broadcasted_iota(jnp.int32, (ch, ch), 0)
        c_i = lax.broadcasted_iota(jnp.int32, (ch, ch), 1)
        tri = (c_i >= r_i).astype(BF16)
        bv = b_ref[...]

        def step(kk, carry):
            tail, dbs = carry
            k = nch - 1 - kk
            rows = pl.ds(pl.multiple_of(k * ch, ch), ch)
            dcv = dc_ref[rows, :]
            dlf = _tri_dot3(tri, dcv) + tail
            z = f_ref[rows, :] + bv
            df = dlf * (1.0 / (1.0 + jnp.exp(z)))
            df_ref[rows, :] = df.astype(BF16)
            return tail + jnp.sum(dcv, axis=0, keepdims=True), dbs + jnp.sum(df, axis=0, keepdims=True)

        zero = jnp.zeros((1, LANES), F32)
        _, dbs = lax.fori_loop(0, nch, step, (zero, zero))
        db_ref[...] += dbs

    blk = pl.BlockSpec((S, LANES), lambda b: (b, 0))
    one = pl.BlockSpec((1, LANES), lambda b: (0, 0))
    return _pc(body, name=name, out_shape=[_sds((B * S, LANES), BF16), _sds((1, LANES), F32)], grid=(B,),
               in_specs=[blk, blk, one], out_specs=[blk, one], semantics=("arbitrary",))(dc, f_logit, b_f)


def _head_mask(lane, hh):
    return (lane < HEAD_DIM) if hh == 0 else (lane >= HEAD_DIM)


HEAD_GROUP = 3
FWD_HEAD_GROUP = 6


def _g_col_spec(rows, nblk_rows, cb, G):
    return pl.BlockSpec((rows, G * LANES), lambda b, p, i: (b * nblk_rows + i, cb // G + p))


def _g_kv_spec(rows, cb, G):
    return pl.BlockSpec((rows, G * LANES), lambda b, p, i: (b, cb // G + p))


def _g_stat_col_spec(tq, G):
    return pl.BlockSpec((1, 2 * G, tq, 1), lambda b, p, i: (b, p, i, 0))


def _g_stat_row_spec(S, G):
    return pl.BlockSpec((1, 2 * G, 1, S), lambda b, p, i: (b, p, 0, 0))


def _lanes(g):
    return slice(g * LANES, (g + 1) * LANES)


def _streams(x_ref, G, scale=None):
    rows = x_ref.shape[0]
    lane = lax.broadcasted_iota(jnp.int32, (rows, LANES), 1)
    out = []
    for g in range(G):
        x = x_ref[:, _lanes(g)]
        if scale is not None:
            x = x * jnp.asarray(scale, x.dtype)
        for hh in range(2):
            out.append(jnp.where(_head_mask(lane, hh), x, jnp.zeros_like(x)))
    return lane, out


def _wide(stat, width):
    return jnp.tile(stat, (1, width // LANES))


def _fold_lanes(v):
    out = v[:, :LANES]
    for j in range(1, v.shape[1] // LANES):
        out = out + v[:, j * LANES:(j + 1) * LANES]
    return out


def _kv_blocks(ref, ks, tk, G):
    return [ref[pl.ds(ks, tk), _lanes(g)] for g in range(G)]


def _sweep(i, block):
    def step(kb, c):
        block(kb, False)
        return c
    lax.fori_loop(0, i, step, 0)
    block(i, True)


def _fox_fwd_g(qa, ka, va, cr, *, name, B, S, P, q_cb, k_cb, v_cb, G=HEAD_GROUP):
    tq = tk = min(ATT_TILE, S)
    nq = S // tq
    NS = 2 * G

    def body(q_ref, k_ref, v_ref, cr_ref, o_ref, lse_ref, acc_ref, m_ref, l_ref):
        i = pl.program_id(2)
        lane, qh = _streams(q_ref, G, SCALE)
        on_or_below = (lax.broadcasted_iota(jnp.int32, (tq, tk), 1) <= lax.broadcasted_iota(jnp.int32, (tq, tk), 0))
        m_ref[...] = jnp.full(m_ref.shape, NEG, F32)
        l_ref[...] = jnp.zeros(l_ref.shape, F32)
        acc_ref[...] = jnp.zeros(acc_ref.shape, F32)

        def block(kb, diag):
            ks = pl.multiple_of(kb * tk, tk)
            kblk = _kv_blocks(k_ref, ks, tk, G)
            vblk = _kv_blocks(v_ref, ks, tk, G)
            ss = [lax.dot_general(qh[st], kblk[st // 2], NT, preferred_element_type=F32) for st in range(NS)]
            ps = []
            for st in range(NS):
                s = ss[st] - cr_ref[0, st, :, pl.ds(ks, tk)]
                if diag:
                    s = jnp.where(on_or_below, s, NEG)
                m = m_ref[st]
                m_new = jnp.maximum(m, jnp.max(s, axis=-1, keepdims=True))
                alpha = jnp.exp(m - m_new)
                p = jnp.exp(s - _wide(m_new, tk))
                m_ref[st] = m_new
                l_ref[st] = alpha * l_ref[st] + _fold_lanes(p)
                ps.append((alpha, p.astype(BF16)))
            pvs = [jnp.dot(ps[st][1], vblk[st // 2], preferred_element_type=F32) for st in range(NS)]
            for st in range(NS):
                acc_ref[st] = ps[st][0] * acc_ref[st] + pvs[st]

        _sweep(i, block)
        ls = [jnp.sum(l_ref[st], axis=-1, keepdims=True) for st in range(NS)]
        for st in range(NS):
            lse_ref[0, st] = jnp.max(m_ref[st], axis=-1, keepdims=True) + jnp.log(ls[st])
        for g in range(G):
            o_ref[:, _lanes(g)] = jnp.where(lane < HEAD_DIM, acc_ref[2 * g] / ls[2 * g],
                                            acc_ref[2 * g + 1] / ls[2 * g + 1]).astype(BF16)

    return _pc(body, name=name, out_shape=[_sds((B * S, P * LANES), BF16), _sds((B, 2 * P, S, 1), F32)],
               grid=(B, P // G, nq),
               in_specs=[_g_col_spec(tq, nq, q_cb, G), _g_kv_spec(S, k_cb, G), _g_kv_spec(S, v_cb, G),
                         _g_stat_row_spec(S, G)],
               out_specs=[_g_col_spec(tq, nq, 0, G), _g_stat_col_spec(tq, G)],
               scratch_shapes=[pltpu.VMEM((NS, tq, LANES), F32)] * 3,
               semantics=("arbitrary", "arbitrary", "arbitrary"), vmem=VMEM_BIG)(qa, ka, va, cr)


def _fox_bwd_g(qa, ka, va, doa, lse, cr, *, name, B, S, P, q_cb, k_cb, v_cb, do_cb, G=HEAD_GROUP):
    tq = tk = min(ATT_TILE, S)
    nq = S // tq
    NS = 2 * G

    def body(q_ref, k_ref, v_ref, do_ref, lse_ref, cr_ref, dq_ref, dk_ref, dv_ref, dcs_ref, dqa_ref, delta_ref, lse_s,
             p_buf, dp_buf):
        i = pl.program_id(2)

        @pl.when(i == 0)
        def _():
            dk_ref[...] = jnp.zeros_like(dk_ref)
            dv_ref[...] = jnp.zeros_like(dv_ref)
            dcs_ref[...] = jnp.zeros_like(dcs_ref)

        lane, qh = _streams(q_ref, G, SCALE)
        _, doh = _streams(do_ref, G)
        on_or_below = (lax.broadcasted_iota(jnp.int32, (tq, tk), 1) <= lax.broadcasted_iota(jnp.int32, (tq, tk), 0))
        delta_ref[...] = jnp.zeros(delta_ref.shape, F32)
        dqa_ref[...] = jnp.zeros(dqa_ref.shape, F32)
        for st in range(NS):
            lse_s[st] = jnp.broadcast_to(lse_ref[0, st], (tq, LANES))

        def probs(kb, diag):
            ks = pl.multiple_of(kb * tk, tk)
            kblk = _kv_blocks(k_ref, ks, tk, G)
            vblk = _kv_blocks(v_ref, ks, tk, G)
            ss = [lax.dot_general(qh[st], kblk[st // 2], NT, preferred_element_type=F32) for st in range(NS)]
            dps = [lax.dot_general(doh[st], vblk[st // 2], NT, preferred_element_type=F32) for st in range(NS)]
            ps = []
            for st in range(NS):
                s = ss[st] - cr_ref[0, st, :, pl.ds(ks, tk)]
                if diag:
                    s = jnp.where(on_or_below, s, NEG)
                ps.append(jnp.exp(s - _wide(lse_s[st], tk)))
            return ks, kblk, ps, dps

        def delta_block(kb, diag):
            _, _, ps, dps = probs(kb, diag)
            for st in range(NS):
                delta_ref[st] += _fold_lanes(ps[st] * dps[st])
                p_buf[st, kb] = ps[st]
                dp_buf[st, kb] = dps[st]

        _sweep(i, delta_block)
        for st in range(NS):
            delta_ref[st] = jnp.broadcast_to(jnp.sum(delta_ref[st], axis=-1, keepdims=True), (tq, LANES))

        def grad_block(kb, diag):
            ks = pl.multiple_of(kb * tk, tk)
            kblk = _kv_blocks(k_ref, ks, tk, G)
            rows = pl.ds(ks, tk)
            dsb, pb = [], []
            for st in range(NS):
                p = p_buf[st, kb]
                ds = p * (dp_buf[st, kb] - _wide(delta_ref[st], tk))
                dcs_ref[0, st, :, rows] -= jnp.sum(ds, axis=0, keepdims=True)
                dsb.append(ds.astype(BF16))
                pb.append(p.astype(BF16))
            dks = [lax.dot_general(dsb[st], qh[st], TN, preferred_element_type=F32) for st in range(NS)]
            dvs = [lax.dot_general(pb[st], doh[st], TN, preferred_element_type=F32) for st in range(NS)]
            dqs = [jnp.dot(dsb[st], kblk[st // 2], preferred_element_type=F32) for st in range(NS)]
            for g in range(G):
                dk_ref[rows, _lanes(g)] += dks[2 * g] + dks[2 * g + 1]
                dv_ref[rows, _lanes(g)] += dvs[2 * g] + dvs[2 * g + 1]
            for st in range(NS):
                dqa_ref[st] += dqs[st]

        _sweep(i, grad_block)
        for g in range(G):
            dq_ref[:, _lanes(g)] = (jnp.where(lane < HEAD_DIM, dqa_ref[2 * g], dqa_ref[2 * g + 1]) * SCALE).astype(BF16)

    return _pc(body, name=name,
               out_shape=[_sds((B * S, P * LANES), BF16), _sds((B * S, P * LANES), F32), _sds((B * S, P * LANES), F32),
                          _sds((B, 2 * P, 1, S), F32)],
               grid=(B, P // G, nq),
               in_specs=[_g_col_spec(tq, nq, q_cb, G), _g_kv_spec(S, k_cb, G), _g_kv_spec(S, v_cb, G),
                         _g_col_spec(tq, nq, do_cb, G), _g_stat_col_spec(tq, G), _g_stat_row_spec(S, G)],
               out_specs=[_g_col_spec(tq, nq, 0, G), _g_kv_spec(S, 0, G), _g_kv_spec(S, 0, G), _g_stat_row_spec(S, G)],
               scratch_shapes=[pltpu.VMEM((NS, tq, LANES), F32)] * 3 + [pltpu.VMEM((NS, nq, tq, tk), F32)] * 2,
               semantics=("arbitrary", "arbitrary", "arbitrary"), vmem=VMEM_BIG)(qa, ka, va, doa, lse, cr)


def _sb_logs_z(z):
    nz = -z
    lm = jnp.minimum(nz, 0.0) - jnp.log(1.0 + jnp.exp(jnp.minimum(z, nz)))
    return lm + z, lm


def _sb_fwd_g(qa, ka, va, *, name, B, S, P, q_cb, k_cb, v_cb, G=HEAD_GROUP):
    tq = tk = min(ATT_TILE, S)
    nq = S // tq
    NS = 2 * G

    def body(q_ref, k_ref, v_ref, o_ref, rt_ref, acc_ref, run_ref):
        i = pl.program_id(2)
        lane, qh = _streams(q_ref, G, SCALE)
        t_r = lax.broadcasted_iota(jnp.int32, (tk, tk), 0)
        t_c = lax.broadcasted_iota(jnp.int32, (tk, tk), 1)
        after = (t_r > t_c).astype(BF16)
        below = t_c < t_r
        acc_ref[...] = jnp.zeros(acc_ref.shape, F32)
        run_ref[...] = jnp.zeros(run_ref.shape, F32)

        def block(kb, diag):
            ks = pl.multiple_of(kb * tk, tk)
            kblk = _kv_blocks(k_ref, ks, tk, G)
            vblk = _kv_blocks(v_ref, ks, tk, G)
            zs = [lax.dot_general(qh[st], kblk[st // 2], NT, preferred_element_type=F32) for st in range(NS)]
            lss, parts = [], []
            for st in range(NS):
                ls, lm = _sb_logs_z(zs[st])
                if diag:
                    lm = jnp.where(below, lm, 0.0)
                lss.append(ls + _wide(run_ref[st], tk))
                run_ref[st] += jnp.sum(lm, axis=-1, keepdims=True)
                parts.append(_split2(lm))
            sufs = [jnp.dot(parts[st][0], after, preferred_element_type=F32)
                    + jnp.dot(parts[st][1], after, preferred_element_type=F32) for st in range(NS)]
            ab = []
            for st in range(NS):
                a = jnp.exp(lss[st] + sufs[st])
                if diag:
                    a = jnp.where(below, a, 0.0)
                ab.append(a.astype(BF16))
            pvs = [jnp.dot(ab[st], vblk[st // 2], preferred_element_type=F32) for st in range(NS)]
            for st in range(NS):
                acc_ref[st] += pvs[st]

        block(i, True)

        def step(jj, c):
            block(i - 1 - jj, False)
            return c

        lax.fori_loop(0, i, step, 0)
        for st in range(NS):
            rt_ref[0, st] = jnp.max(run_ref[st], axis=-1, keepdims=True)
        for g in range(G):
            o_ref[:, _lanes(g)] = jnp.where(lane < HEAD_DIM, acc_ref[2 * g], acc_ref[2 * g + 1]).astype(BF16)

    return _pc(body, name=name, out_shape=[_sds((B * S, P * LANES), BF16), _sds((B, 2 * P, S, 1), F32)],
               grid=(B, P // G, nq),
               in_specs=[_g_col_spec(tq, nq, q_cb, G), _g_kv_spec(S, k_cb, G), _g_kv_spec(S, v_cb, G)],
               out_specs=[_g_col_spec(tq, nq, 0, G), _g_stat_col_spec(tq, G)],
               scratch_shapes=[pltpu.VMEM((NS, tq, LANES), F32)] * 2,
               semantics=("arbitrary", "arbitrary", "arbitrary"), vmem=VMEM_BIG)(qa, ka, va)


def _sb_bwd_g(qa, ka, va, doa, rt, *, name, B, S, P, q_cb, k_cb, v_cb, do_cb, G=HEAD_GROUP):
    tq = tk = min(ATT_TILE, S)
    nq = S // tq
    NS = 2 * G

    def body(q_ref, k_ref, v_ref, do_ref, rt_ref, dq_ref, dk_ref, dv_ref, dqa_ref, pl_ref, pg_ref):
        i = pl.program_id(2)

        @pl.when(i == 0)
        def _():
            dk_ref[...] = jnp.zeros_like(dk_ref)
            dv_ref[...] = jnp.zeros_like(dv_ref)

        lane, qh = _streams(q_ref, G, SCALE)
        _, doh = _streams(do_ref, G)
        t_r = lax.broadcasted_iota(jnp.int32, (tk, tk), 0)
        t_c = lax.broadcasted_iota(jnp.int32, (tk, tk), 1)
        upto = (t_r <= t_c).astype(BF16)
        before = (t_r < t_c).astype(BF16)
        below = t_c < t_r
        dqa_ref[...] = jnp.zeros(dqa_ref.shape, F32)
        pg_ref[...] = jnp.zeros(pg_ref.shape, F32)
        for st in range(NS):
            pl_ref[st] = jnp.broadcast_to(rt_ref[0, st], (tq, LANES))

        def block(kb, diag):
            ks = pl.multiple_of(kb * tk, tk)
            rows = pl.ds(ks, tk)
            kblk = _kv_blocks(k_ref, ks, tk, G)
            vblk = _kv_blocks(v_ref, ks, tk, G)
            zs = [lax.dot_general(qh[st], kblk[st // 2], NT, preferred_element_type=F32) for st in range(NS)]
            das = [lax.dot_general(doh[st], vblk[st // 2], NT, preferred_element_type=F32) for st in range(NS)]
            lss, parts = [], []
            for st in range(NS):
                ls, lm = _sb_logs_z(zs[st])
                if diag:
                    lm = jnp.where(below, lm, 0.0)
                lss.append((ls, ls + _wide(pl_ref[st], tk)))
                pl_ref[st] -= jnp.sum(lm, axis=-1, keepdims=True)
                parts.append(_split2(lm))
            pins = [jnp.dot(parts[st][0], upto, preferred_element_type=F32)
                    + jnp.dot(parts[st][1], upto, preferred_element_type=F32) for st in range(NS)]
            gms, ab, gparts = [], [], []
            for st in range(NS):
                a = jnp.exp(lss[st][1] - pins[st])
                if diag:
                    a = jnp.where(below, a, 0.0)
                gm = a * das[st]
                gms.append(gm)
                ab.append(a.astype(BF16))
                gparts.append(gm.astype(BF16))
            pgs = [jnp.dot(gparts[st], before, preferred_element_type=F32) for st in range(NS)]
            dzb = []
            for st in range(NS):
                gm = gms[st]
                dz = gm - jnp.exp(lss[st][0]) * (gm + (pgs[st] + _wide(pg_ref[st], tk)))
                if diag:
                    dz = jnp.where(below, dz, 0.0)
                pg_ref[st] += jnp.sum(gm, axis=-1, keepdims=True)
                dzb.append(dz.astype(BF16))
            dks = [lax.dot_general(dzb[st], qh[st], TN, preferred_element_type=F32) for st in range(NS)]
            dvs = [lax.dot_general(ab[st], doh[st], TN, preferred_element_type=F32) for st in range(NS)]
            dqs = [jnp.dot(dzb[st], kblk[st // 2], preferred_element_type=F32) for st in range(NS)]
            for g in range(G):
                dk_ref[rows, _lanes(g)] += dks[2 * g] + dks[2 * g + 1]
                dv_ref[rows, _lanes(g)] += dvs[2 * g] + dvs[2 * g + 1]
            for st in range(NS):
                dqa_ref[st] += dqs[st]

        _sweep(i, block)
        for g in range(G):
            dq_ref[:, _lanes(g)] = (jnp.where(lane < HEAD_DIM, dqa_ref[2 * g], dqa_ref[2 * g + 1]) * SCALE).astype(BF16)

    return _pc(body, name=name,
               out_shape=[_sds((B * S, P * LANES), BF16), _sds((B * S, P * LANES), F32), _sds((B * S, P * LANES), F32)],
               grid=(B, P // G, nq),
               in_specs=[_g_col_spec(tq, nq, q_cb, G), _g_kv_spec(S, k_cb, G), _g_kv_spec(S, v_cb, G),
                         _g_col_spec(tq, nq, do_cb, G), _g_stat_col_spec(tq, G)],
               out_specs=[_g_col_spec(tq, nq, 0, G), _g_kv_spec(S, 0, G), _g_kv_spec(S, 0, G)],
               scratch_shapes=[pltpu.VMEM((NS, tq, LANES), F32)] * 3,
               semantics=("arbitrary", "arbitrary", "arbitrary"), vmem=VMEM_BIG)(qa, ka, va, doa, rt)


MEM_GROUP = N_MEM_HEADS // 2


def _mem_fwd(qa, kva, *, name, B, S, NM, q_cb):
    G = MEM_GROUP
    NS = 2 * G
    tq = min(MEM_Q_TILE, S)
    nq = S // tq

    def body(q_ref, k_ref, v_ref, o_ref, lse_ref):
        lane, qh = _streams(q_ref, G, SCALE)
        kblk = [k_ref[:, _lanes(g)] for g in range(G)]
        vblk = [v_ref[:, _lanes(g)] for g in range(G)]
        ss = [lax.dot_general(qh[st], kblk[st // 2], NT, preferred_element_type=F32) for st in range(NS)]
        pb, ls = [], []
        for st in range(NS):
            m = jnp.max(ss[st], axis=-1, keepdims=True)
            p = jnp.exp(ss[st] - m)
            l = jnp.sum(p, axis=-1, keepdims=True)
            lse_ref[0, st] = m + jnp.log(l)
            pb.append(p.astype(BF16))
            ls.append(l)
        pvs = [jnp.dot(pb[st], vblk[st // 2], preferred_element_type=F32) for st in range(NS)]
        for g in range(G):
            o_ref[:, _lanes(g)] = jnp.where(lane < HEAD_DIM, pvs[2 * g] / ls[2 * g],
                                            pvs[2 * g + 1] / ls[2 * g + 1]).astype(BF16)

    return _pc(body, name=name, out_shape=[_sds((B * S, G * LANES), BF16), _sds((B, NS, S, 1), F32)],
               grid=(B, 1, nq),
               in_specs=[_g_col_spec(tq, nq, q_cb, G), _g_kv_spec(NM, 0, G), _g_kv_spec(NM, G, G)],
               out_specs=[_g_col_spec(tq, nq, 0, G), _g_stat_col_spec(tq, G)],
               semantics=("arbitrary", "arbitrary", "arbitrary"), vmem=VMEM_BIG)(qa, kva, kva)


def _mem_bwd(qa, kva, doa, oa, lse, *, name, B, S, NM, q_cb, do_cb):
    G = MEM_GROUP
    NS = 2 * G
    tq = min(MEM_Q_TILE, S)
    nq = S // tq

    def body(q_ref, k_ref, v_ref, do_ref, o_ref, lse_ref, dq_ref, dk_ref, dv_ref):
        @pl.when(pl.program_id(2) == 0)
        def _():
            dk_ref[...] = jnp.zeros_like(dk_ref)
            dv_ref[...] = jnp.zeros_like(dv_ref)

        lane, qh = _streams(q_ref, G, SCALE)
        _, doh = _streams(do_ref, G)
        kblk = [k_ref[:, _lanes(g)] for g in range(G)]
        vblk = [v_ref[:, _lanes(g)] for g in range(G)]
        prod = [do_ref[:, _lanes(g)].astype(F32) * o_ref[:, _lanes(g)].astype(F32) for g in range(G)]
        ss = [lax.dot_general(qh[st], kblk[st // 2], NT, preferred_element_type=F32) for st in range(NS)]
        dps = [lax.dot_general(doh[st], vblk[st // 2], NT, preferred_element_type=F32) for st in range(NS)]
        dsb, pb = [], []
        for st in range(NS):
            delta = jnp.sum(jnp.where(_head_mask(lane, st % 2), prod[st // 2], 0.0), axis=-1, keepdims=True)
            p = jnp.exp(ss[st] - lse_ref[0, st])
            dsb.append((p * (dps[st] - delta)).astype(BF16))
            pb.append(p.astype(BF16))
        dks = [lax.dot_general(dsb[st], qh[st], TN, preferred_element_type=F32) for st in range(NS)]
        dvs = [lax.dot_general(pb[st], doh[st], TN, preferred_element_type=F32) for st in range(NS)]
        dqs = [jnp.dot(dsb[st], kblk[st // 2], preferred_element_type=F32) for st in range(NS)]
        for g in range(G):
            dk_ref[:, _lanes(g)] += dks[2 * g] + dks[2 * g + 1]
            dv_ref[:, _lanes(g)] += dvs[2 * g] + dvs[2 * g + 1]
            dq_ref[:, _lanes(g)] = (jnp.where(lane < HEAD_DIM, dqs[2 * g], dqs[2 * g + 1]) * SCALE).astype(BF16)

    return _pc(body, name=name,
               out_shape=[_sds((B * S, G * LANES), BF16), _sds((B * NM, G * LANES), F32), _sds((B * NM, G * LANES), F32)],
               grid=(B, 1, nq),
               in_specs=[_g_col_spec(tq, nq, q_cb, G), _g_kv_spec(NM, 0, G), _g_kv_spec(NM, G, G),
                         _g_col_spec(tq, nq, do_cb, G), _g_col_spec(tq, nq, 0, G), _g_stat_col_spec(tq, G)],
               out_specs=[_g_col_spec(tq, nq, 0, G), _g_kv_spec(NM, 0, G), _g_kv_spec(NM, 0, G)],
               semantics=("arbitrary", "arbitrary", "arbitrary"), vmem=VMEM_BIG)(qa, kva, kva, doa, oa, lse)


def _sigmoid(v):
    return 0.5 * jnp.tanh(0.5 * v) + 0.5


def _shift_rows(cur, halo_ref, first, rows_idx, k):
    out = pltpu.roll(cur, k, 0)
    top = out[0:8, :]
    for r in range(k):
        hr = halo_ref.shape[0] - k + r
        edge = jnp.where(first, 0.0, halo_ref[hr:hr + 1, :])
        top = jnp.where(rows_idx[0:8, :] == r, edge, top)
    return jnp.concatenate([top, out[8:, :]], axis=0)


def _shift_rows_up(cur, halo_ref, last, rows_idx, k, ts):
    out = pltpu.roll(cur, ts - k, 0)
    bottom = out[ts - 8:, :]
    for r in range(k):
        edge = jnp.where(last, 0.0, halo_ref[r:r + 1, :])
        bottom = jnp.where(rows_idx[0:8, :] == 8 - k + r, edge, bottom)
    return jnp.concatenate([out[:ts - 8, :], bottom], axis=0)


def _ffn_up_gate(x, g, w, cw, cb, *, name, S):
    T, D = x.shape
    F = w.shape[1] // 2
    tm = min(1024, S)
    tn = 256
    nj = F // tn
    tiles_per_seq = S // tm
    halo = 16

    def body(x_ref, xh_ref, g_ref, wg_ref, wv_ref, cwg_ref, cwv_ref, cbg_ref, cbv_ref,
             uc_ref, ub_ref, a_ref, hout_ref, h_ref, hh_ref, eg_ref, ev_ref):
        first = lax.rem(pl.program_id(0), tiles_per_seq) == 0

        @pl.when(pl.program_id(1) == 0)
        def _():
            def norm(v):
                r = lax.rsqrt(jnp.mean(v * v, axis=-1, keepdims=True) + EPS)
                return ((v * r) * g_ref[...]).astype(BF16)
            h = norm(x_ref[...])
            h_ref[...] = h
            hout_ref[...] = h
            hh_ref[...] = norm(xh_ref[...])

        h = h_ref[...]
        rows_idx = lax.broadcasted_iota(jnp.int32, (tm, tn), 0)
        uc = []
        for half, (w_ref, cw_ref, cb_ref, e_ref) in enumerate(((wg_ref, cwg_ref, cbg_ref, eg_ref),
                                                               (wv_ref, cwv_ref, cbv_ref, ev_ref))):
            acc = jnp.dot(h, w_ref[...], preferred_element_type=F32)
            e_ref[...] = jnp.dot(hh_ref[...], w_ref[...], preferred_element_type=F32)
            ub_ref[half] = acc.astype(BF16)
            m1 = _shift_rows(acc, e_ref, first, rows_idx, 1)
            m2 = _shift_rows(acc, e_ref, first, rows_idx, 2)
            uc.append(cb_ref[...] + cw_ref[0:1, :] * m2 + cw_ref[1:2, :] * m1 + cw_ref[2:3, :] * acc)
            uc_ref[half] = uc[half]
        a_ref[...] = (uc[0] * _sigmoid(uc[0]) * uc[1]).astype(BF16)

    in_specs = [pl.BlockSpec((tm, D), lambda i, j: (i, 0)),
                pl.BlockSpec((halo, D), lambda i, j: (jnp.maximum(i * (tm // halo) - 1, 0), 0)),
                pl.BlockSpec((1, D), lambda i, j: (0, 0)),
                pl.BlockSpec((D, tn), lambda i, j: (0, j)), pl.BlockSpec((D, tn), lambda i, j: (0, j + nj)),
                pl.BlockSpec((3, tn), lambda i, j: (0, j)), pl.BlockSpec((3, tn), lambda i, j: (0, j + nj)),
                pl.BlockSpec((1, tn), lambda i, j: (0, j)), pl.BlockSpec((1, tn), lambda i, j: (0, j + nj))]
    return _pc(body, name=name,
               out_shape=[_sds((2, T, F), F32), _sds((2, T, F), BF16), _sds((T, F), BF16), _sds((T, D), BF16)],
               grid=(T // tm, nj), in_specs=in_specs,
               out_specs=[pl.BlockSpec((2, tm, tn), lambda i, j: (0, i, j)), pl.BlockSpec((2, tm, tn), lambda i, j: (0, i, j)),
                          pl.BlockSpec((tm, tn), lambda i, j: (i, j)), pl.BlockSpec((tm, D), lambda i, j: (i, 0))],
               scratch_shapes=[pltpu.VMEM((tm, D), BF16), pltpu.VMEM((halo, D), BF16),
                               pltpu.VMEM((halo, tn), F32), pltpu.VMEM((halo, tn), F32)],
               semantics=("arbitrary", "arbitrary"), vmem=VMEM_BIG)(x, x, g.reshape(1, D), w, w, cw, cw, cb, cb)


def _conv_gate_bwd(da, uc, ub, cw, *, name, B, S):
    F = uc.shape[2]
    tf = F // 2
    ts = min(256, S)
    ns, nf = S // ts, F // tf

    def body(da_ref, uc_ref, ub_ref, wg_ref, wv_ref, dug_ref, duv_ref, pg_ref, pv_ref, nxt_g, nxt_v):
        last = pl.program_id(2) == 0

        @pl.when(jnp.logical_and(pl.program_id(1) == 0, last))
        def _():
            pg_ref[...] = jnp.zeros_like(pg_ref)
            pv_ref[...] = jnp.zeros_like(pv_ref)

        rows_idx = lax.broadcasted_iota(jnp.int32, (ts, tf), 0)
        ucg, ucv = uc_ref[0], uc_ref[1]
        sg = _sigmoid(ucg)
        dav = da_ref[...]
        d_v = dav * (ucg * sg)
        d_g = dav * ucv * (sg * (1.0 + ucg * (1.0 - sg)))
        for half, (o_ref, p_ref, d, w_ref, nxt) in enumerate(((dug_ref, pg_ref, d_g, wg_ref, nxt_g),
                                                               (duv_ref, pv_ref, d_v, wv_ref, nxt_v))):
            p1 = _shift_rows_up(d, nxt, last, rows_idx, 1, ts)
            p2 = _shift_rows_up(d, nxt, last, rows_idx, 2, ts)
            o_ref[...] = (w_ref[2:3, :] * d + w_ref[1:2, :] * p1 + w_ref[0:1, :] * p2).astype(BF16)
            nxt[...] = d[0:8, :]
            uh = ub_ref[half].astype(F32)
            for k, dk in enumerate((p2, p1, d)):
                p_ref[k:k + 1, :] += jnp.sum(dk * uh, axis=0, keepdims=True)
            p_ref[3:4, :] += jnp.sum(d, axis=0, keepdims=True)

    row = pl.BlockSpec((ts, tf), lambda j, b, r: (b * ns + ns - 1 - r, j))
    both = pl.BlockSpec((2, ts, tf), lambda j, b, r: (0, b * ns + ns - 1 - r, j))
    par = pl.BlockSpec((8, tf), lambda j, b, r: (0, j))
    return _pc(body, name=name,
               out_shape=[_sds((B * S, F), BF16), _sds((B * S, F), BF16), _sds((8, F), F32), _sds((8, F), F32)],
               grid=(nf, B, ns),
               in_specs=[row, both, both, pl.BlockSpec((3, tf), lambda j, b, r: (0, j)),
                         pl.BlockSpec((3, tf), lambda j, b, r: (0, j + nf))],
               out_specs=[row, row, par, par],
               scratch_shapes=[pltpu.VMEM((8, tf), F32), pltpu.VMEM((8, tf), F32)],
               semantics=("arbitrary", "arbitrary", "arbitrary"), vmem=VMEM_BIG)(da, uc, ub, cw, cw)


def _adamw(w, g, m, v, *, name):
    rows, cols = w.shape
    tr = rows
    while tr * cols * 4 > 2 ** 20 and tr % 16 == 0:
        tr //= 2

    def body(w_ref, g_ref, m_ref, v_ref, d_ref, nm_ref, nv_ref):
        gv = g_ref[...]
        m_new = ADAM_B1 * m_ref[...] + (1.0 - ADAM_B1) * gv
        v_new = ADAM_B2 * v_ref[...] + (1.0 - ADAM_B2) * (gv * gv)
        m_hat = m_new / (1.0 - ADAM_B1 ** ADAM_STEP)
        v_hat = v_new / (1.0 - ADAM_B2 ** ADAM_STEP)
        d_ref[...] = -ADAM_LR * (m_hat / (jnp.sqrt(v_hat) + ADAM_EPS) + ADAM_WD * w_ref[...])
        nm_ref[...] = m_new
        nv_ref[...] = v_new

    blk = pl.BlockSpec((tr, cols), lambda i: (i, 0))
    return _pc(body, name=name, out_shape=[_sds((rows, cols), F32)] * 3, grid=(rows // tr,),
               in_specs=[blk] * 4, out_specs=[blk] * 3, semantics=("arbitrary",))(w, g, m, v)


def _my_pos():
    return lax.axis_index("x"), lax.axis_index("y"), lax.axis_index("c")


_HBM = pl.BlockSpec(memory_space=pltpu.HBM)
_SEM = pl.BlockSpec(memory_space=pltpu.SEMAPHORE)
_EFFECT = pltpu.SideEffectType.DATAFLOW_SIDE_EFFECTING


def _peers():
    x, y, c = _my_pos()
    out = []
    for k in range(1, N_DEV):
        px, py, pc = x ^ ((k >> 2) & 1), y ^ ((k >> 1) & 1), c ^ (k & 1)
        out.append(((px, py, pc), 4 * px + 2 * py + pc))
    return out


def _scatter_start(srcs, slot_of, *, name, order_after=None):
    n = len(srcs)
    lands = [lax.empty((N_DEV,) + slot_of(s, 0, shape_only=True), s.dtype) for s in srcs]
    extra = [] if order_after is None else [order_after]

    def body(*refs):
        src_refs, land_refs = refs[:n], refs[n:2 * n]
        send_sems, recv_sems = refs[2 * n + len(extra)], refs[2 * n + len(extra) + 1]
        token = refs[-1]
        x, y, c = _my_pos()
        me = 4 * x + 2 * y + c
        for a in range(n):
            for k, (peer, peer_idx) in enumerate(_peers()):
                pltpu.make_async_remote_copy(
                    src_ref=slot_of(src_refs[a], peer_idx), dst_ref=land_refs[a].at[me],
                    send_sem=send_sems.at[a * 7 + k], recv_sem=recv_sems.at[a * 7 + k],
                    device_id=peer, device_id_type=MESH).start()
        token[...] = jnp.zeros_like(token)

    hbm = lambda a: pltpu.HBM(a.shape, a.dtype)
    args = [pltpu.with_memory_space_constraint(a, pltpu.HBM) for a in list(srcs) + lands] + extra
    outs = pl.pallas_call(
        body, name=name,
        out_shape=(pltpu.SemaphoreType.DMA((7 * n,)), pltpu.SemaphoreType.DMA((7 * n,)),
                   *[hbm(a) for a in srcs], *[hbm(a) for a in lands], _sds((8, LANES), F32)),
        in_specs=[_HBM] * (2 * n) + [pl.BlockSpec(memory_space=pl.ANY)] * len(extra),
        out_specs=(_SEM, _SEM, *([_HBM] * (2 * n)), pl.BlockSpec(memory_space=pltpu.VMEM)),
        input_output_aliases={a: 2 + a for a in range(2 * n)},
        compiler_params=pltpu.CompilerParams(has_side_effects=_EFFECT))(*args)
    return outs[0], outs[1], list(outs[2:2 + n]), list(outs[2 + n:2 + 2 * n]), outs[-1]


def _scatter_wait(send_sems, recv_sems, srcs, lands, slot_of, after, *, name):
    n = len(srcs)

    def body(*refs):
        src_refs, land_refs = refs[:n], refs[n:2 * n]
        ssem, rsem = refs[2 * n], refs[2 * n + 1]
        x, y, c = _my_pos()
        me = 4 * x + 2 * y + c
        for a in range(n):
            for k, (peer, peer_idx) in enumerate(_peers()):
                cp = pltpu.make_async_remote_copy(
                    src_ref=slot_of(src_refs[a], peer_idx), dst_ref=land_refs[a].at[me],
                    send_sem=ssem.at[a * 7 + k], recv_sem=rsem.at[a * 7 + k],
                    device_id=peer, device_id_type=MESH)
                cp.wait_send()
                cp.wait_recv()

    hbm = lambda a: pltpu.HBM(a.shape, a.dtype)
    outs = pl.pallas_call(
        body, name=name, out_shape=tuple(hbm(a) for a in list(srcs) + list(lands)),
        in_specs=[_HBM] * (2 * n) + [_SEM, _SEM, pl.BlockSpec(memory_space=pl.ANY)],
        out_specs=tuple([_HBM] * (2 * n)), input_output_aliases={a: a for a in range(2 * n)},
        compiler_params=pltpu.CompilerParams(has_side_effects=_EFFECT))(*srcs, *lands, send_sems, recv_sems, after)
    return list(outs[:n]), list(outs[n:])


def _whole(a, peer_idx, shape_only=False):
    return a.shape if shape_only else a


def _slot(a, peer_idx, shape_only=False):
    return a.shape[1:] if shape_only else a.at[peer_idx]


def _sum_slots(a, *, name, tr=None):
    rows, cols = a.shape[1], a.shape[2]
    if tr is None:
        tr = rows
        while N_DEV * tr * cols * a.dtype.itemsize > 3 * 2 ** 20 and tr % 32 == 0:
            tr //= 2

    def body(a_ref, o_ref):
        acc = a_ref[0].astype(F32)
        for j in range(1, N_DEV):
            acc = acc + a_ref[j].astype(F32)
        o_ref[...] = acc

    return _pc(body, name=name, out_shape=_sds((rows, cols), F32), grid=(rows // tr,),
               in_specs=[pl.BlockSpec((N_DEV, tr, cols), lambda i: (0, i, 0))],
               out_specs=pl.BlockSpec((tr, cols), lambda i: (i, 0)), semantics=("arbitrary",), vmem=VMEM_BIG)(a)


def _to_slots(full, kind):
    if kind == "rows2":
        r, c = full.shape
        return full.reshape(N_DEV, r // N_DEV, c)
    if kind == "cols2":
        r, c = full.shape
        return full.reshape(r, N_DEV, c // N_DEV).transpose(1, 0, 2)
    if kind == "rows3":
        l, r, c = full.shape
        return full.reshape(l, N_DEV, r // N_DEV, c).transpose(1, 0, 2, 3)
    if kind == "cols3":
        l, r, c = full.shape
        return full.reshape(l, r, N_DEV, c // N_DEV).transpose(2, 0, 1, 3)
    raise ValueError(kind)


def _from_slots(slots, kind):
    if kind == "rows2":
        _, r, c = slots.shape
        return slots.reshape(N_DEV * r, c)
    if kind == "cols2":
        _, r, c = slots.shape
        return slots.transpose(1, 0, 2).reshape(r, N_DEV * c)
    if kind == "rows3":
        _, l, r, c = slots.shape
        return slots.transpose(1, 0, 2, 3).reshape(l, N_DEV * r, c)
    if kind == "cols3":
        _, l, r, c = slots.shape
        return slots.transpose(1, 2, 0, 3).reshape(l, r, N_DEV * c)
    raise ValueError(kind)


BIG = (("w_in_a", "rows2"), ("w_in_b", "rows2"), ("w_kv", "cols2"), ("w_memkv", "rows3"),
       ("w_out", "rows3"), ("w_up", "cols3"), ("w_down", "rows3"))


def _round_up(n, m):
    return -(-n // m) * m


def _pad_rows(a, rows, axis):
    pad = [(0, 0)] * a.ndim
    pad[axis] = (0, rows - a.shape[axis])
    return jnp.pad(a, pad)


def kernel(x, mem, ln_mix_g, w_in_a, b_f_a, w_in_b, ln_kv_g, w_kv, ln_mem_g, w_memkv, w_out, ln_ffn_g, w_up, conv_w, conv_b, w_down, final_g, loss_target, m_ln_mix_g, m_w_in_a, m_b_f_a, m_w_in_b, m_ln_kv_g, m_w_kv, m_ln_mem_g, m_w_memkv, m_w_out, m_ln_ffn_g, m_w_up, m_conv_w, m_conv_b, m_w_down, m_final_g, v_ln_mix_g, v_w_in_a, v_b_f_a, v_w_in_b, v_ln_kv_g, v_w_kv, v_ln_mem_g, v_w_memkv, v_w_out, v_ln_ffn_g, v_w_up, v_conv_w, v_conv_b, v_w_down, v_final_g):
    B, S, D = x.shape
    NM = mem.shape[1]
    T = B * S
    F = w_down.shape[1] * N_DEV
    my_idx = 4 * lax.axis_index("x") + 2 * lax.axis_index("y") + lax.axis_index("c")

    shards = {"w_in_a": w_in_a[0], "w_in_b": w_in_b[0], "w_kv": w_kv, "w_memkv": w_memkv, "w_out": w_out,
              "w_up": w_up, "w_down": w_down}
    moms = {"w_in_a": (m_w_in_a[0], v_w_in_a[0]), "w_in_b": (m_w_in_b[0], v_w_in_b[0]), "w_kv": (m_w_kv, v_w_kv),
            "w_memkv": (m_w_memkv, v_w_memkv), "w_out": (m_w_out, v_w_out), "w_up": (m_w_up, v_w_up),
            "w_down": (m_w_down, v_w_down)}

    groups = [("a1", [("w_in_a", None)]),
              ("a2", [("w_memkv", None), ("w_out", None), ("conv_w", None)]),
              ("b0", [("w_up", 0), ("w_down", 0)]), ("a3", [("w_in_b", None), ("w_kv", None)]),
              ("b1", [("w_up", 1), ("w_down", 1)])]
    sources = dict(shards, conv_w=conv_w)
    started, token = {}, None
    for gname, members in groups:
        srcs = []
        for n, layer in members:
            a = sources[n] if layer is None else sources[n][layer]
            srcs.append(a if n == "conv_w" else a.astype(BF16))
        ssem, rsem, thru, lands, token = _scatter_start(srcs, _whole, name=f"gather_start_{gname}", order_after=token)
        started[gname] = (ssem, rsem, thru, lands)

    def gathered(gname, after):
        ssem, rsem, thru, lands = started[gname]
        thru, lands = _scatter_wait(ssem, rsem, thru, lands, _whole, after, name=f"gather_wait_{gname}")
        return [lax.dynamic_update_index_in_dim(land, s, my_idx, 0) for land, s in zip(lands, thru)]

    full = {}
    (g_wa,) = gathered("a1", token)
    full["w_in_a"] = _from_slots(g_wa, "rows2")

    wa = full["w_in_a"]
    n_qkv = 3 * MAIN_W
    wa = jnp.concatenate([wa[:, :n_qkv], wa[:, n_qkv + N_MAIN_HEADS:], wa[:, n_qkv:n_qkv + N_MAIN_HEADS],
                          jnp.zeros((D, LANES - N_MAIN_HEADS), BF16)], axis=1)
    n_main = n_qkv + MEM_W
    full["w_up"], full["w_down"] = {}, {}
    b_f =_pad_rows(b_f_a.reshape(1, N_MAIN_HEADS), LANES, 1)

    x2d = x.reshape(T, D)
    mem2d = mem.reshape(B * NM, D)
    tgt2d = loss_target.reshape(T, D)
    PM, PX = N_MAIN_HEADS // 2, N_MEM_HEADS // 2

    def stats_to_heads(c2d):
        c = c2d.reshape(B, S, LANES)[:, :, :N_MAIN_HEADS].transpose(0, 2, 1)
        return c[:, :, None, :]

    def mem_kv(layer):
        return _mm_fwd(mem2d, full["w_memkv"][layer], name=f"memkv{layer}", tm=B * NM, tn=2 * MEM_W,
                       out_dtype=BF16, g=ln_mem_g[layer], save_h=True)

    def conv_ffn_fwd(xin, layer):
        uc, ub, a, h = _ffn_up_gate(xin, ln_ffn_g[layer], full["w_up"][layer], conv_w_full[layer],
                                    conv_b[layer].reshape(1, 2 * F), name=f"ffn_up{layer}", S=S)
        xo = _mm_fwd(a, full["w_down"][layer], name=f"ffn_down{layer}", tm=min(1024, T), tn=1024, out_dtype=F32, res=xin)
        return xo, (uc, ub, h, a)

    proj_a, h_mix0 = _mm_fwd(x2d, wa, name="in_proj_a", tm=min(1024, T), tn=1280, out_dtype=BF16, g=ln_mix_g[0],
                             ncols=n_main, save_h=True)
    f_logit = _mm_fwd(x2d, wa, name="in_proj_f", tm=min(1024, T), tn=LANES, out_dtype=F32, g=ln_mix_g[0],
                      col0=n_main // LANES, ncols=LANES)
    c2d = _forget_cumsum(f_logit, b_f, B=B, S=S, name="forget_cumsum")
    cr = stats_to_heads(c2d)
    o_main0, lse0 = _fox_fwd_g(proj_a, proj_a, proj_a, cr, name="fox_fwd", B=B, S=S, P=PM, q_cb=0, k_cb=PM, v_cb=2 * PM,
                               G=FWD_HEAD_GROUP)
    g_wmem, g_wout, g_cw = gathered("a2", lse0)
    full["w_memkv"] = _from_slots(g_wmem, "rows3")
    full["w_out"] = _from_slots(g_wout, "rows3")
    conv_w_full = _from_slots(g_cw, "cols3")
    memkv0, h_mem0 = mem_kv(0)
    o_mem0, lse_m0 = _mem_fwd(proj_a, memkv0, name="mem_fwd0", B=B, S=S, NM=NM, q_cb=3 * PM)
    o_cat0 = jnp.concatenate([o_main0, o_mem0], axis=1)
    x1 = _mm_fwd(o_cat0, full["w_out"][0], name="out_proj0", tm=min(1024, T), tn=1024, out_dtype=F32, res=x2d)
    g_up, g_dn = gathered("b0", x1)
    full["w_up"][0], full["w_down"][0] = _from_slots(g_up, "cols2"), _from_slots(g_dn, "rows2")
    x2, ffn_saved0 = conv_ffn_fwd(x1, 0)
    g_wb, g_wkv = gathered("a3", x2)
    wb, wkv = _from_slots(g_wb, "rows2"), _from_slots(g_wkv, "cols2")
    kv, h_kv =_mm_fwd(x2, wkv, name="kv_proj", tm=min(1024, T), tn=1536, out_dtype=BF16, g=ln_kv_g, save_h=True)
    proj_b, h_mix1 = _mm_fwd(x2, wb, name="in_proj_b", tm=min(1024, T), tn=1024, out_dtype=BF16, g=ln_mix_g[1],
                             save_h=True)
    o_main1, rt1 = _sb_fwd_g(proj_b, kv, kv, name="sb_fwd", B=B, S=S, P=PM, q_cb=0, k_cb=0, v_cb=PM,
                             G=FWD_HEAD_GROUP)
    memkv1, h_mem1 = mem_kv(1)
    o_mem1, lse_m1 = _mem_fwd(proj_b, memkv1, name="mem_fwd1", B=B, S=S, NM=NM, q_cb=PM)
    o_cat1 = jnp.concatenate([o_main1, o_mem1], axis=1)
    x3 = _mm_fwd(o_cat1, full["w_out"][1], name="out_proj1", tm=min(1024, T), tn=1024, out_dtype=F32, res=x2)
    g_up, g_dn = gathered("b1", x3)
    full["w_up"][1], full["w_down"][1] = _from_slots(g_up, "cols2"), _from_slots(g_dn, "rows2")
    x4, ffn_saved1 = conv_ffn_fwd(x3, 1)
    dx4, dg_final, loss_part = _loss_head(x4, final_g, tgt2d, name="loss_head")

    grads = {}
    small = {}
    reduce_groups = []

    def start_reduce(gname, keys, kinds):
        slots = [_to_slots(grads[k], kind) for k, kind in zip(keys, kinds)]
        ssem, rsem, thru, lands, tok = _scatter_start(slots, _slot, name=f"reduce_start_{gname}")
        reduce_groups.append((gname, keys, ssem, rsem, thru, lands))
        return tok[0, 0]

    def conv_ffn_bwd(dxo, xin, saved, layer):
        uc, ub, h, a = saved
        w_dn = full["w_down"][layer]
        da = _mm_nt(dxo, w_dn, name=f"d_act{layer}", tm=min(1024, T), tn=F // 2, out_dtype=F32)
        grads[("w_down", layer)] = _wgrad(a, dxo, f"g_w_down{layer}")
        cwl = conv_w_full[layer]
        du_g, du_v, p_g, p_v = _conv_gate_bwd(da, uc, ub, cwl, name=f"conv_bwd{layer}", B=B, S=S)
        small[("conv_w", layer)] = jnp.concatenate([p_g[0:3], p_v[0:3]], axis=1)
        small[("conv_b", layer)] = jnp.concatenate([p_g[3], p_v[3]], axis=0)
        grads[("w_up", layer)] = jnp.concatenate(
            [_wgrad(h, du_g, f"g_w_up_gate{layer}"), _wgrad(h, du_v, f"g_w_up_val{layer}")], axis=1)
        tok = start_reduce(f"ffn{layer}", [("w_down", layer), ("w_up", layer)], ["rows2", "cols2"])
        dxi, dg = _mm_nt_rmsbwd([(du_g, 0), (du_v, 1)], full["w_up"][layer], xin, ln_ffn_g[layer] + tok,
                                name=f"d_ffn_in{layer}", dres=dxo)
        small[("ln_ffn_g", layer)] = dg[0]
        return dxi

    def mem_bwd(proj, q_cb, memkv, h_mem, do_cat, o_mem, lse_m, layer):
        dqm, dmk, dmv = _mem_bwd(proj, memkv, do_cat, o_mem, lse_m, name=f"mem_bwd{layer}", B=B, S=S, NM=NM,
                                 q_cb=q_cb, do_cb=PM)
        grads[("w_memkv", layer)] = jnp.concatenate(
            [_wgrad(h_mem, dmk, f"g_w_memk{layer}"), _wgrad(h_mem, dmv, f"g_w_memv{layer}")], axis=1)
        _, dg = _mm_nt_rmsbwd([(dmk, 0), (dmv, 1)], full["w_memkv"][layer], mem2d, ln_mem_g[layer],
                              name=f"d_mem_in{layer}", want_dx=False)
        small[("ln_mem_g", layer)] = dg[0]
        return dqm

    dx3 = conv_ffn_bwd(dx4, x3, ffn_saved1, 1)
    do_cat1 = _mm_nt(dx3, full["w_out"][1], name="d_o_cat1", tm=min(1024, T), tn=1024, out_dtype=BF16)
    grads[("w_out", 1)] = _wgrad(o_cat1, dx3, "g_w_out1")
    dq1, dk1, dv1 = _sb_bwd_g(proj_b, kv, kv, do_cat1, rt1, name="sb_bwd", B=B, S=S, P=PM, q_cb=0, k_cb=0, v_cb=PM,
                            do_cb=0)
    dqm1 = mem_bwd(proj_b, PM, memkv1, h_mem1, do_cat1, o_mem1, lse_m1, 1)
    grads["w_in_b"] = jnp.concatenate([_wgrad(h_mix1, dq1, "g_w_in_b_q"), _wgrad(h_mix1, dqm1, "g_w_in_b_m")], axis=1)
    grads["w_kv"] = jnp.concatenate([_wgrad(h_kv, dk1, "g_w_kv_k"), _wgrad(h_kv, dv1, "g_w_kv_v")], axis=1)
    tok = start_reduce("mix1", [("w_out", 1), "w_in_b", "w_kv", ("w_memkv", 1)], ["rows2", "rows2", "cols2", "rows2"])
    dx2, dg = _mm_nt_rmsbwd([(dq1, 0), (dqm1, MAIN_W // MEM_W)], wb, x2, ln_mix_g[1] + tok, name="d_mix_in1", dres=dx3)
    small[("ln_mix_g", 1)] = dg[0]
    dx2, dg = _mm_nt_rmsbwd([(dk1, 0), (dv1, 1)], wkv, x2, ln_kv_g, name="d_kv_in", dres=dx2)
    small["ln_kv_g"] = dg[0]
    dx1 = conv_ffn_bwd(dx2, x1, ffn_saved0, 0)
    do_cat0 = _mm_nt(dx1, full["w_out"][0], name="d_o_cat0", tm=min(1024, T), tn=1024, out_dtype=BF16)
    grads[("w_out", 0)] = _wgrad(o_cat0, dx1, "g_w_out0")
    dq0, dk0, dv0, dcs = _fox_bwd_g(proj_a, proj_a, proj_a, do_cat0, lse0, cr, name="fox_bwd", B=B, S=S, P=PM, q_cb=0,
                                  k_cb=PM, v_cb=2 * PM, do_cb=0)
    dqm0 = mem_bwd(proj_a, 3 * PM, memkv0, h_mem0, do_cat0, o_mem0, lse_m0, 0)
    dc2d = _pad_rows(dcs[:, :, 0, :].transpose(0, 2, 1).reshape(T, N_MAIN_HEADS), LANES, 1)
    df, db_f = _forget_cumsum_bwd(dc2d, f_logit, b_f, B=B, S=S, name="forget_cumsum_bwd")
    a_parts = [(dq0, 0), (dk0, 1), (dv0, 2), (dqm0, n_qkv // MEM_W), (df, n_main // LANES)]
    g_wa = jnp.concatenate([_wgrad(h_mix0, p, f"g_w_in_a{k}") for k, (p, _) in enumerate(a_parts)], axis=1)
    grads["w_in_a"] = jnp.concatenate([g_wa[:, :n_qkv], g_wa[:, n_main:n_main + N_MAIN_HEADS], g_wa[:, n_qkv:n_main]],
                                      axis=1)
    tok = start_reduce("mix0", [("w_out", 0), ("w_memkv", 0), "w_in_a"], ["rows2", "rows2", "rows2"])
    dx0, dg = _mm_nt_rmsbwd(a_parts, wa, x2d, ln_mix_g[0] + tok, name="d_mix_in0", dres=dx1)
    small[("ln_mix_g", 0)] = dg[0]
    grad_x = dx0.reshape(B, S, D)

    def both_small(name):
        return jnp.stack([small[(name, 0)], small[(name, 1)]])

    small_list = [("ln_mix_g", both_small("ln_mix_g")), ("b_f_a", db_f[:, :N_MAIN_HEADS]), ("ln_kv_g", small["ln_kv_g"]),
                  ("ln_mem_g", both_small("ln_mem_g")), ("ln_ffn_g", both_small("ln_ffn_g")),
                  ("conv_w", both_small("conv_w")), ("conv_b", both_small("conv_b")), ("final_g", dg_final[0]),
                  ("loss", loss_part[0, :1])]
    sm_rows = []
    for _, a in small_list:
        flat = a.reshape(-1)
        sm_rows.append(_pad_rows(flat, _round_up(flat.size, 8 * LANES), 0).reshape(-1, LANES))
    spack = jnp.concatenate(sm_rows, axis=0)
    s_ssem, s_rsem, s_thru, s_lands, s_tok = _scatter_start([spack], _whole, name="small_start")

    pieces = {}
    for gname, keys, ssem, rsem, thru, lands in reduce_groups:
        thru, lands = _scatter_wait(ssem, rsem, thru, lands, _slot, s_tok, name=f"reduce_wait_{gname}")
        for key, mine, land in zip(keys, thru, lands):
            own = lax.dynamic_index_in_dim(mine, my_idx, 0, keepdims=False)
            land = lax.dynamic_update_index_in_dim(land, own, my_idx, 0)
            tag = key if isinstance(key, str) else f"{key[0]}{key[1]}"
            pieces[key] = _sum_slots(land, name=f"sum_{tag}")

    red = {}
    for n in ("w_in_a", "w_in_b", "w_kv"):
        red[n] = pieces[n].reshape(shards[n].shape)
    for n in ("w_memkv", "w_out", "w_up", "w_down"):
        red[n] = jnp.stack([pieces[(n, 0)], pieces[(n, 1)]])

    weights = {"ln_mix_g": ln_mix_g, "w_in_a": w_in_a, "b_f_a": b_f_a, "w_in_b": w_in_b, "ln_kv_g": ln_kv_g,
               "w_kv": w_kv, "ln_mem_g": ln_mem_g, "w_memkv": w_memkv, "w_out": w_out, "ln_ffn_g": ln_ffn_g,
               "w_up": w_up, "conv_w": conv_w, "conv_b": conv_b, "w_down": w_down, "final_g": final_g}
    m_in = {"ln_mix_g": m_ln_mix_g, "w_in_a": m_w_in_a, "b_f_a": m_b_f_a, "w_in_b": m_w_in_b, "ln_kv_g": m_ln_kv_g,
            "w_kv": m_w_kv, "ln_mem_g": m_ln_mem_g, "w_memkv": m_w_memkv, "w_out": m_w_out, "ln_ffn_g": m_ln_ffn_g,
            "w_up": m_w_up, "conv_w": m_conv_w, "conv_b": m_conv_b, "w_down": m_w_down, "final_g": m_final_g}
    v_in = {"ln_mix_g": v_ln_mix_g, "w_in_a": v_w_in_a, "b_f_a": v_b_f_a, "w_in_b": v_w_in_b, "ln_kv_g": v_ln_kv_g,
            "w_kv": v_w_kv, "ln_mem_g": v_ln_mem_g, "w_memkv": v_w_memkv, "w_out": v_w_out, "ln_ffn_g": v_ln_ffn_g,
            "w_up": v_w_up, "conv_w": v_conv_w, "conv_b": v_conv_b, "w_down": v_w_down, "final_g": v_final_g}
    order = list(weights)
    big_names = [n for n, _ in BIG]
    g_out, d_out, nm_out, nv_out = {}, {}, {}, {}

    def update(n):
        w = weights[n]
        cols = w.shape[-1]
        g = red[n].reshape(w.shape)
        d, nm, nv = _adamw(w.reshape(-1, cols), g.reshape(-1, cols), m_in[n].reshape(-1, cols),
                           v_in[n].reshape(-1, cols), name=f"adamw_{n}")
        g_out[n], d_out[n], nm_out[n], nv_out[n] = g, d.reshape(w.shape), nm.reshape(w.shape), nv.reshape(w.shape)

    for n in big_names:
        update(n)
    all_updated = jnp.stack([d_out[n].reshape(-1)[0] for n in big_names])
    s_thru, s_lands = _scatter_wait(s_ssem, s_rsem, s_thru, s_lands, _whole, all_updated, name="small_wait")
    ssum = _sum_slots(lax.dynamic_update_index_in_dim(s_lands[0], s_thru[0], my_idx, 0), name="sum_small")
    off = 0
    for (n, a), rows in zip(small_list, sm_rows):
        red[n] = ssum[off:off + rows.shape[0]].reshape(-1)[:a.size].reshape(a.shape)
        off += rows.shape[0]
    loss = red["loss"][0]
    shard_cols = conv_w.shape[2]
    red["conv_w"] = lax.dynamic_slice_in_dim(red["conv_w"], my_idx * shard_cols, shard_cols, axis=2)
    red["b_f_a"] = red["b_f_a"].reshape(b_f_a.shape)
    update("conv_w")
    small_names = [n for n in order if n not in g_out]

    def pack_small(src):
        rows = []
        for n in small_names:
            flat = src[n].reshape(-1)
            rows.append(_pad_rows(flat, _round_up(flat.size, 8 * LANES), 0).reshape(-1, LANES))
        return jnp.concatenate(rows, axis=0), [r.shape[0] for r in rows]

    red_small = {n: red[n].reshape(weights[n].shape) for n in small_names}
    wp, counts = pack_small(weights)
    gp, _ = pack_small(red_small)
    mp, _ = pack_small(m_in)
    vp, _ = pack_small(v_in)
    dp, nmp, nvp = _adamw(wp, gp, mp, vp, name="adamw_small")
    off = 0
    for n, cnt in zip(small_names, counts):
        shp = weights[n].shape
        size = weights[n].size
        g_out[n] = red_small[n]
        d_out[n] = dp[off:off + cnt].reshape(-1)[:size].reshape(shp)
        nm_out[n] = nmp[off:off + cnt].reshape(-1)[:size].reshape(shp)
        nv_out[n] = nvp[off:off + cnt].reshape(-1)[:size].reshape(shp)
        off += cnt

    return (loss, grad_x, *[g_out[n] for n in order], *[d_out[n] for n in order],
            *[nm_out[n] for n in order], *[nv_out[n] for n in order])
```

```python
import functools

import jax
import jax.numpy as jnp
from jax import lax
from jax.experimental import pallas as pl
from jax.experimental.pallas import tpu as pltpu

F32 = jnp.float32
BF16 = jnp.bfloat16
LANES = 128
HEAD_DIM = 64
N_MAIN_HEADS = 12
N_MEM_HEADS = 4
MAIN_W = N_MAIN_HEADS * HEAD_DIM
MEM_W = N_MEM_HEADS * HEAD_DIM
SCALE = HEAD_DIM ** -0.5
EPS = 1e-6
NEG = -1e30
N_DEV = 8
ATT_TILE = 256
MEM_Q_TILE = 1024
VMEM_BIG = 56 * 2 ** 20
MESH = pl.DeviceIdType.MESH

ADAM_LR = 0.001
ADAM_B1 = 0.9
ADAM_B2 = 0.999
ADAM_EPS = 1e-08
ADAM_WD = 0.01
ADAM_STEP = 10

NT = (((1,), (1,)), ((), ()))
TN = (((0,), (0,)), ((), ()))


def _pc(body, *, name, out_shape, grid=None, in_specs=None, out_specs=None, scratch_shapes=(),
        semantics=None, vmem=None):
    kw = {}
    if grid is not None:
        kw["grid"] = grid
    params = pltpu.CompilerParams(dimension_semantics=semantics, vmem_limit_bytes=vmem)
    return pl.pallas_call(body, name=name, out_shape=out_shape, in_specs=in_specs, out_specs=out_specs,
                          scratch_shapes=list(scratch_shapes), compiler_params=params, **kw)


def _sds(shape, dtype):
    return jax.ShapeDtypeStruct(shape, dtype)


def _mm_fwd(a, w, *, name, tm, tn, out_dtype, g=None, res=None, col0=0, ncols=None, save_h=False):
    m_rows, k = a.shape
    n = w.shape[1] if ncols is None else ncols
    grid = (m_rows // tm, n // tn)
    norm = g is not None

    def body(*refs):
        refs = list(refs)
        a_ref = refs.pop(0)
        g_ref = refs.pop(0) if norm else None
        w_ref = refs.pop(0)
        res_ref = refs.pop(0) if res is not None else None
        o_ref = refs.pop(0)
        hout_ref = refs.pop(0) if save_h else None
        h_ref = refs.pop(0) if norm else None
        if norm:
            @pl.when(pl.program_id(1) == 0)
            def _():
                xv = a_ref[...]
                r = lax.rsqrt(jnp.mean(xv * xv, axis=-1, keepdims=True) + EPS)
                h = ((xv * r) * g_ref[...]).astype(BF16)
                h_ref[...] = h
                if save_h:
                    hout_ref[...] = h
            lhs = h_ref[...]
        else:
            lhs = a_ref[...].astype(BF16)
        acc = jnp.dot(lhs, w_ref[...], preferred_element_type=F32)
        if res is not None:
            acc = acc + res_ref[...]
        o_ref[...] = acc.astype(out_dtype)

    in_specs = [pl.BlockSpec((tm, k), lambda i, j: (i, 0))]
    args = [a]
    if norm:
        in_specs.append(pl.BlockSpec((1, k), lambda i, j: (0, 0)))
        args.append(g.reshape(1, k))
    in_specs.append(pl.BlockSpec((k, tn), lambda i, j: (0, j + col0)))
    args.append(w)
    if res is not None:
        in_specs.append(pl.BlockSpec((tm, tn), lambda i, j: (i, j)))
        args.append(res)
    out_shape = [_sds((m_rows, n), out_dtype)]
    out_specs = [pl.BlockSpec((tm, tn), lambda i, j: (i, j))]
    if save_h:
        out_shape.append(_sds((m_rows, k), BF16))
        out_specs.append(pl.BlockSpec((tm, k), lambda i, j: (i, 0)))
    scratch = [pltpu.VMEM((tm, k), BF16)] if norm else []
    outs = _pc(body, name=name, out_shape=out_shape, grid=grid, in_specs=in_specs, out_specs=out_specs,
               scratch_shapes=scratch, semantics=("arbitrary", "arbitrary"), vmem=VMEM_BIG)(*args)
    return outs if save_h else outs[0]


def _mm_nt(a, w, *, name, tm, tn, out_dtype):
    m_rows, k = a.shape
    n = w.shape[0]

    def body(a_ref, w_ref, o_ref):
        acc = lax.dot_general(a_ref[...].astype(BF16), w_ref[...], NT, preferred_element_type=F32)
        o_ref[...] = acc.astype(out_dtype)

    return _pc(body, name=name, out_shape=_sds((m_rows, n), out_dtype), grid=(m_rows // tm, n // tn),
               in_specs=[pl.BlockSpec((tm, k), lambda i, j: (i, 0)), pl.BlockSpec((tn, k), lambda i, j: (j, 0))],
               out_specs=pl.BlockSpec((tm, tn), lambda i, j: (i, j)),
               semantics=("arbitrary", "arbitrary"), vmem=VMEM_BIG)(a, w)


def _mm_tn(a, b, *, name, ta, tn, tt):
    t_rows, ka = a.shape
    n = b.shape[1]
    nt = t_rows // tt

    def body(a_ref, b_ref, o_ref, acc_ref):
        t = pl.program_id(2)

        @pl.when(t == 0)
        def _():
            acc_ref[...] = jnp.zeros_like(acc_ref)

        acc_ref[...] += lax.dot_general(a_ref[...].astype(BF16), b_ref[...].astype(BF16), TN,
                                        preferred_element_type=F32)

        @pl.when(t == nt - 1)
        def _():
            o_ref[...] = acc_ref[...].astype(BF16)

    return _pc(body, name=name, out_shape=_sds((ka, n), BF16), grid=(ka // ta, n // tn, nt),
               in_specs=[pl.BlockSpec((tt, ta), lambda i, j, t: (t, i)),
                         pl.BlockSpec((tt, tn), lambda i, j, t: (t, j))],
               out_specs=pl.BlockSpec((ta, tn), lambda i, j, t: (i, j)),
               scratch_shapes=[pltpu.VMEM((ta, tn), F32)],
               semantics=("arbitrary", "arbitrary", "arbitrary"), vmem=VMEM_BIG)(a, b)


def _wgrad(a, b, name):
    t_rows, ka = a.shape
    n = b.shape[1]
    ta = ka if ka <= 1024 else ka // 2
    tn = n
    while ta * tn * 4 > 6 * 2 ** 20 and tn % 256 == 0:
        tn //= 2
    tt = min(1024, t_rows)
    return _mm_tn(a, b, name=name, ta=ta, tn=tn, tt=tt)


def _mm_nt_rmsbwd(parts, w, x, g, *, name, dres=None, want_dx=True):
    m_rows, d = x.shape
    k_total = sum(dy.shape[1] for dy, _ in parts)
    tm = min(512 if k_total <= 2816 else 256, m_rows)
    n_parts = len(parts)

    def body(*refs):
        refs = list(refs)
        dy_refs = [refs.pop(0) for _ in range(n_parts)]
        w_refs = [refs.pop(0) for _ in range(n_parts)]
        x_ref = refs.pop(0)
        g_ref = refs.pop(0)
        dres_ref = refs.pop(0) if dres is not None else None
        dx_ref = refs.pop(0) if want_dx else None
        dg_ref = refs.pop(0)

        @pl.when(pl.program_id(0) == 0)
        def _():
            dg_ref[...] = jnp.zeros_like(dg_ref)

        dh = None
        for dy_ref, w_ref in zip(dy_refs, w_refs):
            t = lax.dot_general(dy_ref[...].astype(BF16), w_ref[...], NT, preferred_element_type=F32)
            dh = t if dh is None else dh + t
        xv = x_ref[...]
        r = lax.rsqrt(jnp.mean(xv * xv, axis=-1, keepdims=True) + EPS)
        xh = xv * r
        dg_ref[...] += jnp.sum(dh * xh, axis=0, keepdims=True)
        if want_dx:
            dhg = dh * g_ref[...]
            dx = r * (dhg - xh * jnp.mean(dhg * xh, axis=-1, keepdims=True))
            if dres is not None:
                dx = dx + dres_ref[...]
            dx_ref[...] = dx

    in_specs, args = [], []
    for dy, _ in parts:
        in_specs.append(pl.BlockSpec((tm, dy.shape[1]), lambda i: (i, 0)))
        args.append(dy)
    for dy, cb in parts:
        in_specs.append(pl.BlockSpec((d, dy.shape[1]), functools.partial(lambda i, cb: (0, cb), cb=cb)))
        args.append(w)
    in_specs += [pl.BlockSpec((tm, d), lambda i: (i, 0)), pl.BlockSpec((1, d), lambda i: (0, 0))]
    args += [x, g.reshape(1, d)]
    if dres is not None:
        in_specs.append(pl.BlockSpec((tm, d), lambda i: (i, 0)))
        args.append(dres)
    out_shape, out_specs = [], []
    if want_dx:
        out_shape.append(_sds((m_rows, d), F32))
        out_specs.append(pl.BlockSpec((tm, d), lambda i: (i, 0)))
    out_shape.append(_sds((1, d), F32))
    out_specs.append(pl.BlockSpec((1, d), lambda i: (0, 0)))
    outs = _pc(body, name=name, out_shape=out_shape, grid=(m_rows // tm,), in_specs=in_specs,
               out_specs=out_specs, semantics=("arbitrary",), vmem=VMEM_BIG)(*args)
    return (outs[0], outs[1]) if want_dx else (None, outs[0])


def _loss_head(x, g, tgt, *, name):
    m_rows, d = x.shape
    tm = min(512, m_rows)

    def body(x_ref, g_ref, t_ref, dx_ref, dg_ref, loss_ref):
        @pl.when(pl.program_id(0) == 0)
        def _():
            dg_ref[...] = jnp.zeros_like(dg_ref)
            loss_ref[...] = jnp.zeros_like(loss_ref)

        xv = x_ref[...]
        r = lax.rsqrt(jnp.mean(xv * xv, axis=-1, keepdims=True) + EPS)
        xh = xv * r
        gv = g_ref[...]
        err = xh * gv - t_ref[...]
        per_tok = jnp.mean(err * err, axis=-1, keepdims=True)
        loss_ref[...] += 0.5 * jnp.sum(per_tok, axis=0, keepdims=True)
        dout = err * (1.0 / d)
        dg_ref[...] += jnp.sum(dout * xh, axis=0, keepdims=True)
        dhg = dout * gv
        dx_ref[...] = r * (dhg - xh * jnp.mean(dhg * xh, axis=-1, keepdims=True))

    row = pl.BlockSpec((tm, d), lambda i: (i, 0))
    return _pc(body, name=name, out_shape=[_sds((m_rows, d), F32), _sds((1, d), F32), _sds((1, LANES), F32)],
               grid=(m_rows // tm,), in_specs=[row, pl.BlockSpec((1, d), lambda i: (0, 0)), row],
               out_specs=[row, pl.BlockSpec((1, d), lambda i: (0, 0)), pl.BlockSpec((1, LANES), lambda i: (0, 0))],
               semantics=("arbitrary",))(x, g.reshape(1, d), tgt)


def _split3(v):
    hi = v.astype(BF16)
    r1 = v - hi.astype(F32)
    mid = r1.astype(BF16)
    lo = (r1 - mid.astype(F32)).astype(BF16)
    return hi, mid, lo


def _split2(v):
    hi = v.astype(BF16)
    lo = (v - hi.astype(F32)).astype(BF16)
    return hi, lo


def _tri_dot3(tri, v):
    hi, mid, lo = _split3(v)
    return (jnp.dot(tri, hi, preferred_element_type=F32) + jnp.dot(tri, mid, preferred_element_type=F32)
            + jnp.dot(tri, lo, preferred_element_type=F32))


def _log_sigmoid(v):
    return jnp.minimum(v, 0.0) - jnp.log(1.0 + jnp.exp(-jnp.abs(v)))


def _forget_cumsum(f_logit, b_f, *, B, S, name):
    ch = min(256, S)
    nch = S // ch

    def body(f_ref, b_ref, c_ref):
        r_i = lax.broadcasted_iota(jnp.int32, (ch, ch), 0)
        c_i = lax.broadcasted_iota(jnp.int32, (ch, ch), 1)
        tri = (c_i <= r_i).astype(BF16)
        bv = b_ref[...]

        def step(k, carry):
            rows = pl.ds(pl.multiple_of(k * ch, ch), ch)
            lf = _log_sigmoid(f_ref[rows, :] + bv)
            c_ref[rows, :] = _tri_dot3(tri, lf) + carry
            return carry + jnp.sum(lf, axis=0, keepdims=True)

        lax.fori_loop(0, nch, step, jnp.zeros((1, LANES), F32))

    blk = pl.BlockSpec((S, LANES), lambda b: (b, 0))
    return _pc(body, name=name, out_shape=_sds((B * S, LANES), F32), grid=(B,),
               in_specs=[blk, pl.BlockSpec((1, LANES), lambda b: (0, 0))], out_specs=blk,
               semantics=("arbitrary",))(f_logit, b_f)


def _forget_cumsum_bwd(dc, f_logit, b_f, *, B, S, name):
    ch = min(256, S)
    nch = S // ch

    def body(dc_ref, f_ref, b_ref, df_ref, db_ref):
        @pl.when(pl.program_id(0) == 0)
        def _():
            db_ref[...] = jnp.zeros_like(db_ref)

        r_i = lax.broadcasted_iota(jnp.int32, (ch, ch), 0)
        c_i = lax.broadcasted_iota(jnp.int32, (ch, ch), 1)
        tri = (c_i >= r_i).astype(BF16)
        bv = b_ref[...]

        def step(kk, carry):
            tail, dbs = carry
            k = nch - 1 - kk
            rows = pl.ds(pl.multiple_of(k * ch, ch), ch)
            dcv = dc_ref[rows, :]
            dlf = _tri_dot3(tri, dcv) + tail
            z = f_ref[rows, :] + bv
            df = dlf * (1.0 / (1.0 + jnp.exp(z)))
            df_ref[rows, :] = df.astype(BF16)
            return tail + jnp.sum(dcv, axis=0, keepdims=True), dbs + jnp.sum(df, axis=0, keepdims=True)

        zero = jnp.zeros((1, LANES), F32)
        _, dbs = lax.fori_loop(0, nch, step, (zero, zero))
        db_ref[...] += dbs

    blk = pl.BlockSpec((S, LANES), lambda b: (b, 0))
    one = pl.BlockSpec((1, LANES), lambda b: (0, 0))
    return _pc(body, name=name, out_shape=[_sds((B * S, LANES), BF16), _sds((1, LANES), F32)], grid=(B,),
               in_specs=[blk, blk, one], out_specs=[blk, one], semantics=("arbitrary",))(dc, f_logit, b_f)


def _head_mask(lane, hh):
    return (lane < HEAD_DIM) if hh == 0 else (lane >= HEAD_DIM)


HEAD_GROUP = 3
FWD_HEAD_GROUP = 6


def _g_col_spec(rows, nblk_rows, cb, G):
    return pl.BlockSpec((rows, G * LANES), lambda b, p, i: (b * nblk_rows + i, cb // G + p))


def _g_kv_spec(rows, cb, G):
    return pl.BlockSpec((rows, G * LANES), lambda b, p, i: (b, cb // G + p))


def _g_stat_col_spec(tq, G):
    return pl.BlockSpec((1, 2 * G, tq, 1), lambda b, p, i: (b, p, i, 0))


def _g_stat_row_spec(S, G):
    return pl.BlockSpec((1, 2 * G, 1, S), lambda b, p, i: (b, p, 0, 0))


def _lanes(g):
    return slice(g * LANES, (g + 1) * LANES)


def _streams(x_ref, G, scale=None):
    rows = x_ref.shape[0]
    lane = lax.broadcasted_iota(jnp.int32, (rows, LANES), 1)
    out = []
    for g in range(G):
        x = x_ref[:, _lanes(g)]
        if scale is not None:
            x = x * jnp.asarray(scale, x.dtype)
        for hh in range(2):
            out.append(jnp.where(_head_mask(lane, hh), x, jnp.zeros_like(x)))
    return lane, out


def _wide(stat, width):
    return jnp.tile(stat, (1, width // LANES))


def _fold_lanes(v):
    out = v[:, :LANES]
    for j in range(1, v.shape[1] // LANES):
        out = out + v[:, j * LANES:(j + 1) * LANES]
    return out


def _kv_blocks(ref, ks, tk, G):
    return [ref[pl.ds(ks, tk), _lanes(g)] for g in range(G)]


def _sweep(i, block):
    def step(kb, c):
        block(kb, False)
        return c
    lax.fori_loop(0, i, step, 0)
    block(i, True)


def _fox_fwd_g(qa, ka, va, cr, *, name, B, S, P, q_cb, k_cb, v_cb, G=HEAD_GROUP):
    tq = tk = min(ATT_TILE, S)
    nq = S // tq
    NS = 2 * G

    def body(q_ref, k_ref, v_ref, cr_ref, o_ref, lse_ref, acc_ref, m_ref, l_ref):
        i = pl.program_id(2)
        lane, qh = _streams(q_ref, G, SCALE)
        on_or_below = (lax.broadcasted_iota(jnp.int32, (tq, tk), 1) <= lax.broadcasted_iota(jnp.int32, (tq, tk), 0))
        m_ref[...] = jnp.full(m_ref.shape, NEG, F32)
        l_ref[...] = jnp.zeros(l_ref.shape, F32)
        acc_ref[...] = jnp.zeros(acc_ref.shape, F32)

        def block(kb, diag):
            ks = pl.multiple_of(kb * tk, tk)
            kblk = _kv_blocks(k_ref, ks, tk, G)
            vblk = _kv_blocks(v_ref, ks, tk, G)
            ss = [lax.dot_general(qh[st], kblk[st // 2], NT, preferred_element_type=F32) for st in range(NS)]
            ps = []
            for st in range(NS):
                s = ss[st] - cr_ref[0, st, :, pl.ds(ks, tk)]
                if diag:
                    s = jnp.where(on_or_below, s, NEG)
                m = m_ref[st]
                m_new = jnp.maximum(m, jnp.max(s, axis=-1, keepdims=True))
                alpha = jnp.exp(m - m_new)
                p = jnp.exp(s - _wide(m_new, tk))
                m_ref[st] = m_new
                l_ref[st] = alpha * l_ref[st] + _fold_lanes(p)
                ps.append((alpha, p.astype(BF16)))
            pvs = [jnp.dot(ps[st][1], vblk[st // 2], preferred_element_type=F32) for st in range(NS)]
            for st in range(NS):
                acc_ref[st] = ps[st][0] * acc_ref[st] + pvs[st]

        _sweep(i, block)
        ls = [jnp.sum(l_ref[st], axis=-1, keepdims=True) for st in range(NS)]
        for st in range(NS):
            lse_ref[0, st] = jnp.max(m_ref[st], axis=-1, keepdims=True) + jnp.log(ls[st])
        for g in range(G):
            o_ref[:, _lanes(g)] = jnp.where(lane < HEAD_DIM, acc_ref[2 * g] / ls[2 * g],
                                            acc_ref[2 * g + 1] / ls[2 * g + 1]).astype(BF16)

    return _pc(body, name=name, out_shape=[_sds((B * S, P * LANES), BF16), _sds((B, 2 * P, S, 1), F32)],
               grid=(B, P // G, nq),
               in_specs=[_g_col_spec(tq, nq, q_cb, G), _g_kv_spec(S, k_cb, G), _g_kv_spec(S, v_cb, G),
                         _g_stat_row_spec(S, G)],
               out_specs=[_g_col_spec(tq, nq, 0, G), _g_stat_col_spec(tq, G)],
               scratch_shapes=[pltpu.VMEM((NS, tq, LANES), F32)] * 3,
               semantics=("arbitrary", "arbitrary", "arbitrary"), vmem=VMEM_BIG)(qa, ka, va, cr)


def _fox_bwd_g(qa, ka, va, doa, lse, cr, *, name, B, S, P, q_cb, k_cb, v_cb, do_cb, G=HEAD_GROUP):
    tq = tk = min(ATT_TILE, S)
    nq = S // tq
    NS = 2 * G

    def body(q_ref, k_ref, v_ref, do_ref, lse_ref, cr_ref, dq_ref, dk_ref, dv_ref, dcs_ref, dqa_ref, delta_ref, lse_s,
             p_buf, dp_buf):
        i = pl.program_id(2)

        @pl.when(i == 0)
        def _():
            dk_ref[...] = jnp.zeros_like(dk_ref)
            dv_ref[...] = jnp.zeros_like(dv_ref)
            dcs_ref[...] = jnp.zeros_like(dcs_ref)

        lane, qh = _streams(q_ref, G, SCALE)
        _, doh = _streams(do_ref, G)
        on_or_below = (lax.broadcasted_iota(jnp.int32, (tq, tk), 1) <= lax.broadcasted_iota(jnp.int32, (tq, tk), 0))
        delta_ref[...] = jnp.zeros(delta_ref.shape, F32)
        dqa_ref[...] = jnp.zeros(dqa_ref.shape, F32)
        for st in range(NS):
            lse_s[st] = jnp.broadcast_to(lse_ref[0, st], (tq, LANES))

        def probs(kb, diag):
            ks = pl.multiple_of(kb * tk, tk)
            kblk = _kv_blocks(k_ref, ks, tk, G)
            vblk = _kv_blocks(v_ref, ks, tk, G)
            ss = [lax.dot_general(qh[st], kblk[st // 2], NT, preferred_element_type=F32) for st in range(NS)]
            dps = [lax.dot_general(doh[st], vblk[st // 2], NT, preferred_element_type=F32) for st in range(NS)]
            ps = []
            for st in range(NS):
                s = ss[st] - cr_ref[0, st, :, pl.ds(ks, tk)]
                if diag:
                    s = jnp.where(on_or_below, s, NEG)
                ps.append(jnp.exp(s - _wide(lse_s[st], tk)))
            return ks, kblk, ps, dps

        def delta_block(kb, diag):
            _, _, ps, dps = probs(kb, diag)
            for st in range(NS):
                delta_ref[st] += _fold_lanes(ps[st] * dps[st])
                p_buf[st, kb] = ps[st]
                dp_buf[st, kb] = dps[st]

        _sweep(i, delta_block)
        for st in range(NS):
            delta_ref[st] = jnp.broadcast_to(jnp.sum(delta_ref[st], axis=-1, keepdims=True), (tq, LANES))

        def grad_block(kb, diag):
            ks = pl.multiple_of(kb * tk, tk)
            kblk = _kv_blocks(k_ref, ks, tk, G)
            rows = pl.ds(ks, tk)
            dsb, pb = [], []
            for st in range(NS):
                p = p_buf[st, kb]
                ds = p * (dp_buf[st, kb] - _wide(delta_ref[st], tk))
                dcs_ref[0, st, :, rows] -= jnp.sum(ds, axis=0, keepdims=True)
                dsb.append(ds.astype(BF16))
                pb.append(p.astype(BF16))
            dks = [lax.dot_general(dsb[st], qh[st], TN, preferred_element_type=F32) for st in range(NS)]
            dvs = [lax.dot_general(pb[st], doh[st], TN, preferred_element_type=F32) for st in range(NS)]
            dqs = [jnp.dot(dsb[st], kblk[st // 2], preferred_element_type=F32) for st in range(NS)]
            for g in range(G):
                dk_ref[rows, _lanes(g)] += dks[2 * g] + dks[2 * g + 1]
                dv_ref[rows, _lanes(g)] += dvs[2 * g] + dvs[2 * g + 1]
            for st in range(NS):
                dqa_ref[st] += dqs[st]

        _sweep(i, grad_block)
        for g in range(G):
            dq_ref[:, _lanes(g)] = (jnp.where(lane < HEAD_DIM, dqa_ref[2 * g], dqa_ref[2 * g + 1]) * SCALE).astype(BF16)

    return _pc(body, name=name,
               out_shape=[_sds((B * S, P * LANES), BF16), _sds((B * S, P * LANES), F32), _sds((B * S, P * LANES), F32),
                          _sds((B, 2 * P, 1, S), F32)],
               grid=(B, P // G, nq),
               in_specs=[_g_col_spec(tq, nq, q_cb, G), _g_kv_spec(S, k_cb, G), _g_kv_spec(S, v_cb, G),
                         _g_col_spec(tq, nq, do_cb, G), _g_stat_col_spec(tq, G), _g_stat_row_spec(S, G)],
               out_specs=[_g_col_spec(tq, nq, 0, G), _g_kv_spec(S, 0, G), _g_kv_spec(S, 0, G), _g_stat_row_spec(S, G)],
               scratch_shapes=[pltpu.VMEM((NS, tq, LANES), F32)] * 3 + [pltpu.VMEM((NS, nq, tq, tk), F32)] * 2,
               semantics=("arbitrary", "arbitrary", "arbitrary"), vmem=VMEM_BIG)(qa, ka, va, doa, lse, cr)


def _sb_logs_z(z):
    nz = -z
    lm = jnp.minimum(nz, 0.0) - jnp.log(1.0 + jnp.exp(jnp.minimum(z, nz)))
    return lm + z, lm


def _sb_fwd_g(qa, ka, va, *, name, B, S, P, q_cb, k_cb, v_cb, G=HEAD_GROUP):
    tq = tk = min(ATT_TILE, S)
    nq = S // tq
    NS = 2 * G

    def body(q_ref, k_ref, v_ref, o_ref, rt_ref, acc_ref, run_ref):
        i = pl.program_id(2)
        lane, qh = _streams(q_ref, G, SCALE)
        t_r = lax.broadcasted_iota(jnp.int32, (tk, tk), 0)
        t_c = lax.broadcasted_iota(jnp.int32, (tk, tk), 1)
        after = (t_r > t_c).astype(BF16)
        below = t_c < t_r
        acc_ref[...] = jnp.zeros(acc_ref.shape, F32)
        run_ref[...] = jnp.zeros(run_ref.shape, F32)

        def block(kb, diag):
            ks = pl.multiple_of(kb * tk, tk)
            kblk = _kv_blocks(k_ref, ks, tk, G)
            vblk = _kv_blocks(v_ref, ks, tk, G)
            zs = [lax.dot_general(qh[st], kblk[st // 2], NT, preferred_element_type=F32) for st in range(NS)]
            lss, parts = [], []
            for st in range(NS):
                ls, lm = _sb_logs_z(zs[st])
                if diag:
                    lm = jnp.where(below, lm, 0.0)
                lss.append(ls + _wide(run_ref[st], tk))
                run_ref[st] += jnp.sum(lm, axis=-1, keepdims=True)
                parts.append(_split2(lm))
            sufs = [jnp.dot(parts[st][0], after, preferred_element_type=F32)
                    + jnp.dot(parts[st][1], after, preferred_element_type=F32) for st in range(NS)]
            ab = []
            for st in range(NS):
                a = jnp.exp(lss[st] + sufs[st])
                if diag:
                    a = jnp.where(below, a, 0.0)
                ab.append(a.astype(BF16))
            pvs = [jnp.dot(ab[st], vblk[st // 2], preferred_element_type=F32) for st in range(NS)]
            for st in range(NS):
                acc_ref[st] += pvs[st]

        block(i, True)

        def step(jj, c):
            block(i - 1 - jj, False)
            return c

        lax.fori_loop(0, i, step, 0)
        for st in range(NS):
            rt_ref[0, st] = jnp.max(run_ref[st], axis=-1, keepdims=True)
        for g in range(G):
            o_ref[:, _lanes(g)] = jnp.where(lane < HEAD_DIM, acc_ref[2 * g], acc_ref[2 * g + 1]).astype(BF16)

    return _pc(body, name=name, out_shape=[_sds((B * S, P * LANES), BF16), _sds((B, 2 * P, S, 1), F32)],
               grid=(B, P // G, nq),
               in_specs=[_g_col_spec(tq, nq, q_cb, G), _g_kv_spec(S, k_cb, G), _g_kv_spec(S, v_cb, G)],
               out_specs=[_g_col_spec(tq, nq, 0, G), _g_stat_col_spec(tq, G)],
               scratch_shapes=[pltpu.VMEM((NS, tq, LANES), F32)] * 2,
               semantics=("arbitrary", "arbitrary", "arbitrary"), vmem=VMEM_BIG)(qa, ka, va)


def _sb_bwd_g(qa, ka, va, doa, rt, *, name, B, S, P, q_cb, k_cb, v_cb, do_cb, G=HEAD_GROUP):
    tq = tk = min(ATT_TILE, S)
    nq = S // tq
    NS = 2 * G

    def body(q_ref, k_ref, v_ref, do_ref, rt_ref, dq_ref, dk_ref, dv_ref, dqa_ref, pl_ref, pg_ref):
        i = pl.program_id(2)

        @pl.when(i == 0)
        def _():
            dk_ref[...] = jnp.zeros_like(dk_ref)
            dv_ref[...] = jnp.zeros_like(dv_ref)

        lane, qh = _streams(q_ref, G, SCALE)
        _, doh = _streams(do_ref, G)
        t_r = lax.broadcasted_iota(jnp.int32, (tk, tk), 0)
        t_c = lax.broadcasted_iota(jnp.int32, (tk, tk), 1)
        upto = (t_r <= t_c).astype(BF16)
        before = (t_r < t_c).astype(BF16)
        below = t_c < t_r
        dqa_ref[...] = jnp.zeros(dqa_ref.shape, F32)
        pg_ref[...] = jnp.zeros(pg_ref.shape, F32)
        for st in range(NS):
            pl_ref[st] = jnp.broadcast_to(rt_ref[0, st], (tq, LANES))

        def block(kb, diag):
            ks = pl.multiple_of(kb * tk, tk)
            rows = pl.ds(ks, tk)
            kblk = _kv_blocks(k_ref, ks, tk, G)
            vblk = _kv_blocks(v_ref, ks, tk, G)
            zs = [lax.dot_general(qh[st], kblk[st // 2], NT, preferred_element_type=F32) for st in range(NS)]
            das = [lax.dot_general(doh[st], vblk[st // 2], NT, preferred_element_type=F32) for st in range(NS)]
            lss, parts = [], []
            for st in range(NS):
                ls, lm = _sb_logs_z(zs[st])
                if diag:
                    lm = jnp.where(below, lm, 0.0)
                lss.append((ls, ls + _wide(pl_ref[st], tk)))
                pl_ref[st] -= jnp.sum(lm, axis=-1, keepdims=True)
                parts.append(_split2(lm))
            pins = [jnp.dot(parts[st][0], upto, preferred_element_type=F32)
                    + jnp.dot(parts[st][1], upto, preferred_element_type=F32) for st in range(NS)]
            gms, ab, gparts = [], [], []
            for st in range(NS):
                a = jnp.exp(lss[st][1] - pins[st])
                if diag:
                    a = jnp.where(below, a, 0.0)
                gm = a * das[st]
                gms.append(gm)
                ab.append(a.astype(BF16))
                gparts.append(gm.astype(BF16))
            pgs = [jnp.dot(gparts[st], before, preferred_element_type=F32) for st in range(NS)]
            dzb = []
            for st in range(NS):
                gm = gms[st]
                dz = gm - jnp.exp(lss[st][0]) * (gm + (pgs[st] + _wide(pg_ref[st], tk)))
                if diag:
                    dz = jnp.where(below, dz, 0.0)
                pg_ref[st] += jnp.sum(gm, axis=-1, keepdims=True)
                dzb.append(dz.astype(BF16))
            dks = [lax.dot_general(dzb[st], qh[st], TN, preferred_element_type=F32) for st in range(NS)]
            dvs = [lax.dot_general(ab[st], doh[st], TN, preferred_element_type=F32) for st in range(NS)]
            dqs = [jnp.dot(dzb[st], kblk[st // 2], preferred_element_type=F32) for st in range(NS)]
            for g in range(G):
                dk_ref[rows, _lanes(g)] += dks[2 * g] + dks[2 * g + 1]
                dv_ref[rows, _lanes(g)] += dvs[2 * g] + dvs[2 * g + 1]
            for st in range(NS):
                dqa_ref[st] += dqs[st]

        _sweep(i, block)
        for g in range(G):
            dq_ref[:, _lanes(g)] = (jnp.where(lane < HEAD_DIM, dqa_ref[2 * g], dqa_ref[2 * g + 1]) * SCALE).astype(BF16)

    return _pc(body, name=name,
               out_shape=[_sds((B * S, P * LANES), BF16), _sds((B * S, P * LANES), F32), _sds((B * S, P * LANES), F32)],
               grid=(B, P // G, nq),
               in_specs=[_g_col_spec(tq, nq, q_cb, G), _g_kv_spec(S, k_cb, G), _g_kv_spec(S, v_cb, G),
                         _g_col_spec(tq, nq, do_cb, G), _g_stat_col_spec(tq, G)],
               out_specs=[_g_col_spec(tq, nq, 0, G), _g_kv_spec(S, 0, G), _g_kv_spec(S, 0, G)],
               scratch_shapes=[pltpu.VMEM((NS, tq, LANES), F32)] * 3,
               semantics=("arbitrary", "arbitrary", "arbitrary"), vmem=VMEM_BIG)(qa, ka, va, doa, rt)


MEM_GROUP = N_MEM_HEADS // 2


def _mem_fwd(qa, kva, *, name, B, S, NM, q_cb):
    G = MEM_GROUP
    NS = 2 * G
    tq = min(MEM_Q_TILE, S)
    nq = S // tq

    def body(q_ref, k_ref, v_ref, o_ref, lse_ref):
        lane, qh = _streams(q_ref, G, SCALE)
        kblk = [k_ref[:, _lanes(g)] for g in range(G)]
        vblk = [v_ref[:, _lanes(g)] for g in range(G)]
        ss = [lax.dot_general(qh[st], kblk[st // 2], NT, preferred_element_type=F32) for st in range(NS)]
        pb, ls = [], []
        for st in range(NS):
            m = jnp.max(ss[st], axis=-1, keepdims=True)
            p = jnp.exp(ss[st] - m)
            l = jnp.sum(p, axis=-1, keepdims=True)
            lse_ref[0, st] = m + jnp.log(l)
            pb.append(p.astype(BF16))
            ls.append(l)
        pvs = [jnp.dot(pb[st], vblk[st // 2], preferred_element_type=F32) for st in range(NS)]
        for g in range(G):
            o_ref[:, _lanes(g)] = jnp.where(lane < HEAD_DIM, pvs[2 * g] / ls[2 * g],
                                            pvs[2 * g + 1] / ls[2 * g + 1]).astype(BF16)

    return _pc(body, name=name, out_shape=[_sds((B * S, G * LANES), BF16), _sds((B, NS, S, 1), F32)],
               grid=(B, 1, nq),
               in_specs=[_g_col_spec(tq, nq, q_cb, G), _g_kv_spec(NM, 0, G), _g_kv_spec(NM, G, G)],
               out_specs=[_g_col_spec(tq, nq, 0, G), _g_stat_col_spec(tq, G)],
               semantics=("arbitrary", "arbitrary", "arbitrary"), vmem=VMEM_BIG)(qa, kva, kva)


def _mem_bwd(qa, kva, doa, oa, lse, *, name, B, S, NM, q_cb, do_cb):
    G = MEM_GROUP
    NS = 2 * G
    tq = min(MEM_Q_TILE, S)
    nq = S // tq

    def body(q_ref, k_ref, v_ref, do_ref, o_ref, lse_ref, dq_ref, dk_ref, dv_ref):
        @pl.when(pl.program_id(2) == 0)
        def _():
            dk_ref[...] = jnp.zeros_like(dk_ref)
            dv_ref[...] = jnp.zeros_like(dv_ref)

        lane, qh = _streams(q_ref, G, SCALE)
        _, doh = _streams(do_ref, G)
        kblk = [k_ref[:, _lanes(g)] for g in range(G)]
        vblk = [v_ref[:, _lanes(g)] for g in range(G)]
        prod = [do_ref[:, _lanes(g)].astype(F32) * o_ref[:, _lanes(g)].astype(F32) for g in range(G)]
        ss = [lax.dot_general(qh[st], kblk[st // 2], NT, preferred_element_type=F32) for st in range(NS)]
        dps = [lax.dot_general(doh[st], vblk[st // 2], NT, preferred_element_type=F32) for st in range(NS)]
        dsb, pb = [], []
        for st in range(NS):
            delta = jnp.sum(jnp.where(_head_mask(lane, st % 2), prod[st // 2], 0.0), axis=-1, keepdims=True)
            p = jnp.exp(ss[st] - lse_ref[0, st])
            dsb.append((p * (dps[st] - delta)).astype(BF16))
            pb.append(p.astype(BF16))
        dks = [lax.dot_general(dsb[st], qh[st], TN, preferred_element_type=F32) for st in range(NS)]
        dvs = [lax.dot_general(pb[st], doh[st], TN, preferred_element_type=F32) for st in range(NS)]
        dqs = [jnp.dot(dsb[st], kblk[st // 2], preferred_element_type=F32) for st in range(NS)]
        for g in range(G):
            dk_ref[:, _lanes(g)] += dks[2 * g] + dks[2 * g + 1]
            dv_ref[:, _lanes(g)] += dvs[2 * g] + dvs[2 * g + 1]
            dq_ref[:, _lanes(g)] = (jnp.where(lane < HEAD_DIM, dqs[2 * g], dqs[2 * g + 1]) * SCALE).astype(BF16)

    return _pc(body, name=name,
               out_shape=[_sds((B * S, G * LANES), BF16), _sds((B * NM, G * LANES), F32), _sds((B * NM, G * LANES), F32)],
               grid=(B, 1, nq),
               in_specs=[_g_col_spec(tq, nq, q_cb, G), _g_kv_spec(NM, 0, G), _g_kv_spec(NM, G, G),
                         _g_col_spec(tq, nq, do_cb, G), _g_col_spec(tq, nq, 0, G), _g_stat_col_spec(tq, G)],
               out_specs=[_g_col_spec(tq, nq, 0, G), _g_kv_spec(NM, 0, G), _g_kv_spec(NM, 0, G)],
               semantics=("arbitrary", "arbitrary", "arbitrary"), vmem=VMEM_BIG)(qa, kva, kva, doa, oa, lse)


def _sigmoid(v):
    return 0.5 * jnp.tanh(0.5 * v) + 0.5


def _shift_rows(cur, halo_ref, first, rows_idx, k):
    out = pltpu.roll(cur, k, 0)
    top = out[0:8, :]
    for r in range(k):
        hr = halo_ref.shape[0] - k + r
        edge = jnp.where(first, 0.0, halo_ref[hr:hr + 1, :])
        top = jnp.where(rows_idx[0:8, :] == r, edge, top)
    return jnp.concatenate([top, out[8:, :]], axis=0)


def _shift_rows_up(cur, halo_ref, last, rows_idx, k, ts):
    out = pltpu.roll(cur, ts - k, 0)
    bottom = out[ts - 8:, :]
    for r in range(k):
        edge = jnp.where(last, 0.0, halo_ref[r:r + 1, :])
        bottom = jnp.where(rows_idx[0:8, :] == 8 - k + r, edge, bottom)
    return jnp.concatenate([out[:ts - 8, :], bottom], axis=0)


def _ffn_up_gate(x, g, w, cw, cb, *, name, S):
    T, D = x.shape
    F = w.shape[1] // 2
    tm = min(1024, S)
    tn = 256
    nj = F // tn
    tiles_per_seq = S // tm
    halo = 16

    def body(x_ref, xh_ref, g_ref, wg_ref, wv_ref, cwg_ref, cwv_ref, cbg_ref, cbv_ref,
             uc_ref, ub_ref, a_ref, hout_ref, h_ref, hh_ref, eg_ref, ev_ref):
        first = lax.rem(pl.program_id(0), tiles_per_seq) == 0

        @pl.when(pl.program_id(1) == 0)
        def _():
            def norm(v):
                r = lax.rsqrt(jnp.mean(v * v, axis=-1, keepdims=True) + EPS)
                return ((v * r) * g_ref[...]).astype(BF16)
            h = norm(x_ref[...])
            h_ref[...] = h
            hout_ref[...] = h
            hh_ref[...] = norm(xh_ref[...])

        h = h_ref[...]
        rows_idx = lax.broadcasted_iota(jnp.int32, (tm, tn), 0)
        uc = []
        for half, (w_ref, cw_ref, cb_ref, e_ref) in enumerate(((wg_ref, cwg_ref, cbg_ref, eg_ref),
                                                               (wv_ref, cwv_ref, cbv_ref, ev_ref))):
            acc = jnp.dot(h, w_ref[...], preferred_element_type=F32)
            e_ref[...] = jnp.dot(hh_ref[...], w_ref[...], preferred_element_type=F32)
            ub_ref[half] = acc.astype(BF16)
            m1 = _shift_rows(acc, e_ref, first, rows_idx, 1)
            m2 = _shift_rows(acc, e_ref, first, rows_idx, 2)
            uc.append(cb_ref[...] + cw_ref[0:1, :] * m2 + cw_ref[1:2, :] * m1 + cw_ref[2:3, :] * acc)
            uc_ref[half] = uc[half]
        a_ref[...] = (uc[0] * _sigmoid(uc[0]) * uc[1]).astype(BF16)

    in_specs = [pl.BlockSpec((tm, D), lambda i, j: (i, 0)),
                pl.BlockSpec((halo, D), lambda i, j: (jnp.maximum(i * (tm // halo) - 1, 0), 0)),
                pl.BlockSpec((1, D), lambda i, j: (0, 0)),
                pl.BlockSpec((D, tn), lambda i, j: (0, j)), pl.BlockSpec((D, tn), lambda i, j: (0, j + nj)),
                pl.BlockSpec((3, tn), lambda i, j: (0, j)), pl.BlockSpec((3, tn), lambda i, j: (0, j + nj)),
                pl.BlockSpec((1, tn), lambda i, j: (0, j)), pl.BlockSpec((1, tn), lambda i, j: (0, j + nj))]
    return _pc(body, name=name,
               out_shape=[_sds((2, T, F), F32), _sds((2, T, F), BF16), _sds((T, F), BF16), _sds((T, D), BF16)],
               grid=(T // tm, nj), in_specs=in_specs,
               out_specs=[pl.BlockSpec((2, tm, tn), lambda i, j: (0, i, j)), pl.BlockSpec((2, tm, tn), lambda i, j: (0, i, j)),
                          pl.BlockSpec((tm, tn), lambda i, j: (i, j)), pl.BlockSpec((tm, D), lambda i, j: (i, 0))],
               scratch_shapes=[pltpu.VMEM((tm, D), BF16), pltpu.VMEM((halo, D), BF16),
                               pltpu.VMEM((halo, tn), F32), pltpu.VMEM((halo, tn), F32)],
               semantics=("arbitrary", "arbitrary"), vmem=VMEM_BIG)(x, x, g.reshape(1, D), w, w, cw, cw, cb, cb)


def _conv_gate_bwd(da, uc, ub, cw, *, name, B, S):
    F = uc.shape[2]
    tf = F // 2
    ts = min(256, S)
    ns, nf = S // ts, F // tf

    def body(da_ref, uc_ref, ub_ref, wg_ref, wv_ref, dug_ref, duv_ref, pg_ref, pv_ref, nxt_g, nxt_v):
        last = pl.program_id(2) == 0

        @pl.when(jnp.logical_and(pl.program_id(1) == 0, last))
        def _():
            pg_ref[...] = jnp.zeros_like(pg_ref)
            pv_ref[...] = jnp.zeros_like(pv_ref)

        rows_idx = lax.broadcasted_iota(jnp.int32, (ts, tf), 0)
        ucg, ucv = uc_ref[0], uc_ref[1]
        sg = _sigmoid(ucg)
        dav = da_ref[...]
        d_v = dav * (ucg * sg)
        d_g = dav * ucv * (sg * (1.0 + ucg * (1.0 - sg)))
        for half, (o_ref, p_ref, d, w_ref, nxt) in enumerate(((dug_ref, pg_ref, d_g, wg_ref, nxt_g),
                                                               (duv_ref, pv_ref, d_v, wv_ref, nxt_v))):
            p1 = _shift_rows_up(d, nxt, last, rows_idx, 1, ts)
            p2 = _shift_rows_up(d, nxt, last, rows_idx, 2, ts)
            o_ref[...] = (w_ref[2:3, :] * d + w_ref[1:2, :] * p1 + w_ref[0:1, :] * p2).astype(BF16)
            nxt[...] = d[0:8, :]
            uh = ub_ref[half].astype(F32)
            for k, dk in enumerate((p2, p1, d)):
                p_ref[k:k + 1, :] += jnp.sum(dk * uh, axis=0, keepdims=True)
            p_ref[3:4, :] += jnp.sum(d, axis=0, keepdims=True)

    row = pl.BlockSpec((ts, tf), lambda j, b, r: (b * ns + ns - 1 - r, j))
    both = pl.BlockSpec((2, ts, tf), lambda j, b, r: (0, b * ns + ns - 1 - r, j))
    par = pl.BlockSpec((8, tf), lambda j, b, r: (0, j))
    return _pc(body, name=name,
               out_shape=[_sds((B * S, F), BF16), _sds((B * S, F), BF16), _sds((8, F), F32), _sds((8, F), F32)],
               grid=(nf, B, ns),
               in_specs=[row, both, both, pl.BlockSpec((3, tf), lambda j, b, r: (0, j)),
                         pl.BlockSpec((3, tf), lambda j, b, r: (0, j + nf))],
               out_specs=[row, row, par, par],
               scratch_shapes=[pltpu.VMEM((8, tf), F32), pltpu.VMEM((8, tf), F32)],
               semantics=("arbitrary", "arbitrary", "arbitrary"), vmem=VMEM_BIG)(da, uc, ub, cw, cw)


def _adamw(w, g, m, v, *, name):
    rows, cols = w.shape
    tr = rows
    while tr * cols * 4 > 2 ** 20 and tr % 16 == 0:
        tr //= 2

    def body(w_ref, g_ref, m_ref, v_ref, d_ref, nm_ref, nv_ref):
        gv = g_ref[...]
        m_new = ADAM_B1 * m_ref[...] + (1.0 - ADAM_B1) * gv
        v_new = ADAM_B2 * v_ref[...] + (1.0 - ADAM_B2) * (gv * gv)
        m_hat = m_new / (1.0 - ADAM_B1 ** ADAM_STEP)
        v_hat = v_new / (1.0 - ADAM_B2 ** ADAM_STEP)
        d_ref[...] = -ADAM_LR * (m_hat / (jnp.sqrt(v_hat) + ADAM_EPS) + ADAM_WD * w_ref[...])
        nm_ref[...] = m_new
        nv_ref[...] = v_new

    blk = pl.BlockSpec((tr, cols), lambda i: (i, 0))
    return _pc(body, name=name, out_shape=[_sds((rows, cols), F32)] * 3, grid=(rows // tr,),
               in_specs=[blk] * 4, out_specs=[blk] * 3, semantics=("arbitrary",))(w, g, m, v)


def _my_pos():
    return lax.axis_index("x"), lax.axis_index("y"), lax.axis_index("c")


_HBM = pl.BlockSpec(memory_space=pltpu.HBM)
_SEM = pl.BlockSpec(memory_space=pltpu.SEMAPHORE)
_EFFECT = pltpu.SideEffectType.DATAFLOW_SIDE_EFFECTING


def _peers():
    x, y, c = _my_pos()
    out = []
    for k in range(1, N_DEV):
        px, py, pc = x ^ ((k >> 2) & 1), y ^ ((k >> 1) & 1), c ^ (k & 1)
        out.append(((px, py, pc), 4 * px + 2 * py + pc))
    return out


def _scatter_start(srcs, slot_of, *, name, order_after=None):
    n = len(srcs)
    lands = [lax.empty((N_DEV,) + slot_of(s, 0, shape_only=True), s.dtype) for s in srcs]
    extra = [] if order_after is None else [order_after]

    def body(*refs):
        src_refs, land_refs = refs[:n], refs[n:2 * n]
        send_sems, recv_sems = refs[2 * n + len(extra)], refs[2 * n + len(extra) + 1]
        token = refs[-1]
        x, y, c = _my_pos()
        me = 4 * x + 2 * y + c
        for a in range(n):
            for k, (peer, peer_idx) in enumerate(_peers()):
                pltpu.make_async_remote_copy(
                    src_ref=slot_of(src_refs[a], peer_idx), dst_ref=land_refs[a].at[me],
                    send_sem=send_sems.at[a * 7 + k], recv_sem=recv_sems.at[a * 7 + k],
                    device_id=peer, device_id_type=MESH).start()
        token[...] = jnp.zeros_like(token)

    hbm = lambda a: pltpu.HBM(a.shape, a.dtype)
    args = [pltpu.with_memory_space_constraint(a, pltpu.HBM) for a in list(srcs) + lands] + extra
    outs = pl.pallas_call(
        body, name=name,
        out_shape=(pltpu.SemaphoreType.DMA((7 * n,)), pltpu.SemaphoreType.DMA((7 * n,)),
                   *[hbm(a) for a in srcs], *[hbm(a) for a in lands], _sds((8, LANES), F32)),
        in_specs=[_HBM] * (2 * n) + [pl.BlockSpec(memory_space=pl.ANY)] * len(extra),
        out_specs=(_SEM, _SEM, *([_HBM] * (2 * n)), pl.BlockSpec(memory_space=pltpu.VMEM)),
        input_output_aliases={a: 2 + a for a in range(2 * n)},
        compiler_params=pltpu.CompilerParams(has_side_effects=_EFFECT))(*args)
    return outs[0], outs[1], list(outs[2:2 + n]), list(outs[2 + n:2 + 2 * n]), outs[-1]


def _scatter_wait(send_sems, recv_sems, srcs, lands, slot_of, after, *, name, first=0):
    n = len(srcs)

    def body(*refs):
        src_refs, land_refs = refs[:n], refs[n:2 * n]
        ssem, rsem = refs[2 * n], refs[2 * n + 1]
        x, y, c = _my_pos()
        me = 4 * x + 2 * y + c
        for a in range(n):
            for k, (peer, peer_idx) in enumerate(_peers()):
                cp = pltpu.make_async_remote_copy(
                    src_ref=slot_of(src_refs[a], peer_idx), dst_ref=land_refs[a].at[me],
                    send_sem=ssem.at[(first + a) * 7 + k], recv_sem=rsem.at[(first + a) * 7 + k],
                    device_id=peer, device_id_type=MESH)
                cp.wait_send()
                cp.wait_recv()

    hbm = lambda a: pltpu.HBM(a.shape, a.dtype)
    outs = pl.pallas_call(
        body, name=name, out_shape=tuple(hbm(a) for a in list(srcs) + list(lands)),
        in_specs=[_HBM] * (2 * n) + [_SEM, _SEM, pl.BlockSpec(memory_space=pl.ANY)],
        out_specs=tuple([_HBM] * (2 * n)), input_output_aliases={a: a for a in range(2 * n)},
        compiler_params=pltpu.CompilerParams(has_side_effects=_EFFECT))(*srcs, *lands, send_sems, recv_sems, after)
    return list(outs[:n]), list(outs[n:])


def _whole(a, peer_idx, shape_only=False):
    return a.shape if shape_only else a


def _slot(a, peer_idx, shape_only=False):
    return a.shape[1:] if shape_only else a.at[peer_idx]


def _sum_slots(a, *, name, tr=None):
    rows, cols = a.shape[1], a.shape[2]
    if tr is None:
        tr = rows
        while N_DEV * tr * cols * a.dtype.itemsize > 3 * 2 ** 20 and tr % 32 == 0:
            tr //= 2

    def body(a_ref, o_ref):
        acc = a_ref[0].astype(F32)
        for j in range(1, N_DEV):
            acc = acc + a_ref[j].astype(F32)
        o_ref[...] = acc

    return _pc(body, name=name, out_shape=_sds((rows, cols), F32), grid=(rows // tr,),
               in_specs=[pl.BlockSpec((N_DEV, tr, cols), lambda i: (0, i, 0))],
               out_specs=pl.BlockSpec((tr, cols), lambda i: (i, 0)), semantics=("arbitrary",), vmem=VMEM_BIG)(a)


def _to_slots(full, kind):
    if kind == "rows2":
        r, c = full.shape
        return full.reshape(N_DEV, r // N_DEV, c)
    if kind == "cols2":
        r, c = full.shape
        return full.reshape(r, N_DEV, c // N_DEV).transpose(1, 0, 2)
    if kind == "rows3":
        l, r, c = full.shape
        return full.reshape(l, N_DEV, r // N_DEV, c).transpose(1, 0, 2, 3)
    if kind == "cols3":
        l, r, c = full.shape
        return full.reshape(l, r, N_DEV, c // N_DEV).transpose(2, 0, 1, 3)
    raise ValueError(kind)


def _from_slots(slots, kind):
    if kind == "rows2":
        _, r, c = slots.shape
        return slots.reshape(N_DEV * r, c)
    if kind == "cols2":
        _, r, c = slots.shape
        return slots.transpose(1, 0, 2).reshape(r, N_DEV * c)
    if kind == "rows3":
        _, l, r, c = slots.shape
        return slots.transpose(1, 0, 2, 3).reshape(l, N_DEV * r, c)
    if kind == "cols3":
        _, l, r, c = slots.shape
        return slots.transpose(1, 2, 0, 3).reshape(l, r, N_DEV * c)
    raise ValueError(kind)


BIG = (("w_in_a", "rows2"), ("w_in_b", "rows2"), ("w_kv", "cols2"), ("w_memkv", "rows3"),
       ("w_out", "rows3"), ("w_up", "cols3"), ("w_down", "rows3"))


def _round_up(n, m):
    return -(-n // m) * m


def _pad_rows(a, rows, axis):
    pad = [(0, 0)] * a.ndim
    pad[axis] = (0, rows - a.shape[axis])
    return jnp.pad(a, pad)


def kernel(x, mem, ln_mix_g, w_in_a, b_f_a, w_in_b, ln_kv_g, w_kv, ln_mem_g, w_memkv, w_out, ln_ffn_g, w_up, conv_w, conv_b, w_down, final_g, loss_target, m_ln_mix_g, m_w_in_a, m_b_f_a, m_w_in_b, m_ln_kv_g, m_w_kv, m_ln_mem_g, m_w_memkv, m_w_out, m_ln_ffn_g, m_w_up, m_conv_w, m_conv_b, m_w_down, m_final_g, v_ln_mix_g, v_w_in_a, v_b_f_a, v_w_in_b, v_ln_kv_g, v_w_kv, v_ln_mem_g, v_w_memkv, v_w_out, v_ln_ffn_g, v_w_up, v_conv_w, v_conv_b, v_w_down, v_final_g):
    B, S, D = x.shape
    NM = mem.shape[1]
    T = B * S
    F = w_down.shape[1] * N_DEV
    my_idx = 4 * lax.axis_index("x") + 2 * lax.axis_index("y") + lax.axis_index("c")

    shards = {"w_in_a": w_in_a[0], "w_in_b": w_in_b[0], "w_kv": w_kv, "w_memkv": w_memkv, "w_out": w_out,
              "w_up": w_up, "w_down": w_down}
    moms = {"w_in_a": (m_w_in_a[0], v_w_in_a[0]), "w_in_b": (m_w_in_b[0], v_w_in_b[0]), "w_kv": (m_w_kv, v_w_kv),
            "w_memkv": (m_w_memkv, v_w_memkv), "w_out": (m_w_out, v_w_out), "w_up": (m_w_up, v_w_up),
            "w_down": (m_w_down, v_w_down)}

    groups = [("a1", [("w_in_a", None)]),
              ("a2", [("w_memkv", None), ("w_out", None), ("conv_w", None)]),
              ("b0", [("w_up", 0), ("w_down", 0)]), ("a3", [("w_in_b", None), ("w_kv", None)]),
              ("b1", [("w_up", 1), ("w_down", 1)])]
    sources = dict(shards, conv_w=conv_w)
    srcs, span = [], {}
    for gname, members in groups:
        span[gname] = (len(srcs), len(members))
        for n, layer in members:
            a = sources[n] if layer is None else sources[n][layer]
            srcs.append(a if n == "conv_w" else a.astype(BF16))
    g_ssem, g_rsem, g_thru, g_lands, token = _scatter_start(srcs, _whole, name="gather_start")

    def gathered(gname, after):
        lo, cnt = span[gname]
        thru, lands = _scatter_wait(g_ssem, g_rsem, g_thru[lo:lo + cnt], g_lands[lo:lo + cnt], _whole, after,
                                    name=f"gather_wait_{gname}", first=lo)
        return [lax.dynamic_update_index_in_dim(land, s, my_idx, 0) for land, s in zip(lands, thru)]

    full = {}
    (g_wa,) = gathered("a1", token)
    full["w_in_a"] = _from_slots(g_wa, "rows2")

    wa = full["w_in_a"]
    n_qkv = 3 * MAIN_W
    wa = jnp.concatenate([wa[:, :n_qkv], wa[:, n_qkv + N_MAIN_HEADS:], wa[:, n_qkv:n_qkv + N_MAIN_HEADS],
                          jnp.zeros((D, LANES - N_MAIN_HEADS), BF16)], axis=1)
    n_main = n_qkv + MEM_W
    full["w_up"], full["w_down"] = {}, {}
    b_f =_pad_rows(b_f_a.reshape(1, N_MAIN_HEADS), LANES, 1)

    x2d = x.reshape(T, D)
    mem2d = mem.reshape(B * NM, D)
    tgt2d = loss_target.reshape(T, D)
    PM, PX = N_MAIN_HEADS // 2, N_MEM_HEADS // 2

    def stats_to_heads(c2d):
        c = c2d.reshape(B, S, LANES)[:, :, :N_MAIN_HEADS].transpose(0, 2, 1)
        return c[:, :, None, :]

    def mem_kv(layer):
        return _mm_fwd(mem2d, full["w_memkv"][layer], name=f"memkv{layer}", tm=B * NM, tn=2 * MEM_W,
                       out_dtype=BF16, g=ln_mem_g[layer], save_h=True)

    def conv_ffn_fwd(xin, layer):
        uc, ub, a, h = _ffn_up_gate(xin, ln_ffn_g[layer], full["w_up"][layer], conv_w_full[layer],
                                    conv_b[layer].reshape(1, 2 * F), name=f"ffn_up{layer}", S=S)
        xo = _mm_fwd(a, full["w_down"][layer], name=f"ffn_down{layer}", tm=min(1024, T), tn=1024, out_dtype=F32, res=xin)
        return xo, (uc, ub, h, a)

    proj_a, h_mix0 = _mm_fwd(x2d, wa, name="in_proj_a", tm=min(1024, T), tn=1280, out_dtype=BF16, g=ln_mix_g[0],
                             ncols=n_main, save_h=True)
    f_logit = _mm_fwd(x2d, wa, name="in_proj_f", tm=min(1024, T), tn=LANES, out_dtype=F32, g=ln_mix_g[0],
                      col0=n_main // LANES, ncols=LANES)
    c2d = _forget_cumsum(f_logit, b_f, B=B, S=S, name="forget_cumsum")
    cr = stats_to_heads(c2d)
    o_main0, lse0 = _fox_fwd_g(proj_a, proj_a, proj_a, cr, name="fox_fwd", B=B, S=S, P=PM, q_cb=0, k_cb=PM, v_cb=2 * PM,
                               G=FWD_HEAD_GROUP)
    g_wmem, g_wout, g_cw = gathered("a2", lse0)
    full["w_memkv"] = _from_slots(g_wmem, "rows3")
    full["w_out"] = _from_slots(g_wout, "rows3")
    conv_w_full = _from_slots(g_cw, "cols3")
    memkv0, h_mem0 = mem_kv(0)
    o_mem0, lse_m0 = _mem_fwd(proj_a, memkv0, name="mem_fwd0", B=B, S=S, NM=NM, q_cb=3 * PM)
    o_cat0 = jnp.concatenate([o_main0, o_mem0], axis=1)
    x1 = _mm_fwd(o_cat0, full["w_out"][0], name="out_proj0", tm=min(1024, T), tn=1024, out_dtype=F32, res=x2d)
    g_up, g_dn = gathered("b0", x1)
    full["w_up"][0], full["w_down"][0] = _from_slots(g_up, "cols2"), _from_slots(g_dn, "rows2")
    x2, ffn_saved0 = conv_ffn_fwd(x1, 0)
    g_wb, g_wkv = gathered("a3", x2)
    wb, wkv = _from_slots(g_wb, "rows2"), _from_slots(g_wkv, "cols2")
    kv, h_kv =_mm_fwd(x2, wkv, name="kv_proj", tm=min(1024, T), tn=1536, out_dtype=BF16, g=ln_kv_g, save_h=True)
    proj_b, h_mix1 = _mm_fwd(x2, wb, name="in_proj_b", tm=min(1024, T), tn=1024, out_dtype=BF16, g=ln_mix_g[1],
                             save_h=True)
    o_main1, rt1 = _sb_fwd_g(proj_b, kv, kv, name="sb_fwd", B=B, S=S, P=PM, q_cb=0, k_cb=0, v_cb=PM,
                             G=FWD_HEAD_GROUP)
    memkv1, h_mem1 = mem_kv(1)
    o_mem1, lse_m1 = _mem_fwd(proj_b, memkv1, name="mem_fwd1", B=B, S=S, NM=NM, q_cb=PM)
    o_cat1 = jnp.concatenate([o_main1, o_mem1], axis=1)
    x3 = _mm_fwd(o_cat1, full["w_out"][1], name="out_proj1", tm=min(1024, T), tn=1024, out_dtype=F32, res=x2)
    g_up, g_dn = gathered("b1", x3)
    full["w_up"][1], full["w_down"][1] = _from_slots(g_up, "cols2"), _from_slots(g_dn, "rows2")
    x4, ffn_saved1 = conv_ffn_fwd(x3, 1)
    dx4, dg_final, loss_part = _loss_head(x4, final_g, tgt2d, name="loss_head")

    grads = {}
    small = {}
    reduce_groups = []

    def start_reduce(gname, keys, kinds):
        slots = [_to_slots(grads[k], kind) for k, kind in zip(keys, kinds)]
        ssem, rsem, thru, lands, tok = _scatter_start(slots, _slot, name=f"reduce_start_{gname}")
        reduce_groups.append((gname, keys, ssem, rsem, thru, lands))
        return tok[0, 0]

    def conv_ffn_bwd(dxo, xin, saved, layer):
        uc, ub, h, a = saved
        w_dn = full["w_down"][layer]
        da = _mm_nt(dxo, w_dn, name=f"d_act{layer}", tm=min(1024, T), tn=F // 2, out_dtype=F32)
        grads[("w_down", layer)] = _wgrad(a, dxo, f"g_w_down{layer}")
        cwl = conv_w_full[layer]
        du_g, du_v, p_g, p_v = _conv_gate_bwd(da, uc, ub, cwl, name=f"conv_bwd{layer}", B=B, S=S)
        small[("conv_w", layer)] = jnp.concatenate([p_g[0:3], p_v[0:3]], axis=1)
        small[("conv_b", layer)] = jnp.concatenate([p_g[3], p_v[3]], axis=0)
        grads[("w_up", layer)] = jnp.concatenate(
            [_wgrad(h, du_g, f"g_w_up_gate{layer}"), _wgrad(h, du_v, f"g_w_up_val{layer}")], axis=1)
        tok = start_reduce(f"ffn{layer}", [("w_down", layer), ("w_up", layer)], ["rows2", "cols2"])
        dxi, dg = _mm_nt_rmsbwd([(du_g, 0), (du_v, 1)], full["w_up"][layer], xin, ln_ffn_g[layer] + tok,
                                name=f"d_ffn_in{layer}", dres=dxo)
        small[("ln_ffn_g", layer)] = dg[0]
        return dxi

    def mem_bwd(proj, q_cb, memkv, h_mem, do_cat, o_mem, lse_m, layer):
        dqm, dmk, dmv = _mem_bwd(proj, memkv, do_cat, o_mem, lse_m, name=f"mem_bwd{layer}", B=B, S=S, NM=NM,
                                 q_cb=q_cb, do_cb=PM)
        grads[("w_memkv", layer)] = jnp.concatenate(
            [_wgrad(h_mem, dmk, f"g_w_memk{layer}"), _wgrad(h_mem, dmv, f"g_w_memv{layer}")], axis=1)
        _, dg = _mm_nt_rmsbwd([(dmk, 0), (dmv, 1)], full["w_memkv"][layer], mem2d, ln_mem_g[layer],
                              name=f"d_mem_in{layer}", want_dx=False)
        small[("ln_mem_g", layer)] = dg[0]
        return dqm

    dx3 = conv_ffn_bwd(dx4, x3, ffn_saved1, 1)
    do_cat1 = _mm_nt(dx3, full["w_out"][1], name="d_o_cat1", tm=min(1024, T), tn=1024, out_dtype=BF16)
    grads[("w_out", 1)] = _wgrad(o_cat1, dx3, "g_w_out1")
    dq1, dk1, dv1 = _sb_bwd_g(proj_b, kv, kv, do_cat1, rt1, name="sb_bwd", B=B, S=S, P=PM, q_cb=0, k_cb=0, v_cb=PM,
                            do_cb=0)
    dqm1 = mem_bwd(proj_b, PM, memkv1, h_mem1, do_cat1, o_mem1, lse_m1, 1)
    grads["w_in_b"] = jnp.concatenate([_wgrad(h_mix1, dq1, "g_w_in_b_q"), _wgrad(h_mix1, dqm1, "g_w_in_b_m")], axis=1)
    grads["w_kv"] = jnp.concatenate([_wgrad(h_kv, dk1, "g_w_kv_k"), _wgrad(h_kv, dv1, "g_w_kv_v")], axis=1)
    tok = start_reduce("mix1", [("w_out", 1), "w_in_b", "w_kv", ("w_memkv", 1)], ["rows2", "rows2", "cols2", "rows2"])
    dx2, dg = _mm_nt_rmsbwd([(dq1, 0), (dqm1, MAIN_W // MEM_W)], wb, x2, ln_mix_g[1] + tok, name="d_mix_in1", dres=dx3)
    small[("ln_mix_g", 1)] = dg[0]
    dx2, dg = _mm_nt_rmsbwd([(dk1, 0), (dv1, 1)], wkv, x2, ln_kv_g, name="d_kv_in", dres=dx2)
    small["ln_kv_g"] = dg[0]
    dx1 = conv_ffn_bwd(dx2, x1, ffn_saved0, 0)
    do_cat0 = _mm_nt(dx1, full["w_out"][0], name="d_o_cat0", tm=min(1024, T), tn=1024, out_dtype=BF16)
    grads[("w_out", 0)] = _wgrad(o_cat0, dx1, "g_w_out0")
    dq0, dk0, dv0, dcs = _fox_bwd_g(proj_a, proj_a, proj_a, do_cat0, lse0, cr, name="fox_bwd", B=B, S=S, P=PM, q_cb=0,
                                  k_cb=PM, v_cb=2 * PM, do_cb=0)
    dqm0 = mem_bwd(proj_a, 3 * PM, memkv0, h_mem0, do_cat0, o_mem0, lse_m0, 0)
    dc2d = _pad_rows(dcs[:, :, 0, :].transpose(0, 2, 1).reshape(T, N_MAIN_HEADS), LANES, 1)
    df, db_f = _forget_cumsum_bwd(dc2d, f_logit, b_f, B=B, S=S, name="forget_cumsum_bwd")
    a_parts = [(dq0, 0), (dk0, 1), (dv0, 2), (dqm0, n_qkv // MEM_W), (df, n_main // LANES)]
    g_wa = jnp.concatenate([_wgrad(h_mix0, p, f"g_w_in_a{k}") for k, (p, _) in enumerate(a_parts)], axis=1)
    grads["w_in_a"] = jnp.concatenate([g_wa[:, :n_qkv], g_wa[:, n_main:n_main + N_MAIN_HEADS], g_wa[:, n_qkv:n_main]],
                                      axis=1)
    tok = start_reduce("mix0", [("w_out", 0), ("w_memkv", 0), "w_in_a"], ["rows2", "rows2", "rows2"])
    dx0, dg = _mm_nt_rmsbwd(a_parts, wa, x2d, ln_mix_g[0] + tok, name="d_mix_in0", dres=dx1)
    small[("ln_mix_g", 0)] = dg[0]
    grad_x = dx0.reshape(B, S, D)

    def both_small(name):
        return jnp.stack([small[(name, 0)], small[(name, 1)]])

    small_list = [("ln_mix_g", both_small("ln_mix_g")), ("b_f_a", db_f[:, :N_MAIN_HEADS]), ("ln_kv_g", small["ln_kv_g"]),
                  ("ln_mem_g", both_small("ln_mem_g")), ("ln_ffn_g", both_small("ln_ffn_g")),
                  ("conv_w", both_small("conv_w")), ("conv_b", both_small("conv_b")), ("final_g", dg_final[0]),
                  ("loss", loss_part[0, :1])]
    sm_rows = []
    for _, a in small_list:
        flat = a.reshape(-1)
        sm_rows.append(_pad_rows(flat, _round_up(flat.size, 8 * LANES), 0).reshape(-1, LANES))
    spack = jnp.concatenate(sm_rows, axis=0)
    s_ssem, s_rsem, s_thru, s_lands, s_tok = _scatter_start([spack], _whole, name="small_start")

    pieces = {}
    for gname, keys, ssem, rsem, thru, lands in reduce_groups:
        thru, lands = _scatter_wait(ssem, rsem, thru, lands, _slot, s_tok, name=f"reduce_wait_{gname}")
        for key, mine, land in zip(keys, thru, lands):
            own = lax.dynamic_index_in_dim(mine, my_idx, 0, keepdims=False)
            land = lax.dynamic_update_index_in_dim(land, own, my_idx, 0)
            tag = key if isinstance(key, str) else f"{key[0]}{key[1]}"
            pieces[key] = _sum_slots(land, name=f"sum_{tag}")

    red = {}
    for n in ("w_in_a", "w_in_b", "w_kv"):
        red[n] = pieces[n].reshape(shards[n].shape)
    for n in ("w_memkv", "w_out", "w_up", "w_down"):
        red[n] = jnp.stack([pieces[(n, 0)], pieces[(n, 1)]])

    weights = {"ln_mix_g": ln_mix_g, "w_in_a": w_in_a, "b_f_a": b_f_a, "w_in_b": w_in_b, "ln_kv_g": ln_kv_g,
               "w_kv": w_kv, "ln_mem_g": ln_mem_g, "w_memkv": w_memkv, "w_out": w_out, "ln_ffn_g": ln_ffn_g,
               "w_up": w_up, "conv_w": conv_w, "conv_b": conv_b, "w_down": w_down, "final_g": final_g}
    m_in = {"ln_mix_g": m_ln_mix_g, "w_in_a": m_w_in_a, "b_f_a": m_b_f_a, "w_in_b": m_w_in_b, "ln_kv_g": m_ln_kv_g,
            "w_kv": m_w_kv, "ln_mem_g": m_ln_mem_g, "w_memkv": m_w_memkv, "w_out": m_w_out, "ln_ffn_g": m_ln_ffn_g,
            "w_up": m_w_up, "conv_w": m_conv_w, "conv_b": m_conv_b, "w_down": m_w_down, "final_g": m_final_g}
    v_in = {"ln_mix_g": v_ln_mix_g, "w_in_a": v_w_in_a, "b_f_a": v_b_f_a, "w_in_b": v_w_in_b, "ln_kv_g": v_ln_kv_g,
            "w_kv": v_w_kv, "ln_mem_g": v_ln_mem_g, "w_memkv": v_w_memkv, "w_out": v_w_out, "ln_ffn_g": v_ln_ffn_g,
            "w_up": v_w_up, "conv_w": v_conv_w, "conv_b": v_conv_b, "w_down": v_w_down, "final_g": v_final_g}
    order = list(weights)
    big_names = [n for n, _ in BIG]
    g_out, d_out, nm_out, nv_out = {}, {}, {}, {}

    def update(n):
        w = weights[n]
        cols = w.shape[-1]
        g = red[n].reshape(w.shape)
        d, nm, nv = _adamw(w.reshape(-1, cols), g.reshape(-1, cols), m_in[n].reshape(-1, cols),
                           v_in[n].reshape(-1, cols), name=f"adamw_{n}")
        g_out[n], d_out[n], nm_out[n], nv_out[n] = g, d.reshape(w.shape), nm.reshape(w.shape), nv.reshape(w.shape)

    for n in big_names:
        update(n)
    all_updated = jnp.stack([d_out[n].reshape(-1)[0] for n in big_names])
    s_thru, s_lands = _scatter_wait(s_ssem, s_rsem, s_thru, s_lands, _whole, all_updated, name="small_wait")
    ssum = _sum_slots(lax.dynamic_update_index_in_dim(s_lands[0], s_thru[0], my_idx, 0), name="sum_small")
    off = 0
    for (n, a), rows in zip(small_list, sm_rows):
        red[n] = ssum[off:off + rows.shape[0]].reshape(-1)[:a.size].reshape(a.shape)
        off += rows.shape[0]
    loss = red["loss"][0]
    shard_cols = conv_w.shape[2]
    red["conv_w"] = lax.dynamic_slice_in_dim(red["conv_w"], my_idx * shard_cols, shard_cols, axis=2)
    red["b_f_a"] = red["b_f_a"].reshape(b_f_a.shape)
    update("conv_w")
    small_names = [n for n in order if n not in g_out]

    def pack_small(src):
        rows = []
        for n in small_names:
            flat = src[n].reshape(-1)
            rows.append(_pad_rows(flat, _round_up(flat.size, 8 * LANES), 0).reshape(-1, LANES))
        return jnp.concatenate(rows, axis=0), [r.shape[0] for r in rows]

    red_small = {n: red[n].reshape(weights[n].shape) for n in small_names}
    wp, counts = pack_small(weights)
    gp, _ = pack_small(red_small)
    mp, _ = pack_small(m_in)
    vp, _ = pack_small(v_in)
    dp, nmp, nvp = _adamw(wp, gp, mp, vp, name="adamw_small")
    off = 0
    for n, cnt in zip(small_names, counts):
        shp = weights[n].shape
        size = weights[n].size
        g_out[n] = red_small[n]
        d_out[n] = dp[off:off + cnt].reshape(-1)[:size].reshape(shp)
        nm_out[n] = nmp[off:off + cnt].reshape(-1)[:size].reshape(shp)
        nv_out[n] = nvp[off:off + cnt].reshape(-1)[:size].reshape(shp)
        off += cnt

    return (loss, grad_x, *[g_out[n] for n in order], *[d_out[n] for n in order],
            *[nm_out[n] for n in order], *[nv_out[n] for n in order])
```

```python
import functools

import jax
import jax.numpy as jnp
from jax import lax
from jax.experimental import pallas as pl
from jax.experimental.pallas import tpu as pltpu

F32 = jnp.float32
BF16 = jnp.bfloat16
LANES = 128
HEAD_DIM = 64
N_MAIN_HEADS = 12
N_MEM_HEADS = 4
MAIN_W = N_MAIN_HEADS * HEAD_DIM
MEM_W = N_MEM_HEADS * HEAD_DIM
SCALE = HEAD_DIM ** -0.5
EPS = 1e-6
NEG = -1e30
N_DEV = 8
ATT_TILE = 256
MEM_Q_TILE = 1024
VMEM_BIG = 56 * 2 ** 20
MESH = pl.DeviceIdType.MESH

ADAM_LR = 0.001
ADAM_B1 = 0.9
ADAM_B2 = 0.999
ADAM_EPS = 1e-08
ADAM_WD = 0.01
ADAM_STEP = 10

NT = (((1,), (1,)), ((), ()))
TN = (((0,), (0,)), ((), ()))


def _pc(body, *, name, out_shape, grid=None, in_specs=None, out_specs=None, scratch_shapes=(),
        semantics=None, vmem=None):
    kw = {}
    if grid is not None:
        kw["grid"] = grid
    params = pltpu.CompilerParams(dimension_semantics=semantics, vmem_limit_bytes=vmem)
    return pl.pallas_call(body, name=name, out_shape=out_shape, in_specs=in_specs, out_specs=out_specs,
                          scratch_shapes=list(scratch_shapes), compiler_params=params, **kw)


def _sds(shape, dtype):
    return jax.ShapeDtypeStruct(shape, dtype)


def _mm_fwd(a, w, *, name, tm, tn, out_dtype, g=None, res=None, col0=0, ncols=None, save_h=False):
    m_rows, k = a.shape
    n = w.shape[1] if ncols is None else ncols
    grid = (m_rows // tm, n // tn)
    norm = g is not None

    def body(*refs):
        refs = list(refs)
        a_ref = refs.pop(0)
        g_ref = refs.pop(0) if norm else None
        w_ref = refs.pop(0)
        res_ref = refs.pop(0) if res is not None else None
        o_ref = refs.pop(0)
        hout_ref = refs.pop(0) if save_h else None
        h_ref = refs.pop(0) if norm else None
        if norm:
            @pl.when(pl.program_id(1) == 0)
            def _():
                xv = a_ref[...]
                r = lax.rsqrt(jnp.mean(xv * xv, axis=-1, keepdims=True) + EPS)
                h = ((xv * r) * g_ref[...]).astype(BF16)
                h_ref[...] = h
                if save_h:
                    hout_ref[...] = h
            lhs = h_ref[...]
        else:
            lhs = a_ref[...].astype(BF16)
        acc = jnp.dot(lhs, w_ref[...], preferred_element_type=F32)
        if res is not None:
            acc = acc + res_ref[...]
        o_ref[...] = acc.astype(out_dtype)

    in_specs = [pl.BlockSpec((tm, k), lambda i, j: (i, 0))]
    args = [a]
    if norm:
        in_specs.append(pl.BlockSpec((1, k), lambda i, j: (0, 0)))
        args.append(g.reshape(1, k))
    in_specs.append(pl.BlockSpec((k, tn), lambda i, j: (0, j + col0)))
    args.append(w)
    if res is not None:
        in_specs.append(pl.BlockSpec((tm, tn), lambda i, j: (i, j)))
        args.append(res)
    out_shape = [_sds((m_rows, n), out_dtype)]
    out_specs = [pl.BlockSpec((tm, tn), lambda i, j: (i, j))]
    if save_h:
        out_shape.append(_sds((m_rows, k), BF16))
        out_specs.append(pl.BlockSpec((tm, k), lambda i, j: (i, 0)))
    scratch = [pltpu.VMEM((tm, k), BF16)] if norm else []
    outs = _pc(body, name=name, out_shape=out_shape, grid=grid, in_specs=in_specs, out_specs=out_specs,
               scratch_shapes=scratch, semantics=("arbitrary", "arbitrary"), vmem=VMEM_BIG)(*args)
    return outs if save_h else outs[0]


def _mm_nt(a, w, *, name, tm, tn, out_dtype):
    m_rows, k = a.shape
    n = w.shape[0]

    def body(a_ref, w_ref, o_ref):
        acc = lax.dot_general(a_ref[...].astype(BF16), w_ref[...], NT, preferred_element_type=F32)
        o_ref[...] = acc.astype(out_dtype)

    return _pc(body, name=name, out_shape=_sds((m_rows, n), out_dtype), grid=(m_rows // tm, n // tn),
               in_specs=[pl.BlockSpec((tm, k), lambda i, j: (i, 0)), pl.BlockSpec((tn, k), lambda i, j: (j, 0))],
               out_specs=pl.BlockSpec((tm, tn), lambda i, j: (i, j)),
               semantics=("arbitrary", "arbitrary"), vmem=VMEM_BIG)(a, w)


def _mm_tn(a, b, *, name, ta, tn, tt):
    t_rows, ka = a.shape
    n = b.shape[1]
    nt = t_rows // tt

    def body(a_ref, b_ref, o_ref, acc_ref):
        t = pl.program_id(2)

        @pl.when(t == 0)
        def _():
            acc_ref[...] = jnp.zeros_like(acc_ref)

        acc_ref[...] += lax.dot_general(a_ref[...].astype(BF16), b_ref[...].astype(BF16), TN,
                                        preferred_element_type=F32)

        @pl.when(t == nt - 1)
        def _():
            o_ref[...] = acc_ref[...].astype(BF16)

    return _pc(body, name=name, out_shape=_sds((ka, n), BF16), grid=(ka // ta, n // tn, nt),
               in_specs=[pl.BlockSpec((tt, ta), lambda i, j, t: (t, i)),
                         pl.BlockSpec((tt, tn), lambda i, j, t: (t, j))],
               out_specs=pl.BlockSpec((ta, tn), lambda i, j, t: (i, j)),
               scratch_shapes=[pltpu.VMEM((ta, tn), F32)],
               semantics=("arbitrary", "arbitrary", "arbitrary"), vmem=VMEM_BIG)(a, b)


def _wgrad(a, b, name):
    t_rows, ka = a.shape
    n = b.shape[1]
    ta = ka if ka <= 1024 else ka // 2
    tn = n
    while ta * tn * 4 > 6 * 2 ** 20 and tn % 256 == 0:
        tn //= 2
    tt = min(2048, t_rows)
    return _mm_tn(a, b, name=name, ta=ta, tn=tn, tt=tt)


def _mm_nt_rmsbwd(parts, w, x, g, *, name, dres=None, want_dx=True):
    m_rows, d = x.shape
    k_total = sum(dy.shape[1] for dy, _ in parts)
    tm = min(512 if k_total <= 2816 else 256, m_rows)
    n_parts = len(parts)

    def body(*refs):
        refs = list(refs)
        dy_refs = [refs.pop(0) for _ in range(n_parts)]
        w_refs = [refs.pop(0) for _ in range(n_parts)]
        x_ref = refs.pop(0)
        g_ref = refs.pop(0)
        dres_ref = refs.pop(0) if dres is not None else None
        dx_ref = refs.pop(0) if want_dx else None
        dg_ref = refs.pop(0)

        @pl.when(pl.program_id(0) == 0)
        def _():
            dg_ref[...] = jnp.zeros_like(dg_ref)

        dh = None
        for dy_ref, w_ref in zip(dy_refs, w_refs):
            t = lax.dot_general(dy_ref[...].astype(BF16), w_ref[...], NT, preferred_element_type=F32)
            dh = t if dh is None else dh + t
        xv = x_ref[...]
        r = lax.rsqrt(jnp.mean(xv * xv, axis=-1, keepdims=True) + EPS)
        xh = xv * r
        dg_ref[...] += jnp.sum(dh * xh, axis=0, keepdims=True)
        if want_dx:
            dhg = dh * g_ref[...]
            dx = r * (dhg - xh * jnp.mean(dhg * xh, axis=-1, keepdims=True))
            if dres is not None:
                dx = dx + dres_ref[...]
            dx_ref[...] = dx

    in_specs, args = [], []
    for dy, _ in parts:
        in_specs.append(pl.BlockSpec((tm, dy.shape[1]), lambda i: (i, 0)))
        args.append(dy)
    for dy, cb in parts:
        in_specs.append(pl.BlockSpec((d, dy.shape[1]), functools.partial(lambda i, cb: (0, cb), cb=cb)))
        args.append(w)
    in_specs += [pl.BlockSpec((tm, d), lambda i: (i, 0)), pl.BlockSpec((1, d), lambda i: (0, 0))]
    args += [x, g.reshape(1, d)]
    if dres is not None:
        in_specs.append(pl.BlockSpec((tm, d), lambda i: (i, 0)))
        args.append(dres)
    out_shape, out_specs = [], []
    if want_dx:
        out_shape.append(_sds((m_rows, d), F32))
        out_specs.append(pl.BlockSpec((tm, d), lambda i: (i, 0)))
    out_shape.append(_sds((1, d), F32))
    out_specs.append(pl.BlockSpec((1, d), lambda i: (0, 0)))
    outs = _pc(body, name=name, out_shape=out_shape, grid=(m_rows // tm,), in_specs=in_specs,
               out_specs=out_specs, semantics=("arbitrary",), vmem=VMEM_BIG)(*args)
    return (outs[0], outs[1]) if want_dx else (None, outs[0])


def _loss_head(x, g, tgt, *, name):
    m_rows, d = x.shape
    tm = min(512, m_rows)

    def body(x_ref, g_ref, t_ref, dx_ref, dg_ref, loss_ref):
        @pl.when(pl.program_id(0) == 0)
        def _():
            dg_ref[...] = jnp.zeros_like(dg_ref)
            loss_ref[...] = jnp.zeros_like(loss_ref)

        xv = x_ref[...]
        r = lax.rsqrt(jnp.mean(xv * xv, axis=-1, keepdims=True) + EPS)
        xh = xv * r
        gv = g_ref[...]
        err = xh * gv - t_ref[...]
        per_tok = jnp.mean(err * err, axis=-1, keepdims=True)
        loss_ref[...] += 0.5 * jnp.sum(per_tok, axis=0, keepdims=True)
        dout = err * (1.0 / d)
        dg_ref[...] += jnp.sum(dout * xh, axis=0, keepdims=True)
        dhg = dout * gv
        dx_ref[...] = r * (dhg - xh * jnp.mean(dhg * xh, axis=-1, keepdims=True))

    row = pl.BlockSpec((tm, d), lambda i: (i, 0))
    return _pc(body, name=name, out_shape=[_sds((m_rows, d), F32), _sds((1, d), F32), _sds((1, LANES), F32)],
               grid=(m_rows // tm,), in_specs=[row, pl.BlockSpec((1, d), lambda i: (0, 0)), row],
               out_specs=[row, pl.BlockSpec((1, d), lambda i: (0, 0)), pl.BlockSpec((1, LANES), lambda i: (0, 0))],
               semantics=("arbitrary",))(x, g.reshape(1, d), tgt)


def _split3(v):
    hi = v.astype(BF16)
    r1 = v - hi.astype(F32)
    mid = r1.astype(BF16)
    lo = (r1 - mid.astype(F32)).astype(BF16)
    return hi, mid, lo


def _split2(v):
    hi = v.astype(BF16)
    lo = (v - hi.astype(F32)).astype(BF16)
    return hi, lo


def _tri_dot3(tri, v):
    hi, mid, lo = _split3(v)
    return (jnp.dot(tri, hi, preferred_element_type=F32) + jnp.dot(tri, mid, preferred_element_type=F32)
            + jnp.dot(tri, lo, preferred_element_type=F32))


def _log_sigmoid(v):
    return jnp.minimum(v, 0.0) - jnp.log(1.0 + jnp.exp(-jnp.abs(v)))


def _forget_cumsum(f_logit, b_f, *, B, S, name):
    ch = min(256, S)
    nch = S // ch

    def body(f_ref, b_ref, c_ref):
        r_i = lax.broadcasted_iota(jnp.int32, (ch, ch), 0)
        c_i = lax.broadcasted_iota(jnp.int32, (ch, ch), 1)
        tri = (c_i <= r_i).astype(BF16)
        bv = b_ref[...]

        def step(k, carry):
            rows = pl.ds(pl.multiple_of(k * ch, ch), ch)
            lf = _log_sigmoid(f_ref[rows, :] + bv)
            c_ref[rows, :] = _tri_dot3(tri, lf) + carry
            return carry + jnp.sum(lf, axis=0, keepdims=True)

        lax.fori_loop(0, nch, step, jnp.zeros((1, LANES), F32))

    blk = pl.BlockSpec((S, LANES), lambda b: (b, 0))
    return _pc(body, name=name, out_shape=_sds((B * S, LANES), F32), grid=(B,),
               in_specs=[blk, pl.BlockSpec((1, LANES), lambda b: (0, 0))], out_specs=blk,
               semantics=("arbitrary",))(f_logit, b_f)


def _forget_cumsum_bwd(dc, f_logit, b_f, *, B, S, name):
    ch = min(256, S)
    nch = S // ch

    def body(dc_ref, f_ref, b_ref, df_ref, db_ref):
        @pl.when(pl.program_id(0) == 0)
        def _():
            db_ref[...] = jnp.zeros_like(db_ref)

        r_i = lax.broadcasted_iota(jnp.int32, (ch, ch), 0)
        c_i = lax.broadcasted_iota(jnp.int32, (ch, ch), 1)
        tri = (c_i >= r_i).astype(BF16)
        bv = b_ref[...]

        def step(kk, carry):
            tail, dbs = carry
            k = nch - 1 - kk
            rows = pl.ds(pl.multiple_of(k * ch, ch), ch)
            dcv = dc_ref[rows, :]
            dlf = _tri_dot3(tri, dcv) + tail
            z = f_ref[rows, :] + bv
            df = dlf * (1.0 / (1.0 + jnp.exp(z)))
            df_ref[rows, :] = df.astype(BF16)
            return tail + jnp.sum(dcv, axis=0, keepdims=True), dbs + jnp.sum(df, axis=0, keepdims=True)

        zero = jnp.zeros((1, LANES), F32)
        _, dbs = lax.fori_loop(0, nch, step, (zero, zero))
        db_ref[...] += dbs

    blk = pl.BlockSpec((S, LANES), lambda b: (b, 0))
    one = pl.BlockSpec((1, LANES), lambda b: (0, 0))
    return _pc(body, name=name, out_shape=[_sds((B * S, LANES), BF16), _sds((1, LANES), F32)], grid=(B,),
               in_specs=[blk, blk, one], out_specs=[blk, one], semantics=("arbitrary",))(dc, f_logit, b_f)


def _head_mask(lane, hh):
    return (lane < HEAD_DIM) if hh == 0 else (lane >= HEAD_DIM)


HEAD_GROUP = 3
FWD_HEAD_GROUP = 6


def _g_col_spec(rows, nblk_rows, cb, G):
    return pl.BlockSpec((rows, G * LANES), lambda b, p, i: (b * nblk_rows + i, cb // G + p))


def _g_kv_spec(rows, cb, G):
    return pl.BlockSpec((rows, G * LANES), lambda b, p, i: (b, cb // G + p))


def _g_stat_col_spec(tq, G):
    return pl.BlockSpec((1, 2 * G, tq, 1), lambda b, p, i: (b, p, i, 0))


def _g_stat_row_spec(S, G):
    return pl.BlockSpec((1, 2 * G, 1, S), lambda b, p, i: (b, p, 0, 0))


def _lanes(g):
    return slice(g * LANES, (g + 1) * LANES)


def _streams(x_ref, G, scale=None):
    rows = x_ref.shape[0]
    lane = lax.broadcasted_iota(jnp.int32, (rows, LANES), 1)
    out = []
    for g in range(G):
        x = x_ref[:, _lanes(g)]
        if scale is not None:
            x = x * jnp.asarray(scale, x.dtype)
        for hh in range(2):
            out.append(jnp.where(_head_mask(lane, hh), x, jnp.zeros_like(x)))
    return lane, out


def _wide(stat, width):
    return jnp.tile(stat, (1, width // LANES))


def _fold_lanes(v):
    out = v[:, :LANES]
    for j in range(1, v.shape[1] // LANES):
        out = out + v[:, j * LANES:(j + 1) * LANES]
    return out


def _kv_blocks(ref, ks, tk, G):
    return [ref[pl.ds(ks, tk), _lanes(g)] for g in range(G)]


def _sweep(i, block):
    def step(kb, c):
        block(kb, False)
        return c
    lax.fori_loop(0, i, step, 0)
    block(i, True)


def _fox_fwd_g(qa, ka, va, cr, *, name, B, S, P, q_cb, k_cb, v_cb, G=HEAD_GROUP):
    tq = tk = min(ATT_TILE, S)
    nq = S // tq
    NS = 2 * G

    def body(q_ref, k_ref, v_ref, cr_ref, o_ref, lse_ref, acc_ref, m_ref, l_ref):
        i = pl.program_id(2)
        lane, qh = _streams(q_ref, G, SCALE)
        on_or_below = (lax.broadcasted_iota(jnp.int32, (tq, tk), 1) <= lax.broadcasted_iota(jnp.int32, (tq, tk), 0))
        m_ref[...] = jnp.full(m_ref.shape, NEG, F32)
        l_ref[...] = jnp.zeros(l_ref.shape, F32)
        acc_ref[...] = jnp.zeros(acc_ref.shape, F32)

        def block(kb, diag):
            ks = pl.multiple_of(kb * tk, tk)
            kblk = _kv_blocks(k_ref, ks, tk, G)
            vblk = _kv_blocks(v_ref, ks, tk, G)
            ss = [lax.dot_general(qh[st], kblk[st // 2], NT, preferred_element_type=F32) for st in range(NS)]
            ps = []
            for st in range(NS):
                s = ss[st] - cr_ref[0, st, :, pl.ds(ks, tk)]
                if diag:
                    s = jnp.where(on_or_below, s, NEG)
                m = m_ref[st]
                m_new = jnp.maximum(m, jnp.max(s, axis=-1, keepdims=True))
                alpha = jnp.exp(m - m_new)
                p = jnp.exp(s - _wide(m_new, tk))
                m_ref[st] = m_new
                l_ref[st] = alpha * l_ref[st] + _fold_lanes(p)
                ps.append((alpha, p.astype(BF16)))
            pvs = [jnp.dot(ps[st][1], vblk[st // 2], preferred_element_type=F32) for st in range(NS)]
            for st in range(NS):
                acc_ref[st] = ps[st][0] * acc_ref[st] + pvs[st]

        _sweep(i, block)
        ls = [jnp.sum(l_ref[st], axis=-1, keepdims=True) for st in range(NS)]
        for st in range(NS):
            lse_ref[0, st] = jnp.max(m_ref[st], axis=-1, keepdims=True) + jnp.log(ls[st])
        for g in range(G):
            o_ref[:, _lanes(g)] = jnp.where(lane < HEAD_DIM, acc_ref[2 * g] / ls[2 * g],
                                            acc_ref[2 * g + 1] / ls[2 * g + 1]).astype(BF16)

    return _pc(body, name=name, out_shape=[_sds((B * S, P * LANES), BF16), _sds((B, 2 * P, S, 1), F32)],
               grid=(B, P // G, nq),
               in_specs=[_g_col_spec(tq, nq, q_cb, G), _g_kv_spec(S, k_cb, G), _g_kv_spec(S, v_cb, G),
                         _g_stat_row_spec(S, G)],
               out_specs=[_g_col_spec(tq, nq, 0, G), _g_stat_col_spec(tq, G)],
               scratch_shapes=[pltpu.VMEM((NS, tq, LANES), F32)] * 3,
               semantics=("arbitrary", "arbitrary", "arbitrary"), vmem=VMEM_BIG)(qa, ka, va, cr)


def _fox_bwd_g(qa, ka, va, doa, lse, cr, *, name, B, S, P, q_cb, k_cb, v_cb, do_cb, G=HEAD_GROUP):
    tq = tk = min(ATT_TILE, S)
    nq = S // tq
    NS = 2 * G

    def body(q_ref, k_ref, v_ref, do_ref, lse_ref, cr_ref, dq_ref, dk_ref, dv_ref, dcs_ref, dqa_ref, delta_ref, lse_s,
             p_buf, dp_buf):
        i = pl.program_id(2)

        @pl.when(i == 0)
        def _():
            dk_ref[...] = jnp.zeros_like(dk_ref)
            dv_ref[...] = jnp.zeros_like(dv_ref)
            dcs_ref[...] = jnp.zeros_like(dcs_ref)

        lane, qh = _streams(q_ref, G, SCALE)
        _, doh = _streams(do_ref, G)
        on_or_below = (lax.broadcasted_iota(jnp.int32, (tq, tk), 1) <= lax.broadcasted_iota(jnp.int32, (tq, tk), 0))
        delta_ref[...] = jnp.zeros(delta_ref.shape, F32)
        dqa_ref[...] = jnp.zeros(dqa_ref.shape, F32)
        for st in range(NS):
            lse_s[st] = jnp.broadcast_to(lse_ref[0, st], (tq, LANES))

        def probs(kb, diag):
            ks = pl.multiple_of(kb * tk, tk)
            kblk = _kv_blocks(k_ref, ks, tk, G)
            vblk = _kv_blocks(v_ref, ks, tk, G)
            ss = [lax.dot_general(qh[st], kblk[st // 2], NT, preferred_element_type=F32) for st in range(NS)]
            dps = [lax.dot_general(doh[st], vblk[st // 2], NT, preferred_element_type=F32) for st in range(NS)]
            ps = []
            for st in range(NS):
                s = ss[st] - cr_ref[0, st, :, pl.ds(ks, tk)]
                if diag:
                    s = jnp.where(on_or_below, s, NEG)
                ps.append(jnp.exp(s - _wide(lse_s[st], tk)))
            return ks, kblk, ps, dps

        def delta_block(kb, diag):
            _, _, ps, dps = probs(kb, diag)
            for st in range(NS):
                delta_ref[st] += _fold_lanes(ps[st] * dps[st])
                p_buf[st, kb] = ps[st]
                dp_buf[st, kb] = dps[st]

        _sweep(i, delta_block)
        for st in range(NS):
            delta_ref[st] = jnp.broadcast_to(jnp.sum(delta_ref[st], axis=-1, keepdims=True), (tq, LANES))

        def grad_block(kb, diag):
            ks = pl.multiple_of(kb * tk, tk)
            kblk = _kv_blocks(k_ref, ks, tk, G)
            rows = pl.ds(ks, tk)
            dsb, pb = [], []
            for st in range(NS):
                p = p_buf[st, kb]
                ds = p * (dp_buf[st, kb] - _wide(delta_ref[st], tk))
                dcs_ref[0, st, :, rows] -= jnp.sum(ds, axis=0, keepdims=True)
                dsb.append(ds.astype(BF16))
                pb.append(p.astype(BF16))
            dks = [lax.dot_general(dsb[st], qh[st], TN, preferred_element_type=F32) for st in range(NS)]
            dvs = [lax.dot_general(pb[st], doh[st], TN, preferred_element_type=F32) for st in range(NS)]
            dqs = [jnp.dot(dsb[st], kblk[st // 2], preferred_element_type=F32) for st in range(NS)]
            for g in range(G):
                dk_ref[rows, _lanes(g)] += dks[2 * g] + dks[2 * g + 1]
                dv_ref[rows, _lanes(g)] += dvs[2 * g] + dvs[2 * g + 1]
            for st in range(NS):
                dqa_ref[st] += dqs[st]

        _sweep(i, grad_block)
        for g in range(G):
            dq_ref[:, _lanes(g)] = (jnp.where(lane < HEAD_DIM, dqa_ref[2 * g], dqa_ref[2 * g + 1]) * SCALE).astype(BF16)

    return _pc(body, name=name,
               out_shape=[_sds((B * S, P * LANES), BF16), _sds((B * S, P * LANES), F32), _sds((B * S, P * LANES), F32),
                          _sds((B, 2 * P, 1, S), F32)],
               grid=(B, P // G, nq),
               in_specs=[_g_col_spec(tq, nq, q_cb, G), _g_kv_spec(S, k_cb, G), _g_kv_spec(S, v_cb, G),
                         _g_col_spec(tq, nq, do_cb, G), _g_stat_col_spec(tq, G), _g_stat_row_spec(S, G)],
               out_specs=[_g_col_spec(tq, nq, 0, G), _g_kv_spec(S, 0, G), _g_kv_spec(S, 0, G), _g_stat_row_spec(S, G)],
               scratch_shapes=[pltpu.VMEM((NS, tq, LANES), F32)] * 3 + [pltpu.VMEM((NS, nq, tq, tk), F32)] * 2,
               semantics=("arbitrary", "arbitrary", "arbitrary"), vmem=VMEM_BIG)(qa, ka, va, doa, lse, cr)


def _sb_logs_z(z):
    nz = -z
    lm = jnp.minimum(nz, 0.0) - jnp.log(1.0 + jnp.exp(jnp.minimum(z, nz)))
    return lm + z, lm


def _sb_fwd_g(qa, ka, va, *, name, B, S, P, q_cb, k_cb, v_cb, G=HEAD_GROUP):
    tq = tk = min(ATT_TILE, S)
    nq = S // tq
    NS = 2 * G

    def body(q_ref, k_ref, v_ref, o_ref, rt_ref, acc_ref, run_ref):
        i = pl.program_id(2)
        lane, qh = _streams(q_ref, G, SCALE)
        t_r = lax.broadcasted_iota(jnp.int32, (tk, tk), 0)
        t_c = lax.broadcasted_iota(jnp.int32, (tk, tk), 1)
        after = (t_r > t_c).astype(BF16)
        below = t_c < t_r
        acc_ref[...] = jnp.zeros(acc_ref.shape, F32)
        run_ref[...] = jnp.zeros(run_ref.shape, F32)

        def block(kb, diag):
            ks = pl.multiple_of(kb * tk, tk)
            kblk = _kv_blocks(k_ref, ks, tk, G)
            vblk = _kv_blocks(v_ref, ks, tk, G)
            zs = [lax.dot_general(qh[st], kblk[st // 2], NT, preferred_element_type=F32) for st in range(NS)]
            lss, parts = [], []
            for st in range(NS):
                ls, lm = _sb_logs_z(zs[st])
                if diag:
                    lm = jnp.where(below, lm, 0.0)
                lss.append(ls + _wide(run_ref[st], tk))
                run_ref[st] += jnp.sum(lm, axis=-1, keepdims=True)
                parts.append(_split2(lm))
            sufs = [jnp.dot(parts[st][0], after, preferred_element_type=F32)
                    + jnp.dot(parts[st][1], after, preferred_element_type=F32) for st in range(NS)]
            ab = []
            for st in range(NS):
                a = jnp.exp(lss[st] + sufs[st])
                if diag:
                    a = jnp.where(below, a, 0.0)
                ab.append(a.astype(BF16))
            pvs = [jnp.dot(ab[st], vblk[st // 2], preferred_element_type=F32) for st in range(NS)]
            for st in range(NS):
                acc_ref[st] += pvs[st]

        block(i, True)

        def step(jj, c):
            block(i - 1 - jj, False)
            return c

        lax.fori_loop(0, i, step, 0)
        for st in range(NS):
            rt_ref[0, st] = jnp.max(run_ref[st], axis=-1, keepdims=True)
        for g in range(G):
            o_ref[:, _lanes(g)] = jnp.where(lane < HEAD_DIM, acc_ref[2 * g], acc_ref[2 * g + 1]).astype(BF16)

    return _pc(body, name=name, out_shape=[_sds((B * S, P * LANES), BF16), _sds((B, 2 * P, S, 1), F32)],
               grid=(B, P // G, nq),
               in_specs=[_g_col_spec(tq, nq, q_cb, G), _g_kv_spec(S, k_cb, G), _g_kv_spec(S, v_cb, G)],
               out_specs=[_g_col_spec(tq, nq, 0, G), _g_stat_col_spec(tq, G)],
               scratch_shapes=[pltpu.VMEM((NS, tq, LANES), F32)] * 2,
               semantics=("arbitrary", "arbitrary", "arbitrary"), vmem=VMEM_BIG)(qa, ka, va)


def _sb_bwd_g(qa, ka, va, doa, rt, *, name, B, S, P, q_cb, k_cb, v_cb, do_cb, G=HEAD_GROUP):
    tq = tk = min(ATT_TILE, S)
    nq = S // tq
    NS = 2 * G

    def body(q_ref, k_ref, v_ref, do_ref, rt_ref, dq_ref, dk_ref, dv_ref, dqa_ref, pl_ref, pg_ref):
        i = pl.program_id(2)

        @pl.when(i == 0)
        def _():
            dk_ref[...] = jnp.zeros_like(dk_ref)
            dv_ref[...] = jnp.zeros_like(dv_ref)

        lane, qh = _streams(q_ref, G, SCALE)
        _, doh = _streams(do_ref, G)
        t_r = lax.broadcasted_iota(jnp.int32, (tk, tk), 0)
        t_c = lax.broadcasted_iota(jnp.int32, (tk, tk), 1)
        upto = (t_r <= t_c).astype(BF16)
        before = (t_r < t_c).astype(BF16)
        below = t_c < t_r
        dqa_ref[...] = jnp.zeros(dqa_ref.shape, F32)
        pg_ref[...] = jnp.zeros(pg_ref.shape, F32)
        for st in range(NS):
            pl_ref[st] = jnp.broadcast_to(rt_ref[0, st], (tq, LANES))

        def block(kb, diag):
            ks = pl.multiple_of(kb * tk, tk)
            rows = pl.ds(ks, tk)
            kblk = _kv_blocks(k_ref, ks, tk, G)
            vblk = _kv_blocks(v_ref, ks, tk, G)
            zs = [lax.dot_general(qh[st], kblk[st // 2], NT, preferred_element_type=F32) for st in range(NS)]
            das = [lax.dot_general(doh[st], vblk[st // 2], NT, preferred_element_type=F32) for st in range(NS)]
            lss, parts = [], []
            for st in range(NS):
                ls, lm = _sb_logs_z(zs[st])
                if diag:
                    lm = jnp.where(below, lm, 0.0)
                lss.append((ls, ls + _wide(pl_ref[st], tk)))
                pl_ref[st] -= jnp.sum(lm, axis=-1, keepdims=True)
                parts.append(_split2(lm))
            pins = [jnp.dot(parts[st][0], upto, preferred_element_type=F32)
                    + jnp.dot(parts[st][1], upto, preferred_element_type=F32) for st in range(NS)]
            gms, ab, gparts = [], [], []
            for st in range(NS):
                a = jnp.exp(lss[st][1] - pins[st])
                if diag:
                    a = jnp.where(below, a, 0.0)
                gm = a * das[st]
                gms.append(gm)
                ab.append(a.astype(BF16))
                gparts.append(gm.astype(BF16))
            pgs = [jnp.dot(gparts[st], before, preferred_element_type=F32) for st in range(NS)]
            dzb = []
            for st in range(NS):
                gm = gms[st]
                dz = gm - jnp.exp(lss[st][0]) * (gm + (pgs[st] + _wide(pg_ref[st], tk)))
                if diag:
                    dz = jnp.where(below, dz, 0.0)
                pg_ref[st] += jnp.sum(gm, axis=-1, keepdims=True)
                dzb.append(dz.astype(BF16))
            dks = [lax.dot_general(dzb[st], qh[st], TN, preferred_element_type=F32) for st in range(NS)]
            dvs = [lax.dot_general(ab[st], doh[st], TN, preferred_element_type=F32) for st in range(NS)]
            dqs = [jnp.dot(dzb[st], kblk[st // 2], preferred_element_type=F32) for st in range(NS)]
            for g in range(G):
                dk_ref[rows, _lanes(g)] += dks[2 * g] + dks[2 * g + 1]
                dv_ref[rows, _lanes(g)] += dvs[2 * g] + dvs[2 * g + 1]
            for st in range(NS):
                dqa_ref[st] += dqs[st]

        _sweep(i, block)
        for g in range(G):
            dq_ref[:, _lanes(g)] = (jnp.where(lane < HEAD_DIM, dqa_ref[2 * g], dqa_ref[2 * g + 1]) * SCALE).astype(BF16)

    return _pc(body, name=name,
               out_shape=[_sds((B * S, P * LANES), BF16), _sds((B * S, P * LANES), F32), _sds((B * S, P * LANES), F32)],
               grid=(B, P // G, nq),
               in_specs=[_g_col_spec(tq, nq, q_cb, G), _g_kv_spec(S, k_cb, G), _g_kv_spec(S, v_cb, G),
                         _g_col_spec(tq, nq, do_cb, G), _g_stat_col_spec(tq, G)],
               out_specs=[_g_col_spec(tq, nq, 0, G), _g_kv_spec(S, 0, G), _g_kv_spec(S, 0, G)],
               scratch_shapes=[pltpu.VMEM((NS, tq, LANES), F32)] * 3,
               semantics=("arbitrary", "arbitrary", "arbitrary"), vmem=VMEM_BIG)(qa, ka, va, doa, rt)


MEM_GROUP = N_MEM_HEADS // 2


def _mem_fwd(qa, kva, *, name, B, S, NM, q_cb):
    G = MEM_GROUP
    NS = 2 * G
    tq = min(MEM_Q_TILE, S)
    nq = S // tq

    def body(q_ref, k_ref, v_ref, o_ref, lse_ref):
        lane, qh = _streams(q_ref, G, SCALE)
        kblk = [k_ref[:, _lanes(g)] for g in range(G)]
        vblk = [v_ref[:, _lanes(g)] for g in range(G)]
        ss = [lax.dot_general(qh[st], kblk[st // 2], NT, preferred_element_type=F32) for st in range(NS)]
        pb, ls = [], []
        for st in range(NS):
            m = jnp.max(ss[st], axis=-1, keepdims=True)
            p = jnp.exp(ss[st] - m)
            l = jnp.sum(p, axis=-1, keepdims=True)
            lse_ref[0, st] = m + jnp.log(l)
            pb.append(p.astype(BF16))
            ls.append(l)
        pvs = [jnp.dot(pb[st], vblk[st // 2], preferred_element_type=F32) for st in range(NS)]
        for g in range(G):
            o_ref[:, _lanes(g)] = jnp.where(lane < HEAD_DIM, pvs[2 * g] / ls[2 * g],
                                            pvs[2 * g + 1] / ls[2 * g + 1]).astype(BF16)

    return _pc(body, name=name, out_shape=[_sds((B * S, G * LANES), BF16), _sds((B, NS, S, 1), F32)],
               grid=(B, 1, nq),
               in_specs=[_g_col_spec(tq, nq, q_cb, G), _g_kv_spec(NM, 0, G), _g_kv_spec(NM, G, G)],
               out_specs=[_g_col_spec(tq, nq, 0, G), _g_stat_col_spec(tq, G)],
               semantics=("arbitrary", "arbitrary", "arbitrary"), vmem=VMEM_BIG)(qa, kva, kva)


def _mem_bwd(qa, kva, doa, oa, lse, *, name, B, S, NM, q_cb, do_cb):
    G = MEM_GROUP
    NS = 2 * G
    tq = min(MEM_Q_TILE, S)
    nq = S // tq

    def body(q_ref, k_ref, v_ref, do_ref, o_ref, lse_ref, dq_ref, dk_ref, dv_ref):
        @pl.when(pl.program_id(2) == 0)
        def _():
            dk_ref[...] = jnp.zeros_like(dk_ref)
            dv_ref[...] = jnp.zeros_like(dv_ref)

        lane, qh = _streams(q_ref, G, SCALE)
        _, doh = _streams(do_ref, G)
        kblk = [k_ref[:, _lanes(g)] for g in range(G)]
        vblk = [v_ref[:, _lanes(g)] for g in range(G)]
        prod = [do_ref[:, _lanes(g)].astype(F32) * o_ref[:, _lanes(g)].astype(F32) for g in range(G)]
        ss = [lax.dot_general(qh[st], kblk[st // 2], NT, preferred_element_type=F32) for st in range(NS)]
        dps = [lax.dot_general(doh[st], vblk[st // 2], NT, preferred_element_type=F32) for st in range(NS)]
        dsb, pb = [], []
        for st in range(NS):
            delta = jnp.sum(jnp.where(_head_mask(lane, st % 2), prod[st // 2], 0.0), axis=-1, keepdims=True)
            p = jnp.exp(ss[st] - lse_ref[0, st])
            dsb.append((p * (dps[st] - delta)).astype(BF16))
            pb.append(p.astype(BF16))
        dks = [lax.dot_general(dsb[st], qh[st], TN, preferred_element_type=F32) for st in range(NS)]
        dvs = [lax.dot_general(pb[st], doh[st], TN, preferred_element_type=F32) for st in range(NS)]
        dqs = [jnp.dot(dsb[st], kblk[st // 2], preferred_element_type=F32) for st in range(NS)]
        for g in range(G):
            dk_ref[:, _lanes(g)] += dks[2 * g] + dks[2 * g + 1]
            dv_ref[:, _lanes(g)] += dvs[2 * g] + dvs[2 * g + 1]
            dq_ref[:, _lanes(g)] = (jnp.where(lane < HEAD_DIM, dqs[2 * g], dqs[2 * g + 1]) * SCALE).astype(BF16)

    return _pc(body, name=name,
               out_shape=[_sds((B * S, G * LANES), BF16), _sds((B * NM, G * LANES), F32), _sds((B * NM, G * LANES), F32)],
               grid=(B, 1, nq),
               in_specs=[_g_col_spec(tq, nq, q_cb, G), _g_kv_spec(NM, 0, G), _g_kv_spec(NM, G, G),
                         _g_col_spec(tq, nq, do_cb, G), _g_col_spec(tq, nq, 0, G), _g_stat_col_spec(tq, G)],
               out_specs=[_g_col_spec(tq, nq, 0, G), _g_kv_spec(NM, 0, G), _g_kv_spec(NM, 0, G)],
               semantics=("arbitrary", "arbitrary", "arbitrary"), vmem=VMEM_BIG)(qa, kva, kva, doa, oa, lse)


def _sigmoid(v):
    return 0.5 * jnp.tanh(0.5 * v) + 0.5


def _shift_rows(cur, halo_ref, first, rows_idx, k):
    out = pltpu.roll(cur, k, 0)
    top = out[0:8, :]
    for r in range(k):
        hr = halo_ref.shape[0] - k + r
        edge = jnp.where(first, 0.0, halo_ref[hr:hr + 1, :])
        top = jnp.where(rows_idx[0:8, :] == r, edge, top)
    return jnp.concatenate([top, out[8:, :]], axis=0)


def _shift_rows_up(cur, halo_ref, last, rows_idx, k, ts):
    out = pltpu.roll(cur, ts - k, 0)
    bottom = out[ts - 8:, :]
    for r in range(k):
        edge = jnp.where(last, 0.0, halo_ref[r:r + 1, :])
        bottom = jnp.where(rows_idx[0:8, :] == 8 - k + r, edge, bottom)
    return jnp.concatenate([out[:ts - 8, :], bottom], axis=0)


def _ffn_up_gate(x, g, w, cw, cb, *, name, S):
    T, D = x.shape
    F = w.shape[1] // 2
    tm = min(1024, S)
    tn = 256
    nj = F // tn
    tiles_per_seq = S // tm
    halo = 16

    def body(x_ref, xh_ref, g_ref, wg_ref, wv_ref, cwg_ref, cwv_ref, cbg_ref, cbv_ref,
             uc_ref, ub_ref, a_ref, hout_ref, h_ref, hh_ref, eg_ref, ev_ref):
        first = lax.rem(pl.program_id(0), tiles_per_seq) == 0

        @pl.when(pl.program_id(1) == 0)
        def _():
            def norm(v):
                r = lax.rsqrt(jnp.mean(v * v, axis=-1, keepdims=True) + EPS)
                return ((v * r) * g_ref[...]).astype(BF16)
            h = norm(x_ref[...])
            h_ref[...] = h
            hout_ref[...] = h
            hh_ref[...] = norm(xh_ref[...])

        h = h_ref[...]
        rows_idx = lax.broadcasted_iota(jnp.int32, (tm, tn), 0)
        uc = []
        for half, (w_ref, cw_ref, cb_ref, e_ref) in enumerate(((wg_ref, cwg_ref, cbg_ref, eg_ref),
                                                               (wv_ref, cwv_ref, cbv_ref, ev_ref))):
            acc = jnp.dot(h, w_ref[...], preferred_element_type=F32)
            e_ref[...] = jnp.dot(hh_ref[...], w_ref[...], preferred_element_type=F32)
            ub_ref[half] = acc.astype(BF16)
            m1 = _shift_rows(acc, e_ref, first, rows_idx, 1)
            m2 = _shift_rows(acc, e_ref, first, rows_idx, 2)
            uc.append(cb_ref[...] + cw_ref[0:1, :] * m2 + cw_ref[1:2, :] * m1 + cw_ref[2:3, :] * acc)
            uc_ref[half] = uc[half]
        a_ref[...] = (uc[0] * _sigmoid(uc[0]) * uc[1]).astype(BF16)

    in_specs = [pl.BlockSpec((tm, D), lambda i, j: (i, 0)),
                pl.BlockSpec((halo, D), lambda i, j: (jnp.maximum(i * (tm // halo) - 1, 0), 0)),
                pl.BlockSpec((1, D), lambda i, j: (0, 0)),
                pl.BlockSpec((D, tn), lambda i, j: (0, j)), pl.BlockSpec((D, tn), lambda i, j: (0, j + nj)),
                pl.BlockSpec((3, tn), lambda i, j: (0, j)), pl.BlockSpec((3, tn), lambda i, j: (0, j + nj)),
                pl.BlockSpec((1, tn), lambda i, j: (0, j)), pl.BlockSpec((1, tn), lambda i, j: (0, j + nj))]
    return _pc(body, name=name,
               out_shape=[_sds((2, T, F), F32), _sds((2, T, F), BF16), _sds((T, F), BF16), _sds((T, D), BF16)],
               grid=(T // tm, nj), in_specs=in_specs,
               out_specs=[pl.BlockSpec((2, tm, tn), lambda i, j: (0, i, j)), pl.BlockSpec((2, tm, tn), lambda i, j: (0, i, j)),
                          pl.BlockSpec((tm, tn), lambda i, j: (i, j)), pl.BlockSpec((tm, D), lambda i, j: (i, 0))],
               scratch_shapes=[pltpu.VMEM((tm, D), BF16), pltpu.VMEM((halo, D), BF16),
                               pltpu.VMEM((halo, tn), F32), pltpu.VMEM((halo, tn), F32)],
               semantics=("arbitrary", "arbitrary"), vmem=VMEM_BIG)(x, x, g.reshape(1, D), w, w, cw, cw, cb, cb)


def _conv_gate_bwd(da, uc, ub, cw, *, name, B, S):
    F = uc.shape[2]
    tf = F // 2
    ts = min(512, S)
    ns, nf = S // ts, F // tf

    def body(da_ref, uc_ref, ub_ref, wg_ref, wv_ref, dug_ref, duv_ref, pg_ref, pv_ref, nxt_g, nxt_v):
        last = pl.program_id(2) == 0

        @pl.when(jnp.logical_and(pl.program_id(1) == 0, last))
        def _():
            pg_ref[...] = jnp.zeros_like(pg_ref)
            pv_ref[...] = jnp.zeros_like(pv_ref)

        rows_idx = lax.broadcasted_iota(jnp.int32, (ts, tf), 0)
        ucg, ucv = uc_ref[0], uc_ref[1]
        sg = _sigmoid(ucg)
        dav = da_ref[...]
        d_v = dav * (ucg * sg)
        d_g = dav * ucv * (sg * (1.0 + ucg * (1.0 - sg)))
        for half, (o_ref, p_ref, d, w_ref, nxt) in enumerate(((dug_ref, pg_ref, d_g, wg_ref, nxt_g),
                                                               (duv_ref, pv_ref, d_v, wv_ref, nxt_v))):
            p1 = _shift_rows_up(d, nxt, last, rows_idx, 1, ts)
            p2 = _shift_rows_up(d, nxt, last, rows_idx, 2, ts)
            o_ref[...] = (w_ref[2:3, :] * d + w_ref[1:2, :] * p1 + w_ref[0:1, :] * p2).astype(BF16)
            nxt[...] = d[0:8, :]
            uh = ub_ref[half].astype(F32)
            for k, dk in enumerate((p2, p1, d)):
                p_ref[k:k + 1, :] += jnp.sum(dk * uh, axis=0, keepdims=True)
            p_ref[3:4, :] += jnp.sum(d, axis=0, keepdims=True)

    row = pl.BlockSpec((ts, tf), lambda j, b, r: (b * ns + ns - 1 - r, j))
    both = pl.BlockSpec((2, ts, tf), lambda j, b, r: (0, b * ns + ns - 1 - r, j))
    par = pl.BlockSpec((8, tf), lambda j, b, r: (0, j))
    return _pc(body, name=name,
               out_shape=[_sds((B * S, F), BF16), _sds((B * S, F), BF16), _sds((8, F), F32), _sds((8, F), F32)],
               grid=(nf, B, ns),
               in_specs=[row, both, both, pl.BlockSpec((3, tf), lambda j, b, r: (0, j)),
                         pl.BlockSpec((3, tf), lambda j, b, r: (0, j + nf))],
               out_specs=[row, row, par, par],
               scratch_shapes=[pltpu.VMEM((8, tf), F32), pltpu.VMEM((8, tf), F32)],
               semantics=("arbitrary", "arbitrary", "arbitrary"), vmem=VMEM_BIG)(da, uc, ub, cw, cw)


def _adamw(w, g, m, v, *, name):
    rows, cols = w.shape
    tr = rows
    while tr * cols * 4 > 2 ** 20 and tr % 16 == 0:
        tr //= 2

    def body(w_ref, g_ref, m_ref, v_ref, d_ref, nm_ref, nv_ref):
        gv = g_ref[...]
        m_new = ADAM_B1 * m_ref[...] + (1.0 - ADAM_B1) * gv
        v_new = ADAM_B2 * v_ref[...] + (1.0 - ADAM_B2) * (gv * gv)
        m_hat = m_new / (1.0 - ADAM_B1 ** ADAM_STEP)
        v_hat = v_new / (1.0 - ADAM_B2 ** ADAM_STEP)
        d_ref[...] = -ADAM_LR * (m_hat / (jnp.sqrt(v_hat) + ADAM_EPS) + ADAM_WD * w_ref[...])
        nm_ref[...] = m_new
        nv_ref[...] = v_new

    blk = pl.BlockSpec((tr, cols), lambda i: (i, 0))
    return _pc(body, name=name, out_shape=[_sds((rows, cols), F32)] * 3, grid=(rows // tr,),
               in_specs=[blk] * 4, out_specs=[blk] * 3, semantics=("arbitrary",))(w, g, m, v)


def _my_pos():
    return lax.axis_index("x"), lax.axis_index("y"), lax.axis_index("c")


_HBM = pl.BlockSpec(memory_space=pltpu.HBM)
_SEM = pl.BlockSpec(memory_space=pltpu.SEMAPHORE)
_EFFECT = pltpu.SideEffectType.DATAFLOW_SIDE_EFFECTING


def _peers():
    x, y, c = _my_pos()
    out = []
    for k in range(1, N_DEV):
        px, py, pc = x ^ ((k >> 2) & 1), y ^ ((k >> 1) & 1), c ^ (k & 1)
        out.append(((px, py, pc), 4 * px + 2 * py + pc))
    return out


def _scatter_start(srcs, slot_of, *, name, order_after=None):
    n = len(srcs)
    lands = [lax.empty((N_DEV,) + slot_of(s, 0, shape_only=True), s.dtype) for s in srcs]
    extra = [] if order_after is None else [order_after]

    def body(*refs):
        src_refs, land_refs = refs[:n], refs[n:2 * n]
        send_sems, recv_sems = refs[2 * n + len(extra)], refs[2 * n + len(extra) + 1]
        token = refs[-1]
        x, y, c = _my_pos()
        me = 4 * x + 2 * y + c
        for a in range(n):
            for k, (peer, peer_idx) in enumerate(_peers()):
                pltpu.make_async_remote_copy(
                    src_ref=slot_of(src_refs[a], peer_idx), dst_ref=land_refs[a].at[me],
                    send_sem=send_sems.at[a * 7 + k], recv_sem=recv_sems.at[a * 7 + k],
                    device_id=peer, device_id_type=MESH).start()
        token[...] = jnp.zeros_like(token)

    hbm = lambda a: pltpu.HBM(a.shape, a.dtype)
    args = [pltpu.with_memory_space_constraint(a, pltpu.HBM) for a in list(srcs) + lands] + extra
    outs = pl.pallas_call(
        body, name=name,
        out_shape=(pltpu.SemaphoreType.DMA((7 * n,)), pltpu.SemaphoreType.DMA((7 * n,)),
                   *[hbm(a) for a in srcs], *[hbm(a) for a in lands], _sds((8, LANES), F32)),
        in_specs=[_HBM] * (2 * n) + [pl.BlockSpec(memory_space=pl.ANY)] * len(extra),
        out_specs=(_SEM, _SEM, *([_HBM] * (2 * n)), pl.BlockSpec(memory_space=pltpu.VMEM)),
        input_output_aliases={a: 2 + a for a in range(2 * n)},
        compiler_params=pltpu.CompilerParams(has_side_effects=_EFFECT))(*args)
    return outs[0], outs[1], list(outs[2:2 + n]), list(outs[2 + n:2 + 2 * n]), outs[-1]


def _scatter_wait(send_sems, recv_sems, srcs, lands, slot_of, after, *, name, first=0):
    n = len(srcs)

    def body(*refs):
        src_refs, land_refs = refs[:n], refs[n:2 * n]
        ssem, rsem = refs[2 * n], refs[2 * n + 1]
        x, y, c = _my_pos()
        me = 4 * x + 2 * y + c
        for a in range(n):
            for k, (peer, peer_idx) in enumerate(_peers()):
                cp = pltpu.make_async_remote_copy(
                    src_ref=slot_of(src_refs[a], peer_idx), dst_ref=land_refs[a].at[me],
                    send_sem=ssem.at[(first + a) * 7 + k], recv_sem=rsem.at[(first + a) * 7 + k],
                    device_id=peer, device_id_type=MESH)
                cp.wait_send()
                cp.wait_recv()

    hbm = lambda a: pltpu.HBM(a.shape, a.dtype)
    outs = pl.pallas_call(
        body, name=name, out_shape=tuple(hbm(a) for a in list(srcs) + list(lands)),
        in_specs=[_HBM] * (2 * n) + [_SEM, _SEM, pl.BlockSpec(memory_space=pl.ANY)],
        out_specs=tuple([_HBM] * (2 * n)), input_output_aliases={a: a for a in range(2 * n)},
        compiler_params=pltpu.CompilerParams(has_side_effects=_EFFECT))(*srcs, *lands, send_sems, recv_sems, after)
    return list(outs[:n]), list(outs[n:])


def _whole(a, peer_idx, shape_only=False):
    return a.shape if shape_only else a


def _slot(a, peer_idx, shape_only=False):
    return a.shape[1:] if shape_only else a.at[peer_idx]


def _sum_slots(a, *, name, tr=None):
    rows, cols = a.shape[1], a.shape[2]
    if tr is None:
        tr = rows
        while N_DEV * tr * cols * a.dtype.itemsize > 3 * 2 ** 20 and tr % 32 == 0:
            tr //= 2

    def body(a_ref, o_ref):
        acc = a_ref[0].astype(F32)
        for j in range(1, N_DEV):
            acc = acc + a_ref[j].astype(F32)
        o_ref[...] = acc

    return _pc(body, name=name, out_shape=_sds((rows, cols), F32), grid=(rows // tr,),
               in_specs=[pl.BlockSpec((N_DEV, tr, cols), lambda i: (0, i, 0))],
               out_specs=pl.BlockSpec((tr, cols), lambda i: (i, 0)), semantics=("arbitrary",), vmem=VMEM_BIG)(a)


def _to_slots(full, kind):
    if kind == "rows2":
        r, c = full.shape
        return full.reshape(N_DEV, r // N_DEV, c)
    if kind == "cols2":
        r, c = full.shape
        return full.reshape(r, N_DEV, c // N_DEV).transpose(1, 0, 2)
    if kind == "rows3":
        l, r, c = full.shape
        return full.reshape(l, N_DEV, r // N_DEV, c).transpose(1, 0, 2, 3)
    if kind == "cols3":
        l, r, c = full.shape
        return full.reshape(l, r, N_DEV, c // N_DEV).transpose(2, 0, 1, 3)
    raise ValueError(kind)


def _from_slots(slots, kind):
    if kind == "rows2":
        _, r, c = slots.shape
        return slots.reshape(N_DEV * r, c)
    if kind == "cols2":
        _, r, c = slots.shape
        return slots.transpose(1, 0, 2).reshape(r, N_DEV * c)
    if kind == "rows3":
        _, l, r, c = slots.shape
        return slots.transpose(1, 0, 2, 3).reshape(l, N_DEV * r, c)
    if kind == "cols3":
        _, l, r, c = slots.shape
        return slots.transpose(1, 2, 0, 3).reshape(l, r, N_DEV * c)
    raise ValueError(kind)


BIG = (("w_in_a", "rows2"), ("w_in_b", "rows2"), ("w_kv", "cols2"), ("w_memkv", "rows3"),
       ("w_out", "rows3"), ("w_up", "cols3"), ("w_down", "rows3"))


def _round_up(n, m):
    return -(-n // m) * m


def _pad_rows(a, rows, axis):
    pad = [(0, 0)] * a.ndim
    pad[axis] = (0, rows - a.shape[axis])
    return jnp.pad(a, pad)


def kernel(x, mem, ln_mix_g, w_in_a, b_f_a, w_in_b, ln_kv_g, w_kv, ln_mem_g, w_memkv, w_out, ln_ffn_g, w_up, conv_w, conv_b, w_down, final_g, loss_target, m_ln_mix_g, m_w_in_a, m_b_f_a, m_w_in_b, m_ln_kv_g, m_w_kv, m_ln_mem_g, m_w_memkv, m_w_out, m_ln_ffn_g, m_w_up, m_conv_w, m_conv_b, m_w_down, m_final_g, v_ln_mix_g, v_w_in_a, v_b_f_a, v_w_in_b, v_ln_kv_g, v_w_kv, v_ln_mem_g, v_w_memkv, v_w_out, v_ln_ffn_g, v_w_up, v_conv_w, v_conv_b, v_w_down, v_final_g):
    B, S, D = x.shape
    NM = mem.shape[1]
    T = B * S
    F = w_down.shape[1] * N_DEV
    my_idx = 4 * lax.axis_index("x") + 2 * lax.axis_index("y") + lax.axis_index("c")

    shards = {"w_in_a": w_in_a[0], "w_in_b": w_in_b[0], "w_kv": w_kv, "w_memkv": w_memkv, "w_out": w_out,
              "w_up": w_up, "w_down": w_down}
    moms = {"w_in_a": (m_w_in_a[0], v_w_in_a[0]), "w_in_b": (m_w_in_b[0], v_w_in_b[0]), "w_kv": (m_w_kv, v_w_kv),
            "w_memkv": (m_w_memkv, v_w_memkv), "w_out": (m_w_out, v_w_out), "w_up": (m_w_up, v_w_up),
            "w_down": (m_w_down, v_w_down)}

    groups = [("a1", [("w_in_a", None)]),
              ("a2", [("w_memkv", None), ("w_out", None), ("conv_w", None)]),
              ("b0", [("w_up", 0), ("w_down", 0)]), ("a3", [("w_in_b", None), ("w_kv", None)]),
              ("b1", [("w_up", 1), ("w_down", 1)])]
    sources = dict(shards, conv_w=conv_w)
    srcs, span = [], {}
    for gname, members in groups:
        span[gname] = (len(srcs), len(members))
        for n, layer in members:
            a = sources[n] if layer is None else sources[n][layer]
            srcs.append(a if n == "conv_w" else a.astype(BF16))
    g_ssem, g_rsem, g_thru, g_lands, token = _scatter_start(srcs, _whole, name="gather_start")

    def gathered(gname, after):
        lo, cnt = span[gname]
        thru, lands = _scatter_wait(g_ssem, g_rsem, g_thru[lo:lo + cnt], g_lands[lo:lo + cnt], _whole, after,
                                    name=f"gather_wait_{gname}", first=lo)
        return [lax.dynamic_update_index_in_dim(land, s, my_idx, 0) for land, s in zip(lands, thru)]

    full = {}
    (g_wa,) = gathered("a1", token)
    full["w_in_a"] = _from_slots(g_wa, "rows2")

    wa = full["w_in_a"]
    n_qkv = 3 * MAIN_W
    wa = jnp.concatenate([wa[:, :n_qkv], wa[:, n_qkv + N_MAIN_HEADS:], wa[:, n_qkv:n_qkv + N_MAIN_HEADS],
                          jnp.zeros((D, LANES - N_MAIN_HEADS), BF16)], axis=1)
    n_main = n_qkv + MEM_W
    full["w_up"], full["w_down"] = {}, {}
    b_f =_pad_rows(b_f_a.reshape(1, N_MAIN_HEADS), LANES, 1)

    x2d = x.reshape(T, D)
    mem2d = mem.reshape(B * NM, D)
    tgt2d = loss_target.reshape(T, D)
    PM, PX = N_MAIN_HEADS // 2, N_MEM_HEADS // 2

    def stats_to_heads(c2d):
        c = c2d.reshape(B, S, LANES)[:, :, :N_MAIN_HEADS].transpose(0, 2, 1)
        return c[:, :, None, :]

    def mem_kv(layer):
        return _mm_fwd(mem2d, full["w_memkv"][layer], name=f"memkv{layer}", tm=B * NM, tn=2 * MEM_W,
                       out_dtype=BF16, g=ln_mem_g[layer], save_h=True)

    def conv_ffn_fwd(xin, layer):
        uc, ub, a, h = _ffn_up_gate(xin, ln_ffn_g[layer], full["w_up"][layer], conv_w_full[layer],
                                    conv_b[layer].reshape(1, 2 * F), name=f"ffn_up{layer}", S=S)
        xo = _mm_fwd(a, full["w_down"][layer], name=f"ffn_down{layer}", tm=min(1024, T), tn=1024, out_dtype=F32, res=xin)
        return xo, (uc, ub, h, a)

    proj_a, h_mix0 = _mm_fwd(x2d, wa, name="in_proj_a", tm=min(1024, T), tn=2560, out_dtype=BF16, g=ln_mix_g[0],
                             ncols=n_main, save_h=True)
    f_logit = _mm_fwd(x2d, wa, name="in_proj_f", tm=min(1024, T), tn=LANES, out_dtype=F32, g=ln_mix_g[0],
                      col0=n_main // LANES, ncols=LANES)
    c2d = _forget_cumsum(f_logit, b_f, B=B, S=S, name="forget_cumsum")
    cr = stats_to_heads(c2d)
    o_main0, lse0 = _fox_fwd_g(proj_a, proj_a, proj_a, cr, name="fox_fwd", B=B, S=S, P=PM, q_cb=0, k_cb=PM, v_cb=2 * PM,
                               G=FWD_HEAD_GROUP)
    g_wmem, g_wout, g_cw = gathered("a2", lse0)
    full["w_memkv"] = _from_slots(g_wmem, "rows3")
    full["w_out"] = _from_slots(g_wout, "rows3")
    conv_w_full = _from_slots(g_cw, "cols3")
    memkv0, h_mem0 = mem_kv(0)
    o_mem0, lse_m0 = _mem_fwd(proj_a, memkv0, name="mem_fwd0", B=B, S=S, NM=NM, q_cb=3 * PM)
    o_cat0 = jnp.concatenate([o_main0, o_mem0], axis=1)
    x1 = _mm_fwd(o_cat0, full["w_out"][0], name="out_proj0", tm=min(1024, T), tn=1024, out_dtype=F32, res=x2d)
    g_up, g_dn = gathered("b0", x1)
    full["w_up"][0], full["w_down"][0] = _from_slots(g_up, "cols2"), _from_slots(g_dn, "rows2")
    x2, ffn_saved0 = conv_ffn_fwd(x1, 0)
    g_wb, g_wkv = gathered("a3", x2)
    wb, wkv = _from_slots(g_wb, "rows2"), _from_slots(g_wkv, "cols2")
    kv, h_kv =_mm_fwd(x2, wkv, name="kv_proj", tm=min(1024, T), tn=1536, out_dtype=BF16, g=ln_kv_g, save_h=True)
    proj_b, h_mix1 = _mm_fwd(x2, wb, name="in_proj_b", tm=min(1024, T), tn=1024, out_dtype=BF16, g=ln_mix_g[1],
                             save_h=True)
    o_main1, rt1 = _sb_fwd_g(proj_b, kv, kv, name="sb_fwd", B=B, S=S, P=PM, q_cb=0, k_cb=0, v_cb=PM,
                             G=FWD_HEAD_GROUP)
    memkv1, h_mem1 = mem_kv(1)
    o_mem1, lse_m1 = _mem_fwd(proj_b, memkv1, name="mem_fwd1", B=B, S=S, NM=NM, q_cb=PM)
    o_cat1 = jnp.concatenate([o_main1, o_mem1], axis=1)
    x3 = _mm_fwd(o_cat1, full["w_out"][1], name="out_proj1", tm=min(1024, T), tn=1024, out_dtype=F32, res=x2)
    g_up, g_dn = gathered("b1", x3)
    full["w_up"][1], full["w_down"][1] = _from_slots(g_up, "cols2"), _from_slots(g_dn, "rows2")
    x4, ffn_saved1 = conv_ffn_fwd(x3, 1)
    dx4, dg_final, loss_part = _loss_head(x4, final_g, tgt2d, name="loss_head")

    grads = {}
    small = {}
    reduce_groups = []

    def start_reduce(gname, keys, kinds):
        slots = [_to_slots(grads[k], kind) for k, kind in zip(keys, kinds)]
        ssem, rsem, thru, lands, tok = _scatter_start(slots, _slot, name=f"reduce_start_{gname}")
        reduce_groups.append((gname, keys, ssem, rsem, thru, lands))
        return tok[0, 0]

    def conv_ffn_bwd(dxo, xin, saved, layer):
        uc, ub, h, a = saved
        w_dn = full["w_down"][layer]
        da = _mm_nt(dxo, w_dn, name=f"d_act{layer}", tm=min(1024, T), tn=F // 2, out_dtype=F32)
        grads[("w_down", layer)] = _wgrad(a, dxo, f"g_w_down{layer}")
        cwl = conv_w_full[layer]
        du_g, du_v, p_g, p_v = _conv_gate_bwd(da, uc, ub, cwl, name=f"conv_bwd{layer}", B=B, S=S)
        small[("conv_w", layer)] = jnp.concatenate([p_g[0:3], p_v[0:3]], axis=1)
        small[("conv_b", layer)] = jnp.concatenate([p_g[3], p_v[3]], axis=0)
        grads[("w_up", layer)] = jnp.concatenate(
            [_wgrad(h, du_g, f"g_w_up_gate{layer}"), _wgrad(h, du_v, f"g_w_up_val{layer}")], axis=1)
        tok = start_reduce(f"ffn{layer}", [("w_down", layer), ("w_up", layer)], ["rows2", "cols2"])
        dxi, dg = _mm_nt_rmsbwd([(du_g, 0), (du_v, 1)], full["w_up"][layer], xin, ln_ffn_g[layer] + tok,
                                name=f"d_ffn_in{layer}", dres=dxo)
        small[("ln_ffn_g", layer)] = dg[0]
        return dxi

    def mem_bwd(proj, q_cb, memkv, h_mem, do_cat, o_mem, lse_m, layer):
        dqm, dmk, dmv = _mem_bwd(proj, memkv, do_cat, o_mem, lse_m, name=f"mem_bwd{layer}", B=B, S=S, NM=NM,
                                 q_cb=q_cb, do_cb=PM)
        grads[("w_memkv", layer)] = jnp.concatenate(
            [_wgrad(h_mem, dmk, f"g_w_memk{layer}"), _wgrad(h_mem, dmv, f"g_w_memv{layer}")], axis=1)
        _, dg = _mm_nt_rmsbwd([(dmk, 0), (dmv, 1)], full["w_memkv"][layer], mem2d, ln_mem_g[layer],
                              name=f"d_mem_in{layer}", want_dx=False)
        small[("ln_mem_g", layer)] = dg[0]
        return dqm

    dx3 = conv_ffn_bwd(dx4, x3, ffn_saved1, 1)
    do_cat1 = _mm_nt(dx3, full["w_out"][1], name="d_o_cat1", tm=min(1024, T), tn=1024, out_dtype=BF16)
    grads[("w_out", 1)] = _wgrad(o_cat1, dx3, "g_w_out1")
    dq1, dk1, dv1 = _sb_bwd_g(proj_b, kv, kv, do_cat1, rt1, name="sb_bwd", B=B, S=S, P=PM, q_cb=0, k_cb=0, v_cb=PM,
                            do_cb=0)
    dqm1 = mem_bwd(proj_b, PM, memkv1, h_mem1, do_cat1, o_mem1, lse_m1, 1)
    grads["w_in_b"] = jnp.concatenate([_wgrad(h_mix1, dq1, "g_w_in_b_q"), _wgrad(h_mix1, dqm1, "g_w_in_b_m")], axis=1)
    grads["w_kv"] = jnp.concatenate([_wgrad(h_kv, dk1, "g_w_kv_k"), _wgrad(h_kv, dv1, "g_w_kv_v")], axis=1)
    tok = start_reduce("mix1", [("w_out", 1), "w_in_b", "w_kv", ("w_memkv", 1)], ["rows2", "rows2", "cols2", "rows2"])
    dx2, dg = _mm_nt_rmsbwd([(dq1, 0), (dqm1, MAIN_W // MEM_W)], wb, x2, ln_mix_g[1] + tok, name="d_mix_in1", dres=dx3)
    small[("ln_mix_g", 1)] = dg[0]
    dx2, dg = _mm_nt_rmsbwd([(dk1, 0), (dv1, 1)], wkv, x2, ln_kv_g, name="d_kv_in", dres=dx2)
    small["ln_kv_g"] = dg[0]
    dx1 = conv_ffn_bwd(dx2, x1, ffn_saved0, 0)
    do_cat0 = _mm_nt(dx1, full["w_out"][0], name="d_o_cat0", tm=min(1024, T), tn=1024, out_dtype=BF16)
    grads[("w_out", 0)] = _wgrad(o_cat0, dx1, "g_w_out0")
    dq0, dk0, dv0, dcs = _fox_bwd_g(proj_a, proj_a, proj_a, do_cat0, lse0, cr, name="fox_bwd", B=B, S=S, P=PM, q_cb=0,
                                  k_cb=PM, v_cb=2 * PM, do_cb=0)
    dqm0 = mem_bwd(proj_a, 3 * PM, memkv0, h_mem0, do_cat0, o_mem0, lse_m0, 0)
    dc2d = _pad_rows(dcs[:, :, 0, :].transpose(0, 2, 1).reshape(T, N_MAIN_HEADS), LANES, 1)
    df, db_f = _forget_cumsum_bwd(dc2d, f_logit, b_f, B=B, S=S, name="forget_cumsum_bwd")
    a_parts = [(dq0, 0), (dk0, 1), (dv0, 2), (dqm0, n_qkv // MEM_W), (df, n_main // LANES)]
    g_wa = jnp.concatenate([_wgrad(h_mix0, p, f"g_w_in_a{k}") for k, (p, _) in enumerate(a_parts)], axis=1)
    grads["w_in_a"] = jnp.concatenate([g_wa[:, :n_qkv], g_wa[:, n_main:n_main + N_MAIN_HEADS], g_wa[:, n_qkv:n_main]],
                                      axis=1)
    tok = start_reduce("mix0", [("w_out", 0), ("w_memkv", 0), "w_in_a"], ["rows2", "rows2", "rows2"])
    dx0, dg = _mm_nt_rmsbwd(a_parts, wa, x2d, ln_mix_g[0] + tok, name="d_mix_in0", dres=dx1)
    small[("ln_mix_g", 0)] = dg[0]
    grad_x = dx0.reshape(B, S, D)

    def both_small(name):
        return jnp.stack([small[(name, 0)], small[(name, 1)]])

    small_list = [("ln_mix_g", both_small("ln_mix_g")), ("b_f_a", db_f[:, :N_MAIN_HEADS]), ("ln_kv_g", small["ln_kv_g"]),
                  ("ln_mem_g", both_small("ln_mem_g")), ("ln_ffn_g", both_small("ln_ffn_g")),
                  ("conv_w", both_small("conv_w")), ("conv_b", both_small("conv_b")), ("final_g", dg_final[0]),
                  ("loss", loss_part[0, :1])]
    sm_rows = []
    for _, a in small_list:
        flat = a.reshape(-1)
        sm_rows.append(_pad_rows(flat, _round_up(flat.size, 8 * LANES), 0).reshape(-1, LANES))
    spack = jnp.concatenate(sm_rows, axis=0)
    s_ssem, s_rsem, s_thru, s_lands, s_tok = _scatter_start([spack], _whole, name="small_start")

    pieces = {}
    for gname, keys, ssem, rsem, thru, lands in reduce_groups:
        thru, lands = _scatter_wait(ssem, rsem, thru, lands, _slot, s_tok, name=f"reduce_wait_{gname}")
        for key, mine, land in zip(keys, thru, lands):
            own = lax.dynamic_index_in_dim(mine, my_idx, 0, keepdims=False)
            land = lax.dynamic_update_index_in_dim(land, own, my_idx, 0)
            tag = key if isinstance(key, str) else f"{key[0]}{key[1]}"
            pieces[key] = _sum_slots(land, name=f"sum_{tag}")

    red = {}
    for n in ("w_in_a", "w_in_b", "w_kv"):
        red[n] = pieces[n].reshape(shards[n].shape)
    for n in ("w_memkv", "w_out", "w_up", "w_down"):
        red[n] = jnp.stack([pieces[(n, 0)], pieces[(n, 1)]])

    weights = {"ln_mix_g": ln_mix_g, "w_in_a": w_in_a, "b_f_a": b_f_a, "w_in_b": w_in_b, "ln_kv_g": ln_kv_g,
               "w_kv": w_kv, "ln_mem_g": ln_mem_g, "w_memkv": w_memkv, "w_out": w_out, "ln_ffn_g": ln_ffn_g,
               "w_up": w_up, "conv_w": conv_w, "conv_b": conv_b, "w_down": w_down, "final_g": final_g}
    m_in = {"ln_mix_g": m_ln_mix_g, "w_in_a": m_w_in_a, "b_f_a": m_b_f_a, "w_in_b": m_w_in_b, "ln_kv_g": m_ln_kv_g,
            "w_kv": m_w_kv, "ln_mem_g": m_ln_mem_g, "w_memkv": m_w_memkv, "w_out": m_w_out, "ln_ffn_g": m_ln_ffn_g,
            "w_up": m_w_up, "conv_w": m_conv_w, "conv_b": m_conv_b, "w_down": m_w_down, "final_g": m_final_g}
    v_in = {"ln_mix_g": v_ln_mix_g, "w_in_a": v_w_in_a, "b_f_a": v_b_f_a, "w_in_b": v_w_in_b, "ln_kv_g": v_ln_kv_g,
            "w_kv": v_w_kv, "ln_mem_g": v_ln_mem_g, "w_memkv": v_w_memkv, "w_out": v_w_out, "ln_ffn_g": v_ln_ffn_g,
            "w_up": v_w_up, "conv_w": v_conv_w, "conv_b": v_conv_b, "w_down": v_w_down, "final_g": v_final_g}
    order = list(weights)
    big_names = [n for n, _ in BIG]
    g_out, d_out, nm_out, nv_out = {}, {}, {}, {}

    def update(n):
        w = weights[n]
        cols = w.shape[-1]
        g = red[n].reshape(w.shape)
        d, nm, nv = _adamw(w.reshape(-1, cols), g.reshape(-1, cols), m_in[n].reshape(-1, cols),
                           v_in[n].reshape(-1, cols), name=f"adamw_{n}")
        g_out[n], d_out[n], nm_out[n], nv_out[n] = g, d.reshape(w.shape), nm.reshape(w.shape), nv.reshape(w.shape)

    for n in big_names:
        update(n)
    all_updated = jnp.stack([d_out[n].reshape(-1)[0] for n in big_names])
    s_thru, s_lands = _scatter_wait(s_ssem, s_rsem, s_thru, s_lands, _whole, all_updated, name="small_wait")
    ssum = _sum_slots(lax.dynamic_update_index_in_dim(s_lands[0], s_thru[0], my_idx, 0), name="sum_small")
    off = 0
    for (n, a), rows in zip(small_list, sm_rows):
        red[n] = ssum[off:off + rows.shape[0]].reshape(-1)[:a.size].reshape(a.shape)
        off += rows.shape[0]
    loss = red["loss"][0]
    shard_cols = conv_w.shape[2]
    red["conv_w"] = lax.dynamic_slice_in_dim(red["conv_w"], my_idx * shard_cols, shard_cols, axis=2)
    red["b_f_a"] = red["b_f_a"].reshape(b_f_a.shape)
    update("conv_w")
    small_names = [n for n in order if n not in g_out]

    def pack_small(src):
        rows = []
        for n in small_names:
            flat = src[n].reshape(-1)
            rows.append(_pad_rows(flat, _round_up(flat.size, 8 * LANES), 0).reshape(-1, LANES))
        return jnp.concatenate(rows, axis=0), [r.shape[0] for r in rows]

    red_small = {n: red[n].reshape(weights[n].shape) for n in small_names}
    wp, counts = pack_small(weights)
    gp, _ = pack_small(red_small)
    mp, _ = pack_small(m_in)
    vp, _ = pack_small(v_in)
    dp, nmp, nvp = _adamw(wp, gp, mp, vp, name="adamw_small")
    off = 0
    for n, cnt in zip(small_names, counts):
        shp = weights[n].shape
        size = weights[n].size
        g_out[n] = red_small[n]
        d_out[n] = dp[off:off + cnt].reshape(-1)[:size].reshape(shp)
        nm_out[n] = nmp[off:off + cnt].reshape(-1)[:size].reshape(shp)
        nv_out[n] = nvp[off:off + cnt].reshape(-1)[:size].reshape(shp)
        off += cnt

    return (loss, grad_x, *[g_out[n] for n in order], *[d_out[n] for n in order],
            *[nm_out[n] for n in order], *[nv_out[n] for n in order])
```

```python
import functools

import jax
import jax.numpy as jnp
from jax import lax
from jax.experimental import pallas as pl
from jax.experimental.pallas import tpu as pltpu

F32 = jnp.float32
BF16 = jnp.bfloat16
LANES = 128
HEAD_DIM = 64
N_MAIN_HEADS = 12
N_MEM_HEADS = 4
MAIN_W = N_MAIN_HEADS * HEAD_DIM
MEM_W = N_MEM_HEADS * HEAD_DIM
SCALE = HEAD_DIM ** -0.5
EPS = 1e-6
NEG = -1e30
N_DEV = 8
ATT_TILE = 256
MEM_Q_TILE = 1024
VMEM_BIG = 56 * 2 ** 20
MESH = pl.DeviceIdType.MESH

ADAM_LR = 0.001
ADAM_B1 = 0.9
ADAM_B2 = 0.999
ADAM_EPS = 1e-08
ADAM_WD = 0.01
ADAM_STEP = 10

NT = (((1,), (1,)), ((), ()))
TN = (((0,), (0,)), ((), ()))


def _pc(body, *, name, out_shape, grid=None, in_specs=None, out_specs=None, scratch_shapes=(),
        semantics=None, vmem=None):
    kw = {}
    if grid is not None:
        kw["grid"] = grid
    params = pltpu.CompilerParams(dimension_semantics=semantics, vmem_limit_bytes=vmem)
    return pl.pallas_call(body, name=name, out_shape=out_shape, in_specs=in_specs, out_specs=out_specs,
                          scratch_shapes=list(scratch_shapes), compiler_params=params, **kw)


def _sds(shape, dtype):
    return jax.ShapeDtypeStruct(shape, dtype)


def _mm_fwd(a, w, *, name, tm, tn, out_dtype, g=None, res=None, col0=0, ncols=None, save_h=False):
    m_rows, k = a.shape
    n = w.shape[1] if ncols is None else ncols
    grid = (m_rows // tm, n // tn)
    norm = g is not None

    def body(*refs):
        refs = list(refs)
        a_ref = refs.pop(0)
        g_ref = refs.pop(0) if norm else None
        w_ref = refs.pop(0)
        res_ref = refs.pop(0) if res is not None else None
        o_ref = refs.pop(0)
        hout_ref = refs.pop(0) if save_h else None
        h_ref = refs.pop(0) if norm else None
        if norm:
            @pl.when(pl.program_id(1) == 0)
            def _():
                xv = a_ref[...]
                r = lax.rsqrt(jnp.mean(xv * xv, axis=-1, keepdims=True) + EPS)
                h = ((xv * r) * g_ref[...]).astype(BF16)
                h_ref[...] = h
                if save_h:
                    hout_ref[...] = h
            lhs = h_ref[...]
        else:
            lhs = a_ref[...].astype(BF16)
        acc = jnp.dot(lhs, w_ref[...], preferred_element_type=F32)
        if res is not None:
            acc = acc + res_ref[...]
        o_ref[...] = acc.astype(out_dtype)

    in_specs = [pl.BlockSpec((tm, k), lambda i, j: (i, 0))]
    args = [a]
    if norm:
        in_specs.append(pl.BlockSpec((1, k), lambda i, j: (0, 0)))
        args.append(g.reshape(1, k))
    in_specs.append(pl.BlockSpec((k, tn), lambda i, j: (0, j + col0)))
    args.append(w)
    if res is not None:
        in_specs.append(pl.BlockSpec((tm, tn), lambda i, j: (i, j)))
        args.append(res)
    out_shape = [_sds((m_rows, n), out_dtype)]
    out_specs = [pl.BlockSpec((tm, tn), lambda i, j: (i, j))]
    if save_h:
        out_shape.append(_sds((m_rows, k), BF16))
        out_specs.append(pl.BlockSpec((tm, k), lambda i, j: (i, 0)))
    scratch = [pltpu.VMEM((tm, k), BF16)] if norm else []
    outs = _pc(body, name=name, out_shape=out_shape, grid=grid, in_specs=in_specs, out_specs=out_specs,
               scratch_shapes=scratch, semantics=("arbitrary", "arbitrary"), vmem=VMEM_BIG)(*args)
    return outs if save_h else outs[0]


def _mm_nt(a, w, *, name, tm, tn, out_dtype):
    m_rows, k = a.shape
    n = w.shape[0]

    def body(a_ref, w_ref, o_ref):
        acc = lax.dot_general(a_ref[...].astype(BF16), w_ref[...], NT, preferred_element_type=F32)
        o_ref[...] = acc.astype(out_dtype)

    return _pc(body, name=name, out_shape=_sds((m_rows, n), out_dtype), grid=(m_rows // tm, n // tn),
               in_specs=[pl.BlockSpec((tm, k), lambda i, j: (i, 0)), pl.BlockSpec((tn, k), lambda i, j: (j, 0))],
               out_specs=pl.BlockSpec((tm, tn), lambda i, j: (i, j)),
               semantics=("arbitrary", "arbitrary"), vmem=VMEM_BIG)(a, w)


def _mm_tn(a, b, *, name, ta, tn, tt):
    t_rows, ka = a.shape
    n = b.shape[1]
    nt = t_rows // tt

    def body(a_ref, b_ref, o_ref, acc_ref):
        t = pl.program_id(2)

        @pl.when(t == 0)
        def _():
            acc_ref[...] = jnp.zeros_like(acc_ref)

        acc_ref[...] += lax.dot_general(a_ref[...].astype(BF16), b_ref[...].astype(BF16), TN,
                                        preferred_element_type=F32)

        @pl.when(t == nt - 1)
        def _():
            o_ref[...] = acc_ref[...].astype(BF16)

    return _pc(body, name=name, out_shape=_sds((ka, n), BF16), grid=(ka // ta, n // tn, nt),
               in_specs=[pl.BlockSpec((tt, ta), lambda i, j, t: (t, i)),
                         pl.BlockSpec((tt, tn), lambda i, j, t: (t, j))],
               out_specs=pl.BlockSpec((ta, tn), lambda i, j, t: (i, j)),
               scratch_shapes=[pltpu.VMEM((ta, tn), F32)],
               semantics=("arbitrary", "arbitrary", "arbitrary"), vmem=VMEM_BIG)(a, b)


def _wgrad(a, b, name):
    t_rows, ka = a.shape
    n = b.shape[1]
    ta = ka if ka <= 1024 else ka // 2
    tn = n
    while ta * tn * 4 > 6 * 2 ** 20 and tn % 256 == 0:
        tn //= 2
    tt = min(2048, t_rows)
    return _mm_tn(a, b, name=name, ta=ta, tn=tn, tt=tt)


def _mm_nt_rmsbwd(parts, w, x, g, *, name, dres=None, want_dx=True):
    m_rows, d = x.shape
    tm = min(512, m_rows)
    n_parts = len(parts)

    def body(*refs):
        refs = list(refs)
        dy_refs = [refs.pop(0) for _ in range(n_parts)]
        w_refs = [refs.pop(0) for _ in range(n_parts)]
        x_ref = refs.pop(0)
        g_ref = refs.pop(0)
        dres_ref = refs.pop(0) if dres is not None else None
        dx_ref = refs.pop(0) if want_dx else None
        dg_ref = refs.pop(0)

        @pl.when(pl.program_id(0) == 0)
        def _():
            dg_ref[...] = jnp.zeros_like(dg_ref)

        dh = None
        for dy_ref, w_ref in zip(dy_refs, w_refs):
            t = lax.dot_general(dy_ref[...].astype(BF16), w_ref[...], NT, preferred_element_type=F32)
            dh = t if dh is None else dh + t
        xv = x_ref[...]
        r = lax.rsqrt(jnp.mean(xv * xv, axis=-1, keepdims=True) + EPS)
        xh = xv * r
        dg_ref[...] += jnp.sum(dh * xh, axis=0, keepdims=True)
        if want_dx:
            dhg = dh * g_ref[...]
            dx = r * (dhg - xh * jnp.mean(dhg * xh, axis=-1, keepdims=True))
            if dres is not None:
                dx = dx + dres_ref[...]
            dx_ref[...] = dx

    in_specs, args = [], []
    for dy, _ in parts:
        in_specs.append(pl.BlockSpec((tm, dy.shape[1]), lambda i: (i, 0)))
        args.append(dy)
    for dy, cb in parts:
        in_specs.append(pl.BlockSpec((d, dy.shape[1]), functools.partial(lambda i, cb: (0, cb), cb=cb)))
        args.append(w)
    in_specs += [pl.BlockSpec((tm, d), lambda i: (i, 0)), pl.BlockSpec((1, d), lambda i: (0, 0))]
    args += [x, g.reshape(1, d)]
    if dres is not None:
        in_specs.append(pl.BlockSpec((tm, d), lambda i: (i, 0)))
        args.append(dres)
    out_shape, out_specs = [], []
    if want_dx:
        out_shape.append(_sds((m_rows, d), F32))
        out_specs.append(pl.BlockSpec((tm, d), lambda i: (i, 0)))
    out_shape.append(_sds((1, d), F32))
    out_specs.append(pl.BlockSpec((1, d), lambda i: (0, 0)))
    outs = _pc(body, name=name, out_shape=out_shape, grid=(m_rows // tm,), in_specs=in_specs,
               out_specs=out_specs, semantics=("arbitrary",), vmem=VMEM_BIG)(*args)
    return (outs[0], outs[1]) if want_dx else (None, outs[0])


def _loss_head(x, g, tgt, *, name):
    m_rows, d = x.shape
    tm = min(512, m_rows)

    def body(x_ref, g_ref, t_ref, dx_ref, dg_ref, loss_ref):
        @pl.when(pl.program_id(0) == 0)
        def _():
            dg_ref[...] = jnp.zeros_like(dg_ref)
            loss_ref[...] = jnp.zeros_like(loss_ref)

        xv = x_ref[...]
        r = lax.rsqrt(jnp.mean(xv * xv, axis=-1, keepdims=True) + EPS)
        xh = xv * r
        gv = g_ref[...]
        err = xh * gv - t_ref[...]
        per_tok = jnp.mean(err * err, axis=-1, keepdims=True)
        loss_ref[...] += 0.5 * jnp.sum(per_tok, axis=0, keepdims=True)
        dout = err * (1.0 / d)
        dg_ref[...] += jnp.sum(dout * xh, axis=0, keepdims=True)
        dhg = dout * gv
        dx_ref[...] = r * (dhg - xh * jnp.mean(dhg * xh, axis=-1, keepdims=True))

    row = pl.BlockSpec((tm, d), lambda i: (i, 0))
    return _pc(body, name=name, out_shape=[_sds((m_rows, d), F32), _sds((1, d), F32), _sds((1, LANES), F32)],
               grid=(m_rows // tm,), in_specs=[row, pl.BlockSpec((1, d), lambda i: (0, 0)), row],
               out_specs=[row, pl.BlockSpec((1, d), lambda i: (0, 0)), pl.BlockSpec((1, LANES), lambda i: (0, 0))],
               semantics=("arbitrary",))(x, g.reshape(1, d), tgt)


def _split3(v):
    hi = v.astype(BF16)
    r1 = v - hi.astype(F32)
    mid = r1.astype(BF16)
    lo = (r1 - mid.astype(F32)).astype(BF16)
    return hi, mid, lo


def _split2(v):
    hi = v.astype(BF16)
    lo = (v - hi.astype(F32)).astype(BF16)
    return hi, lo


def _tri_dot3(tri, v):
    hi, mid, lo = _split3(v)
    return (jnp.dot(tri, hi, preferred_element_type=F32) + jnp.dot(tri, mid, preferred_element_type=F32)
            + jnp.dot(tri, lo, preferred_element_type=F32))


def _log_sigmoid(v):
    return jnp.minimum(v, 0.0) - jnp.log(1.0 + jnp.exp(-jnp.abs(v)))


def _forget_cumsum(f_logit, b_f, *, B, S, name):
    ch = min(256, S)
    nch = S // ch

    def body(f_ref, b_ref, c_ref):
        r_i = lax.broadcasted_iota(jnp.int32, (ch, ch), 0)
        c_i = lax.broadcasted_iota(jnp.int32, (ch, ch), 1)
        tri = (c_i <= r_i).astype(BF16)
        bv = b_ref[...]

        def step(k, carry):
            rows = pl.ds(pl.multiple_of(k * ch, ch), ch)
            lf = _log_sigmoid(f_ref[rows, :] + bv)
            c_ref[rows, :] = _tri_dot3(tri, lf) + carry
            return carry + jnp.sum(lf, axis=0, keepdims=True)

        lax.fori_loop(0, nch, step, jnp.zeros((1, LANES), F32))

    blk = pl.BlockSpec((S, LANES), lambda b: (b, 0))
    return _pc(body, name=name, out_shape=_sds((B * S, LANES), F32), grid=(B,),
               in_specs=[blk, pl.BlockSpec((1, LANES), lambda b: (0, 0))], out_specs=blk,
               semantics=("arbitrary",))(f_logit, b_f)


def _forget_cumsum_bwd(dc, f_logit, b_f, *, B, S, name):
    ch = min(256, S)
    nch = S // ch

    def body(dc_ref, f_ref, b_ref, df_ref, db_ref):
        @pl.when(pl.program_id(0) == 0)
        def _():
            db_ref[...] = jnp.zeros_like(db_ref)

        r_i = lax.broadcasted_iota(jnp.int32, (ch, ch), 0)
        c_i = lax.broadcasted_iota(jnp.int32, (ch, ch), 1)
        tri = (c_i >= r_i).astype(BF16)
        bv = b_ref[...]

        def step(kk, carry):
            tail, dbs = carry
            k = nch - 1 - kk
            rows = pl.ds(pl.multiple_of(k * ch, ch), ch)
            dcv = dc_ref[rows, :]
            dlf = _tri_dot3(tri, dcv) + tail
            z = f_ref[rows, :] + bv
            df = dlf * (1.0 / (1.0 + jnp.exp(z)))
            df_ref[rows, :] = df.astype(BF16)
            return tail + jnp.sum(dcv, axis=0, keepdims=True), dbs + jnp.sum(df, axis=0, keepdims=True)

        zero = jnp.zeros((1, LANES), F32)
        _, dbs = lax.fori_loop(0, nch, step, (zero, zero))
        db_ref[...] += dbs

    blk = pl.BlockSpec((S, LANES), lambda b: (b, 0))
    one = pl.BlockSpec((1, LANES), lambda b: (0, 0))
    return _pc(body, name=name, out_shape=[_sds((B * S, LANES), BF16), _sds((1, LANES), F32)], grid=(B,),
               in_specs=[blk, blk, one], out_specs=[blk, one], semantics=("arbitrary",))(dc, f_logit, b_f)


def _head_mask(lane, hh):
    return (lane < HEAD_DIM) if hh == 0 else (lane >= HEAD_DIM)


HEAD_GROUP = 3
FWD_HEAD_GROUP = 6


def _g_col_spec(rows, nblk_rows, cb, G):
    return pl.BlockSpec((rows, G * LANES), lambda b, p, i: (b * nblk_rows + i, cb // G + p))


def _g_kv_spec(rows, cb, G):
    return pl.BlockSpec((rows, G * LANES), lambda b, p, i: (b, cb // G + p))


def _g_stat_col_spec(tq, G):
    return pl.BlockSpec((1, 2 * G, tq, 1), lambda b, p, i: (b, p, i, 0))


def _g_stat_row_spec(S, G):
    return pl.BlockSpec((1, 2 * G, 1, S), lambda b, p, i: (b, p, 0, 0))


def _lanes(g):
    return slice(g * LANES, (g + 1) * LANES)


def _streams(x_ref, G, scale=None):
    rows = x_ref.shape[0]
    lane = lax.broadcasted_iota(jnp.int32, (rows, LANES), 1)
    out = []
    for g in range(G):
        x = x_ref[:, _lanes(g)]
        if scale is not None:
            x = x * jnp.asarray(scale, x.dtype)
        for hh in range(2):
            out.append(jnp.where(_head_mask(lane, hh), x, jnp.zeros_like(x)))
    return lane, out


def _wide(stat, width):
    return jnp.tile(stat, (1, width // LANES))


def _fold_lanes(v):
    out = v[:, :LANES]
    for j in range(1, v.shape[1] // LANES):
        out = out + v[:, j * LANES:(j + 1) * LANES]
    return out


def _kv_blocks(ref, ks, tk, G):
    return [ref[pl.ds(ks, tk), _lanes(g)] for g in range(G)]


def _sweep(i, block):
    def step(kb, c):
        block(kb, False)
        return c
    lax.fori_loop(0, i, step, 0)
    block(i, True)


def _fox_fwd_g(qa, ka, va, cr, *, name, B, S, P, q_cb, k_cb, v_cb, G=HEAD_GROUP):
    tq = tk = min(ATT_TILE, S)
    nq = S // tq
    NS = 2 * G

    def body(q_ref, k_ref, v_ref, cr_ref, o_ref, lse_ref, acc_ref, m_ref, l_ref):
        i = pl.program_id(2)
        lane, qh = _streams(q_ref, G, SCALE)
        on_or_below = (lax.broadcasted_iota(jnp.int32, (tq, tk), 1) <= lax.broadcasted_iota(jnp.int32, (tq, tk), 0))
        m_ref[...] = jnp.full(m_ref.shape, NEG, F32)
        l_ref[...] = jnp.zeros(l_ref.shape, F32)
        acc_ref[...] = jnp.zeros(acc_ref.shape, F32)

        def block(kb, diag):
            ks = pl.multiple_of(kb * tk, tk)
            kblk = _kv_blocks(k_ref, ks, tk, G)
            vblk = _kv_blocks(v_ref, ks, tk, G)
            ss = [lax.dot_general(qh[st], kblk[st // 2], NT, preferred_element_type=F32) for st in range(NS)]
            ps = []
            for st in range(NS):
                s = ss[st] - cr_ref[0, st, :, pl.ds(ks, tk)]
                if diag:
                    s = jnp.where(on_or_below, s, NEG)
                m = m_ref[st]
                m_new = jnp.maximum(m, jnp.max(s, axis=-1, keepdims=True))
                alpha = jnp.exp(m - m_new)
                p = jnp.exp(s - _wide(m_new, tk))
                m_ref[st] = m_new
                l_ref[st] = alpha * l_ref[st] + _fold_lanes(p)
                ps.append((alpha, p.astype(BF16)))
            pvs = [jnp.dot(ps[st][1], vblk[st // 2], preferred_element_type=F32) for st in range(NS)]
            for st in range(NS):
                acc_ref[st] = ps[st][0] * acc_ref[st] + pvs[st]

        _sweep(i, block)
        ls = [jnp.sum(l_ref[st], axis=-1, keepdims=True) for st in range(NS)]
        for st in range(NS):
            lse_ref[0, st] = jnp.max(m_ref[st], axis=-1, keepdims=True) + jnp.log(ls[st])
        for g in range(G):
            o_ref[:, _lanes(g)] = jnp.where(lane < HEAD_DIM, acc_ref[2 * g] / ls[2 * g],
                                            acc_ref[2 * g + 1] / ls[2 * g + 1]).astype(BF16)

    return _pc(body, name=name, out_shape=[_sds((B * S, P * LANES), BF16), _sds((B, 2 * P, S, 1), F32)],
               grid=(B, P // G, nq),
               in_specs=[_g_col_spec(tq, nq, q_cb, G), _g_kv_spec(S, k_cb, G), _g_kv_spec(S, v_cb, G),
                         _g_stat_row_spec(S, G)],
               out_specs=[_g_col_spec(tq, nq, 0, G), _g_stat_col_spec(tq, G)],
               scratch_shapes=[pltpu.VMEM((NS, tq, LANES), F32)] * 3,
               semantics=("arbitrary", "arbitrary", "arbitrary"), vmem=VMEM_BIG)(qa, ka, va, cr)


def _fox_bwd_g(qa, ka, va, doa, lse, cr, *, name, B, S, P, q_cb, k_cb, v_cb, do_cb, G=HEAD_GROUP):
    tq = tk = min(ATT_TILE, S)
    nq = S // tq
    NS = 2 * G

    def body(q_ref, k_ref, v_ref, do_ref, lse_ref, cr_ref, dq_ref, dk_ref, dv_ref, dcs_ref, dqa_ref, delta_ref, lse_s,
             p_buf, dp_buf):
        i = pl.program_id(2)

        @pl.when(i == 0)
        def _():
            dk_ref[...] = jnp.zeros_like(dk_ref)
            dv_ref[...] = jnp.zeros_like(dv_ref)
            dcs_ref[...] = jnp.zeros_like(dcs_ref)

        lane, qh = _streams(q_ref, G, SCALE)
        _, doh = _streams(do_ref, G)
        on_or_below = (lax.broadcasted_iota(jnp.int32, (tq, tk), 1) <= lax.broadcasted_iota(jnp.int32, (tq, tk), 0))
        delta_ref[...] = jnp.zeros(delta_ref.shape, F32)
        dqa_ref[...] = jnp.zeros(dqa_ref.shape, F32)
        for st in range(NS):
            lse_s[st] = jnp.broadcast_to(lse_ref[0, st], (tq, LANES))

        def probs(kb, diag):
            ks = pl.multiple_of(kb * tk, tk)
            kblk = _kv_blocks(k_ref, ks, tk, G)
            vblk = _kv_blocks(v_ref, ks, tk, G)
            ss = [lax.dot_general(qh[st], kblk[st // 2], NT, preferred_element_type=F32) for st in range(NS)]
            dps = [lax.dot_general(doh[st], vblk[st // 2], NT, preferred_element_type=F32) for st in range(NS)]
            ps = []
            for st in range(NS):
                s = ss[st] - cr_ref[0, st, :, pl.ds(ks, tk)]
                if diag:
                    s = jnp.where(on_or_below, s, NEG)
                ps.append(jnp.exp(s - _wide(lse_s[st], tk)))
            return ks, kblk, ps, dps

        def delta_block(kb, diag):
            _, _, ps, dps = probs(kb, diag)
            for st in range(NS):
                delta_ref[st] += _fold_lanes(ps[st] * dps[st])
                p_buf[st, kb] = ps[st]
                dp_buf[st, kb] = dps[st]

        _sweep(i, delta_block)
        for st in range(NS):
            delta_ref[st] = jnp.broadcast_to(jnp.sum(delta_ref[st], axis=-1, keepdims=True), (tq, LANES))

        def grad_block(kb, diag):
            ks = pl.multiple_of(kb * tk, tk)
            kblk = _kv_blocks(k_ref, ks, tk, G)
            rows = pl.ds(ks, tk)
            dsb, pb = [], []
            for st in range(NS):
                p = p_buf[st, kb]
                ds = p * (dp_buf[st, kb] - _wide(delta_ref[st], tk))
                dcs_ref[0, st, :, rows] -= jnp.sum(ds, axis=0, keepdims=True)
                dsb.append(ds.astype(BF16))
                pb.append(p.astype(BF16))
            dks = [lax.dot_general(dsb[st], qh[st], TN, preferred_element_type=F32) for st in range(NS)]
            dvs = [lax.dot_general(pb[st], doh[st], TN, preferred_element_type=F32) for st in range(NS)]
            dqs = [jnp.dot(dsb[st], kblk[st // 2], preferred_element_type=F32) for st in range(NS)]
            for g in range(G):
                dk_ref[rows, _lanes(g)] += dks[2 * g] + dks[2 * g + 1]
                dv_ref[rows, _lanes(g)] += dvs[2 * g] + dvs[2 * g + 1]
            for st in range(NS):
                dqa_ref[st] += dqs[st]

        _sweep(i, grad_block)
        for g in range(G):
            dq_ref[:, _lanes(g)] = (jnp.where(lane < HEAD_DIM, dqa_ref[2 * g], dqa_ref[2 * g + 1]) * SCALE).astype(BF16)

    return _pc(body, name=name,
               out_shape=[_sds((B * S, P * LANES), BF16), _sds((B * S, P * LANES), F32), _sds((B * S, P * LANES), F32),
                          _sds((B, 2 * P, 1, S), F32)],
               grid=(B, P // G, nq),
               in_specs=[_g_col_spec(tq, nq, q_cb, G), _g_kv_spec(S, k_cb, G), _g_kv_spec(S, v_cb, G),
                         _g_col_spec(tq, nq, do_cb, G), _g_stat_col_spec(tq, G), _g_stat_row_spec(S, G)],
               out_specs=[_g_col_spec(tq, nq, 0, G), _g_kv_spec(S, 0, G), _g_kv_spec(S, 0, G), _g_stat_row_spec(S, G)],
               scratch_shapes=[pltpu.VMEM((NS, tq, LANES), F32)] * 3 + [pltpu.VMEM((NS, nq, tq, tk), F32)] * 2,
               semantics=("arbitrary", "arbitrary", "arbitrary"), vmem=VMEM_BIG)(qa, ka, va, doa, lse, cr)


def _sb_logs_z(z):
    nz = -z
    lm = jnp.minimum(nz, 0.0) - jnp.log(1.0 + jnp.exp(jnp.minimum(z, nz)))
    return lm + z, lm


def _sb_fwd_g(qa, ka, va, *, name, B, S, P, q_cb, k_cb, v_cb, G=HEAD_GROUP):
    tq = tk = min(ATT_TILE, S)
    nq = S // tq
    NS = 2 * G

    def body(q_ref, k_ref, v_ref, o_ref, rt_ref, acc_ref, run_ref):
        i = pl.program_id(2)
        lane, qh = _streams(q_ref, G, SCALE)
        t_r = lax.broadcasted_iota(jnp.int32, (tk, tk), 0)
        t_c = lax.broadcasted_iota(jnp.int32, (tk, tk), 1)
        after = (t_r > t_c).astype(BF16)
        below = t_c < t_r
        acc_ref[...] = jnp.zeros(acc_ref.shape, F32)
        run_ref[...] = jnp.zeros(run_ref.shape, F32)

        def block(kb, diag):
            ks = pl.multiple_of(kb * tk, tk)
            kblk = _kv_blocks(k_ref, ks, tk, G)
            vblk = _kv_blocks(v_ref, ks, tk, G)
            zs = [lax.dot_general(qh[st], kblk[st // 2], NT, preferred_element_type=F32) for st in range(NS)]
            lss, parts = [], []
            for st in range(NS):
                ls, lm = _sb_logs_z(zs[st])
                if diag:
                    lm = jnp.where(below, lm, 0.0)
                lss.append(ls + _wide(run_ref[st], tk))
                run_ref[st] += jnp.sum(lm, axis=-1, keepdims=True)
                parts.append(_split2(lm))
            sufs = [jnp.dot(parts[st][0], after, preferred_element_type=F32)
                    + jnp.dot(parts[st][1], after, preferred_element_type=F32) for st in range(NS)]
            ab = []
            for st in range(NS):
                a = jnp.exp(lss[st] + sufs[st])
                if diag:
                    a = jnp.where(below, a, 0.0)
                ab.append(a.astype(BF16))
            pvs = [jnp.dot(ab[st], vblk[st // 2], preferred_element_type=F32) for st in range(NS)]
            for st in range(NS):
                acc_ref[st] += pvs[st]

        block(i, True)

        def step(jj, c):
            block(i - 1 - jj, False)
            return c

        lax.fori_loop(0, i, step, 0)
        for st in range(NS):
            rt_ref[0, st] = jnp.max(run_ref[st], axis=-1, keepdims=True)
        for g in range(G):
            o_ref[:, _lanes(g)] = jnp.where(lane < HEAD_DIM, acc_ref[2 * g], acc_ref[2 * g + 1]).astype(BF16)

    return _pc(body, name=name, out_shape=[_sds((B * S, P * LANES), BF16), _sds((B, 2 * P, S, 1), F32)],
               grid=(B, P // G, nq),
               in_specs=[_g_col_spec(tq, nq, q_cb, G), _g_kv_spec(S, k_cb, G), _g_kv_spec(S, v_cb, G)],
               out_specs=[_g_col_spec(tq, nq, 0, G), _g_stat_col_spec(tq, G)],
               scratch_shapes=[pltpu.VMEM((NS, tq, LANES), F32)] * 2,
               semantics=("arbitrary", "arbitrary", "arbitrary"), vmem=VMEM_BIG)(qa, ka, va)


def _sb_bwd_g(qa, ka, va, doa, rt, *, name, B, S, P, q_cb, k_cb, v_cb, do_cb, G=HEAD_GROUP):
    tq = tk = min(ATT_TILE, S)
    nq = S // tq
    NS = 2 * G

    def body(q_ref, k_ref, v_ref, do_ref, rt_ref, dq_ref, dk_ref, dv_ref, dqa_ref, pl_ref, pg_ref):
        i = pl.program_id(2)

        @pl.when(i == 0)
        def _():
            dk_ref[...] = jnp.zeros_like(dk_ref)
            dv_ref[...] = jnp.zeros_like(dv_ref)

        lane, qh = _streams(q_ref, G, SCALE)
        _, doh = _streams(do_ref, G)
        t_r = lax.broadcasted_iota(jnp.int32, (tk, tk), 0)
        t_c = lax.broadcasted_iota(jnp.int32, (tk, tk), 1)
        upto = (t_r <= t_c).astype(BF16)
        before = (t_r < t_c).astype(BF16)
        below = t_c < t_r
        dqa_ref[...] = jnp.zeros(dqa_ref.shape, F32)
        pg_ref[...] = jnp.zeros(pg_ref.shape, F32)
        for st in range(NS):
            pl_ref[st] = jnp.broadcast_to(rt_ref[0, st], (tq, LANES))

        def block(kb, diag):
            ks = pl.multiple_of(kb * tk, tk)
            rows = pl.ds(ks, tk)
            kblk = _kv_blocks(k_ref, ks, tk, G)
            vblk = _kv_blocks(v_ref, ks, tk, G)
            zs = [lax.dot_general(qh[st], kblk[st // 2], NT, preferred_element_type=F32) for st in range(NS)]
            das = [lax.dot_general(doh[st], vblk[st // 2], NT, preferred_element_type=F32) for st in range(NS)]
            lss, parts = [], []
            for st in range(NS):
                ls, lm = _sb_logs_z(zs[st])
                if diag:
                    lm = jnp.where(below, lm, 0.0)
                lss.append((ls, ls + _wide(pl_ref[st], tk)))
                pl_ref[st] -= jnp.sum(lm, axis=-1, keepdims=True)
                parts.append(_split2(lm))
            pins = [jnp.dot(parts[st][0], upto, preferred_element_type=F32)
                    + jnp.dot(parts[st][1], upto, preferred_element_type=F32) for st in range(NS)]
            gms, ab, gparts = [], [], []
            for st in range(NS):
                a = jnp.exp(lss[st][1] - pins[st])
                if diag:
                    a = jnp.where(below, a, 0.0)
                gm = a * das[st]
                gms.append(gm)
                ab.append(a.astype(BF16))
                gparts.append(gm.astype(BF16))
            pgs = [jnp.dot(gparts[st], before, preferred_element_type=F32) for st in range(NS)]
            dzb = []
            for st in range(NS):
                gm = gms[st]
                dz = gm - jnp.exp(lss[st][0]) * (gm + (pgs[st] + _wide(pg_ref[st], tk)))
                if diag:
                    dz = jnp.where(below, dz, 0.0)
                pg_ref[st] += jnp.sum(gm, axis=-1, keepdims=True)
                dzb.append(dz.astype(BF16))
            dks = [lax.dot_general(dzb[st], qh[st], TN, preferred_element_type=F32) for st in range(NS)]
            dvs = [lax.dot_general(ab[st], doh[st], TN, preferred_element_type=F32) for st in range(NS)]
            dqs = [jnp.dot(dzb[st], kblk[st // 2], preferred_element_type=F32) for st in range(NS)]
            for g in range(G):
                dk_ref[rows, _lanes(g)] += dks[2 * g] + dks[2 * g + 1]
                dv_ref[rows, _lanes(g)] += dvs[2 * g] + dvs[2 * g + 1]
            for st in range(NS):
                dqa_ref[st] += dqs[st]

        _sweep(i, block)
        for g in range(G):
            dq_ref[:, _lanes(g)] = (jnp.where(lane < HEAD_DIM, dqa_ref[2 * g], dqa_ref[2 * g + 1]) * SCALE).astype(BF16)

    return _pc(body, name=name,
               out_shape=[_sds((B * S, P * LANES), BF16), _sds((B * S, P * LANES), F32), _sds((B * S, P * LANES), F32)],
               grid=(B, P // G, nq),
               in_specs=[_g_col_spec(tq, nq, q_cb, G), _g_kv_spec(S, k_cb, G), _g_kv_spec(S, v_cb, G),
                         _g_col_spec(tq, nq, do_cb, G), _g_stat_col_spec(tq, G)],
               out_specs=[_g_col_spec(tq, nq, 0, G), _g_kv_spec(S, 0, G), _g_kv_spec(S, 0, G)],
               scratch_shapes=[pltpu.VMEM((NS, tq, LANES), F32)] * 3,
               semantics=("arbitrary", "arbitrary", "arbitrary"), vmem=VMEM_BIG)(qa, ka, va, doa, rt)


MEM_GROUP = N_MEM_HEADS // 2


def _mem_fwd(qa, kva, *, name, B, S, NM, q_cb):
    G = MEM_GROUP
    NS = 2 * G
    tq = min(MEM_Q_TILE, S)
    nq = S // tq

    def body(q_ref, k_ref, v_ref, o_ref, lse_ref):
        lane, qh = _streams(q_ref, G, SCALE)
        kblk = [k_ref[:, _lanes(g)] for g in range(G)]
        vblk = [v_ref[:, _lanes(g)] for g in range(G)]
        ss = [lax.dot_general(qh[st], kblk[st // 2], NT, preferred_element_type=F32) for st in range(NS)]
        pb, ls = [], []
        for st in range(NS):
            m = jnp.max(ss[st], axis=-1, keepdims=True)
            p = jnp.exp(ss[st] - m)
            l = jnp.sum(p, axis=-1, keepdims=True)
            lse_ref[0, st] = m + jnp.log(l)
            pb.append(p.astype(BF16))
            ls.append(l)
        pvs = [jnp.dot(pb[st], vblk[st // 2], preferred_element_type=F32) for st in range(NS)]
        for g in range(G):
            o_ref[:, _lanes(g)] = jnp.where(lane < HEAD_DIM, pvs[2 * g] / ls[2 * g],
                                            pvs[2 * g + 1] / ls[2 * g + 1]).astype(BF16)

    return _pc(body, name=name, out_shape=[_sds((B * S, G * LANES), BF16), _sds((B, NS, S, 1), F32)],
               grid=(B, 1, nq),
               in_specs=[_g_col_spec(tq, nq, q_cb, G), _g_kv_spec(NM, 0, G), _g_kv_spec(NM, G, G)],
               out_specs=[_g_col_spec(tq, nq, 0, G), _g_stat_col_spec(tq, G)],
               semantics=("arbitrary", "arbitrary", "arbitrary"), vmem=VMEM_BIG)(qa, kva, kva)


def _mem_bwd(qa, kva, doa, oa, lse, *, name, B, S, NM, q_cb, do_cb):
    G = MEM_GROUP
    NS = 2 * G
    tq = min(MEM_Q_TILE, S)
    nq = S // tq

    def body(q_ref, k_ref, v_ref, do_ref, o_ref, lse_ref, dq_ref, dk_ref, dv_ref):
        @pl.when(pl.program_id(2) == 0)
        def _():
            dk_ref[...] = jnp.zeros_like(dk_ref)
            dv_ref[...] = jnp.zeros_like(dv_ref)

        lane, qh = _streams(q_ref, G, SCALE)
        _, doh = _streams(do_ref, G)
        kblk = [k_ref[:, _lanes(g)] for g in range(G)]
        vblk = [v_ref[:, _lanes(g)] for g in range(G)]
        prod = [do_ref[:, _lanes(g)].astype(F32) * o_ref[:, _lanes(g)].astype(F32) for g in range(G)]
        ss = [lax.dot_general(qh[st], kblk[st // 2], NT, preferred_element_type=F32) for st in range(NS)]
        dps = [lax.dot_general(doh[st], vblk[st // 2], NT, preferred_element_type=F32) for st in range(NS)]
        dsb, pb = [], []
        for st in range(NS):
            delta = jnp.sum(jnp.where(_head_mask(lane, st % 2), prod[st // 2], 0.0), axis=-1, keepdims=True)
            p = jnp.exp(ss[st] - lse_ref[0, st])
            dsb.append((p * (dps[st] - delta)).astype(BF16))
            pb.append(p.astype(BF16))
        dks = [lax.dot_general(dsb[st], qh[st], TN, preferred_element_type=F32) for st in range(NS)]
        dvs = [lax.dot_general(pb[st], doh[st], TN, preferred_element_type=F32) for st in range(NS)]
        dqs = [jnp.dot(dsb[st], kblk[st // 2], preferred_element_type=F32) for st in range(NS)]
        for g in range(G):
            dk_ref[:, _lanes(g)] += dks[2 * g] + dks[2 * g + 1]
            dv_ref[:, _lanes(g)] += dvs[2 * g] + dvs[2 * g + 1]
            dq_ref[:, _lanes(g)] = (jnp.where(lane < HEAD_DIM, dqs[2 * g], dqs[2 * g + 1]) * SCALE).astype(BF16)

    return _pc(body, name=name,
               out_shape=[_sds((B * S, G * LANES), BF16), _sds((B * NM, G * LANES), F32), _sds((B * NM, G * LANES), F32)],
               grid=(B, 1, nq),
               in_specs=[_g_col_spec(tq, nq, q_cb, G), _g_kv_spec(NM, 0, G), _g_kv_spec(NM, G, G),
                         _g_col_spec(tq, nq, do_cb, G), _g_col_spec(tq, nq, 0, G), _g_stat_col_spec(tq, G)],
               out_specs=[_g_col_spec(tq, nq, 0, G), _g_kv_spec(NM, 0, G), _g_kv_spec(NM, 0, G)],
               semantics=("arbitrary", "arbitrary", "arbitrary"), vmem=VMEM_BIG)(qa, kva, kva, doa, oa, lse)


def _sigmoid(v):
    return 0.5 * jnp.tanh(0.5 * v) + 0.5


def _shift_rows(cur, halo_ref, first, rows_idx, k):
    out = pltpu.roll(cur, k, 0)
    top = out[0:8, :]
    for r in range(k):
        hr = halo_ref.shape[0] - k + r
        edge = jnp.where(first, 0.0, halo_ref[hr:hr + 1, :])
        top = jnp.where(rows_idx[0:8, :] == r, edge, top)
    return jnp.concatenate([top, out[8:, :]], axis=0)


def _shift_rows_up(cur, halo_ref, last, rows_idx, k, ts):
    out = pltpu.roll(cur, ts - k, 0)
    bottom = out[ts - 8:, :]
    for r in range(k):
        edge = jnp.where(last, 0.0, halo_ref[r:r + 1, :])
        bottom = jnp.where(rows_idx[0:8, :] == 8 - k + r, edge, bottom)
    return jnp.concatenate([out[:ts - 8, :], bottom], axis=0)


def _ffn_up_gate(x, g, w, cw, cb, *, name, S):
    T, D = x.shape
    F = w.shape[1] // 2
    tm = min(1024, S)
    tn = 256
    nj = F // tn
    tiles_per_seq = S // tm
    halo = 16

    def body(x_ref, xh_ref, g_ref, wg_ref, wv_ref, cwg_ref, cwv_ref, cbg_ref, cbv_ref,
             uc_ref, ub_ref, a_ref, hout_ref, h_ref, hh_ref, eg_ref, ev_ref):
        first = lax.rem(pl.program_id(0), tiles_per_seq) == 0

        @pl.when(pl.program_id(1) == 0)
        def _():
            def norm(v):
                r = lax.rsqrt(jnp.mean(v * v, axis=-1, keepdims=True) + EPS)
                return ((v * r) * g_ref[...]).astype(BF16)
            h = norm(x_ref[...])
            h_ref[...] = h
            hout_ref[...] = h
            hh_ref[...] = norm(xh_ref[...])

        h = h_ref[...]
        rows_idx = lax.broadcasted_iota(jnp.int32, (tm, tn), 0)
        uc = []
        for half, (w_ref, cw_ref, cb_ref, e_ref) in enumerate(((wg_ref, cwg_ref, cbg_ref, eg_ref),
                                                               (wv_ref, cwv_ref, cbv_ref, ev_ref))):
            acc = jnp.dot(h, w_ref[...], preferred_element_type=F32)
            e_ref[...] = jnp.dot(hh_ref[...], w_ref[...], preferred_element_type=F32)
            ub_ref[half] = acc.astype(BF16)
            m1 = _shift_rows(acc, e_ref, first, rows_idx, 1)
            m2 = _shift_rows(acc, e_ref, first, rows_idx, 2)
            uc.append(cb_ref[...] + cw_ref[0:1, :] * m2 + cw_ref[1:2, :] * m1 + cw_ref[2:3, :] * acc)
            uc_ref[half] = uc[half]
        a_ref[...] = (uc[0] * _sigmoid(uc[0]) * uc[1]).astype(BF16)

    in_specs = [pl.BlockSpec((tm, D), lambda i, j: (i, 0)),
                pl.BlockSpec((halo, D), lambda i, j: (jnp.maximum(i * (tm // halo) - 1, 0), 0)),
                pl.BlockSpec((1, D), lambda i, j: (0, 0)),
                pl.BlockSpec((D, tn), lambda i, j: (0, j)), pl.BlockSpec((D, tn), lambda i, j: (0, j + nj)),
                pl.BlockSpec((3, tn), lambda i, j: (0, j)), pl.BlockSpec((3, tn), lambda i, j: (0, j + nj)),
                pl.BlockSpec((1, tn), lambda i, j: (0, j)), pl.BlockSpec((1, tn), lambda i, j: (0, j + nj))]
    return _pc(body, name=name,
               out_shape=[_sds((2, T, F), F32), _sds((2, T, F), BF16), _sds((T, F), BF16), _sds((T, D), BF16)],
               grid=(T // tm, nj), in_specs=in_specs,
               out_specs=[pl.BlockSpec((2, tm, tn), lambda i, j: (0, i, j)), pl.BlockSpec((2, tm, tn), lambda i, j: (0, i, j)),
                          pl.BlockSpec((tm, tn), lambda i, j: (i, j)), pl.BlockSpec((tm, D), lambda i, j: (i, 0))],
               scratch_shapes=[pltpu.VMEM((tm, D), BF16), pltpu.VMEM((halo, D), BF16),
                               pltpu.VMEM((halo, tn), F32), pltpu.VMEM((halo, tn), F32)],
               semantics=("arbitrary", "arbitrary"), vmem=VMEM_BIG)(x, x, g.reshape(1, D), w, w, cw, cw, cb, cb)


def _conv_gate_bwd(da, uc, ub, cw, *, name, B, S):
    F = uc.shape[2]
    tf = F // 2
    ts = min(512, S)
    ns, nf = S // ts, F // tf

    def body(da_ref, uc_ref, ub_ref, wg_ref, wv_ref, dug_ref, duv_ref, pg_ref, pv_ref, nxt_g, nxt_v):
        last = pl.program_id(2) == 0

        @pl.when(jnp.logical_and(pl.program_id(1) == 0, last))
        def _():
            pg_ref[...] = jnp.zeros_like(pg_ref)
            pv_ref[...] = jnp.zeros_like(pv_ref)

        rows_idx = lax.broadcasted_iota(jnp.int32, (ts, tf), 0)
        ucg, ucv = uc_ref[0], uc_ref[1]
        sg = _sigmoid(ucg)
        dav = da_ref[...]
        d_v = dav * (ucg * sg)
        d_g = dav * ucv * (sg * (1.0 + ucg * (1.0 - sg)))
        for half, (o_ref, p_ref, d, w_ref, nxt) in enumerate(((dug_ref, pg_ref, d_g, wg_ref, nxt_g),
                                                               (duv_ref, pv_ref, d_v, wv_ref, nxt_v))):
            p1 = _shift_rows_up(d, nxt, last, rows_idx, 1, ts)
            p2 = _shift_rows_up(d, nxt, last, rows_idx, 2, ts)
            o_ref[...] = (w_ref[2:3, :] * d + w_ref[1:2, :] * p1 + w_ref[0:1, :] * p2).astype(BF16)
            nxt[...] = d[0:8, :]
            uh = ub_ref[half].astype(F32)
            for k, dk in enumerate((p2, p1, d)):
                p_ref[k:k + 1, :] += jnp.sum(dk * uh, axis=0, keepdims=True)
            p_ref[3:4, :] += jnp.sum(d, axis=0, keepdims=True)

    row = pl.BlockSpec((ts, tf), lambda j, b, r: (b * ns + ns - 1 - r, j))
    both = pl.BlockSpec((2, ts, tf), lambda j, b, r: (0, b * ns + ns - 1 - r, j))
    par = pl.BlockSpec((8, tf), lambda j, b, r: (0, j))
    return _pc(body, name=name,
               out_shape=[_sds((B * S, F), BF16), _sds((B * S, F), BF16), _sds((8, F), F32), _sds((8, F), F32)],
               grid=(nf, B, ns),
               in_specs=[row, both, both, pl.BlockSpec((3, tf), lambda j, b, r: (0, j)),
                         pl.BlockSpec((3, tf), lambda j, b, r: (0, j + nf))],
               out_specs=[row, row, par, par],
               scratch_shapes=[pltpu.VMEM((8, tf), F32), pltpu.VMEM((8, tf), F32)],
               semantics=("arbitrary", "arbitrary", "arbitrary"), vmem=VMEM_BIG)(da, uc, ub, cw, cw)


def _adamw(w, g, m, v, *, name):
    rows, cols = w.shape
    tr = rows
    while tr * cols * 4 > 2 ** 20 and tr % 16 == 0:
        tr //= 2

    def body(w_ref, g_ref, m_ref, v_ref, d_ref, nm_ref, nv_ref):
        gv = g_ref[...]
        m_new = ADAM_B1 * m_ref[...] + (1.0 - ADAM_B1) * gv
        v_new = ADAM_B2 * v_ref[...] + (1.0 - ADAM_B2) * (gv * gv)
        m_hat = m_new / (1.0 - ADAM_B1 ** ADAM_STEP)
        v_hat = v_new / (1.0 - ADAM_B2 ** ADAM_STEP)
        d_ref[...] = -ADAM_LR * (m_hat / (jnp.sqrt(v_hat) + ADAM_EPS) + ADAM_WD * w_ref[...])
        nm_ref[...] = m_new
        nv_ref[...] = v_new

    blk = pl.BlockSpec((tr, cols), lambda i: (i, 0))
    return _pc(body, name=name, out_shape=[_sds((rows, cols), F32)] * 3, grid=(rows // tr,),
               in_specs=[blk] * 4, out_specs=[blk] * 3, semantics=("arbitrary",))(w, g, m, v)


def _my_pos():
    return lax.axis_index("x"), lax.axis_index("y"), lax.axis_index("c")


_HBM = pl.BlockSpec(memory_space=pltpu.HBM)
_SEM = pl.BlockSpec(memory_space=pltpu.SEMAPHORE)
_EFFECT = pltpu.SideEffectType.DATAFLOW_SIDE_EFFECTING


def _peers():
    x, y, c = _my_pos()
    out = []
    for k in range(1, N_DEV):
        px, py, pc = x ^ ((k >> 2) & 1), y ^ ((k >> 1) & 1), c ^ (k & 1)
        out.append(((px, py, pc), 4 * px + 2 * py + pc))
    return out


def _scatter_start(srcs, slot_of, *, name):
    n = len(srcs)
    lands = [lax.empty((N_DEV,) + slot_of(s, 0, shape_only=True), s.dtype) for s in srcs]

    def body(*refs):
        src_refs, land_refs = refs[:n], refs[n:2 * n]
        send_sems, recv_sems = refs[2 * n], refs[2 * n + 1]
        token = refs[-1]
        x, y, c = _my_pos()
        me = 4 * x + 2 * y + c
        for a in range(n):
            for k, (peer, peer_idx) in enumerate(_peers()):
                pltpu.make_async_remote_copy(
                    src_ref=slot_of(src_refs[a], peer_idx), dst_ref=land_refs[a].at[me],
                    send_sem=send_sems.at[a * 7 + k], recv_sem=recv_sems.at[a * 7 + k],
                    device_id=peer, device_id_type=MESH).start()
        token[...] = jnp.zeros_like(token)

    hbm = lambda a: pltpu.HBM(a.shape, a.dtype)
    args = [pltpu.with_memory_space_constraint(a, pltpu.HBM) for a in list(srcs) + lands]
    outs = pl.pallas_call(
        body, name=name,
        out_shape=(pltpu.SemaphoreType.DMA((7 * n,)), pltpu.SemaphoreType.DMA((7 * n,)),
                   *[hbm(a) for a in srcs], *[hbm(a) for a in lands], _sds((8, LANES), F32)),
        in_specs=[_HBM] * (2 * n),
        out_specs=(_SEM, _SEM, *([_HBM] * (2 * n)), pl.BlockSpec(memory_space=pltpu.VMEM)),
        input_output_aliases={a: 2 + a for a in range(2 * n)},
        compiler_params=pltpu.CompilerParams(has_side_effects=_EFFECT))(*args)
    return outs[0], outs[1], list(outs[2:2 + n]), list(outs[2 + n:2 + 2 * n]), outs[-1]


def _scatter_wait(send_sems, recv_sems, srcs, lands, slot_of, after, *, name, first=0):
    n = len(srcs)

    def body(*refs):
        src_refs, land_refs = refs[:n], refs[n:2 * n]
        ssem, rsem = refs[2 * n], refs[2 * n + 1]
        x, y, c = _my_pos()
        me = 4 * x + 2 * y + c
        for a in range(n):
            for k, (peer, peer_idx) in enumerate(_peers()):
                cp = pltpu.make_async_remote_copy(
                    src_ref=slot_of(src_refs[a], peer_idx), dst_ref=land_refs[a].at[me],
                    send_sem=ssem.at[(first + a) * 7 + k], recv_sem=rsem.at[(first + a) * 7 + k],
                    device_id=peer, device_id_type=MESH)
                cp.wait_send()
                cp.wait_recv()

    hbm = lambda a: pltpu.HBM(a.shape, a.dtype)
    outs = pl.pallas_call(
        body, name=name, out_shape=tuple(hbm(a) for a in list(srcs) + list(lands)),
        in_specs=[_HBM] * (2 * n) + [_SEM, _SEM, pl.BlockSpec(memory_space=pl.ANY)],
        out_specs=tuple([_HBM] * (2 * n)), input_output_aliases={a: a for a in range(2 * n)},
        compiler_params=pltpu.CompilerParams(has_side_effects=_EFFECT))(*srcs, *lands, send_sems, recv_sems, after)
    return list(outs[:n]), list(outs[n:])


def _whole(a, peer_idx, shape_only=False):
    return a.shape if shape_only else a


def _slot(a, peer_idx, shape_only=False):
    return a.shape[1:] if shape_only else a.at[peer_idx]


def _sum_slots(a, *, name, tr=None):
    rows, cols = a.shape[1], a.shape[2]
    if tr is None:
        tr = rows
        while N_DEV * tr * cols * a.dtype.itemsize > 3 * 2 ** 20 and tr % 32 == 0:
            tr //= 2

    def body(a_ref, o_ref):
        acc = a_ref[0].astype(F32)
        for j in range(1, N_DEV):
            acc = acc + a_ref[j].astype(F32)
        o_ref[...] = acc

    return _pc(body, name=name, out_shape=_sds((rows, cols), F32), grid=(rows // tr,),
               in_specs=[pl.BlockSpec((N_DEV, tr, cols), lambda i: (0, i, 0))],
               out_specs=pl.BlockSpec((tr, cols), lambda i: (i, 0)), semantics=("arbitrary",), vmem=VMEM_BIG)(a)


def _to_slots(full, kind):
    if kind == "rows2":
        r, c = full.shape
        return full.reshape(N_DEV, r // N_DEV, c)
    if kind == "cols2":
        r, c = full.shape
        return full.reshape(r, N_DEV, c // N_DEV).transpose(1, 0, 2)
    if kind == "rows3":
        l, r, c = full.shape
        return full.reshape(l, N_DEV, r // N_DEV, c).transpose(1, 0, 2, 3)
    if kind == "cols3":
        l, r, c = full.shape
        return full.reshape(l, r, N_DEV, c // N_DEV).transpose(2, 0, 1, 3)
    raise ValueError(kind)


def _from_slots(slots, kind):
    if kind == "rows2":
        _, r, c = slots.shape
        return slots.reshape(N_DEV * r, c)
    if kind == "cols2":
        _, r, c = slots.shape
        return slots.transpose(1, 0, 2).reshape(r, N_DEV * c)
    if kind == "rows3":
        _, l, r, c = slots.shape
        return slots.transpose(1, 0, 2, 3).reshape(l, N_DEV * r, c)
    if kind == "cols3":
        _, l, r, c = slots.shape
        return slots.transpose(1, 2, 0, 3).reshape(l, r, N_DEV * c)
    raise ValueError(kind)


BIG = (("w_in_a", "rows2"), ("w_in_b", "rows2"), ("w_kv", "cols2"), ("w_memkv", "rows3"),
       ("w_out", "rows3"), ("w_up", "cols3"), ("w_down", "rows3"))


def _round_up(n, m):
    return -(-n // m) * m


def _pad_rows(a, rows, axis):
    pad = [(0, 0)] * a.ndim
    pad[axis] = (0, rows - a.shape[axis])
    return jnp.pad(a, pad)


def kernel(x, mem, ln_mix_g, w_in_a, b_f_a, w_in_b, ln_kv_g, w_kv, ln_mem_g, w_memkv, w_out, ln_ffn_g, w_up, conv_w, conv_b, w_down, final_g, loss_target, m_ln_mix_g, m_w_in_a, m_b_f_a, m_w_in_b, m_ln_kv_g, m_w_kv, m_ln_mem_g, m_w_memkv, m_w_out, m_ln_ffn_g, m_w_up, m_conv_w, m_conv_b, m_w_down, m_final_g, v_ln_mix_g, v_w_in_a, v_b_f_a, v_w_in_b, v_ln_kv_g, v_w_kv, v_ln_mem_g, v_w_memkv, v_w_out, v_ln_ffn_g, v_w_up, v_conv_w, v_conv_b, v_w_down, v_final_g):
    B, S, D = x.shape
    NM = mem.shape[1]
    T = B * S
    F = w_down.shape[1] * N_DEV
    my_idx = 4 * lax.axis_index("x") + 2 * lax.axis_index("y") + lax.axis_index("c")

    shards = {"w_in_a": w_in_a[0], "w_in_b": w_in_b[0], "w_kv": w_kv, "w_memkv": w_memkv, "w_out": w_out,
              "w_up": w_up, "w_down": w_down}
    moms = {"w_in_a": (m_w_in_a[0], v_w_in_a[0]), "w_in_b": (m_w_in_b[0], v_w_in_b[0]), "w_kv": (m_w_kv, v_w_kv),
            "w_memkv": (m_w_memkv, v_w_memkv), "w_out": (m_w_out, v_w_out), "w_up": (m_w_up, v_w_up),
            "w_down": (m_w_down, v_w_down)}

    groups = [("a1", [("w_in_a", None)]),
              ("a2", [("w_memkv", None), ("w_out", None), ("conv_w", None)]),
              ("b0", [("w_up", 0), ("w_down", 0)]), ("a3", [("w_in_b", None), ("w_kv", None)]),
              ("b1", [("w_up", 1), ("w_down", 1)])]
    sources = dict(shards, conv_w=conv_w)
    srcs, span = [], {}
    for gname, members in groups:
        span[gname] = (len(srcs), len(members))
        for n, layer in members:
            a = sources[n] if layer is None else sources[n][layer]
            srcs.append(a if n == "conv_w" else a.astype(BF16))
    g_ssem, g_rsem, g_thru, g_lands, token = _scatter_start(srcs, _whole, name="gather_start")

    def gathered(gname, after):
        lo, cnt = span[gname]
        thru, lands = _scatter_wait(g_ssem, g_rsem, g_thru[lo:lo + cnt], g_lands[lo:lo + cnt], _whole, after,
                                    name=f"gather_wait_{gname}", first=lo)
        return [lax.dynamic_update_index_in_dim(land, s, my_idx, 0) for land, s in zip(lands, thru)]

    full = {}
    (g_wa,) = gathered("a1", token)
    full["w_in_a"] = _from_slots(g_wa, "rows2")

    wa = full["w_in_a"]
    n_qkv = 3 * MAIN_W
    wa = jnp.concatenate([wa[:, :n_qkv], wa[:, n_qkv + N_MAIN_HEADS:], wa[:, n_qkv:n_qkv + N_MAIN_HEADS],
                          jnp.zeros((D, LANES - N_MAIN_HEADS), BF16)], axis=1)
    n_main = n_qkv + MEM_W
    full["w_up"], full["w_down"] = {}, {}
    b_f =_pad_rows(b_f_a.reshape(1, N_MAIN_HEADS), LANES, 1)

    x2d = x.reshape(T, D)
    mem2d = mem.reshape(B * NM, D)
    tgt2d = loss_target.reshape(T, D)
    PM, PX = N_MAIN_HEADS // 2, N_MEM_HEADS // 2

    def stats_to_heads(c2d):
        c = c2d.reshape(B, S, LANES)[:, :, :N_MAIN_HEADS].transpose(0, 2, 1)
        return c[:, :, None, :]

    def mem_kv(layer):
        return _mm_fwd(mem2d, full["w_memkv"][layer], name=f"memkv{layer}", tm=B * NM, tn=2 * MEM_W,
                       out_dtype=BF16, g=ln_mem_g[layer], save_h=True)

    def conv_ffn_fwd(xin, layer):
        uc, ub, a, h = _ffn_up_gate(xin, ln_ffn_g[layer], full["w_up"][layer], conv_w_full[layer],
                                    conv_b[layer].reshape(1, 2 * F), name=f"ffn_up{layer}", S=S)
        xo = _mm_fwd(a, full["w_down"][layer], name=f"ffn_down{layer}", tm=min(1024, T), tn=1024, out_dtype=F32, res=xin)
        return xo, (uc, ub, h, a)

    proj_a, h_mix0 = _mm_fwd(x2d, wa, name="in_proj_a", tm=min(1024, T), tn=2560, out_dtype=BF16, g=ln_mix_g[0],
                             ncols=n_main, save_h=True)
    f_logit = _mm_fwd(x2d, wa, name="in_proj_f", tm=min(1024, T), tn=LANES, out_dtype=F32, g=ln_mix_g[0],
                      col0=n_main // LANES, ncols=LANES)
    c2d = _forget_cumsum(f_logit, b_f, B=B, S=S, name="forget_cumsum")
    cr = stats_to_heads(c2d)
    o_main0, lse0 = _fox_fwd_g(proj_a, proj_a, proj_a, cr, name="fox_fwd", B=B, S=S, P=PM, q_cb=0, k_cb=PM, v_cb=2 * PM,
                               G=FWD_HEAD_GROUP)
    g_wmem, g_wout, g_cw = gathered("a2", lse0)
    full["w_memkv"] = _from_slots(g_wmem, "rows3")
    full["w_out"] = _from_slots(g_wout, "rows3")
    conv_w_full = _from_slots(g_cw, "cols3")
    memkv0, h_mem0 = mem_kv(0)
    o_mem0, lse_m0 = _mem_fwd(proj_a, memkv0, name="mem_fwd0", B=B, S=S, NM=NM, q_cb=3 * PM)
    o_cat0 = jnp.concatenate([o_main0, o_mem0], axis=1)
    x1 = _mm_fwd(o_cat0, full["w_out"][0], name="out_proj0", tm=min(1024, T), tn=1024, out_dtype=F32, res=x2d)
    g_up, g_dn = gathered("b0", x1)
    full["w_up"][0], full["w_down"][0] = _from_slots(g_up, "cols2"), _from_slots(g_dn, "rows2")
    x2, ffn_saved0 = conv_ffn_fwd(x1, 0)
    g_wb, g_wkv = gathered("a3", x2)
    wb, wkv = _from_slots(g_wb, "rows2"), _from_slots(g_wkv, "cols2")
    kv, h_kv =_mm_fwd(x2, wkv, name="kv_proj", tm=min(1024, T), tn=1536, out_dtype=BF16, g=ln_kv_g, save_h=True)
    proj_b, h_mix1 = _mm_fwd(x2, wb, name="in_proj_b", tm=min(1024, T), tn=1024, out_dtype=BF16, g=ln_mix_g[1],
                             save_h=True)
    o_main1, rt1 = _sb_fwd_g(proj_b, kv, kv, name="sb_fwd", B=B, S=S, P=PM, q_cb=0, k_cb=0, v_cb=PM,
                             G=FWD_HEAD_GROUP)
    memkv1, h_mem1 = mem_kv(1)
    o_mem1, lse_m1 = _mem_fwd(proj_b, memkv1, name="mem_fwd1", B=B, S=S, NM=NM, q_cb=PM)
    o_cat1 = jnp.concatenate([o_main1, o_mem1], axis=1)
    x3 = _mm_fwd(o_cat1, full["w_out"][1], name="out_proj1", tm=min(1024, T), tn=1024, out_dtype=F32, res=x2)
    g_up, g_dn = gathered("b1", x3)
    full["w_up"][1], full["w_down"][1] = _from_slots(g_up, "cols2"), _from_slots(g_dn, "rows2")
    x4, ffn_saved1 = conv_ffn_fwd(x3, 1)
    dx4, dg_final, loss_part = _loss_head(x4, final_g, tgt2d, name="loss_head")

    grads = {}
    small = {}
    reduce_groups = []

    def start_reduce(gname, keys, kinds):
        slots = [_to_slots(grads[k], kind) for k, kind in zip(keys, kinds)]
        ssem, rsem, thru, lands, tok = _scatter_start(slots, _slot, name=f"reduce_start_{gname}")
        reduce_groups.append((gname, keys, ssem, rsem, thru, lands))
        return tok[0, 0]

    def conv_ffn_bwd(dxo, xin, saved, layer):
        uc, ub, h, a = saved
        w_dn = full["w_down"][layer]
        da = _mm_nt(dxo, w_dn, name=f"d_act{layer}", tm=min(1024, T), tn=F // 2, out_dtype=F32)
        grads[("w_down", layer)] = _wgrad(a, dxo, f"g_w_down{layer}")
        cwl = conv_w_full[layer]
        du_g, du_v, p_g, p_v = _conv_gate_bwd(da, uc, ub, cwl, name=f"conv_bwd{layer}", B=B, S=S)
        small[("conv_w", layer)] = jnp.concatenate([p_g[0:3], p_v[0:3]], axis=1)
        small[("conv_b", layer)] = jnp.concatenate([p_g[3], p_v[3]], axis=0)
        grads[("w_up", layer)] = jnp.concatenate(
            [_wgrad(h, du_g, f"g_w_up_gate{layer}"), _wgrad(h, du_v, f"g_w_up_val{layer}")], axis=1)
        tok = start_reduce(f"ffn{layer}", [("w_down", layer), ("w_up", layer)], ["rows2", "cols2"])
        dxi, dg = _mm_nt_rmsbwd([(du_g, 0), (du_v, 1)], full["w_up"][layer], xin, ln_ffn_g[layer] + tok,
                                name=f"d_ffn_in{layer}", dres=dxo)
        small[("ln_ffn_g", layer)] = dg[0]
        return dxi

    def mem_bwd(proj, q_cb, memkv, h_mem, do_cat, o_mem, lse_m, layer):
        dqm, dmk, dmv = _mem_bwd(proj, memkv, do_cat, o_mem, lse_m, name=f"mem_bwd{layer}", B=B, S=S, NM=NM,
                                 q_cb=q_cb, do_cb=PM)
        grads[("w_memkv", layer)] = jnp.concatenate(
            [_wgrad(h_mem, dmk, f"g_w_memk{layer}"), _wgrad(h_mem, dmv, f"g_w_memv{layer}")], axis=1)
        _, dg = _mm_nt_rmsbwd([(dmk, 0), (dmv, 1)], full["w_memkv"][layer], mem2d, ln_mem_g[layer],
                              name=f"d_mem_in{layer}", want_dx=False)
        small[("ln_mem_g", layer)] = dg[0]
        return dqm

    dx3 = conv_ffn_bwd(dx4, x3, ffn_saved1, 1)
    do_cat1 = _mm_nt(dx3, full["w_out"][1], name="d_o_cat1", tm=min(1024, T), tn=1024, out_dtype=BF16)
    grads[("w_out", 1)] = _wgrad(o_cat1, dx3, "g_w_out1")
    dq1, dk1, dv1 = _sb_bwd_g(proj_b, kv, kv, do_cat1, rt1, name="sb_bwd", B=B, S=S, P=PM, q_cb=0, k_cb=0, v_cb=PM,
                            do_cb=0)
    dqm1 = mem_bwd(proj_b, PM, memkv1, h_mem1, do_cat1, o_mem1, lse_m1, 1)
    grads["w_in_b"] = jnp.concatenate([_wgrad(h_mix1, dq1, "g_w_in_b_q"), _wgrad(h_mix1, dqm1, "g_w_in_b_m")], axis=1)
    grads["w_kv"] = jnp.concatenate([_wgrad(h_kv, dk1, "g_w_kv_k"), _wgrad(h_kv, dv1, "g_w_kv_v")], axis=1)
    tok = start_reduce("mix1", [("w_out", 1), "w_in_b", "w_kv", ("w_memkv", 1)], ["rows2", "rows2", "cols2", "rows2"])
    dx2, dg = _mm_nt_rmsbwd([(dq1, 0), (dqm1, MAIN_W // MEM_W)], wb, x2, ln_mix_g[1] + tok, name="d_mix_in1", dres=dx3)
    small[("ln_mix_g", 1)] = dg[0]
    dx2, dg = _mm_nt_rmsbwd([(dk1, 0), (dv1, 1)], wkv, x2, ln_kv_g, name="d_kv_in", dres=dx2)
    small["ln_kv_g"] = dg[0]
    dx1 = conv_ffn_bwd(dx2, x1, ffn_saved0, 0)
    do_cat0 = _mm_nt(dx1, full["w_out"][0], name="d_o_cat0", tm=min(1024, T), tn=1024, out_dtype=BF16)
    grads[("w_out", 0)] = _wgrad(o_cat0, dx1, "g_w_out0")
    dq0, dk0, dv0, dcs = _fox_bwd_g(proj_a, proj_a, proj_a, do_cat0, lse0, cr, name="fox_bwd", B=B, S=S, P=PM, q_cb=0,
                                  k_cb=PM, v_cb=2 * PM, do_cb=0)
    dqm0 = mem_bwd(proj_a, 3 * PM, memkv0, h_mem0, do_cat0, o_mem0, lse_m0, 0)
    dc2d = _pad_rows(dcs[:, :, 0, :].transpose(0, 2, 1).reshape(T, N_MAIN_HEADS), LANES, 1)
    df, db_f = _forget_cumsum_bwd(dc2d, f_logit, b_f, B=B, S=S, name="forget_cumsum_bwd")
    a_parts = [(dq0, 0), (dk0, 1), (dv0, 2), (dqm0, n_qkv // MEM_W), (df, n_main // LANES)]
    g_wa = jnp.concatenate([_wgrad(h_mix0, p, f"g_w_in_a{k}") for k, (p, _) in enumerate(a_parts)], axis=1)
    grads["w_in_a"] = jnp.concatenate([g_wa[:, :n_qkv], g_wa[:, n_main:n_main + N_MAIN_HEADS], g_wa[:, n_qkv:n_main]],
                                      axis=1)
    tok = start_reduce("mix0", [("w_out", 0), ("w_memkv", 0), "w_in_a"], ["rows2", "rows2", "rows2"])
    dx0, dg = _mm_nt_rmsbwd(a_parts, wa, x2d, ln_mix_g[0] + tok, name="d_mix_in0", dres=dx1)
    small[("ln_mix_g", 0)] = dg[0]
    grad_x = dx0.reshape(B, S, D)

    def both_small(name):
        return jnp.stack([small[(name, 0)], small[(name, 1)]])

    small_list = [("ln_mix_g", both_small("ln_mix_g")), ("b_f_a", db_f[:, :N_MAIN_HEADS]), ("ln_kv_g", small["ln_kv_g"]),
                  ("ln_mem_g", both_small("ln_mem_g")), ("ln_ffn_g", both_small("ln_ffn_g")),
                  ("conv_w", both_small("conv_w")), ("conv_b", both_small("conv_b")), ("final_g", dg_final[0]),
                  ("loss", loss_part[0, :1])]
    sm_rows = []
    for _, a in small_list:
        flat = a.reshape(-1)
        sm_rows.append(_pad_rows(flat, _round_up(flat.size, 8 * LANES), 0).reshape(-1, LANES))
    spack = jnp.concatenate(sm_rows, axis=0)
    s_ssem, s_rsem, s_thru, s_lands, s_tok = _scatter_start([spack], _whole, name="small_start")

    pieces = {}
    for gname, keys, ssem, rsem, thru, lands in reduce_groups:
        thru, lands = _scatter_wait(ssem, rsem, thru, lands, _slot, s_tok, name=f"reduce_wait_{gname}")
        for key, mine, land in zip(keys, thru, lands):
            own = lax.dynamic_index_in_dim(mine, my_idx, 0, keepdims=False)
            land = lax.dynamic_update_index_in_dim(land, own, my_idx, 0)
            tag = key if isinstance(key, str) else f"{key[0]}{key[1]}"
            pieces[key] = _sum_slots(land, name=f"sum_{tag}")

    red = {}
    for n in ("w_in_a", "w_in_b", "w_kv"):
        red[n] = pieces[n].reshape(shards[n].shape)
    for n in ("w_memkv", "w_out", "w_up", "w_down"):
        red[n] = jnp.stack([pieces[(n, 0)], pieces[(n, 1)]])

    weights = {"ln_mix_g": ln_mix_g, "w_in_a": w_in_a, "b_f_a": b_f_a, "w_in_b": w_in_b, "ln_kv_g": ln_kv_g,
               "w_kv": w_kv, "ln_mem_g": ln_mem_g, "w_memkv": w_memkv, "w_out": w_out, "ln_ffn_g": ln_ffn_g,
               "w_up": w_up, "conv_w": conv_w, "conv_b": conv_b, "w_down": w_down, "final_g": final_g}
    m_in = {"ln_mix_g": m_ln_mix_g, "w_in_a": m_w_in_a, "b_f_a": m_b_f_a, "w_in_b": m_w_in_b, "ln_kv_g": m_ln_kv_g,
            "w_kv": m_w_kv, "ln_mem_g": m_ln_mem_g, "w_memkv": m_w_memkv, "w_out": m_w_out, "ln_ffn_g": m_ln_ffn_g,
            "w_up": m_w_up, "conv_w": m_conv_w, "conv_b": m_conv_b, "w_down": m_w_down, "final_g": m_final_g}
    v_in = {"ln_mix_g": v_ln_mix_g, "w_in_a": v_w_in_a, "b_f_a": v_b_f_a, "w_in_b": v_w_in_b, "ln_kv_g": v_ln_kv_g,
            "w_kv": v_w_kv, "ln_mem_g": v_ln_mem_g, "w_memkv": v_w_memkv, "w_out": v_w_out, "ln_ffn_g": v_ln_ffn_g,
            "w_up": v_w_up, "conv_w": v_conv_w, "conv_b": v_conv_b, "w_down": v_w_down, "final_g": v_final_g}
    order = list(weights)
    big_names = [n for n, _ in BIG]
    g_out, d_out, nm_out, nv_out = {}, {}, {}, {}

    def update(n):
        w = weights[n]
        cols = w.shape[-1]
        g = red[n].reshape(w.shape)
        d, nm, nv = _adamw(w.reshape(-1, cols), g.reshape(-1, cols), m_in[n].reshape(-1, cols),
                           v_in[n].reshape(-1, cols), name=f"adamw_{n}")
        g_out[n], d_out[n], nm_out[n], nv_out[n] = g, d.reshape(w.shape), nm.reshape(w.shape), nv.reshape(w.shape)

    for n in big_names:
        update(n)
    all_updated = jnp.stack([d_out[n].reshape(-1)[0] for n in big_names])
    s_thru, s_lands = _scatter_wait(s_ssem, s_rsem, s_thru, s_lands, _whole, all_updated, name="small_wait")
    ssum = _sum_slots(lax.dynamic_update_index_in_dim(s_lands[0], s_thru[0], my_idx, 0), name="sum_small")
    off = 0
    for (n, a), rows in zip(small_list, sm_rows):
        red[n] = ssum[off:off + rows.shape[0]].reshape(-1)[:a.size].reshape(a.shape)
        off += rows.shape[0]
    loss = red["loss"][0]
    shard_cols = conv_w.shape[2]
    red["conv_w"] = lax.dynamic_slice_in_dim(red["conv_w"], my_idx * shard_cols, shard_cols, axis=2)
    red["b_f_a"] = red["b_f_a"].reshape(b_f_a.shape)
    update("conv_w")
    small_names = [n for n in order if n not in g_out]

    def pack_small(src):
        rows = []
        for n in small_names:
            flat = src[n].reshape(-1)
            rows.append(_pad_rows(flat, _round_up(flat.size, 8 * LANES), 0).reshape(-1, LANES))
        return jnp.concatenate(rows, axis=0), [r.shape[0] for r in rows]

    red_small = {n: red[n].reshape(weights[n].shape) for n in small_names}
    wp, counts = pack_small(weights)
    gp, _ = pack_small(red_small)
    mp, _ = pack_small(m_in)
    vp, _ = pack_small(v_in)
    dp, nmp, nvp = _adamw(wp, gp, mp, vp, name="adamw_small")
    off = 0
    for n, cnt in zip(small_names, counts):
        shp = weights[n].shape
        size = weights[n].size
        g_out[n] = red_small[n]
        d_out[n] = dp[off:off + cnt].reshape(-1)[:size].reshape(shp)
        nm_out[n] = nmp[off:off + cnt].reshape(-1)[:size].reshape(shp)
        nv_out[n] = nvp[off:off + cnt].reshape(-1)[:size].reshape(shp)
        off += cnt

    return (loss, grad_x, *[g_out[n] for n in order], *[d_out[n] for n in order],
            *[nm_out[n] for n in order], *[nv_out[n] for n in order])
```

```python
import functools

import jax
import jax.numpy as jnp
from jax import lax
from jax.experimental import pallas as pl
from jax.experimental.pallas import tpu as pltpu

F32 = jnp.float32
BF16 = jnp.bfloat16
LANES = 128
HEAD_DIM = 64
N_MAIN_HEADS = 12
N_MEM_HEADS = 4
MAIN_W = N_MAIN_HEADS * HEAD_DIM
MEM_W = N_MEM_HEADS * HEAD_DIM
SCALE = HEAD_DIM ** -0.5
EPS = 1e-6
NEG = -1e30
N_DEV = 8
ATT_TILE = 256
MEM_Q_TILE = 2048
VMEM_BIG = 56 * 2 ** 20
MESH = pl.DeviceIdType.MESH

ADAM_LR = 0.001
ADAM_B1 = 0.9
ADAM_B2 = 0.999
ADAM_EPS = 1e-08
ADAM_WD = 0.01
ADAM_STEP = 10

NT = (((1,), (1,)), ((), ()))
TN = (((0,), (0,)), ((), ()))


def _pc(body, *, name, out_shape, grid=None, in_specs=None, out_specs=None, scratch_shapes=(),
        semantics=None, vmem=None):
    kw = {}
    if grid is not None:
        kw["grid"] = grid
    params = pltpu.CompilerParams(dimension_semantics=semantics, vmem_limit_bytes=vmem)
    return pl.pallas_call(body, name=name, out_shape=out_shape, in_specs=in_specs, out_specs=out_specs,
                          scratch_shapes=list(scratch_shapes), compiler_params=params, **kw)


def _sds(shape, dtype):
    return jax.ShapeDtypeStruct(shape, dtype)


def _mm_fwd(a, w, *, name, tm, tn, out_dtype, g=None, res=None, col0=0, ncols=None, save_h=False):
    m_rows, k = a.shape
    n = w.shape[1] if ncols is None else ncols
    grid = (m_rows // tm, n // tn)
    norm = g is not None

    def body(*refs):
        refs = list(refs)
        a_ref = refs.pop(0)
        g_ref = refs.pop(0) if norm else None
        w_ref = refs.pop(0)
        res_ref = refs.pop(0) if res is not None else None
        o_ref = refs.pop(0)
        hout_ref = refs.pop(0) if save_h else None
        h_ref = refs.pop(0) if norm else None
        if norm:
            @pl.when(pl.program_id(1) == 0)
            def _():
                xv = a_ref[...]
                r = lax.rsqrt(jnp.mean(xv * xv, axis=-1, keepdims=True) + EPS)
                h = ((xv * r) * g_ref[...]).astype(BF16)
                h_ref[...] = h
                if save_h:
                    hout_ref[...] = h
            lhs = h_ref[...]
        else:
            lhs = a_ref[...].astype(BF16)
        acc = jnp.dot(lhs, w_ref[...], preferred_element_type=F32)
        if res is not None:
            acc = acc + res_ref[...]
        o_ref[...] = acc.astype(out_dtype)

    in_specs = [pl.BlockSpec((tm, k), lambda i, j: (i, 0))]
    args = [a]
    if norm:
        in_specs.append(pl.BlockSpec((1, k), lambda i, j: (0, 0)))
        args.append(g.reshape(1, k))
    in_specs.append(pl.BlockSpec((k, tn), lambda i, j: (0, j + col0)))
    args.append(w)
    if res is not None:
        in_specs.append(pl.BlockSpec((tm, tn), lambda i, j: (i, j)))
        args.append(res)
    out_shape = [_sds((m_rows, n), out_dtype)]
    out_specs = [pl.BlockSpec((tm, tn), lambda i, j: (i, j))]
    if save_h:
        out_shape.append(_sds((m_rows, k), BF16))
        out_specs.append(pl.BlockSpec((tm, k), lambda i, j: (i, 0)))
    scratch = [pltpu.VMEM((tm, k), BF16)] if norm else []
    outs = _pc(body, name=name, out_shape=out_shape, grid=grid, in_specs=in_specs, out_specs=out_specs,
               scratch_shapes=scratch, semantics=("arbitrary", "arbitrary"), vmem=VMEM_BIG)(*args)
    return outs if save_h else outs[0]


def _mm_nt(a, w, *, name, tm, tn, out_dtype):
    m_rows, k = a.shape
    n = w.shape[0]

    def body(a_ref, w_ref, o_ref):
        acc = lax.dot_general(a_ref[...].astype(BF16), w_ref[...], NT, preferred_element_type=F32)
        o_ref[...] = acc.astype(out_dtype)

    return _pc(body, name=name, out_shape=_sds((m_rows, n), out_dtype), grid=(m_rows // tm, n // tn),
               in_specs=[pl.BlockSpec((tm, k), lambda i, j: (i, 0)), pl.BlockSpec((tn, k), lambda i, j: (j, 0))],
               out_specs=pl.BlockSpec((tm, tn), lambda i, j: (i, j)),
               semantics=("arbitrary", "arbitrary"), vmem=VMEM_BIG)(a, w)


def _mm_tn(a, b, *, name, ta, tn, tt):
    t_rows, ka = a.shape
    n = b.shape[1]
    nt = t_rows // tt

    def body(a_ref, b_ref, o_ref, acc_ref):
        t = pl.program_id(2)

        @pl.when(t == 0)
        def _():
            acc_ref[...] = jnp.zeros_like(acc_ref)

        acc_ref[...] += lax.dot_general(a_ref[...].astype(BF16), b_ref[...].astype(BF16), TN,
                                        preferred_element_type=F32)

        @pl.when(t == nt - 1)
        def _():
            o_ref[...] = acc_ref[...].astype(BF16)

    return _pc(body, name=name, out_shape=_sds((ka, n), BF16), grid=(ka // ta, n // tn, nt),
               in_specs=[pl.BlockSpec((tt, ta), lambda i, j, t: (t, i)),
                         pl.BlockSpec((tt, tn), lambda i, j, t: (t, j))],
               out_specs=pl.BlockSpec((ta, tn), lambda i, j, t: (i, j)),
               scratch_shapes=[pltpu.VMEM((ta, tn), F32)],
               semantics=("arbitrary", "arbitrary", "arbitrary"), vmem=VMEM_BIG)(a, b)


def _wgrad(a, b, name):
    t_rows, ka = a.shape
    n = b.shape[1]
    ta = ka if ka <= 1024 else ka // 2
    tn = n
    while ta * tn * 4 > 6 * 2 ** 20 and tn % 256 == 0:
        tn //= 2
    tt = min(2048, t_rows)
    return _mm_tn(a, b, name=name, ta=ta, tn=tn, tt=tt)


def _mm_nt_rmsbwd(parts, w, x, g, *, name, dres=None, want_dx=True):
    m_rows, d = x.shape
    tm = min(512, m_rows)
    n_parts = len(parts)

    def body(*refs):
        refs = list(refs)
        dy_refs = [refs.pop(0) for _ in range(n_parts)]
        w_refs = [refs.pop(0) for _ in range(n_parts)]
        x_ref = refs.pop(0)
        g_ref = refs.pop(0)
        dres_ref = refs.pop(0) if dres is not None else None
        dx_ref = refs.pop(0) if want_dx else None
        dg_ref = refs.pop(0)

        @pl.when(pl.program_id(0) == 0)
        def _():
            dg_ref[...] = jnp.zeros_like(dg_ref)

        dh = None
        for dy_ref, w_ref in zip(dy_refs, w_refs):
            t = lax.dot_general(dy_ref[...].astype(BF16), w_ref[...], NT, preferred_element_type=F32)
            dh = t if dh is None else dh + t
        xv = x_ref[...]
        r = lax.rsqrt(jnp.mean(xv * xv, axis=-1, keepdims=True) + EPS)
        xh = xv * r
        dg_ref[...] += jnp.sum(dh * xh, axis=0, keepdims=True)
        if want_dx:
            dhg = dh * g_ref[...]
            dx = r * (dhg - xh * jnp.mean(dhg * xh, axis=-1, keepdims=True))
            if dres is not None:
                dx = dx + dres_ref[...]
            dx_ref[...] = dx

    in_specs, args = [], []
    for dy, _ in parts:
        in_specs.append(pl.BlockSpec((tm, dy.shape[1]), lambda i: (i, 0)))
        args.append(dy)
    for dy, cb in parts:
        in_specs.append(pl.BlockSpec((d, dy.shape[1]), functools.partial(lambda i, cb: (0, cb), cb=cb)))
        args.append(w)
    in_specs += [pl.BlockSpec((tm, d), lambda i: (i, 0)), pl.BlockSpec((1, d), lambda i: (0, 0))]
    args += [x, g.reshape(1, d)]
    if dres is not None:
        in_specs.append(pl.BlockSpec((tm, d), lambda i: (i, 0)))
        args.append(dres)
    out_shape, out_specs = [], []
    if want_dx:
        out_shape.append(_sds((m_rows, d), F32))
        out_specs.append(pl.BlockSpec((tm, d), lambda i: (i, 0)))
    out_shape.append(_sds((1, d), F32))
    out_specs.append(pl.BlockSpec((1, d), lambda i: (0, 0)))
    outs = _pc(body, name=name, out_shape=out_shape, grid=(m_rows // tm,), in_specs=in_specs,
               out_specs=out_specs, semantics=("arbitrary",), vmem=VMEM_BIG)(*args)
    return (outs[0], outs[1]) if want_dx else (None, outs[0])


def _loss_head(x, g, tgt, *, name):
    m_rows, d = x.shape
    tm = min(512, m_rows)

    def body(x_ref, g_ref, t_ref, dx_ref, dg_ref, loss_ref):
        @pl.when(pl.program_id(0) == 0)
        def _():
            dg_ref[...] = jnp.zeros_like(dg_ref)
            loss_ref[...] = jnp.zeros_like(loss_ref)

        xv = x_ref[...]
        r = lax.rsqrt(jnp.mean(xv * xv, axis=-1, keepdims=True) + EPS)
        xh = xv * r
        gv = g_ref[...]
        err = xh * gv - t_ref[...]
        per_tok = jnp.mean(err * err, axis=-1, keepdims=True)
        loss_ref[...] += 0.5 * jnp.sum(per_tok, axis=0, keepdims=True)
        dout = err * (1.0 / d)
        dg_ref[...] += jnp.sum(dout * xh, axis=0, keepdims=True)
        dhg = dout * gv
        dx_ref[...] = r * (dhg - xh * jnp.mean(dhg * xh, axis=-1, keepdims=True))

    row = pl.BlockSpec((tm, d), lambda i: (i, 0))
    return _pc(body, name=name, out_shape=[_sds((m_rows, d), F32), _sds((1, d), F32), _sds((1, LANES), F32)],
               grid=(m_rows // tm,), in_specs=[row, pl.BlockSpec((1, d), lambda i: (0, 0)), row],
               out_specs=[row, pl.BlockSpec((1, d), lambda i: (0, 0)), pl.BlockSpec((1, LANES), lambda i: (0, 0))],
               semantics=("arbitrary",))(x, g.reshape(1, d), tgt)


def _split3(v):
    hi = v.astype(BF16)
    r1 = v - hi.astype(F32)
    mid = r1.astype(BF16)
    lo = (r1 - mid.astype(F32)).astype(BF16)
    return hi, mid, lo


def _split2(v):
    hi = v.astype(BF16)
    lo = (v - hi.astype(F32)).astype(BF16)
    return hi, lo


def _tri_dot3(tri, v):
    hi, mid, lo = _split3(v)
    return (jnp.dot(tri, hi, preferred_element_type=F32) + jnp.dot(tri, mid, preferred_element_type=F32)
            + jnp.dot(tri, lo, preferred_element_type=F32))


def _log_sigmoid(v):
    return jnp.minimum(v, 0.0) - jnp.log(1.0 + jnp.exp(-jnp.abs(v)))


def _forget_cumsum(f_logit, b_f, *, B, S, name):
    ch = min(256, S)
    nch = S // ch

    def body(f_ref, b_ref, c_ref):
        r_i = lax.broadcasted_iota(jnp.int32, (ch, ch), 0)
        c_i = lax.broadcasted_iota(jnp.int32, (ch, ch), 1)
        tri = (c_i <= r_i).astype(BF16)
        bv = b_ref[...]

        def step(k, carry):
            rows = pl.ds(pl.multiple_of(k * ch, ch), ch)
            lf = _log_sigmoid(f_ref[rows, :] + bv)
            c_ref[rows, :] = _tri_dot3(tri, lf) + carry
            return carry + jnp.sum(lf, axis=0, keepdims=True)

        lax.fori_loop(0, nch, step, jnp.zeros((1, LANES), F32))

    blk = pl.BlockSpec((S, LANES), lambda b: (b, 0))
    return _pc(body, name=name, out_shape=_sds((B * S, LANES), F32), grid=(B,),
               in_specs=[blk, pl.BlockSpec((1, LANES), lambda b: (0, 0))], out_specs=blk,
               semantics=("arbitrary",))(f_logit, b_f)


def _forget_cumsum_bwd(dc, f_logit, b_f, *, B, S, name):
    ch = min(256, S)
    nch = S // ch

    def body(dc_ref, f_ref, b_ref, df_ref, db_ref):
        @pl.when(pl.program_id(0) == 0)
        def _():
            db_ref[...] = jnp.zeros_like(db_ref)

        r_i = lax.broadcasted_iota(jnp.int32, (ch, ch), 0)
        c_i = lax.broadcasted_iota(jnp.int32, (ch, ch), 1)
        tri = (c_i >= r_i).astype(BF16)
        bv = b_ref[...]

        def step(kk, carry):
            tail, dbs = carry
            k = nch - 1 - kk
            rows = pl.ds(pl.multiple_of(k * ch, ch), ch)
            dcv = dc_ref[rows, :]
            dlf = _tri_dot3(tri, dcv) + tail
            z = f_ref[rows, :] + bv
            df = dlf * (1.0 / (1.0 + jnp.exp(z)))
            df_ref[rows, :] = df.astype(BF16)
            return tail + jnp.sum(dcv, axis=0, keepdims=True), dbs + jnp.sum(df, axis=0, keepdims=True)

        zero = jnp.zeros((1, LANES), F32)
        _, dbs = lax.fori_loop(0, nch, step, (zero, zero))
        db_ref[...] += dbs

    blk = pl.BlockSpec((S, LANES), lambda b: (b, 0))
    one = pl.BlockSpec((1, LANES), lambda b: (0, 0))
    return _pc(body, name=name, out_shape=[_sds((B * S, LANES), BF16), _sds((1, LANES), F32)], grid=(B,),
               in_specs=[blk, blk, one], out_specs=[blk, one], semantics=("arbitrary",))(dc, f_logit, b_f)


def _head_mask(lane, hh):
    return (lane < HEAD_DIM) if hh == 0 else (lane >= HEAD_DIM)


HEAD_GROUP = 3
FWD_HEAD_GROUP = 6


def _g_col_spec(rows, nblk_rows, cb, G):
    return pl.BlockSpec((rows, G * LANES), lambda b, p, i: (b * nblk_rows + i, cb // G + p))


def _g_kv_spec(rows, cb, G):
    return pl.BlockSpec((rows, G * LANES), lambda b, p, i: (b, cb // G + p))


def _g_stat_col_spec(tq, G):
    return pl.BlockSpec((1, 2 * G, tq, 1), lambda b, p, i: (b, p, i, 0))


def _g_stat_row_spec(S, G):
    return pl.BlockSpec((1, 2 * G, 1, S), lambda b, p, i: (b, p, 0, 0))


def _lanes(g):
    return slice(g * LANES, (g + 1) * LANES)


def _streams(x_ref, G, scale=None):
    rows = x_ref.shape[0]
    lane = lax.broadcasted_iota(jnp.int32, (rows, LANES), 1)
    out = []
    for g in range(G):
        x = x_ref[:, _lanes(g)]
        if scale is not None:
            x = x * jnp.asarray(scale, x.dtype)
        for hh in range(2):
            out.append(jnp.where(_head_mask(lane, hh), x, jnp.zeros_like(x)))
    return lane, out


def _wide(stat, width):
    return jnp.tile(stat, (1, width // LANES))


def _fold_lanes(v):
    out = v[:, :LANES]
    for j in range(1, v.shape[1] // LANES):
        out = out + v[:, j * LANES:(j + 1) * LANES]
    return out


def _kv_blocks(ref, ks, tk, G):
    return [ref[pl.ds(ks, tk), _lanes(g)] for g in range(G)]


def _sweep(i, block):
    def step(kb, c):
        block(kb, False)
        return c
    lax.fori_loop(0, i, step, 0)
    block(i, True)


def _fox_fwd_g(qa, ka, va, cr, *, name, B, S, P, q_cb, k_cb, v_cb, G=HEAD_GROUP):
    tq = tk = min(ATT_TILE, S)
    nq = S // tq
    NS = 2 * G

    def body(q_ref, k_ref, v_ref, cr_ref, o_ref, lse_ref, acc_ref, m_ref, l_ref):
        i = pl.program_id(2)
        lane, qh = _streams(q_ref, G, SCALE)
        on_or_below = (lax.broadcasted_iota(jnp.int32, (tq, tk), 1) <= lax.broadcasted_iota(jnp.int32, (tq, tk), 0))
        m_ref[...] = jnp.full(m_ref.shape, NEG, F32)
        l_ref[...] = jnp.zeros(l_ref.shape, F32)
        acc_ref[...] = jnp.zeros(acc_ref.shape, F32)

        def block(kb, diag):
            ks = pl.multiple_of(kb * tk, tk)
            kblk = _kv_blocks(k_ref, ks, tk, G)
            vblk = _kv_blocks(v_ref, ks, tk, G)
            ss = [lax.dot_general(qh[st], kblk[st // 2], NT, preferred_element_type=F32) for st in range(NS)]
            ps = []
            for st in range(NS):
                s = ss[st] - cr_ref[0, st, :, pl.ds(ks, tk)]
                if diag:
                    s = jnp.where(on_or_below, s, NEG)
                m = m_ref[st]
                m_new = jnp.maximum(m, jnp.max(s, axis=-1, keepdims=True))
                alpha = jnp.exp(m - m_new)
                p = jnp.exp(s - _wide(m_new, tk))
                m_ref[st] = m_new
                l_ref[st] = alpha * l_ref[st] + _fold_lanes(p)
                ps.append((alpha, p.astype(BF16)))
            pvs = [jnp.dot(ps[st][1], vblk[st // 2], preferred_element_type=F32) for st in range(NS)]
            for st in range(NS):
                acc_ref[st] = ps[st][0] * acc_ref[st] + pvs[st]

        _sweep(i, block)
        ls = [jnp.sum(l_ref[st], axis=-1, keepdims=True) for st in range(NS)]
        for st in range(NS):
            lse_ref[0, st] = jnp.max(m_ref[st], axis=-1, keepdims=True) + jnp.log(ls[st])
        for g in range(G):
            o_ref[:, _lanes(g)] = jnp.where(lane < HEAD_DIM, acc_ref[2 * g] / ls[2 * g],
                                            acc_ref[2 * g + 1] / ls[2 * g + 1]).astype(BF16)

    return _pc(body, name=name, out_shape=[_sds((B * S, P * LANES), BF16), _sds((B, 2 * P, S, 1), F32)],
               grid=(B, P // G, nq),
               in_specs=[_g_col_spec(tq, nq, q_cb, G), _g_kv_spec(S, k_cb, G), _g_kv_spec(S, v_cb, G),
                         _g_stat_row_spec(S, G)],
               out_specs=[_g_col_spec(tq, nq, 0, G), _g_stat_col_spec(tq, G)],
               scratch_shapes=[pltpu.VMEM((NS, tq, LANES), F32)] * 3,
               semantics=("arbitrary", "arbitrary", "arbitrary"), vmem=VMEM_BIG)(qa, ka, va, cr)


def _fox_bwd_g(qa, ka, va, doa, lse, cr, *, name, B, S, P, q_cb, k_cb, v_cb, do_cb, G=HEAD_GROUP):
    tq = tk = min(ATT_TILE, S)
    nq = S // tq
    NS = 2 * G

    def body(q_ref, k_ref, v_ref, do_ref, lse_ref, cr_ref, dq_ref, dk_ref, dv_ref, dcs_ref, dqa_ref, delta_ref, lse_s,
             p_buf, dp_buf):
        i = pl.program_id(2)

        @pl.when(i == 0)
        def _():
            dk_ref[...] = jnp.zeros_like(dk_ref)
            dv_ref[...] = jnp.zeros_like(dv_ref)
            dcs_ref[...] = jnp.zeros_like(dcs_ref)

        lane, qh = _streams(q_ref, G, SCALE)
        _, doh = _streams(do_ref, G)
        on_or_below = (lax.broadcasted_iota(jnp.int32, (tq, tk), 1) <= lax.broadcasted_iota(jnp.int32, (tq, tk), 0))
        delta_ref[...] = jnp.zeros(delta_ref.shape, F32)
        dqa_ref[...] = jnp.zeros(dqa_ref.shape, F32)
        for st in range(NS):
            lse_s[st] = jnp.broadcast_to(lse_ref[0, st], (tq, LANES))

        def probs(kb, diag):
            ks = pl.multiple_of(kb * tk, tk)
            kblk = _kv_blocks(k_ref, ks, tk, G)
            vblk = _kv_blocks(v_ref, ks, tk, G)
            ss = [lax.dot_general(qh[st], kblk[st // 2], NT, preferred_element_type=F32) for st in range(NS)]
            dps = [lax.dot_general(doh[st], vblk[st // 2], NT, preferred_element_type=F32) for st in range(NS)]
            ps = []
            for st in range(NS):
                s = ss[st] - cr_ref[0, st, :, pl.ds(ks, tk)]
                if diag:
                    s = jnp.where(on_or_below, s, NEG)
                ps.append(jnp.exp(s - _wide(lse_s[st], tk)))
            return ks, kblk, ps, dps

        def delta_block(kb, diag):
            _, _, ps, dps = probs(kb, diag)
            for st in range(NS):
                delta_ref[st] += _fold_lanes(ps[st] * dps[st])
                p_buf[st, kb] = ps[st]
                dp_buf[st, kb] = dps[st]

        _sweep(i, delta_block)
        for st in range(NS):
            delta_ref[st] = jnp.broadcast_to(jnp.sum(delta_ref[st], axis=-1, keepdims=True), (tq, LANES))

        def grad_block(kb, diag):
            ks = pl.multiple_of(kb * tk, tk)
            kblk = _kv_blocks(k_ref, ks, tk, G)
            rows = pl.ds(ks, tk)
            dsb, pb = [], []
            for st in range(NS):
                p = p_buf[st, kb]
                ds = p * (dp_buf[st, kb] - _wide(delta_ref[st], tk))
                dcs_ref[0, st, :, rows] -= jnp.sum(ds, axis=0, keepdims=True)
                dsb.append(ds.astype(BF16))
                pb.append(p.astype(BF16))
            dks = [lax.dot_general(dsb[st], qh[st], TN, preferred_element_type=F32) for st in range(NS)]
            dvs = [lax.dot_general(pb[st], doh[st], TN, preferred_element_type=F32) for st in range(NS)]
            dqs = [jnp.dot(dsb[st], kblk[st // 2], preferred_element_type=F32) for st in range(NS)]
            for g in range(G):
                dk_ref[rows, _lanes(g)] += dks[2 * g] + dks[2 * g + 1]
                dv_ref[rows, _lanes(g)] += dvs[2 * g] + dvs[2 * g + 1]
            for st in range(NS):
                dqa_ref[st] += dqs[st]

        _sweep(i, grad_block)
        for g in range(G):
            dq_ref[:, _lanes(g)] = (jnp.where(lane < HEAD_DIM, dqa_ref[2 * g], dqa_ref[2 * g + 1]) * SCALE).astype(BF16)

    return _pc(body, name=name,
               out_shape=[_sds((B * S, P * LANES), BF16), _sds((B * S, P * LANES), F32), _sds((B * S, P * LANES), F32),
                          _sds((B, 2 * P, 1, S), F32)],
               grid=(B, P // G, nq),
               in_specs=[_g_col_spec(tq, nq, q_cb, G), _g_kv_spec(S, k_cb, G), _g_kv_spec(S, v_cb, G),
                         _g_col_spec(tq, nq, do_cb, G), _g_stat_col_spec(tq, G), _g_stat_row_spec(S, G)],
               out_specs=[_g_col_spec(tq, nq, 0, G), _g_kv_spec(S, 0, G), _g_kv_spec(S, 0, G), _g_stat_row_spec(S, G)],
               scratch_shapes=[pltpu.VMEM((NS, tq, LANES), F32)] * 3 + [pltpu.VMEM((NS, nq, tq, tk), F32)] * 2,
               semantics=("arbitrary", "arbitrary", "arbitrary"), vmem=VMEM_BIG)(qa, ka, va, doa, lse, cr)


def _sb_logs_z(z):
    nz = -z
    lm = jnp.minimum(nz, 0.0) - jnp.log(1.0 + jnp.exp(jnp.minimum(z, nz)))
    return lm + z, lm


def _sb_fwd_g(qa, ka, va, *, name, B, S, P, q_cb, k_cb, v_cb, G=HEAD_GROUP):
    tq = tk = min(ATT_TILE, S)
    nq = S // tq
    NS = 2 * G

    def body(q_ref, k_ref, v_ref, o_ref, rt_ref, acc_ref, run_ref):
        i = pl.program_id(2)
        lane, qh = _streams(q_ref, G, SCALE)
        t_r = lax.broadcasted_iota(jnp.int32, (tk, tk), 0)
        t_c = lax.broadcasted_iota(jnp.int32, (tk, tk), 1)
        after = (t_r > t_c).astype(BF16)
        below = t_c < t_r
        acc_ref[...] = jnp.zeros(acc_ref.shape, F32)
        run_ref[...] = jnp.zeros(run_ref.shape, F32)

        def block(kb, diag):
            ks = pl.multiple_of(kb * tk, tk)
            kblk = _kv_blocks(k_ref, ks, tk, G)
            vblk = _kv_blocks(v_ref, ks, tk, G)
            zs = [lax.dot_general(qh[st], kblk[st // 2], NT, preferred_element_type=F32) for st in range(NS)]
            lss, parts = [], []
            for st in range(NS):
                ls, lm = _sb_logs_z(zs[st])
                if diag:
                    lm = jnp.where(below, lm, 0.0)
                lss.append(ls + _wide(run_ref[st], tk))
                run_ref[st] += jnp.sum(lm, axis=-1, keepdims=True)
                parts.append(_split2(lm))
            sufs = [jnp.dot(parts[st][0], after, preferred_element_type=F32)
                    + jnp.dot(parts[st][1], after, preferred_element_type=F32) for st in range(NS)]
            ab = []
            for st in range(NS):
                a = jnp.exp(lss[st] + sufs[st])
                if diag:
                    a = jnp.where(below, a, 0.0)
                ab.append(a.astype(BF16))
            pvs = [jnp.dot(ab[st], vblk[st // 2], preferred_element_type=F32) for st in range(NS)]
            for st in range(NS):
                acc_ref[st] += pvs[st]

        block(i, True)

        def step(jj, c):
            block(i - 1 - jj, False)
            return c

        lax.fori_loop(0, i, step, 0)
        for st in range(NS):
            rt_ref[0, st] = jnp.max(run_ref[st], axis=-1, keepdims=True)
        for g in range(G):
            o_ref[:, _lanes(g)] = jnp.where(lane < HEAD_DIM, acc_ref[2 * g], acc_ref[2 * g + 1]).astype(BF16)

    return _pc(body, name=name, out_shape=[_sds((B * S, P * LANES), BF16), _sds((B, 2 * P, S, 1), F32)],
               grid=(B, P // G, nq),
               in_specs=[_g_col_spec(tq, nq, q_cb, G), _g_kv_spec(S, k_cb, G), _g_kv_spec(S, v_cb, G)],
               out_specs=[_g_col_spec(tq, nq, 0, G), _g_stat_col_spec(tq, G)],
               scratch_shapes=[pltpu.VMEM((NS, tq, LANES), F32)] * 2,
               semantics=("arbitrary", "arbitrary", "arbitrary"), vmem=VMEM_BIG)(qa, ka, va)


def _sb_bwd_g(qa, ka, va, doa, rt, *, name, B, S, P, q_cb, k_cb, v_cb, do_cb, G=HEAD_GROUP):
    tq = tk = min(ATT_TILE, S)
    nq = S // tq
    NS = 2 * G

    def body(q_ref, k_ref, v_ref, do_ref, rt_ref, dq_ref, dk_ref, dv_ref, dqa_ref, pl_ref, pg_ref):
        i = pl.program_id(2)

        @pl.when(i == 0)
        def _():
            dk_ref[...] = jnp.zeros_like(dk_ref)
            dv_ref[...] = jnp.zeros_like(dv_ref)

        lane, qh = _streams(q_ref, G, SCALE)
        _, doh = _streams(do_ref, G)
        t_r = lax.broadcasted_iota(jnp.int32, (tk, tk), 0)
        t_c = lax.broadcasted_iota(jnp.int32, (tk, tk), 1)
        upto = (t_r <= t_c).astype(BF16)
        before = (t_r < t_c).astype(BF16)
        below = t_c < t_r
        dqa_ref[...] = jnp.zeros(dqa_ref.shape, F32)
        pg_ref[...] = jnp.zeros(pg_ref.shape, F32)
        for st in range(NS):
            pl_ref[st] = jnp.broadcast_to(rt_ref[0, st], (tq, LANES))

        def block(kb, diag):
            ks = pl.multiple_of(kb * tk, tk)
            rows = pl.ds(ks, tk)
            kblk = _kv_blocks(k_ref, ks, tk, G)
            vblk = _kv_blocks(v_ref, ks, tk, G)
            zs = [lax.dot_general(qh[st], kblk[st // 2], NT, preferred_element_type=F32) for st in range(NS)]
            das = [lax.dot_general(doh[st], vblk[st // 2], NT, preferred_element_type=F32) for st in range(NS)]
            lss, parts = [], []
            for st in range(NS):
                ls, lm = _sb_logs_z(zs[st])
                if diag:
                    lm = jnp.where(below, lm, 0.0)
                lss.append((ls, ls + _wide(pl_ref[st], tk)))
                pl_ref[st] -= jnp.sum(lm, axis=-1, keepdims=True)
                parts.append(_split2(lm))
            pins = [jnp.dot(parts[st][0], upto, preferred_element_type=F32)
                    + jnp.dot(parts[st][1], upto, preferred_element_type=F32) for st in range(NS)]
            gms, ab, gparts = [], [], []
            for st in range(NS):
                a = jnp.exp(lss[st][1] - pins[st])
                if diag:
                    a = jnp.where(below, a, 0.0)
                gm = a * das[st]
                gms.append(gm)
                ab.append(a.astype(BF16))
                gparts.append(gm.astype(BF16))
            pgs = [jnp.dot(gparts[st], before, preferred_element_type=F32) for st in range(NS)]
            dzb = []
            for st in range(NS):
                gm = gms[st]
                dz = gm - jnp.exp(lss[st][0]) * (gm + (pgs[st] + _wide(pg_ref[st], tk)))
                if diag:
                    dz = jnp.where(below, dz, 0.0)
                pg_ref[st] += jnp.sum(gm, axis=-1, keepdims=True)
                dzb.append(dz.astype(BF16))
            dks = [lax.dot_general(dzb[st], qh[st], TN, preferred_element_type=F32) for st in range(NS)]
            dvs = [lax.dot_general(ab[st], doh[st], TN, preferred_element_type=F32) for st in range(NS)]
            dqs = [jnp.dot(dzb[st], kblk[st // 2], preferred_element_type=F32) for st in range(NS)]
            for g in range(G):
                dk_ref[rows, _lanes(g)] += dks[2 * g] + dks[2 * g + 1]
                dv_ref[rows, _lanes(g)] += dvs[2 * g] + dvs[2 * g + 1]
            for st in range(NS):
                dqa_ref[st] += dqs[st]

        _sweep(i, block)
        for g in range(G):
            dq_ref[:, _lanes(g)] = (jnp.where(lane < HEAD_DIM, dqa_ref[2 * g], dqa_ref[2 * g + 1]) * SCALE).astype(BF16)

    return _pc(body, name=name,
               out_shape=[_sds((B * S, P * LANES), BF16), _sds((B * S, P * LANES), F32), _sds((B * S, P * LANES), F32)],
               grid=(B, P // G, nq),
               in_specs=[_g_col_spec(tq, nq, q_cb, G), _g_kv_spec(S, k_cb, G), _g_kv_spec(S, v_cb, G),
                         _g_col_spec(tq, nq, do_cb, G), _g_stat_col_spec(tq, G)],
               out_specs=[_g_col_spec(tq, nq, 0, G), _g_kv_spec(S, 0, G), _g_kv_spec(S, 0, G)],
               scratch_shapes=[pltpu.VMEM((NS, tq, LANES), F32)] * 3,
               semantics=("arbitrary", "arbitrary", "arbitrary"), vmem=VMEM_BIG)(qa, ka, va, doa, rt)


MEM_GROUP = N_MEM_HEADS // 2


def _mem_fwd(qa, kva, *, name, B, S, NM, q_cb):
    G = MEM_GROUP
    NS = 2 * G
    tq = min(MEM_Q_TILE, S)
    nq = S // tq

    def body(q_ref, k_ref, v_ref, o_ref, lse_ref):
        lane, qh = _streams(q_ref, G, SCALE)
        kblk = [k_ref[:, _lanes(g)] for g in range(G)]
        vblk = [v_ref[:, _lanes(g)] for g in range(G)]
        ss = [lax.dot_general(qh[st], kblk[st // 2], NT, preferred_element_type=F32) for st in range(NS)]
        pb, ls = [], []
        for st in range(NS):
            m = jnp.max(ss[st], axis=-1, keepdims=True)
            p = jnp.exp(ss[st] - m)
            l = jnp.sum(p, axis=-1, keepdims=True)
            lse_ref[0, st] = m + jnp.log(l)
            pb.append(p.astype(BF16))
            ls.append(l)
        pvs = [jnp.dot(pb[st], vblk[st // 2], preferred_element_type=F32) for st in range(NS)]
        for g in range(G):
            o_ref[:, _lanes(g)] = jnp.where(lane < HEAD_DIM, pvs[2 * g] / ls[2 * g],
                                            pvs[2 * g + 1] / ls[2 * g + 1]).astype(BF16)

    return _pc(body, name=name, out_shape=[_sds((B * S, G * LANES), BF16), _sds((B, NS, S, 1), F32)],
               grid=(B, 1, nq),
               in_specs=[_g_col_spec(tq, nq, q_cb, G), _g_kv_spec(NM, 0, G), _g_kv_spec(NM, G, G)],
               out_specs=[_g_col_spec(tq, nq, 0, G), _g_stat_col_spec(tq, G)],
               semantics=("arbitrary", "arbitrary", "arbitrary"), vmem=VMEM_BIG)(qa, kva, kva)


def _mem_bwd(qa, kva, doa, oa, lse, *, name, B, S, NM, q_cb, do_cb):
    G = MEM_GROUP
    NS = 2 * G
    tq = min(MEM_Q_TILE, S)
    nq = S // tq

    def body(q_ref, k_ref, v_ref, do_ref, o_ref, lse_ref, dq_ref, dk_ref, dv_ref):
        @pl.when(pl.program_id(2) == 0)
        def _():
            dk_ref[...] = jnp.zeros_like(dk_ref)
            dv_ref[...] = jnp.zeros_like(dv_ref)

        lane, qh = _streams(q_ref, G, SCALE)
        _, doh = _streams(do_ref, G)
        kblk = [k_ref[:, _lanes(g)] for g in range(G)]
        vblk = [v_ref[:, _lanes(g)] for g in range(G)]
        prod = [do_ref[:, _lanes(g)].astype(F32) * o_ref[:, _lanes(g)].astype(F32) for g in range(G)]
        ss = [lax.dot_general(qh[st], kblk[st // 2], NT, preferred_element_type=F32) for st in range(NS)]
        dps = [lax.dot_general(doh[st], vblk[st // 2], NT, preferred_element_type=F32) for st in range(NS)]
        dsb, pb = [], []
        for st in range(NS):
            delta = jnp.sum(jnp.where(_head_mask(lane, st % 2), prod[st // 2], 0.0), axis=-1, keepdims=True)
            p = jnp.exp(ss[st] - lse_ref[0, st])
            dsb.append((p * (dps[st] - delta)).astype(BF16))
            pb.append(p.astype(BF16))
        dks = [lax.dot_general(dsb[st], qh[st], TN, preferred_element_type=F32) for st in range(NS)]
        dvs = [lax.dot_general(pb[st], doh[st], TN, preferred_element_type=F32) for st in range(NS)]
        dqs = [jnp.dot(dsb[st], kblk[st // 2], preferred_element_type=F32) for st in range(NS)]
        for g in range(G):
            dk_ref[:, _lanes(g)] += dks[2 * g] + dks[2 * g + 1]
            dv_ref[:, _lanes(g)] += dvs[2 * g] + dvs[2 * g + 1]
            dq_ref[:, _lanes(g)] = (jnp.where(lane < HEAD_DIM, dqs[2 * g], dqs[2 * g + 1]) * SCALE).astype(BF16)

    return _pc(body, name=name,
               out_shape=[_sds((B * S, G * LANES), BF16), _sds((B * NM, G * LANES), F32), _sds((B * NM, G * LANES), F32)],
               grid=(B, 1, nq),
               in_specs=[_g_col_spec(tq, nq, q_cb, G), _g_kv_spec(NM, 0, G), _g_kv_spec(NM, G, G),
                         _g_col_spec(tq, nq, do_cb, G), _g_col_spec(tq, nq, 0, G), _g_stat_col_spec(tq, G)],
               out_specs=[_g_col_spec(tq, nq, 0, G), _g_kv_spec(NM, 0, G), _g_kv_spec(NM, 0, G)],
               semantics=("arbitrary", "arbitrary", "arbitrary"), vmem=VMEM_BIG)(qa, kva, kva, doa, oa, lse)


def _sigmoid(v):
    return 0.5 * jnp.tanh(0.5 * v) + 0.5


def _shift_rows(cur, halo_ref, first, rows_idx, k):
    out = pltpu.roll(cur, k, 0)
    top = out[0:8, :]
    for r in range(k):
        hr = halo_ref.shape[0] - k + r
        edge = jnp.where(first, 0.0, halo_ref[hr:hr + 1, :])
        top = jnp.where(rows_idx[0:8, :] == r, edge, top)
    return jnp.concatenate([top, out[8:, :]], axis=0)


def _shift_rows_up(cur, halo_ref, last, rows_idx, k, ts):
    out = pltpu.roll(cur, ts - k, 0)
    bottom = out[ts - 8:, :]
    for r in range(k):
        edge = jnp.where(last, 0.0, halo_ref[r:r + 1, :])
        bottom = jnp.where(rows_idx[0:8, :] == 8 - k + r, edge, bottom)
    return jnp.concatenate([out[:ts - 8, :], bottom], axis=0)


def _ffn_up_gate(x, g, w, cw, cb, *, name, S):
    T, D = x.shape
    F = w.shape[1] // 2
    tm = min(1024, S)
    tn = 256
    nj = F // tn
    tiles_per_seq = S // tm
    halo = 16

    def body(x_ref, xh_ref, g_ref, wg_ref, wv_ref, cwg_ref, cwv_ref, cbg_ref, cbv_ref,
             uc_ref, ub_ref, a_ref, hout_ref, h_ref, hh_ref, eg_ref, ev_ref):
        first = lax.rem(pl.program_id(0), tiles_per_seq) == 0

        @pl.when(pl.program_id(1) == 0)
        def _():
            def norm(v):
                r = lax.rsqrt(jnp.mean(v * v, axis=-1, keepdims=True) + EPS)
                return ((v * r) * g_ref[...]).astype(BF16)
            h = norm(x_ref[...])
            h_ref[...] = h
            hout_ref[...] = h
            hh_ref[...] = norm(xh_ref[...])

        h = h_ref[...]
        rows_idx = lax.broadcasted_iota(jnp.int32, (tm, tn), 0)
        uc = []
        for half, (w_ref, cw_ref, cb_ref, e_ref) in enumerate(((wg_ref, cwg_ref, cbg_ref, eg_ref),
                                                               (wv_ref, cwv_ref, cbv_ref, ev_ref))):
            acc = jnp.dot(h, w_ref[...], preferred_element_type=F32)
            e_ref[...] = jnp.dot(hh_ref[...], w_ref[...], preferred_element_type=F32)
            ub_ref[half] = acc.astype(BF16)
            m1 = _shift_rows(acc, e_ref, first, rows_idx, 1)
            m2 = _shift_rows(acc, e_ref, first, rows_idx, 2)
            uc.append(cb_ref[...] + cw_ref[0:1, :] * m2 + cw_ref[1:2, :] * m1 + cw_ref[2:3, :] * acc)
            uc_ref[half] = uc[half]
        a_ref[...] = (uc[0] * _sigmoid(uc[0]) * uc[1]).astype(BF16)

    in_specs = [pl.BlockSpec((tm, D), lambda i, j: (i, 0)),
                pl.BlockSpec((halo, D), lambda i, j: (jnp.maximum(i * (tm // halo) - 1, 0), 0)),
                pl.BlockSpec((1, D), lambda i, j: (0, 0)),
                pl.BlockSpec((D, tn), lambda i, j: (0, j)), pl.BlockSpec((D, tn), lambda i, j: (0, j + nj)),
                pl.BlockSpec((3, tn), lambda i, j: (0, j)), pl.BlockSpec((3, tn), lambda i, j: (0, j + nj)),
                pl.BlockSpec((1, tn), lambda i, j: (0, j)), pl.BlockSpec((1, tn), lambda i, j: (0, j + nj))]
    return _pc(body, name=name,
               out_shape=[_sds((2, T, F), F32), _sds((2, T, F), BF16), _sds((T, F), BF16), _sds((T, D), BF16)],
               grid=(T // tm, nj), in_specs=in_specs,
               out_specs=[pl.BlockSpec((2, tm, tn), lambda i, j: (0, i, j)), pl.BlockSpec((2, tm, tn), lambda i, j: (0, i, j)),
                          pl.BlockSpec((tm, tn), lambda i, j: (i, j)), pl.BlockSpec((tm, D), lambda i, j: (i, 0))],
               scratch_shapes=[pltpu.VMEM((tm, D), BF16), pltpu.VMEM((halo, D), BF16),
                               pltpu.VMEM((halo, tn), F32), pltpu.VMEM((halo, tn), F32)],
               semantics=("arbitrary", "arbitrary"), vmem=VMEM_BIG)(x, x, g.reshape(1, D), w, w, cw, cw, cb, cb)


def _conv_gate_bwd(da, uc, ub, cw, *, name, B, S):
    F = uc.shape[2]
    tf = F // 2
    ts = min(512, S)
    ns, nf = S // ts, F // tf

    def body(da_ref, uc_ref, ub_ref, wg_ref, wv_ref, dug_ref, duv_ref, pg_ref, pv_ref, nxt_g, nxt_v):
        last = pl.program_id(2) == 0

        @pl.when(jnp.logical_and(pl.program_id(1) == 0, last))
        def _():
            pg_ref[...] = jnp.zeros_like(pg_ref)
            pv_ref[...] = jnp.zeros_like(pv_ref)

        rows_idx = lax.broadcasted_iota(jnp.int32, (ts, tf), 0)
        ucg, ucv = uc_ref[0], uc_ref[1]
        sg = _sigmoid(ucg)
        dav = da_ref[...]
        d_v = dav * (ucg * sg)
        d_g = dav * ucv * (sg * (1.0 + ucg * (1.0 - sg)))
        for half, (o_ref, p_ref, d, w_ref, nxt) in enumerate(((dug_ref, pg_ref, d_g, wg_ref, nxt_g),
                                                               (duv_ref, pv_ref, d_v, wv_ref, nxt_v))):
            p1 = _shift_rows_up(d, nxt, last, rows_idx, 1, ts)
            p2 = _shift_rows_up(d, nxt, last, rows_idx, 2, ts)
            o_ref[...] = (w_ref[2:3, :] * d + w_ref[1:2, :] * p1 + w_ref[0:1, :] * p2).astype(BF16)
            nxt[...] = d[0:8, :]
            uh = ub_ref[half].astype(F32)
            for k, dk in enumerate((p2, p1, d)):
                p_ref[k:k + 1, :] += jnp.sum(dk * uh, axis=0, keepdims=True)
            p_ref[3:4, :] += jnp.sum(d, axis=0, keepdims=True)

    row = pl.BlockSpec((ts, tf), lambda j, b, r: (b * ns + ns - 1 - r, j))
    both = pl.BlockSpec((2, ts, tf), lambda j, b, r: (0, b * ns + ns - 1 - r, j))
    par = pl.BlockSpec((8, tf), lambda j, b, r: (0, j))
    return _pc(body, name=name,
               out_shape=[_sds((B * S, F), BF16), _sds((B * S, F), BF16), _sds((8, F), F32), _sds((8, F), F32)],
               grid=(nf, B, ns),
               in_specs=[row, both, both, pl.BlockSpec((3, tf), lambda j, b, r: (0, j)),
                         pl.BlockSpec((3, tf), lambda j, b, r: (0, j + nf))],
               out_specs=[row, row, par, par],
               scratch_shapes=[pltpu.VMEM((8, tf), F32), pltpu.VMEM((8, tf), F32)],
               semantics=("arbitrary", "arbitrary", "arbitrary"), vmem=VMEM_BIG)(da, uc, ub, cw, cw)


def _adamw(w, g, m, v, *, name):
    rows, cols = w.shape
    tr = rows
    while tr * cols * 4 > 2 ** 21 and tr % 16 == 0:
        tr //= 2

    def body(w_ref, g_ref, m_ref, v_ref, d_ref, nm_ref, nv_ref):
        gv = g_ref[...]
        m_new = ADAM_B1 * m_ref[...] + (1.0 - ADAM_B1) * gv
        v_new = ADAM_B2 * v_ref[...] + (1.0 - ADAM_B2) * (gv * gv)
        m_hat = m_new / (1.0 - ADAM_B1 ** ADAM_STEP)
        v_hat = v_new / (1.0 - ADAM_B2 ** ADAM_STEP)
        d_ref[...] = -ADAM_LR * (m_hat / (jnp.sqrt(v_hat) + ADAM_EPS) + ADAM_WD * w_ref[...])
        nm_ref[...] = m_new
        nv_ref[...] = v_new

    blk = pl.BlockSpec((tr, cols), lambda i: (i, 0))
    return _pc(body, name=name, out_shape=[_sds((rows, cols), F32)] * 3, grid=(rows // tr,),
               in_specs=[blk] * 4, out_specs=[blk] * 3, semantics=("arbitrary",))(w, g, m, v)


def _my_pos():
    return lax.axis_index("x"), lax.axis_index("y"), lax.axis_index("c")


_HBM = pl.BlockSpec(memory_space=pltpu.HBM)
_SEM = pl.BlockSpec(memory_space=pltpu.SEMAPHORE)
_EFFECT = pltpu.SideEffectType.DATAFLOW_SIDE_EFFECTING


def _peers():
    x, y, c = _my_pos()
    out = []
    for k in range(1, N_DEV):
        px, py, pc = x ^ ((k >> 2) & 1), y ^ ((k >> 1) & 1), c ^ (k & 1)
        out.append(((px, py, pc), 4 * px + 2 * py + pc))
    return out


def _scatter_start(srcs, slot_of, *, name):
    n = len(srcs)
    lands = [lax.empty((N_DEV,) + slot_of(s, 0, shape_only=True), s.dtype) for s in srcs]

    def body(*refs):
        src_refs, land_refs = refs[:n], refs[n:2 * n]
        send_sems, recv_sems = refs[2 * n], refs[2 * n + 1]
        token = refs[-1]
        x, y, c = _my_pos()
        me = 4 * x + 2 * y + c
        for a in range(n):
            for k, (peer, peer_idx) in enumerate(_peers()):
                pltpu.make_async_remote_copy(
                    src_ref=slot_of(src_refs[a], peer_idx), dst_ref=land_refs[a].at[me],
                    send_sem=send_sems.at[a * 7 + k], recv_sem=recv_sems.at[a * 7 + k],
                    device_id=peer, device_id_type=MESH).start()
        token[...] = jnp.zeros_like(token)

    hbm = lambda a: pltpu.HBM(a.shape, a.dtype)
    args = [pltpu.with_memory_space_constraint(a, pltpu.HBM) for a in list(srcs) + lands]
    outs = pl.pallas_call(
        body, name=name,
        out_shape=(pltpu.SemaphoreType.DMA((7 * n,)), pltpu.SemaphoreType.DMA((7 * n,)),
                   *[hbm(a) for a in srcs], *[hbm(a) for a in lands], _sds((8, LANES), F32)),
        in_specs=[_HBM] * (2 * n),
        out_specs=(_SEM, _SEM, *([_HBM] * (2 * n)), pl.BlockSpec(memory_space=pltpu.VMEM)),
        input_output_aliases={a: 2 + a for a in range(2 * n)},
        compiler_params=pltpu.CompilerParams(has_side_effects=_EFFECT))(*args)
    return outs[0], outs[1], list(outs[2:2 + n]), list(outs[2 + n:2 + 2 * n]), outs[-1]


def _scatter_wait(send_sems, recv_sems, srcs, lands, slot_of, after, *, name, first=0):
    n = len(srcs)

    def body(*refs):
        src_refs, land_refs = refs[:n], refs[n:2 * n]
        ssem, rsem = refs[2 * n], refs[2 * n + 1]
        x, y, c = _my_pos()
        me = 4 * x + 2 * y + c
        for a in range(n):
            for k, (peer, peer_idx) in enumerate(_peers()):
                cp = pltpu.make_async_remote_copy(
                    src_ref=slot_of(src_refs[a], peer_idx), dst_ref=land_refs[a].at[me],
                    send_sem=ssem.at[(first + a) * 7 + k], recv_sem=rsem.at[(first + a) * 7 + k],
                    device_id=peer, device_id_type=MESH)
                cp.wait_send()
                cp.wait_recv()

    hbm = lambda a: pltpu.HBM(a.shape, a.dtype)
    outs = pl.pallas_call(
        body, name=name, out_shape=tuple(hbm(a) for a in list(srcs) + list(lands)),
        in_specs=[_HBM] * (2 * n) + [_SEM, _SEM, pl.BlockSpec(memory_space=pl.ANY)],
        out_specs=tuple([_HBM] * (2 * n)), input_output_aliases={a: a for a in range(2 * n)},
        compiler_params=pltpu.CompilerParams(has_side_effects=_EFFECT))(*srcs, *lands, send_sems, recv_sems, after)
    return list(outs[:n]), list(outs[n:])


def _whole(a, peer_idx, shape_only=False):
    return a.shape if shape_only else a


def _slot(a, peer_idx, shape_only=False):
    return a.shape[1:] if shape_only else a.at[peer_idx]


def _sum_slots(a, *, name, tr=None):
    rows, cols = a.shape[1], a.shape[2]
    if tr is None:
        tr = rows
        while N_DEV * tr * cols * a.dtype.itemsize > 6 * 2 ** 20 and tr % 32 == 0:
            tr //= 2

    def body(a_ref, o_ref):
        acc = a_ref[0].astype(F32)
        for j in range(1, N_DEV):
            acc = acc + a_ref[j].astype(F32)
        o_ref[...] = acc

    return _pc(body, name=name, out_shape=_sds((rows, cols), F32), grid=(rows // tr,),
               in_specs=[pl.BlockSpec((N_DEV, tr, cols), lambda i: (0, i, 0))],
               out_specs=pl.BlockSpec((tr, cols), lambda i: (i, 0)), semantics=("arbitrary",), vmem=VMEM_BIG)(a)


def _to_slots(full, kind):
    if kind == "rows2":
        r, c = full.shape
        return full.reshape(N_DEV, r // N_DEV, c)
    if kind == "cols2":
        r, c = full.shape
        return full.reshape(r, N_DEV, c // N_DEV).transpose(1, 0, 2)
    if kind == "rows3":
        l, r, c = full.shape
        return full.reshape(l, N_DEV, r // N_DEV, c).transpose(1, 0, 2, 3)
    if kind == "cols3":
        l, r, c = full.shape
        return full.reshape(l, r, N_DEV, c // N_DEV).transpose(2, 0, 1, 3)
    raise ValueError(kind)


def _from_slots(slots, kind):
    if kind == "rows2":
        _, r, c = slots.shape
        return slots.reshape(N_DEV * r, c)
    if kind == "cols2":
        _, r, c = slots.shape
        return slots.transpose(1, 0, 2).reshape(r, N_DEV * c)
    if kind == "rows3":
        _, l, r, c = slots.shape
        return slots.transpose(1, 0, 2, 3).reshape(l, N_DEV * r, c)
    if kind == "cols3":
        _, l, r, c = slots.shape
        return slots.transpose(1, 2, 0, 3).reshape(l, r, N_DEV * c)
    raise ValueError(kind)


BIG = (("w_in_a", "rows2"), ("w_in_b", "rows2"), ("w_kv", "cols2"), ("w_memkv", "rows3"),
       ("w_out", "rows3"), ("w_up", "cols3"), ("w_down", "rows3"))


def _round_up(n, m):
    return -(-n // m) * m


def _pad_rows(a, rows, axis):
    pad = [(0, 0)] * a.ndim
    pad[axis] = (0, rows - a.shape[axis])
    return jnp.pad(a, pad)


def kernel(x, mem, ln_mix_g, w_in_a, b_f_a, w_in_b, ln_kv_g, w_kv, ln_mem_g, w_memkv, w_out, ln_ffn_g, w_up, conv_w, conv_b, w_down, final_g, loss_target, m_ln_mix_g, m_w_in_a, m_b_f_a, m_w_in_b, m_ln_kv_g, m_w_kv, m_ln_mem_g, m_w_memkv, m_w_out, m_ln_ffn_g, m_w_up, m_conv_w, m_conv_b, m_w_down, m_final_g, v_ln_mix_g, v_w_in_a, v_b_f_a, v_w_in_b, v_ln_kv_g, v_w_kv, v_ln_mem_g, v_w_memkv, v_w_out, v_ln_ffn_g, v_w_up, v_conv_w, v_conv_b, v_w_down, v_final_g):
    B, S, D = x.shape
    NM = mem.shape[1]
    T = B * S
    F = w_down.shape[1] * N_DEV
    my_idx = 4 * lax.axis_index("x") + 2 * lax.axis_index("y") + lax.axis_index("c")

    shards = {"w_in_a": w_in_a[0], "w_in_b": w_in_b[0], "w_kv": w_kv, "w_memkv": w_memkv, "w_out": w_out,
              "w_up": w_up, "w_down": w_down}
    moms = {"w_in_a": (m_w_in_a[0], v_w_in_a[0]), "w_in_b": (m_w_in_b[0], v_w_in_b[0]), "w_kv": (m_w_kv, v_w_kv),
            "w_memkv": (m_w_memkv, v_w_memkv), "w_out": (m_w_out, v_w_out), "w_up": (m_w_up, v_w_up),
            "w_down": (m_w_down, v_w_down)}

    groups = [("a1", [("w_in_a", None)]),
              ("a2", [("w_memkv", None), ("w_out", None), ("conv_w", None)]),
              ("b0", [("w_up", 0), ("w_down", 0)]), ("a3", [("w_in_b", None), ("w_kv", None)]),
              ("b1", [("w_up", 1), ("w_down", 1)])]
    sources = dict(shards, conv_w=conv_w)
    srcs, span = [], {}
    for gname, members in groups:
        span[gname] = (len(srcs), len(members))
        for n, layer in members:
            a = sources[n] if layer is None else sources[n][layer]
            srcs.append(a if n == "conv_w" else a.astype(BF16))
    g_ssem, g_rsem, g_thru, g_lands, token = _scatter_start(srcs, _whole, name="gather_start")

    def gathered(gname, after):
        lo, cnt = span[gname]
        thru, lands = _scatter_wait(g_ssem, g_rsem, g_thru[lo:lo + cnt], g_lands[lo:lo + cnt], _whole, after,
                                    name=f"gather_wait_{gname}", first=lo)
        return [lax.dynamic_update_index_in_dim(land, s, my_idx, 0) for land, s in zip(lands, thru)]

    full = {}
    (g_wa,) = gathered("a1", token)
    full["w_in_a"] = _from_slots(g_wa, "rows2")

    wa = full["w_in_a"]
    n_qkv = 3 * MAIN_W
    wa = jnp.concatenate([wa[:, :n_qkv], wa[:, n_qkv + N_MAIN_HEADS:], wa[:, n_qkv:n_qkv + N_MAIN_HEADS],
                          jnp.zeros((D, LANES - N_MAIN_HEADS), BF16)], axis=1)
    n_main = n_qkv + MEM_W
    full["w_up"], full["w_down"] = {}, {}
    b_f =_pad_rows(b_f_a.reshape(1, N_MAIN_HEADS), LANES, 1)

    x2d = x.reshape(T, D)
    mem2d = mem.reshape(B * NM, D)
    tgt2d = loss_target.reshape(T, D)
    PM, PX = N_MAIN_HEADS // 2, N_MEM_HEADS // 2

    def stats_to_heads(c2d):
        c = c2d.reshape(B, S, LANES)[:, :, :N_MAIN_HEADS].transpose(0, 2, 1)
        return c[:, :, None, :]

    def mem_kv(layer):
        return _mm_fwd(mem2d, full["w_memkv"][layer], name=f"memkv{layer}", tm=B * NM, tn=2 * MEM_W,
                       out_dtype=BF16, g=ln_mem_g[layer], save_h=True)

    def conv_ffn_fwd(xin, layer):
        uc, ub, a, h = _ffn_up_gate(xin, ln_ffn_g[layer], full["w_up"][layer], conv_w_full[layer],
                                    conv_b[layer].reshape(1, 2 * F), name=f"ffn_up{layer}", S=S)
        xo = _mm_fwd(a, full["w_down"][layer], name=f"ffn_down{layer}", tm=min(1024, T), tn=1024, out_dtype=F32, res=xin)
        return xo, (uc, ub, h, a)

    proj_a, h_mix0 = _mm_fwd(x2d, wa, name="in_proj_a", tm=min(1024, T), tn=2560, out_dtype=BF16, g=ln_mix_g[0],
                             ncols=n_main, save_h=True)
    f_logit = _mm_fwd(x2d, wa, name="in_proj_f", tm=min(1024, T), tn=LANES, out_dtype=F32, g=ln_mix_g[0],
                      col0=n_main // LANES, ncols=LANES)
    c2d = _forget_cumsum(f_logit, b_f, B=B, S=S, name="forget_cumsum")
    cr = stats_to_heads(c2d)
    o_main0, lse0 = _fox_fwd_g(proj_a, proj_a, proj_a, cr, name="fox_fwd", B=B, S=S, P=PM, q_cb=0, k_cb=PM, v_cb=2 * PM,
                               G=FWD_HEAD_GROUP)
    g_wmem, g_wout, g_cw = gathered("a2", lse0)
    full["w_memkv"] = _from_slots(g_wmem, "rows3")
    full["w_out"] = _from_slots(g_wout, "rows3")
    conv_w_full = _from_slots(g_cw, "cols3")
    memkv0, h_mem0 = mem_kv(0)
    o_mem0, lse_m0 = _mem_fwd(proj_a, memkv0, name="mem_fwd0", B=B, S=S, NM=NM, q_cb=3 * PM)
    o_cat0 = jnp.concatenate([o_main0, o_mem0], axis=1)
    x1 = _mm_fwd(o_cat0, full["w_out"][0], name="out_proj0", tm=min(1024, T), tn=1024, out_dtype=F32, res=x2d)
    g_up, g_dn = gathered("b0", x1)
    full["w_up"][0], full["w_down"][0] = _from_slots(g_up, "cols2"), _from_slots(g_dn, "rows2")
    x2, ffn_saved0 = conv_ffn_fwd(x1, 0)
    g_wb, g_wkv = gathered("a3", x2)
    wb, wkv = _from_slots(g_wb, "rows2"), _from_slots(g_wkv, "cols2")
    kv, h_kv =_mm_fwd(x2, wkv, name="kv_proj", tm=min(1024, T), tn=1536, out_dtype=BF16, g=ln_kv_g, save_h=True)
    proj_b, h_mix1 = _mm_fwd(x2, wb, name="in_proj_b", tm=min(1024, T), tn=1024, out_dtype=BF16, g=ln_mix_g[1],
                             save_h=True)
    o_main1, rt1 = _sb_fwd_g(proj_b, kv, kv, name="sb_fwd", B=B, S=S, P=PM, q_cb=0, k_cb=0, v_cb=PM,
                             G=FWD_HEAD_GROUP)
    memkv1, h_mem1 = mem_kv(1)
    o_mem1, lse_m1 = _mem_fwd(proj_b, memkv1, name="mem_fwd1", B=B, S=S, NM=NM, q_cb=PM)
    o_cat1 = jnp.concatenate([o_main1, o_mem1], axis=1)
    x3 = _mm_fwd(o_cat1, full["w_out"][1], name="out_proj1", tm=min(1024, T), tn=1024, out_dtype=F32, res=x2)
    g_up, g_dn = gathered("b1", x3)
    full["w_up"][1], full["w_down"][1] = _from_slots(g_up, "cols2"), _from_slots(g_dn, "rows2")
    x4, ffn_saved1 = conv_ffn_fwd(x3, 1)
    dx4, dg_final, loss_part = _loss_head(x4, final_g, tgt2d, name="loss_head")

    grads = {}
    small = {}
    reduce_groups = []

    def start_reduce(gname, keys, kinds):
        slots = [_to_slots(grads[k], kind) for k, kind in zip(keys, kinds)]
        ssem, rsem, thru, lands, tok = _scatter_start(slots, _slot, name=f"reduce_start_{gname}")
        reduce_groups.append((gname, keys, ssem, rsem, thru, lands))
        return tok[0, 0]

    def conv_ffn_bwd(dxo, xin, saved, layer):
        uc, ub, h, a = saved
        w_dn = full["w_down"][layer]
        da = _mm_nt(dxo, w_dn, name=f"d_act{layer}", tm=min(1024, T), tn=F // 2, out_dtype=F32)
        grads[("w_down", layer)] = _wgrad(a, dxo, f"g_w_down{layer}")
        cwl = conv_w_full[layer]
        du_g, du_v, p_g, p_v = _conv_gate_bwd(da, uc, ub, cwl, name=f"conv_bwd{layer}", B=B, S=S)
        small[("conv_w", layer)] = jnp.concatenate([p_g[0:3], p_v[0:3]], axis=1)
        small[("conv_b", layer)] = jnp.concatenate([p_g[3], p_v[3]], axis=0)
        grads[("w_up", layer)] = jnp.concatenate(
            [_wgrad(h, du_g, f"g_w_up_gate{layer}"), _wgrad(h, du_v, f"g_w_up_val{layer}")], axis=1)
        tok = start_reduce(f"ffn{layer}", [("w_down", layer), ("w_up", layer)], ["rows2", "cols2"])
        dxi, dg = _mm_nt_rmsbwd([(du_g, 0), (du_v, 1)], full["w_up"][layer], xin, ln_ffn_g[layer] + tok,
                                name=f"d_ffn_in{layer}", dres=dxo)
        small[("ln_ffn_g", layer)] = dg[0]
        return dxi

    def mem_bwd(proj, q_cb, memkv, h_mem, do_cat, o_mem, lse_m, layer):
        dqm, dmk, dmv = _mem_bwd(proj, memkv, do_cat, o_mem, lse_m, name=f"mem_bwd{layer}", B=B, S=S, NM=NM,
                                 q_cb=q_cb, do_cb=PM)
        grads[("w_memkv", layer)] = jnp.concatenate(
            [_wgrad(h_mem, dmk, f"g_w_memk{layer}"), _wgrad(h_mem, dmv, f"g_w_memv{layer}")], axis=1)
        _, dg = _mm_nt_rmsbwd([(dmk, 0), (dmv, 1)], full["w_memkv"][layer], mem2d, ln_mem_g[layer],
                              name=f"d_mem_in{layer}", want_dx=False)
        small[("ln_mem_g", layer)] = dg[0]
        return dqm

    dx3 = conv_ffn_bwd(dx4, x3, ffn_saved1, 1)
    do_cat1 = _mm_nt(dx3, full["w_out"][1], name="d_o_cat1", tm=min(1024, T), tn=1024, out_dtype=BF16)
    grads[("w_out", 1)] = _wgrad(o_cat1, dx3, "g_w_out1")
    dq1, dk1, dv1 = _sb_bwd_g(proj_b, kv, kv, do_cat1, rt1, name="sb_bwd", B=B, S=S, P=PM, q_cb=0, k_cb=0, v_cb=PM,
                            do_cb=0)
    dqm1 = mem_bwd(proj_b, PM, memkv1, h_mem1, do_cat1, o_mem1, lse_m1, 1)
    grads["w_in_b"] = jnp.concatenate([_wgrad(h_mix1, dq1, "g_w_in_b_q"), _wgrad(h_mix1, dqm1, "g_w_in_b_m")], axis=1)
    grads["w_kv"] = jnp.concatenate([_wgrad(h_kv, dk1, "g_w_kv_k"), _wgrad(h_kv, dv1, "g_w_kv_v")], axis=1)
    tok = start_reduce("mix1", [("w_out", 1), "w_in_b", "w_kv", ("w_memkv", 1)], ["rows2", "rows2", "cols2", "rows2"])
    dx2, dg = _mm_nt_rmsbwd([(dq1, 0), (dqm1, MAIN_W // MEM_W)], wb, x2, ln_mix_g[1] + tok, name="d_mix_in1", dres=dx3)
    small[("ln_mix_g", 1)] = dg[0]
    dx2, dg = _mm_nt_rmsbwd([(dk1, 0), (dv1, 1)], wkv, x2, ln_kv_g, name="d_kv_in", dres=dx2)
    small["ln_kv_g"] = dg[0]
    dx1 = conv_ffn_bwd(dx2, x1, ffn_saved0, 0)
    do_cat0 = _mm_nt(dx1, full["w_out"][0], name="d_o_cat0", tm=min(1024, T), tn=1024, out_dtype=BF16)
    grads[("w_out", 0)] = _wgrad(o_cat0, dx1, "g_w_out0")
    dq0, dk0, dv0, dcs = _fox_bwd_g(proj_a, proj_a, proj_a, do_cat0, lse0, cr, name="fox_bwd", B=B, S=S, P=PM, q_cb=0,
                                  k_cb=PM, v_cb=2 * PM, do_cb=0)
    dqm0 = mem_bwd(proj_a, 3 * PM, memkv0, h_mem0, do_cat0, o_mem0, lse_m0, 0)
    dc2d = _pad_rows(dcs[:, :, 0, :].transpose(0, 2, 1).reshape(T, N_MAIN_HEADS), LANES, 1)
    df, db_f = _forget_cumsum_bwd(dc2d, f_logit, b_f, B=B, S=S, name="forget_cumsum_bwd")
    a_parts = [(dq0, 0), (dk0, 1), (dv0, 2), (dqm0, n_qkv // MEM_W), (df, n_main // LANES)]
    g_wa = jnp.concatenate([_wgrad(h_mix0, p, f"g_w_in_a{k}") for k, (p, _) in enumerate(a_parts)], axis=1)
    grads["w_in_a"] = jnp.concatenate([g_wa[:, :n_qkv], g_wa[:, n_main:n_main + N_MAIN_HEADS], g_wa[:, n_qkv:n_main]],
                                      axis=1)
    tok = start_reduce("mix0", [("w_out", 0), ("w_memkv", 0), "w_in_a"], ["rows2", "rows2", "rows2"])
    dx0, dg = _mm_nt_rmsbwd(a_parts, wa, x2d, ln_mix_g[0] + tok, name="d_mix_in0", dres=dx1)
    small[("ln_mix_g", 0)] = dg[0]
    grad_x = dx0.reshape(B, S, D)

    def both_small(name):
        return jnp.stack([small[(name, 0)], small[(name, 1)]])

    small_list = [("ln_mix_g", both_small("ln_mix_g")), ("b_f_a", db_f[:, :N_MAIN_HEADS]), ("ln_kv_g", small["ln_kv_g"]),
                  ("ln_mem_g", both_small("ln_mem_g")), ("ln_ffn_g", both_small("ln_ffn_g")),
                  ("conv_w", both_small("conv_w")), ("conv_b", both_small("conv_b")), ("final_g", dg_final[0]),
                  ("loss", loss_part[0, :1])]
    sm_rows = []
    for _, a in small_list:
        flat = a.reshape(-1)
        sm_rows.append(_pad_rows(flat, _round_up(flat.size, 8 * LANES), 0).reshape(-1, LANES))
    spack = jnp.concatenate(sm_rows, axis=0)
    s_ssem, s_rsem, s_thru, s_lands, s_tok = _scatter_start([spack], _whole, name="small_start")

    pieces = {}
    for gname, keys, ssem, rsem, thru, lands in reduce_groups:
        thru, lands = _scatter_wait(ssem, rsem, thru, lands, _slot, s_tok, name=f"reduce_wait_{gname}")
        for key, mine, land in zip(keys, thru, lands):
            own = lax.dynamic_index_in_dim(mine, my_idx, 0, keepdims=False)
            land = lax.dynamic_update_index_in_dim(land, own, my_idx, 0)
            tag = key if isinstance(key, str) else f"{key[0]}{key[1]}"
            pieces[key] = _sum_slots(land, name=f"sum_{tag}")

    red = {}
    for n in ("w_in_a", "w_in_b", "w_kv"):
        red[n] = pieces[n].reshape(shards[n].shape)
    for n in ("w_memkv", "w_out", "w_up", "w_down"):
        red[n] = jnp.stack([pieces[(n, 0)], pieces[(n, 1)]])

    weights = {"ln_mix_g": ln_mix_g, "w_in_a": w_in_a, "b_f_a": b_f_a, "w_in_b": w_in_b, "ln_kv_g": ln_kv_g,
               "w_kv": w_kv, "ln_mem_g": ln_mem_g, "w_memkv": w_memkv, "w_out": w_out, "ln_ffn_g": ln_ffn_g,
               "w_up": w_up, "conv_w": conv_w, "conv_b": conv_b, "w_down": w_down, "final_g": final_g}
    m_in = {"ln_mix_g": m_ln_mix_g, "w_in_a": m_w_in_a, "b_f_a": m_b_f_a, "w_in_b": m_w_in_b, "ln_kv_g": m_ln_kv_g,
            "w_kv": m_w_kv, "ln_mem_g": m_ln_mem_g, "w_memkv": m_w_memkv, "w_out": m_w_out, "ln_ffn_g": m_ln_ffn_g,
            "w_up": m_w_up, "conv_w": m_conv_w, "conv_b": m_conv_b, "w_down": m_w_down, "final_g": m_final_g}
    v_in = {"ln_mix_g": v_ln_mix_g, "w_in_a": v_w_in_a, "b_f_a": v_b_f_a, "w_in_b": v_w_in_b, "ln_kv_g": v_ln_kv_g,
            "w_kv": v_w_kv, "ln_mem_g": v_ln_mem_g, "w_memkv": v_w_memkv, "w_out": v_w_out, "ln_ffn_g": v_ln_ffn_g,
            "w_up": v_w_up, "conv_w": v_conv_w, "conv_b": v_conv_b, "w_down": v_w_down, "final_g": v_final_g}
    order = list(weights)
    big_names = [n for n, _ in BIG]
    g_out, d_out, nm_out, nv_out = {}, {}, {}, {}

    def update(n):
        w = weights[n]
        cols = w.shape[-1]
        g = red[n].reshape(w.shape)
        d, nm, nv = _adamw(w.reshape(-1, cols), g.reshape(-1, cols), m_in[n].reshape(-1, cols),
                           v_in[n].reshape(-1, cols), name=f"adamw_{n}")
        g_out[n], d_out[n], nm_out[n], nv_out[n] = g, d.reshape(w.shape), nm.reshape(w.shape), nv.reshape(w.shape)

    for n in big_names:
        update(n)
    all_updated = jnp.stack([d_out[n].reshape(-1)[0] for n in big_names])
    s_thru, s_lands = _scatter_wait(s_ssem, s_rsem, s_thru, s_lands, _whole, all_updated, name="small_wait")
    ssum = _sum_slots(lax.dynamic_update_index_in_dim(s_lands[0], s_thru[0], my_idx, 0), name="sum_small")
    off = 0
    for (n, a), rows in zip(small_list, sm_rows):
        red[n] = ssum[off:off + rows.shape[0]].reshape(-1)[:a.size].reshape(a.shape)
        off += rows.shape[0]
    loss = red["loss"][0]
    shard_cols = conv_w.shape[2]
    red["conv_w"] = lax.dynamic_slice_in_dim(red["conv_w"], my_idx * shard_cols, shard_cols, axis=2)
    red["b_f_a"] = red["b_f_a"].reshape(b_f_a.shape)
    update("conv_w")
    small_names = [n for n in order if n not in g_out]

    def pack_small(src):
        rows = []
        for n in small_names:
            flat = src[n].reshape(-1)
            rows.append(_pad_rows(flat, _round_up(flat.size, 8 * LANES), 0).reshape(-1, LANES))
        return jnp.concatenate(rows, axis=0), [r.shape[0] for r in rows]

    red_small = {n: red[n].reshape(weights[n].shape) for n in small_names}
    wp, counts = pack_small(weights)
    gp, _ = pack_small(red_small)
    mp, _ = pack_small(m_in)
    vp, _ = pack_small(v_in)
    dp, nmp, nvp = _adamw(wp, gp, mp, vp, name="adamw_small")
    off = 0
    for n, cnt in zip(small_names, counts):
        shp = weights[n].shape
        size = weights[n].size
        g_out[n] = red_small[n]
        d_out[n] = dp[off:off + cnt].reshape(-1)[:size].reshape(shp)
        nm_out[n] = nmp[off:off + cnt].reshape(-1)[:size].reshape(shp)
        nv_out[n] = nvp[off:off + cnt].reshape(-1)[:size].reshape(shp)
        off += cnt

    return (loss, grad_x, *[g_out[n] for n in order], *[d_out[n] for n in order],
            *[nm_out[n] for n in order], *[nv_out[n] for n in order])
```

```python
import functools

import jax
import jax.numpy as jnp
from jax import lax
from jax.experimental import pallas as pl
from jax.experimental.pallas import tpu as pltpu

F32 = jnp.float32
BF16 = jnp.bfloat16
LANES = 128
HEAD_DIM = 64
N_MAIN_HEADS = 12
N_MEM_HEADS = 4
MAIN_W = N_MAIN_HEADS * HEAD_DIM
MEM_W = N_MEM_HEADS * HEAD_DIM
SCALE = HEAD_DIM ** -0.5
EPS = 1e-6
NEG = -1e30
N_DEV = 8
ATT_TILE = 256
MEM_Q_TILE = 1024
VMEM_BIG = 56 * 2 ** 20
MESH = pl.DeviceIdType.MESH

ADAM_LR = 0.001
ADAM_B1 = 0.9
ADAM_B2 = 0.999
ADAM_EPS = 1e-08
ADAM_WD = 0.01
ADAM_STEP = 10

NT = (((1,), (1,)), ((), ()))
TN = (((0,), (0,)), ((), ()))


def _pc(body, *, name, out_shape, grid=None, in_specs=None, out_specs=None, scratch_shapes=(),
        semantics=None, vmem=None):
    kw = {}
    if grid is not None:
        kw["grid"] = grid
    params = pltpu.CompilerParams(dimension_semantics=semantics, vmem_limit_bytes=vmem)
    return pl.pallas_call(body, name=name, out_shape=out_shape, in_specs=in_specs, out_specs=out_specs,
                          scratch_shapes=list(scratch_shapes), compiler_params=params, **kw)


def _sds(shape, dtype):
    return jax.ShapeDtypeStruct(shape, dtype)


def _mm_fwd(a, w, *, name, tm, tn, out_dtype, g=None, res=None, col0=0, ncols=None, save_h=False):
    m_rows, k = a.shape
    n = w.shape[1] if ncols is None else ncols
    grid = (m_rows // tm, n // tn)
    norm = g is not None

    def body(*refs):
        refs = list(refs)
        a_ref = refs.pop(0)
        g_ref = refs.pop(0) if norm else None
        w_ref = refs.pop(0)
        res_ref = refs.pop(0) if res is not None else None
        o_ref = refs.pop(0)
        hout_ref = refs.pop(0) if save_h else None
        h_ref = refs.pop(0) if norm else None
        if norm:
            @pl.when(pl.program_id(1) == 0)
            def _():
                xv = a_ref[...]
                r = lax.rsqrt(jnp.mean(xv * xv, axis=-1, keepdims=True) + EPS)
                h = ((xv * r) * g_ref[...]).astype(BF16)
                h_ref[...] = h
                if save_h:
                    hout_ref[...] = h
            lhs = h_ref[...]
        else:
            lhs = a_ref[...].astype(BF16)
        acc = jnp.dot(lhs, w_ref[...], preferred_element_type=F32)
        if res is not None:
            acc = acc + res_ref[...]
        o_ref[...] = acc.astype(out_dtype)

    in_specs = [pl.BlockSpec((tm, k), lambda i, j: (i, 0))]
    args = [a]
    if norm:
        in_specs.append(pl.BlockSpec((1, k), lambda i, j: (0, 0)))
        args.append(g.reshape(1, k))
    in_specs.append(pl.BlockSpec((k, tn), lambda i, j: (0, j + col0)))
    args.append(w)
    if res is not None:
        in_specs.append(pl.BlockSpec((tm, tn), lambda i, j: (i, j)))
        args.append(res)
    out_shape = [_sds((m_rows, n), out_dtype)]
    out_specs = [pl.BlockSpec((tm, tn), lambda i, j: (i, j))]
    if save_h:
        out_shape.append(_sds((m_rows, k), BF16))
        out_specs.append(pl.BlockSpec((tm, k), lambda i, j: (i, 0)))
    scratch = [pltpu.VMEM((tm, k), BF16)] if norm else []
    outs = _pc(body, name=name, out_shape=out_shape, grid=grid, in_specs=in_specs, out_specs=out_specs,
               scratch_shapes=scratch, semantics=("arbitrary", "arbitrary"), vmem=VMEM_BIG)(*args)
    return outs if save_h else outs[0]


def _mm_nt(a, w, *, name, tm, tn, out_dtype):
    m_rows, k = a.shape
    n = w.shape[0]

    def body(a_ref, w_ref, o_ref):
        acc = lax.dot_general(a_ref[...].astype(BF16), w_ref[...], NT, preferred_element_type=F32)
        o_ref[...] = acc.astype(out_dtype)

    return _pc(body, name=name, out_shape=_sds((m_rows, n), out_dtype), grid=(m_rows // tm, n // tn),
               in_specs=[pl.BlockSpec((tm, k), lambda i, j: (i, 0)), pl.BlockSpec((tn, k), lambda i, j: (j, 0))],
               out_specs=pl.BlockSpec((tm, tn), lambda i, j: (i, j)),
               semantics=("arbitrary", "arbitrary"), vmem=VMEM_BIG)(a, w)


def _mm_tn(a, b, *, name, ta, tn, tt):
    t_rows, ka = a.shape
    n = b.shape[1]
    nt = t_rows // tt

    def body(a_ref, b_ref, o_ref, acc_ref):
        t = pl.program_id(2)

        @pl.when(t == 0)
        def _():
            acc_ref[...] = jnp.zeros_like(acc_ref)

        acc_ref[...] += lax.dot_general(a_ref[...].astype(BF16), b_ref[...].astype(BF16), TN,
                                        preferred_element_type=F32)

        @pl.when(t == nt - 1)
        def _():
            o_ref[...] = acc_ref[...].astype(BF16)

    return _pc(body, name=name, out_shape=_sds((ka, n), BF16), grid=(ka // ta, n // tn, nt),
               in_specs=[pl.BlockSpec((tt, ta), lambda i, j, t: (t, i)),
                         pl.BlockSpec((tt, tn), lambda i, j, t: (t, j))],
               out_specs=pl.BlockSpec((ta, tn), lambda i, j, t: (i, j)),
               scratch_shapes=[pltpu.VMEM((ta, tn), F32)],
               semantics=("arbitrary", "arbitrary", "arbitrary"), vmem=VMEM_BIG)(a, b)


def _wgrad(a, b, name):
    t_rows, ka = a.shape
    n = b.shape[1]
    ta = ka if ka <= 1024 else ka // 2
    tn = n
    while ta * tn * 4 > 6 * 2 ** 20 and tn % 256 == 0:
        tn //= 2
    tt = min(2048, t_rows)
    return _mm_tn(a, b, name=name, ta=ta, tn=tn, tt=tt)


def _mm_nt_rmsbwd(parts, w, x, g, *, name, dres=None, want_dx=True):
    m_rows, d = x.shape
    tm = min(512, m_rows)
    n_parts = len(parts)

    def body(*refs):
        refs = list(refs)
        dy_refs = [refs.pop(0) for _ in range(n_parts)]
        w_refs = [refs.pop(0) for _ in range(n_parts)]
        x_ref = refs.pop(0)
        g_ref = refs.pop(0)
        dres_ref = refs.pop(0) if dres is not None else None
        dx_ref = refs.pop(0) if want_dx else None
        dg_ref = refs.pop(0)

        @pl.when(pl.program_id(0) == 0)
        def _():
            dg_ref[...] = jnp.zeros_like(dg_ref)

        dh = None
        for dy_ref, w_ref in zip(dy_refs, w_refs):
            t = lax.dot_general(dy_ref[...].astype(BF16), w_ref[...], NT, preferred_element_type=F32)
            dh = t if dh is None else dh + t
        xv = x_ref[...]
        r = lax.rsqrt(jnp.mean(xv * xv, axis=-1, keepdims=True) + EPS)
        xh = xv * r
        dg_ref[...] += jnp.sum(dh * xh, axis=0, keepdims=True)
        if want_dx:
            dhg = dh * g_ref[...]
            dx = r * (dhg - xh * jnp.mean(dhg * xh, axis=-1, keepdims=True))
            if dres is not None:
                dx = dx + dres_ref[...]
            dx_ref[...] = dx

    in_specs, args = [], []
    for dy, _ in parts:
        in_specs.append(pl.BlockSpec((tm, dy.shape[1]), lambda i: (i, 0)))
        args.append(dy)
    for dy, cb in parts:
        in_specs.append(pl.BlockSpec((d, dy.shape[1]), functools.partial(lambda i, cb: (0, cb), cb=cb)))
        args.append(w)
    in_specs += [pl.BlockSpec((tm, d), lambda i: (i, 0)), pl.BlockSpec((1, d), lambda i: (0, 0))]
    args += [x, g.reshape(1, d)]
    if dres is not None:
        in_specs.append(pl.BlockSpec((tm, d), lambda i: (i, 0)))
        args.append(dres)
    out_shape, out_specs = [], []
    if want_dx:
        out_shape.append(_sds((m_rows, d), F32))
        out_specs.append(pl.BlockSpec((tm, d), lambda i: (i, 0)))
    out_shape.append(_sds((1, d), F32))
    out_specs.append(pl.BlockSpec((1, d), lambda i: (0, 0)))
    outs = _pc(body, name=name, out_shape=out_shape, grid=(m_rows // tm,), in_specs=in_specs,
               out_specs=out_specs, semantics=("arbitrary",), vmem=VMEM_BIG)(*args)
    return (outs[0], outs[1]) if want_dx else (None, outs[0])


def _loss_head(x, g, tgt, *, name):
    m_rows, d = x.shape
    tm = min(512, m_rows)

    def body(x_ref, g_ref, t_ref, dx_ref, dg_ref, loss_ref):
        @pl.when(pl.program_id(0) == 0)
        def _():
            dg_ref[...] = jnp.zeros_like(dg_ref)
            loss_ref[...] = jnp.zeros_like(loss_ref)

        xv = x_ref[...]
        r = lax.rsqrt(jnp.mean(xv * xv, axis=-1, keepdims=True) + EPS)
        xh = xv * r
        gv = g_ref[...]
        err = xh * gv - t_ref[...]
        per_tok = jnp.mean(err * err, axis=-1, keepdims=True)
        loss_ref[...] += 0.5 * jnp.sum(per_tok, axis=0, keepdims=True)
        dout = err * (1.0 / d)
        dg_ref[...] += jnp.sum(dout * xh, axis=0, keepdims=True)
        dhg = dout * gv
        dx_ref[...] = r * (dhg - xh * jnp.mean(dhg * xh, axis=-1, keepdims=True))

    row = pl.BlockSpec((tm, d), lambda i: (i, 0))
    return _pc(body, name=name, out_shape=[_sds((m_rows, d), F32), _sds((1, d), F32), _sds((1, LANES), F32)],
               grid=(m_rows // tm,), in_specs=[row, pl.BlockSpec((1, d), lambda i: (0, 0)), row],
               out_specs=[row, pl.BlockSpec((1, d), lambda i: (0, 0)), pl.BlockSpec((1, LANES), lambda i: (0, 0))],
               semantics=("arbitrary",))(x, g.reshape(1, d), tgt)


def _split3(v):
    hi = v.astype(BF16)
    r1 = v - hi.astype(F32)
    mid = r1.astype(BF16)
    lo = (r1 - mid.astype(F32)).astype(BF16)
    return hi, mid, lo


def _split2(v):
    hi = v.astype(BF16)
    lo = (v - hi.astype(F32)).astype(BF16)
    return hi, lo


def _tri_dot3(tri, v):
    hi, mid, lo = _split3(v)
    return (jnp.dot(tri, hi, preferred_element_type=F32) + jnp.dot(tri, mid, preferred_element_type=F32)
            + jnp.dot(tri, lo, preferred_element_type=F32))


def _log_sigmoid(v):
    return jnp.minimum(v, 0.0) - jnp.log(1.0 + jnp.exp(-jnp.abs(v)))


def _forget_cumsum(f_logit, b_f, *, B, S, name):
    ch = min(256, S)
    nch = S // ch

    def body(f_ref, b_ref, c_ref):
        r_i = lax.broadcasted_iota(jnp.int32, (ch, ch), 0)
        c_i = lax.broadcasted_iota(jnp.int32, (ch, ch), 1)
        tri = (c_i <= r_i).astype(BF16)
        bv = b_ref[...]

        def step(k, carry):
            rows = pl.ds(pl.multiple_of(k * ch, ch), ch)
            lf = _log_sigmoid(f_ref[rows, :] + bv)
            c_ref[rows, :] = _tri_dot3(tri, lf) + carry
            return carry + jnp.sum(lf, axis=0, keepdims=True)

        lax.fori_loop(0, nch, step, jnp.zeros((1, LANES), F32))

    blk = pl.BlockSpec((S, LANES), lambda b: (b, 0))
    return _pc(body, name=name, out_shape=_sds((B * S, LANES), F32), grid=(B,),
               in_specs=[blk, pl.BlockSpec((1, LANES), lambda b: (0, 0))], out_specs=blk,
               semantics=("arbitrary",))(f_logit, b_f)


def _forget_cumsum_bwd(dc, f_logit, b_f, *, B, S, name):
    ch = min(256, S)
    nch = S // ch

    def body(dc_ref, f_ref, b_ref, df_ref, db_ref):
        @pl.when(pl.program_id(0) == 0)
        def _():
            db_ref[...] = jnp.zeros_like(db_ref)

        r_i = lax.broadcasted_iota(jnp.int32, (ch, ch), 0)
        c_i = lax.broadcasted_iota(jnp.int32, (ch, ch), 1)
        tri = (c_i >= r_i).astype(BF16)
        bv = b_ref[...]

        def step(kk, carry):
            tail, dbs = carry
            k = nch - 1 - kk
            rows = pl.ds(pl.multiple_of(k * ch, ch), ch)
            dcv = dc_ref[rows, :]
            dlf = _tri_dot3(tri, dcv) + tail
            z = f_ref[rows, :] + bv
            df = dlf * (1.0 / (1.0 + jnp.exp(z)))
            df_ref[rows, :] = df.astype(BF16)
            return tail + jnp.sum(dcv, axis=0, keepdims=True), dbs + jnp.sum(df, axis=0, keepdims=True)

        zero = jnp.zeros((1, LANES), F32)
        _, dbs = lax.fori_loop(0, nch, step, (zero, zero))
        db_ref[...] += dbs

    blk = pl.BlockSpec((S, LANES), lambda b: (b, 0))
    one = pl.BlockSpec((1, LANES), lambda b: (0, 0))
    return _pc(body, name=name, out_shape=[_sds((B * S, LANES), BF16), _sds((1, LANES), F32)], grid=(B,),
               in_specs=[blk, blk, one], out_specs=[blk, one], semantics=("arbitrary",))(dc, f_logit, b_f)


def _head_mask(lane, hh):
    return (lane < HEAD_DIM) if hh == 0 else (lane >= HEAD_DIM)


HEAD_GROUP = 3
FWD_HEAD_GROUP = 6


def _g_col_spec(rows, nblk_rows, cb, G):
    return pl.BlockSpec((rows, G * LANES), lambda b, p, i: (b * nblk_rows + i, cb // G + p))


def _g_kv_spec(rows, cb, G):
    return pl.BlockSpec((rows, G * LANES), lambda b, p, i: (b, cb // G + p))


def _g_stat_col_spec(tq, G):
    return pl.BlockSpec((1, 2 * G, tq, 1), lambda b, p, i: (b, p, i, 0))


def _g_stat_row_spec(S, G):
    return pl.BlockSpec((1, 2 * G, 1, S), lambda b, p, i: (b, p, 0, 0))


def _lanes(g):
    return slice(g * LANES, (g + 1) * LANES)


def _streams(x_ref, G, scale=None):
    rows = x_ref.shape[0]
    lane = lax.broadcasted_iota(jnp.int32, (rows, LANES), 1)
    out = []
    for g in range(G):
        x = x_ref[:, _lanes(g)]
        if scale is not None:
            x = x * jnp.asarray(scale, x.dtype)
        for hh in range(2):
            out.append(jnp.where(_head_mask(lane, hh), x, jnp.zeros_like(x)))
    return lane, out


def _wide(stat, width):
    return jnp.tile(stat, (1, width // LANES))


def _fold_lanes(v):
    out = v[:, :LANES]
    for j in range(1, v.shape[1] // LANES):
        out = out + v[:, j * LANES:(j + 1) * LANES]
    return out


def _kv_blocks(ref, ks, tk, G):
    return [ref[pl.ds(ks, tk), _lanes(g)] for g in range(G)]


def _sweep(i, block):
    def step(kb, c):
        block(kb, False)
        return c
    lax.fori_loop(0, i, step, 0)
    block(i, True)


def _fox_fwd_g(qa, ka, va, cr, *, name, B, S, P, q_cb, k_cb, v_cb, G=HEAD_GROUP):
    tq = tk = min(ATT_TILE, S)
    nq = S // tq
    NS = 2 * G

    def body(q_ref, k_ref, v_ref, cr_ref, o_ref, lse_ref, acc_ref, m_ref, l_ref):
        i = pl.program_id(2)
        lane, qh = _streams(q_ref, G, SCALE)
        on_or_below = (lax.broadcasted_iota(jnp.int32, (tq, tk), 1) <= lax.broadcasted_iota(jnp.int32, (tq, tk), 0))
        m_ref[...] = jnp.full(m_ref.shape, NEG, F32)
        l_ref[...] = jnp.zeros(l_ref.shape, F32)
        acc_ref[...] = jnp.zeros(acc_ref.shape, F32)

        def block(kb, diag):
            ks = pl.multiple_of(kb * tk, tk)
            kblk = _kv_blocks(k_ref, ks, tk, G)
            vblk = _kv_blocks(v_ref, ks, tk, G)
            ss = [lax.dot_general(qh[st], kblk[st // 2], NT, preferred_element_type=F32) for st in range(NS)]
            ps = []
            for st in range(NS):
                s = ss[st] - cr_ref[0, st, :, pl.ds(ks, tk)]
                if diag:
                    s = jnp.where(on_or_below, s, NEG)
                m = m_ref[st]
                m_new = jnp.maximum(m, jnp.max(s, axis=-1, keepdims=True))
                alpha = jnp.exp(m - m_new)
                p = jnp.exp(s - _wide(m_new, tk))
                m_ref[st] = m_new
                l_ref[st] = alpha * l_ref[st] + _fold_lanes(p)
                ps.append((alpha, p.astype(BF16)))
            pvs = [jnp.dot(ps[st][1], vblk[st // 2], preferred_element_type=F32) for st in range(NS)]
            for st in range(NS):
                acc_ref[st] = ps[st][0] * acc_ref[st] + pvs[st]

        _sweep(i, block)
        ls = [jnp.sum(l_ref[st], axis=-1, keepdims=True) for st in range(NS)]
        for st in range(NS):
            lse_ref[0, st] = jnp.max(m_ref[st], axis=-1, keepdims=True) + jnp.log(ls[st])
        for g in range(G):
            o_ref[:, _lanes(g)] = jnp.where(lane < HEAD_DIM, acc_ref[2 * g] / ls[2 * g],
                                            acc_ref[2 * g + 1] / ls[2 * g + 1]).astype(BF16)

    return _pc(body, name=name, out_shape=[_sds((B * S, P * LANES), BF16), _sds((B, 2 * P, S, 1), F32)],
               grid=(B, P // G, nq),
               in_specs=[_g_col_spec(tq, nq, q_cb, G), _g_kv_spec(S, k_cb, G), _g_kv_spec(S, v_cb, G),
                         _g_stat_row_spec(S, G)],
               out_specs=[_g_col_spec(tq, nq, 0, G), _g_stat_col_spec(tq, G)],
               scratch_shapes=[pltpu.VMEM((NS, tq, LANES), F32)] * 3,
               semantics=("arbitrary", "arbitrary", "arbitrary"), vmem=VMEM_BIG)(qa, ka, va, cr)


def _fox_bwd_g(qa, ka, va, doa, lse, cr, *, name, B, S, P, q_cb, k_cb, v_cb, do_cb, G=HEAD_GROUP):
    tq = tk = min(ATT_TILE, S)
    nq = S // tq
    NS = 2 * G

    def body(q_ref, k_ref, v_ref, do_ref, lse_ref, cr_ref, dq_ref, dk_ref, dv_ref, dcs_ref, dqa_ref, delta_ref, lse_s,
             p_buf, dp_buf):
        i = pl.program_id(2)

        @pl.when(i == 0)
        def _():
            dk_ref[...] = jnp.zeros_like(dk_ref)
            dv_ref[...] = jnp.zeros_like(dv_ref)
            dcs_ref[...] = jnp.zeros_like(dcs_ref)

        lane, qh = _streams(q_ref, G, SCALE)
        _, doh = _streams(do_ref, G)
        on_or_below = (lax.broadcasted_iota(jnp.int32, (tq, tk), 1) <= lax.broadcasted_iota(jnp.int32, (tq, tk), 0))
        delta_ref[...] = jnp.zeros(delta_ref.shape, F32)
        dqa_ref[...] = jnp.zeros(dqa_ref.shape, F32)
        for st in range(NS):
            lse_s[st] = jnp.broadcast_to(lse_ref[0, st], (tq, LANES))

        def probs(kb, diag):
            ks = pl.multiple_of(kb * tk, tk)
            kblk = _kv_blocks(k_ref, ks, tk, G)
            vblk = _kv_blocks(v_ref, ks, tk, G)
            ss = [lax.dot_general(qh[st], kblk[st // 2], NT, preferred_element_type=F32) for st in range(NS)]
            dps = [lax.dot_general(doh[st], vblk[st // 2], NT, preferred_element_type=F32) for st in range(NS)]
            ps = []
            for st in range(NS):
                s = ss[st] - cr_ref[0, st, :, pl.ds(ks, tk)]
                if diag:
                    s = jnp.where(on_or_below, s, NEG)
                ps.append(jnp.exp(s - _wide(lse_s[st], tk)))
            return ks, kblk, ps, dps

        def delta_block(kb, diag):
            _, _, ps, dps = probs(kb, diag)
            for st in range(NS):
                delta_ref[st] += _fold_lanes(ps[st] * dps[st])
                p_buf[st, kb] = ps[st]
                dp_buf[st, kb] = dps[st]

        _sweep(i, delta_block)
        for st in range(NS):
            delta_ref[st] = jnp.broadcast_to(jnp.sum(delta_ref[st], axis=-1, keepdims=True), (tq, LANES))

        def grad_block(kb, diag):
            ks = pl.multiple_of(kb * tk, tk)
            kblk = _kv_blocks(k_ref, ks, tk, G)
            rows = pl.ds(ks, tk)
            dsb, pb = [], []
            for st in range(NS):
                p = p_buf[st, kb]
                ds = p * (dp_buf[st, kb] - _wide(delta_ref[st], tk))
                dcs_ref[0, st, :, rows] -= jnp.sum(ds, axis=0, keepdims=True)
                dsb.append(ds.astype(BF16))
                pb.append(p.astype(BF16))
            dks = [lax.dot_general(dsb[st], qh[st], TN, preferred_element_type=F32) for st in range(NS)]
            dvs = [lax.dot_general(pb[st], doh[st], TN, preferred_element_type=F32) for st in range(NS)]
            dqs = [jnp.dot(dsb[st], kblk[st // 2], preferred_element_type=F32) for st in range(NS)]
            for g in range(G):
                dk_ref[rows, _lanes(g)] += dks[2 * g] + dks[2 * g + 1]
                dv_ref[rows, _lanes(g)] += dvs[2 * g] + dvs[2 * g + 1]
            for st in range(NS):
                dqa_ref[st] += dqs[st]

        _sweep(i, grad_block)
        for g in range(G):
            dq_ref[:, _lanes(g)] = (jnp.where(lane < HEAD_DIM, dqa_ref[2 * g], dqa_ref[2 * g + 1]) * SCALE).astype(BF16)

    return _pc(body, name=name,
               out_shape=[_sds((B * S, P * LANES), BF16), _sds((B * S, P * LANES), F32), _sds((B * S, P * LANES), F32),
                          _sds((B, 2 * P, 1, S), F32)],
               grid=(B, P // G, nq),
               in_specs=[_g_col_spec(tq, nq, q_cb, G), _g_kv_spec(S, k_cb, G), _g_kv_spec(S, v_cb, G),
                         _g_col_spec(tq, nq, do_cb, G), _g_stat_col_spec(tq, G), _g_stat_row_spec(S, G)],
               out_specs=[_g_col_spec(tq, nq, 0, G), _g_kv_spec(S, 0, G), _g_kv_spec(S, 0, G), _g_stat_row_spec(S, G)],
               scratch_shapes=[pltpu.VMEM((NS, tq, LANES), F32)] * 3 + [pltpu.VMEM((NS, nq, tq, tk), F32)] * 2,
               semantics=("arbitrary", "arbitrary", "arbitrary"), vmem=VMEM_BIG)(qa, ka, va, doa, lse, cr)


def _sb_logs_z(z):
    nz = -z
    lm = jnp.minimum(nz, 0.0) - jnp.log(1.0 + jnp.exp(jnp.minimum(z, nz)))
    return lm + z, lm


def _sb_fwd_g(qa, ka, va, *, name, B, S, P, q_cb, k_cb, v_cb, G=HEAD_GROUP):
    tq = tk = min(ATT_TILE, S)
    nq = S // tq
    NS = 2 * G

    def body(q_ref, k_ref, v_ref, o_ref, rt_ref, acc_ref, run_ref):
        i = pl.program_id(2)
        lane, qh = _streams(q_ref, G, SCALE)
        t_r = lax.broadcasted_iota(jnp.int32, (tk, tk), 0)
        t_c = lax.broadcasted_iota(jnp.int32, (tk, tk), 1)
        after = (t_r > t_c).astype(BF16)
        below = t_c < t_r
        acc_ref[...] = jnp.zeros(acc_ref.shape, F32)
        run_ref[...] = jnp.zeros(run_ref.shape, F32)

        def block(kb, diag):
            ks = pl.multiple_of(kb * tk, tk)
            kblk = _kv_blocks(k_ref, ks, tk, G)
            vblk = _kv_blocks(v_ref, ks, tk, G)
            zs = [lax.dot_general(qh[st], kblk[st // 2], NT, preferred_element_type=F32) for st in range(NS)]
            lss, parts = [], []
            for st in range(NS):
                ls, lm = _sb_logs_z(zs[st])
                if diag:
                    lm = jnp.where(below, lm, 0.0)
                lss.append(ls + _wide(run_ref[st], tk))
                run_ref[st] += jnp.sum(lm, axis=-1, keepdims=True)
                parts.append(_split2(lm))
            sufs = [jnp.dot(parts[st][0], after, preferred_element_type=F32)
                    + jnp.dot(parts[st][1], after, preferred_element_type=F32) for st in range(NS)]
            ab = []
            for st in range(NS):
                a = jnp.exp(lss[st] + sufs[st])
                if diag:
                    a = jnp.where(below, a, 0.0)
                ab.append(a.astype(BF16))
            pvs = [jnp.dot(ab[st], vblk[st // 2], preferred_element_type=F32) for st in range(NS)]
            for st in range(NS):
                acc_ref[st] += pvs[st]

        block(i, True)

        def step(jj, c):
            block(i - 1 - jj, False)
            return c

        lax.fori_loop(0, i, step, 0)
        for st in range(NS):
            rt_ref[0, st] = jnp.max(run_ref[st], axis=-1, keepdims=True)
        for g in range(G):
            o_ref[:, _lanes(g)] = jnp.where(lane < HEAD_DIM, acc_ref[2 * g], acc_ref[2 * g + 1]).astype(BF16)

    return _pc(body, name=name, out_shape=[_sds((B * S, P * LANES), BF16), _sds((B, 2 * P, S, 1), F32)],
               grid=(B, P // G, nq),
               in_specs=[_g_col_spec(tq, nq, q_cb, G), _g_kv_spec(S, k_cb, G), _g_kv_spec(S, v_cb, G)],
               out_specs=[_g_col_spec(tq, nq, 0, G), _g_stat_col_spec(tq, G)],
               scratch_shapes=[pltpu.VMEM((NS, tq, LANES), F32)] * 2,
               semantics=("arbitrary", "arbitrary", "arbitrary"), vmem=VMEM_BIG)(qa, ka, va)


def _sb_bwd_g(qa, ka, va, doa, rt, *, name, B, S, P, q_cb, k_cb, v_cb, do_cb, G=HEAD_GROUP):
    tq = tk = min(ATT_TILE, S)
    nq = S // tq
    NS = 2 * G

    def body(q_ref, k_ref, v_ref, do_ref, rt_ref, dq_ref, dk_ref, dv_ref, dqa_ref, pl_ref, pg_ref):
        i = pl.program_id(2)

        @pl.when(i == 0)
        def _():
            dk_ref[...] = jnp.zeros_like(dk_ref)
            dv_ref[...] = jnp.zeros_like(dv_ref)

        lane, qh = _streams(q_ref, G, SCALE)
        _, doh = _streams(do_ref, G)
        t_r = lax.broadcasted_iota(jnp.int32, (tk, tk), 0)
        t_c = lax.broadcasted_iota(jnp.int32, (tk, tk), 1)
        upto = (t_r <= t_c).astype(BF16)
        before = (t_r < t_c).astype(BF16)
        below = t_c < t_r
        dqa_ref[...] = jnp.zeros(dqa_ref.shape, F32)
        pg_ref[...] = jnp.zeros(pg_ref.shape, F32)
        for st in range(NS):
            pl_ref[st] = jnp.broadcast_to(rt_ref[0, st], (tq, LANES))

        def block(kb, diag):
            ks = pl.multiple_of(kb * tk, tk)
            rows = pl.ds(ks, tk)
            kblk = _kv_blocks(k_ref, ks, tk, G)
            vblk = _kv_blocks(v_ref, ks, tk, G)
            zs = [lax.dot_general(qh[st], kblk[st // 2], NT, preferred_element_type=F32) for st in range(NS)]
            das = [lax.dot_general(doh[st], vblk[st // 2], NT, preferred_element_type=F32) for st in range(NS)]
            lss, parts = [], []
            for st in range(NS):
                ls, lm = _sb_logs_z(zs[st])
                if diag:
                    lm = jnp.where(below, lm, 0.0)
                lss.append((ls, ls + _wide(pl_ref[st], tk)))
                pl_ref[st] -= jnp.sum(lm, axis=-1, keepdims=True)
                parts.append(_split2(lm))
            pins = [jnp.dot(parts[st][0], upto, preferred_element_type=F32)
                    + jnp.dot(parts[st][1], upto, preferred_element_type=F32) for st in range(NS)]
            gms, ab, gparts = [], [], []
            for st in range(NS):
                a = jnp.exp(lss[st][1] - pins[st])
                if diag:
                    a = jnp.where(below, a, 0.0)
                gm = a * das[st]
                gms.append(gm)
                ab.append(a.astype(BF16))
                gparts.append(gm.astype(BF16))
            pgs = [jnp.dot(gparts[st], before, preferred_element_type=F32) for st in range(NS)]
            dzb = []
            for st in range(NS):
                gm = gms[st]
                dz = gm - jnp.exp(lss[st][0]) * (gm + (pgs[st] + _wide(pg_ref[st], tk)))
                if diag:
                    dz = jnp.where(below, dz, 0.0)
                pg_ref[st] += jnp.sum(gm, axis=-1, keepdims=True)
                dzb.append(dz.astype(BF16))
            dks = [lax.dot_general(dzb[st], qh[st], TN, preferred_element_type=F32) for st in range(NS)]
            dvs = [lax.dot_general(ab[st], doh[st], TN, preferred_element_type=F32) for st in range(NS)]
            dqs = [jnp.dot(dzb[st], kblk[st // 2], preferred_element_type=F32) for st in range(NS)]
            for g in range(G):
                dk_ref[rows, _lanes(g)] += dks[2 * g] + dks[2 * g + 1]
                dv_ref[rows, _lanes(g)] += dvs[2 * g] + dvs[2 * g + 1]
            for st in range(NS):
                dqa_ref[st] += dqs[st]

        _sweep(i, block)
        for g in range(G):
            dq_ref[:, _lanes(g)] = (jnp.where(lane < HEAD_DIM, dqa_ref[2 * g], dqa_ref[2 * g + 1]) * SCALE).astype(BF16)

    return _pc(body, name=name,
               out_shape=[_sds((B * S, P * LANES), BF16), _sds((B * S, P * LANES), F32), _sds((B * S, P * LANES), F32)],
               grid=(B, P // G, nq),
               in_specs=[_g_col_spec(tq, nq, q_cb, G), _g_kv_spec(S, k_cb, G), _g_kv_spec(S, v_cb, G),
                         _g_col_spec(tq, nq, do_cb, G), _g_stat_col_spec(tq, G)],
               out_specs=[_g_col_spec(tq, nq, 0, G), _g_kv_spec(S, 0, G), _g_kv_spec(S, 0, G)],
               scratch_shapes=[pltpu.VMEM((NS, tq, LANES), F32)] * 3,
               semantics=("arbitrary", "arbitrary", "arbitrary"), vmem=VMEM_BIG)(qa, ka, va, doa, rt)


MEM_GROUP = N_MEM_HEADS // 2


def _mem_fwd(qa, kva, *, name, B, S, NM, q_cb):
    G = MEM_GROUP
    NS = 2 * G
    tq = min(MEM_Q_TILE, S)
    nq = S // tq

    def body(q_ref, k_ref, v_ref, o_ref, lse_ref):
        lane, qh = _streams(q_ref, G, SCALE)
        kblk = [k_ref[:, _lanes(g)] for g in range(G)]
        vblk = [v_ref[:, _lanes(g)] for g in range(G)]
        ss = [lax.dot_general(qh[st], kblk[st // 2], NT, preferred_element_type=F32) for st in range(NS)]
        pb, ls = [], []
        for st in range(NS):
            m = jnp.max(ss[st], axis=-1, keepdims=True)
            p = jnp.exp(ss[st] - m)
            l = jnp.sum(p, axis=-1, keepdims=True)
            lse_ref[0, st] = m + jnp.log(l)
            pb.append(p.astype(BF16))
            ls.append(l)
        pvs = [jnp.dot(pb[st], vblk[st // 2], preferred_element_type=F32) for st in range(NS)]
        for g in range(G):
            o_ref[:, _lanes(g)] = jnp.where(lane < HEAD_DIM, pvs[2 * g] / ls[2 * g],
                                            pvs[2 * g + 1] / ls[2 * g + 1]).astype(BF16)

    return _pc(body, name=name, out_shape=[_sds((B * S, G * LANES), BF16), _sds((B, NS, S, 1), F32)],
               grid=(B, 1, nq),
               in_specs=[_g_col_spec(tq, nq, q_cb, G), _g_kv_spec(NM, 0, G), _g_kv_spec(NM, G, G)],
               out_specs=[_g_col_spec(tq, nq, 0, G), _g_stat_col_spec(tq, G)],
               semantics=("arbitrary", "arbitrary", "arbitrary"), vmem=VMEM_BIG)(qa, kva, kva)


def _mem_bwd(qa, kva, doa, oa, lse, *, name, B, S, NM, q_cb, do_cb):
    G = MEM_GROUP
    NS = 2 * G
    tq = min(MEM_Q_TILE, S)
    nq = S // tq

    def body(q_ref, k_ref, v_ref, do_ref, o_ref, lse_ref, dq_ref, dk_ref, dv_ref):
        @pl.when(pl.program_id(2) == 0)
        def _():
            dk_ref[...] = jnp.zeros_like(dk_ref)
            dv_ref[...] = jnp.zeros_like(dv_ref)

        lane, qh = _streams(q_ref, G, SCALE)
        _, doh = _streams(do_ref, G)
        kblk = [k_ref[:, _lanes(g)] for g in range(G)]
        vblk = [v_ref[:, _lanes(g)] for g in range(G)]
        prod = [do_ref[:, _lanes(g)].astype(F32) * o_ref[:, _lanes(g)].astype(F32) for g in range(G)]
        ss = [lax.dot_general(qh[st], kblk[st // 2], NT, preferred_element_type=F32) for st in range(NS)]
        dps = [lax.dot_general(doh[st], vblk[st // 2], NT, preferred_element_type=F32) for st in range(NS)]
        dsb, pb = [], []
        for st in range(NS):
            delta = jnp.sum(jnp.where(_head_mask(lane, st % 2), prod[st // 2], 0.0), axis=-1, keepdims=True)
            p = jnp.exp(ss[st] - lse_ref[0, st])
            dsb.append((p * (dps[st] - delta)).astype(BF16))
            pb.append(p.astype(BF16))
        dks = [lax.dot_general(dsb[st], qh[st], TN, preferred_element_type=F32) for st in range(NS)]
        dvs = [lax.dot_general(pb[st], doh[st], TN, preferred_element_type=F32) for st in range(NS)]
        dqs = [jnp.dot(dsb[st], kblk[st // 2], preferred_element_type=F32) for st in range(NS)]
        for g in range(G):
            dk_ref[:, _lanes(g)] += dks[2 * g] + dks[2 * g + 1]
            dv_ref[:, _lanes(g)] += dvs[2 * g] + dvs[2 * g + 1]
            dq_ref[:, _lanes(g)] = (jnp.where(lane < HEAD_DIM, dqs[2 * g], dqs[2 * g + 1]) * SCALE).astype(BF16)

    return _pc(body, name=name,
               out_shape=[_sds((B * S, G * LANES), BF16), _sds((B * NM, G * LANES), F32), _sds((B * NM, G * LANES), F32)],
               grid=(B, 1, nq),
               in_specs=[_g_col_spec(tq, nq, q_cb, G), _g_kv_spec(NM, 0, G), _g_kv_spec(NM, G, G),
                         _g_col_spec(tq, nq, do_cb, G), _g_col_spec(tq, nq, 0, G), _g_stat_col_spec(tq, G)],
               out_specs=[_g_col_spec(tq, nq, 0, G), _g_kv_spec(NM, 0, G), _g_kv_spec(NM, 0, G)],
               semantics=("arbitrary", "arbitrary", "arbitrary"), vmem=VMEM_BIG)(qa, kva, kva, doa, oa, lse)


def _sigmoid(v):
    return 0.5 * jnp.tanh(0.5 * v) + 0.5


def _shift_rows(cur, halo_ref, first, rows_idx, k):
    out = pltpu.roll(cur, k, 0)
    top = out[0:8, :]
    for r in range(k):
        hr = halo_ref.shape[0] - k + r
        edge = jnp.where(first, 0.0, halo_ref[hr:hr + 1, :])
        top = jnp.where(rows_idx[0:8, :] == r, edge, top)
    return jnp.concatenate([top, out[8:, :]], axis=0)


def _shift_rows_up(cur, halo_ref, last, rows_idx, k, ts):
    out = pltpu.roll(cur, ts - k, 0)
    bottom = out[ts - 8:, :]
    for r in range(k):
        edge = jnp.where(last, 0.0, halo_ref[r:r + 1, :])
        bottom = jnp.where(rows_idx[0:8, :] == 8 - k + r, edge, bottom)
    return jnp.concatenate([out[:ts - 8, :], bottom], axis=0)


def _ffn_up_gate(x, g, w, cw, cb, *, name, S):
    T, D = x.shape
    F = w.shape[1] // 2
    tm = min(1024, S)
    tn = 256
    nj = F // tn
    tiles_per_seq = S // tm
    halo = 16

    def body(x_ref, xh_ref, g_ref, wg_ref, wv_ref, cwg_ref, cwv_ref, cbg_ref, cbv_ref,
             uc_ref, ub_ref, a_ref, hout_ref, h_ref, hh_ref, eg_ref, ev_ref):
        first = lax.rem(pl.program_id(0), tiles_per_seq) == 0

        @pl.when(pl.program_id(1) == 0)
        def _():
            def norm(v):
                r = lax.rsqrt(jnp.mean(v * v, axis=-1, keepdims=True) + EPS)
                return ((v * r) * g_ref[...]).astype(BF16)
            h = norm(x_ref[...])
            h_ref[...] = h
            hout_ref[...] = h
            hh_ref[...] = norm(xh_ref[...])

        h = h_ref[...]
        rows_idx = lax.broadcasted_iota(jnp.int32, (tm, tn), 0)
        uc = []
        for half, (w_ref, cw_ref, cb_ref, e_ref) in enumerate(((wg_ref, cwg_ref, cbg_ref, eg_ref),
                                                               (wv_ref, cwv_ref, cbv_ref, ev_ref))):
            acc = jnp.dot(h, w_ref[...], preferred_element_type=F32)
            e_ref[...] = jnp.dot(hh_ref[...], w_ref[...], preferred_element_type=F32)
            ub_ref[half] = acc.astype(BF16)
            m1 = _shift_rows(acc, e_ref, first, rows_idx, 1)
            m2 = _shift_rows(acc, e_ref, first, rows_idx, 2)
            uc.append(cb_ref[...] + cw_ref[0:1, :] * m2 + cw_ref[1:2, :] * m1 + cw_ref[2:3, :] * acc)
            uc_ref[half] = uc[half]
        a_ref[...] = (uc[0] * _sigmoid(uc[0]) * uc[1]).astype(BF16)

    in_specs = [pl.BlockSpec((tm, D), lambda i, j: (i, 0)),
                pl.BlockSpec((halo, D), lambda i, j: (jnp.maximum(i * (tm // halo) - 1, 0), 0)),
                pl.BlockSpec((1, D), lambda i, j: (0, 0)),
                pl.BlockSpec((D, tn), lambda i, j: (0, j)), pl.BlockSpec((D, tn), lambda i, j: (0, j + nj)),
                pl.BlockSpec((3, tn), lambda i, j: (0, j)), pl.BlockSpec((3, tn), lambda i, j: (0, j + nj)),
                pl.BlockSpec((1, tn), lambda i, j: (0, j)), pl.BlockSpec((1, tn), lambda i, j: (0, j + nj))]
    return _pc(body, name=name,
               out_shape=[_sds((2, T, F), F32), _sds((2, T, F), BF16), _sds((T, F), BF16), _sds((T, D), BF16)],
               grid=(T // tm, nj), in_specs=in_specs,
               out_specs=[pl.BlockSpec((2, tm, tn), lambda i, j: (0, i, j)), pl.BlockSpec((2, tm, tn), lambda i, j: (0, i, j)),
                          pl.BlockSpec((tm, tn), lambda i, j: (i, j)), pl.BlockSpec((tm, D), lambda i, j: (i, 0))],
               scratch_shapes=[pltpu.VMEM((tm, D), BF16), pltpu.VMEM((halo, D), BF16),
                               pltpu.VMEM((halo, tn), F32), pltpu.VMEM((halo, tn), F32)],
               semantics=("arbitrary", "arbitrary"), vmem=VMEM_BIG)(x, x, g.reshape(1, D), w, w, cw, cw, cb, cb)


def _conv_gate_bwd(da, uc, ub, cw, *, name, B, S):
    F = uc.shape[2]
    tf = F // 2
    ts = min(512, S)
    ns, nf = S // ts, F // tf

    def body(da_ref, uc_ref, ub_ref, wg_ref, wv_ref, dug_ref, duv_ref, pg_ref, pv_ref, nxt_g, nxt_v):
        last = pl.program_id(2) == 0

        @pl.when(jnp.logical_and(pl.program_id(1) == 0, last))
        def _():
            pg_ref[...] = jnp.zeros_like(pg_ref)
            pv_ref[...] = jnp.zeros_like(pv_ref)

        rows_idx = lax.broadcasted_iota(jnp.int32, (ts, tf), 0)
        ucg, ucv = uc_ref[0], uc_ref[1]
        sg = _sigmoid(ucg)
        dav = da_ref[...]
        d_v = dav * (ucg * sg)
        d_g = dav * ucv * (sg * (1.0 + ucg * (1.0 - sg)))
        for half, (o_ref, p_ref, d, w_ref, nxt) in enumerate(((dug_ref, pg_ref, d_g, wg_ref, nxt_g),
                                                               (duv_ref, pv_ref, d_v, wv_ref, nxt_v))):
            p1 = _shift_rows_up(d, nxt, last, rows_idx, 1, ts)
            p2 = _shift_rows_up(d, nxt, last, rows_idx, 2, ts)
            o_ref[...] = (w_ref[2:3, :] * d + w_ref[1:2, :] * p1 + w_ref[0:1, :] * p2).astype(BF16)
            nxt[...] = d[0:8, :]
            uh = ub_ref[half].astype(F32)
            for k, dk in enumerate((p2, p1, d)):
                p_ref[k:k + 1, :] += jnp.sum(dk * uh, axis=0, keepdims=True)
            p_ref[3:4, :] += jnp.sum(d, axis=0, keepdims=True)

    row = pl.BlockSpec((ts, tf), lambda j, b, r: (b * ns + ns - 1 - r, j))
    both = pl.BlockSpec((2, ts, tf), lambda j, b, r: (0, b * ns + ns - 1 - r, j))
    par = pl.BlockSpec((8, tf), lambda j, b, r: (0, j))
    return _pc(body, name=name,
               out_shape=[_sds((B * S, F), BF16), _sds((B * S, F), BF16), _sds((8, F), F32), _sds((8, F), F32)],
               grid=(nf, B, ns),
               in_specs=[row, both, both, pl.BlockSpec((3, tf), lambda j, b, r: (0, j)),
                         pl.BlockSpec((3, tf), lambda j, b, r: (0, j + nf))],
               out_specs=[row, row, par, par],
               scratch_shapes=[pltpu.VMEM((8, tf), F32), pltpu.VMEM((8, tf), F32)],
               semantics=("arbitrary", "arbitrary", "arbitrary"), vmem=VMEM_BIG)(da, uc, ub, cw, cw)


def _adamw(w, g, m, v, *, name):
    rows, cols = w.shape
    tr = rows
    while tr * cols * 4 > 2 ** 20 and tr % 16 == 0:
        tr //= 2

    def body(w_ref, g_ref, m_ref, v_ref, d_ref, nm_ref, nv_ref):
        gv = g_ref[...]
        m_new = ADAM_B1 * m_ref[...] + (1.0 - ADAM_B1) * gv
        v_new = ADAM_B2 * v_ref[...] + (1.0 - ADAM_B2) * (gv * gv)
        m_hat = m_new / (1.0 - ADAM_B1 ** ADAM_STEP)
        v_hat = v_new / (1.0 - ADAM_B2 ** ADAM_STEP)
        d_ref[...] = -ADAM_LR * (m_hat / (jnp.sqrt(v_hat) + ADAM_EPS) + ADAM_WD * w_ref[...])
        nm_ref[...] = m_new
        nv_ref[...] = v_new

    blk = pl.BlockSpec((tr, cols), lambda i: (i, 0))
    return _pc(body, name=name, out_shape=[_sds((rows, cols), F32)] * 3, grid=(rows // tr,),
               in_specs=[blk] * 4, out_specs=[blk] * 3, semantics=("arbitrary",))(w, g, m, v)


def _my_pos():
    return lax.axis_index("x"), lax.axis_index("y"), lax.axis_index("c")


_HBM = pl.BlockSpec(memory_space=pltpu.HBM)
_SEM = pl.BlockSpec(memory_space=pltpu.SEMAPHORE)
_EFFECT = pltpu.SideEffectType.DATAFLOW_SIDE_EFFECTING


def _peers():
    x, y, c = _my_pos()
    out = []
    for k in range(1, N_DEV):
        px, py, pc = x ^ ((k >> 2) & 1), y ^ ((k >> 1) & 1), c ^ (k & 1)
        out.append(((px, py, pc), 4 * px + 2 * py + pc))
    return out


def _scatter_start(srcs, slot_of, *, name):
    n = len(srcs)
    lands = [lax.empty((N_DEV,) + slot_of(s, 0, shape_only=True), s.dtype) for s in srcs]

    def body(*refs):
        src_refs, land_refs = refs[:n], refs[n:2 * n]
        send_sems, recv_sems = refs[2 * n], refs[2 * n + 1]
        token = refs[-1]
        x, y, c = _my_pos()
        me = 4 * x + 2 * y + c
        for a in range(n):
            for k, (peer, peer_idx) in enumerate(_peers()):
                pltpu.make_async_remote_copy(
                    src_ref=slot_of(src_refs[a], peer_idx), dst_ref=land_refs[a].at[me],
                    send_sem=send_sems.at[a * 7 + k], recv_sem=recv_sems.at[a * 7 + k],
                    device_id=peer, device_id_type=MESH).start()
        token[...] = jnp.zeros_like(token)

    hbm = lambda a: pltpu.HBM(a.shape, a.dtype)
    args = [pltpu.with_memory_space_constraint(a, pltpu.HBM) for a in list(srcs) + lands]
    outs = pl.pallas_call(
        body, name=name,
        out_shape=(pltpu.SemaphoreType.DMA((7 * n,)), pltpu.SemaphoreType.DMA((7 * n,)),
                   *[hbm(a) for a in srcs], *[hbm(a) for a in lands], _sds((8, LANES), F32)),
        in_specs=[_HBM] * (2 * n),
        out_specs=(_SEM, _SEM, *([_HBM] * (2 * n)), pl.BlockSpec(memory_space=pltpu.VMEM)),
        input_output_aliases={a: 2 + a for a in range(2 * n)},
        compiler_params=pltpu.CompilerParams(has_side_effects=_EFFECT))(*args)
    return outs[0], outs[1], list(outs[2:2 + n]), list(outs[2 + n:2 + 2 * n]), outs[-1]


def _scatter_wait(send_sems, recv_sems, srcs, lands, slot_of, after, *, name, first=0):
    n = len(srcs)

    def body(*refs):
        src_refs, land_refs = refs[:n], refs[n:2 * n]
        ssem, rsem = refs[2 * n], refs[2 * n + 1]
        x, y, c = _my_pos()
        me = 4 * x + 2 * y + c
        for a in range(n):
            for k, (peer, peer_idx) in enumerate(_peers()):
                cp = pltpu.make_async_remote_copy(
                    src_ref=slot_of(src_refs[a], peer_idx), dst_ref=land_refs[a].at[me],
                    send_sem=ssem.at[(first + a) * 7 + k], recv_sem=rsem.at[(first + a) * 7 + k],
                    device_id=peer, device_id_type=MESH)
                cp.wait_send()
                cp.wait_recv()

    hbm = lambda a: pltpu.HBM(a.shape, a.dtype)
    outs = pl.pallas_call(
        body, name=name, out_shape=tuple(hbm(a) for a in list(srcs) + list(lands)),
        in_specs=[_HBM] * (2 * n) + [_SEM, _SEM, pl.BlockSpec(memory_space=pl.ANY)],
        out_specs=tuple([_HBM] * (2 * n)), input_output_aliases={a: a for a in range(2 * n)},
        compiler_params=pltpu.CompilerParams(has_side_effects=_EFFECT))(*srcs, *lands, send_sems, recv_sems, after)
    return list(outs[:n]), list(outs[n:])


def _whole(a, peer_idx, shape_only=False):
    return a.shape if shape_only else a


def _slot(a, peer_idx, shape_only=False):
    return a.shape[1:] if shape_only else a.at[peer_idx]


def _sum_slots(a, *, name, tr=None):
    rows, cols = a.shape[1], a.shape[2]
    if tr is None:
        tr = rows
        while N_DEV * tr * cols * a.dtype.itemsize > 3 * 2 ** 20 and tr % 32 == 0:
            tr //= 2

    def body(a_ref, o_ref):
        acc = a_ref[0].astype(F32)
        for j in range(1, N_DEV):
            acc = acc + a_ref[j].astype(F32)
        o_ref[...] = acc

    return _pc(body, name=name, out_shape=_sds((rows, cols), F32), grid=(rows // tr,),
               in_specs=[pl.BlockSpec((N_DEV, tr, cols), lambda i: (0, i, 0))],
               out_specs=pl.BlockSpec((tr, cols), lambda i: (i, 0)), semantics=("arbitrary",), vmem=VMEM_BIG)(a)


def _to_slots(full, kind):
    if kind == "rows2":
        r, c = full.shape
        return full.reshape(N_DEV, r // N_DEV, c)
    if kind == "cols2":
        r, c = full.shape
        return full.reshape(r, N_DEV, c // N_DEV).transpose(1, 0, 2)
    if kind == "rows3":
        l, r, c = full.shape
        return full.reshape(l, N_DEV, r // N_DEV, c).transpose(1, 0, 2, 3)
    if kind == "cols3":
        l, r, c = full.shape
        return full.reshape(l, r, N_DEV, c // N_DEV).transpose(2, 0, 1, 3)
    raise ValueError(kind)


def _from_slots(slots, kind):
    if kind == "rows2":
        _, r, c = slots.shape
        return slots.reshape(N_DEV * r, c)
    if kind == "cols2":
        _, r, c = slots.shape
        return slots.transpose(1, 0, 2).reshape(r, N_DEV * c)
    if kind == "rows3":
        _, l, r, c = slots.shape
        return slots.transpose(1, 0, 2, 3).reshape(l, N_DEV * r, c)
    if kind == "cols3":
        _, l, r, c = slots.shape
        return slots.transpose(1, 2, 0, 3).reshape(l, r, N_DEV * c)
    raise ValueError(kind)


BIG = (("w_in_a", "rows2"), ("w_in_b", "rows2"), ("w_kv", "cols2"), ("w_memkv", "rows3"),
       ("w_out", "rows3"), ("w_up", "cols3"), ("w_down", "rows3"))


def _round_up(n, m):
    return -(-n // m) * m


def _pad_rows(a, rows, axis):
    pad = [(0, 0)] * a.ndim
    pad[axis] = (0, rows - a.shape[axis])
    return jnp.pad(a, pad)


def kernel(x, mem, ln_mix_g, w_in_a, b_f_a, w_in_b, ln_kv_g, w_kv, ln_mem_g, w_memkv, w_out, ln_ffn_g, w_up, conv_w, conv_b, w_down, final_g, loss_target, m_ln_mix_g, m_w_in_a, m_b_f_a, m_w_in_b, m_ln_kv_g, m_w_kv, m_ln_mem_g, m_w_memkv, m_w_out, m_ln_ffn_g, m_w_up, m_conv_w, m_conv_b, m_w_down, m_final_g, v_ln_mix_g, v_w_in_a, v_b_f_a, v_w_in_b, v_ln_kv_g, v_w_kv, v_ln_mem_g, v_w_memkv, v_w_out, v_ln_ffn_g, v_w_up, v_conv_w, v_conv_b, v_w_down, v_final_g):
    B, S, D = x.shape
    NM = mem.shape[1]
    T = B * S
    F = w_down.shape[1] * N_DEV
    my_idx = 4 * lax.axis_index("x") + 2 * lax.axis_index("y") + lax.axis_index("c")

    shards = {"w_in_a": w_in_a[0], "w_in_b": w_in_b[0], "w_kv": w_kv, "w_memkv": w_memkv, "w_out": w_out,
              "w_up": w_up, "w_down": w_down}
    moms = {"w_in_a": (m_w_in_a[0], v_w_in_a[0]), "w_in_b": (m_w_in_b[0], v_w_in_b[0]), "w_kv": (m_w_kv, v_w_kv),
            "w_memkv": (m_w_memkv, v_w_memkv), "w_out": (m_w_out, v_w_out), "w_up": (m_w_up, v_w_up),
            "w_down": (m_w_down, v_w_down)}

    groups = [("a1", [("w_in_a", None)]),
              ("a2", [("w_memkv", None), ("w_out", None), ("conv_w", None)]),
              ("b0", [("w_up", 0), ("w_down", 0)]), ("a3", [("w_in_b", None), ("w_kv", None)]),
              ("b1", [("w_up", 1), ("w_down", 1)])]
    sources = dict(shards, conv_w=conv_w)
    span = {}
    for sname, batch in (("first", groups[:1]), ("rest", groups[1:])):
        srcs = []
        for gname, members in batch:
            span[gname] = (sname, len(srcs), len(members))
            for n, layer in members:
                a = sources[n] if layer is None else sources[n][layer]
                srcs.append(a if n == "conv_w" else a.astype(BF16))
        span[sname] = _scatter_start(srcs, _whole, name=f"gather_start_{sname}")
    token = span["rest"][4]

    def gathered(gname, after):
        sname, lo, cnt = span[gname]
        ssem, rsem, thru, lands, _ = span[sname]
        thru, lands = _scatter_wait(ssem, rsem, thru[lo:lo + cnt], lands[lo:lo + cnt], _whole, after,
                                    name=f"gather_wait_{gname}", first=lo)
        return [lax.dynamic_update_index_in_dim(land, s, my_idx, 0) for land, s in zip(lands, thru)]

    full = {}
    (g_wa,) = gathered("a1", token)
    full["w_in_a"] = _from_slots(g_wa, "rows2")

    wa = full["w_in_a"]
    n_qkv = 3 * MAIN_W
    wa = jnp.concatenate([wa[:, :n_qkv], wa[:, n_qkv + N_MAIN_HEADS:], wa[:, n_qkv:n_qkv + N_MAIN_HEADS],
                          jnp.zeros((D, LANES - N_MAIN_HEADS), BF16)], axis=1)
    n_main = n_qkv + MEM_W
    full["w_up"], full["w_down"] = {}, {}
    b_f =_pad_rows(b_f_a.reshape(1, N_MAIN_HEADS), LANES, 1)

    x2d = x.reshape(T, D)
    mem2d = mem.reshape(B * NM, D)
    tgt2d = loss_target.reshape(T, D)
    PM, PX = N_MAIN_HEADS // 2, N_MEM_HEADS // 2

    def stats_to_heads(c2d):
        c = c2d.reshape(B, S, LANES)[:, :, :N_MAIN_HEADS].transpose(0, 2, 1)
        return c[:, :, None, :]

    def mem_kv(layer):
        return _mm_fwd(mem2d, full["w_memkv"][layer], name=f"memkv{layer}", tm=B * NM, tn=2 * MEM_W,
                       out_dtype=BF16, g=ln_mem_g[layer], save_h=True)

    def conv_ffn_fwd(xin, layer):
        uc, ub, a, h = _ffn_up_gate(xin, ln_ffn_g[layer], full["w_up"][layer], conv_w_full[layer],
                                    conv_b[layer].reshape(1, 2 * F), name=f"ffn_up{layer}", S=S)
        xo = _mm_fwd(a, full["w_down"][layer], name=f"ffn_down{layer}", tm=min(1024, T), tn=1024, out_dtype=F32, res=xin)
        return xo, (uc, ub, h, a)

    proj_a, h_mix0 = _mm_fwd(x2d, wa, name="in_proj_a", tm=min(1024, T), tn=2560, out_dtype=BF16, g=ln_mix_g[0],
                             ncols=n_main, save_h=True)
    f_logit = _mm_fwd(x2d, wa, name="in_proj_f", tm=min(1024, T), tn=LANES, out_dtype=F32, g=ln_mix_g[0],
                      col0=n_main // LANES, ncols=LANES)
    c2d = _forget_cumsum(f_logit, b_f, B=B, S=S, name="forget_cumsum")
    cr = stats_to_heads(c2d)
    o_main0, lse0 = _fox_fwd_g(proj_a, proj_a, proj_a, cr, name="fox_fwd", B=B, S=S, P=PM, q_cb=0, k_cb=PM, v_cb=2 * PM,
                               G=FWD_HEAD_GROUP)
    g_wmem, g_wout, g_cw = gathered("a2", lse0)
    full["w_memkv"] = _from_slots(g_wmem, "rows3")
    full["w_out"] = _from_slots(g_wout, "rows3")
    conv_w_full = _from_slots(g_cw, "cols3")
    memkv0, h_mem0 = mem_kv(0)
    o_mem0, lse_m0 = _mem_fwd(proj_a, memkv0, name="mem_fwd0", B=B, S=S, NM=NM, q_cb=3 * PM)
    o_cat0 = jnp.concatenate([o_main0, o_mem0], axis=1)
    x1 = _mm_fwd(o_cat0, full["w_out"][0], name="out_proj0", tm=min(1024, T), tn=1024, out_dtype=F32, res=x2d)
    g_up, g_dn = gathered("b0", x1)
    full["w_up"][0], full["w_down"][0] = _from_slots(g_up, "cols2"), _from_slots(g_dn, "rows2")
    x2, ffn_saved0 = conv_ffn_fwd(x1, 0)
    g_wb, g_wkv = gathered("a3", x2)
    wb, wkv = _from_slots(g_wb, "rows2"), _from_slots(g_wkv, "cols2")
    kv, h_kv =_mm_fwd(x2, wkv, name="kv_proj", tm=min(1024, T), tn=1536, out_dtype=BF16, g=ln_kv_g, save_h=True)
    proj_b, h_mix1 = _mm_fwd(x2, wb, name="in_proj_b", tm=min(1024, T), tn=1024, out_dtype=BF16, g=ln_mix_g[1],
                             save_h=True)
    o_main1, rt1 = _sb_fwd_g(proj_b, kv, kv, name="sb_fwd", B=B, S=S, P=PM, q_cb=0, k_cb=0, v_cb=PM,
                             G=FWD_HEAD_GROUP)
    memkv1, h_mem1 = mem_kv(1)
    o_mem1, lse_m1 = _mem_fwd(proj_b, memkv1, name="mem_fwd1", B=B, S=S, NM=NM, q_cb=PM)
    o_cat1 = jnp.concatenate([o_main1, o_mem1], axis=1)
    x3 = _mm_fwd(o_cat1, full["w_out"][1], name="out_proj1", tm=min(1024, T), tn=1024, out_dtype=F32, res=x2)
    g_up, g_dn = gathered("b1", x3)
    full["w_up"][1], full["w_down"][1] = _from_slots(g_up, "cols2"), _from_slots(g_dn, "rows2")
    x4, ffn_saved1 = conv_ffn_fwd(x3, 1)
    dx4, dg_final, loss_part = _loss_head(x4, final_g, tgt2d, name="loss_head")

    grads = {}
    small = {}
    reduce_groups = []

    def start_reduce(gname, keys, kinds):
        slots = [_to_slots(grads[k], kind) for k, kind in zip(keys, kinds)]
        ssem, rsem, thru, lands, tok = _scatter_start(slots, _slot, name=f"reduce_start_{gname}")
        reduce_groups.append((gname, keys, ssem, rsem, thru, lands))
        return tok[0, 0]

    def conv_ffn_bwd(dxo, xin, saved, layer):
        uc, ub, h, a = saved
        w_dn = full["w_down"][layer]
        da = _mm_nt(dxo, w_dn, name=f"d_act{layer}", tm=min(1024, T), tn=F // 2, out_dtype=F32)
        grads[("w_down", layer)] = _wgrad(a, dxo, f"g_w_down{layer}")
        cwl = conv_w_full[layer]
        du_g, du_v, p_g, p_v = _conv_gate_bwd(da, uc, ub, cwl, name=f"conv_bwd{layer}", B=B, S=S)
        small[("conv_w", layer)] = jnp.concatenate([p_g[0:3], p_v[0:3]], axis=1)
        small[("conv_b", layer)] = jnp.concatenate([p_g[3], p_v[3]], axis=0)
        grads[("w_up", layer)] = jnp.concatenate(
            [_wgrad(h, du_g, f"g_w_up_gate{layer}"), _wgrad(h, du_v, f"g_w_up_val{layer}")], axis=1)
        tok = start_reduce(f"ffn{layer}", [("w_down", layer), ("w_up", layer)], ["rows2", "cols2"])
        dxi, dg = _mm_nt_rmsbwd([(du_g, 0), (du_v, 1)], full["w_up"][layer], xin, ln_ffn_g[layer] + tok,
                                name=f"d_ffn_in{layer}", dres=dxo)
        small[("ln_ffn_g", layer)] = dg[0]
        return dxi

    def mem_bwd(proj, q_cb, memkv, h_mem, do_cat, o_mem, lse_m, layer):
        dqm, dmk, dmv = _mem_bwd(proj, memkv, do_cat, o_mem, lse_m, name=f"mem_bwd{layer}", B=B, S=S, NM=NM,
                                 q_cb=q_cb, do_cb=PM)
        grads[("w_memkv", layer)] = jnp.concatenate(
            [_wgrad(h_mem, dmk, f"g_w_memk{layer}"), _wgrad(h_mem, dmv, f"g_w_memv{layer}")], axis=1)
        _, dg = _mm_nt_rmsbwd([(dmk, 0), (dmv, 1)], full["w_memkv"][layer], mem2d, ln_mem_g[layer],
                              name=f"d_mem_in{layer}", want_dx=False)
        small[("ln_mem_g", layer)] = dg[0]
        return dqm

    dx3 = conv_ffn_bwd(dx4, x3, ffn_saved1, 1)
    do_cat1 = _mm_nt(dx3, full["w_out"][1], name="d_o_cat1", tm=min(1024, T), tn=1024, out_dtype=BF16)
    grads[("w_out", 1)] = _wgrad(o_cat1, dx3, "g_w_out1")
    dq1, dk1, dv1 = _sb_bwd_g(proj_b, kv, kv, do_cat1, rt1, name="sb_bwd", B=B, S=S, P=PM, q_cb=0, k_cb=0, v_cb=PM,
                            do_cb=0)
    dqm1 = mem_bwd(proj_b, PM, memkv1, h_mem1, do_cat1, o_mem1, lse_m1, 1)
    grads["w_in_b"] = jnp.concatenate([_wgrad(h_mix1, dq1, "g_w_in_b_q"), _wgrad(h_mix1, dqm1, "g_w_in_b_m")], axis=1)
    grads["w_kv"] = jnp.concatenate([_wgrad(h_kv, dk1, "g_w_kv_k"), _wgrad(h_kv, dv1, "g_w_kv_v")], axis=1)
    tok = start_reduce("mix1", [("w_out", 1), "w_in_b", "w_kv", ("w_memkv", 1)], ["rows2", "rows2", "cols2", "rows2"])
    dx2, dg = _mm_nt_rmsbwd([(dq1, 0), (dqm1, MAIN_W // MEM_W)], wb, x2, ln_mix_g[1] + tok, name="d_mix_in1", dres=dx3)
    small[("ln_mix_g", 1)] = dg[0]
    dx2, dg = _mm_nt_rmsbwd([(dk1, 0), (dv1, 1)], wkv, x2, ln_kv_g, name="d_kv_in", dres=dx2)
    small["ln_kv_g"] = dg[0]
    dx1 = conv_ffn_bwd(dx2, x1, ffn_saved0, 0)
    do_cat0 = _mm_nt(dx1, full["w_out"][0], name="d_o_cat0", tm=min(1024, T), tn=1024, out_dtype=BF16)
    grads[("w_out", 0)] = _wgrad(o_cat0, dx1, "g_w_out0")
    dq0, dk0, dv0, dcs = _fox_bwd_g(proj_a, proj_a, proj_a, do_cat0, lse0, cr, name="fox_bwd", B=B, S=S, P=PM, q_cb=0,
                                  k_cb=PM, v_cb=2 * PM, do_cb=0)
    dqm0 = mem_bwd(proj_a, 3 * PM, memkv0, h_mem0, do_cat0, o_mem0, lse_m0, 0)
    dc2d = _pad_rows(dcs[:, :, 0, :].transpose(0, 2, 1).reshape(T, N_MAIN_HEADS), LANES, 1)
    df, db_f = _forget_cumsum_bwd(dc2d, f_logit, b_f, B=B, S=S, name="forget_cumsum_bwd")
    a_parts = [(dq0, 0), (dk0, 1), (dv0, 2), (dqm0, n_qkv // MEM_W), (df, n_main // LANES)]
    g_wa = jnp.concatenate([_wgrad(h_mix0, p, f"g_w_in_a{k}") for k, (p, _) in enumerate(a_parts)], axis=1)
    grads["w_in_a"] = jnp.concatenate([g_wa[:, :n_qkv], g_wa[:, n_main:n_main + N_MAIN_HEADS], g_wa[:, n_qkv:n_main]],
                                      axis=1)
    tok = start_reduce("mix0", [("w_out", 0), ("w_memkv", 0), "w_in_a"], ["rows2", "rows2", "rows2"])
    dx0, dg = _mm_nt_rmsbwd(a_parts, wa, x2d, ln_mix_g[0] + tok, name="d_mix_in0", dres=dx1)
    small[("ln_mix_g", 0)] = dg[0]
    grad_x = dx0.reshape(B, S, D)

    def both_small(name):
        return jnp.stack([small[(name, 0)], small[(name, 1)]])

    small_list = [("ln_mix_g", both_small("ln_mix_g")), ("b_f_a", db_f[:, :N_MAIN_HEADS]), ("ln_kv_g", small["ln_kv_g"]),
                  ("ln_mem_g", both_small("ln_mem_g")), ("ln_ffn_g", both_small("ln_ffn_g")),
                  ("conv_w", both_small("conv_w")), ("conv_b", both_small("conv_b")), ("final_g", dg_final[0]),
                  ("loss", loss_part[0, :1])]
    sm_rows = []
    for _, a in small_list:
        flat = a.reshape(-1)
        sm_rows.append(_pad_rows(flat, _round_up(flat.size, 8 * LANES), 0).reshape(-1, LANES))
    spack = jnp.concatenate(sm_rows, axis=0)
    s_ssem, s_rsem, s_thru, s_lands, s_tok = _scatter_start([spack], _whole, name="small_start")

    pieces = {}
    for gname, keys, ssem, rsem, thru, lands in reduce_groups:
        thru, lands = _scatter_wait(ssem, rsem, thru, lands, _slot, s_tok, name=f"reduce_wait_{gname}")
        for key, mine, land in zip(keys, thru, lands):
            own = lax.dynamic_index_in_dim(mine, my_idx, 0, keepdims=False)
            land = lax.dynamic_update_index_in_dim(land, own, my_idx, 0)
            tag = key if isinstance(key, str) else f"{key[0]}{key[1]}"
            pieces[key] = _sum_slots(land, name=f"sum_{tag}")

    red = {}
    for n in ("w_in_a", "w_in_b", "w_kv"):
        red[n] = pieces[n].reshape(shards[n].shape)
    for n in ("w_memkv", "w_out", "w_up", "w_down"):
        red[n] = jnp.stack([pieces[(n, 0)], pieces[(n, 1)]])

    weights = {"ln_mix_g": ln_mix_g, "w_in_a": w_in_a, "b_f_a": b_f_a, "w_in_b": w_in_b, "ln_kv_g": ln_kv_g,
               "w_kv": w_kv, "ln_mem_g": ln_mem_g, "w_memkv": w_memkv, "w_out": w_out, "ln_ffn_g": ln_ffn_g,
               "w_up": w_up, "conv_w": conv_w, "conv_b": conv_b, "w_down": w_down, "final_g": final_g}
    m_in = {"ln_mix_g": m_ln_mix_g, "w_in_a": m_w_in_a, "b_f_a": m_b_f_a, "w_in_b": m_w_in_b, "ln_kv_g": m_ln_kv_g,
            "w_kv": m_w_kv, "ln_mem_g": m_ln_mem_g, "w_memkv": m_w_memkv, "w_out": m_w_out, "ln_ffn_g": m_ln_ffn_g,
            "w_up": m_w_up, "conv_w": m_conv_w, "conv_b": m_conv_b, "w_down": m_w_down, "final_g": m_final_g}
    v_in = {"ln_mix_g": v_ln_mix_g, "w_in_a": v_w_in_a, "b_f_a": v_b_f_a, "w_in_b": v_w_in_b, "ln_kv_g": v_ln_kv_g,
            "w_kv": v_w_kv, "ln_mem_g": v_ln_mem_g, "w_memkv": v_w_memkv, "w_out": v_w_out, "ln_ffn_g": v_ln_ffn_g,
            "w_up": v_w_up, "conv_w": v_conv_w, "conv_b": v_conv_b, "w_down": v_w_down, "final_g": v_final_g}
    order = list(weights)
    big_names = [n for n, _ in BIG]
    g_out, d_out, nm_out, nv_out = {}, {}, {}, {}

    def update(n):
        w = weights[n]
        cols = w.shape[-1]
        g = red[n].reshape(w.shape)
        d, nm, nv = _adamw(w.reshape(-1, cols), g.reshape(-1, cols), m_in[n].reshape(-1, cols),
                           v_in[n].reshape(-1, cols), name=f"adamw_{n}")
        g_out[n], d_out[n], nm_out[n], nv_out[n] = g, d.reshape(w.shape), nm.reshape(w.shape), nv.reshape(w.shape)

    for n in big_names:
        update(n)
    all_updated = jnp.stack([d_out[n].reshape(-1)[0] for n in big_names])
    s_thru, s_lands = _scatter_wait(s_ssem, s_rsem, s_thru, s_lands, _whole, all_updated, name="small_wait")
    ssum = _sum_slots(lax.dynamic_update_index_in_dim(s_lands[0], s_thru[0], my_idx, 0), name="sum_small")
    off = 0
    for (n, a), rows in zip(small_list, sm_rows):
        red[n] = ssum[off:off + rows.shape[0]].reshape(-1)[:a.size].reshape(a.shape)
        off += rows.shape[0]
    loss = red["loss"][0]
    shard_cols = conv_w.shape[2]
    red["conv_w"] = lax.dynamic_slice_in_dim(red["conv_w"], my_idx * shard_cols, shard_cols, axis=2)
    red["b_f_a"] = red["b_f_a"].reshape(b_f_a.shape)
    update("conv_w")
    small_names = [n for n in order if n not in g_out]

    def pack_small(src):
        rows = []
        for n in small_names:
            flat = src[n].reshape(-1)
            rows.append(_pad_rows(flat, _round_up(flat.size, 8 * LANES), 0).reshape(-1, LANES))
        return jnp.concatenate(rows, axis=0), [r.shape[0] for r in rows]

    red_small = {n: red[n].reshape(weights[n].shape) for n in small_names}
    wp, counts = pack_small(weights)
    gp, _ = pack_small(red_small)
    mp, _ = pack_small(m_in)
    vp, _ = pack_small(v_in)
    dp, nmp, nvp = _adamw(wp, gp, mp, vp, name="adamw_small")
    off = 0
    for n, cnt in zip(small_names, counts):
        shp = weights[n].shape
        size = weights[n].size
        g_out[n] = red_small[n]
        d_out[n] = dp[off:off + cnt].reshape(-1)[:size].reshape(shp)
        nm_out[n] = nmp[off:off + cnt].reshape(-1)[:size].reshape(shp)
        nv_out[n] = nvp[off:off + cnt].reshape(-1)[:size].reshape(shp)
        off += cnt

    return (loss, grad_x, *[g_out[n] for n in order], *[d_out[n] for n in order],
            *[nm_out[n] for n in order], *[nv_out[n] for n in order])
```

```python
import functools

import jax
import jax.numpy as jnp
from jax import lax
from jax.experimental import pallas as pl
from jax.experimental.pallas import tpu as pltpu

F32 = jnp.float32
BF16 = jnp.bfloat16
LANES = 128
HEAD_DIM = 64
N_MAIN_HEADS = 12
N_MEM_HEADS = 4
MAIN_W = N_MAIN_HEADS * HEAD_DIM
MEM_W = N_MEM_HEADS * HEAD_DIM
SCALE = HEAD_DIM ** -0.5
EPS = 1e-6
NEG = -1e30
N_DEV = 8
ATT_TILE = 256
MEM_Q_TILE = 1024
VMEM_BIG = 56 * 2 ** 20
MESH = pl.DeviceIdType.MESH

ADAM_LR = 0.001
ADAM_B1 = 0.9
ADAM_B2 = 0.999
ADAM_EPS = 1e-08
ADAM_WD = 0.01
ADAM_STEP = 10

NT = (((1,), (1,)), ((), ()))
TN = (((0,), (0,)), ((), ()))


def _pc(body, *, name, out_shape, grid=None, in_specs=None, out_specs=None, scratch_shapes=(),
        semantics=None, vmem=None):
    kw = {}
    if grid is not None:
        kw["grid"] = grid
    params = pltpu.CompilerParams(dimension_semantics=semantics, vmem_limit_bytes=vmem)
    return pl.pallas_call(body, name=name, out_shape=out_shape, in_specs=in_specs, out_specs=out_specs,
                          scratch_shapes=list(scratch_shapes), compiler_params=params, **kw)


def _sds(shape, dtype):
    return jax.ShapeDtypeStruct(shape, dtype)


def _mm_fwd(a, w, *, name, tm, tn, out_dtype, g=None, res=None, col0=0, ncols=None, save_h=False):
    m_rows, k = a.shape
    n = w.shape[1] if ncols is None else ncols
    grid = (m_rows // tm, n // tn)
    norm = g is not None

    def body(*refs):
        refs = list(refs)
        a_ref = refs.pop(0)
        g_ref = refs.pop(0) if norm else None
        w_ref = refs.pop(0)
        res_ref = refs.pop(0) if res is not None else None
        o_ref = refs.pop(0)
        hout_ref = refs.pop(0) if save_h else None
        h_ref = refs.pop(0) if norm else None
        if norm:
            @pl.when(pl.program_id(1) == 0)
            def _():
                xv = a_ref[...]
                r = lax.rsqrt(jnp.mean(xv * xv, axis=-1, keepdims=True) + EPS)
                h = ((xv * r) * g_ref[...]).astype(BF16)
                h_ref[...] = h
                if save_h:
                    hout_ref[...] = h
            lhs = h_ref[...]
        else:
            lhs = a_ref[...].astype(BF16)
        acc = jnp.dot(lhs, w_ref[...], preferred_element_type=F32)
        if res is not None:
            acc = acc + res_ref[...]
        o_ref[...] = acc.astype(out_dtype)

    in_specs = [pl.BlockSpec((tm, k), lambda i, j: (i, 0))]
    args = [a]
    if norm:
        in_specs.append(pl.BlockSpec((1, k), lambda i, j: (0, 0)))
        args.append(g.reshape(1, k))
    in_specs.append(pl.BlockSpec((k, tn), lambda i, j: (0, j + col0)))
    args.append(w)
    if res is not None:
        in_specs.append(pl.BlockSpec((tm, tn), lambda i, j: (i, j)))
        args.append(res)
    out_shape = [_sds((m_rows, n), out_dtype)]
    out_specs = [pl.BlockSpec((tm, tn), lambda i, j: (i, j))]
    if save_h:
        out_shape.append(_sds((m_rows, k), BF16))
        out_specs.append(pl.BlockSpec((tm, k), lambda i, j: (i, 0)))
    scratch = [pltpu.VMEM((tm, k), BF16)] if norm else []
    outs = _pc(body, name=name, out_shape=out_shape, grid=grid, in_specs=in_specs, out_specs=out_specs,
               scratch_shapes=scratch, semantics=("arbitrary", "arbitrary"), vmem=VMEM_BIG)(*args)
    return outs if save_h else outs[0]


def _mm_nt(a, w, *, name, tm, tn, out_dtype):
    m_rows, k = a.shape
    n = w.shape[0]

    def body(a_ref, w_ref, o_ref):
        acc = lax.dot_general(a_ref[...].astype(BF16), w_ref[...], NT, preferred_element_type=F32)
        o_ref[...] = acc.astype(out_dtype)

    return _pc(body, name=name, out_shape=_sds((m_rows, n), out_dtype), grid=(m_rows // tm, n // tn),
               in_specs=[pl.BlockSpec((tm, k), lambda i, j: (i, 0)), pl.BlockSpec((tn, k), lambda i, j: (j, 0))],
               out_specs=pl.BlockSpec((tm, tn), lambda i, j: (i, j)),
               semantics=("arbitrary", "arbitrary"), vmem=VMEM_BIG)(a, w)


def _mm_tn(a, b, *, name, ta, tn, tt):
    t_rows, ka = a.shape
    n = b.shape[1]
    nt = t_rows // tt

    def body(a_ref, b_ref, o_ref, acc_ref):
        t = pl.program_id(2)

        @pl.when(t == 0)
        def _():
            acc_ref[...] = jnp.zeros_like(acc_ref)

        acc_ref[...] += lax.dot_general(a_ref[...].astype(BF16), b_ref[...].astype(BF16), TN,
                                        preferred_element_type=F32)

        @pl.when(t == nt - 1)
        def _():
            o_ref[...] = acc_ref[...].astype(BF16)

    return _pc(body, name=name, out_shape=_sds((ka, n), BF16), grid=(ka // ta, n // tn, nt),
               in_specs=[pl.BlockSpec((tt, ta), lambda i, j, t: (t, i)),
                         pl.BlockSpec((tt, tn), lambda i, j, t: (t, j))],
               out_specs=pl.BlockSpec((ta, tn), lambda i, j, t: (i, j)),
               scratch_shapes=[pltpu.VMEM((ta, tn), F32)],
               semantics=("arbitrary", "arbitrary", "arbitrary"), vmem=VMEM_BIG)(a, b)


def _wgrad(a, b, name):
    t_rows, ka = a.shape
    n = b.shape[1]
    ta = ka if ka <= 1024 else ka // 2
    tn = n
    while ta * tn * 4 > 6 * 2 ** 20 and tn % 256 == 0:
        tn //= 2
    tt = min(2048, t_rows)
    return _mm_tn(a, b, name=name, ta=ta, tn=tn, tt=tt)


def _mm_nt_rmsbwd(parts, w, x, g, *, name, dres=None, want_dx=True):
    m_rows, d = x.shape
    tm = min(512, m_rows)
    n_parts = len(parts)

    def body(*refs):
        refs = list(refs)
        dy_refs = [refs.pop(0) for _ in range(n_parts)]
        w_refs = [refs.pop(0) for _ in range(n_parts)]
        x_ref = refs.pop(0)
        g_ref = refs.pop(0)
        dres_ref = refs.pop(0) if dres is not None else None
        dx_ref = refs.pop(0) if want_dx else None
        dg_ref = refs.pop(0)

        @pl.when(pl.program_id(0) == 0)
        def _():
            dg_ref[...] = jnp.zeros_like(dg_ref)

        dh = None
        for dy_ref, w_ref in zip(dy_refs, w_refs):
            t = lax.dot_general(dy_ref[...].astype(BF16), w_ref[...], NT, preferred_element_type=F32)
            dh = t if dh is None else dh + t
        xv = x_ref[...]
        r = lax.rsqrt(jnp.mean(xv * xv, axis=-1, keepdims=True) + EPS)
        xh = xv * r
        dg_ref[...] += jnp.sum(dh * xh, axis=0, keepdims=True)
        if want_dx:
            dhg = dh * g_ref[...]
            dx = r * (dhg - xh * jnp.mean(dhg * xh, axis=-1, keepdims=True))
            if dres is not None:
                dx = dx + dres_ref[...]
            dx_ref[...] = dx

    in_specs, args = [], []
    for dy, _ in parts:
        in_specs.append(pl.BlockSpec((tm, dy.shape[1]), lambda i: (i, 0)))
        args.append(dy)
    for dy, cb in parts:
        in_specs.append(pl.BlockSpec((d, dy.shape[1]), functools.partial(lambda i, cb: (0, cb), cb=cb)))
        args.append(w)
    in_specs += [pl.BlockSpec((tm, d), lambda i: (i, 0)), pl.BlockSpec((1, d), lambda i: (0, 0))]
    args += [x, g.reshape(1, d)]
    if dres is not None:
        in_specs.append(pl.BlockSpec((tm, d), lambda i: (i, 0)))
        args.append(dres)
    out_shape, out_specs = [], []
    if want_dx:
        out_shape.append(_sds((m_rows, d), F32))
        out_specs.append(pl.BlockSpec((tm, d), lambda i: (i, 0)))
    out_shape.append(_sds((1, d), F32))
    out_specs.append(pl.BlockSpec((1, d), lambda i: (0, 0)))
    outs = _pc(body, name=name, out_shape=out_shape, grid=(m_rows // tm,), in_specs=in_specs,
               out_specs=out_specs, semantics=("arbitrary",), vmem=VMEM_BIG)(*args)
    return (outs[0], outs[1]) if want_dx else (None, outs[0])


def _loss_head(x, g, tgt, *, name):
    m_rows, d = x.shape
    tm = min(512, m_rows)

    def body(x_ref, g_ref, t_ref, dx_ref, dg_ref, loss_ref):
        @pl.when(pl.program_id(0) == 0)
        def _():
            dg_ref[...] = jnp.zeros_like(dg_ref)
            loss_ref[...] = jnp.zeros_like(loss_ref)

        xv = x_ref[...]
        r = lax.rsqrt(jnp.mean(xv * xv, axis=-1, keepdims=True) + EPS)
        xh = xv * r
        gv = g_ref[...]
        err = xh * gv - t_ref[...]
        per_tok = jnp.mean(err * err, axis=-1, keepdims=True)
        loss_ref[...] += 0.5 * jnp.sum(per_tok, axis=0, keepdims=True)
        dout = err * (1.0 / d)
        dg_ref[...] += jnp.sum(dout * xh, axis=0, keepdims=True)
        dhg = dout * gv
        dx_ref[...] = r * (dhg - xh * jnp.mean(dhg * xh, axis=-1, keepdims=True))

    row = pl.BlockSpec((tm, d), lambda i: (i, 0))
    return _pc(body, name=name, out_shape=[_sds((m_rows, d), F32), _sds((1, d), F32), _sds((1, LANES), F32)],
               grid=(m_rows // tm,), in_specs=[row, pl.BlockSpec((1, d), lambda i: (0, 0)), row],
               out_specs=[row, pl.BlockSpec((1, d), lambda i: (0, 0)), pl.BlockSpec((1, LANES), lambda i: (0, 0))],
               semantics=("arbitrary",))(x, g.reshape(1, d), tgt)


def _split3(v):
    hi = v.astype(BF16)
    r1 = v - hi.astype(F32)
    mid = r1.astype(BF16)
    lo = (r1 - mid.astype(F32)).astype(BF16)
    return hi, mid, lo


def _split2(v):
    hi = v.astype(BF16)
    lo = (v - hi.astype(F32)).astype(BF16)
    return hi, lo


def _tri_dot3(tri, v):
    hi, mid, lo = _split3(v)
    return (jnp.dot(tri, hi, preferred_element_type=F32) + jnp.dot(tri, mid, preferred_element_type=F32)
            + jnp.dot(tri, lo, preferred_element_type=F32))


def _log_sigmoid(v):
    return jnp.minimum(v, 0.0) - jnp.log(1.0 + jnp.exp(-jnp.abs(v)))


def _forget_cumsum(f_logit, b_f, *, B, S, name):
    ch = min(256, S)
    nch = S // ch

    def body(f_ref, b_ref, c_ref):
        r_i = lax.broadcasted_iota(jnp.int32, (ch, ch), 0)
        c_i = lax.broadcasted_iota(jnp.int32, (ch, ch), 1)
        tri = (c_i <= r_i).astype(BF16)
        bv = b_ref[...]

        def step(k, carry):
            rows = pl.ds(pl.multiple_of(k * ch, ch), ch)
            lf = _log_sigmoid(f_ref[rows, :] + bv)
            c_ref[rows, :] = _tri_dot3(tri, lf) + carry
            return carry + jnp.sum(lf, axis=0, keepdims=True)

        lax.fori_loop(0, nch, step, jnp.zeros((1, LANES), F32))

    blk = pl.BlockSpec((S, LANES), lambda b: (b, 0))
    return _pc(body, name=name, out_shape=_sds((B * S, LANES), F32), grid=(B,),
               in_specs=[blk, pl.BlockSpec((1, LANES), lambda b: (0, 0))], out_specs=blk,
               semantics=("arbitrary",))(f_logit, b_f)


def _forget_cumsum_bwd(dc, f_logit, b_f, *, B, S, name):
    ch = min(256, S)
    nch = S // ch

    def body(dc_ref, f_ref, b_ref, df_ref, db_ref):
        @pl.when(pl.program_id(0) == 0)
        def _():
            db_ref[...] = jnp.zeros_like(db_ref)

        r_i = lax.broadcasted_iota(jnp.int32, (ch, ch), 0)
        c_i = lax.broadcasted_iota(jnp.int32, (ch, ch), 1)
        tri = (c_i >= r_i).astype(BF16)
        bv = b_ref[...]

        def step(kk, carry):
            tail, dbs = carry
            k = nch - 1 - kk
            rows = pl.ds(pl.multiple_of(k * ch, ch), ch)
            dcv = dc_ref[rows, :]
            dlf = _tri_dot3(tri, dcv) + tail
            z = f_ref[rows, :] + bv
            df = dlf * (1.0 / (1.0 + jnp.exp(z)))
            df_ref[rows, :] = df.astype(BF16)
            return tail + jnp.sum(dcv, axis=0, keepdims=True), dbs + jnp.sum(df, axis=0, keepdims=True)

        zero = jnp.zeros((1, LANES), F32)
        _, dbs = lax.fori_loop(0, nch, step, (zero, zero))
        db_ref[...] += dbs

    blk = pl.BlockSpec((S, LANES), lambda b: (b, 0))
    one = pl.BlockSpec((1, LANES), lambda b: (0, 0))
    return _pc(body, name=name, out_shape=[_sds((B * S, LANES), BF16), _sds((1, LANES), F32)], grid=(B,),
               in_specs=[blk, blk, one], out_specs=[blk, one], semantics=("arbitrary",))(dc, f_logit, b_f)


def _head_mask(lane, hh):
    return (lane < HEAD_DIM) if hh == 0 else (lane >= HEAD_DIM)


HEAD_GROUP = 3
FWD_HEAD_GROUP = 6


def _g_col_spec(rows, nblk_rows, cb, G):
    return pl.BlockSpec((rows, G * LANES), lambda b, p, i: (b * nblk_rows + i, cb // G + p))


def _g_kv_spec(rows, cb, G):
    return pl.BlockSpec((rows, G * LANES), lambda b, p, i: (b, cb // G + p))


def _g_stat_col_spec(tq, G):
    return pl.BlockSpec((1, 2 * G, tq, 1), lambda b, p, i: (b, p, i, 0))


def _g_stat_row_spec(S, G):
    return pl.BlockSpec((1, 2 * G, 1, S), lambda b, p, i: (b, p, 0, 0))


def _lanes(g):
    return slice(g * LANES, (g + 1) * LANES)


def _streams(x_ref, G, scale=None):
    rows = x_ref.shape[0]
    lane = lax.broadcasted_iota(jnp.int32, (rows, LANES), 1)
    out = []
    for g in range(G):
        x = x_ref[:, _lanes(g)]
        if scale is not None:
            x = x * jnp.asarray(scale, x.dtype)
        for hh in range(2):
            out.append(jnp.where(_head_mask(lane, hh), x, jnp.zeros_like(x)))
    return lane, out


def _wide(stat, width):
    return jnp.tile(stat, (1, width // LANES))


def _fold_lanes(v):
    out = v[:, :LANES]
    for j in range(1, v.shape[1] // LANES):
        out = out + v[:, j * LANES:(j + 1) * LANES]
    return out


def _kv_blocks(ref, ks, tk, G):
    return [ref[pl.ds(ks, tk), _lanes(g)] for g in range(G)]


def _sweep(i, block):
    def step(kb, c):
        block(kb, False)
        return c
    lax.fori_loop(0, i, step, 0)
    block(i, True)


def _fox_fwd_g(qa, ka, va, cr, *, name, B, S, P, q_cb, k_cb, v_cb, G=HEAD_GROUP):
    tq = tk = min(ATT_TILE, S)
    nq = S // tq
    NS = 2 * G

    def body(q_ref, k_ref, v_ref, cr_ref, o_ref, lse_ref, acc_ref, m_ref, l_ref):
        i = pl.program_id(2)
        lane, qh = _streams(q_ref, G, SCALE)
        on_or_below = (lax.broadcasted_iota(jnp.int32, (tq, tk), 1) <= lax.broadcasted_iota(jnp.int32, (tq, tk), 0))
        m_ref[...] = jnp.full(m_ref.shape, NEG, F32)
        l_ref[...] = jnp.zeros(l_ref.shape, F32)
        acc_ref[...] = jnp.zeros(acc_ref.shape, F32)

        def block(kb, diag):
            ks = pl.multiple_of(kb * tk, tk)
            kblk = _kv_blocks(k_ref, ks, tk, G)
            vblk = _kv_blocks(v_ref, ks, tk, G)
            ss = [lax.dot_general(qh[st], kblk[st // 2], NT, preferred_element_type=F32) for st in range(NS)]
            ps = []
            for st in range(NS):
                s = ss[st] - cr_ref[0, st, :, pl.ds(ks, tk)]
                if diag:
                    s = jnp.where(on_or_below, s, NEG)
                m = m_ref[st]
                m_new = jnp.maximum(m, jnp.max(s, axis=-1, keepdims=True))
                alpha = jnp.exp(m - m_new)
                p = jnp.exp(s - _wide(m_new, tk))
                m_ref[st] = m_new
                l_ref[st] = alpha * l_ref[st] + _fold_lanes(p)
                ps.append((alpha, p.astype(BF16)))
            pvs = [jnp.dot(ps[st][1], vblk[st // 2], preferred_element_type=F32) for st in range(NS)]
            for st in range(NS):
                acc_ref[st] = ps[st][0] * acc_ref[st] + pvs[st]

        _sweep(i, block)
        ls = [jnp.sum(l_ref[st], axis=-1, keepdims=True) for st in range(NS)]
        for st in range(NS):
            lse_ref[0, st] = jnp.max(m_ref[st], axis=-1, keepdims=True) + jnp.log(ls[st])
        for g in range(G):
            o_ref[:, _lanes(g)] = jnp.where(lane < HEAD_DIM, acc_ref[2 * g] / ls[2 * g],
                                            acc_ref[2 * g + 1] / ls[2 * g + 1]).astype(BF16)

    return _pc(body, name=name, out_shape=[_sds((B * S, P * LANES), BF16), _sds((B, 2 * P, S, 1), F32)],
               grid=(B, P // G, nq),
               in_specs=[_g_col_spec(tq, nq, q_cb, G), _g_kv_spec(S, k_cb, G), _g_kv_spec(S, v_cb, G),
                         _g_stat_row_spec(S, G)],
               out_specs=[_g_col_spec(tq, nq, 0, G), _g_stat_col_spec(tq, G)],
               scratch_shapes=[pltpu.VMEM((NS, tq, LANES), F32)] * 3,
               semantics=("arbitrary", "arbitrary", "arbitrary"), vmem=VMEM_BIG)(qa, ka, va, cr)


def _fox_bwd_g(qa, ka, va, doa, lse, cr, *, name, B, S, P, q_cb, k_cb, v_cb, do_cb, G=HEAD_GROUP):
    tq = tk = min(ATT_TILE, S)
    nq = S // tq
    NS = 2 * G

    def body(q_ref, k_ref, v_ref, do_ref, lse_ref, cr_ref, dq_ref, dk_ref, dv_ref, dcs_ref, dqa_ref, delta_ref, lse_s,
             p_buf, dp_buf):
        i = pl.program_id(2)

        @pl.when(i == 0)
        def _():
            dk_ref[...] = jnp.zeros_like(dk_ref)
            dv_ref[...] = jnp.zeros_like(dv_ref)
            dcs_ref[...] = jnp.zeros_like(dcs_ref)

        lane, qh = _streams(q_ref, G, SCALE)
        _, doh = _streams(do_ref, G)
        on_or_below = (lax.broadcasted_iota(jnp.int32, (tq, tk), 1) <= lax.broadcasted_iota(jnp.int32, (tq, tk), 0))
        delta_ref[...] = jnp.zeros(delta_ref.shape, F32)
        dqa_ref[...] = jnp.zeros(dqa_ref.shape, F32)
        for st in range(NS):
            lse_s[st] = jnp.broadcast_to(lse_ref[0, st], (tq, LANES))

        def probs(kb, diag):
            ks = pl.multiple_of(kb * tk, tk)
            kblk = _kv_blocks(k_ref, ks, tk, G)
            vblk = _kv_blocks(v_ref, ks, tk, G)
            ss = [lax.dot_general(qh[st], kblk[st // 2], NT, preferred_element_type=F32) for st in range(NS)]
            dps = [lax.dot_general(doh[st], vblk[st // 2], NT, preferred_element_type=F32) for st in range(NS)]
            ps = []
            for st in range(NS):
                s = ss[st] - cr_ref[0, st, :, pl.ds(ks, tk)]
                if diag:
                    s = jnp.where(on_or_below, s, NEG)
                ps.append(jnp.exp(s - _wide(lse_s[st], tk)))
            return ks, kblk, ps, dps

        def delta_block(kb, diag):
            _, _, ps, dps = probs(kb, diag)
            for st in range(NS):
                delta_ref[st] += _fold_lanes(ps[st] * dps[st])
                p_buf[st, kb] = ps[st]
                dp_buf[st, kb] = dps[st]

        _sweep(i, delta_block)
        for st in range(NS):
            delta_ref[st] = jnp.broadcast_to(jnp.sum(delta_ref[st], axis=-1, keepdims=True), (tq, LANES))

        def grad_block(kb, diag):
            ks = pl.multiple_of(kb * tk, tk)
            kblk = _kv_blocks(k_ref, ks, tk, G)
            rows = pl.ds(ks, tk)
            dsb, pb = [], []
            for st in range(NS):
                p = p_buf[st, kb]
                ds = p * (dp_buf[st, kb] - _wide(delta_ref[st], tk))
                dcs_ref[0, st, :, rows] -= jnp.sum(ds, axis=0, keepdims=True)
                dsb.append(ds.astype(BF16))
                pb.append(p.astype(BF16))
            dks = [lax.dot_general(dsb[st], qh[st], TN, preferred_element_type=F32) for st in range(NS)]
            dvs = [lax.dot_general(pb[st], doh[st], TN, preferred_element_type=F32) for st in range(NS)]
            dqs = [jnp.dot(dsb[st], kblk[st // 2], preferred_element_type=F32) for st in range(NS)]
            for g in range(G):
                dk_ref[rows, _lanes(g)] += dks[2 * g] + dks[2 * g + 1]
                dv_ref[rows, _lanes(g)] += dvs[2 * g] + dvs[2 * g + 1]
            for st in range(NS):
                dqa_ref[st] += dqs[st]

        _sweep(i, grad_block)
        for g in range(G):
            dq_ref[:, _lanes(g)] = (jnp.where(lane < HEAD_DIM, dqa_ref[2 * g], dqa_ref[2 * g + 1]) * SCALE).astype(BF16)

    return _pc(body, name=name,
               out_shape=[_sds((B * S, P * LANES), BF16), _sds((B * S, P * LANES), F32), _sds((B * S, P * LANES), F32),
                          _sds((B, 2 * P, 1, S), F32)],
               grid=(B, P // G, nq),
               in_specs=[_g_col_spec(tq, nq, q_cb, G), _g_kv_spec(S, k_cb, G), _g_kv_spec(S, v_cb, G),
                         _g_col_spec(tq, nq, do_cb, G), _g_stat_col_spec(tq, G), _g_stat_row_spec(S, G)],
               out_specs=[_g_col_spec(tq, nq, 0, G), _g_kv_spec(S, 0, G), _g_kv_spec(S, 0, G), _g_stat_row_spec(S, G)],
               scratch_shapes=[pltpu.VMEM((NS, tq, LANES), F32)] * 3 + [pltpu.VMEM((NS, nq, tq, tk), F32)] * 2,
               semantics=("arbitrary", "arbitrary", "arbitrary"), vmem=VMEM_BIG)(qa, ka, va, doa, lse, cr)


def _sb_logs_z(z):
    nz = -z
    lm = jnp.minimum(nz, 0.0) - jnp.log(1.0 + jnp.exp(jnp.minimum(z, nz)))
    return lm + z, lm


def _sb_fwd_g(qa, ka, va, *, name, B, S, P, q_cb, k_cb, v_cb, G=HEAD_GROUP):
    tq = tk = min(ATT_TILE, S)
    nq = S // tq
    NS = 2 * G

    def body(q_ref, k_ref, v_ref, o_ref, rt_ref, acc_ref, run_ref):
        i = pl.program_id(2)
        lane, qh = _streams(q_ref, G, SCALE)
        t_r = lax.broadcasted_iota(jnp.int32, (tk, tk), 0)
        t_c = lax.broadcasted_iota(jnp.int32, (tk, tk), 1)
        after = (t_r > t_c).astype(BF16)
        below = t_c < t_r
        acc_ref[...] = jnp.zeros(acc_ref.shape, F32)
        run_ref[...] = jnp.zeros(run_ref.shape, F32)

        def block(kb, diag):
            ks = pl.multiple_of(kb * tk, tk)
            kblk = _kv_blocks(k_ref, ks, tk, G)
            vblk = _kv_blocks(v_ref, ks, tk, G)
            zs = [lax.dot_general(qh[st], kblk[st // 2], NT, preferred_element_type=F32) for st in range(NS)]
            lss, parts = [], []
            for st in range(NS):
                ls, lm = _sb_logs_z(zs[st])
                if diag:
                    lm = jnp.where(below, lm, 0.0)
                lss.append(ls + _wide(run_ref[st], tk))
                run_ref[st] += jnp.sum(lm, axis=-1, keepdims=True)
                parts.append(_split2(lm))
            sufs = [jnp.dot(parts[st][0], after, preferred_element_type=F32)
                    + jnp.dot(parts[st][1], after, preferred_element_type=F32) for st in range(NS)]
            ab = []
            for st in range(NS):
                a = jnp.exp(lss[st] + sufs[st])
                if diag:
                    a = jnp.where(below, a, 0.0)
                ab.append(a.astype(BF16))
            pvs = [jnp.dot(ab[st], vblk[st // 2], preferred_element_type=F32) for st in range(NS)]
            for st in range(NS):
                acc_ref[st] += pvs[st]

        block(i, True)

        def step(jj, c):
            block(i - 1 - jj, False)
            return c

        lax.fori_loop(0, i, step, 0)
        for st in range(NS):
            rt_ref[0, st] = jnp.max(run_ref[st], axis=-1, keepdims=True)
        for g in range(G):
            o_ref[:, _lanes(g)] = jnp.where(lane < HEAD_DIM, acc_ref[2 * g], acc_ref[2 * g + 1]).astype(BF16)

    return _pc(body, name=name, out_shape=[_sds((B * S, P * LANES), BF16), _sds((B, 2 * P, S, 1), F32)],
               grid=(B, P // G, nq),
               in_specs=[_g_col_spec(tq, nq, q_cb, G), _g_kv_spec(S, k_cb, G), _g_kv_spec(S, v_cb, G)],
               out_specs=[_g_col_spec(tq, nq, 0, G), _g_stat_col_spec(tq, G)],
               scratch_shapes=[pltpu.VMEM((NS, tq, LANES), F32)] * 2,
               semantics=("arbitrary", "arbitrary", "arbitrary"), vmem=VMEM_BIG)(qa, ka, va)


def _sb_bwd_g(qa, ka, va, doa, rt, *, name, B, S, P, q_cb, k_cb, v_cb, do_cb, G=HEAD_GROUP):
    tq = tk = min(ATT_TILE, S)
    nq = S // tq
    NS = 2 * G

    def body(q_ref, k_ref, v_ref, do_ref, rt_ref, dq_ref, dk_ref, dv_ref, dqa_ref, pl_ref, pg_ref):
        i = pl.program_id(2)

        @pl.when(i == 0)
        def _():
            dk_ref[...] = jnp.zeros_like(dk_ref)
            dv_ref[...] = jnp.zeros_like(dv_ref)

        lane, qh = _streams(q_ref, G, SCALE)
        _, doh = _streams(do_ref, G)
        t_r = lax.broadcasted_iota(jnp.int32, (tk, tk), 0)
        t_c = lax.broadcasted_iota(jnp.int32, (tk, tk), 1)
        upto = (t_r <= t_c).astype(BF16)
        before = (t_r < t_c).astype(BF16)
        below = t_c < t_r
        dqa_ref[...] = jnp.zeros(dqa_ref.shape, F32)
        pg_ref[...] = jnp.zeros(pg_ref.shape, F32)
        for st in range(NS):
            pl_ref[st] = jnp.broadcast_to(rt_ref[0, st], (tq, LANES))

        def block(kb, diag):
            ks = pl.multiple_of(kb * tk, tk)
            rows = pl.ds(ks, tk)
            kblk = _kv_blocks(k_ref, ks, tk, G)
            vblk = _kv_blocks(v_ref, ks, tk, G)
            zs = [lax.dot_general(qh[st], kblk[st // 2], NT, preferred_element_type=F32) for st in range(NS)]
            das = [lax.dot_general(doh[st], vblk[st // 2], NT, preferred_element_type=F32) for st in range(NS)]
            lss, parts = [], []
            for st in range(NS):
                ls, lm = _sb_logs_z(zs[st])
                if diag:
                    lm = jnp.where(below, lm, 0.0)
                lss.append((ls, ls + _wide(pl_ref[st], tk)))
                pl_ref[st] -= jnp.sum(lm, axis=-1, keepdims=True)
                parts.append(_split2(lm))
            pins = [jnp.dot(parts[st][0], upto, preferred_element_type=F32)
                    + jnp.dot(parts[st][1], upto, preferred_element_type=F32) for st in range(NS)]
            gms, ab, gparts = [], [], []
            for st in range(NS):
                a = jnp.exp(lss[st][1] - pins[st])
                if diag:
                    a = jnp.where(below, a, 0.0)
                gm = a * das[st]
                gms.append(gm)
                ab.append(a.astype(BF16))
                gparts.append(gm.astype(BF16))
            pgs = [jnp.dot(gparts[st], before, preferred_element_type=F32) for st in range(NS)]
            dzb = []
            for st in range(NS):
                gm = gms[st]
                dz = gm - jnp.exp(lss[st][0]) * (gm + (pgs[st] + _wide(pg_ref[st], tk)))
                if diag:
                    dz = jnp.where(below, dz, 0.0)
                pg_ref[st] += jnp.sum(gm, axis=-1, keepdims=True)
                dzb.append(dz.astype(BF16))
            dks = [lax.dot_general(dzb[st], qh[st], TN, preferred_element_type=F32) for st in range(NS)]
            dvs = [lax.dot_general(ab[st], doh[st], TN, preferred_element_type=F32) for st in range(NS)]
            dqs = [jnp.dot(dzb[st], kblk[st // 2], preferred_element_type=F32) for st in range(NS)]
            for g in range(G):
                dk_ref[rows, _lanes(g)] += dks[2 * g] + dks[2 * g + 1]
                dv_ref[rows, _lanes(g)] += dvs[2 * g] + dvs[2 * g + 1]
            for st in range(NS):
                dqa_ref[st] += dqs[st]

        _sweep(i, block)
        for g in range(G):
            dq_ref[:, _lanes(g)] = (jnp.where(lane < HEAD_DIM, dqa_ref[2 * g], dqa_ref[2 * g + 1]) * SCALE).astype(BF16)

    return _pc(body, name=name,
               out_shape=[_sds((B * S, P * LANES), BF16), _sds((B * S, P * LANES), F32), _sds((B * S, P * LANES), F32)],
               grid=(B, P // G, nq),
               in_specs=[_g_col_spec(tq, nq, q_cb, G), _g_kv_spec(S, k_cb, G), _g_kv_spec(S, v_cb, G),
                         _g_col_spec(tq, nq, do_cb, G), _g_stat_col_spec(tq, G)],
               out_specs=[_g_col_spec(tq, nq, 0, G), _g_kv_spec(S, 0, G), _g_kv_spec(S, 0, G)],
               scratch_shapes=[pltpu.VMEM((NS, tq, LANES), F32)] * 3,
               semantics=("arbitrary", "arbitrary", "arbitrary"), vmem=VMEM_BIG)(qa, ka, va, doa, rt)


MEM_GROUP = N_MEM_HEADS // 2


def _mem_fwd(qa, kva, *, name, B, S, NM, q_cb):
    G = MEM_GROUP
    NS = 2 * G
    tq = min(MEM_Q_TILE, S)
    nq = S // tq

    def body(q_ref, k_ref, v_ref, o_ref, lse_ref):
        lane, qh = _streams(q_ref, G, SCALE)
        kblk = [k_ref[:, _lanes(g)] for g in range(G)]
        vblk = [v_ref[:, _lanes(g)] for g in range(G)]
        ss = [lax.dot_general(qh[st], kblk[st // 2], NT, preferred_element_type=F32) for st in range(NS)]
        pb, ls = [], []
        for st in range(NS):
            m = jnp.max(ss[st], axis=-1, keepdims=True)
            p = jnp.exp(ss[st] - m)
            l = jnp.sum(p, axis=-1, keepdims=True)
            lse_ref[0, st] = m + jnp.log(l)
            pb.append(p.astype(BF16))
            ls.append(l)
        pvs = [jnp.dot(pb[st], vblk[st // 2], preferred_element_type=F32) for st in range(NS)]
        for g in range(G):
            o_ref[:, _lanes(g)] = jnp.where(lane < HEAD_DIM, pvs[2 * g] / ls[2 * g],
                                            pvs[2 * g + 1] / ls[2 * g + 1]).astype(BF16)

    return _pc(body, name=name, out_shape=[_sds((B * S, G * LANES), BF16), _sds((B, NS, S, 1), F32)],
               grid=(B, 1, nq),
               in_specs=[_g_col_spec(tq, nq, q_cb, G), _g_kv_spec(NM, 0, G), _g_kv_spec(NM, G, G)],
               out_specs=[_g_col_spec(tq, nq, 0, G), _g_stat_col_spec(tq, G)],
               semantics=("arbitrary", "arbitrary", "arbitrary"), vmem=VMEM_BIG)(qa, kva, kva)


def _mem_bwd(qa, kva, doa, oa, lse, *, name, B, S, NM, q_cb, do_cb):
    G = MEM_GROUP
    NS = 2 * G
    tq = min(MEM_Q_TILE, S)
    nq = S // tq

    def body(q_ref, k_ref, v_ref, do_ref, o_ref, lse_ref, dq_ref, dk_ref, dv_ref):
        @pl.when(pl.program_id(2) == 0)
        def _():
            dk_ref[...] = jnp.zeros_like(dk_ref)
            dv_ref[...] = jnp.zeros_like(dv_ref)

        lane, qh = _streams(q_ref, G, SCALE)
        _, doh = _streams(do_ref, G)
        kblk = [k_ref[:, _lanes(g)] for g in range(G)]
        vblk = [v_ref[:, _lanes(g)] for g in range(G)]
        prod = [do_ref[:, _lanes(g)].astype(F32) * o_ref[:, _lanes(g)].astype(F32) for g in range(G)]
        ss = [lax.dot_general(qh[st], kblk[st // 2], NT, preferred_element_type=F32) for st in range(NS)]
        dps = [lax.dot_general(doh[st], vblk[st // 2], NT, preferred_element_type=F32) for st in range(NS)]
        dsb, pb = [], []
        for st in range(NS):
            delta = jnp.sum(jnp.where(_head_mask(lane, st % 2), prod[st // 2], 0.0), axis=-1, keepdims=True)
            p = jnp.exp(ss[st] - lse_ref[0, st])
            dsb.append((p * (dps[st] - delta)).astype(BF16))
            pb.append(p.astype(BF16))
        dks = [lax.dot_general(dsb[st], qh[st], TN, preferred_element_type=F32) for st in range(NS)]
        dvs = [lax.dot_general(pb[st], doh[st], TN, preferred_element_type=F32) for st in range(NS)]
        dqs = [jnp.dot(dsb[st], kblk[st // 2], preferred_element_type=F32) for st in range(NS)]
        for g in range(G):
            dk_ref[:, _lanes(g)] += dks[2 * g] + dks[2 * g + 1]
            dv_ref[:, _lanes(g)] += dvs[2 * g] + dvs[2 * g + 1]
            dq_ref[:, _lanes(g)] = (jnp.where(lane < HEAD_DIM, dqs[2 * g], dqs[2 * g + 1]) * SCALE).astype(BF16)

    return _pc(body, name=name,
               out_shape=[_sds((B * S, G * LANES), BF16), _sds((B * NM, G * LANES), F32), _sds((B * NM, G * LANES), F32)],
               grid=(B, 1, nq),
               in_specs=[_g_col_spec(tq, nq, q_cb, G), _g_kv_spec(NM, 0, G), _g_kv_spec(NM, G, G),
                         _g_col_spec(tq, nq, do_cb, G), _g_col_spec(tq, nq, 0, G), _g_stat_col_spec(tq, G)],
               out_specs=[_g_col_spec(tq, nq, 0, G), _g_kv_spec(NM, 0, G), _g_kv_spec(NM, 0, G)],
               semantics=("arbitrary", "arbitrary", "arbitrary"), vmem=VMEM_BIG)(qa, kva, kva, doa, oa, lse)


def _sigmoid(v):
    return 0.5 * jnp.tanh(0.5 * v) + 0.5


def _shift_rows(cur, halo_ref, first, rows_idx, k):
    out = pltpu.roll(cur, k, 0)
    top = out[0:8, :]
    for r in range(k):
        hr = halo_ref.shape[0] - k + r
        edge = jnp.where(first, 0.0, halo_ref[hr:hr + 1, :])
        top = jnp.where(rows_idx[0:8, :] == r, edge, top)
    return jnp.concatenate([top, out[8:, :]], axis=0)


def _shift_rows_up(cur, halo_ref, last, rows_idx, k, ts):
    out = pltpu.roll(cur, ts - k, 0)
    bottom = out[ts - 8:, :]
    for r in range(k):
        edge = jnp.where(last, 0.0, halo_ref[r:r + 1, :])
        bottom = jnp.where(rows_idx[0:8, :] == 8 - k + r, edge, bottom)
    return jnp.concatenate([out[:ts - 8, :], bottom], axis=0)


def _ffn_up_gate(x, g, w, cw, cb, *, name, S):
    T, D = x.shape
    F = w.shape[1] // 2
    tm = min(1024, S)
    tn = 256
    nj = F // tn
    tiles_per_seq = S // tm
    halo = 16

    def body(x_ref, xh_ref, g_ref, wg_ref, wv_ref, cwg_ref, cwv_ref, cbg_ref, cbv_ref,
             uc_ref, ub_ref, a_ref, hout_ref, h_ref, hh_ref, eg_ref, ev_ref):
        first = lax.rem(pl.program_id(0), tiles_per_seq) == 0

        @pl.when(pl.program_id(1) == 0)
        def _():
            def norm(v):
                r = lax.rsqrt(jnp.mean(v * v, axis=-1, keepdims=True) + EPS)
                return ((v * r) * g_ref[...]).astype(BF16)
            h = norm(x_ref[...])
            h_ref[...] = h
            hout_ref[...] = h
            hh_ref[...] = norm(xh_ref[...])

        h = h_ref[...]
        rows_idx = lax.broadcasted_iota(jnp.int32, (tm, tn), 0)
        uc = []
        for half, (w_ref, cw_ref, cb_ref, e_ref) in enumerate(((wg_ref, cwg_ref, cbg_ref, eg_ref),
                                                               (wv_ref, cwv_ref, cbv_ref, ev_ref))):
            acc = jnp.dot(h, w_ref[...], preferred_element_type=F32)
            e_ref[...] = jnp.dot(hh_ref[...], w_ref[...], preferred_element_type=F32)
            ub_ref[half] = acc.astype(BF16)
            m1 = _shift_rows(acc, e_ref, first, rows_idx, 1)
            m2 = _shift_rows(acc, e_ref, first, rows_idx, 2)
            uc.append(cb_ref[...] + cw_ref[0:1, :] * m2 + cw_ref[1:2, :] * m1 + cw_ref[2:3, :] * acc)
            uc_ref[half] = uc[half]
        a_ref[...] = (uc[0] * _sigmoid(uc[0]) * uc[1]).astype(BF16)

    in_specs = [pl.BlockSpec((tm, D), lambda i, j: (i, 0)),
                pl.BlockSpec((halo, D), lambda i, j: (jnp.maximum(i * (tm // halo) - 1, 0), 0)),
                pl.BlockSpec((1, D), lambda i, j: (0, 0)),
                pl.BlockSpec((D, tn), lambda i, j: (0, j)), pl.BlockSpec((D, tn), lambda i, j: (0, j + nj)),
                pl.BlockSpec((3, tn), lambda i, j: (0, j)), pl.BlockSpec((3, tn), lambda i, j: (0, j + nj)),
                pl.BlockSpec((1, tn), lambda i, j: (0, j)), pl.BlockSpec((1, tn), lambda i, j: (0, j + nj))]
    return _pc(body, name=name,
               out_shape=[_sds((2, T, F), F32), _sds((2, T, F), BF16), _sds((T, F), BF16), _sds((T, D), BF16)],
               grid=(T // tm, nj), in_specs=in_specs,
               out_specs=[pl.BlockSpec((2, tm, tn), lambda i, j: (0, i, j)), pl.BlockSpec((2, tm, tn), lambda i, j: (0, i, j)),
                          pl.BlockSpec((tm, tn), lambda i, j: (i, j)), pl.BlockSpec((tm, D), lambda i, j: (i, 0))],
               scratch_shapes=[pltpu.VMEM((tm, D), BF16), pltpu.VMEM((halo, D), BF16),
                               pltpu.VMEM((halo, tn), F32), pltpu.VMEM((halo, tn), F32)],
               semantics=("arbitrary", "arbitrary"), vmem=VMEM_BIG)(x, x, g.reshape(1, D), w, w, cw, cw, cb, cb)


def _conv_gate_bwd(da, uc, ub, cw, *, name, B, S):
    F = uc.shape[2]
    tf = F // 2
    ts = min(512, S)
    ns, nf = S // ts, F // tf

    def body(da_ref, uc_ref, ub_ref, wg_ref, wv_ref, dug_ref, duv_ref, pg_ref, pv_ref, nxt_g, nxt_v):
        last = pl.program_id(2) == 0

        @pl.when(jnp.logical_and(pl.program_id(1) == 0, last))
        def _():
            pg_ref[...] = jnp.zeros_like(pg_ref)
            pv_ref[...] = jnp.zeros_like(pv_ref)

        rows_idx = lax.broadcasted_iota(jnp.int32, (ts, tf), 0)
        ucg, ucv = uc_ref[0], uc_ref[1]
        sg = _sigmoid(ucg)
        dav = da_ref[...]
        d_v = dav * (ucg * sg)
        d_g = dav * ucv * (sg * (1.0 + ucg * (1.0 - sg)))
        for half, (o_ref, p_ref, d, w_ref, nxt) in enumerate(((dug_ref, pg_ref, d_g, wg_ref, nxt_g),
                                                               (duv_ref, pv_ref, d_v, wv_ref, nxt_v))):
            p1 = _shift_rows_up(d, nxt, last, rows_idx, 1, ts)
            p2 = _shift_rows_up(d, nxt, last, rows_idx, 2, ts)
            o_ref[...] = (w_ref[2:3, :] * d + w_ref[1:2, :] * p1 + w_ref[0:1, :] * p2).astype(BF16)
            nxt[...] = d[0:8, :]
            uh = ub_ref[half].astype(F32)
            for k, dk in enumerate((p2, p1, d)):
                p_ref[k:k + 1, :] += jnp.sum(dk * uh, axis=0, keepdims=True)
            p_ref[3:4, :] += jnp.sum(d, axis=0, keepdims=True)

    row = pl.BlockSpec((ts, tf), lambda j, b, r: (b * ns + ns - 1 - r, j))
    both = pl.BlockSpec((2, ts, tf), lambda j, b, r: (0, b * ns + ns - 1 - r, j))
    par = pl.BlockSpec((8, tf), lambda j, b, r: (0, j))
    return _pc(body, name=name,
               out_shape=[_sds((B * S, F), BF16), _sds((B * S, F), BF16), _sds((8, F), F32), _sds((8, F), F32)],
               grid=(nf, B, ns),
               in_specs=[row, both, both, pl.BlockSpec((3, tf), lambda j, b, r: (0, j)),
                         pl.BlockSpec((3, tf), lambda j, b, r: (0, j + nf))],
               out_specs=[row, row, par, par],
               scratch_shapes=[pltpu.VMEM((8, tf), F32), pltpu.VMEM((8, tf), F32)],
               semantics=("arbitrary", "arbitrary", "arbitrary"), vmem=VMEM_BIG)(da, uc, ub, cw, cw)


def _adamw(w, g, m, v, *, name):
    rows, cols = w.shape
    tr = rows
    while tr * cols * 4 > 2 ** 20 and tr % 16 == 0:
        tr //= 2

    def body(w_ref, g_ref, m_ref, v_ref, d_ref, nm_ref, nv_ref):
        gv = g_ref[...]
        m_new = ADAM_B1 * m_ref[...] + (1.0 - ADAM_B1) * gv
        v_new = ADAM_B2 * v_ref[...] + (1.0 - ADAM_B2) * (gv * gv)
        m_hat = m_new / (1.0 - ADAM_B1 ** ADAM_STEP)
        v_hat = v_new / (1.0 - ADAM_B2 ** ADAM_STEP)
        d_ref[...] = -ADAM_LR * (m_hat / (jnp.sqrt(v_hat) + ADAM_EPS) + ADAM_WD * w_ref[...])
        nm_ref[...] = m_new
        nv_ref[...] = v_new

    blk = pl.BlockSpec((tr, cols), lambda i: (i, 0))
    return _pc(body, name=name, out_shape=[_sds((rows, cols), F32)] * 3, grid=(rows // tr,),
               in_specs=[blk] * 4, out_specs=[blk] * 3, semantics=("arbitrary",))(w, g, m, v)


def _my_pos():
    return lax.axis_index("x"), lax.axis_index("y"), lax.axis_index("c")


_HBM = pl.BlockSpec(memory_space=pltpu.HBM)
_SEM = pl.BlockSpec(memory_space=pltpu.SEMAPHORE)
_EFFECT = pltpu.SideEffectType.DATAFLOW_SIDE_EFFECTING


def _peers(same_core=False):
    x, y, c = _my_pos()
    out = []
    for k in ((2, 4, 6) if same_core else range(1, N_DEV)):
        px, py, pc = x ^ ((k >> 2) & 1), y ^ ((k >> 1) & 1), c ^ (k & 1)
        out.append(((px, py, pc), 4 * px + 2 * py + pc))
    return out


def _scatter_start(srcs, slot_of, *, name, same_core=False):
    n = len(srcs)
    npr = 3 if same_core else N_DEV - 1
    lands = [lax.empty((N_DEV,) + slot_of(s, 0, shape_only=True), s.dtype) for s in srcs]

    def body(*refs):
        src_refs, land_refs = refs[:n], refs[n:2 * n]
        send_sems, recv_sems = refs[2 * n], refs[2 * n + 1]
        token = refs[-1]
        x, y, c = _my_pos()
        me = 4 * x + 2 * y + c
        for a in range(n):
            for k, (peer, peer_idx) in enumerate(_peers(same_core)):
                pltpu.make_async_remote_copy(
                    src_ref=slot_of(src_refs[a], peer_idx), dst_ref=land_refs[a].at[me],
                    send_sem=send_sems.at[a * npr + k], recv_sem=recv_sems.at[a * npr + k],
                    device_id=peer, device_id_type=MESH).start()
        token[...] = jnp.zeros_like(token)

    hbm = lambda a: pltpu.HBM(a.shape, a.dtype)
    args = [pltpu.with_memory_space_constraint(a, pltpu.HBM) for a in list(srcs) + lands]
    outs = pl.pallas_call(
        body, name=name,
        out_shape=(pltpu.SemaphoreType.DMA((npr * n,)), pltpu.SemaphoreType.DMA((npr * n,)),
                   *[hbm(a) for a in srcs], *[hbm(a) for a in lands], _sds((8, LANES), F32)),
        in_specs=[_HBM] * (2 * n),
        out_specs=(_SEM, _SEM, *([_HBM] * (2 * n)), pl.BlockSpec(memory_space=pltpu.VMEM)),
        input_output_aliases={a: 2 + a for a in range(2 * n)},
        compiler_params=pltpu.CompilerParams(has_side_effects=_EFFECT))(*args)
    return outs[0], outs[1], list(outs[2:2 + n]), list(outs[2 + n:2 + 2 * n]), outs[-1]


def _scatter_wait(send_sems, recv_sems, srcs, lands, slot_of, after, *, name, first=0, same_core=False):
    n = len(srcs)
    npr = 3 if same_core else N_DEV - 1

    def body(*refs):
        src_refs, land_refs = refs[:n], refs[n:2 * n]
        ssem, rsem = refs[2 * n], refs[2 * n + 1]
        x, y, c = _my_pos()
        me = 4 * x + 2 * y + c
        for a in range(n):
            for k, (peer, peer_idx) in enumerate(_peers(same_core)):
                cp = pltpu.make_async_remote_copy(
                    src_ref=slot_of(src_refs[a], peer_idx), dst_ref=land_refs[a].at[me],
                    send_sem=ssem.at[(first + a) * npr + k], recv_sem=rsem.at[(first + a) * npr + k],
                    device_id=peer, device_id_type=MESH)
                cp.wait_send()
                cp.wait_recv()

    hbm = lambda a: pltpu.HBM(a.shape, a.dtype)
    outs = pl.pallas_call(
        body, name=name, out_shape=tuple(hbm(a) for a in list(srcs) + list(lands)),
        in_specs=[_HBM] * (2 * n) + [_SEM, _SEM, pl.BlockSpec(memory_space=pl.ANY)],
        out_specs=tuple([_HBM] * (2 * n)), input_output_aliases={a: a for a in range(2 * n)},
        compiler_params=pltpu.CompilerParams(has_side_effects=_EFFECT))(*srcs, *lands, send_sems, recv_sems, after)
    return list(outs[:n]), list(outs[n:])


def _sibling_start(lands, *, name):
    n = len(lands)

    def body(*refs):
        land_refs = refs[:n]
        send_sems, recv_sems = refs[n], refs[n + 1]
        token = refs[-1]
        x, y, c = _my_pos()
        for a in range(n):
            for k in range(4):
                pltpu.make_async_remote_copy(
                    src_ref=land_refs[a].at[2 * k + c], dst_ref=land_refs[a].at[2 * k + c],
                    send_sem=send_sems.at[a * 4 + k], recv_sem=recv_sems.at[a * 4 + k],
                    device_id=(x, y, 1 - c), device_id_type=MESH).start()
        token[...] = jnp.zeros_like(token)

    hbm = lambda a: pltpu.HBM(a.shape, a.dtype)
    outs = pl.pallas_call(
        body, name=name,
        out_shape=(pltpu.SemaphoreType.DMA((4 * n,)), pltpu.SemaphoreType.DMA((4 * n,)),
                   *[hbm(a) for a in lands], _sds((8, LANES), F32)),
        in_specs=[_HBM] * n,
        out_specs=(_SEM, _SEM, *([_HBM] * n), pl.BlockSpec(memory_space=pltpu.VMEM)),
        input_output_aliases={a: 2 + a for a in range(n)},
        compiler_params=pltpu.CompilerParams(has_side_effects=_EFFECT))(
            *[pltpu.with_memory_space_constraint(a, pltpu.HBM) for a in lands])
    return outs[0], outs[1], list(outs[2:2 + n]), outs[-1]


def _sibling_wait(send_sems, recv_sems, lands, after, *, name):
    n = len(lands)

    def body(*refs):
        land_refs = refs[:n]
        ssem, rsem = refs[n], refs[n + 1]
        x, y, c = _my_pos()
        for a in range(n):
            for k in range(4):
                cp = pltpu.make_async_remote_copy(
                    src_ref=land_refs[a].at[2 * k + c], dst_ref=land_refs[a].at[2 * k + 1 - c],
                    send_sem=ssem.at[a * 4 + k], recv_sem=rsem.at[a * 4 + k],
                    device_id=(x, y, 1 - c), device_id_type=MESH)
                cp.wait_send()
                cp.wait_recv()

    hbm = lambda a: pltpu.HBM(a.shape, a.dtype)
    outs = pl.pallas_call(
        body, name=name, out_shape=tuple(hbm(a) for a in lands),
        in_specs=[_HBM] * n + [_SEM, _SEM, pl.BlockSpec(memory_space=pl.ANY)],
        out_specs=tuple([_HBM] * n), input_output_aliases={a: a for a in range(n)},
        compiler_params=pltpu.CompilerParams(has_side_effects=_EFFECT))(*lands, send_sems, recv_sems, after)
    return list(outs)


def _whole(a, peer_idx, shape_only=False):
    return a.shape if shape_only else a


def _slot(a, peer_idx, shape_only=False):
    return a.shape[1:] if shape_only else a.at[peer_idx]


def _sum_slots(a, *, name, tr=None):
    rows, cols = a.shape[1], a.shape[2]
    if tr is None:
        tr = rows
        while N_DEV * tr * cols * a.dtype.itemsize > 3 * 2 ** 20 and tr % 32 == 0:
            tr //= 2

    def body(a_ref, o_ref):
        acc = a_ref[0].astype(F32)
        for j in range(1, N_DEV):
            acc = acc + a_ref[j].astype(F32)
        o_ref[...] = acc

    return _pc(body, name=name, out_shape=_sds((rows, cols), F32), grid=(rows // tr,),
               in_specs=[pl.BlockSpec((N_DEV, tr, cols), lambda i: (0, i, 0))],
               out_specs=pl.BlockSpec((tr, cols), lambda i: (i, 0)), semantics=("arbitrary",), vmem=VMEM_BIG)(a)


def _to_slots(full, kind):
    if kind == "rows2":
        r, c = full.shape
        return full.reshape(N_DEV, r // N_DEV, c)
    if kind == "cols2":
        r, c = full.shape
        return full.reshape(r, N_DEV, c // N_DEV).transpose(1, 0, 2)
    if kind == "rows3":
        l, r, c = full.shape
        return full.reshape(l, N_DEV, r // N_DEV, c).transpose(1, 0, 2, 3)
    if kind == "cols3":
        l, r, c = full.shape
        return full.reshape(l, r, N_DEV, c // N_DEV).transpose(2, 0, 1, 3)
    raise ValueError(kind)


def _from_slots(slots, kind):
    if kind == "rows2":
        _, r, c = slots.shape
        return slots.reshape(N_DEV * r, c)
    if kind == "cols2":
        _, r, c = slots.shape
        return slots.transpose(1, 0, 2).reshape(r, N_DEV * c)
    if kind == "rows3":
        _, l, r, c = slots.shape
        return slots.transpose(1, 0, 2, 3).reshape(l, N_DEV * r, c)
    if kind == "cols3":
        _, l, r, c = slots.shape
        return slots.transpose(1, 2, 0, 3).reshape(l, r, N_DEV * c)
    raise ValueError(kind)


BIG = (("w_in_a", "rows2"), ("w_in_b", "rows2"), ("w_kv", "cols2"), ("w_memkv", "rows3"),
       ("w_out", "rows3"), ("w_up", "cols3"), ("w_down", "rows3"))


def _round_up(n, m):
    return -(-n // m) * m


def _pad_rows(a, rows, axis):
    pad = [(0, 0)] * a.ndim
    pad[axis] = (0, rows - a.shape[axis])
    return jnp.pad(a, pad)


def kernel(x, mem, ln_mix_g, w_in_a, b_f_a, w_in_b, ln_kv_g, w_kv, ln_mem_g, w_memkv, w_out, ln_ffn_g, w_up, conv_w, conv_b, w_down, final_g, loss_target, m_ln_mix_g, m_w_in_a, m_b_f_a, m_w_in_b, m_ln_kv_g, m_w_kv, m_ln_mem_g, m_w_memkv, m_w_out, m_ln_ffn_g, m_w_up, m_conv_w, m_conv_b, m_w_down, m_final_g, v_ln_mix_g, v_w_in_a, v_b_f_a, v_w_in_b, v_ln_kv_g, v_w_kv, v_ln_mem_g, v_w_memkv, v_w_out, v_ln_ffn_g, v_w_up, v_conv_w, v_conv_b, v_w_down, v_final_g):
    B, S, D = x.shape
    NM = mem.shape[1]
    T = B * S
    F = w_down.shape[1] * N_DEV
    my_idx = 4 * lax.axis_index("x") + 2 * lax.axis_index("y") + lax.axis_index("c")

    shards = {"w_in_a": w_in_a[0], "w_in_b": w_in_b[0], "w_kv": w_kv, "w_memkv": w_memkv, "w_out": w_out,
              "w_up": w_up, "w_down": w_down}
    moms = {"w_in_a": (m_w_in_a[0], v_w_in_a[0]), "w_in_b": (m_w_in_b[0], v_w_in_b[0]), "w_kv": (m_w_kv, v_w_kv),
            "w_memkv": (m_w_memkv, v_w_memkv), "w_out": (m_w_out, v_w_out), "w_up": (m_w_up, v_w_up),
            "w_down": (m_w_down, v_w_down)}

    groups = [("a1", [("w_in_a", None)]),
              ("a2", [("w_memkv", None), ("w_out", None), ("conv_w", None)]),
              ("b0", [("w_up", 0), ("w_down", 0)]), ("a3", [("w_in_b", None), ("w_kv", None)]),
              ("b1", [("w_up", 1), ("w_down", 1)])]
    sources = dict(shards, conv_w=conv_w)
    span = {}
    for sname, batch in (("first", groups[:1]), ("rest", groups[1:])):
        srcs = []
        for gname, members in batch:
            span[gname] = (sname, len(srcs), len(members))
            for n, layer in members:
                a = sources[n] if layer is None else sources[n][layer]
                srcs.append(a if n == "conv_w" else a.astype(BF16))
        span[sname] = _scatter_start(srcs, _whole, name=f"gather_start_{sname}", same_core=sname == "first")
    token = span["rest"][4]

    def gathered(gname, after):
        sname, lo, cnt = span[gname]
        ssem, rsem, thru, lands, _ = span[sname]
        two_level = sname == "first"
        thru, lands = _scatter_wait(ssem, rsem, thru[lo:lo + cnt], lands[lo:lo + cnt], _whole, after,
                                    name=f"gather_wait_{gname}", first=lo, same_core=two_level)
        lands = [lax.dynamic_update_index_in_dim(land, s, my_idx, 0) for land, s in zip(lands, thru)]
        if two_level:
            s2, r2, lands, tok2 = _sibling_start(lands, name=f"gather_pass_{gname}")
            lands = _sibling_wait(s2, r2, lands, tok2, name=f"gather_pass_wait_{gname}")
        return lands

    full = {}
    (g_wa,) = gathered("a1", token)
    full["w_in_a"] = _from_slots(g_wa, "rows2")

    wa = full["w_in_a"]
    n_qkv = 3 * MAIN_W
    wa = jnp.concatenate([wa[:, :n_qkv], wa[:, n_qkv + N_MAIN_HEADS:], wa[:, n_qkv:n_qkv + N_MAIN_HEADS],
                          jnp.zeros((D, LANES - N_MAIN_HEADS), BF16)], axis=1)
    n_main = n_qkv + MEM_W
    full["w_up"], full["w_down"] = {}, {}
    b_f =_pad_rows(b_f_a.reshape(1, N_MAIN_HEADS), LANES, 1)

    x2d = x.reshape(T, D)
    mem2d = mem.reshape(B * NM, D)
    tgt2d = loss_target.reshape(T, D)
    PM, PX = N_MAIN_HEADS // 2, N_MEM_HEADS // 2

    def stats_to_heads(c2d):
        c = c2d.reshape(B, S, LANES)[:, :, :N_MAIN_HEADS].transpose(0, 2, 1)
        return c[:, :, None, :]

    def mem_kv(layer):
        return _mm_fwd(mem2d, full["w_memkv"][layer], name=f"memkv{layer}", tm=B * NM, tn=2 * MEM_W,
                       out_dtype=BF16, g=ln_mem_g[layer], save_h=True)

    def conv_ffn_fwd(xin, layer):
        uc, ub, a, h = _ffn_up_gate(xin, ln_ffn_g[layer], full["w_up"][layer], conv_w_full[layer],
                                    conv_b[layer].reshape(1, 2 * F), name=f"ffn_up{layer}", S=S)
        xo = _mm_fwd(a, full["w_down"][layer], name=f"ffn_down{layer}", tm=min(1024, T), tn=1024, out_dtype=F32, res=xin)
        return xo, (uc, ub, h, a)

    proj_a, h_mix0 = _mm_fwd(x2d, wa, name="in_proj_a", tm=min(1024, T), tn=2560, out_dtype=BF16, g=ln_mix_g[0],
                             ncols=n_main, save_h=True)
    f_logit = _mm_fwd(x2d, wa, name="in_proj_f", tm=min(1024, T), tn=LANES, out_dtype=F32, g=ln_mix_g[0],
                      col0=n_main // LANES, ncols=LANES)
    c2d = _forget_cumsum(f_logit, b_f, B=B, S=S, name="forget_cumsum")
    cr = stats_to_heads(c2d)
    o_main0, lse0 = _fox_fwd_g(proj_a, proj_a, proj_a, cr, name="fox_fwd", B=B, S=S, P=PM, q_cb=0, k_cb=PM, v_cb=2 * PM,
                               G=FWD_HEAD_GROUP)
    g_wmem, g_wout, g_cw = gathered("a2", lse0)
    full["w_memkv"] = _from_slots(g_wmem, "rows3")
    full["w_out"] = _from_slots(g_wout, "rows3")
    conv_w_full = _from_slots(g_cw, "cols3")
    memkv0, h_mem0 = mem_kv(0)
    o_mem0, lse_m0 = _mem_fwd(proj_a, memkv0, name="mem_fwd0", B=B, S=S, NM=NM, q_cb=3 * PM)
    o_cat0 = jnp.concatenate([o_main0, o_mem0], axis=1)
    x1 = _mm_fwd(o_cat0, full["w_out"][0], name="out_proj0", tm=min(1024, T), tn=1024, out_dtype=F32, res=x2d)
    g_up, g_dn = gathered("b0", x1)
    full["w_up"][0], full["w_down"][0] = _from_slots(g_up, "cols2"), _from_slots(g_dn, "rows2")
    x2, ffn_saved0 = conv_ffn_fwd(x1, 0)
    g_wb, g_wkv = gathered("a3", x2)
    wb, wkv = _from_slots(g_wb, "rows2"), _from_slots(g_wkv, "cols2")
    kv, h_kv =_mm_fwd(x2, wkv, name="kv_proj", tm=min(1024, T), tn=1536, out_dtype=BF16, g=ln_kv_g, save_h=True)
    proj_b, h_mix1 = _mm_fwd(x2, wb, name="in_proj_b", tm=min(1024, T), tn=1024, out_dtype=BF16, g=ln_mix_g[1],
                             save_h=True)
    o_main1, rt1 = _sb_fwd_g(proj_b, kv, kv, name="sb_fwd", B=B, S=S, P=PM, q_cb=0, k_cb=0, v_cb=PM,
                             G=FWD_HEAD_GROUP)
    memkv1, h_mem1 = mem_kv(1)
    o_mem1, lse_m1 = _mem_fwd(proj_b, memkv1, name="mem_fwd1", B=B, S=S, NM=NM, q_cb=PM)
    o_cat1 = jnp.concatenate([o_main1, o_mem1], axis=1)
    x3 = _mm_fwd(o_cat1, full["w_out"][1], name="out_proj1", tm=min(1024, T), tn=1024, out_dtype=F32, res=x2)
    g_up, g_dn = gathered("b1", x3)
    full["w_up"][1], full["w_down"][1] = _from_slots(g_up, "cols2"), _from_slots(g_dn, "rows2")
    x4, ffn_saved1 = conv_ffn_fwd(x3, 1)
    dx4, dg_final, loss_part = _loss_head(x4, final_g, tgt2d, name="loss_head")

    grads = {}
    small = {}
    reduce_groups = []

    def start_reduce(gname, keys, kinds):
        slots = [_to_slots(grads[k], kind) for k, kind in zip(keys, kinds)]
        ssem, rsem, thru, lands, tok = _scatter_start(slots, _slot, name=f"reduce_start_{gname}")
        reduce_groups.append((gname, keys, ssem, rsem, thru, lands))
        return tok[0, 0]

    def conv_ffn_bwd(dxo, xin, saved, layer):
        uc, ub, h, a = saved
        w_dn = full["w_down"][layer]
        da = _mm_nt(dxo, w_dn, name=f"d_act{layer}", tm=min(1024, T), tn=F // 2, out_dtype=F32)
        grads[("w_down", layer)] = _wgrad(a, dxo, f"g_w_down{layer}")
        cwl = conv_w_full[layer]
        du_g, du_v, p_g, p_v = _conv_gate_bwd(da, uc, ub, cwl, name=f"conv_bwd{layer}", B=B, S=S)
        small[("conv_w", layer)] = jnp.concatenate([p_g[0:3], p_v[0:3]], axis=1)
        small[("conv_b", layer)] = jnp.concatenate([p_g[3], p_v[3]], axis=0)
        grads[("w_up", layer)] = jnp.concatenate(
            [_wgrad(h, du_g, f"g_w_up_gate{layer}"), _wgrad(h, du_v, f"g_w_up_val{layer}")], axis=1)
        tok = start_reduce(f"ffn{layer}", [("w_down", layer), ("w_up", layer)], ["rows2", "cols2"])
        dxi, dg = _mm_nt_rmsbwd([(du_g, 0), (du_v, 1)], full["w_up"][layer], xin, ln_ffn_g[layer] + tok,
                                name=f"d_ffn_in{layer}", dres=dxo)
        small[("ln_ffn_g", layer)] = dg[0]
        return dxi

    def mem_bwd(proj, q_cb, memkv, h_mem, do_cat, o_mem, lse_m, layer):
        dqm, dmk, dmv = _mem_bwd(proj, memkv, do_cat, o_mem, lse_m, name=f"mem_bwd{layer}", B=B, S=S, NM=NM,
                                 q_cb=q_cb, do_cb=PM)
        grads[("w_memkv", layer)] = jnp.concatenate(
            [_wgrad(h_mem, dmk, f"g_w_memk{layer}"), _wgrad(h_mem, dmv, f"g_w_memv{layer}")], axis=1)
        _, dg = _mm_nt_rmsbwd([(dmk, 0), (dmv, 1)], full["w_memkv"][layer], mem2d, ln_mem_g[layer],
                              name=f"d_mem_in{layer}", want_dx=False)
        small[("ln_mem_g", layer)] = dg[0]
        return dqm

    dx3 = conv_ffn_bwd(dx4, x3, ffn_saved1, 1)
    do_cat1 = _mm_nt(dx3, full["w_out"][1], name="d_o_cat1", tm=min(1024, T), tn=1024, out_dtype=BF16)
    grads[("w_out", 1)] = _wgrad(o_cat1, dx3, "g_w_out1")
    dq1, dk1, dv1 = _sb_bwd_g(proj_b, kv, kv, do_cat1, rt1, name="sb_bwd", B=B, S=S, P=PM, q_cb=0, k_cb=0, v_cb=PM,
                            do_cb=0)
    dqm1 = mem_bwd(proj_b, PM, memkv1, h_mem1, do_cat1, o_mem1, lse_m1, 1)
    grads["w_in_b"] = jnp.concatenate([_wgrad(h_mix1, dq1, "g_w_in_b_q"), _wgrad(h_mix1, dqm1, "g_w_in_b_m")], axis=1)
    grads["w_kv"] = jnp.concatenate([_wgrad(h_kv, dk1, "g_w_kv_k"), _wgrad(h_kv, dv1, "g_w_kv_v")], axis=1)
    tok = start_reduce("mix1", [("w_out", 1), "w_in_b", "w_kv", ("w_memkv", 1)], ["rows2", "rows2", "cols2", "rows2"])
    dx2, dg = _mm_nt_rmsbwd([(dq1, 0), (dqm1, MAIN_W // MEM_W)], wb, x2, ln_mix_g[1] + tok, name="d_mix_in1", dres=dx3)
    small[("ln_mix_g", 1)] = dg[0]
    dx2, dg = _mm_nt_rmsbwd([(dk1, 0), (dv1, 1)], wkv, x2, ln_kv_g, name="d_kv_in", dres=dx2)
    small["ln_kv_g"] = dg[0]
    dx1 = conv_ffn_bwd(dx2, x1, ffn_saved0, 0)
    do_cat0 = _mm_nt(dx1, full["w_out"][0], name="d_o_cat0", tm=min(1024, T), tn=1024, out_dtype=BF16)
    grads[("w_out", 0)] = _wgrad(o_cat0, dx1, "g_w_out0")
    dq0, dk0, dv0, dcs = _fox_bwd_g(proj_a, proj_a, proj_a, do_cat0, lse0, cr, name="fox_bwd", B=B, S=S, P=PM, q_cb=0,
                                  k_cb=PM, v_cb=2 * PM, do_cb=0)
    dqm0 = mem_bwd(proj_a, 3 * PM, memkv0, h_mem0, do_cat0, o_mem0, lse_m0, 0)
    dc2d = _pad_rows(dcs[:, :, 0, :].transpose(0, 2, 1).reshape(T, N_MAIN_HEADS), LANES, 1)
    df, db_f = _forget_cumsum_bwd(dc2d, f_logit, b_f, B=B, S=S, name="forget_cumsum_bwd")
    a_parts = [(dq0, 0), (dk0, 1), (dv0, 2), (dqm0, n_qkv // MEM_W), (df, n_main // LANES)]
    g_wa = jnp.concatenate([_wgrad(h_mix0, p, f"g_w_in_a{k}") for k, (p, _) in enumerate(a_parts)], axis=1)
    grads["w_in_a"] = jnp.concatenate([g_wa[:, :n_qkv], g_wa[:, n_main:n_main + N_MAIN_HEADS], g_wa[:, n_qkv:n_main]],
                                      axis=1)
    tok = start_reduce("mix0", [("w_out", 0), ("w_memkv", 0), "w_in_a"], ["rows2", "rows2", "rows2"])
    dx0, dg = _mm_nt_rmsbwd(a_parts, wa, x2d, ln_mix_g[0] + tok, name="d_mix_in0", dres=dx1)
    small[("ln_mix_g", 0)] = dg[0]
    grad_x = dx0.reshape(B, S, D)

    def both_small(name):
        return jnp.stack([small[(name, 0)], small[(name, 1)]])

    small_list = [("ln_mix_g", both_small("ln_mix_g")), ("b_f_a", db_f[:, :N_MAIN_HEADS]), ("ln_kv_g", small["ln_kv_g"]),
                  ("ln_mem_g", both_small("ln_mem_g")), ("ln_ffn_g", both_small("ln_ffn_g")),
                  ("conv_w", both_small("conv_w")), ("conv_b", both_small("conv_b")), ("final_g", dg_final[0]),
                  ("loss", loss_part[0, :1])]
    sm_rows = []
    for _, a in small_list:
        flat = a.reshape(-1)
        sm_rows.append(_pad_rows(flat, _round_up(flat.size, 8 * LANES), 0).reshape(-1, LANES))
    spack = jnp.concatenate(sm_rows, axis=0)
    s_ssem, s_rsem, s_thru, s_lands, s_tok = _scatter_start([spack], _whole, name="small_start")

    pieces = {}
    for gname, keys, ssem, rsem, thru, lands in reduce_groups:
        thru, lands = _scatter_wait(ssem, rsem, thru, lands, _slot, s_tok, name=f"reduce_wait_{gname}")
        for key, mine, land in zip(keys, thru, lands):
            own = lax.dynamic_index_in_dim(mine, my_idx, 0, keepdims=False)
            land = lax.dynamic_update_index_in_dim(land, own, my_idx, 0)
            tag = key if isinstance(key, str) else f"{key[0]}{key[1]}"
            pieces[key] = _sum_slots(land, name=f"sum_{tag}")

    red = {}
    for n in ("w_in_a", "w_in_b", "w_kv"):
        red[n] = pieces[n].reshape(shards[n].shape)
    for n in ("w_memkv", "w_out", "w_up", "w_down"):
        red[n] = jnp.stack([pieces[(n, 0)], pieces[(n, 1)]])

    weights = {"ln_mix_g": ln_mix_g, "w_in_a": w_in_a, "b_f_a": b_f_a, "w_in_b": w_in_b, "ln_kv_g": ln_kv_g,
               "w_kv": w_kv, "ln_mem_g": ln_mem_g, "w_memkv": w_memkv, "w_out": w_out, "ln_ffn_g": ln_ffn_g,
               "w_up": w_up, "conv_w": conv_w, "conv_b": conv_b, "w_down": w_down, "final_g": final_g}
    m_in = {"ln_mix_g": m_ln_mix_g, "w_in_a": m_w_in_a, "b_f_a": m_b_f_a, "w_in_b": m_w_in_b, "ln_kv_g": m_ln_kv_g,
            "w_kv": m_w_kv, "ln_mem_g": m_ln_mem_g, "w_memkv": m_w_memkv, "w_out": m_w_out, "ln_ffn_g": m_ln_ffn_g,
            "w_up": m_w_up, "conv_w": m_conv_w, "conv_b": m_conv_b, "w_down": m_w_down, "final_g": m_final_g}
    v_in = {"ln_mix_g": v_ln_mix_g, "w_in_a": v_w_in_a, "b_f_a": v_b_f_a, "w_in_b": v_w_in_b, "ln_kv_g": v_ln_kv_g,
            "w_kv": v_w_kv, "ln_mem_g": v_ln_mem_g, "w_memkv": v_w_memkv, "w_out": v_w_out, "ln_ffn_g": v_ln_ffn_g,
            "w_up": v_w_up, "conv_w": v_conv_w, "conv_b": v_conv_b, "w_down": v_w_down, "final_g": v_final_g}
    order = list(weights)
    big_names = [n for n, _ in BIG]
    g_out, d_out, nm_out, nv_out = {}, {}, {}, {}

    def update(n):
        w = weights[n]
        cols = w.shape[-1]
        g = red[n].reshape(w.shape)
        d, nm, nv = _adamw(w.reshape(-1, cols), g.reshape(-1, cols), m_in[n].reshape(-1, cols),
                           v_in[n].reshape(-1, cols), name=f"adamw_{n}")
        g_out[n], d_out[n], nm_out[n], nv_out[n] = g, d.reshape(w.shape), nm.reshape(w.shape), nv.reshape(w.shape)

    for n in big_names:
        update(n)
    all_updated = jnp.stack([d_out[n].reshape(-1)[0] for n in big_names])
    s_thru, s_lands = _scatter_wait(s_ssem, s_rsem, s_thru, s_lands, _whole, all_updated, name="small_wait")
    ssum = _sum_slots(lax.dynamic_update_index_in_dim(s_lands[0], s_thru[0], my_idx, 0), name="sum_small")
    off = 0
    for (n, a), rows in zip(small_list, sm_rows):
        red[n] = ssum[off:off + rows.shape[0]].reshape(-1)[:a.size].reshape(a.shape)
        off += rows.shape[0]
    loss = red["loss"][0]
    shard_cols = conv_w.shape[2]
    red["conv_w"] = lax.dynamic_slice_in_dim(red["conv_w"], my_idx * shard_cols, shard_cols, axis=2)
    red["b_f_a"] = red["b_f_a"].reshape(b_f_a.shape)
    update("conv_w")
    small_names = [n for n in order if n not in g_out]

    def pack_small(src):
        rows = []
        for n in small_names:
            flat = src[n].reshape(-1)
            rows.append(_pad_rows(flat, _round_up(flat.size, 8 * LANES), 0).reshape(-1, LANES))
        return jnp.concatenate(rows, axis=0), [r.shape[0] for r in rows]

    red_small = {n: red[n].reshape(weights[n].shape) for n in small_names}
    wp, counts = pack_small(weights)
    gp, _ = pack_small(red_small)
    mp, _ = pack_small(m_in)
    vp, _ = pack_small(v_in)
    dp, nmp, nvp = _adamw(wp, gp, mp, vp, name="adamw_small")
    off = 0
    for n, cnt in zip(small_names, counts):
        shp = weights[n].shape
        size = weights[n].size
        g_out[n] = red_small[n]
        d_out[n] = dp[off:off + cnt].reshape(-1)[:size].reshape(shp)
        nm_out[n] = nmp[off:off + cnt].reshape(-1)[:size].reshape(shp)
        nv_out[n] = nvp[off:off + cnt].reshape(-1)[:size].reshape(shp)
        off += cnt

    return (loss, grad_x, *[g_out[n] for n in order], *[d_out[n] for n in order],
            *[nm_out[n] for n in order], *[nv_out[n] for n in order])
```

```python
import functools

import jax
import jax.numpy as jnp
from jax import lax
from jax.experimental import pallas as pl
from jax.experimental.pallas import tpu as pltpu

F32 = jnp.float32
BF16 = jnp.bfloat16
LANES = 128
HEAD_DIM = 64
N_MAIN_HEADS = 12
N_MEM_HEADS = 4
MAIN_W = N_MAIN_HEADS * HEAD_DIM
MEM_W = N_MEM_HEADS * HEAD_DIM
SCALE = HEAD_DIM ** -0.5
EPS = 1e-6
NEG = -1e30
N_DEV = 8
ATT_TILE = 256
MEM_Q_TILE = 1024
VMEM_BIG = 56 * 2 ** 20
MESH = pl.DeviceIdType.MESH

ADAM_LR = 0.001
ADAM_B1 = 0.9
ADAM_B2 = 0.999
ADAM_EPS = 1e-08
ADAM_WD = 0.01
ADAM_STEP = 10

NT = (((1,), (1,)), ((), ()))
TN = (((0,), (0,)), ((), ()))


def _pc(body, *, name, out_shape, grid=None, in_specs=None, out_specs=None, scratch_shapes=(),
        semantics=None, vmem=None):
    kw = {}
    if grid is not None:
        kw["grid"] = grid
    params = pltpu.CompilerParams(dimension_semantics=semantics, vmem_limit_bytes=vmem)
    return pl.pallas_call(body, name=name, out_shape=out_shape, in_specs=in_specs, out_specs=out_specs,
                          scratch_shapes=list(scratch_shapes), compiler_params=params, **kw)


def _sds(shape, dtype):
    return jax.ShapeDtypeStruct(shape, dtype)


def _mm_fwd(a, w, *, name, tm, tn, out_dtype, g=None, res=None, col0=0, ncols=None, save_h=False):
    m_rows, k = a.shape
    n = w.shape[1] if ncols is None else ncols
    grid = (m_rows // tm, n // tn)
    norm = g is not None

    def body(*refs):
        refs = list(refs)
        a_ref = refs.pop(0)
        g_ref = refs.pop(0) if norm else None
        w_ref = refs.pop(0)
        res_ref = refs.pop(0) if res is not None else None
        o_ref = refs.pop(0)
        hout_ref = refs.pop(0) if save_h else None
        h_ref = refs.pop(0) if norm else None
        if norm:
            @pl.when(pl.program_id(1) == 0)
            def _():
                xv = a_ref[...]
                r = lax.rsqrt(jnp.mean(xv * xv, axis=-1, keepdims=True) + EPS)
                h = ((xv * r) * g_ref[...]).astype(BF16)
                h_ref[...] = h
                if save_h:
                    hout_ref[...] = h
            lhs = h_ref[...]
        else:
            lhs = a_ref[...].astype(BF16)
        acc = jnp.dot(lhs, w_ref[...], preferred_element_type=F32)
        if res is not None:
            acc = acc + res_ref[...]
        o_ref[...] = acc.astype(out_dtype)

    in_specs = [pl.BlockSpec((tm, k), lambda i, j: (i, 0))]
    args = [a]
    if norm:
        in_specs.append(pl.BlockSpec((1, k), lambda i, j: (0, 0)))
        args.append(g.reshape(1, k))
    in_specs.append(pl.BlockSpec((k, tn), lambda i, j: (0, j + col0)))
    args.append(w)
    if res is not None:
        in_specs.append(pl.BlockSpec((tm, tn), lambda i, j: (i, j)))
        args.append(res)
    out_shape = [_sds((m_rows, n), out_dtype)]
    out_specs = [pl.BlockSpec((tm, tn), lambda i, j: (i, j))]
    if save_h:
        out_shape.append(_sds((m_rows, k), BF16))
        out_specs.append(pl.BlockSpec((tm, k), lambda i, j: (i, 0)))
    scratch = [pltpu.VMEM((tm, k), BF16)] if norm else []
    outs = _pc(body, name=name, out_shape=out_shape, grid=grid, in_specs=in_specs, out_specs=out_specs,
               scratch_shapes=scratch, semantics=("arbitrary", "arbitrary"), vmem=VMEM_BIG)(*args)
    return outs if save_h else outs[0]


def _rmsnorm(x, g, *, name):
    m_rows, d = x.shape
    tm = min(1024, m_rows)

    def body(x_ref, g_ref, o_ref):
        xv = x_ref[...]
        r = lax.rsqrt(jnp.mean(xv * xv, axis=-1, keepdims=True) + EPS)
        o_ref[...] = ((xv * r) * g_ref[...]).astype(BF16)

    row = pl.BlockSpec((tm, d), lambda i: (i, 0))
    return _pc(body, name=name, out_shape=_sds((m_rows, d), BF16), grid=(m_rows // tm,),
               in_specs=[row, pl.BlockSpec((1, d), lambda i: (0, 0))], out_specs=row,
               semantics=("arbitrary",), vmem=VMEM_BIG)(x, g.reshape(1, d))


def _mm_nt(a, w, *, name, tm, tn, out_dtype):
    m_rows, k = a.shape
    n = w.shape[0]

    def body(a_ref, w_ref, o_ref):
        acc = lax.dot_general(a_ref[...].astype(BF16), w_ref[...], NT, preferred_element_type=F32)
        o_ref[...] = acc.astype(out_dtype)

    return _pc(body, name=name, out_shape=_sds((m_rows, n), out_dtype), grid=(m_rows // tm, n // tn),
               in_specs=[pl.BlockSpec((tm, k), lambda i, j: (i, 0)), pl.BlockSpec((tn, k), lambda i, j: (j, 0))],
               out_specs=pl.BlockSpec((tm, tn), lambda i, j: (i, j)),
               semantics=("arbitrary", "arbitrary"), vmem=VMEM_BIG)(a, w)


def _mm_tn(a, b, *, name, ta, tn, tt):
    t_rows, ka = a.shape
    n = b.shape[1]
    nt = t_rows // tt

    def body(a_ref, b_ref, o_ref, acc_ref):
        t = pl.program_id(2)

        @pl.when(t == 0)
        def _():
            acc_ref[...] = jnp.zeros_like(acc_ref)

        acc_ref[...] += lax.dot_general(a_ref[...].astype(BF16), b_ref[...].astype(BF16), TN,
                                        preferred_element_type=F32)

        @pl.when(t == nt - 1)
        def _():
            o_ref[...] = acc_ref[...].astype(BF16)

    return _pc(body, name=name, out_shape=_sds((ka, n), BF16), grid=(ka // ta, n // tn, nt),
               in_specs=[pl.BlockSpec((tt, ta), lambda i, j, t: (t, i)),
                         pl.BlockSpec((tt, tn), lambda i, j, t: (t, j))],
               out_specs=pl.BlockSpec((ta, tn), lambda i, j, t: (i, j)),
               scratch_shapes=[pltpu.VMEM((ta, tn), F32)],
               semantics=("arbitrary", "arbitrary", "arbitrary"), vmem=VMEM_BIG)(a, b)


def _wgrad(a, b, name):
    t_rows, ka = a.shape
    n = b.shape[1]
    ta = ka if ka <= 1024 else ka // 2
    tn = n
    while ta * tn * 4 > 6 * 2 ** 20 and tn % 256 == 0:
        tn //= 2
    tt = min(2048, t_rows)
    return _mm_tn(a, b, name=name, ta=ta, tn=tn, tt=tt)


def _mm_nt_rmsbwd(parts, w, x, g, *, name, dres=None, want_dx=True):
    m_rows, d = x.shape
    tm = min(512, m_rows)
    n_parts = len(parts)

    def body(*refs):
        refs = list(refs)
        dy_refs = [refs.pop(0) for _ in range(n_parts)]
        w_refs = [refs.pop(0) for _ in range(n_parts)]
        x_ref = refs.pop(0)
        g_ref = refs.pop(0)
        dres_ref = refs.pop(0) if dres is not None else None
        dx_ref = refs.pop(0) if want_dx else None
        dg_ref = refs.pop(0)

        @pl.when(pl.program_id(0) == 0)
        def _():
            dg_ref[...] = jnp.zeros_like(dg_ref)

        dh = None
        for dy_ref, w_ref in zip(dy_refs, w_refs):
            t = lax.dot_general(dy_ref[...].astype(BF16), w_ref[...], NT, preferred_element_type=F32)
            dh = t if dh is None else dh + t
        xv = x_ref[...]
        r = lax.rsqrt(jnp.mean(xv * xv, axis=-1, keepdims=True) + EPS)
        xh = xv * r
        dg_ref[...] += jnp.sum(dh * xh, axis=0, keepdims=True)
        if want_dx:
            dhg = dh * g_ref[...]
            dx = r * (dhg - xh * jnp.mean(dhg * xh, axis=-1, keepdims=True))
            if dres is not None:
                dx = dx + dres_ref[...]
            dx_ref[...] = dx

    in_specs, args = [], []
    for dy, _ in parts:
        in_specs.append(pl.BlockSpec((tm, dy.shape[1]), lambda i: (i, 0)))
        args.append(dy)
    for dy, cb in parts:
        in_specs.append(pl.BlockSpec((d, dy.shape[1]), functools.partial(lambda i, cb: (0, cb), cb=cb)))
        args.append(w)
    in_specs += [pl.BlockSpec((tm, d), lambda i: (i, 0)), pl.BlockSpec((1, d), lambda i: (0, 0))]
    args += [x, g.reshape(1, d)]
    if dres is not None:
        in_specs.append(pl.BlockSpec((tm, d), lambda i: (i, 0)))
        args.append(dres)
    out_shape, out_specs = [], []
    if want_dx:
        out_shape.append(_sds((m_rows, d), F32))
        out_specs.append(pl.BlockSpec((tm, d), lambda i: (i, 0)))
    out_shape.append(_sds((1, d), F32))
    out_specs.append(pl.BlockSpec((1, d), lambda i: (0, 0)))
    outs = _pc(body, name=name, out_shape=out_shape, grid=(m_rows // tm,), in_specs=in_specs,
               out_specs=out_specs, semantics=("arbitrary",), vmem=VMEM_BIG)(*args)
    return (outs[0], outs[1]) if want_dx else (None, outs[0])


def _loss_head(x, g, tgt, *, name):
    m_rows, d = x.shape
    tm = min(512, m_rows)

    def body(x_ref, g_ref, t_ref, dx_ref, dg_ref, loss_ref):
        @pl.when(pl.program_id(0) == 0)
        def _():
            dg_ref[...] = jnp.zeros_like(dg_ref)
            loss_ref[...] = jnp.zeros_like(loss_ref)

        xv = x_ref[...]
        r = lax.rsqrt(jnp.mean(xv * xv, axis=-1, keepdims=True) + EPS)
        xh = xv * r
        gv = g_ref[...]
        err = xh * gv - t_ref[...]
        per_tok = jnp.mean(err * err, axis=-1, keepdims=True)
        loss_ref[...] += 0.5 * jnp.sum(per_tok, axis=0, keepdims=True)
        dout = err * (1.0 / d)
        dg_ref[...] += jnp.sum(dout * xh, axis=0, keepdims=True)
        dhg = dout * gv
        dx_ref[...] = r * (dhg - xh * jnp.mean(dhg * xh, axis=-1, keepdims=True))

    row = pl.BlockSpec((tm, d), lambda i: (i, 0))
    return _pc(body, name=name, out_shape=[_sds((m_rows, d), F32), _sds((1, d), F32), _sds((1, LANES), F32)],
               grid=(m_rows // tm,), in_specs=[row, pl.BlockSpec((1, d), lambda i: (0, 0)), row],
               out_specs=[row, pl.BlockSpec((1, d), lambda i: (0, 0)), pl.BlockSpec((1, LANES), lambda i: (0, 0))],
               semantics=("arbitrary",))(x, g.reshape(1, d), tgt)


def _split3(v):
    hi = v.astype(BF16)
    r1 = v - hi.astype(F32)
    mid = r1.astype(BF16)
    lo = (r1 - mid.astype(F32)).astype(BF16)
    return hi, mid, lo


def _split2(v):
    hi = v.astype(BF16)
    lo = (v - hi.astype(F32)).astype(BF16)
    return hi, lo


def _tri_dot3(tri, v):
    hi, mid, lo = _split3(v)
    return (jnp.dot(tri, hi, preferred_element_type=F32) + jnp.dot(tri, mid, preferred_element_type=F32)
            + jnp.dot(tri, lo, preferred_element_type=F32))


def _log_sigmoid(v):
    return jnp.minimum(v, 0.0) - jnp.log(1.0 + jnp.exp(-jnp.abs(v)))


def _forget_cumsum(f_logit, b_f, *, B, S, name):
    ch = min(256, S)
    nch = S // ch

    def body(f_ref, b_ref, c_ref):
        r_i = lax.broadcasted_iota(jnp.int32, (ch, ch), 0)
        c_i = lax.broadcasted_iota(jnp.int32, (ch, ch), 1)
        tri = (c_i <= r_i).astype(BF16)
        bv = b_ref[...]

        def step(k, carry):
            rows = pl.ds(pl.multiple_of(k * ch, ch), ch)
            lf = _log_sigmoid(f_ref[rows, :] + bv)
            c_ref[rows, :] = _tri_dot3(tri, lf) + carry
            return carry + jnp.sum(lf, axis=0, keepdims=True)

        lax.fori_loop(0, nch, step, jnp.zeros((1, LANES), F32))

    blk = pl.BlockSpec((S, LANES), lambda b: (b, 0))
    return _pc(body, name=name, out_shape=_sds((B * S, LANES), F32), grid=(B,),
               in_specs=[blk, pl.BlockSpec((1, LANES), lambda b: (0, 0))], out_specs=blk,
               semantics=("arbitrary",))(f_logit, b_f)


def _forget_cumsum_bwd(dc, f_logit, b_f, *, B, S, name):
    ch = min(256, S)
    nch = S // ch

    def body(dc_ref, f_ref, b_ref, df_ref, db_ref):
        @pl.when(pl.program_id(0) == 0)
        def _():
            db_ref[...] = jnp.zeros_like(db_ref)

        r_i = lax.broadcasted_iota(jnp.int32, (ch, ch), 0)
        c_i = lax.broadcasted_iota(jnp.int32, (ch, ch), 1)
        tri = (c_i >= r_i).astype(BF16)
        bv = b_ref[...]

        def step(kk, carry):
            tail, dbs = carry
            k = nch - 1 - kk
            rows = pl.ds(pl.multiple_of(k * ch, ch), ch)
            dcv = dc_ref[rows, :]
            dlf = _tri_dot3(tri, dcv) + tail
            z = f_ref[rows, :] + bv
            df = dlf * (1.0 / (1.0 + jnp.exp(z)))
            df_ref[rows, :] = df.astype(BF16)
            return tail + jnp.sum(dcv, axis=0, keepdims=True), dbs + jnp.sum(df, axis=0, keepdims=True)

        zero = jnp.zeros((1, LANES), F32)
        _, dbs = lax.fori_loop(0, nch, step, (zero, zero))
        db_ref[...] += dbs

    blk = pl.BlockSpec((S, LANES), lambda b: (b, 0))
    one = pl.BlockSpec((1, LANES), lambda b: (0, 0))
    return _pc(body, name=name, out_shape=[_sds((B * S, LANES), BF16), _sds((1, LANES), F32)], grid=(B,),
               in_specs=[blk, blk, one], out_specs=[blk, one], semantics=("arbitrary",))(dc, f_logit, b_f)


def _head_mask(lane, hh):
    return (lane < HEAD_DIM) if hh == 0 else (lane >= HEAD_DIM)


HEAD_GROUP = 3
FWD_HEAD_GROUP = 6


def _g_col_spec(rows, nblk_rows, cb, G):
    return pl.BlockSpec((rows, G * LANES), lambda b, p, i: (b * nblk_rows + i, cb // G + p))


def _g_kv_spec(rows, cb, G):
    return pl.BlockSpec((rows, G * LANES), lambda b, p, i: (b, cb // G + p))


def _g_stat_col_spec(tq, G):
    return pl.BlockSpec((1, 2 * G, tq, 1), lambda b, p, i: (b, p, i, 0))


def _g_stat_row_spec(S, G):
    return pl.BlockSpec((1, 2 * G, 1, S), lambda b, p, i: (b, p, 0, 0))


def _lanes(g):
    return slice(g * LANES, (g + 1) * LANES)


def _streams(x_ref, G, scale=None):
    rows = x_ref.shape[0]
    lane = lax.broadcasted_iota(jnp.int32, (rows, LANES), 1)
    out = []
    for g in range(G):
        x = x_ref[:, _lanes(g)]
        if scale is not None:
            x = x * jnp.asarray(scale, x.dtype)
        for hh in range(2):
            out.append(jnp.where(_head_mask(lane, hh), x, jnp.zeros_like(x)))
    return lane, out


def _wide(stat, width):
    return jnp.tile(stat, (1, width // LANES))


def _fold_lanes(v):
    out = v[:, :LANES]
    for j in range(1, v.shape[1] // LANES):
        out = out + v[:, j * LANES:(j + 1) * LANES]
    return out


def _kv_blocks(ref, ks, tk, G):
    return [ref[pl.ds(ks, tk), _lanes(g)] for g in range(G)]


def _sweep(i, block):
    def step(kb, c):
        block(kb, False)
        return c
    lax.fori_loop(0, i, step, 0)
    block(i, True)


def _fox_fwd_g(qa, ka, va, cr, *, name, B, S, P, q_cb, k_cb, v_cb, G=HEAD_GROUP):
    tq = tk = min(ATT_TILE, S)
    nq = S // tq
    NS = 2 * G

    def body(q_ref, k_ref, v_ref, cr_ref, o_ref, lse_ref, acc_ref, m_ref, l_ref):
        i = pl.program_id(2)
        lane, qh = _streams(q_ref, G, SCALE)
        on_or_below = (lax.broadcasted_iota(jnp.int32, (tq, tk), 1) <= lax.broadcasted_iota(jnp.int32, (tq, tk), 0))
        m_ref[...] = jnp.full(m_ref.shape, NEG, F32)
        l_ref[...] = jnp.zeros(l_ref.shape, F32)
        acc_ref[...] = jnp.zeros(acc_ref.shape, F32)

        def block(kb, diag):
            ks = pl.multiple_of(kb * tk, tk)
            kblk = _kv_blocks(k_ref, ks, tk, G)
            vblk = _kv_blocks(v_ref, ks, tk, G)
            ss = [lax.dot_general(qh[st], kblk[st // 2], NT, preferred_element_type=F32) for st in range(NS)]
            ps = []
            for st in range(NS):
                s = ss[st] - cr_ref[0, st, :, pl.ds(ks, tk)]
                if diag:
                    s = jnp.where(on_or_below, s, NEG)
                m = m_ref[st]
                m_new = jnp.maximum(m, jnp.max(s, axis=-1, keepdims=True))
                alpha = jnp.exp(m - m_new)
                p = jnp.exp(s - _wide(m_new, tk))
                m_ref[st] = m_new
                l_ref[st] = alpha * l_ref[st] + _fold_lanes(p)
                ps.append((alpha, p.astype(BF16)))
            pvs = [jnp.dot(ps[st][1], vblk[st // 2], preferred_element_type=F32) for st in range(NS)]
            for st in range(NS):
                acc_ref[st] = ps[st][0] * acc_ref[st] + pvs[st]

        _sweep(i, block)
        ls = [jnp.sum(l_ref[st], axis=-1, keepdims=True) for st in range(NS)]
        for st in range(NS):
            lse_ref[0, st] = jnp.max(m_ref[st], axis=-1, keepdims=True) + jnp.log(ls[st])
        for g in range(G):
            o_ref[:, _lanes(g)] = jnp.where(lane < HEAD_DIM, acc_ref[2 * g] / ls[2 * g],
                                            acc_ref[2 * g + 1] / ls[2 * g + 1]).astype(BF16)

    return _pc(body, name=name, out_shape=[_sds((B * S, P * LANES), BF16), _sds((B, 2 * P, S, 1), F32)],
               grid=(B, P // G, nq),
               in_specs=[_g_col_spec(tq, nq, q_cb, G), _g_kv_spec(S, k_cb, G), _g_kv_spec(S, v_cb, G),
                         _g_stat_row_spec(S, G)],
               out_specs=[_g_col_spec(tq, nq, 0, G), _g_stat_col_spec(tq, G)],
               scratch_shapes=[pltpu.VMEM((NS, tq, LANES), F32)] * 3,
               semantics=("arbitrary", "arbitrary", "arbitrary"), vmem=VMEM_BIG)(qa, ka, va, cr)


def _fox_bwd_g(qa, ka, va, doa, lse, cr, *, name, B, S, P, q_cb, k_cb, v_cb, do_cb, G=HEAD_GROUP):
    tq = tk = min(ATT_TILE, S)
    nq = S // tq
    NS = 2 * G

    def body(q_ref, k_ref, v_ref, do_ref, lse_ref, cr_ref, dq_ref, dk_ref, dv_ref, dcs_ref, dqa_ref, delta_ref, lse_s,
             p_buf, dp_buf):
        i = pl.program_id(2)

        @pl.when(i == 0)
        def _():
            dk_ref[...] = jnp.zeros_like(dk_ref)
            dv_ref[...] = jnp.zeros_like(dv_ref)
            dcs_ref[...] = jnp.zeros_like(dcs_ref)

        lane, qh = _streams(q_ref, G, SCALE)
        _, doh = _streams(do_ref, G)
        on_or_below = (lax.broadcasted_iota(jnp.int32, (tq, tk), 1) <= lax.broadcasted_iota(jnp.int32, (tq, tk), 0))
        delta_ref[...] = jnp.zeros(delta_ref.shape, F32)
        dqa_ref[...] = jnp.zeros(dqa_ref.shape, F32)
        for st in range(NS):
            lse_s[st] = jnp.broadcast_to(lse_ref[0, st], (tq, LANES))

        def probs(kb, diag):
            ks = pl.multiple_of(kb * tk, tk)
            kblk = _kv_blocks(k_ref, ks, tk, G)
            vblk = _kv_blocks(v_ref, ks, tk, G)
            ss = [lax.dot_general(qh[st], kblk[st // 2], NT, preferred_element_type=F32) for st in range(NS)]
            dps = [lax.dot_general(doh[st], vblk[st // 2], NT, preferred_element_type=F32) for st in range(NS)]
            ps = []
            for st in range(NS):
                s = ss[st] - cr_ref[0, st, :, pl.ds(ks, tk)]
                if diag:
                    s = jnp.where(on_or_below, s, NEG)
                ps.append(jnp.exp(s - _wide(lse_s[st], tk)))
            return ks, kblk, ps, dps

        def delta_block(kb, diag):
            _, _, ps, dps = probs(kb, diag)
            for st in range(NS):
                delta_ref[st] += _fold_lanes(ps[st] * dps[st])
                p_buf[st, kb] = ps[st]
                dp_buf[st, kb] = dps[st]

        _sweep(i, delta_block)
        for st in range(NS):
            delta_ref[st] = jnp.broadcast_to(jnp.sum(delta_ref[st], axis=-1, keepdims=True), (tq, LANES))

        def grad_block(kb, diag):
            ks = pl.multiple_of(kb * tk, tk)
            kblk = _kv_blocks(k_ref, ks, tk, G)
            rows = pl.ds(ks, tk)
            dsb, pb = [], []
            for st in range(NS):
                p = p_buf[st, kb]
                ds = p * (dp_buf[st, kb] - _wide(delta_ref[st], tk))
                dcs_ref[0, st, :, rows] -= jnp.sum(ds, axis=0, keepdims=True)
                dsb.append(ds.astype(BF16))
                pb.append(p.astype(BF16))
            dks = [lax.dot_general(dsb[st], qh[st], TN, preferred_element_type=F32) for st in range(NS)]
            dvs = [lax.dot_general(pb[st], doh[st], TN, preferred_element_type=F32) for st in range(NS)]
            dqs = [jnp.dot(dsb[st], kblk[st // 2], preferred_element_type=F32) for st in range(NS)]
            for g in range(G):
                dk_ref[rows, _lanes(g)] += dks[2 * g] + dks[2 * g + 1]
                dv_ref[rows, _lanes(g)] += dvs[2 * g] + dvs[2 * g + 1]
            for st in range(NS):
                dqa_ref[st] += dqs[st]

        _sweep(i, grad_block)
        for g in range(G):
            dq_ref[:, _lanes(g)] = (jnp.where(lane < HEAD_DIM, dqa_ref[2 * g], dqa_ref[2 * g + 1]) * SCALE).astype(BF16)

    return _pc(body, name=name,
               out_shape=[_sds((B * S, P * LANES), BF16), _sds((B * S, P * LANES), F32), _sds((B * S, P * LANES), F32),
                          _sds((B, 2 * P, 1, S), F32)],
               grid=(B, P // G, nq),
               in_specs=[_g_col_spec(tq, nq, q_cb, G), _g_kv_spec(S, k_cb, G), _g_kv_spec(S, v_cb, G),
                         _g_col_spec(tq, nq, do_cb, G), _g_stat_col_spec(tq, G), _g_stat_row_spec(S, G)],
               out_specs=[_g_col_spec(tq, nq, 0, G), _g_kv_spec(S, 0, G), _g_kv_spec(S, 0, G), _g_stat_row_spec(S, G)],
               scratch_shapes=[pltpu.VMEM((NS, tq, LANES), F32)] * 3 + [pltpu.VMEM((NS, nq, tq, tk), F32)] * 2,
               semantics=("arbitrary", "arbitrary", "arbitrary"), vmem=VMEM_BIG)(qa, ka, va, doa, lse, cr)


def _sb_logs_z(z):
    nz = -z
    lm = jnp.minimum(nz, 0.0) - jnp.log(1.0 + jnp.exp(jnp.minimum(z, nz)))
    return lm + z, lm


def _sb_fwd_g(qa, ka, va, *, name, B, S, P, q_cb, k_cb, v_cb, G=HEAD_GROUP):
    tq = tk = min(ATT_TILE, S)
    nq = S // tq
    NS = 2 * G

    def body(q_ref, k_ref, v_ref, o_ref, rt_ref, acc_ref, run_ref):
        i = pl.program_id(2)
        lane, qh = _streams(q_ref, G, SCALE)
        t_r = lax.broadcasted_iota(jnp.int32, (tk, tk), 0)
        t_c = lax.broadcasted_iota(jnp.int32, (tk, tk), 1)
        after = (t_r > t_c).astype(BF16)
        below = t_c < t_r
        acc_ref[...] = jnp.zeros(acc_ref.shape, F32)
        run_ref[...] = jnp.zeros(run_ref.shape, F32)

        def block(kb, diag):
            ks = pl.multiple_of(kb * tk, tk)
            kblk = _kv_blocks(k_ref, ks, tk, G)
            vblk = _kv_blocks(v_ref, ks, tk, G)
            zs = [lax.dot_general(qh[st], kblk[st // 2], NT, preferred_element_type=F32) for st in range(NS)]
            lss, parts = [], []
            for st in range(NS):
                ls, lm = _sb_logs_z(zs[st])
                if diag:
                    lm = jnp.where(below, lm, 0.0)
                lss.append(ls + _wide(run_ref[st], tk))
                run_ref[st] += jnp.sum(lm, axis=-1, keepdims=True)
                parts.append(_split2(lm))
            sufs = [jnp.dot(parts[st][0], after, preferred_element_type=F32)
                    + jnp.dot(parts[st][1], after, preferred_element_type=F32) for st in range(NS)]
            ab = []
            for st in range(NS):
                a = jnp.exp(lss[st] + sufs[st])
                if diag:
                    a = jnp.where(below, a, 0.0)
                ab.append(a.astype(BF16))
            pvs = [jnp.dot(ab[st], vblk[st // 2], preferred_element_type=F32) for st in range(NS)]
            for st in range(NS):
                acc_ref[st] += pvs[st]

        block(i, True)

        def step(jj, c):
            block(i - 1 - jj, False)
            return c

        lax.fori_loop(0, i, step, 0)
        for st in range(NS):
            rt_ref[0, st] = jnp.max(run_ref[st], axis=-1, keepdims=True)
        for g in range(G):
            o_ref[:, _lanes(g)] = jnp.where(lane < HEAD_DIM, acc_ref[2 * g], acc_ref[2 * g + 1]).astype(BF16)

    return _pc(body, name=name, out_shape=[_sds((B * S, P * LANES), BF16), _sds((B, 2 * P, S, 1), F32)],
               grid=(B, P // G, nq),
               in_specs=[_g_col_spec(tq, nq, q_cb, G), _g_kv_spec(S, k_cb, G), _g_kv_spec(S, v_cb, G)],
               out_specs=[_g_col_spec(tq, nq, 0, G), _g_stat_col_spec(tq, G)],
               scratch_shapes=[pltpu.VMEM((NS, tq, LANES), F32)] * 2,
               semantics=("arbitrary", "arbitrary", "arbitrary"), vmem=VMEM_BIG)(qa, ka, va)


def _sb_bwd_g(qa, ka, va, doa, rt, *, name, B, S, P, q_cb, k_cb, v_cb, do_cb, G=HEAD_GROUP):
    tq = tk = min(ATT_TILE, S)
    nq = S // tq
    NS = 2 * G

    def body(q_ref, k_ref, v_ref, do_ref, rt_ref, dq_ref, dk_ref, dv_ref, dqa_ref, pl_ref, pg_ref):
        i = pl.program_id(2)

        @pl.when(i == 0)
        def _():
            dk_ref[...] = jnp.zeros_like(dk_ref)
            dv_ref[...] = jnp.zeros_like(dv_ref)

        lane, qh = _streams(q_ref, G, SCALE)
        _, doh = _streams(do_ref, G)
        t_r = lax.broadcasted_iota(jnp.int32, (tk, tk), 0)
        t_c = lax.broadcasted_iota(jnp.int32, (tk, tk), 1)
        upto = (t_r <= t_c).astype(BF16)
        before = (t_r < t_c).astype(BF16)
        below = t_c < t_r
        dqa_ref[...] = jnp.zeros(dqa_ref.shape, F32)
        pg_ref[...] = jnp.zeros(pg_ref.shape, F32)
        for st in range(NS):
            pl_ref[st] = jnp.broadcast_to(rt_ref[0, st], (tq, LANES))

        def block(kb, diag):
            ks = pl.multiple_of(kb * tk, tk)
            rows = pl.ds(ks, tk)
            kblk = _kv_blocks(k_ref, ks, tk, G)
            vblk = _kv_blocks(v_ref, ks, tk, G)
            zs = [lax.dot_general(qh[st], kblk[st // 2], NT, preferred_element_type=F32) for st in range(NS)]
            das = [lax.dot_general(doh[st], vblk[st // 2], NT, preferred_element_type=F32) for st in range(NS)]
            lss, parts = [], []
            for st in range(NS):
                ls, lm = _sb_logs_z(zs[st])
                if diag:
                    lm = jnp.where(below, lm, 0.0)
                lss.append((ls, ls + _wide(pl_ref[st], tk)))
                pl_ref[st] -= jnp.sum(lm, axis=-1, keepdims=True)
                parts.append(_split2(lm))
            pins = [jnp.dot(parts[st][0], upto, preferred_element_type=F32)
                    + jnp.dot(parts[st][1], upto, preferred_element_type=F32) for st in range(NS)]
            gms, ab, gparts = [], [], []
            for st in range(NS):
                a = jnp.exp(lss[st][1] - pins[st])
                if diag:
                    a = jnp.where(below, a, 0.0)
                gm = a * das[st]
                gms.append(gm)
                ab.append(a.astype(BF16))
                gparts.append(gm.astype(BF16))
            pgs = [jnp.dot(gparts[st], before, preferred_element_type=F32) for st in range(NS)]
            dzb = []
            for st in range(NS):
                gm = gms[st]
                dz = gm - jnp.exp(lss[st][0]) * (gm + (pgs[st] + _wide(pg_ref[st], tk)))
                if diag:
                    dz = jnp.where(below, dz, 0.0)
                pg_ref[st] += jnp.sum(gm, axis=-1, keepdims=True)
                dzb.append(dz.astype(BF16))
            dks = [lax.dot_general(dzb[st], qh[st], TN, preferred_element_type=F32) for st in range(NS)]
            dvs = [lax.dot_general(ab[st], doh[st], TN, preferred_element_type=F32) for st in range(NS)]
            dqs = [jnp.dot(dzb[st], kblk[st // 2], preferred_element_type=F32) for st in range(NS)]
            for g in range(G):
                dk_ref[rows, _lanes(g)] += dks[2 * g] + dks[2 * g + 1]
                dv_ref[rows, _lanes(g)] += dvs[2 * g] + dvs[2 * g + 1]
            for st in range(NS):
                dqa_ref[st] += dqs[st]

        _sweep(i, block)
        for g in range(G):
            dq_ref[:, _lanes(g)] = (jnp.where(lane < HEAD_DIM, dqa_ref[2 * g], dqa_ref[2 * g + 1]) * SCALE).astype(BF16)

    return _pc(body, name=name,
               out_shape=[_sds((B * S, P * LANES), BF16), _sds((B * S, P * LANES), F32), _sds((B * S, P * LANES), F32)],
               grid=(B, P // G, nq),
               in_specs=[_g_col_spec(tq, nq, q_cb, G), _g_kv_spec(S, k_cb, G), _g_kv_spec(S, v_cb, G),
                         _g_col_spec(tq, nq, do_cb, G), _g_stat_col_spec(tq, G)],
               out_specs=[_g_col_spec(tq, nq, 0, G), _g_kv_spec(S, 0, G), _g_kv_spec(S, 0, G)],
               scratch_shapes=[pltpu.VMEM((NS, tq, LANES), F32)] * 3,
               semantics=("arbitrary", "arbitrary", "arbitrary"), vmem=VMEM_BIG)(qa, ka, va, doa, rt)


MEM_GROUP = N_MEM_HEADS // 2


def _mem_fwd(qa, kva, *, name, B, S, NM, q_cb):
    G = MEM_GROUP
    NS = 2 * G
    tq = min(MEM_Q_TILE, S)
    nq = S // tq

    def body(q_ref, k_ref, v_ref, o_ref, lse_ref):
        lane, qh = _streams(q_ref, G, SCALE)
        kblk = [k_ref[:, _lanes(g)] for g in range(G)]
        vblk = [v_ref[:, _lanes(g)] for g in range(G)]
        ss = [lax.dot_general(qh[st], kblk[st // 2], NT, preferred_element_type=F32) for st in range(NS)]
        pb, ls = [], []
        for st in range(NS):
            m = jnp.max(ss[st], axis=-1, keepdims=True)
            p = jnp.exp(ss[st] - m)
            l = jnp.sum(p, axis=-1, keepdims=True)
            lse_ref[0, st] = m + jnp.log(l)
            pb.append(p.astype(BF16))
            ls.append(l)
        pvs = [jnp.dot(pb[st], vblk[st // 2], preferred_element_type=F32) for st in range(NS)]
        for g in range(G):
            o_ref[:, _lanes(g)] = jnp.where(lane < HEAD_DIM, pvs[2 * g] / ls[2 * g],
                                            pvs[2 * g + 1] / ls[2 * g + 1]).astype(BF16)

    return _pc(body, name=name, out_shape=[_sds((B * S, G * LANES), BF16), _sds((B, NS, S, 1), F32)],
               grid=(B, 1, nq),
               in_specs=[_g_col_spec(tq, nq, q_cb, G), _g_kv_spec(NM, 0, G), _g_kv_spec(NM, G, G)],
               out_specs=[_g_col_spec(tq, nq, 0, G), _g_stat_col_spec(tq, G)],
               semantics=("arbitrary", "arbitrary", "arbitrary"), vmem=VMEM_BIG)(qa, kva, kva)


def _mem_bwd(qa, kva, doa, oa, lse, *, name, B, S, NM, q_cb, do_cb):
    G = MEM_GROUP
    NS = 2 * G
    tq = min(MEM_Q_TILE, S)
    nq = S // tq

    def body(q_ref, k_ref, v_ref, do_ref, o_ref, lse_ref, dq_ref, dk_ref, dv_ref):
        @pl.when(pl.program_id(2) == 0)
        def _():
            dk_ref[...] = jnp.zeros_like(dk_ref)
            dv_ref[...] = jnp.zeros_like(dv_ref)

        lane, qh = _streams(q_ref, G, SCALE)
        _, doh = _streams(do_ref, G)
        kblk = [k_ref[:, _lanes(g)] for g in range(G)]
        vblk = [v_ref[:, _lanes(g)] for g in range(G)]
        prod = [do_ref[:, _lanes(g)].astype(F32) * o_ref[:, _lanes(g)].astype(F32) for g in range(G)]
        ss = [lax.dot_general(qh[st], kblk[st // 2], NT, preferred_element_type=F32) for st in range(NS)]
        dps = [lax.dot_general(doh[st], vblk[st // 2], NT, preferred_element_type=F32) for st in range(NS)]
        dsb, pb = [], []
        for st in range(NS):
            delta = jnp.sum(jnp.where(_head_mask(lane, st % 2), prod[st // 2], 0.0), axis=-1, keepdims=True)
            p = jnp.exp(ss[st] - lse_ref[0, st])
            dsb.append((p * (dps[st] - delta)).astype(BF16))
            pb.append(p.astype(BF16))
        dks = [lax.dot_general(dsb[st], qh[st], TN, preferred_element_type=F32) for st in range(NS)]
        dvs = [lax.dot_general(pb[st], doh[st], TN, preferred_element_type=F32) for st in range(NS)]
        dqs = [jnp.dot(dsb[st], kblk[st // 2], preferred_element_type=F32) for st in range(NS)]
        for g in range(G):
            dk_ref[:, _lanes(g)] += dks[2 * g] + dks[2 * g + 1]
            dv_ref[:, _lanes(g)] += dvs[2 * g] + dvs[2 * g + 1]
            dq_ref[:, _lanes(g)] = (jnp.where(lane < HEAD_DIM, dqs[2 * g], dqs[2 * g + 1]) * SCALE).astype(BF16)

    return _pc(body, name=name,
               out_shape=[_sds((B * S, G * LANES), BF16), _sds((B * NM, G * LANES), F32), _sds((B * NM, G * LANES), F32)],
               grid=(B, 1, nq),
               in_specs=[_g_col_spec(tq, nq, q_cb, G), _g_kv_spec(NM, 0, G), _g_kv_spec(NM, G, G),
                         _g_col_spec(tq, nq, do_cb, G), _g_col_spec(tq, nq, 0, G), _g_stat_col_spec(tq, G)],
               out_specs=[_g_col_spec(tq, nq, 0, G), _g_kv_spec(NM, 0, G), _g_kv_spec(NM, 0, G)],
               semantics=("arbitrary", "arbitrary", "arbitrary"), vmem=VMEM_BIG)(qa, kva, kva, doa, oa, lse)


def _sigmoid(v):
    return 0.5 * jnp.tanh(0.5 * v) + 0.5


def _shift_rows(cur, halo_ref, first, rows_idx, k):
    out = pltpu.roll(cur, k, 0)
    top = out[0:8, :]
    for r in range(k):
        hr = halo_ref.shape[0] - k + r
        edge = jnp.where(first, 0.0, halo_ref[hr:hr + 1, :])
        top = jnp.where(rows_idx[0:8, :] == r, edge, top)
    return jnp.concatenate([top, out[8:, :]], axis=0)


def _shift_rows_up(cur, halo_ref, last, rows_idx, k, ts):
    out = pltpu.roll(cur, ts - k, 0)
    bottom = out[ts - 8:, :]
    for r in range(k):
        edge = jnp.where(last, 0.0, halo_ref[r:r + 1, :])
        bottom = jnp.where(rows_idx[0:8, :] == 8 - k + r, edge, bottom)
    return jnp.concatenate([out[:ts - 8, :], bottom], axis=0)


def _ffn_up_gate(x, g, w, cw, cb, *, name, S):
    T, D = x.shape
    F = w.shape[1] // 2
    tm = min(1024, S)
    tn = 256
    nj = F // tn
    tiles_per_seq = S // tm
    halo = 16

    def body(x_ref, xh_ref, g_ref, wg_ref, wv_ref, cwg_ref, cwv_ref, cbg_ref, cbv_ref,
             uc_ref, ub_ref, a_ref, hout_ref, h_ref, hh_ref, eg_ref, ev_ref):
        first = lax.rem(pl.program_id(0), tiles_per_seq) == 0

        @pl.when(pl.program_id(1) == 0)
        def _():
            def norm(v):
                r = lax.rsqrt(jnp.mean(v * v, axis=-1, keepdims=True) + EPS)
                return ((v * r) * g_ref[...]).astype(BF16)
            h = norm(x_ref[...])
            h_ref[...] = h
            hout_ref[...] = h
            hh_ref[...] = norm(xh_ref[...])

        h = h_ref[...]
        rows_idx = lax.broadcasted_iota(jnp.int32, (tm, tn), 0)
        uc = []
        for half, (w_ref, cw_ref, cb_ref, e_ref) in enumerate(((wg_ref, cwg_ref, cbg_ref, eg_ref),
                                                               (wv_ref, cwv_ref, cbv_ref, ev_ref))):
            acc = jnp.dot(h, w_ref[...], preferred_element_type=F32)
            e_ref[...] = jnp.dot(hh_ref[...], w_ref[...], preferred_element_type=F32)
            ub_ref[half] = acc.astype(BF16)
            m1 = _shift_rows(acc, e_ref, first, rows_idx, 1)
            m2 = _shift_rows(acc, e_ref, first, rows_idx, 2)
            uc.append(cb_ref[...] + cw_ref[0:1, :] * m2 + cw_ref[1:2, :] * m1 + cw_ref[2:3, :] * acc)
            uc_ref[half] = uc[half]
        a_ref[...] = (uc[0] * _sigmoid(uc[0]) * uc[1]).astype(BF16)

    in_specs = [pl.BlockSpec((tm, D), lambda i, j: (i, 0)),
                pl.BlockSpec((halo, D), lambda i, j: (jnp.maximum(i * (tm // halo) - 1, 0), 0)),
                pl.BlockSpec((1, D), lambda i, j: (0, 0)),
                pl.BlockSpec((D, tn), lambda i, j: (0, j)), pl.BlockSpec((D, tn), lambda i, j: (0, j + nj)),
                pl.BlockSpec((3, tn), lambda i, j: (0, j)), pl.BlockSpec((3, tn), lambda i, j: (0, j + nj)),
                pl.BlockSpec((1, tn), lambda i, j: (0, j)), pl.BlockSpec((1, tn), lambda i, j: (0, j + nj))]
    return _pc(body, name=name,
               out_shape=[_sds((2, T, F), F32), _sds((2, T, F), BF16), _sds((T, F), BF16), _sds((T, D), BF16)],
               grid=(T // tm, nj), in_specs=in_specs,
               out_specs=[pl.BlockSpec((2, tm, tn), lambda i, j: (0, i, j)), pl.BlockSpec((2, tm, tn), lambda i, j: (0, i, j)),
                          pl.BlockSpec((tm, tn), lambda i, j: (i, j)), pl.BlockSpec((tm, D), lambda i, j: (i, 0))],
               scratch_shapes=[pltpu.VMEM((tm, D), BF16), pltpu.VMEM((halo, D), BF16),
                               pltpu.VMEM((halo, tn), F32), pltpu.VMEM((halo, tn), F32)],
               semantics=("arbitrary", "arbitrary"), vmem=VMEM_BIG)(x, x, g.reshape(1, D), w, w, cw, cw, cb, cb)


def _conv_gate_bwd(da, uc, ub, cw, *, name, B, S):
    F = uc.shape[2]
    tf = F // 2
    ts = min(512, S)
    ns, nf = S // ts, F // tf

    def body(da_ref, uc_ref, ub_ref, wg_ref, wv_ref, dug_ref, duv_ref, pg_ref, pv_ref, nxt_g, nxt_v):
        last = pl.program_id(2) == 0

        @pl.when(jnp.logical_and(pl.program_id(1) == 0, last))
        def _():
            pg_ref[...] = jnp.zeros_like(pg_ref)
            pv_ref[...] = jnp.zeros_like(pv_ref)

        rows_idx = lax.broadcasted_iota(jnp.int32, (ts, tf), 0)
        ucg, ucv = uc_ref[0], uc_ref[1]
        sg = _sigmoid(ucg)
        dav = da_ref[...]
        d_v = dav * (ucg * sg)
        d_g = dav * ucv * (sg * (1.0 + ucg * (1.0 - sg)))
        for half, (o_ref, p_ref, d, w_ref, nxt) in enumerate(((dug_ref, pg_ref, d_g, wg_ref, nxt_g),
                                                               (duv_ref, pv_ref, d_v, wv_ref, nxt_v))):
            p1 = _shift_rows_up(d, nxt, last, rows_idx, 1, ts)
            p2 = _shift_rows_up(d, nxt, last, rows_idx, 2, ts)
            o_ref[...] = (w_ref[2:3, :] * d + w_ref[1:2, :] * p1 + w_ref[0:1, :] * p2).astype(BF16)
            nxt[...] = d[0:8, :]
            uh = ub_ref[half].astype(F32)
            for k, dk in enumerate((p2, p1, d)):
                p_ref[k:k + 1, :] += jnp.sum(dk * uh, axis=0, keepdims=True)
            p_ref[3:4, :] += jnp.sum(d, axis=0, keepdims=True)

    row = pl.BlockSpec((ts, tf), lambda j, b, r: (b * ns + ns - 1 - r, j))
    both = pl.BlockSpec((2, ts, tf), lambda j, b, r: (0, b * ns + ns - 1 - r, j))
    par = pl.BlockSpec((8, tf), lambda j, b, r: (0, j))
    return _pc(body, name=name,
               out_shape=[_sds((B * S, F), BF16), _sds((B * S, F), BF16), _sds((8, F), F32), _sds((8, F), F32)],
               grid=(nf, B, ns),
               in_specs=[row, both, both, pl.BlockSpec((3, tf), lambda j, b, r: (0, j)),
                         pl.BlockSpec((3, tf), lambda j, b, r: (0, j + nf))],
               out_specs=[row, row, par, par],
               scratch_shapes=[pltpu.VMEM((8, tf), F32), pltpu.VMEM((8, tf), F32)],
               semantics=("arbitrary", "arbitrary", "arbitrary"), vmem=VMEM_BIG)(da, uc, ub, cw, cw)


def _adamw(w, g, m, v, *, name):
    rows, cols = w.shape
    tr = rows
    while tr * cols * 4 > 2 ** 20 and tr % 16 == 0:
        tr //= 2

    def body(w_ref, g_ref, m_ref, v_ref, d_ref, nm_ref, nv_ref):
        gv = g_ref[...]
        m_new = ADAM_B1 * m_ref[...] + (1.0 - ADAM_B1) * gv
        v_new = ADAM_B2 * v_ref[...] + (1.0 - ADAM_B2) * (gv * gv)
        m_hat = m_new / (1.0 - ADAM_B1 ** ADAM_STEP)
        v_hat = v_new / (1.0 - ADAM_B2 ** ADAM_STEP)
        d_ref[...] = -ADAM_LR * (m_hat / (jnp.sqrt(v_hat) + ADAM_EPS) + ADAM_WD * w_ref[...])
        nm_ref[...] = m_new
        nv_ref[...] = v_new

    blk = pl.BlockSpec((tr, cols), lambda i: (i, 0))
    return _pc(body, name=name, out_shape=[_sds((rows, cols), F32)] * 3, grid=(rows // tr,),
               in_specs=[blk] * 4, out_specs=[blk] * 3, semantics=("arbitrary",))(w, g, m, v)


def _my_pos():
    return lax.axis_index("x"), lax.axis_index("y"), lax.axis_index("c")


_HBM = pl.BlockSpec(memory_space=pltpu.HBM)
_SEM = pl.BlockSpec(memory_space=pltpu.SEMAPHORE)
_EFFECT = pltpu.SideEffectType.DATAFLOW_SIDE_EFFECTING


def _peers(same_core=False):
    x, y, c = _my_pos()
    out = []
    for k in ((2, 4, 6) if same_core else range(1, N_DEV)):
        px, py, pc = x ^ ((k >> 2) & 1), y ^ ((k >> 1) & 1), c ^ (k & 1)
        out.append(((px, py, pc), 4 * px + 2 * py + pc))
    return out


def _scatter_start(srcs, slot_of, *, name, same_core=False):
    n = len(srcs)
    npr = 3 if same_core else N_DEV - 1
    lands = [lax.empty((N_DEV,) + slot_of(s, 0, shape_only=True), s.dtype) for s in srcs]

    def body(*refs):
        src_refs, land_refs = refs[:n], refs[n:2 * n]
        send_sems, recv_sems = refs[2 * n], refs[2 * n + 1]
        token = refs[-1]
        x, y, c = _my_pos()
        me = 4 * x + 2 * y + c
        for a in range(n):
            for k, (peer, peer_idx) in enumerate(_peers(same_core)):
                pltpu.make_async_remote_copy(
                    src_ref=slot_of(src_refs[a], peer_idx), dst_ref=land_refs[a].at[me],
                    send_sem=send_sems.at[a * npr + k], recv_sem=recv_sems.at[a * npr + k],
                    device_id=peer, device_id_type=MESH).start()
        token[...] = jnp.zeros_like(token)

    hbm = lambda a: pltpu.HBM(a.shape, a.dtype)
    args = [pltpu.with_memory_space_constraint(a, pltpu.HBM) for a in list(srcs) + lands]
    outs = pl.pallas_call(
        body, name=name,
        out_shape=(pltpu.SemaphoreType.DMA((npr * n,)), pltpu.SemaphoreType.DMA((npr * n,)),
                   *[hbm(a) for a in srcs], *[hbm(a) for a in lands], _sds((8, LANES), F32)),
        in_specs=[_HBM] * (2 * n),
        out_specs=(_SEM, _SEM, *([_HBM] * (2 * n)), pl.BlockSpec(memory_space=pltpu.VMEM)),
        input_output_aliases={a: 2 + a for a in range(2 * n)},
        compiler_params=pltpu.CompilerParams(has_side_effects=_EFFECT))(*args)
    return outs[0], outs[1], list(outs[2:2 + n]), list(outs[2 + n:2 + 2 * n]), outs[-1]


def _scatter_wait(send_sems, recv_sems, srcs, lands, slot_of, after, *, name, first=0, same_core=False):
    n = len(srcs)
    npr = 3 if same_core else N_DEV - 1

    def body(*refs):
        src_refs, land_refs = refs[:n], refs[n:2 * n]
        ssem, rsem = refs[2 * n], refs[2 * n + 1]
        x, y, c = _my_pos()
        me = 4 * x + 2 * y + c
        for a in range(n):
            for k, (peer, peer_idx) in enumerate(_peers(same_core)):
                cp = pltpu.make_async_remote_copy(
                    src_ref=slot_of(src_refs[a], peer_idx), dst_ref=land_refs[a].at[me],
                    send_sem=ssem.at[(first + a) * npr + k], recv_sem=rsem.at[(first + a) * npr + k],
                    device_id=peer, device_id_type=MESH)
                cp.wait_send()
                cp.wait_recv()

    hbm = lambda a: pltpu.HBM(a.shape, a.dtype)
    outs = pl.pallas_call(
        body, name=name, out_shape=tuple(hbm(a) for a in list(srcs) + list(lands)),
        in_specs=[_HBM] * (2 * n) + [_SEM, _SEM, pl.BlockSpec(memory_space=pl.ANY)],
        out_specs=tuple([_HBM] * (2 * n)), input_output_aliases={a: a for a in range(2 * n)},
        compiler_params=pltpu.CompilerParams(has_side_effects=_EFFECT))(*srcs, *lands, send_sems, recv_sems, after)
    return list(outs[:n]), list(outs[n:])


def _sibling_start(lands, *, name):
    n = len(lands)

    def body(*refs):
        land_refs = refs[:n]
        send_sems, recv_sems = refs[n], refs[n + 1]
        token = refs[-1]
        x, y, c = _my_pos()
        for a in range(n):
            for k in range(4):
                pltpu.make_async_remote_copy(
                    src_ref=land_refs[a].at[2 * k + c], dst_ref=land_refs[a].at[2 * k + c],
                    send_sem=send_sems.at[a * 4 + k], recv_sem=recv_sems.at[a * 4 + k],
                    device_id=(x, y, 1 - c), device_id_type=MESH).start()
        token[...] = jnp.zeros_like(token)

    hbm = lambda a: pltpu.HBM(a.shape, a.dtype)
    outs = pl.pallas_call(
        body, name=name,
        out_shape=(pltpu.SemaphoreType.DMA((4 * n,)), pltpu.SemaphoreType.DMA((4 * n,)),
                   *[hbm(a) for a in lands], _sds((8, LANES), F32)),
        in_specs=[_HBM] * n,
        out_specs=(_SEM, _SEM, *([_HBM] * n), pl.BlockSpec(memory_space=pltpu.VMEM)),
        input_output_aliases={a: 2 + a for a in range(n)},
        compiler_params=pltpu.CompilerParams(has_side_effects=_EFFECT))(
            *[pltpu.with_memory_space_constraint(a, pltpu.HBM) for a in lands])
    return outs[0], outs[1], list(outs[2:2 + n]), outs[-1]


def _sibling_wait(send_sems, recv_sems, lands, after, *, name):
    n = len(lands)

    def body(*refs):
        land_refs = refs[:n]
        ssem, rsem = refs[n], refs[n + 1]
        x, y, c = _my_pos()
        for a in range(n):
            for k in range(4):
                cp = pltpu.make_async_remote_copy(
                    src_ref=land_refs[a].at[2 * k + c], dst_ref=land_refs[a].at[2 * k + 1 - c],
                    send_sem=ssem.at[a * 4 + k], recv_sem=rsem.at[a * 4 + k],
                    device_id=(x, y, 1 - c), device_id_type=MESH)
                cp.wait_send()
                cp.wait_recv()

    hbm = lambda a: pltpu.HBM(a.shape, a.dtype)
    outs = pl.pallas_call(
        body, name=name, out_shape=tuple(hbm(a) for a in lands),
        in_specs=[_HBM] * n + [_SEM, _SEM, pl.BlockSpec(memory_space=pl.ANY)],
        out_specs=tuple([_HBM] * n), input_output_aliases={a: a for a in range(n)},
        compiler_params=pltpu.CompilerParams(has_side_effects=_EFFECT))(*lands, send_sems, recv_sems, after)
    return list(outs)


def _whole(a, peer_idx, shape_only=False):
    return a.shape if shape_only else a


def _slot(a, peer_idx, shape_only=False):
    return a.shape[1:] if shape_only else a.at[peer_idx]


def _sum_slots(a, *, name, tr=None):
    rows, cols = a.shape[1], a.shape[2]
    if tr is None:
        tr = rows
        while N_DEV * tr * cols * a.dtype.itemsize > 3 * 2 ** 20 and tr % 32 == 0:
            tr //= 2

    def body(a_ref, o_ref):
        acc = a_ref[0].astype(F32)
        for j in range(1, N_DEV):
            acc = acc + a_ref[j].astype(F32)
        o_ref[...] = acc

    return _pc(body, name=name, out_shape=_sds((rows, cols), F32), grid=(rows // tr,),
               in_specs=[pl.BlockSpec((N_DEV, tr, cols), lambda i: (0, i, 0))],
               out_specs=pl.BlockSpec((tr, cols), lambda i: (i, 0)), semantics=("arbitrary",), vmem=VMEM_BIG)(a)


def _to_slots(full, kind):
    if kind == "rows2":
        r, c = full.shape
        return full.reshape(N_DEV, r // N_DEV, c)
    if kind == "cols2":
        r, c = full.shape
        return full.reshape(r, N_DEV, c // N_DEV).transpose(1, 0, 2)
    if kind == "rows3":
        l, r, c = full.shape
        return full.reshape(l, N_DEV, r // N_DEV, c).transpose(1, 0, 2, 3)
    if kind == "cols3":
        l, r, c = full.shape
        return full.reshape(l, r, N_DEV, c // N_DEV).transpose(2, 0, 1, 3)
    raise ValueError(kind)


def _from_slots(slots, kind):
    if kind == "rows2":
        _, r, c = slots.shape
        return slots.reshape(N_DEV * r, c)
    if kind == "cols2":
        _, r, c = slots.shape
        return slots.transpose(1, 0, 2).reshape(r, N_DEV * c)
    if kind == "rows3":
        _, l, r, c = slots.shape
        return slots.transpose(1, 0, 2, 3).reshape(l, N_DEV * r, c)
    if kind == "cols3":
        _, l, r, c = slots.shape
        return slots.transpose(1, 2, 0, 3).reshape(l, r, N_DEV * c)
    raise ValueError(kind)


BIG = (("w_in_a", "rows2"), ("w_in_b", "rows2"), ("w_kv", "cols2"), ("w_memkv", "rows3"),
       ("w_out", "rows3"), ("w_up", "cols3"), ("w_down", "rows3"))


def _round_up(n, m):
    return -(-n // m) * m


def _pad_rows(a, rows, axis):
    pad = [(0, 0)] * a.ndim
    pad[axis] = (0, rows - a.shape[axis])
    return jnp.pad(a, pad)


def kernel(x, mem, ln_mix_g, w_in_a, b_f_a, w_in_b, ln_kv_g, w_kv, ln_mem_g, w_memkv, w_out, ln_ffn_g, w_up, conv_w, conv_b, w_down, final_g, loss_target, m_ln_mix_g, m_w_in_a, m_b_f_a, m_w_in_b, m_ln_kv_g, m_w_kv, m_ln_mem_g, m_w_memkv, m_w_out, m_ln_ffn_g, m_w_up, m_conv_w, m_conv_b, m_w_down, m_final_g, v_ln_mix_g, v_w_in_a, v_b_f_a, v_w_in_b, v_ln_kv_g, v_w_kv, v_ln_mem_g, v_w_memkv, v_w_out, v_ln_ffn_g, v_w_up, v_conv_w, v_conv_b, v_w_down, v_final_g):
    B, S, D = x.shape
    NM = mem.shape[1]
    T = B * S
    F = w_down.shape[1] * N_DEV
    my_idx = 4 * lax.axis_index("x") + 2 * lax.axis_index("y") + lax.axis_index("c")

    shards = {"w_in_a": w_in_a[0], "w_in_b": w_in_b[0], "w_kv": w_kv, "w_memkv": w_memkv, "w_out": w_out,
              "w_up": w_up, "w_down": w_down}
    moms = {"w_in_a": (m_w_in_a[0], v_w_in_a[0]), "w_in_b": (m_w_in_b[0], v_w_in_b[0]), "w_kv": (m_w_kv, v_w_kv),
            "w_memkv": (m_w_memkv, v_w_memkv), "w_out": (m_w_out, v_w_out), "w_up": (m_w_up, v_w_up),
            "w_down": (m_w_down, v_w_down)}

    groups = [("a1", [("w_in_a", None)]),
              ("a2", [("w_memkv", None), ("w_out", None), ("conv_w", None)]),
              ("b0", [("w_up", 0), ("w_down", 0)]), ("a3", [("w_in_b", None), ("w_kv", None)]),
              ("b1", [("w_up", 1), ("w_down", 1)])]
    sources = dict(shards, conv_w=conv_w)
    span = {}
    for sname, batch in (("first", groups[:1]), ("rest", groups[1:])):
        srcs = []
        for gname, members in batch:
            span[gname] = (sname, len(srcs), len(members))
            for n, layer in members:
                a = sources[n] if layer is None else sources[n][layer]
                srcs.append(a if n == "conv_w" else a.astype(BF16))
        span[sname] = _scatter_start(srcs, _whole, name=f"gather_start_{sname}", same_core=sname == "first")
    token = span["rest"][4]

    def gathered(gname, after):
        sname, lo, cnt = span[gname]
        ssem, rsem, thru, lands, _ = span[sname]
        two_level = sname == "first"
        thru, lands = _scatter_wait(ssem, rsem, thru[lo:lo + cnt], lands[lo:lo + cnt], _whole, after,
                                    name=f"gather_wait_{gname}", first=lo, same_core=two_level)
        lands = [lax.dynamic_update_index_in_dim(land, s, my_idx, 0) for land, s in zip(lands, thru)]
        if two_level:
            s2, r2, lands, tok2 = _sibling_start(lands, name=f"gather_pass_{gname}")
            lands = _sibling_wait(s2, r2, lands, tok2, name=f"gather_pass_wait_{gname}")
        return lands

    x2d = x.reshape(T, D)
    h_mix0 = _rmsnorm(x2d, ln_mix_g[0] + token[0, 0], name="norm_mix0")
    full = {}
    (g_wa,) = gathered("a1", h_mix0)
    full["w_in_a"] = _from_slots(g_wa, "rows2")

    wa = full["w_in_a"]
    n_qkv = 3 * MAIN_W
    wa = jnp.concatenate([wa[:, :n_qkv], wa[:, n_qkv + N_MAIN_HEADS:], wa[:, n_qkv:n_qkv + N_MAIN_HEADS],
                          jnp.zeros((D, LANES - N_MAIN_HEADS), BF16)], axis=1)
    n_main = n_qkv + MEM_W
    full["w_up"], full["w_down"] = {}, {}
    b_f =_pad_rows(b_f_a.reshape(1, N_MAIN_HEADS), LANES, 1)

    x2d = x.reshape(T, D)
    mem2d = mem.reshape(B * NM, D)
    tgt2d = loss_target.reshape(T, D)
    PM, PX = N_MAIN_HEADS // 2, N_MEM_HEADS // 2

    def stats_to_heads(c2d):
        c = c2d.reshape(B, S, LANES)[:, :, :N_MAIN_HEADS].transpose(0, 2, 1)
        return c[:, :, None, :]

    def mem_kv(layer):
        return _mm_fwd(mem2d, full["w_memkv"][layer], name=f"memkv{layer}", tm=B * NM, tn=2 * MEM_W,
                       out_dtype=BF16, g=ln_mem_g[layer], save_h=True)

    def conv_ffn_fwd(xin, layer):
        uc, ub, a, h = _ffn_up_gate(xin, ln_ffn_g[layer], full["w_up"][layer], conv_w_full[layer],
                                    conv_b[layer].reshape(1, 2 * F), name=f"ffn_up{layer}", S=S)
        xo = _mm_fwd(a, full["w_down"][layer], name=f"ffn_down{layer}", tm=min(1024, T), tn=1024, out_dtype=F32, res=xin)
        return xo, (uc, ub, h, a)

    proj_a = _mm_fwd(h_mix0, wa, name="in_proj_a", tm=min(1024, T), tn=2560, out_dtype=BF16, ncols=n_main)
    f_logit = _mm_fwd(h_mix0, wa, name="in_proj_f", tm=min(1024, T), tn=LANES, out_dtype=F32,
                      col0=n_main // LANES, ncols=LANES)
    c2d = _forget_cumsum(f_logit, b_f, B=B, S=S, name="forget_cumsum")
    cr = stats_to_heads(c2d)
    o_main0, lse0 = _fox_fwd_g(proj_a, proj_a, proj_a, cr, name="fox_fwd", B=B, S=S, P=PM, q_cb=0, k_cb=PM, v_cb=2 * PM,
                               G=FWD_HEAD_GROUP)
    g_wmem, g_wout, g_cw = gathered("a2", lse0)
    full["w_memkv"] = _from_slots(g_wmem, "rows3")
    full["w_out"] = _from_slots(g_wout, "rows3")
    conv_w_full = _from_slots(g_cw, "cols3")
    memkv0, h_mem0 = mem_kv(0)
    o_mem0, lse_m0 = _mem_fwd(proj_a, memkv0, name="mem_fwd0", B=B, S=S, NM=NM, q_cb=3 * PM)
    o_cat0 = jnp.concatenate([o_main0, o_mem0], axis=1)
    x1 = _mm_fwd(o_cat0, full["w_out"][0], name="out_proj0", tm=min(1024, T), tn=1024, out_dtype=F32, res=x2d)
    g_up, g_dn = gathered("b0", x1)
    full["w_up"][0], full["w_down"][0] = _from_slots(g_up, "cols2"), _from_slots(g_dn, "rows2")
    x2, ffn_saved0 = conv_ffn_fwd(x1, 0)
    g_wb, g_wkv = gathered("a3", x2)
    wb, wkv = _from_slots(g_wb, "rows2"), _from_slots(g_wkv, "cols2")
    kv, h_kv =_mm_fwd(x2, wkv, name="kv_proj", tm=min(1024, T), tn=1536, out_dtype=BF16, g=ln_kv_g, save_h=True)
    proj_b, h_mix1 = _mm_fwd(x2, wb, name="in_proj_b", tm=min(1024, T), tn=1024, out_dtype=BF16, g=ln_mix_g[1],
                             save_h=True)
    o_main1, rt1 = _sb_fwd_g(proj_b, kv, kv, name="sb_fwd", B=B, S=S, P=PM, q_cb=0, k_cb=0, v_cb=PM,
                             G=FWD_HEAD_GROUP)
    memkv1, h_mem1 = mem_kv(1)
    o_mem1, lse_m1 = _mem_fwd(proj_b, memkv1, name="mem_fwd1", B=B, S=S, NM=NM, q_cb=PM)
    o_cat1 = jnp.concatenate([o_main1, o_mem1], axis=1)
    x3 = _mm_fwd(o_cat1, full["w_out"][1], name="out_proj1", tm=min(1024, T), tn=1024, out_dtype=F32, res=x2)
    g_up, g_dn = gathered("b1", x3)
    full["w_up"][1], full["w_down"][1] = _from_slots(g_up, "cols2"), _from_slots(g_dn, "rows2")
    x4, ffn_saved1 = conv_ffn_fwd(x3, 1)
    dx4, dg_final, loss_part = _loss_head(x4, final_g, tgt2d, name="loss_head")

    grads = {}
    small = {}
    reduce_groups = []

    def start_reduce(gname, keys, kinds):
        slots = [_to_slots(grads[k], kind) for k, kind in zip(keys, kinds)]
        ssem, rsem, thru, lands, tok = _scatter_start(slots, _slot, name=f"reduce_start_{gname}")
        reduce_groups.append((gname, keys, ssem, rsem, thru, lands))
        return tok[0, 0]

    def conv_ffn_bwd(dxo, xin, saved, layer):
        uc, ub, h, a = saved
        w_dn = full["w_down"][layer]
        da = _mm_nt(dxo, w_dn, name=f"d_act{layer}", tm=min(1024, T), tn=F // 2, out_dtype=F32)
        grads[("w_down", layer)] = _wgrad(a, dxo, f"g_w_down{layer}")
        cwl = conv_w_full[layer]
        du_g, du_v, p_g, p_v = _conv_gate_bwd(da, uc, ub, cwl, name=f"conv_bwd{layer}", B=B, S=S)
        small[("conv_w", layer)] = jnp.concatenate([p_g[0:3], p_v[0:3]], axis=1)
        small[("conv_b", layer)] = jnp.concatenate([p_g[3], p_v[3]], axis=0)
        grads[("w_up", layer)] = jnp.concatenate(
            [_wgrad(h, du_g, f"g_w_up_gate{layer}"), _wgrad(h, du_v, f"g_w_up_val{layer}")], axis=1)
        tok = start_reduce(f"ffn{layer}", [("w_down", layer), ("w_up", layer)], ["rows2", "cols2"])
        dxi, dg = _mm_nt_rmsbwd([(du_g, 0), (du_v, 1)], full["w_up"][layer], xin, ln_ffn_g[layer] + tok,
                                name=f"d_ffn_in{layer}", dres=dxo)
        small[("ln_ffn_g", layer)] = dg[0]
        return dxi

    def mem_bwd(proj, q_cb, memkv, h_mem, do_cat, o_mem, lse_m, layer):
        dqm, dmk, dmv = _mem_bwd(proj, memkv, do_cat, o_mem, lse_m, name=f"mem_bwd{layer}", B=B, S=S, NM=NM,
                                 q_cb=q_cb, do_cb=PM)
        grads[("w_memkv", layer)] = jnp.concatenate(
            [_wgrad(h_mem, dmk, f"g_w_memk{layer}"), _wgrad(h_mem, dmv, f"g_w_memv{layer}")], axis=1)
        _, dg = _mm_nt_rmsbwd([(dmk, 0), (dmv, 1)], full["w_memkv"][layer], mem2d, ln_mem_g[layer],
                              name=f"d_mem_in{layer}", want_dx=False)
        small[("ln_mem_g", layer)] = dg[0]
        return dqm

    dx3 = conv_ffn_bwd(dx4, x3, ffn_saved1, 1)
    do_cat1 = _mm_nt(dx3, full["w_out"][1], name="d_o_cat1", tm=min(1024, T), tn=1024, out_dtype=BF16)
    grads[("w_out", 1)] = _wgrad(o_cat1, dx3, "g_w_out1")
    dq1, dk1, dv1 = _sb_bwd_g(proj_b, kv, kv, do_cat1, rt1, name="sb_bwd", B=B, S=S, P=PM, q_cb=0, k_cb=0, v_cb=PM,
                            do_cb=0)
    dqm1 = mem_bwd(proj_b, PM, memkv1, h_mem1, do_cat1, o_mem1, lse_m1, 1)
    grads["w_in_b"] = jnp.concatenate([_wgrad(h_mix1, dq1, "g_w_in_b_q"), _wgrad(h_mix1, dqm1, "g_w_in_b_m")], axis=1)
    grads["w_kv"] = jnp.concatenate([_wgrad(h_kv, dk1, "g_w_kv_k"), _wgrad(h_kv, dv1, "g_w_kv_v")], axis=1)
    tok = start_reduce("mix1", [("w_out", 1), "w_in_b", "w_kv", ("w_memkv", 1)], ["rows2", "rows2", "cols2", "rows2"])
    dx2, dg = _mm_nt_rmsbwd([(dq1, 0), (dqm1, MAIN_W // MEM_W)], wb, x2, ln_mix_g[1] + tok, name="d_mix_in1", dres=dx3)
    small[("ln_mix_g", 1)] = dg[0]
    dx2, dg = _mm_nt_rmsbwd([(dk1, 0), (dv1, 1)], wkv, x2, ln_kv_g, name="d_kv_in", dres=dx2)
    small["ln_kv_g"] = dg[0]
    dx1 = conv_ffn_bwd(dx2, x1, ffn_saved0, 0)
    do_cat0 = _mm_nt(dx1, full["w_out"][0], name="d_o_cat0", tm=min(1024, T), tn=1024, out_dtype=BF16)
    grads[("w_out", 0)] = _wgrad(o_cat0, dx1, "g_w_out0")
    dq0, dk0, dv0, dcs = _fox_bwd_g(proj_a, proj_a, proj_a, do_cat0, lse0, cr, name="fox_bwd", B=B, S=S, P=PM, q_cb=0,
                                  k_cb=PM, v_cb=2 * PM, do_cb=0)
    dqm0 = mem_bwd(proj_a, 3 * PM, memkv0, h_mem0, do_cat0, o_mem0, lse_m0, 0)
    dc2d = _pad_rows(dcs[:, :, 0, :].transpose(0, 2, 1).reshape(T, N_MAIN_HEADS), LANES, 1)
    df, db_f = _forget_cumsum_bwd(dc2d, f_logit, b_f, B=B, S=S, name="forget_cumsum_bwd")
    a_parts = [(dq0, 0), (dk0, 1), (dv0, 2), (dqm0, n_qkv // MEM_W), (df, n_main // LANES)]
    g_wa = jnp.concatenate([_wgrad(h_mix0, p, f"g_w_in_a{k}") for k, (p, _) in enumerate(a_parts)], axis=1)
    grads["w_in_a"] = jnp.concatenate([g_wa[:, :n_qkv], g_wa[:, n_main:n_main + N_MAIN_HEADS], g_wa[:, n_qkv:n_main]],
                                      axis=1)
    tok = start_reduce("mix0", [("w_out", 0), ("w_memkv", 0), "w_in_a"], ["rows2", "rows2", "rows2"])
    dx0, dg = _mm_nt_rmsbwd(a_parts, wa, x2d, ln_mix_g[0] + tok, name="d_mix_in0", dres=dx1)
    small[("ln_mix_g", 0)] = dg[0]
    grad_x = dx0.reshape(B, S, D)

    def both_small(name):
        return jnp.stack([small[(name, 0)], small[(name, 1)]])

    small_list = [("ln_mix_g", both_small("ln_mix_g")), ("b_f_a", db_f[:, :N_MAIN_HEADS]), ("ln_kv_g", small["ln_kv_g"]),
                  ("ln_mem_g", both_small("ln_mem_g")), ("ln_ffn_g", both_small("ln_ffn_g")),
                  ("conv_w", both_small("conv_w")), ("conv_b", both_small("conv_b")), ("final_g", dg_final[0]),
                  ("loss", loss_part[0, :1])]
    sm_rows = []
    for _, a in small_list:
        flat = a.reshape(-1)
        sm_rows.append(_pad_rows(flat, _round_up(flat.size, 8 * LANES), 0).reshape(-1, LANES))
    spack = jnp.concatenate(sm_rows, axis=0)
    s_ssem, s_rsem, s_thru, s_lands, s_tok = _scatter_start([spack], _whole, name="small_start")

    pieces = {}
    for gname, keys, ssem, rsem, thru, lands in reduce_groups:
        thru, lands = _scatter_wait(ssem, rsem, thru, lands, _slot, s_tok, name=f"reduce_wait_{gname}")
        for key, mine, land in zip(keys, thru, lands):
            own = lax.dynamic_index_in_dim(mine, my_idx, 0, keepdims=False)
            land = lax.dynamic_update_index_in_dim(land, own, my_idx, 0)
            tag = key if isinstance(key, str) else f"{key[0]}{key[1]}"
            pieces[key] = _sum_slots(land, name=f"sum_{tag}")

    red = {}
    for n in ("w_in_a", "w_in_b", "w_kv"):
        red[n] = pieces[n].reshape(shards[n].shape)
    for n in ("w_memkv", "w_out", "w_up", "w_down"):
        red[n] = jnp.stack([pieces[(n, 0)], pieces[(n, 1)]])

    weights = {"ln_mix_g": ln_mix_g, "w_in_a": w_in_a, "b_f_a": b_f_a, "w_in_b": w_in_b, "ln_kv_g": ln_kv_g,
               "w_kv": w_kv, "ln_mem_g": ln_mem_g, "w_memkv": w_memkv, "w_out": w_out, "ln_ffn_g": ln_ffn_g,
               "w_up": w_up, "conv_w": conv_w, "conv_b": conv_b, "w_down": w_down, "final_g": final_g}
    m_in = {"ln_mix_g": m_ln_mix_g, "w_in_a": m_w_in_a, "b_f_a": m_b_f_a, "w_in_b": m_w_in_b, "ln_kv_g": m_ln_kv_g,
            "w_kv": m_w_kv, "ln_mem_g": m_ln_mem_g, "w_memkv": m_w_memkv, "w_out": m_w_out, "ln_ffn_g": m_ln_ffn_g,
            "w_up": m_w_up, "conv_w": m_conv_w, "conv_b": m_conv_b, "w_down": m_w_down, "final_g": m_final_g}
    v_in = {"ln_mix_g": v_ln_mix_g, "w_in_a": v_w_in_a, "b_f_a": v_b_f_a, "w_in_b": v_w_in_b, "ln_kv_g": v_ln_kv_g,
            "w_kv": v_w_kv, "ln_mem_g": v_ln_mem_g, "w_memkv": v_w_memkv, "w_out": v_w_out, "ln_ffn_g": v_ln_ffn_g,
            "w_up": v_w_up, "conv_w": v_conv_w, "conv_b": v_conv_b, "w_down": v_w_down, "final_g": v_final_g}
    order = list(weights)
    big_names = [n for n, _ in BIG]
    g_out, d_out, nm_out, nv_out = {}, {}, {}, {}

    def update(n):
        w = weights[n]
        cols = w.shape[-1]
        g = red[n].reshape(w.shape)
        d, nm, nv = _adamw(w.reshape(-1, cols), g.reshape(-1, cols), m_in[n].reshape(-1, cols),
                           v_in[n].reshape(-1, cols), name=f"adamw_{n}")
        g_out[n], d_out[n], nm_out[n], nv_out[n] = g, d.reshape(w.shape), nm.reshape(w.shape), nv.reshape(w.shape)

    for n in big_names:
        update(n)
    all_updated = jnp.stack([d_out[n].reshape(-1)[0] for n in big_names])
    s_thru, s_lands = _scatter_wait(s_ssem, s_rsem, s_thru, s_lands, _whole, all_updated, name="small_wait")
    ssum = _sum_slots(lax.dynamic_update_index_in_dim(s_lands[0], s_thru[0], my_idx, 0), name="sum_small")
    off = 0
    for (n, a), rows in zip(small_list, sm_rows):
        red[n] = ssum[off:off + rows.shape[0]].reshape(-1)[:a.size].reshape(a.shape)
        off += rows.shape[0]
    loss = red["loss"][0]
    shard_cols = conv_w.shape[2]
    red["conv_w"] = lax.dynamic_slice_in_dim(red["conv_w"], my_idx * shard_cols, shard_cols, axis=2)
    red["b_f_a"] = red["b_f_a"].reshape(b_f_a.shape)
    update("conv_w")
    small_names = [n for n in order if n not in g_out]

    def pack_small(src):
        rows = []
        for n in small_names:
            flat = src[n].reshape(-1)
            rows.append(_pad_rows(flat, _round_up(flat.size, 8 * LANES), 0).reshape(-1, LANES))
        return jnp.concatenate(rows, axis=0), [r.shape[0] for r in rows]

    red_small = {n: red[n].reshape(weights[n].shape) for n in small_names}
    wp, counts = pack_small(weights)
    gp, _ = pack_small(red_small)
    mp, _ = pack_small(m_in)
    vp, _ = pack_small(v_in)
    dp, nmp, nvp = _adamw(wp, gp, mp, vp, name="adamw_small")
    off = 0
    for n, cnt in zip(small_names, counts):
        shp = weights[n].shape
        size = weights[n].size
        g_out[n] = red_small[n]
        d_out[n] = dp[off:off + cnt].reshape(-1)[:size].reshape(shp)
        nm_out[n] = nmp[off:off + cnt].reshape(-1)[:size].reshape(shp)
        nv_out[n] = nvp[off:off + cnt].reshape(-1)[:size].reshape(shp)
        off += cnt

    return (loss, grad_x, *[g_out[n] for n in order], *[d_out[n] for n in order],
            *[nm_out[n] for n in order], *[nv_out[n] for n in order])
```

```python
import functools

import jax
import jax.numpy as jnp
from jax import lax
from jax.experimental import pallas as pl
from jax.experimental.pallas import tpu as pltpu

F32 = jnp.float32
BF16 = jnp.bfloat16
LANES = 128
HEAD_DIM = 64
N_MAIN_HEADS = 12
N_MEM_HEADS = 4
MAIN_W = N_MAIN_HEADS * HEAD_DIM
MEM_W = N_MEM_HEADS * HEAD_DIM
SCALE = HEAD_DIM ** -0.5
EPS = 1e-6
NEG = -1e30
N_DEV = 8
ATT_TILE = 256
MEM_Q_TILE = 1024
VMEM_BIG = 56 * 2 ** 20
MESH = pl.DeviceIdType.MESH

ADAM_LR = 0.001
ADAM_B1 = 0.9
ADAM_B2 = 0.999
ADAM_EPS = 1e-08
ADAM_WD = 0.01
ADAM_STEP = 10

NT = (((1,), (1,)), ((), ()))
TN = (((0,), (0,)), ((), ()))


def _pc(body, *, name, out_shape, grid=None, in_specs=None, out_specs=None, scratch_shapes=(),
        semantics=None, vmem=None):
    kw = {}
    if grid is not None:
        kw["grid"] = grid
    params = pltpu.CompilerParams(dimension_semantics=semantics, vmem_limit_bytes=vmem)
    return pl.pallas_call(body, name=name, out_shape=out_shape, in_specs=in_specs, out_specs=out_specs,
                          scratch_shapes=list(scratch_shapes), compiler_params=params, **kw)


def _sds(shape, dtype):
    return jax.ShapeDtypeStruct(shape, dtype)


def _mm_fwd(a, w, *, name, tm, tn, out_dtype, g=None, res=None, col0=0, ncols=None, save_h=False):
    m_rows, k = a.shape
    n = w.shape[1] if ncols is None else ncols
    grid = (m_rows // tm, n // tn)
    norm = g is not None

    def body(*refs):
        refs = list(refs)
        a_ref = refs.pop(0)
        g_ref = refs.pop(0) if norm else None
        w_ref = refs.pop(0)
        res_ref = refs.pop(0) if res is not None else None
        o_ref = refs.pop(0)
        hout_ref = refs.pop(0) if save_h else None
        h_ref = refs.pop(0) if norm else None
        if norm:
            @pl.when(pl.program_id(1) == 0)
            def _():
                xv = a_ref[...]
                r = lax.rsqrt(jnp.mean(xv * xv, axis=-1, keepdims=True) + EPS)
                h = ((xv * r) * g_ref[...]).astype(BF16)
                h_ref[...] = h
                if save_h:
                    hout_ref[...] = h
            lhs = h_ref[...]
        else:
            lhs = a_ref[...].astype(BF16)
        acc = jnp.dot(lhs, w_ref[...], preferred_element_type=F32)
        if res is not None:
            acc = acc + res_ref[...]
        o_ref[...] = acc.astype(out_dtype)

    in_specs = [pl.BlockSpec((tm, k), lambda i, j: (i, 0))]
    args = [a]
    if norm:
        in_specs.append(pl.BlockSpec((1, k), lambda i, j: (0, 0)))
        args.append(g.reshape(1, k))
    in_specs.append(pl.BlockSpec((k, tn), lambda i, j: (0, j + col0)))
    args.append(w)
    if res is not None:
        in_specs.append(pl.BlockSpec((tm, tn), lambda i, j: (i, j)))
        args.append(res)
    out_shape = [_sds((m_rows, n), out_dtype)]
    out_specs = [pl.BlockSpec((tm, tn), lambda i, j: (i, j))]
    if save_h:
        out_shape.append(_sds((m_rows, k), BF16))
        out_specs.append(pl.BlockSpec((tm, k), lambda i, j: (i, 0)))
    scratch = [pltpu.VMEM((tm, k), BF16)] if norm else []
    outs = _pc(body, name=name, out_shape=out_shape, grid=grid, in_specs=in_specs, out_specs=out_specs,
               scratch_shapes=scratch, semantics=("arbitrary", "arbitrary"), vmem=VMEM_BIG)(*args)
    return outs if save_h else outs[0]


def _rmsnorm(x, g, *, name):
    m_rows, d = x.shape
    tm = min(1024, m_rows)

    def body(x_ref, g_ref, o_ref):
        xv = x_ref[...]
        r = lax.rsqrt(jnp.mean(xv * xv, axis=-1, keepdims=True) + EPS)
        o_ref[...] = ((xv * r) * g_ref[...]).astype(BF16)

    row = pl.BlockSpec((tm, d), lambda i: (i, 0))
    return _pc(body, name=name, out_shape=_sds((m_rows, d), BF16), grid=(m_rows // tm,),
               in_specs=[row, pl.BlockSpec((1, d), lambda i: (0, 0))], out_specs=row,
               semantics=("arbitrary",), vmem=VMEM_BIG)(x, g.reshape(1, d))


def _mm_nt(a, w, *, name, tm, tn, out_dtype):
    m_rows, k = a.shape
    n = w.shape[0]

    def body(a_ref, w_ref, o_ref):
        acc = lax.dot_general(a_ref[...].astype(BF16), w_ref[...], NT, preferred_element_type=F32)
        o_ref[...] = acc.astype(out_dtype)

    return _pc(body, name=name, out_shape=_sds((m_rows, n), out_dtype), grid=(m_rows // tm, n // tn),
               in_specs=[pl.BlockSpec((tm, k), lambda i, j: (i, 0)), pl.BlockSpec((tn, k), lambda i, j: (j, 0))],
               out_specs=pl.BlockSpec((tm, tn), lambda i, j: (i, j)),
               semantics=("arbitrary", "arbitrary"), vmem=VMEM_BIG)(a, w)


def _mm_tn(a, b, *, name, ta, tn, tt):
    t_rows, ka = a.shape
    n = b.shape[1]
    nt = t_rows // tt

    def body(a_ref, b_ref, o_ref, acc_ref):
        t = pl.program_id(2)

        @pl.when(t == 0)
        def _():
            acc_ref[...] = jnp.zeros_like(acc_ref)

        acc_ref[...] += lax.dot_general(a_ref[...].astype(BF16), b_ref[...].astype(BF16), TN,
                                        preferred_element_type=F32)

        @pl.when(t == nt - 1)
        def _():
            o_ref[...] = acc_ref[...].astype(BF16)

    return _pc(body, name=name, out_shape=_sds((ka, n), BF16), grid=(ka // ta, n // tn, nt),
               in_specs=[pl.BlockSpec((tt, ta), lambda i, j, t: (t, i)),
                         pl.BlockSpec((tt, tn), lambda i, j, t: (t, j))],
               out_specs=pl.BlockSpec((ta, tn), lambda i, j, t: (i, j)),
               scratch_shapes=[pltpu.VMEM((ta, tn), F32)],
               semantics=("arbitrary", "arbitrary", "arbitrary"), vmem=VMEM_BIG)(a, b)


def _wgrad(a, b, name):
    t_rows, ka = a.shape
    n = b.shape[1]
    ta = ka if ka <= 1024 else ka // 2
    tn = n
    while ta * tn * 4 > 6 * 2 ** 20 and tn % 256 == 0:
        tn //= 2
    tt = min(2048, t_rows)
    return _mm_tn(a, b, name=name, ta=ta, tn=tn, tt=tt)


def _mm_nt_rmsbwd(parts, w, x, g, *, name, dres=None, want_dx=True):
    m_rows, d = x.shape
    tm = min(512, m_rows)
    n_parts = len(parts)

    def body(*refs):
        refs = list(refs)
        dy_refs = [refs.pop(0) for _ in range(n_parts)]
        w_refs = [refs.pop(0) for _ in range(n_parts)]
        x_ref = refs.pop(0)
        g_ref = refs.pop(0)
        dres_ref = refs.pop(0) if dres is not None else None
        dx_ref = refs.pop(0) if want_dx else None
        dg_ref = refs.pop(0)

        @pl.when(pl.program_id(0) == 0)
        def _():
            dg_ref[...] = jnp.zeros_like(dg_ref)

        dh = None
        for dy_ref, w_ref in zip(dy_refs, w_refs):
            t = lax.dot_general(dy_ref[...].astype(BF16), w_ref[...], NT, preferred_element_type=F32)
            dh = t if dh is None else dh + t
        xv = x_ref[...]
        r = lax.rsqrt(jnp.mean(xv * xv, axis=-1, keepdims=True) + EPS)
        xh = xv * r
        dg_ref[...] += jnp.sum(dh * xh, axis=0, keepdims=True)
        if want_dx:
            dhg = dh * g_ref[...]
            dx = r * (dhg - xh * jnp.mean(dhg * xh, axis=-1, keepdims=True))
            if dres is not None:
                dx = dx + dres_ref[...]
            dx_ref[...] = dx

    in_specs, args = [], []
    for dy, _ in parts:
        in_specs.append(pl.BlockSpec((tm, dy.shape[1]), lambda i: (i, 0)))
        args.append(dy)
    for dy, cb in parts:
        in_specs.append(pl.BlockSpec((d, dy.shape[1]), functools.partial(lambda i, cb: (0, cb), cb=cb)))
        args.append(w)
    in_specs += [pl.BlockSpec((tm, d), lambda i: (i, 0)), pl.BlockSpec((1, d), lambda i: (0, 0))]
    args += [x, g.reshape(1, d)]
    if dres is not None:
        in_specs.append(pl.BlockSpec((tm, d), lambda i: (i, 0)))
        args.append(dres)
    out_shape, out_specs = [], []
    if want_dx:
        out_shape.append(_sds((m_rows, d), F32))
        out_specs.append(pl.BlockSpec((tm, d), lambda i: (i, 0)))
    out_shape.append(_sds((1, d), F32))
    out_specs.append(pl.BlockSpec((1, d), lambda i: (0, 0)))
    outs = _pc(body, name=name, out_shape=out_shape, grid=(m_rows // tm,), in_specs=in_specs,
               out_specs=out_specs, semantics=("arbitrary",), vmem=VMEM_BIG)(*args)
    return (outs[0], outs[1]) if want_dx else (None, outs[0])


def _loss_head(x, g, tgt, *, name):
    m_rows, d = x.shape
    tm = min(512, m_rows)

    def body(x_ref, g_ref, t_ref, dx_ref, dg_ref, loss_ref):
        @pl.when(pl.program_id(0) == 0)
        def _():
            dg_ref[...] = jnp.zeros_like(dg_ref)
            loss_ref[...] = jnp.zeros_like(loss_ref)

        xv = x_ref[...]
        r = lax.rsqrt(jnp.mean(xv * xv, axis=-1, keepdims=True) + EPS)
        xh = xv * r
        gv = g_ref[...]
        err = xh * gv - t_ref[...]
        per_tok = jnp.mean(err * err, axis=-1, keepdims=True)
        loss_ref[...] += 0.5 * jnp.sum(per_tok, axis=0, keepdims=True)
        dout = err * (1.0 / d)
        dg_ref[...] += jnp.sum(dout * xh, axis=0, keepdims=True)
        dhg = dout * gv
        dx_ref[...] = r * (dhg - xh * jnp.mean(dhg * xh, axis=-1, keepdims=True))

    row = pl.BlockSpec((tm, d), lambda i: (i, 0))
    return _pc(body, name=name, out_shape=[_sds((m_rows, d), F32), _sds((1, d), F32), _sds((1, LANES), F32)],
               grid=(m_rows // tm,), in_specs=[row, pl.BlockSpec((1, d), lambda i: (0, 0)), row],
               out_specs=[row, pl.BlockSpec((1, d), lambda i: (0, 0)), pl.BlockSpec((1, LANES), lambda i: (0, 0))],
               semantics=("arbitrary",))(x, g.reshape(1, d), tgt)


def _split3(v):
    hi = v.astype(BF16)
    r1 = v - hi.astype(F32)
    mid = r1.astype(BF16)
    lo = (r1 - mid.astype(F32)).astype(BF16)
    return hi, mid, lo


def _split2(v):
    hi = v.astype(BF16)
    lo = (v - hi.astype(F32)).astype(BF16)
    return hi, lo


def _tri_dot3(tri, v):
    hi, mid, lo = _split3(v)
    return (jnp.dot(tri, hi, preferred_element_type=F32) + jnp.dot(tri, mid, preferred_element_type=F32)
            + jnp.dot(tri, lo, preferred_element_type=F32))


def _log_sigmoid(v):
    return jnp.minimum(v, 0.0) - jnp.log(1.0 + jnp.exp(-jnp.abs(v)))


def _forget_cumsum(f_logit, b_f, *, B, S, name):
    ch = min(256, S)
    nch = S // ch

    def body(f_ref, b_ref, c_ref):
        r_i = lax.broadcasted_iota(jnp.int32, (ch, ch), 0)
        c_i = lax.broadcasted_iota(jnp.int32, (ch, ch), 1)
        tri = (c_i <= r_i).astype(BF16)
        bv = b_ref[...]

        def step(k, carry):
            rows = pl.ds(pl.multiple_of(k * ch, ch), ch)
            lf = _log_sigmoid(f_ref[rows, :] + bv)
            c_ref[rows, :] = _tri_dot3(tri, lf) + carry
            return carry + jnp.sum(lf, axis=0, keepdims=True)

        lax.fori_loop(0, nch, step, jnp.zeros((1, LANES), F32))

    blk = pl.BlockSpec((S, LANES), lambda b: (b, 0))
    return _pc(body, name=name, out_shape=_sds((B * S, LANES), F32), grid=(B,),
               in_specs=[blk, pl.BlockSpec((1, LANES), lambda b: (0, 0))], out_specs=blk,
               semantics=("arbitrary",))(f_logit, b_f)


def _forget_cumsum_bwd(dc, f_logit, b_f, *, B, S, name):
    ch = min(256, S)
    nch = S // ch

    def body(dc_ref, f_ref, b_ref, df_ref, db_ref):
        @pl.when(pl.program_id(0) == 0)
        def _():
            db_ref[...] = jnp.zeros_like(db_ref)

        r_i = lax.broadcasted_iota(jnp.int32, (ch, ch), 0)
        c_i = lax.broadcasted_iota(jnp.int32, (ch, ch), 1)
        tri = (c_i >= r_i).astype(BF16)
        bv = b_ref[...]

        def step(kk, carry):
            tail, dbs = carry
            k = nch - 1 - kk
            rows = pl.ds(pl.multiple_of(k * ch, ch), ch)
            dcv = dc_ref[rows, :]
            dlf = _tri_dot3(tri, dcv) + tail
            z = f_ref[rows, :] + bv
            df = dlf * (1.0 / (1.0 + jnp.exp(z)))
            df_ref[rows, :] = df.astype(BF16)
            return tail + jnp.sum(dcv, axis=0, keepdims=True), dbs + jnp.sum(df, axis=0, keepdims=True)

        zero = jnp.zeros((1, LANES), F32)
        _, dbs = lax.fori_loop(0, nch, step, (zero, zero))
        db_ref[...] += dbs

    blk = pl.BlockSpec((S, LANES), lambda b: (b, 0))
    one = pl.BlockSpec((1, LANES), lambda b: (0, 0))
    return _pc(body, name=name, out_shape=[_sds((B * S, LANES), BF16), _sds((1, LANES), F32)], grid=(B,),
               in_specs=[blk, blk, one], out_specs=[blk, one], semantics=("arbitrary",))(dc, f_logit, b_f)


def _head_mask(lane, hh):
    return (lane < HEAD_DIM) if hh == 0 else (lane >= HEAD_DIM)


HEAD_GROUP = 3
FWD_HEAD_GROUP = 6


def _g_col_spec(rows, nblk_rows, cb, G):
    return pl.BlockSpec((rows, G * LANES), lambda b, p, i: (b * nblk_rows + i, cb // G + p))


def _g_kv_spec(rows, cb, G):
    return pl.BlockSpec((rows, G * LANES), lambda b, p, i: (b, cb // G + p))


def _g_stat_col_spec(tq, G):
    return pl.BlockSpec((1, 2 * G, tq, 1), lambda b, p, i: (b, p, i, 0))


def _g_stat_row_spec(S, G):
    return pl.BlockSpec((1, 2 * G, 1, S), lambda b, p, i: (b, p, 0, 0))


def _lanes(g):
    return slice(g * LANES, (g + 1) * LANES)


def _streams(x_ref, G, scale=None):
    rows = x_ref.shape[0]
    lane = lax.broadcasted_iota(jnp.int32, (rows, LANES), 1)
    out = []
    for g in range(G):
        x = x_ref[:, _lanes(g)]
        if scale is not None:
            x = x * jnp.asarray(scale, x.dtype)
        for hh in range(2):
            out.append(jnp.where(_head_mask(lane, hh), x, jnp.zeros_like(x)))
    return lane, out


def _wide(stat, width):
    return jnp.tile(stat, (1, width // LANES))


def _fold_lanes(v):
    out = v[:, :LANES]
    for j in range(1, v.shape[1] // LANES):
        out = out + v[:, j * LANES:(j + 1) * LANES]
    return out


def _kv_blocks(ref, ks, tk, G):
    return [ref[pl.ds(ks, tk), _lanes(g)] for g in range(G)]


def _sweep(i, block):
    def step(kb, c):
        block(kb, False)
        return c
    lax.fori_loop(0, i, step, 0)
    block(i, True)


def _fox_fwd_g(qa, ka, va, cr, *, name, B, S, P, q_cb, k_cb, v_cb, G=HEAD_GROUP):
    tq = tk = min(ATT_TILE, S)
    nq = S // tq
    NS = 2 * G

    def body(q_ref, k_ref, v_ref, cr_ref, o_ref, lse_ref, acc_ref, m_ref, l_ref):
        i = pl.program_id(2)
        lane, qh = _streams(q_ref, G, SCALE)
        on_or_below = (lax.broadcasted_iota(jnp.int32, (tq, tk), 1) <= lax.broadcasted_iota(jnp.int32, (tq, tk), 0))
        m_ref[...] = jnp.full(m_ref.shape, NEG, F32)
        l_ref[...] = jnp.zeros(l_ref.shape, F32)
        acc_ref[...] = jnp.zeros(acc_ref.shape, F32)

        def block(kb, diag):
            ks = pl.multiple_of(kb * tk, tk)
            kblk = _kv_blocks(k_ref, ks, tk, G)
            vblk = _kv_blocks(v_ref, ks, tk, G)
            ss = [lax.dot_general(qh[st], kblk[st // 2], NT, preferred_element_type=F32) for st in range(NS)]
            ps = []
            for st in range(NS):
                s = ss[st] - cr_ref[0, st, :, pl.ds(ks, tk)]
                if diag:
                    s = jnp.where(on_or_below, s, NEG)
                m = m_ref[st]
                m_new = jnp.maximum(m, jnp.max(s, axis=-1, keepdims=True))
                alpha = jnp.exp(m - m_new)
                p = jnp.exp(s - _wide(m_new, tk))
                m_ref[st] = m_new
                l_ref[st] = alpha * l_ref[st] + _fold_lanes(p)
                ps.append((alpha, p.astype(BF16)))
            pvs = [jnp.dot(ps[st][1], vblk[st // 2], preferred_element_type=F32) for st in range(NS)]
            for st in range(NS):
                acc_ref[st] = ps[st][0] * acc_ref[st] + pvs[st]

        _sweep(i, block)
        ls = [jnp.sum(l_ref[st], axis=-1, keepdims=True) for st in range(NS)]
        for st in range(NS):
            lse_ref[0, st] = jnp.max(m_ref[st], axis=-1, keepdims=True) + jnp.log(ls[st])
        for g in range(G):
            o_ref[:, _lanes(g)] = jnp.where(lane < HEAD_DIM, acc_ref[2 * g] / ls[2 * g],
                                            acc_ref[2 * g + 1] / ls[2 * g + 1]).astype(BF16)

    return _pc(body, name=name, out_shape=[_sds((B * S, P * LANES), BF16), _sds((B, 2 * P, S, 1), F32)],
               grid=(B, P // G, nq),
               in_specs=[_g_col_spec(tq, nq, q_cb, G), _g_kv_spec(S, k_cb, G), _g_kv_spec(S, v_cb, G),
                         _g_stat_row_spec(S, G)],
               out_specs=[_g_col_spec(tq, nq, 0, G), _g_stat_col_spec(tq, G)],
               scratch_shapes=[pltpu.VMEM((NS, tq, LANES), F32)] * 3,
               semantics=("arbitrary", "arbitrary", "arbitrary"), vmem=VMEM_BIG)(qa, ka, va, cr)


def _fox_bwd_g(qa, ka, va, doa, lse, cr, *, name, B, S, P, q_cb, k_cb, v_cb, do_cb, G=HEAD_GROUP):
    tq = tk = min(ATT_TILE, S)
    nq = S // tq
    NS = 2 * G

    def body(q_ref, k_ref, v_ref, do_ref, lse_ref, cr_ref, dq_ref, dk_ref, dv_ref, dcs_ref, dqa_ref, delta_ref, lse_s,
             p_buf, dp_buf):
        i = pl.program_id(2)

        @pl.when(i == 0)
        def _():
            dk_ref[...] = jnp.zeros_like(dk_ref)
            dv_ref[...] = jnp.zeros_like(dv_ref)
            dcs_ref[...] = jnp.zeros_like(dcs_ref)

        lane, qh = _streams(q_ref, G, SCALE)
        _, doh = _streams(do_ref, G)
        on_or_below = (lax.broadcasted_iota(jnp.int32, (tq, tk), 1) <= lax.broadcasted_iota(jnp.int32, (tq, tk), 0))
        delta_ref[...] = jnp.zeros(delta_ref.shape, F32)
        dqa_ref[...] = jnp.zeros(dqa_ref.shape, F32)
        for st in range(NS):
            lse_s[st] = jnp.broadcast_to(lse_ref[0, st], (tq, LANES))

        def probs(kb, diag):
            ks = pl.multiple_of(kb * tk, tk)
            kblk = _kv_blocks(k_ref, ks, tk, G)
            vblk = _kv_blocks(v_ref, ks, tk, G)
            ss = [lax.dot_general(qh[st], kblk[st // 2], NT, preferred_element_type=F32) for st in range(NS)]
            dps = [lax.dot_general(doh[st], vblk[st // 2], NT, preferred_element_type=F32) for st in range(NS)]
            ps = []
            for st in range(NS):
                s = ss[st] - cr_ref[0, st, :, pl.ds(ks, tk)]
                if diag:
                    s = jnp.where(on_or_below, s, NEG)
                ps.append(jnp.exp(s - _wide(lse_s[st], tk)))
            return ks, kblk, ps, dps

        def delta_block(kb, diag):
            _, _, ps, dps = probs(kb, diag)
            for st in range(NS):
                delta_ref[st] += _fold_lanes(ps[st] * dps[st])
                p_buf[st, kb] = ps[st]
                dp_buf[st, kb] = dps[st]

        _sweep(i, delta_block)
        for st in range(NS):
            delta_ref[st] = jnp.broadcast_to(jnp.sum(delta_ref[st], axis=-1, keepdims=True), (tq, LANES))

        def grad_block(kb, diag):
            ks = pl.multiple_of(kb * tk, tk)
            kblk = _kv_blocks(k_ref, ks, tk, G)
            rows = pl.ds(ks, tk)
            dsb, pb = [], []
            for st in range(NS):
                p = p_buf[st, kb]
                ds = p * (dp_buf[st, kb] - _wide(delta_ref[st], tk))
                dcs_ref[0, st, :, rows] -= jnp.sum(ds, axis=0, keepdims=True)
                dsb.append(ds.astype(BF16))
                pb.append(p.astype(BF16))
            dks = [lax.dot_general(dsb[st], qh[st], TN, preferred_element_type=F32) for st in range(NS)]
            dvs = [lax.dot_general(pb[st], doh[st], TN, preferred_element_type=F32) for st in range(NS)]
            dqs = [jnp.dot(dsb[st], kblk[st // 2], preferred_element_type=F32) for st in range(NS)]
            for g in range(G):
                dk_ref[rows, _lanes(g)] += dks[2 * g] + dks[2 * g + 1]
                dv_ref[rows, _lanes(g)] += dvs[2 * g] + dvs[2 * g + 1]
            for st in range(NS):
                dqa_ref[st] += dqs[st]

        _sweep(i, grad_block)
        for g in range(G):
            dq_ref[:, _lanes(g)] = (jnp.where(lane < HEAD_DIM, dqa_ref[2 * g], dqa_ref[2 * g + 1]) * SCALE).astype(BF16)

    return _pc(body, name=name,
               out_shape=[_sds((B * S, P * LANES), BF16), _sds((B * S, P * LANES), F32), _sds((B * S, P * LANES), F32),
                          _sds((B, 2 * P, 1, S), F32)],
               grid=(B, P // G, nq),
               in_specs=[_g_col_spec(tq, nq, q_cb, G), _g_kv_spec(S, k_cb, G), _g_kv_spec(S, v_cb, G),
                         _g_col_spec(tq, nq, do_cb, G), _g_stat_col_spec(tq, G), _g_stat_row_spec(S, G)],
               out_specs=[_g_col_spec(tq, nq, 0, G), _g_kv_spec(S, 0, G), _g_kv_spec(S, 0, G), _g_stat_row_spec(S, G)],
               scratch_shapes=[pltpu.VMEM((NS, tq, LANES), F32)] * 3 + [pltpu.VMEM((NS, nq, tq, tk), F32)] * 2,
               semantics=("arbitrary", "arbitrary", "arbitrary"), vmem=VMEM_BIG)(qa, ka, va, doa, lse, cr)


def _sb_logs_z(z):
    nz = -z
    lm = jnp.minimum(nz, 0.0) - jnp.log(1.0 + jnp.exp(jnp.minimum(z, nz)))
    return lm + z, lm


def _sb_fwd_g(qa, ka, va, *, name, B, S, P, q_cb, k_cb, v_cb, G=HEAD_GROUP):
    tq = tk = min(ATT_TILE, S)
    nq = S // tq
    NS = 2 * G

    def body(q_ref, k_ref, v_ref, o_ref, rt_ref, acc_ref, run_ref):
        i = pl.program_id(2)
        lane, qh = _streams(q_ref, G, SCALE)
        t_r = lax.broadcasted_iota(jnp.int32, (tk, tk), 0)
        t_c = lax.broadcasted_iota(jnp.int32, (tk, tk), 1)
        after = (t_r > t_c).astype(BF16)
        below = t_c < t_r
        acc_ref[...] = jnp.zeros(acc_ref.shape, F32)
        run_ref[...] = jnp.zeros(run_ref.shape, F32)

        def block(kb, diag):
            ks = pl.multiple_of(kb * tk, tk)
            kblk = _kv_blocks(k_ref, ks, tk, G)
            vblk = _kv_blocks(v_ref, ks, tk, G)
            zs = [lax.dot_general(qh[st], kblk[st // 2], NT, preferred_element_type=F32) for st in range(NS)]
            lss, parts = [], []
            for st in range(NS):
                ls, lm = _sb_logs_z(zs[st])
                if diag:
                    lm = jnp.where(below, lm, 0.0)
                lss.append(ls + _wide(run_ref[st], tk))
                run_ref[st] += jnp.sum(lm, axis=-1, keepdims=True)
                parts.append(_split2(lm))
            sufs = [jnp.dot(parts[st][0], after, preferred_element_type=F32)
                    + jnp.dot(parts[st][1], after, preferred_element_type=F32) for st in range(NS)]
            ab = []
            for st in range(NS):
                a = jnp.exp(lss[st] + sufs[st])
                if diag:
                    a = jnp.where(below, a, 0.0)
                ab.append(a.astype(BF16))
            pvs = [jnp.dot(ab[st], vblk[st // 2], preferred_element_type=F32) for st in range(NS)]
            for st in range(NS):
                acc_ref[st] += pvs[st]

        block(i, True)

        def step(jj, c):
            block(i - 1 - jj, False)
            return c

        lax.fori_loop(0, i, step, 0)
        for st in range(NS):
            rt_ref[0, st] = jnp.max(run_ref[st], axis=-1, keepdims=True)
        for g in range(G):
            o_ref[:, _lanes(g)] = jnp.where(lane < HEAD_DIM, acc_ref[2 * g], acc_ref[2 * g + 1]).astype(BF16)

    return _pc(body, name=name, out_shape=[_sds((B * S, P * LANES), BF16), _sds((B, 2 * P, S, 1), F32)],
               grid=(B, P // G, nq),
               in_specs=[_g_col_spec(tq, nq, q_cb, G), _g_kv_spec(S, k_cb, G), _g_kv_spec(S, v_cb, G)],
               out_specs=[_g_col_spec(tq, nq, 0, G), _g_stat_col_spec(tq, G)],
               scratch_shapes=[pltpu.VMEM((NS, tq, LANES), F32)] * 2,
               semantics=("arbitrary", "arbitrary", "arbitrary"), vmem=VMEM_BIG)(qa, ka, va)


def _sb_bwd_g(qa, ka, va, doa, rt, *, name, B, S, P, q_cb, k_cb, v_cb, do_cb, G=HEAD_GROUP):
    tq = tk = min(ATT_TILE, S)
    nq = S // tq
    NS = 2 * G

    def body(q_ref, k_ref, v_ref, do_ref, rt_ref, dq_ref, dk_ref, dv_ref, dqa_ref, pl_ref, pg_ref):
        i = pl.program_id(2)

        @pl.when(i == 0)
        def _():
            dk_ref[...] = jnp.zeros_like(dk_ref)
            dv_ref[...] = jnp.zeros_like(dv_ref)

        lane, qh = _streams(q_ref, G, SCALE)
        _, doh = _streams(do_ref, G)
        t_r = lax.broadcasted_iota(jnp.int32, (tk, tk), 0)
        t_c = lax.broadcasted_iota(jnp.int32, (tk, tk), 1)
        upto = (t_r <= t_c).astype(BF16)
        before = (t_r < t_c).astype(BF16)
        below = t_c < t_r
        dqa_ref[...] = jnp.zeros(dqa_ref.shape, F32)
        pg_ref[...] = jnp.zeros(pg_ref.shape, F32)
        for st in range(NS):
            pl_ref[st] = jnp.broadcast_to(rt_ref[0, st], (tq, LANES))

        def block(kb, diag):
            ks = pl.multiple_of(kb * tk, tk)
            rows = pl.ds(ks, tk)
            kblk = _kv_blocks(k_ref, ks, tk, G)
            vblk = _kv_blocks(v_ref, ks, tk, G)
            zs = [lax.dot_general(qh[st], kblk[st // 2], NT, preferred_element_type=F32) for st in range(NS)]
            das = [lax.dot_general(doh[st], vblk[st // 2], NT, preferred_element_type=F32) for st in range(NS)]
            lss, parts = [], []
            for st in range(NS):
                ls, lm = _sb_logs_z(zs[st])
                if diag:
                    lm = jnp.where(below, lm, 0.0)
                lss.append((ls, ls + _wide(pl_ref[st], tk)))
                pl_ref[st] -= jnp.sum(lm, axis=-1, keepdims=True)
                parts.append(_split2(lm))
            pins = [jnp.dot(parts[st][0], upto, preferred_element_type=F32)
                    + jnp.dot(parts[st][1], upto, preferred_element_type=F32) for st in range(NS)]
            gms, ab, gparts = [], [], []
            for st in range(NS):
                a = jnp.exp(lss[st][1] - pins[st])
                if diag:
                    a = jnp.where(below, a, 0.0)
                gm = a * das[st]
                gms.append(gm)
                ab.append(a.astype(BF16))
                gparts.append(gm.astype(BF16))
            pgs = [jnp.dot(gparts[st], before, preferred_element_type=F32) for st in range(NS)]
            dzb = []
            for st in range(NS):
                gm = gms[st]
                dz = gm - jnp.exp(lss[st][0]) * (gm + (pgs[st] + _wide(pg_ref[st], tk)))
                if diag:
                    dz = jnp.where(below, dz, 0.0)
                pg_ref[st] += jnp.sum(gm, axis=-1, keepdims=True)
                dzb.append(dz.astype(BF16))
            dks = [lax.dot_general(dzb[st], qh[st], TN, preferred_element_type=F32) for st in range(NS)]
            dvs = [lax.dot_general(ab[st], doh[st], TN, preferred_element_type=F32) for st in range(NS)]
            dqs = [jnp.dot(dzb[st], kblk[st // 2], preferred_element_type=F32) for st in range(NS)]
            for g in range(G):
                dk_ref[rows, _lanes(g)] += dks[2 * g] + dks[2 * g + 1]
                dv_ref[rows, _lanes(g)] += dvs[2 * g] + dvs[2 * g + 1]
            for st in range(NS):
                dqa_ref[st] += dqs[st]

        _sweep(i, block)
        for g in range(G):
            dq_ref[:, _lanes(g)] = (jnp.where(lane < HEAD_DIM, dqa_ref[2 * g], dqa_ref[2 * g + 1]) * SCALE).astype(BF16)

    return _pc(body, name=name,
               out_shape=[_sds((B * S, P * LANES), BF16), _sds((B * S, P * LANES), F32), _sds((B * S, P * LANES), F32)],
               grid=(B, P // G, nq),
               in_specs=[_g_col_spec(tq, nq, q_cb, G), _g_kv_spec(S, k_cb, G), _g_kv_spec(S, v_cb, G),
                         _g_col_spec(tq, nq, do_cb, G), _g_stat_col_spec(tq, G)],
               out_specs=[_g_col_spec(tq, nq, 0, G), _g_kv_spec(S, 0, G), _g_kv_spec(S, 0, G)],
               scratch_shapes=[pltpu.VMEM((NS, tq, LANES), F32)] * 3,
               semantics=("arbitrary", "arbitrary", "arbitrary"), vmem=VMEM_BIG)(qa, ka, va, doa, rt)


MEM_GROUP = N_MEM_HEADS // 2


def _mem_fwd(qa, kva, *, name, B, S, NM, q_cb):
    G = MEM_GROUP
    NS = 2 * G
    tq = min(MEM_Q_TILE, S)
    nq = S // tq

    def body(q_ref, k_ref, v_ref, o_ref, lse_ref):
        lane, qh = _streams(q_ref, G, SCALE)
        kblk = [k_ref[:, _lanes(g)] for g in range(G)]
        vblk = [v_ref[:, _lanes(g)] for g in range(G)]
        ss = [lax.dot_general(qh[st], kblk[st // 2], NT, preferred_element_type=F32) for st in range(NS)]
        pb, ls = [], []
        for st in range(NS):
            m = jnp.max(ss[st], axis=-1, keepdims=True)
            p = jnp.exp(ss[st] - m)
            l = jnp.sum(p, axis=-1, keepdims=True)
            lse_ref[0, st] = m + jnp.log(l)
            pb.append(p.astype(BF16))
            ls.append(l)
        pvs = [jnp.dot(pb[st], vblk[st // 2], preferred_element_type=F32) for st in range(NS)]
        for g in range(G):
            o_ref[:, _lanes(g)] = jnp.where(lane < HEAD_DIM, pvs[2 * g] / ls[2 * g],
                                            pvs[2 * g + 1] / ls[2 * g + 1]).astype(BF16)

    return _pc(body, name=name, out_shape=[_sds((B * S, G * LANES), BF16), _sds((B, NS, S, 1), F32)],
               grid=(B, 1, nq),
               in_specs=[_g_col_spec(tq, nq, q_cb, G), _g_kv_spec(NM, 0, G), _g_kv_spec(NM, G, G)],
               out_specs=[_g_col_spec(tq, nq, 0, G), _g_stat_col_spec(tq, G)],
               semantics=("arbitrary", "arbitrary", "arbitrary"), vmem=VMEM_BIG)(qa, kva, kva)


def _mem_bwd(qa, kva, doa, oa, lse, *, name, B, S, NM, q_cb, do_cb):
    G = MEM_GROUP
    NS = 2 * G
    tq = min(MEM_Q_TILE, S)
    nq = S // tq

    def body(q_ref, k_ref, v_ref, do_ref, o_ref, lse_ref, dq_ref, dk_ref, dv_ref):
        @pl.when(pl.program_id(2) == 0)
        def _():
            dk_ref[...] = jnp.zeros_like(dk_ref)
            dv_ref[...] = jnp.zeros_like(dv_ref)

        lane, qh = _streams(q_ref, G, SCALE)
        _, doh = _streams(do_ref, G)
        kblk = [k_ref[:, _lanes(g)] for g in range(G)]
        vblk = [v_ref[:, _lanes(g)] for g in range(G)]
        prod = [do_ref[:, _lanes(g)].astype(F32) * o_ref[:, _lanes(g)].astype(F32) for g in range(G)]
        ss = [lax.dot_general(qh[st], kblk[st // 2], NT, preferred_element_type=F32) for st in range(NS)]
        dps = [lax.dot_general(doh[st], vblk[st // 2], NT, preferred_element_type=F32) for st in range(NS)]
        dsb, pb = [], []
        for st in range(NS):
            delta = jnp.sum(jnp.where(_head_mask(lane, st % 2), prod[st // 2], 0.0), axis=-1, keepdims=True)
            p = jnp.exp(ss[st] - lse_ref[0, st])
            dsb.append((p * (dps[st] - delta)).astype(BF16))
            pb.append(p.astype(BF16))
        dks = [lax.dot_general(dsb[st], qh[st], TN, preferred_element_type=F32) for st in range(NS)]
        dvs = [lax.dot_general(pb[st], doh[st], TN, preferred_element_type=F32) for st in range(NS)]
        dqs = [jnp.dot(dsb[st], kblk[st // 2], preferred_element_type=F32) for st in range(NS)]
        for g in range(G):
            dk_ref[:, _lanes(g)] += dks[2 * g] + dks[2 * g + 1]
            dv_ref[:, _lanes(g)] += dvs[2 * g] + dvs[2 * g + 1]
            dq_ref[:, _lanes(g)] = (jnp.where(lane < HEAD_DIM, dqs[2 * g], dqs[2 * g + 1]) * SCALE).astype(BF16)

    return _pc(body, name=name,
               out_shape=[_sds((B * S, G * LANES), BF16), _sds((B * NM, G * LANES), F32), _sds((B * NM, G * LANES), F32)],
               grid=(B, 1, nq),
               in_specs=[_g_col_spec(tq, nq, q_cb, G), _g_kv_spec(NM, 0, G), _g_kv_spec(NM, G, G),
                         _g_col_spec(tq, nq, do_cb, G), _g_col_spec(tq, nq, 0, G), _g_stat_col_spec(tq, G)],
               out_specs=[_g_col_spec(tq, nq, 0, G), _g_kv_spec(NM, 0, G), _g_kv_spec(NM, 0, G)],
               semantics=("arbitrary", "arbitrary", "arbitrary"), vmem=VMEM_BIG)(qa, kva, kva, doa, oa, lse)


def _sigmoid(v):
    return 0.5 * jnp.tanh(0.5 * v) + 0.5


def _shift_rows(cur, halo_ref, first, rows_idx, k):
    out = pltpu.roll(cur, k, 0)
    top = out[0:8, :]
    for r in range(k):
        hr = halo_ref.shape[0] - k + r
        edge = jnp.where(first, 0.0, halo_ref[hr:hr + 1, :])
        top = jnp.where(rows_idx[0:8, :] == r, edge, top)
    return jnp.concatenate([top, out[8:, :]], axis=0)


def _shift_rows_up(cur, halo_ref, last, rows_idx, k, ts):
    out = pltpu.roll(cur, ts - k, 0)
    bottom = out[ts - 8:, :]
    for r in range(k):
        edge = jnp.where(last, 0.0, halo_ref[r:r + 1, :])
        bottom = jnp.where(rows_idx[0:8, :] == 8 - k + r, edge, bottom)
    return jnp.concatenate([out[:ts - 8, :], bottom], axis=0)


def _ffn_up_gate(x, g, w, cw, cb, *, name, S):
    T, D = x.shape
    F = w.shape[1] // 2
    tm = min(1024, S)
    tn = 256
    nj = F // tn
    tiles_per_seq = S // tm
    halo = 16

    def body(x_ref, xh_ref, g_ref, wg_ref, wv_ref, cwg_ref, cwv_ref, cbg_ref, cbv_ref,
             uc_ref, ub_ref, a_ref, hout_ref, h_ref, hh_ref, eg_ref, ev_ref):
        first = lax.rem(pl.program_id(0), tiles_per_seq) == 0

        @pl.when(pl.program_id(1) == 0)
        def _():
            def norm(v):
                r = lax.rsqrt(jnp.mean(v * v, axis=-1, keepdims=True) + EPS)
                return ((v * r) * g_ref[...]).astype(BF16)
            h = norm(x_ref[...])
            h_ref[...] = h
            hout_ref[...] = h
            hh_ref[...] = norm(xh_ref[...])

        h = h_ref[...]
        rows_idx = lax.broadcasted_iota(jnp.int32, (tm, tn), 0)
        uc = []
        for half, (w_ref, cw_ref, cb_ref, e_ref) in enumerate(((wg_ref, cwg_ref, cbg_ref, eg_ref),
                                                               (wv_ref, cwv_ref, cbv_ref, ev_ref))):
            acc = jnp.dot(h, w_ref[...], preferred_element_type=F32)
            e_ref[...] = jnp.dot(hh_ref[...], w_ref[...], preferred_element_type=F32)
            ub_ref[half] = acc.astype(BF16)
            m1 = _shift_rows(acc, e_ref, first, rows_idx, 1)
            m2 = _shift_rows(acc, e_ref, first, rows_idx, 2)
            uc.append(cb_ref[...] + cw_ref[0:1, :] * m2 + cw_ref[1:2, :] * m1 + cw_ref[2:3, :] * acc)
            uc_ref[half] = uc[half]
        a_ref[...] = (uc[0] * _sigmoid(uc[0]) * uc[1]).astype(BF16)

    in_specs = [pl.BlockSpec((tm, D), lambda i, j: (i, 0)),
                pl.BlockSpec((halo, D), lambda i, j: (jnp.maximum(i * (tm // halo) - 1, 0), 0)),
                pl.BlockSpec((1, D), lambda i, j: (0, 0)),
                pl.BlockSpec((D, tn), lambda i, j: (0, j)), pl.BlockSpec((D, tn), lambda i, j: (0, j + nj)),
                pl.BlockSpec((3, tn), lambda i, j: (0, j)), pl.BlockSpec((3, tn), lambda i, j: (0, j + nj)),
                pl.BlockSpec((1, tn), lambda i, j: (0, j)), pl.BlockSpec((1, tn), lambda i, j: (0, j + nj))]
    return _pc(body, name=name,
               out_shape=[_sds((2, T, F), F32), _sds((2, T, F), BF16), _sds((T, F), BF16), _sds((T, D), BF16)],
               grid=(T // tm, nj), in_specs=in_specs,
               out_specs=[pl.BlockSpec((2, tm, tn), lambda i, j: (0, i, j)), pl.BlockSpec((2, tm, tn), lambda i, j: (0, i, j)),
                          pl.BlockSpec((tm, tn), lambda i, j: (i, j)), pl.BlockSpec((tm, D), lambda i, j: (i, 0))],
               scratch_shapes=[pltpu.VMEM((tm, D), BF16), pltpu.VMEM((halo, D), BF16),
                               pltpu.VMEM((halo, tn), F32), pltpu.VMEM((halo, tn), F32)],
               semantics=("arbitrary", "arbitrary"), vmem=VMEM_BIG)(x, x, g.reshape(1, D), w, w, cw, cw, cb, cb)


def _conv_gate_bwd(da, uc, ub, cw, *, name, B, S):
    F = uc.shape[2]
    tf = F // 2
    ts = min(512, S)
    ns, nf = S // ts, F // tf

    def body(da_ref, uc_ref, ub_ref, wg_ref, wv_ref, dug_ref, duv_ref, pg_ref, pv_ref, nxt_g, nxt_v):
        last = pl.program_id(2) == 0

        @pl.when(jnp.logical_and(pl.program_id(1) == 0, last))
        def _():
            pg_ref[...] = jnp.zeros_like(pg_ref)
            pv_ref[...] = jnp.zeros_like(pv_ref)

        rows_idx = lax.broadcasted_iota(jnp.int32, (ts, tf), 0)
        ucg, ucv = uc_ref[0], uc_ref[1]
        sg = _sigmoid(ucg)
        dav = da_ref[...]
        d_v = dav * (ucg * sg)
        d_g = dav * ucv * (sg * (1.0 + ucg * (1.0 - sg)))
        for half, (o_ref, p_ref, d, w_ref, nxt) in enumerate(((dug_ref, pg_ref, d_g, wg_ref, nxt_g),
                                                               (duv_ref, pv_ref, d_v, wv_ref, nxt_v))):
            p1 = _shift_rows_up(d, nxt, last, rows_idx, 1, ts)
            p2 = _shift_rows_up(d, nxt, last, rows_idx, 2, ts)
            o_ref[...] = (w_ref[2:3, :] * d + w_ref[1:2, :] * p1 + w_ref[0:1, :] * p2).astype(BF16)
            nxt[...] = d[0:8, :]
            uh = ub_ref[half].astype(F32)
            for k, dk in enumerate((p2, p1, d)):
                p_ref[k:k + 1, :] += jnp.sum(dk * uh, axis=0, keepdims=True)
            p_ref[3:4, :] += jnp.sum(d, axis=0, keepdims=True)

    row = pl.BlockSpec((ts, tf), lambda j, b, r: (b * ns + ns - 1 - r, j))
    both = pl.BlockSpec((2, ts, tf), lambda j, b, r: (0, b * ns + ns - 1 - r, j))
    par = pl.BlockSpec((8, tf), lambda j, b, r: (0, j))
    return _pc(body, name=name,
               out_shape=[_sds((B * S, F), BF16), _sds((B * S, F), BF16), _sds((8, F), F32), _sds((8, F), F32)],
               grid=(nf, B, ns),
               in_specs=[row, both, both, pl.BlockSpec((3, tf), lambda j, b, r: (0, j)),
                         pl.BlockSpec((3, tf), lambda j, b, r: (0, j + nf))],
               out_specs=[row, row, par, par],
               scratch_shapes=[pltpu.VMEM((8, tf), F32), pltpu.VMEM((8, tf), F32)],
               semantics=("arbitrary", "arbitrary", "arbitrary"), vmem=VMEM_BIG)(da, uc, ub, cw, cw)


def _adamw(w, g, m, v, *, name):
    rows, cols = w.shape
    tr = rows
    while tr * cols * 4 > 2 ** 20 and tr % 16 == 0:
        tr //= 2

    def body(w_ref, g_ref, m_ref, v_ref, d_ref, nm_ref, nv_ref):
        gv = g_ref[...]
        m_new = ADAM_B1 * m_ref[...] + (1.0 - ADAM_B1) * gv
        v_new = ADAM_B2 * v_ref[...] + (1.0 - ADAM_B2) * (gv * gv)
        m_hat = m_new / (1.0 - ADAM_B1 ** ADAM_STEP)
        v_hat = v_new / (1.0 - ADAM_B2 ** ADAM_STEP)
        d_ref[...] = -ADAM_LR * (m_hat / (jnp.sqrt(v_hat) + ADAM_EPS) + ADAM_WD * w_ref[...])
        nm_ref[...] = m_new
        nv_ref[...] = v_new

    blk = pl.BlockSpec((tr, cols), lambda i: (i, 0))
    return _pc(body, name=name, out_shape=[_sds((rows, cols), F32)] * 3, grid=(rows // tr,),
               in_specs=[blk] * 4, out_specs=[blk] * 3, semantics=("arbitrary",))(w, g, m, v)


def _my_pos():
    return lax.axis_index("x"), lax.axis_index("y"), lax.axis_index("c")


_HBM = pl.BlockSpec(memory_space=pltpu.HBM)
_SEM = pl.BlockSpec(memory_space=pltpu.SEMAPHORE)
_EFFECT = pltpu.SideEffectType.DATAFLOW_SIDE_EFFECTING


def _peers(same_core=False):
    x, y, c = _my_pos()
    out = []
    for k in ((2, 4, 6) if same_core else range(1, N_DEV)):
        px, py, pc = x ^ ((k >> 2) & 1), y ^ ((k >> 1) & 1), c ^ (k & 1)
        out.append(((px, py, pc), 4 * px + 2 * py + pc))
    return out


def _scatter_start(srcs, slot_of, *, name, same_core=False):
    n = len(srcs)
    npr = 3 if same_core else N_DEV - 1
    lands = [lax.empty((N_DEV,) + slot_of(s, 0, shape_only=True), s.dtype) for s in srcs]

    def body(*refs):
        src_refs, land_refs = refs[:n], refs[n:2 * n]
        send_sems, recv_sems = refs[2 * n], refs[2 * n + 1]
        token = refs[-1]
        x, y, c = _my_pos()
        me = 4 * x + 2 * y + c
        for a in range(n):
            for k, (peer, peer_idx) in enumerate(_peers(same_core)):
                pltpu.make_async_remote_copy(
                    src_ref=slot_of(src_refs[a], peer_idx), dst_ref=land_refs[a].at[me],
                    send_sem=send_sems.at[a * npr + k], recv_sem=recv_sems.at[a * npr + k],
                    device_id=peer, device_id_type=MESH).start()
        token[...] = jnp.zeros_like(token)

    hbm = lambda a: pltpu.HBM(a.shape, a.dtype)
    args = [pltpu.with_memory_space_constraint(a, pltpu.HBM) for a in list(srcs) + lands]
    outs = pl.pallas_call(
        body, name=name,
        out_shape=(pltpu.SemaphoreType.DMA((npr * n,)), pltpu.SemaphoreType.DMA((npr * n,)),
                   *[hbm(a) for a in srcs], *[hbm(a) for a in lands], _sds((8, LANES), F32)),
        in_specs=[_HBM] * (2 * n),
        out_specs=(_SEM, _SEM, *([_HBM] * (2 * n)), pl.BlockSpec(memory_space=pltpu.VMEM)),
        input_output_aliases={a: 2 + a for a in range(2 * n)},
        compiler_params=pltpu.CompilerParams(has_side_effects=_EFFECT))(*args)
    return outs[0], outs[1], list(outs[2:2 + n]), list(outs[2 + n:2 + 2 * n]), outs[-1]


def _scatter_wait(send_sems, recv_sems, srcs, lands, slot_of, after, *, name, first=0, same_core=False):
    n = len(srcs)
    npr = 3 if same_core else N_DEV - 1

    def body(*refs):
        src_refs, land_refs = refs[:n], refs[n:2 * n]
        ssem, rsem = refs[2 * n], refs[2 * n + 1]
        x, y, c = _my_pos()
        me = 4 * x + 2 * y + c
        for a in range(n):
            for k, (peer, peer_idx) in enumerate(_peers(same_core)):
                cp = pltpu.make_async_remote_copy(
                    src_ref=slot_of(src_refs[a], peer_idx), dst_ref=land_refs[a].at[me],
                    send_sem=ssem.at[(first + a) * npr + k], recv_sem=rsem.at[(first + a) * npr + k],
                    device_id=peer, device_id_type=MESH)
                cp.wait_send()
                cp.wait_recv()

    hbm = lambda a: pltpu.HBM(a.shape, a.dtype)
    outs = pl.pallas_call(
        body, name=name, out_shape=tuple(hbm(a) for a in list(srcs) + list(lands)),
        in_specs=[_HBM] * (2 * n) + [_SEM, _SEM, pl.BlockSpec(memory_space=pl.ANY)],
        out_specs=tuple([_HBM] * (2 * n)), input_output_aliases={a: a for a in range(2 * n)},
        compiler_params=pltpu.CompilerParams(has_side_effects=_EFFECT))(*srcs, *lands, send_sems, recv_sems, after)
    return list(outs[:n]), list(outs[n:])


def _sibling_start(lands, *, name):
    n = len(lands)

    def body(*refs):
        land_refs = refs[:n]
        send_sems, recv_sems = refs[n], refs[n + 1]
        token = refs[-1]
        x, y, c = _my_pos()
        for a in range(n):
            for k in range(4):
                pltpu.make_async_remote_copy(
                    src_ref=land_refs[a].at[2 * k + c], dst_ref=land_refs[a].at[2 * k + c],
                    send_sem=send_sems.at[a * 4 + k], recv_sem=recv_sems.at[a * 4 + k],
                    device_id=(x, y, 1 - c), device_id_type=MESH).start()
        token[...] = jnp.zeros_like(token)

    hbm = lambda a: pltpu.HBM(a.shape, a.dtype)
    outs = pl.pallas_call(
        body, name=name,
        out_shape=(pltpu.SemaphoreType.DMA((4 * n,)), pltpu.SemaphoreType.DMA((4 * n,)),
                   *[hbm(a) for a in lands], _sds((8, LANES), F32)),
        in_specs=[_HBM] * n,
        out_specs=(_SEM, _SEM, *([_HBM] * n), pl.BlockSpec(memory_space=pltpu.VMEM)),
        input_output_aliases={a: 2 + a for a in range(n)},
        compiler_params=pltpu.CompilerParams(has_side_effects=_EFFECT))(
            *[pltpu.with_memory_space_constraint(a, pltpu.HBM) for a in lands])
    return outs[0], outs[1], list(outs[2:2 + n]), outs[-1]


def _sibling_wait(send_sems, recv_sems, lands, after, *, name):
    n = len(lands)

    def body(*refs):
        land_refs = refs[:n]
        ssem, rsem = refs[n], refs[n + 1]
        x, y, c = _my_pos()
        for a in range(n):
            for k in range(4):
                cp = pltpu.make_async_remote_copy(
                    src_ref=land_refs[a].at[2 * k + c], dst_ref=land_refs[a].at[2 * k + 1 - c],
                    send_sem=ssem.at[a * 4 + k], recv_sem=rsem.at[a * 4 + k],
                    device_id=(x, y, 1 - c), device_id_type=MESH)
                cp.wait_send()
                cp.wait_recv()

    hbm = lambda a: pltpu.HBM(a.shape, a.dtype)
    outs = pl.pallas_call(
        body, name=name, out_shape=tuple(hbm(a) for a in lands),
        in_specs=[_HBM] * n + [_SEM, _SEM, pl.BlockSpec(memory_space=pl.ANY)],
        out_specs=tuple([_HBM] * n), input_output_aliases={a: a for a in range(n)},
        compiler_params=pltpu.CompilerParams(has_side_effects=_EFFECT))(*lands, send_sems, recv_sems, after)
    return list(outs)


def _whole(a, peer_idx, shape_only=False):
    return a.shape if shape_only else a


def _slot(a, peer_idx, shape_only=False):
    return a.shape[1:] if shape_only else a.at[peer_idx]


def _sum_slots(a, *, name, tr=None):
    rows, cols = a.shape[1], a.shape[2]
    if tr is None:
        tr = rows
        while N_DEV * tr * cols * a.dtype.itemsize > 3 * 2 ** 20 and tr % 32 == 0:
            tr //= 2

    def body(a_ref, o_ref):
        acc = a_ref[0].astype(F32)
        for j in range(1, N_DEV):
            acc = acc + a_ref[j].astype(F32)
        o_ref[...] = acc

    return _pc(body, name=name, out_shape=_sds((rows, cols), F32), grid=(rows // tr,),
               in_specs=[pl.BlockSpec((N_DEV, tr, cols), lambda i: (0, i, 0))],
               out_specs=pl.BlockSpec((tr, cols), lambda i: (i, 0)), semantics=("arbitrary",), vmem=VMEM_BIG)(a)


def _to_slots(full, kind):
    if kind == "rows2":
        r, c = full.shape
        return full.reshape(N_DEV, r // N_DEV, c)
    if kind == "cols2":
        r, c = full.shape
        return full.reshape(r, N_DEV, c // N_DEV).transpose(1, 0, 2)
    if kind == "rows3":
        l, r, c = full.shape
        return full.reshape(l, N_DEV, r // N_DEV, c).transpose(1, 0, 2, 3)
    if kind == "cols3":
        l, r, c = full.shape
        return full.reshape(l, r, N_DEV, c // N_DEV).transpose(2, 0, 1, 3)
    raise ValueError(kind)


def _from_slots(slots, kind):
    if kind == "rows2":
        _, r, c = slots.shape
        return slots.reshape(N_DEV * r, c)
    if kind == "cols2":
        _, r, c = slots.shape
        return slots.transpose(1, 0, 2).reshape(r, N_DEV * c)
    if kind == "rows3":
        _, l, r, c = slots.shape
        return slots.transpose(1, 0, 2, 3).reshape(l, N_DEV * r, c)
    if kind == "cols3":
        _, l, r, c = slots.shape
        return slots.transpose(1, 2, 0, 3).reshape(l, r, N_DEV * c)
    raise ValueError(kind)


BIG = (("w_in_a", "rows2"), ("w_in_b", "rows2"), ("w_kv", "cols2"), ("w_memkv", "rows3"),
       ("w_out", "rows3"), ("w_up", "cols3"), ("w_down", "rows3"))


def _round_up(n, m):
    return -(-n // m) * m


def _pad_rows(a, rows, axis):
    pad = [(0, 0)] * a.ndim
    pad[axis] = (0, rows - a.shape[axis])
    return jnp.pad(a, pad)


def kernel(x, mem, ln_mix_g, w_in_a, b_f_a, w_in_b, ln_kv_g, w_kv, ln_mem_g, w_memkv, w_out, ln_ffn_g, w_up, conv_w, conv_b, w_down, final_g, loss_target, m_ln_mix_g, m_w_in_a, m_b_f_a, m_w_in_b, m_ln_kv_g, m_w_kv, m_ln_mem_g, m_w_memkv, m_w_out, m_ln_ffn_g, m_w_up, m_conv_w, m_conv_b, m_w_down, m_final_g, v_ln_mix_g, v_w_in_a, v_b_f_a, v_w_in_b, v_ln_kv_g, v_w_kv, v_ln_mem_g, v_w_memkv, v_w_out, v_ln_ffn_g, v_w_up, v_conv_w, v_conv_b, v_w_down, v_final_g):
    B, S, D = x.shape
    NM = mem.shape[1]
    T = B * S
    F = w_down.shape[1] * N_DEV
    my_idx = 4 * lax.axis_index("x") + 2 * lax.axis_index("y") + lax.axis_index("c")

    shards = {"w_in_a": w_in_a[0], "w_in_b": w_in_b[0], "w_kv": w_kv, "w_memkv": w_memkv, "w_out": w_out,
              "w_up": w_up, "w_down": w_down}
    moms = {"w_in_a": (m_w_in_a[0], v_w_in_a[0]), "w_in_b": (m_w_in_b[0], v_w_in_b[0]), "w_kv": (m_w_kv, v_w_kv),
            "w_memkv": (m_w_memkv, v_w_memkv), "w_out": (m_w_out, v_w_out), "w_up": (m_w_up, v_w_up),
            "w_down": (m_w_down, v_w_down)}

    groups = [("a1", [("w_in_a", None)]),
              ("a2", [("w_memkv", None), ("w_out", None), ("conv_w", None)]),
              ("b0", [("w_up", 0), ("w_down", 0)]), ("a3", [("w_in_b", None), ("w_kv", None)]),
              ("b1", [("w_up", 1), ("w_down", 1)])]
    sources = dict(shards, conv_w=conv_w)
    span = {}
    for sname, batch in (("first", groups[:1]), ("rest", groups[1:])):
        srcs = []
        for gname, members in batch:
            span[gname] = (sname, len(srcs), len(members))
            for n, layer in members:
                a = sources[n] if layer is None else sources[n][layer]
                if n == "conv_w":
                    srcs.append(a + span["first"][4][0, 0])
                else:
                    srcs.append(a.astype(BF16))
        span[sname] = _scatter_start(srcs, _whole, name=f"gather_start_{sname}", same_core=sname == "first")
    token = span["rest"][4]

    def gathered(gname, after):
        sname, lo, cnt = span[gname]
        ssem, rsem, thru, lands, _ = span[sname]
        two_level = sname == "first"
        thru, lands = _scatter_wait(ssem, rsem, thru[lo:lo + cnt], lands[lo:lo + cnt], _whole, after,
                                    name=f"gather_wait_{gname}", first=lo, same_core=two_level)
        lands = [lax.dynamic_update_index_in_dim(land, s, my_idx, 0) for land, s in zip(lands, thru)]
        if two_level:
            s2, r2, lands, tok2 = _sibling_start(lands, name=f"gather_pass_{gname}")
            lands = _sibling_wait(s2, r2, lands, tok2, name=f"gather_pass_wait_{gname}")
        return lands

    x2d = x.reshape(T, D)
    h_mix0 = _rmsnorm(x2d, ln_mix_g[0] + token[0, 0], name="norm_mix0")
    full = {}
    (g_wa,) = gathered("a1", h_mix0)
    full["w_in_a"] = _from_slots(g_wa, "rows2")

    wa = full["w_in_a"]
    n_qkv = 3 * MAIN_W
    wa = jnp.concatenate([wa[:, :n_qkv], wa[:, n_qkv + N_MAIN_HEADS:], wa[:, n_qkv:n_qkv + N_MAIN_HEADS],
                          jnp.zeros((D, LANES - N_MAIN_HEADS), BF16)], axis=1)
    n_main = n_qkv + MEM_W
    full["w_up"], full["w_down"] = {}, {}
    b_f =_pad_rows(b_f_a.reshape(1, N_MAIN_HEADS), LANES, 1)

    x2d = x.reshape(T, D)
    mem2d = mem.reshape(B * NM, D)
    tgt2d = loss_target.reshape(T, D)
    PM, PX = N_MAIN_HEADS // 2, N_MEM_HEADS // 2

    def stats_to_heads(c2d):
        c = c2d.reshape(B, S, LANES)[:, :, :N_MAIN_HEADS].transpose(0, 2, 1)
        return c[:, :, None, :]

    def mem_kv(layer):
        return _mm_fwd(mem2d, full["w_memkv"][layer], name=f"memkv{layer}", tm=B * NM, tn=2 * MEM_W,
                       out_dtype=BF16, g=ln_mem_g[layer], save_h=True)

    def conv_ffn_fwd(xin, layer):
        uc, ub, a, h = _ffn_up_gate(xin, ln_ffn_g[layer], full["w_up"][layer], conv_w_full[layer],
                                    conv_b[layer].reshape(1, 2 * F), name=f"ffn_up{layer}", S=S)
        xo = _mm_fwd(a, full["w_down"][layer], name=f"ffn_down{layer}", tm=min(1024, T), tn=1024, out_dtype=F32, res=xin)
        return xo, (uc, ub, h, a)

    proj_a = _mm_fwd(h_mix0, wa, name="in_proj_a", tm=min(1024, T), tn=2560, out_dtype=BF16, ncols=n_main)
    f_logit = _mm_fwd(h_mix0, wa, name="in_proj_f", tm=min(1024, T), tn=LANES, out_dtype=F32,
                      col0=n_main // LANES, ncols=LANES)
    c2d = _forget_cumsum(f_logit, b_f, B=B, S=S, name="forget_cumsum")
    cr = stats_to_heads(c2d)
    o_main0, lse0 = _fox_fwd_g(proj_a, proj_a, proj_a, cr, name="fox_fwd", B=B, S=S, P=PM, q_cb=0, k_cb=PM, v_cb=2 * PM,
                               G=FWD_HEAD_GROUP)
    g_wmem, g_wout, g_cw = gathered("a2", lse0)
    full["w_memkv"] = _from_slots(g_wmem, "rows3")
    full["w_out"] = _from_slots(g_wout, "rows3")
    conv_w_full = _from_slots(g_cw, "cols3")
    memkv0, h_mem0 = mem_kv(0)
    o_mem0, lse_m0 = _mem_fwd(proj_a, memkv0, name="mem_fwd0", B=B, S=S, NM=NM, q_cb=3 * PM)
    o_cat0 = jnp.concatenate([o_main0, o_mem0], axis=1)
    x1 = _mm_fwd(o_cat0, full["w_out"][0], name="out_proj0", tm=min(1024, T), tn=1024, out_dtype=F32, res=x2d)
    g_up, g_dn = gathered("b0", x1)
    full["w_up"][0], full["w_down"][0] = _from_slots(g_up, "cols2"), _from_slots(g_dn, "rows2")
    x2, ffn_saved0 = conv_ffn_fwd(x1, 0)
    g_wb, g_wkv = gathered("a3", x2)
    wb, wkv = _from_slots(g_wb, "rows2"), _from_slots(g_wkv, "cols2")
    kv, h_kv =_mm_fwd(x2, wkv, name="kv_proj", tm=min(1024, T), tn=1536, out_dtype=BF16, g=ln_kv_g, save_h=True)
    proj_b, h_mix1 = _mm_fwd(x2, wb, name="in_proj_b", tm=min(1024, T), tn=1024, out_dtype=BF16, g=ln_mix_g[1],
                             save_h=True)
    o_main1, rt1 = _sb_fwd_g(proj_b, kv, kv, name="sb_fwd", B=B, S=S, P=PM, q_cb=0, k_cb=0, v_cb=PM,
                             G=FWD_HEAD_GROUP)
    memkv1, h_mem1 = mem_kv(1)
    o_mem1, lse_m1 = _mem_fwd(proj_b, memkv1, name="mem_fwd1", B=B, S=S, NM=NM, q_cb=PM)
    o_cat1 = jnp.concatenate([o_main1, o_mem1], axis=1)
    x3 = _mm_fwd(o_cat1, full["w_out"][1], name="out_proj1", tm=min(1024, T), tn=1024, out_dtype=F32, res=x2)
    g_up, g_dn = gathered("b1", x3)
    full["w_up"][1], full["w_down"][1] = _from_slots(g_up, "cols2"), _from_slots(g_dn, "rows2")
    x4, ffn_saved1 = conv_ffn_fwd(x3, 1)
    dx4, dg_final, loss_part = _loss_head(x4, final_g, tgt2d, name="loss_head")

    grads = {}
    small = {}
    reduce_groups = []

    def start_reduce(gname, keys, kinds):
        slots = [_to_slots(grads[k], kind) for k, kind in zip(keys, kinds)]
        ssem, rsem, thru, lands, tok = _scatter_start(slots, _slot, name=f"reduce_start_{gname}")
        reduce_groups.append((gname, keys, ssem, rsem, thru, lands))
        return tok[0, 0]

    def conv_ffn_bwd(dxo, xin, saved, layer):
        uc, ub, h, a = saved
        w_dn = full["w_down"][layer]
        da = _mm_nt(dxo, w_dn, name=f"d_act{layer}", tm=min(1024, T), tn=F // 2, out_dtype=F32)
        grads[("w_down", layer)] = _wgrad(a, dxo, f"g_w_down{layer}")
        cwl = conv_w_full[layer]
        du_g, du_v, p_g, p_v = _conv_gate_bwd(da, uc, ub, cwl, name=f"conv_bwd{layer}", B=B, S=S)
        small[("conv_w", layer)] = jnp.concatenate([p_g[0:3], p_v[0:3]], axis=1)
        small[("conv_b", layer)] = jnp.concatenate([p_g[3], p_v[3]], axis=0)
        grads[("w_up", layer)] = jnp.concatenate(
            [_wgrad(h, du_g, f"g_w_up_gate{layer}"), _wgrad(h, du_v, f"g_w_up_val{layer}")], axis=1)
        tok = start_reduce(f"ffn{layer}", [("w_down", layer), ("w_up", layer)], ["rows2", "cols2"])
        dxi, dg = _mm_nt_rmsbwd([(du_g, 0), (du_v, 1)], full["w_up"][layer], xin, ln_ffn_g[layer] + tok,
                                name=f"d_ffn_in{layer}", dres=dxo)
        small[("ln_ffn_g", layer)] = dg[0]
        return dxi

    def mem_bwd(proj, q_cb, memkv, h_mem, do_cat, o_mem, lse_m, layer):
        dqm, dmk, dmv = _mem_bwd(proj, memkv, do_cat, o_mem, lse_m, name=f"mem_bwd{layer}", B=B, S=S, NM=NM,
                                 q_cb=q_cb, do_cb=PM)
        grads[("w_memkv", layer)] = jnp.concatenate(
            [_wgrad(h_mem, dmk, f"g_w_memk{layer}"), _wgrad(h_mem, dmv, f"g_w_memv{layer}")], axis=1)
        _, dg = _mm_nt_rmsbwd([(dmk, 0), (dmv, 1)], full["w_memkv"][layer], mem2d, ln_mem_g[layer],
                              name=f"d_mem_in{layer}", want_dx=False)
        small[("ln_mem_g", layer)] = dg[0]
        return dqm

    dx3 = conv_ffn_bwd(dx4, x3, ffn_saved1, 1)
    do_cat1 = _mm_nt(dx3, full["w_out"][1], name="d_o_cat1", tm=min(1024, T), tn=1024, out_dtype=BF16)
    grads[("w_out", 1)] = _wgrad(o_cat1, dx3, "g_w_out1")
    dq1, dk1, dv1 = _sb_bwd_g(proj_b, kv, kv, do_cat1, rt1, name="sb_bwd", B=B, S=S, P=PM, q_cb=0, k_cb=0, v_cb=PM,
                            do_cb=0)
    dqm1 = mem_bwd(proj_b, PM, memkv1, h_mem1, do_cat1, o_mem1, lse_m1, 1)
    grads["w_in_b"] = jnp.concatenate([_wgrad(h_mix1, dq1, "g_w_in_b_q"), _wgrad(h_mix1, dqm1, "g_w_in_b_m")], axis=1)
    grads["w_kv"] = jnp.concatenate([_wgrad(h_kv, dk1, "g_w_kv_k"), _wgrad(h_kv, dv1, "g_w_kv_v")], axis=1)
    tok = start_reduce("mix1", [("w_out", 1), "w_in_b", "w_kv", ("w_memkv", 1)], ["rows2", "rows2", "cols2", "rows2"])
    dx2, dg = _mm_nt_rmsbwd([(dq1, 0), (dqm1, MAIN_W // MEM_W)], wb, x2, ln_mix_g[1] + tok, name="d_mix_in1", dres=dx3)
    small[("ln_mix_g", 1)] = dg[0]
    dx2, dg = _mm_nt_rmsbwd([(dk1, 0), (dv1, 1)], wkv, x2, ln_kv_g, name="d_kv_in", dres=dx2)
    small["ln_kv_g"] = dg[0]
    dx1 = conv_ffn_bwd(dx2, x1, ffn_saved0, 0)
    do_cat0 = _mm_nt(dx1, full["w_out"][0], name="d_o_cat0", tm=min(1024, T), tn=1024, out_dtype=BF16)
    grads[("w_out", 0)] = _wgrad(o_cat0, dx1, "g_w_out0")
    dq0, dk0, dv0, dcs = _fox_bwd_g(proj_a, proj_a, proj_a, do_cat0, lse0, cr, name="fox_bwd", B=B, S=S, P=PM, q_cb=0,
                                  k_cb=PM, v_cb=2 * PM, do_cb=0)
    dqm0 = mem_bwd(proj_a, 3 * PM, memkv0, h_mem0, do_cat0, o_mem0, lse_m0, 0)
    dc2d = _pad_rows(dcs[:, :, 0, :].transpose(0, 2, 1).reshape(T, N_MAIN_HEADS), LANES, 1)
    df, db_f = _forget_cumsum_bwd(dc2d, f_logit, b_f, B=B, S=S, name="forget_cumsum_bwd")
    a_parts = [(dq0, 0), (dk0, 1), (dv0, 2), (dqm0, n_qkv // MEM_W), (df, n_main // LANES)]
    g_wa = jnp.concatenate([_wgrad(h_mix0, p, f"g_w_in_a{k}") for k, (p, _) in enumerate(a_parts)], axis=1)
    grads["w_in_a"] = jnp.concatenate([g_wa[:, :n_qkv], g_wa[:, n_main:n_main + N_MAIN_HEADS], g_wa[:, n_qkv:n_main]],
                                      axis=1)
    tok = start_reduce("mix0", [("w_out", 0), ("w_memkv", 0), "w_in_a"], ["rows2", "rows2", "rows2"])
    dx0, dg = _mm_nt_rmsbwd(a_parts, wa, x2d, ln_mix_g[0] + tok, name="d_mix_in0", dres=dx1)
    small[("ln_mix_g", 0)] = dg[0]
    grad_x = dx0.reshape(B, S, D)

    def both_small(name):
        return jnp.stack([small[(name, 0)], small[(name, 1)]])

    small_list = [("ln_mix_g", both_small("ln_mix_g")), ("b_f_a", db_f[:, :N_MAIN_HEADS]), ("ln_kv_g", small["ln_kv_g"]),
                  ("ln_mem_g", both_small("ln_mem_g")), ("ln_ffn_g", both_small("ln_ffn_g")),
                  ("conv_w", both_small("conv_w")), ("conv_b", both_small("conv_b")), ("final_g", dg_final[0]),
                  ("loss", loss_part[0, :1])]
    sm_rows = []
    for _, a in small_list:
        flat = a.reshape(-1)
        sm_rows.append(_pad_rows(flat, _round_up(flat.size, 8 * LANES), 0).reshape(-1, LANES))
    spack = jnp.concatenate(sm_rows, axis=0)
    s_ssem, s_rsem, s_thru, s_lands, s_tok = _scatter_start([spack], _whole, name="small_start")

    pieces = {}
    for gname, keys, ssem, rsem, thru, lands in reduce_groups:
        thru, lands = _scatter_wait(ssem, rsem, thru, lands, _slot, s_tok, name=f"reduce_wait_{gname}")
        for key, mine, land in zip(keys, thru, lands):
            own = lax.dynamic_index_in_dim(mine, my_idx, 0, keepdims=False)
            land = lax.dynamic_update_index_in_dim(land, own, my_idx, 0)
            tag = key if isinstance(key, str) else f"{key[0]}{key[1]}"
            pieces[key] = _sum_slots(land, name=f"sum_{tag}")

    red = {}
    for n in ("w_in_a", "w_in_b", "w_kv"):
        red[n] = pieces[n].reshape(shards[n].shape)
    for n in ("w_memkv", "w_out", "w_up", "w_down"):
        red[n] = jnp.stack([pieces[(n, 0)], pieces[(n, 1)]])

    weights = {"ln_mix_g": ln_mix_g, "w_in_a": w_in_a, "b_f_a": b_f_a, "w_in_b": w_in_b, "ln_kv_g": ln_kv_g,
               "w_kv": w_kv, "ln_mem_g": ln_mem_g, "w_memkv": w_memkv, "w_out": w_out, "ln_ffn_g": ln_ffn_g,
               "w_up": w_up, "conv_w": conv_w, "conv_b": conv_b, "w_down": w_down, "final_g": final_g}
    m_in = {"ln_mix_g": m_ln_mix_g, "w_in_a": m_w_in_a, "b_f_a": m_b_f_a, "w_in_b": m_w_in_b, "ln_kv_g": m_ln_kv_g,
            "w_kv": m_w_kv, "ln_mem_g": m_ln_mem_g, "w_memkv": m_w_memkv, "w_out": m_w_out, "ln_ffn_g": m_ln_ffn_g,
            "w_up": m_w_up, "conv_w": m_conv_w, "conv_b": m_conv_b, "w_down": m_w_down, "final_g": m_final_g}
    v_in = {"ln_mix_g": v_ln_mix_g, "w_in_a": v_w_in_a, "b_f_a": v_b_f_a, "w_in_b": v_w_in_b, "ln_kv_g": v_ln_kv_g,
            "w_kv": v_w_kv, "ln_mem_g": v_ln_mem_g, "w_memkv": v_w_memkv, "w_out": v_w_out, "ln_ffn_g": v_ln_ffn_g,
            "w_up": v_w_up, "conv_w": v_conv_w, "conv_b": v_conv_b, "w_down": v_w_down, "final_g": v_final_g}
    order = list(weights)
    big_names = [n for n, _ in BIG]
    g_out, d_out, nm_out, nv_out = {}, {}, {}, {}

    def update(n):
        w = weights[n]
        cols = w.shape[-1]
        g = red[n].reshape(w.shape)
        d, nm, nv = _adamw(w.reshape(-1, cols), g.reshape(-1, cols), m_in[n].reshape(-1, cols),
                           v_in[n].reshape(-1, cols), name=f"adamw_{n}")
        g_out[n], d_out[n], nm_out[n], nv_out[n] = g, d.reshape(w.shape), nm.reshape(w.shape), nv.reshape(w.shape)

    for n in big_names:
        update(n)
    all_updated = jnp.stack([d_out[n].reshape(-1)[0] for n in big_names])
    s_thru, s_lands = _scatter_wait(s_ssem, s_rsem, s_thru, s_lands, _whole, all_updated, name="small_wait")
    ssum = _sum_slots(lax.dynamic_update_index_in_dim(s_lands[0], s_thru[0], my_idx, 0), name="sum_small")
    off = 0
    for (n, a), rows in zip(small_list, sm_rows):
        red[n] = ssum[off:off + rows.shape[0]].reshape(-1)[:a.size].reshape(a.shape)
        off += rows.shape[0]
    loss = red["loss"][0]
    shard_cols = conv_w.shape[2]
    red["conv_w"] = lax.dynamic_slice_in_dim(red["conv_w"], my_idx * shard_cols, shard_cols, axis=2)
    red["b_f_a"] = red["b_f_a"].reshape(b_f_a.shape)
    update("conv_w")
    small_names = [n for n in order if n not in g_out]

    def pack_small(src):
        rows = []
        for n in small_names:
            flat = src[n].reshape(-1)
            rows.append(_pad_rows(flat, _round_up(flat.size, 8 * LANES), 0).reshape(-1, LANES))
        return jnp.concatenate(rows, axis=0), [r.shape[0] for r in rows]

    red_small = {n: red[n].reshape(weights[n].shape) for n in small_names}
    wp, counts = pack_small(weights)
    gp, _ = pack_small(red_small)
    mp, _ = pack_small(m_in)
    vp, _ = pack_small(v_in)
    dp, nmp, nvp = _adamw(wp, gp, mp, vp, name="adamw_small")
    off = 0
    for n, cnt in zip(small_names, counts):
        shp = weights[n].shape
        size = weights[n].size
        g_out[n] = red_small[n]
        d_out[n] = dp[off:off + cnt].reshape(-1)[:size].reshape(shp)
        nm_out[n] = nmp[off:off + cnt].reshape(-1)[:size].reshape(shp)
        nv_out[n] = nvp[off:off + cnt].reshape(-1)[:size].reshape(shp)
        off += cnt

    return (loss, grad_x, *[g_out[n] for n in order], *[d_out[n] for n in order],
            *[nm_out[n] for n in order], *[nv_out[n] for n in order])
```
